```python
import jax, jax.numpy as jnp
from jax import lax
import numpy as np


D_MODEL = 1024
BATCH = 8
SEQ = 4096
DEPTH = 2

N_A_LAYERS = DEPTH // 2
N_B_LAYERS = DEPTH - N_A_LAYERS
HG_EXPAND = 128
HG_HEADS = D_MODEL // HG_EXPAND
HG_DV = D_MODEL // HG_HEADS
HG_CHUNK = 32
ATT_HEAD_DIM = 64
ATT_Q_HEADS = D_MODEL // ATT_HEAD_DIM
ATT_KV_HEADS = 2
ATT_GROUP = ATT_Q_HEADS // ATT_KV_HEADS
WINDOW = 128
D_FF = 2816
CONV_WIDTH = 3
EPS = 1e-6

kernel_name = 'yoco_hgrn2_swa_sink_alibi_convffn'

F32 = jnp.float32


def rms_norm(x, g):
    xf = x.astype(F32)
    xf = xf * lax.rsqrt(jnp.mean(xf * xf, axis=-1, keepdims=True) + EPS)
    return (xf * g.astype(F32)).astype(x.dtype)


def alibi_slopes(n_heads):
    return jnp.asarray(2.0 ** (-8.0 * np.arange(1, n_heads + 1) / n_heads), F32)


def hgrn2_chunked(q, k, v, logf):
    b_, s_, h_, dk = q.shape
    dv = v.shape[-1]
    n_chunks = s_ // HG_CHUNK

    def to_chunks(t):
        return t.reshape(b_, n_chunks, HG_CHUNK, h_, t.shape[-1]).transpose(1, 0, 3, 2, 4)

    qc, kc, vc, gc = to_chunks(q), to_chunks(k), to_chunks(v), to_chunks(logf)
    causal = jnp.tril(jnp.ones((HG_CHUNK, HG_CHUNK), bool))[:, :, None]

    def step(state, inp):
        qb, kb, vb, gb = inp
        cum = jnp.cumsum(gb, axis=2)
        o_inter = jnp.einsum('bhtk,bhkv->bhtv', qb * jnp.exp(cum), state)
        rel = cum[:, :, :, None, :] - cum[:, :, None, :, :]
        decay = jnp.exp(jnp.where(causal, rel, -jnp.inf))
        scores = jnp.einsum('bhtk,bhsk,bhtsk->bhts', qb, kb, decay)
        o_intra = jnp.einsum('bhts,bhsv->bhtv', scores, vb)
        last = cum[:, :, -1:, :]
        new_state = (jnp.exp(last[:, :, 0, :])[..., None] * state
                     + jnp.einsum('bhsk,bhsv->bhkv', kb * jnp.exp(last - cum), vb))
        return new_state, o_inter + o_intra

    s0 = jnp.zeros((b_, h_, dk, dv), F32)
    _, o = lax.scan(step, s0, (qc, kc, vc, gc))
    return o.transpose(1, 0, 3, 2, 4).reshape(b_, s_, h_, dv)


def hgrn2_mixer(x, w_in, lower_bound, out_norm, w_out):
    b_, s_, _ = x.shape
    q, f, i, g = jnp.split(x @ w_in, 4, axis=-1)
    q = jax.nn.silu(q.astype(F32)) * HG_EXPAND ** -0.5
    forget = lower_bound + (1.0 - lower_bound) * jax.nn.sigmoid(f.astype(F32))
    logf = jnp.log(forget)
    k = 1.0 - forget
    heads = lambda t: t.reshape(b_, s_, HG_HEADS, -1)
    o = hgrn2_chunked(heads(q), heads(k), heads(i.astype(F32)), heads(logf))
    o = rms_norm(o, out_norm) * jax.nn.silu(heads(g.astype(F32)))
    return o.reshape(b_, s_, D_MODEL).astype(x.dtype) @ w_out


def shared_kv(h, kv_norm, w_kv):
    b_, s_, _ = h.shape
    k, v = jnp.split(rms_norm(h, kv_norm) @ w_kv, 2, axis=-1)
    return (k.reshape(b_, s_, ATT_KV_HEADS, ATT_HEAD_DIM),
            v.reshape(b_, s_, ATT_KV_HEADS, ATT_HEAD_DIM))


def swa_sink_attention(x, k, v, w_q, sinks, w_o):
    b_, s_, _ = x.shape
    nb = s_ // WINDOW
    q = (x @ w_q).reshape(b_, nb, WINDOW, ATT_KV_HEADS, ATT_GROUP, ATT_HEAD_DIM)

    def band(t):
        tb = t.reshape(b_, nb, WINDOW, ATT_KV_HEADS, ATT_HEAD_DIM)
        prev = jnp.pad(tb[:, :-1], ((0, 0), (1, 0), (0, 0), (0, 0), (0, 0)))
        return jnp.concatenate([prev, tb], axis=2)

    kb, vb = band(k), band(v)
    scores = jnp.einsum('bnqkgd,bnskd->bnkgqs', q.astype(F32), kb.astype(F32)) * ATT_HEAD_DIM ** -0.5
    q_idx = jnp.arange(WINDOW)[:, None] + WINDOW
    k_idx = jnp.arange(2 * WINDOW)[None, :]
    dist = q_idx - k_idx
    key_abs = (jnp.arange(nb) * WINDOW)[:, None] + jnp.arange(2 * WINDOW)[None, :] - WINDOW
    valid = ((dist >= 0) & (dist < WINDOW))[None] & (key_abs >= 0)[:, None, :]
    slopes = alibi_slopes(ATT_Q_HEADS).reshape(ATT_KV_HEADS, ATT_GROUP)
    scores = scores - slopes[:, :, None, None] * dist.astype(F32)
    scores = jnp.where(valid[None, :, None, None], scores, -jnp.inf)
    sink = sinks.astype(F32).reshape(ATT_KV_HEADS, ATT_GROUP)[None, None, :, :, None, None]
    m = jnp.maximum(jnp.max(scores, axis=-1, keepdims=True), sink)
    e = jnp.exp(scores - m)
    probs = e / (jnp.sum(e, axis=-1, keepdims=True) + jnp.exp(sink - m))
    out = jnp.einsum('bnkgqs,bnskd->bnqkgd', probs, vb.astype(F32))
    return out.reshape(b_, s_, ATT_Q_HEADS * ATT_HEAD_DIM).astype(x.dtype) @ w_o


def conv_ffn(x, w_up, conv_w, conv_b, w_down):
    s_ = x.shape[1]
    gate, val = jnp.split(x @ w_up, 2, axis=-1)
    gp = jnp.pad(gate, ((0, 0), (CONV_WIDTH - 1, 0), (0, 0)))
    conv = conv_b
    for j in range(CONV_WIDTH):
        conv = conv + conv_w[j] * gp[:, j:j + s_]
    return (jax.nn.silu(conv) * val) @ w_down


def _fwd_setup_inputs(seed: int = 0) -> dict:
    key = jax.random.key(seed)
    ks = jax.random.split(key, 18)
    D = D_MODEL
    HQD = ATT_Q_HEADS * ATT_HEAD_DIM
    KVD = ATT_KV_HEADS * ATT_HEAD_DIM

    def w(k, shape, fan_in):
        return jax.random.normal(k, shape, F32) * fan_in ** -0.5

    def gain(k, shape):
        return 1.0 + 0.02 * jax.random.normal(k, shape, F32)

    return {
        'x': jax.random.normal(ks[0], (BATCH, SEQ, D), F32),
        'hg_norm': gain(ks[1], (N_A_LAYERS, D)),
        'hg_w_in': w(ks[2], (N_A_LAYERS, D, 4 * D), D),
        'hg_lb_logits': 0.1 * jax.random.normal(ks[3], (N_A_LAYERS + 1, D), F32),
        'hg_out_norm': gain(ks[4], (N_A_LAYERS, HG_DV)),
        'hg_w_out': w(ks[5], (N_A_LAYERS, D, D), D),
        'kv_norm': gain(ks[6], (D,)),
        'w_kv': w(ks[7], (D, 2 * KVD), D),
        'attn_norm': gain(ks[8], (N_B_LAYERS, D)),
        'attn_w_q': w(ks[9], (N_B_LAYERS, D, HQD), D),
        'attn_sinks': 0.5 * jax.random.normal(ks[10], (N_B_LAYERS, ATT_Q_HEADS), F32),
        'attn_w_o': w(ks[11], (N_B_LAYERS, HQD, D), HQD),
        'ffn_norm': gain(ks[12], (DEPTH, D)),
        'ffn_w_up': w(ks[13], (DEPTH, D, 2 * D_FF), D),
        'ffn_conv_w': w(ks[14], (DEPTH, CONV_WIDTH, D_FF), CONV_WIDTH),
        'ffn_conv_b': 0.02 * jax.random.normal(ks[15], (DEPTH, D_FF), F32),
        'ffn_w_down': w(ks[16], (DEPTH, D_FF, D), D_FF),
        'final_norm': gain(ks[17], (D,)),
    }


def _fwd_reference(x, hg_norm, hg_w_in, hg_lb_logits, hg_out_norm, hg_w_out, kv_norm, w_kv,
              attn_norm, attn_w_q, attn_sinks, attn_w_o, ffn_norm, ffn_w_up, ffn_conv_w,
              ffn_conv_b, ffn_w_down, final_norm):
    lower_bounds = jnp.cumsum(jax.nn.softmax(hg_lb_logits.astype(F32), axis=0), axis=0)
    h = x
    k_sh, v_sh = None, None
    for layer in range(DEPTH):
        if layer < N_A_LAYERS:
            a = layer
            h = h + hgrn2_mixer(rms_norm(h, hg_norm[a]), hg_w_in[a], lower_bounds[a],
                                hg_out_norm[a], hg_w_out[a])
        else:
            bi = layer - N_A_LAYERS
            if bi == 0:
                k_sh, v_sh = shared_kv(h, kv_norm, w_kv)
            h = h + swa_sink_attention(rms_norm(h, attn_norm[bi]), k_sh, v_sh,
                                       attn_w_q[bi], attn_sinks[bi], attn_w_o[bi])
        h = h + conv_ffn(rms_norm(h, ffn_norm[layer]), ffn_w_up[layer], ffn_conv_w[layer],
                         ffn_conv_b[layer], ffn_w_down[layer])
    return rms_norm(h, final_norm)


import jax as _jax
import jax.numpy as _jnp

TWIN_FORMAT = 'train_step'
FWD_PARAMS = ['x', 'hg_norm', 'hg_w_in', 'hg_lb_logits', 'hg_out_norm', 'hg_w_out', 'kv_norm', 'w_kv', 'attn_norm', 'attn_w_q', 'attn_sinks', 'attn_w_o', 'ffn_norm', 'ffn_w_up', 'ffn_conv_w', 'ffn_conv_b', 'ffn_w_down', 'final_norm']
TWIN_WEIGHTS = ['hg_norm', 'hg_w_in', 'hg_lb_logits', 'hg_out_norm', 'hg_w_out', 'kv_norm', 'w_kv', 'attn_norm', 'attn_w_q', 'attn_sinks', 'attn_w_o', 'ffn_norm', 'ffn_w_up', 'ffn_conv_w', 'ffn_conv_b', 'ffn_w_down', 'final_norm']
TWIN_DIFF_INPUT = 'x'
TWIN_INPUTS = ['x', 'hg_norm', 'hg_w_in', 'hg_lb_logits', 'hg_out_norm', 'hg_w_out', 'kv_norm', 'w_kv', 'attn_norm', 'attn_w_q', 'attn_sinks', 'attn_w_o', 'ffn_norm', 'ffn_w_up', 'ffn_conv_w', 'ffn_conv_b', 'ffn_w_down', 'final_norm', 'loss_target', 'm_hg_norm', 'm_hg_w_in', 'm_hg_lb_logits', 'm_hg_out_norm', 'm_hg_w_out', 'm_kv_norm', 'm_w_kv', 'm_attn_norm', 'm_attn_w_q', 'm_attn_sinks', 'm_attn_w_o', 'm_ffn_norm', 'm_ffn_w_up', 'm_ffn_conv_w', 'm_ffn_conv_b', 'm_ffn_w_down', 'm_final_norm', 'v_hg_norm', 'v_hg_w_in', 'v_hg_lb_logits', 'v_hg_out_norm', 'v_hg_w_out', 'v_kv_norm', 'v_w_kv', 'v_attn_norm', 'v_attn_w_q', 'v_attn_sinks', 'v_attn_w_o', 'v_ffn_norm', 'v_ffn_w_up', 'v_ffn_conv_w', 'v_ffn_conv_b', 'v_ffn_w_down', 'v_final_norm']
TWIN_OUTPUTS = ['loss', 'grad_x', 'grad_hg_norm', 'grad_hg_w_in', 'grad_hg_lb_logits', 'grad_hg_out_norm', 'grad_hg_w_out', 'grad_kv_norm', 'grad_w_kv', 'grad_attn_norm', 'grad_attn_w_q', 'grad_attn_sinks', 'grad_attn_w_o', 'grad_ffn_norm', 'grad_ffn_w_up', 'grad_ffn_conv_w', 'grad_ffn_conv_b', 'grad_ffn_w_down', 'grad_final_norm', 'delta_hg_norm', 'delta_hg_w_in', 'delta_hg_lb_logits', 'delta_hg_out_norm', 'delta_hg_w_out', 'delta_kv_norm', 'delta_w_kv', 'delta_attn_norm', 'delta_attn_w_q', 'delta_attn_sinks', 'delta_attn_w_o', 'delta_ffn_norm', 'delta_ffn_w_up', 'delta_ffn_conv_w', 'delta_ffn_conv_b', 'delta_ffn_w_down', 'delta_final_norm', 'new_m_hg_norm', 'new_m_hg_w_in', 'new_m_hg_lb_logits', 'new_m_hg_out_norm', 'new_m_hg_w_out', 'new_m_kv_norm', 'new_m_w_kv', 'new_m_attn_norm', 'new_m_attn_w_q', 'new_m_attn_sinks', 'new_m_attn_w_o', 'new_m_ffn_norm', 'new_m_ffn_w_up', 'new_m_ffn_conv_w', 'new_m_ffn_conv_b', 'new_m_ffn_w_down', 'new_m_final_norm', 'new_v_hg_norm', 'new_v_hg_w_in', 'new_v_hg_lb_logits', 'new_v_hg_out_norm', 'new_v_hg_w_out', 'new_v_kv_norm', 'new_v_w_kv', 'new_v_attn_norm', 'new_v_attn_w_q', 'new_v_attn_sinks', 'new_v_attn_w_o', 'new_v_ffn_norm', 'new_v_ffn_w_up', 'new_v_ffn_conv_w', 'new_v_ffn_conv_b', 'new_v_ffn_w_down', 'new_v_final_norm']
TWIN_LEAF_KINDS = {'loss': 'loss', 'grad_x': 'grad_x', 'grad_hg_norm': 'grad_w', 'grad_hg_w_in': 'grad_w', 'grad_hg_lb_logits': 'grad_w', 'grad_hg_out_norm': 'grad_w', 'grad_hg_w_out': 'grad_w', 'grad_kv_norm': 'grad_w', 'grad_w_kv': 'grad_w', 'grad_attn_norm': 'grad_w', 'grad_attn_w_q': 'grad_w', 'grad_attn_sinks': 'grad_w', 'grad_attn_w_o': 'grad_w', 'grad_ffn_norm': 'grad_w', 'grad_ffn_w_up': 'grad_w', 'grad_ffn_conv_w': 'grad_w', 'grad_ffn_conv_b': 'grad_w', 'grad_ffn_w_down': 'grad_w', 'grad_final_norm': 'grad_w', 'delta_hg_norm': 'delta_w', 'delta_hg_w_in': 'delta_w', 'delta_hg_lb_logits': 'delta_w', 'delta_hg_out_norm': 'delta_w', 'delta_hg_w_out': 'delta_w', 'delta_kv_norm': 'delta_w', 'delta_w_kv': 'delta_w', 'delta_attn_norm': 'delta_w', 'delta_attn_w_q': 'delta_w', 'delta_attn_sinks': 'delta_w', 'delta_attn_w_o': 'delta_w', 'delta_ffn_norm': 'delta_w', 'delta_ffn_w_up': 'delta_w', 'delta_ffn_conv_w': 'delta_w', 'delta_ffn_conv_b': 'delta_w', 'delta_ffn_w_down': 'delta_w', 'delta_final_norm': 'delta_w', 'new_m_hg_norm': 'new_m', 'new_m_hg_w_in': 'new_m', 'new_m_hg_lb_logits': 'new_m', 'new_m_hg_out_norm': 'new_m', 'new_m_hg_w_out': 'new_m', 'new_m_kv_norm': 'new_m', 'new_m_w_kv': 'new_m', 'new_m_attn_norm': 'new_m', 'new_m_attn_w_q': 'new_m', 'new_m_attn_sinks': 'new_m', 'new_m_attn_w_o': 'new_m', 'new_m_ffn_norm': 'new_m', 'new_m_ffn_w_up': 'new_m', 'new_m_ffn_conv_w': 'new_m', 'new_m_ffn_conv_b': 'new_m', 'new_m_ffn_w_down': 'new_m', 'new_m_final_norm': 'new_m', 'new_v_hg_norm': 'new_v', 'new_v_hg_w_in': 'new_v', 'new_v_hg_lb_logits': 'new_v', 'new_v_hg_out_norm': 'new_v', 'new_v_hg_w_out': 'new_v', 'new_v_kv_norm': 'new_v', 'new_v_w_kv': 'new_v', 'new_v_attn_norm': 'new_v', 'new_v_attn_w_q': 'new_v', 'new_v_attn_sinks': 'new_v', 'new_v_attn_w_o': 'new_v', 'new_v_ffn_norm': 'new_v', 'new_v_ffn_w_up': 'new_v', 'new_v_ffn_conv_w': 'new_v', 'new_v_ffn_conv_b': 'new_v', 'new_v_ffn_w_down': 'new_v', 'new_v_final_norm': 'new_v'}


def _forward(args):
    return _fwd_reference(*[args[k] for k in FWD_PARAMS])


def _output_shape():
    out = _jax.eval_shape(lambda: _forward(_fwd_setup_inputs(0)))
    return out.shape, out.dtype

N_MICROBATCH = 1
ADAM_LR = 0.001
ADAM_B1 = 0.9
ADAM_B2 = 0.999
ADAM_EPS = 1e-08
ADAM_WD = 0.01
ADAM_STEP = 10
PER_EXAMPLE_BATCH_AXIS = {'x': 0, 'loss_target': 0}
SHARED_INPUTS = []
_WEIGHT_DTYPES = {'hg_norm': _jnp.float32, 'hg_w_in': _jnp.float32, 'hg_lb_logits': _jnp.float32, 'hg_out_norm': _jnp.float32, 'hg_w_out': _jnp.float32, 'kv_norm': _jnp.float32, 'w_kv': _jnp.float32, 'attn_norm': _jnp.float32, 'attn_w_q': _jnp.float32, 'attn_sinks': _jnp.float32, 'attn_w_o': _jnp.float32, 'ffn_norm': _jnp.float32, 'ffn_w_up': _jnp.float32, 'ffn_conv_w': _jnp.float32, 'ffn_conv_b': _jnp.float32, 'ffn_w_down': _jnp.float32, 'final_norm': _jnp.float32}
MOMENT_SCALE = {'hg_norm': 1.659103e-01, 'hg_w_in': 8.175690e-02, 'hg_lb_logits': 1.110561e-02, 'hg_out_norm': 3.863548e-01, 'hg_w_out': 1.126677e-01, 'kv_norm': 5.650929e-02, 'w_kv': 1.130546e-01, 'attn_norm': 3.417247e-02, 'attn_w_q': 3.364065e-02, 'attn_sinks': 6.377624e-02, 'attn_w_o': 4.513145e-02, 'ffn_norm': 1.224084e-01, 'ffn_w_up': 5.119153e-02, 'ffn_conv_w': 5.307401e-02, 'ffn_conv_b': 5.085150e-02, 'ffn_w_down': 8.328182e-02, 'final_norm': 3.200870e+01}


def _to_microbatches(a, axis):
    t = _jnp.moveaxis(a, axis, 0)
    t = t.reshape((N_MICROBATCH, t.shape[0] // N_MICROBATCH) + t.shape[1:])
    return _jnp.moveaxis(t, 1, axis + 1)


def setup_inputs(seed: int = 0) -> dict:
    inp = _fwd_setup_inputs(seed)
    key = _jax.random.fold_in(_jax.random.key(seed), 7919)
    shape, _ = _output_shape()
    out = dict(inp)
    out["loss_target"] = _jax.random.normal(_jax.random.fold_in(key, 0), shape, _jnp.float32)
    for i, name in enumerate(TWIN_WEIGHTS):
        w = inp[name].astype(_jnp.float32)
        if MOMENT_SCALE is None:
            s = _jnp.sqrt(_jnp.mean(_jnp.square(w)) + 1e-30)
        else:
            s = MOMENT_SCALE[name]
        km, kv = _jax.random.split(_jax.random.fold_in(key, i + 1))
        out[name] = w
        out["m_" + name] = s * _jax.random.normal(km, w.shape, _jnp.float32)
        out["v_" + name] = (s * s) * _jax.random.uniform(kv, w.shape, _jnp.float32, 0.5, 1.5)
    if N_MICROBATCH > 1:
        for name, axis in PER_EXAMPLE_BATCH_AXIS.items():
            out[name] = _to_microbatches(out[name], axis)
    return {'x': out['x'], 'hg_norm': out['hg_norm'], 'hg_w_in': out['hg_w_in'], 'hg_lb_logits': out['hg_lb_logits'], 'hg_out_norm': out['hg_out_norm'], 'hg_w_out': out['hg_w_out'], 'kv_norm': out['kv_norm'], 'w_kv': out['w_kv'], 'attn_norm': out['attn_norm'], 'attn_w_q': out['attn_w_q'], 'attn_sinks': out['attn_sinks'], 'attn_w_o': out['attn_w_o'], 'ffn_norm': out['ffn_norm'], 'ffn_w_up': out['ffn_w_up'], 'ffn_conv_w': out['ffn_conv_w'], 'ffn_conv_b': out['ffn_conv_b'], 'ffn_w_down': out['ffn_w_down'], 'final_norm': out['final_norm'], 'loss_target': out['loss_target'], 'm_hg_norm': out['m_hg_norm'], 'm_hg_w_in': out['m_hg_w_in'], 'm_hg_lb_logits': out['m_hg_lb_logits'], 'm_hg_out_norm': out['m_hg_out_norm'], 'm_hg_w_out': out['m_hg_w_out'], 'm_kv_norm': out['m_kv_norm'], 'm_w_kv': out['m_w_kv'], 'm_attn_norm': out['m_attn_norm'], 'm_attn_w_q': out['m_attn_w_q'], 'm_attn_sinks': out['m_attn_sinks'], 'm_attn_w_o': out['m_attn_w_o'], 'm_ffn_norm': out['m_ffn_norm'], 'm_ffn_w_up': out['m_ffn_w_up'], 'm_ffn_conv_w': out['m_ffn_conv_w'], 'm_ffn_conv_b': out['m_ffn_conv_b'], 'm_ffn_w_down': out['m_ffn_w_down'], 'm_final_norm': out['m_final_norm'], 'v_hg_norm': out['v_hg_norm'], 'v_hg_w_in': out['v_hg_w_in'], 'v_hg_lb_logits': out['v_hg_lb_logits'], 'v_hg_out_norm': out['v_hg_out_norm'], 'v_hg_w_out': out['v_hg_w_out'], 'v_kv_norm': out['v_kv_norm'], 'v_w_kv': out['v_w_kv'], 'v_attn_norm': out['v_attn_norm'], 'v_attn_w_q': out['v_attn_w_q'], 'v_attn_sinks': out['v_attn_sinks'], 'v_attn_w_o': out['v_attn_w_o'], 'v_ffn_norm': out['v_ffn_norm'], 'v_ffn_w_up': out['v_ffn_w_up'], 'v_ffn_conv_w': out['v_ffn_conv_w'], 'v_ffn_conv_b': out['v_ffn_conv_b'], 'v_ffn_w_down': out['v_ffn_w_down'], 'v_final_norm': out['v_final_norm']}


def _loss(weights, diff, rest, loss_target):
    with _jax.named_scope("forward"):
        args = {**rest, TWIN_DIFF_INPUT: diff, **{k: w.astype(_WEIGHT_DTYPES[k]) for k, w in weights.items()}}
        y = _forward(args)
    with _jax.named_scope("loss_head"):
        err = _jnp.square(y.astype(_jnp.float32) - loss_target)
        return 0.5 * _jnp.sum(_jnp.mean(err, axis=-1)) if err.ndim else 0.5 * err


def _adamw(w, g, m, v):
    m = ADAM_B1 * m + (1.0 - ADAM_B1) * g
    v = ADAM_B2 * v + (1.0 - ADAM_B2) * _jnp.square(g)
    m_hat = m / (1.0 - ADAM_B1 ** ADAM_STEP)
    v_hat = v / (1.0 - ADAM_B2 ** ADAM_STEP)
    delta = -ADAM_LR * (m_hat / (_jnp.sqrt(v_hat) + ADAM_EPS) + ADAM_WD * w)
    return delta, m, v


def reference(x, hg_norm, hg_w_in, hg_lb_logits, hg_out_norm, hg_w_out, kv_norm, w_kv, attn_norm, attn_w_q, attn_sinks, attn_w_o, ffn_norm, ffn_w_up, ffn_conv_w, ffn_conv_b, ffn_w_down, final_norm, loss_target, m_hg_norm, m_hg_w_in, m_hg_lb_logits, m_hg_out_norm, m_hg_w_out, m_kv_norm, m_w_kv, m_attn_norm, m_attn_w_q, m_attn_sinks, m_attn_w_o, m_ffn_norm, m_ffn_w_up, m_ffn_conv_w, m_ffn_conv_b, m_ffn_w_down, m_final_norm, v_hg_norm, v_hg_w_in, v_hg_lb_logits, v_hg_out_norm, v_hg_w_out, v_kv_norm, v_w_kv, v_attn_norm, v_attn_w_q, v_attn_sinks, v_attn_w_o, v_ffn_norm, v_ffn_w_up, v_ffn_conv_w, v_ffn_conv_b, v_ffn_w_down, v_final_norm):
    given = dict(x=x, hg_norm=hg_norm, hg_w_in=hg_w_in, hg_lb_logits=hg_lb_logits, hg_out_norm=hg_out_norm, hg_w_out=hg_w_out, kv_norm=kv_norm, w_kv=w_kv, attn_norm=attn_norm, attn_w_q=attn_w_q, attn_sinks=attn_sinks, attn_w_o=attn_w_o, ffn_norm=ffn_norm, ffn_w_up=ffn_w_up, ffn_conv_w=ffn_conv_w, ffn_conv_b=ffn_conv_b, ffn_w_down=ffn_w_down, final_norm=final_norm, loss_target=loss_target, m_hg_norm=m_hg_norm, m_hg_w_in=m_hg_w_in, m_hg_lb_logits=m_hg_lb_logits, m_hg_out_norm=m_hg_out_norm, m_hg_w_out=m_hg_w_out, m_kv_norm=m_kv_norm, m_w_kv=m_w_kv, m_attn_norm=m_attn_norm, m_attn_w_q=m_attn_w_q, m_attn_sinks=m_attn_sinks, m_attn_w_o=m_attn_w_o, m_ffn_norm=m_ffn_norm, m_ffn_w_up=m_ffn_w_up, m_ffn_conv_w=m_ffn_conv_w, m_ffn_conv_b=m_ffn_conv_b, m_ffn_w_down=m_ffn_w_down, m_final_norm=m_final_norm, v_hg_norm=v_hg_norm, v_hg_w_in=v_hg_w_in, v_hg_lb_logits=v_hg_lb_logits, v_hg_out_norm=v_hg_out_norm, v_hg_w_out=v_hg_w_out, v_kv_norm=v_kv_norm, v_w_kv=v_w_kv, v_attn_norm=v_attn_norm, v_attn_w_q=v_attn_w_q, v_attn_sinks=v_attn_sinks, v_attn_w_o=v_attn_w_o, v_ffn_norm=v_ffn_norm, v_ffn_w_up=v_ffn_w_up, v_ffn_conv_w=v_ffn_conv_w, v_ffn_conv_b=v_ffn_conv_b, v_ffn_w_down=v_ffn_w_down, v_final_norm=v_final_norm)
    weights = {n: given[n] for n in TWIN_WEIGHTS}
    shared = {n: given[n] for n in SHARED_INPUTS}
    per_example = {n: given[n] for n in ['x']}
    grad_fn = _jax.value_and_grad(_loss, argnums=(0, 1))

    def one_microbatch(ex, loss_target):
        ex = dict(ex)
        diff = ex.pop(TWIN_DIFF_INPUT)
        return grad_fn(weights, diff, {**shared, **ex}, loss_target)

    if N_MICROBATCH == 1:
        loss, (grad_w, grad_x) = one_microbatch(per_example, given["loss_target"])
    else:
        def body(carry, xs):
            loss_sum, grad_sum = carry
            l_k, (gw_k, gx_k) = one_microbatch(xs[0], xs[1])
            with _jax.named_scope("update"):
                return (loss_sum + l_k, _jax.tree.map(_jnp.add, grad_sum, gw_k)), gx_k

        init = (_jnp.zeros((), _jnp.float32), _jax.tree.map(_jnp.zeros_like, weights))
        (loss, grad_w), grad_x = _jax.lax.scan(body, init, (per_example, given["loss_target"]))
    with _jax.named_scope("update"):
        delta_w, new_m, new_v = {}, {}, {}
        for n in TWIN_WEIGHTS:
            delta_w[n], new_m[n], new_v[n] = _adamw(weights[n], grad_w[n], given["m_" + n], given["v_" + n])
    return (loss, grad_x, *[grad_w[n] for n in TWIN_WEIGHTS], *[delta_w[n] for n in TWIN_WEIGHTS],
            *[new_m[n] for n in TWIN_WEIGHTS], *[new_v[n] for n in TWIN_WEIGHTS])
```

```python
import functools

import jax
import jax.numpy as jnp
from jax import lax
from jax.experimental import pallas as pl
from jax.experimental.pallas import tpu as pltpu

F32 = jnp.float32
BF16 = jnp.bfloat16
MESH = pl.DeviceIdType.MESH

EPS = 1e-6
D_MODEL = 1024
HG_HEADS = 8
HG_DK = 128
HG_CHUNK = 64
ATT_HD = 64
ATT_QH = 16
ATT_KVH = 2
ATT_GROUP = ATT_QH // ATT_KVH
WINDOW = 128
D_FF = 2816
N_CHIPS = 4
N_DEV = 8
LANES = 128
SUBLANES = 8
VMEM_LIMIT_BYTES = 56 * 1024 * 1024
NEG = -1e30
ALIBI_SLOPES = tuple(2.0 ** (-8.0 * h / ATT_QH) for h in range(1, ATT_QH + 1))

ADAM_LR = 0.001
ADAM_B1 = 0.9
ADAM_B2 = 0.999
ADAM_EPS = 1e-08
ADAM_WD = 0.01
ADAM_STEP = 10


def _cparams(sem=None):
    return pltpu.CompilerParams(dimension_semantics=sem, vmem_limit_bytes=VMEM_LIMIT_BYTES)


def _pick(n, cands):
    for c in cands:
        if n % c == 0:
            return c
    return n


def _sigmoid(x):
    return 1.0 / (1.0 + jnp.exp(-x))


def _dot(a, b, dims):
    return lax.dot_general(a, b, (dims, ((), ())), preferred_element_type=F32)


NN = ((1,), (0,))
NT = ((1,), (1,))
TN = ((0,), (0,))


def _mm_nn(a, w, res=None, out_dtype=F32, name="mm_nn"):
    m, k = a.shape
    s, _, ns = w.shape
    tm = min(m, 512)
    tn = _pick(ns, (512, 1408, 256, 128))
    npb = ns // tn

    def body(a_ref, w_ref, *rest):
        o_ref = rest[-1]
        acc = _dot(a_ref[...].astype(BF16), w_ref[...], NN)
        if res is not None:
            acc = acc + rest[0][...]
        o_ref[...] = acc.astype(o_ref.dtype)

    in_specs = [
        pl.BlockSpec((tm, k), lambda i, j: (i, 0)),
        pl.BlockSpec((None, k, tn), lambda i, j: (j // npb, 0, j % npb)),
    ]
    args = [a, w]
    if res is not None:
        in_specs.append(pl.BlockSpec((tm, tn), lambda i, j: (i, j)))
        args.append(res)
    return pl.pallas_call(
        body,
        name=name,
        grid=(m // tm, s * npb),
        in_specs=in_specs,
        out_specs=pl.BlockSpec((tm, tn), lambda i, j: (i, j)),
        out_shape=jax.ShapeDtypeStruct((m, s * ns), out_dtype),
        compiler_params=_cparams(("parallel", "parallel")),
    )(*args)


def _dy_spec(stacked, tm, tn, npb, row, kk):
    if stacked:
        return pl.BlockSpec((None, tm, tn), lambda *g: (kk(g) // npb, row(g), kk(g) % npb))
    return pl.BlockSpec((tm, tn), lambda *g: (row(g), kk(g)))


def _mm_nt(dy, w, stacked=False, out_dtype=F32, name="mm_nt"):
    s, k, ns = w.shape
    m = dy.shape[1] if stacked else dy.shape[0]
    tm = min(m, 512)
    tko = _pick(k, (1024, 1408, 512, 256))
    tn = _pick(ns, (1024, 1408, 512, 256))
    npb = ns // tn
    nk = s * npb

    def body(dy_ref, w_ref, o_ref, acc_ref):
        kk = pl.program_id(2)

        @pl.when(kk == 0)
        def _():
            acc_ref[...] = jnp.zeros_like(acc_ref)

        acc_ref[...] += _dot(dy_ref[...].astype(BF16), w_ref[...], NT)

        @pl.when(kk == nk - 1)
        def _():
            o_ref[...] = acc_ref[...].astype(o_ref.dtype)

    return pl.pallas_call(
        body,
        name=name,
        grid=(m // tm, k // tko, nk),
        in_specs=[
            _dy_spec(stacked, tm, tn, npb, lambda g: g[0], lambda g: g[2]),
            pl.BlockSpec((None, tko, tn), lambda i, j, kk: (kk // npb, j, kk % npb)),
        ],
        out_specs=pl.BlockSpec((tm, tko), lambda i, j, kk: (i, j)),
        out_shape=jax.ShapeDtypeStruct((m, k), out_dtype),
        scratch_shapes=[pltpu.VMEM((tm, tko), F32)],
        compiler_params=_cparams(("parallel", "parallel", "arbitrary")),
    )(dy, w)


def _mm_tn(a, dy, s, ns, stacked=False, name="mm_tn"):
    m, k = a.shape
    tm = min(m, 512)
    tk = _pick(k, (1024, 1408, 512, 256))
    tn = _pick(ns, (512, 1408, 256, 128))
    npb = ns // tn
    nm = m // tm

    def body(a_ref, dy_ref, o_ref, acc_ref):
        mm = pl.program_id(2)

        @pl.when(mm == 0)
        def _():
            acc_ref[...] = jnp.zeros_like(acc_ref)

        acc_ref[...] += _dot(a_ref[...].astype(BF16), dy_ref[...].astype(BF16), TN)

        @pl.when(mm == nm - 1)
        def _():
            o_ref[...] = acc_ref[...]

    return pl.pallas_call(
        body,
        name=name,
        grid=(k // tk, s * npb, nm),
        in_specs=[
            pl.BlockSpec((tm, tk), lambda i, j, mm: (mm, i)),
            _dy_spec(stacked, tm, tn, npb, lambda g: g[2], lambda g: g[1]),
        ],
        out_specs=pl.BlockSpec((None, tk, tn), lambda i, j, mm: (j // npb, i, j % npb)),
        out_shape=jax.ShapeDtypeStruct((s, k, ns), F32),
        scratch_shapes=[pltpu.VMEM((tk, tn), F32)],
        compiler_params=_cparams(("parallel", "parallel", "arbitrary")),
    )(a, dy)


ROW_TILE = 256


def _rms_fwd(x, g, name="rms_fwd"):
    t, d = x.shape
    r = min(t, ROW_TILE)

    def body(x_ref, g_ref, o_ref):
        xv = x_ref[...]
        rstd = lax.rsqrt(jnp.mean(xv * xv, axis=-1, keepdims=True) + EPS)
        o_ref[...] = (xv * rstd * g_ref[...]).astype(BF16)

    return pl.pallas_call(
        body,
        name=name,
        grid=(t // r,),
        in_specs=[pl.BlockSpec((r, d), lambda i: (i, 0)), pl.BlockSpec((1, d), lambda i: (0, 0))],
        out_specs=pl.BlockSpec((r, d), lambda i: (i, 0)),
        out_shape=jax.ShapeDtypeStruct((t, d), BF16),
        compiler_params=_cparams(("parallel",)),
    )(x, g)


def _rms_bwd(x, g, dxn, dres, name="rms_bwd"):
    t, d = x.shape
    r = min(t, ROW_TILE)

    def body(x_ref, g_ref, dxn_ref, dres_ref, dx_ref, dg_ref):
        @pl.when(pl.program_id(0) == 0)
        def _():
            dg_ref[...] = jnp.zeros_like(dg_ref)

        xv = x_ref[...]
        rstd = lax.rsqrt(jnp.mean(xv * xv, axis=-1, keepdims=True) + EPS)
        xhat = xv * rstd
        dxn_v = dxn_ref[...].astype(F32)
        gd = dxn_v * g_ref[...]
        dx_ref[...] = dres_ref[...] + rstd * (gd - xhat * jnp.mean(gd * xhat, axis=-1, keepdims=True))
        dg_ref[...] += jnp.sum(dxn_v * xhat, axis=0, keepdims=True)

    return pl.pallas_call(
        body,
        name=name,
        grid=(t // r,),
        in_specs=[
            pl.BlockSpec((r, d), lambda i: (i, 0)),
            pl.BlockSpec((1, d), lambda i: (0, 0)),
            pl.BlockSpec((r, d), lambda i: (i, 0)),
            pl.BlockSpec((r, d), lambda i: (i, 0)),
        ],
        out_specs=[pl.BlockSpec((r, d), lambda i: (i, 0)), pl.BlockSpec((1, d), lambda i: (0, 0))],
        out_shape=[jax.ShapeDtypeStruct((t, d), F32), jax.ShapeDtypeStruct((1, d), F32)],
        compiler_params=_cparams(("arbitrary",)),
    )(x, g, dxn, dres)


def _loss_head(h, g, target):
    t, d = h.shape
    r = min(t, ROW_TILE)

    def body(h_ref, g_ref, t_ref, dh_ref, dg_ref, loss_ref):
        @pl.when(pl.program_id(0) == 0)
        def _():
            dg_ref[...] = jnp.zeros_like(dg_ref)
            loss_ref[...] = jnp.zeros_like(loss_ref)

        xv = h_ref[...]
        rstd = lax.rsqrt(jnp.mean(xv * xv, axis=-1, keepdims=True) + EPS)
        xhat = xv * rstd
        gv = g_ref[...]
        err = xhat * gv - t_ref[...]
        loss_ref[...] += 0.5 * jnp.sum(jnp.mean(err * err, axis=-1, keepdims=True), axis=0, keepdims=True)
        dy = err * (1.0 / d)
        gd = dy * gv
        dh_ref[...] = rstd * (gd - xhat * jnp.mean(gd * xhat, axis=-1, keepdims=True))
        dg_ref[...] += jnp.sum(dy * xhat, axis=0, keepdims=True)

    return pl.pallas_call(
        body,
        name="loss_head",
        grid=(t // r,),
        in_specs=[
            pl.BlockSpec((r, d), lambda i: (i, 0)),
            pl.BlockSpec((1, d), lambda i: (0, 0)),
            pl.BlockSpec((r, d), lambda i: (i, 0)),
        ],
        out_specs=[
            pl.BlockSpec((r, d), lambda i: (i, 0)),
            pl.BlockSpec((1, d), lambda i: (0, 0)),
            pl.BlockSpec((1, LANES), lambda i: (0, 0)),
        ],
        out_shape=[
            jax.ShapeDtypeStruct((t, d), F32),
            jax.ShapeDtypeStruct((1, d), F32),
            jax.ShapeDtypeStruct((1, LANES), F32),
        ],
        compiler_params=_cparams(("arbitrary",)),
    )(h, g, target)


CONV_ROWS = 128
CONV_COLS = 1408


def _conv_taps(x_ext, n):
    tot = x_ext.shape[0]
    g1 = pltpu.roll(x_ext, 1, 0)[tot - n:]
    g2 = pltpu.roll(x_ext, 2, 0)[tot - n:]
    return g2, g1


def _conv_fwd(up, conv_w, conv_b, name="conv_fwd"):
    t = up.shape[0]
    r = min(t, CONV_ROWS)
    tc = CONV_COLS
    ncb = D_FF // tc
    hb = r // SUBLANES

    def body(g_ref, halo_ref, v_ref, w_ref, b_ref, o_ref):
        i = pl.program_id(1)
        g0 = g_ref[...]
        halo = halo_ref[...] * jnp.where(i > 0, 1.0, 0.0)
        g2, g1 = _conv_taps(jnp.concatenate([halo, g0], axis=0), r)
        c = b_ref[...] + w_ref[0:1, :] * g2 + w_ref[1:2, :] * g1 + w_ref[2:3, :] * g0
        o_ref[...] = (c * _sigmoid(c) * v_ref[...]).astype(BF16)

    return pl.pallas_call(
        body,
        name=name,
        grid=(ncb, t // r),
        in_specs=[
            pl.BlockSpec((r, tc), lambda j, i: (i, j)),
            pl.BlockSpec((SUBLANES, tc), lambda j, i: (jnp.maximum(i * hb - 1, 0), j)),
            pl.BlockSpec((r, tc), lambda j, i: (i, ncb + j)),
            pl.BlockSpec((3, tc), lambda j, i: (0, j)),
            pl.BlockSpec((1, tc), lambda j, i: (0, j)),
        ],
        out_specs=pl.BlockSpec((r, tc), lambda j, i: (i, j)),
        out_shape=jax.ShapeDtypeStruct((t, D_FF), BF16),
        compiler_params=_cparams(("parallel", "parallel")),
    )(up, up, up, conv_w, conv_b)


def _conv_bwd(up, conv_w, conv_b, dact, name="conv_bwd"):
    t = up.shape[0]
    r = min(t, CONV_ROWS)
    tc = CONV_COLS
    ncb = D_FF // tc
    hb = r // SUBLANES
    nrt = t // r

    def body(g_ref, halo_ref, v_ref, w_ref, b_ref, da_ref, dup_ref, dw_ref, db_ref, nxt_ref):
        ii = pl.program_id(1)
        i = nrt - 1 - ii

        @pl.when(ii == 0)
        def _():
            nxt_ref[...] = jnp.zeros_like(nxt_ref)
            dw_ref[...] = jnp.zeros_like(dw_ref)
            db_ref[...] = jnp.zeros_like(db_ref)

        g0 = g_ref[...]
        halo = halo_ref[...] * jnp.where(i > 0, 1.0, 0.0)
        g2, g1 = _conv_taps(jnp.concatenate([halo, g0], axis=0), r)
        w0, w1, w2 = w_ref[0:1, :], w_ref[1:2, :], w_ref[2:3, :]
        c = b_ref[...] + w0 * g2 + w1 * g1 + w2 * g0
        sg = _sigmoid(c)
        da = da_ref[...]
        dval = da * (c * sg)
        dc = da * v_ref[...] * (sg * (1.0 + c * (1.0 - sg)))
        db_ref[...] += jnp.sum(dc, axis=0, keepdims=True)
        dw_ref[0:1, :] += jnp.sum(dc * g2, axis=0, keepdims=True)
        dw_ref[1:2, :] += jnp.sum(dc * g1, axis=0, keepdims=True)
        dw_ref[2:3, :] += jnp.sum(dc * g0, axis=0, keepdims=True)
        ext = jnp.concatenate([dc, nxt_ref[...]], axis=0)
        tot = r + SUBLANES
        d1 = pltpu.roll(ext, tot - 1, 0)[:r]
        d2 = pltpu.roll(ext, tot - 2, 0)[:r]
        dgate = w2 * dc + w1 * d1 + w0 * d2
        nxt_ref[...] = dc[:SUBLANES]
        dup_ref[0] = dgate.astype(BF16)
        dup_ref[1] = dval.astype(BF16)

    rev = lambda ii: nrt - 1 - ii
    dup, dw, db = pl.pallas_call(
        body,
        name=name,
        grid=(ncb, nrt),
        in_specs=[
            pl.BlockSpec((r, tc), lambda j, ii: (rev(ii), j)),
            pl.BlockSpec((SUBLANES, tc), lambda j, ii: (jnp.maximum(rev(ii) * hb - 1, 0), j)),
            pl.BlockSpec((r, tc), lambda j, ii: (rev(ii), ncb + j)),
            pl.BlockSpec((3, tc), lambda j, ii: (0, j)),
            pl.BlockSpec((1, tc), lambda j, ii: (0, j)),
            pl.BlockSpec((r, tc), lambda j, ii: (rev(ii), j)),
        ],
        out_specs=[
            pl.BlockSpec((2, None, r, tc), lambda j, ii: (0, j, rev(ii), 0)),
            pl.BlockSpec((3, tc), lambda j, ii: (0, j)),
            pl.BlockSpec((1, tc), lambda j, ii: (0, j)),
        ],
        out_shape=[
            jax.ShapeDtypeStruct((2, ncb, t, tc), BF16),
            jax.ShapeDtypeStruct((3, D_FF), F32),
            jax.ShapeDtypeStruct((1, D_FF), F32),
        ],
        scratch_shapes=[pltpu.VMEM((SUBLANES, tc), F32)],
        compiler_params=_cparams(("parallel", "arbitrary")),
    )(up, up, up, conv_w, conv_b, dact)
    return dup.reshape(2 * ncb, t, tc), dw, db


def _split3(x):
    x1 = x.astype(BF16)
    r1 = x - x1.astype(F32)
    x2 = r1.astype(BF16)
    x3 = (r1 - x2.astype(F32)).astype(BF16)
    return x1, x2, x3


def _tri_dot(tri, x, dims):
    x1, x2, x3 = _split3(x)
    return _dot(tri, x1, dims) + _dot(tri, x2, dims) + _dot(tri, x3, dims)


def _lower_bound(logits_ref):
    return _sigmoid(logits_ref[0:1, :] - logits_ref[1:2, :])


def _hg_gates(qr, fr, lb):
    q = qr * _sigmoid(qr) * (HG_DK ** -0.5)
    sf = _sigmoid(fr)
    fg = lb + (1.0 - lb) * sf
    return q, sf, fg


def _hg_chunk_terms(q, fg, tril_b, low_half):
    g = jnp.log(fg)
    k = 1.0 - fg
    cum = _tri_dot(tril_b, g, NN)
    c_last = jnp.sum(g, axis=0, keepdims=True)
    c_mid = jnp.sum(jnp.where(low_half, g, 0.0), axis=0, keepdims=True)
    e_q = jnp.exp(cum - c_mid)
    e_k = jnp.exp(c_mid - cum)
    e_0 = jnp.exp(cum)
    e_l = jnp.exp(c_last - cum)
    return k, e_q, e_k, e_0, e_l, jnp.exp(c_last)


def _hg_specs(t, col0s):
    return [pl.BlockSpec((t, HG_DK), functools.partial(lambda h, c0: (0, c0 + h), c0=c0)) for c0 in col0s]


def _hgrn_fwd(proj, lb, wn):
    t = proj.shape[0]
    c = HG_CHUNK
    nc = t // c

    def body(q_ref, f_ref, i_ref, g_ref, lb_ref, wn_ref, o_ref, y_ref, st_ref, s_scr):
        s_scr[...] = jnp.zeros_like(s_scr)
        lbv = _lower_bound(lb_ref)
        wnv = wn_ref[...]
        ri = lax.broadcasted_iota(jnp.int32, (c, c), 0)
        ci = lax.broadcasted_iota(jnp.int32, (c, c), 1)
        tril = ri >= ci
        tril_b = tril.astype(BF16)
        low_half = lax.broadcasted_iota(jnp.int32, (c, HG_DK), 0) < c // 2

        def chunk(n, carry):
            rows = pl.ds(pl.multiple_of(n * c, c), c)
            q, _, fg = _hg_gates(q_ref[rows, :], f_ref[rows, :], lbv)
            v = i_ref[rows, :].astype(BF16)
            k, e_q, e_k, e_0, e_l, e_last = _hg_chunk_terms(q, fg, tril_b, low_half)
            st = s_scr[...]
            st_ref[n] = st
            a = jnp.where(tril, _dot((q * e_q).astype(BF16), (k * e_k).astype(BF16), NT), 0.0)
            o = _dot((q * e_0).astype(BF16), st.astype(BF16), NT) + _dot(a.astype(BF16), v, NN)
            s_scr[...] = st * e_last + _dot(v, (k * e_l).astype(BF16), TN)
            o_ref[rows, :] = o
            rstd = lax.rsqrt(jnp.mean(o * o, axis=-1, keepdims=True) + EPS)
            gr = g_ref[rows, :]
            y_ref[rows, :] = (o * rstd * wnv * (gr * _sigmoid(gr))).astype(BF16)
            return carry

        lax.fori_loop(0, nc, chunk, 0)

    vec = pl.BlockSpec((2, HG_DK), lambda h: (0, h))
    return pl.pallas_call(
        body,
        name="hgrn_fwd",
        grid=(HG_HEADS,),
        in_specs=_hg_specs(t, (0, HG_HEADS, 2 * HG_HEADS, 3 * HG_HEADS)) + [vec, pl.BlockSpec((1, HG_DK), lambda h: (0, 0))],
        out_specs=[
            pl.BlockSpec((t, HG_DK), lambda h: (0, h)),
            pl.BlockSpec((t, HG_DK), lambda h: (0, h)),
            pl.BlockSpec((None, nc, HG_DK, HG_DK), lambda h: (h, 0, 0, 0)),
        ],
        out_shape=[
            jax.ShapeDtypeStruct((t, D_MODEL), F32),
            jax.ShapeDtypeStruct((t, D_MODEL), BF16),
            jax.ShapeDtypeStruct((HG_HEADS, nc, HG_DK, HG_DK), F32),
        ],
        scratch_shapes=[pltpu.VMEM((HG_DK, HG_DK), F32)],
        compiler_params=_cparams(("parallel",)),
    )(proj, proj, proj, proj, lb, wn)


def _hgrn_bwd(proj, lb, wn, o, states, dy):
    t = proj.shape[0]
    c = HG_CHUNK
    nc = t // c

    def body(q_ref, f_ref, i_ref, g_ref, lb_ref, wn_ref, o_ref, st_ref, dy_ref, dp_ref, dl_ref, dwn_ref, ds_scr, dlb_scr):
        @pl.when(pl.program_id(0) == 0)
        def _():
            dwn_ref[...] = jnp.zeros_like(dwn_ref)

        ds_scr[...] = jnp.zeros_like(ds_scr)
        dlb_scr[...] = jnp.zeros_like(dlb_scr)
        lbv = _lower_bound(lb_ref)
        wnv = wn_ref[...]
        ri = lax.broadcasted_iota(jnp.int32, (c, c), 0)
        ci = lax.broadcasted_iota(jnp.int32, (c, c), 1)
        tril = ri >= ci
        tril_b = tril.astype(BF16)
        low_half = lax.broadcasted_iota(jnp.int32, (c, HG_DK), 0) < c // 2

        def chunk(nn, carry):
            n = nc - 1 - nn
            rows = pl.ds(pl.multiple_of(n * c, c), c)
            ov = o_ref[rows, :]
            gr = g_ref[rows, :]
            dyv = dy_ref[rows, :].astype(F32)
            rstd = lax.rsqrt(jnp.mean(ov * ov, axis=-1, keepdims=True) + EPS)
            ohat = ov * rstd
            sg = _sigmoid(gr)
            dg_raw = dyv * (ohat * wnv) * (sg * (1.0 + gr * (1.0 - sg)))
            don = dyv * (gr * sg)
            dwn_ref[...] += jnp.sum(don * ohat, axis=0, keepdims=True)
            gd = don * wnv
            do = rstd * (gd - ohat * jnp.mean(gd * ohat, axis=-1, keepdims=True))
            do_b = do.astype(BF16)
            qr = q_ref[rows, :]
            q, sf, fg = _hg_gates(qr, f_ref[rows, :], lbv)
            v = i_ref[rows, :].astype(BF16)
            k, e_q, e_k, e_0, e_l, e_last = _hg_chunk_terms(q, fg, tril_b, low_half)
            qi, qi_lo, _ = _split3(q * e_q)
            ki, ki_lo, _ = _split3(k * e_k)
            q0 = (q * e_0).astype(BF16)
            kl = (k * e_l).astype(BF16)
            st = st_ref[n]
            st_b = st.astype(BF16)
            ds = ds_scr[...]
            ds_b = ds.astype(BF16)
            a_b = jnp.where(tril, _dot(qi, ki, NT), 0.0).astype(BF16)
            da_b = jnp.where(tril, _dot(do_b, v, NT), 0.0).astype(BF16)
            dq = _dot(do_b, st_b, NN) * e_0 + (_dot(da_b, ki, NN) + _dot(da_b, ki_lo, NN)) * e_q
            dk_state = _dot(v, ds_b, NN) * e_l
            dk = (_dot(da_b, qi, TN) + _dot(da_b, qi_lo, TN)) * e_k + dk_state
            dv = _dot(a_b, do_b, TN) + _dot(kl, ds_b, NT)
            ds_scr[...] = ds * e_last + _dot(do_b, q0, TN)
            d_last = jnp.sum(dk_state * k, axis=0, keepdims=True) + jnp.sum(ds * st, axis=0, keepdims=True) * e_last
            dlogf = _tri_dot(tril_b, q * dq - k * dk, TN) + d_last
            dfg = dlogf / fg - dk
            dlb_scr[...] += jnp.sum(dfg * (1.0 - sf), axis=0, keepdims=True)
            sq = _sigmoid(qr)
            dp_ref[0, rows, :] = (dq * (HG_DK ** -0.5) * (sq * (1.0 + qr * (1.0 - sq)))).astype(BF16)
            dp_ref[1, rows, :] = (dfg * (1.0 - lbv) * sf * (1.0 - sf)).astype(BF16)
            dp_ref[2, rows, :] = dv.astype(BF16)
            dp_ref[3, rows, :] = dg_raw.astype(BF16)
            return carry

        lax.fori_loop(0, nc, chunk, 0)
        d0 = dlb_scr[...] * lbv * (1.0 - lbv)
        dl_ref[0:1, :] = d0
        dl_ref[1:2, :] = -d0

    vec = pl.BlockSpec((2, HG_DK), lambda h: (0, h))
    one = pl.BlockSpec((1, HG_DK), lambda h: (0, 0))
    col = pl.BlockSpec((t, HG_DK), lambda h: (0, h))
    return pl.pallas_call(
        body,
        name="hgrn_bwd",
        grid=(HG_HEADS,),
        in_specs=_hg_specs(t, (0, HG_HEADS, 2 * HG_HEADS, 3 * HG_HEADS))
        + [vec, one, col, pl.BlockSpec((None, nc, HG_DK, HG_DK), lambda h: (h, 0, 0, 0)), col],
        out_specs=[pl.BlockSpec((4, t, HG_DK), lambda h: (0, 0, h)), vec, one],
        out_shape=[
            jax.ShapeDtypeStruct((4, t, D_MODEL), BF16),
            jax.ShapeDtypeStruct((2, D_MODEL), F32),
            jax.ShapeDtypeStruct((1, HG_DK), F32),
        ],
        scratch_shapes=[pltpu.VMEM((HG_DK, HG_DK), F32), pltpu.VMEM((1, HG_DK), F32)],
        compiler_params=_cparams(("arbitrary",)),
    )(proj, proj, proj, proj, lb, wn, o, states, dy)


def _att_masks(n):
    tq = lax.broadcasted_iota(jnp.int32, (WINDOW, WINDOW), 0)
    sk = lax.broadcasted_iota(jnp.int32, (WINDOW, WINDOW), 1)
    valid_c = sk <= tq
    valid_p = (sk - tq) > jnp.where(n > 0, 0, WINDOW)
    dist_c = (tq - sk).astype(F32)
    dist_p = dist_c + float(WINDOW)
    return valid_p, valid_c, dist_p, dist_c


def _att_halves(x, lo, kh):
    r = pltpu.roll(x, ATT_HD, 1)
    zero = jnp.zeros_like(x)
    if kh == 0:
        return jnp.where(lo, x, r), jnp.where(lo, x, zero), jnp.where(lo, zero, r)
    return jnp.where(lo, r, x), jnp.where(lo, r, zero), jnp.where(lo, zero, x)


def _att_probs(qm, k2p, k2c, masks, slope, sink):
    valid_p, valid_c, dist_p, dist_c = masks
    sp = jnp.where(valid_p, _dot(qm, k2p, NT) * (ATT_HD ** -0.5) - slope * dist_p, NEG)
    sc = jnp.where(valid_c, _dot(qm, k2c, NT) * (ATT_HD ** -0.5) - slope * dist_c, NEG)
    m = jnp.maximum(jnp.maximum(jnp.max(sp, axis=-1, keepdims=True), jnp.max(sc, axis=-1, keepdims=True)), sink)
    ep = jnp.exp(sp - m)
    ec = jnp.exp(sc - m)
    es = jnp.exp(sink - m)
    inv = 1.0 / (jnp.sum(ep, axis=-1, keepdims=True) + jnp.sum(ec, axis=-1, keepdims=True) + es)
    return ep * inv, ec * inv, es * inv


def _attn_fwd(q, kv, sinks):
    t = q.shape[0]
    nb = t // WINDOW

    def body(sink_ref, q_ref, kvp_ref, kvc_ref, o_ref):
        n = pl.program_id(0)
        masks = _att_masks(n)
        lo = lax.broadcasted_iota(jnp.int32, (WINDOW, LANES), 1) < ATT_HD
        for kh in range(ATT_KVH):
            k2p, _, _ = _att_halves(kvp_ref[:, 0:LANES], lo, kh)
            k2c, _, _ = _att_halves(kvc_ref[:, 0:LANES], lo, kh)
            _, vlo_p, vhi_p = _att_halves(kvp_ref[:, LANES:2 * LANES], lo, kh)
            _, vlo_c, vhi_c = _att_halves(kvc_ref[:, LANES:2 * LANES], lo, kh)
            for jj in range(ATT_GROUP // 2):
                j = kh * (ATT_GROUP // 2) + jj
                qp = q_ref[:, j * LANES:(j + 1) * LANES]
                zero = jnp.zeros_like(qp)
                out = None
                for par in range(2):
                    hq = 2 * j + par
                    qm = jnp.where(lo, qp, zero) if par == 0 else jnp.where(lo, zero, qp)
                    pp, pc, _ = _att_probs(qm, k2p, k2c, masks, ALIBI_SLOPES[hq], sink_ref[hq])
                    vp, vc = (vlo_p, vlo_c) if par == 0 else (vhi_p, vhi_c)
                    part = _dot(pp.astype(BF16), vp, NN) + _dot(pc.astype(BF16), vc, NN)
                    out = part if out is None else out + part
                o_ref[:, j * LANES:(j + 1) * LANES] = out.astype(BF16)

    return pl.pallas_call(
        body,
        name="attn_fwd",
        grid=(nb,),
        in_specs=[
            pl.BlockSpec(memory_space=pltpu.SMEM),
            pl.BlockSpec((WINDOW, D_MODEL), lambda n: (n, 0)),
            pl.BlockSpec((WINDOW, 2 * LANES), lambda n: (jnp.maximum(n - 1, 0), 0)),
            pl.BlockSpec((WINDOW, 2 * LANES), lambda n: (n, 0)),
        ],
        out_specs=pl.BlockSpec((WINDOW, D_MODEL), lambda n: (n, 0)),
        out_shape=jax.ShapeDtypeStruct((t, D_MODEL), BF16),
        compiler_params=_cparams(("parallel",)),
    )(sinks, q, kv, kv)


def _attn_bwd(q, kv, sinks, out, dout):
    t = q.shape[0]
    nb = t // WINDOW

    def body(sink_ref, q_ref, kvp_ref, kvc_ref, o_ref, do_ref, dq_ref, dkv_ref, dsink_ref, carry_ref):
        n = pl.program_id(0)

        @pl.when(n == 0)
        def _():
            carry_ref[...] = jnp.zeros_like(carry_ref)
            dsink_ref[...] = jnp.zeros_like(dsink_ref)

        @pl.when(n == nb)
        def _():
            dkv_ref[...] = carry_ref[...].astype(BF16)

        @pl.when(n < nb)
        def _():
            masks = _att_masks(n)
            lo = lax.broadcasted_iota(jnp.int32, (WINDOW, LANES), 1) < ATT_HD
            lane1 = lax.broadcasted_iota(jnp.int32, (1, LANES), 1)
            dsink = jnp.zeros((1, LANES), F32)
            halves = []
            for kh in range(ATT_KVH):
                k2p, klo_p, khi_p = _att_halves(kvp_ref[:, 0:LANES], lo, kh)
                k2c, klo_c, khi_c = _att_halves(kvc_ref[:, 0:LANES], lo, kh)
                v2p, _, _ = _att_halves(kvp_ref[:, LANES:2 * LANES], lo, kh)
                v2c, _, _ = _att_halves(kvc_ref[:, LANES:2 * LANES], lo, kh)
                acc = [jnp.zeros((WINDOW, LANES), F32) for _ in range(4)]
                for jj in range(ATT_GROUP // 2):
                    j = kh * (ATT_GROUP // 2) + jj
                    cols = slice(j * LANES, (j + 1) * LANES)
                    qp = q_ref[:, cols]
                    dop = do_ref[:, cols]
                    prod = dop.astype(F32) * o_ref[:, cols].astype(F32)
                    zero = jnp.zeros_like(qp)
                    dq_pair = None
                    for par in range(2):
                        hq = 2 * j + par
                        sel = lo if par == 0 else jnp.logical_not(lo)
                        qm = jnp.where(sel, qp, zero)
                        dom = jnp.where(sel, dop, zero)
                        pp, pc, ps = _att_probs(qm, k2p, k2c, masks, ALIBI_SLOPES[hq], sink_ref[hq])
                        delta = jnp.sum(jnp.where(sel, prod, 0.0), axis=-1, keepdims=True)
                        dsp = (pp * (_dot(dom, v2p, NT) - delta)).astype(BF16)
                        dsc = (pc * (_dot(dom, v2c, NT) - delta)).astype(BF16)
                        dsink = dsink + jnp.where(lane1 == hq, -jnp.sum(ps * delta, axis=0, keepdims=True), 0.0)
                        kp_, kc_ = (klo_p, klo_c) if par == 0 else (khi_p, khi_c)
                        part = _dot(dsp, kp_, NN) + _dot(dsc, kc_, NN)
                        dq_pair = part if dq_pair is None else dq_pair + part
                        acc[0] = acc[0] + _dot(dsp, qm, TN)
                        acc[1] = acc[1] + _dot(dsc, qm, TN)
                        acc[2] = acc[2] + _dot(pp.astype(BF16), dom, TN)
                        acc[3] = acc[3] + _dot(pc.astype(BF16), dom, TN)
                    dq_ref[:, cols] = (dq_pair * (ATT_HD ** -0.5)).astype(BF16)
                halves.append([a + pltpu.roll(a, ATT_HD, 1) for a in acc])
            scale = ATT_HD ** -0.5
            prev = jnp.concatenate(
                [jnp.where(lo, halves[0][0], halves[1][0]) * scale, jnp.where(lo, halves[0][2], halves[1][2])], axis=1)
            cur = jnp.concatenate(
                [jnp.where(lo, halves[0][1], halves[1][1]) * scale, jnp.where(lo, halves[0][3], halves[1][3])], axis=1)
            dkv_ref[...] = (carry_ref[...] + prev).astype(BF16)
            carry_ref[...] = cur
            dsink_ref[...] += dsink

    blk = lambda n: jnp.minimum(n, nb - 1)
    return pl.pallas_call(
        body,
        name="attn_bwd",
        grid=(nb + 1,),
        in_specs=[
            pl.BlockSpec(memory_space=pltpu.SMEM),
            pl.BlockSpec((WINDOW, D_MODEL), lambda n: (blk(n), 0)),
            pl.BlockSpec((WINDOW, 2 * LANES), lambda n: (jnp.maximum(blk(n) - 1, 0), 0)),
            pl.BlockSpec((WINDOW, 2 * LANES), lambda n: (blk(n), 0)),
            pl.BlockSpec((WINDOW, D_MODEL), lambda n: (blk(n), 0)),
            pl.BlockSpec((WINDOW, D_MODEL), lambda n: (blk(n), 0)),
        ],
        out_specs=[
            pl.BlockSpec((WINDOW, D_MODEL), lambda n: (blk(n), 0)),
            pl.BlockSpec((WINDOW, 2 * LANES), lambda n: (jnp.maximum(n - 1, 0), 0)),
            pl.BlockSpec((1, LANES), lambda n: (0, 0)),
        ],
        out_shape=[
            jax.ShapeDtypeStruct((t, D_MODEL), BF16),
            jax.ShapeDtypeStruct((t, 2 * LANES), BF16),
            jax.ShapeDtypeStruct((1, LANES), F32),
        ],
        scratch_shapes=[pltpu.VMEM((WINDOW, 2 * LANES), F32)],
        compiler_params=_cparams(("arbitrary",)),
    )(sinks, q, kv, kv, out, dout)


def _ffn_fwd(h, norm_g, w_up, conv_w, conv_b, w_down, tag):
    xn = _rms_fwd(h, norm_g, name=f"ffn{tag}_norm")
    up = _mm_nn(xn, w_up, name=f"ffn{tag}_up")
    act = _conv_fwd(up, conv_w, conv_b, name=f"ffn{tag}_conv")
    h_out = _mm_nn(act, w_down, res=h, name=f"ffn{tag}_down")
    return h_out, (xn, up, act)


def _ffn_bwd(dh, h, norm_g, w_up, conv_w, conv_b, w_down, saved, tag):
    xn, up, act = saved
    dw_down = _mm_tn(act, dh, 1, D_MODEL, name=f"ffn{tag}_dwdown")
    dact = _mm_nt(dh, w_down, name=f"ffn{tag}_dact")
    dup, dconv_w, dconv_b = _conv_bwd(up, conv_w, conv_b, dact, name=f"ffn{tag}_dconv")
    dw_up = _mm_tn(xn, dup, N_CHIPS, CONV_COLS, stacked=True, name=f"ffn{tag}_dwup")
    dxn = _mm_nt(dup, w_up, stacked=True, name=f"ffn{tag}_dxn")
    dh_in, dnorm = _rms_bwd(h, norm_g, dxn, dh, name=f"ffn{tag}_dnorm")
    return dh_in, dict(ffn_w_down=dw_down, ffn_w_up=dw_up, ffn_conv_w=dconv_w, ffn_conv_b=dconv_b, ffn_norm=dnorm)


def _local_step(x, target, w):
    xn0 = _rms_fwd(x, w["hg_norm"], name="hg_norm")
    proj = _mm_nn(xn0, w["hg_w_in"], name="hg_in")
    o, y, states = _hgrn_fwd(proj, w["hg_lb"], w["hg_out_norm"])
    h_a = _mm_nn(y, w["hg_w_out"], res=x, name="hg_out")
    h1, ffn0 = _ffn_fwd(h_a, w["ffn_norm"][0], w["ffn_w_up"][0], w["ffn_conv_w"][0], w["ffn_conv_b"][0], w["ffn_w_down"][0], 0)
    kvn = _rms_fwd(h1, w["kv_norm"], name="kv_norm")
    kv = _mm_nn(kvn, w["w_kv"], out_dtype=BF16, name="kv_proj")
    xa = _rms_fwd(h1, w["attn_norm"], name="attn_norm")
    qa = _mm_nn(xa, w["attn_w_q"], out_dtype=BF16, name="attn_q")
    ao = _attn_fwd(qa, kv, w["attn_sinks"])
    h_b = _mm_nn(ao, w["attn_w_o"], res=h1, name="attn_o")
    h2, ffn1 = _ffn_fwd(h_b, w["ffn_norm"][1], w["ffn_w_up"][1], w["ffn_conv_w"][1], w["ffn_conv_b"][1], w["ffn_w_down"][1], 1)
    dh2, d_final, loss = _loss_head(h2, w["final_norm"], target)

    dh_b, g1 = _ffn_bwd(dh2, h_b, w["ffn_norm"][1], w["ffn_w_up"][1], w["ffn_conv_w"][1], w["ffn_conv_b"][1], w["ffn_w_down"][1], ffn1, 1)
    dw_o = _mm_tn(ao, dh_b, 1, D_MODEL, name="attn_dwo")
    dao = _mm_nt(dh_b, w["attn_w_o"], out_dtype=BF16, name="attn_dao")
    dqa, dkv, dsinks = _attn_bwd(qa, kv, w["attn_sinks"], ao, dao)
    dw_q = _mm_tn(xa, dqa, 1, D_MODEL, name="attn_dwq")
    dxa = _mm_nt(dqa, w["attn_w_q"], name="attn_dxa")
    dh1, d_attn_norm = _rms_bwd(h1, w["attn_norm"], dxa, dh_b, name="attn_dnorm")
    dw_kv = _mm_tn(kvn, dkv, 1, 2 * LANES, name="kv_dw")
    dkvn = _mm_nt(dkv, w["w_kv"], name="kv_dx")
    dh1, d_kv_norm = _rms_bwd(h1, w["kv_norm"], dkvn, dh1, name="kv_dnorm")
    dh_a, g0 = _ffn_bwd(dh1, h_a, w["ffn_norm"][0], w["ffn_w_up"][0], w["ffn_conv_w"][0], w["ffn_conv_b"][0], w["ffn_w_down"][0], ffn0, 0)
    dw_out = _mm_tn(y, dh_a, 1, D_MODEL, name="hg_dwout")
    dy = _mm_nt(dh_a, w["hg_w_out"], out_dtype=BF16, name="hg_dy")
    dproj, dlb, d_out_norm = _hgrn_bwd(proj, w["hg_lb"], w["hg_out_norm"], o, states, dy)
    dw_in = _mm_tn(xn0, dproj, N_CHIPS, D_MODEL, stacked=True, name="hg_dwin")
    dxn0 = _mm_nt(dproj, w["hg_w_in"], stacked=True, name="hg_dxn")
    dx, d_hg_norm = _rms_bwd(x, w["hg_norm"], dxn0, dh_a, name="hg_dnorm")

    grads = dict(
        hg_norm=d_hg_norm, hg_w_in=dw_in, hg_lb=dlb, hg_out_norm=d_out_norm, hg_w_out=dw_out,
        kv_norm=d_kv_norm, w_kv=dw_kv, attn_norm=d_attn_norm, attn_w_q=dw_q, attn_sinks=dsinks, attn_w_o=dw_o,
        final_norm=d_final,
    )
    for name in g0:
        grads[name] = [g0[name], g1[name]]
    return loss, dx, grads


ANY = pl.BlockSpec(memory_space=pl.ANY)


def _place():
    x, y, c = lax.axis_index("x"), lax.axis_index("y"), lax.axis_index("c")
    chips = [(1 - x, y), (x, 1 - y), (1 - x, 1 - y)]
    return x, y, c, chips


def _rcopy(src, dst, send_sem, recv_sem, to):
    return pltpu.make_async_remote_copy(src_ref=src, dst_ref=dst, send_sem=send_sem, recv_sem=recv_sem, device_id=to, device_id_type=MESH)


def _gather_shards(shards):
    n = len(shards)

    def body(*refs):
        ins, outs = refs[:n], refs[n:2 * n]
        send_sems, recv_sems, local_sems = refs[2 * n:]
        x, y, c, chips = _place()
        me = 2 * x + y
        sibling = (x, y, 1 - c)
        sends = []
        for i in range(n):
            cp = pltpu.make_async_copy(ins[i], outs[i].at[me], local_sems.at[i])
            cp.start()
            sends.append(cp)
        for i in range(n):
            h = ins[i].shape[0] // 2
            mine = pl.ds(c * h, h)
            for j, (px, py) in enumerate(chips):
                cp = _rcopy(ins[i].at[mine], outs[i].at[me, mine], send_sems.at[i, j], recv_sems.at[i, j], (px, py, c))
                cp.start()
                sends.append(cp)
        for j, (px, py) in enumerate(chips):
            src = 2 * px + py
            for i in range(n):
                h = ins[i].shape[0] // 2
                mine = pl.ds(c * h, h)
                landed = outs[i].at[src, mine]
                _rcopy(landed, landed, send_sems.at[i, j], recv_sems.at[i, j], sibling).wait_recv()
                cp = _rcopy(landed, landed, send_sems.at[i, 3 + j], recv_sems.at[i, 3 + j], sibling)
                cp.start()
                sends.append(cp)
        for j, (px, py) in enumerate(chips):
            src = 2 * px + py
            for i in range(n):
                h = ins[i].shape[0] // 2
                other = outs[i].at[src, pl.ds((1 - c) * h, h)]
                _rcopy(other, other, send_sems.at[i, 3 + j], recv_sems.at[i, 3 + j], sibling).wait_recv()
        for cp in sends[:n]:
            cp.wait()
        for cp in sends[n:]:
            cp.wait_send()

    return pl.pallas_call(
        body,
        name="gather_weights",
        in_specs=[ANY] * n,
        out_specs=[ANY] * n,
        out_shape=[jax.ShapeDtypeStruct((N_CHIPS,) + s.shape, s.dtype) for s in shards],
        scratch_shapes=[pltpu.SemaphoreType.DMA((n, 6)), pltpu.SemaphoreType.DMA((n, 6)), pltpu.SemaphoreType.DMA((n,))],
    )(*shards)


def _allreduce_small(vec):
    rows = vec.shape[0]

    def body(v_ref, o_ref, buf, send_sems, recv_sems):
        x, y, c, _ = _place()
        me = 4 * x + 2 * y + c
        buf[me] = v_ref[...]
        copies = []
        for k in range(1, N_DEV):
            peer = (x ^ (k >> 2), y ^ ((k >> 1) & 1), c ^ (k & 1))
            cp = _rcopy(v_ref, buf.at[me], send_sems.at[k - 1], recv_sems.at[k - 1], peer)
            cp.start()
            copies.append(cp)
        for cp in copies:
            cp.wait()
        acc = buf[0]
        for d in range(1, N_DEV):
            acc = acc + buf[d]
        o_ref[...] = acc

    return pl.pallas_call(
        body,
        name="allreduce_small",
        in_specs=[pl.BlockSpec(memory_space=pltpu.VMEM)],
        out_specs=pl.BlockSpec(memory_space=pltpu.VMEM),
        out_shape=jax.ShapeDtypeStruct(vec.shape, F32),
        scratch_shapes=[pltpu.VMEM((N_DEV, rows, LANES), F32), pltpu.SemaphoreType.DMA((N_DEV - 1,)), pltpu.SemaphoreType.DMA((N_DEV - 1,))],
        compiler_params=pltpu.CompilerParams(vmem_limit_bytes=VMEM_LIMIT_BYTES),
    )(vec)


def _swap_halves(grads):
    n = len(grads)

    def body(*refs):
        ins, outs = refs[:n], refs[n:2 * n]
        send_sems, recv_sems = refs[2 * n:]
        x, y, c, _ = _place()
        copies = []
        for i in range(n):
            h = ins[i].shape[1] // 2
            cp = _rcopy(ins[i].at[:, pl.ds((1 - c) * h, h)], outs[i], send_sems.at[i], recv_sems.at[i], (x, y, 1 - c))
            cp.start()
            copies.append(cp)
        for cp in copies:
            cp.wait()

    return pl.pallas_call(
        body,
        name="rs_swap_halves",
        in_specs=[ANY] * n,
        out_specs=[ANY] * n,
        out_shape=[jax.ShapeDtypeStruct((N_CHIPS, g.shape[1] // 2, g.shape[2]), F32) for g in grads],
        scratch_shapes=[pltpu.SemaphoreType.DMA((n,)), pltpu.SemaphoreType.DMA((n,))],
    )(*grads)


def _send_partials(parts):
    n = len(parts)

    def body(*refs):
        ins, outs = refs[:n], refs[n:2 * n]
        send_sems, recv_sems = refs[2 * n:]
        x, y, c, chips = _place()
        copies = []
        for i in range(n):
            for j, (px, py) in enumerate(chips):
                cp = _rcopy(ins[i].at[2 * px + py], outs[i].at[j], send_sems.at[i, j], recv_sems.at[i, j], (px, py, c))
                cp.start()
                copies.append(cp)
        for cp in copies:
            cp.wait()

    return pl.pallas_call(
        body,
        name="rs_send_partials",
        in_specs=[ANY] * n,
        out_specs=[ANY] * n,
        out_shape=[jax.ShapeDtypeStruct((3,) + p.shape[1:], p.dtype) for p in parts],
        scratch_shapes=[pltpu.SemaphoreType.DMA((n, 3)), pltpu.SemaphoreType.DMA((n, 3))],
    )(*parts)


def _share_halves(halves):
    n = len(halves)

    def body(*refs):
        ins, outs = refs[:n], refs[n:2 * n]
        send_sems, recv_sems, local_sems = refs[2 * n:]
        x, y, c, _ = _place()
        local, remote = [], []
        for i in range(n):
            cp = pltpu.make_async_copy(ins[i], outs[i].at[c], local_sems.at[i])
            cp.start()
            local.append(cp)
            cp = _rcopy(ins[i], outs[i].at[c], send_sems.at[i], recv_sems.at[i], (x, y, 1 - c))
            cp.start()
            remote.append(cp)
        for cp in local + remote:
            cp.wait()

    return pl.pallas_call(
        body,
        name="rs_share_halves",
        in_specs=[ANY] * n,
        out_specs=[ANY] * n,
        out_shape=[jax.ShapeDtypeStruct((2,) + h.shape, F32) for h in halves],
        scratch_shapes=[pltpu.SemaphoreType.DMA((n,)), pltpu.SemaphoreType.DMA((n,)), pltpu.SemaphoreType.DMA((n,))],
    )(*halves)


ELEM_ROWS = (256, 176, 128, 64, 32, 16, 8)


def _add_core_halves(grad, got, c, name):
    s, r, cols = grad.shape
    h = r // 2
    tr = _pick(h, ELEM_ROWS)

    def body(c_ref, g_ref, o_ref, f_ref, b_ref):
        acc = g_ref[...] + o_ref[...]
        f_ref[...] = acc
        b_ref[...] = acc.astype(BF16)

    blk = pl.BlockSpec((None, tr, cols), lambda k, i, c_ref: (k, i, 0))
    return pl.pallas_call(
        body,
        name=name,
        grid_spec=pltpu.PrefetchScalarGridSpec(
            num_scalar_prefetch=1,
            grid=(s, h // tr),
            in_specs=[pl.BlockSpec((None, None, tr, cols), lambda k, i, c_ref: (k, c_ref[0], i, 0)), blk],
            out_specs=[blk, blk],
        ),
        out_shape=[jax.ShapeDtypeStruct((s, h, cols), F32), jax.ShapeDtypeStruct((s, h, cols), BF16)],
        compiler_params=_cparams(("parallel", "parallel")),
    )(c, grad.reshape(s, 2, h, cols), got)


def _add_chip_partials(mine, got, chip, name):
    _, h, cols = mine.shape
    tr = _pick(h, ELEM_ROWS)

    def body(chip_ref, m_ref, g_ref, o_ref):
        acc = m_ref[...]
        for j in range(3):
            acc = acc + g_ref[j].astype(F32)
        o_ref[...] = acc

    return pl.pallas_call(
        body,
        name=name,
        grid_spec=pltpu.PrefetchScalarGridSpec(
            num_scalar_prefetch=1,
            grid=(h // tr,),
            in_specs=[
                pl.BlockSpec((None, tr, cols), lambda i, chip_ref: (chip_ref[0], i, 0)),
                pl.BlockSpec((3, tr, cols), lambda i, chip_ref: (0, i, 0)),
            ],
            out_specs=pl.BlockSpec((tr, cols), lambda i, chip_ref: (i, 0)),
        ),
        out_shape=jax.ShapeDtypeStruct((h, cols), F32),
        compiler_params=_cparams(("parallel",)),
    )(chip, mine, got)


def _adamw(w, m, v, g, name):
    r, cols = w.shape
    tr = _pick(r, ELEM_ROWS)
    c1 = 1.0 / (1.0 - ADAM_B1 ** ADAM_STEP)
    c2 = 1.0 / (1.0 - ADAM_B2 ** ADAM_STEP)

    def body(w_ref, m_ref, v_ref, g_ref, d_ref, nm_ref, nv_ref):
        gv = g_ref[...]
        nm = ADAM_B1 * m_ref[...] + (1.0 - ADAM_B1) * gv
        nv = ADAM_B2 * v_ref[...] + (1.0 - ADAM_B2) * (gv * gv)
        d_ref[...] = -ADAM_LR * ((nm * c1) / (jnp.sqrt(nv * c2) + ADAM_EPS) + ADAM_WD * w_ref[...])
        nm_ref[...] = nm
        nv_ref[...] = nv

    blk = pl.BlockSpec((tr, cols), lambda i: (i, 0))
    return pl.pallas_call(
        body,
        name=name,
        grid=(r // tr,),
        in_specs=[blk] * 4,
        out_specs=[blk] * 3,
        out_shape=[jax.ShapeDtypeStruct((r, cols), F32)] * 3,
        compiler_params=_cparams(("parallel",)),
    )(w, m, v, g)


SMALL_COLS = 384
SMALL_ROWS = 16


def _pad_rows(flat, rows, cols):
    return jnp.pad(flat, (0, rows * cols - flat.shape[0])).reshape(rows, cols)


def kernel(x, hg_norm, hg_w_in, hg_lb_logits, hg_out_norm, hg_w_out, kv_norm, w_kv, attn_norm, attn_w_q, attn_sinks, attn_w_o, ffn_norm, ffn_w_up, ffn_conv_w, ffn_conv_b, ffn_w_down, final_norm, loss_target, m_hg_norm, m_hg_w_in, m_hg_lb_logits, m_hg_out_norm, m_hg_w_out, m_kv_norm, m_w_kv, m_attn_norm, m_attn_w_q, m_attn_sinks, m_attn_w_o, m_ffn_norm, m_ffn_w_up, m_ffn_conv_w, m_ffn_conv_b, m_ffn_w_down, m_final_norm, v_hg_norm, v_hg_w_in, v_hg_lb_logits, v_hg_out_norm, v_hg_w_out, v_kv_norm, v_w_kv, v_attn_norm, v_attn_w_q, v_attn_sinks, v_attn_w_o, v_ffn_norm, v_ffn_w_up, v_ffn_conv_w, v_ffn_conv_b, v_ffn_w_down, v_final_norm):
    wts = dict(hg_norm=hg_norm, hg_w_in=hg_w_in, hg_lb_logits=hg_lb_logits, hg_out_norm=hg_out_norm, hg_w_out=hg_w_out, kv_norm=kv_norm, w_kv=w_kv, attn_norm=attn_norm, attn_w_q=attn_w_q, attn_sinks=attn_sinks, attn_w_o=attn_w_o, ffn_norm=ffn_norm, ffn_w_up=ffn_w_up, ffn_conv_w=ffn_conv_w, ffn_conv_b=ffn_conv_b, ffn_w_down=ffn_w_down, final_norm=final_norm)
    mom1 = dict(hg_norm=m_hg_norm, hg_w_in=m_hg_w_in, hg_lb_logits=m_hg_lb_logits, hg_out_norm=m_hg_out_norm, hg_w_out=m_hg_w_out, kv_norm=m_kv_norm, w_kv=m_w_kv, attn_norm=m_attn_norm, attn_w_q=m_attn_w_q, attn_sinks=m_attn_sinks, attn_w_o=m_attn_w_o, ffn_norm=m_ffn_norm, ffn_w_up=m_ffn_w_up, ffn_conv_w=m_ffn_conv_w, ffn_conv_b=m_ffn_conv_b, ffn_w_down=m_ffn_w_down, final_norm=m_final_norm)
    mom2 = dict(hg_norm=v_hg_norm, hg_w_in=v_hg_w_in, hg_lb_logits=v_hg_lb_logits, hg_out_norm=v_hg_out_norm, hg_w_out=v_hg_w_out, kv_norm=v_kv_norm, w_kv=v_w_kv, attn_norm=v_attn_norm, attn_w_q=v_attn_w_q, attn_sinks=v_attn_sinks, attn_w_o=v_attn_w_o, ffn_norm=v_ffn_norm, ffn_w_up=v_ffn_w_up, ffn_conv_w=v_ffn_conv_w, ffn_conv_b=v_ffn_conv_b, ffn_w_down=v_ffn_w_down, final_norm=v_final_norm)
    names = list(wts)
    chip = 2 * lax.axis_index("x") + lax.axis_index("y")
    core = lax.axis_index("c")
    chip_arr = jnp.reshape(chip, (1,)).astype(jnp.int32)
    core_arr = jnp.reshape(core, (1,)).astype(jnp.int32)
    fs = D_FF // N_CHIPS
    ds = D_MODEL // N_CHIPS

    small = jnp.concatenate([hg_norm.reshape(-1), hg_lb_logits.reshape(-1), ffn_conv_w.reshape(-1)])
    n_small = small.shape[0]
    big_shards = [
        hg_w_in[0].astype(BF16), hg_w_out[0].astype(BF16), w_kv.astype(BF16), attn_w_q[0].astype(BF16), attn_w_o[0].astype(BF16),
        ffn_w_up[0].astype(BF16), ffn_w_up[1].astype(BF16), ffn_w_down[0].astype(BF16), ffn_w_down[1].astype(BF16),
    ]
    gathered = _gather_shards(big_shards + [_pad_rows(small, SMALL_ROWS, SMALL_COLS)])
    g_in, g_out, g_kv, g_q, g_o, g_up0, g_up1, g_dn0, g_dn1, g_small = gathered
    g_small = g_small.reshape(N_CHIPS, -1)[:, :n_small]
    full_hg_norm = g_small[:, :ds].reshape(1, D_MODEL)
    full_lb = g_small[:, ds:3 * ds].reshape(N_CHIPS, 2, ds).transpose(1, 0, 2).reshape(2, D_MODEL)
    full_conv_w = g_small[:, 3 * ds:].reshape(N_CHIPS, 2, 3, fs).transpose(1, 2, 0, 3).reshape(2, 3, D_FF)
    whole = dict(
        hg_norm=full_hg_norm, hg_w_in=g_in, hg_lb=full_lb, hg_out_norm=hg_out_norm,
        hg_w_out=g_out.reshape(1, D_MODEL, D_MODEL), kv_norm=kv_norm.reshape(1, D_MODEL), w_kv=g_kv.reshape(1, D_MODEL, 2 * LANES),
        attn_norm=attn_norm, attn_w_q=g_q.reshape(1, D_MODEL, D_MODEL), attn_sinks=attn_sinks.reshape(ATT_QH),
        attn_w_o=g_o.reshape(1, D_MODEL, D_MODEL),
        ffn_norm=[ffn_norm[0:1], ffn_norm[1:2]], ffn_w_up=[g_up0, g_up1],
        ffn_conv_w=[full_conv_w[0], full_conv_w[1]], ffn_conv_b=[ffn_conv_b[0:1], ffn_conv_b[1:2]],
        ffn_w_down=[g_dn0.reshape(1, D_FF, D_MODEL), g_dn1.reshape(1, D_FF, D_MODEL)],
        final_norm=final_norm.reshape(1, D_MODEL),
    )

    loss, dx, grads = _local_step(x[0], loss_target[0], whole)

    small_parts = [
        loss.reshape(-1), grads["hg_out_norm"].reshape(-1), grads["attn_sinks"].reshape(-1), grads["kv_norm"].reshape(-1),
        grads["attn_norm"].reshape(-1), grads["ffn_norm"][0].reshape(-1), grads["ffn_norm"][1].reshape(-1),
        grads["ffn_conv_b"][0].reshape(-1), grads["ffn_conv_b"][1].reshape(-1), grads["final_norm"].reshape(-1),
        grads["hg_norm"].reshape(-1), grads["hg_lb"].reshape(-1), grads["ffn_conv_w"][0].reshape(-1), grads["ffn_conv_w"][1].reshape(-1),
    ]
    sizes = [p.shape[0] for p in small_parts]
    flat = jnp.concatenate(small_parts)
    rows = -(-flat.shape[0] // (SUBLANES * LANES)) * SUBLANES
    summed = _allreduce_small(_pad_rows(flat, rows, LANES)).reshape(-1)
    offs = [0]
    for sz in sizes:
        offs.append(offs[-1] + sz)
    sm = [summed[offs[i]:offs[i + 1]] for i in range(len(sizes))]
    loss_out = sm[0][0]
    conv_w_full = jnp.stack([sm[12].reshape(3, D_FF), sm[13].reshape(3, D_FF)])
    small_grads = dict(
        hg_out_norm=sm[1].reshape(1, HG_DK), attn_sinks=sm[2][:ATT_QH].reshape(1, ATT_QH), kv_norm=sm[3], attn_norm=sm[4].reshape(1, D_MODEL),
        ffn_norm=jnp.stack([sm[5], sm[6]]), ffn_conv_b=jnp.stack([sm[7], sm[8]]), final_norm=sm[9],
        hg_norm=lax.dynamic_slice(sm[10].reshape(1, D_MODEL), (0, chip * ds), (1, ds)),
        hg_lb_logits=lax.dynamic_slice(sm[11].reshape(2, D_MODEL), (0, chip * ds), (2, ds)),
        ffn_conv_w=lax.dynamic_slice(conv_w_full, (0, 0, chip * fs), (2, 3, fs)),
    )

    big_names = ["hg_w_in", "hg_w_out", "w_kv", "attn_w_q", "attn_w_o", "ffn_w_up", "ffn_w_up", "ffn_w_down", "ffn_w_down"]
    stacked = [
        grads["hg_w_in"], grads["hg_w_out"].reshape(N_CHIPS, ds, D_MODEL), grads["w_kv"].reshape(N_CHIPS, ds, 2 * LANES),
        grads["attn_w_q"].reshape(N_CHIPS, ds, D_MODEL), grads["attn_w_o"].reshape(N_CHIPS, ds, D_MODEL),
        grads["ffn_w_up"][0], grads["ffn_w_up"][1],
        grads["ffn_w_down"][0].reshape(N_CHIPS, fs, D_MODEL), grads["ffn_w_down"][1].reshape(N_CHIPS, fs, D_MODEL),
    ]
    from_core = _swap_halves(stacked)
    sums = [_add_core_halves(g, o, core_arr, name=f"rs_add_core_{i}") for i, (g, o) in enumerate(zip(stacked, from_core))]
    from_chips = _send_partials([b for _, b in sums])
    halves = [_add_chip_partials(f, o, chip_arr, name=f"rs_add_chip_{i}") for i, ((f, _), o) in enumerate(zip(sums, from_chips))]
    reduced = [r.reshape((-1,) + r.shape[2:]) for r in _share_halves(halves)]
    big_grads = dict(
        hg_w_in=reduced[0], hg_w_out=reduced[1], w_kv=reduced[2], attn_w_q=reduced[3], attn_w_o=reduced[4],
        ffn_w_up=jnp.concatenate([reduced[5], reduced[6]]), ffn_w_down=jnp.concatenate([reduced[7], reduced[8]]),
    )

    out_g, out_d, out_m, out_v = {}, {}, {}, {}
    for name in big_grads:
        g2 = big_grads[name]
        shape = wts[name].shape
        d2, m2, v2 = _adamw(wts[name].reshape(g2.shape), mom1[name].reshape(g2.shape), mom2[name].reshape(g2.shape), g2, name=f"adamw_{name}")
        out_g[name], out_d[name], out_m[name], out_v[name] = g2.reshape(shape), d2.reshape(shape), m2.reshape(shape), v2.reshape(shape)
    small_names = [n for n in names if n not in big_grads]
    cat = lambda d: jnp.concatenate([d[n].reshape(-1) for n in small_names])
    n_flat = sum(wts[n].size for n in small_names)
    srows = -(-n_flat // (SUBLANES * LANES)) * SUBLANES
    packed = [_pad_rows(cat(d), srows, LANES) for d in (wts, mom1, mom2, small_grads)]
    d_s, m_s, v_s = _adamw(*packed, name="adamw_small")
    off = 0
    for n in small_names:
        sz, shape = wts[n].size, wts[n].shape
        out_g[n] = small_grads[n].reshape(shape)
        out_d[n] = d_s.reshape(-1)[off:off + sz].reshape(shape)
        out_m[n] = m_s.reshape(-1)[off:off + sz].reshape(shape)
        out_v[n] = v_s.reshape(-1)[off:off + sz].reshape(shape)
        off += sz

    grad_x = dx.reshape(x.shape)
    return (loss_out, grad_x, *[out_g[n] for n in names], *[out_d[n] for n in names], *[out_m[n] for n in names], *[out_v[n] for n in names])
```

```python
import functools

import jax
import jax.numpy as jnp
from jax import lax
from jax.experimental import pallas as pl
from jax.experimental.pallas import tpu as pltpu

F32 = jnp.float32
BF16 = jnp.bfloat16
MESH = pl.DeviceIdType.MESH

EPS = 1e-6
D_MODEL = 1024
HG_HEADS = 8
HG_DK = 128
HG_CHUNK = 64
ATT_HD = 64
ATT_QH = 16
ATT_KVH = 2
ATT_GROUP = ATT_QH // ATT_KVH
WINDOW = 128
D_FF = 2816
N_CHIPS = 4
N_DEV = 8
LANES = 128
SUBLANES = 8
VMEM_LIMIT_BYTES = 56 * 1024 * 1024
NEG = -1e30
ALIBI_SLOPES = tuple(2.0 ** (-8.0 * h / ATT_QH) for h in range(1, ATT_QH + 1))

ADAM_LR = 0.001
ADAM_B1 = 0.9
ADAM_B2 = 0.999
ADAM_EPS = 1e-08
ADAM_WD = 0.01
ADAM_STEP = 10


def _cparams(sem=None):
    return pltpu.CompilerParams(dimension_semantics=sem, vmem_limit_bytes=VMEM_LIMIT_BYTES)


def _pick(n, cands):
    for c in cands:
        if n % c == 0:
            return c
    return n


def _sigmoid(x):
    return 1.0 / (1.0 + jnp.exp(-x))


def _dot(a, b, dims):
    return lax.dot_general(a, b, (dims, ((), ())), preferred_element_type=F32)


NN = ((1,), (0,))
NT = ((1,), (1,))
TN = ((0,), (0,))


def _mm_nn(a, w, res=None, out_dtype=F32, name="mm_nn"):
    m, k = a.shape
    s, _, ns = w.shape
    tm = min(m, 512)
    tn = _pick(ns, (512, 1408, 256, 128))
    npb = ns // tn

    def body(a_ref, w_ref, *rest):
        o_ref = rest[-1]
        acc = _dot(a_ref[...].astype(BF16), w_ref[...], NN)
        if res is not None:
            acc = acc + rest[0][...]
        o_ref[...] = acc.astype(o_ref.dtype)

    in_specs = [
        pl.BlockSpec((tm, k), lambda i, j: (i, 0)),
        pl.BlockSpec((None, k, tn), lambda i, j: (j // npb, 0, j % npb)),
    ]
    args = [a, w]
    if res is not None:
        in_specs.append(pl.BlockSpec((tm, tn), lambda i, j: (i, j)))
        args.append(res)
    return pl.pallas_call(
        body,
        name=name,
        grid=(m // tm, s * npb),
        in_specs=in_specs,
        out_specs=pl.BlockSpec((tm, tn), lambda i, j: (i, j)),
        out_shape=jax.ShapeDtypeStruct((m, s * ns), out_dtype),
        compiler_params=_cparams(("parallel", "parallel")),
    )(*args)


def _dy_spec(stacked, tm, tn, npb, row, kk):
    if stacked:
        return pl.BlockSpec((None, tm, tn), lambda *g: (kk(g) // npb, row(g), kk(g) % npb))
    return pl.BlockSpec((tm, tn), lambda *g: (row(g), kk(g)))


def _mm_nt(dy, w, stacked=False, out_dtype=F32, name="mm_nt"):
    s, k, ns = w.shape
    m = dy.shape[1] if stacked else dy.shape[0]
    tm = min(m, 512)
    tko = _pick(k, (1024, 1408, 512, 256))
    tn = _pick(ns, (1024, 1408, 512, 256))
    npb = ns // tn
    nk = s * npb

    def body(dy_ref, w_ref, o_ref, acc_ref):
        kk = pl.program_id(2)

        @pl.when(kk == 0)
        def _():
            acc_ref[...] = jnp.zeros_like(acc_ref)

        acc_ref[...] += _dot(dy_ref[...].astype(BF16), w_ref[...], NT)

        @pl.when(kk == nk - 1)
        def _():
            o_ref[...] = acc_ref[...].astype(o_ref.dtype)

    return pl.pallas_call(
        body,
        name=name,
        grid=(m // tm, k // tko, nk),
        in_specs=[
            _dy_spec(stacked, tm, tn, npb, lambda g: g[0], lambda g: g[2]),
            pl.BlockSpec((None, tko, tn), lambda i, j, kk: (kk // npb, j, kk % npb)),
        ],
        out_specs=pl.BlockSpec((tm, tko), lambda i, j, kk: (i, j)),
        out_shape=jax.ShapeDtypeStruct((m, k), out_dtype),
        scratch_shapes=[pltpu.VMEM((tm, tko), F32)],
        compiler_params=_cparams(("parallel", "parallel", "arbitrary")),
    )(dy, w)


def _mm_tn(a, dy, s, ns, stacked=False, name="mm_tn"):
    m, k = a.shape
    tm = min(m, 512)
    tk = _pick(k, (1024, 1408, 512, 256))
    tn = _pick(ns, (512, 1408, 256, 128))
    npb = ns // tn
    nm = m // tm

    def body(a_ref, dy_ref, o_ref, acc_ref):
        mm = pl.program_id(2)

        @pl.when(mm == 0)
        def _():
            acc_ref[...] = jnp.zeros_like(acc_ref)

        acc_ref[...] += _dot(a_ref[...].astype(BF16), dy_ref[...].astype(BF16), TN)

        @pl.when(mm == nm - 1)
        def _():
            o_ref[...] = acc_ref[...]

    return pl.pallas_call(
        body,
        name=name,
        grid=(k // tk, s * npb, nm),
        in_specs=[
            pl.BlockSpec((tm, tk), lambda i, j, mm: (mm, i)),
            _dy_spec(stacked, tm, tn, npb, lambda g: g[2], lambda g: g[1]),
        ],
        out_specs=pl.BlockSpec((None, tk, tn), lambda i, j, mm: (j // npb, i, j % npb)),
        out_shape=jax.ShapeDtypeStruct((s, k, ns), F32),
        scratch_shapes=[pltpu.VMEM((tk, tn), F32)],
        compiler_params=_cparams(("parallel", "parallel", "arbitrary")),
    )(a, dy)


ROW_TILE = 256


def _rms_fwd(x, g, name="rms_fwd"):
    t, d = x.shape
    r = min(t, ROW_TILE)

    def body(x_ref, g_ref, o_ref):
        xv = x_ref[...]
        rstd = lax.rsqrt(jnp.mean(xv * xv, axis=-1, keepdims=True) + EPS)
        o_ref[...] = (xv * rstd * g_ref[...]).astype(BF16)

    return pl.pallas_call(
        body,
        name=name,
        grid=(t // r,),
        in_specs=[pl.BlockSpec((r, d), lambda i: (i, 0)), pl.BlockSpec((1, d), lambda i: (0, 0))],
        out_specs=pl.BlockSpec((r, d), lambda i: (i, 0)),
        out_shape=jax.ShapeDtypeStruct((t, d), BF16),
        compiler_params=_cparams(("parallel",)),
    )(x, g)


def _rms_bwd(x, g, dxn, dres, name="rms_bwd"):
    t, d = x.shape
    r = min(t, ROW_TILE)

    def body(x_ref, g_ref, dxn_ref, dres_ref, dx_ref, dg_ref):
        @pl.when(pl.program_id(0) == 0)
        def _():
            dg_ref[...] = jnp.zeros_like(dg_ref)

        xv = x_ref[...]
        rstd = lax.rsqrt(jnp.mean(xv * xv, axis=-1, keepdims=True) + EPS)
        xhat = xv * rstd
        dxn_v = dxn_ref[...].astype(F32)
        gd = dxn_v * g_ref[...]
        dx_ref[...] = dres_ref[...] + rstd * (gd - xhat * jnp.mean(gd * xhat, axis=-1, keepdims=True))
        dg_ref[...] += jnp.sum(dxn_v * xhat, axis=0, keepdims=True)

    return pl.pallas_call(
        body,
        name=name,
        grid=(t // r,),
        in_specs=[
            pl.BlockSpec((r, d), lambda i: (i, 0)),
            pl.BlockSpec((1, d), lambda i: (0, 0)),
            pl.BlockSpec((r, d), lambda i: (i, 0)),
            pl.BlockSpec((r, d), lambda i: (i, 0)),
        ],
        out_specs=[pl.BlockSpec((r, d), lambda i: (i, 0)), pl.BlockSpec((1, d), lambda i: (0, 0))],
        out_shape=[jax.ShapeDtypeStruct((t, d), F32), jax.ShapeDtypeStruct((1, d), F32)],
        compiler_params=_cparams(("arbitrary",)),
    )(x, g, dxn, dres)


def _loss_head(h, g, target):
    t, d = h.shape
    r = min(t, ROW_TILE)

    def body(h_ref, g_ref, t_ref, dh_ref, dg_ref, loss_ref):
        @pl.when(pl.program_id(0) == 0)
        def _():
            dg_ref[...] = jnp.zeros_like(dg_ref)
            loss_ref[...] = jnp.zeros_like(loss_ref)

        xv = h_ref[...]
        rstd = lax.rsqrt(jnp.mean(xv * xv, axis=-1, keepdims=True) + EPS)
        xhat = xv * rstd
        gv = g_ref[...]
        err = xhat * gv - t_ref[...]
        loss_ref[...] += 0.5 * jnp.sum(jnp.mean(err * err, axis=-1, keepdims=True), axis=0, keepdims=True)
        dy = err * (1.0 / d)
        gd = dy * gv
        dh_ref[...] = rstd * (gd - xhat * jnp.mean(gd * xhat, axis=-1, keepdims=True))
        dg_ref[...] += jnp.sum(dy * xhat, axis=0, keepdims=True)

    return pl.pallas_call(
        body,
        name="loss_head",
        grid=(t // r,),
        in_specs=[
            pl.BlockSpec((r, d), lambda i: (i, 0)),
            pl.BlockSpec((1, d), lambda i: (0, 0)),
            pl.BlockSpec((r, d), lambda i: (i, 0)),
        ],
        out_specs=[
            pl.BlockSpec((r, d), lambda i: (i, 0)),
            pl.BlockSpec((1, d), lambda i: (0, 0)),
            pl.BlockSpec((1, LANES), lambda i: (0, 0)),
        ],
        out_shape=[
            jax.ShapeDtypeStruct((t, d), F32),
            jax.ShapeDtypeStruct((1, d), F32),
            jax.ShapeDtypeStruct((1, LANES), F32),
        ],
        compiler_params=_cparams(("arbitrary",)),
    )(h, g, target)


CONV_ROWS = 128
CONV_COLS = 1408


def _conv_taps(x_ext, n):
    tot = x_ext.shape[0]
    g1 = pltpu.roll(x_ext, 1, 0)[tot - n:]
    g2 = pltpu.roll(x_ext, 2, 0)[tot - n:]
    return g2, g1


def _conv_fwd(up, conv_w, conv_b, name="conv_fwd"):
    t = up.shape[0]
    r = min(t, CONV_ROWS)
    tc = CONV_COLS
    ncb = D_FF // tc
    hb = r // SUBLANES

    def body(g_ref, halo_ref, v_ref, w_ref, b_ref, o_ref):
        i = pl.program_id(1)
        g0 = g_ref[...]
        halo = halo_ref[...] * jnp.where(i > 0, 1.0, 0.0)
        g2, g1 = _conv_taps(jnp.concatenate([halo, g0], axis=0), r)
        c = b_ref[...] + w_ref[0:1, :] * g2 + w_ref[1:2, :] * g1 + w_ref[2:3, :] * g0
        o_ref[...] = (c * _sigmoid(c) * v_ref[...]).astype(BF16)

    return pl.pallas_call(
        body,
        name=name,
        grid=(ncb, t // r),
        in_specs=[
            pl.BlockSpec((r, tc), lambda j, i: (i, j)),
            pl.BlockSpec((SUBLANES, tc), lambda j, i: (jnp.maximum(i * hb - 1, 0), j)),
            pl.BlockSpec((r, tc), lambda j, i: (i, ncb + j)),
            pl.BlockSpec((3, tc), lambda j, i: (0, j)),
            pl.BlockSpec((1, tc), lambda j, i: (0, j)),
        ],
        out_specs=pl.BlockSpec((r, tc), lambda j, i: (i, j)),
        out_shape=jax.ShapeDtypeStruct((t, D_FF), BF16),
        compiler_params=_cparams(("parallel", "parallel")),
    )(up, up, up, conv_w, conv_b)


def _conv_bwd(up, conv_w, conv_b, dact, name="conv_bwd"):
    t = up.shape[0]
    r = min(t, CONV_ROWS)
    tc = CONV_COLS
    ncb = D_FF // tc
    hb = r // SUBLANES
    nrt = t // r

    def body(g_ref, halo_ref, v_ref, w_ref, b_ref, da_ref, dup_ref, dw_ref, db_ref, nxt_ref):
        ii = pl.program_id(1)
        i = nrt - 1 - ii

        @pl.when(ii == 0)
        def _():
            nxt_ref[...] = jnp.zeros_like(nxt_ref)
            dw_ref[...] = jnp.zeros_like(dw_ref)
            db_ref[...] = jnp.zeros_like(db_ref)

        g0 = g_ref[...]
        halo = halo_ref[...] * jnp.where(i > 0, 1.0, 0.0)
        g2, g1 = _conv_taps(jnp.concatenate([halo, g0], axis=0), r)
        w0, w1, w2 = w_ref[0:1, :], w_ref[1:2, :], w_ref[2:3, :]
        c = b_ref[...] + w0 * g2 + w1 * g1 + w2 * g0
        sg = _sigmoid(c)
        da = da_ref[...]
        dval = da * (c * sg)
        dc = da * v_ref[...] * (sg * (1.0 + c * (1.0 - sg)))
        db_ref[...] += jnp.sum(dc, axis=0, keepdims=True)
        dw_ref[0:1, :] += jnp.sum(dc * g2, axis=0, keepdims=True)
        dw_ref[1:2, :] += jnp.sum(dc * g1, axis=0, keepdims=True)
        dw_ref[2:3, :] += jnp.sum(dc * g0, axis=0, keepdims=True)
        ext = jnp.concatenate([dc, nxt_ref[...]], axis=0)
        tot = r + SUBLANES
        d1 = pltpu.roll(ext, tot - 1, 0)[:r]
        d2 = pltpu.roll(ext, tot - 2, 0)[:r]
        dgate = w2 * dc + w1 * d1 + w0 * d2
        nxt_ref[...] = dc[:SUBLANES]
        dup_ref[0] = dgate.astype(BF16)
        dup_ref[1] = dval.astype(BF16)

    rev = lambda ii: nrt - 1 - ii
    dup, dw, db = pl.pallas_call(
        body,
        name=name,
        grid=(ncb, nrt),
        in_specs=[
            pl.BlockSpec((r, tc), lambda j, ii: (rev(ii), j)),
            pl.BlockSpec((SUBLANES, tc), lambda j, ii: (jnp.maximum(rev(ii) * hb - 1, 0), j)),
            pl.BlockSpec((r, tc), lambda j, ii: (rev(ii), ncb + j)),
            pl.BlockSpec((3, tc), lambda j, ii: (0, j)),
            pl.BlockSpec((1, tc), lambda j, ii: (0, j)),
            pl.BlockSpec((r, tc), lambda j, ii: (rev(ii), j)),
        ],
        out_specs=[
            pl.BlockSpec((2, None, r, tc), lambda j, ii: (0, j, rev(ii), 0)),
            pl.BlockSpec((3, tc), lambda j, ii: (0, j)),
            pl.BlockSpec((1, tc), lambda j, ii: (0, j)),
        ],
        out_shape=[
            jax.ShapeDtypeStruct((2, ncb, t, tc), BF16),
            jax.ShapeDtypeStruct((3, D_FF), F32),
            jax.ShapeDtypeStruct((1, D_FF), F32),
        ],
        scratch_shapes=[pltpu.VMEM((SUBLANES, tc), F32)],
        compiler_params=_cparams(("parallel", "arbitrary")),
    )(up, up, up, conv_w, conv_b, dact)
    return dup.reshape(2 * ncb, t, tc), dw, db


def _split3(x):
    x1 = x.astype(BF16)
    r1 = x - x1.astype(F32)
    x2 = r1.astype(BF16)
    x3 = (r1 - x2.astype(F32)).astype(BF16)
    return x1, x2, x3


def _tri_dot(tri, x, dims):
    x1, x2, x3 = _split3(x)
    return _dot(tri, x1, dims) + _dot(tri, x2, dims) + _dot(tri, x3, dims)


def _lower_bound(logits_ref):
    return _sigmoid(logits_ref[0:1, :] - logits_ref[1:2, :])


def _hg_gates(qr, fr, lb):
    q = qr * _sigmoid(qr) * (HG_DK ** -0.5)
    sf = _sigmoid(fr)
    fg = lb + (1.0 - lb) * sf
    return q, sf, fg


def _hg_chunk_terms(q, fg, tril_b, low_half):
    g = jnp.log(fg)
    k = 1.0 - fg
    cum = _tri_dot(tril_b, g, NN)
    c_last = jnp.sum(g, axis=0, keepdims=True)
    c_mid = jnp.sum(jnp.where(low_half, g, 0.0), axis=0, keepdims=True)
    e_q = jnp.exp(cum - c_mid)
    e_k = jnp.exp(c_mid - cum)
    e_0 = jnp.exp(cum)
    e_l = jnp.exp(c_last - cum)
    return k, e_q, e_k, e_0, e_l, jnp.exp(c_last)


def _hg_specs(t, col0s):
    return [pl.BlockSpec((t, HG_DK), functools.partial(lambda h, c0: (0, c0 + h), c0=c0)) for c0 in col0s]


def _hgrn_fwd(proj, lb, wn):
    t = proj.shape[0]
    c = HG_CHUNK
    nc = t // c

    def body(q_ref, f_ref, i_ref, g_ref, lb_ref, wn_ref, o_ref, y_ref, st_ref, s_scr):
        s_scr[...] = jnp.zeros_like(s_scr)
        lbv = _lower_bound(lb_ref)
        wnv = wn_ref[...]
        ri = lax.broadcasted_iota(jnp.int32, (c, c), 0)
        ci = lax.broadcasted_iota(jnp.int32, (c, c), 1)
        tril = ri >= ci
        tril_b = tril.astype(BF16)
        low_half = lax.broadcasted_iota(jnp.int32, (c, HG_DK), 0) < c // 2

        def chunk(n, carry):
            rows = pl.ds(pl.multiple_of(n * c, c), c)
            q, _, fg = _hg_gates(q_ref[rows, :], f_ref[rows, :], lbv)
            v = i_ref[rows, :].astype(BF16)
            k, e_q, e_k, e_0, e_l, e_last = _hg_chunk_terms(q, fg, tril_b, low_half)
            st = s_scr[...]
            st_ref[n] = st
            a = jnp.where(tril, _dot((q * e_q).astype(BF16), (k * e_k).astype(BF16), NT), 0.0)
            o = _dot((q * e_0).astype(BF16), st.astype(BF16), NT) + _dot(a.astype(BF16), v, NN)
            s_scr[...] = st * e_last + _dot(v, (k * e_l).astype(BF16), TN)
            o_ref[rows, :] = o
            rstd = lax.rsqrt(jnp.mean(o * o, axis=-1, keepdims=True) + EPS)
            gr = g_ref[rows, :]
            y_ref[rows, :] = (o * rstd * wnv * (gr * _sigmoid(gr))).astype(BF16)
            return carry

        lax.fori_loop(0, nc, chunk, 0)

    vec = pl.BlockSpec((2, HG_DK), lambda h: (0, h))
    return pl.pallas_call(
        body,
        name="hgrn_fwd",
        grid=(HG_HEADS,),
        in_specs=_hg_specs(t, (0, HG_HEADS, 2 * HG_HEADS, 3 * HG_HEADS)) + [vec, pl.BlockSpec((1, HG_DK), lambda h: (0, 0))],
        out_specs=[
            pl.BlockSpec((t, HG_DK), lambda h: (0, h)),
            pl.BlockSpec((t, HG_DK), lambda h: (0, h)),
            pl.BlockSpec((None, nc, HG_DK, HG_DK), lambda h: (h, 0, 0, 0)),
        ],
        out_shape=[
            jax.ShapeDtypeStruct((t, D_MODEL), F32),
            jax.ShapeDtypeStruct((t, D_MODEL), BF16),
            jax.ShapeDtypeStruct((HG_HEADS, nc, HG_DK, HG_DK), F32),
        ],
        scratch_shapes=[pltpu.VMEM((HG_DK, HG_DK), F32)],
        compiler_params=_cparams(("parallel",)),
    )(proj, proj, proj, proj, lb, wn)


def _hgrn_bwd(proj, lb, wn, o, states, dy):
    t = proj.shape[0]
    c = HG_CHUNK
    nc = t // c

    def body(q_ref, f_ref, i_ref, g_ref, lb_ref, wn_ref, o_ref, st_ref, dy_ref, dp_ref, dl_ref, dwn_ref, ds_scr, dlb_scr):
        @pl.when(pl.program_id(0) == 0)
        def _():
            dwn_ref[...] = jnp.zeros_like(dwn_ref)

        ds_scr[...] = jnp.zeros_like(ds_scr)
        dlb_scr[...] = jnp.zeros_like(dlb_scr)
        lbv = _lower_bound(lb_ref)
        wnv = wn_ref[...]
        ri = lax.broadcasted_iota(jnp.int32, (c, c), 0)
        ci = lax.broadcasted_iota(jnp.int32, (c, c), 1)
        tril = ri >= ci
        tril_b = tril.astype(BF16)
        low_half = lax.broadcasted_iota(jnp.int32, (c, HG_DK), 0) < c // 2

        def chunk(nn, carry):
            n = nc - 1 - nn
            rows = pl.ds(pl.multiple_of(n * c, c), c)
            ov = o_ref[rows, :]
            gr = g_ref[rows, :]
            dyv = dy_ref[rows, :].astype(F32)
            rstd = lax.rsqrt(jnp.mean(ov * ov, axis=-1, keepdims=True) + EPS)
            ohat = ov * rstd
            sg = _sigmoid(gr)
            dg_raw = dyv * (ohat * wnv) * (sg * (1.0 + gr * (1.0 - sg)))
            don = dyv * (gr * sg)
            dwn_ref[...] += jnp.sum(don * ohat, axis=0, keepdims=True)
            gd = don * wnv
            do = rstd * (gd - ohat * jnp.mean(gd * ohat, axis=-1, keepdims=True))
            do_b = do.astype(BF16)
            qr = q_ref[rows, :]
            q, sf, fg = _hg_gates(qr, f_ref[rows, :], lbv)
            v = i_ref[rows, :].astype(BF16)
            k, e_q, e_k, e_0, e_l, e_last = _hg_chunk_terms(q, fg, tril_b, low_half)
            qi, qi_lo, _ = _split3(q * e_q)
            ki, ki_lo, _ = _split3(k * e_k)
            q0 = (q * e_0).astype(BF16)
            kl = (k * e_l).astype(BF16)
            st = st_ref[n]
            st_b = st.astype(BF16)
            ds = ds_scr[...]
            ds_b = ds.astype(BF16)
            a_b = jnp.where(tril, _dot(qi, ki, NT), 0.0).astype(BF16)
            da_b = jnp.where(tril, _dot(do_b, v, NT), 0.0).astype(BF16)
            dq = _dot(do_b, st_b, NN) * e_0 + (_dot(da_b, ki, NN) + _dot(da_b, ki_lo, NN)) * e_q
            dk_state = _dot(v, ds_b, NN) * e_l
            dk = (_dot(da_b, qi, TN) + _dot(da_b, qi_lo, TN)) * e_k + dk_state
            dv = _dot(a_b, do_b, TN) + _dot(kl, ds_b, NT)
            ds_scr[...] = ds * e_last + _dot(do_b, q0, TN)
            d_last = jnp.sum(dk_state * k, axis=0, keepdims=True) + jnp.sum(ds * st, axis=0, keepdims=True) * e_last
            dlogf = _tri_dot(tril_b, q * dq - k * dk, TN) + d_last
            dfg = dlogf / fg - dk
            dlb_scr[...] += jnp.sum(dfg * (1.0 - sf), axis=0, keepdims=True)
            sq = _sigmoid(qr)
            dp_ref[0, rows, :] = (dq * (HG_DK ** -0.5) * (sq * (1.0 + qr * (1.0 - sq)))).astype(BF16)
            dp_ref[1, rows, :] = (dfg * (1.0 - lbv) * sf * (1.0 - sf)).astype(BF16)
            dp_ref[2, rows, :] = dv.astype(BF16)
            dp_ref[3, rows, :] = dg_raw.astype(BF16)
            return carry

        lax.fori_loop(0, nc, chunk, 0)
        d0 = dlb_scr[...] * lbv * (1.0 - lbv)
        dl_ref[0:1, :] = d0
        dl_ref[1:2, :] = -d0

    vec = pl.BlockSpec((2, HG_DK), lambda h: (0, h))
    one = pl.BlockSpec((1, HG_DK), lambda h: (0, 0))
    col = pl.BlockSpec((t, HG_DK), lambda h: (0, h))
    return pl.pallas_call(
        body,
        name="hgrn_bwd",
        grid=(HG_HEADS,),
        in_specs=_hg_specs(t, (0, HG_HEADS, 2 * HG_HEADS, 3 * HG_HEADS))
        + [vec, one, col, pl.BlockSpec((None, nc, HG_DK, HG_DK), lambda h: (h, 0, 0, 0)), col],
        out_specs=[pl.BlockSpec((4, t, HG_DK), lambda h: (0, 0, h)), vec, one],
        out_shape=[
            jax.ShapeDtypeStruct((4, t, D_MODEL), BF16),
            jax.ShapeDtypeStruct((2, D_MODEL), F32),
            jax.ShapeDtypeStruct((1, HG_DK), F32),
        ],
        scratch_shapes=[pltpu.VMEM((HG_DK, HG_DK), F32), pltpu.VMEM((1, HG_DK), F32)],
        compiler_params=_cparams(("arbitrary",)),
    )(proj, proj, proj, proj, lb, wn, o, states, dy)


def _att_masks(n):
    tq = lax.broadcasted_iota(jnp.int32, (WINDOW, WINDOW), 0)
    sk = lax.broadcasted_iota(jnp.int32, (WINDOW, WINDOW), 1)
    valid_c = sk <= tq
    valid_p = (sk - tq) > jnp.where(n > 0, 0, WINDOW)
    dist_c = (tq - sk).astype(F32)
    dist_p = dist_c + float(WINDOW)
    return valid_p, valid_c, dist_p, dist_c


def _att_halves(x, lo, kh):
    r = pltpu.roll(x, ATT_HD, 1)
    zero = jnp.zeros_like(x)
    if kh == 0:
        return jnp.where(lo, x, r), jnp.where(lo, x, zero), jnp.where(lo, zero, r)
    return jnp.where(lo, r, x), jnp.where(lo, r, zero), jnp.where(lo, zero, x)


def _att_probs(qm, k2p, k2c, masks, slope, sink):
    valid_p, valid_c, dist_p, dist_c = masks
    sp = jnp.where(valid_p, _dot(qm, k2p, NT) * (ATT_HD ** -0.5) - slope * dist_p, NEG)
    sc = jnp.where(valid_c, _dot(qm, k2c, NT) * (ATT_HD ** -0.5) - slope * dist_c, NEG)
    m = jnp.maximum(jnp.maximum(jnp.max(sp, axis=-1, keepdims=True), jnp.max(sc, axis=-1, keepdims=True)), sink)
    ep = jnp.exp(sp - m)
    ec = jnp.exp(sc - m)
    es = jnp.exp(sink - m)
    inv = 1.0 / (jnp.sum(ep, axis=-1, keepdims=True) + jnp.sum(ec, axis=-1, keepdims=True) + es)
    return ep * inv, ec * inv, es * inv


def _attn_fwd(q, kv, sinks):
    t = q.shape[0]
    nb = t // WINDOW

    def body(sink_ref, q_ref, kvp_ref, kvc_ref, o_ref):
        n = pl.program_id(0)
        masks = _att_masks(n)
        lo = lax.broadcasted_iota(jnp.int32, (WINDOW, LANES), 1) < ATT_HD
        for kh in range(ATT_KVH):
            k2p, _, _ = _att_halves(kvp_ref[:, 0:LANES], lo, kh)
            k2c, _, _ = _att_halves(kvc_ref[:, 0:LANES], lo, kh)
            _, vlo_p, vhi_p = _att_halves(kvp_ref[:, LANES:2 * LANES], lo, kh)
            _, vlo_c, vhi_c = _att_halves(kvc_ref[:, LANES:2 * LANES], lo, kh)
            for jj in range(ATT_GROUP // 2):
                j = kh * (ATT_GROUP // 2) + jj
                qp = q_ref[:, j * LANES:(j + 1) * LANES]
                zero = jnp.zeros_like(qp)
                out = None
                for par in range(2):
                    hq = 2 * j + par
                    qm = jnp.where(lo, qp, zero) if par == 0 else jnp.where(lo, zero, qp)
                    pp, pc, _ = _att_probs(qm, k2p, k2c, masks, ALIBI_SLOPES[hq], sink_ref[hq])
                    vp, vc = (vlo_p, vlo_c) if par == 0 else (vhi_p, vhi_c)
                    part = _dot(pp.astype(BF16), vp, NN) + _dot(pc.astype(BF16), vc, NN)
                    out = part if out is None else out + part
                o_ref[:, j * LANES:(j + 1) * LANES] = out.astype(BF16)

    return pl.pallas_call(
        body,
        name="attn_fwd",
        grid=(nb,),
        in_specs=[
            pl.BlockSpec(memory_space=pltpu.SMEM),
            pl.BlockSpec((WINDOW, D_MODEL), lambda n: (n, 0)),
            pl.BlockSpec((WINDOW, 2 * LANES), lambda n: (jnp.maximum(n - 1, 0), 0)),
            pl.BlockSpec((WINDOW, 2 * LANES), lambda n: (n, 0)),
        ],
        out_specs=pl.BlockSpec((WINDOW, D_MODEL), lambda n: (n, 0)),
        out_shape=jax.ShapeDtypeStruct((t, D_MODEL), BF16),
        compiler_params=_cparams(("parallel",)),
    )(sinks, q, kv, kv)


def _attn_bwd(q, kv, sinks, out, dout):
    t = q.shape[0]
    nb = t // WINDOW

    def body(sink_ref, q_ref, kvp_ref, kvc_ref, o_ref, do_ref, dq_ref, dkv_ref, dsink_ref, carry_ref):
        n = pl.program_id(0)

        @pl.when(n == 0)
        def _():
            carry_ref[...] = jnp.zeros_like(carry_ref)
            dsink_ref[...] = jnp.zeros_like(dsink_ref)

        @pl.when(n == nb)
        def _():
            dkv_ref[...] = carry_ref[...].astype(BF16)

        @pl.when(n < nb)
        def _():
            masks = _att_masks(n)
            lo = lax.broadcasted_iota(jnp.int32, (WINDOW, LANES), 1) < ATT_HD
            lane1 = lax.broadcasted_iota(jnp.int32, (1, LANES), 1)
            dsink = jnp.zeros((1, LANES), F32)
            halves = []
            for kh in range(ATT_KVH):
                k2p, klo_p, khi_p = _att_halves(kvp_ref[:, 0:LANES], lo, kh)
                k2c, klo_c, khi_c = _att_halves(kvc_ref[:, 0:LANES], lo, kh)
                v2p, _, _ = _att_halves(kvp_ref[:, LANES:2 * LANES], lo, kh)
                v2c, _, _ = _att_halves(kvc_ref[:, LANES:2 * LANES], lo, kh)
                acc = [jnp.zeros((WINDOW, LANES), F32) for _ in range(4)]
                for jj in range(ATT_GROUP // 2):
                    j = kh * (ATT_GROUP // 2) + jj
                    cols = slice(j * LANES, (j + 1) * LANES)
                    qp = q_ref[:, cols]
                    dop = do_ref[:, cols]
                    prod = dop.astype(F32) * o_ref[:, cols].astype(F32)
                    zero = jnp.zeros_like(qp)
                    dq_pair = None
                    for par in range(2):
                        hq = 2 * j + par
                        sel = lo if par == 0 else jnp.logical_not(lo)
                        qm = jnp.where(sel, qp, zero)
                        dom = jnp.where(sel, dop, zero)
                        pp, pc, ps = _att_probs(qm, k2p, k2c, masks, ALIBI_SLOPES[hq], sink_ref[hq])
                        delta = jnp.sum(jnp.where(sel, prod, 0.0), axis=-1, keepdims=True)
                        dsp = (pp * (_dot(dom, v2p, NT) - delta)).astype(BF16)
                        dsc = (pc * (_dot(dom, v2c, NT) - delta)).astype(BF16)
                        dsink = dsink + jnp.where(lane1 == hq, -jnp.sum(ps * delta, axis=0, keepdims=True), 0.0)
                        kp_, kc_ = (klo_p, klo_c) if par == 0 else (khi_p, khi_c)
                        part = _dot(dsp, kp_, NN) + _dot(dsc, kc_, NN)
                        dq_pair = part if dq_pair is None else dq_pair + part
                        acc[0] = acc[0] + _dot(dsp, qm, TN)
                        acc[1] = acc[1] + _dot(dsc, qm, TN)
                        acc[2] = acc[2] + _dot(pp.astype(BF16), dom, TN)
                        acc[3] = acc[3] + _dot(pc.astype(BF16), dom, TN)
                    dq_ref[:, cols] = (dq_pair * (ATT_HD ** -0.5)).astype(BF16)
                halves.append([a + pltpu.roll(a, ATT_HD, 1) for a in acc])
            scale = ATT_HD ** -0.5
            prev = jnp.concatenate(
                [jnp.where(lo, halves[0][0], halves[1][0]) * scale, jnp.where(lo, halves[0][2], halves[1][2])], axis=1)
            cur = jnp.concatenate(
                [jnp.where(lo, halves[0][1], halves[1][1]) * scale, jnp.where(lo, halves[0][3], halves[1][3])], axis=1)
            dkv_ref[...] = (carry_ref[...] + prev).astype(BF16)
            carry_ref[...] = cur
            dsink_ref[...] += dsink

    blk = lambda n: jnp.minimum(n, nb - 1)
    return pl.pallas_call(
        body,
        name="attn_bwd",
        grid=(nb + 1,),
        in_specs=[
            pl.BlockSpec(memory_space=pltpu.SMEM),
            pl.BlockSpec((WINDOW, D_MODEL), lambda n: (blk(n), 0)),
            pl.BlockSpec((WINDOW, 2 * LANES), lambda n: (jnp.maximum(blk(n) - 1, 0), 0)),
            pl.BlockSpec((WINDOW, 2 * LANES), lambda n: (blk(n), 0)),
            pl.BlockSpec((WINDOW, D_MODEL), lambda n: (blk(n), 0)),
            pl.BlockSpec((WINDOW, D_MODEL), lambda n: (blk(n), 0)),
        ],
        out_specs=[
            pl.BlockSpec((WINDOW, D_MODEL), lambda n: (blk(n), 0)),
            pl.BlockSpec((WINDOW, 2 * LANES), lambda n: (jnp.maximum(n - 1, 0), 0)),
            pl.BlockSpec((1, LANES), lambda n: (0, 0)),
        ],
        out_shape=[
            jax.ShapeDtypeStruct((t, D_MODEL), BF16),
            jax.ShapeDtypeStruct((t, 2 * LANES), BF16),
            jax.ShapeDtypeStruct((1, LANES), F32),
        ],
        scratch_shapes=[pltpu.VMEM((WINDOW, 2 * LANES), F32)],
        compiler_params=_cparams(("arbitrary",)),
    )(sinks, q, kv, kv, out, dout)


def _ffn_fwd(h, norm_g, w_up, conv_w, conv_b, w_down, tag):
    xn = _rms_fwd(h, norm_g, name=f"ffn{tag}_norm")
    up = _mm_nn(xn, w_up, name=f"ffn{tag}_up")
    act = _conv_fwd(up, conv_w, conv_b, name=f"ffn{tag}_conv")
    h_out = _mm_nn(act, w_down, res=h, name=f"ffn{tag}_down")
    return h_out, (xn, up, act)


def _ffn_bwd(dh, h, norm_g, w_up, conv_w, conv_b, w_down, saved, tag):
    xn, up, act = saved
    dw_down = _mm_tn(act, dh, 1, D_MODEL, name=f"ffn{tag}_dwdown")
    dact = _mm_nt(dh, w_down, name=f"ffn{tag}_dact")
    dup, dconv_w, dconv_b = _conv_bwd(up, conv_w, conv_b, dact, name=f"ffn{tag}_dconv")
    dw_up = _mm_tn(xn, dup, N_CHIPS, CONV_COLS, stacked=True, name=f"ffn{tag}_dwup")
    dxn = _mm_nt(dup, w_up, stacked=True, name=f"ffn{tag}_dxn")
    dh_in, dnorm = _rms_bwd(h, norm_g, dxn, dh, name=f"ffn{tag}_dnorm")
    return dh_in, dict(ffn_w_down=dw_down, ffn_w_up=dw_up, ffn_conv_w=dconv_w, ffn_conv_b=dconv_b, ffn_norm=dnorm)


def _local_step(x, target, w):
    xn0 = _rms_fwd(x, w["hg_norm"], name="hg_norm")
    proj = _mm_nn(xn0, w["hg_w_in"], name="hg_in")
    o, y, states = _hgrn_fwd(proj, w["hg_lb"], w["hg_out_norm"])
    h_a = _mm_nn(y, w["hg_w_out"], res=x, name="hg_out")
    h1, ffn0 = _ffn_fwd(h_a, w["ffn_norm"][0], w["ffn_w_up"][0], w["ffn_conv_w"][0], w["ffn_conv_b"][0], w["ffn_w_down"][0], 0)
    kvn = _rms_fwd(h1, w["kv_norm"], name="kv_norm")
    kv = _mm_nn(kvn, w["w_kv"], out_dtype=BF16, name="kv_proj")
    xa = _rms_fwd(h1, w["attn_norm"], name="attn_norm")
    qa = _mm_nn(xa, w["attn_w_q"], out_dtype=BF16, name="attn_q")
    ao = _attn_fwd(qa, kv, w["attn_sinks"])
    h_b = _mm_nn(ao, w["attn_w_o"], res=h1, name="attn_o")
    h2, ffn1 = _ffn_fwd(h_b, w["ffn_norm"][1], w["ffn_w_up"][1], w["ffn_conv_w"][1], w["ffn_conv_b"][1], w["ffn_w_down"][1], 1)
    dh2, d_final, loss = _loss_head(h2, w["final_norm"], target)

    dh_b, g1 = _ffn_bwd(dh2, h_b, w["ffn_norm"][1], w["ffn_w_up"][1], w["ffn_conv_w"][1], w["ffn_conv_b"][1], w["ffn_w_down"][1], ffn1, 1)
    dw_o = _mm_tn(ao, dh_b, 1, D_MODEL, name="attn_dwo")
    dao = _mm_nt(dh_b, w["attn_w_o"], out_dtype=BF16, name="attn_dao")
    dqa, dkv, dsinks = _attn_bwd(qa, kv, w["attn_sinks"], ao, dao)
    dw_q = _mm_tn(xa, dqa, 1, D_MODEL, name="attn_dwq")
    dxa = _mm_nt(dqa, w["attn_w_q"], name="attn_dxa")
    dh1, d_attn_norm = _rms_bwd(h1, w["attn_norm"], dxa, dh_b, name="attn_dnorm")
    dw_kv = _mm_tn(kvn, dkv, 1, 2 * LANES, name="kv_dw")
    dkvn = _mm_nt(dkv, w["w_kv"], name="kv_dx")
    dh1, d_kv_norm = _rms_bwd(h1, w["kv_norm"], dkvn, dh1, name="kv_dnorm")
    dh_a, g0 = _ffn_bwd(dh1, h_a, w["ffn_norm"][0], w["ffn_w_up"][0], w["ffn_conv_w"][0], w["ffn_conv_b"][0], w["ffn_w_down"][0], ffn0, 0)
    dw_out = _mm_tn(y, dh_a, 1, D_MODEL, name="hg_dwout")
    dy = _mm_nt(dh_a, w["hg_w_out"], out_dtype=BF16, name="hg_dy")
    dproj, dlb, d_out_norm = _hgrn_bwd(proj, w["hg_lb"], w["hg_out_norm"], o, states, dy)
    dw_in = _mm_tn(xn0, dproj, N_CHIPS, D_MODEL, stacked=True, name="hg_dwin")
    dxn0 = _mm_nt(dproj, w["hg_w_in"], stacked=True, name="hg_dxn")
    dx, d_hg_norm = _rms_bwd(x, w["hg_norm"], dxn0, dh_a, name="hg_dnorm")

    grads = dict(
        hg_norm=d_hg_norm, hg_w_in=dw_in, hg_lb=dlb, hg_out_norm=d_out_norm, hg_w_out=dw_out,
        kv_norm=d_kv_norm, w_kv=dw_kv, attn_norm=d_attn_norm, attn_w_q=dw_q, attn_sinks=dsinks, attn_w_o=dw_o,
        final_norm=d_final,
    )
    for name in g0:
        grads[name] = [g0[name], g1[name]]
    return loss, dx, grads


ANY = pl.BlockSpec(memory_space=pl.ANY)


def _place():
    x, y, c = lax.axis_index("x"), lax.axis_index("y"), lax.axis_index("c")
    chips = [(1 - x, y), (x, 1 - y), (1 - x, 1 - y)]
    return x, y, c, chips


def _rcopy(src, dst, send_sem, recv_sem, to):
    return pltpu.make_async_remote_copy(src_ref=src, dst_ref=dst, send_sem=send_sem, recv_sem=recv_sem, device_id=to, device_id_type=MESH)


MAX_LOCAL_COPIES = 8
BF16_TILE_ROWS = 16


def _n_local_copies(rows):
    for k in (MAX_LOCAL_COPIES, 4, 2):
        if rows % (k * BF16_TILE_ROWS) == 0:
            return k
    return 1


def _gather_shards(shards):
    n = len(shards)

    def body(*refs):
        ins, outs = refs[:n], refs[n:2 * n]
        send_sems, recv_sems, local_sems = refs[2 * n:]
        x, y, c, chips = _place()
        me = 2 * x + y
        sibling = (x, y, 1 - c)
        sends, local = [], []
        for i in range(n):
            rows = ins[i].shape[0]
            k = _n_local_copies(rows)
            for q in range(k):
                part = pl.ds(q * (rows // k), rows // k)
                cp = pltpu.make_async_copy(ins[i].at[part], outs[i].at[me, part], local_sems.at[i, q])
                cp.start()
                local.append(cp)
        for i in range(n):
            h = ins[i].shape[0] // 2
            mine = pl.ds(c * h, h)
            for j, (px, py) in enumerate(chips):
                cp = _rcopy(ins[i].at[mine], outs[i].at[me, mine], send_sems.at[i, j], recv_sems.at[i, j], (px, py, c))
                cp.start()
                sends.append(cp)
        for j, (px, py) in enumerate(chips):
            src = 2 * px + py
            for i in range(n):
                h = ins[i].shape[0] // 2
                mine = pl.ds(c * h, h)
                landed = outs[i].at[src, mine]
                _rcopy(landed, landed, send_sems.at[i, j], recv_sems.at[i, j], sibling).wait_recv()
                cp = _rcopy(landed, landed, send_sems.at[i, 3 + j], recv_sems.at[i, 3 + j], sibling)
                cp.start()
                sends.append(cp)
        for j, (px, py) in enumerate(chips):
            src = 2 * px + py
            for i in range(n):
                h = ins[i].shape[0] // 2
                other = outs[i].at[src, pl.ds((1 - c) * h, h)]
                _rcopy(other, other, send_sems.at[i, 3 + j], recv_sems.at[i, 3 + j], sibling).wait_recv()
        for cp in local:
            cp.wait()
        for cp in sends:
            cp.wait_send()

    return pl.pallas_call(
        body,
        name="gather_weights",
        in_specs=[ANY] * n,
        out_specs=[ANY] * n,
        out_shape=[jax.ShapeDtypeStruct((N_CHIPS,) + s.shape, s.dtype) for s in shards],
        scratch_shapes=[pltpu.SemaphoreType.DMA((n, 6)), pltpu.SemaphoreType.DMA((n, 6)), pltpu.SemaphoreType.DMA((n, MAX_LOCAL_COPIES))],
    )(*shards)


def _allreduce_small(vec):
    rows = vec.shape[0]

    def body(v_ref, o_ref, buf, send_sems, recv_sems):
        x, y, c, _ = _place()
        me = 4 * x + 2 * y + c
        buf[me] = v_ref[...]
        copies = []
        for k in range(1, N_DEV):
            peer = (x ^ (k >> 2), y ^ ((k >> 1) & 1), c ^ (k & 1))
            cp = _rcopy(v_ref, buf.at[me], send_sems.at[k - 1], recv_sems.at[k - 1], peer)
            cp.start()
            copies.append(cp)
        for cp in copies:
            cp.wait()
        acc = buf[0]
        for d in range(1, N_DEV):
            acc = acc + buf[d]
        o_ref[...] = acc

    return pl.pallas_call(
        body,
        name="allreduce_small",
        in_specs=[pl.BlockSpec(memory_space=pltpu.VMEM)],
        out_specs=pl.BlockSpec(memory_space=pltpu.VMEM),
        out_shape=jax.ShapeDtypeStruct(vec.shape, F32),
        scratch_shapes=[pltpu.VMEM((N_DEV, rows, LANES), F32), pltpu.SemaphoreType.DMA((N_DEV - 1,)), pltpu.SemaphoreType.DMA((N_DEV - 1,))],
        compiler_params=pltpu.CompilerParams(vmem_limit_bytes=VMEM_LIMIT_BYTES),
    )(vec)


def _swap_halves(grads):
    n = len(grads)

    def body(*refs):
        ins, outs = refs[:n], refs[n:2 * n]
        send_sems, recv_sems = refs[2 * n:]
        x, y, c, _ = _place()
        copies = []
        for i in range(n):
            h = ins[i].shape[1] // 2
            cp = _rcopy(ins[i].at[:, pl.ds((1 - c) * h, h)], outs[i], send_sems.at[i], recv_sems.at[i], (x, y, 1 - c))
            cp.start()
            copies.append(cp)
        for cp in copies:
            cp.wait()

    return pl.pallas_call(
        body,
        name="rs_swap_halves",
        in_specs=[ANY] * n,
        out_specs=[ANY] * n,
        out_shape=[jax.ShapeDtypeStruct((N_CHIPS, g.shape[1] // 2, g.shape[2]), F32) for g in grads],
        scratch_shapes=[pltpu.SemaphoreType.DMA((n,)), pltpu.SemaphoreType.DMA((n,))],
    )(*grads)


def _send_partials(parts):
    n = len(parts)

    def body(*refs):
        ins, outs = refs[:n], refs[n:2 * n]
        send_sems, recv_sems = refs[2 * n:]
        x, y, c, chips = _place()
        copies = []
        for i in range(n):
            for j, (px, py) in enumerate(chips):
                cp = _rcopy(ins[i].at[2 * px + py], outs[i].at[j], send_sems.at[i, j], recv_sems.at[i, j], (px, py, c))
                cp.start()
                copies.append(cp)
        for cp in copies:
            cp.wait()

    return pl.pallas_call(
        body,
        name="rs_send_partials",
        in_specs=[ANY] * n,
        out_specs=[ANY] * n,
        out_shape=[jax.ShapeDtypeStruct((3,) + p.shape[1:], p.dtype) for p in parts],
        scratch_shapes=[pltpu.SemaphoreType.DMA((n, 3)), pltpu.SemaphoreType.DMA((n, 3))],
    )(*parts)


def _share_halves(bufs):
    n = len(bufs)

    def body(*refs):
        outs = refs[n:2 * n]
        send_sems, recv_sems = refs[2 * n:]
        x, y, c, _ = _place()
        copies = []
        for i in range(n):
            cp = _rcopy(outs[i].at[c], outs[i].at[c], send_sems.at[i], recv_sems.at[i], (x, y, 1 - c))
            cp.start()
            copies.append(cp)
        for cp in copies:
            cp.wait()

    return pl.pallas_call(
        body,
        name="rs_share_halves",
        in_specs=[ANY] * n,
        out_specs=[ANY] * n,
        out_shape=[jax.ShapeDtypeStruct(b.shape, F32) for b in bufs],
        input_output_aliases={i: i for i in range(n)},
        scratch_shapes=[pltpu.SemaphoreType.DMA((n,)), pltpu.SemaphoreType.DMA((n,))],
    )(*bufs)


ELEM_ROWS = (256, 176, 128, 64, 32, 16, 8)


def _add_core_halves(grad, got, c, name):
    s, r, cols = grad.shape
    h = r // 2
    tr = _pick(h, ELEM_ROWS)

    def body(c_ref, g_ref, o_ref, f_ref, b_ref):
        acc = g_ref[...] + o_ref[...]
        f_ref[...] = acc
        b_ref[...] = acc.astype(BF16)

    blk = pl.BlockSpec((None, tr, cols), lambda k, i, c_ref: (k, i, 0))
    return pl.pallas_call(
        body,
        name=name,
        grid_spec=pltpu.PrefetchScalarGridSpec(
            num_scalar_prefetch=1,
            grid=(s, h // tr),
            in_specs=[pl.BlockSpec((None, None, tr, cols), lambda k, i, c_ref: (k, c_ref[0], i, 0)), blk],
            out_specs=[blk, blk],
        ),
        out_shape=[jax.ShapeDtypeStruct((s, h, cols), F32), jax.ShapeDtypeStruct((s, h, cols), BF16)],
        compiler_params=_cparams(("parallel", "parallel")),
    )(c, grad.reshape(s, 2, h, cols), got)


def _add_chip_partials(mine, got, place, name):
    _, h, cols = mine.shape
    tr = _pick(h, ELEM_ROWS)

    def body(place_ref, m_ref, g_ref, o_ref):
        acc = m_ref[...]
        for j in range(3):
            acc = acc + g_ref[j].astype(F32)
        o_ref[...] = acc

    return pl.pallas_call(
        body,
        name=name,
        grid_spec=pltpu.PrefetchScalarGridSpec(
            num_scalar_prefetch=1,
            grid=(h // tr,),
            in_specs=[
                pl.BlockSpec((None, tr, cols), lambda i, place_ref: (place_ref[0], i, 0)),
                pl.BlockSpec((3, tr, cols), lambda i, place_ref: (0, i, 0)),
            ],
            out_specs=pl.BlockSpec((None, tr, cols), lambda i, place_ref: (place_ref[1], i, 0)),
        ),
        out_shape=jax.ShapeDtypeStruct((2, h, cols), F32),
        compiler_params=_cparams(("parallel",)),
    )(place, mine, got)


def _adamw(w, m, v, g, name):
    r, cols = w.shape
    tr = _pick(r, ELEM_ROWS)
    c1 = 1.0 / (1.0 - ADAM_B1 ** ADAM_STEP)
    c2 = 1.0 / (1.0 - ADAM_B2 ** ADAM_STEP)

    def body(w_ref, m_ref, v_ref, g_ref, d_ref, nm_ref, nv_ref):
        gv = g_ref[...]
        nm = ADAM_B1 * m_ref[...] + (1.0 - ADAM_B1) * gv
        nv = ADAM_B2 * v_ref[...] + (1.0 - ADAM_B2) * (gv * gv)
        d_ref[...] = -ADAM_LR * ((nm * c1) / (jnp.sqrt(nv * c2) + ADAM_EPS) + ADAM_WD * w_ref[...])
        nm_ref[...] = nm
        nv_ref[...] = nv

    blk = pl.BlockSpec((tr, cols), lambda i: (i, 0))
    return pl.pallas_call(
        body,
        name=name,
        grid=(r // tr,),
        in_specs=[blk] * 4,
        out_specs=[blk] * 3,
        out_shape=[jax.ShapeDtypeStruct((r, cols), F32)] * 3,
        compiler_params=_cparams(("parallel",)),
    )(w, m, v, g)


SMALL_COLS = 384
SMALL_ROWS = 16


def _pad_rows(flat, rows, cols):
    return jnp.pad(flat, (0, rows * cols - flat.shape[0])).reshape(rows, cols)


def kernel(x, hg_norm, hg_w_in, hg_lb_logits, hg_out_norm, hg_w_out, kv_norm, w_kv, attn_norm, attn_w_q, attn_sinks, attn_w_o, ffn_norm, ffn_w_up, ffn_conv_w, ffn_conv_b, ffn_w_down, final_norm, loss_target, m_hg_norm, m_hg_w_in, m_hg_lb_logits, m_hg_out_norm, m_hg_w_out, m_kv_norm, m_w_kv, m_attn_norm, m_attn_w_q, m_attn_sinks, m_attn_w_o, m_ffn_norm, m_ffn_w_up, m_ffn_conv_w, m_ffn_conv_b, m_ffn_w_down, m_final_norm, v_hg_norm, v_hg_w_in, v_hg_lb_logits, v_hg_out_norm, v_hg_w_out, v_kv_norm, v_w_kv, v_attn_norm, v_attn_w_q, v_attn_sinks, v_attn_w_o, v_ffn_norm, v_ffn_w_up, v_ffn_conv_w, v_ffn_conv_b, v_ffn_w_down, v_final_norm):
    wts = dict(hg_norm=hg_norm, hg_w_in=hg_w_in, hg_lb_logits=hg_lb_logits, hg_out_norm=hg_out_norm, hg_w_out=hg_w_out, kv_norm=kv_norm, w_kv=w_kv, attn_norm=attn_norm, attn_w_q=attn_w_q, attn_sinks=attn_sinks, attn_w_o=attn_w_o, ffn_norm=ffn_norm, ffn_w_up=ffn_w_up, ffn_conv_w=ffn_conv_w, ffn_conv_b=ffn_conv_b, ffn_w_down=ffn_w_down, final_norm=final_norm)
    mom1 = dict(hg_norm=m_hg_norm, hg_w_in=m_hg_w_in, hg_lb_logits=m_hg_lb_logits, hg_out_norm=m_hg_out_norm, hg_w_out=m_hg_w_out, kv_norm=m_kv_norm, w_kv=m_w_kv, attn_norm=m_attn_norm, attn_w_q=m_attn_w_q, attn_sinks=m_attn_sinks, attn_w_o=m_attn_w_o, ffn_norm=m_ffn_norm, ffn_w_up=m_ffn_w_up, ffn_conv_w=m_ffn_conv_w, ffn_conv_b=m_ffn_conv_b, ffn_w_down=m_ffn_w_down, final_norm=m_final_norm)
    mom2 = dict(hg_norm=v_hg_norm, hg_w_in=v_hg_w_in, hg_lb_logits=v_hg_lb_logits, hg_out_norm=v_hg_out_norm, hg_w_out=v_hg_w_out, kv_norm=v_kv_norm, w_kv=v_w_kv, attn_norm=v_attn_norm, attn_w_q=v_attn_w_q, attn_sinks=v_attn_sinks, attn_w_o=v_attn_w_o, ffn_norm=v_ffn_norm, ffn_w_up=v_ffn_w_up, ffn_conv_w=v_ffn_conv_w, ffn_conv_b=v_ffn_conv_b, ffn_w_down=v_ffn_w_down, final_norm=v_final_norm)
    names = list(wts)
    chip = 2 * lax.axis_index("x") + lax.axis_index("y")
    core = lax.axis_index("c")
    core_arr = jnp.reshape(core, (1,)).astype(jnp.int32)
    fs = D_FF // N_CHIPS
    ds = D_MODEL // N_CHIPS

    small = jnp.concatenate([hg_norm.reshape(-1), hg_lb_logits.reshape(-1), ffn_conv_w.reshape(-1)])
    n_small = small.shape[0]
    big_shards = [
        hg_w_in[0].astype(BF16), hg_w_out[0].astype(BF16), w_kv.astype(BF16), attn_w_q[0].astype(BF16), attn_w_o[0].astype(BF16),
        ffn_w_up[0].astype(BF16), ffn_w_up[1].astype(BF16), ffn_w_down[0].astype(BF16), ffn_w_down[1].astype(BF16),
    ]
    gathered = _gather_shards(big_shards + [_pad_rows(small, SMALL_ROWS, SMALL_COLS)])
    g_in, g_out, g_kv, g_q, g_o, g_up0, g_up1, g_dn0, g_dn1, g_small = gathered
    g_small = g_small.reshape(N_CHIPS, -1)[:, :n_small]
    full_hg_norm = g_small[:, :ds].reshape(1, D_MODEL)
    full_lb = g_small[:, ds:3 * ds].reshape(N_CHIPS, 2, ds).transpose(1, 0, 2).reshape(2, D_MODEL)
    full_conv_w = g_small[:, 3 * ds:].reshape(N_CHIPS, 2, 3, fs).transpose(1, 2, 0, 3).reshape(2, 3, D_FF)
    whole = dict(
        hg_norm=full_hg_norm, hg_w_in=g_in, hg_lb=full_lb, hg_out_norm=hg_out_norm,
        hg_w_out=g_out.reshape(1, D_MODEL, D_MODEL), kv_norm=kv_norm.reshape(1, D_MODEL), w_kv=g_kv.reshape(1, D_MODEL, 2 * LANES),
        attn_norm=attn_norm, attn_w_q=g_q.reshape(1, D_MODEL, D_MODEL), attn_sinks=attn_sinks.reshape(ATT_QH),
        attn_w_o=g_o.reshape(1, D_MODEL, D_MODEL),
        ffn_norm=[ffn_norm[0:1], ffn_norm[1:2]], ffn_w_up=[g_up0, g_up1],
        ffn_conv_w=[full_conv_w[0], full_conv_w[1]], ffn_conv_b=[ffn_conv_b[0:1], ffn_conv_b[1:2]],
        ffn_w_down=[g_dn0.reshape(1, D_FF, D_MODEL), g_dn1.reshape(1, D_FF, D_MODEL)],
        final_norm=final_norm.reshape(1, D_MODEL),
    )

    loss, dx, grads = _local_step(x[0], loss_target[0], whole)

    small_parts = [
        loss.reshape(-1), grads["hg_out_norm"].reshape(-1), grads["attn_sinks"].reshape(-1), grads["kv_norm"].reshape(-1),
        grads["attn_norm"].reshape(-1), grads["ffn_norm"][0].reshape(-1), grads["ffn_norm"][1].reshape(-1),
        grads["ffn_conv_b"][0].reshape(-1), grads["ffn_conv_b"][1].reshape(-1), grads["final_norm"].reshape(-1),
        grads["hg_norm"].reshape(-1), grads["hg_lb"].reshape(-1), grads["ffn_conv_w"][0].reshape(-1), grads["ffn_conv_w"][1].reshape(-1),
    ]
    sizes = [p.shape[0] for p in small_parts]
    flat = jnp.concatenate(small_parts)
    rows = -(-flat.shape[0] // (SUBLANES * LANES)) * SUBLANES
    summed = _allreduce_small(_pad_rows(flat, rows, LANES)).reshape(-1)
    offs = [0]
    for sz in sizes:
        offs.append(offs[-1] + sz)
    sm = [summed[offs[i]:offs[i + 1]] for i in range(len(sizes))]
    loss_out = sm[0][0]
    conv_w_full = jnp.stack([sm[12].reshape(3, D_FF), sm[13].reshape(3, D_FF)])
    small_grads = dict(
        hg_out_norm=sm[1].reshape(1, HG_DK), attn_sinks=sm[2][:ATT_QH].reshape(1, ATT_QH), kv_norm=sm[3], attn_norm=sm[4].reshape(1, D_MODEL),
        ffn_norm=jnp.stack([sm[5], sm[6]]), ffn_conv_b=jnp.stack([sm[7], sm[8]]), final_norm=sm[9],
        hg_norm=lax.dynamic_slice(sm[10].reshape(1, D_MODEL), (0, chip * ds), (1, ds)),
        hg_lb_logits=lax.dynamic_slice(sm[11].reshape(2, D_MODEL), (0, chip * ds), (2, ds)),
        ffn_conv_w=lax.dynamic_slice(conv_w_full, (0, 0, chip * fs), (2, 3, fs)),
    )

    big_names = ["hg_w_in", "hg_w_out", "w_kv", "attn_w_q", "attn_w_o", "ffn_w_up", "ffn_w_up", "ffn_w_down", "ffn_w_down"]
    stacked = [
        grads["hg_w_in"], grads["hg_w_out"].reshape(N_CHIPS, ds, D_MODEL), grads["w_kv"].reshape(N_CHIPS, ds, 2 * LANES),
        grads["attn_w_q"].reshape(N_CHIPS, ds, D_MODEL), grads["attn_w_o"].reshape(N_CHIPS, ds, D_MODEL),
        grads["ffn_w_up"][0], grads["ffn_w_up"][1],
        grads["ffn_w_down"][0].reshape(N_CHIPS, fs, D_MODEL), grads["ffn_w_down"][1].reshape(N_CHIPS, fs, D_MODEL),
    ]
    from_core = _swap_halves(stacked)
    sums = [_add_core_halves(g, o, core_arr, name=f"rs_add_core_{i}") for i, (g, o) in enumerate(zip(stacked, from_core))]
    from_chips = _send_partials([b for _, b in sums])
    place_arr = jnp.stack([chip, core]).astype(jnp.int32)
    halves = [_add_chip_partials(f, o, place_arr, name=f"rs_add_chip_{i}") for i, ((f, _), o) in enumerate(zip(sums, from_chips))]
    reduced = [r.reshape((-1,) + r.shape[2:]) for r in _share_halves(halves)]
    big_grads = dict(
        hg_w_in=reduced[0], hg_w_out=reduced[1], w_kv=reduced[2], attn_w_q=reduced[3], attn_w_o=reduced[4],
        ffn_w_up=jnp.concatenate([reduced[5], reduced[6]]), ffn_w_down=jnp.concatenate([reduced[7], reduced[8]]),
    )

    out_g, out_d, out_m, out_v = {}, {}, {}, {}
    for name in big_grads:
        g2 = big_grads[name]
        shape = wts[name].shape
        d2, m2, v2 = _adamw(wts[name].reshape(g2.shape), mom1[name].reshape(g2.shape), mom2[name].reshape(g2.shape), g2, name=f"adamw_{name}")
        out_g[name], out_d[name], out_m[name], out_v[name] = g2.reshape(shape), d2.reshape(shape), m2.reshape(shape), v2.reshape(shape)
    small_names = [n for n in names if n not in big_grads]
    cat = lambda d: jnp.concatenate([d[n].reshape(-1) for n in small_names])
    n_flat = sum(wts[n].size for n in small_names)
    srows = -(-n_flat // (SUBLANES * LANES)) * SUBLANES
    packed = [_pad_rows(cat(d), srows, LANES) for d in (wts, mom1, mom2, small_grads)]
    d_s, m_s, v_s = _adamw(*packed, name="adamw_small")
    off = 0
    for n in small_names:
        sz, shape = wts[n].size, wts[n].shape
        out_g[n] = small_grads[n].reshape(shape)
        out_d[n] = d_s.reshape(-1)[off:off + sz].reshape(shape)
        out_m[n] = m_s.reshape(-1)[off:off + sz].reshape(shape)
        out_v[n] = v_s.reshape(-1)[off:off + sz].reshape(shape)
        off += sz

    grad_x = dx.reshape(x.shape)
    return (loss_out, grad_x, *[out_g[n] for n in names], *[out_d[n] for n in names], *[out_m[n] for n in names], *[out_v[n] for n in names])
```

```python
import functools

import jax
import jax.numpy as jnp
from jax import lax
from jax.experimental import pallas as pl
from jax.experimental.pallas import tpu as pltpu

F32 = jnp.float32
BF16 = jnp.bfloat16
MESH = pl.DeviceIdType.MESH

EPS = 1e-6
D_MODEL = 1024
HG_HEADS = 8
HG_DK = 128
HG_CHUNK = 64
ATT_HD = 64
ATT_QH = 16
ATT_KVH = 2
ATT_GROUP = ATT_QH // ATT_KVH
WINDOW = 128
D_FF = 2816
N_CHIPS = 4
N_DEV = 8
LANES = 128
SUBLANES = 8
VMEM_LIMIT_BYTES = 56 * 1024 * 1024
NEG = -1e30
ALIBI_SLOPES = tuple(2.0 ** (-8.0 * h / ATT_QH) for h in range(1, ATT_QH + 1))

ADAM_LR = 0.001
ADAM_B1 = 0.9
ADAM_B2 = 0.999
ADAM_EPS = 1e-08
ADAM_WD = 0.01
ADAM_STEP = 10


def _cparams(sem=None):
    return pltpu.CompilerParams(dimension_semantics=sem, vmem_limit_bytes=VMEM_LIMIT_BYTES)


def _pick(n, cands):
    for c in cands:
        if n % c == 0:
            return c
    return n


def _sigmoid(x):
    return 1.0 / (1.0 + jnp.exp(-x))


def _dot(a, b, dims):
    return lax.dot_general(a, b, (dims, ((), ())), preferred_element_type=F32)


NN = ((1,), (0,))
NT = ((1,), (1,))
TN = ((0,), (0,))


def _mm_nn(a, w, res=None, out_dtype=F32, name="mm_nn"):
    m, k = a.shape
    s, _, ns = w.shape
    tm = min(m, 512)
    tn = _pick(ns, (512, 1408, 256, 128))
    npb = ns // tn

    def body(a_ref, w_ref, *rest):
        o_ref = rest[-1]
        acc = _dot(a_ref[...].astype(BF16), w_ref[...], NN)
        if res is not None:
            acc = acc + rest[0][...]
        o_ref[...] = acc.astype(o_ref.dtype)

    in_specs = [
        pl.BlockSpec((tm, k), lambda i, j: (i, 0)),
        pl.BlockSpec((None, k, tn), lambda i, j: (j // npb, 0, j % npb)),
    ]
    args = [a, w]
    if res is not None:
        in_specs.append(pl.BlockSpec((tm, tn), lambda i, j: (i, j)))
        args.append(res)
    return pl.pallas_call(
        body,
        name=name,
        grid=(m // tm, s * npb),
        in_specs=in_specs,
        out_specs=pl.BlockSpec((tm, tn), lambda i, j: (i, j)),
        out_shape=jax.ShapeDtypeStruct((m, s * ns), out_dtype),
        compiler_params=_cparams(("parallel", "parallel")),
    )(*args)


def _dy_spec(stacked, tm, tn, npb, row, kk):
    if stacked:
        return pl.BlockSpec((None, tm, tn), lambda *g: (kk(g) // npb, row(g), kk(g) % npb))
    return pl.BlockSpec((tm, tn), lambda *g: (row(g), kk(g)))


def _mm_nt(dy, w, stacked=False, out_dtype=F32, name="mm_nt"):
    s, k, ns = w.shape
    m = dy.shape[1] if stacked else dy.shape[0]
    tm = min(m, 512)
    tko = _pick(k, (1024, 1408, 512, 256))
    tn = _pick(ns, (1024, 1408, 512, 256))
    npb = ns // tn
    nk = s * npb

    def body(dy_ref, w_ref, o_ref, acc_ref):
        kk = pl.program_id(2)

        @pl.when(kk == 0)
        def _():
            acc_ref[...] = jnp.zeros_like(acc_ref)

        acc_ref[...] += _dot(dy_ref[...].astype(BF16), w_ref[...], NT)

        @pl.when(kk == nk - 1)
        def _():
            o_ref[...] = acc_ref[...].astype(o_ref.dtype)

    return pl.pallas_call(
        body,
        name=name,
        grid=(m // tm, k // tko, nk),
        in_specs=[
            _dy_spec(stacked, tm, tn, npb, lambda g: g[0], lambda g: g[2]),
            pl.BlockSpec((None, tko, tn), lambda i, j, kk: (kk // npb, j, kk % npb)),
        ],
        out_specs=pl.BlockSpec((tm, tko), lambda i, j, kk: (i, j)),
        out_shape=jax.ShapeDtypeStruct((m, k), out_dtype),
        scratch_shapes=[pltpu.VMEM((tm, tko), F32)],
        compiler_params=_cparams(("parallel", "parallel", "arbitrary")),
    )(dy, w)


def _mm_tn(a, dy, s, ns, stacked=False, name="mm_tn"):
    m, k = a.shape
    tm = min(m, 512)
    tk = _pick(k, (1024, 1408, 512, 256))
    tn = _pick(ns, (512, 1408, 256, 128))
    npb = ns // tn
    nm = m // tm

    def body(a_ref, dy_ref, o_ref, acc_ref):
        mm = pl.program_id(2)

        @pl.when(mm == 0)
        def _():
            acc_ref[...] = jnp.zeros_like(acc_ref)

        acc_ref[...] += _dot(a_ref[...].astype(BF16), dy_ref[...].astype(BF16), TN)

        @pl.when(mm == nm - 1)
        def _():
            o_ref[...] = acc_ref[...]

    return pl.pallas_call(
        body,
        name=name,
        grid=(k // tk, s * npb, nm),
        in_specs=[
            pl.BlockSpec((tm, tk), lambda i, j, mm: (mm, i)),
            _dy_spec(stacked, tm, tn, npb, lambda g: g[2], lambda g: g[1]),
        ],
        out_specs=pl.BlockSpec((None, tk, tn), lambda i, j, mm: (j // npb, i, j % npb)),
        out_shape=jax.ShapeDtypeStruct((s, k, ns), F32),
        scratch_shapes=[pltpu.VMEM((tk, tn), F32)],
        compiler_params=_cparams(("parallel", "parallel", "arbitrary")),
    )(a, dy)


ROW_TILE = 256


def _rms_fwd(x, g, name="rms_fwd"):
    t, d = x.shape
    r = min(t, ROW_TILE)

    def body(x_ref, g_ref, o_ref):
        xv = x_ref[...]
        rstd = lax.rsqrt(jnp.mean(xv * xv, axis=-1, keepdims=True) + EPS)
        o_ref[...] = (xv * rstd * g_ref[...]).astype(BF16)

    return pl.pallas_call(
        body,
        name=name,
        grid=(t // r,),
        in_specs=[pl.BlockSpec((r, d), lambda i: (i, 0)), pl.BlockSpec((1, d), lambda i: (0, 0))],
        out_specs=pl.BlockSpec((r, d), lambda i: (i, 0)),
        out_shape=jax.ShapeDtypeStruct((t, d), BF16),
        compiler_params=_cparams(("parallel",)),
    )(x, g)


def _rms_bwd(x, g, dxn, dres, name="rms_bwd"):
    t, d = x.shape
    r = min(t, ROW_TILE)

    def body(x_ref, g_ref, dxn_ref, dres_ref, dx_ref, dg_ref):
        @pl.when(pl.program_id(0) == 0)
        def _():
            dg_ref[...] = jnp.zeros_like(dg_ref)

        xv = x_ref[...]
        rstd = lax.rsqrt(jnp.mean(xv * xv, axis=-1, keepdims=True) + EPS)
        xhat = xv * rstd
        dxn_v = dxn_ref[...].astype(F32)
        gd = dxn_v * g_ref[...]
        dx_ref[...] = dres_ref[...] + rstd * (gd - xhat * jnp.mean(gd * xhat, axis=-1, keepdims=True))
        dg_ref[...] += jnp.sum(dxn_v * xhat, axis=0, keepdims=True)

    return pl.pallas_call(
        body,
        name=name,
        grid=(t // r,),
        in_specs=[
            pl.BlockSpec((r, d), lambda i: (i, 0)),
            pl.BlockSpec((1, d), lambda i: (0, 0)),
            pl.BlockSpec((r, d), lambda i: (i, 0)),
            pl.BlockSpec((r, d), lambda i: (i, 0)),
        ],
        out_specs=[pl.BlockSpec((r, d), lambda i: (i, 0)), pl.BlockSpec((1, d), lambda i: (0, 0))],
        out_shape=[jax.ShapeDtypeStruct((t, d), F32), jax.ShapeDtypeStruct((1, d), F32)],
        compiler_params=_cparams(("arbitrary",)),
    )(x, g, dxn, dres)


def _loss_head(h, g, target):
    t, d = h.shape
    r = min(t, ROW_TILE)

    def body(h_ref, g_ref, t_ref, dh_ref, dg_ref, loss_ref):
        @pl.when(pl.program_id(0) == 0)
        def _():
            dg_ref[...] = jnp.zeros_like(dg_ref)
            loss_ref[...] = jnp.zeros_like(loss_ref)

        xv = h_ref[...]
        rstd = lax.rsqrt(jnp.mean(xv * xv, axis=-1, keepdims=True) + EPS)
        xhat = xv * rstd
        gv = g_ref[...]
        err = xhat * gv - t_ref[...]
        loss_ref[...] += 0.5 * jnp.sum(jnp.mean(err * err, axis=-1, keepdims=True), axis=0, keepdims=True)
        dy = err * (1.0 / d)
        gd = dy * gv
        dh_ref[...] = rstd * (gd - xhat * jnp.mean(gd * xhat, axis=-1, keepdims=True))
        dg_ref[...] += jnp.sum(dy * xhat, axis=0, keepdims=True)

    return pl.pallas_call(
        body,
        name="loss_head",
        grid=(t // r,),
        in_specs=[
            pl.BlockSpec((r, d), lambda i: (i, 0)),
            pl.BlockSpec((1, d), lambda i: (0, 0)),
            pl.BlockSpec((r, d), lambda i: (i, 0)),
        ],
        out_specs=[
            pl.BlockSpec((r, d), lambda i: (i, 0)),
            pl.BlockSpec((1, d), lambda i: (0, 0)),
            pl.BlockSpec((1, LANES), lambda i: (0, 0)),
        ],
        out_shape=[
            jax.ShapeDtypeStruct((t, d), F32),
            jax.ShapeDtypeStruct((1, d), F32),
            jax.ShapeDtypeStruct((1, LANES), F32),
        ],
        compiler_params=_cparams(("arbitrary",)),
    )(h, g, target)


CONV_ROWS = 128
CONV_COLS = 1408


def _conv_taps(x_ext, n):
    tot = x_ext.shape[0]
    g1 = pltpu.roll(x_ext, 1, 0)[tot - n:]
    g2 = pltpu.roll(x_ext, 2, 0)[tot - n:]
    return g2, g1


def _conv_fwd(up, conv_w, conv_b, name="conv_fwd"):
    t = up.shape[0]
    r = min(t, CONV_ROWS)
    tc = CONV_COLS
    ncb = D_FF // tc
    hb = r // SUBLANES

    def body(g_ref, halo_ref, v_ref, w_ref, b_ref, o_ref):
        i = pl.program_id(1)
        g0 = g_ref[...]
        halo = halo_ref[...] * jnp.where(i > 0, 1.0, 0.0)
        g2, g1 = _conv_taps(jnp.concatenate([halo, g0], axis=0), r)
        c = b_ref[...] + w_ref[0:1, :] * g2 + w_ref[1:2, :] * g1 + w_ref[2:3, :] * g0
        o_ref[...] = (c * _sigmoid(c) * v_ref[...]).astype(BF16)

    return pl.pallas_call(
        body,
        name=name,
        grid=(ncb, t // r),
        in_specs=[
            pl.BlockSpec((r, tc), lambda j, i: (i, j)),
            pl.BlockSpec((SUBLANES, tc), lambda j, i: (jnp.maximum(i * hb - 1, 0), j)),
            pl.BlockSpec((r, tc), lambda j, i: (i, ncb + j)),
            pl.BlockSpec((3, tc), lambda j, i: (0, j)),
            pl.BlockSpec((1, tc), lambda j, i: (0, j)),
        ],
        out_specs=pl.BlockSpec((r, tc), lambda j, i: (i, j)),
        out_shape=jax.ShapeDtypeStruct((t, D_FF), BF16),
        compiler_params=_cparams(("parallel", "parallel")),
    )(up, up, up, conv_w, conv_b)


def _conv_bwd(up, conv_w, conv_b, dact, name="conv_bwd"):
    t = up.shape[0]
    r = min(t, CONV_ROWS)
    tc = CONV_COLS
    ncb = D_FF // tc
    hb = r // SUBLANES
    nrt = t // r

    def body(g_ref, halo_ref, v_ref, w_ref, b_ref, da_ref, dup_ref, dw_ref, db_ref, nxt_ref):
        ii = pl.program_id(1)
        i = nrt - 1 - ii

        @pl.when(ii == 0)
        def _():
            nxt_ref[...] = jnp.zeros_like(nxt_ref)
            dw_ref[...] = jnp.zeros_like(dw_ref)
            db_ref[...] = jnp.zeros_like(db_ref)

        g0 = g_ref[...]
        halo = halo_ref[...] * jnp.where(i > 0, 1.0, 0.0)
        g2, g1 = _conv_taps(jnp.concatenate([halo, g0], axis=0), r)
        w0, w1, w2 = w_ref[0:1, :], w_ref[1:2, :], w_ref[2:3, :]
        c = b_ref[...] + w0 * g2 + w1 * g1 + w2 * g0
        sg = _sigmoid(c)
        da = da_ref[...]
        dval = da * (c * sg)
        dc = da * v_ref[...] * (sg * (1.0 + c * (1.0 - sg)))
        db_ref[...] += jnp.sum(dc, axis=0, keepdims=True)
        dw_ref[0:1, :] += jnp.sum(dc * g2, axis=0, keepdims=True)
        dw_ref[1:2, :] += jnp.sum(dc * g1, axis=0, keepdims=True)
        dw_ref[2:3, :] += jnp.sum(dc * g0, axis=0, keepdims=True)
        ext = jnp.concatenate([dc, nxt_ref[...]], axis=0)
        tot = r + SUBLANES
        d1 = pltpu.roll(ext, tot - 1, 0)[:r]
        d2 = pltpu.roll(ext, tot - 2, 0)[:r]
        dgate = w2 * dc + w1 * d1 + w0 * d2
        nxt_ref[...] = dc[:SUBLANES]
        dup_ref[0] = dgate.astype(BF16)
        dup_ref[1] = dval.astype(BF16)

    rev = lambda ii: nrt - 1 - ii
    dup, dw, db = pl.pallas_call(
        body,
        name=name,
        grid=(ncb, nrt),
        in_specs=[
            pl.BlockSpec((r, tc), lambda j, ii: (rev(ii), j)),
            pl.BlockSpec((SUBLANES, tc), lambda j, ii: (jnp.maximum(rev(ii) * hb - 1, 0), j)),
            pl.BlockSpec((r, tc), lambda j, ii: (rev(ii), ncb + j)),
            pl.BlockSpec((3, tc), lambda j, ii: (0, j)),
            pl.BlockSpec((1, tc), lambda j, ii: (0, j)),
            pl.BlockSpec((r, tc), lambda j, ii: (rev(ii), j)),
        ],
        out_specs=[
            pl.BlockSpec((2, None, r, tc), lambda j, ii: (0, j, rev(ii), 0)),
            pl.BlockSpec((3, tc), lambda j, ii: (0, j)),
            pl.BlockSpec((1, tc), lambda j, ii: (0, j)),
        ],
        out_shape=[
            jax.ShapeDtypeStruct((2, ncb, t, tc), BF16),
            jax.ShapeDtypeStruct((3, D_FF), F32),
            jax.ShapeDtypeStruct((1, D_FF), F32),
        ],
        scratch_shapes=[pltpu.VMEM((SUBLANES, tc), F32)],
        compiler_params=_cparams(("parallel", "arbitrary")),
    )(up, up, up, conv_w, conv_b, dact)
    return dup.reshape(2 * ncb, t, tc), dw, db


def _split3(x):
    x1 = x.astype(BF16)
    r1 = x - x1.astype(F32)
    x2 = r1.astype(BF16)
    x3 = (r1 - x2.astype(F32)).astype(BF16)
    return x1, x2, x3


def _tri_dot(tri, x, dims):
    x1, x2, x3 = _split3(x)
    return _dot(tri, x1, dims) + _dot(tri, x2, dims) + _dot(tri, x3, dims)


def _lower_bound(logits_ref):
    return _sigmoid(logits_ref[0:1, :] - logits_ref[1:2, :])


def _hg_gates(qr, fr, lb):
    q = qr * _sigmoid(qr) * (HG_DK ** -0.5)
    sf = _sigmoid(fr)
    fg = lb + (1.0 - lb) * sf
    return q, sf, fg


def _hg_chunk_terms(q, fg, tril_b, low_half):
    g = jnp.log(fg)
    k = 1.0 - fg
    cum = _tri_dot(tril_b, g, NN)
    c_last = jnp.sum(g, axis=0, keepdims=True)
    c_mid = jnp.sum(jnp.where(low_half, g, 0.0), axis=0, keepdims=True)
    e_q = jnp.exp(cum - c_mid)
    e_k = jnp.exp(c_mid - cum)
    e_0 = jnp.exp(cum)
    e_l = jnp.exp(c_last - cum)
    return k, e_q, e_k, e_0, e_l, jnp.exp(c_last)


def _hg_specs(t, col0s):
    return [pl.BlockSpec((t, HG_DK), functools.partial(lambda h, c0: (0, c0 + h), c0=c0)) for c0 in col0s]


def _hgrn_fwd(proj, lb, wn):
    t = proj.shape[0]
    c = HG_CHUNK
    nc = t // c

    def body(q_ref, f_ref, i_ref, g_ref, lb_ref, wn_ref, o_ref, y_ref, st_ref, s_scr):
        s_scr[...] = jnp.zeros_like(s_scr)
        lbv = _lower_bound(lb_ref)
        wnv = wn_ref[...]
        ri = lax.broadcasted_iota(jnp.int32, (c, c), 0)
        ci = lax.broadcasted_iota(jnp.int32, (c, c), 1)
        tril = ri >= ci
        tril_b = tril.astype(BF16)
        low_half = lax.broadcasted_iota(jnp.int32, (c, HG_DK), 0) < c // 2

        def chunk(n, carry):
            rows = pl.ds(pl.multiple_of(n * c, c), c)
            q, _, fg = _hg_gates(q_ref[rows, :], f_ref[rows, :], lbv)
            v = i_ref[rows, :].astype(BF16)
            k, e_q, e_k, e_0, e_l, e_last = _hg_chunk_terms(q, fg, tril_b, low_half)
            st = s_scr[...]
            st_ref[n] = st
            a = jnp.where(tril, _dot((q * e_q).astype(BF16), (k * e_k).astype(BF16), NT), 0.0)
            o = _dot((q * e_0).astype(BF16), st.astype(BF16), NT) + _dot(a.astype(BF16), v, NN)
            s_scr[...] = st * e_last + _dot(v, (k * e_l).astype(BF16), TN)
            o_ref[rows, :] = o
            rstd = lax.rsqrt(jnp.mean(o * o, axis=-1, keepdims=True) + EPS)
            gr = g_ref[rows, :]
            y_ref[rows, :] = (o * rstd * wnv * (gr * _sigmoid(gr))).astype(BF16)
            return carry

        lax.fori_loop(0, nc, chunk, 0)

    vec = pl.BlockSpec((2, HG_DK), lambda h: (0, h))
    return pl.pallas_call(
        body,
        name="hgrn_fwd",
        grid=(HG_HEADS,),
        in_specs=_hg_specs(t, (0, HG_HEADS, 2 * HG_HEADS, 3 * HG_HEADS)) + [vec, pl.BlockSpec((1, HG_DK), lambda h: (0, 0))],
        out_specs=[
            pl.BlockSpec((t, HG_DK), lambda h: (0, h)),
            pl.BlockSpec((t, HG_DK), lambda h: (0, h)),
            pl.BlockSpec((None, nc, HG_DK, HG_DK), lambda h: (h, 0, 0, 0)),
        ],
        out_shape=[
            jax.ShapeDtypeStruct((t, D_MODEL), F32),
            jax.ShapeDtypeStruct((t, D_MODEL), BF16),
            jax.ShapeDtypeStruct((HG_HEADS, nc, HG_DK, HG_DK), F32),
        ],
        scratch_shapes=[pltpu.VMEM((HG_DK, HG_DK), F32)],
        compiler_params=_cparams(("parallel",)),
    )(proj, proj, proj, proj, lb, wn)


def _hgrn_bwd(proj, lb, wn, o, states, dy):
    t = proj.shape[0]
    c = HG_CHUNK
    nc = t // c

    def body(q_ref, f_ref, i_ref, g_ref, lb_ref, wn_ref, o_ref, st_ref, dy_ref, dp_ref, dl_ref, dwn_ref, ds_scr, dlb_scr):
        @pl.when(pl.program_id(0) == 0)
        def _():
            dwn_ref[...] = jnp.zeros_like(dwn_ref)

        ds_scr[...] = jnp.zeros_like(ds_scr)
        dlb_scr[...] = jnp.zeros_like(dlb_scr)
        lbv = _lower_bound(lb_ref)
        wnv = wn_ref[...]
        ri = lax.broadcasted_iota(jnp.int32, (c, c), 0)
        ci = lax.broadcasted_iota(jnp.int32, (c, c), 1)
        tril = ri >= ci
        tril_b = tril.astype(BF16)
        low_half = lax.broadcasted_iota(jnp.int32, (c, HG_DK), 0) < c // 2

        def chunk(nn, carry):
            n = nc - 1 - nn
            rows = pl.ds(pl.multiple_of(n * c, c), c)
            ov = o_ref[rows, :]
            gr = g_ref[rows, :]
            dyv = dy_ref[rows, :].astype(F32)
            rstd = lax.rsqrt(jnp.mean(ov * ov, axis=-1, keepdims=True) + EPS)
            ohat = ov * rstd
            sg = _sigmoid(gr)
            dg_raw = dyv * (ohat * wnv) * (sg * (1.0 + gr * (1.0 - sg)))
            don = dyv * (gr * sg)
            dwn_ref[...] += jnp.sum(don * ohat, axis=0, keepdims=True)
            gd = don * wnv
            do = rstd * (gd - ohat * jnp.mean(gd * ohat, axis=-1, keepdims=True))
            do_b = do.astype(BF16)
            qr = q_ref[rows, :]
            q, sf, fg = _hg_gates(qr, f_ref[rows, :], lbv)
            v = i_ref[rows, :].astype(BF16)
            k, e_q, e_k, e_0, e_l, e_last = _hg_chunk_terms(q, fg, tril_b, low_half)
            qi, qi_lo, _ = _split3(q * e_q)
            ki, ki_lo, _ = _split3(k * e_k)
            q0 = (q * e_0).astype(BF16)
            kl = (k * e_l).astype(BF16)
            st = st_ref[n]
            st_b = st.astype(BF16)
            ds = ds_scr[...]
            ds_b = ds.astype(BF16)
            a_b = jnp.where(tril, _dot(qi, ki, NT), 0.0).astype(BF16)
            da_b = jnp.where(tril, _dot(do_b, v, NT), 0.0).astype(BF16)
            dq = _dot(do_b, st_b, NN) * e_0 + (_dot(da_b, ki, NN) + _dot(da_b, ki_lo, NN)) * e_q
            dk_state = _dot(v, ds_b, NN) * e_l
            dk = (_dot(da_b, qi, TN) + _dot(da_b, qi_lo, TN)) * e_k + dk_state
            dv = _dot(a_b, do_b, TN) + _dot(kl, ds_b, NT)
            ds_scr[...] = ds * e_last + _dot(do_b, q0, TN)
            d_last = jnp.sum(dk_state * k, axis=0, keepdims=True) + jnp.sum(ds * st, axis=0, keepdims=True) * e_last
            dlogf = _tri_dot(tril_b, q * dq - k * dk, TN) + d_last
            dfg = dlogf / fg - dk
            dlb_scr[...] += jnp.sum(dfg * (1.0 - sf), axis=0, keepdims=True)
            sq = _sigmoid(qr)
            dp_ref[0, rows, :] = (dq * (HG_DK ** -0.5) * (sq * (1.0 + qr * (1.0 - sq)))).astype(BF16)
            dp_ref[1, rows, :] = (dfg * (1.0 - lbv) * sf * (1.0 - sf)).astype(BF16)
            dp_ref[2, rows, :] = dv.astype(BF16)
            dp_ref[3, rows, :] = dg_raw.astype(BF16)
            return carry

        lax.fori_loop(0, nc, chunk, 0)
        d0 = dlb_scr[...] * lbv * (1.0 - lbv)
        dl_ref[0:1, :] = d0
        dl_ref[1:2, :] = -d0

    vec = pl.BlockSpec((2, HG_DK), lambda h: (0, h))
    one = pl.BlockSpec((1, HG_DK), lambda h: (0, 0))
    col = pl.BlockSpec((t, HG_DK), lambda h: (0, h))
    return pl.pallas_call(
        body,
        name="hgrn_bwd",
        grid=(HG_HEADS,),
        in_specs=_hg_specs(t, (0, HG_HEADS, 2 * HG_HEADS, 3 * HG_HEADS))
        + [vec, one, col, pl.BlockSpec((None, nc, HG_DK, HG_DK), lambda h: (h, 0, 0, 0)), col],
        out_specs=[pl.BlockSpec((4, t, HG_DK), lambda h: (0, 0, h)), vec, one],
        out_shape=[
            jax.ShapeDtypeStruct((4, t, D_MODEL), BF16),
            jax.ShapeDtypeStruct((2, D_MODEL), F32),
            jax.ShapeDtypeStruct((1, HG_DK), F32),
        ],
        scratch_shapes=[pltpu.VMEM((HG_DK, HG_DK), F32), pltpu.VMEM((1, HG_DK), F32)],
        compiler_params=_cparams(("arbitrary",)),
    )(proj, proj, proj, proj, lb, wn, o, states, dy)


def _att_masks(n):
    tq = lax.broadcasted_iota(jnp.int32, (WINDOW, WINDOW), 0)
    sk = lax.broadcasted_iota(jnp.int32, (WINDOW, WINDOW), 1)
    valid_c = sk <= tq
    valid_p = (sk - tq) > jnp.where(n > 0, 0, WINDOW)
    dist_c = (tq - sk).astype(F32)
    dist_p = dist_c + float(WINDOW)
    return valid_p, valid_c, dist_p, dist_c


def _att_halves(x, lo, kh):
    r = pltpu.roll(x, ATT_HD, 1)
    zero = jnp.zeros_like(x)
    if kh == 0:
        return jnp.where(lo, x, r), jnp.where(lo, x, zero), jnp.where(lo, zero, r)
    return jnp.where(lo, r, x), jnp.where(lo, r, zero), jnp.where(lo, zero, x)


def _att_probs(qm, k2p, k2c, masks, slope, sink):
    valid_p, valid_c, dist_p, dist_c = masks
    sp = jnp.where(valid_p, _dot(qm, k2p, NT) * (ATT_HD ** -0.5) - slope * dist_p, NEG)
    sc = jnp.where(valid_c, _dot(qm, k2c, NT) * (ATT_HD ** -0.5) - slope * dist_c, NEG)
    m = jnp.maximum(jnp.maximum(jnp.max(sp, axis=-1, keepdims=True), jnp.max(sc, axis=-1, keepdims=True)), sink)
    ep = jnp.exp(sp - m)
    ec = jnp.exp(sc - m)
    es = jnp.exp(sink - m)
    inv = 1.0 / (jnp.sum(ep, axis=-1, keepdims=True) + jnp.sum(ec, axis=-1, keepdims=True) + es)
    return ep * inv, ec * inv, es * inv


def _attn_fwd(q, kv, sinks):
    t = q.shape[0]
    nb = t // WINDOW

    def body(sink_ref, q_ref, kvp_ref, kvc_ref, o_ref):
        n = pl.program_id(0)
        masks = _att_masks(n)
        lo = lax.broadcasted_iota(jnp.int32, (WINDOW, LANES), 1) < ATT_HD
        for kh in range(ATT_KVH):
            k2p, _, _ = _att_halves(kvp_ref[:, 0:LANES], lo, kh)
            k2c, _, _ = _att_halves(kvc_ref[:, 0:LANES], lo, kh)
            _, vlo_p, vhi_p = _att_halves(kvp_ref[:, LANES:2 * LANES], lo, kh)
            _, vlo_c, vhi_c = _att_halves(kvc_ref[:, LANES:2 * LANES], lo, kh)
            for jj in range(ATT_GROUP // 2):
                j = kh * (ATT_GROUP // 2) + jj
                qp = q_ref[:, j * LANES:(j + 1) * LANES]
                zero = jnp.zeros_like(qp)
                out = None
                for par in range(2):
                    hq = 2 * j + par
                    qm = jnp.where(lo, qp, zero) if par == 0 else jnp.where(lo, zero, qp)
                    pp, pc, _ = _att_probs(qm, k2p, k2c, masks, ALIBI_SLOPES[hq], sink_ref[hq])
                    vp, vc = (vlo_p, vlo_c) if par == 0 else (vhi_p, vhi_c)
                    part = _dot(pp.astype(BF16), vp, NN) + _dot(pc.astype(BF16), vc, NN)
                    out = part if out is None else out + part
                o_ref[:, j * LANES:(j + 1) * LANES] = out.astype(BF16)

    return pl.pallas_call(
        body,
        name="attn_fwd",
        grid=(nb,),
        in_specs=[
            pl.BlockSpec(memory_space=pltpu.SMEM),
            pl.BlockSpec((WINDOW, D_MODEL), lambda n: (n, 0)),
            pl.BlockSpec((WINDOW, 2 * LANES), lambda n: (jnp.maximum(n - 1, 0), 0)),
            pl.BlockSpec((WINDOW, 2 * LANES), lambda n: (n, 0)),
        ],
        out_specs=pl.BlockSpec((WINDOW, D_MODEL), lambda n: (n, 0)),
        out_shape=jax.ShapeDtypeStruct((t, D_MODEL), BF16),
        compiler_params=_cparams(("parallel",)),
    )(sinks, q, kv, kv)


def _attn_bwd(q, kv, sinks, out, dout):
    t = q.shape[0]
    nb = t // WINDOW

    def body(sink_ref, q_ref, kvp_ref, kvc_ref, o_ref, do_ref, dq_ref, dkv_ref, dsink_ref, carry_ref):
        n = pl.program_id(0)

        @pl.when(n == 0)
        def _():
            carry_ref[...] = jnp.zeros_like(carry_ref)
            dsink_ref[...] = jnp.zeros_like(dsink_ref)

        @pl.when(n == nb)
        def _():
            dkv_ref[...] = carry_ref[...].astype(BF16)

        @pl.when(n < nb)
        def _():
            masks = _att_masks(n)
            lo = lax.broadcasted_iota(jnp.int32, (WINDOW, LANES), 1) < ATT_HD
            lane1 = lax.broadcasted_iota(jnp.int32, (1, LANES), 1)
            dsink = jnp.zeros((1, LANES), F32)
            halves = []
            for kh in range(ATT_KVH):
                k2p, klo_p, khi_p = _att_halves(kvp_ref[:, 0:LANES], lo, kh)
                k2c, klo_c, khi_c = _att_halves(kvc_ref[:, 0:LANES], lo, kh)
                v2p, _, _ = _att_halves(kvp_ref[:, LANES:2 * LANES], lo, kh)
                v2c, _, _ = _att_halves(kvc_ref[:, LANES:2 * LANES], lo, kh)
                acc = [jnp.zeros((WINDOW, LANES), F32) for _ in range(4)]
                for jj in range(ATT_GROUP // 2):
                    j = kh * (ATT_GROUP // 2) + jj
                    cols = slice(j * LANES, (j + 1) * LANES)
                    qp = q_ref[:, cols]
                    dop = do_ref[:, cols]
                    prod = dop.astype(F32) * o_ref[:, cols].astype(F32)
                    zero = jnp.zeros_like(qp)
                    dq_pair = None
                    for par in range(2):
                        hq = 2 * j + par
                        sel = lo if par == 0 else jnp.logical_not(lo)
                        qm = jnp.where(sel, qp, zero)
                        dom = jnp.where(sel, dop, zero)
                        pp, pc, ps = _att_probs(qm, k2p, k2c, masks, ALIBI_SLOPES[hq], sink_ref[hq])
                        delta = jnp.sum(jnp.where(sel, prod, 0.0), axis=-1, keepdims=True)
                        dsp = (pp * (_dot(dom, v2p, NT) - delta)).astype(BF16)
                        dsc = (pc * (_dot(dom, v2c, NT) - delta)).astype(BF16)
                        dsink = dsink + jnp.where(lane1 == hq, -jnp.sum(ps * delta, axis=0, keepdims=True), 0.0)
                        kp_, kc_ = (klo_p, klo_c) if par == 0 else (khi_p, khi_c)
                        part = _dot(dsp, kp_, NN) + _dot(dsc, kc_, NN)
                        dq_pair = part if dq_pair is None else dq_pair + part
                        acc[0] = acc[0] + _dot(dsp, qm, TN)
                        acc[1] = acc[1] + _dot(dsc, qm, TN)
                        acc[2] = acc[2] + _dot(pp.astype(BF16), dom, TN)
                        acc[3] = acc[3] + _dot(pc.astype(BF16), dom, TN)
                    dq_ref[:, cols] = (dq_pair * (ATT_HD ** -0.5)).astype(BF16)
                halves.append([a + pltpu.roll(a, ATT_HD, 1) for a in acc])
            scale = ATT_HD ** -0.5
            prev = jnp.concatenate(
                [jnp.where(lo, halves[0][0], halves[1][0]) * scale, jnp.where(lo, halves[0][2], halves[1][2])], axis=1)
            cur = jnp.concatenate(
                [jnp.where(lo, halves[0][1], halves[1][1]) * scale, jnp.where(lo, halves[0][3], halves[1][3])], axis=1)
            dkv_ref[...] = (carry_ref[...] + prev).astype(BF16)
            carry_ref[...] = cur
            dsink_ref[...] += dsink

    blk = lambda n: jnp.minimum(n, nb - 1)
    return pl.pallas_call(
        body,
        name="attn_bwd",
        grid=(nb + 1,),
        in_specs=[
            pl.BlockSpec(memory_space=pltpu.SMEM),
            pl.BlockSpec((WINDOW, D_MODEL), lambda n: (blk(n), 0)),
            pl.BlockSpec((WINDOW, 2 * LANES), lambda n: (jnp.maximum(blk(n) - 1, 0), 0)),
            pl.BlockSpec((WINDOW, 2 * LANES), lambda n: (blk(n), 0)),
            pl.BlockSpec((WINDOW, D_MODEL), lambda n: (blk(n), 0)),
            pl.BlockSpec((WINDOW, D_MODEL), lambda n: (blk(n), 0)),
        ],
        out_specs=[
            pl.BlockSpec((WINDOW, D_MODEL), lambda n: (blk(n), 0)),
            pl.BlockSpec((WINDOW, 2 * LANES), lambda n: (jnp.maximum(n - 1, 0), 0)),
            pl.BlockSpec((1, LANES), lambda n: (0, 0)),
        ],
        out_shape=[
            jax.ShapeDtypeStruct((t, D_MODEL), BF16),
            jax.ShapeDtypeStruct((t, 2 * LANES), BF16),
            jax.ShapeDtypeStruct((1, LANES), F32),
        ],
        scratch_shapes=[pltpu.VMEM((WINDOW, 2 * LANES), F32)],
        compiler_params=_cparams(("arbitrary",)),
    )(sinks, q, kv, kv, out, dout)


def _ffn_fwd(h, norm_g, w_up, conv_w, conv_b, w_down, tag):
    xn = _rms_fwd(h, norm_g, name=f"ffn{tag}_norm")
    up = _mm_nn(xn, w_up, name=f"ffn{tag}_up")
    act = _conv_fwd(up, conv_w, conv_b, name=f"ffn{tag}_conv")
    h_out = _mm_nn(act, w_down, res=h, name=f"ffn{tag}_down")
    return h_out, (xn, up, act)


def _ffn_bwd(dh, h, norm_g, w_up, conv_w, conv_b, w_down, saved, tag):
    xn, up, act = saved
    dw_down = _mm_tn(act, dh, 1, D_MODEL, name=f"ffn{tag}_dwdown")
    dact = _mm_nt(dh, w_down, name=f"ffn{tag}_dact")
    dup, dconv_w, dconv_b = _conv_bwd(up, conv_w, conv_b, dact, name=f"ffn{tag}_dconv")
    dw_up = _mm_tn(xn, dup, N_CHIPS, CONV_COLS, stacked=True, name=f"ffn{tag}_dwup")
    dxn = _mm_nt(dup, w_up, stacked=True, name=f"ffn{tag}_dxn")
    dh_in, dnorm = _rms_bwd(h, norm_g, dxn, dh, name=f"ffn{tag}_dnorm")
    return dh_in, dict(ffn_w_down=dw_down, ffn_w_up=dw_up, ffn_conv_w=dconv_w, ffn_conv_b=dconv_b, ffn_norm=dnorm)


def _local_step(x, target, w, fetch=lambda w, stage, after: w):
    xn0 = _rms_fwd(x, w["hg_norm"], name="hg_norm")
    proj = _mm_nn(xn0, w["hg_w_in"], name="hg_in")
    o, y, states = _hgrn_fwd(proj, w["hg_lb"], w["hg_out_norm"])
    w = fetch(w, "layer0", y)
    h_a = _mm_nn(y, w["hg_w_out"], res=x, name="hg_out")
    h1, ffn0 = _ffn_fwd(h_a, w["ffn_norm"][0], w["ffn_w_up"][0], w["ffn_conv_w"][0], w["ffn_conv_b"][0], w["ffn_w_down"][0], 0)
    w = fetch(w, "layer1", h1)
    kvn = _rms_fwd(h1, w["kv_norm"], name="kv_norm")
    kv = _mm_nn(kvn, w["w_kv"], out_dtype=BF16, name="kv_proj")
    xa = _rms_fwd(h1, w["attn_norm"], name="attn_norm")
    qa = _mm_nn(xa, w["attn_w_q"], out_dtype=BF16, name="attn_q")
    ao = _attn_fwd(qa, kv, w["attn_sinks"])
    h_b = _mm_nn(ao, w["attn_w_o"], res=h1, name="attn_o")
    h2, ffn1 = _ffn_fwd(h_b, w["ffn_norm"][1], w["ffn_w_up"][1], w["ffn_conv_w"][1], w["ffn_conv_b"][1], w["ffn_w_down"][1], 1)
    dh2, d_final, loss = _loss_head(h2, w["final_norm"], target)

    dh_b, g1 = _ffn_bwd(dh2, h_b, w["ffn_norm"][1], w["ffn_w_up"][1], w["ffn_conv_w"][1], w["ffn_conv_b"][1], w["ffn_w_down"][1], ffn1, 1)
    dw_o = _mm_tn(ao, dh_b, 1, D_MODEL, name="attn_dwo")
    dao = _mm_nt(dh_b, w["attn_w_o"], out_dtype=BF16, name="attn_dao")
    dqa, dkv, dsinks = _attn_bwd(qa, kv, w["attn_sinks"], ao, dao)
    dw_q = _mm_tn(xa, dqa, 1, D_MODEL, name="attn_dwq")
    dxa = _mm_nt(dqa, w["attn_w_q"], name="attn_dxa")
    dh1, d_attn_norm = _rms_bwd(h1, w["attn_norm"], dxa, dh_b, name="attn_dnorm")
    dw_kv = _mm_tn(kvn, dkv, 1, 2 * LANES, name="kv_dw")
    dkvn = _mm_nt(dkv, w["w_kv"], name="kv_dx")
    dh1, d_kv_norm = _rms_bwd(h1, w["kv_norm"], dkvn, dh1, name="kv_dnorm")
    dh_a, g0 = _ffn_bwd(dh1, h_a, w["ffn_norm"][0], w["ffn_w_up"][0], w["ffn_conv_w"][0], w["ffn_conv_b"][0], w["ffn_w_down"][0], ffn0, 0)
    dw_out = _mm_tn(y, dh_a, 1, D_MODEL, name="hg_dwout")
    dy = _mm_nt(dh_a, w["hg_w_out"], out_dtype=BF16, name="hg_dy")
    dproj, dlb, d_out_norm = _hgrn_bwd(proj, w["hg_lb"], w["hg_out_norm"], o, states, dy)
    dw_in = _mm_tn(xn0, dproj, N_CHIPS, D_MODEL, stacked=True, name="hg_dwin")
    dxn0 = _mm_nt(dproj, w["hg_w_in"], stacked=True, name="hg_dxn")
    dx, d_hg_norm = _rms_bwd(x, w["hg_norm"], dxn0, dh_a, name="hg_dnorm")

    grads = dict(
        hg_norm=d_hg_norm, hg_w_in=dw_in, hg_lb=dlb, hg_out_norm=d_out_norm, hg_w_out=dw_out,
        kv_norm=d_kv_norm, w_kv=dw_kv, attn_norm=d_attn_norm, attn_w_q=dw_q, attn_sinks=dsinks, attn_w_o=dw_o,
        final_norm=d_final,
    )
    for name in g0:
        grads[name] = [g0[name], g1[name]]
    return loss, dx, grads


ANY = pl.BlockSpec(memory_space=pl.ANY)


def _place():
    x, y, c = lax.axis_index("x"), lax.axis_index("y"), lax.axis_index("c")
    chips = [(1 - x, y), (x, 1 - y), (1 - x, 1 - y)]
    return x, y, c, chips


def _rcopy(src, dst, send_sem, recv_sem, to):
    return pltpu.make_async_remote_copy(src_ref=src, dst_ref=dst, send_sem=send_sem, recv_sem=recv_sem, device_id=to, device_id_type=MESH)


HBM = pl.BlockSpec(memory_space=pltpu.HBM)
SEM = pl.BlockSpec(memory_space=pltpu.SEMAPHORE)
EFFECT = pltpu.SideEffectType.DATAFLOW_SIDE_EFFECTING


def _in_hbm(a):
    return pltpu.with_memory_space_constraint(a, pltpu.HBM)


def _place_shard(shard, place, dtype, name):
    r, cols = shard.shape
    tr = _pick(r, ELEM_ROWS)

    def body(place_ref, s_ref, o_ref):
        o_ref[...] = s_ref[...].astype(o_ref.dtype)

    return pl.pallas_call(
        body,
        name=name,
        grid_spec=pltpu.PrefetchScalarGridSpec(
            num_scalar_prefetch=1,
            grid=(r // tr,),
            in_specs=[pl.BlockSpec((tr, cols), lambda i, place_ref: (i, 0))],
            out_specs=pl.BlockSpec((None, tr, cols), lambda i, place_ref: (place_ref[0], i, 0)),
        ),
        out_shape=jax.ShapeDtypeStruct((N_CHIPS, r, cols), dtype),
        compiler_params=_cparams(("parallel",)),
    )(place, shard)


def _gather_start(bufs):
    n = len(bufs)

    def body(*refs):
        ins = refs[:n]
        send_sems, recv_sems = refs[n], refs[n + 1]
        x, y, c, chips = _place()
        me = 2 * x + y
        for i in range(n):
            for j, (px, py) in enumerate(chips):
                _rcopy(ins[i].at[me], ins[i].at[me], send_sems.at[3 * i + j], recv_sems.at[3 * i + j], (px, py, c)).start()

    outs = pl.pallas_call(
        body,
        name="gather_start",
        in_specs=[HBM] * n,
        out_specs=[SEM, SEM] + [HBM] * n,
        out_shape=[pltpu.SemaphoreType.DMA((3 * n,)), pltpu.SemaphoreType.DMA((3 * n,))] + [pltpu.HBM(b.shape, b.dtype) for b in bufs],
        input_output_aliases={i: 2 + i for i in range(n)},
        compiler_params=pltpu.CompilerParams(has_side_effects=EFFECT),
    )(*[_in_hbm(b) for b in bufs])
    return outs[0], outs[1], list(outs[2:])


def _gather_wait(bufs, first, send_sems, recv_sems, after, name):
    n = len(bufs)

    def body(*refs):
        ins = refs[:n]
        send_sems, recv_sems = refs[n], refs[n + 1]
        x, y, c, chips = _place()
        me = 2 * x + y
        for i in range(n):
            for j, (px, py) in enumerate(chips):
                k = 3 * (first + i) + j
                cp = _rcopy(ins[i].at[me], ins[i].at[2 * px + py], send_sems.at[k], recv_sems.at[k], (px, py, c))
                cp.wait_send()
                cp.wait_recv()

    return pl.pallas_call(
        body,
        name=name,
        in_specs=[HBM] * n + [SEM, SEM, ANY],
        out_specs=[HBM] * n,
        out_shape=[pltpu.HBM(b.shape, b.dtype) for b in bufs],
        input_output_aliases={i: i for i in range(n)},
        compiler_params=pltpu.CompilerParams(has_side_effects=EFFECT),
    )(*bufs, send_sems, recv_sems, after)


def _allreduce_small(vec):
    rows = vec.shape[0]

    def body(v_ref, o_ref, buf, send_sems, recv_sems):
        x, y, c, _ = _place()
        me = 4 * x + 2 * y + c
        buf[me] = v_ref[...]
        copies = []
        for k in range(1, N_DEV):
            peer = (x ^ (k >> 2), y ^ ((k >> 1) & 1), c ^ (k & 1))
            cp = _rcopy(v_ref, buf.at[me], send_sems.at[k - 1], recv_sems.at[k - 1], peer)
            cp.start()
            copies.append(cp)
        for cp in copies:
            cp.wait()
        acc = buf[0]
        for d in range(1, N_DEV):
            acc = acc + buf[d]
        o_ref[...] = acc

    return pl.pallas_call(
        body,
        name="allreduce_small",
        in_specs=[pl.BlockSpec(memory_space=pltpu.VMEM)],
        out_specs=pl.BlockSpec(memory_space=pltpu.VMEM),
        out_shape=jax.ShapeDtypeStruct(vec.shape, F32),
        scratch_shapes=[pltpu.VMEM((N_DEV, rows, LANES), F32), pltpu.SemaphoreType.DMA((N_DEV - 1,)), pltpu.SemaphoreType.DMA((N_DEV - 1,))],
        compiler_params=pltpu.CompilerParams(vmem_limit_bytes=VMEM_LIMIT_BYTES),
    )(vec)


def _swap_halves(grads):
    n = len(grads)

    def body(*refs):
        ins, outs = refs[:n], refs[n:2 * n]
        send_sems, recv_sems = refs[2 * n:]
        x, y, c, _ = _place()
        copies = []
        for i in range(n):
            h = ins[i].shape[1] // 2
            cp = _rcopy(ins[i].at[:, pl.ds((1 - c) * h, h)], outs[i], send_sems.at[i], recv_sems.at[i], (x, y, 1 - c))
            cp.start()
            copies.append(cp)
        for cp in copies:
            cp.wait()

    return pl.pallas_call(
        body,
        name="rs_swap_halves",
        in_specs=[ANY] * n,
        out_specs=[ANY] * n,
        out_shape=[jax.ShapeDtypeStruct((N_CHIPS, g.shape[1] // 2, g.shape[2]), F32) for g in grads],
        scratch_shapes=[pltpu.SemaphoreType.DMA((n,)), pltpu.SemaphoreType.DMA((n,))],
    )(*grads)


def _send_partials(parts):
    n = len(parts)

    def body(*refs):
        ins, outs = refs[:n], refs[n:2 * n]
        send_sems, recv_sems = refs[2 * n:]
        x, y, c, chips = _place()
        copies = []
        for i in range(n):
            for j, (px, py) in enumerate(chips):
                cp = _rcopy(ins[i].at[2 * px + py], outs[i].at[j], send_sems.at[i, j], recv_sems.at[i, j], (px, py, c))
                cp.start()
                copies.append(cp)
        for cp in copies:
            cp.wait()

    return pl.pallas_call(
        body,
        name="rs_send_partials",
        in_specs=[ANY] * n,
        out_specs=[ANY] * n,
        out_shape=[jax.ShapeDtypeStruct((3,) + p.shape[1:], p.dtype) for p in parts],
        scratch_shapes=[pltpu.SemaphoreType.DMA((n, 3)), pltpu.SemaphoreType.DMA((n, 3))],
    )(*parts)


def _share_halves(bufs):
    n = len(bufs)

    def body(*refs):
        outs = refs[n:2 * n]
        send_sems, recv_sems = refs[2 * n:]
        x, y, c, _ = _place()
        copies = []
        for i in range(n):
            cp = _rcopy(outs[i].at[c], outs[i].at[c], send_sems.at[i], recv_sems.at[i], (x, y, 1 - c))
            cp.start()
            copies.append(cp)
        for cp in copies:
            cp.wait()

    return pl.pallas_call(
        body,
        name="rs_share_halves",
        in_specs=[ANY] * n,
        out_specs=[ANY] * n,
        out_shape=[jax.ShapeDtypeStruct(b.shape, F32) for b in bufs],
        input_output_aliases={i: i for i in range(n)},
        scratch_shapes=[pltpu.SemaphoreType.DMA((n,)), pltpu.SemaphoreType.DMA((n,))],
    )(*bufs)


ELEM_ROWS = (256, 176, 128, 64, 32, 16, 8)


def _add_core_halves(grad, got, c, name):
    s, r, cols = grad.shape
    h = r // 2
    tr = _pick(h, ELEM_ROWS)

    def body(c_ref, g_ref, o_ref, f_ref, b_ref):
        acc = g_ref[...] + o_ref[...]
        f_ref[...] = acc
        b_ref[...] = acc.astype(BF16)

    blk = pl.BlockSpec((None, tr, cols), lambda k, i, c_ref: (k, i, 0))
    return pl.pallas_call(
        body,
        name=name,
        grid_spec=pltpu.PrefetchScalarGridSpec(
            num_scalar_prefetch=1,
            grid=(s, h // tr),
            in_specs=[pl.BlockSpec((None, None, tr, cols), lambda k, i, c_ref: (k, c_ref[0], i, 0)), blk],
            out_specs=[blk, blk],
        ),
        out_shape=[jax.ShapeDtypeStruct((s, h, cols), F32), jax.ShapeDtypeStruct((s, h, cols), BF16)],
        compiler_params=_cparams(("parallel", "parallel")),
    )(c, grad.reshape(s, 2, h, cols), got)


def _add_chip_partials(mine, got, place, name):
    _, h, cols = mine.shape
    tr = _pick(h, ELEM_ROWS)

    def body(place_ref, m_ref, g_ref, o_ref):
        acc = m_ref[...]
        for j in range(3):
            acc = acc + g_ref[j].astype(F32)
        o_ref[...] = acc

    return pl.pallas_call(
        body,
        name=name,
        grid_spec=pltpu.PrefetchScalarGridSpec(
            num_scalar_prefetch=1,
            grid=(h // tr,),
            in_specs=[
                pl.BlockSpec((None, tr, cols), lambda i, place_ref: (place_ref[0], i, 0)),
                pl.BlockSpec((3, tr, cols), lambda i, place_ref: (0, i, 0)),
            ],
            out_specs=pl.BlockSpec((None, tr, cols), lambda i, place_ref: (place_ref[1], i, 0)),
        ),
        out_shape=jax.ShapeDtypeStruct((2, h, cols), F32),
        compiler_params=_cparams(("parallel",)),
    )(place, mine, got)


def _adamw(w, m, v, g, name):
    r, cols = w.shape
    tr = _pick(r, ELEM_ROWS)
    c1 = 1.0 / (1.0 - ADAM_B1 ** ADAM_STEP)
    c2 = 1.0 / (1.0 - ADAM_B2 ** ADAM_STEP)

    def body(w_ref, m_ref, v_ref, g_ref, d_ref, nm_ref, nv_ref):
        gv = g_ref[...]
        nm = ADAM_B1 * m_ref[...] + (1.0 - ADAM_B1) * gv
        nv = ADAM_B2 * v_ref[...] + (1.0 - ADAM_B2) * (gv * gv)
        d_ref[...] = -ADAM_LR * ((nm * c1) / (jnp.sqrt(nv * c2) + ADAM_EPS) + ADAM_WD * w_ref[...])
        nm_ref[...] = nm
        nv_ref[...] = nv

    blk = pl.BlockSpec((tr, cols), lambda i: (i, 0))
    return pl.pallas_call(
        body,
        name=name,
        grid=(r // tr,),
        in_specs=[blk] * 4,
        out_specs=[blk] * 3,
        out_shape=[jax.ShapeDtypeStruct((r, cols), F32)] * 3,
        compiler_params=_cparams(("parallel",)),
    )(w, m, v, g)


SMALL_COLS = 384
SMALL_ROWS = 16


def _pad_rows(flat, rows, cols):
    return jnp.pad(flat, (0, rows * cols - flat.shape[0])).reshape(rows, cols)


def kernel(x, hg_norm, hg_w_in, hg_lb_logits, hg_out_norm, hg_w_out, kv_norm, w_kv, attn_norm, attn_w_q, attn_sinks, attn_w_o, ffn_norm, ffn_w_up, ffn_conv_w, ffn_conv_b, ffn_w_down, final_norm, loss_target, m_hg_norm, m_hg_w_in, m_hg_lb_logits, m_hg_out_norm, m_hg_w_out, m_kv_norm, m_w_kv, m_attn_norm, m_attn_w_q, m_attn_sinks, m_attn_w_o, m_ffn_norm, m_ffn_w_up, m_ffn_conv_w, m_ffn_conv_b, m_ffn_w_down, m_final_norm, v_hg_norm, v_hg_w_in, v_hg_lb_logits, v_hg_out_norm, v_hg_w_out, v_kv_norm, v_w_kv, v_attn_norm, v_attn_w_q, v_attn_sinks, v_attn_w_o, v_ffn_norm, v_ffn_w_up, v_ffn_conv_w, v_ffn_conv_b, v_ffn_w_down, v_final_norm):
    wts = dict(hg_norm=hg_norm, hg_w_in=hg_w_in, hg_lb_logits=hg_lb_logits, hg_out_norm=hg_out_norm, hg_w_out=hg_w_out, kv_norm=kv_norm, w_kv=w_kv, attn_norm=attn_norm, attn_w_q=attn_w_q, attn_sinks=attn_sinks, attn_w_o=attn_w_o, ffn_norm=ffn_norm, ffn_w_up=ffn_w_up, ffn_conv_w=ffn_conv_w, ffn_conv_b=ffn_conv_b, ffn_w_down=ffn_w_down, final_norm=final_norm)
    mom1 = dict(hg_norm=m_hg_norm, hg_w_in=m_hg_w_in, hg_lb_logits=m_hg_lb_logits, hg_out_norm=m_hg_out_norm, hg_w_out=m_hg_w_out, kv_norm=m_kv_norm, w_kv=m_w_kv, attn_norm=m_attn_norm, attn_w_q=m_attn_w_q, attn_sinks=m_attn_sinks, attn_w_o=m_attn_w_o, ffn_norm=m_ffn_norm, ffn_w_up=m_ffn_w_up, ffn_conv_w=m_ffn_conv_w, ffn_conv_b=m_ffn_conv_b, ffn_w_down=m_ffn_w_down, final_norm=m_final_norm)
    mom2 = dict(hg_norm=v_hg_norm, hg_w_in=v_hg_w_in, hg_lb_logits=v_hg_lb_logits, hg_out_norm=v_hg_out_norm, hg_w_out=v_hg_w_out, kv_norm=v_kv_norm, w_kv=v_w_kv, attn_norm=v_attn_norm, attn_w_q=v_attn_w_q, attn_sinks=v_attn_sinks, attn_w_o=v_attn_w_o, ffn_norm=v_ffn_norm, ffn_w_up=v_ffn_w_up, ffn_conv_w=v_ffn_conv_w, ffn_conv_b=v_ffn_conv_b, ffn_w_down=v_ffn_w_down, final_norm=v_final_norm)
    names = list(wts)
    chip = 2 * lax.axis_index("x") + lax.axis_index("y")
    core = lax.axis_index("c")
    core_arr = jnp.reshape(core, (1,)).astype(jnp.int32)
    fs = D_FF // N_CHIPS
    ds = D_MODEL // N_CHIPS

    place_arr = jnp.stack([chip, core]).astype(jnp.int32)
    small = jnp.concatenate([hg_norm.reshape(-1), hg_lb_logits.reshape(-1), ffn_conv_w.reshape(-1)])
    n_small = small.shape[0]
    shards = [
        ("small", _pad_rows(small, SMALL_ROWS, SMALL_COLS), F32), ("hg_w_in", hg_w_in[0], BF16),
        ("hg_w_out", hg_w_out[0], BF16), ("ffn_w_up0", ffn_w_up[0], BF16), ("ffn_w_down0", ffn_w_down[0], BF16),
        ("w_kv", w_kv, BF16), ("attn_w_q", attn_w_q[0], BF16), ("attn_w_o", attn_w_o[0], BF16),
        ("ffn_w_up1", ffn_w_up[1], BF16), ("ffn_w_down1", ffn_w_down[1], BF16),
    ]
    stages = dict(first=(0, 2), layer0=(2, 5), layer1=(5, 10))
    send_sems, recv_sems, bufs = _gather_start([_place_shard(s, place_arr, dt, name=f"place_{nm}") for nm, s, dt in shards])

    def fetch(w, stage, after):
        lo, hi = stages[stage]
        got = _gather_wait(bufs[lo:hi], lo, send_sems, recv_sems, after, name=f"gather_wait_{stage}")
        w = dict(w)
        if stage == "first":
            g_small = got[0].reshape(N_CHIPS, -1)[:, :n_small]
            conv_w = g_small[:, 3 * ds:].reshape(N_CHIPS, 2, 3, fs).transpose(1, 2, 0, 3).reshape(2, 3, D_FF)
            w.update(
                hg_norm=g_small[:, :ds].reshape(1, D_MODEL),
                hg_lb=g_small[:, ds:3 * ds].reshape(N_CHIPS, 2, ds).transpose(1, 0, 2).reshape(2, D_MODEL),
                ffn_conv_w=[conv_w[0], conv_w[1]], hg_w_in=got[1],
            )
        elif stage == "layer0":
            w.update(hg_w_out=got[0].reshape(1, D_MODEL, D_MODEL), ffn_w_up=[got[1], None], ffn_w_down=[got[2].reshape(1, D_FF, D_MODEL), None])
        else:
            w.update(
                w_kv=got[0].reshape(1, D_MODEL, 2 * LANES), attn_w_q=got[1].reshape(1, D_MODEL, D_MODEL),
                attn_w_o=got[2].reshape(1, D_MODEL, D_MODEL), ffn_w_up=[w["ffn_w_up"][0], got[3]],
                ffn_w_down=[w["ffn_w_down"][0], got[4].reshape(1, D_FF, D_MODEL)],
            )
        return w

    whole = dict(
        hg_out_norm=hg_out_norm, kv_norm=kv_norm.reshape(1, D_MODEL), attn_norm=attn_norm, attn_sinks=attn_sinks.reshape(ATT_QH),
        ffn_norm=[ffn_norm[0:1], ffn_norm[1:2]], ffn_conv_b=[ffn_conv_b[0:1], ffn_conv_b[1:2]], final_norm=final_norm.reshape(1, D_MODEL),
    )
    whole = fetch(whole, "first", place_arr)

    loss, dx, grads = _local_step(x[0], loss_target[0], whole, fetch)

    small_parts = [
        loss.reshape(-1), grads["hg_out_norm"].reshape(-1), grads["attn_sinks"].reshape(-1), grads["kv_norm"].reshape(-1),
        grads["attn_norm"].reshape(-1), grads["ffn_norm"][0].reshape(-1), grads["ffn_norm"][1].reshape(-1),
        grads["ffn_conv_b"][0].reshape(-1), grads["ffn_conv_b"][1].reshape(-1), grads["final_norm"].reshape(-1),
        grads["hg_norm"].reshape(-1), grads["hg_lb"].reshape(-1), grads["ffn_conv_w"][0].reshape(-1), grads["ffn_conv_w"][1].reshape(-1),
    ]
    sizes = [p.shape[0] for p in small_parts]
    flat = jnp.concatenate(small_parts)
    rows = -(-flat.shape[0] // (SUBLANES * LANES)) * SUBLANES
    summed = _allreduce_small(_pad_rows(flat, rows, LANES)).reshape(-1)
    offs = [0]
    for sz in sizes:
        offs.append(offs[-1] + sz)
    sm = [summed[offs[i]:offs[i + 1]] for i in range(len(sizes))]
    loss_out = sm[0][0]
    conv_w_full = jnp.stack([sm[12].reshape(3, D_FF), sm[13].reshape(3, D_FF)])
    small_grads = dict(
        hg_out_norm=sm[1].reshape(1, HG_DK), attn_sinks=sm[2][:ATT_QH].reshape(1, ATT_QH), kv_norm=sm[3], attn_norm=sm[4].reshape(1, D_MODEL),
        ffn_norm=jnp.stack([sm[5], sm[6]]), ffn_conv_b=jnp.stack([sm[7], sm[8]]), final_norm=sm[9],
        hg_norm=lax.dynamic_slice(sm[10].reshape(1, D_MODEL), (0, chip * ds), (1, ds)),
        hg_lb_logits=lax.dynamic_slice(sm[11].reshape(2, D_MODEL), (0, chip * ds), (2, ds)),
        ffn_conv_w=lax.dynamic_slice(conv_w_full, (0, 0, chip * fs), (2, 3, fs)),
    )

    big_names = ["hg_w_in", "hg_w_out", "w_kv", "attn_w_q", "attn_w_o", "ffn_w_up", "ffn_w_up", "ffn_w_down", "ffn_w_down"]
    stacked = [
        grads["hg_w_in"], grads["hg_w_out"].reshape(N_CHIPS, ds, D_MODEL), grads["w_kv"].reshape(N_CHIPS, ds, 2 * LANES),
        grads["attn_w_q"].reshape(N_CHIPS, ds, D_MODEL), grads["attn_w_o"].reshape(N_CHIPS, ds, D_MODEL),
        grads["ffn_w_up"][0], grads["ffn_w_up"][1],
        grads["ffn_w_down"][0].reshape(N_CHIPS, fs, D_MODEL), grads["ffn_w_down"][1].reshape(N_CHIPS, fs, D_MODEL),
    ]
    from_core = _swap_halves(stacked)
    sums = [_add_core_halves(g, o, core_arr, name=f"rs_add_core_{i}") for i, (g, o) in enumerate(zip(stacked, from_core))]
    from_chips = _send_partials([b for _, b in sums])
    halves = [_add_chip_partials(f, o, place_arr, name=f"rs_add_chip_{i}") for i, ((f, _), o) in enumerate(zip(sums, from_chips))]
    reduced = [r.reshape((-1,) + r.shape[2:]) for r in _share_halves(halves)]
    big_grads = dict(
        hg_w_in=reduced[0], hg_w_out=reduced[1], w_kv=reduced[2], attn_w_q=reduced[3], attn_w_o=reduced[4],
        ffn_w_up=jnp.concatenate([reduced[5], reduced[6]]), ffn_w_down=jnp.concatenate([reduced[7], reduced[8]]),
    )

    out_g, out_d, out_m, out_v = {}, {}, {}, {}
    for name in big_grads:
        g2 = big_grads[name]
        shape = wts[name].shape
        d2, m2, v2 = _adamw(wts[name].reshape(g2.shape), mom1[name].reshape(g2.shape), mom2[name].reshape(g2.shape), g2, name=f"adamw_{name}")
        out_g[name], out_d[name], out_m[name], out_v[name] = g2.reshape(shape), d2.reshape(shape), m2.reshape(shape), v2.reshape(shape)
    small_names = [n for n in names if n not in big_grads]
    cat = lambda d: jnp.concatenate([d[n].reshape(-1) for n in small_names])
    n_flat = sum(wts[n].size for n in small_names)
    srows = -(-n_flat // (SUBLANES * LANES)) * SUBLANES
    packed = [_pad_rows(cat(d), srows, LANES) for d in (wts, mom1, mom2, small_grads)]
    d_s, m_s, v_s = _adamw(*packed, name="adamw_small")
    off = 0
    for n in small_names:
        sz, shape = wts[n].size, wts[n].shape
        out_g[n] = small_grads[n].reshape(shape)
        out_d[n] = d_s.reshape(-1)[off:off + sz].reshape(shape)
        out_m[n] = m_s.reshape(-1)[off:off + sz].reshape(shape)
        out_v[n] = v_s.reshape(-1)[off:off + sz].reshape(shape)
        off += sz

    grad_x = dx.reshape(x.shape)
    return (loss_out, grad_x, *[out_g[n] for n in names], *[out_d[n] for n in names], *[out_m[n] for n in names], *[out_v[n] for n in names])
```

```python
import functools

import jax
import jax.numpy as jnp
from jax import lax
from jax.experimental import pallas as pl
from jax.experimental.pallas import tpu as pltpu

F32 = jnp.float32
BF16 = jnp.bfloat16
MESH = pl.DeviceIdType.MESH

EPS = 1e-6
D_MODEL = 1024
HG_HEADS = 8
HG_DK = 128
HG_CHUNK = 64
ATT_HD = 64
ATT_QH = 16
ATT_KVH = 2
ATT_GROUP = ATT_QH // ATT_KVH
WINDOW = 128
D_FF = 2816
N_CHIPS = 4
N_DEV = 8
LANES = 128
SUBLANES = 8
VMEM_LIMIT_BYTES = 56 * 1024 * 1024
NEG = -1e30
ALIBI_SLOPES = tuple(2.0 ** (-8.0 * h / ATT_QH) for h in range(1, ATT_QH + 1))

ADAM_LR = 0.001
ADAM_B1 = 0.9
ADAM_B2 = 0.999
ADAM_EPS = 1e-08
ADAM_WD = 0.01
ADAM_STEP = 10


def _cparams(sem=None):
    return pltpu.CompilerParams(dimension_semantics=sem, vmem_limit_bytes=VMEM_LIMIT_BYTES)


def _pick(n, cands):
    for c in cands:
        if n % c == 0:
            return c
    return n


def _sigmoid(x):
    return 1.0 / (1.0 + jnp.exp(-x))


def _dot(a, b, dims):
    return lax.dot_general(a, b, (dims, ((), ())), preferred_element_type=F32)


NN = ((1,), (0,))
NT = ((1,), (1,))
TN = ((0,), (0,))


def _mm_nn(a, w, res=None, out_dtype=F32, name="mm_nn"):
    m, k = a.shape
    s, _, ns = w.shape
    tm = min(m, 512)
    tn = _pick(ns, (512, 1408, 256, 128))
    npb = ns // tn

    def body(a_ref, w_ref, *rest):
        o_ref = rest[-1]
        acc = _dot(a_ref[...].astype(BF16), w_ref[...], NN)
        if res is not None:
            acc = acc + rest[0][...]
        o_ref[...] = acc.astype(o_ref.dtype)

    in_specs = [
        pl.BlockSpec((tm, k), lambda i, j: (i, 0)),
        pl.BlockSpec((None, k, tn), lambda i, j: (j // npb, 0, j % npb)),
    ]
    args = [a, w]
    if res is not None:
        in_specs.append(pl.BlockSpec((tm, tn), lambda i, j: (i, j)))
        args.append(res)
    return pl.pallas_call(
        body,
        name=name,
        grid=(m // tm, s * npb),
        in_specs=in_specs,
        out_specs=pl.BlockSpec((tm, tn), lambda i, j: (i, j)),
        out_shape=jax.ShapeDtypeStruct((m, s * ns), out_dtype),
        compiler_params=_cparams(("parallel", "parallel")),
    )(*args)


def _dy_spec(stacked, tm, tn, npb, row, kk):
    if stacked:
        return pl.BlockSpec((None, tm, tn), lambda *g: (kk(g) // npb, row(g), kk(g) % npb))
    return pl.BlockSpec((tm, tn), lambda *g: (row(g), kk(g)))


def _mm_nt(dy, w, stacked=False, out_dtype=F32, name="mm_nt"):
    s, k, ns = w.shape
    m = dy.shape[1] if stacked else dy.shape[0]
    tm = min(m, 512)
    tko = _pick(k, (1024, 1408, 512, 256))
    tn = _pick(ns, (1024, 1408, 512, 256))
    npb = ns // tn
    nk = s * npb

    def body(dy_ref, w_ref, o_ref, acc_ref):
        kk = pl.program_id(2)

        @pl.when(kk == 0)
        def _():
            acc_ref[...] = jnp.zeros_like(acc_ref)

        acc_ref[...] += _dot(dy_ref[...].astype(BF16), w_ref[...], NT)

        @pl.when(kk == nk - 1)
        def _():
            o_ref[...] = acc_ref[...].astype(o_ref.dtype)

    return pl.pallas_call(
        body,
        name=name,
        grid=(m // tm, k // tko, nk),
        in_specs=[
            _dy_spec(stacked, tm, tn, npb, lambda g: g[0], lambda g: g[2]),
            pl.BlockSpec((None, tko, tn), lambda i, j, kk: (kk // npb, j, kk % npb)),
        ],
        out_specs=pl.BlockSpec((tm, tko), lambda i, j, kk: (i, j)),
        out_shape=jax.ShapeDtypeStruct((m, k), out_dtype),
        scratch_shapes=[pltpu.VMEM((tm, tko), F32)],
        compiler_params=_cparams(("parallel", "parallel", "arbitrary")),
    )(dy, w)


def _mm_tn(a, dy, s, ns, stacked=False, name="mm_tn"):
    m, k = a.shape
    tm = min(m, 512)
    tk = _pick(k, (1024, 1408, 512, 256))
    tn = _pick(ns, (512, 1408, 256, 128))
    npb = ns // tn
    nm = m // tm

    def body(a_ref, dy_ref, o_ref, acc_ref):
        mm = pl.program_id(2)

        @pl.when(mm == 0)
        def _():
            acc_ref[...] = jnp.zeros_like(acc_ref)

        acc_ref[...] += _dot(a_ref[...].astype(BF16), dy_ref[...].astype(BF16), TN)

        @pl.when(mm == nm - 1)
        def _():
            o_ref[...] = acc_ref[...]

    return pl.pallas_call(
        body,
        name=name,
        grid=(k // tk, s * npb, nm),
        in_specs=[
            pl.BlockSpec((tm, tk), lambda i, j, mm: (mm, i)),
            _dy_spec(stacked, tm, tn, npb, lambda g: g[2], lambda g: g[1]),
        ],
        out_specs=pl.BlockSpec((None, tk, tn), lambda i, j, mm: (j // npb, i, j % npb)),
        out_shape=jax.ShapeDtypeStruct((s, k, ns), F32),
        scratch_shapes=[pltpu.VMEM((tk, tn), F32)],
        compiler_params=_cparams(("parallel", "parallel", "arbitrary")),
    )(a, dy)


ROW_TILE = 256


def _rms_fwd(x, g, name="rms_fwd"):
    t, d = x.shape
    r = min(t, ROW_TILE)

    def body(x_ref, g_ref, o_ref):
        xv = x_ref[...]
        rstd = lax.rsqrt(jnp.mean(xv * xv, axis=-1, keepdims=True) + EPS)
        o_ref[...] = (xv * rstd * g_ref[...]).astype(BF16)

    return pl.pallas_call(
        body,
        name=name,
        grid=(t // r,),
        in_specs=[pl.BlockSpec((r, d), lambda i: (i, 0)), pl.BlockSpec((1, d), lambda i: (0, 0))],
        out_specs=pl.BlockSpec((r, d), lambda i: (i, 0)),
        out_shape=jax.ShapeDtypeStruct((t, d), BF16),
        compiler_params=_cparams(("parallel",)),
    )(x, g)


def _rms_bwd(x, g, dxn, dres, name="rms_bwd"):
    t, d = x.shape
    r = min(t, ROW_TILE)

    def body(x_ref, g_ref, dxn_ref, dres_ref, dx_ref, dg_ref):
        @pl.when(pl.program_id(0) == 0)
        def _():
            dg_ref[...] = jnp.zeros_like(dg_ref)

        xv = x_ref[...]
        rstd = lax.rsqrt(jnp.mean(xv * xv, axis=-1, keepdims=True) + EPS)
        xhat = xv * rstd
        dxn_v = dxn_ref[...].astype(F32)
        gd = dxn_v * g_ref[...]
        dx_ref[...] = dres_ref[...] + rstd * (gd - xhat * jnp.mean(gd * xhat, axis=-1, keepdims=True))
        dg_ref[...] += jnp.sum(dxn_v * xhat, axis=0, keepdims=True)

    return pl.pallas_call(
        body,
        name=name,
        grid=(t // r,),
        in_specs=[
            pl.BlockSpec((r, d), lambda i: (i, 0)),
            pl.BlockSpec((1, d), lambda i: (0, 0)),
            pl.BlockSpec((r, d), lambda i: (i, 0)),
            pl.BlockSpec((r, d), lambda i: (i, 0)),
        ],
        out_specs=[pl.BlockSpec((r, d), lambda i: (i, 0)), pl.BlockSpec((1, d), lambda i: (0, 0))],
        out_shape=[jax.ShapeDtypeStruct((t, d), F32), jax.ShapeDtypeStruct((1, d), F32)],
        compiler_params=_cparams(("arbitrary",)),
    )(x, g, dxn, dres)


def _loss_head(h, g, target):
    t, d = h.shape
    r = min(t, ROW_TILE)

    def body(h_ref, g_ref, t_ref, dh_ref, dg_ref, loss_ref):
        @pl.when(pl.program_id(0) == 0)
        def _():
            dg_ref[...] = jnp.zeros_like(dg_ref)
            loss_ref[...] = jnp.zeros_like(loss_ref)

        xv = h_ref[...]
        rstd = lax.rsqrt(jnp.mean(xv * xv, axis=-1, keepdims=True) + EPS)
        xhat = xv * rstd
        gv = g_ref[...]
        err = xhat * gv - t_ref[...]
        loss_ref[...] += 0.5 * jnp.sum(jnp.mean(err * err, axis=-1, keepdims=True), axis=0, keepdims=True)
        dy = err * (1.0 / d)
        gd = dy * gv
        dh_ref[...] = rstd * (gd - xhat * jnp.mean(gd * xhat, axis=-1, keepdims=True))
        dg_ref[...] += jnp.sum(dy * xhat, axis=0, keepdims=True)

    return pl.pallas_call(
        body,
        name="loss_head",
        grid=(t // r,),
        in_specs=[
            pl.BlockSpec((r, d), lambda i: (i, 0)),
            pl.BlockSpec((1, d), lambda i: (0, 0)),
            pl.BlockSpec((r, d), lambda i: (i, 0)),
        ],
        out_specs=[
            pl.BlockSpec((r, d), lambda i: (i, 0)),
            pl.BlockSpec((1, d), lambda i: (0, 0)),
            pl.BlockSpec((1, LANES), lambda i: (0, 0)),
        ],
        out_shape=[
            jax.ShapeDtypeStruct((t, d), F32),
            jax.ShapeDtypeStruct((1, d), F32),
            jax.ShapeDtypeStruct((1, LANES), F32),
        ],
        compiler_params=_cparams(("arbitrary",)),
    )(h, g, target)


CONV_ROWS = 128
CONV_COLS = 1408


def _conv_taps(x_ext, n):
    tot = x_ext.shape[0]
    g1 = pltpu.roll(x_ext, 1, 0)[tot - n:]
    g2 = pltpu.roll(x_ext, 2, 0)[tot - n:]
    return g2, g1


def _conv_fwd(up, conv_w, conv_b, name="conv_fwd"):
    t = up.shape[0]
    r = min(t, CONV_ROWS)
    tc = CONV_COLS
    ncb = D_FF // tc
    hb = r // SUBLANES

    def body(g_ref, halo_ref, v_ref, w_ref, b_ref, o_ref):
        i = pl.program_id(1)
        g0 = g_ref[...]
        halo = halo_ref[...] * jnp.where(i > 0, 1.0, 0.0)
        g2, g1 = _conv_taps(jnp.concatenate([halo, g0], axis=0), r)
        c = b_ref[...] + w_ref[0:1, :] * g2 + w_ref[1:2, :] * g1 + w_ref[2:3, :] * g0
        o_ref[...] = (c * _sigmoid(c) * v_ref[...]).astype(BF16)

    return pl.pallas_call(
        body,
        name=name,
        grid=(ncb, t // r),
        in_specs=[
            pl.BlockSpec((r, tc), lambda j, i: (i, j)),
            pl.BlockSpec((SUBLANES, tc), lambda j, i: (jnp.maximum(i * hb - 1, 0), j)),
            pl.BlockSpec((r, tc), lambda j, i: (i, ncb + j)),
            pl.BlockSpec((3, tc), lambda j, i: (0, j)),
            pl.BlockSpec((1, tc), lambda j, i: (0, j)),
        ],
        out_specs=pl.BlockSpec((r, tc), lambda j, i: (i, j)),
        out_shape=jax.ShapeDtypeStruct((t, D_FF), BF16),
        compiler_params=_cparams(("parallel", "parallel")),
    )(up, up, up, conv_w, conv_b)


def _conv_bwd(up, conv_w, conv_b, dact, name="conv_bwd"):
    t = up.shape[0]
    r = min(t, CONV_ROWS)
    tc = CONV_COLS
    ncb = D_FF // tc
    hb = r // SUBLANES
    nrt = t // r

    def body(g_ref, halo_ref, v_ref, w_ref, b_ref, da_ref, dup_ref, dw_ref, db_ref, nxt_ref):
        ii = pl.program_id(1)
        i = nrt - 1 - ii

        @pl.when(ii == 0)
        def _():
            nxt_ref[...] = jnp.zeros_like(nxt_ref)
            dw_ref[...] = jnp.zeros_like(dw_ref)
            db_ref[...] = jnp.zeros_like(db_ref)

        g0 = g_ref[...]
        halo = halo_ref[...] * jnp.where(i > 0, 1.0, 0.0)
        g2, g1 = _conv_taps(jnp.concatenate([halo, g0], axis=0), r)
        w0, w1, w2 = w_ref[0:1, :], w_ref[1:2, :], w_ref[2:3, :]
        c = b_ref[...] + w0 * g2 + w1 * g1 + w2 * g0
        sg = _sigmoid(c)
        da = da_ref[...]
        dval = da * (c * sg)
        dc = da * v_ref[...] * (sg * (1.0 + c * (1.0 - sg)))
        db_ref[...] += jnp.sum(dc, axis=0, keepdims=True)
        dw_ref[0:1, :] += jnp.sum(dc * g2, axis=0, keepdims=True)
        dw_ref[1:2, :] += jnp.sum(dc * g1, axis=0, keepdims=True)
        dw_ref[2:3, :] += jnp.sum(dc * g0, axis=0, keepdims=True)
        ext = jnp.concatenate([dc, nxt_ref[...]], axis=0)
        tot = r + SUBLANES
        d1 = pltpu.roll(ext, tot - 1, 0)[:r]
        d2 = pltpu.roll(ext, tot - 2, 0)[:r]
        dgate = w2 * dc + w1 * d1 + w0 * d2
        nxt_ref[...] = dc[:SUBLANES]
        dup_ref[0] = dgate.astype(BF16)
        dup_ref[1] = dval.astype(BF16)

    rev = lambda ii: nrt - 1 - ii
    dup, dw, db = pl.pallas_call(
        body,
        name=name,
        grid=(ncb, nrt),
        in_specs=[
            pl.BlockSpec((r, tc), lambda j, ii: (rev(ii), j)),
            pl.BlockSpec((SUBLANES, tc), lambda j, ii: (jnp.maximum(rev(ii) * hb - 1, 0), j)),
            pl.BlockSpec((r, tc), lambda j, ii: (rev(ii), ncb + j)),
            pl.BlockSpec((3, tc), lambda j, ii: (0, j)),
            pl.BlockSpec((1, tc), lambda j, ii: (0, j)),
            pl.BlockSpec((r, tc), lambda j, ii: (rev(ii), j)),
        ],
        out_specs=[
            pl.BlockSpec((2, None, r, tc), lambda j, ii: (0, j, rev(ii), 0)),
            pl.BlockSpec((3, tc), lambda j, ii: (0, j)),
            pl.BlockSpec((1, tc), lambda j, ii: (0, j)),
        ],
        out_shape=[
            jax.ShapeDtypeStruct((2, ncb, t, tc), BF16),
            jax.ShapeDtypeStruct((3, D_FF), F32),
            jax.ShapeDtypeStruct((1, D_FF), F32),
        ],
        scratch_shapes=[pltpu.VMEM((SUBLANES, tc), F32)],
        compiler_params=_cparams(("parallel", "arbitrary")),
    )(up, up, up, conv_w, conv_b, dact)
    return dup.reshape(2 * ncb, t, tc), dw, db


def _split3(x):
    x1 = x.astype(BF16)
    r1 = x - x1.astype(F32)
    x2 = r1.astype(BF16)
    x3 = (r1 - x2.astype(F32)).astype(BF16)
    return x1, x2, x3


def _tri_dot(tri, x, dims):
    x1, x2, x3 = _split3(x)
    return _dot(tri, x1, dims) + _dot(tri, x2, dims) + _dot(tri, x3, dims)


def _lower_bound(logits_ref):
    return _sigmoid(logits_ref[0:1, :] - logits_ref[1:2, :])


def _hg_gates(qr, fr, lb):
    q = qr * _sigmoid(qr) * (HG_DK ** -0.5)
    sf = _sigmoid(fr)
    fg = lb + (1.0 - lb) * sf
    return q, sf, fg


def _hg_chunk_terms(q, fg, tril_b, low_half):
    g = jnp.log(fg)
    k = 1.0 - fg
    cum = _tri_dot(tril_b, g, NN)
    c_last = jnp.sum(g, axis=0, keepdims=True)
    c_mid = jnp.sum(jnp.where(low_half, g, 0.0), axis=0, keepdims=True)
    e_q = jnp.exp(cum - c_mid)
    e_k = jnp.exp(c_mid - cum)
    e_0 = jnp.exp(cum)
    e_l = jnp.exp(c_last - cum)
    return k, e_q, e_k, e_0, e_l, jnp.exp(c_last)


def _hg_specs(t, col0s):
    return [pl.BlockSpec((t, HG_DK), functools.partial(lambda h, c0: (0, c0 + h), c0=c0)) for c0 in col0s]


def _hgrn_fwd(proj, lb, wn):
    t = proj.shape[0]
    c = HG_CHUNK
    nc = t // c

    def body(q_ref, f_ref, i_ref, g_ref, lb_ref, wn_ref, o_ref, y_ref, st_ref, s_scr):
        s_scr[...] = jnp.zeros_like(s_scr)
        lbv = _lower_bound(lb_ref)
        wnv = wn_ref[...]
        ri = lax.broadcasted_iota(jnp.int32, (c, c), 0)
        ci = lax.broadcasted_iota(jnp.int32, (c, c), 1)
        tril = ri >= ci
        tril_b = tril.astype(BF16)
        low_half = lax.broadcasted_iota(jnp.int32, (c, HG_DK), 0) < c // 2

        def chunk(n, carry):
            rows = pl.ds(pl.multiple_of(n * c, c), c)
            q, _, fg = _hg_gates(q_ref[rows, :], f_ref[rows, :], lbv)
            v = i_ref[rows, :].astype(BF16)
            k, e_q, e_k, e_0, e_l, e_last = _hg_chunk_terms(q, fg, tril_b, low_half)
            st = s_scr[...]
            st_ref[n] = st
            a = jnp.where(tril, _dot((q * e_q).astype(BF16), (k * e_k).astype(BF16), NT), 0.0)
            o = _dot((q * e_0).astype(BF16), st.astype(BF16), NT) + _dot(a.astype(BF16), v, NN)
            s_scr[...] = st * e_last + _dot(v, (k * e_l).astype(BF16), TN)
            o_ref[rows, :] = o
            rstd = lax.rsqrt(jnp.mean(o * o, axis=-1, keepdims=True) + EPS)
            gr = g_ref[rows, :]
            y_ref[rows, :] = (o * rstd * wnv * (gr * _sigmoid(gr))).astype(BF16)
            return carry

        lax.fori_loop(0, nc, chunk, 0)

    vec = pl.BlockSpec((2, HG_DK), lambda h: (0, h))
    return pl.pallas_call(
        body,
        name="hgrn_fwd",
        grid=(HG_HEADS,),
        in_specs=_hg_specs(t, (0, HG_HEADS, 2 * HG_HEADS, 3 * HG_HEADS)) + [vec, pl.BlockSpec((1, HG_DK), lambda h: (0, 0))],
        out_specs=[
            pl.BlockSpec((t, HG_DK), lambda h: (0, h)),
            pl.BlockSpec((t, HG_DK), lambda h: (0, h)),
            pl.BlockSpec((None, nc, HG_DK, HG_DK), lambda h: (h, 0, 0, 0)),
        ],
        out_shape=[
            jax.ShapeDtypeStruct((t, D_MODEL), F32),
            jax.ShapeDtypeStruct((t, D_MODEL), BF16),
            jax.ShapeDtypeStruct((HG_HEADS, nc, HG_DK, HG_DK), F32),
        ],
        scratch_shapes=[pltpu.VMEM((HG_DK, HG_DK), F32)],
        compiler_params=_cparams(("parallel",)),
    )(proj, proj, proj, proj, lb, wn)


def _hgrn_bwd(proj, lb, wn, o, states, dy):
    t = proj.shape[0]
    c = HG_CHUNK
    nc = t // c

    def body(q_ref, f_ref, i_ref, g_ref, lb_ref, wn_ref, o_ref, st_ref, dy_ref, dp_ref, dl_ref, dwn_ref, ds_scr, dlb_scr):
        @pl.when(pl.program_id(0) == 0)
        def _():
            dwn_ref[...] = jnp.zeros_like(dwn_ref)

        ds_scr[...] = jnp.zeros_like(ds_scr)
        dlb_scr[...] = jnp.zeros_like(dlb_scr)
        lbv = _lower_bound(lb_ref)
        wnv = wn_ref[...]
        ri = lax.broadcasted_iota(jnp.int32, (c, c), 0)
        ci = lax.broadcasted_iota(jnp.int32, (c, c), 1)
        tril = ri >= ci
        tril_b = tril.astype(BF16)
        low_half = lax.broadcasted_iota(jnp.int32, (c, HG_DK), 0) < c // 2

        def chunk(nn, carry):
            n = nc - 1 - nn
            rows = pl.ds(pl.multiple_of(n * c, c), c)
            ov = o_ref[rows, :]
            gr = g_ref[rows, :]
            dyv = dy_ref[rows, :].astype(F32)
            rstd = lax.rsqrt(jnp.mean(ov * ov, axis=-1, keepdims=True) + EPS)
            ohat = ov * rstd
            sg = _sigmoid(gr)
            dg_raw = dyv * (ohat * wnv) * (sg * (1.0 + gr * (1.0 - sg)))
            don = dyv * (gr * sg)
            dwn_ref[...] += jnp.sum(don * ohat, axis=0, keepdims=True)
            gd = don * wnv
            do = rstd * (gd - ohat * jnp.mean(gd * ohat, axis=-1, keepdims=True))
            do_b = do.astype(BF16)
            qr = q_ref[rows, :]
            q, sf, fg = _hg_gates(qr, f_ref[rows, :], lbv)
            v = i_ref[rows, :].astype(BF16)
            k, e_q, e_k, e_0, e_l, e_last = _hg_chunk_terms(q, fg, tril_b, low_half)
            qi, qi_lo, _ = _split3(q * e_q)
            ki, ki_lo, _ = _split3(k * e_k)
            q0 = (q * e_0).astype(BF16)
            kl = (k * e_l).astype(BF16)
            st = st_ref[n]
            st_b = st.astype(BF16)
            ds = ds_scr[...]
            ds_b = ds.astype(BF16)
            a_b = jnp.where(tril, _dot(qi, ki, NT), 0.0).astype(BF16)
            da_b = jnp.where(tril, _dot(do_b, v, NT), 0.0).astype(BF16)
            dq = _dot(do_b, st_b, NN) * e_0 + (_dot(da_b, ki, NN) + _dot(da_b, ki_lo, NN)) * e_q
            dk_state = _dot(v, ds_b, NN) * e_l
            dk = (_dot(da_b, qi, TN) + _dot(da_b, qi_lo, TN)) * e_k + dk_state
            dv = _dot(a_b, do_b, TN) + _dot(kl, ds_b, NT)
            ds_scr[...] = ds * e_last + _dot(do_b, q0, TN)
            d_last = jnp.sum(dk_state * k, axis=0, keepdims=True) + jnp.sum(ds * st, axis=0, keepdims=True) * e_last
            dlogf = _tri_dot(tril_b, q * dq - k * dk, TN) + d_last
            dfg = dlogf / fg - dk
            dlb_scr[...] += jnp.sum(dfg * (1.0 - sf), axis=0, keepdims=True)
            sq = _sigmoid(qr)
            dp_ref[0, rows, :] = (dq * (HG_DK ** -0.5) * (sq * (1.0 + qr * (1.0 - sq)))).astype(BF16)
            dp_ref[1, rows, :] = (dfg * (1.0 - lbv) * sf * (1.0 - sf)).astype(BF16)
            dp_ref[2, rows, :] = dv.astype(BF16)
            dp_ref[3, rows, :] = dg_raw.astype(BF16)
            return carry

        lax.fori_loop(0, nc, chunk, 0)
        d0 = dlb_scr[...] * lbv * (1.0 - lbv)
        dl_ref[0:1, :] = d0
        dl_ref[1:2, :] = -d0

    vec = pl.BlockSpec((2, HG_DK), lambda h: (0, h))
    one = pl.BlockSpec((1, HG_DK), lambda h: (0, 0))
    col = pl.BlockSpec((t, HG_DK), lambda h: (0, h))
    return pl.pallas_call(
        body,
        name="hgrn_bwd",
        grid=(HG_HEADS,),
        in_specs=_hg_specs(t, (0, HG_HEADS, 2 * HG_HEADS, 3 * HG_HEADS))
        + [vec, one, col, pl.BlockSpec((None, nc, HG_DK, HG_DK), lambda h: (h, 0, 0, 0)), col],
        out_specs=[pl.BlockSpec((4, t, HG_DK), lambda h: (0, 0, h)), vec, one],
        out_shape=[
            jax.ShapeDtypeStruct((4, t, D_MODEL), BF16),
            jax.ShapeDtypeStruct((2, D_MODEL), F32),
            jax.ShapeDtypeStruct((1, HG_DK), F32),
        ],
        scratch_shapes=[pltpu.VMEM((HG_DK, HG_DK), F32), pltpu.VMEM((1, HG_DK), F32)],
        compiler_params=_cparams(("arbitrary",)),
    )(proj, proj, proj, proj, lb, wn, o, states, dy)


def _att_masks(n):
    tq = lax.broadcasted_iota(jnp.int32, (WINDOW, WINDOW), 0)
    sk = lax.broadcasted_iota(jnp.int32, (WINDOW, WINDOW), 1)
    valid_c = sk <= tq
    valid_p = (sk - tq) > jnp.where(n > 0, 0, WINDOW)
    dist_c = (tq - sk).astype(F32)
    dist_p = dist_c + float(WINDOW)
    return valid_p, valid_c, dist_p, dist_c


def _att_halves(x, lo, kh):
    r = pltpu.roll(x, ATT_HD, 1)
    zero = jnp.zeros_like(x)
    if kh == 0:
        return jnp.where(lo, x, r), jnp.where(lo, x, zero), jnp.where(lo, zero, r)
    return jnp.where(lo, r, x), jnp.where(lo, r, zero), jnp.where(lo, zero, x)


def _att_probs(qm, k2p, k2c, masks, slope, sink):
    valid_p, valid_c, dist_p, dist_c = masks
    sp = jnp.where(valid_p, _dot(qm, k2p, NT) * (ATT_HD ** -0.5) - slope * dist_p, NEG)
    sc = jnp.where(valid_c, _dot(qm, k2c, NT) * (ATT_HD ** -0.5) - slope * dist_c, NEG)
    m = jnp.maximum(jnp.maximum(jnp.max(sp, axis=-1, keepdims=True), jnp.max(sc, axis=-1, keepdims=True)), sink)
    ep = jnp.exp(sp - m)
    ec = jnp.exp(sc - m)
    es = jnp.exp(sink - m)
    inv = 1.0 / (jnp.sum(ep, axis=-1, keepdims=True) + jnp.sum(ec, axis=-1, keepdims=True) + es)
    return ep * inv, ec * inv, es * inv


def _attn_fwd(q, kv, sinks):
    t = q.shape[0]
    nb = t // WINDOW

    def body(sink_ref, q_ref, kvp_ref, kvc_ref, o_ref):
        n = pl.program_id(0)
        masks = _att_masks(n)
        lo = lax.broadcasted_iota(jnp.int32, (WINDOW, LANES), 1) < ATT_HD
        for kh in range(ATT_KVH):
            k2p, _, _ = _att_halves(kvp_ref[:, 0:LANES], lo, kh)
            k2c, _, _ = _att_halves(kvc_ref[:, 0:LANES], lo, kh)
            _, vlo_p, vhi_p = _att_halves(kvp_ref[:, LANES:2 * LANES], lo, kh)
            _, vlo_c, vhi_c = _att_halves(kvc_ref[:, LANES:2 * LANES], lo, kh)
            for jj in range(ATT_GROUP // 2):
                j = kh * (ATT_GROUP // 2) + jj
                qp = q_ref[:, j * LANES:(j + 1) * LANES]
                zero = jnp.zeros_like(qp)
                out = None
                for par in range(2):
                    hq = 2 * j + par
                    qm = jnp.where(lo, qp, zero) if par == 0 else jnp.where(lo, zero, qp)
                    pp, pc, _ = _att_probs(qm, k2p, k2c, masks, ALIBI_SLOPES[hq], sink_ref[hq])
                    vp, vc = (vlo_p, vlo_c) if par == 0 else (vhi_p, vhi_c)
                    part = _dot(pp.astype(BF16), vp, NN) + _dot(pc.astype(BF16), vc, NN)
                    out = part if out is None else out + part
                o_ref[:, j * LANES:(j + 1) * LANES] = out.astype(BF16)

    return pl.pallas_call(
        body,
        name="attn_fwd",
        grid=(nb,),
        in_specs=[
            pl.BlockSpec(memory_space=pltpu.SMEM),
            pl.BlockSpec((WINDOW, D_MODEL), lambda n: (n, 0)),
            pl.BlockSpec((WINDOW, 2 * LANES), lambda n: (jnp.maximum(n - 1, 0), 0)),
            pl.BlockSpec((WINDOW, 2 * LANES), lambda n: (n, 0)),
        ],
        out_specs=pl.BlockSpec((WINDOW, D_MODEL), lambda n: (n, 0)),
        out_shape=jax.ShapeDtypeStruct((t, D_MODEL), BF16),
        compiler_params=_cparams(("parallel",)),
    )(sinks, q, kv, kv)


def _attn_bwd(q, kv, sinks, dout):
    t = q.shape[0]
    nb = t // WINDOW

    def body(sink_ref, q_ref, kvp_ref, kvc_ref, do_ref, dq_ref, dkv_ref, dsink_ref, carry_ref):
        n = pl.program_id(0)

        @pl.when(n == 0)
        def _():
            carry_ref[...] = jnp.zeros_like(carry_ref)
            dsink_ref[...] = jnp.zeros_like(dsink_ref)

        @pl.when(n == nb)
        def _():
            dkv_ref[...] = carry_ref[...].astype(BF16)

        @pl.when(n < nb)
        def _():
            masks = _att_masks(n)
            lo = lax.broadcasted_iota(jnp.int32, (WINDOW, LANES), 1) < ATT_HD
            lane1 = lax.broadcasted_iota(jnp.int32, (1, LANES), 1)
            dsink = jnp.zeros((1, LANES), F32)
            halves = []
            for kh in range(ATT_KVH):
                k2p, klo_p, khi_p = _att_halves(kvp_ref[:, 0:LANES], lo, kh)
                k2c, klo_c, khi_c = _att_halves(kvc_ref[:, 0:LANES], lo, kh)
                v2p, _, _ = _att_halves(kvp_ref[:, LANES:2 * LANES], lo, kh)
                v2c, _, _ = _att_halves(kvc_ref[:, LANES:2 * LANES], lo, kh)
                acc = [jnp.zeros((WINDOW, LANES), F32) for _ in range(4)]
                for jj in range(ATT_GROUP // 2):
                    j = kh * (ATT_GROUP // 2) + jj
                    cols = slice(j * LANES, (j + 1) * LANES)
                    qp = q_ref[:, cols]
                    dop = do_ref[:, cols]
                    zero = jnp.zeros_like(qp)
                    dq_pair = None
                    for par in range(2):
                        hq = 2 * j + par
                        sel = lo if par == 0 else jnp.logical_not(lo)
                        qm = jnp.where(sel, qp, zero)
                        dom = jnp.where(sel, dop, zero)
                        pp, pc, ps = _att_probs(qm, k2p, k2c, masks, ALIBI_SLOPES[hq], sink_ref[hq])
                        dpp = _dot(dom, v2p, NT)
                        dpc = _dot(dom, v2c, NT)
                        delta = jnp.sum(pp * dpp, axis=-1, keepdims=True) + jnp.sum(pc * dpc, axis=-1, keepdims=True)
                        dsp = (pp * (dpp - delta)).astype(BF16)
                        dsc = (pc * (dpc - delta)).astype(BF16)
                        dsink = dsink + jnp.where(lane1 == hq, -jnp.sum(ps * delta, axis=0, keepdims=True), 0.0)
                        kp_, kc_ = (klo_p, klo_c) if par == 0 else (khi_p, khi_c)
                        part = _dot(dsp, kp_, NN) + _dot(dsc, kc_, NN)
                        dq_pair = part if dq_pair is None else dq_pair + part
                        acc[0] = acc[0] + _dot(dsp, qm, TN)
                        acc[1] = acc[1] + _dot(dsc, qm, TN)
                        acc[2] = acc[2] + _dot(pp.astype(BF16), dom, TN)
                        acc[3] = acc[3] + _dot(pc.astype(BF16), dom, TN)
                    dq_ref[:, cols] = (dq_pair * (ATT_HD ** -0.5)).astype(BF16)
                halves.append([a + pltpu.roll(a, ATT_HD, 1) for a in acc])
            scale = ATT_HD ** -0.5
            prev = jnp.concatenate(
                [jnp.where(lo, halves[0][0], halves[1][0]) * scale, jnp.where(lo, halves[0][2], halves[1][2])], axis=1)
            cur = jnp.concatenate(
                [jnp.where(lo, halves[0][1], halves[1][1]) * scale, jnp.where(lo, halves[0][3], halves[1][3])], axis=1)
            dkv_ref[...] = (carry_ref[...] + prev).astype(BF16)
            carry_ref[...] = cur
            dsink_ref[...] += dsink

    blk = lambda n: jnp.minimum(n, nb - 1)
    return pl.pallas_call(
        body,
        name="attn_bwd",
        grid=(nb + 1,),
        in_specs=[
            pl.BlockSpec(memory_space=pltpu.SMEM),
            pl.BlockSpec((WINDOW, D_MODEL), lambda n: (blk(n), 0)),
            pl.BlockSpec((WINDOW, 2 * LANES), lambda n: (jnp.maximum(blk(n) - 1, 0), 0)),
            pl.BlockSpec((WINDOW, 2 * LANES), lambda n: (blk(n), 0)),
            pl.BlockSpec((WINDOW, D_MODEL), lambda n: (blk(n), 0)),
        ],
        out_specs=[
            pl.BlockSpec((WINDOW, D_MODEL), lambda n: (blk(n), 0)),
            pl.BlockSpec((WINDOW, 2 * LANES), lambda n: (jnp.maximum(n - 1, 0), 0)),
            pl.BlockSpec((1, LANES), lambda n: (0, 0)),
        ],
        out_shape=[
            jax.ShapeDtypeStruct((t, D_MODEL), BF16),
            jax.ShapeDtypeStruct((t, 2 * LANES), BF16),
            jax.ShapeDtypeStruct((1, LANES), F32),
        ],
        scratch_shapes=[pltpu.VMEM((WINDOW, 2 * LANES), F32)],
        compiler_params=_cparams(("arbitrary",)),
    )(sinks, q, kv, kv, dout)


def _ffn_fwd(h, norm_g, w_up, conv_w, conv_b, w_down, tag):
    xn = _rms_fwd(h, norm_g, name=f"ffn{tag}_norm")
    up = _mm_nn(xn, w_up, name=f"ffn{tag}_up")
    act = _conv_fwd(up, conv_w, conv_b, name=f"ffn{tag}_conv")
    h_out = _mm_nn(act, w_down, res=h, name=f"ffn{tag}_down")
    return h_out, (xn, up, act)


def _ffn_bwd(dh, h, norm_g, w_up, conv_w, conv_b, w_down, saved, tag):
    xn, up, act = saved
    dw_down = _mm_tn(act, dh, 1, D_MODEL, name=f"ffn{tag}_dwdown")
    dact = _mm_nt(dh, w_down, name=f"ffn{tag}_dact")
    dup, dconv_w, dconv_b = _conv_bwd(up, conv_w, conv_b, dact, name=f"ffn{tag}_dconv")
    dw_up = _mm_tn(xn, dup, N_CHIPS, CONV_COLS, stacked=True, name=f"ffn{tag}_dwup")
    dxn = _mm_nt(dup, w_up, stacked=True, name=f"ffn{tag}_dxn")
    dh_in, dnorm = _rms_bwd(h, norm_g, dxn, dh, name=f"ffn{tag}_dnorm")
    return dh_in, dict(ffn_w_down=dw_down, ffn_w_up=dw_up, ffn_conv_w=dconv_w, ffn_conv_b=dconv_b, ffn_norm=dnorm)


def _local_step(x, target, w, fetch=lambda w, stage, after: w, hook=lambda point, after, grads: None):
    xn0 = _rms_fwd(x, w["hg_norm"], name="hg_norm")
    proj = _mm_nn(xn0, w["hg_w_in"], name="hg_in")
    o, y, states = _hgrn_fwd(proj, w["hg_lb"], w["hg_out_norm"])
    w = fetch(w, "layer0", y)
    h_a = _mm_nn(y, w["hg_w_out"], res=x, name="hg_out")
    h1, ffn0 = _ffn_fwd(h_a, w["ffn_norm"][0], w["ffn_w_up"][0], w["ffn_conv_w"][0], w["ffn_conv_b"][0], w["ffn_w_down"][0], 0)
    w = fetch(w, "layer1", h1)
    kvn = _rms_fwd(h1, w["kv_norm"], name="kv_norm")
    kv = _mm_nn(kvn, w["w_kv"], out_dtype=BF16, name="kv_proj")
    xa = _rms_fwd(h1, w["attn_norm"], name="attn_norm")
    qa = _mm_nn(xa, w["attn_w_q"], out_dtype=BF16, name="attn_q")
    ao = _attn_fwd(qa, kv, w["attn_sinks"])
    h_b = _mm_nn(ao, w["attn_w_o"], res=h1, name="attn_o")
    h2, ffn1 = _ffn_fwd(h_b, w["ffn_norm"][1], w["ffn_w_up"][1], w["ffn_conv_w"][1], w["ffn_conv_b"][1], w["ffn_w_down"][1], 1)
    dh2, d_final, loss = _loss_head(h2, w["final_norm"], target)

    dh_b, g1 = _ffn_bwd(dh2, h_b, w["ffn_norm"][1], w["ffn_w_up"][1], w["ffn_conv_w"][1], w["ffn_conv_b"][1], w["ffn_w_down"][1], ffn1, 1)
    hook("ffn1", dh_b, g1)
    dw_o = _mm_tn(ao, dh_b, 1, D_MODEL, name="attn_dwo")
    dao = _mm_nt(dh_b, w["attn_w_o"], out_dtype=BF16, name="attn_dao")
    dqa, dkv, dsinks = _attn_bwd(qa, kv, w["attn_sinks"], dao)
    dw_q = _mm_tn(xa, dqa, 1, D_MODEL, name="attn_dwq")
    dxa = _mm_nt(dqa, w["attn_w_q"], name="attn_dxa")
    dh1, d_attn_norm = _rms_bwd(h1, w["attn_norm"], dxa, dh_b, name="attn_dnorm")
    dw_kv = _mm_tn(kvn, dkv, 1, 2 * LANES, name="kv_dw")
    dkvn = _mm_nt(dkv, w["w_kv"], name="kv_dx")
    dh1, d_kv_norm = _rms_bwd(h1, w["kv_norm"], dkvn, dh1, name="kv_dnorm")
    hook("attn", dh1, dict(attn_w_o=dw_o, attn_w_q=dw_q, w_kv=dw_kv))
    dh_a, g0 = _ffn_bwd(dh1, h_a, w["ffn_norm"][0], w["ffn_w_up"][0], w["ffn_conv_w"][0], w["ffn_conv_b"][0], w["ffn_w_down"][0], ffn0, 0)
    hook("ffn0", dh_a, g0)
    dw_out = _mm_tn(y, dh_a, 1, D_MODEL, name="hg_dwout")
    dy = _mm_nt(dh_a, w["hg_w_out"], out_dtype=BF16, name="hg_dy")
    dproj, dlb, d_out_norm = _hgrn_bwd(proj, w["hg_lb"], w["hg_out_norm"], o, states, dy)
    hook("hgrn", dproj, None)
    dw_in = _mm_tn(xn0, dproj, N_CHIPS, D_MODEL, stacked=True, name="hg_dwin")
    hook("hg_w", dw_in, dict(hg_w_out=dw_out, hg_w_in=dw_in))
    dxn0 = _mm_nt(dproj, w["hg_w_in"], stacked=True, name="hg_dxn")
    dx, d_hg_norm = _rms_bwd(x, w["hg_norm"], dxn0, dh_a, name="hg_dnorm")

    grads = dict(
        hg_norm=d_hg_norm, hg_w_in=dw_in, hg_lb=dlb, hg_out_norm=d_out_norm, hg_w_out=dw_out,
        kv_norm=d_kv_norm, w_kv=dw_kv, attn_norm=d_attn_norm, attn_w_q=dw_q, attn_sinks=dsinks, attn_w_o=dw_o,
        final_norm=d_final,
    )
    for name in g0:
        grads[name] = [g0[name], g1[name]]
    return loss, dx, grads


ANY = pl.BlockSpec(memory_space=pl.ANY)


def _place():
    x, y, c = lax.axis_index("x"), lax.axis_index("y"), lax.axis_index("c")
    chips = [(1 - x, y), (x, 1 - y), (1 - x, 1 - y)]
    return x, y, c, chips


def _rcopy(src, dst, send_sem, recv_sem, to):
    return pltpu.make_async_remote_copy(src_ref=src, dst_ref=dst, send_sem=send_sem, recv_sem=recv_sem, device_id=to, device_id_type=MESH)


HBM = pl.BlockSpec(memory_space=pltpu.HBM)
SEM = pl.BlockSpec(memory_space=pltpu.SEMAPHORE)
EFFECT = pltpu.SideEffectType.DATAFLOW_SIDE_EFFECTING


def _in_hbm(a):
    return pltpu.with_memory_space_constraint(a, pltpu.HBM)


def _place_shard(shard, place, dtype, name):
    r, cols = shard.shape
    tr = _pick(r, ELEM_ROWS)

    def body(place_ref, s_ref, o_ref):
        o_ref[...] = s_ref[...].astype(o_ref.dtype)

    return pl.pallas_call(
        body,
        name=name,
        grid_spec=pltpu.PrefetchScalarGridSpec(
            num_scalar_prefetch=1,
            grid=(r // tr,),
            in_specs=[pl.BlockSpec((tr, cols), lambda i, place_ref: (i, 0))],
            out_specs=pl.BlockSpec((None, tr, cols), lambda i, place_ref: (place_ref[0], i, 0)),
        ),
        out_shape=jax.ShapeDtypeStruct((N_CHIPS, r, cols), dtype),
        compiler_params=_cparams(("parallel",)),
    )(place, shard)


def _start_copies(name, bufs, n_sem, copies):
    n = len(bufs)

    def body(*refs):
        for cp in copies(refs[:n], refs[n], refs[n + 1]):
            cp.start()

    outs = pl.pallas_call(
        body,
        name=name,
        in_specs=[HBM] * n,
        out_specs=[SEM, SEM] + [HBM] * n,
        out_shape=[pltpu.SemaphoreType.DMA((n_sem,)), pltpu.SemaphoreType.DMA((n_sem,))] + [pltpu.HBM(b.shape, b.dtype) for b in bufs],
        input_output_aliases={i: 2 + i for i in range(n)},
        compiler_params=pltpu.CompilerParams(has_side_effects=EFFECT),
    )(*[_in_hbm(b) for b in bufs])
    return outs[0], outs[1], list(outs[2:])


def _wait_copies(name, bufs, send_sems, recv_sems, after, copies):
    n = len(bufs)

    def body(*refs):
        for cp in copies(refs[:n], refs[n], refs[n + 1]):
            cp.wait_send()
            cp.wait_recv()

    return pl.pallas_call(
        body,
        name=name,
        in_specs=[HBM] * n + [SEM, SEM, ANY],
        out_specs=[HBM] * n,
        out_shape=[pltpu.HBM(b.shape, b.dtype) for b in bufs],
        input_output_aliases={i: i for i in range(n)},
        compiler_params=pltpu.CompilerParams(has_side_effects=EFFECT),
    )(*bufs, send_sems, recv_sems, after)


def _gather_copies(first, count):
    def copies(refs, send_sems, recv_sems):
        x, y, c, chips = _place()
        me = 2 * x + y
        out = []
        for i in range(count):
            for j, (px, py) in enumerate(chips):
                k = 3 * (first + i) + j
                out.append(_rcopy(refs[i].at[me], refs[i].at[me], send_sems.at[k], recv_sems.at[k], (px, py, c)))
        return out

    return copies


def _swap_copies(n):
    def copies(refs, send_sems, recv_sems):
        x, y, c, _ = _place()
        out = []
        for i in range(n):
            h = refs[i].shape[1] // 2
            out.append(_rcopy(refs[i].at[:, pl.ds((1 - c) * h, h)], refs[n + i], send_sems.at[i], recv_sems.at[i], (x, y, 1 - c)))
        return out

    return copies


def _partial_copies(n):
    def copies(refs, send_sems, recv_sems):
        x, y, c, chips = _place()
        out = []
        for i in range(n):
            for j, (px, py) in enumerate(chips):
                out.append(_rcopy(refs[i].at[2 * px + py], refs[n + i].at[j], send_sems.at[3 * i + j], recv_sems.at[3 * i + j], (px, py, c)))
        return out

    return copies


def _share_copies(n):
    def copies(refs, send_sems, recv_sems):
        x, y, c, _ = _place()
        return [_rcopy(refs[i].at[c], refs[i].at[c], send_sems.at[i], recv_sems.at[i], (x, y, 1 - c)) for i in range(n)]

    return copies


def _allreduce_small(vec):
    rows = vec.shape[0]

    def body(v_ref, o_ref, buf, send_sems, recv_sems):
        x, y, c, _ = _place()
        me = 4 * x + 2 * y + c
        buf[me] = v_ref[...]
        copies = []
        for k in range(1, N_DEV):
            peer = (x ^ (k >> 2), y ^ ((k >> 1) & 1), c ^ (k & 1))
            cp = _rcopy(v_ref, buf.at[me], send_sems.at[k - 1], recv_sems.at[k - 1], peer)
            cp.start()
            copies.append(cp)
        for cp in copies:
            cp.wait()
        acc = buf[0]
        for d in range(1, N_DEV):
            acc = acc + buf[d]
        o_ref[...] = acc

    return pl.pallas_call(
        body,
        name="allreduce_small",
        in_specs=[pl.BlockSpec(memory_space=pltpu.VMEM)],
        out_specs=pl.BlockSpec(memory_space=pltpu.VMEM),
        out_shape=jax.ShapeDtypeStruct(vec.shape, F32),
        scratch_shapes=[pltpu.VMEM((N_DEV, rows, LANES), F32), pltpu.SemaphoreType.DMA((N_DEV - 1,)), pltpu.SemaphoreType.DMA((N_DEV - 1,))],
        compiler_params=pltpu.CompilerParams(vmem_limit_bytes=VMEM_LIMIT_BYTES),
    )(vec)


class _Reduction:
    def __init__(self, tag, grads, place, core):
        self.tag, self.n, self.place, self.core = tag, len(grads), place, core
        lands = [lax.empty((N_CHIPS, g.shape[1] // 2, g.shape[2]), F32) for g in grads]
        self.flight = _start_copies(f"rs_swap_start_{tag}", list(grads) + lands, self.n, _swap_copies(self.n))

    def _landed(self, stage, after, copies):
        send_sems, recv_sems, bufs = self.flight
        return _wait_copies(f"rs_{stage}_wait_{self.tag}", bufs, send_sems, recv_sems, after, copies)

    def to_chips(self, after):
        n = self.n
        bufs = self._landed("swap", after, _swap_copies(n))
        sums = [_add_core_halves(g, o, self.core, name=f"rs_add_core_{self.tag}_{i}") for i, (g, o) in enumerate(zip(bufs[:n], bufs[n:]))]
        self.mine = [f for f, _ in sums]
        parts = [b for _, b in sums]
        lands = [lax.empty((3,) + p.shape[1:], BF16) for p in parts]
        self.flight = _start_copies(f"rs_send_start_{self.tag}", parts + lands, 3 * n, _partial_copies(n))

    def to_core(self, after):
        n = self.n
        bufs = self._landed("send", after, _partial_copies(n))
        halves = [_add_chip_partials(f, o, self.place, name=f"rs_add_chip_{self.tag}_{i}") for i, (f, o) in enumerate(zip(self.mine, bufs[n:]))]
        self.flight = _start_copies(f"rs_share_start_{self.tag}", halves, n, _share_copies(n))

    def finish(self, after):
        return [b.reshape((-1,) + b.shape[2:]) for b in self._landed("share", after, _share_copies(self.n))]


ELEM_ROWS = (256, 176, 128, 64, 32, 16, 8)


def _add_core_halves(grad, got, c, name):
    s, r, cols = grad.shape
    h = r // 2
    tr = _pick(h, ELEM_ROWS)

    def body(c_ref, g_ref, o_ref, f_ref, b_ref):
        acc = g_ref[...] + o_ref[...]
        f_ref[...] = acc
        b_ref[...] = acc.astype(BF16)

    blk = pl.BlockSpec((None, tr, cols), lambda k, i, c_ref: (k, i, 0))
    return pl.pallas_call(
        body,
        name=name,
        grid_spec=pltpu.PrefetchScalarGridSpec(
            num_scalar_prefetch=1,
            grid=(s, h // tr),
            in_specs=[pl.BlockSpec((None, None, tr, cols), lambda k, i, c_ref: (k, c_ref[0], i, 0)), blk],
            out_specs=[blk, blk],
        ),
        out_shape=[jax.ShapeDtypeStruct((s, h, cols), F32), jax.ShapeDtypeStruct((s, h, cols), BF16)],
        compiler_params=_cparams(("parallel", "parallel")),
    )(c, grad.reshape(s, 2, h, cols), got)


def _add_chip_partials(mine, got, place, name):
    _, h, cols = mine.shape
    tr = _pick(h, ELEM_ROWS)

    def body(place_ref, m_ref, g_ref, o_ref):
        acc = m_ref[...]
        for j in range(3):
            acc = acc + g_ref[j].astype(F32)
        o_ref[...] = acc

    return pl.pallas_call(
        body,
        name=name,
        grid_spec=pltpu.PrefetchScalarGridSpec(
            num_scalar_prefetch=1,
            grid=(h // tr,),
            in_specs=[
                pl.BlockSpec((None, tr, cols), lambda i, place_ref: (place_ref[0], i, 0)),
                pl.BlockSpec((3, tr, cols), lambda i, place_ref: (0, i, 0)),
            ],
            out_specs=pl.BlockSpec((None, tr, cols), lambda i, place_ref: (place_ref[1], i, 0)),
        ),
        out_shape=jax.ShapeDtypeStruct((2, h, cols), F32),
        compiler_params=_cparams(("parallel",)),
    )(place, mine, got)


def _adamw_math(w, m, v, g):
    nm = ADAM_B1 * m + (1.0 - ADAM_B1) * g
    nv = ADAM_B2 * v + (1.0 - ADAM_B2) * (g * g)
    m_hat = nm * (1.0 / (1.0 - ADAM_B1 ** ADAM_STEP))
    v_hat = nv * (1.0 / (1.0 - ADAM_B2 ** ADAM_STEP))
    return -ADAM_LR * (m_hat / (jnp.sqrt(v_hat) + ADAM_EPS) + ADAM_WD * w), nm, nv


def _adamw_layer(w, m, v, g, layer, prev, name):
    nl, r, cols = w.shape
    tr = _pick(r, ELEM_ROWS)

    def body(w_ref, m_ref, v_ref, g_ref, *rest):
        go_ref, d_ref, nm_ref, nv_ref = rest[-4:]
        gv = g_ref[...]
        d_ref[...], nm_ref[...], nv_ref[...] = _adamw_math(w_ref[...], m_ref[...], v_ref[...], gv)
        go_ref[...] = gv

    lay = pl.BlockSpec((None, tr, cols), lambda i: (layer, i, 0))
    return pl.pallas_call(
        body,
        name=name,
        grid=(r // tr,),
        in_specs=[lay] * 3 + [pl.BlockSpec((tr, cols), lambda i: (i, 0))] + ([ANY] * 4 if prev else []),
        out_specs=[lay] * 4,
        out_shape=[jax.ShapeDtypeStruct((nl, r, cols), F32)] * 4,
        input_output_aliases={4 + k: k for k in range(4)} if prev else {},
        compiler_params=_cparams(("parallel",)),
    )(w, m, v, g, *(prev or ()))


def _adamw(w, m, v, g, name):
    r, cols = w.shape
    tr = _pick(r, ELEM_ROWS)

    def body(w_ref, m_ref, v_ref, g_ref, d_ref, nm_ref, nv_ref):
        d_ref[...], nm_ref[...], nv_ref[...] = _adamw_math(w_ref[...], m_ref[...], v_ref[...], g_ref[...])

    blk = pl.BlockSpec((tr, cols), lambda i: (i, 0))
    return pl.pallas_call(
        body,
        name=name,
        grid=(r // tr,),
        in_specs=[blk] * 4,
        out_specs=[blk] * 3,
        out_shape=[jax.ShapeDtypeStruct((r, cols), F32)] * 3,
        compiler_params=_cparams(("parallel",)),
    )(w, m, v, g)


SMALL_COLS = 384
SMALL_ROWS = 16


def _pad_rows(flat, rows, cols):
    return jnp.pad(flat, (0, rows * cols - flat.shape[0])).reshape(rows, cols)


def kernel(x, hg_norm, hg_w_in, hg_lb_logits, hg_out_norm, hg_w_out, kv_norm, w_kv, attn_norm, attn_w_q, attn_sinks, attn_w_o, ffn_norm, ffn_w_up, ffn_conv_w, ffn_conv_b, ffn_w_down, final_norm, loss_target, m_hg_norm, m_hg_w_in, m_hg_lb_logits, m_hg_out_norm, m_hg_w_out, m_kv_norm, m_w_kv, m_attn_norm, m_attn_w_q, m_attn_sinks, m_attn_w_o, m_ffn_norm, m_ffn_w_up, m_ffn_conv_w, m_ffn_conv_b, m_ffn_w_down, m_final_norm, v_hg_norm, v_hg_w_in, v_hg_lb_logits, v_hg_out_norm, v_hg_w_out, v_kv_norm, v_w_kv, v_attn_norm, v_attn_w_q, v_attn_sinks, v_attn_w_o, v_ffn_norm, v_ffn_w_up, v_ffn_conv_w, v_ffn_conv_b, v_ffn_w_down, v_final_norm):
    wts = dict(hg_norm=hg_norm, hg_w_in=hg_w_in, hg_lb_logits=hg_lb_logits, hg_out_norm=hg_out_norm, hg_w_out=hg_w_out, kv_norm=kv_norm, w_kv=w_kv, attn_norm=attn_norm, attn_w_q=attn_w_q, attn_sinks=attn_sinks, attn_w_o=attn_w_o, ffn_norm=ffn_norm, ffn_w_up=ffn_w_up, ffn_conv_w=ffn_conv_w, ffn_conv_b=ffn_conv_b, ffn_w_down=ffn_w_down, final_norm=final_norm)
    mom1 = dict(hg_norm=m_hg_norm, hg_w_in=m_hg_w_in, hg_lb_logits=m_hg_lb_logits, hg_out_norm=m_hg_out_norm, hg_w_out=m_hg_w_out, kv_norm=m_kv_norm, w_kv=m_w_kv, attn_norm=m_attn_norm, attn_w_q=m_attn_w_q, attn_sinks=m_attn_sinks, attn_w_o=m_attn_w_o, ffn_norm=m_ffn_norm, ffn_w_up=m_ffn_w_up, ffn_conv_w=m_ffn_conv_w, ffn_conv_b=m_ffn_conv_b, ffn_w_down=m_ffn_w_down, final_norm=m_final_norm)
    mom2 = dict(hg_norm=v_hg_norm, hg_w_in=v_hg_w_in, hg_lb_logits=v_hg_lb_logits, hg_out_norm=v_hg_out_norm, hg_w_out=v_hg_w_out, kv_norm=v_kv_norm, w_kv=v_w_kv, attn_norm=v_attn_norm, attn_w_q=v_attn_w_q, attn_sinks=v_attn_sinks, attn_w_o=v_attn_w_o, ffn_norm=v_ffn_norm, ffn_w_up=v_ffn_w_up, ffn_conv_w=v_ffn_conv_w, ffn_conv_b=v_ffn_conv_b, ffn_w_down=v_ffn_w_down, final_norm=v_final_norm)
    names = list(wts)
    chip = 2 * lax.axis_index("x") + lax.axis_index("y")
    core = lax.axis_index("c")
    core_arr = jnp.reshape(core, (1,)).astype(jnp.int32)
    fs = D_FF // N_CHIPS
    ds = D_MODEL // N_CHIPS

    place_arr = jnp.stack([chip, core]).astype(jnp.int32)
    small = jnp.concatenate([hg_norm.reshape(-1), hg_lb_logits.reshape(-1), ffn_conv_w.reshape(-1)])
    n_small = small.shape[0]
    shards = [
        ("small", _pad_rows(small, SMALL_ROWS, SMALL_COLS), F32), ("hg_w_in", hg_w_in[0], BF16),
        ("hg_w_out", hg_w_out[0], BF16), ("ffn_w_up0", ffn_w_up[0], BF16), ("ffn_w_down0", ffn_w_down[0], BF16),
        ("w_kv", w_kv, BF16), ("attn_w_q", attn_w_q[0], BF16), ("attn_w_o", attn_w_o[0], BF16),
        ("ffn_w_up1", ffn_w_up[1], BF16), ("ffn_w_down1", ffn_w_down[1], BF16),
    ]
    stages = dict(first=(0, 2), layer0=(2, 5), layer1=(5, 10))
    placed = [_place_shard(s, place_arr, dt, name=f"place_{nm}") for nm, s, dt in shards]
    send_sems, recv_sems, bufs = _start_copies("gather_start", placed, 3 * len(placed), _gather_copies(0, len(placed)))

    def fetch(w, stage, after):
        lo, hi = stages[stage]
        got = _wait_copies(f"gather_wait_{stage}", bufs[lo:hi], send_sems, recv_sems, after, _gather_copies(lo, hi - lo))
        w = dict(w)
        if stage == "first":
            g_small = got[0].reshape(N_CHIPS, -1)[:, :n_small]
            conv_w = g_small[:, 3 * ds:].reshape(N_CHIPS, 2, 3, fs).transpose(1, 2, 0, 3).reshape(2, 3, D_FF)
            w.update(
                hg_norm=g_small[:, :ds].reshape(1, D_MODEL),
                hg_lb=g_small[:, ds:3 * ds].reshape(N_CHIPS, 2, ds).transpose(1, 0, 2).reshape(2, D_MODEL),
                ffn_conv_w=[conv_w[0], conv_w[1]], hg_w_in=got[1],
            )
        elif stage == "layer0":
            w.update(hg_w_out=got[0].reshape(1, D_MODEL, D_MODEL), ffn_w_up=[got[1], None], ffn_w_down=[got[2].reshape(1, D_FF, D_MODEL), None])
        else:
            w.update(
                w_kv=got[0].reshape(1, D_MODEL, 2 * LANES), attn_w_q=got[1].reshape(1, D_MODEL, D_MODEL),
                attn_w_o=got[2].reshape(1, D_MODEL, D_MODEL), ffn_w_up=[w["ffn_w_up"][0], got[3]],
                ffn_w_down=[w["ffn_w_down"][0], got[4].reshape(1, D_FF, D_MODEL)],
            )
        return w

    whole = dict(
        hg_out_norm=hg_out_norm, kv_norm=kv_norm.reshape(1, D_MODEL), attn_norm=attn_norm, attn_sinks=attn_sinks.reshape(ATT_QH),
        ffn_norm=[ffn_norm[0:1], ffn_norm[1:2]], ffn_conv_b=[ffn_conv_b[0:1], ffn_conv_b[1:2]], final_norm=final_norm.reshape(1, D_MODEL),
    )
    whole = fetch(whole, "first", place_arr)

    red, layer1 = {}, {}

    def by_rows(g, rows):
        return g.reshape(N_CHIPS, rows, g.shape[2])

    def hook(point, after, grads):
        if point == "ffn1":
            red["ffn1"] = _Reduction("ffn1", [by_rows(grads["ffn_w_down"], fs), grads["ffn_w_up"]], place_arr, core_arr)
        elif point == "attn":
            red["ffn1"].to_chips(after)
            layer1.update(grads)
        elif point == "ffn0":
            group = [by_rows(layer1["attn_w_o"], ds), by_rows(layer1["attn_w_q"], ds), by_rows(layer1["w_kv"], ds),
                     by_rows(grads["ffn_w_down"], fs), grads["ffn_w_up"]]
            red["mid"] = _Reduction("mid", group, place_arr, core_arr)
        elif point == "hgrn":
            red["ffn1"].to_core(after)
            red["mid"].to_chips(after)
        elif point == "hg_w":
            red["hg"] = _Reduction("hg", [by_rows(grads["hg_w_out"], ds), grads["hg_w_in"]], place_arr, core_arr)

    loss, dx, grads = _local_step(x[0], loss_target[0], whole, fetch, hook)
    red["hg"].to_chips(dx)

    small_parts = [
        loss.reshape(-1), grads["hg_out_norm"].reshape(-1), grads["attn_sinks"].reshape(-1), grads["kv_norm"].reshape(-1),
        grads["attn_norm"].reshape(-1), grads["ffn_norm"][0].reshape(-1), grads["ffn_norm"][1].reshape(-1),
        grads["ffn_conv_b"][0].reshape(-1), grads["ffn_conv_b"][1].reshape(-1), grads["final_norm"].reshape(-1),
        grads["hg_norm"].reshape(-1), grads["hg_lb"].reshape(-1), grads["ffn_conv_w"][0].reshape(-1), grads["ffn_conv_w"][1].reshape(-1),
    ]
    sizes = [p.shape[0] for p in small_parts]
    flat = jnp.concatenate(small_parts)
    rows = -(-flat.shape[0] // (SUBLANES * LANES)) * SUBLANES
    summed = _allreduce_small(_pad_rows(flat, rows, LANES)).reshape(-1)
    offs = [0]
    for sz in sizes:
        offs.append(offs[-1] + sz)
    sm = [summed[offs[i]:offs[i + 1]] for i in range(len(sizes))]
    loss_out = sm[0][0]
    conv_w_full = jnp.stack([sm[12].reshape(3, D_FF), sm[13].reshape(3, D_FF)])
    small_grads = dict(
        hg_out_norm=sm[1].reshape(1, HG_DK), attn_sinks=sm[2][:ATT_QH].reshape(1, ATT_QH), kv_norm=sm[3], attn_norm=sm[4].reshape(1, D_MODEL),
        ffn_norm=jnp.stack([sm[5], sm[6]]), ffn_conv_b=jnp.stack([sm[7], sm[8]]), final_norm=sm[9],
        hg_norm=lax.dynamic_slice(sm[10].reshape(1, D_MODEL), (0, chip * ds), (1, ds)),
        hg_lb_logits=lax.dynamic_slice(sm[11].reshape(2, D_MODEL), (0, chip * ds), (2, ds)),
        ffn_conv_w=lax.dynamic_slice(conv_w_full, (0, 0, chip * fs), (2, 3, fs)),
    )

    out_g, out_d, out_m, out_v = {}, {}, {}, {}

    def update(name, g2):
        shape = wts[name].shape
        d2, m2, v2 = _adamw(wts[name].reshape(g2.shape), mom1[name].reshape(g2.shape), mom2[name].reshape(g2.shape), g2, name=f"adamw_{name}")
        out_g[name], out_d[name], out_m[name], out_v[name] = g2.reshape(shape), d2.reshape(shape), m2.reshape(shape), v2.reshape(shape)
        return d2

    def update_layer(name, g2, layer, prev):
        res = _adamw_layer(wts[name], mom1[name], mom2[name], g2, layer, prev, name=f"adamw_{name}{layer}")
        out_g[name], out_d[name], out_m[name], out_v[name] = res
        return res

    g_down1, g_up1 = red["ffn1"].finish(summed)
    down1 = update_layer("ffn_w_down", g_down1, 1, None)
    up1 = update_layer("ffn_w_up", g_up1, 1, None)
    red["mid"].to_core(up1[1])
    g_o, g_q, g_kv, g_down0, g_up0 = red["mid"].finish(up1[2])
    update("attn_w_o", g_o)
    update("attn_w_q", g_q)
    update("w_kv", g_kv)
    update_layer("ffn_w_down", g_down0, 0, down1)
    last = update_layer("ffn_w_up", g_up0, 0, up1)
    red["hg"].to_core(last[1])
    g_out, g_in = red["hg"].finish(last[2])
    update("hg_w_out", g_out)
    update("hg_w_in", g_in)

    small_names = [n for n in names if n not in out_g]
    cat = lambda d: jnp.concatenate([d[n].reshape(-1) for n in small_names])
    n_flat = sum(wts[n].size for n in small_names)
    srows = -(-n_flat // (SUBLANES * LANES)) * SUBLANES
    packed = [_pad_rows(cat(d), srows, LANES) for d in (wts, mom1, mom2, small_grads)]
    d_s, m_s, v_s = _adamw(*packed, name="adamw_small")
    off = 0
    for n in small_names:
        sz, shape = wts[n].size, wts[n].shape
        out_g[n] = small_grads[n].reshape(shape)
        out_d[n] = d_s.reshape(-1)[off:off + sz].reshape(shape)
        out_m[n] = m_s.reshape(-1)[off:off + sz].reshape(shape)
        out_v[n] = v_s.reshape(-1)[off:off + sz].reshape(shape)
        off += sz

    grad_x = dx.reshape(x.shape)
    return (loss_out, grad_x, *[out_g[n] for n in names], *[out_d[n] for n in names], *[out_m[n] for n in names], *[out_v[n] for n in names])
```

```python
import functools

import jax
import jax.numpy as jnp
from jax import lax
from jax.experimental import pallas as pl
from jax.experimental.pallas import tpu as pltpu

F32 = jnp.float32
BF16 = jnp.bfloat16
MESH = pl.DeviceIdType.MESH

EPS = 1e-6
D_MODEL = 1024
HG_HEADS = 8
HG_DK = 128
HG_CHUNK = 64
ATT_HD = 64
ATT_QH = 16
ATT_KVH = 2
ATT_GROUP = ATT_QH // ATT_KVH
WINDOW = 128
D_FF = 2816
N_CHIPS = 4
N_DEV = 8
LANES = 128
SUBLANES = 8
VMEM_LIMIT_BYTES = 56 * 1024 * 1024
NEG = -1e30
ALIBI_SLOPES = tuple(2.0 ** (-8.0 * h / ATT_QH) for h in range(1, ATT_QH + 1))

ADAM_LR = 0.001
ADAM_B1 = 0.9
ADAM_B2 = 0.999
ADAM_EPS = 1e-08
ADAM_WD = 0.01
ADAM_STEP = 10


def _cparams(sem=None):
    return pltpu.CompilerParams(dimension_semantics=sem, vmem_limit_bytes=VMEM_LIMIT_BYTES)


def _pick(n, cands):
    for c in cands:
        if n % c == 0:
            return c
    return n


def _sigmoid(x):
    return 1.0 / (1.0 + jnp.exp(-x))


def _dot(a, b, dims):
    return lax.dot_general(a, b, (dims, ((), ())), preferred_element_type=F32)


NN = ((1,), (0,))
NT = ((1,), (1,))
TN = ((0,), (0,))


def _mm_nn(a, w, res=None, out_dtype=F32, name="mm_nn"):
    m, k = a.shape
    s, _, ns = w.shape
    tm = min(m, 512)
    tn = _pick(ns, (512, 1408, 256, 128))
    npb = ns // tn

    def body(a_ref, w_ref, *rest):
        o_ref = rest[-1]
        acc = _dot(a_ref[...].astype(BF16), w_ref[...], NN)
        if res is not None:
            acc = acc + rest[0][...]
        o_ref[...] = acc.astype(o_ref.dtype)

    in_specs = [
        pl.BlockSpec((tm, k), lambda i, j: (i, 0)),
        pl.BlockSpec((None, k, tn), lambda i, j: (j // npb, 0, j % npb)),
    ]
    args = [a, w]
    if res is not None:
        in_specs.append(pl.BlockSpec((tm, tn), lambda i, j: (i, j)))
        args.append(res)
    return pl.pallas_call(
        body,
        name=name,
        grid=(m // tm, s * npb),
        in_specs=in_specs,
        out_specs=pl.BlockSpec((tm, tn), lambda i, j: (i, j)),
        out_shape=jax.ShapeDtypeStruct((m, s * ns), out_dtype),
        compiler_params=_cparams(("parallel", "parallel")),
    )(*args)


def _dy_spec(stacked, tm, tn, npb, row, kk):
    if stacked:
        return pl.BlockSpec((None, tm, tn), lambda *g: (kk(g) // npb, row(g), kk(g) % npb))
    return pl.BlockSpec((tm, tn), lambda *g: (row(g), kk(g)))


def _mm_nt(dy, w, stacked=False, out_dtype=F32, name="mm_nt"):
    s, k, ns = w.shape
    m = dy.shape[1] if stacked else dy.shape[0]
    tm = min(m, 512)
    tko = _pick(k, (1024, 1408, 512, 256))
    tn = _pick(ns, (1024, 1408, 512, 256))
    npb = ns // tn
    nk = s * npb

    def body(dy_ref, w_ref, o_ref, acc_ref):
        kk = pl.program_id(2)

        @pl.when(kk == 0)
        def _():
            acc_ref[...] = jnp.zeros_like(acc_ref)

        acc_ref[...] += _dot(dy_ref[...].astype(BF16), w_ref[...], NT)

        @pl.when(kk == nk - 1)
        def _():
            o_ref[...] = acc_ref[...].astype(o_ref.dtype)

    return pl.pallas_call(
        body,
        name=name,
        grid=(m // tm, k // tko, nk),
        in_specs=[
            _dy_spec(stacked, tm, tn, npb, lambda g: g[0], lambda g: g[2]),
            pl.BlockSpec((None, tko, tn), lambda i, j, kk: (kk // npb, j, kk % npb)),
        ],
        out_specs=pl.BlockSpec((tm, tko), lambda i, j, kk: (i, j)),
        out_shape=jax.ShapeDtypeStruct((m, k), out_dtype),
        scratch_shapes=[pltpu.VMEM((tm, tko), F32)],
        compiler_params=_cparams(("parallel", "parallel", "arbitrary")),
    )(dy, w)


def _mm_tn(a, dy, s, ns, stacked=False, name="mm_tn"):
    m, k = a.shape
    tm = min(m, 512)
    tk = _pick(k, (1024, 1408, 512, 256))
    tn = _pick(ns, (512, 1408, 256, 128))
    npb = ns // tn
    nm = m // tm

    def body(a_ref, dy_ref, o_ref, acc_ref):
        mm = pl.program_id(2)

        @pl.when(mm == 0)
        def _():
            acc_ref[...] = jnp.zeros_like(acc_ref)

        acc_ref[...] += _dot(a_ref[...].astype(BF16), dy_ref[...].astype(BF16), TN)

        @pl.when(mm == nm - 1)
        def _():
            o_ref[...] = acc_ref[...]

    return pl.pallas_call(
        body,
        name=name,
        grid=(k // tk, s * npb, nm),
        in_specs=[
            pl.BlockSpec((tm, tk), lambda i, j, mm: (mm, i)),
            _dy_spec(stacked, tm, tn, npb, lambda g: g[2], lambda g: g[1]),
        ],
        out_specs=pl.BlockSpec((None, tk, tn), lambda i, j, mm: (j // npb, i, j % npb)),
        out_shape=jax.ShapeDtypeStruct((s, k, ns), F32),
        scratch_shapes=[pltpu.VMEM((tk, tn), F32)],
        compiler_params=_cparams(("parallel", "parallel", "arbitrary")),
    )(a, dy)


ROW_TILE = 256


def _rms_fwd(x, g, name="rms_fwd"):
    t, d = x.shape
    r = min(t, ROW_TILE)

    def body(x_ref, g_ref, o_ref):
        xv = x_ref[...]
        rstd = lax.rsqrt(jnp.mean(xv * xv, axis=-1, keepdims=True) + EPS)
        o_ref[...] = (xv * rstd * g_ref[...]).astype(BF16)

    return pl.pallas_call(
        body,
        name=name,
        grid=(t // r,),
        in_specs=[pl.BlockSpec((r, d), lambda i: (i, 0)), pl.BlockSpec((1, d), lambda i: (0, 0))],
        out_specs=pl.BlockSpec((r, d), lambda i: (i, 0)),
        out_shape=jax.ShapeDtypeStruct((t, d), BF16),
        compiler_params=_cparams(("parallel",)),
    )(x, g)


def _rms_bwd(x, g, dxn, dres, name="rms_bwd"):
    t, d = x.shape
    r = min(t, ROW_TILE)

    def body(x_ref, g_ref, dxn_ref, dres_ref, dx_ref, dg_ref):
        @pl.when(pl.program_id(0) == 0)
        def _():
            dg_ref[...] = jnp.zeros_like(dg_ref)

        xv = x_ref[...]
        rstd = lax.rsqrt(jnp.mean(xv * xv, axis=-1, keepdims=True) + EPS)
        xhat = xv * rstd
        dxn_v = dxn_ref[...].astype(F32)
        gd = dxn_v * g_ref[...]
        dx_ref[...] = dres_ref[...] + rstd * (gd - xhat * jnp.mean(gd * xhat, axis=-1, keepdims=True))
        dg_ref[...] += jnp.sum(dxn_v * xhat, axis=0, keepdims=True)

    return pl.pallas_call(
        body,
        name=name,
        grid=(t // r,),
        in_specs=[
            pl.BlockSpec((r, d), lambda i: (i, 0)),
            pl.BlockSpec((1, d), lambda i: (0, 0)),
            pl.BlockSpec((r, d), lambda i: (i, 0)),
            pl.BlockSpec((r, d), lambda i: (i, 0)),
        ],
        out_specs=[pl.BlockSpec((r, d), lambda i: (i, 0)), pl.BlockSpec((1, d), lambda i: (0, 0))],
        out_shape=[jax.ShapeDtypeStruct((t, d), F32), jax.ShapeDtypeStruct((1, d), F32)],
        compiler_params=_cparams(("arbitrary",)),
    )(x, g, dxn, dres)


def _loss_head(h, g, target):
    t, d = h.shape
    r = min(t, ROW_TILE)

    def body(h_ref, g_ref, t_ref, dh_ref, dg_ref, loss_ref):
        @pl.when(pl.program_id(0) == 0)
        def _():
            dg_ref[...] = jnp.zeros_like(dg_ref)
            loss_ref[...] = jnp.zeros_like(loss_ref)

        xv = h_ref[...]
        rstd = lax.rsqrt(jnp.mean(xv * xv, axis=-1, keepdims=True) + EPS)
        xhat = xv * rstd
        gv = g_ref[...]
        err = xhat * gv - t_ref[...]
        loss_ref[...] += 0.5 * jnp.sum(jnp.mean(err * err, axis=-1, keepdims=True), axis=0, keepdims=True)
        dy = err * (1.0 / d)
        gd = dy * gv
        dh_ref[...] = rstd * (gd - xhat * jnp.mean(gd * xhat, axis=-1, keepdims=True))
        dg_ref[...] += jnp.sum(dy * xhat, axis=0, keepdims=True)

    return pl.pallas_call(
        body,
        name="loss_head",
        grid=(t // r,),
        in_specs=[
            pl.BlockSpec((r, d), lambda i: (i, 0)),
            pl.BlockSpec((1, d), lambda i: (0, 0)),
            pl.BlockSpec((r, d), lambda i: (i, 0)),
        ],
        out_specs=[
            pl.BlockSpec((r, d), lambda i: (i, 0)),
            pl.BlockSpec((1, d), lambda i: (0, 0)),
            pl.BlockSpec((1, LANES), lambda i: (0, 0)),
        ],
        out_shape=[
            jax.ShapeDtypeStruct((t, d), F32),
            jax.ShapeDtypeStruct((1, d), F32),
            jax.ShapeDtypeStruct((1, LANES), F32),
        ],
        compiler_params=_cparams(("arbitrary",)),
    )(h, g, target)


CONV_ROWS = 128
CONV_COLS = 1408


def _conv_taps(x_ext, n):
    tot = x_ext.shape[0]
    g1 = pltpu.roll(x_ext, 1, 0)[tot - n:]
    g2 = pltpu.roll(x_ext, 2, 0)[tot - n:]
    return g2, g1


def _conv_fwd(up, conv_w, conv_b, name="conv_fwd"):
    t = up.shape[0]
    r = min(t, CONV_ROWS)
    tc = CONV_COLS
    ncb = D_FF // tc
    hb = r // SUBLANES

    def body(g_ref, halo_ref, v_ref, w_ref, b_ref, o_ref):
        i = pl.program_id(1)
        g0 = g_ref[...]
        halo = halo_ref[...] * jnp.where(i > 0, 1.0, 0.0)
        g2, g1 = _conv_taps(jnp.concatenate([halo, g0], axis=0), r)
        c = b_ref[...] + w_ref[0:1, :] * g2 + w_ref[1:2, :] * g1 + w_ref[2:3, :] * g0
        o_ref[...] = (c * _sigmoid(c) * v_ref[...]).astype(BF16)

    return pl.pallas_call(
        body,
        name=name,
        grid=(ncb, t // r),
        in_specs=[
            pl.BlockSpec((r, tc), lambda j, i: (i, j)),
            pl.BlockSpec((SUBLANES, tc), lambda j, i: (jnp.maximum(i * hb - 1, 0), j)),
            pl.BlockSpec((r, tc), lambda j, i: (i, ncb + j)),
            pl.BlockSpec((3, tc), lambda j, i: (0, j)),
            pl.BlockSpec((1, tc), lambda j, i: (0, j)),
        ],
        out_specs=pl.BlockSpec((r, tc), lambda j, i: (i, j)),
        out_shape=jax.ShapeDtypeStruct((t, D_FF), BF16),
        compiler_params=_cparams(("parallel", "parallel")),
    )(up, up, up, conv_w, conv_b)


def _conv_bwd(up, conv_w, conv_b, dact, name="conv_bwd"):
    t = up.shape[0]
    r = min(t, CONV_ROWS)
    tc = CONV_COLS
    ncb = D_FF // tc
    hb = r // SUBLANES
    nrt = t // r

    def body(g_ref, halo_ref, v_ref, w_ref, b_ref, da_ref, dup_ref, dw_ref, db_ref, nxt_ref):
        ii = pl.program_id(1)
        i = nrt - 1 - ii

        @pl.when(ii == 0)
        def _():
            nxt_ref[...] = jnp.zeros_like(nxt_ref)
            dw_ref[...] = jnp.zeros_like(dw_ref)
            db_ref[...] = jnp.zeros_like(db_ref)

        g0 = g_ref[...]
        halo = halo_ref[...] * jnp.where(i > 0, 1.0, 0.0)
        g2, g1 = _conv_taps(jnp.concatenate([halo, g0], axis=0), r)
        w0, w1, w2 = w_ref[0:1, :], w_ref[1:2, :], w_ref[2:3, :]
        c = b_ref[...] + w0 * g2 + w1 * g1 + w2 * g0
        sg = _sigmoid(c)
        da = da_ref[...]
        dval = da * (c * sg)
        dc = da * v_ref[...] * (sg * (1.0 + c * (1.0 - sg)))
        db_ref[...] += jnp.sum(dc, axis=0, keepdims=True)
        dw_ref[0:1, :] += jnp.sum(dc * g2, axis=0, keepdims=True)
        dw_ref[1:2, :] += jnp.sum(dc * g1, axis=0, keepdims=True)
        dw_ref[2:3, :] += jnp.sum(dc * g0, axis=0, keepdims=True)
        ext = jnp.concatenate([dc, nxt_ref[...]], axis=0)
        tot = r + SUBLANES
        d1 = pltpu.roll(ext, tot - 1, 0)[:r]
        d2 = pltpu.roll(ext, tot - 2, 0)[:r]
        dgate = w2 * dc + w1 * d1 + w0 * d2
        nxt_ref[...] = dc[:SUBLANES]
        dup_ref[0] = dgate.astype(BF16)
        dup_ref[1] = dval.astype(BF16)

    rev = lambda ii: nrt - 1 - ii
    dup, dw, db = pl.pallas_call(
        body,
        name=name,
        grid=(ncb, nrt),
        in_specs=[
            pl.BlockSpec((r, tc), lambda j, ii: (rev(ii), j)),
            pl.BlockSpec((SUBLANES, tc), lambda j, ii: (jnp.maximum(rev(ii) * hb - 1, 0), j)),
            pl.BlockSpec((r, tc), lambda j, ii: (rev(ii), ncb + j)),
            pl.BlockSpec((3, tc), lambda j, ii: (0, j)),
            pl.BlockSpec((1, tc), lambda j, ii: (0, j)),
            pl.BlockSpec((r, tc), lambda j, ii: (rev(ii), j)),
        ],
        out_specs=[
            pl.BlockSpec((2, None, r, tc), lambda j, ii: (0, j, rev(ii), 0)),
            pl.BlockSpec((3, tc), lambda j, ii: (0, j)),
            pl.BlockSpec((1, tc), lambda j, ii: (0, j)),
        ],
        out_shape=[
            jax.ShapeDtypeStruct((2, ncb, t, tc), BF16),
            jax.ShapeDtypeStruct((3, D_FF), F32),
            jax.ShapeDtypeStruct((1, D_FF), F32),
        ],
        scratch_shapes=[pltpu.VMEM((SUBLANES, tc), F32)],
        compiler_params=_cparams(("parallel", "arbitrary")),
    )(up, up, up, conv_w, conv_b, dact)
    return dup.reshape(2 * ncb, t, tc), dw, db


def _split3(x):
    x1 = x.astype(BF16)
    r1 = x - x1.astype(F32)
    x2 = r1.astype(BF16)
    x3 = (r1 - x2.astype(F32)).astype(BF16)
    return x1, x2, x3


def _tri_dot(tri, x, dims):
    x1, x2, x3 = _split3(x)
    return _dot(tri, x1, dims) + _dot(tri, x2, dims) + _dot(tri, x3, dims)


def _lower_bound(logits_ref):
    return _sigmoid(logits_ref[0:1, :] - logits_ref[1:2, :])


def _hg_gates(qr, fr, lb):
    q = qr * _sigmoid(qr) * (HG_DK ** -0.5)
    sf = _sigmoid(fr)
    fg = lb + (1.0 - lb) * sf
    return q, sf, fg


def _hg_chunk_terms(q, fg, tril_b, low_half):
    g = jnp.log(fg)
    k = 1.0 - fg
    cum = _tri_dot(tril_b, g, NN)
    c_last = jnp.sum(g, axis=0, keepdims=True)
    c_mid = jnp.sum(jnp.where(low_half, g, 0.0), axis=0, keepdims=True)
    e_q = jnp.exp(cum - c_mid)
    e_k = jnp.exp(c_mid - cum)
    e_0 = jnp.exp(cum)
    e_l = jnp.exp(c_last - cum)
    return k, e_q, e_k, e_0, e_l, jnp.exp(c_last)


def _hg_specs(t, col0s):
    return [pl.BlockSpec((t, HG_DK), functools.partial(lambda h, c0: (0, c0 + h), c0=c0)) for c0 in col0s]


def _hgrn_fwd(proj, lb, wn):
    t = proj.shape[0]
    c = HG_CHUNK
    nc = t // c

    def body(q_ref, f_ref, i_ref, g_ref, lb_ref, wn_ref, o_ref, y_ref, st_ref, s_scr):
        s_scr[...] = jnp.zeros_like(s_scr)
        lbv = _lower_bound(lb_ref)
        wnv = wn_ref[...]
        ri = lax.broadcasted_iota(jnp.int32, (c, c), 0)
        ci = lax.broadcasted_iota(jnp.int32, (c, c), 1)
        tril = ri >= ci
        tril_b = tril.astype(BF16)
        low_half = lax.broadcasted_iota(jnp.int32, (c, HG_DK), 0) < c // 2

        def chunk(n, carry):
            rows = pl.ds(pl.multiple_of(n * c, c), c)
            q, _, fg = _hg_gates(q_ref[rows, :], f_ref[rows, :], lbv)
            v = i_ref[rows, :].astype(BF16)
            k, e_q, e_k, e_0, e_l, e_last = _hg_chunk_terms(q, fg, tril_b, low_half)
            st = s_scr[...]
            st_ref[n] = st
            a = jnp.where(tril, _dot((q * e_q).astype(BF16), (k * e_k).astype(BF16), NT), 0.0)
            o = _dot((q * e_0).astype(BF16), st.astype(BF16), NT) + _dot(a.astype(BF16), v, NN)
            s_scr[...] = st * e_last + _dot(v, (k * e_l).astype(BF16), TN)
            o_ref[rows, :] = o
            rstd = lax.rsqrt(jnp.mean(o * o, axis=-1, keepdims=True) + EPS)
            gr = g_ref[rows, :]
            y_ref[rows, :] = (o * rstd * wnv * (gr * _sigmoid(gr))).astype(BF16)
            return carry

        lax.fori_loop(0, nc, chunk, 0)

    vec = pl.BlockSpec((2, HG_DK), lambda h: (0, h))
    return pl.pallas_call(
        body,
        name="hgrn_fwd",
        grid=(HG_HEADS,),
        in_specs=_hg_specs(t, (0, HG_HEADS, 2 * HG_HEADS, 3 * HG_HEADS)) + [vec, pl.BlockSpec((1, HG_DK), lambda h: (0, 0))],
        out_specs=[
            pl.BlockSpec((t, HG_DK), lambda h: (0, h)),
            pl.BlockSpec((t, HG_DK), lambda h: (0, h)),
            pl.BlockSpec((None, nc, HG_DK, HG_DK), lambda h: (h, 0, 0, 0)),
        ],
        out_shape=[
            jax.ShapeDtypeStruct((t, D_MODEL), F32),
            jax.ShapeDtypeStruct((t, D_MODEL), BF16),
            jax.ShapeDtypeStruct((HG_HEADS, nc, HG_DK, HG_DK), F32),
        ],
        scratch_shapes=[pltpu.VMEM((HG_DK, HG_DK), F32)],
        compiler_params=_cparams(("parallel",)),
    )(proj, proj, proj, proj, lb, wn)


def _hgrn_bwd(proj, lb, wn, o, states, dy):
    t = proj.shape[0]
    c = HG_CHUNK
    nc = t // c

    def body(q_ref, f_ref, i_ref, g_ref, lb_ref, wn_ref, o_ref, st_ref, dy_ref, dp_ref, dl_ref, dwn_ref, ds_scr, dlb_scr):
        @pl.when(pl.program_id(0) == 0)
        def _():
            dwn_ref[...] = jnp.zeros_like(dwn_ref)

        ds_scr[...] = jnp.zeros_like(ds_scr)
        dlb_scr[...] = jnp.zeros_like(dlb_scr)
        lbv = _lower_bound(lb_ref)
        wnv = wn_ref[...]
        ri = lax.broadcasted_iota(jnp.int32, (c, c), 0)
        ci = lax.broadcasted_iota(jnp.int32, (c, c), 1)
        tril = ri >= ci
        tril_b = tril.astype(BF16)
        low_half = lax.broadcasted_iota(jnp.int32, (c, HG_DK), 0) < c // 2

        def chunk(nn, carry):
            n = nc - 1 - nn
            rows = pl.ds(pl.multiple_of(n * c, c), c)
            ov = o_ref[rows, :]
            gr = g_ref[rows, :]
            dyv = dy_ref[rows, :].astype(F32)
            rstd = lax.rsqrt(jnp.mean(ov * ov, axis=-1, keepdims=True) + EPS)
            ohat = ov * rstd
            sg = _sigmoid(gr)
            dg_raw = dyv * (ohat * wnv) * (sg * (1.0 + gr * (1.0 - sg)))
            don = dyv * (gr * sg)
            dwn_ref[...] += jnp.sum(don * ohat, axis=0, keepdims=True)
            gd = don * wnv
            do = rstd * (gd - ohat * jnp.mean(gd * ohat, axis=-1, keepdims=True))
            do_b = do.astype(BF16)
            qr = q_ref[rows, :]
            q, sf, fg = _hg_gates(qr, f_ref[rows, :], lbv)
            v = i_ref[rows, :].astype(BF16)
            k, e_q, e_k, e_0, e_l, e_last = _hg_chunk_terms(q, fg, tril_b, low_half)
            qi, qi_lo, _ = _split3(q * e_q)
            ki, ki_lo, _ = _split3(k * e_k)
            q0 = (q * e_0).astype(BF16)
            kl = (k * e_l).astype(BF16)
            st = st_ref[n]
            st_b = st.astype(BF16)
            ds = ds_scr[...]
            ds_b = ds.astype(BF16)
            a_b = jnp.where(tril, _dot(qi, ki, NT), 0.0).astype(BF16)
            da_b = jnp.where(tril, _dot(do_b, v, NT), 0.0).astype(BF16)
            dq = _dot(do_b, st_b, NN) * e_0 + (_dot(da_b, ki, NN) + _dot(da_b, ki_lo, NN)) * e_q
            dk_state = _dot(v, ds_b, NN) * e_l
            dk = (_dot(da_b, qi, TN) + _dot(da_b, qi_lo, TN)) * e_k + dk_state
            dv = _dot(a_b, do_b, TN) + _dot(kl, ds_b, NT)
            ds_scr[...] = ds * e_last + _dot(do_b, q0, TN)
            d_last = jnp.sum(dk_state * k, axis=0, keepdims=True) + jnp.sum(ds * st, axis=0, keepdims=True) * e_last
            dlogf = _tri_dot(tril_b, q * dq - k * dk, TN) + d_last
            dfg = dlogf / fg - dk
            dlb_scr[...] += jnp.sum(dfg * (1.0 - sf), axis=0, keepdims=True)
            sq = _sigmoid(qr)
            dp_ref[0, rows, :] = (dq * (HG_DK ** -0.5) * (sq * (1.0 + qr * (1.0 - sq)))).astype(BF16)
            dp_ref[1, rows, :] = (dfg * (1.0 - lbv) * sf * (1.0 - sf)).astype(BF16)
            dp_ref[2, rows, :] = dv.astype(BF16)
            dp_ref[3, rows, :] = dg_raw.astype(BF16)
            return carry

        lax.fori_loop(0, nc, chunk, 0)
        d0 = dlb_scr[...] * lbv * (1.0 - lbv)
        dl_ref[0:1, :] = d0
        dl_ref[1:2, :] = -d0

    vec = pl.BlockSpec((2, HG_DK), lambda h: (0, h))
    one = pl.BlockSpec((1, HG_DK), lambda h: (0, 0))
    col = pl.BlockSpec((t, HG_DK), lambda h: (0, h))
    return pl.pallas_call(
        body,
        name="hgrn_bwd",
        grid=(HG_HEADS,),
        in_specs=_hg_specs(t, (0, HG_HEADS, 2 * HG_HEADS, 3 * HG_HEADS))
        + [vec, one, col, pl.BlockSpec((None, nc, HG_DK, HG_DK), lambda h: (h, 0, 0, 0)), col],
        out_specs=[pl.BlockSpec((4, t, HG_DK), lambda h: (0, 0, h)), vec, one],
        out_shape=[
            jax.ShapeDtypeStruct((4, t, D_MODEL), BF16),
            jax.ShapeDtypeStruct((2, D_MODEL), F32),
            jax.ShapeDtypeStruct((1, HG_DK), F32),
        ],
        scratch_shapes=[pltpu.VMEM((HG_DK, HG_DK), F32), pltpu.VMEM((1, HG_DK), F32)],
        compiler_params=_cparams(("arbitrary",)),
    )(proj, proj, proj, proj, lb, wn, o, states, dy)


def _att_masks(n):
    tq = lax.broadcasted_iota(jnp.int32, (WINDOW, WINDOW), 0)
    sk = lax.broadcasted_iota(jnp.int32, (WINDOW, WINDOW), 1)
    valid_c = sk <= tq
    valid_p = (sk - tq) > jnp.where(n > 0, 0, WINDOW)
    dist_c = (tq - sk).astype(F32)
    dist_p = dist_c + float(WINDOW)
    return valid_p, valid_c, dist_p, dist_c


def _att_halves(x, lo, kh):
    r = pltpu.roll(x, ATT_HD, 1)
    zero = jnp.zeros_like(x)
    if kh == 0:
        return jnp.where(lo, x, r), jnp.where(lo, x, zero), jnp.where(lo, zero, r)
    return jnp.where(lo, r, x), jnp.where(lo, r, zero), jnp.where(lo, zero, x)


def _att_probs(qm, k2p, k2c, masks, slope, sink):
    valid_p, valid_c, dist_p, dist_c = masks
    sp = jnp.where(valid_p, _dot(qm, k2p, NT) * (ATT_HD ** -0.5) - slope * dist_p, NEG)
    sc = jnp.where(valid_c, _dot(qm, k2c, NT) * (ATT_HD ** -0.5) - slope * dist_c, NEG)
    m = jnp.maximum(jnp.maximum(jnp.max(sp, axis=-1, keepdims=True), jnp.max(sc, axis=-1, keepdims=True)), sink)
    ep = jnp.exp(sp - m)
    ec = jnp.exp(sc - m)
    es = jnp.exp(sink - m)
    inv = 1.0 / (jnp.sum(ep, axis=-1, keepdims=True) + jnp.sum(ec, axis=-1, keepdims=True) + es)
    return ep * inv, ec * inv, es * inv


def _attn_fwd(q, kv, sinks):
    t = q.shape[0]
    nb = t // WINDOW

    def body(sink_ref, q_ref, kvp_ref, kvc_ref, o_ref):
        n = pl.program_id(0)
        masks = _att_masks(n)
        lo = lax.broadcasted_iota(jnp.int32, (WINDOW, LANES), 1) < ATT_HD
        for kh in range(ATT_KVH):
            k2p, _, _ = _att_halves(kvp_ref[:, 0:LANES], lo, kh)
            k2c, _, _ = _att_halves(kvc_ref[:, 0:LANES], lo, kh)
            _, vlo_p, vhi_p = _att_halves(kvp_ref[:, LANES:2 * LANES], lo, kh)
            _, vlo_c, vhi_c = _att_halves(kvc_ref[:, LANES:2 * LANES], lo, kh)
            for jj in range(ATT_GROUP // 2):
                j = kh * (ATT_GROUP // 2) + jj
                qp = q_ref[:, j * LANES:(j + 1) * LANES]
                zero = jnp.zeros_like(qp)
                out = None
                for par in range(2):
                    hq = 2 * j + par
                    qm = jnp.where(lo, qp, zero) if par == 0 else jnp.where(lo, zero, qp)
                    pp, pc, _ = _att_probs(qm, k2p, k2c, masks, ALIBI_SLOPES[hq], sink_ref[hq])
                    vp, vc = (vlo_p, vlo_c) if par == 0 else (vhi_p, vhi_c)
                    part = _dot(pp.astype(BF16), vp, NN) + _dot(pc.astype(BF16), vc, NN)
                    out = part if out is None else out + part
                o_ref[:, j * LANES:(j + 1) * LANES] = out.astype(BF16)

    return pl.pallas_call(
        body,
        name="attn_fwd",
        grid=(nb,),
        in_specs=[
            pl.BlockSpec(memory_space=pltpu.SMEM),
            pl.BlockSpec((WINDOW, D_MODEL), lambda n: (n, 0)),
            pl.BlockSpec((WINDOW, 2 * LANES), lambda n: (jnp.maximum(n - 1, 0), 0)),
            pl.BlockSpec((WINDOW, 2 * LANES), lambda n: (n, 0)),
        ],
        out_specs=pl.BlockSpec((WINDOW, D_MODEL), lambda n: (n, 0)),
        out_shape=jax.ShapeDtypeStruct((t, D_MODEL), BF16),
        compiler_params=_cparams(("parallel",)),
    )(sinks, q, kv, kv)


def _attn_bwd(q, kv, sinks, dout):
    t = q.shape[0]
    nb = t // WINDOW

    def body(sink_ref, q_ref, kvp_ref, kvc_ref, do_ref, dq_ref, dkv_ref, dsink_ref, carry_ref):
        n = pl.program_id(0)

        @pl.when(n == 0)
        def _():
            carry_ref[...] = jnp.zeros_like(carry_ref)
            dsink_ref[...] = jnp.zeros_like(dsink_ref)

        @pl.when(n == nb)
        def _():
            dkv_ref[...] = carry_ref[...].astype(BF16)

        @pl.when(n < nb)
        def _():
            masks = _att_masks(n)
            lo = lax.broadcasted_iota(jnp.int32, (WINDOW, LANES), 1) < ATT_HD
            lane1 = lax.broadcasted_iota(jnp.int32, (1, LANES), 1)
            dsink = jnp.zeros((1, LANES), F32)
            halves = []
            for kh in range(ATT_KVH):
                k2p, klo_p, khi_p = _att_halves(kvp_ref[:, 0:LANES], lo, kh)
                k2c, klo_c, khi_c = _att_halves(kvc_ref[:, 0:LANES], lo, kh)
                v2p, _, _ = _att_halves(kvp_ref[:, LANES:2 * LANES], lo, kh)
                v2c, _, _ = _att_halves(kvc_ref[:, LANES:2 * LANES], lo, kh)
                acc = [jnp.zeros((WINDOW, LANES), F32) for _ in range(4)]
                for jj in range(ATT_GROUP // 2):
                    j = kh * (ATT_GROUP // 2) + jj
                    cols = slice(j * LANES, (j + 1) * LANES)
                    qp = q_ref[:, cols]
                    dop = do_ref[:, cols]
                    zero = jnp.zeros_like(qp)
                    dq_pair = None
                    for par in range(2):
                        hq = 2 * j + par
                        sel = lo if par == 0 else jnp.logical_not(lo)
                        qm = jnp.where(sel, qp, zero)
                        dom = jnp.where(sel, dop, zero)
                        pp, pc, ps = _att_probs(qm, k2p, k2c, masks, ALIBI_SLOPES[hq], sink_ref[hq])
                        dpp = _dot(dom, v2p, NT)
                        dpc = _dot(dom, v2c, NT)
                        delta = jnp.sum(pp * dpp, axis=-1, keepdims=True) + jnp.sum(pc * dpc, axis=-1, keepdims=True)
                        dsp = (pp * (dpp - delta)).astype(BF16)
                        dsc = (pc * (dpc - delta)).astype(BF16)
                        dsink = dsink + jnp.where(lane1 == hq, -jnp.sum(ps * delta, axis=0, keepdims=True), 0.0)
                        kp_, kc_ = (klo_p, klo_c) if par == 0 else (khi_p, khi_c)
                        part = _dot(dsp, kp_, NN) + _dot(dsc, kc_, NN)
                        dq_pair = part if dq_pair is None else dq_pair + part
                        acc[0] = acc[0] + _dot(dsp, qm, TN)
                        acc[1] = acc[1] + _dot(dsc, qm, TN)
                        acc[2] = acc[2] + _dot(pp.astype(BF16), dom, TN)
                        acc[3] = acc[3] + _dot(pc.astype(BF16), dom, TN)
                    dq_ref[:, cols] = (dq_pair * (ATT_HD ** -0.5)).astype(BF16)
                halves.append([a + pltpu.roll(a, ATT_HD, 1) for a in acc])
            scale = ATT_HD ** -0.5
            prev = jnp.concatenate(
                [jnp.where(lo, halves[0][0], halves[1][0]) * scale, jnp.where(lo, halves[0][2], halves[1][2])], axis=1)
            cur = jnp.concatenate(
                [jnp.where(lo, halves[0][1], halves[1][1]) * scale, jnp.where(lo, halves[0][3], halves[1][3])], axis=1)
            dkv_ref[...] = (carry_ref[...] + prev).astype(BF16)
            carry_ref[...] = cur
            dsink_ref[...] += dsink

    blk = lambda n: jnp.minimum(n, nb - 1)
    return pl.pallas_call(
        body,
        name="attn_bwd",
        grid=(nb + 1,),
        in_specs=[
            pl.BlockSpec(memory_space=pltpu.SMEM),
            pl.BlockSpec((WINDOW, D_MODEL), lambda n: (blk(n), 0)),
            pl.BlockSpec((WINDOW, 2 * LANES), lambda n: (jnp.maximum(blk(n) - 1, 0), 0)),
            pl.BlockSpec((WINDOW, 2 * LANES), lambda n: (blk(n), 0)),
            pl.BlockSpec((WINDOW, D_MODEL), lambda n: (blk(n), 0)),
        ],
        out_specs=[
            pl.BlockSpec((WINDOW, D_MODEL), lambda n: (blk(n), 0)),
            pl.BlockSpec((WINDOW, 2 * LANES), lambda n: (jnp.maximum(n - 1, 0), 0)),
            pl.BlockSpec((1, LANES), lambda n: (0, 0)),
        ],
        out_shape=[
            jax.ShapeDtypeStruct((t, D_MODEL), BF16),
            jax.ShapeDtypeStruct((t, 2 * LANES), BF16),
            jax.ShapeDtypeStruct((1, LANES), F32),
        ],
        scratch_shapes=[pltpu.VMEM((WINDOW, 2 * LANES), F32)],
        compiler_params=_cparams(("arbitrary",)),
    )(sinks, q, kv, kv, dout)


def _ffn_fwd(h, norm_g, w_up, conv_w, conv_b, w_down, tag):
    xn = _rms_fwd(h, norm_g, name=f"ffn{tag}_norm")
    up = _mm_nn(xn, w_up, name=f"ffn{tag}_up")
    act = _conv_fwd(up, conv_w, conv_b, name=f"ffn{tag}_conv")
    h_out = _mm_nn(act, w_down, res=h, name=f"ffn{tag}_down")
    return h_out, (xn, up, act)


def _ffn_bwd(dh, h, norm_g, w_up, conv_w, conv_b, w_down, saved, tag):
    xn, up, act = saved
    dw_down = _mm_tn(act, dh, 1, D_MODEL, name=f"ffn{tag}_dwdown")
    dact = _mm_nt(dh, w_down, name=f"ffn{tag}_dact")
    dup, dconv_w, dconv_b = _conv_bwd(up, conv_w, conv_b, dact, name=f"ffn{tag}_dconv")
    dw_up = _mm_tn(xn, dup, N_CHIPS, CONV_COLS, stacked=True, name=f"ffn{tag}_dwup")
    dxn = _mm_nt(dup, w_up, stacked=True, name=f"ffn{tag}_dxn")
    dh_in, dnorm = _rms_bwd(h, norm_g, dxn, dh, name=f"ffn{tag}_dnorm")
    return dh_in, dict(ffn_w_down=dw_down, ffn_w_up=dw_up, ffn_conv_w=dconv_w, ffn_conv_b=dconv_b, ffn_norm=dnorm)


def _local_step(x, target, w, fetch=lambda w, stage, after: w, hook=lambda point, dh, grads: dh):
    xn0 = _rms_fwd(x, w["hg_norm"], name="hg_norm")
    proj = _mm_nn(xn0, w["hg_w_in"], name="hg_in")
    o, y, states = _hgrn_fwd(proj, w["hg_lb"], w["hg_out_norm"])
    w = fetch(w, "layer0", y)
    h_a = _mm_nn(y, w["hg_w_out"], res=x, name="hg_out")
    h1, ffn0 = _ffn_fwd(h_a, w["ffn_norm"][0], w["ffn_w_up"][0], w["ffn_conv_w"][0], w["ffn_conv_b"][0], w["ffn_w_down"][0], 0)
    w = fetch(w, "layer1", h1)
    kvn = _rms_fwd(h1, w["kv_norm"], name="kv_norm")
    kv = _mm_nn(kvn, w["w_kv"], out_dtype=BF16, name="kv_proj")
    xa = _rms_fwd(h1, w["attn_norm"], name="attn_norm")
    qa = _mm_nn(xa, w["attn_w_q"], out_dtype=BF16, name="attn_q")
    ao = _attn_fwd(qa, kv, w["attn_sinks"])
    h_b = _mm_nn(ao, w["attn_w_o"], res=h1, name="attn_o")
    h2, ffn1 = _ffn_fwd(h_b, w["ffn_norm"][1], w["ffn_w_up"][1], w["ffn_conv_w"][1], w["ffn_conv_b"][1], w["ffn_w_down"][1], 1)
    dh2, d_final, loss = _loss_head(h2, w["final_norm"], target)

    dh_b, g1 = _ffn_bwd(dh2, h_b, w["ffn_norm"][1], w["ffn_w_up"][1], w["ffn_conv_w"][1], w["ffn_conv_b"][1], w["ffn_w_down"][1], ffn1, 1)
    dh_b = hook("ffn1", dh_b, g1)
    dw_o = _mm_tn(ao, dh_b, 1, D_MODEL, name="attn_dwo")
    dao = _mm_nt(dh_b, w["attn_w_o"], out_dtype=BF16, name="attn_dao")
    dqa, dkv, dsinks = _attn_bwd(qa, kv, w["attn_sinks"], dao)
    dw_q = _mm_tn(xa, dqa, 1, D_MODEL, name="attn_dwq")
    dxa = _mm_nt(dqa, w["attn_w_q"], name="attn_dxa")
    dh1, d_attn_norm = _rms_bwd(h1, w["attn_norm"], dxa, dh_b, name="attn_dnorm")
    dw_kv = _mm_tn(kvn, dkv, 1, 2 * LANES, name="kv_dw")
    dkvn = _mm_nt(dkv, w["w_kv"], name="kv_dx")
    dh1, d_kv_norm = _rms_bwd(h1, w["kv_norm"], dkvn, dh1, name="kv_dnorm")
    dh1 = hook("attn", dh1, dict(attn_w_o=dw_o, attn_w_q=dw_q, w_kv=dw_kv))
    dh_a, g0 = _ffn_bwd(dh1, h_a, w["ffn_norm"][0], w["ffn_w_up"][0], w["ffn_conv_w"][0], w["ffn_conv_b"][0], w["ffn_w_down"][0], ffn0, 0)
    dh_a = hook("ffn0", dh_a, g0)
    dw_out = _mm_tn(y, dh_a, 1, D_MODEL, name="hg_dwout")
    dy = _mm_nt(dh_a, w["hg_w_out"], out_dtype=BF16, name="hg_dy")
    dproj, dlb, d_out_norm = _hgrn_bwd(proj, w["hg_lb"], w["hg_out_norm"], o, states, dy)
    dproj = hook("hgrn", dproj, None)
    dw_in = _mm_tn(xn0, dproj, N_CHIPS, D_MODEL, stacked=True, name="hg_dwin")
    dproj = hook("hg_w", dproj, dict(hg_w_out=dw_out, hg_w_in=dw_in))
    dxn0 = _mm_nt(dproj, w["hg_w_in"], stacked=True, name="hg_dxn")
    dx, d_hg_norm = _rms_bwd(x, w["hg_norm"], dxn0, dh_a, name="hg_dnorm")

    grads = dict(
        hg_norm=d_hg_norm, hg_w_in=dw_in, hg_lb=dlb, hg_out_norm=d_out_norm, hg_w_out=dw_out,
        kv_norm=d_kv_norm, w_kv=dw_kv, attn_norm=d_attn_norm, attn_w_q=dw_q, attn_sinks=dsinks, attn_w_o=dw_o,
        final_norm=d_final,
    )
    for name in g0:
        grads[name] = [g0[name], g1[name]]
    return loss, dx, grads


ANY = pl.BlockSpec(memory_space=pl.ANY)


def _place():
    x, y, c = lax.axis_index("x"), lax.axis_index("y"), lax.axis_index("c")
    chips = [(1 - x, y), (x, 1 - y), (1 - x, 1 - y)]
    return x, y, c, chips


def _rcopy(src, dst, send_sem, recv_sem, to):
    return pltpu.make_async_remote_copy(src_ref=src, dst_ref=dst, send_sem=send_sem, recv_sem=recv_sem, device_id=to, device_id_type=MESH)


HBM = pl.BlockSpec(memory_space=pltpu.HBM)
SEM = pl.BlockSpec(memory_space=pltpu.SEMAPHORE)
EFFECT = pltpu.SideEffectType.DATAFLOW_SIDE_EFFECTING


def _in_hbm(a):
    return pltpu.with_memory_space_constraint(a, pltpu.HBM)


def _place_shard(shard, place, dtype, name):
    r, cols = shard.shape
    tr = _pick(r, ELEM_ROWS)

    def body(place_ref, s_ref, o_ref):
        o_ref[...] = s_ref[...].astype(o_ref.dtype)

    return pl.pallas_call(
        body,
        name=name,
        grid_spec=pltpu.PrefetchScalarGridSpec(
            num_scalar_prefetch=1,
            grid=(r // tr,),
            in_specs=[pl.BlockSpec((tr, cols), lambda i, place_ref: (i, 0))],
            out_specs=pl.BlockSpec((None, tr, cols), lambda i, place_ref: (place_ref[0], i, 0)),
        ),
        out_shape=jax.ShapeDtypeStruct((N_CHIPS, r, cols), dtype),
        compiler_params=_cparams(("parallel",)),
    )(place, shard)


def _start_copies(name, bufs, n_sem, copies):
    n = len(bufs)

    def body(*refs):
        for cp in copies(refs[:n], refs[n], refs[n + 1]):
            cp.start()
        refs[-1][...] = jnp.zeros_like(refs[-1])

    outs = pl.pallas_call(
        body,
        name=name,
        in_specs=[HBM] * n,
        out_specs=[SEM, SEM] + [HBM] * n + [pl.BlockSpec(memory_space=pltpu.VMEM)],
        out_shape=[pltpu.SemaphoreType.DMA((n_sem,)), pltpu.SemaphoreType.DMA((n_sem,))] + [pltpu.HBM(b.shape, b.dtype) for b in bufs]
        + [jax.ShapeDtypeStruct((SUBLANES, LANES), F32)],
        input_output_aliases={i: 2 + i for i in range(n)},
        compiler_params=pltpu.CompilerParams(has_side_effects=EFFECT),
    )(*[_in_hbm(b) for b in bufs])
    return outs[0], outs[1], list(outs[2:-1]), outs[-1]


def _after(x, *tokens):
    return lax.optimization_barrier((x, tokens))[0]


def _wait_copies(name, bufs, send_sems, recv_sems, after, copies):
    n = len(bufs)

    def body(*refs):
        for cp in copies(refs[:n], refs[n], refs[n + 1]):
            cp.wait_send()
            cp.wait_recv()

    return pl.pallas_call(
        body,
        name=name,
        in_specs=[HBM] * n + [SEM, SEM, ANY],
        out_specs=[HBM] * n,
        out_shape=[pltpu.HBM(b.shape, b.dtype) for b in bufs],
        input_output_aliases={i: i for i in range(n)},
        compiler_params=pltpu.CompilerParams(has_side_effects=EFFECT),
    )(*bufs, send_sems, recv_sems, after)


def _gather_copies(first, count):
    def copies(refs, send_sems, recv_sems):
        x, y, c, chips = _place()
        me = 2 * x + y
        out = []
        for i in range(count):
            for j, (px, py) in enumerate(chips):
                k = 3 * (first + i) + j
                out.append(_rcopy(refs[i].at[me], refs[i].at[me], send_sems.at[k], recv_sems.at[k], (px, py, c)))
        return out

    return copies


def _swap_copies(n):
    def copies(refs, send_sems, recv_sems):
        x, y, c, _ = _place()
        out = []
        for i in range(n):
            h = refs[i].shape[1] // 2
            out.append(_rcopy(refs[i].at[:, pl.ds((1 - c) * h, h)], refs[n + i], send_sems.at[i], recv_sems.at[i], (x, y, 1 - c)))
        return out

    return copies


def _partial_copies(n):
    def copies(refs, send_sems, recv_sems):
        x, y, c, chips = _place()
        out = []
        for i in range(n):
            for j, (px, py) in enumerate(chips):
                out.append(_rcopy(refs[i].at[2 * px + py], refs[n + i].at[j], send_sems.at[3 * i + j], recv_sems.at[3 * i + j], (px, py, c)))
        return out

    return copies


def _share_copies(n):
    def copies(refs, send_sems, recv_sems):
        x, y, c, _ = _place()
        return [_rcopy(refs[i].at[c], refs[i].at[c], send_sems.at[i], recv_sems.at[i], (x, y, 1 - c)) for i in range(n)]

    return copies


def _allreduce_small(vec):
    rows = vec.shape[0]

    def body(v_ref, o_ref, buf, send_sems, recv_sems):
        x, y, c, _ = _place()
        me = 4 * x + 2 * y + c
        buf[me] = v_ref[...]
        copies = []
        for k in range(1, N_DEV):
            peer = (x ^ (k >> 2), y ^ ((k >> 1) & 1), c ^ (k & 1))
            cp = _rcopy(v_ref, buf.at[me], send_sems.at[k - 1], recv_sems.at[k - 1], peer)
            cp.start()
            copies.append(cp)
        for cp in copies:
            cp.wait()
        acc = buf[0]
        for d in range(1, N_DEV):
            acc = acc + buf[d]
        o_ref[...] = acc

    return pl.pallas_call(
        body,
        name="allreduce_small",
        in_specs=[pl.BlockSpec(memory_space=pltpu.VMEM)],
        out_specs=pl.BlockSpec(memory_space=pltpu.VMEM),
        out_shape=jax.ShapeDtypeStruct(vec.shape, F32),
        scratch_shapes=[pltpu.VMEM((N_DEV, rows, LANES), F32), pltpu.SemaphoreType.DMA((N_DEV - 1,)), pltpu.SemaphoreType.DMA((N_DEV - 1,))],
        compiler_params=pltpu.CompilerParams(vmem_limit_bytes=VMEM_LIMIT_BYTES),
    )(vec)


class _Reduction:
    def __init__(self, tag, grads, place, core):
        self.tag, self.n, self.place, self.core = tag, len(grads), place, core
        lands = [lax.empty((N_CHIPS, g.shape[1] // 2, g.shape[2]), F32) for g in grads]
        self._start("swap", list(grads) + lands, self.n, _swap_copies(self.n))

    def _start(self, stage, bufs, n_sem, copies):
        *self.flight, self.token = _start_copies(f"rs_{stage}_start_{self.tag}", bufs, n_sem, copies)

    def _landed(self, stage, after, copies):
        send_sems, recv_sems, bufs = self.flight
        return _wait_copies(f"rs_{stage}_wait_{self.tag}", bufs, send_sems, recv_sems, after, copies)

    def to_chips(self, after):
        n = self.n
        bufs = self._landed("swap", after, _swap_copies(n))
        sums = [_add_core_halves(g, o, self.core, name=f"rs_add_core_{self.tag}_{i}") for i, (g, o) in enumerate(zip(bufs[:n], bufs[n:]))]
        self.mine = [f for f, _ in sums]
        parts = [b for _, b in sums]
        lands = [lax.empty((3,) + p.shape[1:], BF16) for p in parts]
        self._start("send", parts + lands, 3 * n, _partial_copies(n))

    def to_core(self, after):
        n = self.n
        bufs = self._landed("send", after, _partial_copies(n))
        halves = [_add_chip_partials(f, o, self.place, name=f"rs_add_chip_{self.tag}_{i}") for i, (f, o) in enumerate(zip(self.mine, bufs[n:]))]
        self._start("share", halves, n, _share_copies(n))

    def finish(self, after):
        return [b.reshape((-1,) + b.shape[2:]) for b in self._landed("share", after, _share_copies(self.n))]


ELEM_ROWS = (256, 176, 128, 64, 32, 16, 8)


def _add_core_halves(grad, got, c, name):
    s, r, cols = grad.shape
    h = r // 2
    tr = _pick(h, ELEM_ROWS)

    def body(c_ref, g_ref, o_ref, f_ref, b_ref):
        acc = g_ref[...] + o_ref[...]
        f_ref[...] = acc
        b_ref[...] = acc.astype(BF16)

    blk = pl.BlockSpec((None, tr, cols), lambda k, i, c_ref: (k, i, 0))
    return pl.pallas_call(
        body,
        name=name,
        grid_spec=pltpu.PrefetchScalarGridSpec(
            num_scalar_prefetch=1,
            grid=(s, h // tr),
            in_specs=[pl.BlockSpec((None, None, tr, cols), lambda k, i, c_ref: (k, c_ref[0], i, 0)), blk],
            out_specs=[blk, blk],
        ),
        out_shape=[jax.ShapeDtypeStruct((s, h, cols), F32), jax.ShapeDtypeStruct((s, h, cols), BF16)],
        compiler_params=_cparams(("parallel", "parallel")),
    )(c, grad.reshape(s, 2, h, cols), got)


def _add_chip_partials(mine, got, place, name):
    _, h, cols = mine.shape
    tr = _pick(h, ELEM_ROWS)

    def body(place_ref, m_ref, g_ref, o_ref):
        acc = m_ref[...]
        for j in range(3):
            acc = acc + g_ref[j].astype(F32)
        o_ref[...] = acc

    return pl.pallas_call(
        body,
        name=name,
        grid_spec=pltpu.PrefetchScalarGridSpec(
            num_scalar_prefetch=1,
            grid=(h // tr,),
            in_specs=[
                pl.BlockSpec((None, tr, cols), lambda i, place_ref: (place_ref[0], i, 0)),
                pl.BlockSpec((3, tr, cols), lambda i, place_ref: (0, i, 0)),
            ],
            out_specs=pl.BlockSpec((None, tr, cols), lambda i, place_ref: (place_ref[1], i, 0)),
        ),
        out_shape=jax.ShapeDtypeStruct((2, h, cols), F32),
        compiler_params=_cparams(("parallel",)),
    )(place, mine, got)


def _adamw_math(w, m, v, g):
    nm = ADAM_B1 * m + (1.0 - ADAM_B1) * g
    nv = ADAM_B2 * v + (1.0 - ADAM_B2) * (g * g)
    m_hat = nm * (1.0 / (1.0 - ADAM_B1 ** ADAM_STEP))
    v_hat = nv * (1.0 / (1.0 - ADAM_B2 ** ADAM_STEP))
    return -ADAM_LR * (m_hat / (jnp.sqrt(v_hat) + ADAM_EPS) + ADAM_WD * w), nm, nv


def _adamw_layer(w, m, v, g, layer, prev, name):
    nl, r, cols = w.shape
    tr = _pick(r, ELEM_ROWS)

    def body(w_ref, m_ref, v_ref, g_ref, *rest):
        go_ref, d_ref, nm_ref, nv_ref = rest[-4:]
        gv = g_ref[...]
        d_ref[...], nm_ref[...], nv_ref[...] = _adamw_math(w_ref[...], m_ref[...], v_ref[...], gv)
        go_ref[...] = gv

    lay = pl.BlockSpec((None, tr, cols), lambda i: (layer, i, 0))
    return pl.pallas_call(
        body,
        name=name,
        grid=(r // tr,),
        in_specs=[lay] * 3 + [pl.BlockSpec((tr, cols), lambda i: (i, 0))] + ([ANY] * 4 if prev else []),
        out_specs=[lay] * 4,
        out_shape=[jax.ShapeDtypeStruct((nl, r, cols), F32)] * 4,
        input_output_aliases={4 + k: k for k in range(4)} if prev else {},
        compiler_params=_cparams(("parallel",)),
    )(w, m, v, g, *(prev or ()))


def _adamw(w, m, v, g, name):
    r, cols = w.shape
    tr = _pick(r, ELEM_ROWS)

    def body(w_ref, m_ref, v_ref, g_ref, d_ref, nm_ref, nv_ref):
        d_ref[...], nm_ref[...], nv_ref[...] = _adamw_math(w_ref[...], m_ref[...], v_ref[...], g_ref[...])

    blk = pl.BlockSpec((tr, cols), lambda i: (i, 0))
    return pl.pallas_call(
        body,
        name=name,
        grid=(r // tr,),
        in_specs=[blk] * 4,
        out_specs=[blk] * 3,
        out_shape=[jax.ShapeDtypeStruct((r, cols), F32)] * 3,
        compiler_params=_cparams(("parallel",)),
    )(w, m, v, g)


SMALL_COLS = 384
SMALL_ROWS = 16


def _pad_rows(flat, rows, cols):
    return jnp.pad(flat, (0, rows * cols - flat.shape[0])).reshape(rows, cols)


def kernel(x, hg_norm, hg_w_in, hg_lb_logits, hg_out_norm, hg_w_out, kv_norm, w_kv, attn_norm, attn_w_q, attn_sinks, attn_w_o, ffn_norm, ffn_w_up, ffn_conv_w, ffn_conv_b, ffn_w_down, final_norm, loss_target, m_hg_norm, m_hg_w_in, m_hg_lb_logits, m_hg_out_norm, m_hg_w_out, m_kv_norm, m_w_kv, m_attn_norm, m_attn_w_q, m_attn_sinks, m_attn_w_o, m_ffn_norm, m_ffn_w_up, m_ffn_conv_w, m_ffn_conv_b, m_ffn_w_down, m_final_norm, v_hg_norm, v_hg_w_in, v_hg_lb_logits, v_hg_out_norm, v_hg_w_out, v_kv_norm, v_w_kv, v_attn_norm, v_attn_w_q, v_attn_sinks, v_attn_w_o, v_ffn_norm, v_ffn_w_up, v_ffn_conv_w, v_ffn_conv_b, v_ffn_w_down, v_final_norm):
    wts = dict(hg_norm=hg_norm, hg_w_in=hg_w_in, hg_lb_logits=hg_lb_logits, hg_out_norm=hg_out_norm, hg_w_out=hg_w_out, kv_norm=kv_norm, w_kv=w_kv, attn_norm=attn_norm, attn_w_q=attn_w_q, attn_sinks=attn_sinks, attn_w_o=attn_w_o, ffn_norm=ffn_norm, ffn_w_up=ffn_w_up, ffn_conv_w=ffn_conv_w, ffn_conv_b=ffn_conv_b, ffn_w_down=ffn_w_down, final_norm=final_norm)
    mom1 = dict(hg_norm=m_hg_norm, hg_w_in=m_hg_w_in, hg_lb_logits=m_hg_lb_logits, hg_out_norm=m_hg_out_norm, hg_w_out=m_hg_w_out, kv_norm=m_kv_norm, w_kv=m_w_kv, attn_norm=m_attn_norm, attn_w_q=m_attn_w_q, attn_sinks=m_attn_sinks, attn_w_o=m_attn_w_o, ffn_norm=m_ffn_norm, ffn_w_up=m_ffn_w_up, ffn_conv_w=m_ffn_conv_w, ffn_conv_b=m_ffn_conv_b, ffn_w_down=m_ffn_w_down, final_norm=m_final_norm)
    mom2 = dict(hg_norm=v_hg_norm, hg_w_in=v_hg_w_in, hg_lb_logits=v_hg_lb_logits, hg_out_norm=v_hg_out_norm, hg_w_out=v_hg_w_out, kv_norm=v_kv_norm, w_kv=v_w_kv, attn_norm=v_attn_norm, attn_w_q=v_attn_w_q, attn_sinks=v_attn_sinks, attn_w_o=v_attn_w_o, ffn_norm=v_ffn_norm, ffn_w_up=v_ffn_w_up, ffn_conv_w=v_ffn_conv_w, ffn_conv_b=v_ffn_conv_b, ffn_w_down=v_ffn_w_down, final_norm=v_final_norm)
    names = list(wts)
    chip = 2 * lax.axis_index("x") + lax.axis_index("y")
    core = lax.axis_index("c")
    core_arr = jnp.reshape(core, (1,)).astype(jnp.int32)
    fs = D_FF // N_CHIPS
    ds = D_MODEL // N_CHIPS

    place_arr = jnp.stack([chip, core]).astype(jnp.int32)
    small = jnp.concatenate([hg_norm.reshape(-1), hg_lb_logits.reshape(-1), ffn_conv_w.reshape(-1)])
    n_small = small.shape[0]
    shards = [
        ("small", _pad_rows(small, SMALL_ROWS, SMALL_COLS), F32), ("hg_w_in", hg_w_in[0], BF16),
        ("hg_w_out", hg_w_out[0], BF16), ("ffn_w_up0", ffn_w_up[0], BF16), ("ffn_w_down0", ffn_w_down[0], BF16),
        ("w_kv", w_kv, BF16), ("attn_w_q", attn_w_q[0], BF16), ("attn_w_o", attn_w_o[0], BF16),
        ("ffn_w_up1", ffn_w_up[1], BF16), ("ffn_w_down1", ffn_w_down[1], BF16),
    ]
    stages = dict(first=(0, 2), layer0=(2, 5), layer1=(5, 10))
    placed = [_place_shard(s, place_arr, dt, name=f"place_{nm}") for nm, s, dt in shards]
    send_sems, recv_sems, bufs, _ = _start_copies("gather_start", placed, 3 * len(placed), _gather_copies(0, len(placed)))

    def fetch(w, stage, after):
        lo, hi = stages[stage]
        got = _wait_copies(f"gather_wait_{stage}", bufs[lo:hi], send_sems, recv_sems, after, _gather_copies(lo, hi - lo))
        w = dict(w)
        if stage == "first":
            g_small = got[0].reshape(N_CHIPS, -1)[:, :n_small]
            conv_w = g_small[:, 3 * ds:].reshape(N_CHIPS, 2, 3, fs).transpose(1, 2, 0, 3).reshape(2, 3, D_FF)
            w.update(
                hg_norm=g_small[:, :ds].reshape(1, D_MODEL),
                hg_lb=g_small[:, ds:3 * ds].reshape(N_CHIPS, 2, ds).transpose(1, 0, 2).reshape(2, D_MODEL),
                ffn_conv_w=[conv_w[0], conv_w[1]], hg_w_in=got[1],
            )
        elif stage == "layer0":
            w.update(hg_w_out=got[0].reshape(1, D_MODEL, D_MODEL), ffn_w_up=[got[1], None], ffn_w_down=[got[2].reshape(1, D_FF, D_MODEL), None])
        else:
            w.update(
                w_kv=got[0].reshape(1, D_MODEL, 2 * LANES), attn_w_q=got[1].reshape(1, D_MODEL, D_MODEL),
                attn_w_o=got[2].reshape(1, D_MODEL, D_MODEL), ffn_w_up=[w["ffn_w_up"][0], got[3]],
                ffn_w_down=[w["ffn_w_down"][0], got[4].reshape(1, D_FF, D_MODEL)],
            )
        return w

    whole = dict(
        hg_out_norm=hg_out_norm, kv_norm=kv_norm.reshape(1, D_MODEL), attn_norm=attn_norm, attn_sinks=attn_sinks.reshape(ATT_QH),
        ffn_norm=[ffn_norm[0:1], ffn_norm[1:2]], ffn_conv_b=[ffn_conv_b[0:1], ffn_conv_b[1:2]], final_norm=final_norm.reshape(1, D_MODEL),
    )
    whole = fetch(whole, "first", place_arr)

    red, layer1 = {}, {}

    def by_rows(g, rows):
        return g.reshape(N_CHIPS, rows, g.shape[2])

    def hook(point, dh, grads):
        if point == "ffn1":
            red["ffn1"] = _Reduction("ffn1", [by_rows(grads["ffn_w_down"], fs), grads["ffn_w_up"]], place_arr, core_arr)
            return _after(dh, red["ffn1"].token)
        if point == "attn":
            red["ffn1"].to_chips(dh)
            layer1.update(grads)
            return _after(dh, red["ffn1"].token)
        if point == "ffn0":
            group = [by_rows(layer1["attn_w_o"], ds), by_rows(layer1["attn_w_q"], ds), by_rows(layer1["w_kv"], ds),
                     by_rows(grads["ffn_w_down"], fs), grads["ffn_w_up"]]
            red["mid"] = _Reduction("mid", group, place_arr, core_arr)
            return _after(dh, red["mid"].token)
        if point == "hgrn":
            red["ffn1"].to_core(dh)
            red["mid"].to_chips(dh)
            return _after(dh, red["ffn1"].token, red["mid"].token)
        red["hg"] = _Reduction("hg", [by_rows(grads["hg_w_out"], ds), grads["hg_w_in"]], place_arr, core_arr)
        return _after(dh, red["hg"].token)

    loss, dx, grads = _local_step(x[0], loss_target[0], whole, fetch, hook)
    red["hg"].to_chips(dx)

    small_parts = [
        loss.reshape(-1), grads["hg_out_norm"].reshape(-1), grads["attn_sinks"].reshape(-1), grads["kv_norm"].reshape(-1),
        grads["attn_norm"].reshape(-1), grads["ffn_norm"][0].reshape(-1), grads["ffn_norm"][1].reshape(-1),
        grads["ffn_conv_b"][0].reshape(-1), grads["ffn_conv_b"][1].reshape(-1), grads["final_norm"].reshape(-1),
        grads["hg_norm"].reshape(-1), grads["hg_lb"].reshape(-1), grads["ffn_conv_w"][0].reshape(-1), grads["ffn_conv_w"][1].reshape(-1),
    ]
    sizes = [p.shape[0] for p in small_parts]
    flat = jnp.concatenate(small_parts)
    rows = -(-flat.shape[0] // (SUBLANES * LANES)) * SUBLANES
    summed = _allreduce_small(_after(_pad_rows(flat, rows, LANES), red["hg"].token)).reshape(-1)
    offs = [0]
    for sz in sizes:
        offs.append(offs[-1] + sz)
    sm = [summed[offs[i]:offs[i + 1]] for i in range(len(sizes))]
    loss_out = sm[0][0]
    conv_w_full = jnp.stack([sm[12].reshape(3, D_FF), sm[13].reshape(3, D_FF)])
    small_grads = dict(
        hg_out_norm=sm[1].reshape(1, HG_DK), attn_sinks=sm[2][:ATT_QH].reshape(1, ATT_QH), kv_norm=sm[3], attn_norm=sm[4].reshape(1, D_MODEL),
        ffn_norm=jnp.stack([sm[5], sm[6]]), ffn_conv_b=jnp.stack([sm[7], sm[8]]), final_norm=sm[9],
        hg_norm=lax.dynamic_slice(sm[10].reshape(1, D_MODEL), (0, chip * ds), (1, ds)),
        hg_lb_logits=lax.dynamic_slice(sm[11].reshape(2, D_MODEL), (0, chip * ds), (2, ds)),
        ffn_conv_w=lax.dynamic_slice(conv_w_full, (0, 0, chip * fs), (2, 3, fs)),
    )

    out_g, out_d, out_m, out_v = {}, {}, {}, {}

    def update(name, g2):
        shape = wts[name].shape
        d2, m2, v2 = _adamw(wts[name].reshape(g2.shape), mom1[name].reshape(g2.shape), mom2[name].reshape(g2.shape), g2, name=f"adamw_{name}")
        out_g[name], out_d[name], out_m[name], out_v[name] = g2.reshape(shape), d2.reshape(shape), m2.reshape(shape), v2.reshape(shape)
        return d2

    def update_layer(name, g2, layer, prev):
        res = _adamw_layer(wts[name], mom1[name], mom2[name], g2, layer, prev, name=f"adamw_{name}{layer}")
        out_g[name], out_d[name], out_m[name], out_v[name] = res
        return res

    g_down1, g_up1 = red["ffn1"].finish(summed)
    down1 = update_layer("ffn_w_down", g_down1, 1, None)
    up1 = update_layer("ffn_w_up", g_up1, 1, None)
    red["mid"].to_core(up1[1])
    g_o, g_q, g_kv, g_down0, g_up0 = red["mid"].finish(up1[2])
    update("attn_w_o", g_o)
    update("attn_w_q", g_q)
    update("w_kv", g_kv)
    update_layer("ffn_w_down", g_down0, 0, down1)
    last = update_layer("ffn_w_up", g_up0, 0, up1)
    red["hg"].to_core(last[1])
    g_out, g_in = red["hg"].finish(last[2])
    update("hg_w_out", g_out)
    update("hg_w_in", g_in)

    small_names = [n for n in names if n not in out_g]
    cat = lambda d: jnp.concatenate([d[n].reshape(-1) for n in small_names])
    n_flat = sum(wts[n].size for n in small_names)
    srows = -(-n_flat // (SUBLANES * LANES)) * SUBLANES
    packed = [_pad_rows(cat(d), srows, LANES) for d in (wts, mom1, mom2, small_grads)]
    d_s, m_s, v_s = _adamw(*packed, name="adamw_small")
    off = 0
    for n in small_names:
        sz, shape = wts[n].size, wts[n].shape
        out_g[n] = small_grads[n].reshape(shape)
        out_d[n] = d_s.reshape(-1)[off:off + sz].reshape(shape)
        out_m[n] = m_s.reshape(-1)[off:off + sz].reshape(shape)
        out_v[n] = v_s.reshape(-1)[off:off + sz].reshape(shape)
        off += sz

    grad_x = dx.reshape(x.shape)
    return (loss_out, grad_x, *[out_g[n] for n in names], *[out_d[n] for n in names], *[out_m[n] for n in names], *[out_v[n] for n in names])
```

```python
import functools

import jax
import jax.numpy as jnp
from jax import lax
from jax.experimental import pallas as pl
from jax.experimental.pallas import tpu as pltpu

F32 = jnp.float32
BF16 = jnp.bfloat16
MESH = pl.DeviceIdType.MESH

EPS = 1e-6
D_MODEL = 1024
HG_HEADS = 8
HG_DK = 128
HG_CHUNK = 64
ATT_HD = 64
ATT_QH = 16
ATT_KVH = 2
ATT_GROUP = ATT_QH // ATT_KVH
WINDOW = 128
D_FF = 2816
N_CHIPS = 4
N_DEV = 8
LANES = 128
SUBLANES = 8
VMEM_LIMIT_BYTES = 56 * 1024 * 1024
NEG = -1e30
ALIBI_SLOPES = tuple(2.0 ** (-8.0 * h / ATT_QH) for h in range(1, ATT_QH + 1))

ADAM_LR = 0.001
ADAM_B1 = 0.9
ADAM_B2 = 0.999
ADAM_EPS = 1e-08
ADAM_WD = 0.01
ADAM_STEP = 10


def _cparams(sem=None):
    return pltpu.CompilerParams(dimension_semantics=sem, vmem_limit_bytes=VMEM_LIMIT_BYTES)


def _pick(n, cands):
    for c in cands:
        if n % c == 0:
            return c
    return n


def _sigmoid(x):
    return 1.0 / (1.0 + jnp.exp(-x))


def _dot(a, b, dims):
    return lax.dot_general(a, b, (dims, ((), ())), preferred_element_type=F32)


NN = ((1,), (0,))
NT = ((1,), (1,))
TN = ((0,), (0,))


def _mm_nn(a, w, res=None, out_dtype=F32, name="mm_nn"):
    m, k = a.shape
    s, _, ns = w.shape
    tm = min(m, 512)
    tn = _pick(ns, (512, 1408, 256, 128))
    npb = ns // tn

    def body(a_ref, w_ref, *rest):
        o_ref = rest[-1]
        acc = _dot(a_ref[...].astype(BF16), w_ref[...], NN)
        if res is not None:
            acc = acc + rest[0][...]
        o_ref[...] = acc.astype(o_ref.dtype)

    in_specs = [
        pl.BlockSpec((tm, k), lambda i, j: (i, 0)),
        pl.BlockSpec((None, k, tn), lambda i, j: (j // npb, 0, j % npb)),
    ]
    args = [a, w]
    if res is not None:
        in_specs.append(pl.BlockSpec((tm, tn), lambda i, j: (i, j)))
        args.append(res)
    return pl.pallas_call(
        body,
        name=name,
        grid=(m // tm, s * npb),
        in_specs=in_specs,
        out_specs=pl.BlockSpec((tm, tn), lambda i, j: (i, j)),
        out_shape=jax.ShapeDtypeStruct((m, s * ns), out_dtype),
        compiler_params=_cparams(("parallel", "parallel")),
    )(*args)


def _dy_spec(stacked, tm, tn, npb, row, kk):
    if stacked:
        return pl.BlockSpec((None, tm, tn), lambda *g: (kk(g) // npb, row(g), kk(g) % npb))
    return pl.BlockSpec((tm, tn), lambda *g: (row(g), kk(g)))


def _dep_specs(deps):
    return [pl.BlockSpec(d.shape, lambda *g: (0, 0)) for d in deps]


def _mm_nt(dy, w, stacked=False, out_dtype=F32, name="mm_nt", deps=()):
    s, k, ns = w.shape
    m = dy.shape[1] if stacked else dy.shape[0]
    tm = min(m, 512)
    tko = _pick(k, (1024, 1408, 512, 256))
    tn = _pick(ns, (1024, 1408, 512, 256))
    npb = ns // tn
    nk = s * npb

    def body(dy_ref, w_ref, *rest):
        o_ref, acc_ref = rest[-2:]
        kk = pl.program_id(2)

        @pl.when(kk == 0)
        def _():
            acc_ref[...] = jnp.zeros_like(acc_ref)

        acc_ref[...] += _dot(dy_ref[...].astype(BF16), w_ref[...], NT)

        @pl.when(kk == nk - 1)
        def _():
            o_ref[...] = acc_ref[...].astype(o_ref.dtype)

    return pl.pallas_call(
        body,
        name=name,
        grid=(m // tm, k // tko, nk),
        in_specs=[
            _dy_spec(stacked, tm, tn, npb, lambda g: g[0], lambda g: g[2]),
            pl.BlockSpec((None, tko, tn), lambda i, j, kk: (kk // npb, j, kk % npb)),
        ] + _dep_specs(deps),
        out_specs=pl.BlockSpec((tm, tko), lambda i, j, kk: (i, j)),
        out_shape=jax.ShapeDtypeStruct((m, k), out_dtype),
        scratch_shapes=[pltpu.VMEM((tm, tko), F32)],
        compiler_params=_cparams(("parallel", "parallel", "arbitrary")),
    )(dy, w, *deps)


def _mm_tn(a, dy, s, ns, stacked=False, name="mm_tn", deps=()):
    m, k = a.shape
    tm = min(m, 512)
    tk = _pick(k, (1024, 1408, 512, 256))
    tn = _pick(ns, (512, 1408, 256, 128))
    npb = ns // tn
    nm = m // tm

    def body(a_ref, dy_ref, *rest):
        o_ref, acc_ref = rest[-2:]
        mm = pl.program_id(2)

        @pl.when(mm == 0)
        def _():
            acc_ref[...] = jnp.zeros_like(acc_ref)

        acc_ref[...] += _dot(a_ref[...].astype(BF16), dy_ref[...].astype(BF16), TN)

        @pl.when(mm == nm - 1)
        def _():
            o_ref[...] = acc_ref[...]

    return pl.pallas_call(
        body,
        name=name,
        grid=(k // tk, s * npb, nm),
        in_specs=[
            pl.BlockSpec((tm, tk), lambda i, j, mm: (mm, i)),
            _dy_spec(stacked, tm, tn, npb, lambda g: g[2], lambda g: g[1]),
        ] + _dep_specs(deps),
        out_specs=pl.BlockSpec((None, tk, tn), lambda i, j, mm: (j // npb, i, j % npb)),
        out_shape=jax.ShapeDtypeStruct((s, k, ns), F32),
        scratch_shapes=[pltpu.VMEM((tk, tn), F32)],
        compiler_params=_cparams(("parallel", "parallel", "arbitrary")),
    )(a, dy, *deps)


ROW_TILE = 256


def _rms_fwd(x, g, name="rms_fwd"):
    t, d = x.shape
    r = min(t, ROW_TILE)

    def body(x_ref, g_ref, o_ref):
        xv = x_ref[...]
        rstd = lax.rsqrt(jnp.mean(xv * xv, axis=-1, keepdims=True) + EPS)
        o_ref[...] = (xv * rstd * g_ref[...]).astype(BF16)

    return pl.pallas_call(
        body,
        name=name,
        grid=(t // r,),
        in_specs=[pl.BlockSpec((r, d), lambda i: (i, 0)), pl.BlockSpec((1, d), lambda i: (0, 0))],
        out_specs=pl.BlockSpec((r, d), lambda i: (i, 0)),
        out_shape=jax.ShapeDtypeStruct((t, d), BF16),
        compiler_params=_cparams(("parallel",)),
    )(x, g)


def _rms_bwd(x, g, dxn, dres, name="rms_bwd"):
    t, d = x.shape
    r = min(t, ROW_TILE)

    def body(x_ref, g_ref, dxn_ref, dres_ref, dx_ref, dg_ref):
        @pl.when(pl.program_id(0) == 0)
        def _():
            dg_ref[...] = jnp.zeros_like(dg_ref)

        xv = x_ref[...]
        rstd = lax.rsqrt(jnp.mean(xv * xv, axis=-1, keepdims=True) + EPS)
        xhat = xv * rstd
        dxn_v = dxn_ref[...].astype(F32)
        gd = dxn_v * g_ref[...]
        dx_ref[...] = dres_ref[...] + rstd * (gd - xhat * jnp.mean(gd * xhat, axis=-1, keepdims=True))
        dg_ref[...] += jnp.sum(dxn_v * xhat, axis=0, keepdims=True)

    return pl.pallas_call(
        body,
        name=name,
        grid=(t // r,),
        in_specs=[
            pl.BlockSpec((r, d), lambda i: (i, 0)),
            pl.BlockSpec((1, d), lambda i: (0, 0)),
            pl.BlockSpec((r, d), lambda i: (i, 0)),
            pl.BlockSpec((r, d), lambda i: (i, 0)),
        ],
        out_specs=[pl.BlockSpec((r, d), lambda i: (i, 0)), pl.BlockSpec((1, d), lambda i: (0, 0))],
        out_shape=[jax.ShapeDtypeStruct((t, d), F32), jax.ShapeDtypeStruct((1, d), F32)],
        compiler_params=_cparams(("arbitrary",)),
    )(x, g, dxn, dres)


def _loss_head(h, g, target):
    t, d = h.shape
    r = min(t, ROW_TILE)

    def body(h_ref, g_ref, t_ref, dh_ref, dg_ref, loss_ref):
        @pl.when(pl.program_id(0) == 0)
        def _():
            dg_ref[...] = jnp.zeros_like(dg_ref)
            loss_ref[...] = jnp.zeros_like(loss_ref)

        xv = h_ref[...]
        rstd = lax.rsqrt(jnp.mean(xv * xv, axis=-1, keepdims=True) + EPS)
        xhat = xv * rstd
        gv = g_ref[...]
        err = xhat * gv - t_ref[...]
        loss_ref[...] += 0.5 * jnp.sum(jnp.mean(err * err, axis=-1, keepdims=True), axis=0, keepdims=True)
        dy = err * (1.0 / d)
        gd = dy * gv
        dh_ref[...] = rstd * (gd - xhat * jnp.mean(gd * xhat, axis=-1, keepdims=True))
        dg_ref[...] += jnp.sum(dy * xhat, axis=0, keepdims=True)

    return pl.pallas_call(
        body,
        name="loss_head",
        grid=(t // r,),
        in_specs=[
            pl.BlockSpec((r, d), lambda i: (i, 0)),
            pl.BlockSpec((1, d), lambda i: (0, 0)),
            pl.BlockSpec((r, d), lambda i: (i, 0)),
        ],
        out_specs=[
            pl.BlockSpec((r, d), lambda i: (i, 0)),
            pl.BlockSpec((1, d), lambda i: (0, 0)),
            pl.BlockSpec((1, LANES), lambda i: (0, 0)),
        ],
        out_shape=[
            jax.ShapeDtypeStruct((t, d), F32),
            jax.ShapeDtypeStruct((1, d), F32),
            jax.ShapeDtypeStruct((1, LANES), F32),
        ],
        compiler_params=_cparams(("arbitrary",)),
    )(h, g, target)


CONV_ROWS = 128
CONV_COLS = 1408


def _conv_taps(x_ext, n):
    tot = x_ext.shape[0]
    g1 = pltpu.roll(x_ext, 1, 0)[tot - n:]
    g2 = pltpu.roll(x_ext, 2, 0)[tot - n:]
    return g2, g1


def _conv_fwd(up, conv_w, conv_b, name="conv_fwd"):
    t = up.shape[0]
    r = min(t, CONV_ROWS)
    tc = CONV_COLS
    ncb = D_FF // tc
    hb = r // SUBLANES

    def body(g_ref, halo_ref, v_ref, w_ref, b_ref, o_ref):
        i = pl.program_id(1)
        g0 = g_ref[...]
        halo = halo_ref[...] * jnp.where(i > 0, 1.0, 0.0)
        g2, g1 = _conv_taps(jnp.concatenate([halo, g0], axis=0), r)
        c = b_ref[...] + w_ref[0:1, :] * g2 + w_ref[1:2, :] * g1 + w_ref[2:3, :] * g0
        o_ref[...] = (c * _sigmoid(c) * v_ref[...]).astype(BF16)

    return pl.pallas_call(
        body,
        name=name,
        grid=(ncb, t // r),
        in_specs=[
            pl.BlockSpec((r, tc), lambda j, i: (i, j)),
            pl.BlockSpec((SUBLANES, tc), lambda j, i: (jnp.maximum(i * hb - 1, 0), j)),
            pl.BlockSpec((r, tc), lambda j, i: (i, ncb + j)),
            pl.BlockSpec((3, tc), lambda j, i: (0, j)),
            pl.BlockSpec((1, tc), lambda j, i: (0, j)),
        ],
        out_specs=pl.BlockSpec((r, tc), lambda j, i: (i, j)),
        out_shape=jax.ShapeDtypeStruct((t, D_FF), BF16),
        compiler_params=_cparams(("parallel", "parallel")),
    )(up, up, up, conv_w, conv_b)


def _conv_bwd(up, conv_w, conv_b, dact, name="conv_bwd"):
    t = up.shape[0]
    r = min(t, CONV_ROWS)
    tc = CONV_COLS
    ncb = D_FF // tc
    hb = r // SUBLANES
    nrt = t // r

    def body(g_ref, halo_ref, v_ref, w_ref, b_ref, da_ref, dup_ref, dw_ref, db_ref, nxt_ref):
        ii = pl.program_id(1)
        i = nrt - 1 - ii

        @pl.when(ii == 0)
        def _():
            nxt_ref[...] = jnp.zeros_like(nxt_ref)
            dw_ref[...] = jnp.zeros_like(dw_ref)
            db_ref[...] = jnp.zeros_like(db_ref)

        g0 = g_ref[...]
        halo = halo_ref[...] * jnp.where(i > 0, 1.0, 0.0)
        g2, g1 = _conv_taps(jnp.concatenate([halo, g0], axis=0), r)
        w0, w1, w2 = w_ref[0:1, :], w_ref[1:2, :], w_ref[2:3, :]
        c = b_ref[...] + w0 * g2 + w1 * g1 + w2 * g0
        sg = _sigmoid(c)
        da = da_ref[...]
        dval = da * (c * sg)
        dc = da * v_ref[...] * (sg * (1.0 + c * (1.0 - sg)))
        db_ref[...] += jnp.sum(dc, axis=0, keepdims=True)
        dw_ref[0:1, :] += jnp.sum(dc * g2, axis=0, keepdims=True)
        dw_ref[1:2, :] += jnp.sum(dc * g1, axis=0, keepdims=True)
        dw_ref[2:3, :] += jnp.sum(dc * g0, axis=0, keepdims=True)
        ext = jnp.concatenate([dc, nxt_ref[...]], axis=0)
        tot = r + SUBLANES
        d1 = pltpu.roll(ext, tot - 1, 0)[:r]
        d2 = pltpu.roll(ext, tot - 2, 0)[:r]
        dgate = w2 * dc + w1 * d1 + w0 * d2
        nxt_ref[...] = dc[:SUBLANES]
        dup_ref[0] = dgate.astype(BF16)
        dup_ref[1] = dval.astype(BF16)

    rev = lambda ii: nrt - 1 - ii
    dup, dw, db = pl.pallas_call(
        body,
        name=name,
        grid=(ncb, nrt),
        in_specs=[
            pl.BlockSpec((r, tc), lambda j, ii: (rev(ii), j)),
            pl.BlockSpec((SUBLANES, tc), lambda j, ii: (jnp.maximum(rev(ii) * hb - 1, 0), j)),
            pl.BlockSpec((r, tc), lambda j, ii: (rev(ii), ncb + j)),
            pl.BlockSpec((3, tc), lambda j, ii: (0, j)),
            pl.BlockSpec((1, tc), lambda j, ii: (0, j)),
            pl.BlockSpec((r, tc), lambda j, ii: (rev(ii), j)),
        ],
        out_specs=[
            pl.BlockSpec((2, None, r, tc), lambda j, ii: (0, j, rev(ii), 0)),
            pl.BlockSpec((3, tc), lambda j, ii: (0, j)),
            pl.BlockSpec((1, tc), lambda j, ii: (0, j)),
        ],
        out_shape=[
            jax.ShapeDtypeStruct((2, ncb, t, tc), BF16),
            jax.ShapeDtypeStruct((3, D_FF), F32),
            jax.ShapeDtypeStruct((1, D_FF), F32),
        ],
        scratch_shapes=[pltpu.VMEM((SUBLANES, tc), F32)],
        compiler_params=_cparams(("parallel", "arbitrary")),
    )(up, up, up, conv_w, conv_b, dact)
    return dup.reshape(2 * ncb, t, tc), dw, db


def _split3(x):
    x1 = x.astype(BF16)
    r1 = x - x1.astype(F32)
    x2 = r1.astype(BF16)
    x3 = (r1 - x2.astype(F32)).astype(BF16)
    return x1, x2, x3


def _tri_dot(tri, x, dims):
    x1, x2, x3 = _split3(x)
    return _dot(tri, x1, dims) + _dot(tri, x2, dims) + _dot(tri, x3, dims)


def _lower_bound(logits_ref):
    return _sigmoid(logits_ref[0:1, :] - logits_ref[1:2, :])


def _hg_gates(qr, fr, lb):
    q = qr * _sigmoid(qr) * (HG_DK ** -0.5)
    sf = _sigmoid(fr)
    fg = lb + (1.0 - lb) * sf
    return q, sf, fg


def _hg_chunk_terms(q, fg, tril_b, low_half):
    g = jnp.log(fg)
    k = 1.0 - fg
    cum = _tri_dot(tril_b, g, NN)
    c_last = jnp.sum(g, axis=0, keepdims=True)
    c_mid = jnp.sum(jnp.where(low_half, g, 0.0), axis=0, keepdims=True)
    e_q = jnp.exp(cum - c_mid)
    e_k = jnp.exp(c_mid - cum)
    e_0 = jnp.exp(cum)
    e_l = jnp.exp(c_last - cum)
    return k, e_q, e_k, e_0, e_l, jnp.exp(c_last)


def _hg_specs(t, col0s):
    return [pl.BlockSpec((t, HG_DK), functools.partial(lambda h, c0: (0, c0 + h), c0=c0)) for c0 in col0s]


def _hgrn_fwd(proj, lb, wn):
    t = proj.shape[0]
    c = HG_CHUNK
    nc = t // c

    def body(q_ref, f_ref, i_ref, g_ref, lb_ref, wn_ref, o_ref, y_ref, st_ref, s_scr):
        s_scr[...] = jnp.zeros_like(s_scr)
        lbv = _lower_bound(lb_ref)
        wnv = wn_ref[...]
        ri = lax.broadcasted_iota(jnp.int32, (c, c), 0)
        ci = lax.broadcasted_iota(jnp.int32, (c, c), 1)
        tril = ri >= ci
        tril_b = tril.astype(BF16)
        low_half = lax.broadcasted_iota(jnp.int32, (c, HG_DK), 0) < c // 2

        def chunk(n, carry):
            rows = pl.ds(pl.multiple_of(n * c, c), c)
            q, _, fg = _hg_gates(q_ref[rows, :], f_ref[rows, :], lbv)
            v = i_ref[rows, :].astype(BF16)
            k, e_q, e_k, e_0, e_l, e_last = _hg_chunk_terms(q, fg, tril_b, low_half)
            st = s_scr[...]
            st_ref[n] = st
            a = jnp.where(tril, _dot((q * e_q).astype(BF16), (k * e_k).astype(BF16), NT), 0.0)
            o = _dot((q * e_0).astype(BF16), st.astype(BF16), NT) + _dot(a.astype(BF16), v, NN)
            s_scr[...] = st * e_last + _dot(v, (k * e_l).astype(BF16), TN)
            o_ref[rows, :] = o
            rstd = lax.rsqrt(jnp.mean(o * o, axis=-1, keepdims=True) + EPS)
            gr = g_ref[rows, :]
            y_ref[rows, :] = (o * rstd * wnv * (gr * _sigmoid(gr))).astype(BF16)
            return carry

        lax.fori_loop(0, nc, chunk, 0)

    vec = pl.BlockSpec((2, HG_DK), lambda h: (0, h))
    return pl.pallas_call(
        body,
        name="hgrn_fwd",
        grid=(HG_HEADS,),
        in_specs=_hg_specs(t, (0, HG_HEADS, 2 * HG_HEADS, 3 * HG_HEADS)) + [vec, pl.BlockSpec((1, HG_DK), lambda h: (0, 0))],
        out_specs=[
            pl.BlockSpec((t, HG_DK), lambda h: (0, h)),
            pl.BlockSpec((t, HG_DK), lambda h: (0, h)),
            pl.BlockSpec((None, nc, HG_DK, HG_DK), lambda h: (h, 0, 0, 0)),
        ],
        out_shape=[
            jax.ShapeDtypeStruct((t, D_MODEL), F32),
            jax.ShapeDtypeStruct((t, D_MODEL), BF16),
            jax.ShapeDtypeStruct((HG_HEADS, nc, HG_DK, HG_DK), F32),
        ],
        scratch_shapes=[pltpu.VMEM((HG_DK, HG_DK), F32)],
        compiler_params=_cparams(("parallel",)),
    )(proj, proj, proj, proj, lb, wn)


def _hgrn_bwd(proj, lb, wn, o, states, dy):
    t = proj.shape[0]
    c = HG_CHUNK
    nc = t // c

    def body(q_ref, f_ref, i_ref, g_ref, lb_ref, wn_ref, o_ref, st_ref, dy_ref, dp_ref, dl_ref, dwn_ref, ds_scr, dlb_scr):
        @pl.when(pl.program_id(0) == 0)
        def _():
            dwn_ref[...] = jnp.zeros_like(dwn_ref)

        ds_scr[...] = jnp.zeros_like(ds_scr)
        dlb_scr[...] = jnp.zeros_like(dlb_scr)
        lbv = _lower_bound(lb_ref)
        wnv = wn_ref[...]
        ri = lax.broadcasted_iota(jnp.int32, (c, c), 0)
        ci = lax.broadcasted_iota(jnp.int32, (c, c), 1)
        tril = ri >= ci
        tril_b = tril.astype(BF16)
        low_half = lax.broadcasted_iota(jnp.int32, (c, HG_DK), 0) < c // 2

        def chunk(nn, carry):
            n = nc - 1 - nn
            rows = pl.ds(pl.multiple_of(n * c, c), c)
            ov = o_ref[rows, :]
            gr = g_ref[rows, :]
            dyv = dy_ref[rows, :].astype(F32)
            rstd = lax.rsqrt(jnp.mean(ov * ov, axis=-1, keepdims=True) + EPS)
            ohat = ov * rstd
            sg = _sigmoid(gr)
            dg_raw = dyv * (ohat * wnv) * (sg * (1.0 + gr * (1.0 - sg)))
            don = dyv * (gr * sg)
            dwn_ref[...] += jnp.sum(don * ohat, axis=0, keepdims=True)
            gd = don * wnv
            do = rstd * (gd - ohat * jnp.mean(gd * ohat, axis=-1, keepdims=True))
            do_b = do.astype(BF16)
            qr = q_ref[rows, :]
            q, sf, fg = _hg_gates(qr, f_ref[rows, :], lbv)
            v = i_ref[rows, :].astype(BF16)
            k, e_q, e_k, e_0, e_l, e_last = _hg_chunk_terms(q, fg, tril_b, low_half)
            qi, qi_lo, _ = _split3(q * e_q)
            ki, ki_lo, _ = _split3(k * e_k)
            q0 = (q * e_0).astype(BF16)
            kl = (k * e_l).astype(BF16)
            st = st_ref[n]
            st_b = st.astype(BF16)
            ds = ds_scr[...]
            ds_b = ds.astype(BF16)
            a_b = jnp.where(tril, _dot(qi, ki, NT), 0.0).astype(BF16)
            da_b = jnp.where(tril, _dot(do_b, v, NT), 0.0).astype(BF16)
            dq = _dot(do_b, st_b, NN) * e_0 + (_dot(da_b, ki, NN) + _dot(da_b, ki_lo, NN)) * e_q
            dk_state = _dot(v, ds_b, NN) * e_l
            dk = (_dot(da_b, qi, TN) + _dot(da_b, qi_lo, TN)) * e_k + dk_state
            dv = _dot(a_b, do_b, TN) + _dot(kl, ds_b, NT)
            ds_scr[...] = ds * e_last + _dot(do_b, q0, TN)
            d_last = jnp.sum(dk_state * k, axis=0, keepdims=True) + jnp.sum(ds * st, axis=0, keepdims=True) * e_last
            dlogf = _tri_dot(tril_b, q * dq - k * dk, TN) + d_last
            dfg = dlogf / fg - dk
            dlb_scr[...] += jnp.sum(dfg * (1.0 - sf), axis=0, keepdims=True)
            sq = _sigmoid(qr)
            dp_ref[0, rows, :] = (dq * (HG_DK ** -0.5) * (sq * (1.0 + qr * (1.0 - sq)))).astype(BF16)
            dp_ref[1, rows, :] = (dfg * (1.0 - lbv) * sf * (1.0 - sf)).astype(BF16)
            dp_ref[2, rows, :] = dv.astype(BF16)
            dp_ref[3, rows, :] = dg_raw.astype(BF16)
            return carry

        lax.fori_loop(0, nc, chunk, 0)
        d0 = dlb_scr[...] * lbv * (1.0 - lbv)
        dl_ref[0:1, :] = d0
        dl_ref[1:2, :] = -d0

    vec = pl.BlockSpec((2, HG_DK), lambda h: (0, h))
    one = pl.BlockSpec((1, HG_DK), lambda h: (0, 0))
    col = pl.BlockSpec((t, HG_DK), lambda h: (0, h))
    return pl.pallas_call(
        body,
        name="hgrn_bwd",
        grid=(HG_HEADS,),
        in_specs=_hg_specs(t, (0, HG_HEADS, 2 * HG_HEADS, 3 * HG_HEADS))
        + [vec, one, col, pl.BlockSpec((None, nc, HG_DK, HG_DK), lambda h: (h, 0, 0, 0)), col],
        out_specs=[pl.BlockSpec((4, t, HG_DK), lambda h: (0, 0, h)), vec, one],
        out_shape=[
            jax.ShapeDtypeStruct((4, t, D_MODEL), BF16),
            jax.ShapeDtypeStruct((2, D_MODEL), F32),
            jax.ShapeDtypeStruct((1, HG_DK), F32),
        ],
        scratch_shapes=[pltpu.VMEM((HG_DK, HG_DK), F32), pltpu.VMEM((1, HG_DK), F32)],
        compiler_params=_cparams(("arbitrary",)),
    )(proj, proj, proj, proj, lb, wn, o, states, dy)


def _att_masks(n):
    tq = lax.broadcasted_iota(jnp.int32, (WINDOW, WINDOW), 0)
    sk = lax.broadcasted_iota(jnp.int32, (WINDOW, WINDOW), 1)
    valid_c = sk <= tq
    valid_p = (sk - tq) > jnp.where(n > 0, 0, WINDOW)
    dist_c = (tq - sk).astype(F32)
    dist_p = dist_c + float(WINDOW)
    return valid_p, valid_c, dist_p, dist_c


def _att_halves(x, lo, kh):
    r = pltpu.roll(x, ATT_HD, 1)
    zero = jnp.zeros_like(x)
    if kh == 0:
        return jnp.where(lo, x, r), jnp.where(lo, x, zero), jnp.where(lo, zero, r)
    return jnp.where(lo, r, x), jnp.where(lo, r, zero), jnp.where(lo, zero, x)


def _att_probs(qm, k2p, k2c, masks, slope, sink):
    valid_p, valid_c, dist_p, dist_c = masks
    sp = jnp.where(valid_p, _dot(qm, k2p, NT) * (ATT_HD ** -0.5) - slope * dist_p, NEG)
    sc = jnp.where(valid_c, _dot(qm, k2c, NT) * (ATT_HD ** -0.5) - slope * dist_c, NEG)
    m = jnp.maximum(jnp.maximum(jnp.max(sp, axis=-1, keepdims=True), jnp.max(sc, axis=-1, keepdims=True)), sink)
    ep = jnp.exp(sp - m)
    ec = jnp.exp(sc - m)
    es = jnp.exp(sink - m)
    inv = 1.0 / (jnp.sum(ep, axis=-1, keepdims=True) + jnp.sum(ec, axis=-1, keepdims=True) + es)
    return ep * inv, ec * inv, es * inv


def _attn_fwd(q, kv, sinks):
    t = q.shape[0]
    nb = t // WINDOW

    def body(sink_ref, q_ref, kvp_ref, kvc_ref, o_ref):
        n = pl.program_id(0)
        masks = _att_masks(n)
        lo = lax.broadcasted_iota(jnp.int32, (WINDOW, LANES), 1) < ATT_HD
        for kh in range(ATT_KVH):
            k2p, _, _ = _att_halves(kvp_ref[:, 0:LANES], lo, kh)
            k2c, _, _ = _att_halves(kvc_ref[:, 0:LANES], lo, kh)
            _, vlo_p, vhi_p = _att_halves(kvp_ref[:, LANES:2 * LANES], lo, kh)
            _, vlo_c, vhi_c = _att_halves(kvc_ref[:, LANES:2 * LANES], lo, kh)
            for jj in range(ATT_GROUP // 2):
                j = kh * (ATT_GROUP // 2) + jj
                qp = q_ref[:, j * LANES:(j + 1) * LANES]
                zero = jnp.zeros_like(qp)
                out = None
                for par in range(2):
                    hq = 2 * j + par
                    qm = jnp.where(lo, qp, zero) if par == 0 else jnp.where(lo, zero, qp)
                    pp, pc, _ = _att_probs(qm, k2p, k2c, masks, ALIBI_SLOPES[hq], sink_ref[hq])
                    vp, vc = (vlo_p, vlo_c) if par == 0 else (vhi_p, vhi_c)
                    part = _dot(pp.astype(BF16), vp, NN) + _dot(pc.astype(BF16), vc, NN)
                    out = part if out is None else out + part
                o_ref[:, j * LANES:(j + 1) * LANES] = out.astype(BF16)

    return pl.pallas_call(
        body,
        name="attn_fwd",
        grid=(nb,),
        in_specs=[
            pl.BlockSpec(memory_space=pltpu.SMEM),
            pl.BlockSpec((WINDOW, D_MODEL), lambda n: (n, 0)),
            pl.BlockSpec((WINDOW, 2 * LANES), lambda n: (jnp.maximum(n - 1, 0), 0)),
            pl.BlockSpec((WINDOW, 2 * LANES), lambda n: (n, 0)),
        ],
        out_specs=pl.BlockSpec((WINDOW, D_MODEL), lambda n: (n, 0)),
        out_shape=jax.ShapeDtypeStruct((t, D_MODEL), BF16),
        compiler_params=_cparams(("parallel",)),
    )(sinks, q, kv, kv)


def _attn_bwd(q, kv, sinks, dout):
    t = q.shape[0]
    nb = t // WINDOW

    def body(sink_ref, q_ref, kvp_ref, kvc_ref, do_ref, dq_ref, dkv_ref, dsink_ref, carry_ref):
        n = pl.program_id(0)

        @pl.when(n == 0)
        def _():
            carry_ref[...] = jnp.zeros_like(carry_ref)
            dsink_ref[...] = jnp.zeros_like(dsink_ref)

        @pl.when(n == nb)
        def _():
            dkv_ref[...] = carry_ref[...].astype(BF16)

        @pl.when(n < nb)
        def _():
            masks = _att_masks(n)
            lo = lax.broadcasted_iota(jnp.int32, (WINDOW, LANES), 1) < ATT_HD
            lane1 = lax.broadcasted_iota(jnp.int32, (1, LANES), 1)
            dsink = jnp.zeros((1, LANES), F32)
            halves = []
            for kh in range(ATT_KVH):
                k2p, klo_p, khi_p = _att_halves(kvp_ref[:, 0:LANES], lo, kh)
                k2c, klo_c, khi_c = _att_halves(kvc_ref[:, 0:LANES], lo, kh)
                v2p, _, _ = _att_halves(kvp_ref[:, LANES:2 * LANES], lo, kh)
                v2c, _, _ = _att_halves(kvc_ref[:, LANES:2 * LANES], lo, kh)
                acc = [jnp.zeros((WINDOW, LANES), F32) for _ in range(4)]
                for jj in range(ATT_GROUP // 2):
                    j = kh * (ATT_GROUP // 2) + jj
                    cols = slice(j * LANES, (j + 1) * LANES)
                    qp = q_ref[:, cols]
                    dop = do_ref[:, cols]
                    zero = jnp.zeros_like(qp)
                    dq_pair = None
                    for par in range(2):
                        hq = 2 * j + par
                        sel = lo if par == 0 else jnp.logical_not(lo)
                        qm = jnp.where(sel, qp, zero)
                        dom = jnp.where(sel, dop, zero)
                        pp, pc, ps = _att_probs(qm, k2p, k2c, masks, ALIBI_SLOPES[hq], sink_ref[hq])
                        dpp = _dot(dom, v2p, NT)
                        dpc = _dot(dom, v2c, NT)
                        delta = jnp.sum(pp * dpp, axis=-1, keepdims=True) + jnp.sum(pc * dpc, axis=-1, keepdims=True)
                        dsp = (pp * (dpp - delta)).astype(BF16)
                        dsc = (pc * (dpc - delta)).astype(BF16)
                        dsink = dsink + jnp.where(lane1 == hq, -jnp.sum(ps * delta, axis=0, keepdims=True), 0.0)
                        kp_, kc_ = (klo_p, klo_c) if par == 0 else (khi_p, khi_c)
                        part = _dot(dsp, kp_, NN) + _dot(dsc, kc_, NN)
                        dq_pair = part if dq_pair is None else dq_pair + part
                        acc[0] = acc[0] + _dot(dsp, qm, TN)
                        acc[1] = acc[1] + _dot(dsc, qm, TN)
                        acc[2] = acc[2] + _dot(pp.astype(BF16), dom, TN)
                        acc[3] = acc[3] + _dot(pc.astype(BF16), dom, TN)
                    dq_ref[:, cols] = (dq_pair * (ATT_HD ** -0.5)).astype(BF16)
                halves.append([a + pltpu.roll(a, ATT_HD, 1) for a in acc])
            scale = ATT_HD ** -0.5
            prev = jnp.concatenate(
                [jnp.where(lo, halves[0][0], halves[1][0]) * scale, jnp.where(lo, halves[0][2], halves[1][2])], axis=1)
            cur = jnp.concatenate(
                [jnp.where(lo, halves[0][1], halves[1][1]) * scale, jnp.where(lo, halves[0][3], halves[1][3])], axis=1)
            dkv_ref[...] = (carry_ref[...] + prev).astype(BF16)
            carry_ref[...] = cur
            dsink_ref[...] += dsink

    blk = lambda n: jnp.minimum(n, nb - 1)
    return pl.pallas_call(
        body,
        name="attn_bwd",
        grid=(nb + 1,),
        in_specs=[
            pl.BlockSpec(memory_space=pltpu.SMEM),
            pl.BlockSpec((WINDOW, D_MODEL), lambda n: (blk(n), 0)),
            pl.BlockSpec((WINDOW, 2 * LANES), lambda n: (jnp.maximum(blk(n) - 1, 0), 0)),
            pl.BlockSpec((WINDOW, 2 * LANES), lambda n: (blk(n), 0)),
            pl.BlockSpec((WINDOW, D_MODEL), lambda n: (blk(n), 0)),
        ],
        out_specs=[
            pl.BlockSpec((WINDOW, D_MODEL), lambda n: (blk(n), 0)),
            pl.BlockSpec((WINDOW, 2 * LANES), lambda n: (jnp.maximum(n - 1, 0), 0)),
            pl.BlockSpec((1, LANES), lambda n: (0, 0)),
        ],
        out_shape=[
            jax.ShapeDtypeStruct((t, D_MODEL), BF16),
            jax.ShapeDtypeStruct((t, 2 * LANES), BF16),
            jax.ShapeDtypeStruct((1, LANES), F32),
        ],
        scratch_shapes=[pltpu.VMEM((WINDOW, 2 * LANES), F32)],
        compiler_params=_cparams(("arbitrary",)),
    )(sinks, q, kv, kv, dout)


def _ffn_fwd(h, norm_g, w_up, conv_w, conv_b, w_down, tag):
    xn = _rms_fwd(h, norm_g, name=f"ffn{tag}_norm")
    up = _mm_nn(xn, w_up, name=f"ffn{tag}_up")
    act = _conv_fwd(up, conv_w, conv_b, name=f"ffn{tag}_conv")
    h_out = _mm_nn(act, w_down, res=h, name=f"ffn{tag}_down")
    return h_out, (xn, up, act)


def _ffn_bwd(dh, h, norm_g, w_up, conv_w, conv_b, w_down, saved, tag, deps=()):
    xn, up, act = saved
    dw_down = _mm_tn(act, dh, 1, D_MODEL, name=f"ffn{tag}_dwdown", deps=deps)
    dact = _mm_nt(dh, w_down, name=f"ffn{tag}_dact", deps=deps)
    dup, dconv_w, dconv_b = _conv_bwd(up, conv_w, conv_b, dact, name=f"ffn{tag}_dconv")
    dw_up = _mm_tn(xn, dup, N_CHIPS, CONV_COLS, stacked=True, name=f"ffn{tag}_dwup")
    dxn = _mm_nt(dup, w_up, stacked=True, name=f"ffn{tag}_dxn")
    dh_in, dnorm = _rms_bwd(h, norm_g, dxn, dh, name=f"ffn{tag}_dnorm")
    return dh_in, dict(ffn_w_down=dw_down, ffn_w_up=dw_up, ffn_conv_w=dconv_w, ffn_conv_b=dconv_b, ffn_norm=dnorm)


def _local_step(x, target, w, fetch=lambda w, stage, after: w, hook=lambda point, dh, grads: ()):
    xn0 = _rms_fwd(x, w["hg_norm"], name="hg_norm")
    proj = _mm_nn(xn0, w["hg_w_in"], name="hg_in")
    o, y, states = _hgrn_fwd(proj, w["hg_lb"], w["hg_out_norm"])
    w = fetch(w, "layer0", y)
    h_a = _mm_nn(y, w["hg_w_out"], res=x, name="hg_out")
    h1, ffn0 = _ffn_fwd(h_a, w["ffn_norm"][0], w["ffn_w_up"][0], w["ffn_conv_w"][0], w["ffn_conv_b"][0], w["ffn_w_down"][0], 0)
    w = fetch(w, "layer1", h1)
    kvn = _rms_fwd(h1, w["kv_norm"], name="kv_norm")
    kv = _mm_nn(kvn, w["w_kv"], out_dtype=BF16, name="kv_proj")
    xa = _rms_fwd(h1, w["attn_norm"], name="attn_norm")
    qa = _mm_nn(xa, w["attn_w_q"], out_dtype=BF16, name="attn_q")
    ao = _attn_fwd(qa, kv, w["attn_sinks"])
    h_b = _mm_nn(ao, w["attn_w_o"], res=h1, name="attn_o")
    h2, ffn1 = _ffn_fwd(h_b, w["ffn_norm"][1], w["ffn_w_up"][1], w["ffn_conv_w"][1], w["ffn_conv_b"][1], w["ffn_w_down"][1], 1)
    dh2, d_final, loss = _loss_head(h2, w["final_norm"], target)

    dh_b, g1 = _ffn_bwd(dh2, h_b, w["ffn_norm"][1], w["ffn_w_up"][1], w["ffn_conv_w"][1], w["ffn_conv_b"][1], w["ffn_w_down"][1], ffn1, 1)
    deps = hook("ffn1", dh_b, g1)
    dw_o = _mm_tn(ao, dh_b, 1, D_MODEL, name="attn_dwo", deps=deps)
    dao = _mm_nt(dh_b, w["attn_w_o"], out_dtype=BF16, name="attn_dao", deps=deps)
    dqa, dkv, dsinks = _attn_bwd(qa, kv, w["attn_sinks"], dao)
    dw_q = _mm_tn(xa, dqa, 1, D_MODEL, name="attn_dwq")
    dxa = _mm_nt(dqa, w["attn_w_q"], name="attn_dxa")
    dh1, d_attn_norm = _rms_bwd(h1, w["attn_norm"], dxa, dh_b, name="attn_dnorm")
    dw_kv = _mm_tn(kvn, dkv, 1, 2 * LANES, name="kv_dw")
    dkvn = _mm_nt(dkv, w["w_kv"], name="kv_dx")
    dh1, d_kv_norm = _rms_bwd(h1, w["kv_norm"], dkvn, dh1, name="kv_dnorm")
    deps = hook("attn", dh1, dict(attn_w_o=dw_o, attn_w_q=dw_q, w_kv=dw_kv))
    dh_a, g0 = _ffn_bwd(dh1, h_a, w["ffn_norm"][0], w["ffn_w_up"][0], w["ffn_conv_w"][0], w["ffn_conv_b"][0], w["ffn_w_down"][0], ffn0, 0, deps)
    deps = hook("ffn0", dh_a, g0)
    dw_out = _mm_tn(y, dh_a, 1, D_MODEL, name="hg_dwout", deps=deps)
    dy = _mm_nt(dh_a, w["hg_w_out"], out_dtype=BF16, name="hg_dy", deps=deps)
    dproj, dlb, d_out_norm = _hgrn_bwd(proj, w["hg_lb"], w["hg_out_norm"], o, states, dy)
    deps = hook("hgrn", dproj, None)
    dw_in = _mm_tn(xn0, dproj, N_CHIPS, D_MODEL, stacked=True, name="hg_dwin", deps=deps)
    deps = hook("hg_w", dproj, dict(hg_w_out=dw_out, hg_w_in=dw_in))
    dxn0 = _mm_nt(dproj, w["hg_w_in"], stacked=True, name="hg_dxn", deps=deps)
    dx, d_hg_norm = _rms_bwd(x, w["hg_norm"], dxn0, dh_a, name="hg_dnorm")

    grads = dict(
        hg_norm=d_hg_norm, hg_w_in=dw_in, hg_lb=dlb, hg_out_norm=d_out_norm, hg_w_out=dw_out,
        kv_norm=d_kv_norm, w_kv=dw_kv, attn_norm=d_attn_norm, attn_w_q=dw_q, attn_sinks=dsinks, attn_w_o=dw_o,
        final_norm=d_final,
    )
    for name in g0:
        grads[name] = [g0[name], g1[name]]
    return loss, dx, grads


ANY = pl.BlockSpec(memory_space=pl.ANY)


def _place():
    x, y, c = lax.axis_index("x"), lax.axis_index("y"), lax.axis_index("c")
    chips = [(1 - x, y), (x, 1 - y), (1 - x, 1 - y)]
    return x, y, c, chips


def _rcopy(src, dst, send_sem, recv_sem, to):
    return pltpu.make_async_remote_copy(src_ref=src, dst_ref=dst, send_sem=send_sem, recv_sem=recv_sem, device_id=to, device_id_type=MESH)


HBM = pl.BlockSpec(memory_space=pltpu.HBM)
SEM = pl.BlockSpec(memory_space=pltpu.SEMAPHORE)
EFFECT = pltpu.SideEffectType.DATAFLOW_SIDE_EFFECTING


def _in_hbm(a):
    return pltpu.with_memory_space_constraint(a, pltpu.HBM)


def _place_shard(shard, place, dtype, name):
    r, cols = shard.shape
    tr = _pick(r, ELEM_ROWS)

    def body(place_ref, s_ref, o_ref):
        o_ref[...] = s_ref[...].astype(o_ref.dtype)

    return pl.pallas_call(
        body,
        name=name,
        grid_spec=pltpu.PrefetchScalarGridSpec(
            num_scalar_prefetch=1,
            grid=(r // tr,),
            in_specs=[pl.BlockSpec((tr, cols), lambda i, place_ref: (i, 0))],
            out_specs=pl.BlockSpec((None, tr, cols), lambda i, place_ref: (place_ref[0], i, 0)),
        ),
        out_shape=jax.ShapeDtypeStruct((N_CHIPS, r, cols), dtype),
        compiler_params=_cparams(("parallel",)),
    )(place, shard)


def _start_copies(name, bufs, n_sem, copies):
    n = len(bufs)

    def body(*refs):
        for cp in copies(refs[:n], refs[n], refs[n + 1]):
            cp.start()
        refs[-1][...] = jnp.zeros_like(refs[-1])

    outs = pl.pallas_call(
        body,
        name=name,
        in_specs=[HBM] * n,
        out_specs=[SEM, SEM] + [HBM] * n + [pl.BlockSpec(memory_space=pltpu.VMEM)],
        out_shape=[pltpu.SemaphoreType.DMA((n_sem,)), pltpu.SemaphoreType.DMA((n_sem,))] + [pltpu.HBM(b.shape, b.dtype) for b in bufs]
        + [jax.ShapeDtypeStruct((SUBLANES, LANES), F32)],
        input_output_aliases={i: 2 + i for i in range(n)},
        compiler_params=pltpu.CompilerParams(has_side_effects=EFFECT),
    )(*[_in_hbm(b) for b in bufs])
    return outs[0], outs[1], list(outs[2:-1]), outs[-1]


def _wait_copies(name, bufs, send_sems, recv_sems, after, copies):
    n = len(bufs)

    def body(*refs):
        for cp in copies(refs[:n], refs[n], refs[n + 1]):
            cp.wait_send()
            cp.wait_recv()

    return pl.pallas_call(
        body,
        name=name,
        in_specs=[HBM] * n + [SEM, SEM, ANY],
        out_specs=[HBM] * n,
        out_shape=[pltpu.HBM(b.shape, b.dtype) for b in bufs],
        input_output_aliases={i: i for i in range(n)},
        compiler_params=pltpu.CompilerParams(has_side_effects=EFFECT),
    )(*bufs, send_sems, recv_sems, after)


def _gather_copies(first, count):
    def copies(refs, send_sems, recv_sems):
        x, y, c, chips = _place()
        me = 2 * x + y
        out = []
        for i in range(count):
            for j, (px, py) in enumerate(chips):
                k = 3 * (first + i) + j
                out.append(_rcopy(refs[i].at[me], refs[i].at[me], send_sems.at[k], recv_sems.at[k], (px, py, c)))
        return out

    return copies


def _swap_copies(n):
    def copies(refs, send_sems, recv_sems):
        x, y, c, _ = _place()
        out = []
        for i in range(n):
            h = refs[i].shape[1] // 2
            out.append(_rcopy(refs[i].at[:, pl.ds((1 - c) * h, h)], refs[n + i], send_sems.at[i], recv_sems.at[i], (x, y, 1 - c)))
        return out

    return copies


def _partial_copies(n):
    def copies(refs, send_sems, recv_sems):
        x, y, c, chips = _place()
        out = []
        for i in range(n):
            for j, (px, py) in enumerate(chips):
                out.append(_rcopy(refs[i].at[2 * px + py], refs[n + i].at[j], send_sems.at[3 * i + j], recv_sems.at[3 * i + j], (px, py, c)))
        return out

    return copies


def _share_copies(n):
    def copies(refs, send_sems, recv_sems):
        x, y, c, _ = _place()
        return [_rcopy(refs[i].at[c], refs[i].at[c], send_sems.at[i], recv_sems.at[i], (x, y, 1 - c)) for i in range(n)]

    return copies


def _allreduce_small(vec):
    rows = vec.shape[0]

    def body(v_ref, o_ref, buf, send_sems, recv_sems):
        x, y, c, _ = _place()
        me = 4 * x + 2 * y + c
        buf[me] = v_ref[...]
        copies = []
        for k in range(1, N_DEV):
            peer = (x ^ (k >> 2), y ^ ((k >> 1) & 1), c ^ (k & 1))
            cp = _rcopy(v_ref, buf.at[me], send_sems.at[k - 1], recv_sems.at[k - 1], peer)
            cp.start()
            copies.append(cp)
        for cp in copies:
            cp.wait()
        acc = buf[0]
        for d in range(1, N_DEV):
            acc = acc + buf[d]
        o_ref[...] = acc

    return pl.pallas_call(
        body,
        name="allreduce_small",
        in_specs=[pl.BlockSpec(memory_space=pltpu.VMEM)],
        out_specs=pl.BlockSpec(memory_space=pltpu.VMEM),
        out_shape=jax.ShapeDtypeStruct(vec.shape, F32),
        scratch_shapes=[pltpu.VMEM((N_DEV, rows, LANES), F32), pltpu.SemaphoreType.DMA((N_DEV - 1,)), pltpu.SemaphoreType.DMA((N_DEV - 1,))],
        compiler_params=pltpu.CompilerParams(vmem_limit_bytes=VMEM_LIMIT_BYTES),
    )(vec)


class _Reduction:
    def __init__(self, tag, grads, place, core):
        self.tag, self.n, self.place, self.core = tag, len(grads), place, core
        lands = [lax.empty((N_CHIPS, g.shape[1] // 2, g.shape[2]), F32) for g in grads]
        self._start("swap", list(grads) + lands, self.n, _swap_copies(self.n))

    def _start(self, stage, bufs, n_sem, copies):
        *self.flight, self.token = _start_copies(f"rs_{stage}_start_{self.tag}", bufs, n_sem, copies)

    def _landed(self, stage, after, copies):
        send_sems, recv_sems, bufs = self.flight
        return _wait_copies(f"rs_{stage}_wait_{self.tag}", bufs, send_sems, recv_sems, after, copies)

    def to_chips(self, after):
        n = self.n
        bufs = self._landed("swap", after, _swap_copies(n))
        sums = [_add_core_halves(g, o, self.core, name=f"rs_add_core_{self.tag}_{i}") for i, (g, o) in enumerate(zip(bufs[:n], bufs[n:]))]
        self.mine = [f for f, _ in sums]
        parts = [b for _, b in sums]
        lands = [lax.empty((3,) + p.shape[1:], BF16) for p in parts]
        self._start("send", parts + lands, 3 * n, _partial_copies(n))

    def to_core(self, after):
        n = self.n
        bufs = self._landed("send", after, _partial_copies(n))
        halves = [_add_chip_partials(f, o, self.place, name=f"rs_add_chip_{self.tag}_{i}") for i, (f, o) in enumerate(zip(self.mine, bufs[n:]))]
        self._start("share", halves, n, _share_copies(n))

    def finish(self, after):
        return [b.reshape((-1,) + b.shape[2:]) for b in self._landed("share", after, _share_copies(self.n))]


ELEM_ROWS = (256, 176, 128, 64, 32, 16, 8)


def _add_core_halves(grad, got, c, name):
    s, r, cols = grad.shape
    h = r // 2
    tr = _pick(h, ELEM_ROWS)

    def body(c_ref, g_ref, o_ref, f_ref, b_ref):
        acc = g_ref[...] + o_ref[...]
        f_ref[...] = acc
        b_ref[...] = acc.astype(BF16)

    blk = pl.BlockSpec((None, tr, cols), lambda k, i, c_ref: (k, i, 0))
    return pl.pallas_call(
        body,
        name=name,
        grid_spec=pltpu.PrefetchScalarGridSpec(
            num_scalar_prefetch=1,
            grid=(s, h // tr),
            in_specs=[pl.BlockSpec((None, None, tr, cols), lambda k, i, c_ref: (k, c_ref[0], i, 0)), blk],
            out_specs=[blk, blk],
        ),
        out_shape=[jax.ShapeDtypeStruct((s, h, cols), F32), jax.ShapeDtypeStruct((s, h, cols), BF16)],
        compiler_params=_cparams(("parallel", "parallel")),
    )(c, grad.reshape(s, 2, h, cols), got)


def _add_chip_partials(mine, got, place, name):
    _, h, cols = mine.shape
    tr = _pick(h, ELEM_ROWS)

    def body(place_ref, m_ref, g_ref, o_ref):
        acc = m_ref[...]
        for j in range(3):
            acc = acc + g_ref[j].astype(F32)
        o_ref[...] = acc

    return pl.pallas_call(
        body,
        name=name,
        grid_spec=pltpu.PrefetchScalarGridSpec(
            num_scalar_prefetch=1,
            grid=(h // tr,),
            in_specs=[
                pl.BlockSpec((None, tr, cols), lambda i, place_ref: (place_ref[0], i, 0)),
                pl.BlockSpec((3, tr, cols), lambda i, place_ref: (0, i, 0)),
            ],
            out_specs=pl.BlockSpec((None, tr, cols), lambda i, place_ref: (place_ref[1], i, 0)),
        ),
        out_shape=jax.ShapeDtypeStruct((2, h, cols), F32),
        compiler_params=_cparams(("parallel",)),
    )(place, mine, got)


def _adamw_math(w, m, v, g):
    nm = ADAM_B1 * m + (1.0 - ADAM_B1) * g
    nv = ADAM_B2 * v + (1.0 - ADAM_B2) * (g * g)
    m_hat = nm * (1.0 / (1.0 - ADAM_B1 ** ADAM_STEP))
    v_hat = nv * (1.0 / (1.0 - ADAM_B2 ** ADAM_STEP))
    return -ADAM_LR * (m_hat / (jnp.sqrt(v_hat) + ADAM_EPS) + ADAM_WD * w), nm, nv


def _adamw_layer(w, m, v, g, layer, prev, name):
    nl, r, cols = w.shape
    tr = _pick(r, ELEM_ROWS)

    def body(w_ref, m_ref, v_ref, g_ref, *rest):
        go_ref, d_ref, nm_ref, nv_ref = rest[-4:]
        gv = g_ref[...]
        d_ref[...], nm_ref[...], nv_ref[...] = _adamw_math(w_ref[...], m_ref[...], v_ref[...], gv)
        go_ref[...] = gv

    lay = pl.BlockSpec((None, tr, cols), lambda i: (layer, i, 0))
    return pl.pallas_call(
        body,
        name=name,
        grid=(r // tr,),
        in_specs=[lay] * 3 + [pl.BlockSpec((tr, cols), lambda i: (i, 0))] + ([ANY] * 4 if prev else []),
        out_specs=[lay] * 4,
        out_shape=[jax.ShapeDtypeStruct((nl, r, cols), F32)] * 4,
        input_output_aliases={4 + k: k for k in range(4)} if prev else {},
        compiler_params=_cparams(("parallel",)),
    )(w, m, v, g, *(prev or ()))


def _adamw(w, m, v, g, name):
    r, cols = w.shape
    tr = _pick(r, ELEM_ROWS)

    def body(w_ref, m_ref, v_ref, g_ref, d_ref, nm_ref, nv_ref):
        d_ref[...], nm_ref[...], nv_ref[...] = _adamw_math(w_ref[...], m_ref[...], v_ref[...], g_ref[...])

    blk = pl.BlockSpec((tr, cols), lambda i: (i, 0))
    return pl.pallas_call(
        body,
        name=name,
        grid=(r // tr,),
        in_specs=[blk] * 4,
        out_specs=[blk] * 3,
        out_shape=[jax.ShapeDtypeStruct((r, cols), F32)] * 3,
        compiler_params=_cparams(("parallel",)),
    )(w, m, v, g)


SMALL_COLS = 384
SMALL_ROWS = 16


def _pad_rows(flat, rows, cols):
    return jnp.pad(flat, (0, rows * cols - flat.shape[0])).reshape(rows, cols)


def kernel(x, hg_norm, hg_w_in, hg_lb_logits, hg_out_norm, hg_w_out, kv_norm, w_kv, attn_norm, attn_w_q, attn_sinks, attn_w_o, ffn_norm, ffn_w_up, ffn_conv_w, ffn_conv_b, ffn_w_down, final_norm, loss_target, m_hg_norm, m_hg_w_in, m_hg_lb_logits, m_hg_out_norm, m_hg_w_out, m_kv_norm, m_w_kv, m_attn_norm, m_attn_w_q, m_attn_sinks, m_attn_w_o, m_ffn_norm, m_ffn_w_up, m_ffn_conv_w, m_ffn_conv_b, m_ffn_w_down, m_final_norm, v_hg_norm, v_hg_w_in, v_hg_lb_logits, v_hg_out_norm, v_hg_w_out, v_kv_norm, v_w_kv, v_attn_norm, v_attn_w_q, v_attn_sinks, v_attn_w_o, v_ffn_norm, v_ffn_w_up, v_ffn_conv_w, v_ffn_conv_b, v_ffn_w_down, v_final_norm):
    wts = dict(hg_norm=hg_norm, hg_w_in=hg_w_in, hg_lb_logits=hg_lb_logits, hg_out_norm=hg_out_norm, hg_w_out=hg_w_out, kv_norm=kv_norm, w_kv=w_kv, attn_norm=attn_norm, attn_w_q=attn_w_q, attn_sinks=attn_sinks, attn_w_o=attn_w_o, ffn_norm=ffn_norm, ffn_w_up=ffn_w_up, ffn_conv_w=ffn_conv_w, ffn_conv_b=ffn_conv_b, ffn_w_down=ffn_w_down, final_norm=final_norm)
    mom1 = dict(hg_norm=m_hg_norm, hg_w_in=m_hg_w_in, hg_lb_logits=m_hg_lb_logits, hg_out_norm=m_hg_out_norm, hg_w_out=m_hg_w_out, kv_norm=m_kv_norm, w_kv=m_w_kv, attn_norm=m_attn_norm, attn_w_q=m_attn_w_q, attn_sinks=m_attn_sinks, attn_w_o=m_attn_w_o, ffn_norm=m_ffn_norm, ffn_w_up=m_ffn_w_up, ffn_conv_w=m_ffn_conv_w, ffn_conv_b=m_ffn_conv_b, ffn_w_down=m_ffn_w_down, final_norm=m_final_norm)
    mom2 = dict(hg_norm=v_hg_norm, hg_w_in=v_hg_w_in, hg_lb_logits=v_hg_lb_logits, hg_out_norm=v_hg_out_norm, hg_w_out=v_hg_w_out, kv_norm=v_kv_norm, w_kv=v_w_kv, attn_norm=v_attn_norm, attn_w_q=v_attn_w_q, attn_sinks=v_attn_sinks, attn_w_o=v_attn_w_o, ffn_norm=v_ffn_norm, ffn_w_up=v_ffn_w_up, ffn_conv_w=v_ffn_conv_w, ffn_conv_b=v_ffn_conv_b, ffn_w_down=v_ffn_w_down, final_norm=v_final_norm)
    names = list(wts)
    chip = 2 * lax.axis_index("x") + lax.axis_index("y")
    core = lax.axis_index("c")
    core_arr = jnp.reshape(core, (1,)).astype(jnp.int32)
    fs = D_FF // N_CHIPS
    ds = D_MODEL // N_CHIPS

    place_arr = jnp.stack([chip, core]).astype(jnp.int32)
    small = jnp.concatenate([hg_norm.reshape(-1), hg_lb_logits.reshape(-1), ffn_conv_w.reshape(-1)])
    n_small = small.shape[0]
    shards = [
        ("small", _pad_rows(small, SMALL_ROWS, SMALL_COLS), F32), ("hg_w_in", hg_w_in[0], BF16),
        ("hg_w_out", hg_w_out[0], BF16), ("ffn_w_up0", ffn_w_up[0], BF16), ("ffn_w_down0", ffn_w_down[0], BF16),
        ("w_kv", w_kv, BF16), ("attn_w_q", attn_w_q[0], BF16), ("attn_w_o", attn_w_o[0], BF16),
        ("ffn_w_up1", ffn_w_up[1], BF16), ("ffn_w_down1", ffn_w_down[1], BF16),
    ]
    n_first = 2
    stages = dict(first=(0, 0, n_first), layer0=(1, 0, 3), layer1=(1, 3, 8))
    placed = [_place_shard(s, place_arr, dt, name=f"place_{nm}") for nm, s, dt in shards[:n_first]]
    flights = [_start_copies("gather_start_first", placed, 3 * n_first, _gather_copies(0, n_first))]
    placed = [_place_shard(s, place_arr, dt, name=f"place_{nm}") for nm, s, dt in shards[n_first:]]
    flights.append(_start_copies("gather_start_rest", placed, 3 * len(placed), _gather_copies(0, len(placed))))

    def fetch(w, stage, after):
        call, lo, hi = stages[stage]
        send_sems, recv_sems, bufs, _ = flights[call]
        got = _wait_copies(f"gather_wait_{stage}", bufs[lo:hi], send_sems, recv_sems, after, _gather_copies(lo, hi - lo))
        w = dict(w)
        if stage == "first":
            g_small = got[0].reshape(N_CHIPS, -1)[:, :n_small]
            conv_w = g_small[:, 3 * ds:].reshape(N_CHIPS, 2, 3, fs).transpose(1, 2, 0, 3).reshape(2, 3, D_FF)
            w.update(
                hg_norm=g_small[:, :ds].reshape(1, D_MODEL),
                hg_lb=g_small[:, ds:3 * ds].reshape(N_CHIPS, 2, ds).transpose(1, 0, 2).reshape(2, D_MODEL),
                ffn_conv_w=[conv_w[0], conv_w[1]], hg_w_in=got[1],
            )
        elif stage == "layer0":
            w.update(hg_w_out=got[0].reshape(1, D_MODEL, D_MODEL), ffn_w_up=[got[1], None], ffn_w_down=[got[2].reshape(1, D_FF, D_MODEL), None])
        else:
            w.update(
                w_kv=got[0].reshape(1, D_MODEL, 2 * LANES), attn_w_q=got[1].reshape(1, D_MODEL, D_MODEL),
                attn_w_o=got[2].reshape(1, D_MODEL, D_MODEL), ffn_w_up=[w["ffn_w_up"][0], got[3]],
                ffn_w_down=[w["ffn_w_down"][0], got[4].reshape(1, D_FF, D_MODEL)],
            )
        return w

    whole = dict(
        hg_out_norm=hg_out_norm, kv_norm=kv_norm.reshape(1, D_MODEL), attn_norm=attn_norm, attn_sinks=attn_sinks.reshape(ATT_QH),
        ffn_norm=[ffn_norm[0:1], ffn_norm[1:2]], ffn_conv_b=[ffn_conv_b[0:1], ffn_conv_b[1:2]], final_norm=final_norm.reshape(1, D_MODEL),
    )
    whole = fetch(whole, "first", flights[1][3])

    red, layer1 = {}, {}

    def by_rows(g, rows):
        return g.reshape(N_CHIPS, rows, g.shape[2])

    def hook(point, dh, grads):
        if point == "ffn1":
            red["ffn1"] = _Reduction("ffn1", [by_rows(grads["ffn_w_down"], fs), grads["ffn_w_up"]], place_arr, core_arr)
            return (red["ffn1"].token,)
        if point == "attn":
            red["ffn1"].to_chips(dh)
            layer1.update(grads)
            return (red["ffn1"].token,)
        if point == "ffn0":
            group = [by_rows(layer1["attn_w_o"], ds), by_rows(layer1["attn_w_q"], ds), by_rows(layer1["w_kv"], ds),
                     by_rows(grads["ffn_w_down"], fs), grads["ffn_w_up"]]
            red["mid"] = _Reduction("mid", group, place_arr, core_arr)
            return (red["mid"].token,)
        if point == "hgrn":
            red["ffn1"].to_core(dh)
            red["mid"].to_chips(dh)
            return (red["ffn1"].token, red["mid"].token)
        red["hg"] = _Reduction("hg", [by_rows(grads["hg_w_out"], ds), grads["hg_w_in"]], place_arr, core_arr)
        return (red["hg"].token,)

    loss, dx, grads = _local_step(x[0], loss_target[0], whole, fetch, hook)
    red["hg"].to_chips(dx)

    small_parts = [
        loss.reshape(-1), grads["hg_out_norm"].reshape(-1), grads["attn_sinks"].reshape(-1), grads["kv_norm"].reshape(-1),
        grads["attn_norm"].reshape(-1), grads["ffn_norm"][0].reshape(-1), grads["ffn_norm"][1].reshape(-1),
        grads["ffn_conv_b"][0].reshape(-1), grads["ffn_conv_b"][1].reshape(-1), grads["final_norm"].reshape(-1),
        grads["hg_norm"].reshape(-1), grads["hg_lb"].reshape(-1), grads["ffn_conv_w"][0].reshape(-1), grads["ffn_conv_w"][1].reshape(-1),
    ]
    sizes = [p.shape[0] for p in small_parts]
    flat = jnp.concatenate(small_parts)
    rows = -(-flat.shape[0] // (SUBLANES * LANES)) * SUBLANES
    summed = _allreduce_small(_pad_rows(flat, rows, LANES) + red["hg"].token[0, 0]).reshape(-1)
    offs = [0]
    for sz in sizes:
        offs.append(offs[-1] + sz)
    sm = [summed[offs[i]:offs[i + 1]] for i in range(len(sizes))]
    loss_out = sm[0][0]
    conv_w_full = jnp.stack([sm[12].reshape(3, D_FF), sm[13].reshape(3, D_FF)])
    small_grads = dict(
        hg_out_norm=sm[1].reshape(1, HG_DK), attn_sinks=sm[2][:ATT_QH].reshape(1, ATT_QH), kv_norm=sm[3], attn_norm=sm[4].reshape(1, D_MODEL),
        ffn_norm=jnp.stack([sm[5], sm[6]]), ffn_conv_b=jnp.stack([sm[7], sm[8]]), final_norm=sm[9],
        hg_norm=lax.dynamic_slice(sm[10].reshape(1, D_MODEL), (0, chip * ds), (1, ds)),
        hg_lb_logits=lax.dynamic_slice(sm[11].reshape(2, D_MODEL), (0, chip * ds), (2, ds)),
        ffn_conv_w=lax.dynamic_slice(conv_w_full, (0, 0, chip * fs), (2, 3, fs)),
    )

    out_g, out_d, out_m, out_v = {}, {}, {}, {}

    def update(name, g2):
        shape = wts[name].shape
        d2, m2, v2 = _adamw(wts[name].reshape(g2.shape), mom1[name].reshape(g2.shape), mom2[name].reshape(g2.shape), g2, name=f"adamw_{name}")
        out_g[name], out_d[name], out_m[name], out_v[name] = g2.reshape(shape), d2.reshape(shape), m2.reshape(shape), v2.reshape(shape)
        return d2

    def update_layer(name, g2, layer, prev):
        res = _adamw_layer(wts[name], mom1[name], mom2[name], g2, layer, prev, name=f"adamw_{name}{layer}")
        out_g[name], out_d[name], out_m[name], out_v[name] = res
        return res

    g_down1, g_up1 = red["ffn1"].finish(summed)
    down1 = update_layer("ffn_w_down", g_down1, 1, None)
    up1 = update_layer("ffn_w_up", g_up1, 1, None)
    red["mid"].to_core(up1[1])
    g_o, g_q, g_kv, g_down0, g_up0 = red["mid"].finish(up1[2])
    update("attn_w_o", g_o)
    update("attn_w_q", g_q)
    update("w_kv", g_kv)
    update_layer("ffn_w_down", g_down0, 0, down1)
    last = update_layer("ffn_w_up", g_up0, 0, up1)
    red["hg"].to_core(last[1])
    g_out, g_in = red["hg"].finish(last[2])
    update("hg_w_out", g_out)
    update("hg_w_in", g_in)

    small_names = [n for n in names if n not in out_g]
    cat = lambda d: jnp.concatenate([d[n].reshape(-1) for n in small_names])
    n_flat = sum(wts[n].size for n in small_names)
    srows = -(-n_flat // (SUBLANES * LANES)) * SUBLANES
    packed = [_pad_rows(cat(d), srows, LANES) for d in (wts, mom1, mom2, small_grads)]
    d_s, m_s, v_s = _adamw(*packed, name="adamw_small")
    off = 0
    for n in small_names:
        sz, shape = wts[n].size, wts[n].shape
        out_g[n] = small_grads[n].reshape(shape)
        out_d[n] = d_s.reshape(-1)[off:off + sz].reshape(shape)
        out_m[n] = m_s.reshape(-1)[off:off + sz].reshape(shape)
        out_v[n] = v_s.reshape(-1)[off:off + sz].reshape(shape)
        off += sz

    grad_x = dx.reshape(x.shape)
    return (loss_out, grad_x, *[out_g[n] for n in names], *[out_d[n] for n in names], *[out_m[n] for n in names], *[out_v[n] for n in names])
```

```python
import functools

import jax
import jax.numpy as jnp
from jax import lax
from jax.experimental import pallas as pl
from jax.experimental.pallas import tpu as pltpu

F32 = jnp.float32
BF16 = jnp.bfloat16
MESH = pl.DeviceIdType.MESH

EPS = 1e-6
D_MODEL = 1024
HG_HEADS = 8
HG_DK = 128
HG_CHUNK = 64
ATT_HD = 64
ATT_QH = 16
ATT_KVH = 2
ATT_GROUP = ATT_QH // ATT_KVH
WINDOW = 128
D_FF = 2816
N_CHIPS = 4
N_DEV = 8
LANES = 128
SUBLANES = 8
VMEM_LIMIT_BYTES = 56 * 1024 * 1024
NEG = -1e30
ALIBI_SLOPES = tuple(2.0 ** (-8.0 * h / ATT_QH) for h in range(1, ATT_QH + 1))

ADAM_LR = 0.001
ADAM_B1 = 0.9
ADAM_B2 = 0.999
ADAM_EPS = 1e-08
ADAM_WD = 0.01
ADAM_STEP = 10


def _cparams(sem=None):
    return pltpu.CompilerParams(dimension_semantics=sem, vmem_limit_bytes=VMEM_LIMIT_BYTES)


def _pick(n, cands):
    for c in cands:
        if n % c == 0:
            return c
    return n


def _sigmoid(x):
    return 1.0 / (1.0 + jnp.exp(-x))


def _dot(a, b, dims):
    return lax.dot_general(a, b, (dims, ((), ())), preferred_element_type=F32)


NN = ((1,), (0,))
NT = ((1,), (1,))
TN = ((0,), (0,))


def _mm_nn(a, w, res=None, out_dtype=F32, name="mm_nn"):
    m, k = a.shape
    s, _, ns = w.shape
    tm = min(m, 512)
    tn = _pick(ns, (512, 1408, 256, 128))
    npb = ns // tn

    def body(a_ref, w_ref, *rest):
        o_ref = rest[-1]
        acc = _dot(a_ref[...].astype(BF16), w_ref[...], NN)
        if res is not None:
            acc = acc + rest[0][...]
        o_ref[...] = acc.astype(o_ref.dtype)

    in_specs = [
        pl.BlockSpec((tm, k), lambda i, j: (i, 0)),
        pl.BlockSpec((None, k, tn), lambda i, j: (j // npb, 0, j % npb)),
    ]
    args = [a, w]
    if res is not None:
        in_specs.append(pl.BlockSpec((tm, tn), lambda i, j: (i, j)))
        args.append(res)
    return pl.pallas_call(
        body,
        name=name,
        grid=(m // tm, s * npb),
        in_specs=in_specs,
        out_specs=pl.BlockSpec((tm, tn), lambda i, j: (i, j)),
        out_shape=jax.ShapeDtypeStruct((m, s * ns), out_dtype),
        compiler_params=_cparams(("parallel", "parallel")),
    )(*args)


def _dy_spec(stacked, tm, tn, npb, row, kk):
    if stacked:
        return pl.BlockSpec((None, tm, tn), lambda *g: (kk(g) // npb, row(g), kk(g) % npb))
    return pl.BlockSpec((tm, tn), lambda *g: (row(g), kk(g)))


def _dep_specs(deps):
    return [pl.BlockSpec(d.shape, lambda *g: (0, 0)) for d in deps]


def _mm_nt(dy, w, stacked=False, out_dtype=F32, name="mm_nt", deps=()):
    s, k, ns = w.shape
    m = dy.shape[1] if stacked else dy.shape[0]
    tm = min(m, 512)
    tko = _pick(k, (1024, 1408, 512, 256))
    tn = _pick(ns, (1024, 1408, 512, 256))
    npb = ns // tn
    nk = s * npb

    def body(dy_ref, w_ref, *rest):
        o_ref, acc_ref = rest[-2:]
        kk = pl.program_id(2)

        @pl.when(kk == 0)
        def _():
            acc_ref[...] = jnp.zeros_like(acc_ref)

        acc_ref[...] += _dot(dy_ref[...].astype(BF16), w_ref[...], NT)

        @pl.when(kk == nk - 1)
        def _():
            o_ref[...] = acc_ref[...].astype(o_ref.dtype)

    return pl.pallas_call(
        body,
        name=name,
        grid=(m // tm, k // tko, nk),
        in_specs=[
            _dy_spec(stacked, tm, tn, npb, lambda g: g[0], lambda g: g[2]),
            pl.BlockSpec((None, tko, tn), lambda i, j, kk: (kk // npb, j, kk % npb)),
        ] + _dep_specs(deps),
        out_specs=pl.BlockSpec((tm, tko), lambda i, j, kk: (i, j)),
        out_shape=jax.ShapeDtypeStruct((m, k), out_dtype),
        scratch_shapes=[pltpu.VMEM((tm, tko), F32)],
        compiler_params=_cparams(("parallel", "parallel", "arbitrary")),
    )(dy, w, *deps)


def _mm_tn(a, dy, s, ns, stacked=False, name="mm_tn", deps=()):
    m, k = a.shape
    tm = min(m, 512)
    tk = _pick(k, (1024, 1408, 512, 256))
    tn = _pick(ns, (512, 1408, 256, 128))
    npb = ns // tn
    nm = m // tm

    def body(a_ref, dy_ref, *rest):
        o_ref, acc_ref = rest[-2:]
        mm = pl.program_id(2)

        @pl.when(mm == 0)
        def _():
            acc_ref[...] = jnp.zeros_like(acc_ref)

        acc_ref[...] += _dot(a_ref[...].astype(BF16), dy_ref[...].astype(BF16), TN)

        @pl.when(mm == nm - 1)
        def _():
            o_ref[...] = acc_ref[...]

    return pl.pallas_call(
        body,
        name=name,
        grid=(k // tk, s * npb, nm),
        in_specs=[
            pl.BlockSpec((tm, tk), lambda i, j, mm: (mm, i)),
            _dy_spec(stacked, tm, tn, npb, lambda g: g[2], lambda g: g[1]),
        ] + _dep_specs(deps),
        out_specs=pl.BlockSpec((None, tk, tn), lambda i, j, mm: (j // npb, i, j % npb)),
        out_shape=jax.ShapeDtypeStruct((s, k, ns), F32),
        scratch_shapes=[pltpu.VMEM((tk, tn), F32)],
        compiler_params=_cparams(("parallel", "parallel", "arbitrary")),
    )(a, dy, *deps)


ROW_TILE = 256


def _rms_fwd(x, g, name="rms_fwd"):
    t, d = x.shape
    r = min(t, ROW_TILE)

    def body(x_ref, g_ref, o_ref):
        xv = x_ref[...]
        rstd = lax.rsqrt(jnp.mean(xv * xv, axis=-1, keepdims=True) + EPS)
        o_ref[...] = (xv * rstd * g_ref[...]).astype(BF16)

    return pl.pallas_call(
        body,
        name=name,
        grid=(t // r,),
        in_specs=[pl.BlockSpec((r, d), lambda i: (i, 0)), pl.BlockSpec((1, d), lambda i: (0, 0))],
        out_specs=pl.BlockSpec((r, d), lambda i: (i, 0)),
        out_shape=jax.ShapeDtypeStruct((t, d), BF16),
        compiler_params=_cparams(("parallel",)),
    )(x, g)


def _rms_bwd(x, g, dxn, dres, name="rms_bwd"):
    t, d = x.shape
    r = min(t, ROW_TILE)

    def body(x_ref, g_ref, dxn_ref, dres_ref, dx_ref, dg_ref):
        @pl.when(pl.program_id(0) == 0)
        def _():
            dg_ref[...] = jnp.zeros_like(dg_ref)

        xv = x_ref[...]
        rstd = lax.rsqrt(jnp.mean(xv * xv, axis=-1, keepdims=True) + EPS)
        xhat = xv * rstd
        dxn_v = dxn_ref[...].astype(F32)
        gd = dxn_v * g_ref[...]
        dx_ref[...] = dres_ref[...] + rstd * (gd - xhat * jnp.mean(gd * xhat, axis=-1, keepdims=True))
        dg_ref[...] += jnp.sum(dxn_v * xhat, axis=0, keepdims=True)

    return pl.pallas_call(
        body,
        name=name,
        grid=(t // r,),
        in_specs=[
            pl.BlockSpec((r, d), lambda i: (i, 0)),
            pl.BlockSpec((1, d), lambda i: (0, 0)),
            pl.BlockSpec((r, d), lambda i: (i, 0)),
            pl.BlockSpec((r, d), lambda i: (i, 0)),
        ],
        out_specs=[pl.BlockSpec((r, d), lambda i: (i, 0)), pl.BlockSpec((1, d), lambda i: (0, 0))],
        out_shape=[jax.ShapeDtypeStruct((t, d), F32), jax.ShapeDtypeStruct((1, d), F32)],
        compiler_params=_cparams(("arbitrary",)),
    )(x, g, dxn, dres)


def _loss_head(h, g, target):
    t, d = h.shape
    r = min(t, ROW_TILE)

    def body(h_ref, g_ref, t_ref, dh_ref, dg_ref, loss_ref):
        @pl.when(pl.program_id(0) == 0)
        def _():
            dg_ref[...] = jnp.zeros_like(dg_ref)
            loss_ref[...] = jnp.zeros_like(loss_ref)

        xv = h_ref[...]
        rstd = lax.rsqrt(jnp.mean(xv * xv, axis=-1, keepdims=True) + EPS)
        xhat = xv * rstd
        gv = g_ref[...]
        err = xhat * gv - t_ref[...]
        loss_ref[...] += 0.5 * jnp.sum(jnp.mean(err * err, axis=-1, keepdims=True), axis=0, keepdims=True)
        dy = err * (1.0 / d)
        gd = dy * gv
        dh_ref[...] = rstd * (gd - xhat * jnp.mean(gd * xhat, axis=-1, keepdims=True))
        dg_ref[...] += jnp.sum(dy * xhat, axis=0, keepdims=True)

    return pl.pallas_call(
        body,
        name="loss_head",
        grid=(t // r,),
        in_specs=[
            pl.BlockSpec((r, d), lambda i: (i, 0)),
            pl.BlockSpec((1, d), lambda i: (0, 0)),
            pl.BlockSpec((r, d), lambda i: (i, 0)),
        ],
        out_specs=[
            pl.BlockSpec((r, d), lambda i: (i, 0)),
            pl.BlockSpec((1, d), lambda i: (0, 0)),
            pl.BlockSpec((1, LANES), lambda i: (0, 0)),
        ],
        out_shape=[
            jax.ShapeDtypeStruct((t, d), F32),
            jax.ShapeDtypeStruct((1, d), F32),
            jax.ShapeDtypeStruct((1, LANES), F32),
        ],
        compiler_params=_cparams(("arbitrary",)),
    )(h, g, target)


CONV_ROWS = 128
CONV_COLS = 1408


def _conv_taps(x_ext, n):
    tot = x_ext.shape[0]
    g1 = pltpu.roll(x_ext, 1, 0)[tot - n:]
    g2 = pltpu.roll(x_ext, 2, 0)[tot - n:]
    return g2, g1


def _conv_fwd(up, conv_w, conv_b, name="conv_fwd"):
    t = up.shape[0]
    r = min(t, CONV_ROWS)
    tc = CONV_COLS
    ncb = D_FF // tc
    hb = r // SUBLANES

    def body(g_ref, halo_ref, v_ref, w_ref, b_ref, o_ref):
        i = pl.program_id(1)
        g0 = g_ref[...]
        halo = halo_ref[...] * jnp.where(i > 0, 1.0, 0.0)
        g2, g1 = _conv_taps(jnp.concatenate([halo, g0], axis=0), r)
        c = b_ref[...] + w_ref[0:1, :] * g2 + w_ref[1:2, :] * g1 + w_ref[2:3, :] * g0
        o_ref[...] = (c * _sigmoid(c) * v_ref[...]).astype(BF16)

    return pl.pallas_call(
        body,
        name=name,
        grid=(ncb, t // r),
        in_specs=[
            pl.BlockSpec((r, tc), lambda j, i: (i, j)),
            pl.BlockSpec((SUBLANES, tc), lambda j, i: (jnp.maximum(i * hb - 1, 0), j)),
            pl.BlockSpec((r, tc), lambda j, i: (i, ncb + j)),
            pl.BlockSpec((3, tc), lambda j, i: (0, j)),
            pl.BlockSpec((1, tc), lambda j, i: (0, j)),
        ],
        out_specs=pl.BlockSpec((r, tc), lambda j, i: (i, j)),
        out_shape=jax.ShapeDtypeStruct((t, D_FF), BF16),
        compiler_params=_cparams(("parallel", "parallel")),
    )(up, up, up, conv_w, conv_b)


def _conv_bwd(up, conv_w, conv_b, dact, name="conv_bwd"):
    t = up.shape[0]
    r = min(t, CONV_ROWS)
    tc = CONV_COLS
    ncb = D_FF // tc
    hb = r // SUBLANES
    nrt = t // r

    def body(g_ref, halo_ref, v_ref, w_ref, b_ref, da_ref, dup_ref, dw_ref, db_ref, nxt_ref):
        ii = pl.program_id(1)
        i = nrt - 1 - ii

        @pl.when(ii == 0)
        def _():
            nxt_ref[...] = jnp.zeros_like(nxt_ref)
            dw_ref[...] = jnp.zeros_like(dw_ref)
            db_ref[...] = jnp.zeros_like(db_ref)

        g0 = g_ref[...]
        halo = halo_ref[...] * jnp.where(i > 0, 1.0, 0.0)
        g2, g1 = _conv_taps(jnp.concatenate([halo, g0], axis=0), r)
        w0, w1, w2 = w_ref[0:1, :], w_ref[1:2, :], w_ref[2:3, :]
        c = b_ref[...] + w0 * g2 + w1 * g1 + w2 * g0
        sg = _sigmoid(c)
        da = da_ref[...]
        dval = da * (c * sg)
        dc = da * v_ref[...] * (sg * (1.0 + c * (1.0 - sg)))
        db_ref[...] += jnp.sum(dc, axis=0, keepdims=True)
        dw_ref[0:1, :] += jnp.sum(dc * g2, axis=0, keepdims=True)
        dw_ref[1:2, :] += jnp.sum(dc * g1, axis=0, keepdims=True)
        dw_ref[2:3, :] += jnp.sum(dc * g0, axis=0, keepdims=True)
        ext = jnp.concatenate([dc, nxt_ref[...]], axis=0)
        tot = r + SUBLANES
        d1 = pltpu.roll(ext, tot - 1, 0)[:r]
        d2 = pltpu.roll(ext, tot - 2, 0)[:r]
        dgate = w2 * dc + w1 * d1 + w0 * d2
        nxt_ref[...] = dc[:SUBLANES]
        dup_ref[0] = dgate.astype(BF16)
        dup_ref[1] = dval.astype(BF16)

    rev = lambda ii: nrt - 1 - ii
    dup, dw, db = pl.pallas_call(
        body,
        name=name,
        grid=(ncb, nrt),
        in_specs=[
            pl.BlockSpec((r, tc), lambda j, ii: (rev(ii), j)),
            pl.BlockSpec((SUBLANES, tc), lambda j, ii: (jnp.maximum(rev(ii) * hb - 1, 0), j)),
            pl.BlockSpec((r, tc), lambda j, ii: (rev(ii), ncb + j)),
            pl.BlockSpec((3, tc), lambda j, ii: (0, j)),
            pl.BlockSpec((1, tc), lambda j, ii: (0, j)),
            pl.BlockSpec((r, tc), lambda j, ii: (rev(ii), j)),
        ],
        out_specs=[
            pl.BlockSpec((2, None, r, tc), lambda j, ii: (0, j, rev(ii), 0)),
            pl.BlockSpec((3, tc), lambda j, ii: (0, j)),
            pl.BlockSpec((1, tc), lambda j, ii: (0, j)),
        ],
        out_shape=[
            jax.ShapeDtypeStruct((2, ncb, t, tc), BF16),
            jax.ShapeDtypeStruct((3, D_FF), F32),
            jax.ShapeDtypeStruct((1, D_FF), F32),
        ],
        scratch_shapes=[pltpu.VMEM((SUBLANES, tc), F32)],
        compiler_params=_cparams(("parallel", "arbitrary")),
    )(up, up, up, conv_w, conv_b, dact)
    return dup.reshape(2 * ncb, t, tc), dw, db


def _split3(x):
    x1 = x.astype(BF16)
    r1 = x - x1.astype(F32)
    x2 = r1.astype(BF16)
    x3 = (r1 - x2.astype(F32)).astype(BF16)
    return x1, x2, x3


def _tri_dot(tri, x, dims):
    x1, x2, x3 = _split3(x)
    return _dot(tri, x1, dims) + _dot(tri, x2, dims) + _dot(tri, x3, dims)


def _lower_bound(logits_ref):
    return _sigmoid(logits_ref[0:1, :] - logits_ref[1:2, :])


def _hg_gates(qr, fr, lb):
    q = qr * _sigmoid(qr) * (HG_DK ** -0.5)
    sf = _sigmoid(fr)
    fg = lb + (1.0 - lb) * sf
    return q, sf, fg


def _hg_chunk_terms(q, fg, tril_b, low_half):
    g = jnp.log(fg)
    k = 1.0 - fg
    cum = _tri_dot(tril_b, g, NN)
    c_last = jnp.sum(g, axis=0, keepdims=True)
    c_mid = jnp.sum(jnp.where(low_half, g, 0.0), axis=0, keepdims=True)
    e_q = jnp.exp(cum - c_mid)
    e_k = jnp.exp(c_mid - cum)
    e_0 = jnp.exp(cum)
    e_l = jnp.exp(c_last - cum)
    return k, e_q, e_k, e_0, e_l, jnp.exp(c_last)


def _hg_specs(t, col0s):
    return [pl.BlockSpec((t, HG_DK), functools.partial(lambda h, c0: (0, c0 + h), c0=c0)) for c0 in col0s]


def _hgrn_fwd(proj, lb, wn):
    t = proj.shape[0]
    c = HG_CHUNK
    nc = t // c

    def body(q_ref, f_ref, i_ref, g_ref, lb_ref, wn_ref, o_ref, y_ref, st_ref, s_scr):
        s_scr[...] = jnp.zeros_like(s_scr)
        lbv = _lower_bound(lb_ref)
        wnv = wn_ref[...]
        ri = lax.broadcasted_iota(jnp.int32, (c, c), 0)
        ci = lax.broadcasted_iota(jnp.int32, (c, c), 1)
        tril = ri >= ci
        tril_b = tril.astype(BF16)
        low_half = lax.broadcasted_iota(jnp.int32, (c, HG_DK), 0) < c // 2

        def chunk(n, carry):
            rows = pl.ds(pl.multiple_of(n * c, c), c)
            q, _, fg = _hg_gates(q_ref[rows, :], f_ref[rows, :], lbv)
            v = i_ref[rows, :].astype(BF16)
            k, e_q, e_k, e_0, e_l, e_last = _hg_chunk_terms(q, fg, tril_b, low_half)
            st = s_scr[...]
            st_ref[n] = st
            a = jnp.where(tril, _dot((q * e_q).astype(BF16), (k * e_k).astype(BF16), NT), 0.0)
            o = _dot((q * e_0).astype(BF16), st.astype(BF16), NT) + _dot(a.astype(BF16), v, NN)
            s_scr[...] = st * e_last + _dot(v, (k * e_l).astype(BF16), TN)
            o_ref[rows, :] = o
            rstd = lax.rsqrt(jnp.mean(o * o, axis=-1, keepdims=True) + EPS)
            gr = g_ref[rows, :]
            y_ref[rows, :] = (o * rstd * wnv * (gr * _sigmoid(gr))).astype(BF16)
            return carry

        lax.fori_loop(0, nc, chunk, 0)

    vec = pl.BlockSpec((2, HG_DK), lambda h: (0, h))
    return pl.pallas_call(
        body,
        name="hgrn_fwd",
        grid=(HG_HEADS,),
        in_specs=_hg_specs(t, (0, HG_HEADS, 2 * HG_HEADS, 3 * HG_HEADS)) + [vec, pl.BlockSpec((1, HG_DK), lambda h: (0, 0))],
        out_specs=[
            pl.BlockSpec((t, HG_DK), lambda h: (0, h)),
            pl.BlockSpec((t, HG_DK), lambda h: (0, h)),
            pl.BlockSpec((None, nc, HG_DK, HG_DK), lambda h: (h, 0, 0, 0)),
        ],
        out_shape=[
            jax.ShapeDtypeStruct((t, D_MODEL), F32),
            jax.ShapeDtypeStruct((t, D_MODEL), BF16),
            jax.ShapeDtypeStruct((HG_HEADS, nc, HG_DK, HG_DK), F32),
        ],
        scratch_shapes=[pltpu.VMEM((HG_DK, HG_DK), F32)],
        compiler_params=_cparams(("parallel",)),
    )(proj, proj, proj, proj, lb, wn)


def _hgrn_bwd(proj, lb, wn, o, states, dy):
    t = proj.shape[0]
    c = HG_CHUNK
    nc = t // c

    def body(q_ref, f_ref, i_ref, g_ref, lb_ref, wn_ref, o_ref, st_ref, dy_ref, dp_ref, dl_ref, dwn_ref, ds_scr, dlb_scr):
        @pl.when(pl.program_id(0) == 0)
        def _():
            dwn_ref[...] = jnp.zeros_like(dwn_ref)

        ds_scr[...] = jnp.zeros_like(ds_scr)
        dlb_scr[...] = jnp.zeros_like(dlb_scr)
        lbv = _lower_bound(lb_ref)
        wnv = wn_ref[...]
        ri = lax.broadcasted_iota(jnp.int32, (c, c), 0)
        ci = lax.broadcasted_iota(jnp.int32, (c, c), 1)
        tril = ri >= ci
        tril_b = tril.astype(BF16)
        low_half = lax.broadcasted_iota(jnp.int32, (c, HG_DK), 0) < c // 2

        def chunk(nn, carry):
            n = nc - 1 - nn
            rows = pl.ds(pl.multiple_of(n * c, c), c)
            ov = o_ref[rows, :]
            gr = g_ref[rows, :]
            dyv = dy_ref[rows, :].astype(F32)
            rstd = lax.rsqrt(jnp.mean(ov * ov, axis=-1, keepdims=True) + EPS)
            ohat = ov * rstd
            sg = _sigmoid(gr)
            dg_raw = dyv * (ohat * wnv) * (sg * (1.0 + gr * (1.0 - sg)))
            don = dyv * (gr * sg)
            dwn_ref[...] += jnp.sum(don * ohat, axis=0, keepdims=True)
            gd = don * wnv
            do = rstd * (gd - ohat * jnp.mean(gd * ohat, axis=-1, keepdims=True))
            do_b = do.astype(BF16)
            qr = q_ref[rows, :]
            q, sf, fg = _hg_gates(qr, f_ref[rows, :], lbv)
            v = i_ref[rows, :].astype(BF16)
            k, e_q, e_k, e_0, e_l, e_last = _hg_chunk_terms(q, fg, tril_b, low_half)
            qi, qi_lo, _ = _split3(q * e_q)
            ki, ki_lo, _ = _split3(k * e_k)
            q0 = (q * e_0).astype(BF16)
            kl = (k * e_l).astype(BF16)
            st = st_ref[n]
            st_b = st.astype(BF16)
            ds = ds_scr[...]
            ds_b = ds.astype(BF16)
            a_b = jnp.where(tril, _dot(qi, ki, NT), 0.0).astype(BF16)
            da_b = jnp.where(tril, _dot(do_b, v, NT), 0.0).astype(BF16)
            dq = _dot(do_b, st_b, NN) * e_0 + (_dot(da_b, ki, NN) + _dot(da_b, ki_lo, NN)) * e_q
            dk_state = _dot(v, ds_b, NN) * e_l
            dk = (_dot(da_b, qi, TN) + _dot(da_b, qi_lo, TN)) * e_k + dk_state
            dv = _dot(a_b, do_b, TN) + _dot(kl, ds_b, NT)
            ds_scr[...] = ds * e_last + _dot(do_b, q0, TN)
            d_last = jnp.sum(dk_state * k, axis=0, keepdims=True) + jnp.sum(ds * st, axis=0, keepdims=True) * e_last
            dlogf = _tri_dot(tril_b, q * dq - k * dk, TN) + d_last
            dfg = dlogf / fg - dk
            dlb_scr[...] += jnp.sum(dfg * (1.0 - sf), axis=0, keepdims=True)
            sq = _sigmoid(qr)
            dp_ref[0, rows, :] = (dq * (HG_DK ** -0.5) * (sq * (1.0 + qr * (1.0 - sq)))).astype(BF16)
            dp_ref[1, rows, :] = (dfg * (1.0 - lbv) * sf * (1.0 - sf)).astype(BF16)
            dp_ref[2, rows, :] = dv.astype(BF16)
            dp_ref[3, rows, :] = dg_raw.astype(BF16)
            return carry

        lax.fori_loop(0, nc, chunk, 0)
        d0 = dlb_scr[...] * lbv * (1.0 - lbv)
        dl_ref[0:1, :] = d0
        dl_ref[1:2, :] = -d0

    vec = pl.BlockSpec((2, HG_DK), lambda h: (0, h))
    one = pl.BlockSpec((1, HG_DK), lambda h: (0, 0))
    col = pl.BlockSpec((t, HG_DK), lambda h: (0, h))
    return pl.pallas_call(
        body,
        name="hgrn_bwd",
        grid=(HG_HEADS,),
        in_specs=_hg_specs(t, (0, HG_HEADS, 2 * HG_HEADS, 3 * HG_HEADS))
        + [vec, one, col, pl.BlockSpec((None, nc, HG_DK, HG_DK), lambda h: (h, 0, 0, 0)), col],
        out_specs=[pl.BlockSpec((4, t, HG_DK), lambda h: (0, 0, h)), vec, one],
        out_shape=[
            jax.ShapeDtypeStruct((4, t, D_MODEL), BF16),
            jax.ShapeDtypeStruct((2, D_MODEL), F32),
            jax.ShapeDtypeStruct((1, HG_DK), F32),
        ],
        scratch_shapes=[pltpu.VMEM((HG_DK, HG_DK), F32), pltpu.VMEM((1, HG_DK), F32)],
        compiler_params=_cparams(("arbitrary",)),
    )(proj, proj, proj, proj, lb, wn, o, states, dy)


def _att_masks(n):
    tq = lax.broadcasted_iota(jnp.int32, (WINDOW, WINDOW), 0)
    sk = lax.broadcasted_iota(jnp.int32, (WINDOW, WINDOW), 1)
    valid_c = sk <= tq
    valid_p = (sk - tq) > jnp.where(n > 0, 0, WINDOW)
    dist_c = (tq - sk).astype(F32)
    dist_p = dist_c + float(WINDOW)
    return valid_p, valid_c, dist_p, dist_c


def _att_halves(x, lo, kh):
    r = pltpu.roll(x, ATT_HD, 1)
    zero = jnp.zeros_like(x)
    if kh == 0:
        return jnp.where(lo, x, r), jnp.where(lo, x, zero), jnp.where(lo, zero, r)
    return jnp.where(lo, r, x), jnp.where(lo, r, zero), jnp.where(lo, zero, x)


def _att_probs(qm, k2p, k2c, masks, slope, sink):
    valid_p, valid_c, dist_p, dist_c = masks
    sp = jnp.where(valid_p, _dot(qm, k2p, NT) * (ATT_HD ** -0.5) - slope * dist_p, NEG)
    sc = jnp.where(valid_c, _dot(qm, k2c, NT) * (ATT_HD ** -0.5) - slope * dist_c, NEG)
    m = jnp.maximum(jnp.maximum(jnp.max(sp, axis=-1, keepdims=True), jnp.max(sc, axis=-1, keepdims=True)), sink)
    ep = jnp.exp(sp - m)
    ec = jnp.exp(sc - m)
    es = jnp.exp(sink - m)
    inv = 1.0 / (jnp.sum(ep, axis=-1, keepdims=True) + jnp.sum(ec, axis=-1, keepdims=True) + es)
    return ep * inv, ec * inv, es * inv


def _attn_fwd(q, kv, sinks):
    t = q.shape[0]
    nb = t // WINDOW

    def body(sink_ref, q_ref, kvp_ref, kvc_ref, o_ref):
        n = pl.program_id(0)
        masks = _att_masks(n)
        lo = lax.broadcasted_iota(jnp.int32, (WINDOW, LANES), 1) < ATT_HD
        for kh in range(ATT_KVH):
            k2p, _, _ = _att_halves(kvp_ref[:, 0:LANES], lo, kh)
            k2c, _, _ = _att_halves(kvc_ref[:, 0:LANES], lo, kh)
            _, vlo_p, vhi_p = _att_halves(kvp_ref[:, LANES:2 * LANES], lo, kh)
            _, vlo_c, vhi_c = _att_halves(kvc_ref[:, LANES:2 * LANES], lo, kh)
            for jj in range(ATT_GROUP // 2):
                j = kh * (ATT_GROUP // 2) + jj
                qp = q_ref[:, j * LANES:(j + 1) * LANES]
                zero = jnp.zeros_like(qp)
                out = None
                for par in range(2):
                    hq = 2 * j + par
                    qm = jnp.where(lo, qp, zero) if par == 0 else jnp.where(lo, zero, qp)
                    pp, pc, _ = _att_probs(qm, k2p, k2c, masks, ALIBI_SLOPES[hq], sink_ref[hq])
                    vp, vc = (vlo_p, vlo_c) if par == 0 else (vhi_p, vhi_c)
                    part = _dot(pp.astype(BF16), vp, NN) + _dot(pc.astype(BF16), vc, NN)
                    out = part if out is None else out + part
                o_ref[:, j * LANES:(j + 1) * LANES] = out.astype(BF16)

    return pl.pallas_call(
        body,
        name="attn_fwd",
        grid=(nb,),
        in_specs=[
            pl.BlockSpec(memory_space=pltpu.SMEM),
            pl.BlockSpec((WINDOW, D_MODEL), lambda n: (n, 0)),
            pl.BlockSpec((WINDOW, 2 * LANES), lambda n: (jnp.maximum(n - 1, 0), 0)),
            pl.BlockSpec((WINDOW, 2 * LANES), lambda n: (n, 0)),
        ],
        out_specs=pl.BlockSpec((WINDOW, D_MODEL), lambda n: (n, 0)),
        out_shape=jax.ShapeDtypeStruct((t, D_MODEL), BF16),
        compiler_params=_cparams(("parallel",)),
    )(sinks, q, kv, kv)


def _attn_bwd(q, kv, sinks, dout):
    t = q.shape[0]
    nb = t // WINDOW

    def body(sink_ref, q_ref, kvp_ref, kvc_ref, do_ref, dq_ref, dkv_ref, dsink_ref, carry_ref):
        n = pl.program_id(0)

        @pl.when(n == 0)
        def _():
            carry_ref[...] = jnp.zeros_like(carry_ref)
            dsink_ref[...] = jnp.zeros_like(dsink_ref)

        @pl.when(n == nb)
        def _():
            dkv_ref[...] = carry_ref[...].astype(BF16)

        @pl.when(n < nb)
        def _():
            masks = _att_masks(n)
            lo = lax.broadcasted_iota(jnp.int32, (WINDOW, LANES), 1) < ATT_HD
            lane1 = lax.broadcasted_iota(jnp.int32, (1, LANES), 1)
            dsink = jnp.zeros((1, LANES), F32)
            halves = []
            for kh in range(ATT_KVH):
                k2p, klo_p, khi_p = _att_halves(kvp_ref[:, 0:LANES], lo, kh)
                k2c, klo_c, khi_c = _att_halves(kvc_ref[:, 0:LANES], lo, kh)
                v2p, _, _ = _att_halves(kvp_ref[:, LANES:2 * LANES], lo, kh)
                v2c, _, _ = _att_halves(kvc_ref[:, LANES:2 * LANES], lo, kh)
                acc = [jnp.zeros((WINDOW, LANES), F32) for _ in range(4)]
                for jj in range(ATT_GROUP // 2):
                    j = kh * (ATT_GROUP // 2) + jj
                    cols = slice(j * LANES, (j + 1) * LANES)
                    qp = q_ref[:, cols]
                    dop = do_ref[:, cols]
                    zero = jnp.zeros_like(qp)
                    dq_pair = None
                    for par in range(2):
                        hq = 2 * j + par
                        sel = lo if par == 0 else jnp.logical_not(lo)
                        qm = jnp.where(sel, qp, zero)
                        dom = jnp.where(sel, dop, zero)
                        pp, pc, ps = _att_probs(qm, k2p, k2c, masks, ALIBI_SLOPES[hq], sink_ref[hq])
                        dpp = _dot(dom, v2p, NT)
                        dpc = _dot(dom, v2c, NT)
                        delta = jnp.sum(pp * dpp, axis=-1, keepdims=True) + jnp.sum(pc * dpc, axis=-1, keepdims=True)
                        dsp = (pp * (dpp - delta)).astype(BF16)
                        dsc = (pc * (dpc - delta)).astype(BF16)
                        dsink = dsink + jnp.where(lane1 == hq, -jnp.sum(ps * delta, axis=0, keepdims=True), 0.0)
                        kp_, kc_ = (klo_p, klo_c) if par == 0 else (khi_p, khi_c)
                        part = _dot(dsp, kp_, NN) + _dot(dsc, kc_, NN)
                        dq_pair = part if dq_pair is None else dq_pair + part
                        acc[0] = acc[0] + _dot(dsp, qm, TN)
                        acc[1] = acc[1] + _dot(dsc, qm, TN)
                        acc[2] = acc[2] + _dot(pp.astype(BF16), dom, TN)
                        acc[3] = acc[3] + _dot(pc.astype(BF16), dom, TN)
                    dq_ref[:, cols] = (dq_pair * (ATT_HD ** -0.5)).astype(BF16)
                halves.append([a + pltpu.roll(a, ATT_HD, 1) for a in acc])
            scale = ATT_HD ** -0.5
            prev = jnp.concatenate(
                [jnp.where(lo, halves[0][0], halves[1][0]) * scale, jnp.where(lo, halves[0][2], halves[1][2])], axis=1)
            cur = jnp.concatenate(
                [jnp.where(lo, halves[0][1], halves[1][1]) * scale, jnp.where(lo, halves[0][3], halves[1][3])], axis=1)
            dkv_ref[...] = (carry_ref[...] + prev).astype(BF16)
            carry_ref[...] = cur
            dsink_ref[...] += dsink

    blk = lambda n: jnp.minimum(n, nb - 1)
    return pl.pallas_call(
        body,
        name="attn_bwd",
        grid=(nb + 1,),
        in_specs=[
            pl.BlockSpec(memory_space=pltpu.SMEM),
            pl.BlockSpec((WINDOW, D_MODEL), lambda n: (blk(n), 0)),
            pl.BlockSpec((WINDOW, 2 * LANES), lambda n: (jnp.maximum(blk(n) - 1, 0), 0)),
            pl.BlockSpec((WINDOW, 2 * LANES), lambda n: (blk(n), 0)),
            pl.BlockSpec((WINDOW, D_MODEL), lambda n: (blk(n), 0)),
        ],
        out_specs=[
            pl.BlockSpec((WINDOW, D_MODEL), lambda n: (blk(n), 0)),
            pl.BlockSpec((WINDOW, 2 * LANES), lambda n: (jnp.maximum(n - 1, 0), 0)),
            pl.BlockSpec((1, LANES), lambda n: (0, 0)),
        ],
        out_shape=[
            jax.ShapeDtypeStruct((t, D_MODEL), BF16),
            jax.ShapeDtypeStruct((t, 2 * LANES), BF16),
            jax.ShapeDtypeStruct((1, LANES), F32),
        ],
        scratch_shapes=[pltpu.VMEM((WINDOW, 2 * LANES), F32)],
        compiler_params=_cparams(("arbitrary",)),
    )(sinks, q, kv, kv, dout)


def _ffn_fwd(h, norm_g, w_up, conv_w, conv_b, w_down, tag):
    xn = _rms_fwd(h, norm_g, name=f"ffn{tag}_norm")
    up = _mm_nn(xn, w_up, name=f"ffn{tag}_up")
    act = _conv_fwd(up, conv_w, conv_b, name=f"ffn{tag}_conv")
    h_out = _mm_nn(act, w_down, res=h, name=f"ffn{tag}_down")
    return h_out, (xn, up, act)


def _ffn_bwd(dh, h, norm_g, w_up, conv_w, conv_b, w_down, saved, tag, deps=()):
    xn, up, act = saved
    dw_down = _mm_tn(act, dh, 1, D_MODEL, name=f"ffn{tag}_dwdown", deps=deps)
    dact = _mm_nt(dh, w_down, name=f"ffn{tag}_dact", deps=deps)
    dup, dconv_w, dconv_b = _conv_bwd(up, conv_w, conv_b, dact, name=f"ffn{tag}_dconv")
    dw_up = _mm_tn(xn, dup, N_CHIPS, CONV_COLS, stacked=True, name=f"ffn{tag}_dwup")
    dxn = _mm_nt(dup, w_up, stacked=True, name=f"ffn{tag}_dxn")
    dh_in, dnorm = _rms_bwd(h, norm_g, dxn, dh, name=f"ffn{tag}_dnorm")
    return dh_in, dict(ffn_w_down=dw_down, ffn_w_up=dw_up, ffn_conv_w=dconv_w, ffn_conv_b=dconv_b, ffn_norm=dnorm)


def _local_step(x, target, w, fetch=lambda w, stage, after: w, hook=lambda point, dh, grads: ()):
    xn0 = _rms_fwd(x, w["hg_norm"], name="hg_norm")
    proj = _mm_nn(xn0, w["hg_w_in"], name="hg_in")
    o, y, states = _hgrn_fwd(proj, w["hg_lb"], w["hg_out_norm"])
    w = fetch(w, "layer0", y)
    h_a = _mm_nn(y, w["hg_w_out"], res=x, name="hg_out")
    h1, ffn0 = _ffn_fwd(h_a, w["ffn_norm"][0], w["ffn_w_up"][0], w["ffn_conv_w"][0], w["ffn_conv_b"][0], w["ffn_w_down"][0], 0)
    w = fetch(w, "layer1", h1)
    kvn = _rms_fwd(h1, w["kv_norm"], name="kv_norm")
    kv = _mm_nn(kvn, w["w_kv"], out_dtype=BF16, name="kv_proj")
    xa = _rms_fwd(h1, w["attn_norm"], name="attn_norm")
    qa = _mm_nn(xa, w["attn_w_q"], out_dtype=BF16, name="attn_q")
    ao = _attn_fwd(qa, kv, w["attn_sinks"])
    h_b = _mm_nn(ao, w["attn_w_o"], res=h1, name="attn_o")
    h2, ffn1 = _ffn_fwd(h_b, w["ffn_norm"][1], w["ffn_w_up"][1], w["ffn_conv_w"][1], w["ffn_conv_b"][1], w["ffn_w_down"][1], 1)
    dh2, d_final, loss = _loss_head(h2, w["final_norm"], target)

    dh_b, g1 = _ffn_bwd(dh2, h_b, w["ffn_norm"][1], w["ffn_w_up"][1], w["ffn_conv_w"][1], w["ffn_conv_b"][1], w["ffn_w_down"][1], ffn1, 1)
    deps = hook("ffn1", dh_b, g1)
    dw_o = _mm_tn(ao, dh_b, 1, D_MODEL, name="attn_dwo", deps=deps)
    dao = _mm_nt(dh_b, w["attn_w_o"], out_dtype=BF16, name="attn_dao", deps=deps)
    dqa, dkv, dsinks = _attn_bwd(qa, kv, w["attn_sinks"], dao)
    dw_q = _mm_tn(xa, dqa, 1, D_MODEL, name="attn_dwq")
    dxa = _mm_nt(dqa, w["attn_w_q"], name="attn_dxa")
    dh1, d_attn_norm = _rms_bwd(h1, w["attn_norm"], dxa, dh_b, name="attn_dnorm")
    dw_kv = _mm_tn(kvn, dkv, 1, 2 * LANES, name="kv_dw")
    dkvn = _mm_nt(dkv, w["w_kv"], name="kv_dx")
    dh1, d_kv_norm = _rms_bwd(h1, w["kv_norm"], dkvn, dh1, name="kv_dnorm")
    deps = hook("attn", dh1, dict(attn_w_o=dw_o, attn_w_q=dw_q, w_kv=dw_kv))
    dh_a, g0 = _ffn_bwd(dh1, h_a, w["ffn_norm"][0], w["ffn_w_up"][0], w["ffn_conv_w"][0], w["ffn_conv_b"][0], w["ffn_w_down"][0], ffn0, 0, deps)
    deps = hook("ffn0", dh_a, g0)
    dw_out = _mm_tn(y, dh_a, 1, D_MODEL, name="hg_dwout", deps=deps)
    dy = _mm_nt(dh_a, w["hg_w_out"], out_dtype=BF16, name="hg_dy", deps=deps)
    dproj, dlb, d_out_norm = _hgrn_bwd(proj, w["hg_lb"], w["hg_out_norm"], o, states, dy)
    deps = hook("hgrn", dproj, None)
    dw_in = _mm_tn(xn0, dproj, N_CHIPS, D_MODEL, stacked=True, name="hg_dwin", deps=deps)
    deps = hook("hg_w", dproj, dict(hg_w_out=dw_out, hg_w_in=dw_in))
    dxn0 = _mm_nt(dproj, w["hg_w_in"], stacked=True, name="hg_dxn", deps=deps)
    dx, d_hg_norm = _rms_bwd(x, w["hg_norm"], dxn0, dh_a, name="hg_dnorm")

    grads = dict(
        hg_norm=d_hg_norm, hg_w_in=dw_in, hg_lb=dlb, hg_out_norm=d_out_norm, hg_w_out=dw_out,
        kv_norm=d_kv_norm, w_kv=dw_kv, attn_norm=d_attn_norm, attn_w_q=dw_q, attn_sinks=dsinks, attn_w_o=dw_o,
        final_norm=d_final,
    )
    for name in g0:
        grads[name] = [g0[name], g1[name]]
    return loss, dx, grads


ANY = pl.BlockSpec(memory_space=pl.ANY)


def _place():
    x, y, c = lax.axis_index("x"), lax.axis_index("y"), lax.axis_index("c")
    chips = [(1 - x, y), (x, 1 - y), (1 - x, 1 - y)]
    return x, y, c, chips


def _rcopy(src, dst, send_sem, recv_sem, to):
    return pltpu.make_async_remote_copy(src_ref=src, dst_ref=dst, send_sem=send_sem, recv_sem=recv_sem, device_id=to, device_id_type=MESH)


HBM = pl.BlockSpec(memory_space=pltpu.HBM)
SEM = pl.BlockSpec(memory_space=pltpu.SEMAPHORE)
EFFECT = pltpu.SideEffectType.DATAFLOW_SIDE_EFFECTING


def _in_hbm(a):
    return pltpu.with_memory_space_constraint(a, pltpu.HBM)


def _place_shard(shard, place, dtype, name, deps=()):
    r, cols = shard.shape
    tr = _pick(r, ELEM_ROWS)

    def body(place_ref, s_ref, *rest):
        o_ref = rest[-1]
        o_ref[...] = s_ref[...].astype(o_ref.dtype)

    return pl.pallas_call(
        body,
        name=name,
        grid_spec=pltpu.PrefetchScalarGridSpec(
            num_scalar_prefetch=1,
            grid=(r // tr,),
            in_specs=[pl.BlockSpec((tr, cols), lambda i, place_ref: (i, 0))] + _dep_specs(deps),
            out_specs=pl.BlockSpec((None, tr, cols), lambda i, place_ref: (place_ref[0], i, 0)),
        ),
        out_shape=jax.ShapeDtypeStruct((N_CHIPS, r, cols), dtype),
        compiler_params=_cparams(("parallel",)),
    )(place, shard, *deps)


def _start_copies(name, bufs, n_sem, copies):
    n = len(bufs)

    def body(*refs):
        for cp in copies(refs[:n], refs[n], refs[n + 1]):
            cp.start()
        refs[-1][...] = jnp.zeros_like(refs[-1])

    outs = pl.pallas_call(
        body,
        name=name,
        in_specs=[HBM] * n,
        out_specs=[SEM, SEM] + [HBM] * n + [pl.BlockSpec(memory_space=pltpu.VMEM)],
        out_shape=[pltpu.SemaphoreType.DMA((n_sem,)), pltpu.SemaphoreType.DMA((n_sem,))] + [pltpu.HBM(b.shape, b.dtype) for b in bufs]
        + [jax.ShapeDtypeStruct((SUBLANES, LANES), F32)],
        input_output_aliases={i: 2 + i for i in range(n)},
        compiler_params=pltpu.CompilerParams(has_side_effects=EFFECT),
    )(*[_in_hbm(b) for b in bufs])
    return outs[0], outs[1], list(outs[2:-1]), outs[-1]


def _wait_copies(name, bufs, send_sems, recv_sems, after, copies):
    n = len(bufs)

    def body(*refs):
        for cp in copies(refs[:n], refs[n], refs[n + 1]):
            cp.wait_send()
            cp.wait_recv()

    return pl.pallas_call(
        body,
        name=name,
        in_specs=[HBM] * n + [SEM, SEM, ANY],
        out_specs=[HBM] * n,
        out_shape=[pltpu.HBM(b.shape, b.dtype) for b in bufs],
        input_output_aliases={i: i for i in range(n)},
        compiler_params=pltpu.CompilerParams(has_side_effects=EFFECT),
    )(*bufs, send_sems, recv_sems, after)


def _gather_copies(first, count):
    def copies(refs, send_sems, recv_sems):
        x, y, c, chips = _place()
        me = 2 * x + y
        out = []
        for i in range(count):
            for j, (px, py) in enumerate(chips):
                k = 3 * (first + i) + j
                out.append(_rcopy(refs[i].at[me], refs[i].at[me], send_sems.at[k], recv_sems.at[k], (px, py, c)))
        return out

    return copies


def _swap_copies(n):
    def copies(refs, send_sems, recv_sems):
        x, y, c, _ = _place()
        out = []
        for i in range(n):
            h = refs[i].shape[1] // 2
            out.append(_rcopy(refs[i].at[:, pl.ds((1 - c) * h, h)], refs[n + i], send_sems.at[i], recv_sems.at[i], (x, y, 1 - c)))
        return out

    return copies


def _partial_copies(n):
    def copies(refs, send_sems, recv_sems):
        x, y, c, chips = _place()
        out = []
        for i in range(n):
            for j, (px, py) in enumerate(chips):
                out.append(_rcopy(refs[i].at[2 * px + py], refs[n + i].at[j], send_sems.at[3 * i + j], recv_sems.at[3 * i + j], (px, py, c)))
        return out

    return copies


def _share_copies(n):
    def copies(refs, send_sems, recv_sems):
        x, y, c, _ = _place()
        return [_rcopy(refs[i].at[c], refs[i].at[c], send_sems.at[i], recv_sems.at[i], (x, y, 1 - c)) for i in range(n)]

    return copies


def _allreduce_small(vec):
    rows = vec.shape[0]

    def body(v_ref, o_ref, buf, send_sems, recv_sems):
        x, y, c, _ = _place()
        me = 4 * x + 2 * y + c
        buf[me] = v_ref[...]
        copies = []
        for k in range(1, N_DEV):
            peer = (x ^ (k >> 2), y ^ ((k >> 1) & 1), c ^ (k & 1))
            cp = _rcopy(v_ref, buf.at[me], send_sems.at[k - 1], recv_sems.at[k - 1], peer)
            cp.start()
            copies.append(cp)
        for cp in copies:
            cp.wait()
        acc = buf[0]
        for d in range(1, N_DEV):
            acc = acc + buf[d]
        o_ref[...] = acc

    return pl.pallas_call(
        body,
        name="allreduce_small",
        in_specs=[pl.BlockSpec(memory_space=pltpu.VMEM)],
        out_specs=pl.BlockSpec(memory_space=pltpu.VMEM),
        out_shape=jax.ShapeDtypeStruct(vec.shape, F32),
        scratch_shapes=[pltpu.VMEM((N_DEV, rows, LANES), F32), pltpu.SemaphoreType.DMA((N_DEV - 1,)), pltpu.SemaphoreType.DMA((N_DEV - 1,))],
        compiler_params=pltpu.CompilerParams(vmem_limit_bytes=VMEM_LIMIT_BYTES),
    )(vec)


class _Reduction:
    def __init__(self, tag, grads, place, core):
        self.tag, self.n, self.place, self.core = tag, len(grads), place, core
        lands = [lax.empty((N_CHIPS, g.shape[1] // 2, g.shape[2]), F32) for g in grads]
        self._start("swap", list(grads) + lands, self.n, _swap_copies(self.n))

    def _start(self, stage, bufs, n_sem, copies):
        *self.flight, self.token = _start_copies(f"rs_{stage}_start_{self.tag}", bufs, n_sem, copies)

    def _landed(self, stage, after, copies):
        send_sems, recv_sems, bufs = self.flight
        return _wait_copies(f"rs_{stage}_wait_{self.tag}", bufs, send_sems, recv_sems, after, copies)

    def to_chips(self, after):
        n = self.n
        bufs = self._landed("swap", after, _swap_copies(n))
        sums = [_add_core_halves(g, o, self.core, name=f"rs_add_core_{self.tag}_{i}") for i, (g, o) in enumerate(zip(bufs[:n], bufs[n:]))]
        self.mine = [f for f, _ in sums]
        parts = [b for _, b in sums]
        lands = [lax.empty((3,) + p.shape[1:], BF16) for p in parts]
        self._start("send", parts + lands, 3 * n, _partial_copies(n))

    def to_core(self, after):
        n = self.n
        bufs = self._landed("send", after, _partial_copies(n))
        halves = [_add_chip_partials(f, o, self.place, name=f"rs_add_chip_{self.tag}_{i}") for i, (f, o) in enumerate(zip(self.mine, bufs[n:]))]
        self._start("share", halves, n, _share_copies(n))

    def finish(self, after):
        return [b.reshape((-1,) + b.shape[2:]) for b in self._landed("share", after, _share_copies(self.n))]


ELEM_ROWS = (256, 176, 128, 64, 32, 16, 8)


def _add_core_halves(grad, got, c, name):
    s, r, cols = grad.shape
    h = r // 2
    tr = _pick(h, ELEM_ROWS)

    def body(c_ref, g_ref, o_ref, f_ref, b_ref):
        acc = g_ref[...] + o_ref[...]
        f_ref[...] = acc
        b_ref[...] = acc.astype(BF16)

    blk = pl.BlockSpec((None, tr, cols), lambda k, i, c_ref: (k, i, 0))
    return pl.pallas_call(
        body,
        name=name,
        grid_spec=pltpu.PrefetchScalarGridSpec(
            num_scalar_prefetch=1,
            grid=(s, h // tr),
            in_specs=[pl.BlockSpec((None, None, tr, cols), lambda k, i, c_ref: (k, c_ref[0], i, 0)), blk],
            out_specs=[blk, blk],
        ),
        out_shape=[jax.ShapeDtypeStruct((s, h, cols), F32), jax.ShapeDtypeStruct((s, h, cols), BF16)],
        compiler_params=_cparams(("parallel", "parallel")),
    )(c, grad.reshape(s, 2, h, cols), got)


def _add_chip_partials(mine, got, place, name):
    _, h, cols = mine.shape
    tr = _pick(h, ELEM_ROWS)

    def body(place_ref, m_ref, g_ref, o_ref):
        acc = m_ref[...]
        for j in range(3):
            acc = acc + g_ref[j].astype(F32)
        o_ref[...] = acc

    return pl.pallas_call(
        body,
        name=name,
        grid_spec=pltpu.PrefetchScalarGridSpec(
            num_scalar_prefetch=1,
            grid=(h // tr,),
            in_specs=[
                pl.BlockSpec((None, tr, cols), lambda i, place_ref: (place_ref[0], i, 0)),
                pl.BlockSpec((3, tr, cols), lambda i, place_ref: (0, i, 0)),
            ],
            out_specs=pl.BlockSpec((None, tr, cols), lambda i, place_ref: (place_ref[1], i, 0)),
        ),
        out_shape=jax.ShapeDtypeStruct((2, h, cols), F32),
        compiler_params=_cparams(("parallel",)),
    )(place, mine, got)


def _adamw_math(w, m, v, g):
    nm = ADAM_B1 * m + (1.0 - ADAM_B1) * g
    nv = ADAM_B2 * v + (1.0 - ADAM_B2) * (g * g)
    m_hat = nm * (1.0 / (1.0 - ADAM_B1 ** ADAM_STEP))
    v_hat = nv * (1.0 / (1.0 - ADAM_B2 ** ADAM_STEP))
    return -ADAM_LR * (m_hat / (jnp.sqrt(v_hat) + ADAM_EPS) + ADAM_WD * w), nm, nv


def _adamw_layer(w, m, v, g, layer, prev, name):
    nl, r, cols = w.shape
    tr = _pick(r, ELEM_ROWS)

    def body(w_ref, m_ref, v_ref, g_ref, *rest):
        go_ref, d_ref, nm_ref, nv_ref = rest[-4:]
        gv = g_ref[...]
        d_ref[...], nm_ref[...], nv_ref[...] = _adamw_math(w_ref[...], m_ref[...], v_ref[...], gv)
        go_ref[...] = gv

    lay = pl.BlockSpec((None, tr, cols), lambda i: (layer, i, 0))
    return pl.pallas_call(
        body,
        name=name,
        grid=(r // tr,),
        in_specs=[lay] * 3 + [pl.BlockSpec((tr, cols), lambda i: (i, 0))] + ([ANY] * 4 if prev else []),
        out_specs=[lay] * 4,
        out_shape=[jax.ShapeDtypeStruct((nl, r, cols), F32)] * 4,
        input_output_aliases={4 + k: k for k in range(4)} if prev else {},
        compiler_params=_cparams(("parallel",)),
    )(w, m, v, g, *(prev or ()))


def _adamw(w, m, v, g, name):
    r, cols = w.shape
    tr = _pick(r, ELEM_ROWS)

    def body(w_ref, m_ref, v_ref, g_ref, d_ref, nm_ref, nv_ref):
        d_ref[...], nm_ref[...], nv_ref[...] = _adamw_math(w_ref[...], m_ref[...], v_ref[...], g_ref[...])

    blk = pl.BlockSpec((tr, cols), lambda i: (i, 0))
    return pl.pallas_call(
        body,
        name=name,
        grid=(r // tr,),
        in_specs=[blk] * 4,
        out_specs=[blk] * 3,
        out_shape=[jax.ShapeDtypeStruct((r, cols), F32)] * 3,
        compiler_params=_cparams(("parallel",)),
    )(w, m, v, g)


SMALL_COLS = 384
SMALL_ROWS = 16


def _pad_rows(flat, rows, cols):
    return jnp.pad(flat, (0, rows * cols - flat.shape[0])).reshape(rows, cols)


def kernel(x, hg_norm, hg_w_in, hg_lb_logits, hg_out_norm, hg_w_out, kv_norm, w_kv, attn_norm, attn_w_q, attn_sinks, attn_w_o, ffn_norm, ffn_w_up, ffn_conv_w, ffn_conv_b, ffn_w_down, final_norm, loss_target, m_hg_norm, m_hg_w_in, m_hg_lb_logits, m_hg_out_norm, m_hg_w_out, m_kv_norm, m_w_kv, m_attn_norm, m_attn_w_q, m_attn_sinks, m_attn_w_o, m_ffn_norm, m_ffn_w_up, m_ffn_conv_w, m_ffn_conv_b, m_ffn_w_down, m_final_norm, v_hg_norm, v_hg_w_in, v_hg_lb_logits, v_hg_out_norm, v_hg_w_out, v_kv_norm, v_w_kv, v_attn_norm, v_attn_w_q, v_attn_sinks, v_attn_w_o, v_ffn_norm, v_ffn_w_up, v_ffn_conv_w, v_ffn_conv_b, v_ffn_w_down, v_final_norm):
    wts = dict(hg_norm=hg_norm, hg_w_in=hg_w_in, hg_lb_logits=hg_lb_logits, hg_out_norm=hg_out_norm, hg_w_out=hg_w_out, kv_norm=kv_norm, w_kv=w_kv, attn_norm=attn_norm, attn_w_q=attn_w_q, attn_sinks=attn_sinks, attn_w_o=attn_w_o, ffn_norm=ffn_norm, ffn_w_up=ffn_w_up, ffn_conv_w=ffn_conv_w, ffn_conv_b=ffn_conv_b, ffn_w_down=ffn_w_down, final_norm=final_norm)
    mom1 = dict(hg_norm=m_hg_norm, hg_w_in=m_hg_w_in, hg_lb_logits=m_hg_lb_logits, hg_out_norm=m_hg_out_norm, hg_w_out=m_hg_w_out, kv_norm=m_kv_norm, w_kv=m_w_kv, attn_norm=m_attn_norm, attn_w_q=m_attn_w_q, attn_sinks=m_attn_sinks, attn_w_o=m_attn_w_o, ffn_norm=m_ffn_norm, ffn_w_up=m_ffn_w_up, ffn_conv_w=m_ffn_conv_w, ffn_conv_b=m_ffn_conv_b, ffn_w_down=m_ffn_w_down, final_norm=m_final_norm)
    mom2 = dict(hg_norm=v_hg_norm, hg_w_in=v_hg_w_in, hg_lb_logits=v_hg_lb_logits, hg_out_norm=v_hg_out_norm, hg_w_out=v_hg_w_out, kv_norm=v_kv_norm, w_kv=v_w_kv, attn_norm=v_attn_norm, attn_w_q=v_attn_w_q, attn_sinks=v_attn_sinks, attn_w_o=v_attn_w_o, ffn_norm=v_ffn_norm, ffn_w_up=v_ffn_w_up, ffn_conv_w=v_ffn_conv_w, ffn_conv_b=v_ffn_conv_b, ffn_w_down=v_ffn_w_down, final_norm=v_final_norm)
    names = list(wts)
    chip = 2 * lax.axis_index("x") + lax.axis_index("y")
    core = lax.axis_index("c")
    core_arr = jnp.reshape(core, (1,)).astype(jnp.int32)
    fs = D_FF // N_CHIPS
    ds = D_MODEL // N_CHIPS

    place_arr = jnp.stack([chip, core]).astype(jnp.int32)
    small = jnp.concatenate([hg_norm.reshape(-1), hg_lb_logits.reshape(-1), ffn_conv_w.reshape(-1)])
    n_small = small.shape[0]
    shards = [
        ("small", _pad_rows(small, SMALL_ROWS, SMALL_COLS), F32), ("hg_w_in", hg_w_in[0], BF16),
        ("hg_w_out", hg_w_out[0], BF16), ("ffn_w_up0", ffn_w_up[0], BF16), ("ffn_w_down0", ffn_w_down[0], BF16),
        ("w_kv", w_kv, BF16), ("attn_w_q", attn_w_q[0], BF16), ("attn_w_o", attn_w_o[0], BF16),
        ("ffn_w_up1", ffn_w_up[1], BF16), ("ffn_w_down1", ffn_w_down[1], BF16),
    ]
    n_first = 2
    stages = dict(first=(0, 0, n_first), layer0=(1, 0, 3), layer1=(1, 3, 8))
    placed = [_place_shard(s, place_arr, dt, name=f"place_{nm}") for nm, s, dt in shards[:n_first]]
    flights = [_start_copies("gather_start_first", placed, 3 * n_first, _gather_copies(0, n_first))]
    placed = [_place_shard(s, place_arr, dt, name=f"place_{nm}", deps=(flights[0][3],)) for nm, s, dt in shards[n_first:]]
    flights.append(_start_copies("gather_start_rest", placed, 3 * len(placed), _gather_copies(0, len(placed))))

    def fetch(w, stage, after):
        call, lo, hi = stages[stage]
        send_sems, recv_sems, bufs, _ = flights[call]
        got = _wait_copies(f"gather_wait_{stage}", bufs[lo:hi], send_sems, recv_sems, after, _gather_copies(lo, hi - lo))
        w = dict(w)
        if stage == "first":
            g_small = got[0].reshape(N_CHIPS, -1)[:, :n_small]
            conv_w = g_small[:, 3 * ds:].reshape(N_CHIPS, 2, 3, fs).transpose(1, 2, 0, 3).reshape(2, 3, D_FF)
            w.update(
                hg_norm=g_small[:, :ds].reshape(1, D_MODEL),
                hg_lb=g_small[:, ds:3 * ds].reshape(N_CHIPS, 2, ds).transpose(1, 0, 2).reshape(2, D_MODEL),
                ffn_conv_w=[conv_w[0], conv_w[1]], hg_w_in=got[1],
            )
        elif stage == "layer0":
            w.update(hg_w_out=got[0].reshape(1, D_MODEL, D_MODEL), ffn_w_up=[got[1], None], ffn_w_down=[got[2].reshape(1, D_FF, D_MODEL), None])
        else:
            w.update(
                w_kv=got[0].reshape(1, D_MODEL, 2 * LANES), attn_w_q=got[1].reshape(1, D_MODEL, D_MODEL),
                attn_w_o=got[2].reshape(1, D_MODEL, D_MODEL), ffn_w_up=[w["ffn_w_up"][0], got[3]],
                ffn_w_down=[w["ffn_w_down"][0], got[4].reshape(1, D_FF, D_MODEL)],
            )
        return w

    whole = dict(
        hg_out_norm=hg_out_norm, kv_norm=kv_norm.reshape(1, D_MODEL), attn_norm=attn_norm, attn_sinks=attn_sinks.reshape(ATT_QH),
        ffn_norm=[ffn_norm[0:1], ffn_norm[1:2]], ffn_conv_b=[ffn_conv_b[0:1], ffn_conv_b[1:2]], final_norm=final_norm.reshape(1, D_MODEL),
    )
    whole = fetch(whole, "first", flights[1][3])

    red, layer1 = {}, {}

    def by_rows(g, rows):
        return g.reshape(N_CHIPS, rows, g.shape[2])

    def hook(point, dh, grads):
        if point == "ffn1":
            red["ffn1"] = _Reduction("ffn1", [by_rows(grads["ffn_w_down"], fs), grads["ffn_w_up"]], place_arr, core_arr)
            return (red["ffn1"].token,)
        if point == "attn":
            red["ffn1"].to_chips(dh)
            layer1.update(grads)
            return (red["ffn1"].token,)
        if point == "ffn0":
            group = [by_rows(layer1["attn_w_o"], ds), by_rows(layer1["attn_w_q"], ds), by_rows(layer1["w_kv"], ds),
                     by_rows(grads["ffn_w_down"], fs), grads["ffn_w_up"]]
            red["mid"] = _Reduction("mid", group, place_arr, core_arr)
            return (red["mid"].token,)
        if point == "hgrn":
            red["ffn1"].to_core(dh)
            red["mid"].to_chips(dh)
            return (red["ffn1"].token, red["mid"].token)
        red["hg"] = _Reduction("hg", [by_rows(grads["hg_w_out"], ds), grads["hg_w_in"]], place_arr, core_arr)
        return (red["hg"].token,)

    loss, dx, grads = _local_step(x[0], loss_target[0], whole, fetch, hook)

    small_parts = [
        loss.reshape(-1), grads["hg_out_norm"].reshape(-1), grads["attn_sinks"].reshape(-1), grads["kv_norm"].reshape(-1),
        grads["attn_norm"].reshape(-1), grads["ffn_norm"][0].reshape(-1), grads["ffn_norm"][1].reshape(-1),
        grads["ffn_conv_b"][0].reshape(-1), grads["ffn_conv_b"][1].reshape(-1), grads["final_norm"].reshape(-1),
        grads["hg_norm"].reshape(-1), grads["hg_lb"].reshape(-1), grads["ffn_conv_w"][0].reshape(-1), grads["ffn_conv_w"][1].reshape(-1),
    ]
    sizes = [p.shape[0] for p in small_parts]
    flat = jnp.concatenate(small_parts)
    rows = -(-flat.shape[0] // (SUBLANES * LANES)) * SUBLANES
    summed = _allreduce_small(_pad_rows(flat, rows, LANES)).reshape(-1)
    red["hg"].to_chips(summed)
    offs = [0]
    for sz in sizes:
        offs.append(offs[-1] + sz)
    sm = [summed[offs[i]:offs[i + 1]] for i in range(len(sizes))]
    loss_out = sm[0][0]
    conv_w_full = jnp.stack([sm[12].reshape(3, D_FF), sm[13].reshape(3, D_FF)])
    small_grads = dict(
        hg_out_norm=sm[1].reshape(1, HG_DK), attn_sinks=sm[2][:ATT_QH].reshape(1, ATT_QH), kv_norm=sm[3], attn_norm=sm[4].reshape(1, D_MODEL),
        ffn_norm=jnp.stack([sm[5], sm[6]]), ffn_conv_b=jnp.stack([sm[7], sm[8]]), final_norm=sm[9],
        hg_norm=lax.dynamic_slice(sm[10].reshape(1, D_MODEL), (0, chip * ds), (1, ds)),
        hg_lb_logits=lax.dynamic_slice(sm[11].reshape(2, D_MODEL), (0, chip * ds), (2, ds)),
        ffn_conv_w=lax.dynamic_slice(conv_w_full, (0, 0, chip * fs), (2, 3, fs)),
    )

    out_g, out_d, out_m, out_v = {}, {}, {}, {}

    def update(name, g2):
        shape = wts[name].shape
        d2, m2, v2 = _adamw(wts[name].reshape(g2.shape), mom1[name].reshape(g2.shape), mom2[name].reshape(g2.shape), g2, name=f"adamw_{name}")
        out_g[name], out_d[name], out_m[name], out_v[name] = g2.reshape(shape), d2.reshape(shape), m2.reshape(shape), v2.reshape(shape)
        return d2

    def update_layer(name, g2, layer, prev):
        res = _adamw_layer(wts[name], mom1[name], mom2[name], g2, layer, prev, name=f"adamw_{name}{layer}")
        out_g[name], out_d[name], out_m[name], out_v[name] = res
        return res

    g_down1, g_up1 = red["ffn1"].finish(summed)
    down1 = update_layer("ffn_w_down", g_down1, 1, None)
    up1 = update_layer("ffn_w_up", g_up1, 1, None)
    red["mid"].to_core(up1[1])
    g_o, g_q, g_kv, g_down0, g_up0 = red["mid"].finish(up1[2])
    update("attn_w_o", g_o)
    update("attn_w_q", g_q)
    update("w_kv", g_kv)
    update_layer("ffn_w_down", g_down0, 0, down1)
    last = update_layer("ffn_w_up", g_up0, 0, up1)
    red["hg"].to_core(last[1])
    g_out, g_in = red["hg"].finish(last[2])
    update("hg_w_out", g_out)
    update("hg_w_in", g_in)

    small_names = [n for n in names if n not in out_g]
    cat = lambda d: jnp.concatenate([d[n].reshape(-1) for n in small_names])
    n_flat = sum(wts[n].size for n in small_names)
    srows = -(-n_flat // (SUBLANES * LANES)) * SUBLANES
    packed = [_pad_rows(cat(d), srows, LANES) for d in (wts, mom1, mom2, small_grads)]
    d_s, m_s, v_s = _adamw(*packed, name="adamw_small")
    off = 0
    for n in small_names:
        sz, shape = wts[n].size, wts[n].shape
        out_g[n] = small_grads[n].reshape(shape)
        out_d[n] = d_s.reshape(-1)[off:off + sz].reshape(shape)
        out_m[n] = m_s.reshape(-1)[off:off + sz].reshape(shape)
        out_v[n] = v_s.reshape(-1)[off:off + sz].reshape(shape)
        off += sz

    grad_x = dx.reshape(x.shape)
    return (loss_out, grad_x, *[out_g[n] for n in names], *[out_d[n] for n in names], *[out_m[n] for n in names], *[out_v[n] for n in names])
```

```python
import functools

import jax
import jax.numpy as jnp
from jax import lax
from jax.experimental import pallas as pl
from jax.experimental.pallas import tpu as pltpu

F32 = jnp.float32
BF16 = jnp.bfloat16
MESH = pl.DeviceIdType.MESH

EPS = 1e-6
D_MODEL = 1024
HG_HEADS = 8
HG_DK = 128
HG_CHUNK = 64
ATT_HD = 64
ATT_QH = 16
ATT_KVH = 2
ATT_GROUP = ATT_QH // ATT_KVH
WINDOW = 128
D_FF = 2816
N_CHIPS = 4
N_DEV = 8
LANES = 128
SUBLANES = 8
VMEM_LIMIT_BYTES = 56 * 1024 * 1024
NEG = -1e30
ALIBI_SLOPES = tuple(2.0 ** (-8.0 * h / ATT_QH) for h in range(1, ATT_QH + 1))

ADAM_LR = 0.001
ADAM_B1 = 0.9
ADAM_B2 = 0.999
ADAM_EPS = 1e-08
ADAM_WD = 0.01
ADAM_STEP = 10


def _cparams(sem=None):
    return pltpu.CompilerParams(dimension_semantics=sem, vmem_limit_bytes=VMEM_LIMIT_BYTES)


def _pick(n, cands):
    for c in cands:
        if n % c == 0:
            return c
    return n


def _sigmoid(x):
    return 1.0 / (1.0 + jnp.exp(-x))


def _dot(a, b, dims):
    return lax.dot_general(a, b, (dims, ((), ())), preferred_element_type=F32)


NN = ((1,), (0,))
NT = ((1,), (1,))
TN = ((0,), (0,))


def _mm_nn(a, w, res=None, out_dtype=F32, name="mm_nn"):
    m, k = a.shape
    s, _, ns = w.shape
    tm = min(m, 512)
    tn = _pick(ns, (512, 1408, 256, 128))
    npb = ns // tn

    def body(a_ref, w_ref, *rest):
        o_ref = rest[-1]
        acc = _dot(a_ref[...].astype(BF16), w_ref[...], NN)
        if res is not None:
            acc = acc + rest[0][...]
        o_ref[...] = acc.astype(o_ref.dtype)

    in_specs = [
        pl.BlockSpec((tm, k), lambda i, j: (i, 0)),
        pl.BlockSpec((None, k, tn), lambda i, j: (j // npb, 0, j % npb)),
    ]
    args = [a, w]
    if res is not None:
        in_specs.append(pl.BlockSpec((tm, tn), lambda i, j: (i, j)))
        args.append(res)
    return pl.pallas_call(
        body,
        name=name,
        grid=(m // tm, s * npb),
        in_specs=in_specs,
        out_specs=pl.BlockSpec((tm, tn), lambda i, j: (i, j)),
        out_shape=jax.ShapeDtypeStruct((m, s * ns), out_dtype),
        compiler_params=_cparams(("parallel", "parallel")),
    )(*args)


def _dy_spec(stacked, tm, tn, npb, row, kk):
    if stacked:
        return pl.BlockSpec((None, tm, tn), lambda *g: (kk(g) // npb, row(g), kk(g) % npb))
    return pl.BlockSpec((tm, tn), lambda *g: (row(g), kk(g)))


def _dep_specs(deps):
    return [pl.BlockSpec(d.shape, lambda *g: (0, 0)) for d in deps]


def _mm_nt(dy, w, stacked=False, out_dtype=F32, name="mm_nt", deps=()):
    s, k, ns = w.shape
    m = dy.shape[1] if stacked else dy.shape[0]
    tm = min(m, 512)
    tko = _pick(k, (1024, 1408, 512, 256))
    tn = _pick(ns, (1024, 1408, 512, 256))
    npb = ns // tn
    nk = s * npb

    def body(dy_ref, w_ref, *rest):
        o_ref, acc_ref = rest[-2:]
        kk = pl.program_id(2)

        @pl.when(kk == 0)
        def _():
            acc_ref[...] = jnp.zeros_like(acc_ref)

        acc_ref[...] += _dot(dy_ref[...].astype(BF16), w_ref[...], NT)

        @pl.when(kk == nk - 1)
        def _():
            o_ref[...] = acc_ref[...].astype(o_ref.dtype)

    return pl.pallas_call(
        body,
        name=name,
        grid=(m // tm, k // tko, nk),
        in_specs=[
            _dy_spec(stacked, tm, tn, npb, lambda g: g[0], lambda g: g[2]),
            pl.BlockSpec((None, tko, tn), lambda i, j, kk: (kk // npb, j, kk % npb)),
        ] + _dep_specs(deps),
        out_specs=pl.BlockSpec((tm, tko), lambda i, j, kk: (i, j)),
        out_shape=jax.ShapeDtypeStruct((m, k), out_dtype),
        scratch_shapes=[pltpu.VMEM((tm, tko), F32)],
        compiler_params=_cparams(("parallel", "parallel", "arbitrary")),
    )(dy, w, *deps)


def _mm_tn(a, dy, s, ns, stacked=False, name="mm_tn", deps=()):
    m, k = a.shape
    tm = min(m, 512)
    tk = _pick(k, (1024, 1408, 512, 256))
    tn = _pick(ns, (512, 1408, 256, 128))
    npb = ns // tn
    nm = m // tm

    def body(a_ref, dy_ref, *rest):
        o_ref, acc_ref = rest[-2:]
        mm = pl.program_id(2)

        @pl.when(mm == 0)
        def _():
            acc_ref[...] = jnp.zeros_like(acc_ref)

        acc_ref[...] += _dot(a_ref[...].astype(BF16), dy_ref[...].astype(BF16), TN)

        @pl.when(mm == nm - 1)
        def _():
            o_ref[...] = acc_ref[...]

    return pl.pallas_call(
        body,
        name=name,
        grid=(k // tk, s * npb, nm),
        in_specs=[
            pl.BlockSpec((tm, tk), lambda i, j, mm: (mm, i)),
            _dy_spec(stacked, tm, tn, npb, lambda g: g[2], lambda g: g[1]),
        ] + _dep_specs(deps),
        out_specs=pl.BlockSpec((None, tk, tn), lambda i, j, mm: (j // npb, i, j % npb)),
        out_shape=jax.ShapeDtypeStruct((s, k, ns), F32),
        scratch_shapes=[pltpu.VMEM((tk, tn), F32)],
        compiler_params=_cparams(("parallel", "parallel", "arbitrary")),
    )(a, dy, *deps)


ROW_TILE = 256


def _rms_fwd(x, g, name="rms_fwd"):
    t, d = x.shape
    r = min(t, ROW_TILE)

    def body(x_ref, g_ref, o_ref):
        xv = x_ref[...]
        rstd = lax.rsqrt(jnp.mean(xv * xv, axis=-1, keepdims=True) + EPS)
        o_ref[...] = (xv * rstd * g_ref[...]).astype(BF16)

    return pl.pallas_call(
        body,
        name=name,
        grid=(t // r,),
        in_specs=[pl.BlockSpec((r, d), lambda i: (i, 0)), pl.BlockSpec((1, d), lambda i: (0, 0))],
        out_specs=pl.BlockSpec((r, d), lambda i: (i, 0)),
        out_shape=jax.ShapeDtypeStruct((t, d), BF16),
        compiler_params=_cparams(("parallel",)),
    )(x, g)


def _rms_bwd(x, g, dxn, dres, name="rms_bwd"):
    t, d = x.shape
    r = min(t, ROW_TILE)

    def body(x_ref, g_ref, dxn_ref, dres_ref, dx_ref, dg_ref):
        @pl.when(pl.program_id(0) == 0)
        def _():
            dg_ref[...] = jnp.zeros_like(dg_ref)

        xv = x_ref[...]
        rstd = lax.rsqrt(jnp.mean(xv * xv, axis=-1, keepdims=True) + EPS)
        xhat = xv * rstd
        dxn_v = dxn_ref[...].astype(F32)
        gd = dxn_v * g_ref[...]
        dx_ref[...] = dres_ref[...] + rstd * (gd - xhat * jnp.mean(gd * xhat, axis=-1, keepdims=True))
        dg_ref[...] += jnp.sum(dxn_v * xhat, axis=0, keepdims=True)

    return pl.pallas_call(
        body,
        name=name,
        grid=(t // r,),
        in_specs=[
            pl.BlockSpec((r, d), lambda i: (i, 0)),
            pl.BlockSpec((1, d), lambda i: (0, 0)),
            pl.BlockSpec((r, d), lambda i: (i, 0)),
            pl.BlockSpec((r, d), lambda i: (i, 0)),
        ],
        out_specs=[pl.BlockSpec((r, d), lambda i: (i, 0)), pl.BlockSpec((1, d), lambda i: (0, 0))],
        out_shape=[jax.ShapeDtypeStruct((t, d), F32), jax.ShapeDtypeStruct((1, d), F32)],
        compiler_params=_cparams(("arbitrary",)),
    )(x, g, dxn, dres)


def _loss_head(h, g, target):
    t, d = h.shape
    r = min(t, ROW_TILE)

    def body(h_ref, g_ref, t_ref, dh_ref, dg_ref, loss_ref):
        @pl.when(pl.program_id(0) == 0)
        def _():
            dg_ref[...] = jnp.zeros_like(dg_ref)
            loss_ref[...] = jnp.zeros_like(loss_ref)

        xv = h_ref[...]
        rstd = lax.rsqrt(jnp.mean(xv * xv, axis=-1, keepdims=True) + EPS)
        xhat = xv * rstd
        gv = g_ref[...]
        err = xhat * gv - t_ref[...]
        loss_ref[...] += 0.5 * jnp.sum(jnp.mean(err * err, axis=-1, keepdims=True), axis=0, keepdims=True)
        dy = err * (1.0 / d)
        gd = dy * gv
        dh_ref[...] = rstd * (gd - xhat * jnp.mean(gd * xhat, axis=-1, keepdims=True))
        dg_ref[...] += jnp.sum(dy * xhat, axis=0, keepdims=True)

    return pl.pallas_call(
        body,
        name="loss_head",
        grid=(t // r,),
        in_specs=[
            pl.BlockSpec((r, d), lambda i: (i, 0)),
            pl.BlockSpec((1, d), lambda i: (0, 0)),
            pl.BlockSpec((r, d), lambda i: (i, 0)),
        ],
        out_specs=[
            pl.BlockSpec((r, d), lambda i: (i, 0)),
            pl.BlockSpec((1, d), lambda i: (0, 0)),
            pl.BlockSpec((1, LANES), lambda i: (0, 0)),
        ],
        out_shape=[
            jax.ShapeDtypeStruct((t, d), F32),
            jax.ShapeDtypeStruct((1, d), F32),
            jax.ShapeDtypeStruct((1, LANES), F32),
        ],
        compiler_params=_cparams(("arbitrary",)),
    )(h, g, target)


CONV_ROWS = 128
CONV_COLS = 1408


def _conv_taps(x_ext, n):
    tot = x_ext.shape[0]
    g1 = pltpu.roll(x_ext, 1, 0)[tot - n:]
    g2 = pltpu.roll(x_ext, 2, 0)[tot - n:]
    return g2, g1


def _conv_fwd(up, conv_w, conv_b, name="conv_fwd"):
    t = up.shape[0]
    r = min(t, CONV_ROWS)
    tc = CONV_COLS
    ncb = D_FF // tc
    hb = r // SUBLANES

    def body(g_ref, halo_ref, v_ref, w_ref, b_ref, o_ref):
        i = pl.program_id(1)
        g0 = g_ref[...]
        halo = halo_ref[...] * jnp.where(i > 0, 1.0, 0.0)
        g2, g1 = _conv_taps(jnp.concatenate([halo, g0], axis=0), r)
        c = b_ref[...] + w_ref[0:1, :] * g2 + w_ref[1:2, :] * g1 + w_ref[2:3, :] * g0
        o_ref[...] = (c * _sigmoid(c) * v_ref[...]).astype(BF16)

    return pl.pallas_call(
        body,
        name=name,
        grid=(ncb, t // r),
        in_specs=[
            pl.BlockSpec((r, tc), lambda j, i: (i, j)),
            pl.BlockSpec((SUBLANES, tc), lambda j, i: (jnp.maximum(i * hb - 1, 0), j)),
            pl.BlockSpec((r, tc), lambda j, i: (i, ncb + j)),
            pl.BlockSpec((3, tc), lambda j, i: (0, j)),
            pl.BlockSpec((1, tc), lambda j, i: (0, j)),
        ],
        out_specs=pl.BlockSpec((r, tc), lambda j, i: (i, j)),
        out_shape=jax.ShapeDtypeStruct((t, D_FF), BF16),
        compiler_params=_cparams(("parallel", "parallel")),
    )(up, up, up, conv_w, conv_b)


def _conv_bwd(up, conv_w, conv_b, dact, name="conv_bwd"):
    t = up.shape[0]
    r = min(t, CONV_ROWS)
    tc = CONV_COLS
    ncb = D_FF // tc
    hb = r // SUBLANES
    nrt = t // r

    def body(g_ref, halo_ref, v_ref, w_ref, b_ref, da_ref, dup_ref, dw_ref, db_ref, nxt_ref):
        ii = pl.program_id(1)
        i = nrt - 1 - ii

        @pl.when(ii == 0)
        def _():
            nxt_ref[...] = jnp.zeros_like(nxt_ref)
            dw_ref[...] = jnp.zeros_like(dw_ref)
            db_ref[...] = jnp.zeros_like(db_ref)

        g0 = g_ref[...]
        halo = halo_ref[...] * jnp.where(i > 0, 1.0, 0.0)
        g2, g1 = _conv_taps(jnp.concatenate([halo, g0], axis=0), r)
        w0, w1, w2 = w_ref[0:1, :], w_ref[1:2, :], w_ref[2:3, :]
        c = b_ref[...] + w0 * g2 + w1 * g1 + w2 * g0
        sg = _sigmoid(c)
        da = da_ref[...]
        dval = da * (c * sg)
        dc = da * v_ref[...] * (sg * (1.0 + c * (1.0 - sg)))
        db_ref[...] += jnp.sum(dc, axis=0, keepdims=True)
        dw_ref[0:1, :] += jnp.sum(dc * g2, axis=0, keepdims=True)
        dw_ref[1:2, :] += jnp.sum(dc * g1, axis=0, keepdims=True)
        dw_ref[2:3, :] += jnp.sum(dc * g0, axis=0, keepdims=True)
        ext = jnp.concatenate([dc, nxt_ref[...]], axis=0)
        tot = r + SUBLANES
        d1 = pltpu.roll(ext, tot - 1, 0)[:r]
        d2 = pltpu.roll(ext, tot - 2, 0)[:r]
        dgate = w2 * dc + w1 * d1 + w0 * d2
        nxt_ref[...] = dc[:SUBLANES]
        dup_ref[0] = dgate.astype(BF16)
        dup_ref[1] = dval.astype(BF16)

    rev = lambda ii: nrt - 1 - ii
    dup, dw, db = pl.pallas_call(
        body,
        name=name,
        grid=(ncb, nrt),
        in_specs=[
            pl.BlockSpec((r, tc), lambda j, ii: (rev(ii), j)),
            pl.BlockSpec((SUBLANES, tc), lambda j, ii: (jnp.maximum(rev(ii) * hb - 1, 0), j)),
            pl.BlockSpec((r, tc), lambda j, ii: (rev(ii), ncb + j)),
            pl.BlockSpec((3, tc), lambda j, ii: (0, j)),
            pl.BlockSpec((1, tc), lambda j, ii: (0, j)),
            pl.BlockSpec((r, tc), lambda j, ii: (rev(ii), j)),
        ],
        out_specs=[
            pl.BlockSpec((2, None, r, tc), lambda j, ii: (0, j, rev(ii), 0)),
            pl.BlockSpec((3, tc), lambda j, ii: (0, j)),
            pl.BlockSpec((1, tc), lambda j, ii: (0, j)),
        ],
        out_shape=[
            jax.ShapeDtypeStruct((2, ncb, t, tc), BF16),
            jax.ShapeDtypeStruct((3, D_FF), F32),
            jax.ShapeDtypeStruct((1, D_FF), F32),
        ],
        scratch_shapes=[pltpu.VMEM((SUBLANES, tc), F32)],
        compiler_params=_cparams(("parallel", "arbitrary")),
    )(up, up, up, conv_w, conv_b, dact)
    return dup.reshape(2 * ncb, t, tc), dw, db


def _split3(x):
    x1 = x.astype(BF16)
    r1 = x - x1.astype(F32)
    x2 = r1.astype(BF16)
    x3 = (r1 - x2.astype(F32)).astype(BF16)
    return x1, x2, x3


def _tri_dot(tri, x, dims):
    x1, x2, x3 = _split3(x)
    return _dot(tri, x1, dims) + _dot(tri, x2, dims) + _dot(tri, x3, dims)


def _lower_bound(logits_ref):
    return _sigmoid(logits_ref[0:1, :] - logits_ref[1:2, :])


def _hg_gates(qr, fr, lb):
    q = qr * _sigmoid(qr) * (HG_DK ** -0.5)
    sf = _sigmoid(fr)
    fg = lb + (1.0 - lb) * sf
    return q, sf, fg


def _hg_chunk_terms(q, fg, tril_b, low_half):
    g = jnp.log(fg)
    k = 1.0 - fg
    cum = _tri_dot(tril_b, g, NN)
    c_last = jnp.sum(g, axis=0, keepdims=True)
    c_mid = jnp.sum(jnp.where(low_half, g, 0.0), axis=0, keepdims=True)
    e_q = jnp.exp(cum - c_mid)
    e_k = jnp.exp(c_mid - cum)
    e_0 = jnp.exp(cum)
    e_l = jnp.exp(c_last - cum)
    return k, e_q, e_k, e_0, e_l, jnp.exp(c_last)


HG_BLOCK = 256


def _hg_proj_specs(rb, row):
    return [pl.BlockSpec((rb, D_MODEL), functools.partial(lambda i, k: (row(i), k), k=k)) for k in range(4)]


def _hg_consts(c):
    tril = lax.broadcasted_iota(jnp.int32, (c, c), 0) >= lax.broadcasted_iota(jnp.int32, (c, c), 1)
    low_half = lax.broadcasted_iota(jnp.int32, (c, HG_DK), 0) < c // 2
    return tril, tril.astype(BF16), low_half


def _hgrn_fwd(proj, lb, wn):
    t = proj.shape[0]
    c = HG_CHUNK
    rb = min(t, HG_BLOCK)
    cpb = rb // c

    def body(q_ref, f_ref, i_ref, g_ref, lb_ref, wn_ref, o_ref, y_ref, st_ref, s_scr):
        @pl.when(pl.program_id(0) == 0)
        def _():
            s_scr[...] = jnp.zeros_like(s_scr)

        lb_all = _lower_bound(lb_ref)
        wnv = wn_ref[...]
        tril, tril_b, low_half = _hg_consts(c)

        def chunk(n, carry):
            rows = pl.ds(pl.multiple_of(n * c, c), c)
            for h in range(HG_HEADS):
                cols = slice(h * HG_DK, (h + 1) * HG_DK)
                q, _, fg = _hg_gates(q_ref[rows, cols], f_ref[rows, cols], lb_all[:, cols])
                v = i_ref[rows, cols].astype(BF16)
                k, e_q, e_k, e_0, e_l, e_last = _hg_chunk_terms(q, fg, tril_b, low_half)
                st = s_scr[h]
                st_ref[h, n] = st
                a = jnp.where(tril, _dot((q * e_q).astype(BF16), (k * e_k).astype(BF16), NT), 0.0)
                o = _dot((q * e_0).astype(BF16), st.astype(BF16), NT) + _dot(a.astype(BF16), v, NN)
                s_scr[h] = st * e_last + _dot(v, (k * e_l).astype(BF16), TN)
                o_ref[rows, cols] = o
                rstd = lax.rsqrt(jnp.mean(o * o, axis=-1, keepdims=True) + EPS)
                gr = g_ref[rows, cols]
                y_ref[rows, cols] = (o * rstd * wnv * (gr * _sigmoid(gr))).astype(BF16)
            return carry

        lax.fori_loop(0, cpb, chunk, 0)

    blk = pl.BlockSpec((rb, D_MODEL), lambda i: (i, 0))
    return pl.pallas_call(
        body,
        name="hgrn_fwd",
        grid=(t // rb,),
        in_specs=_hg_proj_specs(rb, lambda i: i) + [pl.BlockSpec((2, D_MODEL), lambda i: (0, 0)), pl.BlockSpec((1, HG_DK), lambda i: (0, 0))],
        out_specs=[blk, blk, pl.BlockSpec((HG_HEADS, cpb, HG_DK, HG_DK), lambda i: (0, i, 0, 0))],
        out_shape=[
            jax.ShapeDtypeStruct((t, D_MODEL), F32),
            jax.ShapeDtypeStruct((t, D_MODEL), BF16),
            jax.ShapeDtypeStruct((HG_HEADS, t // c, HG_DK, HG_DK), F32),
        ],
        scratch_shapes=[pltpu.VMEM((HG_HEADS, HG_DK, HG_DK), F32)],
        compiler_params=_cparams(("arbitrary",)),
    )(proj, proj, proj, proj, lb, wn)


def _hgrn_bwd(proj, lb, wn, o, states, dy):
    t = proj.shape[0]
    c = HG_CHUNK
    rb = min(t, HG_BLOCK)
    cpb = rb // c
    nb = t // rb

    def body(q_ref, f_ref, i_ref, g_ref, lb_ref, wn_ref, o_ref, st_ref, dy_ref, dp_ref, dl_ref, dwn_ref, ds_scr, dlb_scr):
        step = pl.program_id(0)

        @pl.when(step == 0)
        def _():
            dwn_ref[...] = jnp.zeros_like(dwn_ref)
            ds_scr[...] = jnp.zeros_like(ds_scr)
            dlb_scr[...] = jnp.zeros_like(dlb_scr)

        lb_all = _lower_bound(lb_ref)
        wnv = wn_ref[...]
        tril, tril_b, low_half = _hg_consts(c)

        def chunk(nn, carry):
            n = cpb - 1 - nn
            rows = pl.ds(pl.multiple_of(n * c, c), c)
            for h in range(HG_HEADS):
                cols = slice(h * HG_DK, (h + 1) * HG_DK)
                lbv = lb_all[:, cols]
                ov = o_ref[rows, cols]
                gr = g_ref[rows, cols]
                dyv = dy_ref[rows, cols].astype(F32)
                rstd = lax.rsqrt(jnp.mean(ov * ov, axis=-1, keepdims=True) + EPS)
                ohat = ov * rstd
                sg = _sigmoid(gr)
                dg_raw = dyv * (ohat * wnv) * (sg * (1.0 + gr * (1.0 - sg)))
                don = dyv * (gr * sg)
                dwn_ref[...] += jnp.sum(don * ohat, axis=0, keepdims=True)
                gd = don * wnv
                do = rstd * (gd - ohat * jnp.mean(gd * ohat, axis=-1, keepdims=True))
                do_b = do.astype(BF16)
                qr = q_ref[rows, cols]
                q, sf, fg = _hg_gates(qr, f_ref[rows, cols], lbv)
                v = i_ref[rows, cols].astype(BF16)
                k, e_q, e_k, e_0, e_l, e_last = _hg_chunk_terms(q, fg, tril_b, low_half)
                qi, qi_lo, _ = _split3(q * e_q)
                ki, ki_lo, _ = _split3(k * e_k)
                q0 = (q * e_0).astype(BF16)
                kl = (k * e_l).astype(BF16)
                st = st_ref[h, n]
                st_b = st.astype(BF16)
                ds = ds_scr[h]
                ds_b = ds.astype(BF16)
                a_b = jnp.where(tril, _dot(qi, ki, NT), 0.0).astype(BF16)
                da_b = jnp.where(tril, _dot(do_b, v, NT), 0.0).astype(BF16)
                dq = _dot(do_b, st_b, NN) * e_0 + (_dot(da_b, ki, NN) + _dot(da_b, ki_lo, NN)) * e_q
                dk_state = _dot(v, ds_b, NN) * e_l
                dk = (_dot(da_b, qi, TN) + _dot(da_b, qi_lo, TN)) * e_k + dk_state
                dv = _dot(a_b, do_b, TN) + _dot(kl, ds_b, NT)
                ds_scr[h] = ds * e_last + _dot(do_b, q0, TN)
                d_last = jnp.sum(dk_state * k, axis=0, keepdims=True) + jnp.sum(ds * st, axis=0, keepdims=True) * e_last
                dlogf = _tri_dot(tril_b, q * dq - k * dk, TN) + d_last
                dfg = dlogf / fg - dk
                dlb_scr[:, cols] += jnp.sum(dfg * (1.0 - sf), axis=0, keepdims=True)
                sq = _sigmoid(qr)
                dp_ref[0, rows, cols] = (dq * (HG_DK ** -0.5) * (sq * (1.0 + qr * (1.0 - sq)))).astype(BF16)
                dp_ref[1, rows, cols] = (dfg * (1.0 - lbv) * sf * (1.0 - sf)).astype(BF16)
                dp_ref[2, rows, cols] = dv.astype(BF16)
                dp_ref[3, rows, cols] = dg_raw.astype(BF16)
            return carry

        lax.fori_loop(0, cpb, chunk, 0)

        @pl.when(step == nb - 1)
        def _():
            d0 = dlb_scr[...] * lb_all * (1.0 - lb_all)
            dl_ref[0:1, :] = d0
            dl_ref[1:2, :] = -d0

    rev = lambda i: nb - 1 - i
    blk = pl.BlockSpec((rb, D_MODEL), lambda i: (rev(i), 0))
    return pl.pallas_call(
        body,
        name="hgrn_bwd",
        grid=(nb,),
        in_specs=_hg_proj_specs(rb, rev)
        + [pl.BlockSpec((2, D_MODEL), lambda i: (0, 0)), pl.BlockSpec((1, HG_DK), lambda i: (0, 0)), blk,
           pl.BlockSpec((HG_HEADS, cpb, HG_DK, HG_DK), lambda i: (0, rev(i), 0, 0)), blk],
        out_specs=[
            pl.BlockSpec((4, rb, D_MODEL), lambda i: (0, rev(i), 0)),
            pl.BlockSpec((2, D_MODEL), lambda i: (0, 0)),
            pl.BlockSpec((1, HG_DK), lambda i: (0, 0)),
        ],
        out_shape=[
            jax.ShapeDtypeStruct((4, t, D_MODEL), BF16),
            jax.ShapeDtypeStruct((2, D_MODEL), F32),
            jax.ShapeDtypeStruct((1, HG_DK), F32),
        ],
        scratch_shapes=[pltpu.VMEM((HG_HEADS, HG_DK, HG_DK), F32), pltpu.VMEM((1, D_MODEL), F32)],
        compiler_params=_cparams(("arbitrary",)),
    )(proj, proj, proj, proj, lb, wn, o, states, dy)


def _att_masks(n):
    tq = lax.broadcasted_iota(jnp.int32, (WINDOW, WINDOW), 0)
    sk = lax.broadcasted_iota(jnp.int32, (WINDOW, WINDOW), 1)
    valid_c = sk <= tq
    valid_p = (sk - tq) > jnp.where(n > 0, 0, WINDOW)
    dist_c = (tq - sk).astype(F32)
    dist_p = dist_c + float(WINDOW)
    return valid_p, valid_c, dist_p, dist_c


def _att_halves(x, lo, kh):
    r = pltpu.roll(x, ATT_HD, 1)
    zero = jnp.zeros_like(x)
    if kh == 0:
        return jnp.where(lo, x, r), jnp.where(lo, x, zero), jnp.where(lo, zero, r)
    return jnp.where(lo, r, x), jnp.where(lo, r, zero), jnp.where(lo, zero, x)


def _att_probs(qm, k2p, k2c, masks, slope, sink):
    valid_p, valid_c, dist_p, dist_c = masks
    sp = jnp.where(valid_p, _dot(qm, k2p, NT) * (ATT_HD ** -0.5) - slope * dist_p, NEG)
    sc = jnp.where(valid_c, _dot(qm, k2c, NT) * (ATT_HD ** -0.5) - slope * dist_c, NEG)
    m = jnp.maximum(jnp.maximum(jnp.max(sp, axis=-1, keepdims=True), jnp.max(sc, axis=-1, keepdims=True)), sink)
    ep = jnp.exp(sp - m)
    ec = jnp.exp(sc - m)
    es = jnp.exp(sink - m)
    inv = 1.0 / (jnp.sum(ep, axis=-1, keepdims=True) + jnp.sum(ec, axis=-1, keepdims=True) + es)
    return ep * inv, ec * inv, es * inv


def _attn_fwd(q, kv, sinks):
    t = q.shape[0]
    nb = t // WINDOW

    def body(sink_ref, q_ref, kvp_ref, kvc_ref, o_ref):
        n = pl.program_id(0)
        masks = _att_masks(n)
        lo = lax.broadcasted_iota(jnp.int32, (WINDOW, LANES), 1) < ATT_HD
        for kh in range(ATT_KVH):
            k2p, _, _ = _att_halves(kvp_ref[:, 0:LANES], lo, kh)
            k2c, _, _ = _att_halves(kvc_ref[:, 0:LANES], lo, kh)
            _, vlo_p, vhi_p = _att_halves(kvp_ref[:, LANES:2 * LANES], lo, kh)
            _, vlo_c, vhi_c = _att_halves(kvc_ref[:, LANES:2 * LANES], lo, kh)
            for jj in range(ATT_GROUP // 2):
                j = kh * (ATT_GROUP // 2) + jj
                qp = q_ref[:, j * LANES:(j + 1) * LANES]
                zero = jnp.zeros_like(qp)
                out = None
                for par in range(2):
                    hq = 2 * j + par
                    qm = jnp.where(lo, qp, zero) if par == 0 else jnp.where(lo, zero, qp)
                    pp, pc, _ = _att_probs(qm, k2p, k2c, masks, ALIBI_SLOPES[hq], sink_ref[hq])
                    vp, vc = (vlo_p, vlo_c) if par == 0 else (vhi_p, vhi_c)
                    part = _dot(pp.astype(BF16), vp, NN) + _dot(pc.astype(BF16), vc, NN)
                    out = part if out is None else out + part
                o_ref[:, j * LANES:(j + 1) * LANES] = out.astype(BF16)

    return pl.pallas_call(
        body,
        name="attn_fwd",
        grid=(nb,),
        in_specs=[
            pl.BlockSpec(memory_space=pltpu.SMEM),
            pl.BlockSpec((WINDOW, D_MODEL), lambda n: (n, 0)),
            pl.BlockSpec((WINDOW, 2 * LANES), lambda n: (jnp.maximum(n - 1, 0), 0)),
            pl.BlockSpec((WINDOW, 2 * LANES), lambda n: (n, 0)),
        ],
        out_specs=pl.BlockSpec((WINDOW, D_MODEL), lambda n: (n, 0)),
        out_shape=jax.ShapeDtypeStruct((t, D_MODEL), BF16),
        compiler_params=_cparams(("parallel",)),
    )(sinks, q, kv, kv)


def _attn_bwd(q, kv, sinks, dout):
    t = q.shape[0]
    nb = t // WINDOW

    def body(sink_ref, q_ref, kvp_ref, kvc_ref, do_ref, dq_ref, dkv_ref, dsink_ref, carry_ref):
        n = pl.program_id(0)

        @pl.when(n == 0)
        def _():
            carry_ref[...] = jnp.zeros_like(carry_ref)
            dsink_ref[...] = jnp.zeros_like(dsink_ref)

        @pl.when(n == nb)
        def _():
            dkv_ref[...] = carry_ref[...].astype(BF16)

        @pl.when(n < nb)
        def _():
            masks = _att_masks(n)
            lo = lax.broadcasted_iota(jnp.int32, (WINDOW, LANES), 1) < ATT_HD
            lane1 = lax.broadcasted_iota(jnp.int32, (1, LANES), 1)
            dsink = jnp.zeros((1, LANES), F32)
            halves = []
            for kh in range(ATT_KVH):
                k2p, klo_p, khi_p = _att_halves(kvp_ref[:, 0:LANES], lo, kh)
                k2c, klo_c, khi_c = _att_halves(kvc_ref[:, 0:LANES], lo, kh)
                v2p, _, _ = _att_halves(kvp_ref[:, LANES:2 * LANES], lo, kh)
                v2c, _, _ = _att_halves(kvc_ref[:, LANES:2 * LANES], lo, kh)
                acc = [jnp.zeros((WINDOW, LANES), F32) for _ in range(4)]
                for jj in range(ATT_GROUP // 2):
                    j = kh * (ATT_GROUP // 2) + jj
                    cols = slice(j * LANES, (j + 1) * LANES)
                    qp = q_ref[:, cols]
                    dop = do_ref[:, cols]
                    zero = jnp.zeros_like(qp)
                    dq_pair = None
                    for par in range(2):
                        hq = 2 * j + par
                        sel = lo if par == 0 else jnp.logical_not(lo)
                        qm = jnp.where(sel, qp, zero)
                        dom = jnp.where(sel, dop, zero)
                        pp, pc, ps = _att_probs(qm, k2p, k2c, masks, ALIBI_SLOPES[hq], sink_ref[hq])
                        dpp = _dot(dom, v2p, NT)
                        dpc = _dot(dom, v2c, NT)
                        delta = jnp.sum(pp * dpp, axis=-1, keepdims=True) + jnp.sum(pc * dpc, axis=-1, keepdims=True)
                        dsp = (pp * (dpp - delta)).astype(BF16)
                        dsc = (pc * (dpc - delta)).astype(BF16)
                        dsink = dsink + jnp.where(lane1 == hq, -jnp.sum(ps * delta, axis=0, keepdims=True), 0.0)
                        kp_, kc_ = (klo_p, klo_c) if par == 0 else (khi_p, khi_c)
                        part = _dot(dsp, kp_, NN) + _dot(dsc, kc_, NN)
                        dq_pair = part if dq_pair is None else dq_pair + part
                        acc[0] = acc[0] + _dot(dsp, qm, TN)
                        acc[1] = acc[1] + _dot(dsc, qm, TN)
                        acc[2] = acc[2] + _dot(pp.astype(BF16), dom, TN)
                        acc[3] = acc[3] + _dot(pc.astype(BF16), dom, TN)
                    dq_ref[:, cols] = (dq_pair * (ATT_HD ** -0.5)).astype(BF16)
                halves.append([a + pltpu.roll(a, ATT_HD, 1) for a in acc])
            scale = ATT_HD ** -0.5
            prev = jnp.concatenate(
                [jnp.where(lo, halves[0][0], halves[1][0]) * scale, jnp.where(lo, halves[0][2], halves[1][2])], axis=1)
            cur = jnp.concatenate(
                [jnp.where(lo, halves[0][1], halves[1][1]) * scale, jnp.where(lo, halves[0][3], halves[1][3])], axis=1)
            dkv_ref[...] = (carry_ref[...] + prev).astype(BF16)
            carry_ref[...] = cur
            dsink_ref[...] += dsink

    blk = lambda n: jnp.minimum(n, nb - 1)
    return pl.pallas_call(
        body,
        name="attn_bwd",
        grid=(nb + 1,),
        in_specs=[
            pl.BlockSpec(memory_space=pltpu.SMEM),
            pl.BlockSpec((WINDOW, D_MODEL), lambda n: (blk(n), 0)),
            pl.BlockSpec((WINDOW, 2 * LANES), lambda n: (jnp.maximum(blk(n) - 1, 0), 0)),
            pl.BlockSpec((WINDOW, 2 * LANES), lambda n: (blk(n), 0)),
            pl.BlockSpec((WINDOW, D_MODEL), lambda n: (blk(n), 0)),
        ],
        out_specs=[
            pl.BlockSpec((WINDOW, D_MODEL), lambda n: (blk(n), 0)),
            pl.BlockSpec((WINDOW, 2 * LANES), lambda n: (jnp.maximum(n - 1, 0), 0)),
            pl.BlockSpec((1, LANES), lambda n: (0, 0)),
        ],
        out_shape=[
            jax.ShapeDtypeStruct((t, D_MODEL), BF16),
            jax.ShapeDtypeStruct((t, 2 * LANES), BF16),
            jax.ShapeDtypeStruct((1, LANES), F32),
        ],
        scratch_shapes=[pltpu.VMEM((WINDOW, 2 * LANES), F32)],
        compiler_params=_cparams(("arbitrary",)),
    )(sinks, q, kv, kv, dout)


def _ffn_fwd(h, norm_g, w_up, conv_w, conv_b, w_down, tag):
    xn = _rms_fwd(h, norm_g, name=f"ffn{tag}_norm")
    up = _mm_nn(xn, w_up, name=f"ffn{tag}_up")
    act = _conv_fwd(up, conv_w, conv_b, name=f"ffn{tag}_conv")
    h_out = _mm_nn(act, w_down, res=h, name=f"ffn{tag}_down")
    return h_out, (xn, up, act)


def _ffn_bwd(dh, h, norm_g, w_up, conv_w, conv_b, w_down, saved, tag, deps=()):
    xn, up, act = saved
    dw_down = _mm_tn(act, dh, 1, D_MODEL, name=f"ffn{tag}_dwdown", deps=deps)
    dact = _mm_nt(dh, w_down, name=f"ffn{tag}_dact", deps=deps)
    dup, dconv_w, dconv_b = _conv_bwd(up, conv_w, conv_b, dact, name=f"ffn{tag}_dconv")
    dw_up = _mm_tn(xn, dup, N_CHIPS, CONV_COLS, stacked=True, name=f"ffn{tag}_dwup")
    dxn = _mm_nt(dup, w_up, stacked=True, name=f"ffn{tag}_dxn")
    dh_in, dnorm = _rms_bwd(h, norm_g, dxn, dh, name=f"ffn{tag}_dnorm")
    return dh_in, dict(ffn_w_down=dw_down, ffn_w_up=dw_up, ffn_conv_w=dconv_w, ffn_conv_b=dconv_b, ffn_norm=dnorm)


def _local_step(x, target, w, fetch=lambda w, stage, after: w, hook=lambda point, dh, grads: ()):
    xn0 = _rms_fwd(x, w["hg_norm"], name="hg_norm")
    proj = _mm_nn(xn0, w["hg_w_in"], name="hg_in")
    o, y, states = _hgrn_fwd(proj, w["hg_lb"], w["hg_out_norm"])
    w = fetch(w, "layer0", y)
    h_a = _mm_nn(y, w["hg_w_out"], res=x, name="hg_out")
    h1, ffn0 = _ffn_fwd(h_a, w["ffn_norm"][0], w["ffn_w_up"][0], w["ffn_conv_w"][0], w["ffn_conv_b"][0], w["ffn_w_down"][0], 0)
    w = fetch(w, "layer1", h1)
    kvn = _rms_fwd(h1, w["kv_norm"], name="kv_norm")
    kv = _mm_nn(kvn, w["w_kv"], out_dtype=BF16, name="kv_proj")
    xa = _rms_fwd(h1, w["attn_norm"], name="attn_norm")
    qa = _mm_nn(xa, w["attn_w_q"], out_dtype=BF16, name="attn_q")
    ao = _attn_fwd(qa, kv, w["attn_sinks"])
    h_b = _mm_nn(ao, w["attn_w_o"], res=h1, name="attn_o")
    h2, ffn1 = _ffn_fwd(h_b, w["ffn_norm"][1], w["ffn_w_up"][1], w["ffn_conv_w"][1], w["ffn_conv_b"][1], w["ffn_w_down"][1], 1)
    dh2, d_final, loss = _loss_head(h2, w["final_norm"], target)

    dh_b, g1 = _ffn_bwd(dh2, h_b, w["ffn_norm"][1], w["ffn_w_up"][1], w["ffn_conv_w"][1], w["ffn_conv_b"][1], w["ffn_w_down"][1], ffn1, 1)
    deps = hook("ffn1", dh_b, g1)
    dw_o = _mm_tn(ao, dh_b, 1, D_MODEL, name="attn_dwo", deps=deps)
    dao = _mm_nt(dh_b, w["attn_w_o"], out_dtype=BF16, name="attn_dao", deps=deps)
    dqa, dkv, dsinks = _attn_bwd(qa, kv, w["attn_sinks"], dao)
    dw_q = _mm_tn(xa, dqa, 1, D_MODEL, name="attn_dwq")
    dxa = _mm_nt(dqa, w["attn_w_q"], name="attn_dxa")
    dh1, d_attn_norm = _rms_bwd(h1, w["attn_norm"], dxa, dh_b, name="attn_dnorm")
    dw_kv = _mm_tn(kvn, dkv, 1, 2 * LANES, name="kv_dw")
    dkvn = _mm_nt(dkv, w["w_kv"], name="kv_dx")
    dh1, d_kv_norm = _rms_bwd(h1, w["kv_norm"], dkvn, dh1, name="kv_dnorm")
    deps = hook("attn", dh1, dict(attn_w_o=dw_o, attn_w_q=dw_q, w_kv=dw_kv))
    dh_a, g0 = _ffn_bwd(dh1, h_a, w["ffn_norm"][0], w["ffn_w_up"][0], w["ffn_conv_w"][0], w["ffn_conv_b"][0], w["ffn_w_down"][0], ffn0, 0, deps)
    deps = hook("ffn0", dh_a, g0)
    dw_out = _mm_tn(y, dh_a, 1, D_MODEL, name="hg_dwout", deps=deps)
    dy = _mm_nt(dh_a, w["hg_w_out"], out_dtype=BF16, name="hg_dy", deps=deps)
    dproj, dlb, d_out_norm = _hgrn_bwd(proj, w["hg_lb"], w["hg_out_norm"], o, states, dy)
    deps = hook("hgrn", dproj, None)
    dw_in = _mm_tn(xn0, dproj, N_CHIPS, D_MODEL, stacked=True, name="hg_dwin", deps=deps)
    deps = hook("hg_w", dproj, dict(hg_w_out=dw_out, hg_w_in=dw_in))
    dxn0 = _mm_nt(dproj, w["hg_w_in"], stacked=True, name="hg_dxn", deps=deps)
    dx, d_hg_norm = _rms_bwd(x, w["hg_norm"], dxn0, dh_a, name="hg_dnorm")

    grads = dict(
        hg_norm=d_hg_norm, hg_w_in=dw_in, hg_lb=dlb, hg_out_norm=d_out_norm, hg_w_out=dw_out,
        kv_norm=d_kv_norm, w_kv=dw_kv, attn_norm=d_attn_norm, attn_w_q=dw_q, attn_sinks=dsinks, attn_w_o=dw_o,
        final_norm=d_final,
    )
    for name in g0:
        grads[name] = [g0[name], g1[name]]
    return loss, dx, grads


ANY = pl.BlockSpec(memory_space=pl.ANY)


def _place():
    x, y, c = lax.axis_index("x"), lax.axis_index("y"), lax.axis_index("c")
    chips = [(1 - x, y), (x, 1 - y), (1 - x, 1 - y)]
    return x, y, c, chips


def _rcopy(src, dst, send_sem, recv_sem, to):
    return pltpu.make_async_remote_copy(src_ref=src, dst_ref=dst, send_sem=send_sem, recv_sem=recv_sem, device_id=to, device_id_type=MESH)


HBM = pl.BlockSpec(memory_space=pltpu.HBM)
SEM = pl.BlockSpec(memory_space=pltpu.SEMAPHORE)
EFFECT = pltpu.SideEffectType.DATAFLOW_SIDE_EFFECTING


def _in_hbm(a):
    return pltpu.with_memory_space_constraint(a, pltpu.HBM)


def _place_shard(shard, place, dtype, name, deps=()):
    r, cols = shard.shape
    tr = _pick(r, ELEM_ROWS)

    def body(place_ref, s_ref, *rest):
        o_ref = rest[-1]
        o_ref[...] = s_ref[...].astype(o_ref.dtype)

    return pl.pallas_call(
        body,
        name=name,
        grid_spec=pltpu.PrefetchScalarGridSpec(
            num_scalar_prefetch=1,
            grid=(r // tr,),
            in_specs=[pl.BlockSpec((tr, cols), lambda i, place_ref: (i, 0))] + _dep_specs(deps),
            out_specs=pl.BlockSpec((None, tr, cols), lambda i, place_ref: (place_ref[0], i, 0)),
        ),
        out_shape=jax.ShapeDtypeStruct((N_CHIPS, r, cols), dtype),
        compiler_params=_cparams(("parallel",)),
    )(place, shard, *deps)


def _start_copies(name, bufs, n_sem, copies):
    n = len(bufs)

    def body(*refs):
        for cp in copies(refs[:n], refs[n], refs[n + 1]):
            cp.start()
        refs[-1][...] = jnp.zeros_like(refs[-1])

    outs = pl.pallas_call(
        body,
        name=name,
        in_specs=[HBM] * n,
        out_specs=[SEM, SEM] + [HBM] * n + [pl.BlockSpec(memory_space=pltpu.VMEM)],
        out_shape=[pltpu.SemaphoreType.DMA((n_sem,)), pltpu.SemaphoreType.DMA((n_sem,))] + [pltpu.HBM(b.shape, b.dtype) for b in bufs]
        + [jax.ShapeDtypeStruct((SUBLANES, LANES), F32)],
        input_output_aliases={i: 2 + i for i in range(n)},
        compiler_params=pltpu.CompilerParams(has_side_effects=EFFECT),
    )(*[_in_hbm(b) for b in bufs])
    return outs[0], outs[1], list(outs[2:-1]), outs[-1]


def _wait_copies(name, bufs, send_sems, recv_sems, after, copies):
    n = len(bufs)

    def body(*refs):
        for cp in copies(refs[:n], refs[n], refs[n + 1]):
            cp.wait_send()
            cp.wait_recv()

    return pl.pallas_call(
        body,
        name=name,
        in_specs=[HBM] * n + [SEM, SEM, ANY],
        out_specs=[HBM] * n,
        out_shape=[pltpu.HBM(b.shape, b.dtype) for b in bufs],
        input_output_aliases={i: i for i in range(n)},
        compiler_params=pltpu.CompilerParams(has_side_effects=EFFECT),
    )(*bufs, send_sems, recv_sems, after)


def _gather_copies(first, count):
    def copies(refs, send_sems, recv_sems):
        x, y, c, chips = _place()
        me = 2 * x + y
        out = []
        for i in range(count):
            for j, (px, py) in enumerate(chips):
                k = 3 * (first + i) + j
                out.append(_rcopy(refs[i].at[me], refs[i].at[me], send_sems.at[k], recv_sems.at[k], (px, py, c)))
        return out

    return copies


def _swap_copies(n):
    def copies(refs, send_sems, recv_sems):
        x, y, c, _ = _place()
        out = []
        for i in range(n):
            h = refs[i].shape[1] // 2
            out.append(_rcopy(refs[i].at[:, pl.ds((1 - c) * h, h)], refs[n + i], send_sems.at[i], recv_sems.at[i], (x, y, 1 - c)))
        return out

    return copies


def _partial_copies(n):
    def copies(refs, send_sems, recv_sems):
        x, y, c, chips = _place()
        out = []
        for i in range(n):
            for j, (px, py) in enumerate(chips):
                out.append(_rcopy(refs[i].at[2 * px + py], refs[n + i].at[j], send_sems.at[3 * i + j], recv_sems.at[3 * i + j], (px, py, c)))
        return out

    return copies


def _share_copies(n):
    def copies(refs, send_sems, recv_sems):
        x, y, c, _ = _place()
        return [_rcopy(refs[i].at[c], refs[i].at[c], send_sems.at[i], recv_sems.at[i], (x, y, 1 - c)) for i in range(n)]

    return copies


def _allreduce_small(vec):
    rows = vec.shape[0]

    def body(v_ref, o_ref, buf, send_sems, recv_sems):
        x, y, c, _ = _place()
        me = 4 * x + 2 * y + c
        buf[me] = v_ref[...]
        copies = []
        for k in range(1, N_DEV):
            peer = (x ^ (k >> 2), y ^ ((k >> 1) & 1), c ^ (k & 1))
            cp = _rcopy(v_ref, buf.at[me], send_sems.at[k - 1], recv_sems.at[k - 1], peer)
            cp.start()
            copies.append(cp)
        for cp in copies:
            cp.wait()
        acc = buf[0]
        for d in range(1, N_DEV):
            acc = acc + buf[d]
        o_ref[...] = acc

    return pl.pallas_call(
        body,
        name="allreduce_small",
        in_specs=[pl.BlockSpec(memory_space=pltpu.VMEM)],
        out_specs=pl.BlockSpec(memory_space=pltpu.VMEM),
        out_shape=jax.ShapeDtypeStruct(vec.shape, F32),
        scratch_shapes=[pltpu.VMEM((N_DEV, rows, LANES), F32), pltpu.SemaphoreType.DMA((N_DEV - 1,)), pltpu.SemaphoreType.DMA((N_DEV - 1,))],
        compiler_params=pltpu.CompilerParams(vmem_limit_bytes=VMEM_LIMIT_BYTES),
    )(vec)


class _Reduction:
    def __init__(self, tag, grads, place, core):
        self.tag, self.n, self.place, self.core = tag, len(grads), place, core
        lands = [lax.empty((N_CHIPS, g.shape[1] // 2, g.shape[2]), F32) for g in grads]
        self._start("swap", list(grads) + lands, self.n, _swap_copies(self.n))

    def _start(self, stage, bufs, n_sem, copies):
        *self.flight, self.token = _start_copies(f"rs_{stage}_start_{self.tag}", bufs, n_sem, copies)

    def _landed(self, stage, after, copies):
        send_sems, recv_sems, bufs = self.flight
        return _wait_copies(f"rs_{stage}_wait_{self.tag}", bufs, send_sems, recv_sems, after, copies)

    def to_chips(self, after):
        n = self.n
        bufs = self._landed("swap", after, _swap_copies(n))
        sums = [_add_core_halves(g, o, self.core, name=f"rs_add_core_{self.tag}_{i}") for i, (g, o) in enumerate(zip(bufs[:n], bufs[n:]))]
        self.mine = [f for f, _ in sums]
        parts = [b for _, b in sums]
        lands = [lax.empty((3,) + p.shape[1:], BF16) for p in parts]
        self._start("send", parts + lands, 3 * n, _partial_copies(n))

    def to_core(self, after):
        n = self.n
        bufs = self._landed("send", after, _partial_copies(n))
        halves = [_add_chip_partials(f, o, self.place, name=f"rs_add_chip_{self.tag}_{i}") for i, (f, o) in enumerate(zip(self.mine, bufs[n:]))]
        self._start("share", halves, n, _share_copies(n))

    def finish(self, after):
        return [b.reshape((-1,) + b.shape[2:]) for b in self._landed("share", after, _share_copies(self.n))]


ELEM_ROWS = (256, 176, 128, 64, 32, 16, 8)


def _add_core_halves(grad, got, c, name):
    s, r, cols = grad.shape
    h = r // 2
    tr = _pick(h, ELEM_ROWS)

    def body(c_ref, g_ref, o_ref, f_ref, b_ref):
        acc = g_ref[...] + o_ref[...]
        f_ref[...] = acc
        b_ref[...] = acc.astype(BF16)

    blk = pl.BlockSpec((None, tr, cols), lambda k, i, c_ref: (k, i, 0))
    return pl.pallas_call(
        body,
        name=name,
        grid_spec=pltpu.PrefetchScalarGridSpec(
            num_scalar_prefetch=1,
            grid=(s, h // tr),
            in_specs=[pl.BlockSpec((None, None, tr, cols), lambda k, i, c_ref: (k, c_ref[0], i, 0)), blk],
            out_specs=[blk, blk],
        ),
        out_shape=[jax.ShapeDtypeStruct((s, h, cols), F32), jax.ShapeDtypeStruct((s, h, cols), BF16)],
        compiler_params=_cparams(("parallel", "parallel")),
    )(c, grad.reshape(s, 2, h, cols), got)


def _add_chip_partials(mine, got, place, name):
    _, h, cols = mine.shape
    tr = _pick(h, ELEM_ROWS)

    def body(place_ref, m_ref, g_ref, o_ref):
        acc = m_ref[...]
        for j in range(3):
            acc = acc + g_ref[j].astype(F32)
        o_ref[...] = acc

    return pl.pallas_call(
        body,
        name=name,
        grid_spec=pltpu.PrefetchScalarGridSpec(
            num_scalar_prefetch=1,
            grid=(h // tr,),
            in_specs=[
                pl.BlockSpec((None, tr, cols), lambda i, place_ref: (place_ref[0], i, 0)),
                pl.BlockSpec((3, tr, cols), lambda i, place_ref: (0, i, 0)),
            ],
            out_specs=pl.BlockSpec((None, tr, cols), lambda i, place_ref: (place_ref[1], i, 0)),
        ),
        out_shape=jax.ShapeDtypeStruct((2, h, cols), F32),
        compiler_params=_cparams(("parallel",)),
    )(place, mine, got)


def _adamw_math(w, m, v, g):
    nm = ADAM_B1 * m + (1.0 - ADAM_B1) * g
    nv = ADAM_B2 * v + (1.0 - ADAM_B2) * (g * g)
    m_hat = nm * (1.0 / (1.0 - ADAM_B1 ** ADAM_STEP))
    v_hat = nv * (1.0 / (1.0 - ADAM_B2 ** ADAM_STEP))
    return -ADAM_LR * (m_hat / (jnp.sqrt(v_hat) + ADAM_EPS) + ADAM_WD * w), nm, nv


def _adamw_layer(w, m, v, g, layer, prev, name):
    nl, r, cols = w.shape
    tr = _pick(r, ELEM_ROWS)

    def body(w_ref, m_ref, v_ref, g_ref, *rest):
        go_ref, d_ref, nm_ref, nv_ref = rest[-4:]
        gv = g_ref[...]
        d_ref[...], nm_ref[...], nv_ref[...] = _adamw_math(w_ref[...], m_ref[...], v_ref[...], gv)
        go_ref[...] = gv

    lay = pl.BlockSpec((None, tr, cols), lambda i: (layer, i, 0))
    return pl.pallas_call(
        body,
        name=name,
        grid=(r // tr,),
        in_specs=[lay] * 3 + [pl.BlockSpec((tr, cols), lambda i: (i, 0))] + ([ANY] * 4 if prev else []),
        out_specs=[lay] * 4,
        out_shape=[jax.ShapeDtypeStruct((nl, r, cols), F32)] * 4,
        input_output_aliases={4 + k: k for k in range(4)} if prev else {},
        compiler_params=_cparams(("parallel",)),
    )(w, m, v, g, *(prev or ()))


def _adamw(w, m, v, g, name):
    r, cols = w.shape
    tr = _pick(r, ELEM_ROWS)

    def body(w_ref, m_ref, v_ref, g_ref, d_ref, nm_ref, nv_ref):
        d_ref[...], nm_ref[...], nv_ref[...] = _adamw_math(w_ref[...], m_ref[...], v_ref[...], g_ref[...])

    blk = pl.BlockSpec((tr, cols), lambda i: (i, 0))
    return pl.pallas_call(
        body,
        name=name,
        grid=(r // tr,),
        in_specs=[blk] * 4,
        out_specs=[blk] * 3,
        out_shape=[jax.ShapeDtypeStruct((r, cols), F32)] * 3,
        compiler_params=_cparams(("parallel",)),
    )(w, m, v, g)


SMALL_COLS = 384
SMALL_ROWS = 16


def _pad_rows(flat, rows, cols):
    return jnp.pad(flat, (0, rows * cols - flat.shape[0])).reshape(rows, cols)


def kernel(x, hg_norm, hg_w_in, hg_lb_logits, hg_out_norm, hg_w_out, kv_norm, w_kv, attn_norm, attn_w_q, attn_sinks, attn_w_o, ffn_norm, ffn_w_up, ffn_conv_w, ffn_conv_b, ffn_w_down, final_norm, loss_target, m_hg_norm, m_hg_w_in, m_hg_lb_logits, m_hg_out_norm, m_hg_w_out, m_kv_norm, m_w_kv, m_attn_norm, m_attn_w_q, m_attn_sinks, m_attn_w_o, m_ffn_norm, m_ffn_w_up, m_ffn_conv_w, m_ffn_conv_b, m_ffn_w_down, m_final_norm, v_hg_norm, v_hg_w_in, v_hg_lb_logits, v_hg_out_norm, v_hg_w_out, v_kv_norm, v_w_kv, v_attn_norm, v_attn_w_q, v_attn_sinks, v_attn_w_o, v_ffn_norm, v_ffn_w_up, v_ffn_conv_w, v_ffn_conv_b, v_ffn_w_down, v_final_norm):
    wts = dict(hg_norm=hg_norm, hg_w_in=hg_w_in, hg_lb_logits=hg_lb_logits, hg_out_norm=hg_out_norm, hg_w_out=hg_w_out, kv_norm=kv_norm, w_kv=w_kv, attn_norm=attn_norm, attn_w_q=attn_w_q, attn_sinks=attn_sinks, attn_w_o=attn_w_o, ffn_norm=ffn_norm, ffn_w_up=ffn_w_up, ffn_conv_w=ffn_conv_w, ffn_conv_b=ffn_conv_b, ffn_w_down=ffn_w_down, final_norm=final_norm)
    mom1 = dict(hg_norm=m_hg_norm, hg_w_in=m_hg_w_in, hg_lb_logits=m_hg_lb_logits, hg_out_norm=m_hg_out_norm, hg_w_out=m_hg_w_out, kv_norm=m_kv_norm, w_kv=m_w_kv, attn_norm=m_attn_norm, attn_w_q=m_attn_w_q, attn_sinks=m_attn_sinks, attn_w_o=m_attn_w_o, ffn_norm=m_ffn_norm, ffn_w_up=m_ffn_w_up, ffn_conv_w=m_ffn_conv_w, ffn_conv_b=m_ffn_conv_b, ffn_w_down=m_ffn_w_down, final_norm=m_final_norm)
    mom2 = dict(hg_norm=v_hg_norm, hg_w_in=v_hg_w_in, hg_lb_logits=v_hg_lb_logits, hg_out_norm=v_hg_out_norm, hg_w_out=v_hg_w_out, kv_norm=v_kv_norm, w_kv=v_w_kv, attn_norm=v_attn_norm, attn_w_q=v_attn_w_q, attn_sinks=v_attn_sinks, attn_w_o=v_attn_w_o, ffn_norm=v_ffn_norm, ffn_w_up=v_ffn_w_up, ffn_conv_w=v_ffn_conv_w, ffn_conv_b=v_ffn_conv_b, ffn_w_down=v_ffn_w_down, final_norm=v_final_norm)
    names = list(wts)
    chip = 2 * lax.axis_index("x") + lax.axis_index("y")
    core = lax.axis_index("c")
    core_arr = jnp.reshape(core, (1,)).astype(jnp.int32)
    fs = D_FF // N_CHIPS
    ds = D_MODEL // N_CHIPS

    place_arr = jnp.stack([chip, core]).astype(jnp.int32)
    small = jnp.concatenate([hg_norm.reshape(-1), hg_lb_logits.reshape(-1), ffn_conv_w.reshape(-1)])
    n_small = small.shape[0]
    shards = [
        ("small", _pad_rows(small, SMALL_ROWS, SMALL_COLS), F32), ("hg_w_in", hg_w_in[0], BF16),
        ("hg_w_out", hg_w_out[0], BF16), ("ffn_w_up0", ffn_w_up[0], BF16), ("ffn_w_down0", ffn_w_down[0], BF16),
        ("w_kv", w_kv, BF16), ("attn_w_q", attn_w_q[0], BF16), ("attn_w_o", attn_w_o[0], BF16),
        ("ffn_w_up1", ffn_w_up[1], BF16), ("ffn_w_down1", ffn_w_down[1], BF16),
    ]
    n_first = 2
    stages = dict(first=(0, 0, n_first), layer0=(1, 0, 3), layer1=(1, 3, 8))
    placed = [_place_shard(s, place_arr, dt, name=f"place_{nm}") for nm, s, dt in shards[:n_first]]
    flights = [_start_copies("gather_start_first", placed, 3 * n_first, _gather_copies(0, n_first))]
    placed = [_place_shard(s, place_arr, dt, name=f"place_{nm}", deps=(flights[0][3],)) for nm, s, dt in shards[n_first:]]
    flights.append(_start_copies("gather_start_rest", placed, 3 * len(placed), _gather_copies(0, len(placed))))

    def fetch(w, stage, after):
        call, lo, hi = stages[stage]
        send_sems, recv_sems, bufs, _ = flights[call]
        got = _wait_copies(f"gather_wait_{stage}", bufs[lo:hi], send_sems, recv_sems, after, _gather_copies(lo, hi - lo))
        w = dict(w)
        if stage == "first":
            g_small = got[0].reshape(N_CHIPS, -1)[:, :n_small]
            conv_w = g_small[:, 3 * ds:].reshape(N_CHIPS, 2, 3, fs).transpose(1, 2, 0, 3).reshape(2, 3, D_FF)
            w.update(
                hg_norm=g_small[:, :ds].reshape(1, D_MODEL),
                hg_lb=g_small[:, ds:3 * ds].reshape(N_CHIPS, 2, ds).transpose(1, 0, 2).reshape(2, D_MODEL),
                ffn_conv_w=[conv_w[0], conv_w[1]], hg_w_in=got[1],
            )
        elif stage == "layer0":
            w.update(hg_w_out=got[0].reshape(1, D_MODEL, D_MODEL), ffn_w_up=[got[1], None], ffn_w_down=[got[2].reshape(1, D_FF, D_MODEL), None])
        else:
            w.update(
                w_kv=got[0].reshape(1, D_MODEL, 2 * LANES), attn_w_q=got[1].reshape(1, D_MODEL, D_MODEL),
                attn_w_o=got[2].reshape(1, D_MODEL, D_MODEL), ffn_w_up=[w["ffn_w_up"][0], got[3]],
                ffn_w_down=[w["ffn_w_down"][0], got[4].reshape(1, D_FF, D_MODEL)],
            )
        return w

    whole = dict(
        hg_out_norm=hg_out_norm, kv_norm=kv_norm.reshape(1, D_MODEL), attn_norm=attn_norm, attn_sinks=attn_sinks.reshape(ATT_QH),
        ffn_norm=[ffn_norm[0:1], ffn_norm[1:2]], ffn_conv_b=[ffn_conv_b[0:1], ffn_conv_b[1:2]], final_norm=final_norm.reshape(1, D_MODEL),
    )
    whole = fetch(whole, "first", flights[1][3])

    red, layer1 = {}, {}

    def by_rows(g, rows):
        return g.reshape(N_CHIPS, rows, g.shape[2])

    def hook(point, dh, grads):
        if point == "ffn1":
            red["ffn1"] = _Reduction("ffn1", [by_rows(grads["ffn_w_down"], fs), grads["ffn_w_up"]], place_arr, core_arr)
            return (red["ffn1"].token,)
        if point == "attn":
            red["ffn1"].to_chips(dh)
            layer1.update(grads)
            return (red["ffn1"].token,)
        if point == "ffn0":
            group = [by_rows(layer1["attn_w_o"], ds), by_rows(layer1["attn_w_q"], ds), by_rows(layer1["w_kv"], ds),
                     by_rows(grads["ffn_w_down"], fs), grads["ffn_w_up"]]
            red["mid"] = _Reduction("mid", group, place_arr, core_arr)
            return (red["mid"].token,)
        if point == "hgrn":
            red["ffn1"].to_core(dh)
            red["mid"].to_chips(dh)
            return (red["ffn1"].token, red["mid"].token)
        red["hg"] = _Reduction("hg", [by_rows(grads["hg_w_out"], ds), grads["hg_w_in"]], place_arr, core_arr)
        return (red["hg"].token,)

    loss, dx, grads = _local_step(x[0], loss_target[0], whole, fetch, hook)

    small_parts = [
        loss.reshape(-1), grads["hg_out_norm"].reshape(-1), grads["attn_sinks"].reshape(-1), grads["kv_norm"].reshape(-1),
        grads["attn_norm"].reshape(-1), grads["ffn_norm"][0].reshape(-1), grads["ffn_norm"][1].reshape(-1),
        grads["ffn_conv_b"][0].reshape(-1), grads["ffn_conv_b"][1].reshape(-1), grads["final_norm"].reshape(-1),
        grads["hg_norm"].reshape(-1), grads["hg_lb"].reshape(-1), grads["ffn_conv_w"][0].reshape(-1), grads["ffn_conv_w"][1].reshape(-1),
    ]
    sizes = [p.shape[0] for p in small_parts]
    flat = jnp.concatenate(small_parts)
    rows = -(-flat.shape[0] // (SUBLANES * LANES)) * SUBLANES
    summed = _allreduce_small(_pad_rows(flat, rows, LANES)).reshape(-1)
    red["hg"].to_chips(summed)
    offs = [0]
    for sz in sizes:
        offs.append(offs[-1] + sz)
    sm = [summed[offs[i]:offs[i + 1]] for i in range(len(sizes))]
    loss_out = sm[0][0]
    conv_w_full = jnp.stack([sm[12].reshape(3, D_FF), sm[13].reshape(3, D_FF)])
    small_grads = dict(
        hg_out_norm=sm[1].reshape(1, HG_DK), attn_sinks=sm[2][:ATT_QH].reshape(1, ATT_QH), kv_norm=sm[3], attn_norm=sm[4].reshape(1, D_MODEL),
        ffn_norm=jnp.stack([sm[5], sm[6]]), ffn_conv_b=jnp.stack([sm[7], sm[8]]), final_norm=sm[9],
        hg_norm=lax.dynamic_slice(sm[10].reshape(1, D_MODEL), (0, chip * ds), (1, ds)),
        hg_lb_logits=lax.dynamic_slice(sm[11].reshape(2, D_MODEL), (0, chip * ds), (2, ds)),
        ffn_conv_w=lax.dynamic_slice(conv_w_full, (0, 0, chip * fs), (2, 3, fs)),
    )

    out_g, out_d, out_m, out_v = {}, {}, {}, {}

    def update(name, g2):
        shape = wts[name].shape
        d2, m2, v2 = _adamw(wts[name].reshape(g2.shape), mom1[name].reshape(g2.shape), mom2[name].reshape(g2.shape), g2, name=f"adamw_{name}")
        out_g[name], out_d[name], out_m[name], out_v[name] = g2.reshape(shape), d2.reshape(shape), m2.reshape(shape), v2.reshape(shape)
        return d2

    def update_layer(name, g2, layer, prev):
        res = _adamw_layer(wts[name], mom1[name], mom2[name], g2, layer, prev, name=f"adamw_{name}{layer}")
        out_g[name], out_d[name], out_m[name], out_v[name] = res
        return res

    g_down1, g_up1 = red["ffn1"].finish(red["hg"].token)
    down1 = update_layer("ffn_w_down", g_down1, 1, None)
    up1 = update_layer("ffn_w_up", g_up1, 1, None)
    red["mid"].to_core(up1[1])
    g_o, g_q, g_kv, g_down0, g_up0 = red["mid"].finish(up1[2])
    update("attn_w_o", g_o)
    update("attn_w_q", g_q)
    update("w_kv", g_kv)
    update_layer("ffn_w_down", g_down0, 0, down1)
    last = update_layer("ffn_w_up", g_up0, 0, up1)
    red["hg"].to_core(last[1])
    g_out, g_in = red["hg"].finish(last[2])
    update("hg_w_out", g_out)
    update("hg_w_in", g_in)

    small_names = [n for n in names if n not in out_g]
    cat = lambda d: jnp.concatenate([d[n].reshape(-1) for n in small_names])
    n_flat = sum(wts[n].size for n in small_names)
    srows = -(-n_flat // (SUBLANES * LANES)) * SUBLANES
    packed = [_pad_rows(cat(d), srows, LANES) for d in (wts, mom1, mom2, small_grads)]
    d_s, m_s, v_s = _adamw(*packed, name="adamw_small")
    off = 0
    for n in small_names:
        sz, shape = wts[n].size, wts[n].shape
        out_g[n] = small_grads[n].reshape(shape)
        out_d[n] = d_s.reshape(-1)[off:off + sz].reshape(shape)
        out_m[n] = m_s.reshape(-1)[off:off + sz].reshape(shape)
        out_v[n] = v_s.reshape(-1)[off:off + sz].reshape(shape)
        off += sz

    grad_x = dx.reshape(x.shape)
    return (loss_out, grad_x, *[out_g[n] for n in names], *[out_d[n] for n in names], *[out_m[n] for n in names], *[out_v[n] for n in names])
```

```python
import functools

import jax
import jax.numpy as jnp
from jax import lax
from jax.experimental import pallas as pl
from jax.experimental.pallas import tpu as pltpu

F32 = jnp.float32
BF16 = jnp.bfloat16
MESH = pl.DeviceIdType.MESH

EPS = 1e-6
D_MODEL = 1024
HG_HEADS = 8
HG_DK = 128
HG_CHUNK = 64
ATT_HD = 64
ATT_QH = 16
ATT_KVH = 2
ATT_GROUP = ATT_QH // ATT_KVH
WINDOW = 128
D_FF = 2816
N_CHIPS = 4
N_DEV = 8
LANES = 128
SUBLANES = 8
VMEM_LIMIT_BYTES = 56 * 1024 * 1024
NEG = -1e30
ALIBI_SLOPES = tuple(2.0 ** (-8.0 * h / ATT_QH) for h in range(1, ATT_QH + 1))

ADAM_LR = 0.001
ADAM_B1 = 0.9
ADAM_B2 = 0.999
ADAM_EPS = 1e-08
ADAM_WD = 0.01
ADAM_STEP = 10


def _cparams(sem=None):
    return pltpu.CompilerParams(dimension_semantics=sem, vmem_limit_bytes=VMEM_LIMIT_BYTES)


def _pick(n, cands):
    for c in cands:
        if n % c == 0:
            return c
    return n


def _sigmoid(x):
    return 1.0 / (1.0 + jnp.exp(-x))


def _dot(a, b, dims):
    return lax.dot_general(a, b, (dims, ((), ())), preferred_element_type=F32)


NN = ((1,), (0,))
NT = ((1,), (1,))
TN = ((0,), (0,))


MM_ROWS = 1024


def _mm_nn(a, w, res=None, out_dtype=F32, name="mm_nn"):
    m, k = a.shape
    s, _, ns = w.shape
    tm = min(m, MM_ROWS)
    tn = _pick(ns, (512, 1408, 256, 128))
    npb = ns // tn

    def body(a_ref, w_ref, *rest):
        o_ref = rest[-1]
        acc = _dot(a_ref[...].astype(BF16), w_ref[...], NN)
        if res is not None:
            acc = acc + rest[0][...]
        o_ref[...] = acc.astype(o_ref.dtype)

    in_specs = [
        pl.BlockSpec((tm, k), lambda i, j: (i, 0)),
        pl.BlockSpec((None, k, tn), lambda i, j: (j // npb, 0, j % npb)),
    ]
    args = [a, w]
    if res is not None:
        in_specs.append(pl.BlockSpec((tm, tn), lambda i, j: (i, j)))
        args.append(res)
    return pl.pallas_call(
        body,
        name=name,
        grid=(m // tm, s * npb),
        in_specs=in_specs,
        out_specs=pl.BlockSpec((tm, tn), lambda i, j: (i, j)),
        out_shape=jax.ShapeDtypeStruct((m, s * ns), out_dtype),
        compiler_params=_cparams(("parallel", "parallel")),
    )(*args)


def _dy_spec(stacked, tm, tn, npb, row, kk):
    if stacked:
        return pl.BlockSpec((None, tm, tn), lambda *g: (kk(g) // npb, row(g), kk(g) % npb))
    return pl.BlockSpec((tm, tn), lambda *g: (row(g), kk(g)))


def _dep_specs(deps):
    return [pl.BlockSpec(d.shape, lambda *g: (0, 0)) for d in deps]


def _mm_nt(dy, w, stacked=False, out_dtype=F32, name="mm_nt", deps=()):
    s, k, ns = w.shape
    m = dy.shape[1] if stacked else dy.shape[0]
    tm = min(m, MM_ROWS)
    tko = _pick(k, (1024, 1408, 512, 256))
    tn = _pick(ns, (1024, 1408, 512, 256))
    npb = ns // tn
    nk = s * npb

    def body(dy_ref, w_ref, *rest):
        o_ref, acc_ref = rest[-2:]
        kk = pl.program_id(2)

        @pl.when(kk == 0)
        def _():
            acc_ref[...] = jnp.zeros_like(acc_ref)

        acc_ref[...] += _dot(dy_ref[...].astype(BF16), w_ref[...], NT)

        @pl.when(kk == nk - 1)
        def _():
            o_ref[...] = acc_ref[...].astype(o_ref.dtype)

    return pl.pallas_call(
        body,
        name=name,
        grid=(m // tm, k // tko, nk),
        in_specs=[
            _dy_spec(stacked, tm, tn, npb, lambda g: g[0], lambda g: g[2]),
            pl.BlockSpec((None, tko, tn), lambda i, j, kk: (kk // npb, j, kk % npb)),
        ] + _dep_specs(deps),
        out_specs=pl.BlockSpec((tm, tko), lambda i, j, kk: (i, j)),
        out_shape=jax.ShapeDtypeStruct((m, k), out_dtype),
        scratch_shapes=[pltpu.VMEM((tm, tko), F32)],
        compiler_params=_cparams(("parallel", "parallel", "arbitrary")),
    )(dy, w, *deps)


def _mm_tn(a, dy, s, ns, stacked=False, name="mm_tn", deps=()):
    m, k = a.shape
    tm = min(m, MM_ROWS)
    tk = _pick(k, (1024, 1408, 512, 256))
    tn = _pick(ns, (512, 1408, 256, 128))
    npb = ns // tn
    nm = m // tm

    def body(a_ref, dy_ref, *rest):
        o_ref, acc_ref = rest[-2:]
        mm = pl.program_id(2)

        @pl.when(mm == 0)
        def _():
            acc_ref[...] = jnp.zeros_like(acc_ref)

        acc_ref[...] += _dot(a_ref[...].astype(BF16), dy_ref[...].astype(BF16), TN)

        @pl.when(mm == nm - 1)
        def _():
            o_ref[...] = acc_ref[...]

    return pl.pallas_call(
        body,
        name=name,
        grid=(k // tk, s * npb, nm),
        in_specs=[
            pl.BlockSpec((tm, tk), lambda i, j, mm: (mm, i)),
            _dy_spec(stacked, tm, tn, npb, lambda g: g[2], lambda g: g[1]),
        ] + _dep_specs(deps),
        out_specs=pl.BlockSpec((None, tk, tn), lambda i, j, mm: (j // npb, i, j % npb)),
        out_shape=jax.ShapeDtypeStruct((s, k, ns), F32),
        scratch_shapes=[pltpu.VMEM((tk, tn), F32)],
        compiler_params=_cparams(("parallel", "parallel", "arbitrary")),
    )(a, dy, *deps)


ROW_TILE = 512


def _rms_fwd(x, g, name="rms_fwd"):
    t, d = x.shape
    r = min(t, ROW_TILE)

    def body(x_ref, g_ref, o_ref):
        xv = x_ref[...]
        rstd = lax.rsqrt(jnp.mean(xv * xv, axis=-1, keepdims=True) + EPS)
        o_ref[...] = (xv * rstd * g_ref[...]).astype(BF16)

    return pl.pallas_call(
        body,
        name=name,
        grid=(t // r,),
        in_specs=[pl.BlockSpec((r, d), lambda i: (i, 0)), pl.BlockSpec((1, d), lambda i: (0, 0))],
        out_specs=pl.BlockSpec((r, d), lambda i: (i, 0)),
        out_shape=jax.ShapeDtypeStruct((t, d), BF16),
        compiler_params=_cparams(("parallel",)),
    )(x, g)


def _rms_bwd(x, g, dxn, dres, name="rms_bwd"):
    t, d = x.shape
    r = min(t, ROW_TILE)

    def body(x_ref, g_ref, dxn_ref, dres_ref, dx_ref, dg_ref):
        @pl.when(pl.program_id(0) == 0)
        def _():
            dg_ref[...] = jnp.zeros_like(dg_ref)

        xv = x_ref[...]
        rstd = lax.rsqrt(jnp.mean(xv * xv, axis=-1, keepdims=True) + EPS)
        xhat = xv * rstd
        dxn_v = dxn_ref[...].astype(F32)
        gd = dxn_v * g_ref[...]
        dx_ref[...] = dres_ref[...] + rstd * (gd - xhat * jnp.mean(gd * xhat, axis=-1, keepdims=True))
        dg_ref[...] += jnp.sum(dxn_v * xhat, axis=0, keepdims=True)

    return pl.pallas_call(
        body,
        name=name,
        grid=(t // r,),
        in_specs=[
            pl.BlockSpec((r, d), lambda i: (i, 0)),
            pl.BlockSpec((1, d), lambda i: (0, 0)),
            pl.BlockSpec((r, d), lambda i: (i, 0)),
            pl.BlockSpec((r, d), lambda i: (i, 0)),
        ],
        out_specs=[pl.BlockSpec((r, d), lambda i: (i, 0)), pl.BlockSpec((1, d), lambda i: (0, 0))],
        out_shape=[jax.ShapeDtypeStruct((t, d), F32), jax.ShapeDtypeStruct((1, d), F32)],
        compiler_params=_cparams(("arbitrary",)),
    )(x, g, dxn, dres)


def _loss_head(h, g, target):
    t, d = h.shape
    r = min(t, ROW_TILE)

    def body(h_ref, g_ref, t_ref, dh_ref, dg_ref, loss_ref):
        @pl.when(pl.program_id(0) == 0)
        def _():
            dg_ref[...] = jnp.zeros_like(dg_ref)
            loss_ref[...] = jnp.zeros_like(loss_ref)

        xv = h_ref[...]
        rstd = lax.rsqrt(jnp.mean(xv * xv, axis=-1, keepdims=True) + EPS)
        xhat = xv * rstd
        gv = g_ref[...]
        err = xhat * gv - t_ref[...]
        loss_ref[...] += 0.5 * jnp.sum(jnp.mean(err * err, axis=-1, keepdims=True), axis=0, keepdims=True)
        dy = err * (1.0 / d)
        gd = dy * gv
        dh_ref[...] = rstd * (gd - xhat * jnp.mean(gd * xhat, axis=-1, keepdims=True))
        dg_ref[...] += jnp.sum(dy * xhat, axis=0, keepdims=True)

    return pl.pallas_call(
        body,
        name="loss_head",
        grid=(t // r,),
        in_specs=[
            pl.BlockSpec((r, d), lambda i: (i, 0)),
            pl.BlockSpec((1, d), lambda i: (0, 0)),
            pl.BlockSpec((r, d), lambda i: (i, 0)),
        ],
        out_specs=[
            pl.BlockSpec((r, d), lambda i: (i, 0)),
            pl.BlockSpec((1, d), lambda i: (0, 0)),
            pl.BlockSpec((1, LANES), lambda i: (0, 0)),
        ],
        out_shape=[
            jax.ShapeDtypeStruct((t, d), F32),
            jax.ShapeDtypeStruct((1, d), F32),
            jax.ShapeDtypeStruct((1, LANES), F32),
        ],
        compiler_params=_cparams(("arbitrary",)),
    )(h, g, target)


CONV_ROWS = 256
CONV_COLS = 1408


def _conv_taps(x_ext, n):
    tot = x_ext.shape[0]
    g1 = pltpu.roll(x_ext, 1, 0)[tot - n:]
    g2 = pltpu.roll(x_ext, 2, 0)[tot - n:]
    return g2, g1


def _conv_fwd(up, conv_w, conv_b, name="conv_fwd"):
    t = up.shape[0]
    r = min(t, CONV_ROWS)
    tc = CONV_COLS
    ncb = D_FF // tc
    hb = r // SUBLANES

    def body(g_ref, halo_ref, v_ref, w_ref, b_ref, o_ref):
        i = pl.program_id(1)
        g0 = g_ref[...]
        halo = halo_ref[...] * jnp.where(i > 0, 1.0, 0.0)
        g2, g1 = _conv_taps(jnp.concatenate([halo, g0], axis=0), r)
        c = b_ref[...] + w_ref[0:1, :] * g2 + w_ref[1:2, :] * g1 + w_ref[2:3, :] * g0
        o_ref[...] = (c * _sigmoid(c) * v_ref[...]).astype(BF16)

    return pl.pallas_call(
        body,
        name=name,
        grid=(ncb, t // r),
        in_specs=[
            pl.BlockSpec((r, tc), lambda j, i: (i, j)),
            pl.BlockSpec((SUBLANES, tc), lambda j, i: (jnp.maximum(i * hb - 1, 0), j)),
            pl.BlockSpec((r, tc), lambda j, i: (i, ncb + j)),
            pl.BlockSpec((3, tc), lambda j, i: (0, j)),
            pl.BlockSpec((1, tc), lambda j, i: (0, j)),
        ],
        out_specs=pl.BlockSpec((r, tc), lambda j, i: (i, j)),
        out_shape=jax.ShapeDtypeStruct((t, D_FF), BF16),
        compiler_params=_cparams(("parallel", "parallel")),
    )(up, up, up, conv_w, conv_b)


def _conv_bwd(up, conv_w, conv_b, dact, name="conv_bwd"):
    t = up.shape[0]
    r = min(t, CONV_ROWS)
    tc = CONV_COLS
    ncb = D_FF // tc
    hb = r // SUBLANES
    nrt = t // r

    def body(g_ref, halo_ref, v_ref, w_ref, b_ref, da_ref, dup_ref, dw_ref, db_ref, nxt_ref):
        ii = pl.program_id(1)
        i = nrt - 1 - ii

        @pl.when(ii == 0)
        def _():
            nxt_ref[...] = jnp.zeros_like(nxt_ref)
            dw_ref[...] = jnp.zeros_like(dw_ref)
            db_ref[...] = jnp.zeros_like(db_ref)

        g0 = g_ref[...]
        halo = halo_ref[...] * jnp.where(i > 0, 1.0, 0.0)
        g2, g1 = _conv_taps(jnp.concatenate([halo, g0], axis=0), r)
        w0, w1, w2 = w_ref[0:1, :], w_ref[1:2, :], w_ref[2:3, :]
        c = b_ref[...] + w0 * g2 + w1 * g1 + w2 * g0
        sg = _sigmoid(c)
        da = da_ref[...]
        dval = da * (c * sg)
        dc = da * v_ref[...] * (sg * (1.0 + c * (1.0 - sg)))
        db_ref[...] += jnp.sum(dc, axis=0, keepdims=True)
        dw_ref[0:1, :] += jnp.sum(dc * g2, axis=0, keepdims=True)
        dw_ref[1:2, :] += jnp.sum(dc * g1, axis=0, keepdims=True)
        dw_ref[2:3, :] += jnp.sum(dc * g0, axis=0, keepdims=True)
        ext = jnp.concatenate([dc, nxt_ref[...]], axis=0)
        tot = r + SUBLANES
        d1 = pltpu.roll(ext, tot - 1, 0)[:r]
        d2 = pltpu.roll(ext, tot - 2, 0)[:r]
        dgate = w2 * dc + w1 * d1 + w0 * d2
        nxt_ref[...] = dc[:SUBLANES]
        dup_ref[0] = dgate.astype(BF16)
        dup_ref[1] = dval.astype(BF16)

    rev = lambda ii: nrt - 1 - ii
    dup, dw, db = pl.pallas_call(
        body,
        name=name,
        grid=(ncb, nrt),
        in_specs=[
            pl.BlockSpec((r, tc), lambda j, ii: (rev(ii), j)),
            pl.BlockSpec((SUBLANES, tc), lambda j, ii: (jnp.maximum(rev(ii) * hb - 1, 0), j)),
            pl.BlockSpec((r, tc), lambda j, ii: (rev(ii), ncb + j)),
            pl.BlockSpec((3, tc), lambda j, ii: (0, j)),
            pl.BlockSpec((1, tc), lambda j, ii: (0, j)),
            pl.BlockSpec((r, tc), lambda j, ii: (rev(ii), j)),
        ],
        out_specs=[
            pl.BlockSpec((2, None, r, tc), lambda j, ii: (0, j, rev(ii), 0)),
            pl.BlockSpec((3, tc), lambda j, ii: (0, j)),
            pl.BlockSpec((1, tc), lambda j, ii: (0, j)),
        ],
        out_shape=[
            jax.ShapeDtypeStruct((2, ncb, t, tc), BF16),
            jax.ShapeDtypeStruct((3, D_FF), F32),
            jax.ShapeDtypeStruct((1, D_FF), F32),
        ],
        scratch_shapes=[pltpu.VMEM((SUBLANES, tc), F32)],
        compiler_params=_cparams(("parallel", "arbitrary")),
    )(up, up, up, conv_w, conv_b, dact)
    return dup.reshape(2 * ncb, t, tc), dw, db


def _split3(x):
    x1 = x.astype(BF16)
    r1 = x - x1.astype(F32)
    x2 = r1.astype(BF16)
    x3 = (r1 - x2.astype(F32)).astype(BF16)
    return x1, x2, x3


def _tri_dot(tri, x, dims):
    x1, x2, x3 = _split3(x)
    return _dot(tri, x1, dims) + _dot(tri, x2, dims) + _dot(tri, x3, dims)


def _lower_bound(logits_ref):
    return _sigmoid(logits_ref[0:1, :] - logits_ref[1:2, :])


def _hg_gates(qr, fr, lb):
    q = qr * _sigmoid(qr) * (HG_DK ** -0.5)
    sf = _sigmoid(fr)
    fg = lb + (1.0 - lb) * sf
    return q, sf, fg


def _hg_chunk_terms(q, fg, tril_b, low_half):
    g = jnp.log(fg)
    k = 1.0 - fg
    cum = _tri_dot(tril_b, g, NN)
    c_last = jnp.sum(g, axis=0, keepdims=True)
    c_mid = jnp.sum(jnp.where(low_half, g, 0.0), axis=0, keepdims=True)
    e_q = jnp.exp(cum - c_mid)
    e_k = jnp.exp(c_mid - cum)
    e_0 = jnp.exp(cum)
    e_l = jnp.exp(c_last - cum)
    return k, e_q, e_k, e_0, e_l, jnp.exp(c_last)


HG_BLOCK = 256


def _hg_proj_specs(rb, row):
    return [pl.BlockSpec((rb, D_MODEL), functools.partial(lambda i, k: (row(i), k), k=k)) for k in range(4)]


def _hg_consts(c):
    tril = lax.broadcasted_iota(jnp.int32, (c, c), 0) >= lax.broadcasted_iota(jnp.int32, (c, c), 1)
    low_half = lax.broadcasted_iota(jnp.int32, (c, HG_DK), 0) < c // 2
    return tril, tril.astype(BF16), low_half


def _hgrn_fwd(proj, lb, wn):
    t = proj.shape[0]
    c = HG_CHUNK
    rb = min(t, HG_BLOCK)
    cpb = rb // c

    def body(q_ref, f_ref, i_ref, g_ref, lb_ref, wn_ref, o_ref, y_ref, st_ref, s_scr):
        @pl.when(pl.program_id(0) == 0)
        def _():
            s_scr[...] = jnp.zeros_like(s_scr)

        lb_all = _lower_bound(lb_ref)
        wnv = wn_ref[...]
        tril, tril_b, low_half = _hg_consts(c)

        def chunk(n, carry):
            rows = pl.ds(pl.multiple_of(n * c, c), c)
            for h in range(HG_HEADS):
                cols = slice(h * HG_DK, (h + 1) * HG_DK)
                q, _, fg = _hg_gates(q_ref[rows, cols], f_ref[rows, cols], lb_all[:, cols])
                v = i_ref[rows, cols].astype(BF16)
                k, e_q, e_k, e_0, e_l, e_last = _hg_chunk_terms(q, fg, tril_b, low_half)
                st = s_scr[h]
                st_ref[h, n] = st
                a = jnp.where(tril, _dot((q * e_q).astype(BF16), (k * e_k).astype(BF16), NT), 0.0)
                o = _dot((q * e_0).astype(BF16), st.astype(BF16), NT) + _dot(a.astype(BF16), v, NN)
                s_scr[h] = st * e_last + _dot(v, (k * e_l).astype(BF16), TN)
                o_ref[rows, cols] = o
                rstd = lax.rsqrt(jnp.mean(o * o, axis=-1, keepdims=True) + EPS)
                gr = g_ref[rows, cols]
                y_ref[rows, cols] = (o * rstd * wnv * (gr * _sigmoid(gr))).astype(BF16)
            return carry

        lax.fori_loop(0, cpb, chunk, 0)

    blk = pl.BlockSpec((rb, D_MODEL), lambda i: (i, 0))
    return pl.pallas_call(
        body,
        name="hgrn_fwd",
        grid=(t // rb,),
        in_specs=_hg_proj_specs(rb, lambda i: i) + [pl.BlockSpec((2, D_MODEL), lambda i: (0, 0)), pl.BlockSpec((1, HG_DK), lambda i: (0, 0))],
        out_specs=[blk, blk, pl.BlockSpec((HG_HEADS, cpb, HG_DK, HG_DK), lambda i: (0, i, 0, 0))],
        out_shape=[
            jax.ShapeDtypeStruct((t, D_MODEL), F32),
            jax.ShapeDtypeStruct((t, D_MODEL), BF16),
            jax.ShapeDtypeStruct((HG_HEADS, t // c, HG_DK, HG_DK), F32),
        ],
        scratch_shapes=[pltpu.VMEM((HG_HEADS, HG_DK, HG_DK), F32)],
        compiler_params=_cparams(("arbitrary",)),
    )(proj, proj, proj, proj, lb, wn)


def _hgrn_bwd(proj, lb, wn, o, states, dy):
    t = proj.shape[0]
    c = HG_CHUNK
    rb = min(t, HG_BLOCK)
    cpb = rb // c
    nb = t // rb

    def body(q_ref, f_ref, i_ref, g_ref, lb_ref, wn_ref, o_ref, st_ref, dy_ref, dp_ref, dl_ref, dwn_ref, ds_scr, dlb_scr):
        step = pl.program_id(0)

        @pl.when(step == 0)
        def _():
            dwn_ref[...] = jnp.zeros_like(dwn_ref)
            ds_scr[...] = jnp.zeros_like(ds_scr)
            dlb_scr[...] = jnp.zeros_like(dlb_scr)

        lb_all = _lower_bound(lb_ref)
        wnv = wn_ref[...]
        tril, tril_b, low_half = _hg_consts(c)

        def chunk(nn, carry):
            n = cpb - 1 - nn
            rows = pl.ds(pl.multiple_of(n * c, c), c)
            for h in range(HG_HEADS):
                cols = slice(h * HG_DK, (h + 1) * HG_DK)
                lbv = lb_all[:, cols]
                ov = o_ref[rows, cols]
                gr = g_ref[rows, cols]
                dyv = dy_ref[rows, cols].astype(F32)
                rstd = lax.rsqrt(jnp.mean(ov * ov, axis=-1, keepdims=True) + EPS)
                ohat = ov * rstd
                sg = _sigmoid(gr)
                dg_raw = dyv * (ohat * wnv) * (sg * (1.0 + gr * (1.0 - sg)))
                don = dyv * (gr * sg)
                dwn_ref[...] += jnp.sum(don * ohat, axis=0, keepdims=True)
                gd = don * wnv
                do = rstd * (gd - ohat * jnp.mean(gd * ohat, axis=-1, keepdims=True))
                do_b = do.astype(BF16)
                qr = q_ref[rows, cols]
                q, sf, fg = _hg_gates(qr, f_ref[rows, cols], lbv)
                v = i_ref[rows, cols].astype(BF16)
                k, e_q, e_k, e_0, e_l, e_last = _hg_chunk_terms(q, fg, tril_b, low_half)
                qi, qi_lo, _ = _split3(q * e_q)
                ki, ki_lo, _ = _split3(k * e_k)
                q0 = (q * e_0).astype(BF16)
                kl = (k * e_l).astype(BF16)
                st = st_ref[h, n]
                st_b = st.astype(BF16)
                ds = ds_scr[h]
                ds_b = ds.astype(BF16)
                a_b = jnp.where(tril, _dot(qi, ki, NT), 0.0).astype(BF16)
                da_b = jnp.where(tril, _dot(do_b, v, NT), 0.0).astype(BF16)
                dq = _dot(do_b, st_b, NN) * e_0 + (_dot(da_b, ki, NN) + _dot(da_b, ki_lo, NN)) * e_q
                dk_state = _dot(v, ds_b, NN) * e_l
                dk = (_dot(da_b, qi, TN) + _dot(da_b, qi_lo, TN)) * e_k + dk_state
                dv = _dot(a_b, do_b, TN) + _dot(kl, ds_b, NT)
                ds_scr[h] = ds * e_last + _dot(do_b, q0, TN)
                d_last = jnp.sum(dk_state * k, axis=0, keepdims=True) + jnp.sum(ds * st, axis=0, keepdims=True) * e_last
                dlogf = _tri_dot(tril_b, q * dq - k * dk, TN) + d_last
                dfg = dlogf / fg - dk
                dlb_scr[:, cols] += jnp.sum(dfg * (1.0 - sf), axis=0, keepdims=True)
                sq = _sigmoid(qr)
                dp_ref[0, rows, cols] = (dq * (HG_DK ** -0.5) * (sq * (1.0 + qr * (1.0 - sq)))).astype(BF16)
                dp_ref[1, rows, cols] = (dfg * (1.0 - lbv) * sf * (1.0 - sf)).astype(BF16)
                dp_ref[2, rows, cols] = dv.astype(BF16)
                dp_ref[3, rows, cols] = dg_raw.astype(BF16)
            return carry

        lax.fori_loop(0, cpb, chunk, 0)

        @pl.when(step == nb - 1)
        def _():
            d0 = dlb_scr[...] * lb_all * (1.0 - lb_all)
            dl_ref[0:1, :] = d0
            dl_ref[1:2, :] = -d0

    rev = lambda i: nb - 1 - i
    blk = pl.BlockSpec((rb, D_MODEL), lambda i: (rev(i), 0))
    return pl.pallas_call(
        body,
        name="hgrn_bwd",
        grid=(nb,),
        in_specs=_hg_proj_specs(rb, rev)
        + [pl.BlockSpec((2, D_MODEL), lambda i: (0, 0)), pl.BlockSpec((1, HG_DK), lambda i: (0, 0)), blk,
           pl.BlockSpec((HG_HEADS, cpb, HG_DK, HG_DK), lambda i: (0, rev(i), 0, 0)), blk],
        out_specs=[
            pl.BlockSpec((4, rb, D_MODEL), lambda i: (0, rev(i), 0)),
            pl.BlockSpec((2, D_MODEL), lambda i: (0, 0)),
            pl.BlockSpec((1, HG_DK), lambda i: (0, 0)),
        ],
        out_shape=[
            jax.ShapeDtypeStruct((4, t, D_MODEL), BF16),
            jax.ShapeDtypeStruct((2, D_MODEL), F32),
            jax.ShapeDtypeStruct((1, HG_DK), F32),
        ],
        scratch_shapes=[pltpu.VMEM((HG_HEADS, HG_DK, HG_DK), F32), pltpu.VMEM((1, D_MODEL), F32)],
        compiler_params=_cparams(("arbitrary",)),
    )(proj, proj, proj, proj, lb, wn, o, states, dy)


def _att_masks(n):
    tq = lax.broadcasted_iota(jnp.int32, (WINDOW, WINDOW), 0)
    sk = lax.broadcasted_iota(jnp.int32, (WINDOW, WINDOW), 1)
    valid_c = sk <= tq
    valid_p = (sk - tq) > jnp.where(n > 0, 0, WINDOW)
    dist_c = (tq - sk).astype(F32)
    dist_p = dist_c + float(WINDOW)
    return valid_p, valid_c, dist_p, dist_c


def _att_halves(x, lo, kh):
    r = pltpu.roll(x, ATT_HD, 1)
    zero = jnp.zeros_like(x)
    if kh == 0:
        return jnp.where(lo, x, r), jnp.where(lo, x, zero), jnp.where(lo, zero, r)
    return jnp.where(lo, r, x), jnp.where(lo, r, zero), jnp.where(lo, zero, x)


def _att_probs(qm, k2p, k2c, masks, slope, sink):
    valid_p, valid_c, dist_p, dist_c = masks
    sp = jnp.where(valid_p, _dot(qm, k2p, NT) * (ATT_HD ** -0.5) - slope * dist_p, NEG)
    sc = jnp.where(valid_c, _dot(qm, k2c, NT) * (ATT_HD ** -0.5) - slope * dist_c, NEG)
    m = jnp.maximum(jnp.maximum(jnp.max(sp, axis=-1, keepdims=True), jnp.max(sc, axis=-1, keepdims=True)), sink)
    ep = jnp.exp(sp - m)
    ec = jnp.exp(sc - m)
    es = jnp.exp(sink - m)
    inv = 1.0 / (jnp.sum(ep, axis=-1, keepdims=True) + jnp.sum(ec, axis=-1, keepdims=True) + es)
    return ep * inv, ec * inv, es * inv


def _attn_fwd(q, kv, sinks):
    t = q.shape[0]
    nb = t // WINDOW

    def body(sink_ref, q_ref, kvp_ref, kvc_ref, o_ref):
        n = pl.program_id(0)
        masks = _att_masks(n)
        lo = lax.broadcasted_iota(jnp.int32, (WINDOW, LANES), 1) < ATT_HD
        for kh in range(ATT_KVH):
            k2p, _, _ = _att_halves(kvp_ref[:, 0:LANES], lo, kh)
            k2c, _, _ = _att_halves(kvc_ref[:, 0:LANES], lo, kh)
            _, vlo_p, vhi_p = _att_halves(kvp_ref[:, LANES:2 * LANES], lo, kh)
            _, vlo_c, vhi_c = _att_halves(kvc_ref[:, LANES:2 * LANES], lo, kh)
            for jj in range(ATT_GROUP // 2):
                j = kh * (ATT_GROUP // 2) + jj
                qp = q_ref[:, j * LANES:(j + 1) * LANES]
                zero = jnp.zeros_like(qp)
                out = None
                for par in range(2):
                    hq = 2 * j + par
                    qm = jnp.where(lo, qp, zero) if par == 0 else jnp.where(lo, zero, qp)
                    pp, pc, _ = _att_probs(qm, k2p, k2c, masks, ALIBI_SLOPES[hq], sink_ref[hq])
                    vp, vc = (vlo_p, vlo_c) if par == 0 else (vhi_p, vhi_c)
                    part = _dot(pp.astype(BF16), vp, NN) + _dot(pc.astype(BF16), vc, NN)
                    out = part if out is None else out + part
                o_ref[:, j * LANES:(j + 1) * LANES] = out.astype(BF16)

    return pl.pallas_call(
        body,
        name="attn_fwd",
        grid=(nb,),
        in_specs=[
            pl.BlockSpec(memory_space=pltpu.SMEM),
            pl.BlockSpec((WINDOW, D_MODEL), lambda n: (n, 0)),
            pl.BlockSpec((WINDOW, 2 * LANES), lambda n: (jnp.maximum(n - 1, 0), 0)),
            pl.BlockSpec((WINDOW, 2 * LANES), lambda n: (n, 0)),
        ],
        out_specs=pl.BlockSpec((WINDOW, D_MODEL), lambda n: (n, 0)),
        out_shape=jax.ShapeDtypeStruct((t, D_MODEL), BF16),
        compiler_params=_cparams(("parallel",)),
    )(sinks, q, kv, kv)


def _attn_bwd(q, kv, sinks, dout):
    t = q.shape[0]
    nb = t // WINDOW

    def body(sink_ref, q_ref, kvp_ref, kvc_ref, do_ref, dq_ref, dkv_ref, dsink_ref, carry_ref):
        n = pl.program_id(0)

        @pl.when(n == 0)
        def _():
            carry_ref[...] = jnp.zeros_like(carry_ref)
            dsink_ref[...] = jnp.zeros_like(dsink_ref)

        @pl.when(n == nb)
        def _():
            dkv_ref[...] = carry_ref[...].astype(BF16)

        @pl.when(n < nb)
        def _():
            masks = _att_masks(n)
            lo = lax.broadcasted_iota(jnp.int32, (WINDOW, LANES), 1) < ATT_HD
            lane1 = lax.broadcasted_iota(jnp.int32, (1, LANES), 1)
            dsink = jnp.zeros((1, LANES), F32)
            halves = []
            for kh in range(ATT_KVH):
                k2p, klo_p, khi_p = _att_halves(kvp_ref[:, 0:LANES], lo, kh)
                k2c, klo_c, khi_c = _att_halves(kvc_ref[:, 0:LANES], lo, kh)
                v2p, _, _ = _att_halves(kvp_ref[:, LANES:2 * LANES], lo, kh)
                v2c, _, _ = _att_halves(kvc_ref[:, LANES:2 * LANES], lo, kh)
                acc = [jnp.zeros((WINDOW, LANES), F32) for _ in range(4)]
                for jj in range(ATT_GROUP // 2):
                    j = kh * (ATT_GROUP // 2) + jj
                    cols = slice(j * LANES, (j + 1) * LANES)
                    qp = q_ref[:, cols]
                    dop = do_ref[:, cols]
                    zero = jnp.zeros_like(qp)
                    dq_pair = None
                    for par in range(2):
                        hq = 2 * j + par
                        sel = lo if par == 0 else jnp.logical_not(lo)
                        qm = jnp.where(sel, qp, zero)
                        dom = jnp.where(sel, dop, zero)
                        pp, pc, ps = _att_probs(qm, k2p, k2c, masks, ALIBI_SLOPES[hq], sink_ref[hq])
                        dpp = _dot(dom, v2p, NT)
                        dpc = _dot(dom, v2c, NT)
                        delta = jnp.sum(pp * dpp, axis=-1, keepdims=True) + jnp.sum(pc * dpc, axis=-1, keepdims=True)
                        dsp = (pp * (dpp - delta)).astype(BF16)
                        dsc = (pc * (dpc - delta)).astype(BF16)
                        dsink = dsink + jnp.where(lane1 == hq, -jnp.sum(ps * delta, axis=0, keepdims=True), 0.0)
                        kp_, kc_ = (klo_p, klo_c) if par == 0 else (khi_p, khi_c)
                        part = _dot(dsp, kp_, NN) + _dot(dsc, kc_, NN)
                        dq_pair = part if dq_pair is None else dq_pair + part
                        acc[0] = acc[0] + _dot(dsp, qm, TN)
                        acc[1] = acc[1] + _dot(dsc, qm, TN)
                        acc[2] = acc[2] + _dot(pp.astype(BF16), dom, TN)
                        acc[3] = acc[3] + _dot(pc.astype(BF16), dom, TN)
                    dq_ref[:, cols] = (dq_pair * (ATT_HD ** -0.5)).astype(BF16)
                halves.append([a + pltpu.roll(a, ATT_HD, 1) for a in acc])
            scale = ATT_HD ** -0.5
            prev = jnp.concatenate(
                [jnp.where(lo, halves[0][0], halves[1][0]) * scale, jnp.where(lo, halves[0][2], halves[1][2])], axis=1)
            cur = jnp.concatenate(
                [jnp.where(lo, halves[0][1], halves[1][1]) * scale, jnp.where(lo, halves[0][3], halves[1][3])], axis=1)
            dkv_ref[...] = (carry_ref[...] + prev).astype(BF16)
            carry_ref[...] = cur
            dsink_ref[...] += dsink

    blk = lambda n: jnp.minimum(n, nb - 1)
    return pl.pallas_call(
        body,
        name="attn_bwd",
        grid=(nb + 1,),
        in_specs=[
            pl.BlockSpec(memory_space=pltpu.SMEM),
            pl.BlockSpec((WINDOW, D_MODEL), lambda n: (blk(n), 0)),
            pl.BlockSpec((WINDOW, 2 * LANES), lambda n: (jnp.maximum(blk(n) - 1, 0), 0)),
            pl.BlockSpec((WINDOW, 2 * LANES), lambda n: (blk(n), 0)),
            pl.BlockSpec((WINDOW, D_MODEL), lambda n: (blk(n), 0)),
        ],
        out_specs=[
            pl.BlockSpec((WINDOW, D_MODEL), lambda n: (blk(n), 0)),
            pl.BlockSpec((WINDOW, 2 * LANES), lambda n: (jnp.maximum(n - 1, 0), 0)),
            pl.BlockSpec((1, LANES), lambda n: (0, 0)),
        ],
        out_shape=[
            jax.ShapeDtypeStruct((t, D_MODEL), BF16),
            jax.ShapeDtypeStruct((t, 2 * LANES), BF16),
            jax.ShapeDtypeStruct((1, LANES), F32),
        ],
        scratch_shapes=[pltpu.VMEM((WINDOW, 2 * LANES), F32)],
        compiler_params=_cparams(("arbitrary",)),
    )(sinks, q, kv, kv, dout)


def _ffn_fwd(h, norm_g, w_up, conv_w, conv_b, w_down, tag):
    xn = _rms_fwd(h, norm_g, name=f"ffn{tag}_norm")
    up = _mm_nn(xn, w_up, name=f"ffn{tag}_up")
    act = _conv_fwd(up, conv_w, conv_b, name=f"ffn{tag}_conv")
    h_out = _mm_nn(act, w_down, res=h, name=f"ffn{tag}_down")
    return h_out, (xn, up, act)


def _ffn_bwd(dh, h, norm_g, w_up, conv_w, conv_b, w_down, saved, tag, deps=()):
    xn, up, act = saved
    dw_down = _mm_tn(act, dh, 1, D_MODEL, name=f"ffn{tag}_dwdown", deps=deps)
    dact = _mm_nt(dh, w_down, name=f"ffn{tag}_dact", deps=deps)
    dup, dconv_w, dconv_b = _conv_bwd(up, conv_w, conv_b, dact, name=f"ffn{tag}_dconv")
    dw_up = _mm_tn(xn, dup, N_CHIPS, CONV_COLS, stacked=True, name=f"ffn{tag}_dwup")
    dxn = _mm_nt(dup, w_up, stacked=True, name=f"ffn{tag}_dxn")
    dh_in, dnorm = _rms_bwd(h, norm_g, dxn, dh, name=f"ffn{tag}_dnorm")
    return dh_in, dict(ffn_w_down=dw_down, ffn_w_up=dw_up, ffn_conv_w=dconv_w, ffn_conv_b=dconv_b, ffn_norm=dnorm)


def _local_step(x, target, w, fetch=lambda w, stage, after: w, hook=lambda point, dh, grads: ()):
    xn0 = _rms_fwd(x, w["hg_norm"], name="hg_norm")
    proj = _mm_nn(xn0, w["hg_w_in"], name="hg_in")
    o, y, states = _hgrn_fwd(proj, w["hg_lb"], w["hg_out_norm"])
    w = fetch(w, "layer0", y)
    h_a = _mm_nn(y, w["hg_w_out"], res=x, name="hg_out")
    h1, ffn0 = _ffn_fwd(h_a, w["ffn_norm"][0], w["ffn_w_up"][0], w["ffn_conv_w"][0], w["ffn_conv_b"][0], w["ffn_w_down"][0], 0)
    w = fetch(w, "layer1", h1)
    kvn = _rms_fwd(h1, w["kv_norm"], name="kv_norm")
    kv = _mm_nn(kvn, w["w_kv"], out_dtype=BF16, name="kv_proj")
    xa = _rms_fwd(h1, w["attn_norm"], name="attn_norm")
    qa = _mm_nn(xa, w["attn_w_q"], out_dtype=BF16, name="attn_q")
    ao = _attn_fwd(qa, kv, w["attn_sinks"])
    h_b = _mm_nn(ao, w["attn_w_o"], res=h1, name="attn_o")
    h2, ffn1 = _ffn_fwd(h_b, w["ffn_norm"][1], w["ffn_w_up"][1], w["ffn_conv_w"][1], w["ffn_conv_b"][1], w["ffn_w_down"][1], 1)
    dh2, d_final, loss = _loss_head(h2, w["final_norm"], target)

    dh_b, g1 = _ffn_bwd(dh2, h_b, w["ffn_norm"][1], w["ffn_w_up"][1], w["ffn_conv_w"][1], w["ffn_conv_b"][1], w["ffn_w_down"][1], ffn1, 1)
    deps = hook("ffn1", dh_b, g1)
    dw_o = _mm_tn(ao, dh_b, 1, D_MODEL, name="attn_dwo", deps=deps)
    dao = _mm_nt(dh_b, w["attn_w_o"], out_dtype=BF16, name="attn_dao", deps=deps)
    dqa, dkv, dsinks = _attn_bwd(qa, kv, w["attn_sinks"], dao)
    dw_q = _mm_tn(xa, dqa, 1, D_MODEL, name="attn_dwq")
    dxa = _mm_nt(dqa, w["attn_w_q"], name="attn_dxa")
    dh1, d_attn_norm = _rms_bwd(h1, w["attn_norm"], dxa, dh_b, name="attn_dnorm")
    dw_kv = _mm_tn(kvn, dkv, 1, 2 * LANES, name="kv_dw")
    dkvn = _mm_nt(dkv, w["w_kv"], name="kv_dx")
    dh1, d_kv_norm = _rms_bwd(h1, w["kv_norm"], dkvn, dh1, name="kv_dnorm")
    deps = hook("attn", dh1, dict(attn_w_o=dw_o, attn_w_q=dw_q, w_kv=dw_kv))
    dh_a, g0 = _ffn_bwd(dh1, h_a, w["ffn_norm"][0], w["ffn_w_up"][0], w["ffn_conv_w"][0], w["ffn_conv_b"][0], w["ffn_w_down"][0], ffn0, 0, deps)
    deps = hook("ffn0", dh_a, g0)
    dw_out = _mm_tn(y, dh_a, 1, D_MODEL, name="hg_dwout", deps=deps)
    dy = _mm_nt(dh_a, w["hg_w_out"], out_dtype=BF16, name="hg_dy", deps=deps)
    dproj, dlb, d_out_norm = _hgrn_bwd(proj, w["hg_lb"], w["hg_out_norm"], o, states, dy)
    deps = hook("hgrn", dproj, None)
    dw_in = _mm_tn(xn0, dproj, N_CHIPS, D_MODEL, stacked=True, name="hg_dwin", deps=deps)
    deps = hook("hg_w", dproj, dict(hg_w_out=dw_out, hg_w_in=dw_in))
    dxn0 = _mm_nt(dproj, w["hg_w_in"], stacked=True, name="hg_dxn", deps=deps)
    dx, d_hg_norm = _rms_bwd(x, w["hg_norm"], dxn0, dh_a, name="hg_dnorm")

    grads = dict(
        hg_norm=d_hg_norm, hg_w_in=dw_in, hg_lb=dlb, hg_out_norm=d_out_norm, hg_w_out=dw_out,
        kv_norm=d_kv_norm, w_kv=dw_kv, attn_norm=d_attn_norm, attn_w_q=dw_q, attn_sinks=dsinks, attn_w_o=dw_o,
        final_norm=d_final,
    )
    for name in g0:
        grads[name] = [g0[name], g1[name]]
    return loss, dx, grads


ANY = pl.BlockSpec(memory_space=pl.ANY)


def _place():
    x, y, c = lax.axis_index("x"), lax.axis_index("y"), lax.axis_index("c")
    chips = [(1 - x, y), (x, 1 - y), (1 - x, 1 - y)]
    return x, y, c, chips


def _rcopy(src, dst, send_sem, recv_sem, to):
    return pltpu.make_async_remote_copy(src_ref=src, dst_ref=dst, send_sem=send_sem, recv_sem=recv_sem, device_id=to, device_id_type=MESH)


HBM = pl.BlockSpec(memory_space=pltpu.HBM)
SEM = pl.BlockSpec(memory_space=pltpu.SEMAPHORE)
EFFECT = pltpu.SideEffectType.DATAFLOW_SIDE_EFFECTING


def _in_hbm(a):
    return pltpu.with_memory_space_constraint(a, pltpu.HBM)


def _place_shard(shard, place, dtype, name, deps=()):
    r, cols = shard.shape
    tr = _pick(r, ELEM_ROWS)

    def body(place_ref, s_ref, *rest):
        o_ref = rest[-1]
        o_ref[...] = s_ref[...].astype(o_ref.dtype)

    return pl.pallas_call(
        body,
        name=name,
        grid_spec=pltpu.PrefetchScalarGridSpec(
            num_scalar_prefetch=1,
            grid=(r // tr,),
            in_specs=[pl.BlockSpec((tr, cols), lambda i, place_ref: (i, 0))] + _dep_specs(deps),
            out_specs=pl.BlockSpec((None, tr, cols), lambda i, place_ref: (place_ref[0], i, 0)),
        ),
        out_shape=jax.ShapeDtypeStruct((N_CHIPS, r, cols), dtype),
        compiler_params=_cparams(("parallel",)),
    )(place, shard, *deps)


def _start_copies(name, bufs, n_sem, copies):
    n = len(bufs)

    def body(*refs):
        for cp in copies(refs[:n], refs[n], refs[n + 1]):
            cp.start()
        refs[-1][...] = jnp.zeros_like(refs[-1])

    outs = pl.pallas_call(
        body,
        name=name,
        in_specs=[HBM] * n,
        out_specs=[SEM, SEM] + [HBM] * n + [pl.BlockSpec(memory_space=pltpu.VMEM)],
        out_shape=[pltpu.SemaphoreType.DMA((n_sem,)), pltpu.SemaphoreType.DMA((n_sem,))] + [pltpu.HBM(b.shape, b.dtype) for b in bufs]
        + [jax.ShapeDtypeStruct((SUBLANES, LANES), F32)],
        input_output_aliases={i: 2 + i for i in range(n)},
        compiler_params=pltpu.CompilerParams(has_side_effects=EFFECT),
    )(*[_in_hbm(b) for b in bufs])
    return outs[0], outs[1], list(outs[2:-1]), outs[-1]


def _wait_copies(name, bufs, send_sems, recv_sems, after, copies):
    n = len(bufs)

    def body(*refs):
        for cp in copies(refs[:n], refs[n], refs[n + 1]):
            cp.wait_send()
            cp.wait_recv()

    return pl.pallas_call(
        body,
        name=name,
        in_specs=[HBM] * n + [SEM, SEM, ANY],
        out_specs=[HBM] * n,
        out_shape=[pltpu.HBM(b.shape, b.dtype) for b in bufs],
        input_output_aliases={i: i for i in range(n)},
        compiler_params=pltpu.CompilerParams(has_side_effects=EFFECT),
    )(*bufs, send_sems, recv_sems, after)


def _gather_copies(first, count):
    def copies(refs, send_sems, recv_sems):
        x, y, c, chips = _place()
        me = 2 * x + y
        out = []
        for i in range(count):
            for j, (px, py) in enumerate(chips):
                k = 3 * (first + i) + j
                out.append(_rcopy(refs[i].at[me], refs[i].at[me], send_sems.at[k], recv_sems.at[k], (px, py, c)))
        return out

    return copies


def _swap_copies(n):
    def copies(refs, send_sems, recv_sems):
        x, y, c, _ = _place()
        out = []
        for i in range(n):
            h = refs[i].shape[1] // 2
            out.append(_rcopy(refs[i].at[:, pl.ds((1 - c) * h, h)], refs[n + i], send_sems.at[i], recv_sems.at[i], (x, y, 1 - c)))
        return out

    return copies


def _partial_copies(n):
    def copies(refs, send_sems, recv_sems):
        x, y, c, chips = _place()
        out = []
        for i in range(n):
            for j, (px, py) in enumerate(chips):
                out.append(_rcopy(refs[i].at[2 * px + py], refs[n + i].at[j], send_sems.at[3 * i + j], recv_sems.at[3 * i + j], (px, py, c)))
        return out

    return copies


def _share_copies(n):
    def copies(refs, send_sems, recv_sems):
        x, y, c, _ = _place()
        return [_rcopy(refs[i].at[c], refs[i].at[c], send_sems.at[i], recv_sems.at[i], (x, y, 1 - c)) for i in range(n)]

    return copies


def _allreduce_small(vec):
    rows = vec.shape[0]

    def body(v_ref, o_ref, buf, send_sems, recv_sems):
        x, y, c, _ = _place()
        me = 4 * x + 2 * y + c
        buf[me] = v_ref[...]
        copies = []
        for k in range(1, N_DEV):
            peer = (x ^ (k >> 2), y ^ ((k >> 1) & 1), c ^ (k & 1))
            cp = _rcopy(v_ref, buf.at[me], send_sems.at[k - 1], recv_sems.at[k - 1], peer)
            cp.start()
            copies.append(cp)
        for cp in copies:
            cp.wait()
        acc = buf[0]
        for d in range(1, N_DEV):
            acc = acc + buf[d]
        o_ref[...] = acc

    return pl.pallas_call(
        body,
        name="allreduce_small",
        in_specs=[pl.BlockSpec(memory_space=pltpu.VMEM)],
        out_specs=pl.BlockSpec(memory_space=pltpu.VMEM),
        out_shape=jax.ShapeDtypeStruct(vec.shape, F32),
        scratch_shapes=[pltpu.VMEM((N_DEV, rows, LANES), F32), pltpu.SemaphoreType.DMA((N_DEV - 1,)), pltpu.SemaphoreType.DMA((N_DEV - 1,))],
        compiler_params=pltpu.CompilerParams(vmem_limit_bytes=VMEM_LIMIT_BYTES),
    )(vec)


class _Reduction:
    def __init__(self, tag, grads, place, core):
        self.tag, self.n, self.place, self.core = tag, len(grads), place, core
        lands = [lax.empty((N_CHIPS, g.shape[1] // 2, g.shape[2]), F32) for g in grads]
        self._start("swap", list(grads) + lands, self.n, _swap_copies(self.n))

    def _start(self, stage, bufs, n_sem, copies):
        *self.flight, self.token = _start_copies(f"rs_{stage}_start_{self.tag}", bufs, n_sem, copies)

    def _landed(self, stage, after, copies):
        send_sems, recv_sems, bufs = self.flight
        return _wait_copies(f"rs_{stage}_wait_{self.tag}", bufs, send_sems, recv_sems, after, copies)

    def to_chips(self, after):
        n = self.n
        bufs = self._landed("swap", after, _swap_copies(n))
        sums = [_add_core_halves(g, o, self.core, name=f"rs_add_core_{self.tag}_{i}") for i, (g, o) in enumerate(zip(bufs[:n], bufs[n:]))]
        self.mine = [f for f, _ in sums]
        parts = [b for _, b in sums]
        lands = [lax.empty((3,) + p.shape[1:], BF16) for p in parts]
        self._start("send", parts + lands, 3 * n, _partial_copies(n))

    def to_core(self, after):
        n = self.n
        bufs = self._landed("send", after, _partial_copies(n))
        halves = [_add_chip_partials(f, o, self.place, name=f"rs_add_chip_{self.tag}_{i}") for i, (f, o) in enumerate(zip(self.mine, bufs[n:]))]
        self._start("share", halves, n, _share_copies(n))

    def finish(self, after):
        return [b.reshape((-1,) + b.shape[2:]) for b in self._landed("share", after, _share_copies(self.n))]


ELEM_ROWS = (256, 176, 128, 64, 32, 16, 8)


def _add_core_halves(grad, got, c, name):
    s, r, cols = grad.shape
    h = r // 2
    tr = _pick(h, ELEM_ROWS)

    def body(c_ref, g_ref, o_ref, f_ref, b_ref):
        acc = g_ref[...] + o_ref[...]
        f_ref[...] = acc
        b_ref[...] = acc.astype(BF16)

    blk = pl.BlockSpec((None, tr, cols), lambda k, i, c_ref: (k, i, 0))
    return pl.pallas_call(
        body,
        name=name,
        grid_spec=pltpu.PrefetchScalarGridSpec(
            num_scalar_prefetch=1,
            grid=(s, h // tr),
            in_specs=[pl.BlockSpec((None, None, tr, cols), lambda k, i, c_ref: (k, c_ref[0], i, 0)), blk],
            out_specs=[blk, blk],
        ),
        out_shape=[jax.ShapeDtypeStruct((s, h, cols), F32), jax.ShapeDtypeStruct((s, h, cols), BF16)],
        compiler_params=_cparams(("parallel", "parallel")),
    )(c, grad.reshape(s, 2, h, cols), got)


def _add_chip_partials(mine, got, place, name):
    _, h, cols = mine.shape
    tr = _pick(h, ELEM_ROWS)

    def body(place_ref, m_ref, g_ref, o_ref):
        acc = m_ref[...]
        for j in range(3):
            acc = acc + g_ref[j].astype(F32)
        o_ref[...] = acc

    return pl.pallas_call(
        body,
        name=name,
        grid_spec=pltpu.PrefetchScalarGridSpec(
            num_scalar_prefetch=1,
            grid=(h // tr,),
            in_specs=[
                pl.BlockSpec((None, tr, cols), lambda i, place_ref: (place_ref[0], i, 0)),
                pl.BlockSpec((3, tr, cols), lambda i, place_ref: (0, i, 0)),
            ],
            out_specs=pl.BlockSpec((None, tr, cols), lambda i, place_ref: (place_ref[1], i, 0)),
        ),
        out_shape=jax.ShapeDtypeStruct((2, h, cols), F32),
        compiler_params=_cparams(("parallel",)),
    )(place, mine, got)


def _adamw_math(w, m, v, g):
    nm = ADAM_B1 * m + (1.0 - ADAM_B1) * g
    nv = ADAM_B2 * v + (1.0 - ADAM_B2) * (g * g)
    m_hat = nm * (1.0 / (1.0 - ADAM_B1 ** ADAM_STEP))
    v_hat = nv * (1.0 / (1.0 - ADAM_B2 ** ADAM_STEP))
    return -ADAM_LR * (m_hat / (jnp.sqrt(v_hat) + ADAM_EPS) + ADAM_WD * w), nm, nv


def _adamw_layer(w, m, v, g, layer, prev, name):
    nl, r, cols = w.shape
    tr = _pick(r, ELEM_ROWS)

    def body(w_ref, m_ref, v_ref, g_ref, *rest):
        go_ref, d_ref, nm_ref, nv_ref = rest[-4:]
        gv = g_ref[...]
        d_ref[...], nm_ref[...], nv_ref[...] = _adamw_math(w_ref[...], m_ref[...], v_ref[...], gv)
        go_ref[...] = gv

    lay = pl.BlockSpec((None, tr, cols), lambda i: (layer, i, 0))
    return pl.pallas_call(
        body,
        name=name,
        grid=(r // tr,),
        in_specs=[lay] * 3 + [pl.BlockSpec((tr, cols), lambda i: (i, 0))] + ([ANY] * 4 if prev else []),
        out_specs=[lay] * 4,
        out_shape=[jax.ShapeDtypeStruct((nl, r, cols), F32)] * 4,
        input_output_aliases={4 + k: k for k in range(4)} if prev else {},
        compiler_params=_cparams(("parallel",)),
    )(w, m, v, g, *(prev or ()))


def _adamw(w, m, v, g, name):
    r, cols = w.shape
    tr = _pick(r, ELEM_ROWS)

    def body(w_ref, m_ref, v_ref, g_ref, d_ref, nm_ref, nv_ref):
        d_ref[...], nm_ref[...], nv_ref[...] = _adamw_math(w_ref[...], m_ref[...], v_ref[...], g_ref[...])

    blk = pl.BlockSpec((tr, cols), lambda i: (i, 0))
    return pl.pallas_call(
        body,
        name=name,
        grid=(r // tr,),
        in_specs=[blk] * 4,
        out_specs=[blk] * 3,
        out_shape=[jax.ShapeDtypeStruct((r, cols), F32)] * 3,
        compiler_params=_cparams(("parallel",)),
    )(w, m, v, g)


SMALL_COLS = 384
SMALL_ROWS = 16


def _pad_rows(flat, rows, cols):
    return jnp.pad(flat, (0, rows * cols - flat.shape[0])).reshape(rows, cols)


def kernel(x, hg_norm, hg_w_in, hg_lb_logits, hg_out_norm, hg_w_out, kv_norm, w_kv, attn_norm, attn_w_q, attn_sinks, attn_w_o, ffn_norm, ffn_w_up, ffn_conv_w, ffn_conv_b, ffn_w_down, final_norm, loss_target, m_hg_norm, m_hg_w_in, m_hg_lb_logits, m_hg_out_norm, m_hg_w_out, m_kv_norm, m_w_kv, m_attn_norm, m_attn_w_q, m_attn_sinks, m_attn_w_o, m_ffn_norm, m_ffn_w_up, m_ffn_conv_w, m_ffn_conv_b, m_ffn_w_down, m_final_norm, v_hg_norm, v_hg_w_in, v_hg_lb_logits, v_hg_out_norm, v_hg_w_out, v_kv_norm, v_w_kv, v_attn_norm, v_attn_w_q, v_attn_sinks, v_attn_w_o, v_ffn_norm, v_ffn_w_up, v_ffn_conv_w, v_ffn_conv_b, v_ffn_w_down, v_final_norm):
    wts = dict(hg_norm=hg_norm, hg_w_in=hg_w_in, hg_lb_logits=hg_lb_logits, hg_out_norm=hg_out_norm, hg_w_out=hg_w_out, kv_norm=kv_norm, w_kv=w_kv, attn_norm=attn_norm, attn_w_q=attn_w_q, attn_sinks=attn_sinks, attn_w_o=attn_w_o, ffn_norm=ffn_norm, ffn_w_up=ffn_w_up, ffn_conv_w=ffn_conv_w, ffn_conv_b=ffn_conv_b, ffn_w_down=ffn_w_down, final_norm=final_norm)
    mom1 = dict(hg_norm=m_hg_norm, hg_w_in=m_hg_w_in, hg_lb_logits=m_hg_lb_logits, hg_out_norm=m_hg_out_norm, hg_w_out=m_hg_w_out, kv_norm=m_kv_norm, w_kv=m_w_kv, attn_norm=m_attn_norm, attn_w_q=m_attn_w_q, attn_sinks=m_attn_sinks, attn_w_o=m_attn_w_o, ffn_norm=m_ffn_norm, ffn_w_up=m_ffn_w_up, ffn_conv_w=m_ffn_conv_w, ffn_conv_b=m_ffn_conv_b, ffn_w_down=m_ffn_w_down, final_norm=m_final_norm)
    mom2 = dict(hg_norm=v_hg_norm, hg_w_in=v_hg_w_in, hg_lb_logits=v_hg_lb_logits, hg_out_norm=v_hg_out_norm, hg_w_out=v_hg_w_out, kv_norm=v_kv_norm, w_kv=v_w_kv, attn_norm=v_attn_norm, attn_w_q=v_attn_w_q, attn_sinks=v_attn_sinks, attn_w_o=v_attn_w_o, ffn_norm=v_ffn_norm, ffn_w_up=v_ffn_w_up, ffn_conv_w=v_ffn_conv_w, ffn_conv_b=v_ffn_conv_b, ffn_w_down=v_ffn_w_down, final_norm=v_final_norm)
    names = list(wts)
    chip = 2 * lax.axis_index("x") + lax.axis_index("y")
    core = lax.axis_index("c")
    core_arr = jnp.reshape(core, (1,)).astype(jnp.int32)
    fs = D_FF // N_CHIPS
    ds = D_MODEL // N_CHIPS

    place_arr = jnp.stack([chip, core]).astype(jnp.int32)
    small = jnp.concatenate([hg_norm.reshape(-1), hg_lb_logits.reshape(-1), ffn_conv_w.reshape(-1)])
    n_small = small.shape[0]
    shards = [
        ("small", _pad_rows(small, SMALL_ROWS, SMALL_COLS), F32), ("hg_w_in", hg_w_in[0], BF16),
        ("hg_w_out", hg_w_out[0], BF16), ("ffn_w_up0", ffn_w_up[0], BF16), ("ffn_w_down0", ffn_w_down[0], BF16),
        ("w_kv", w_kv, BF16), ("attn_w_q", attn_w_q[0], BF16), ("attn_w_o", attn_w_o[0], BF16),
        ("ffn_w_up1", ffn_w_up[1], BF16), ("ffn_w_down1", ffn_w_down[1], BF16),
    ]
    n_first = 2
    stages = dict(first=(0, 0, n_first), layer0=(1, 0, 3), layer1=(1, 3, 8))
    placed = [_place_shard(s, place_arr, dt, name=f"place_{nm}") for nm, s, dt in shards[:n_first]]
    flights = [_start_copies("gather_start_first", placed, 3 * n_first, _gather_copies(0, n_first))]
    placed = [_place_shard(s, place_arr, dt, name=f"place_{nm}", deps=(flights[0][3],)) for nm, s, dt in shards[n_first:]]
    flights.append(_start_copies("gather_start_rest", placed, 3 * len(placed), _gather_copies(0, len(placed))))

    def fetch(w, stage, after):
        call, lo, hi = stages[stage]
        send_sems, recv_sems, bufs, _ = flights[call]
        got = _wait_copies(f"gather_wait_{stage}", bufs[lo:hi], send_sems, recv_sems, after, _gather_copies(lo, hi - lo))
        w = dict(w)
        if stage == "first":
            g_small = got[0].reshape(N_CHIPS, -1)[:, :n_small]
            conv_w = g_small[:, 3 * ds:].reshape(N_CHIPS, 2, 3, fs).transpose(1, 2, 0, 3).reshape(2, 3, D_FF)
            w.update(
                hg_norm=g_small[:, :ds].reshape(1, D_MODEL),
                hg_lb=g_small[:, ds:3 * ds].reshape(N_CHIPS, 2, ds).transpose(1, 0, 2).reshape(2, D_MODEL),
                ffn_conv_w=[conv_w[0], conv_w[1]], hg_w_in=got[1],
            )
        elif stage == "layer0":
            w.update(hg_w_out=got[0].reshape(1, D_MODEL, D_MODEL), ffn_w_up=[got[1], None], ffn_w_down=[got[2].reshape(1, D_FF, D_MODEL), None])
        else:
            w.update(
                w_kv=got[0].reshape(1, D_MODEL, 2 * LANES), attn_w_q=got[1].reshape(1, D_MODEL, D_MODEL),
                attn_w_o=got[2].reshape(1, D_MODEL, D_MODEL), ffn_w_up=[w["ffn_w_up"][0], got[3]],
                ffn_w_down=[w["ffn_w_down"][0], got[4].reshape(1, D_FF, D_MODEL)],
            )
        return w

    whole = dict(
        hg_out_norm=hg_out_norm, kv_norm=kv_norm.reshape(1, D_MODEL), attn_norm=attn_norm, attn_sinks=attn_sinks.reshape(ATT_QH),
        ffn_norm=[ffn_norm[0:1], ffn_norm[1:2]], ffn_conv_b=[ffn_conv_b[0:1], ffn_conv_b[1:2]], final_norm=final_norm.reshape(1, D_MODEL),
    )
    whole = fetch(whole, "first", flights[1][3])

    red, layer1 = {}, {}

    def by_rows(g, rows):
        return g.reshape(N_CHIPS, rows, g.shape[2])

    def hook(point, dh, grads):
        if point == "ffn1":
            red["ffn1"] = _Reduction("ffn1", [by_rows(grads["ffn_w_down"], fs), grads["ffn_w_up"]], place_arr, core_arr)
            return (red["ffn1"].token,)
        if point == "attn":
            red["ffn1"].to_chips(dh)
            layer1.update(grads)
            return (red["ffn1"].token,)
        if point == "ffn0":
            group = [by_rows(layer1["attn_w_o"], ds), by_rows(layer1["attn_w_q"], ds), by_rows(layer1["w_kv"], ds),
                     by_rows(grads["ffn_w_down"], fs), grads["ffn_w_up"]]
            red["mid"] = _Reduction("mid", group, place_arr, core_arr)
            return (red["mid"].token,)
        if point == "hgrn":
            red["ffn1"].to_core(dh)
            red["mid"].to_chips(dh)
            return (red["ffn1"].token, red["mid"].token)
        red["hg"] = _Reduction("hg", [by_rows(grads["hg_w_out"], ds), grads["hg_w_in"]], place_arr, core_arr)
        return (red["hg"].token,)

    loss, dx, grads = _local_step(x[0], loss_target[0], whole, fetch, hook)

    small_parts = [
        loss.reshape(-1), grads["hg_out_norm"].reshape(-1), grads["attn_sinks"].reshape(-1), grads["kv_norm"].reshape(-1),
        grads["attn_norm"].reshape(-1), grads["ffn_norm"][0].reshape(-1), grads["ffn_norm"][1].reshape(-1),
        grads["ffn_conv_b"][0].reshape(-1), grads["ffn_conv_b"][1].reshape(-1), grads["final_norm"].reshape(-1),
        grads["hg_norm"].reshape(-1), grads["hg_lb"].reshape(-1), grads["ffn_conv_w"][0].reshape(-1), grads["ffn_conv_w"][1].reshape(-1),
    ]
    sizes = [p.shape[0] for p in small_parts]
    flat = jnp.concatenate(small_parts)
    rows = -(-flat.shape[0] // (SUBLANES * LANES)) * SUBLANES
    summed = _allreduce_small(_pad_rows(flat, rows, LANES)).reshape(-1)
    red["hg"].to_chips(summed)
    offs = [0]
    for sz in sizes:
        offs.append(offs[-1] + sz)
    sm = [summed[offs[i]:offs[i + 1]] for i in range(len(sizes))]
    loss_out = sm[0][0]
    conv_w_full = jnp.stack([sm[12].reshape(3, D_FF), sm[13].reshape(3, D_FF)])
    small_grads = dict(
        hg_out_norm=sm[1].reshape(1, HG_DK), attn_sinks=sm[2][:ATT_QH].reshape(1, ATT_QH), kv_norm=sm[3], attn_norm=sm[4].reshape(1, D_MODEL),
        ffn_norm=jnp.stack([sm[5], sm[6]]), ffn_conv_b=jnp.stack([sm[7], sm[8]]), final_norm=sm[9],
        hg_norm=lax.dynamic_slice(sm[10].reshape(1, D_MODEL), (0, chip * ds), (1, ds)),
        hg_lb_logits=lax.dynamic_slice(sm[11].reshape(2, D_MODEL), (0, chip * ds), (2, ds)),
        ffn_conv_w=lax.dynamic_slice(conv_w_full, (0, 0, chip * fs), (2, 3, fs)),
    )

    out_g, out_d, out_m, out_v = {}, {}, {}, {}

    def update(name, g2):
        shape = wts[name].shape
        d2, m2, v2 = _adamw(wts[name].reshape(g2.shape), mom1[name].reshape(g2.shape), mom2[name].reshape(g2.shape), g2, name=f"adamw_{name}")
        out_g[name], out_d[name], out_m[name], out_v[name] = g2.reshape(shape), d2.reshape(shape), m2.reshape(shape), v2.reshape(shape)
        return d2

    def update_layer(name, g2, layer, prev):
        res = _adamw_layer(wts[name], mom1[name], mom2[name], g2, layer, prev, name=f"adamw_{name}{layer}")
        out_g[name], out_d[name], out_m[name], out_v[name] = res
        return res

    g_down1, g_up1 = red["ffn1"].finish(red["hg"].token)
    down1 = update_layer("ffn_w_down", g_down1, 1, None)
    up1 = update_layer("ffn_w_up", g_up1, 1, None)
    red["mid"].to_core(up1[1])
    g_o, g_q, g_kv, g_down0, g_up0 = red["mid"].finish(up1[2])
    update("attn_w_o", g_o)
    update("attn_w_q", g_q)
    update("w_kv", g_kv)
    update_layer("ffn_w_down", g_down0, 0, down1)
    last = update_layer("ffn_w_up", g_up0, 0, up1)
    red["hg"].to_core(last[1])
    g_out, g_in = red["hg"].finish(last[2])
    update("hg_w_out", g_out)
    update("hg_w_in", g_in)

    small_names = [n for n in names if n not in out_g]
    cat = lambda d: jnp.concatenate([d[n].reshape(-1) for n in small_names])
    n_flat = sum(wts[n].size for n in small_names)
    srows = -(-n_flat // (SUBLANES * LANES)) * SUBLANES
    packed = [_pad_rows(cat(d), srows, LANES) for d in (wts, mom1, mom2, small_grads)]
    d_s, m_s, v_s = _adamw(*packed, name="adamw_small")
    off = 0
    for n in small_names:
        sz, shape = wts[n].size, wts[n].shape
        out_g[n] = small_grads[n].reshape(shape)
        out_d[n] = d_s.reshape(-1)[off:off + sz].reshape(shape)
        out_m[n] = m_s.reshape(-1)[off:off + sz].reshape(shape)
        out_v[n] = v_s.reshape(-1)[off:off + sz].reshape(shape)
        off += sz

    grad_x = dx.reshape(x.shape)
    return (loss_out, grad_x, *[out_g[n] for n in names], *[out_d[n] for n in names], *[out_m[n] for n in names], *[out_v[n] for n in names])
```

```python
import functools

import jax
import jax.numpy as jnp
from jax import lax
from jax.experimental import pallas as pl
from jax.experimental.pallas import tpu as pltpu

F32 = jnp.float32
BF16 = jnp.bfloat16
MESH = pl.DeviceIdType.MESH

EPS = 1e-6
D_MODEL = 1024
HG_HEADS = 8
HG_DK = 128
HG_CHUNK = 64
ATT_HD = 64
ATT_QH = 16
ATT_KVH = 2
ATT_GROUP = ATT_QH // ATT_KVH
WINDOW = 128
D_FF = 2816
N_CHIPS = 4
N_DEV = 8
LANES = 128
SUBLANES = 8
VMEM_LIMIT_BYTES = 56 * 1024 * 1024
NEG = -1e30
ALIBI_SLOPES = tuple(2.0 ** (-8.0 * h / ATT_QH) for h in range(1, ATT_QH + 1))

ADAM_LR = 0.001
ADAM_B1 = 0.9
ADAM_B2 = 0.999
ADAM_EPS = 1e-08
ADAM_WD = 0.01
ADAM_STEP = 10


def _cparams(sem=None):
    return pltpu.CompilerParams(dimension_semantics=sem, vmem_limit_bytes=VMEM_LIMIT_BYTES)


def _pick(n, cands):
    for c in cands:
        if n % c == 0:
            return c
    return n


def _sigmoid(x):
    return 1.0 / (1.0 + jnp.exp(-x))


def _dot(a, b, dims):
    return lax.dot_general(a, b, (dims, ((), ())), preferred_element_type=F32)


NN = ((1,), (0,))
NT = ((1,), (1,))
TN = ((0,), (0,))


MM_ROWS = 1024


def _mm_nn(a, w, res=None, out_dtype=F32, name="mm_nn"):
    m, k = a.shape
    s, _, ns = w.shape
    tm = min(m, MM_ROWS)
    tn = _pick(ns, (512, 1408, 256, 128))
    npb = ns // tn

    def body(a_ref, w_ref, *rest):
        o_ref = rest[-1]
        acc = _dot(a_ref[...].astype(BF16), w_ref[...], NN)
        if res is not None:
            acc = acc + rest[0][...]
        o_ref[...] = acc.astype(o_ref.dtype)

    in_specs = [
        pl.BlockSpec((tm, k), lambda i, j: (i, 0)),
        pl.BlockSpec((None, k, tn), lambda i, j: (j // npb, 0, j % npb)),
    ]
    args = [a, w]
    if res is not None:
        in_specs.append(pl.BlockSpec((tm, tn), lambda i, j: (i, j)))
        args.append(res)
    return pl.pallas_call(
        body,
        name=name,
        grid=(m // tm, s * npb),
        in_specs=in_specs,
        out_specs=pl.BlockSpec((tm, tn), lambda i, j: (i, j)),
        out_shape=jax.ShapeDtypeStruct((m, s * ns), out_dtype),
        compiler_params=_cparams(("parallel", "parallel")),
    )(*args)


def _dy_spec(stacked, tm, tn, npb, row, kk):
    if stacked:
        return pl.BlockSpec((None, tm, tn), lambda *g: (kk(g) // npb, row(g), kk(g) % npb))
    return pl.BlockSpec((tm, tn), lambda *g: (row(g), kk(g)))


def _dep_specs(deps):
    return [pl.BlockSpec(d.shape, lambda *g: (0, 0)) for d in deps]


def _mm_nt(dy, w, stacked=False, out_dtype=F32, name="mm_nt", deps=()):
    s, k, ns = w.shape
    m = dy.shape[1] if stacked else dy.shape[0]
    tm = min(m, MM_ROWS)
    tko = _pick(k, (1024, 1408, 512, 256))
    tn = _pick(ns, (1024, 1408, 512, 256))
    npb = ns // tn
    nk = s * npb

    def body(dy_ref, w_ref, *rest):
        o_ref, acc_ref = rest[-2:]
        kk = pl.program_id(2)

        @pl.when(kk == 0)
        def _():
            acc_ref[...] = jnp.zeros_like(acc_ref)

        acc_ref[...] += _dot(dy_ref[...].astype(BF16), w_ref[...], NT)

        @pl.when(kk == nk - 1)
        def _():
            o_ref[...] = acc_ref[...].astype(o_ref.dtype)

    return pl.pallas_call(
        body,
        name=name,
        grid=(m // tm, k // tko, nk),
        in_specs=[
            _dy_spec(stacked, tm, tn, npb, lambda g: g[0], lambda g: g[2]),
            pl.BlockSpec((None, tko, tn), lambda i, j, kk: (kk // npb, j, kk % npb)),
        ] + _dep_specs(deps),
        out_specs=pl.BlockSpec((tm, tko), lambda i, j, kk: (i, j)),
        out_shape=jax.ShapeDtypeStruct((m, k), out_dtype),
        scratch_shapes=[pltpu.VMEM((tm, tko), F32)],
        compiler_params=_cparams(("parallel", "parallel", "arbitrary")),
    )(dy, w, *deps)


def _mm_tn(a, dy, s, ns, stacked=False, name="mm_tn", deps=()):
    m, k = a.shape
    tm = min(m, MM_ROWS)
    tk = _pick(k, (1024, 1408, 512, 256))
    tn = _pick(ns, (512, 1408, 256, 128))
    npb = ns // tn
    nm = m // tm

    def body(a_ref, dy_ref, *rest):
        o_ref, acc_ref = rest[-2:]
        mm = pl.program_id(2)

        @pl.when(mm == 0)
        def _():
            acc_ref[...] = jnp.zeros_like(acc_ref)

        acc_ref[...] += _dot(a_ref[...].astype(BF16), dy_ref[...].astype(BF16), TN)

        @pl.when(mm == nm - 1)
        def _():
            o_ref[...] = acc_ref[...]

    return pl.pallas_call(
        body,
        name=name,
        grid=(k // tk, s * npb, nm),
        in_specs=[
            pl.BlockSpec((tm, tk), lambda i, j, mm: (mm, i)),
            _dy_spec(stacked, tm, tn, npb, lambda g: g[2], lambda g: g[1]),
        ] + _dep_specs(deps),
        out_specs=pl.BlockSpec((None, tk, tn), lambda i, j, mm: (j // npb, i, j % npb)),
        out_shape=jax.ShapeDtypeStruct((s, k, ns), F32),
        scratch_shapes=[pltpu.VMEM((tk, tn), F32)],
        compiler_params=_cparams(("parallel", "parallel", "arbitrary")),
    )(a, dy, *deps)


ROW_TILE = 512


def _rms_fwd(x, g, name="rms_fwd"):
    t, d = x.shape
    r = min(t, ROW_TILE)

    def body(x_ref, g_ref, o_ref):
        xv = x_ref[...]
        rstd = lax.rsqrt(jnp.mean(xv * xv, axis=-1, keepdims=True) + EPS)
        o_ref[...] = (xv * rstd * g_ref[...]).astype(BF16)

    return pl.pallas_call(
        body,
        name=name,
        grid=(t // r,),
        in_specs=[pl.BlockSpec((r, d), lambda i: (i, 0)), pl.BlockSpec((1, d), lambda i: (0, 0))],
        out_specs=pl.BlockSpec((r, d), lambda i: (i, 0)),
        out_shape=jax.ShapeDtypeStruct((t, d), BF16),
        compiler_params=_cparams(("parallel",)),
    )(x, g)


def _rms_bwd(x, g, dxn, dres, name="rms_bwd"):
    t, d = x.shape
    r = min(t, ROW_TILE)

    def body(x_ref, g_ref, dxn_ref, dres_ref, dx_ref, dg_ref):
        @pl.when(pl.program_id(0) == 0)
        def _():
            dg_ref[...] = jnp.zeros_like(dg_ref)

        xv = x_ref[...]
        rstd = lax.rsqrt(jnp.mean(xv * xv, axis=-1, keepdims=True) + EPS)
        xhat = xv * rstd
        dxn_v = dxn_ref[...].astype(F32)
        gd = dxn_v * g_ref[...]
        dx_ref[...] = dres_ref[...] + rstd * (gd - xhat * jnp.mean(gd * xhat, axis=-1, keepdims=True))
        dg_ref[...] += jnp.sum(dxn_v * xhat, axis=0, keepdims=True)

    return pl.pallas_call(
        body,
        name=name,
        grid=(t // r,),
        in_specs=[
            pl.BlockSpec((r, d), lambda i: (i, 0)),
            pl.BlockSpec((1, d), lambda i: (0, 0)),
            pl.BlockSpec((r, d), lambda i: (i, 0)),
            pl.BlockSpec((r, d), lambda i: (i, 0)),
        ],
        out_specs=[pl.BlockSpec((r, d), lambda i: (i, 0)), pl.BlockSpec((1, d), lambda i: (0, 0))],
        out_shape=[jax.ShapeDtypeStruct((t, d), F32), jax.ShapeDtypeStruct((1, d), F32)],
        compiler_params=_cparams(("arbitrary",)),
    )(x, g, dxn, dres)


def _loss_head(h, g, target):
    t, d = h.shape
    r = min(t, ROW_TILE)

    def body(h_ref, g_ref, t_ref, dh_ref, dg_ref, loss_ref):
        @pl.when(pl.program_id(0) == 0)
        def _():
            dg_ref[...] = jnp.zeros_like(dg_ref)
            loss_ref[...] = jnp.zeros_like(loss_ref)

        xv = h_ref[...]
        rstd = lax.rsqrt(jnp.mean(xv * xv, axis=-1, keepdims=True) + EPS)
        xhat = xv * rstd
        gv = g_ref[...]
        err = xhat * gv - t_ref[...]
        loss_ref[...] += 0.5 * jnp.sum(jnp.mean(err * err, axis=-1, keepdims=True), axis=0, keepdims=True)
        dy = err * (1.0 / d)
        gd = dy * gv
        dh_ref[...] = rstd * (gd - xhat * jnp.mean(gd * xhat, axis=-1, keepdims=True))
        dg_ref[...] += jnp.sum(dy * xhat, axis=0, keepdims=True)

    return pl.pallas_call(
        body,
        name="loss_head",
        grid=(t // r,),
        in_specs=[
            pl.BlockSpec((r, d), lambda i: (i, 0)),
            pl.BlockSpec((1, d), lambda i: (0, 0)),
            pl.BlockSpec((r, d), lambda i: (i, 0)),
        ],
        out_specs=[
            pl.BlockSpec((r, d), lambda i: (i, 0)),
            pl.BlockSpec((1, d), lambda i: (0, 0)),
            pl.BlockSpec((1, LANES), lambda i: (0, 0)),
        ],
        out_shape=[
            jax.ShapeDtypeStruct((t, d), F32),
            jax.ShapeDtypeStruct((1, d), F32),
            jax.ShapeDtypeStruct((1, LANES), F32),
        ],
        compiler_params=_cparams(("arbitrary",)),
    )(h, g, target)


CONV_ROWS = 256
CONV_COLS = 1408


def _conv_taps(x_ext, n):
    tot = x_ext.shape[0]
    g1 = pltpu.roll(x_ext, 1, 0)[tot - n:]
    g2 = pltpu.roll(x_ext, 2, 0)[tot - n:]
    return g2, g1


def _conv_fwd(up, conv_w, conv_b, name="conv_fwd"):
    t = up.shape[0]
    r = min(t, CONV_ROWS)
    tc = CONV_COLS
    ncb = D_FF // tc
    hb = r // SUBLANES

    def body(g_ref, halo_ref, v_ref, w_ref, b_ref, o_ref):
        i = pl.program_id(1)
        g0 = g_ref[...]
        halo = halo_ref[...] * jnp.where(i > 0, 1.0, 0.0)
        g2, g1 = _conv_taps(jnp.concatenate([halo, g0], axis=0), r)
        c = b_ref[...] + w_ref[0:1, :] * g2 + w_ref[1:2, :] * g1 + w_ref[2:3, :] * g0
        o_ref[...] = (c * _sigmoid(c) * v_ref[...]).astype(BF16)

    return pl.pallas_call(
        body,
        name=name,
        grid=(ncb, t // r),
        in_specs=[
            pl.BlockSpec((r, tc), lambda j, i: (i, j)),
            pl.BlockSpec((SUBLANES, tc), lambda j, i: (jnp.maximum(i * hb - 1, 0), j)),
            pl.BlockSpec((r, tc), lambda j, i: (i, ncb + j)),
            pl.BlockSpec((3, tc), lambda j, i: (0, j)),
            pl.BlockSpec((1, tc), lambda j, i: (0, j)),
        ],
        out_specs=pl.BlockSpec((r, tc), lambda j, i: (i, j)),
        out_shape=jax.ShapeDtypeStruct((t, D_FF), BF16),
        compiler_params=_cparams(("parallel", "parallel")),
    )(up, up, up, conv_w, conv_b)


def _conv_bwd(up, conv_w, conv_b, dact, name="conv_bwd"):
    t = up.shape[0]
    r = min(t, CONV_ROWS)
    tc = CONV_COLS
    ncb = D_FF // tc
    hb = r // SUBLANES
    nrt = t // r

    def body(g_ref, halo_ref, v_ref, w_ref, b_ref, da_ref, dup_ref, dw_ref, db_ref, nxt_ref):
        ii = pl.program_id(1)
        i = nrt - 1 - ii

        @pl.when(ii == 0)
        def _():
            nxt_ref[...] = jnp.zeros_like(nxt_ref)
            dw_ref[...] = jnp.zeros_like(dw_ref)
            db_ref[...] = jnp.zeros_like(db_ref)

        g0 = g_ref[...]
        halo = halo_ref[...] * jnp.where(i > 0, 1.0, 0.0)
        g2, g1 = _conv_taps(jnp.concatenate([halo, g0], axis=0), r)
        w0, w1, w2 = w_ref[0:1, :], w_ref[1:2, :], w_ref[2:3, :]
        c = b_ref[...] + w0 * g2 + w1 * g1 + w2 * g0
        sg = _sigmoid(c)
        da = da_ref[...]
        dval = da * (c * sg)
        dc = da * v_ref[...] * (sg * (1.0 + c * (1.0 - sg)))
        db_ref[...] += jnp.sum(dc, axis=0, keepdims=True)
        dw_ref[0:1, :] += jnp.sum(dc * g2, axis=0, keepdims=True)
        dw_ref[1:2, :] += jnp.sum(dc * g1, axis=0, keepdims=True)
        dw_ref[2:3, :] += jnp.sum(dc * g0, axis=0, keepdims=True)
        ext = jnp.concatenate([dc, nxt_ref[...]], axis=0)
        tot = r + SUBLANES
        d1 = pltpu.roll(ext, tot - 1, 0)[:r]
        d2 = pltpu.roll(ext, tot - 2, 0)[:r]
        dgate = w2 * dc + w1 * d1 + w0 * d2
        nxt_ref[...] = dc[:SUBLANES]
        dup_ref[0] = dgate.astype(BF16)
        dup_ref[1] = dval.astype(BF16)

    rev = lambda ii: nrt - 1 - ii
    dup, dw, db = pl.pallas_call(
        body,
        name=name,
        grid=(ncb, nrt),
        in_specs=[
            pl.BlockSpec((r, tc), lambda j, ii: (rev(ii), j)),
            pl.BlockSpec((SUBLANES, tc), lambda j, ii: (jnp.maximum(rev(ii) * hb - 1, 0), j)),
            pl.BlockSpec((r, tc), lambda j, ii: (rev(ii), ncb + j)),
            pl.BlockSpec((3, tc), lambda j, ii: (0, j)),
            pl.BlockSpec((1, tc), lambda j, ii: (0, j)),
            pl.BlockSpec((r, tc), lambda j, ii: (rev(ii), j)),
        ],
        out_specs=[
            pl.BlockSpec((2, None, r, tc), lambda j, ii: (0, j, rev(ii), 0)),
            pl.BlockSpec((3, tc), lambda j, ii: (0, j)),
            pl.BlockSpec((1, tc), lambda j, ii: (0, j)),
        ],
        out_shape=[
            jax.ShapeDtypeStruct((2, ncb, t, tc), BF16),
            jax.ShapeDtypeStruct((3, D_FF), F32),
            jax.ShapeDtypeStruct((1, D_FF), F32),
        ],
        scratch_shapes=[pltpu.VMEM((SUBLANES, tc), F32)],
        compiler_params=_cparams(("parallel", "arbitrary")),
    )(up, up, up, conv_w, conv_b, dact)
    return dup.reshape(2 * ncb, t, tc), dw, db


def _split3(x):
    x1 = x.astype(BF16)
    r1 = x - x1.astype(F32)
    x2 = r1.astype(BF16)
    x3 = (r1 - x2.astype(F32)).astype(BF16)
    return x1, x2, x3


def _tri_dot(tri, x, dims):
    x1, x2, x3 = _split3(x)
    return _dot(tri, x1, dims) + _dot(tri, x2, dims) + _dot(tri, x3, dims)


def _lower_bound(logits_ref):
    return _sigmoid(logits_ref[0:1, :] - logits_ref[1:2, :])


def _hg_gates(qr, fr, lb):
    q = qr * _sigmoid(qr) * (HG_DK ** -0.5)
    sf = _sigmoid(fr)
    fg = lb + (1.0 - lb) * sf
    return q, sf, fg


def _hg_chunk_terms(q, fg, tril_b, low_half):
    g = jnp.log(fg)
    k = 1.0 - fg
    cum = _tri_dot(tril_b, g, NN)
    c_last = jnp.sum(g, axis=0, keepdims=True)
    c_mid = jnp.sum(jnp.where(low_half, g, 0.0), axis=0, keepdims=True)
    e_q = jnp.exp(cum - c_mid)
    e_k = jnp.exp(c_mid - cum)
    e_0 = jnp.exp(cum)
    e_l = jnp.exp(c_last - cum)
    return k, e_q, e_k, e_0, e_l, jnp.exp(c_last)


HG_BLOCK = 256


def _hg_proj_specs(rb, row):
    return [pl.BlockSpec((rb, D_MODEL), functools.partial(lambda i, k: (row(i), k), k=k)) for k in range(4)]


def _hg_consts(c):
    tril = lax.broadcasted_iota(jnp.int32, (c, c), 0) >= lax.broadcasted_iota(jnp.int32, (c, c), 1)
    low_half = lax.broadcasted_iota(jnp.int32, (c, HG_DK), 0) < c // 2
    return tril, tril.astype(BF16), low_half


def _hgrn_fwd(proj, lb, wn):
    t = proj.shape[0]
    c = HG_CHUNK
    rb = min(t, HG_BLOCK)
    cpb = rb // c

    def body(q_ref, f_ref, i_ref, g_ref, lb_ref, wn_ref, o_ref, y_ref, st_ref, s_scr):
        @pl.when(pl.program_id(0) == 0)
        def _():
            s_scr[...] = jnp.zeros_like(s_scr)

        lb_all = _lower_bound(lb_ref)
        wnv = wn_ref[...]
        tril, tril_b, low_half = _hg_consts(c)

        def chunk(n, carry):
            rows = pl.ds(pl.multiple_of(n * c, c), c)
            for h in range(HG_HEADS):
                cols = slice(h * HG_DK, (h + 1) * HG_DK)
                q, _, fg = _hg_gates(q_ref[rows, cols], f_ref[rows, cols], lb_all[:, cols])
                v = i_ref[rows, cols].astype(BF16)
                k, e_q, e_k, e_0, e_l, e_last = _hg_chunk_terms(q, fg, tril_b, low_half)
                st = s_scr[h]
                st_ref[h, n] = st
                a = jnp.where(tril, _dot((q * e_q).astype(BF16), (k * e_k).astype(BF16), NT), 0.0)
                o = _dot((q * e_0).astype(BF16), st.astype(BF16), NT) + _dot(a.astype(BF16), v, NN)
                s_scr[h] = st * e_last + _dot(v, (k * e_l).astype(BF16), TN)
                o_ref[rows, cols] = o
                rstd = lax.rsqrt(jnp.mean(o * o, axis=-1, keepdims=True) + EPS)
                gr = g_ref[rows, cols]
                y_ref[rows, cols] = (o * rstd * wnv * (gr * _sigmoid(gr))).astype(BF16)
            return carry

        lax.fori_loop(0, cpb, chunk, 0)

    blk = pl.BlockSpec((rb, D_MODEL), lambda i: (i, 0))
    return pl.pallas_call(
        body,
        name="hgrn_fwd",
        grid=(t // rb,),
        in_specs=_hg_proj_specs(rb, lambda i: i) + [pl.BlockSpec((2, D_MODEL), lambda i: (0, 0)), pl.BlockSpec((1, HG_DK), lambda i: (0, 0))],
        out_specs=[blk, blk, pl.BlockSpec((HG_HEADS, cpb, HG_DK, HG_DK), lambda i: (0, i, 0, 0))],
        out_shape=[
            jax.ShapeDtypeStruct((t, D_MODEL), F32),
            jax.ShapeDtypeStruct((t, D_MODEL), BF16),
            jax.ShapeDtypeStruct((HG_HEADS, t // c, HG_DK, HG_DK), F32),
        ],
        scratch_shapes=[pltpu.VMEM((HG_HEADS, HG_DK, HG_DK), F32)],
        compiler_params=_cparams(("arbitrary",)),
    )(proj, proj, proj, proj, lb, wn)


def _hgrn_bwd(proj, lb, wn, o, states, dy):
    t = proj.shape[0]
    c = HG_CHUNK
    rb = min(t, HG_BLOCK)
    cpb = rb // c
    nb = t // rb

    def body(q_ref, f_ref, i_ref, g_ref, lb_ref, wn_ref, o_ref, st_ref, dy_ref, dp_ref, dl_ref, dwn_ref, ds_scr, dlb_scr):
        step = pl.program_id(0)

        @pl.when(step == 0)
        def _():
            dwn_ref[...] = jnp.zeros_like(dwn_ref)
            ds_scr[...] = jnp.zeros_like(ds_scr)
            dlb_scr[...] = jnp.zeros_like(dlb_scr)

        lb_all = _lower_bound(lb_ref)
        wnv = wn_ref[...]
        tril, tril_b, low_half = _hg_consts(c)

        def chunk(nn, carry):
            n = cpb - 1 - nn
            rows = pl.ds(pl.multiple_of(n * c, c), c)
            for h in range(HG_HEADS):
                cols = slice(h * HG_DK, (h + 1) * HG_DK)
                lbv = lb_all[:, cols]
                ov = o_ref[rows, cols]
                gr = g_ref[rows, cols]
                dyv = dy_ref[rows, cols].astype(F32)
                rstd = lax.rsqrt(jnp.mean(ov * ov, axis=-1, keepdims=True) + EPS)
                ohat = ov * rstd
                sg = _sigmoid(gr)
                dg_raw = dyv * (ohat * wnv) * (sg * (1.0 + gr * (1.0 - sg)))
                don = dyv * (gr * sg)
                dwn_ref[...] += jnp.sum(don * ohat, axis=0, keepdims=True)
                gd = don * wnv
                do = rstd * (gd - ohat * jnp.mean(gd * ohat, axis=-1, keepdims=True))
                do_b = do.astype(BF16)
                qr = q_ref[rows, cols]
                q, sf, fg = _hg_gates(qr, f_ref[rows, cols], lbv)
                v = i_ref[rows, cols].astype(BF16)
                k, e_q, e_k, e_0, e_l, e_last = _hg_chunk_terms(q, fg, tril_b, low_half)
                qi, qi_lo, _ = _split3(q * e_q)
                ki, ki_lo, _ = _split3(k * e_k)
                q0 = (q * e_0).astype(BF16)
                kl = (k * e_l).astype(BF16)
                st = st_ref[h, n]
                st_b = st.astype(BF16)
                ds = ds_scr[h]
                ds_b = ds.astype(BF16)
                a_b = jnp.where(tril, _dot(qi, ki, NT), 0.0).astype(BF16)
                da_b = jnp.where(tril, _dot(do_b, v, NT), 0.0).astype(BF16)
                dq = _dot(do_b, st_b, NN) * e_0 + (_dot(da_b, ki, NN) + _dot(da_b, ki_lo, NN)) * e_q
                dk_state = _dot(v, ds_b, NN) * e_l
                dk = (_dot(da_b, qi, TN) + _dot(da_b, qi_lo, TN)) * e_k + dk_state
                dv = _dot(a_b, do_b, TN) + _dot(kl, ds_b, NT)
                ds_scr[h] = ds * e_last + _dot(do_b, q0, TN)
                d_last = jnp.sum(dk_state * k, axis=0, keepdims=True) + jnp.sum(ds * st, axis=0, keepdims=True) * e_last
                dlogf = _tri_dot(tril_b, q * dq - k * dk, TN) + d_last
                dfg = dlogf / fg - dk
                dlb_scr[:, cols] += jnp.sum(dfg * (1.0 - sf), axis=0, keepdims=True)
                sq = _sigmoid(qr)
                dp_ref[0, rows, cols] = (dq * (HG_DK ** -0.5) * (sq * (1.0 + qr * (1.0 - sq)))).astype(BF16)
                dp_ref[1, rows, cols] = (dfg * (1.0 - lbv) * sf * (1.0 - sf)).astype(BF16)
                dp_ref[2, rows, cols] = dv.astype(BF16)
                dp_ref[3, rows, cols] = dg_raw.astype(BF16)
            return carry

        lax.fori_loop(0, cpb, chunk, 0)

        @pl.when(step == nb - 1)
        def _():
            d0 = dlb_scr[...] * lb_all * (1.0 - lb_all)
            dl_ref[0:1, :] = d0
            dl_ref[1:2, :] = -d0

    rev = lambda i: nb - 1 - i
    blk = pl.BlockSpec((rb, D_MODEL), lambda i: (rev(i), 0))
    return pl.pallas_call(
        body,
        name="hgrn_bwd",
        grid=(nb,),
        in_specs=_hg_proj_specs(rb, rev)
        + [pl.BlockSpec((2, D_MODEL), lambda i: (0, 0)), pl.BlockSpec((1, HG_DK), lambda i: (0, 0)), blk,
           pl.BlockSpec((HG_HEADS, cpb, HG_DK, HG_DK), lambda i: (0, rev(i), 0, 0)), blk],
        out_specs=[
            pl.BlockSpec((4, rb, D_MODEL), lambda i: (0, rev(i), 0)),
            pl.BlockSpec((2, D_MODEL), lambda i: (0, 0)),
            pl.BlockSpec((1, HG_DK), lambda i: (0, 0)),
        ],
        out_shape=[
            jax.ShapeDtypeStruct((4, t, D_MODEL), BF16),
            jax.ShapeDtypeStruct((2, D_MODEL), F32),
            jax.ShapeDtypeStruct((1, HG_DK), F32),
        ],
        scratch_shapes=[pltpu.VMEM((HG_HEADS, HG_DK, HG_DK), F32), pltpu.VMEM((1, D_MODEL), F32)],
        compiler_params=_cparams(("arbitrary",)),
    )(proj, proj, proj, proj, lb, wn, o, states, dy)


def _att_masks(n):
    tq = lax.broadcasted_iota(jnp.int32, (WINDOW, WINDOW), 0)
    sk = lax.broadcasted_iota(jnp.int32, (WINDOW, WINDOW), 1)
    valid_c = sk <= tq
    valid_p = (sk - tq) > jnp.where(n > 0, 0, WINDOW)
    dist_c = (tq - sk).astype(F32)
    dist_p = dist_c + float(WINDOW)
    return valid_p, valid_c, dist_p, dist_c


def _att_halves(x, lo, kh):
    r = pltpu.roll(x, ATT_HD, 1)
    zero = jnp.zeros_like(x)
    if kh == 0:
        return jnp.where(lo, x, r), jnp.where(lo, x, zero), jnp.where(lo, zero, r)
    return jnp.where(lo, r, x), jnp.where(lo, r, zero), jnp.where(lo, zero, x)


def _att_probs(qm, k2p, k2c, masks, slope, sink):
    valid_p, valid_c, dist_p, dist_c = masks
    sp = jnp.where(valid_p, _dot(qm, k2p, NT) * (ATT_HD ** -0.5) - slope * dist_p, NEG)
    sc = jnp.where(valid_c, _dot(qm, k2c, NT) * (ATT_HD ** -0.5) - slope * dist_c, NEG)
    m = jnp.maximum(jnp.maximum(jnp.max(sp, axis=-1, keepdims=True), jnp.max(sc, axis=-1, keepdims=True)), sink)
    ep = jnp.exp(sp - m)
    ec = jnp.exp(sc - m)
    es = jnp.exp(sink - m)
    inv = 1.0 / (jnp.sum(ep, axis=-1, keepdims=True) + jnp.sum(ec, axis=-1, keepdims=True) + es)
    return ep * inv, ec * inv, es * inv


def _attn_fwd(q, kv, sinks):
    t = q.shape[0]
    nb = t // WINDOW

    def body(sink_ref, q_ref, kvp_ref, kvc_ref, o_ref):
        n = pl.program_id(0)
        masks = _att_masks(n)
        lo = lax.broadcasted_iota(jnp.int32, (WINDOW, LANES), 1) < ATT_HD
        for kh in range(ATT_KVH):
            k2p, _, _ = _att_halves(kvp_ref[:, 0:LANES], lo, kh)
            k2c, _, _ = _att_halves(kvc_ref[:, 0:LANES], lo, kh)
            _, vlo_p, vhi_p = _att_halves(kvp_ref[:, LANES:2 * LANES], lo, kh)
            _, vlo_c, vhi_c = _att_halves(kvc_ref[:, LANES:2 * LANES], lo, kh)
            for jj in range(ATT_GROUP // 2):
                j = kh * (ATT_GROUP // 2) + jj
                qp = q_ref[:, j * LANES:(j + 1) * LANES]
                zero = jnp.zeros_like(qp)
                out = None
                for par in range(2):
                    hq = 2 * j + par
                    qm = jnp.where(lo, qp, zero) if par == 0 else jnp.where(lo, zero, qp)
                    pp, pc, _ = _att_probs(qm, k2p, k2c, masks, ALIBI_SLOPES[hq], sink_ref[hq])
                    vp, vc = (vlo_p, vlo_c) if par == 0 else (vhi_p, vhi_c)
                    part = _dot(pp.astype(BF16), vp, NN) + _dot(pc.astype(BF16), vc, NN)
                    out = part if out is None else out + part
                o_ref[:, j * LANES:(j + 1) * LANES] = out.astype(BF16)

    return pl.pallas_call(
        body,
        name="attn_fwd",
        grid=(nb,),
        in_specs=[
            pl.BlockSpec(memory_space=pltpu.SMEM),
            pl.BlockSpec((WINDOW, D_MODEL), lambda n: (n, 0)),
            pl.BlockSpec((WINDOW, 2 * LANES), lambda n: (jnp.maximum(n - 1, 0), 0)),
            pl.BlockSpec((WINDOW, 2 * LANES), lambda n: (n, 0)),
        ],
        out_specs=pl.BlockSpec((WINDOW, D_MODEL), lambda n: (n, 0)),
        out_shape=jax.ShapeDtypeStruct((t, D_MODEL), BF16),
        compiler_params=_cparams(("parallel",)),
    )(sinks, q, kv, kv)


def _attn_bwd(q, kv, sinks, dout):
    t = q.shape[0]
    nb = t // WINDOW

    def body(sink_ref, q_ref, kvp_ref, kvc_ref, do_ref, dq_ref, dkv_ref, dsink_ref, carry_ref):
        n = pl.program_id(0)

        @pl.when(n == 0)
        def _():
            carry_ref[...] = jnp.zeros_like(carry_ref)
            dsink_ref[...] = jnp.zeros_like(dsink_ref)

        @pl.when(n == nb)
        def _():
            dkv_ref[...] = carry_ref[...].astype(BF16)

        @pl.when(n < nb)
        def _():
            masks = _att_masks(n)
            lo = lax.broadcasted_iota(jnp.int32, (WINDOW, LANES), 1) < ATT_HD
            lane1 = lax.broadcasted_iota(jnp.int32, (1, LANES), 1)
            dsink = jnp.zeros((1, LANES), F32)
            halves = []
            for kh in range(ATT_KVH):
                k2p, klo_p, khi_p = _att_halves(kvp_ref[:, 0:LANES], lo, kh)
                k2c, klo_c, khi_c = _att_halves(kvc_ref[:, 0:LANES], lo, kh)
                v2p, _, _ = _att_halves(kvp_ref[:, LANES:2 * LANES], lo, kh)
                v2c, _, _ = _att_halves(kvc_ref[:, LANES:2 * LANES], lo, kh)
                acc = [jnp.zeros((WINDOW, LANES), F32) for _ in range(4)]
                for jj in range(ATT_GROUP // 2):
                    j = kh * (ATT_GROUP // 2) + jj
                    cols = slice(j * LANES, (j + 1) * LANES)
                    qp = q_ref[:, cols]
                    dop = do_ref[:, cols]
                    zero = jnp.zeros_like(qp)
                    dq_pair = None
                    for par in range(2):
                        hq = 2 * j + par
                        sel = lo if par == 0 else jnp.logical_not(lo)
                        qm = jnp.where(sel, qp, zero)
                        dom = jnp.where(sel, dop, zero)
                        pp, pc, ps = _att_probs(qm, k2p, k2c, masks, ALIBI_SLOPES[hq], sink_ref[hq])
                        dpp = _dot(dom, v2p, NT)
                        dpc = _dot(dom, v2c, NT)
                        delta = jnp.sum(pp * dpp, axis=-1, keepdims=True) + jnp.sum(pc * dpc, axis=-1, keepdims=True)
                        dsp = (pp * (dpp - delta)).astype(BF16)
                        dsc = (pc * (dpc - delta)).astype(BF16)
                        dsink = dsink + jnp.where(lane1 == hq, -jnp.sum(ps * delta, axis=0, keepdims=True), 0.0)
                        kp_, kc_ = (klo_p, klo_c) if par == 0 else (khi_p, khi_c)
                        part = _dot(dsp, kp_, NN) + _dot(dsc, kc_, NN)
                        dq_pair = part if dq_pair is None else dq_pair + part
                        acc[0] = acc[0] + _dot(dsp, qm, TN)
                        acc[1] = acc[1] + _dot(dsc, qm, TN)
                        acc[2] = acc[2] + _dot(pp.astype(BF16), dom, TN)
                        acc[3] = acc[3] + _dot(pc.astype(BF16), dom, TN)
                    dq_ref[:, cols] = (dq_pair * (ATT_HD ** -0.5)).astype(BF16)
                halves.append([a + pltpu.roll(a, ATT_HD, 1) for a in acc])
            scale = ATT_HD ** -0.5
            prev = jnp.concatenate(
                [jnp.where(lo, halves[0][0], halves[1][0]) * scale, jnp.where(lo, halves[0][2], halves[1][2])], axis=1)
            cur = jnp.concatenate(
                [jnp.where(lo, halves[0][1], halves[1][1]) * scale, jnp.where(lo, halves[0][3], halves[1][3])], axis=1)
            dkv_ref[...] = (carry_ref[...] + prev).astype(BF16)
            carry_ref[...] = cur
            dsink_ref[...] += dsink

    blk = lambda n: jnp.minimum(n, nb - 1)
    return pl.pallas_call(
        body,
        name="attn_bwd",
        grid=(nb + 1,),
        in_specs=[
            pl.BlockSpec(memory_space=pltpu.SMEM),
            pl.BlockSpec((WINDOW, D_MODEL), lambda n: (blk(n), 0)),
            pl.BlockSpec((WINDOW, 2 * LANES), lambda n: (jnp.maximum(blk(n) - 1, 0), 0)),
            pl.BlockSpec((WINDOW, 2 * LANES), lambda n: (blk(n), 0)),
            pl.BlockSpec((WINDOW, D_MODEL), lambda n: (blk(n), 0)),
        ],
        out_specs=[
            pl.BlockSpec((WINDOW, D_MODEL), lambda n: (blk(n), 0)),
            pl.BlockSpec((WINDOW, 2 * LANES), lambda n: (jnp.maximum(n - 1, 0), 0)),
            pl.BlockSpec((1, LANES), lambda n: (0, 0)),
        ],
        out_shape=[
            jax.ShapeDtypeStruct((t, D_MODEL), BF16),
            jax.ShapeDtypeStruct((t, 2 * LANES), BF16),
            jax.ShapeDtypeStruct((1, LANES), F32),
        ],
        scratch_shapes=[pltpu.VMEM((WINDOW, 2 * LANES), F32)],
        compiler_params=_cparams(("arbitrary",)),
    )(sinks, q, kv, kv, dout)


def _ffn_fwd(h, norm_g, w_up, conv_w, conv_b, w_down, tag, after_up=lambda up: None):
    xn = _rms_fwd(h, norm_g, name=f"ffn{tag}_norm")
    up = _mm_nn(xn, w_up, name=f"ffn{tag}_up")
    after_up(up)
    act = _conv_fwd(up, conv_w, conv_b, name=f"ffn{tag}_conv")
    h_out = _mm_nn(act, w_down, res=h, name=f"ffn{tag}_down")
    return h_out, (xn, up, act)


def _ffn_bwd(dh, h, norm_g, w_up, conv_w, conv_b, w_down, saved, tag, deps=()):
    xn, up, act = saved
    dw_down = _mm_tn(act, dh, 1, D_MODEL, name=f"ffn{tag}_dwdown", deps=deps)
    dact = _mm_nt(dh, w_down, name=f"ffn{tag}_dact", deps=deps)
    dup, dconv_w, dconv_b = _conv_bwd(up, conv_w, conv_b, dact, name=f"ffn{tag}_dconv")
    dw_up = _mm_tn(xn, dup, N_CHIPS, CONV_COLS, stacked=True, name=f"ffn{tag}_dwup")
    dxn = _mm_nt(dup, w_up, stacked=True, name=f"ffn{tag}_dxn")
    dh_in, dnorm = _rms_bwd(h, norm_g, dxn, dh, name=f"ffn{tag}_dnorm")
    return dh_in, dict(ffn_w_down=dw_down, ffn_w_up=dw_up, ffn_conv_w=dconv_w, ffn_conv_b=dconv_b, ffn_norm=dnorm)


def _local_step(x, target, w, fetch=lambda w, stage, after: w, hook=lambda point, dh, grads: ()):
    xn0 = _rms_fwd(x, w["hg_norm"], name="hg_norm")
    proj = _mm_nn(xn0, w["hg_w_in"], name="hg_in")
    fetch(w, "layer0_relay", proj)
    o, y, states = _hgrn_fwd(proj, w["hg_lb"], w["hg_out_norm"])
    w = fetch(w, "layer0", y)
    h_a = _mm_nn(y, w["hg_w_out"], res=x, name="hg_out")
    h1, ffn0 = _ffn_fwd(h_a, w["ffn_norm"][0], w["ffn_w_up"][0], w["ffn_conv_w"][0], w["ffn_conv_b"][0], w["ffn_w_down"][0], 0,
                        lambda up: fetch(w, "layer1_relay", up))
    w = fetch(w, "layer1", h1)
    kvn = _rms_fwd(h1, w["kv_norm"], name="kv_norm")
    kv = _mm_nn(kvn, w["w_kv"], out_dtype=BF16, name="kv_proj")
    xa = _rms_fwd(h1, w["attn_norm"], name="attn_norm")
    qa = _mm_nn(xa, w["attn_w_q"], out_dtype=BF16, name="attn_q")
    ao = _attn_fwd(qa, kv, w["attn_sinks"])
    h_b = _mm_nn(ao, w["attn_w_o"], res=h1, name="attn_o")
    h2, ffn1 = _ffn_fwd(h_b, w["ffn_norm"][1], w["ffn_w_up"][1], w["ffn_conv_w"][1], w["ffn_conv_b"][1], w["ffn_w_down"][1], 1)
    dh2, d_final, loss = _loss_head(h2, w["final_norm"], target)

    dh_b, g1 = _ffn_bwd(dh2, h_b, w["ffn_norm"][1], w["ffn_w_up"][1], w["ffn_conv_w"][1], w["ffn_conv_b"][1], w["ffn_w_down"][1], ffn1, 1)
    deps = hook("ffn1", dh_b, g1)
    dw_o = _mm_tn(ao, dh_b, 1, D_MODEL, name="attn_dwo", deps=deps)
    dao = _mm_nt(dh_b, w["attn_w_o"], out_dtype=BF16, name="attn_dao", deps=deps)
    dqa, dkv, dsinks = _attn_bwd(qa, kv, w["attn_sinks"], dao)
    dw_q = _mm_tn(xa, dqa, 1, D_MODEL, name="attn_dwq")
    dxa = _mm_nt(dqa, w["attn_w_q"], name="attn_dxa")
    dh1, d_attn_norm = _rms_bwd(h1, w["attn_norm"], dxa, dh_b, name="attn_dnorm")
    dw_kv = _mm_tn(kvn, dkv, 1, 2 * LANES, name="kv_dw")
    dkvn = _mm_nt(dkv, w["w_kv"], name="kv_dx")
    dh1, d_kv_norm = _rms_bwd(h1, w["kv_norm"], dkvn, dh1, name="kv_dnorm")
    deps = hook("attn", dh1, dict(attn_w_o=dw_o, attn_w_q=dw_q, w_kv=dw_kv))
    dh_a, g0 = _ffn_bwd(dh1, h_a, w["ffn_norm"][0], w["ffn_w_up"][0], w["ffn_conv_w"][0], w["ffn_conv_b"][0], w["ffn_w_down"][0], ffn0, 0, deps)
    deps = hook("ffn0", dh_a, g0)
    dw_out = _mm_tn(y, dh_a, 1, D_MODEL, name="hg_dwout", deps=deps)
    dy = _mm_nt(dh_a, w["hg_w_out"], out_dtype=BF16, name="hg_dy", deps=deps)
    dproj, dlb, d_out_norm = _hgrn_bwd(proj, w["hg_lb"], w["hg_out_norm"], o, states, dy)
    deps = hook("hgrn", dproj, None)
    dw_in = _mm_tn(xn0, dproj, N_CHIPS, D_MODEL, stacked=True, name="hg_dwin", deps=deps)
    deps = hook("hg_w", dproj, dict(hg_w_out=dw_out, hg_w_in=dw_in))
    dxn0 = _mm_nt(dproj, w["hg_w_in"], stacked=True, name="hg_dxn", deps=deps)
    dx, d_hg_norm = _rms_bwd(x, w["hg_norm"], dxn0, dh_a, name="hg_dnorm")

    grads = dict(
        hg_norm=d_hg_norm, hg_w_in=dw_in, hg_lb=dlb, hg_out_norm=d_out_norm, hg_w_out=dw_out,
        kv_norm=d_kv_norm, w_kv=dw_kv, attn_norm=d_attn_norm, attn_w_q=dw_q, attn_sinks=dsinks, attn_w_o=dw_o,
        final_norm=d_final,
    )
    for name in g0:
        grads[name] = [g0[name], g1[name]]
    return loss, dx, grads


ANY = pl.BlockSpec(memory_space=pl.ANY)


def _place():
    x, y, c = lax.axis_index("x"), lax.axis_index("y"), lax.axis_index("c")
    chips = [(1 - x, y), (x, 1 - y), (1 - x, 1 - y)]
    return x, y, c, chips


def _rcopy(src, dst, send_sem, recv_sem, to):
    return pltpu.make_async_remote_copy(src_ref=src, dst_ref=dst, send_sem=send_sem, recv_sem=recv_sem, device_id=to, device_id_type=MESH)


HBM = pl.BlockSpec(memory_space=pltpu.HBM)
SEM = pl.BlockSpec(memory_space=pltpu.SEMAPHORE)
EFFECT = pltpu.SideEffectType.DATAFLOW_SIDE_EFFECTING


def _in_hbm(a):
    return pltpu.with_memory_space_constraint(a, pltpu.HBM)


def _place_shard(shard, place, dtype, name, deps=()):
    r, cols = shard.shape
    tr = _pick(r, ELEM_ROWS)

    def body(place_ref, s_ref, *rest):
        o_ref = rest[-1]
        o_ref[...] = s_ref[...].astype(o_ref.dtype)

    return pl.pallas_call(
        body,
        name=name,
        grid_spec=pltpu.PrefetchScalarGridSpec(
            num_scalar_prefetch=1,
            grid=(r // tr,),
            in_specs=[pl.BlockSpec((tr, cols), lambda i, place_ref: (i, 0))] + _dep_specs(deps),
            out_specs=pl.BlockSpec((None, tr, cols), lambda i, place_ref: (place_ref[0], i, 0)),
        ),
        out_shape=jax.ShapeDtypeStruct((N_CHIPS, r, cols), dtype),
        compiler_params=_cparams(("parallel",)),
    )(place, shard, *deps)


def _start_copies(name, bufs, n_sem, copies):
    n = len(bufs)

    def body(*refs):
        for cp in copies(refs[:n], refs[n], refs[n + 1]):
            cp.start()
        refs[-1][...] = jnp.zeros_like(refs[-1])

    outs = pl.pallas_call(
        body,
        name=name,
        in_specs=[HBM] * n,
        out_specs=[SEM, SEM] + [HBM] * n + [pl.BlockSpec(memory_space=pltpu.VMEM)],
        out_shape=[pltpu.SemaphoreType.DMA((n_sem,)), pltpu.SemaphoreType.DMA((n_sem,))] + [pltpu.HBM(b.shape, b.dtype) for b in bufs]
        + [jax.ShapeDtypeStruct((SUBLANES, LANES), F32)],
        input_output_aliases={i: 2 + i for i in range(n)},
        compiler_params=pltpu.CompilerParams(has_side_effects=EFFECT),
    )(*[_in_hbm(b) for b in bufs])
    return outs[0], outs[1], list(outs[2:-1]), outs[-1]


def _wait_copies(name, bufs, send_sems, recv_sems, after, copies):
    n = len(bufs)

    def body(*refs):
        for cp in copies(refs[:n], refs[n], refs[n + 1]):
            cp.wait_send()
            cp.wait_recv()

    return pl.pallas_call(
        body,
        name=name,
        in_specs=[HBM] * n + [SEM, SEM, ANY],
        out_specs=[HBM] * n,
        out_shape=[pltpu.HBM(b.shape, b.dtype) for b in bufs],
        input_output_aliases={i: i for i in range(n)},
        compiler_params=pltpu.CompilerParams(has_side_effects=EFFECT),
    )(*bufs, send_sems, recv_sems, after)


def _relay_copies(name, bufs, send_sems, recv_sems, after, landed, n_sem, onward):
    n = len(bufs)

    def body(*refs):
        for cp in landed(refs[:n], refs[n], refs[n + 1]):
            cp.wait_send()
            cp.wait_recv()
        for cp in onward(refs[:n], refs[n + 3], refs[n + 4]):
            cp.start()
        refs[-1][...] = jnp.zeros_like(refs[-1])

    outs = pl.pallas_call(
        body,
        name=name,
        in_specs=[HBM] * n + [SEM, SEM, ANY],
        out_specs=[SEM, SEM] + [HBM] * n + [pl.BlockSpec(memory_space=pltpu.VMEM)],
        out_shape=[pltpu.SemaphoreType.DMA((n_sem,)), pltpu.SemaphoreType.DMA((n_sem,))] + [pltpu.HBM(b.shape, b.dtype) for b in bufs]
        + [jax.ShapeDtypeStruct((SUBLANES, LANES), F32)],
        input_output_aliases={i: 2 + i for i in range(n)},
        compiler_params=pltpu.CompilerParams(has_side_effects=EFFECT),
    )(*bufs, send_sems, recv_sems, after)
    return outs[0], outs[1], list(outs[2:-1]), outs[-1]


def _gather_half_copies(first, count, over_ici):
    def copies(refs, send_sems, recv_sems):
        x, y, c, chips = _place()
        out = []
        for i in range(count):
            h = refs[i].shape[1] // 2
            mine = pl.ds(c * h, h)
            for j, (px, py) in enumerate(chips):
                k = 3 * (first + i) + j
                slot = 2 * x + y if over_ici else 2 * px + py
                to = (px, py, c) if over_ici else (x, y, 1 - c)
                out.append(_rcopy(refs[i].at[slot, mine], refs[i].at[slot, mine], send_sems.at[k], recv_sems.at[k], to))
        return out

    return copies


def _gather_copies(first, count):
    def copies(refs, send_sems, recv_sems):
        x, y, c, chips = _place()
        me = 2 * x + y
        out = []
        for i in range(count):
            for j, (px, py) in enumerate(chips):
                k = 3 * (first + i) + j
                out.append(_rcopy(refs[i].at[me], refs[i].at[me], send_sems.at[k], recv_sems.at[k], (px, py, c)))
        return out

    return copies


def _swap_copies(n):
    def copies(refs, send_sems, recv_sems):
        x, y, c, _ = _place()
        out = []
        for i in range(n):
            h = refs[i].shape[1] // 2
            out.append(_rcopy(refs[i].at[:, pl.ds((1 - c) * h, h)], refs[n + i], send_sems.at[i], recv_sems.at[i], (x, y, 1 - c)))
        return out

    return copies


def _partial_copies(n):
    def copies(refs, send_sems, recv_sems):
        x, y, c, chips = _place()
        out = []
        for i in range(n):
            for j, (px, py) in enumerate(chips):
                out.append(_rcopy(refs[i].at[2 * px + py], refs[n + i].at[j], send_sems.at[3 * i + j], recv_sems.at[3 * i + j], (px, py, c)))
        return out

    return copies


def _share_copies(n):
    def copies(refs, send_sems, recv_sems):
        x, y, c, _ = _place()
        return [_rcopy(refs[i].at[c], refs[i].at[c], send_sems.at[i], recv_sems.at[i], (x, y, 1 - c)) for i in range(n)]

    return copies


def _allreduce_small(vec):
    rows = vec.shape[0]

    def body(v_ref, o_ref, buf, send_sems, recv_sems):
        x, y, c, _ = _place()
        me = 4 * x + 2 * y + c
        buf[me] = v_ref[...]
        copies = []
        for k in range(1, N_DEV):
            peer = (x ^ (k >> 2), y ^ ((k >> 1) & 1), c ^ (k & 1))
            cp = _rcopy(v_ref, buf.at[me], send_sems.at[k - 1], recv_sems.at[k - 1], peer)
            cp.start()
            copies.append(cp)
        for cp in copies:
            cp.wait()
        acc = buf[0]
        for d in range(1, N_DEV):
            acc = acc + buf[d]
        o_ref[...] = acc

    return pl.pallas_call(
        body,
        name="allreduce_small",
        in_specs=[pl.BlockSpec(memory_space=pltpu.VMEM)],
        out_specs=pl.BlockSpec(memory_space=pltpu.VMEM),
        out_shape=jax.ShapeDtypeStruct(vec.shape, F32),
        scratch_shapes=[pltpu.VMEM((N_DEV, rows, LANES), F32), pltpu.SemaphoreType.DMA((N_DEV - 1,)), pltpu.SemaphoreType.DMA((N_DEV - 1,))],
        compiler_params=pltpu.CompilerParams(vmem_limit_bytes=VMEM_LIMIT_BYTES),
    )(vec)


class _Reduction:
    def __init__(self, tag, grads, place, core):
        self.tag, self.n, self.place, self.core = tag, len(grads), place, core
        lands = [lax.empty((N_CHIPS, g.shape[1] // 2, g.shape[2]), F32) for g in grads]
        self._start("swap", list(grads) + lands, self.n, _swap_copies(self.n))

    def _start(self, stage, bufs, n_sem, copies):
        *self.flight, self.token = _start_copies(f"rs_{stage}_start_{self.tag}", bufs, n_sem, copies)

    def _landed(self, stage, after, copies):
        send_sems, recv_sems, bufs = self.flight
        return _wait_copies(f"rs_{stage}_wait_{self.tag}", bufs, send_sems, recv_sems, after, copies)

    def to_chips(self, after):
        n = self.n
        bufs = self._landed("swap", after, _swap_copies(n))
        sums = [_add_core_halves(g, o, self.core, name=f"rs_add_core_{self.tag}_{i}") for i, (g, o) in enumerate(zip(bufs[:n], bufs[n:]))]
        self.mine = [f for f, _ in sums]
        parts = [b for _, b in sums]
        lands = [lax.empty((3,) + p.shape[1:], BF16) for p in parts]
        self._start("send", parts + lands, 3 * n, _partial_copies(n))

    def to_core(self, after):
        n = self.n
        bufs = self._landed("send", after, _partial_copies(n))
        halves = [_add_chip_partials(f, o, self.place, name=f"rs_add_chip_{self.tag}_{i}") for i, (f, o) in enumerate(zip(self.mine, bufs[n:]))]
        self._start("share", halves, n, _share_copies(n))

    def finish(self, after):
        return [b.reshape((-1,) + b.shape[2:]) for b in self._landed("share", after, _share_copies(self.n))]


ELEM_ROWS = (256, 176, 128, 64, 32, 16, 8)


def _add_core_halves(grad, got, c, name):
    s, r, cols = grad.shape
    h = r // 2
    tr = _pick(h, ELEM_ROWS)

    def body(c_ref, g_ref, o_ref, f_ref, b_ref):
        acc = g_ref[...] + o_ref[...]
        f_ref[...] = acc
        b_ref[...] = acc.astype(BF16)

    blk = pl.BlockSpec((None, tr, cols), lambda k, i, c_ref: (k, i, 0))
    return pl.pallas_call(
        body,
        name=name,
        grid_spec=pltpu.PrefetchScalarGridSpec(
            num_scalar_prefetch=1,
            grid=(s, h // tr),
            in_specs=[pl.BlockSpec((None, None, tr, cols), lambda k, i, c_ref: (k, c_ref[0], i, 0)), blk],
            out_specs=[blk, blk],
        ),
        out_shape=[jax.ShapeDtypeStruct((s, h, cols), F32), jax.ShapeDtypeStruct((s, h, cols), BF16)],
        compiler_params=_cparams(("parallel", "parallel")),
    )(c, grad.reshape(s, 2, h, cols), got)


def _add_chip_partials(mine, got, place, name):
    _, h, cols = mine.shape
    tr = _pick(h, ELEM_ROWS)

    def body(place_ref, m_ref, g_ref, o_ref):
        acc = m_ref[...]
        for j in range(3):
            acc = acc + g_ref[j].astype(F32)
        o_ref[...] = acc

    return pl.pallas_call(
        body,
        name=name,
        grid_spec=pltpu.PrefetchScalarGridSpec(
            num_scalar_prefetch=1,
            grid=(h // tr,),
            in_specs=[
                pl.BlockSpec((None, tr, cols), lambda i, place_ref: (place_ref[0], i, 0)),
                pl.BlockSpec((3, tr, cols), lambda i, place_ref: (0, i, 0)),
            ],
            out_specs=pl.BlockSpec((None, tr, cols), lambda i, place_ref: (place_ref[1], i, 0)),
        ),
        out_shape=jax.ShapeDtypeStruct((2, h, cols), F32),
        compiler_params=_cparams(("parallel",)),
    )(place, mine, got)


def _adamw_math(w, m, v, g):
    nm = ADAM_B1 * m + (1.0 - ADAM_B1) * g
    nv = ADAM_B2 * v + (1.0 - ADAM_B2) * (g * g)
    m_hat = nm * (1.0 / (1.0 - ADAM_B1 ** ADAM_STEP))
    v_hat = nv * (1.0 / (1.0 - ADAM_B2 ** ADAM_STEP))
    return -ADAM_LR * (m_hat / (jnp.sqrt(v_hat) + ADAM_EPS) + ADAM_WD * w), nm, nv


def _adamw_layer(w, m, v, g, layer, prev, name):
    nl, r, cols = w.shape
    tr = _pick(r, ELEM_ROWS)

    def body(w_ref, m_ref, v_ref, g_ref, *rest):
        go_ref, d_ref, nm_ref, nv_ref = rest[-4:]
        gv = g_ref[...]
        d_ref[...], nm_ref[...], nv_ref[...] = _adamw_math(w_ref[...], m_ref[...], v_ref[...], gv)
        go_ref[...] = gv

    lay = pl.BlockSpec((None, tr, cols), lambda i: (layer, i, 0))
    return pl.pallas_call(
        body,
        name=name,
        grid=(r // tr,),
        in_specs=[lay] * 3 + [pl.BlockSpec((tr, cols), lambda i: (i, 0))] + ([ANY] * 4 if prev else []),
        out_specs=[lay] * 4,
        out_shape=[jax.ShapeDtypeStruct((nl, r, cols), F32)] * 4,
        input_output_aliases={4 + k: k for k in range(4)} if prev else {},
        compiler_params=_cparams(("parallel",)),
    )(w, m, v, g, *(prev or ()))


def _adamw(w, m, v, g, name):
    r, cols = w.shape
    tr = _pick(r, ELEM_ROWS)

    def body(w_ref, m_ref, v_ref, g_ref, d_ref, nm_ref, nv_ref):
        d_ref[...], nm_ref[...], nv_ref[...] = _adamw_math(w_ref[...], m_ref[...], v_ref[...], g_ref[...])

    blk = pl.BlockSpec((tr, cols), lambda i: (i, 0))
    return pl.pallas_call(
        body,
        name=name,
        grid=(r // tr,),
        in_specs=[blk] * 4,
        out_specs=[blk] * 3,
        out_shape=[jax.ShapeDtypeStruct((r, cols), F32)] * 3,
        compiler_params=_cparams(("parallel",)),
    )(w, m, v, g)


SMALL_COLS = 384
SMALL_ROWS = 16


def _pad_rows(flat, rows, cols):
    return jnp.pad(flat, (0, rows * cols - flat.shape[0])).reshape(rows, cols)


def kernel(x, hg_norm, hg_w_in, hg_lb_logits, hg_out_norm, hg_w_out, kv_norm, w_kv, attn_norm, attn_w_q, attn_sinks, attn_w_o, ffn_norm, ffn_w_up, ffn_conv_w, ffn_conv_b, ffn_w_down, final_norm, loss_target, m_hg_norm, m_hg_w_in, m_hg_lb_logits, m_hg_out_norm, m_hg_w_out, m_kv_norm, m_w_kv, m_attn_norm, m_attn_w_q, m_attn_sinks, m_attn_w_o, m_ffn_norm, m_ffn_w_up, m_ffn_conv_w, m_ffn_conv_b, m_ffn_w_down, m_final_norm, v_hg_norm, v_hg_w_in, v_hg_lb_logits, v_hg_out_norm, v_hg_w_out, v_kv_norm, v_w_kv, v_attn_norm, v_attn_w_q, v_attn_sinks, v_attn_w_o, v_ffn_norm, v_ffn_w_up, v_ffn_conv_w, v_ffn_conv_b, v_ffn_w_down, v_final_norm):
    wts = dict(hg_norm=hg_norm, hg_w_in=hg_w_in, hg_lb_logits=hg_lb_logits, hg_out_norm=hg_out_norm, hg_w_out=hg_w_out, kv_norm=kv_norm, w_kv=w_kv, attn_norm=attn_norm, attn_w_q=attn_w_q, attn_sinks=attn_sinks, attn_w_o=attn_w_o, ffn_norm=ffn_norm, ffn_w_up=ffn_w_up, ffn_conv_w=ffn_conv_w, ffn_conv_b=ffn_conv_b, ffn_w_down=ffn_w_down, final_norm=final_norm)
    mom1 = dict(hg_norm=m_hg_norm, hg_w_in=m_hg_w_in, hg_lb_logits=m_hg_lb_logits, hg_out_norm=m_hg_out_norm, hg_w_out=m_hg_w_out, kv_norm=m_kv_norm, w_kv=m_w_kv, attn_norm=m_attn_norm, attn_w_q=m_attn_w_q, attn_sinks=m_attn_sinks, attn_w_o=m_attn_w_o, ffn_norm=m_ffn_norm, ffn_w_up=m_ffn_w_up, ffn_conv_w=m_ffn_conv_w, ffn_conv_b=m_ffn_conv_b, ffn_w_down=m_ffn_w_down, final_norm=m_final_norm)
    mom2 = dict(hg_norm=v_hg_norm, hg_w_in=v_hg_w_in, hg_lb_logits=v_hg_lb_logits, hg_out_norm=v_hg_out_norm, hg_w_out=v_hg_w_out, kv_norm=v_kv_norm, w_kv=v_w_kv, attn_norm=v_attn_norm, attn_w_q=v_attn_w_q, attn_sinks=v_attn_sinks, attn_w_o=v_attn_w_o, ffn_norm=v_ffn_norm, ffn_w_up=v_ffn_w_up, ffn_conv_w=v_ffn_conv_w, ffn_conv_b=v_ffn_conv_b, ffn_w_down=v_ffn_w_down, final_norm=v_final_norm)
    names = list(wts)
    chip = 2 * lax.axis_index("x") + lax.axis_index("y")
    core = lax.axis_index("c")
    core_arr = jnp.reshape(core, (1,)).astype(jnp.int32)
    fs = D_FF // N_CHIPS
    ds = D_MODEL // N_CHIPS

    place_arr = jnp.stack([chip, core]).astype(jnp.int32)
    small = jnp.concatenate([hg_norm.reshape(-1), hg_lb_logits.reshape(-1), ffn_conv_w.reshape(-1)])
    n_small = small.shape[0]
    shards = [
        ("small", _pad_rows(small, SMALL_ROWS, SMALL_COLS), F32), ("hg_w_in", hg_w_in[0], BF16),
        ("hg_w_out", hg_w_out[0], BF16), ("ffn_w_up0", ffn_w_up[0], BF16), ("ffn_w_down0", ffn_w_down[0], BF16),
        ("w_kv", w_kv, BF16), ("attn_w_q", attn_w_q[0], BF16), ("attn_w_o", attn_w_o[0], BF16),
        ("ffn_w_up1", ffn_w_up[1], BF16), ("ffn_w_down1", ffn_w_down[1], BF16),
    ]
    n_first = 2
    spans = dict(layer0=(0, 3), layer1=(3, 8))
    placed = [_place_shard(s, place_arr, dt, name=f"place_{nm}") for nm, s, dt in shards[:n_first]]
    first = _start_copies("gather_start_first", placed, 3 * n_first, _gather_copies(0, n_first))
    placed = [_place_shard(s, place_arr, dt, name=f"place_{nm}", deps=(first[3],)) for nm, s, dt in shards[n_first:]]
    rest = _start_copies("gather_start_rest", placed, 3 * len(placed), _gather_half_copies(0, len(placed), True))
    relayed = {}

    def fetch(w, stage, after):
        if stage == "first":
            got = _wait_copies("gather_wait_first", first[2], first[0], first[1], after, _gather_copies(0, n_first))
        elif stage.endswith("_relay"):
            lo, hi = spans[stage[:-6]]
            relayed[stage[:-6]] = _relay_copies(
                f"gather_{stage}", rest[2][lo:hi], rest[0], rest[1], after,
                _gather_half_copies(lo, hi - lo, True), 3 * (hi - lo), _gather_half_copies(0, hi - lo, False))
            return w
        else:
            lo, hi = spans[stage]
            send_sems, recv_sems, bufs, _ = relayed[stage]
            got = _wait_copies(f"gather_wait_{stage}", bufs, send_sems, recv_sems, after, _gather_half_copies(0, hi - lo, False))
        w = dict(w)
        if stage == "first":
            g_small = got[0].reshape(N_CHIPS, -1)[:, :n_small]
            conv_w = g_small[:, 3 * ds:].reshape(N_CHIPS, 2, 3, fs).transpose(1, 2, 0, 3).reshape(2, 3, D_FF)
            w.update(
                hg_norm=g_small[:, :ds].reshape(1, D_MODEL),
                hg_lb=g_small[:, ds:3 * ds].reshape(N_CHIPS, 2, ds).transpose(1, 0, 2).reshape(2, D_MODEL),
                ffn_conv_w=[conv_w[0], conv_w[1]], hg_w_in=got[1],
            )
        elif stage == "layer0":
            w.update(hg_w_out=got[0].reshape(1, D_MODEL, D_MODEL), ffn_w_up=[got[1], None], ffn_w_down=[got[2].reshape(1, D_FF, D_MODEL), None])
        else:
            w.update(
                w_kv=got[0].reshape(1, D_MODEL, 2 * LANES), attn_w_q=got[1].reshape(1, D_MODEL, D_MODEL),
                attn_w_o=got[2].reshape(1, D_MODEL, D_MODEL), ffn_w_up=[w["ffn_w_up"][0], got[3]],
                ffn_w_down=[w["ffn_w_down"][0], got[4].reshape(1, D_FF, D_MODEL)],
            )
        return w

    whole = dict(
        hg_out_norm=hg_out_norm, kv_norm=kv_norm.reshape(1, D_MODEL), attn_norm=attn_norm, attn_sinks=attn_sinks.reshape(ATT_QH),
        ffn_norm=[ffn_norm[0:1], ffn_norm[1:2]], ffn_conv_b=[ffn_conv_b[0:1], ffn_conv_b[1:2]], final_norm=final_norm.reshape(1, D_MODEL),
    )
    whole = fetch(whole, "first", rest[3])

    red, layer1 = {}, {}

    def by_rows(g, rows):
        return g.reshape(N_CHIPS, rows, g.shape[2])

    def hook(point, dh, grads):
        if point == "ffn1":
            red["ffn1"] = _Reduction("ffn1", [by_rows(grads["ffn_w_down"], fs), grads["ffn_w_up"]], place_arr, core_arr)
            return (red["ffn1"].token,)
        if point == "attn":
            red["ffn1"].to_chips(dh)
            layer1.update(grads)
            return (red["ffn1"].token,)
        if point == "ffn0":
            group = [by_rows(layer1["attn_w_o"], ds), by_rows(layer1["attn_w_q"], ds), by_rows(layer1["w_kv"], ds),
                     by_rows(grads["ffn_w_down"], fs), grads["ffn_w_up"]]
            red["mid"] = _Reduction("mid", group, place_arr, core_arr)
            return (red["mid"].token,)
        if point == "hgrn":
            red["ffn1"].to_core(dh)
            red["mid"].to_chips(dh)
            return (red["ffn1"].token, red["mid"].token)
        red["hg"] = _Reduction("hg", [by_rows(grads["hg_w_out"], ds), grads["hg_w_in"]], place_arr, core_arr)
        return (red["hg"].token,)

    loss, dx, grads = _local_step(x[0], loss_target[0], whole, fetch, hook)

    small_parts = [
        loss.reshape(-1), grads["hg_out_norm"].reshape(-1), grads["attn_sinks"].reshape(-1), grads["kv_norm"].reshape(-1),
        grads["attn_norm"].reshape(-1), grads["ffn_norm"][0].reshape(-1), grads["ffn_norm"][1].reshape(-1),
        grads["ffn_conv_b"][0].reshape(-1), grads["ffn_conv_b"][1].reshape(-1), grads["final_norm"].reshape(-1),
        grads["hg_norm"].reshape(-1), grads["hg_lb"].reshape(-1), grads["ffn_conv_w"][0].reshape(-1), grads["ffn_conv_w"][1].reshape(-1),
    ]
    sizes = [p.shape[0] for p in small_parts]
    flat = jnp.concatenate(small_parts)
    rows = -(-flat.shape[0] // (SUBLANES * LANES)) * SUBLANES
    summed = _allreduce_small(_pad_rows(flat, rows, LANES)).reshape(-1)
    red["hg"].to_chips(summed)
    offs = [0]
    for sz in sizes:
        offs.append(offs[-1] + sz)
    sm = [summed[offs[i]:offs[i + 1]] for i in range(len(sizes))]
    loss_out = sm[0][0]
    conv_w_full = jnp.stack([sm[12].reshape(3, D_FF), sm[13].reshape(3, D_FF)])
    small_grads = dict(
        hg_out_norm=sm[1].reshape(1, HG_DK), attn_sinks=sm[2][:ATT_QH].reshape(1, ATT_QH), kv_norm=sm[3], attn_norm=sm[4].reshape(1, D_MODEL),
        ffn_norm=jnp.stack([sm[5], sm[6]]), ffn_conv_b=jnp.stack([sm[7], sm[8]]), final_norm=sm[9],
        hg_norm=lax.dynamic_slice(sm[10].reshape(1, D_MODEL), (0, chip * ds), (1, ds)),
        hg_lb_logits=lax.dynamic_slice(sm[11].reshape(2, D_MODEL), (0, chip * ds), (2, ds)),
        ffn_conv_w=lax.dynamic_slice(conv_w_full, (0, 0, chip * fs), (2, 3, fs)),
    )

    out_g, out_d, out_m, out_v = {}, {}, {}, {}

    def update(name, g2):
        shape = wts[name].shape
        d2, m2, v2 = _adamw(wts[name].reshape(g2.shape), mom1[name].reshape(g2.shape), mom2[name].reshape(g2.shape), g2, name=f"adamw_{name}")
        out_g[name], out_d[name], out_m[name], out_v[name] = g2.reshape(shape), d2.reshape(shape), m2.reshape(shape), v2.reshape(shape)
        return d2

    def update_layer(name, g2, layer, prev):
        res = _adamw_layer(wts[name], mom1[name], mom2[name], g2, layer, prev, name=f"adamw_{name}{layer}")
        out_g[name], out_d[name], out_m[name], out_v[name] = res
        return res

    g_down1, g_up1 = red["ffn1"].finish(red["hg"].token)
    down1 = update_layer("ffn_w_down", g_down1, 1, None)
    up1 = update_layer("ffn_w_up", g_up1, 1, None)
    red["mid"].to_core(up1[1])
    g_o, g_q, g_kv, g_down0, g_up0 = red["mid"].finish(up1[2])
    update("attn_w_o", g_o)
    update("attn_w_q", g_q)
    update("w_kv", g_kv)
    update_layer("ffn_w_down", g_down0, 0, down1)
    last = update_layer("ffn_w_up", g_up0, 0, up1)
    red["hg"].to_core(last[1])
    g_out, g_in = red["hg"].finish(last[2])
    update("hg_w_out", g_out)
    update("hg_w_in", g_in)

    small_names = [n for n in names if n not in out_g]
    cat = lambda d: jnp.concatenate([d[n].reshape(-1) for n in small_names])
    n_flat = sum(wts[n].size for n in small_names)
    srows = -(-n_flat // (SUBLANES * LANES)) * SUBLANES
    packed = [_pad_rows(cat(d), srows, LANES) for d in (wts, mom1, mom2, small_grads)]
    d_s, m_s, v_s = _adamw(*packed, name="adamw_small")
    off = 0
    for n in small_names:
        sz, shape = wts[n].size, wts[n].shape
        out_g[n] = small_grads[n].reshape(shape)
        out_d[n] = d_s.reshape(-1)[off:off + sz].reshape(shape)
        out_m[n] = m_s.reshape(-1)[off:off + sz].reshape(shape)
        out_v[n] = v_s.reshape(-1)[off:off + sz].reshape(shape)
        off += sz

    grad_x = dx.reshape(x.shape)
    return (loss_out, grad_x, *[out_g[n] for n in names], *[out_d[n] for n in names], *[out_m[n] for n in names], *[out_v[n] for n in names])
```

```python
import functools

import jax
import jax.numpy as jnp
from jax import lax
from jax.experimental import pallas as pl
from jax.experimental.pallas import tpu as pltpu

F32 = jnp.float32
BF16 = jnp.bfloat16
MESH = pl.DeviceIdType.MESH

EPS = 1e-6
D_MODEL = 1024
HG_HEADS = 8
HG_DK = 128
HG_CHUNK = 64
ATT_HD = 64
ATT_QH = 16
ATT_KVH = 2
ATT_GROUP = ATT_QH // ATT_KVH
WINDOW = 128
D_FF = 2816
N_CHIPS = 4
N_DEV = 8
LANES = 128
SUBLANES = 8
VMEM_LIMIT_BYTES = 56 * 1024 * 1024
NEG = -1e30
ALIBI_SLOPES = tuple(2.0 ** (-8.0 * h / ATT_QH) for h in range(1, ATT_QH + 1))

ADAM_LR = 0.001
ADAM_B1 = 0.9
ADAM_B2 = 0.999
ADAM_EPS = 1e-08
ADAM_WD = 0.01
ADAM_STEP = 10


def _cparams(sem=None):
    return pltpu.CompilerParams(dimension_semantics=sem, vmem_limit_bytes=VMEM_LIMIT_BYTES)


def _pick(n, cands):
    for c in cands:
        if n % c == 0:
            return c
    return n


def _sigmoid(x):
    return 1.0 / (1.0 + jnp.exp(-x))


def _dot(a, b, dims):
    return lax.dot_general(a, b, (dims, ((), ())), preferred_element_type=F32)


NN = ((1,), (0,))
NT = ((1,), (1,))
TN = ((0,), (0,))


MM_ROWS = 1024


def _mm_nn(a, w, res=None, out_dtype=F32, name="mm_nn"):
    m, k = a.shape
    s, _, ns = w.shape
    tm = min(m, MM_ROWS)
    tn = _pick(ns, (512, 1408, 256, 128))
    npb = ns // tn

    def body(a_ref, w_ref, *rest):
        o_ref = rest[-1]
        acc = _dot(a_ref[...].astype(BF16), w_ref[...], NN)
        if res is not None:
            acc = acc + rest[0][...]
        o_ref[...] = acc.astype(o_ref.dtype)

    in_specs = [
        pl.BlockSpec((tm, k), lambda i, j: (i, 0)),
        pl.BlockSpec((None, k, tn), lambda i, j: (j // npb, 0, j % npb)),
    ]
    args = [a, w]
    if res is not None:
        in_specs.append(pl.BlockSpec((tm, tn), lambda i, j: (i, j)))
        args.append(res)
    return pl.pallas_call(
        body,
        name=name,
        grid=(m // tm, s * npb),
        in_specs=in_specs,
        out_specs=pl.BlockSpec((tm, tn), lambda i, j: (i, j)),
        out_shape=jax.ShapeDtypeStruct((m, s * ns), out_dtype),
        compiler_params=_cparams(("parallel", "parallel")),
    )(*args)


def _dy_spec(stacked, tm, tn, npb, row, kk):
    if stacked:
        return pl.BlockSpec((None, tm, tn), lambda *g: (kk(g) // npb, row(g), kk(g) % npb))
    return pl.BlockSpec((tm, tn), lambda *g: (row(g), kk(g)))


def _dep_specs(deps):
    return [pl.BlockSpec(d.shape, lambda *g: (0, 0)) for d in deps]


def _mm_nt(dy, w, stacked=False, out_dtype=F32, name="mm_nt", deps=()):
    s, k, ns = w.shape
    m = dy.shape[1] if stacked else dy.shape[0]
    tm = min(m, MM_ROWS)
    tko = _pick(k, (1024, 1408, 512, 256))
    tn = _pick(ns, (1024, 1408, 512, 256))
    npb = ns // tn
    nk = s * npb

    def body(dy_ref, w_ref, *rest):
        o_ref, acc_ref = rest[-2:]
        kk = pl.program_id(2)

        @pl.when(kk == 0)
        def _():
            acc_ref[...] = jnp.zeros_like(acc_ref)

        acc_ref[...] += _dot(dy_ref[...].astype(BF16), w_ref[...], NT)

        @pl.when(kk == nk - 1)
        def _():
            o_ref[...] = acc_ref[...].astype(o_ref.dtype)

    return pl.pallas_call(
        body,
        name=name,
        grid=(m // tm, k // tko, nk),
        in_specs=[
            _dy_spec(stacked, tm, tn, npb, lambda g: g[0], lambda g: g[2]),
            pl.BlockSpec((None, tko, tn), lambda i, j, kk: (kk // npb, j, kk % npb)),
        ] + _dep_specs(deps),
        out_specs=pl.BlockSpec((tm, tko), lambda i, j, kk: (i, j)),
        out_shape=jax.ShapeDtypeStruct((m, k), out_dtype),
        scratch_shapes=[pltpu.VMEM((tm, tko), F32)],
        compiler_params=_cparams(("parallel", "parallel", "arbitrary")),
    )(dy, w, *deps)


def _mm_tn(a, dy, s, ns, stacked=False, name="mm_tn", deps=()):
    m, k = a.shape
    tm = min(m, MM_ROWS)
    tk = _pick(k, (1024, 1408, 512, 256))
    tn = _pick(ns, (512, 1408, 256, 128))
    npb = ns // tn
    nm = m // tm

    def body(a_ref, dy_ref, *rest):
        o_ref, acc_ref = rest[-2:]
        mm = pl.program_id(2)

        @pl.when(mm == 0)
        def _():
            acc_ref[...] = jnp.zeros_like(acc_ref)

        acc_ref[...] += _dot(a_ref[...].astype(BF16), dy_ref[...].astype(BF16), TN)

        @pl.when(mm == nm - 1)
        def _():
            o_ref[...] = acc_ref[...]

    return pl.pallas_call(
        body,
        name=name,
        grid=(k // tk, s * npb, nm),
        in_specs=[
            pl.BlockSpec((tm, tk), lambda i, j, mm: (mm, i)),
            _dy_spec(stacked, tm, tn, npb, lambda g: g[2], lambda g: g[1]),
        ] + _dep_specs(deps),
        out_specs=pl.BlockSpec((None, tk, tn), lambda i, j, mm: (j // npb, i, j % npb)),
        out_shape=jax.ShapeDtypeStruct((s, k, ns), F32),
        scratch_shapes=[pltpu.VMEM((tk, tn), F32)],
        compiler_params=_cparams(("parallel", "parallel", "arbitrary")),
    )(a, dy, *deps)


ROW_TILE = 512


def _rms_fwd(x, g, name="rms_fwd"):
    t, d = x.shape
    r = min(t, ROW_TILE)

    def body(x_ref, g_ref, o_ref):
        xv = x_ref[...]
        rstd = lax.rsqrt(jnp.mean(xv * xv, axis=-1, keepdims=True) + EPS)
        o_ref[...] = (xv * rstd * g_ref[...]).astype(BF16)

    return pl.pallas_call(
        body,
        name=name,
        grid=(t // r,),
        in_specs=[pl.BlockSpec((r, d), lambda i: (i, 0)), pl.BlockSpec((1, d), lambda i: (0, 0))],
        out_specs=pl.BlockSpec((r, d), lambda i: (i, 0)),
        out_shape=jax.ShapeDtypeStruct((t, d), BF16),
        compiler_params=_cparams(("parallel",)),
    )(x, g)


def _rms_bwd(x, g, dxn, dres, name="rms_bwd"):
    t, d = x.shape
    r = min(t, ROW_TILE)

    def body(x_ref, g_ref, dxn_ref, dres_ref, dx_ref, dg_ref):
        @pl.when(pl.program_id(0) == 0)
        def _():
            dg_ref[...] = jnp.zeros_like(dg_ref)

        xv = x_ref[...]
        rstd = lax.rsqrt(jnp.mean(xv * xv, axis=-1, keepdims=True) + EPS)
        xhat = xv * rstd
        dxn_v = dxn_ref[...].astype(F32)
        gd = dxn_v * g_ref[...]
        dx_ref[...] = dres_ref[...] + rstd * (gd - xhat * jnp.mean(gd * xhat, axis=-1, keepdims=True))
        dg_ref[...] += jnp.sum(dxn_v * xhat, axis=0, keepdims=True)

    return pl.pallas_call(
        body,
        name=name,
        grid=(t // r,),
        in_specs=[
            pl.BlockSpec((r, d), lambda i: (i, 0)),
            pl.BlockSpec((1, d), lambda i: (0, 0)),
            pl.BlockSpec((r, d), lambda i: (i, 0)),
            pl.BlockSpec((r, d), lambda i: (i, 0)),
        ],
        out_specs=[pl.BlockSpec((r, d), lambda i: (i, 0)), pl.BlockSpec((1, d), lambda i: (0, 0))],
        out_shape=[jax.ShapeDtypeStruct((t, d), F32), jax.ShapeDtypeStruct((1, d), F32)],
        compiler_params=_cparams(("arbitrary",)),
    )(x, g, dxn, dres)


def _loss_head(h, g, target):
    t, d = h.shape
    r = min(t, ROW_TILE)

    def body(h_ref, g_ref, t_ref, dh_ref, dg_ref, loss_ref):
        @pl.when(pl.program_id(0) == 0)
        def _():
            dg_ref[...] = jnp.zeros_like(dg_ref)
            loss_ref[...] = jnp.zeros_like(loss_ref)

        xv = h_ref[...]
        rstd = lax.rsqrt(jnp.mean(xv * xv, axis=-1, keepdims=True) + EPS)
        xhat = xv * rstd
        gv = g_ref[...]
        err = xhat * gv - t_ref[...]
        loss_ref[...] += 0.5 * jnp.sum(jnp.mean(err * err, axis=-1, keepdims=True), axis=0, keepdims=True)
        dy = err * (1.0 / d)
        gd = dy * gv
        dh_ref[...] = rstd * (gd - xhat * jnp.mean(gd * xhat, axis=-1, keepdims=True))
        dg_ref[...] += jnp.sum(dy * xhat, axis=0, keepdims=True)

    return pl.pallas_call(
        body,
        name="loss_head",
        grid=(t // r,),
        in_specs=[
            pl.BlockSpec((r, d), lambda i: (i, 0)),
            pl.BlockSpec((1, d), lambda i: (0, 0)),
            pl.BlockSpec((r, d), lambda i: (i, 0)),
        ],
        out_specs=[
            pl.BlockSpec((r, d), lambda i: (i, 0)),
            pl.BlockSpec((1, d), lambda i: (0, 0)),
            pl.BlockSpec((1, LANES), lambda i: (0, 0)),
        ],
        out_shape=[
            jax.ShapeDtypeStruct((t, d), F32),
            jax.ShapeDtypeStruct((1, d), F32),
            jax.ShapeDtypeStruct((1, LANES), F32),
        ],
        compiler_params=_cparams(("arbitrary",)),
    )(h, g, target)


CONV_ROWS = 256
CONV_COLS = 1408


def _conv_taps(x_ext, n):
    tot = x_ext.shape[0]
    g1 = pltpu.roll(x_ext, 1, 0)[tot - n:]
    g2 = pltpu.roll(x_ext, 2, 0)[tot - n:]
    return g2, g1


def _conv_fwd(up, conv_w, conv_b, name="conv_fwd"):
    t = up.shape[0]
    r = min(t, CONV_ROWS)
    tc = CONV_COLS
    ncb = D_FF // tc
    hb = r // SUBLANES

    def body(g_ref, halo_ref, v_ref, w_ref, b_ref, o_ref):
        i = pl.program_id(1)
        g0 = g_ref[...]
        halo = halo_ref[...] * jnp.where(i > 0, 1.0, 0.0)
        g2, g1 = _conv_taps(jnp.concatenate([halo, g0], axis=0), r)
        c = b_ref[...] + w_ref[0:1, :] * g2 + w_ref[1:2, :] * g1 + w_ref[2:3, :] * g0
        o_ref[...] = (c * _sigmoid(c) * v_ref[...]).astype(BF16)

    return pl.pallas_call(
        body,
        name=name,
        grid=(ncb, t // r),
        in_specs=[
            pl.BlockSpec((r, tc), lambda j, i: (i, j)),
            pl.BlockSpec((SUBLANES, tc), lambda j, i: (jnp.maximum(i * hb - 1, 0), j)),
            pl.BlockSpec((r, tc), lambda j, i: (i, ncb + j)),
            pl.BlockSpec((3, tc), lambda j, i: (0, j)),
            pl.BlockSpec((1, tc), lambda j, i: (0, j)),
        ],
        out_specs=pl.BlockSpec((r, tc), lambda j, i: (i, j)),
        out_shape=jax.ShapeDtypeStruct((t, D_FF), BF16),
        compiler_params=_cparams(("parallel", "parallel")),
    )(up, up, up, conv_w, conv_b)


def _conv_bwd(up, conv_w, conv_b, dact, name="conv_bwd"):
    t = up.shape[0]
    r = min(t, CONV_ROWS)
    tc = CONV_COLS
    ncb = D_FF // tc
    hb = r // SUBLANES
    nrt = t // r

    def body(g_ref, halo_ref, v_ref, w_ref, b_ref, da_ref, dup_ref, dw_ref, db_ref, nxt_ref):
        ii = pl.program_id(1)
        i = nrt - 1 - ii

        @pl.when(ii == 0)
        def _():
            nxt_ref[...] = jnp.zeros_like(nxt_ref)
            dw_ref[...] = jnp.zeros_like(dw_ref)
            db_ref[...] = jnp.zeros_like(db_ref)

        g0 = g_ref[...]
        halo = halo_ref[...] * jnp.where(i > 0, 1.0, 0.0)
        g2, g1 = _conv_taps(jnp.concatenate([halo, g0], axis=0), r)
        w0, w1, w2 = w_ref[0:1, :], w_ref[1:2, :], w_ref[2:3, :]
        c = b_ref[...] + w0 * g2 + w1 * g1 + w2 * g0
        sg = _sigmoid(c)
        da = da_ref[...]
        dval = da * (c * sg)
        dc = da * v_ref[...] * (sg * (1.0 + c * (1.0 - sg)))
        db_ref[...] += jnp.sum(dc, axis=0, keepdims=True)
        dw_ref[0:1, :] += jnp.sum(dc * g2, axis=0, keepdims=True)
        dw_ref[1:2, :] += jnp.sum(dc * g1, axis=0, keepdims=True)
        dw_ref[2:3, :] += jnp.sum(dc * g0, axis=0, keepdims=True)
        ext = jnp.concatenate([dc, nxt_ref[...]], axis=0)
        tot = r + SUBLANES
        d1 = pltpu.roll(ext, tot - 1, 0)[:r]
        d2 = pltpu.roll(ext, tot - 2, 0)[:r]
        dgate = w2 * dc + w1 * d1 + w0 * d2
        nxt_ref[...] = dc[:SUBLANES]
        dup_ref[0] = dgate.astype(BF16)
        dup_ref[1] = dval.astype(BF16)

    rev = lambda ii: nrt - 1 - ii
    dup, dw, db = pl.pallas_call(
        body,
        name=name,
        grid=(ncb, nrt),
        in_specs=[
            pl.BlockSpec((r, tc), lambda j, ii: (rev(ii), j)),
            pl.BlockSpec((SUBLANES, tc), lambda j, ii: (jnp.maximum(rev(ii) * hb - 1, 0), j)),
            pl.BlockSpec((r, tc), lambda j, ii: (rev(ii), ncb + j)),
            pl.BlockSpec((3, tc), lambda j, ii: (0, j)),
            pl.BlockSpec((1, tc), lambda j, ii: (0, j)),
            pl.BlockSpec((r, tc), lambda j, ii: (rev(ii), j)),
        ],
        out_specs=[
            pl.BlockSpec((2, None, r, tc), lambda j, ii: (0, j, rev(ii), 0)),
            pl.BlockSpec((3, tc), lambda j, ii: (0, j)),
            pl.BlockSpec((1, tc), lambda j, ii: (0, j)),
        ],
        out_shape=[
            jax.ShapeDtypeStruct((2, ncb, t, tc), BF16),
            jax.ShapeDtypeStruct((3, D_FF), F32),
            jax.ShapeDtypeStruct((1, D_FF), F32),
        ],
        scratch_shapes=[pltpu.VMEM((SUBLANES, tc), F32)],
        compiler_params=_cparams(("parallel", "arbitrary")),
    )(up, up, up, conv_w, conv_b, dact)
    return dup.reshape(2 * ncb, t, tc), dw, db


def _split3(x):
    x1 = x.astype(BF16)
    r1 = x - x1.astype(F32)
    x2 = r1.astype(BF16)
    x3 = (r1 - x2.astype(F32)).astype(BF16)
    return x1, x2, x3


def _tri_dot(tri, x, dims):
    x1, x2, x3 = _split3(x)
    return _dot(tri, x1, dims) + _dot(tri, x2, dims) + _dot(tri, x3, dims)


def _lower_bound(logits_ref):
    return _sigmoid(logits_ref[0:1, :] - logits_ref[1:2, :])


def _hg_gates(qr, fr, lb):
    q = qr * _sigmoid(qr) * (HG_DK ** -0.5)
    sf = _sigmoid(fr)
    fg = lb + (1.0 - lb) * sf
    return q, sf, fg


def _hg_chunk_terms(q, fg, tril_b, low_half):
    g = jnp.log(fg)
    k = 1.0 - fg
    cum = _tri_dot(tril_b, g, NN)
    c_last = jnp.sum(g, axis=0, keepdims=True)
    c_mid = jnp.sum(jnp.where(low_half, g, 0.0), axis=0, keepdims=True)
    e_q = jnp.exp(cum - c_mid)
    e_k = jnp.exp(c_mid - cum)
    e_0 = jnp.exp(cum)
    e_l = jnp.exp(c_last - cum)
    return k, e_q, e_k, e_0, e_l, jnp.exp(c_last)


HG_BLOCK = 256


def _hg_proj_specs(rb, row):
    return [pl.BlockSpec((rb, D_MODEL), functools.partial(lambda i, k: (row(i), k), k=k)) for k in range(4)]


def _hg_consts(c):
    tril = lax.broadcasted_iota(jnp.int32, (c, c), 0) >= lax.broadcasted_iota(jnp.int32, (c, c), 1)
    low_half = lax.broadcasted_iota(jnp.int32, (c, HG_DK), 0) < c // 2
    return tril, tril.astype(BF16), low_half


def _hgrn_fwd(proj, lb, wn):
    t = proj.shape[0]
    c = HG_CHUNK
    rb = min(t, HG_BLOCK)
    cpb = rb // c

    def body(q_ref, f_ref, i_ref, g_ref, lb_ref, wn_ref, o_ref, y_ref, st_ref, s_scr):
        @pl.when(pl.program_id(0) == 0)
        def _():
            s_scr[...] = jnp.zeros_like(s_scr)

        lb_all = _lower_bound(lb_ref)
        wnv = wn_ref[...]
        tril, tril_b, low_half = _hg_consts(c)

        def chunk(n, carry):
            rows = pl.ds(pl.multiple_of(n * c, c), c)
            for h in range(HG_HEADS):
                cols = slice(h * HG_DK, (h + 1) * HG_DK)
                q, _, fg = _hg_gates(q_ref[rows, cols], f_ref[rows, cols], lb_all[:, cols])
                v = i_ref[rows, cols].astype(BF16)
                k, e_q, e_k, e_0, e_l, e_last = _hg_chunk_terms(q, fg, tril_b, low_half)
                st = s_scr[h]
                st_ref[h, n] = st
                a = jnp.where(tril, _dot((q * e_q).astype(BF16), (k * e_k).astype(BF16), NT), 0.0)
                o = _dot((q * e_0).astype(BF16), st.astype(BF16), NT) + _dot(a.astype(BF16), v, NN)
                s_scr[h] = st * e_last + _dot(v, (k * e_l).astype(BF16), TN)
                o_ref[rows, cols] = o
                rstd = lax.rsqrt(jnp.mean(o * o, axis=-1, keepdims=True) + EPS)
                gr = g_ref[rows, cols]
                y_ref[rows, cols] = (o * rstd * wnv * (gr * _sigmoid(gr))).astype(BF16)
            return carry

        lax.fori_loop(0, cpb, chunk, 0)

    blk = pl.BlockSpec((rb, D_MODEL), lambda i: (i, 0))
    return pl.pallas_call(
        body,
        name="hgrn_fwd",
        grid=(t // rb,),
        in_specs=_hg_proj_specs(rb, lambda i: i) + [pl.BlockSpec((2, D_MODEL), lambda i: (0, 0)), pl.BlockSpec((1, HG_DK), lambda i: (0, 0))],
        out_specs=[blk, blk, pl.BlockSpec((HG_HEADS, cpb, HG_DK, HG_DK), lambda i: (0, i, 0, 0))],
        out_shape=[
            jax.ShapeDtypeStruct((t, D_MODEL), F32),
            jax.ShapeDtypeStruct((t, D_MODEL), BF16),
            jax.ShapeDtypeStruct((HG_HEADS, t // c, HG_DK, HG_DK), F32),
        ],
        scratch_shapes=[pltpu.VMEM((HG_HEADS, HG_DK, HG_DK), F32)],
        compiler_params=_cparams(("arbitrary",)),
    )(proj, proj, proj, proj, lb, wn)


def _hgrn_bwd(proj, lb, wn, o, states, dy):
    t = proj.shape[0]
    c = HG_CHUNK
    rb = min(t, HG_BLOCK)
    cpb = rb // c
    nb = t // rb

    def body(q_ref, f_ref, i_ref, g_ref, lb_ref, wn_ref, o_ref, st_ref, dy_ref, dp_ref, dl_ref, dwn_ref, ds_scr, dlb_scr):
        step = pl.program_id(0)

        @pl.when(step == 0)
        def _():
            dwn_ref[...] = jnp.zeros_like(dwn_ref)
            ds_scr[...] = jnp.zeros_like(ds_scr)
            dlb_scr[...] = jnp.zeros_like(dlb_scr)

        lb_all = _lower_bound(lb_ref)
        wnv = wn_ref[...]
        tril, tril_b, low_half = _hg_consts(c)

        def chunk(nn, carry):
            n = cpb - 1 - nn
            rows = pl.ds(pl.multiple_of(n * c, c), c)
            for h in range(HG_HEADS):
                cols = slice(h * HG_DK, (h + 1) * HG_DK)
                lbv = lb_all[:, cols]
                ov = o_ref[rows, cols]
                gr = g_ref[rows, cols]
                dyv = dy_ref[rows, cols].astype(F32)
                rstd = lax.rsqrt(jnp.mean(ov * ov, axis=-1, keepdims=True) + EPS)
                ohat = ov * rstd
                sg = _sigmoid(gr)
                dg_raw = dyv * (ohat * wnv) * (sg * (1.0 + gr * (1.0 - sg)))
                don = dyv * (gr * sg)
                dwn_ref[...] += jnp.sum(don * ohat, axis=0, keepdims=True)
                gd = don * wnv
                do = rstd * (gd - ohat * jnp.mean(gd * ohat, axis=-1, keepdims=True))
                do_b = do.astype(BF16)
                qr = q_ref[rows, cols]
                q, sf, fg = _hg_gates(qr, f_ref[rows, cols], lbv)
                v = i_ref[rows, cols].astype(BF16)
                k, e_q, e_k, e_0, e_l, e_last = _hg_chunk_terms(q, fg, tril_b, low_half)
                qi, qi_lo, _ = _split3(q * e_q)
                ki, ki_lo, _ = _split3(k * e_k)
                q0 = (q * e_0).astype(BF16)
                kl = (k * e_l).astype(BF16)
                st = st_ref[h, n]
                st_b = st.astype(BF16)
                ds = ds_scr[h]
                ds_b = ds.astype(BF16)
                a_b = jnp.where(tril, _dot(qi, ki, NT), 0.0).astype(BF16)
                da_b = jnp.where(tril, _dot(do_b, v, NT), 0.0).astype(BF16)
                dq = _dot(do_b, st_b, NN) * e_0 + (_dot(da_b, ki, NN) + _dot(da_b, ki_lo, NN)) * e_q
                dk_state = _dot(v, ds_b, NN) * e_l
                dk = (_dot(da_b, qi, TN) + _dot(da_b, qi_lo, TN)) * e_k + dk_state
                dv = _dot(a_b, do_b, TN) + _dot(kl, ds_b, NT)
                ds_scr[h] = ds * e_last + _dot(do_b, q0, TN)
                d_last = jnp.sum(dk_state * k, axis=0, keepdims=True) + jnp.sum(ds * st, axis=0, keepdims=True) * e_last
                dlogf = _tri_dot(tril_b, q * dq - k * dk, TN) + d_last
                dfg = dlogf / fg - dk
                dlb_scr[:, cols] += jnp.sum(dfg * (1.0 - sf), axis=0, keepdims=True)
                sq = _sigmoid(qr)
                dp_ref[0, rows, cols] = (dq * (HG_DK ** -0.5) * (sq * (1.0 + qr * (1.0 - sq)))).astype(BF16)
                dp_ref[1, rows, cols] = (dfg * (1.0 - lbv) * sf * (1.0 - sf)).astype(BF16)
                dp_ref[2, rows, cols] = dv.astype(BF16)
                dp_ref[3, rows, cols] = dg_raw.astype(BF16)
            return carry

        lax.fori_loop(0, cpb, chunk, 0)

        @pl.when(step == nb - 1)
        def _():
            d0 = dlb_scr[...] * lb_all * (1.0 - lb_all)
            dl_ref[0:1, :] = d0
            dl_ref[1:2, :] = -d0

    rev = lambda i: nb - 1 - i
    blk = pl.BlockSpec((rb, D_MODEL), lambda i: (rev(i), 0))
    return pl.pallas_call(
        body,
        name="hgrn_bwd",
        grid=(nb,),
        in_specs=_hg_proj_specs(rb, rev)
        + [pl.BlockSpec((2, D_MODEL), lambda i: (0, 0)), pl.BlockSpec((1, HG_DK), lambda i: (0, 0)), blk,
           pl.BlockSpec((HG_HEADS, cpb, HG_DK, HG_DK), lambda i: (0, rev(i), 0, 0)), blk],
        out_specs=[
            pl.BlockSpec((4, rb, D_MODEL), lambda i: (0, rev(i), 0)),
            pl.BlockSpec((2, D_MODEL), lambda i: (0, 0)),
            pl.BlockSpec((1, HG_DK), lambda i: (0, 0)),
        ],
        out_shape=[
            jax.ShapeDtypeStruct((4, t, D_MODEL), BF16),
            jax.ShapeDtypeStruct((2, D_MODEL), F32),
            jax.ShapeDtypeStruct((1, HG_DK), F32),
        ],
        scratch_shapes=[pltpu.VMEM((HG_HEADS, HG_DK, HG_DK), F32), pltpu.VMEM((1, D_MODEL), F32)],
        compiler_params=_cparams(("arbitrary",)),
    )(proj, proj, proj, proj, lb, wn, o, states, dy)


ATT_STACK = 8


def _att_stack(q_ref, sink_ref, first, lo, bias_p, bias_c, extra_ref=None):
    qs, bps, bcs, sinks, extras = [], [], [], None, []
    rows = lax.broadcasted_iota(jnp.int32, (ATT_STACK * WINDOW, 1), 0)
    for i in range(ATT_STACK):
        hq = first + i
        cols = slice((hq // 2) * LANES, (hq // 2 + 1) * LANES)
        sel = lo if hq % 2 == 0 else jnp.logical_not(lo)
        qp = q_ref[:, cols] * (ATT_HD ** -0.5)
        qs.append(jnp.where(sel, qp, jnp.zeros_like(qp)))
        bps.append(ALIBI_SLOPES[hq] * bias_p)
        bcs.append(ALIBI_SLOPES[hq] * bias_c)
        sinks = sink_ref[hq] if sinks is None else jnp.where(rows < i * WINDOW, sinks, sink_ref[hq])
        if extra_ref is not None:
            ep = extra_ref[:, cols]
            extras.append(jnp.where(sel, ep, jnp.zeros_like(ep)))
    cat = lambda parts: jnp.concatenate(parts, axis=0)
    return cat(qs), cat(bps), cat(bcs), sinks, (cat(extras) if extras else None)


def _att_rows(i):
    return slice(i * WINDOW, (i + 1) * WINDOW)


def _att_bias(n):
    tq = lax.broadcasted_iota(jnp.int32, (WINDOW, WINDOW), 0)
    sk = lax.broadcasted_iota(jnp.int32, (WINDOW, WINDOW), 1)
    valid_c = sk <= tq
    valid_p = (sk - tq) > jnp.where(n > 0, 0, WINDOW)
    dist_c = (tq - sk).astype(F32)
    return jnp.where(valid_p, -dist_c - float(WINDOW), NEG), jnp.where(valid_c, -dist_c, NEG)


def _att_halves(x, lo, kh):
    r = pltpu.roll(x, ATT_HD, 1)
    zero = jnp.zeros_like(x)
    if kh == 0:
        return jnp.where(lo, x, r), jnp.where(lo, x, zero), jnp.where(lo, zero, r)
    return jnp.where(lo, r, x), jnp.where(lo, r, zero), jnp.where(lo, zero, x)


def _att_probs(qm, k2p, k2c, bias_p, bias_c, sink):
    sp = _dot(qm, k2p, NT) + bias_p
    sc = _dot(qm, k2c, NT) + bias_c
    m = jnp.maximum(jnp.maximum(jnp.max(sp, axis=-1, keepdims=True), jnp.max(sc, axis=-1, keepdims=True)), sink)
    ep = jnp.exp(sp - m)
    ec = jnp.exp(sc - m)
    es = jnp.exp(sink - m)
    inv = 1.0 / (jnp.sum(ep, axis=-1, keepdims=True) + jnp.sum(ec, axis=-1, keepdims=True) + es)
    return ep * inv, ec * inv, es * inv


def _attn_fwd(q, kv, sinks):
    t = q.shape[0]
    nb = t // WINDOW

    def body(sink_ref, q_ref, kvp_ref, kvc_ref, o_ref):
        n = pl.program_id(0)
        bias_p, bias_c = _att_bias(n)
        lo = lax.broadcasted_iota(jnp.int32, (WINDOW, LANES), 1) < ATT_HD
        for kh in range(ATT_KVH):
            k2p, _, _ = _att_halves(kvp_ref[:, 0:LANES], lo, kh)
            k2c, _, _ = _att_halves(kvc_ref[:, 0:LANES], lo, kh)
            _, vlo_p, vhi_p = _att_halves(kvp_ref[:, LANES:2 * LANES], lo, kh)
            _, vlo_c, vhi_c = _att_halves(kvc_ref[:, LANES:2 * LANES], lo, kh)
            for first in range(kh * ATT_GROUP, (kh + 1) * ATT_GROUP, ATT_STACK):
                qs, bp, bc, sinks, _ = _att_stack(q_ref, sink_ref, first, lo, bias_p, bias_c)
                pp, pc, _ = _att_probs(qs, k2p, k2c, bp, bc, sinks)
                pp, pc = pp.astype(BF16), pc.astype(BF16)
                for i in range(0, ATT_STACK, 2):
                    even, odd = _att_rows(i), _att_rows(i + 1)
                    out = (_dot(pp[even], vlo_p, NN) + _dot(pc[even], vlo_c, NN)
                           + _dot(pp[odd], vhi_p, NN) + _dot(pc[odd], vhi_c, NN))
                    j = (first + i) // 2
                    o_ref[:, j * LANES:(j + 1) * LANES] = out.astype(BF16)

    return pl.pallas_call(
        body,
        name="attn_fwd",
        grid=(nb,),
        in_specs=[
            pl.BlockSpec(memory_space=pltpu.SMEM),
            pl.BlockSpec((WINDOW, D_MODEL), lambda n: (n, 0)),
            pl.BlockSpec((WINDOW, 2 * LANES), lambda n: (jnp.maximum(n - 1, 0), 0)),
            pl.BlockSpec((WINDOW, 2 * LANES), lambda n: (n, 0)),
        ],
        out_specs=pl.BlockSpec((WINDOW, D_MODEL), lambda n: (n, 0)),
        out_shape=jax.ShapeDtypeStruct((t, D_MODEL), BF16),
        compiler_params=_cparams(("parallel",)),
    )(sinks, q, kv, kv)


def _attn_bwd(q, kv, sinks, dout):
    t = q.shape[0]
    nb = t // WINDOW

    def body(sink_ref, q_ref, kvp_ref, kvc_ref, do_ref, dq_ref, dkv_ref, dsink_ref, carry_ref):
        n = pl.program_id(0)

        @pl.when(n == 0)
        def _():
            carry_ref[...] = jnp.zeros_like(carry_ref)
            dsink_ref[...] = jnp.zeros_like(dsink_ref)

        @pl.when(n == nb)
        def _():
            dkv_ref[...] = carry_ref[...].astype(BF16)

        @pl.when(n < nb)
        def _():
            bias_p, bias_c = _att_bias(n)
            lo = lax.broadcasted_iota(jnp.int32, (WINDOW, LANES), 1) < ATT_HD
            lane1 = lax.broadcasted_iota(jnp.int32, (1, LANES), 1)
            dsink = jnp.zeros((1, LANES), F32)
            halves = []
            for kh in range(ATT_KVH):
                k2p, klo_p, khi_p = _att_halves(kvp_ref[:, 0:LANES], lo, kh)
                k2c, klo_c, khi_c = _att_halves(kvc_ref[:, 0:LANES], lo, kh)
                v2p, _, _ = _att_halves(kvp_ref[:, LANES:2 * LANES], lo, kh)
                v2c, _, _ = _att_halves(kvc_ref[:, LANES:2 * LANES], lo, kh)
                acc = [jnp.zeros((WINDOW, LANES), F32) for _ in range(4)]
                for first in range(kh * ATT_GROUP, (kh + 1) * ATT_GROUP, ATT_STACK):
                    qs, bp, bc, sinks, dos = _att_stack(q_ref, sink_ref, first, lo, bias_p, bias_c, do_ref)
                    pp, pc, ps = _att_probs(qs, k2p, k2c, bp, bc, sinks)
                    dpp = _dot(dos, v2p, NT)
                    dpc = _dot(dos, v2c, NT)
                    delta = jnp.sum(pp * dpp, axis=-1, keepdims=True) + jnp.sum(pc * dpc, axis=-1, keepdims=True)
                    dsp = (pp * (dpp - delta)).astype(BF16)
                    dsc = (pc * (dpc - delta)).astype(BF16)
                    sink_term = ps * delta
                    for i in range(ATT_STACK):
                        dsink = dsink + jnp.where(lane1 == first + i, -jnp.sum(sink_term[_att_rows(i)], axis=0, keepdims=True), 0.0)
                    for i in range(0, ATT_STACK, 2):
                        even, odd = _att_rows(i), _att_rows(i + 1)
                        dq_pair = (_dot(dsp[even], klo_p, NN) + _dot(dsc[even], klo_c, NN)
                                   + _dot(dsp[odd], khi_p, NN) + _dot(dsc[odd], khi_c, NN))
                        j = (first + i) // 2
                        dq_ref[:, j * LANES:(j + 1) * LANES] = (dq_pair * (ATT_HD ** -0.5)).astype(BF16)
                    acc[0] = acc[0] + _dot(dsp, qs, TN)
                    acc[1] = acc[1] + _dot(dsc, qs, TN)
                    acc[2] = acc[2] + _dot(pp.astype(BF16), dos, TN)
                    acc[3] = acc[3] + _dot(pc.astype(BF16), dos, TN)
                halves.append([a + pltpu.roll(a, ATT_HD, 1) for a in acc])
            prev = jnp.concatenate(
                [jnp.where(lo, halves[0][0], halves[1][0]), jnp.where(lo, halves[0][2], halves[1][2])], axis=1)
            cur = jnp.concatenate(
                [jnp.where(lo, halves[0][1], halves[1][1]), jnp.where(lo, halves[0][3], halves[1][3])], axis=1)
            dkv_ref[...] = (carry_ref[...] + prev).astype(BF16)
            carry_ref[...] = cur
            dsink_ref[...] += dsink

    blk = lambda n: jnp.minimum(n, nb - 1)
    return pl.pallas_call(
        body,
        name="attn_bwd",
        grid=(nb + 1,),
        in_specs=[
            pl.BlockSpec(memory_space=pltpu.SMEM),
            pl.BlockSpec((WINDOW, D_MODEL), lambda n: (blk(n), 0)),
            pl.BlockSpec((WINDOW, 2 * LANES), lambda n: (jnp.maximum(blk(n) - 1, 0), 0)),
            pl.BlockSpec((WINDOW, 2 * LANES), lambda n: (blk(n), 0)),
            pl.BlockSpec((WINDOW, D_MODEL), lambda n: (blk(n), 0)),
        ],
        out_specs=[
            pl.BlockSpec((WINDOW, D_MODEL), lambda n: (blk(n), 0)),
            pl.BlockSpec((WINDOW, 2 * LANES), lambda n: (jnp.maximum(n - 1, 0), 0)),
            pl.BlockSpec((1, LANES), lambda n: (0, 0)),
        ],
        out_shape=[
            jax.ShapeDtypeStruct((t, D_MODEL), BF16),
            jax.ShapeDtypeStruct((t, 2 * LANES), BF16),
            jax.ShapeDtypeStruct((1, LANES), F32),
        ],
        scratch_shapes=[pltpu.VMEM((WINDOW, 2 * LANES), F32)],
        compiler_params=_cparams(("arbitrary",)),
    )(sinks, q, kv, kv, dout)


def _ffn_fwd(h, norm_g, w_up, conv_w, conv_b, w_down, tag, after_up=lambda up: None):
    xn = _rms_fwd(h, norm_g, name=f"ffn{tag}_norm")
    up = _mm_nn(xn, w_up, name=f"ffn{tag}_up")
    after_up(up)
    act = _conv_fwd(up, conv_w, conv_b, name=f"ffn{tag}_conv")
    h_out = _mm_nn(act, w_down, res=h, name=f"ffn{tag}_down")
    return h_out, (xn, up, act)


def _ffn_bwd(dh, h, norm_g, w_up, conv_w, conv_b, w_down, saved, tag, deps=()):
    xn, up, act = saved
    dw_down = _mm_tn(act, dh, 1, D_MODEL, name=f"ffn{tag}_dwdown", deps=deps)
    dact = _mm_nt(dh, w_down, name=f"ffn{tag}_dact", deps=deps)
    dup, dconv_w, dconv_b = _conv_bwd(up, conv_w, conv_b, dact, name=f"ffn{tag}_dconv")
    dw_up = _mm_tn(xn, dup, N_CHIPS, CONV_COLS, stacked=True, name=f"ffn{tag}_dwup")
    dxn = _mm_nt(dup, w_up, stacked=True, name=f"ffn{tag}_dxn")
    dh_in, dnorm = _rms_bwd(h, norm_g, dxn, dh, name=f"ffn{tag}_dnorm")
    return dh_in, dict(ffn_w_down=dw_down, ffn_w_up=dw_up, ffn_conv_w=dconv_w, ffn_conv_b=dconv_b, ffn_norm=dnorm)


def _local_step(x, target, w, fetch=lambda w, stage, after: w, hook=lambda point, dh, grads: ()):
    xn0 = _rms_fwd(x, w["hg_norm"], name="hg_norm")
    proj = _mm_nn(xn0, w["hg_w_in"], name="hg_in")
    o, y, states = _hgrn_fwd(proj, w["hg_lb"], w["hg_out_norm"])
    w = fetch(w, "mixer_out", y)
    fetch(w, "layer0_relay", y)
    h_a = _mm_nn(y, w["hg_w_out"], res=x, name="hg_out")
    w = fetch(w, "layer0", h_a)
    h1, ffn0 = _ffn_fwd(h_a, w["ffn_norm"][0], w["ffn_w_up"][0], w["ffn_conv_w"][0], w["ffn_conv_b"][0], w["ffn_w_down"][0], 0,
                        lambda up: fetch(w, "layer1_relay", up))
    w = fetch(w, "layer1", h1)
    kvn = _rms_fwd(h1, w["kv_norm"], name="kv_norm")
    kv = _mm_nn(kvn, w["w_kv"], out_dtype=BF16, name="kv_proj")
    xa = _rms_fwd(h1, w["attn_norm"], name="attn_norm")
    qa = _mm_nn(xa, w["attn_w_q"], out_dtype=BF16, name="attn_q")
    ao = _attn_fwd(qa, kv, w["attn_sinks"])
    h_b = _mm_nn(ao, w["attn_w_o"], res=h1, name="attn_o")
    h2, ffn1 = _ffn_fwd(h_b, w["ffn_norm"][1], w["ffn_w_up"][1], w["ffn_conv_w"][1], w["ffn_conv_b"][1], w["ffn_w_down"][1], 1)
    dh2, d_final, loss = _loss_head(h2, w["final_norm"], target)

    dh_b, g1 = _ffn_bwd(dh2, h_b, w["ffn_norm"][1], w["ffn_w_up"][1], w["ffn_conv_w"][1], w["ffn_conv_b"][1], w["ffn_w_down"][1], ffn1, 1)
    deps = hook("ffn1", dh_b, g1)
    dw_o = _mm_tn(ao, dh_b, 1, D_MODEL, name="attn_dwo", deps=deps)
    dao = _mm_nt(dh_b, w["attn_w_o"], out_dtype=BF16, name="attn_dao", deps=deps)
    dqa, dkv, dsinks = _attn_bwd(qa, kv, w["attn_sinks"], dao)
    dw_q = _mm_tn(xa, dqa, 1, D_MODEL, name="attn_dwq")
    dxa = _mm_nt(dqa, w["attn_w_q"], name="attn_dxa")
    dh1, d_attn_norm = _rms_bwd(h1, w["attn_norm"], dxa, dh_b, name="attn_dnorm")
    dw_kv = _mm_tn(kvn, dkv, 1, 2 * LANES, name="kv_dw")
    dkvn = _mm_nt(dkv, w["w_kv"], name="kv_dx")
    dh1, d_kv_norm = _rms_bwd(h1, w["kv_norm"], dkvn, dh1, name="kv_dnorm")
    deps = hook("attn", dh1, dict(attn_w_o=dw_o, attn_w_q=dw_q, w_kv=dw_kv))
    dh_a, g0 = _ffn_bwd(dh1, h_a, w["ffn_norm"][0], w["ffn_w_up"][0], w["ffn_conv_w"][0], w["ffn_conv_b"][0], w["ffn_w_down"][0], ffn0, 0, deps)
    deps = hook("ffn0", dh_a, g0)
    dw_out = _mm_tn(y, dh_a, 1, D_MODEL, name="hg_dwout", deps=deps)
    dy = _mm_nt(dh_a, w["hg_w_out"], out_dtype=BF16, name="hg_dy", deps=deps)
    dproj, dlb, d_out_norm = _hgrn_bwd(proj, w["hg_lb"], w["hg_out_norm"], o, states, dy)
    deps = hook("hgrn", dproj, None)
    dw_in = _mm_tn(xn0, dproj, N_CHIPS, D_MODEL, stacked=True, name="hg_dwin", deps=deps)
    deps = hook("hg_w", dproj, dict(hg_w_out=dw_out, hg_w_in=dw_in))
    dxn0 = _mm_nt(dproj, w["hg_w_in"], stacked=True, name="hg_dxn", deps=deps)
    dx, d_hg_norm = _rms_bwd(x, w["hg_norm"], dxn0, dh_a, name="hg_dnorm")

    grads = dict(
        hg_norm=d_hg_norm, hg_w_in=dw_in, hg_lb=dlb, hg_out_norm=d_out_norm, hg_w_out=dw_out,
        kv_norm=d_kv_norm, w_kv=dw_kv, attn_norm=d_attn_norm, attn_w_q=dw_q, attn_sinks=dsinks, attn_w_o=dw_o,
        final_norm=d_final,
    )
    for name in g0:
        grads[name] = [g0[name], g1[name]]
    return loss, dx, grads


ANY = pl.BlockSpec(memory_space=pl.ANY)


def _place():
    x, y, c = lax.axis_index("x"), lax.axis_index("y"), lax.axis_index("c")
    chips = [(1 - x, y), (x, 1 - y), (1 - x, 1 - y)]
    return x, y, c, chips


def _rcopy(src, dst, send_sem, recv_sem, to):
    return pltpu.make_async_remote_copy(src_ref=src, dst_ref=dst, send_sem=send_sem, recv_sem=recv_sem, device_id=to, device_id_type=MESH)


HBM = pl.BlockSpec(memory_space=pltpu.HBM)
SEM = pl.BlockSpec(memory_space=pltpu.SEMAPHORE)
EFFECT = pltpu.SideEffectType.DATAFLOW_SIDE_EFFECTING


def _in_hbm(a):
    return pltpu.with_memory_space_constraint(a, pltpu.HBM)


def _place_shard(shard, place, dtype, name, deps=()):
    r, cols = shard.shape
    tr = _pick(r, ELEM_ROWS)

    def body(place_ref, s_ref, *rest):
        o_ref = rest[-1]
        o_ref[...] = s_ref[...].astype(o_ref.dtype)

    return pl.pallas_call(
        body,
        name=name,
        grid_spec=pltpu.PrefetchScalarGridSpec(
            num_scalar_prefetch=1,
            grid=(r // tr,),
            in_specs=[pl.BlockSpec((tr, cols), lambda i, place_ref: (i, 0))] + _dep_specs(deps),
            out_specs=pl.BlockSpec((None, tr, cols), lambda i, place_ref: (place_ref[0], i, 0)),
        ),
        out_shape=jax.ShapeDtypeStruct((N_CHIPS, r, cols), dtype),
        compiler_params=_cparams(("parallel",)),
    )(place, shard, *deps)


def _start_copies(name, bufs, n_sem, copies):
    n = len(bufs)

    def body(*refs):
        for cp in copies(refs[:n], refs[n], refs[n + 1]):
            cp.start()
        refs[-1][...] = jnp.zeros_like(refs[-1])

    outs = pl.pallas_call(
        body,
        name=name,
        in_specs=[HBM] * n,
        out_specs=[SEM, SEM] + [HBM] * n + [pl.BlockSpec(memory_space=pltpu.VMEM)],
        out_shape=[pltpu.SemaphoreType.DMA((n_sem,)), pltpu.SemaphoreType.DMA((n_sem,))] + [pltpu.HBM(b.shape, b.dtype) for b in bufs]
        + [jax.ShapeDtypeStruct((SUBLANES, LANES), F32)],
        input_output_aliases={i: 2 + i for i in range(n)},
        compiler_params=pltpu.CompilerParams(has_side_effects=EFFECT),
    )(*[_in_hbm(b) for b in bufs])
    return outs[0], outs[1], list(outs[2:-1]), outs[-1]


def _wait_copies(name, bufs, send_sems, recv_sems, after, copies):
    n = len(bufs)

    def body(*refs):
        for cp in copies(refs[:n], refs[n], refs[n + 1]):
            cp.wait_send()
            cp.wait_recv()

    return pl.pallas_call(
        body,
        name=name,
        in_specs=[HBM] * n + [SEM, SEM, ANY],
        out_specs=[HBM] * n,
        out_shape=[pltpu.HBM(b.shape, b.dtype) for b in bufs],
        input_output_aliases={i: i for i in range(n)},
        compiler_params=pltpu.CompilerParams(has_side_effects=EFFECT),
    )(*bufs, send_sems, recv_sems, after)


def _relay_copies(name, bufs, send_sems, recv_sems, after, landed, n_sem, onward):
    n = len(bufs)

    def body(*refs):
        for cp in landed(refs[:n], refs[n], refs[n + 1]):
            cp.wait_send()
            cp.wait_recv()
        for cp in onward(refs[:n], refs[n + 3], refs[n + 4]):
            cp.start()
        refs[-1][...] = jnp.zeros_like(refs[-1])

    outs = pl.pallas_call(
        body,
        name=name,
        in_specs=[HBM] * n + [SEM, SEM, ANY],
        out_specs=[SEM, SEM] + [HBM] * n + [pl.BlockSpec(memory_space=pltpu.VMEM)],
        out_shape=[pltpu.SemaphoreType.DMA((n_sem,)), pltpu.SemaphoreType.DMA((n_sem,))] + [pltpu.HBM(b.shape, b.dtype) for b in bufs]
        + [jax.ShapeDtypeStruct((SUBLANES, LANES), F32)],
        input_output_aliases={i: 2 + i for i in range(n)},
        compiler_params=pltpu.CompilerParams(has_side_effects=EFFECT),
    )(*bufs, send_sems, recv_sems, after)
    return outs[0], outs[1], list(outs[2:-1]), outs[-1]


def _gather_half_copies(first, count, over_ici):
    def copies(refs, send_sems, recv_sems):
        x, y, c, chips = _place()
        out = []
        for i in range(count):
            h = refs[i].shape[1] // 2
            mine = pl.ds(c * h, h)
            for j, (px, py) in enumerate(chips):
                k = 3 * (first + i) + j
                slot = 2 * x + y if over_ici else 2 * px + py
                to = (px, py, c) if over_ici else (x, y, 1 - c)
                out.append(_rcopy(refs[i].at[slot, mine], refs[i].at[slot, mine], send_sems.at[k], recv_sems.at[k], to))
        return out

    return copies


def _gather_copies(first, count):
    def copies(refs, send_sems, recv_sems):
        x, y, c, chips = _place()
        me = 2 * x + y
        out = []
        for i in range(count):
            for j, (px, py) in enumerate(chips):
                k = 3 * (first + i) + j
                out.append(_rcopy(refs[i].at[me], refs[i].at[me], send_sems.at[k], recv_sems.at[k], (px, py, c)))
        return out

    return copies


def _swap_copies(n):
    def copies(refs, send_sems, recv_sems):
        x, y, c, _ = _place()
        out = []
        for i in range(n):
            h = refs[i].shape[1] // 2
            out.append(_rcopy(refs[i].at[:, pl.ds((1 - c) * h, h)], refs[n + i], send_sems.at[i], recv_sems.at[i], (x, y, 1 - c)))
        return out

    return copies


def _partial_copies(n):
    def copies(refs, send_sems, recv_sems):
        x, y, c, chips = _place()
        out = []
        for i in range(n):
            for j, (px, py) in enumerate(chips):
                out.append(_rcopy(refs[i].at[2 * px + py], refs[n + i].at[j], send_sems.at[3 * i + j], recv_sems.at[3 * i + j], (px, py, c)))
        return out

    return copies


def _share_copies(n):
    def copies(refs, send_sems, recv_sems):
        x, y, c, _ = _place()
        return [_rcopy(refs[i].at[c], refs[i].at[c], send_sems.at[i], recv_sems.at[i], (x, y, 1 - c)) for i in range(n)]

    return copies


def _allreduce_small(vec):
    rows = vec.shape[0]

    def body(v_ref, o_ref, buf, send_sems, recv_sems):
        x, y, c, _ = _place()
        me = 4 * x + 2 * y + c
        buf[me] = v_ref[...]
        copies = []
        for k in range(1, N_DEV):
            peer = (x ^ (k >> 2), y ^ ((k >> 1) & 1), c ^ (k & 1))
            cp = _rcopy(v_ref, buf.at[me], send_sems.at[k - 1], recv_sems.at[k - 1], peer)
            cp.start()
            copies.append(cp)
        for cp in copies:
            cp.wait()
        acc = buf[0]
        for d in range(1, N_DEV):
            acc = acc + buf[d]
        o_ref[...] = acc

    return pl.pallas_call(
        body,
        name="allreduce_small",
        in_specs=[pl.BlockSpec(memory_space=pltpu.VMEM)],
        out_specs=pl.BlockSpec(memory_space=pltpu.VMEM),
        out_shape=jax.ShapeDtypeStruct(vec.shape, F32),
        scratch_shapes=[pltpu.VMEM((N_DEV, rows, LANES), F32), pltpu.SemaphoreType.DMA((N_DEV - 1,)), pltpu.SemaphoreType.DMA((N_DEV - 1,))],
        compiler_params=pltpu.CompilerParams(vmem_limit_bytes=VMEM_LIMIT_BYTES),
    )(vec)


class _Reduction:
    def __init__(self, tag, grads, place, core):
        self.tag, self.n, self.place, self.core = tag, len(grads), place, core
        lands = [lax.empty((N_CHIPS, g.shape[1] // 2, g.shape[2]), F32) for g in grads]
        self._start("swap", list(grads) + lands, self.n, _swap_copies(self.n))

    def _start(self, stage, bufs, n_sem, copies):
        *self.flight, self.token = _start_copies(f"rs_{stage}_start_{self.tag}", bufs, n_sem, copies)

    def _landed(self, stage, after, copies):
        send_sems, recv_sems, bufs = self.flight
        return _wait_copies(f"rs_{stage}_wait_{self.tag}", bufs, send_sems, recv_sems, after, copies)

    def to_chips(self, after):
        n = self.n
        bufs = self._landed("swap", after, _swap_copies(n))
        sums = [_add_core_halves(g, o, self.core, name=f"rs_add_core_{self.tag}_{i}") for i, (g, o) in enumerate(zip(bufs[:n], bufs[n:]))]
        self.mine = [f for f, _ in sums]
        parts = [b for _, b in sums]
        lands = [lax.empty((3,) + p.shape[1:], BF16) for p in parts]
        self._start("send", parts + lands, 3 * n, _partial_copies(n))

    def to_core(self, after):
        n = self.n
        bufs = self._landed("send", after, _partial_copies(n))
        halves = [_add_chip_partials(f, o, self.place, name=f"rs_add_chip_{self.tag}_{i}") for i, (f, o) in enumerate(zip(self.mine, bufs[n:]))]
        self._start("share", halves, n, _share_copies(n))

    def finish(self, after):
        return [b.reshape((-1,) + b.shape[2:]) for b in self._landed("share", after, _share_copies(self.n))]


ELEM_ROWS = (256, 176, 128, 64, 32, 16, 8)


def _add_core_halves(grad, got, c, name):
    s, r, cols = grad.shape
    h = r // 2
    tr = _pick(h, ELEM_ROWS)

    def body(c_ref, g_ref, o_ref, f_ref, b_ref):
        acc = g_ref[...] + o_ref[...]
        f_ref[...] = acc
        b_ref[...] = acc.astype(BF16)

    blk = pl.BlockSpec((None, tr, cols), lambda k, i, c_ref: (k, i, 0))
    return pl.pallas_call(
        body,
        name=name,
        grid_spec=pltpu.PrefetchScalarGridSpec(
            num_scalar_prefetch=1,
            grid=(s, h // tr),
            in_specs=[pl.BlockSpec((None, None, tr, cols), lambda k, i, c_ref: (k, c_ref[0], i, 0)), blk],
            out_specs=[blk, blk],
        ),
        out_shape=[jax.ShapeDtypeStruct((s, h, cols), F32), jax.ShapeDtypeStruct((s, h, cols), BF16)],
        compiler_params=_cparams(("parallel", "parallel")),
    )(c, grad.reshape(s, 2, h, cols), got)


def _add_chip_partials(mine, got, place, name):
    _, h, cols = mine.shape
    tr = _pick(h, ELEM_ROWS)

    def body(place_ref, m_ref, g_ref, o_ref):
        acc = m_ref[...]
        for j in range(3):
            acc = acc + g_ref[j].astype(F32)
        o_ref[...] = acc

    return pl.pallas_call(
        body,
        name=name,
        grid_spec=pltpu.PrefetchScalarGridSpec(
            num_scalar_prefetch=1,
            grid=(h // tr,),
            in_specs=[
                pl.BlockSpec((None, tr, cols), lambda i, place_ref: (place_ref[0], i, 0)),
                pl.BlockSpec((3, tr, cols), lambda i, place_ref: (0, i, 0)),
            ],
            out_specs=pl.BlockSpec((None, tr, cols), lambda i, place_ref: (place_ref[1], i, 0)),
        ),
        out_shape=jax.ShapeDtypeStruct((2, h, cols), F32),
        compiler_params=_cparams(("parallel",)),
    )(place, mine, got)


def _adamw_math(w, m, v, g):
    nm = ADAM_B1 * m + (1.0 - ADAM_B1) * g
    nv = ADAM_B2 * v + (1.0 - ADAM_B2) * (g * g)
    m_hat = nm * (1.0 / (1.0 - ADAM_B1 ** ADAM_STEP))
    v_hat = nv * (1.0 / (1.0 - ADAM_B2 ** ADAM_STEP))
    return -ADAM_LR * (m_hat / (jnp.sqrt(v_hat) + ADAM_EPS) + ADAM_WD * w), nm, nv


def _adamw_layer(w, m, v, g, layer, prev, name):
    nl, r, cols = w.shape
    tr = _pick(r, ELEM_ROWS)

    def body(w_ref, m_ref, v_ref, g_ref, *rest):
        go_ref, d_ref, nm_ref, nv_ref = rest[-4:]
        gv = g_ref[...]
        d_ref[...], nm_ref[...], nv_ref[...] = _adamw_math(w_ref[...], m_ref[...], v_ref[...], gv)
        go_ref[...] = gv

    lay = pl.BlockSpec((None, tr, cols), lambda i: (layer, i, 0))
    return pl.pallas_call(
        body,
        name=name,
        grid=(r // tr,),
        in_specs=[lay] * 3 + [pl.BlockSpec((tr, cols), lambda i: (i, 0))] + ([ANY] * 4 if prev else []),
        out_specs=[lay] * 4,
        out_shape=[jax.ShapeDtypeStruct((nl, r, cols), F32)] * 4,
        input_output_aliases={4 + k: k for k in range(4)} if prev else {},
        compiler_params=_cparams(("parallel",)),
    )(w, m, v, g, *(prev or ()))


def _adamw(w, m, v, g, name):
    r, cols = w.shape
    tr = _pick(r, ELEM_ROWS)

    def body(w_ref, m_ref, v_ref, g_ref, d_ref, nm_ref, nv_ref):
        d_ref[...], nm_ref[...], nv_ref[...] = _adamw_math(w_ref[...], m_ref[...], v_ref[...], g_ref[...])

    blk = pl.BlockSpec((tr, cols), lambda i: (i, 0))
    return pl.pallas_call(
        body,
        name=name,
        grid=(r // tr,),
        in_specs=[blk] * 4,
        out_specs=[blk] * 3,
        out_shape=[jax.ShapeDtypeStruct((r, cols), F32)] * 3,
        compiler_params=_cparams(("parallel",)),
    )(w, m, v, g)


SMALL_COLS = 384
SMALL_ROWS = 16


def _pad_rows(flat, rows, cols):
    return jnp.pad(flat, (0, rows * cols - flat.shape[0])).reshape(rows, cols)


def kernel(x, hg_norm, hg_w_in, hg_lb_logits, hg_out_norm, hg_w_out, kv_norm, w_kv, attn_norm, attn_w_q, attn_sinks, attn_w_o, ffn_norm, ffn_w_up, ffn_conv_w, ffn_conv_b, ffn_w_down, final_norm, loss_target, m_hg_norm, m_hg_w_in, m_hg_lb_logits, m_hg_out_norm, m_hg_w_out, m_kv_norm, m_w_kv, m_attn_norm, m_attn_w_q, m_attn_sinks, m_attn_w_o, m_ffn_norm, m_ffn_w_up, m_ffn_conv_w, m_ffn_conv_b, m_ffn_w_down, m_final_norm, v_hg_norm, v_hg_w_in, v_hg_lb_logits, v_hg_out_norm, v_hg_w_out, v_kv_norm, v_w_kv, v_attn_norm, v_attn_w_q, v_attn_sinks, v_attn_w_o, v_ffn_norm, v_ffn_w_up, v_ffn_conv_w, v_ffn_conv_b, v_ffn_w_down, v_final_norm):
    wts = dict(hg_norm=hg_norm, hg_w_in=hg_w_in, hg_lb_logits=hg_lb_logits, hg_out_norm=hg_out_norm, hg_w_out=hg_w_out, kv_norm=kv_norm, w_kv=w_kv, attn_norm=attn_norm, attn_w_q=attn_w_q, attn_sinks=attn_sinks, attn_w_o=attn_w_o, ffn_norm=ffn_norm, ffn_w_up=ffn_w_up, ffn_conv_w=ffn_conv_w, ffn_conv_b=ffn_conv_b, ffn_w_down=ffn_w_down, final_norm=final_norm)
    mom1 = dict(hg_norm=m_hg_norm, hg_w_in=m_hg_w_in, hg_lb_logits=m_hg_lb_logits, hg_out_norm=m_hg_out_norm, hg_w_out=m_hg_w_out, kv_norm=m_kv_norm, w_kv=m_w_kv, attn_norm=m_attn_norm, attn_w_q=m_attn_w_q, attn_sinks=m_attn_sinks, attn_w_o=m_attn_w_o, ffn_norm=m_ffn_norm, ffn_w_up=m_ffn_w_up, ffn_conv_w=m_ffn_conv_w, ffn_conv_b=m_ffn_conv_b, ffn_w_down=m_ffn_w_down, final_norm=m_final_norm)
    mom2 = dict(hg_norm=v_hg_norm, hg_w_in=v_hg_w_in, hg_lb_logits=v_hg_lb_logits, hg_out_norm=v_hg_out_norm, hg_w_out=v_hg_w_out, kv_norm=v_kv_norm, w_kv=v_w_kv, attn_norm=v_attn_norm, attn_w_q=v_attn_w_q, attn_sinks=v_attn_sinks, attn_w_o=v_attn_w_o, ffn_norm=v_ffn_norm, ffn_w_up=v_ffn_w_up, ffn_conv_w=v_ffn_conv_w, ffn_conv_b=v_ffn_conv_b, ffn_w_down=v_ffn_w_down, final_norm=v_final_norm)
    names = list(wts)
    chip = 2 * lax.axis_index("x") + lax.axis_index("y")
    core = lax.axis_index("c")
    core_arr = jnp.reshape(core, (1,)).astype(jnp.int32)
    fs = D_FF // N_CHIPS
    ds = D_MODEL // N_CHIPS

    place_arr = jnp.stack([chip, core]).astype(jnp.int32)
    small = jnp.concatenate([hg_norm.reshape(-1), hg_lb_logits.reshape(-1), ffn_conv_w.reshape(-1)])
    n_small = small.shape[0]
    shards = [
        ("small", _pad_rows(small, SMALL_ROWS, SMALL_COLS), F32), ("hg_w_in", hg_w_in[0], BF16),
        ("hg_w_out", hg_w_out[0], BF16), ("ffn_w_up0", ffn_w_up[0], BF16), ("ffn_w_down0", ffn_w_down[0], BF16),
        ("w_kv", w_kv, BF16), ("attn_w_q", attn_w_q[0], BF16), ("attn_w_o", attn_w_o[0], BF16),
        ("ffn_w_up1", ffn_w_up[1], BF16), ("ffn_w_down1", ffn_w_down[1], BF16),
    ]
    n_first = 3
    spans = dict(layer0=(0, 2), layer1=(2, 7))
    placed = [_place_shard(s, place_arr, dt, name=f"place_{nm}") for nm, s, dt in shards[:n_first]]
    first = _start_copies("gather_start_first", placed, 3 * n_first, _gather_copies(0, n_first))
    placed = [_place_shard(s, place_arr, dt, name=f"place_{nm}", deps=(first[3],)) for nm, s, dt in shards[n_first:]]
    rest = _start_copies("gather_start_rest", placed, 3 * len(placed), _gather_half_copies(0, len(placed), True))
    relayed = {}

    def fetch(w, stage, after):
        if stage == "first":
            got = _wait_copies("gather_wait_first", first[2][:2], first[0], first[1], after, _gather_copies(0, 2))
        elif stage == "mixer_out":
            got = _wait_copies("gather_wait_mixer_out", first[2][2:], first[0], first[1], after, _gather_copies(2, 1))
        elif stage.endswith("_relay"):
            lo, hi = spans[stage[:-6]]
            relayed[stage[:-6]] = _relay_copies(
                f"gather_{stage}", rest[2][lo:hi], rest[0], rest[1], after,
                _gather_half_copies(lo, hi - lo, True), 3 * (hi - lo), _gather_half_copies(0, hi - lo, False))
            return w
        else:
            lo, hi = spans[stage]
            send_sems, recv_sems, bufs, _ = relayed[stage]
            got = _wait_copies(f"gather_wait_{stage}", bufs, send_sems, recv_sems, after, _gather_half_copies(0, hi - lo, False))
        w = dict(w)
        if stage == "first":
            g_small = got[0].reshape(N_CHIPS, -1)[:, :n_small]
            conv_w = g_small[:, 3 * ds:].reshape(N_CHIPS, 2, 3, fs).transpose(1, 2, 0, 3).reshape(2, 3, D_FF)
            w.update(
                hg_norm=g_small[:, :ds].reshape(1, D_MODEL),
                hg_lb=g_small[:, ds:3 * ds].reshape(N_CHIPS, 2, ds).transpose(1, 0, 2).reshape(2, D_MODEL),
                ffn_conv_w=[conv_w[0], conv_w[1]], hg_w_in=got[1],
            )
        elif stage == "mixer_out":
            w.update(hg_w_out=got[0].reshape(1, D_MODEL, D_MODEL))
        elif stage == "layer0":
            w.update(ffn_w_up=[got[0], None], ffn_w_down=[got[1].reshape(1, D_FF, D_MODEL), None])
        else:
            w.update(
                w_kv=got[0].reshape(1, D_MODEL, 2 * LANES), attn_w_q=got[1].reshape(1, D_MODEL, D_MODEL),
                attn_w_o=got[2].reshape(1, D_MODEL, D_MODEL), ffn_w_up=[w["ffn_w_up"][0], got[3]],
                ffn_w_down=[w["ffn_w_down"][0], got[4].reshape(1, D_FF, D_MODEL)],
            )
        return w

    whole = dict(
        hg_out_norm=hg_out_norm, kv_norm=kv_norm.reshape(1, D_MODEL), attn_norm=attn_norm, attn_sinks=attn_sinks.reshape(ATT_QH),
        ffn_norm=[ffn_norm[0:1], ffn_norm[1:2]], ffn_conv_b=[ffn_conv_b[0:1], ffn_conv_b[1:2]], final_norm=final_norm.reshape(1, D_MODEL),
    )
    whole = fetch(whole, "first", rest[3])

    red, layer1 = {}, {}

    def by_rows(g, rows):
        return g.reshape(N_CHIPS, rows, g.shape[2])

    def hook(point, dh, grads):
        if point == "ffn1":
            red["ffn1"] = _Reduction("ffn1", [by_rows(grads["ffn_w_down"], fs), grads["ffn_w_up"]], place_arr, core_arr)
            return (red["ffn1"].token,)
        if point == "attn":
            red["ffn1"].to_chips(dh)
            layer1.update(grads)
            return (red["ffn1"].token,)
        if point == "ffn0":
            group = [by_rows(layer1["attn_w_o"], ds), by_rows(layer1["attn_w_q"], ds), by_rows(layer1["w_kv"], ds),
                     by_rows(grads["ffn_w_down"], fs), grads["ffn_w_up"]]
            red["mid"] = _Reduction("mid", group, place_arr, core_arr)
            return (red["mid"].token,)
        if point == "hgrn":
            red["ffn1"].to_core(dh)
            red["mid"].to_chips(dh)
            return (red["ffn1"].token, red["mid"].token)
        red["hg"] = _Reduction("hg", [by_rows(grads["hg_w_out"], ds), grads["hg_w_in"]], place_arr, core_arr)
        return (red["hg"].token,)

    loss, dx, grads = _local_step(x[0], loss_target[0], whole, fetch, hook)

    small_parts = [
        loss.reshape(-1), grads["hg_out_norm"].reshape(-1), grads["attn_sinks"].reshape(-1), grads["kv_norm"].reshape(-1),
        grads["attn_norm"].reshape(-1), grads["ffn_norm"][0].reshape(-1), grads["ffn_norm"][1].reshape(-1),
        grads["ffn_conv_b"][0].reshape(-1), grads["ffn_conv_b"][1].reshape(-1), grads["final_norm"].reshape(-1),
        grads["hg_norm"].reshape(-1), grads["hg_lb"].reshape(-1), grads["ffn_conv_w"][0].reshape(-1), grads["ffn_conv_w"][1].reshape(-1),
    ]
    sizes = [p.shape[0] for p in small_parts]
    flat = jnp.concatenate(small_parts)
    rows = -(-flat.shape[0] // (SUBLANES * LANES)) * SUBLANES
    summed = _allreduce_small(_pad_rows(flat, rows, LANES)).reshape(-1)
    red["hg"].to_chips(summed)
    offs = [0]
    for sz in sizes:
        offs.append(offs[-1] + sz)
    sm = [summed[offs[i]:offs[i + 1]] for i in range(len(sizes))]
    loss_out = sm[0][0]
    conv_w_full = jnp.stack([sm[12].reshape(3, D_FF), sm[13].reshape(3, D_FF)])
    small_grads = dict(
        hg_out_norm=sm[1].reshape(1, HG_DK), attn_sinks=sm[2][:ATT_QH].reshape(1, ATT_QH), kv_norm=sm[3], attn_norm=sm[4].reshape(1, D_MODEL),
        ffn_norm=jnp.stack([sm[5], sm[6]]), ffn_conv_b=jnp.stack([sm[7], sm[8]]), final_norm=sm[9],
        hg_norm=lax.dynamic_slice(sm[10].reshape(1, D_MODEL), (0, chip * ds), (1, ds)),
        hg_lb_logits=lax.dynamic_slice(sm[11].reshape(2, D_MODEL), (0, chip * ds), (2, ds)),
        ffn_conv_w=lax.dynamic_slice(conv_w_full, (0, 0, chip * fs), (2, 3, fs)),
    )

    out_g, out_d, out_m, out_v = {}, {}, {}, {}

    def update(name, g2):
        shape = wts[name].shape
        d2, m2, v2 = _adamw(wts[name].reshape(g2.shape), mom1[name].reshape(g2.shape), mom2[name].reshape(g2.shape), g2, name=f"adamw_{name}")
        out_g[name], out_d[name], out_m[name], out_v[name] = g2.reshape(shape), d2.reshape(shape), m2.reshape(shape), v2.reshape(shape)
        return d2

    def update_layer(name, g2, layer, prev):
        res = _adamw_layer(wts[name], mom1[name], mom2[name], g2, layer, prev, name=f"adamw_{name}{layer}")
        out_g[name], out_d[name], out_m[name], out_v[name] = res
        return res

    g_down1, g_up1 = red["ffn1"].finish(red["hg"].token)
    down1 = update_layer("ffn_w_down", g_down1, 1, None)
    up1 = update_layer("ffn_w_up", g_up1, 1, None)
    red["mid"].to_core(up1[1])
    g_o, g_q, g_kv, g_down0, g_up0 = red["mid"].finish(up1[2])
    update("attn_w_o", g_o)
    update("attn_w_q", g_q)
    update("w_kv", g_kv)
    update_layer("ffn_w_down", g_down0, 0, down1)
    last = update_layer("ffn_w_up", g_up0, 0, up1)
    red["hg"].to_core(last[1])
    g_out, g_in = red["hg"].finish(last[2])
    update("hg_w_out", g_out)
    update("hg_w_in", g_in)

    small_names = [n for n in names if n not in out_g]
    cat = lambda d: jnp.concatenate([d[n].reshape(-1) for n in small_names])
    n_flat = sum(wts[n].size for n in small_names)
    srows = -(-n_flat // (SUBLANES * LANES)) * SUBLANES
    packed = [_pad_rows(cat(d), srows, LANES) for d in (wts, mom1, mom2, small_grads)]
    d_s, m_s, v_s = _adamw(*packed, name="adamw_small")
    off = 0
    for n in small_names:
        sz, shape = wts[n].size, wts[n].shape
        out_g[n] = small_grads[n].reshape(shape)
        out_d[n] = d_s.reshape(-1)[off:off + sz].reshape(shape)
        out_m[n] = m_s.reshape(-1)[off:off + sz].reshape(shape)
        out_v[n] = v_s.reshape(-1)[off:off + sz].reshape(shape)
        off += sz

    grad_x = dx.reshape(x.shape)
    return (loss_out, grad_x, *[out_g[n] for n in names], *[out_d[n] for n in names], *[out_m[n] for n in names], *[out_v[n] for n in names])
```

```python
import functools

import jax
import jax.numpy as jnp
from jax import lax
from jax.experimental import pallas as pl
from jax.experimental.pallas import tpu as pltpu

F32 = jnp.float32
BF16 = jnp.bfloat16
MESH = pl.DeviceIdType.MESH

EPS = 1e-6
D_MODEL = 1024
HG_HEADS = 8
HG_DK = 128
HG_CHUNK = 64
ATT_HD = 64
ATT_QH = 16
ATT_KVH = 2
ATT_GROUP = ATT_QH // ATT_KVH
WINDOW = 128
D_FF = 2816
N_CHIPS = 4
N_DEV = 8
LANES = 128
SUBLANES = 8
VMEM_LIMIT_BYTES = 56 * 1024 * 1024
NEG = -1e30
ALIBI_SLOPES = tuple(2.0 ** (-8.0 * h / ATT_QH) for h in range(1, ATT_QH + 1))

ADAM_LR = 0.001
ADAM_B1 = 0.9
ADAM_B2 = 0.999
ADAM_EPS = 1e-08
ADAM_WD = 0.01
ADAM_STEP = 10


def _cparams(sem=None):
    return pltpu.CompilerParams(dimension_semantics=sem, vmem_limit_bytes=VMEM_LIMIT_BYTES)


def _pick(n, cands):
    for c in cands:
        if n % c == 0:
            return c
    return n


def _sigmoid(x):
    return 0.5 * jnp.tanh(0.5 * x) + 0.5


def _dot(a, b, dims):
    return lax.dot_general(a, b, (dims, ((), ())), preferred_element_type=F32)


NN = ((1,), (0,))
NT = ((1,), (1,))
TN = ((0,), (0,))


MM_ROWS = 1024


def _mm_nn(a, w, res=None, out_dtype=F32, name="mm_nn"):
    m, k = a.shape
    s, _, ns = w.shape
    tm = min(m, MM_ROWS)
    tn = _pick(ns, (1024, 1408, 512, 256, 128))
    npb = ns // tn

    def body(a_ref, w_ref, *rest):
        o_ref = rest[-1]
        acc = _dot(a_ref[...].astype(BF16), w_ref[...], NN)
        if res is not None:
            acc = acc + rest[0][...]
        o_ref[...] = acc.astype(o_ref.dtype)

    in_specs = [
        pl.BlockSpec((tm, k), lambda i, j: (i, 0)),
        pl.BlockSpec((None, k, tn), lambda i, j: (j // npb, 0, j % npb)),
    ]
    args = [a, w]
    if res is not None:
        in_specs.append(pl.BlockSpec((tm, tn), lambda i, j: (i, j)))
        args.append(res)
    return pl.pallas_call(
        body,
        name=name,
        grid=(m // tm, s * npb),
        in_specs=in_specs,
        out_specs=pl.BlockSpec((tm, tn), lambda i, j: (i, j)),
        out_shape=jax.ShapeDtypeStruct((m, s * ns), out_dtype),
        compiler_params=_cparams(("parallel", "parallel")),
    )(*args)


def _dy_spec(stacked, tm, tn, npb, row, kk):
    if stacked:
        return pl.BlockSpec((None, tm, tn), lambda *g: (kk(g) // npb, row(g), kk(g) % npb))
    return pl.BlockSpec((tm, tn), lambda *g: (row(g), kk(g)))


def _dep_specs(deps):
    return [pl.BlockSpec(d.shape, lambda *g: (0, 0)) for d in deps]


def _mm_nt(dy, w, stacked=False, out_dtype=F32, name="mm_nt", deps=()):
    s, k, ns = w.shape
    m = dy.shape[1] if stacked else dy.shape[0]
    tm = min(m, MM_ROWS)
    tko = _pick(k, (1024, 1408, 512, 256))
    tn = _pick(ns, (1024, 1408, 512, 256))
    npb = ns // tn
    nk = s * npb

    def body(dy_ref, w_ref, *rest):
        o_ref, acc_ref = rest[-2:]
        kk = pl.program_id(2)

        @pl.when(kk == 0)
        def _():
            acc_ref[...] = jnp.zeros_like(acc_ref)

        acc_ref[...] += _dot(dy_ref[...].astype(BF16), w_ref[...], NT)

        @pl.when(kk == nk - 1)
        def _():
            o_ref[...] = acc_ref[...].astype(o_ref.dtype)

    return pl.pallas_call(
        body,
        name=name,
        grid=(m // tm, k // tko, nk),
        in_specs=[
            _dy_spec(stacked, tm, tn, npb, lambda g: g[0], lambda g: g[2]),
            pl.BlockSpec((None, tko, tn), lambda i, j, kk: (kk // npb, j, kk % npb)),
        ] + _dep_specs(deps),
        out_specs=pl.BlockSpec((tm, tko), lambda i, j, kk: (i, j)),
        out_shape=jax.ShapeDtypeStruct((m, k), out_dtype),
        scratch_shapes=[pltpu.VMEM((tm, tko), F32)],
        compiler_params=_cparams(("parallel", "parallel", "arbitrary")),
    )(dy, w, *deps)


def _mm_tn(a, dy, s, ns, stacked=False, name="mm_tn", deps=()):
    m, k = a.shape
    tm = min(m, MM_ROWS)
    tk = _pick(k, (1024, 1408, 512, 256))
    tn = _pick(ns, (1024, 1408, 512, 256, 128))
    npb = ns // tn
    nm = m // tm

    def body(a_ref, dy_ref, *rest):
        o_ref, acc_ref = rest[-2:]
        mm = pl.program_id(2)

        @pl.when(mm == 0)
        def _():
            acc_ref[...] = jnp.zeros_like(acc_ref)

        acc_ref[...] += _dot(a_ref[...].astype(BF16), dy_ref[...].astype(BF16), TN)

        @pl.when(mm == nm - 1)
        def _():
            o_ref[...] = acc_ref[...]

    return pl.pallas_call(
        body,
        name=name,
        grid=(k // tk, s * npb, nm),
        in_specs=[
            pl.BlockSpec((tm, tk), lambda i, j, mm: (mm, i)),
            _dy_spec(stacked, tm, tn, npb, lambda g: g[2], lambda g: g[1]),
        ] + _dep_specs(deps),
        out_specs=pl.BlockSpec((None, tk, tn), lambda i, j, mm: (j // npb, i, j % npb)),
        out_shape=jax.ShapeDtypeStruct((s, k, ns), F32),
        scratch_shapes=[pltpu.VMEM((tk, tn), F32)],
        compiler_params=_cparams(("parallel", "parallel", "arbitrary")),
    )(a, dy, *deps)


ROW_TILE = 512


def _rms_fwd(x, g, name="rms_fwd"):
    t, d = x.shape
    r = min(t, ROW_TILE)

    def body(x_ref, g_ref, o_ref):
        xv = x_ref[...]
        rstd = lax.rsqrt(jnp.mean(xv * xv, axis=-1, keepdims=True) + EPS)
        o_ref[...] = (xv * rstd * g_ref[...]).astype(BF16)

    return pl.pallas_call(
        body,
        name=name,
        grid=(t // r,),
        in_specs=[pl.BlockSpec((r, d), lambda i: (i, 0)), pl.BlockSpec((1, d), lambda i: (0, 0))],
        out_specs=pl.BlockSpec((r, d), lambda i: (i, 0)),
        out_shape=jax.ShapeDtypeStruct((t, d), BF16),
        compiler_params=_cparams(("parallel",)),
    )(x, g)


def _rms_bwd(x, g, dxn, dres, name="rms_bwd"):
    t, d = x.shape
    r = min(t, ROW_TILE)

    def body(x_ref, g_ref, dxn_ref, dres_ref, dx_ref, dg_ref):
        @pl.when(pl.program_id(0) == 0)
        def _():
            dg_ref[...] = jnp.zeros_like(dg_ref)

        xv = x_ref[...]
        rstd = lax.rsqrt(jnp.mean(xv * xv, axis=-1, keepdims=True) + EPS)
        xhat = xv * rstd
        dxn_v = dxn_ref[...].astype(F32)
        gd = dxn_v * g_ref[...]
        dx_ref[...] = dres_ref[...] + rstd * (gd - xhat * jnp.mean(gd * xhat, axis=-1, keepdims=True))
        dg_ref[...] += jnp.sum(dxn_v * xhat, axis=0, keepdims=True)

    return pl.pallas_call(
        body,
        name=name,
        grid=(t // r,),
        in_specs=[
            pl.BlockSpec((r, d), lambda i: (i, 0)),
            pl.BlockSpec((1, d), lambda i: (0, 0)),
            pl.BlockSpec((r, d), lambda i: (i, 0)),
            pl.BlockSpec((r, d), lambda i: (i, 0)),
        ],
        out_specs=[pl.BlockSpec((r, d), lambda i: (i, 0)), pl.BlockSpec((1, d), lambda i: (0, 0))],
        out_shape=[jax.ShapeDtypeStruct((t, d), F32), jax.ShapeDtypeStruct((1, d), F32)],
        compiler_params=_cparams(("arbitrary",)),
    )(x, g, dxn, dres)


def _loss_head(h, g, target):
    t, d = h.shape
    r = min(t, ROW_TILE)

    def body(h_ref, g_ref, t_ref, dh_ref, dg_ref, loss_ref):
        @pl.when(pl.program_id(0) == 0)
        def _():
            dg_ref[...] = jnp.zeros_like(dg_ref)
            loss_ref[...] = jnp.zeros_like(loss_ref)

        xv = h_ref[...]
        rstd = lax.rsqrt(jnp.mean(xv * xv, axis=-1, keepdims=True) + EPS)
        xhat = xv * rstd
        gv = g_ref[...]
        err = xhat * gv - t_ref[...]
        loss_ref[...] += 0.5 * jnp.sum(jnp.mean(err * err, axis=-1, keepdims=True), axis=0, keepdims=True)
        dy = err * (1.0 / d)
        gd = dy * gv
        dh_ref[...] = rstd * (gd - xhat * jnp.mean(gd * xhat, axis=-1, keepdims=True))
        dg_ref[...] += jnp.sum(dy * xhat, axis=0, keepdims=True)

    return pl.pallas_call(
        body,
        name="loss_head",
        grid=(t // r,),
        in_specs=[
            pl.BlockSpec((r, d), lambda i: (i, 0)),
            pl.BlockSpec((1, d), lambda i: (0, 0)),
            pl.BlockSpec((r, d), lambda i: (i, 0)),
        ],
        out_specs=[
            pl.BlockSpec((r, d), lambda i: (i, 0)),
            pl.BlockSpec((1, d), lambda i: (0, 0)),
            pl.BlockSpec((1, LANES), lambda i: (0, 0)),
        ],
        out_shape=[
            jax.ShapeDtypeStruct((t, d), F32),
            jax.ShapeDtypeStruct((1, d), F32),
            jax.ShapeDtypeStruct((1, LANES), F32),
        ],
        compiler_params=_cparams(("arbitrary",)),
    )(h, g, target)


CONV_ROWS = 256
CONV_COLS = 1408


def _conv_taps(x_ext, n):
    tot = x_ext.shape[0]
    g1 = pltpu.roll(x_ext, 1, 0)[tot - n:]
    g2 = pltpu.roll(x_ext, 2, 0)[tot - n:]
    return g2, g1


def _conv_fwd(up, conv_w, conv_b, name="conv_fwd"):
    t = up.shape[0]
    r = min(t, CONV_ROWS)
    tc = CONV_COLS
    ncb = D_FF // tc
    hb = r // SUBLANES

    def body(g_ref, halo_ref, v_ref, w_ref, b_ref, o_ref):
        i = pl.program_id(1)
        g0 = g_ref[...]
        halo = halo_ref[...] * jnp.where(i > 0, 1.0, 0.0)
        g2, g1 = _conv_taps(jnp.concatenate([halo, g0], axis=0), r)
        c = b_ref[...] + w_ref[0:1, :] * g2 + w_ref[1:2, :] * g1 + w_ref[2:3, :] * g0
        o_ref[...] = (c * _sigmoid(c) * v_ref[...]).astype(BF16)

    return pl.pallas_call(
        body,
        name=name,
        grid=(ncb, t // r),
        in_specs=[
            pl.BlockSpec((r, tc), lambda j, i: (i, j)),
            pl.BlockSpec((SUBLANES, tc), lambda j, i: (jnp.maximum(i * hb - 1, 0), j)),
            pl.BlockSpec((r, tc), lambda j, i: (i, ncb + j)),
            pl.BlockSpec((3, tc), lambda j, i: (0, j)),
            pl.BlockSpec((1, tc), lambda j, i: (0, j)),
        ],
        out_specs=pl.BlockSpec((r, tc), lambda j, i: (i, j)),
        out_shape=jax.ShapeDtypeStruct((t, D_FF), BF16),
        compiler_params=_cparams(("parallel", "parallel")),
    )(up, up, up, conv_w, conv_b)


def _conv_bwd(up, conv_w, conv_b, dact, name="conv_bwd"):
    t = up.shape[0]
    r = min(t, CONV_ROWS)
    tc = CONV_COLS
    ncb = D_FF // tc
    hb = r // SUBLANES
    nrt = t // r

    def body(g_ref, halo_ref, v_ref, w_ref, b_ref, da_ref, dup_ref, dw_ref, db_ref, nxt_ref):
        ii = pl.program_id(1)
        i = nrt - 1 - ii

        @pl.when(ii == 0)
        def _():
            nxt_ref[...] = jnp.zeros_like(nxt_ref)
            dw_ref[...] = jnp.zeros_like(dw_ref)
            db_ref[...] = jnp.zeros_like(db_ref)

        g0 = g_ref[...]
        halo = halo_ref[...] * jnp.where(i > 0, 1.0, 0.0)
        g2, g1 = _conv_taps(jnp.concatenate([halo, g0], axis=0), r)
        w0, w1, w2 = w_ref[0:1, :], w_ref[1:2, :], w_ref[2:3, :]
        c = b_ref[...] + w0 * g2 + w1 * g1 + w2 * g0
        sg = _sigmoid(c)
        da = da_ref[...]
        dval = da * (c * sg)
        dc = da * v_ref[...] * (sg * (1.0 + c * (1.0 - sg)))
        db_ref[...] += jnp.sum(dc, axis=0, keepdims=True)
        dw_ref[0:1, :] += jnp.sum(dc * g2, axis=0, keepdims=True)
        dw_ref[1:2, :] += jnp.sum(dc * g1, axis=0, keepdims=True)
        dw_ref[2:3, :] += jnp.sum(dc * g0, axis=0, keepdims=True)
        ext = jnp.concatenate([dc, nxt_ref[...]], axis=0)
        tot = r + SUBLANES
        d1 = pltpu.roll(ext, tot - 1, 0)[:r]
        d2 = pltpu.roll(ext, tot - 2, 0)[:r]
        dgate = w2 * dc + w1 * d1 + w0 * d2
        nxt_ref[...] = dc[:SUBLANES]
        dup_ref[0] = dgate.astype(BF16)
        dup_ref[1] = dval.astype(BF16)

    rev = lambda ii: nrt - 1 - ii
    dup, dw, db = pl.pallas_call(
        body,
        name=name,
        grid=(ncb, nrt),
        in_specs=[
            pl.BlockSpec((r, tc), lambda j, ii: (rev(ii), j)),
            pl.BlockSpec((SUBLANES, tc), lambda j, ii: (jnp.maximum(rev(ii) * hb - 1, 0), j)),
            pl.BlockSpec((r, tc), lambda j, ii: (rev(ii), ncb + j)),
            pl.BlockSpec((3, tc), lambda j, ii: (0, j)),
            pl.BlockSpec((1, tc), lambda j, ii: (0, j)),
            pl.BlockSpec((r, tc), lambda j, ii: (rev(ii), j)),
        ],
        out_specs=[
            pl.BlockSpec((2, None, r, tc), lambda j, ii: (0, j, rev(ii), 0)),
            pl.BlockSpec((3, tc), lambda j, ii: (0, j)),
            pl.BlockSpec((1, tc), lambda j, ii: (0, j)),
        ],
        out_shape=[
            jax.ShapeDtypeStruct((2, ncb, t, tc), BF16),
            jax.ShapeDtypeStruct((3, D_FF), F32),
            jax.ShapeDtypeStruct((1, D_FF), F32),
        ],
        scratch_shapes=[pltpu.VMEM((SUBLANES, tc), F32)],
        compiler_params=_cparams(("parallel", "arbitrary")),
    )(up, up, up, conv_w, conv_b, dact)
    return dup.reshape(2 * ncb, t, tc), dw, db


def _split3(x):
    x1 = x.astype(BF16)
    r1 = x - x1.astype(F32)
    x2 = r1.astype(BF16)
    x3 = (r1 - x2.astype(F32)).astype(BF16)
    return x1, x2, x3


def _tri_dot(tri, x, dims):
    x1, x2, x3 = _split3(x)
    return _dot(tri, x1, dims) + _dot(tri, x2, dims) + _dot(tri, x3, dims)


def _lower_bound(logits_ref):
    return _sigmoid(logits_ref[0:1, :] - logits_ref[1:2, :])


def _hg_gates(qr, fr, lb):
    q = qr * _sigmoid(qr) * (HG_DK ** -0.5)
    sf = _sigmoid(fr)
    fg = lb + (1.0 - lb) * sf
    return q, sf, fg


def _hg_chunk_terms(q, fg, tril_b, low_half):
    g = jnp.log(fg)
    k = 1.0 - fg
    cum = _tri_dot(tril_b, g, NN)
    c_last = jnp.sum(g, axis=0, keepdims=True)
    c_mid = jnp.sum(jnp.where(low_half, g, 0.0), axis=0, keepdims=True)
    e_q = jnp.exp(cum - c_mid)
    e_k = jnp.exp(c_mid - cum)
    e_0 = jnp.exp(cum)
    e_l = jnp.exp(c_last - cum)
    return k, e_q, e_k, e_0, e_l, jnp.exp(c_last)


HG_BLOCK = 256


def _hg_proj_specs(rb, row):
    return [pl.BlockSpec((rb, D_MODEL), functools.partial(lambda i, k: (row(i), k), k=k)) for k in range(4)]


def _hg_consts(c):
    tril = lax.broadcasted_iota(jnp.int32, (c, c), 0) >= lax.broadcasted_iota(jnp.int32, (c, c), 1)
    low_half = lax.broadcasted_iota(jnp.int32, (c, HG_DK), 0) < c // 2
    return tril, tril.astype(BF16), low_half


def _hgrn_fwd(proj, lb, wn):
    t = proj.shape[0]
    c = HG_CHUNK
    rb = min(t, HG_BLOCK)
    cpb = rb // c

    def body(q_ref, f_ref, i_ref, g_ref, lb_ref, wn_ref, o_ref, y_ref, st_ref, s_scr):
        @pl.when(pl.program_id(0) == 0)
        def _():
            s_scr[...] = jnp.zeros_like(s_scr)

        lb_all = _lower_bound(lb_ref)
        wnv = wn_ref[...]
        tril, tril_b, low_half = _hg_consts(c)

        def chunk(n, carry):
            rows = pl.ds(pl.multiple_of(n * c, c), c)
            for h in range(HG_HEADS):
                cols = slice(h * HG_DK, (h + 1) * HG_DK)
                q, _, fg = _hg_gates(q_ref[rows, cols], f_ref[rows, cols], lb_all[:, cols])
                v = i_ref[rows, cols].astype(BF16)
                k, e_q, e_k, e_0, e_l, e_last = _hg_chunk_terms(q, fg, tril_b, low_half)
                st = s_scr[h]
                st_ref[h, n] = st
                a = jnp.where(tril, _dot((q * e_q).astype(BF16), (k * e_k).astype(BF16), NT), 0.0)
                o = _dot((q * e_0).astype(BF16), st.astype(BF16), NT) + _dot(a.astype(BF16), v, NN)
                s_scr[h] = st * e_last + _dot(v, (k * e_l).astype(BF16), TN)
                o_ref[rows, cols] = o
                rstd = lax.rsqrt(jnp.mean(o * o, axis=-1, keepdims=True) + EPS)
                gr = g_ref[rows, cols]
                y_ref[rows, cols] = (o * rstd * wnv * (gr * _sigmoid(gr))).astype(BF16)
            return carry

        lax.fori_loop(0, cpb, chunk, 0)

    blk = pl.BlockSpec((rb, D_MODEL), lambda i: (i, 0))
    return pl.pallas_call(
        body,
        name="hgrn_fwd",
        grid=(t // rb,),
        in_specs=_hg_proj_specs(rb, lambda i: i) + [pl.BlockSpec((2, D_MODEL), lambda i: (0, 0)), pl.BlockSpec((1, HG_DK), lambda i: (0, 0))],
        out_specs=[blk, blk, pl.BlockSpec((HG_HEADS, cpb, HG_DK, HG_DK), lambda i: (0, i, 0, 0))],
        out_shape=[
            jax.ShapeDtypeStruct((t, D_MODEL), F32),
            jax.ShapeDtypeStruct((t, D_MODEL), BF16),
            jax.ShapeDtypeStruct((HG_HEADS, t // c, HG_DK, HG_DK), F32),
        ],
        scratch_shapes=[pltpu.VMEM((HG_HEADS, HG_DK, HG_DK), F32)],
        compiler_params=_cparams(("arbitrary",)),
    )(proj, proj, proj, proj, lb, wn)


def _hgrn_bwd(proj, lb, wn, o, states, dy):
    t = proj.shape[0]
    c = HG_CHUNK
    rb = min(t, HG_BLOCK)
    cpb = rb // c
    nb = t // rb

    def body(q_ref, f_ref, i_ref, g_ref, lb_ref, wn_ref, o_ref, st_ref, dy_ref, dp_ref, dl_ref, dwn_ref, ds_scr, dlb_scr):
        step = pl.program_id(0)

        @pl.when(step == 0)
        def _():
            dwn_ref[...] = jnp.zeros_like(dwn_ref)
            ds_scr[...] = jnp.zeros_like(ds_scr)
            dlb_scr[...] = jnp.zeros_like(dlb_scr)

        lb_all = _lower_bound(lb_ref)
        wnv = wn_ref[...]
        tril, tril_b, low_half = _hg_consts(c)

        def chunk(nn, carry):
            n = cpb - 1 - nn
            rows = pl.ds(pl.multiple_of(n * c, c), c)
            for h in range(HG_HEADS):
                cols = slice(h * HG_DK, (h + 1) * HG_DK)
                lbv = lb_all[:, cols]
                ov = o_ref[rows, cols]
                gr = g_ref[rows, cols]
                dyv = dy_ref[rows, cols].astype(F32)
                rstd = lax.rsqrt(jnp.mean(ov * ov, axis=-1, keepdims=True) + EPS)
                ohat = ov * rstd
                sg = _sigmoid(gr)
                dg_raw = dyv * (ohat * wnv) * (sg * (1.0 + gr * (1.0 - sg)))
                don = dyv * (gr * sg)
                dwn_ref[...] += jnp.sum(don * ohat, axis=0, keepdims=True)
                gd = don * wnv
                do = rstd * (gd - ohat * jnp.mean(gd * ohat, axis=-1, keepdims=True))
                do_b = do.astype(BF16)
                qr = q_ref[rows, cols]
                q, sf, fg = _hg_gates(qr, f_ref[rows, cols], lbv)
                v = i_ref[rows, cols].astype(BF16)
                k, e_q, e_k, e_0, e_l, e_last = _hg_chunk_terms(q, fg, tril_b, low_half)
                qi, qi_lo, _ = _split3(q * e_q)
                ki, ki_lo, _ = _split3(k * e_k)
                q0 = (q * e_0).astype(BF16)
                kl = (k * e_l).astype(BF16)
                st = st_ref[h, n]
                st_b = st.astype(BF16)
                ds = ds_scr[h]
                ds_b = ds.astype(BF16)
                a_b = jnp.where(tril, _dot(qi, ki, NT), 0.0).astype(BF16)
                da_b = jnp.where(tril, _dot(do_b, v, NT), 0.0).astype(BF16)
                dq = _dot(do_b, st_b, NN) * e_0 + (_dot(da_b, ki, NN) + _dot(da_b, ki_lo, NN)) * e_q
                dk_state = _dot(v, ds_b, NN) * e_l
                dk = (_dot(da_b, qi, TN) + _dot(da_b, qi_lo, TN)) * e_k + dk_state
                dv = _dot(a_b, do_b, TN) + _dot(kl, ds_b, NT)
                ds_scr[h] = ds * e_last + _dot(do_b, q0, TN)
                d_last = jnp.sum(dk_state * k, axis=0, keepdims=True) + jnp.sum(ds * st, axis=0, keepdims=True) * e_last
                dlogf = _tri_dot(tril_b, q * dq - k * dk, TN) + d_last
                dfg = dlogf / fg - dk
                dlb_scr[:, cols] += jnp.sum(dfg * (1.0 - sf), axis=0, keepdims=True)
                sq = _sigmoid(qr)
                dp_ref[0, rows, cols] = (dq * (HG_DK ** -0.5) * (sq * (1.0 + qr * (1.0 - sq)))).astype(BF16)
                dp_ref[1, rows, cols] = (dfg * (1.0 - lbv) * sf * (1.0 - sf)).astype(BF16)
                dp_ref[2, rows, cols] = dv.astype(BF16)
                dp_ref[3, rows, cols] = dg_raw.astype(BF16)
            return carry

        lax.fori_loop(0, cpb, chunk, 0)

        @pl.when(step == nb - 1)
        def _():
            d0 = dlb_scr[...] * lb_all * (1.0 - lb_all)
            dl_ref[0:1, :] = d0
            dl_ref[1:2, :] = -d0

    rev = lambda i: nb - 1 - i
    blk = pl.BlockSpec((rb, D_MODEL), lambda i: (rev(i), 0))
    return pl.pallas_call(
        body,
        name="hgrn_bwd",
        grid=(nb,),
        in_specs=_hg_proj_specs(rb, rev)
        + [pl.BlockSpec((2, D_MODEL), lambda i: (0, 0)), pl.BlockSpec((1, HG_DK), lambda i: (0, 0)), blk,
           pl.BlockSpec((HG_HEADS, cpb, HG_DK, HG_DK), lambda i: (0, rev(i), 0, 0)), blk],
        out_specs=[
            pl.BlockSpec((4, rb, D_MODEL), lambda i: (0, rev(i), 0)),
            pl.BlockSpec((2, D_MODEL), lambda i: (0, 0)),
            pl.BlockSpec((1, HG_DK), lambda i: (0, 0)),
        ],
        out_shape=[
            jax.ShapeDtypeStruct((4, t, D_MODEL), BF16),
            jax.ShapeDtypeStruct((2, D_MODEL), F32),
            jax.ShapeDtypeStruct((1, HG_DK), F32),
        ],
        scratch_shapes=[pltpu.VMEM((HG_HEADS, HG_DK, HG_DK), F32), pltpu.VMEM((1, D_MODEL), F32)],
        compiler_params=_cparams(("arbitrary",)),
    )(proj, proj, proj, proj, lb, wn, o, states, dy)


ATT_STACK = 8


def _att_stack(q_ref, sink_ref, first, lo, bias_p, bias_c, extra_ref=None):
    qs, bps, bcs, sinks, extras = [], [], [], None, []
    rows = lax.broadcasted_iota(jnp.int32, (ATT_STACK * WINDOW, 1), 0)
    for i in range(ATT_STACK):
        hq = first + i
        cols = slice((hq // 2) * LANES, (hq // 2 + 1) * LANES)
        sel = lo if hq % 2 == 0 else jnp.logical_not(lo)
        qp = q_ref[:, cols] * (ATT_HD ** -0.5)
        qs.append(jnp.where(sel, qp, jnp.zeros_like(qp)))
        bps.append(ALIBI_SLOPES[hq] * bias_p)
        bcs.append(ALIBI_SLOPES[hq] * bias_c)
        sinks = sink_ref[hq] if sinks is None else jnp.where(rows < i * WINDOW, sinks, sink_ref[hq])
        if extra_ref is not None:
            ep = extra_ref[:, cols]
            extras.append(jnp.where(sel, ep, jnp.zeros_like(ep)))
    cat = lambda parts: jnp.concatenate(parts, axis=0)
    return cat(qs), cat(bps), cat(bcs), sinks, (cat(extras) if extras else None)


def _att_rows(i):
    return slice(i * WINDOW, (i + 1) * WINDOW)


def _att_bias(n):
    tq = lax.broadcasted_iota(jnp.int32, (WINDOW, WINDOW), 0)
    sk = lax.broadcasted_iota(jnp.int32, (WINDOW, WINDOW), 1)
    valid_c = sk <= tq
    valid_p = (sk - tq) > jnp.where(n > 0, 0, WINDOW)
    dist_c = (tq - sk).astype(F32)
    return jnp.where(valid_p, -dist_c - float(WINDOW), NEG), jnp.where(valid_c, -dist_c, NEG)


def _att_halves(x, lo, kh):
    r = pltpu.roll(x, ATT_HD, 1)
    zero = jnp.zeros_like(x)
    if kh == 0:
        return jnp.where(lo, x, r), jnp.where(lo, x, zero), jnp.where(lo, zero, r)
    return jnp.where(lo, r, x), jnp.where(lo, r, zero), jnp.where(lo, zero, x)


def _att_probs(qm, k2p, k2c, bias_p, bias_c, sink):
    sp = _dot(qm, k2p, NT) + bias_p
    sc = _dot(qm, k2c, NT) + bias_c
    m = jnp.maximum(jnp.maximum(jnp.max(sp, axis=-1, keepdims=True), jnp.max(sc, axis=-1, keepdims=True)), sink)
    ep = jnp.exp(sp - m)
    ec = jnp.exp(sc - m)
    es = jnp.exp(sink - m)
    inv = 1.0 / (jnp.sum(ep, axis=-1, keepdims=True) + jnp.sum(ec, axis=-1, keepdims=True) + es)
    return ep * inv, ec * inv, es * inv


def _attn_fwd(q, kv, sinks):
    t = q.shape[0]
    nb = t // WINDOW

    def body(sink_ref, q_ref, kvp_ref, kvc_ref, o_ref):
        n = pl.program_id(0)
        bias_p, bias_c = _att_bias(n)
        lo = lax.broadcasted_iota(jnp.int32, (WINDOW, LANES), 1) < ATT_HD
        for kh in range(ATT_KVH):
            k2p, _, _ = _att_halves(kvp_ref[:, 0:LANES], lo, kh)
            k2c, _, _ = _att_halves(kvc_ref[:, 0:LANES], lo, kh)
            _, vlo_p, vhi_p = _att_halves(kvp_ref[:, LANES:2 * LANES], lo, kh)
            _, vlo_c, vhi_c = _att_halves(kvc_ref[:, LANES:2 * LANES], lo, kh)
            for first in range(kh * ATT_GROUP, (kh + 1) * ATT_GROUP, ATT_STACK):
                qs, bp, bc, sinks, _ = _att_stack(q_ref, sink_ref, first, lo, bias_p, bias_c)
                pp, pc, _ = _att_probs(qs, k2p, k2c, bp, bc, sinks)
                pp, pc = pp.astype(BF16), pc.astype(BF16)
                for i in range(0, ATT_STACK, 2):
                    even, odd = _att_rows(i), _att_rows(i + 1)
                    out = (_dot(pp[even], vlo_p, NN) + _dot(pc[even], vlo_c, NN)
                           + _dot(pp[odd], vhi_p, NN) + _dot(pc[odd], vhi_c, NN))
                    j = (first + i) // 2
                    o_ref[:, j * LANES:(j + 1) * LANES] = out.astype(BF16)

    return pl.pallas_call(
        body,
        name="attn_fwd",
        grid=(nb,),
        in_specs=[
            pl.BlockSpec(memory_space=pltpu.SMEM),
            pl.BlockSpec((WINDOW, D_MODEL), lambda n: (n, 0)),
            pl.BlockSpec((WINDOW, 2 * LANES), lambda n: (jnp.maximum(n - 1, 0), 0)),
            pl.BlockSpec((WINDOW, 2 * LANES), lambda n: (n, 0)),
        ],
        out_specs=pl.BlockSpec((WINDOW, D_MODEL), lambda n: (n, 0)),
        out_shape=jax.ShapeDtypeStruct((t, D_MODEL), BF16),
        compiler_params=_cparams(("parallel",)),
    )(sinks, q, kv, kv)


def _attn_bwd(q, kv, sinks, dout):
    t = q.shape[0]
    nb = t // WINDOW

    def body(sink_ref, q_ref, kvp_ref, kvc_ref, do_ref, dq_ref, dkv_ref, dsink_ref, carry_ref):
        n = pl.program_id(0)

        @pl.when(n == 0)
        def _():
            carry_ref[...] = jnp.zeros_like(carry_ref)
            dsink_ref[...] = jnp.zeros_like(dsink_ref)

        @pl.when(n == nb)
        def _():
            dkv_ref[...] = carry_ref[...].astype(BF16)

        @pl.when(n < nb)
        def _():
            bias_p, bias_c = _att_bias(n)
            lo = lax.broadcasted_iota(jnp.int32, (WINDOW, LANES), 1) < ATT_HD
            lane1 = lax.broadcasted_iota(jnp.int32, (1, LANES), 1)
            dsink = jnp.zeros((1, LANES), F32)
            halves = []
            for kh in range(ATT_KVH):
                k2p, klo_p, khi_p = _att_halves(kvp_ref[:, 0:LANES], lo, kh)
                k2c, klo_c, khi_c = _att_halves(kvc_ref[:, 0:LANES], lo, kh)
                v2p, _, _ = _att_halves(kvp_ref[:, LANES:2 * LANES], lo, kh)
                v2c, _, _ = _att_halves(kvc_ref[:, LANES:2 * LANES], lo, kh)
                acc = [jnp.zeros((WINDOW, LANES), F32) for _ in range(4)]
                for first in range(kh * ATT_GROUP, (kh + 1) * ATT_GROUP, ATT_STACK):
                    qs, bp, bc, sinks, dos = _att_stack(q_ref, sink_ref, first, lo, bias_p, bias_c, do_ref)
                    pp, pc, ps = _att_probs(qs, k2p, k2c, bp, bc, sinks)
                    dpp = _dot(dos, v2p, NT)
                    dpc = _dot(dos, v2c, NT)
                    delta = jnp.sum(pp * dpp, axis=-1, keepdims=True) + jnp.sum(pc * dpc, axis=-1, keepdims=True)
                    dsp = (pp * (dpp - delta)).astype(BF16)
                    dsc = (pc * (dpc - delta)).astype(BF16)
                    sink_term = ps * delta
                    for i in range(ATT_STACK):
                        dsink = dsink + jnp.where(lane1 == first + i, -jnp.sum(sink_term[_att_rows(i)], axis=0, keepdims=True), 0.0)
                    for i in range(0, ATT_STACK, 2):
                        even, odd = _att_rows(i), _att_rows(i + 1)
                        dq_pair = (_dot(dsp[even], klo_p, NN) + _dot(dsc[even], klo_c, NN)
                                   + _dot(dsp[odd], khi_p, NN) + _dot(dsc[odd], khi_c, NN))
                        j = (first + i) // 2
                        dq_ref[:, j * LANES:(j + 1) * LANES] = (dq_pair * (ATT_HD ** -0.5)).astype(BF16)
                    acc[0] = acc[0] + _dot(dsp, qs, TN)
                    acc[1] = acc[1] + _dot(dsc, qs, TN)
                    acc[2] = acc[2] + _dot(pp.astype(BF16), dos, TN)
                    acc[3] = acc[3] + _dot(pc.astype(BF16), dos, TN)
                halves.append([a + pltpu.roll(a, ATT_HD, 1) for a in acc])
            prev = jnp.concatenate(
                [jnp.where(lo, halves[0][0], halves[1][0]), jnp.where(lo, halves[0][2], halves[1][2])], axis=1)
            cur = jnp.concatenate(
                [jnp.where(lo, halves[0][1], halves[1][1]), jnp.where(lo, halves[0][3], halves[1][3])], axis=1)
            dkv_ref[...] = (carry_ref[...] + prev).astype(BF16)
            carry_ref[...] = cur
            dsink_ref[...] += dsink

    blk = lambda n: jnp.minimum(n, nb - 1)
    return pl.pallas_call(
        body,
        name="attn_bwd",
        grid=(nb + 1,),
        in_specs=[
            pl.BlockSpec(memory_space=pltpu.SMEM),
            pl.BlockSpec((WINDOW, D_MODEL), lambda n: (blk(n), 0)),
            pl.BlockSpec((WINDOW, 2 * LANES), lambda n: (jnp.maximum(blk(n) - 1, 0), 0)),
            pl.BlockSpec((WINDOW, 2 * LANES), lambda n: (blk(n), 0)),
            pl.BlockSpec((WINDOW, D_MODEL), lambda n: (blk(n), 0)),
        ],
        out_specs=[
            pl.BlockSpec((WINDOW, D_MODEL), lambda n: (blk(n), 0)),
            pl.BlockSpec((WINDOW, 2 * LANES), lambda n: (jnp.maximum(n - 1, 0), 0)),
            pl.BlockSpec((1, LANES), lambda n: (0, 0)),
        ],
        out_shape=[
            jax.ShapeDtypeStruct((t, D_MODEL), BF16),
            jax.ShapeDtypeStruct((t, 2 * LANES), BF16),
            jax.ShapeDtypeStruct((1, LANES), F32),
        ],
        scratch_shapes=[pltpu.VMEM((WINDOW, 2 * LANES), F32)],
        compiler_params=_cparams(("arbitrary",)),
    )(sinks, q, kv, kv, dout)


def _ffn_fwd(h, norm_g, w_up, conv_w, conv_b, w_down, tag, after_up=lambda up: None):
    xn = _rms_fwd(h, norm_g, name=f"ffn{tag}_norm")
    up = _mm_nn(xn, w_up, name=f"ffn{tag}_up")
    after_up(up)
    act = _conv_fwd(up, conv_w, conv_b, name=f"ffn{tag}_conv")
    h_out = _mm_nn(act, w_down, res=h, name=f"ffn{tag}_down")
    return h_out, (xn, up, act)


def _ffn_bwd(dh, h, norm_g, w_up, conv_w, conv_b, w_down, saved, tag, deps=()):
    xn, up, act = saved
    dw_down = _mm_tn(act, dh, 1, D_MODEL, name=f"ffn{tag}_dwdown", deps=deps)
    dact = _mm_nt(dh, w_down, name=f"ffn{tag}_dact", deps=deps)
    dup, dconv_w, dconv_b = _conv_bwd(up, conv_w, conv_b, dact, name=f"ffn{tag}_dconv")
    dw_up = _mm_tn(xn, dup, N_CHIPS, CONV_COLS, stacked=True, name=f"ffn{tag}_dwup")
    dxn = _mm_nt(dup, w_up, stacked=True, name=f"ffn{tag}_dxn")
    dh_in, dnorm = _rms_bwd(h, norm_g, dxn, dh, name=f"ffn{tag}_dnorm")
    return dh_in, dict(ffn_w_down=dw_down, ffn_w_up=dw_up, ffn_conv_w=dconv_w, ffn_conv_b=dconv_b, ffn_norm=dnorm)


def _local_step(x, target, w, fetch=lambda w, stage, after: w, hook=lambda point, dh, grads: ()):
    xn0 = _rms_fwd(x, w["hg_norm"], name="hg_norm")
    proj = _mm_nn(xn0, w["hg_w_in"], name="hg_in")
    o, y, states = _hgrn_fwd(proj, w["hg_lb"], w["hg_out_norm"])
    w = fetch(w, "mixer_out", y)
    fetch(w, "layer0_relay", y)
    h_a = _mm_nn(y, w["hg_w_out"], res=x, name="hg_out")
    w = fetch(w, "layer0", h_a)
    h1, ffn0 = _ffn_fwd(h_a, w["ffn_norm"][0], w["ffn_w_up"][0], w["ffn_conv_w"][0], w["ffn_conv_b"][0], w["ffn_w_down"][0], 0,
                        lambda up: fetch(w, "layer1_relay", up))
    w = fetch(w, "layer1", h1)
    kvn = _rms_fwd(h1, w["kv_norm"], name="kv_norm")
    kv = _mm_nn(kvn, w["w_kv"], out_dtype=BF16, name="kv_proj")
    xa = _rms_fwd(h1, w["attn_norm"], name="attn_norm")
    qa = _mm_nn(xa, w["attn_w_q"], out_dtype=BF16, name="attn_q")
    ao = _attn_fwd(qa, kv, w["attn_sinks"])
    h_b = _mm_nn(ao, w["attn_w_o"], res=h1, name="attn_o")
    h2, ffn1 = _ffn_fwd(h_b, w["ffn_norm"][1], w["ffn_w_up"][1], w["ffn_conv_w"][1], w["ffn_conv_b"][1], w["ffn_w_down"][1], 1)
    dh2, d_final, loss = _loss_head(h2, w["final_norm"], target)

    dh_b, g1 = _ffn_bwd(dh2, h_b, w["ffn_norm"][1], w["ffn_w_up"][1], w["ffn_conv_w"][1], w["ffn_conv_b"][1], w["ffn_w_down"][1], ffn1, 1)
    deps = hook("ffn1", dh_b, g1)
    dw_o = _mm_tn(ao, dh_b, 1, D_MODEL, name="attn_dwo", deps=deps)
    dao = _mm_nt(dh_b, w["attn_w_o"], out_dtype=BF16, name="attn_dao", deps=deps)
    dqa, dkv, dsinks = _attn_bwd(qa, kv, w["attn_sinks"], dao)
    dw_q = _mm_tn(xa, dqa, 1, D_MODEL, name="attn_dwq")
    dxa = _mm_nt(dqa, w["attn_w_q"], name="attn_dxa")
    dh1, d_attn_norm = _rms_bwd(h1, w["attn_norm"], dxa, dh_b, name="attn_dnorm")
    dw_kv = _mm_tn(kvn, dkv, 1, 2 * LANES, name="kv_dw")
    dkvn = _mm_nt(dkv, w["w_kv"], name="kv_dx")
    dh1, d_kv_norm = _rms_bwd(h1, w["kv_norm"], dkvn, dh1, name="kv_dnorm")
    deps = hook("attn", dh1, dict(attn_w_o=dw_o, attn_w_q=dw_q, w_kv=dw_kv))
    dh_a, g0 = _ffn_bwd(dh1, h_a, w["ffn_norm"][0], w["ffn_w_up"][0], w["ffn_conv_w"][0], w["ffn_conv_b"][0], w["ffn_w_down"][0], ffn0, 0, deps)
    deps = hook("ffn0", dh_a, g0)
    dw_out = _mm_tn(y, dh_a, 1, D_MODEL, name="hg_dwout", deps=deps)
    dy = _mm_nt(dh_a, w["hg_w_out"], out_dtype=BF16, name="hg_dy", deps=deps)
    dproj, dlb, d_out_norm = _hgrn_bwd(proj, w["hg_lb"], w["hg_out_norm"], o, states, dy)
    deps = hook("hgrn", dproj, None)
    dw_in = _mm_tn(xn0, dproj, N_CHIPS, D_MODEL, stacked=True, name="hg_dwin", deps=deps)
    deps = hook("hg_w", dproj, dict(hg_w_out=dw_out, hg_w_in=dw_in))
    dxn0 = _mm_nt(dproj, w["hg_w_in"], stacked=True, name="hg_dxn", deps=deps)
    dx, d_hg_norm = _rms_bwd(x, w["hg_norm"], dxn0, dh_a, name="hg_dnorm")

    grads = dict(
        hg_norm=d_hg_norm, hg_w_in=dw_in, hg_lb=dlb, hg_out_norm=d_out_norm, hg_w_out=dw_out,
        kv_norm=d_kv_norm, w_kv=dw_kv, attn_norm=d_attn_norm, attn_w_q=dw_q, attn_sinks=dsinks, attn_w_o=dw_o,
        final_norm=d_final,
    )
    for name in g0:
        grads[name] = [g0[name], g1[name]]
    return loss, dx, grads


ANY = pl.BlockSpec(memory_space=pl.ANY)


def _place():
    x, y, c = lax.axis_index("x"), lax.axis_index("y"), lax.axis_index("c")
    chips = [(1 - x, y), (x, 1 - y), (1 - x, 1 - y)]
    return x, y, c, chips


def _rcopy(src, dst, send_sem, recv_sem, to):
    return pltpu.make_async_remote_copy(src_ref=src, dst_ref=dst, send_sem=send_sem, recv_sem=recv_sem, device_id=to, device_id_type=MESH)


HBM = pl.BlockSpec(memory_space=pltpu.HBM)
SEM = pl.BlockSpec(memory_space=pltpu.SEMAPHORE)
EFFECT = pltpu.SideEffectType.DATAFLOW_SIDE_EFFECTING


def _in_hbm(a):
    return pltpu.with_memory_space_constraint(a, pltpu.HBM)


def _place_shard(shard, place, dtype, name, deps=()):
    r, cols = shard.shape
    tr = _pick(r, ELEM_ROWS)

    def body(place_ref, s_ref, *rest):
        o_ref = rest[-1]
        o_ref[...] = s_ref[...].astype(o_ref.dtype)

    return pl.pallas_call(
        body,
        name=name,
        grid_spec=pltpu.PrefetchScalarGridSpec(
            num_scalar_prefetch=1,
            grid=(r // tr,),
            in_specs=[pl.BlockSpec((tr, cols), lambda i, place_ref: (i, 0))] + _dep_specs(deps),
            out_specs=pl.BlockSpec((None, tr, cols), lambda i, place_ref: (place_ref[0], i, 0)),
        ),
        out_shape=jax.ShapeDtypeStruct((N_CHIPS, r, cols), dtype),
        compiler_params=_cparams(("parallel",)),
    )(place, shard, *deps)


def _start_copies(name, bufs, n_sem, copies):
    n = len(bufs)

    def body(*refs):
        for cp in copies(refs[:n], refs[n], refs[n + 1]):
            cp.start()
        refs[-1][...] = jnp.zeros_like(refs[-1])

    outs = pl.pallas_call(
        body,
        name=name,
        in_specs=[HBM] * n,
        out_specs=[SEM, SEM] + [HBM] * n + [pl.BlockSpec(memory_space=pltpu.VMEM)],
        out_shape=[pltpu.SemaphoreType.DMA((n_sem,)), pltpu.SemaphoreType.DMA((n_sem,))] + [pltpu.HBM(b.shape, b.dtype) for b in bufs]
        + [jax.ShapeDtypeStruct((SUBLANES, LANES), F32)],
        input_output_aliases={i: 2 + i for i in range(n)},
        compiler_params=pltpu.CompilerParams(has_side_effects=EFFECT),
    )(*[_in_hbm(b) for b in bufs])
    return outs[0], outs[1], list(outs[2:-1]), outs[-1]


def _wait_copies(name, bufs, send_sems, recv_sems, after, copies):
    n = len(bufs)

    def body(*refs):
        for cp in copies(refs[:n], refs[n], refs[n + 1]):
            cp.wait_send()
            cp.wait_recv()

    return pl.pallas_call(
        body,
        name=name,
        in_specs=[HBM] * n + [SEM, SEM, ANY],
        out_specs=[HBM] * n,
        out_shape=[pltpu.HBM(b.shape, b.dtype) for b in bufs],
        input_output_aliases={i: i for i in range(n)},
        compiler_params=pltpu.CompilerParams(has_side_effects=EFFECT),
    )(*bufs, send_sems, recv_sems, after)


def _relay_copies(name, bufs, send_sems, recv_sems, after, landed, n_sem, onward):
    n = len(bufs)

    def body(*refs):
        for cp in landed(refs[:n], refs[n], refs[n + 1]):
            cp.wait_send()
            cp.wait_recv()
        for cp in onward(refs[:n], refs[n + 3], refs[n + 4]):
            cp.start()
        refs[-1][...] = jnp.zeros_like(refs[-1])

    outs = pl.pallas_call(
        body,
        name=name,
        in_specs=[HBM] * n + [SEM, SEM, ANY],
        out_specs=[SEM, SEM] + [HBM] * n + [pl.BlockSpec(memory_space=pltpu.VMEM)],
        out_shape=[pltpu.SemaphoreType.DMA((n_sem,)), pltpu.SemaphoreType.DMA((n_sem,))] + [pltpu.HBM(b.shape, b.dtype) for b in bufs]
        + [jax.ShapeDtypeStruct((SUBLANES, LANES), F32)],
        input_output_aliases={i: 2 + i for i in range(n)},
        compiler_params=pltpu.CompilerParams(has_side_effects=EFFECT),
    )(*bufs, send_sems, recv_sems, after)
    return outs[0], outs[1], list(outs[2:-1]), outs[-1]


def _gather_half_copies(first, count, over_ici):
    def copies(refs, send_sems, recv_sems):
        x, y, c, chips = _place()
        out = []
        for i in range(count):
            h = refs[i].shape[1] // 2
            mine = pl.ds(c * h, h)
            for j, (px, py) in enumerate(chips):
                k = 3 * (first + i) + j
                slot = 2 * x + y if over_ici else 2 * px + py
                to = (px, py, c) if over_ici else (x, y, 1 - c)
                out.append(_rcopy(refs[i].at[slot, mine], refs[i].at[slot, mine], send_sems.at[k], recv_sems.at[k], to))
        return out

    return copies


def _gather_copies(first, count):
    def copies(refs, send_sems, recv_sems):
        x, y, c, chips = _place()
        me = 2 * x + y
        out = []
        for i in range(count):
            for j, (px, py) in enumerate(chips):
                k = 3 * (first + i) + j
                out.append(_rcopy(refs[i].at[me], refs[i].at[me], send_sems.at[k], recv_sems.at[k], (px, py, c)))
        return out

    return copies


def _swap_copies(n):
    def copies(refs, send_sems, recv_sems):
        x, y, c, _ = _place()
        out = []
        for i in range(n):
            h = refs[i].shape[1] // 2
            out.append(_rcopy(refs[i].at[:, pl.ds((1 - c) * h, h)], refs[n + i], send_sems.at[i], recv_sems.at[i], (x, y, 1 - c)))
        return out

    return copies


def _partial_copies(n):
    def copies(refs, send_sems, recv_sems):
        x, y, c, chips = _place()
        out = []
        for i in range(n):
            for j, (px, py) in enumerate(chips):
                out.append(_rcopy(refs[i].at[2 * px + py], refs[n + i].at[j], send_sems.at[3 * i + j], recv_sems.at[3 * i + j], (px, py, c)))
        return out

    return copies


def _share_copies(n):
    def copies(refs, send_sems, recv_sems):
        x, y, c, _ = _place()
        return [_rcopy(refs[i].at[c], refs[i].at[c], send_sems.at[i], recv_sems.at[i], (x, y, 1 - c)) for i in range(n)]

    return copies


def _allreduce_small(vec):
    rows = vec.shape[0]

    def body(v_ref, o_ref, buf, send_sems, recv_sems):
        x, y, c, _ = _place()
        me = 4 * x + 2 * y + c
        buf[me] = v_ref[...]
        copies = []
        for k in range(1, N_DEV):
            peer = (x ^ (k >> 2), y ^ ((k >> 1) & 1), c ^ (k & 1))
            cp = _rcopy(v_ref, buf.at[me], send_sems.at[k - 1], recv_sems.at[k - 1], peer)
            cp.start()
            copies.append(cp)
        for cp in copies:
            cp.wait()
        acc = buf[0]
        for d in range(1, N_DEV):
            acc = acc + buf[d]
        o_ref[...] = acc

    return pl.pallas_call(
        body,
        name="allreduce_small",
        in_specs=[pl.BlockSpec(memory_space=pltpu.VMEM)],
        out_specs=pl.BlockSpec(memory_space=pltpu.VMEM),
        out_shape=jax.ShapeDtypeStruct(vec.shape, F32),
        scratch_shapes=[pltpu.VMEM((N_DEV, rows, LANES), F32), pltpu.SemaphoreType.DMA((N_DEV - 1,)), pltpu.SemaphoreType.DMA((N_DEV - 1,))],
        compiler_params=pltpu.CompilerParams(vmem_limit_bytes=VMEM_LIMIT_BYTES),
    )(vec)


class _Reduction:
    def __init__(self, tag, grads, place, core):
        self.tag, self.n, self.place, self.core = tag, len(grads), place, core
        lands = [lax.empty((N_CHIPS, g.shape[1] // 2, g.shape[2]), F32) for g in grads]
        self._start("swap", list(grads) + lands, self.n, _swap_copies(self.n))

    def _start(self, stage, bufs, n_sem, copies):
        *self.flight, self.token = _start_copies(f"rs_{stage}_start_{self.tag}", bufs, n_sem, copies)

    def _landed(self, stage, after, copies):
        send_sems, recv_sems, bufs = self.flight
        return _wait_copies(f"rs_{stage}_wait_{self.tag}", bufs, send_sems, recv_sems, after, copies)

    def to_chips(self, after):
        n = self.n
        bufs = self._landed("swap", after, _swap_copies(n))
        sums = [_add_core_halves(g, o, self.core, name=f"rs_add_core_{self.tag}_{i}") for i, (g, o) in enumerate(zip(bufs[:n], bufs[n:]))]
        self.mine = [f for f, _ in sums]
        parts = [b for _, b in sums]
        lands = [lax.empty((3,) + p.shape[1:], BF16) for p in parts]
        self._start("send", parts + lands, 3 * n, _partial_copies(n))

    def to_core(self, after):
        n = self.n
        bufs = self._landed("send", after, _partial_copies(n))
        halves = [_add_chip_partials(f, o, self.place, name=f"rs_add_chip_{self.tag}_{i}") for i, (f, o) in enumerate(zip(self.mine, bufs[n:]))]
        self._start("share", halves, n, _share_copies(n))

    def finish(self, after):
        return [b.reshape((-1,) + b.shape[2:]) for b in self._landed("share", after, _share_copies(self.n))]


ELEM_ROWS = (256, 176, 128, 64, 32, 16, 8)


def _add_core_halves(grad, got, c, name):
    s, r, cols = grad.shape
    h = r // 2
    tr = _pick(h, ELEM_ROWS)

    def body(c_ref, g_ref, o_ref, f_ref, b_ref):
        acc = g_ref[...] + o_ref[...]
        f_ref[...] = acc
        b_ref[...] = acc.astype(BF16)

    blk = pl.BlockSpec((None, tr, cols), lambda k, i, c_ref: (k, i, 0))
    return pl.pallas_call(
        body,
        name=name,
        grid_spec=pltpu.PrefetchScalarGridSpec(
            num_scalar_prefetch=1,
            grid=(s, h // tr),
            in_specs=[pl.BlockSpec((None, None, tr, cols), lambda k, i, c_ref: (k, c_ref[0], i, 0)), blk],
            out_specs=[blk, blk],
        ),
        out_shape=[jax.ShapeDtypeStruct((s, h, cols), F32), jax.ShapeDtypeStruct((s, h, cols), BF16)],
        compiler_params=_cparams(("parallel", "parallel")),
    )(c, grad.reshape(s, 2, h, cols), got)


def _add_chip_partials(mine, got, place, name):
    _, h, cols = mine.shape
    tr = _pick(h, ELEM_ROWS)

    def body(place_ref, m_ref, g_ref, o_ref):
        acc = m_ref[...]
        for j in range(3):
            acc = acc + g_ref[j].astype(F32)
        o_ref[...] = acc

    return pl.pallas_call(
        body,
        name=name,
        grid_spec=pltpu.PrefetchScalarGridSpec(
            num_scalar_prefetch=1,
            grid=(h // tr,),
            in_specs=[
                pl.BlockSpec((None, tr, cols), lambda i, place_ref: (place_ref[0], i, 0)),
                pl.BlockSpec((3, tr, cols), lambda i, place_ref: (0, i, 0)),
            ],
            out_specs=pl.BlockSpec((None, tr, cols), lambda i, place_ref: (place_ref[1], i, 0)),
        ),
        out_shape=jax.ShapeDtypeStruct((2, h, cols), F32),
        compiler_params=_cparams(("parallel",)),
    )(place, mine, got)


def _adamw_math(w, m, v, g):
    nm = ADAM_B1 * m + (1.0 - ADAM_B1) * g
    nv = ADAM_B2 * v + (1.0 - ADAM_B2) * (g * g)
    m_hat = nm * (1.0 / (1.0 - ADAM_B1 ** ADAM_STEP))
    v_hat = nv * (1.0 / (1.0 - ADAM_B2 ** ADAM_STEP))
    return -ADAM_LR * (m_hat / (jnp.sqrt(v_hat) + ADAM_EPS) + ADAM_WD * w), nm, nv


def _adamw_layer(w, m, v, g, layer, prev, name):
    nl, r, cols = w.shape
    tr = _pick(r, ELEM_ROWS)

    def body(w_ref, m_ref, v_ref, g_ref, *rest):
        go_ref, d_ref, nm_ref, nv_ref = rest[-4:]
        gv = g_ref[...]
        d_ref[...], nm_ref[...], nv_ref[...] = _adamw_math(w_ref[...], m_ref[...], v_ref[...], gv)
        go_ref[...] = gv

    lay = pl.BlockSpec((None, tr, cols), lambda i: (layer, i, 0))
    return pl.pallas_call(
        body,
        name=name,
        grid=(r // tr,),
        in_specs=[lay] * 3 + [pl.BlockSpec((tr, cols), lambda i: (i, 0))] + ([ANY] * 4 if prev else []),
        out_specs=[lay] * 4,
        out_shape=[jax.ShapeDtypeStruct((nl, r, cols), F32)] * 4,
        input_output_aliases={4 + k: k for k in range(4)} if prev else {},
        compiler_params=_cparams(("parallel",)),
    )(w, m, v, g, *(prev or ()))


def _adamw(w, m, v, g, name):
    r, cols = w.shape
    tr = _pick(r, ELEM_ROWS)

    def body(w_ref, m_ref, v_ref, g_ref, d_ref, nm_ref, nv_ref):
        d_ref[...], nm_ref[...], nv_ref[...] = _adamw_math(w_ref[...], m_ref[...], v_ref[...], g_ref[...])

    blk = pl.BlockSpec((tr, cols), lambda i: (i, 0))
    return pl.pallas_call(
        body,
        name=name,
        grid=(r // tr,),
        in_specs=[blk] * 4,
        out_specs=[blk] * 3,
        out_shape=[jax.ShapeDtypeStruct((r, cols), F32)] * 3,
        compiler_params=_cparams(("parallel",)),
    )(w, m, v, g)


SMALL_COLS = 384
SMALL_ROWS = 16


def _pad_rows(flat, rows, cols):
    return jnp.pad(flat, (0, rows * cols - flat.shape[0])).reshape(rows, cols)


def kernel(x, hg_norm, hg_w_in, hg_lb_logits, hg_out_norm, hg_w_out, kv_norm, w_kv, attn_norm, attn_w_q, attn_sinks, attn_w_o, ffn_norm, ffn_w_up, ffn_conv_w, ffn_conv_b, ffn_w_down, final_norm, loss_target, m_hg_norm, m_hg_w_in, m_hg_lb_logits, m_hg_out_norm, m_hg_w_out, m_kv_norm, m_w_kv, m_attn_norm, m_attn_w_q, m_attn_sinks, m_attn_w_o, m_ffn_norm, m_ffn_w_up, m_ffn_conv_w, m_ffn_conv_b, m_ffn_w_down, m_final_norm, v_hg_norm, v_hg_w_in, v_hg_lb_logits, v_hg_out_norm, v_hg_w_out, v_kv_norm, v_w_kv, v_attn_norm, v_attn_w_q, v_attn_sinks, v_attn_w_o, v_ffn_norm, v_ffn_w_up, v_ffn_conv_w, v_ffn_conv_b, v_ffn_w_down, v_final_norm):
    wts = dict(hg_norm=hg_norm, hg_w_in=hg_w_in, hg_lb_logits=hg_lb_logits, hg_out_norm=hg_out_norm, hg_w_out=hg_w_out, kv_norm=kv_norm, w_kv=w_kv, attn_norm=attn_norm, attn_w_q=attn_w_q, attn_sinks=attn_sinks, attn_w_o=attn_w_o, ffn_norm=ffn_norm, ffn_w_up=ffn_w_up, ffn_conv_w=ffn_conv_w, ffn_conv_b=ffn_conv_b, ffn_w_down=ffn_w_down, final_norm=final_norm)
    mom1 = dict(hg_norm=m_hg_norm, hg_w_in=m_hg_w_in, hg_lb_logits=m_hg_lb_logits, hg_out_norm=m_hg_out_norm, hg_w_out=m_hg_w_out, kv_norm=m_kv_norm, w_kv=m_w_kv, attn_norm=m_attn_norm, attn_w_q=m_attn_w_q, attn_sinks=m_attn_sinks, attn_w_o=m_attn_w_o, ffn_norm=m_ffn_norm, ffn_w_up=m_ffn_w_up, ffn_conv_w=m_ffn_conv_w, ffn_conv_b=m_ffn_conv_b, ffn_w_down=m_ffn_w_down, final_norm=m_final_norm)
    mom2 = dict(hg_norm=v_hg_norm, hg_w_in=v_hg_w_in, hg_lb_logits=v_hg_lb_logits, hg_out_norm=v_hg_out_norm, hg_w_out=v_hg_w_out, kv_norm=v_kv_norm, w_kv=v_w_kv, attn_norm=v_attn_norm, attn_w_q=v_attn_w_q, attn_sinks=v_attn_sinks, attn_w_o=v_attn_w_o, ffn_norm=v_ffn_norm, ffn_w_up=v_ffn_w_up, ffn_conv_w=v_ffn_conv_w, ffn_conv_b=v_ffn_conv_b, ffn_w_down=v_ffn_w_down, final_norm=v_final_norm)
    names = list(wts)
    chip = 2 * lax.axis_index("x") + lax.axis_index("y")
    core = lax.axis_index("c")
    core_arr = jnp.reshape(core, (1,)).astype(jnp.int32)
    fs = D_FF // N_CHIPS
    ds = D_MODEL // N_CHIPS

    place_arr = jnp.stack([chip, core]).astype(jnp.int32)
    small = jnp.concatenate([hg_norm.reshape(-1), hg_lb_logits.reshape(-1), ffn_conv_w.reshape(-1)])
    n_small = small.shape[0]
    shards = [
        ("small", _pad_rows(small, SMALL_ROWS, SMALL_COLS), F32), ("hg_w_in", hg_w_in[0], BF16),
        ("hg_w_out", hg_w_out[0], BF16), ("ffn_w_up0", ffn_w_up[0], BF16), ("ffn_w_down0", ffn_w_down[0], BF16),
        ("w_kv", w_kv, BF16), ("attn_w_q", attn_w_q[0], BF16), ("attn_w_o", attn_w_o[0], BF16),
        ("ffn_w_up1", ffn_w_up[1], BF16), ("ffn_w_down1", ffn_w_down[1], BF16),
    ]
    n_first = 3
    spans = dict(layer0=(0, 2), layer1=(2, 7))
    placed = [_place_shard(s, place_arr, dt, name=f"place_{nm}") for nm, s, dt in shards[:n_first]]
    first = _start_copies("gather_start_first", placed, 3 * n_first, _gather_copies(0, n_first))
    placed = [_place_shard(s, place_arr, dt, name=f"place_{nm}", deps=(first[3],)) for nm, s, dt in shards[n_first:]]
    rest = _start_copies("gather_start_rest", placed, 3 * len(placed), _gather_half_copies(0, len(placed), True))
    relayed = {}

    def fetch(w, stage, after):
        if stage == "first":
            got = _wait_copies("gather_wait_first", first[2][:2], first[0], first[1], after, _gather_copies(0, 2))
        elif stage == "mixer_out":
            got = _wait_copies("gather_wait_mixer_out", first[2][2:], first[0], first[1], after, _gather_copies(2, 1))
        elif stage.endswith("_relay"):
            lo, hi = spans[stage[:-6]]
            relayed[stage[:-6]] = _relay_copies(
                f"gather_{stage}", rest[2][lo:hi], rest[0], rest[1], after,
                _gather_half_copies(lo, hi - lo, True), 3 * (hi - lo), _gather_half_copies(0, hi - lo, False))
            return w
        else:
            lo, hi = spans[stage]
            send_sems, recv_sems, bufs, _ = relayed[stage]
            got = _wait_copies(f"gather_wait_{stage}", bufs, send_sems, recv_sems, after, _gather_half_copies(0, hi - lo, False))
        w = dict(w)
        if stage == "first":
            g_small = got[0].reshape(N_CHIPS, -1)[:, :n_small]
            conv_w = g_small[:, 3 * ds:].reshape(N_CHIPS, 2, 3, fs).transpose(1, 2, 0, 3).reshape(2, 3, D_FF)
            w.update(
                hg_norm=g_small[:, :ds].reshape(1, D_MODEL),
                hg_lb=g_small[:, ds:3 * ds].reshape(N_CHIPS, 2, ds).transpose(1, 0, 2).reshape(2, D_MODEL),
                ffn_conv_w=[conv_w[0], conv_w[1]], hg_w_in=got[1],
            )
        elif stage == "mixer_out":
            w.update(hg_w_out=got[0].reshape(1, D_MODEL, D_MODEL))
        elif stage == "layer0":
            w.update(ffn_w_up=[got[0], None], ffn_w_down=[got[1].reshape(1, D_FF, D_MODEL), None])
        else:
            w.update(
                w_kv=got[0].reshape(1, D_MODEL, 2 * LANES), attn_w_q=got[1].reshape(1, D_MODEL, D_MODEL),
                attn_w_o=got[2].reshape(1, D_MODEL, D_MODEL), ffn_w_up=[w["ffn_w_up"][0], got[3]],
                ffn_w_down=[w["ffn_w_down"][0], got[4].reshape(1, D_FF, D_MODEL)],
            )
        return w

    whole = dict(
        hg_out_norm=hg_out_norm, kv_norm=kv_norm.reshape(1, D_MODEL), attn_norm=attn_norm, attn_sinks=attn_sinks.reshape(ATT_QH),
        ffn_norm=[ffn_norm[0:1], ffn_norm[1:2]], ffn_conv_b=[ffn_conv_b[0:1], ffn_conv_b[1:2]], final_norm=final_norm.reshape(1, D_MODEL),
    )
    whole = fetch(whole, "first", rest[3])

    red, layer1 = {}, {}

    def by_rows(g, rows):
        return g.reshape(N_CHIPS, rows, g.shape[2])

    def hook(point, dh, grads):
        if point == "ffn1":
            red["ffn1"] = _Reduction("ffn1", [by_rows(grads["ffn_w_down"], fs), grads["ffn_w_up"]], place_arr, core_arr)
            return (red["ffn1"].token,)
        if point == "attn":
            red["ffn1"].to_chips(dh)
            layer1.update(grads)
            return (red["ffn1"].token,)
        if point == "ffn0":
            group = [by_rows(layer1["attn_w_o"], ds), by_rows(layer1["attn_w_q"], ds), by_rows(layer1["w_kv"], ds),
                     by_rows(grads["ffn_w_down"], fs), grads["ffn_w_up"]]
            red["mid"] = _Reduction("mid", group, place_arr, core_arr)
            return (red["mid"].token,)
        if point == "hgrn":
            red["ffn1"].to_core(dh)
            red["mid"].to_chips(dh)
            return (red["ffn1"].token, red["mid"].token)
        red["hg"] = _Reduction("hg", [by_rows(grads["hg_w_out"], ds), grads["hg_w_in"]], place_arr, core_arr)
        return (red["hg"].token,)

    loss, dx, grads = _local_step(x[0], loss_target[0], whole, fetch, hook)

    small_parts = [
        loss.reshape(-1), grads["hg_out_norm"].reshape(-1), grads["attn_sinks"].reshape(-1), grads["kv_norm"].reshape(-1),
        grads["attn_norm"].reshape(-1), grads["ffn_norm"][0].reshape(-1), grads["ffn_norm"][1].reshape(-1),
        grads["ffn_conv_b"][0].reshape(-1), grads["ffn_conv_b"][1].reshape(-1), grads["final_norm"].reshape(-1),
        grads["hg_norm"].reshape(-1), grads["hg_lb"].reshape(-1), grads["ffn_conv_w"][0].reshape(-1), grads["ffn_conv_w"][1].reshape(-1),
    ]
    sizes = [p.shape[0] for p in small_parts]
    flat = jnp.concatenate(small_parts)
    rows = -(-flat.shape[0] // (SUBLANES * LANES)) * SUBLANES
    summed = _allreduce_small(_pad_rows(flat, rows, LANES)).reshape(-1)
    red["hg"].to_chips(summed)
    offs = [0]
    for sz in sizes:
        offs.append(offs[-1] + sz)
    sm = [summed[offs[i]:offs[i + 1]] for i in range(len(sizes))]
    loss_out = sm[0][0]
    conv_w_full = jnp.stack([sm[12].reshape(3, D_FF), sm[13].reshape(3, D_FF)])
    small_grads = dict(
        hg_out_norm=sm[1].reshape(1, HG_DK), attn_sinks=sm[2][:ATT_QH].reshape(1, ATT_QH), kv_norm=sm[3], attn_norm=sm[4].reshape(1, D_MODEL),
        ffn_norm=jnp.stack([sm[5], sm[6]]), ffn_conv_b=jnp.stack([sm[7], sm[8]]), final_norm=sm[9],
        hg_norm=lax.dynamic_slice(sm[10].reshape(1, D_MODEL), (0, chip * ds), (1, ds)),
        hg_lb_logits=lax.dynamic_slice(sm[11].reshape(2, D_MODEL), (0, chip * ds), (2, ds)),
        ffn_conv_w=lax.dynamic_slice(conv_w_full, (0, 0, chip * fs), (2, 3, fs)),
    )

    out_g, out_d, out_m, out_v = {}, {}, {}, {}

    def update(name, g2):
        shape = wts[name].shape
        d2, m2, v2 = _adamw(wts[name].reshape(g2.shape), mom1[name].reshape(g2.shape), mom2[name].reshape(g2.shape), g2, name=f"adamw_{name}")
        out_g[name], out_d[name], out_m[name], out_v[name] = g2.reshape(shape), d2.reshape(shape), m2.reshape(shape), v2.reshape(shape)
        return d2

    def update_layer(name, g2, layer, prev):
        res = _adamw_layer(wts[name], mom1[name], mom2[name], g2, layer, prev, name=f"adamw_{name}{layer}")
        out_g[name], out_d[name], out_m[name], out_v[name] = res
        return res

    g_down1, g_up1 = red["ffn1"].finish(red["hg"].token)
    down1 = update_layer("ffn_w_down", g_down1, 1, None)
    up1 = update_layer("ffn_w_up", g_up1, 1, None)
    red["mid"].to_core(up1[1])
    g_o, g_q, g_kv, g_down0, g_up0 = red["mid"].finish(up1[2])
    update("attn_w_o", g_o)
    update("attn_w_q", g_q)
    update("w_kv", g_kv)
    update_layer("ffn_w_down", g_down0, 0, down1)
    last = update_layer("ffn_w_up", g_up0, 0, up1)
    red["hg"].to_core(last[1])
    g_out, g_in = red["hg"].finish(last[2])
    update("hg_w_out", g_out)
    update("hg_w_in", g_in)

    small_names = [n for n in names if n not in out_g]
    cat = lambda d: jnp.concatenate([d[n].reshape(-1) for n in small_names])
    n_flat = sum(wts[n].size for n in small_names)
    srows = -(-n_flat // (SUBLANES * LANES)) * SUBLANES
    packed = [_pad_rows(cat(d), srows, LANES) for d in (wts, mom1, mom2, small_grads)]
    d_s, m_s, v_s = _adamw(*packed, name="adamw_small")
    off = 0
    for n in small_names:
        sz, shape = wts[n].size, wts[n].shape
        out_g[n] = small_grads[n].reshape(shape)
        out_d[n] = d_s.reshape(-1)[off:off + sz].reshape(shape)
        out_m[n] = m_s.reshape(-1)[off:off + sz].reshape(shape)
        out_v[n] = v_s.reshape(-1)[off:off + sz].reshape(shape)
        off += sz

    grad_x = dx.reshape(x.shape)
    return (loss_out, grad_x, *[out_g[n] for n in names], *[out_d[n] for n in names], *[out_m[n] for n in names], *[out_v[n] for n in names])
```

```python
import functools

import jax
import jax.numpy as jnp
from jax import lax
from jax.experimental import pallas as pl
from jax.experimental.pallas import tpu as pltpu

F32 = jnp.float32
BF16 = jnp.bfloat16
MESH = pl.DeviceIdType.MESH

EPS = 1e-6
D_MODEL = 1024
HG_HEADS = 8
HG_DK = 128
HG_CHUNK = 64
ATT_HD = 64
ATT_QH = 16
ATT_KVH = 2
ATT_GROUP = ATT_QH // ATT_KVH
WINDOW = 128
D_FF = 2816
N_CHIPS = 4
N_DEV = 8
LANES = 128
SUBLANES = 8
VMEM_LIMIT_BYTES = 56 * 1024 * 1024
NEG = -1e30
ALIBI_SLOPES = tuple(2.0 ** (-8.0 * h / ATT_QH) for h in range(1, ATT_QH + 1))

ADAM_LR = 0.001
ADAM_B1 = 0.9
ADAM_B2 = 0.999
ADAM_EPS = 1e-08
ADAM_WD = 0.01
ADAM_STEP = 10


def _cparams(sem=None):
    return pltpu.CompilerParams(dimension_semantics=sem, vmem_limit_bytes=VMEM_LIMIT_BYTES)


def _pick(n, cands):
    for c in cands:
        if n % c == 0:
            return c
    return n


def _sigmoid(x):
    return 0.5 * jnp.tanh(0.5 * x) + 0.5


def _dot(a, b, dims):
    return lax.dot_general(a, b, (dims, ((), ())), preferred_element_type=F32)


NN = ((1,), (0,))
NT = ((1,), (1,))
TN = ((0,), (0,))


MM_ROWS = 1024


def _mm_nn(a, w, res=None, out_dtype=F32, name="mm_nn"):
    m, k = a.shape
    s, _, ns = w.shape
    tm = min(m, MM_ROWS)
    tn = _pick(ns, (1024, 1408, 512, 256, 128))
    npb = ns // tn

    def body(a_ref, w_ref, *rest):
        o_ref = rest[-1]
        acc = _dot(a_ref[...].astype(BF16), w_ref[...], NN)
        if res is not None:
            acc = acc + rest[0][...]
        o_ref[...] = acc.astype(o_ref.dtype)

    in_specs = [
        pl.BlockSpec((tm, k), lambda i, j: (i, 0)),
        pl.BlockSpec((None, k, tn), lambda i, j: (j // npb, 0, j % npb)),
    ]
    args = [a, w]
    if res is not None:
        in_specs.append(pl.BlockSpec((tm, tn), lambda i, j: (i, j)))
        args.append(res)
    return pl.pallas_call(
        body,
        name=name,
        grid=(m // tm, s * npb),
        in_specs=in_specs,
        out_specs=pl.BlockSpec((tm, tn), lambda i, j: (i, j)),
        out_shape=jax.ShapeDtypeStruct((m, s * ns), out_dtype),
        compiler_params=_cparams(("parallel", "parallel")),
    )(*args)


def _dy_spec(stacked, tm, tn, npb, row, kk):
    if stacked:
        return pl.BlockSpec((None, tm, tn), lambda *g: (kk(g) // npb, row(g), kk(g) % npb))
    return pl.BlockSpec((tm, tn), lambda *g: (row(g), kk(g)))


def _dep_specs(deps):
    return [pl.BlockSpec(d.shape, lambda *g: (0, 0)) for d in deps]


def _mm_nt(dy, w, stacked=False, out_dtype=F32, name="mm_nt", deps=()):
    s, k, ns = w.shape
    m = dy.shape[1] if stacked else dy.shape[0]
    tm = min(m, MM_ROWS)
    tko = _pick(k, (1024, 1408, 512, 256))
    tn = _pick(ns, (1024, 1408, 512, 256))
    npb = ns // tn
    nk = s * npb

    def body(dy_ref, w_ref, *rest):
        o_ref, acc_ref = rest[-2:]
        kk = pl.program_id(2)

        @pl.when(kk == 0)
        def _():
            acc_ref[...] = jnp.zeros_like(acc_ref)

        acc_ref[...] += _dot(dy_ref[...].astype(BF16), w_ref[...], NT)

        @pl.when(kk == nk - 1)
        def _():
            o_ref[...] = acc_ref[...].astype(o_ref.dtype)

    return pl.pallas_call(
        body,
        name=name,
        grid=(m // tm, k // tko, nk),
        in_specs=[
            _dy_spec(stacked, tm, tn, npb, lambda g: g[0], lambda g: g[2]),
            pl.BlockSpec((None, tko, tn), lambda i, j, kk: (kk // npb, j, kk % npb)),
        ] + _dep_specs(deps),
        out_specs=pl.BlockSpec((tm, tko), lambda i, j, kk: (i, j)),
        out_shape=jax.ShapeDtypeStruct((m, k), out_dtype),
        scratch_shapes=[pltpu.VMEM((tm, tko), F32)],
        compiler_params=_cparams(("parallel", "parallel", "arbitrary")),
    )(dy, w, *deps)


def _mm_tn(a, dy, s, ns, stacked=False, name="mm_tn", deps=()):
    m, k = a.shape
    tm = min(m, MM_ROWS)
    tk = _pick(k, (1024, 1408, 512, 256))
    tn = _pick(ns, (1024, 1408, 512, 256, 128))
    npb = ns // tn
    nm = m // tm

    def body(a_ref, dy_ref, *rest):
        o_ref, acc_ref = rest[-2:]
        mm = pl.program_id(2)

        @pl.when(mm == 0)
        def _():
            acc_ref[...] = jnp.zeros_like(acc_ref)

        acc_ref[...] += _dot(a_ref[...].astype(BF16), dy_ref[...].astype(BF16), TN)

        @pl.when(mm == nm - 1)
        def _():
            o_ref[...] = acc_ref[...]

    return pl.pallas_call(
        body,
        name=name,
        grid=(k // tk, s * npb, nm),
        in_specs=[
            pl.BlockSpec((tm, tk), lambda i, j, mm: (mm, i)),
            _dy_spec(stacked, tm, tn, npb, lambda g: g[2], lambda g: g[1]),
        ] + _dep_specs(deps),
        out_specs=pl.BlockSpec((None, tk, tn), lambda i, j, mm: (j // npb, i, j % npb)),
        out_shape=jax.ShapeDtypeStruct((s, k, ns), F32),
        scratch_shapes=[pltpu.VMEM((tk, tn), F32)],
        compiler_params=_cparams(("parallel", "parallel", "arbitrary")),
    )(a, dy, *deps)


ROW_TILE = 512


def _rms_fwd(x, g, name="rms_fwd"):
    t, d = x.shape
    r = min(t, ROW_TILE)

    def body(x_ref, g_ref, o_ref):
        xv = x_ref[...]
        rstd = lax.rsqrt(jnp.mean(xv * xv, axis=-1, keepdims=True) + EPS)
        o_ref[...] = (xv * rstd * g_ref[...]).astype(BF16)

    return pl.pallas_call(
        body,
        name=name,
        grid=(t // r,),
        in_specs=[pl.BlockSpec((r, d), lambda i: (i, 0)), pl.BlockSpec((1, d), lambda i: (0, 0))],
        out_specs=pl.BlockSpec((r, d), lambda i: (i, 0)),
        out_shape=jax.ShapeDtypeStruct((t, d), BF16),
        compiler_params=_cparams(("parallel",)),
    )(x, g)


def _rms_bwd(x, g, dxn, dres, name="rms_bwd"):
    t, d = x.shape
    r = min(t, ROW_TILE)

    def body(x_ref, g_ref, dxn_ref, dres_ref, dx_ref, dg_ref):
        @pl.when(pl.program_id(0) == 0)
        def _():
            dg_ref[...] = jnp.zeros_like(dg_ref)

        xv = x_ref[...]
        rstd = lax.rsqrt(jnp.mean(xv * xv, axis=-1, keepdims=True) + EPS)
        xhat = xv * rstd
        dxn_v = dxn_ref[...].astype(F32)
        gd = dxn_v * g_ref[...]
        dx_ref[...] = dres_ref[...] + rstd * (gd - xhat * jnp.mean(gd * xhat, axis=-1, keepdims=True))
        dg_ref[...] += jnp.sum(dxn_v * xhat, axis=0, keepdims=True)

    return pl.pallas_call(
        body,
        name=name,
        grid=(t // r,),
        in_specs=[
            pl.BlockSpec((r, d), lambda i: (i, 0)),
            pl.BlockSpec((1, d), lambda i: (0, 0)),
            pl.BlockSpec((r, d), lambda i: (i, 0)),
            pl.BlockSpec((r, d), lambda i: (i, 0)),
        ],
        out_specs=[pl.BlockSpec((r, d), lambda i: (i, 0)), pl.BlockSpec((1, d), lambda i: (0, 0))],
        out_shape=[jax.ShapeDtypeStruct((t, d), F32), jax.ShapeDtypeStruct((1, d), F32)],
        compiler_params=_cparams(("arbitrary",)),
    )(x, g, dxn, dres)


def _loss_head(h, g, target):
    t, d = h.shape
    r = min(t, ROW_TILE)

    def body(h_ref, g_ref, t_ref, dh_ref, dg_ref, loss_ref):
        @pl.when(pl.program_id(0) == 0)
        def _():
            dg_ref[...] = jnp.zeros_like(dg_ref)
            loss_ref[...] = jnp.zeros_like(loss_ref)

        xv = h_ref[...]
        rstd = lax.rsqrt(jnp.mean(xv * xv, axis=-1, keepdims=True) + EPS)
        xhat = xv * rstd
        gv = g_ref[...]
        err = xhat * gv - t_ref[...]
        loss_ref[...] += 0.5 * jnp.sum(jnp.mean(err * err, axis=-1, keepdims=True), axis=0, keepdims=True)
        dy = err * (1.0 / d)
        gd = dy * gv
        dh_ref[...] = rstd * (gd - xhat * jnp.mean(gd * xhat, axis=-1, keepdims=True))
        dg_ref[...] += jnp.sum(dy * xhat, axis=0, keepdims=True)

    return pl.pallas_call(
        body,
        name="loss_head",
        grid=(t // r,),
        in_specs=[
            pl.BlockSpec((r, d), lambda i: (i, 0)),
            pl.BlockSpec((1, d), lambda i: (0, 0)),
            pl.BlockSpec((r, d), lambda i: (i, 0)),
        ],
        out_specs=[
            pl.BlockSpec((r, d), lambda i: (i, 0)),
            pl.BlockSpec((1, d), lambda i: (0, 0)),
            pl.BlockSpec((1, LANES), lambda i: (0, 0)),
        ],
        out_shape=[
            jax.ShapeDtypeStruct((t, d), F32),
            jax.ShapeDtypeStruct((1, d), F32),
            jax.ShapeDtypeStruct((1, LANES), F32),
        ],
        compiler_params=_cparams(("arbitrary",)),
    )(h, g, target)


CONV_ROWS = 256
CONV_COLS = 1408


def _conv_taps(x_ext, n):
    tot = x_ext.shape[0]
    g1 = pltpu.roll(x_ext, 1, 0)[tot - n:]
    g2 = pltpu.roll(x_ext, 2, 0)[tot - n:]
    return g2, g1


def _conv_fwd(up, conv_w, conv_b, name="conv_fwd"):
    t = up.shape[0]
    r = min(t, CONV_ROWS)
    tc = CONV_COLS
    ncb = D_FF // tc
    hb = r // SUBLANES

    def body(g_ref, halo_ref, v_ref, w_ref, b_ref, o_ref):
        i = pl.program_id(1)
        g0 = g_ref[...]
        halo = halo_ref[...] * jnp.where(i > 0, 1.0, 0.0)
        g2, g1 = _conv_taps(jnp.concatenate([halo, g0], axis=0), r)
        c = b_ref[...] + w_ref[0:1, :] * g2 + w_ref[1:2, :] * g1 + w_ref[2:3, :] * g0
        o_ref[...] = (c * _sigmoid(c) * v_ref[...]).astype(BF16)

    return pl.pallas_call(
        body,
        name=name,
        grid=(ncb, t // r),
        in_specs=[
            pl.BlockSpec((r, tc), lambda j, i: (i, j)),
            pl.BlockSpec((SUBLANES, tc), lambda j, i: (jnp.maximum(i * hb - 1, 0), j)),
            pl.BlockSpec((r, tc), lambda j, i: (i, ncb + j)),
            pl.BlockSpec((3, tc), lambda j, i: (0, j)),
            pl.BlockSpec((1, tc), lambda j, i: (0, j)),
        ],
        out_specs=pl.BlockSpec((r, tc), lambda j, i: (i, j)),
        out_shape=jax.ShapeDtypeStruct((t, D_FF), BF16),
        compiler_params=_cparams(("parallel", "parallel")),
    )(up, up, up, conv_w, conv_b)


def _conv_bwd(up, conv_w, conv_b, dact, name="conv_bwd"):
    t = up.shape[0]
    r = min(t, CONV_ROWS)
    tc = CONV_COLS
    ncb = D_FF // tc
    hb = r // SUBLANES
    nrt = t // r

    def body(g_ref, halo_ref, v_ref, w_ref, b_ref, da_ref, dup_ref, dw_ref, db_ref, nxt_ref):
        ii = pl.program_id(1)
        i = nrt - 1 - ii

        @pl.when(ii == 0)
        def _():
            nxt_ref[...] = jnp.zeros_like(nxt_ref)
            dw_ref[...] = jnp.zeros_like(dw_ref)
            db_ref[...] = jnp.zeros_like(db_ref)

        g0 = g_ref[...]
        halo = halo_ref[...] * jnp.where(i > 0, 1.0, 0.0)
        g2, g1 = _conv_taps(jnp.concatenate([halo, g0], axis=0), r)
        w0, w1, w2 = w_ref[0:1, :], w_ref[1:2, :], w_ref[2:3, :]
        c = b_ref[...] + w0 * g2 + w1 * g1 + w2 * g0
        sg = _sigmoid(c)
        da = da_ref[...]
        dval = da * (c * sg)
        dc = da * v_ref[...] * (sg * (1.0 + c * (1.0 - sg)))
        db_ref[...] += jnp.sum(dc, axis=0, keepdims=True)
        dw_ref[0:1, :] += jnp.sum(dc * g2, axis=0, keepdims=True)
        dw_ref[1:2, :] += jnp.sum(dc * g1, axis=0, keepdims=True)
        dw_ref[2:3, :] += jnp.sum(dc * g0, axis=0, keepdims=True)
        ext = jnp.concatenate([dc, nxt_ref[...]], axis=0)
        tot = r + SUBLANES
        d1 = pltpu.roll(ext, tot - 1, 0)[:r]
        d2 = pltpu.roll(ext, tot - 2, 0)[:r]
        dgate = w2 * dc + w1 * d1 + w0 * d2
        nxt_ref[...] = dc[:SUBLANES]
        dup_ref[0] = dgate.astype(BF16)
        dup_ref[1] = dval.astype(BF16)

    rev = lambda ii: nrt - 1 - ii
    dup, dw, db = pl.pallas_call(
        body,
        name=name,
        grid=(ncb, nrt),
        in_specs=[
            pl.BlockSpec((r, tc), lambda j, ii: (rev(ii), j)),
            pl.BlockSpec((SUBLANES, tc), lambda j, ii: (jnp.maximum(rev(ii) * hb - 1, 0), j)),
            pl.BlockSpec((r, tc), lambda j, ii: (rev(ii), ncb + j)),
            pl.BlockSpec((3, tc), lambda j, ii: (0, j)),
            pl.BlockSpec((1, tc), lambda j, ii: (0, j)),
            pl.BlockSpec((r, tc), lambda j, ii: (rev(ii), j)),
        ],
        out_specs=[
            pl.BlockSpec((2, None, r, tc), lambda j, ii: (0, j, rev(ii), 0)),
            pl.BlockSpec((3, tc), lambda j, ii: (0, j)),
            pl.BlockSpec((1, tc), lambda j, ii: (0, j)),
        ],
        out_shape=[
            jax.ShapeDtypeStruct((2, ncb, t, tc), BF16),
            jax.ShapeDtypeStruct((3, D_FF), F32),
            jax.ShapeDtypeStruct((1, D_FF), F32),
        ],
        scratch_shapes=[pltpu.VMEM((SUBLANES, tc), F32)],
        compiler_params=_cparams(("parallel", "arbitrary")),
    )(up, up, up, conv_w, conv_b, dact)
    return dup.reshape(2 * ncb, t, tc), dw, db


def _split3(x):
    x1 = x.astype(BF16)
    r1 = x - x1.astype(F32)
    x2 = r1.astype(BF16)
    x3 = (r1 - x2.astype(F32)).astype(BF16)
    return x1, x2, x3


def _tri_dot(tri, x, dims):
    x1, x2, x3 = _split3(x)
    return _dot(tri, x1, dims) + _dot(tri, x2, dims) + _dot(tri, x3, dims)


def _lower_bound(logits_ref):
    return _sigmoid(logits_ref[0:1, :] - logits_ref[1:2, :])


def _hg_gates(qr, fr, lb):
    q = qr * _sigmoid(qr) * (HG_DK ** -0.5)
    sf = _sigmoid(fr)
    fg = lb + (1.0 - lb) * sf
    return q, sf, fg


def _hg_chunk_terms(q, fg, tril_b, low_half):
    g = jnp.log(fg)
    k = 1.0 - fg
    cum = _tri_dot(tril_b, g, NN)
    c_last = jnp.sum(g, axis=0, keepdims=True)
    c_mid = jnp.sum(jnp.where(low_half, g, 0.0), axis=0, keepdims=True)
    e_q = jnp.exp(cum - c_mid)
    e_k = jnp.exp(c_mid - cum)
    e_0 = jnp.exp(cum)
    e_l = jnp.exp(c_last - cum)
    return k, e_q, e_k, e_0, e_l, jnp.exp(c_last)


HG_BLOCK = 256


def _hg_proj_specs(rb, row):
    return [pl.BlockSpec((rb, D_MODEL), functools.partial(lambda i, k: (row(i), k), k=k)) for k in range(4)]


def _hg_consts(c):
    tril = lax.broadcasted_iota(jnp.int32, (c, c), 0) >= lax.broadcasted_iota(jnp.int32, (c, c), 1)
    low_half = lax.broadcasted_iota(jnp.int32, (c, D_MODEL), 0) < c // 2
    return tril, tril.astype(BF16), low_half


def _hgrn_fwd(proj, lb, wn):
    t = proj.shape[0]
    c = HG_CHUNK
    rb = min(t, HG_BLOCK)
    cpb = rb // c

    def body(q_ref, f_ref, i_ref, g_ref, lb_ref, wn_ref, o_ref, y_ref, st_ref, s_scr):
        @pl.when(pl.program_id(0) == 0)
        def _():
            s_scr[...] = jnp.zeros_like(s_scr)

        lb_all = _lower_bound(lb_ref)
        wnv = wn_ref[...]
        tril, tril_b, low_half = _hg_consts(c)

        def chunk(n, carry):
            rows = pl.ds(pl.multiple_of(n * c, c), c)
            q, _, fg = _hg_gates(q_ref[rows, :], f_ref[rows, :], lb_all)
            k, e_q, e_k, e_0, e_l, e_last = _hg_chunk_terms(q, fg, tril_b, low_half)
            qi, ki, q0, kl = (q * e_q).astype(BF16), (k * e_k).astype(BF16), (q * e_0).astype(BF16), (k * e_l).astype(BF16)
            v = i_ref[rows, :].astype(BF16)
            gr = g_ref[rows, :]
            gate = gr * _sigmoid(gr)
            for h in range(HG_HEADS):
                cols = slice(h * HG_DK, (h + 1) * HG_DK)
                st = s_scr[h]
                st_ref[h, n] = st
                a = jnp.where(tril, _dot(qi[:, cols], ki[:, cols], NT), 0.0)
                o = _dot(q0[:, cols], st.astype(BF16), NT) + _dot(a.astype(BF16), v[:, cols], NN)
                s_scr[h] = st * e_last[:, cols] + _dot(v[:, cols], kl[:, cols], TN)
                o_ref[rows, cols] = o
                rstd = lax.rsqrt(jnp.mean(o * o, axis=-1, keepdims=True) + EPS)
                y_ref[rows, cols] = (o * rstd * wnv * gate[:, cols]).astype(BF16)
            return carry

        lax.fori_loop(0, cpb, chunk, 0)

    blk = pl.BlockSpec((rb, D_MODEL), lambda i: (i, 0))
    return pl.pallas_call(
        body,
        name="hgrn_fwd",
        grid=(t // rb,),
        in_specs=_hg_proj_specs(rb, lambda i: i) + [pl.BlockSpec((2, D_MODEL), lambda i: (0, 0)), pl.BlockSpec((1, HG_DK), lambda i: (0, 0))],
        out_specs=[blk, blk, pl.BlockSpec((HG_HEADS, cpb, HG_DK, HG_DK), lambda i: (0, i, 0, 0))],
        out_shape=[
            jax.ShapeDtypeStruct((t, D_MODEL), F32),
            jax.ShapeDtypeStruct((t, D_MODEL), BF16),
            jax.ShapeDtypeStruct((HG_HEADS, t // c, HG_DK, HG_DK), F32),
        ],
        scratch_shapes=[pltpu.VMEM((HG_HEADS, HG_DK, HG_DK), F32)],
        compiler_params=_cparams(("arbitrary",)),
    )(proj, proj, proj, proj, lb, wn)


def _hgrn_bwd(proj, lb, wn, o, states, dy):
    t = proj.shape[0]
    c = HG_CHUNK
    rb = min(t, HG_BLOCK)
    cpb = rb // c
    nb = t // rb

    def body(q_ref, f_ref, i_ref, g_ref, lb_ref, wn_ref, o_ref, st_ref, dy_ref, dp_ref, dl_ref, dwn_ref, ds_scr, dlb_scr):
        step = pl.program_id(0)

        @pl.when(step == 0)
        def _():
            dwn_ref[...] = jnp.zeros_like(dwn_ref)
            ds_scr[...] = jnp.zeros_like(ds_scr)
            dlb_scr[...] = jnp.zeros_like(dlb_scr)

        lb_all = _lower_bound(lb_ref)
        wnv = wn_ref[...]
        tril, tril_b, low_half = _hg_consts(c)

        def chunk(nn, carry):
            n = cpb - 1 - nn
            rows = pl.ds(pl.multiple_of(n * c, c), c)
            qr = q_ref[rows, :]
            gr = g_ref[rows, :]
            q, sf, fg = _hg_gates(qr, f_ref[rows, :], lb_all)
            k, e_q, e_k, e_0, e_l, e_last = _hg_chunk_terms(q, fg, tril_b, low_half)
            qi, qi_lo, _ = _split3(q * e_q)
            ki, ki_lo, _ = _split3(k * e_k)
            q0 = (q * e_0).astype(BF16)
            kl = (k * e_l).astype(BF16)
            v = i_ref[rows, :].astype(BF16)
            sg = _sigmoid(gr)
            silu_g = gr * sg
            dsilu_g = sg * (1.0 + gr * (1.0 - sg))
            dqs, dks, d_lasts = [], [], []
            for h in range(HG_HEADS):
                cols = slice(h * HG_DK, (h + 1) * HG_DK)
                ov = o_ref[rows, cols]
                dyv = dy_ref[rows, cols].astype(F32)
                rstd = lax.rsqrt(jnp.mean(ov * ov, axis=-1, keepdims=True) + EPS)
                ohat = ov * rstd
                dp_ref[3, rows, cols] = (dyv * (ohat * wnv) * dsilu_g[:, cols]).astype(BF16)
                don = dyv * silu_g[:, cols]
                dwn_ref[...] += jnp.sum(don * ohat, axis=0, keepdims=True)
                gd = don * wnv
                do_b = (rstd * (gd - ohat * jnp.mean(gd * ohat, axis=-1, keepdims=True))).astype(BF16)
                st = st_ref[h, n]
                ds = ds_scr[h]
                ds_b = ds.astype(BF16)
                vh, kh = v[:, cols], k[:, cols]
                a_b = jnp.where(tril, _dot(qi[:, cols], ki[:, cols], NT), 0.0).astype(BF16)
                da_b = jnp.where(tril, _dot(do_b, vh, NT), 0.0).astype(BF16)
                dqs.append(_dot(do_b, st.astype(BF16), NN) * e_0[:, cols]
                           + (_dot(da_b, ki[:, cols], NN) + _dot(da_b, ki_lo[:, cols], NN)) * e_q[:, cols])
                dk_state = _dot(vh, ds_b, NN) * e_l[:, cols]
                dks.append((_dot(da_b, qi[:, cols], TN) + _dot(da_b, qi_lo[:, cols], TN)) * e_k[:, cols] + dk_state)
                dp_ref[2, rows, cols] = (_dot(a_b, do_b, TN) + _dot(kl[:, cols], ds_b, NT)).astype(BF16)
                ds_scr[h] = ds * e_last[:, cols] + _dot(do_b, q0[:, cols], TN)
                d_lasts.append(jnp.sum(dk_state * kh, axis=0, keepdims=True) + jnp.sum(ds * st, axis=0, keepdims=True) * e_last[:, cols])
            dq = jnp.concatenate(dqs, axis=1)
            dk = jnp.concatenate(dks, axis=1)
            dlogf = _tri_dot(tril_b, q * dq - k * dk, TN) + jnp.concatenate(d_lasts, axis=1)
            dfg = dlogf / fg - dk
            dlb_scr[...] += jnp.sum(dfg * (1.0 - sf), axis=0, keepdims=True)
            sq = _sigmoid(qr)
            dp_ref[0, rows, :] = (dq * (HG_DK ** -0.5) * (sq * (1.0 + qr * (1.0 - sq)))).astype(BF16)
            dp_ref[1, rows, :] = (dfg * (1.0 - lb_all) * sf * (1.0 - sf)).astype(BF16)
            return carry

        lax.fori_loop(0, cpb, chunk, 0)

        @pl.when(step == nb - 1)
        def _():
            d0 = dlb_scr[...] * lb_all * (1.0 - lb_all)
            dl_ref[0:1, :] = d0
            dl_ref[1:2, :] = -d0

    rev = lambda i: nb - 1 - i
    blk = pl.BlockSpec((rb, D_MODEL), lambda i: (rev(i), 0))
    return pl.pallas_call(
        body,
        name="hgrn_bwd",
        grid=(nb,),
        in_specs=_hg_proj_specs(rb, rev)
        + [pl.BlockSpec((2, D_MODEL), lambda i: (0, 0)), pl.BlockSpec((1, HG_DK), lambda i: (0, 0)), blk,
           pl.BlockSpec((HG_HEADS, cpb, HG_DK, HG_DK), lambda i: (0, rev(i), 0, 0)), blk],
        out_specs=[
            pl.BlockSpec((4, rb, D_MODEL), lambda i: (0, rev(i), 0)),
            pl.BlockSpec((2, D_MODEL), lambda i: (0, 0)),
            pl.BlockSpec((1, HG_DK), lambda i: (0, 0)),
        ],
        out_shape=[
            jax.ShapeDtypeStruct((4, t, D_MODEL), BF16),
            jax.ShapeDtypeStruct((2, D_MODEL), F32),
            jax.ShapeDtypeStruct((1, HG_DK), F32),
        ],
        scratch_shapes=[pltpu.VMEM((HG_HEADS, HG_DK, HG_DK), F32), pltpu.VMEM((1, D_MODEL), F32)],
        compiler_params=_cparams(("arbitrary",)),
    )(proj, proj, proj, proj, lb, wn, o, states, dy)


ATT_STACK = 8


def _att_stack(q_ref, sink_ref, first, lo, bias_p, bias_c, extra_ref=None):
    qs, bps, bcs, sinks, extras = [], [], [], None, []
    rows = lax.broadcasted_iota(jnp.int32, (ATT_STACK * WINDOW, 1), 0)
    for i in range(ATT_STACK):
        hq = first + i
        cols = slice((hq // 2) * LANES, (hq // 2 + 1) * LANES)
        sel = lo if hq % 2 == 0 else jnp.logical_not(lo)
        qp = q_ref[:, cols] * (ATT_HD ** -0.5)
        qs.append(jnp.where(sel, qp, jnp.zeros_like(qp)))
        bps.append(ALIBI_SLOPES[hq] * bias_p)
        bcs.append(ALIBI_SLOPES[hq] * bias_c)
        sinks = sink_ref[hq] if sinks is None else jnp.where(rows < i * WINDOW, sinks, sink_ref[hq])
        if extra_ref is not None:
            ep = extra_ref[:, cols]
            extras.append(jnp.where(sel, ep, jnp.zeros_like(ep)))
    cat = lambda parts: jnp.concatenate(parts, axis=0)
    return cat(qs), cat(bps), cat(bcs), sinks, (cat(extras) if extras else None)


def _att_rows(i):
    return slice(i * WINDOW, (i + 1) * WINDOW)


def _att_bias(n):
    tq = lax.broadcasted_iota(jnp.int32, (WINDOW, WINDOW), 0)
    sk = lax.broadcasted_iota(jnp.int32, (WINDOW, WINDOW), 1)
    valid_c = sk <= tq
    valid_p = (sk - tq) > jnp.where(n > 0, 0, WINDOW)
    dist_c = (tq - sk).astype(F32)
    return jnp.where(valid_p, -dist_c - float(WINDOW), NEG), jnp.where(valid_c, -dist_c, NEG)


def _att_halves(x, lo, kh):
    r = pltpu.roll(x, ATT_HD, 1)
    zero = jnp.zeros_like(x)
    if kh == 0:
        return jnp.where(lo, x, r), jnp.where(lo, x, zero), jnp.where(lo, zero, r)
    return jnp.where(lo, r, x), jnp.where(lo, r, zero), jnp.where(lo, zero, x)


def _att_probs(qm, k2p, k2c, bias_p, bias_c, sink):
    sp = _dot(qm, k2p, NT) + bias_p
    sc = _dot(qm, k2c, NT) + bias_c
    m = jnp.maximum(jnp.maximum(jnp.max(sp, axis=-1, keepdims=True), jnp.max(sc, axis=-1, keepdims=True)), sink)
    ep = jnp.exp(sp - m)
    ec = jnp.exp(sc - m)
    es = jnp.exp(sink - m)
    inv = 1.0 / (jnp.sum(ep, axis=-1, keepdims=True) + jnp.sum(ec, axis=-1, keepdims=True) + es)
    return ep * inv, ec * inv, es * inv


def _attn_fwd(q, kv, sinks):
    t = q.shape[0]
    nb = t // WINDOW

    def body(sink_ref, q_ref, kvp_ref, kvc_ref, o_ref):
        n = pl.program_id(0)
        bias_p, bias_c = _att_bias(n)
        lo = lax.broadcasted_iota(jnp.int32, (WINDOW, LANES), 1) < ATT_HD
        for kh in range(ATT_KVH):
            k2p, _, _ = _att_halves(kvp_ref[:, 0:LANES], lo, kh)
            k2c, _, _ = _att_halves(kvc_ref[:, 0:LANES], lo, kh)
            _, vlo_p, vhi_p = _att_halves(kvp_ref[:, LANES:2 * LANES], lo, kh)
            _, vlo_c, vhi_c = _att_halves(kvc_ref[:, LANES:2 * LANES], lo, kh)
            for first in range(kh * ATT_GROUP, (kh + 1) * ATT_GROUP, ATT_STACK):
                qs, bp, bc, sinks, _ = _att_stack(q_ref, sink_ref, first, lo, bias_p, bias_c)
                pp, pc, _ = _att_probs(qs, k2p, k2c, bp, bc, sinks)
                pp, pc = pp.astype(BF16), pc.astype(BF16)
                for i in range(0, ATT_STACK, 2):
                    even, odd = _att_rows(i), _att_rows(i + 1)
                    out = (_dot(pp[even], vlo_p, NN) + _dot(pc[even], vlo_c, NN)
                           + _dot(pp[odd], vhi_p, NN) + _dot(pc[odd], vhi_c, NN))
                    j = (first + i) // 2
                    o_ref[:, j * LANES:(j + 1) * LANES] = out.astype(BF16)

    return pl.pallas_call(
        body,
        name="attn_fwd",
        grid=(nb,),
        in_specs=[
            pl.BlockSpec(memory_space=pltpu.SMEM),
            pl.BlockSpec((WINDOW, D_MODEL), lambda n: (n, 0)),
            pl.BlockSpec((WINDOW, 2 * LANES), lambda n: (jnp.maximum(n - 1, 0), 0)),
            pl.BlockSpec((WINDOW, 2 * LANES), lambda n: (n, 0)),
        ],
        out_specs=pl.BlockSpec((WINDOW, D_MODEL), lambda n: (n, 0)),
        out_shape=jax.ShapeDtypeStruct((t, D_MODEL), BF16),
        compiler_params=_cparams(("parallel",)),
    )(sinks, q, kv, kv)


def _attn_bwd(q, kv, sinks, dout):
    t = q.shape[0]
    nb = t // WINDOW

    def body(sink_ref, q_ref, kvp_ref, kvc_ref, do_ref, dq_ref, dkv_ref, dsink_ref, carry_ref):
        n = pl.program_id(0)

        @pl.when(n == 0)
        def _():
            carry_ref[...] = jnp.zeros_like(carry_ref)
            dsink_ref[...] = jnp.zeros_like(dsink_ref)

        @pl.when(n == nb)
        def _():
            dkv_ref[...] = carry_ref[...].astype(BF16)

        @pl.when(n < nb)
        def _():
            bias_p, bias_c = _att_bias(n)
            lo = lax.broadcasted_iota(jnp.int32, (WINDOW, LANES), 1) < ATT_HD
            lane1 = lax.broadcasted_iota(jnp.int32, (1, LANES), 1)
            dsink = jnp.zeros((1, LANES), F32)
            halves = []
            for kh in range(ATT_KVH):
                k2p, klo_p, khi_p = _att_halves(kvp_ref[:, 0:LANES], lo, kh)
                k2c, klo_c, khi_c = _att_halves(kvc_ref[:, 0:LANES], lo, kh)
                v2p, _, _ = _att_halves(kvp_ref[:, LANES:2 * LANES], lo, kh)
                v2c, _, _ = _att_halves(kvc_ref[:, LANES:2 * LANES], lo, kh)
                acc = [jnp.zeros((WINDOW, LANES), F32) for _ in range(4)]
                for first in range(kh * ATT_GROUP, (kh + 1) * ATT_GROUP, ATT_STACK):
                    qs, bp, bc, sinks, dos = _att_stack(q_ref, sink_ref, first, lo, bias_p, bias_c, do_ref)
                    pp, pc, ps = _att_probs(qs, k2p, k2c, bp, bc, sinks)
                    dpp = _dot(dos, v2p, NT)
                    dpc = _dot(dos, v2c, NT)
                    delta = jnp.sum(pp * dpp, axis=-1, keepdims=True) + jnp.sum(pc * dpc, axis=-1, keepdims=True)
                    dsp = (pp * (dpp - delta)).astype(BF16)
                    dsc = (pc * (dpc - delta)).astype(BF16)
                    sink_term = ps * delta
                    for i in range(ATT_STACK):
                        dsink = dsink + jnp.where(lane1 == first + i, -jnp.sum(sink_term[_att_rows(i)], axis=0, keepdims=True), 0.0)
                    for i in range(0, ATT_STACK, 2):
                        even, odd = _att_rows(i), _att_rows(i + 1)
                        dq_pair = (_dot(dsp[even], klo_p, NN) + _dot(dsc[even], klo_c, NN)
                                   + _dot(dsp[odd], khi_p, NN) + _dot(dsc[odd], khi_c, NN))
                        j = (first + i) // 2
                        dq_ref[:, j * LANES:(j + 1) * LANES] = (dq_pair * (ATT_HD ** -0.5)).astype(BF16)
                    acc[0] = acc[0] + _dot(dsp, qs, TN)
                    acc[1] = acc[1] + _dot(dsc, qs, TN)
                    acc[2] = acc[2] + _dot(pp.astype(BF16), dos, TN)
                    acc[3] = acc[3] + _dot(pc.astype(BF16), dos, TN)
                halves.append([a + pltpu.roll(a, ATT_HD, 1) for a in acc])
            prev = jnp.concatenate(
                [jnp.where(lo, halves[0][0], halves[1][0]), jnp.where(lo, halves[0][2], halves[1][2])], axis=1)
            cur = jnp.concatenate(
                [jnp.where(lo, halves[0][1], halves[1][1]), jnp.where(lo, halves[0][3], halves[1][3])], axis=1)
            dkv_ref[...] = (carry_ref[...] + prev).astype(BF16)
            carry_ref[...] = cur
            dsink_ref[...] += dsink

    blk = lambda n: jnp.minimum(n, nb - 1)
    return pl.pallas_call(
        body,
        name="attn_bwd",
        grid=(nb + 1,),
        in_specs=[
            pl.BlockSpec(memory_space=pltpu.SMEM),
            pl.BlockSpec((WINDOW, D_MODEL), lambda n: (blk(n), 0)),
            pl.BlockSpec((WINDOW, 2 * LANES), lambda n: (jnp.maximum(blk(n) - 1, 0), 0)),
            pl.BlockSpec((WINDOW, 2 * LANES), lambda n: (blk(n), 0)),
            pl.BlockSpec((WINDOW, D_MODEL), lambda n: (blk(n), 0)),
        ],
        out_specs=[
            pl.BlockSpec((WINDOW, D_MODEL), lambda n: (blk(n), 0)),
            pl.BlockSpec((WINDOW, 2 * LANES), lambda n: (jnp.maximum(n - 1, 0), 0)),
            pl.BlockSpec((1, LANES), lambda n: (0, 0)),
        ],
        out_shape=[
            jax.ShapeDtypeStruct((t, D_MODEL), BF16),
            jax.ShapeDtypeStruct((t, 2 * LANES), BF16),
            jax.ShapeDtypeStruct((1, LANES), F32),
        ],
        scratch_shapes=[pltpu.VMEM((WINDOW, 2 * LANES), F32)],
        compiler_params=_cparams(("arbitrary",)),
    )(sinks, q, kv, kv, dout)


def _ffn_fwd(h, norm_g, w_up, conv_w, conv_b, w_down, tag, after_up=lambda up: None):
    xn = _rms_fwd(h, norm_g, name=f"ffn{tag}_norm")
    up = _mm_nn(xn, w_up, name=f"ffn{tag}_up")
    after_up(up)
    act = _conv_fwd(up, conv_w, conv_b, name=f"ffn{tag}_conv")
    h_out = _mm_nn(act, w_down, res=h, name=f"ffn{tag}_down")
    return h_out, (xn, up, act)


def _ffn_bwd(dh, h, norm_g, w_up, conv_w, conv_b, w_down, saved, tag, deps=()):
    xn, up, act = saved
    dw_down = _mm_tn(act, dh, 1, D_MODEL, name=f"ffn{tag}_dwdown", deps=deps)
    dact = _mm_nt(dh, w_down, name=f"ffn{tag}_dact", deps=deps)
    dup, dconv_w, dconv_b = _conv_bwd(up, conv_w, conv_b, dact, name=f"ffn{tag}_dconv")
    dw_up = _mm_tn(xn, dup, N_CHIPS, CONV_COLS, stacked=True, name=f"ffn{tag}_dwup")
    dxn = _mm_nt(dup, w_up, stacked=True, name=f"ffn{tag}_dxn")
    dh_in, dnorm = _rms_bwd(h, norm_g, dxn, dh, name=f"ffn{tag}_dnorm")
    return dh_in, dict(ffn_w_down=dw_down, ffn_w_up=dw_up, ffn_conv_w=dconv_w, ffn_conv_b=dconv_b, ffn_norm=dnorm)


def _local_step(x, target, w, fetch=lambda w, stage, after: w, hook=lambda point, dh, grads: ()):
    xn0 = _rms_fwd(x, w["hg_norm"], name="hg_norm")
    proj = _mm_nn(xn0, w["hg_w_in"], name="hg_in")
    o, y, states = _hgrn_fwd(proj, w["hg_lb"], w["hg_out_norm"])
    w = fetch(w, "mixer_out", y)
    fetch(w, "layer0_relay", y)
    h_a = _mm_nn(y, w["hg_w_out"], res=x, name="hg_out")
    w = fetch(w, "layer0", h_a)
    h1, ffn0 = _ffn_fwd(h_a, w["ffn_norm"][0], w["ffn_w_up"][0], w["ffn_conv_w"][0], w["ffn_conv_b"][0], w["ffn_w_down"][0], 0,
                        lambda up: fetch(w, "layer1_relay", up))
    w = fetch(w, "layer1", h1)
    kvn = _rms_fwd(h1, w["kv_norm"], name="kv_norm")
    kv = _mm_nn(kvn, w["w_kv"], out_dtype=BF16, name="kv_proj")
    xa = _rms_fwd(h1, w["attn_norm"], name="attn_norm")
    qa = _mm_nn(xa, w["attn_w_q"], out_dtype=BF16, name="attn_q")
    ao = _attn_fwd(qa, kv, w["attn_sinks"])
    h_b = _mm_nn(ao, w["attn_w_o"], res=h1, name="attn_o")
    h2, ffn1 = _ffn_fwd(h_b, w["ffn_norm"][1], w["ffn_w_up"][1], w["ffn_conv_w"][1], w["ffn_conv_b"][1], w["ffn_w_down"][1], 1)
    dh2, d_final, loss = _loss_head(h2, w["final_norm"], target)

    dh_b, g1 = _ffn_bwd(dh2, h_b, w["ffn_norm"][1], w["ffn_w_up"][1], w["ffn_conv_w"][1], w["ffn_conv_b"][1], w["ffn_w_down"][1], ffn1, 1)
    deps = hook("ffn1", dh_b, g1)
    dw_o = _mm_tn(ao, dh_b, 1, D_MODEL, name="attn_dwo", deps=deps)
    dao = _mm_nt(dh_b, w["attn_w_o"], out_dtype=BF16, name="attn_dao", deps=deps)
    dqa, dkv, dsinks = _attn_bwd(qa, kv, w["attn_sinks"], dao)
    dw_q = _mm_tn(xa, dqa, 1, D_MODEL, name="attn_dwq")
    dxa = _mm_nt(dqa, w["attn_w_q"], name="attn_dxa")
    dh1, d_attn_norm = _rms_bwd(h1, w["attn_norm"], dxa, dh_b, name="attn_dnorm")
    dw_kv = _mm_tn(kvn, dkv, 1, 2 * LANES, name="kv_dw")
    dkvn = _mm_nt(dkv, w["w_kv"], name="kv_dx")
    dh1, d_kv_norm = _rms_bwd(h1, w["kv_norm"], dkvn, dh1, name="kv_dnorm")
    deps = hook("attn", dh1, dict(attn_w_o=dw_o, attn_w_q=dw_q, w_kv=dw_kv))
    dh_a, g0 = _ffn_bwd(dh1, h_a, w["ffn_norm"][0], w["ffn_w_up"][0], w["ffn_conv_w"][0], w["ffn_conv_b"][0], w["ffn_w_down"][0], ffn0, 0, deps)
    deps = hook("ffn0", dh_a, g0)
    dw_out = _mm_tn(y, dh_a, 1, D_MODEL, name="hg_dwout", deps=deps)
    dy = _mm_nt(dh_a, w["hg_w_out"], out_dtype=BF16, name="hg_dy", deps=deps)
    dproj, dlb, d_out_norm = _hgrn_bwd(proj, w["hg_lb"], w["hg_out_norm"], o, states, dy)
    deps = hook("hgrn", dproj, None)
    dw_in = _mm_tn(xn0, dproj, N_CHIPS, D_MODEL, stacked=True, name="hg_dwin", deps=deps)
    deps = hook("hg_w", dproj, dict(hg_w_out=dw_out, hg_w_in=dw_in))
    dxn0 = _mm_nt(dproj, w["hg_w_in"], stacked=True, name="hg_dxn", deps=deps)
    dx, d_hg_norm = _rms_bwd(x, w["hg_norm"], dxn0, dh_a, name="hg_dnorm")

    grads = dict(
        hg_norm=d_hg_norm, hg_w_in=dw_in, hg_lb=dlb, hg_out_norm=d_out_norm, hg_w_out=dw_out,
        kv_norm=d_kv_norm, w_kv=dw_kv, attn_norm=d_attn_norm, attn_w_q=dw_q, attn_sinks=dsinks, attn_w_o=dw_o,
        final_norm=d_final,
    )
    for name in g0:
        grads[name] = [g0[name], g1[name]]
    return loss, dx, grads


ANY = pl.BlockSpec(memory_space=pl.ANY)


def _place():
    x, y, c = lax.axis_index("x"), lax.axis_index("y"), lax.axis_index("c")
    chips = [(1 - x, y), (x, 1 - y), (1 - x, 1 - y)]
    return x, y, c, chips


def _rcopy(src, dst, send_sem, recv_sem, to):
    return pltpu.make_async_remote_copy(src_ref=src, dst_ref=dst, send_sem=send_sem, recv_sem=recv_sem, device_id=to, device_id_type=MESH)


HBM = pl.BlockSpec(memory_space=pltpu.HBM)
SEM = pl.BlockSpec(memory_space=pltpu.SEMAPHORE)
EFFECT = pltpu.SideEffectType.DATAFLOW_SIDE_EFFECTING


def _in_hbm(a):
    return pltpu.with_memory_space_constraint(a, pltpu.HBM)


def _place_shard(shard, place, dtype, name, deps=()):
    r, cols = shard.shape
    tr = _pick(r, ELEM_ROWS)

    def body(place_ref, s_ref, *rest):
        o_ref = rest[-1]
        o_ref[...] = s_ref[...].astype(o_ref.dtype)

    return pl.pallas_call(
        body,
        name=name,
        grid_spec=pltpu.PrefetchScalarGridSpec(
            num_scalar_prefetch=1,
            grid=(r // tr,),
            in_specs=[pl.BlockSpec((tr, cols), lambda i, place_ref: (i, 0))] + _dep_specs(deps),
            out_specs=pl.BlockSpec((None, tr, cols), lambda i, place_ref: (place_ref[0], i, 0)),
        ),
        out_shape=jax.ShapeDtypeStruct((N_CHIPS, r, cols), dtype),
        compiler_params=_cparams(("parallel",)),
    )(place, shard, *deps)


def _start_copies(name, bufs, n_sem, copies):
    n = len(bufs)

    def body(*refs):
        for cp in copies(refs[:n], refs[n], refs[n + 1]):
            cp.start()
        refs[-1][...] = jnp.zeros_like(refs[-1])

    outs = pl.pallas_call(
        body,
        name=name,
        in_specs=[HBM] * n,
        out_specs=[SEM, SEM] + [HBM] * n + [pl.BlockSpec(memory_space=pltpu.VMEM)],
        out_shape=[pltpu.SemaphoreType.DMA((n_sem,)), pltpu.SemaphoreType.DMA((n_sem,))] + [pltpu.HBM(b.shape, b.dtype) for b in bufs]
        + [jax.ShapeDtypeStruct((SUBLANES, LANES), F32)],
        input_output_aliases={i: 2 + i for i in range(n)},
        compiler_params=pltpu.CompilerParams(has_side_effects=EFFECT),
    )(*[_in_hbm(b) for b in bufs])
    return outs[0], outs[1], list(outs[2:-1]), outs[-1]


def _wait_copies(name, bufs, send_sems, recv_sems, after, copies):
    n = len(bufs)

    def body(*refs):
        for cp in copies(refs[:n], refs[n], refs[n + 1]):
            cp.wait_send()
            cp.wait_recv()

    return pl.pallas_call(
        body,
        name=name,
        in_specs=[HBM] * n + [SEM, SEM, ANY],
        out_specs=[HBM] * n,
        out_shape=[pltpu.HBM(b.shape, b.dtype) for b in bufs],
        input_output_aliases={i: i for i in range(n)},
        compiler_params=pltpu.CompilerParams(has_side_effects=EFFECT),
    )(*bufs, send_sems, recv_sems, after)


def _relay_copies(name, bufs, send_sems, recv_sems, after, landed, n_sem, onward):
    n = len(bufs)

    def body(*refs):
        for cp in landed(refs[:n], refs[n], refs[n + 1]):
            cp.wait_send()
            cp.wait_recv()
        for cp in onward(refs[:n], refs[n + 3], refs[n + 4]):
            cp.start()
        refs[-1][...] = jnp.zeros_like(refs[-1])

    outs = pl.pallas_call(
        body,
        name=name,
        in_specs=[HBM] * n + [SEM, SEM, ANY],
        out_specs=[SEM, SEM] + [HBM] * n + [pl.BlockSpec(memory_space=pltpu.VMEM)],
        out_shape=[pltpu.SemaphoreType.DMA((n_sem,)), pltpu.SemaphoreType.DMA((n_sem,))] + [pltpu.HBM(b.shape, b.dtype) for b in bufs]
        + [jax.ShapeDtypeStruct((SUBLANES, LANES), F32)],
        input_output_aliases={i: 2 + i for i in range(n)},
        compiler_params=pltpu.CompilerParams(has_side_effects=EFFECT),
    )(*bufs, send_sems, recv_sems, after)
    return outs[0], outs[1], list(outs[2:-1]), outs[-1]


def _gather_half_copies(first, count, over_ici):
    def copies(refs, send_sems, recv_sems):
        x, y, c, chips = _place()
        out = []
        for i in range(count):
            h = refs[i].shape[1] // 2
            mine = pl.ds(c * h, h)
            for j, (px, py) in enumerate(chips):
                k = 3 * (first + i) + j
                slot = 2 * x + y if over_ici else 2 * px + py
                to = (px, py, c) if over_ici else (x, y, 1 - c)
                out.append(_rcopy(refs[i].at[slot, mine], refs[i].at[slot, mine], send_sems.at[k], recv_sems.at[k], to))
        return out

    return copies


def _gather_copies(first, count):
    def copies(refs, send_sems, recv_sems):
        x, y, c, chips = _place()
        me = 2 * x + y
        out = []
        for i in range(count):
            for j, (px, py) in enumerate(chips):
                k = 3 * (first + i) + j
                out.append(_rcopy(refs[i].at[me], refs[i].at[me], send_sems.at[k], recv_sems.at[k], (px, py, c)))
        return out

    return copies


def _swap_copies(n):
    def copies(refs, send_sems, recv_sems):
        x, y, c, _ = _place()
        out = []
        for i in range(n):
            h = refs[i].shape[1] // 2
            out.append(_rcopy(refs[i].at[:, pl.ds((1 - c) * h, h)], refs[n + i], send_sems.at[i], recv_sems.at[i], (x, y, 1 - c)))
        return out

    return copies


def _partial_copies(n):
    def copies(refs, send_sems, recv_sems):
        x, y, c, chips = _place()
        out = []
        for i in range(n):
            for j, (px, py) in enumerate(chips):
                out.append(_rcopy(refs[i].at[2 * px + py], refs[n + i].at[j], send_sems.at[3 * i + j], recv_sems.at[3 * i + j], (px, py, c)))
        return out

    return copies


def _share_copies(n):
    def copies(refs, send_sems, recv_sems):
        x, y, c, _ = _place()
        return [_rcopy(refs[i].at[c], refs[i].at[c], send_sems.at[i], recv_sems.at[i], (x, y, 1 - c)) for i in range(n)]

    return copies


def _allreduce_small(vec):
    rows = vec.shape[0]

    def body(v_ref, o_ref, buf, send_sems, recv_sems):
        x, y, c, _ = _place()
        me = 4 * x + 2 * y + c
        buf[me] = v_ref[...]
        copies = []
        for k in range(1, N_DEV):
            peer = (x ^ (k >> 2), y ^ ((k >> 1) & 1), c ^ (k & 1))
            cp = _rcopy(v_ref, buf.at[me], send_sems.at[k - 1], recv_sems.at[k - 1], peer)
            cp.start()
            copies.append(cp)
        for cp in copies:
            cp.wait()
        acc = buf[0]
        for d in range(1, N_DEV):
            acc = acc + buf[d]
        o_ref[...] = acc

    return pl.pallas_call(
        body,
        name="allreduce_small",
        in_specs=[pl.BlockSpec(memory_space=pltpu.VMEM)],
        out_specs=pl.BlockSpec(memory_space=pltpu.VMEM),
        out_shape=jax.ShapeDtypeStruct(vec.shape, F32),
        scratch_shapes=[pltpu.VMEM((N_DEV, rows, LANES), F32), pltpu.SemaphoreType.DMA((N_DEV - 1,)), pltpu.SemaphoreType.DMA((N_DEV - 1,))],
        compiler_params=pltpu.CompilerParams(vmem_limit_bytes=VMEM_LIMIT_BYTES),
    )(vec)


class _Reduction:
    def __init__(self, tag, grads, place, core):
        self.tag, self.n, self.place, self.core = tag, len(grads), place, core
        lands = [lax.empty((N_CHIPS, g.shape[1] // 2, g.shape[2]), F32) for g in grads]
        self._start("swap", list(grads) + lands, self.n, _swap_copies(self.n))

    def _start(self, stage, bufs, n_sem, copies):
        *self.flight, self.token = _start_copies(f"rs_{stage}_start_{self.tag}", bufs, n_sem, copies)

    def _landed(self, stage, after, copies):
        send_sems, recv_sems, bufs = self.flight
        return _wait_copies(f"rs_{stage}_wait_{self.tag}", bufs, send_sems, recv_sems, after, copies)

    def to_chips(self, after):
        n = self.n
        bufs = self._landed("swap", after, _swap_copies(n))
        sums = [_add_core_halves(g, o, self.core, name=f"rs_add_core_{self.tag}_{i}") for i, (g, o) in enumerate(zip(bufs[:n], bufs[n:]))]
        self.mine = [f for f, _ in sums]
        parts = [b for _, b in sums]
        lands = [lax.empty((3,) + p.shape[1:], BF16) for p in parts]
        self._start("send", parts + lands, 3 * n, _partial_copies(n))

    def to_core(self, after):
        n = self.n
        bufs = self._landed("send", after, _partial_copies(n))
        halves = [_add_chip_partials(f, o, self.place, name=f"rs_add_chip_{self.tag}_{i}") for i, (f, o) in enumerate(zip(self.mine, bufs[n:]))]
        self._start("share", halves, n, _share_copies(n))

    def finish(self, after):
        return [b.reshape((-1,) + b.shape[2:]) for b in self._landed("share", after, _share_copies(self.n))]


ELEM_ROWS = (256, 176, 128, 64, 32, 16, 8)


def _add_core_halves(grad, got, c, name):
    s, r, cols = grad.shape
    h = r // 2
    tr = _pick(h, ELEM_ROWS)

    def body(c_ref, g_ref, o_ref, f_ref, b_ref):
        acc = g_ref[...] + o_ref[...]
        f_ref[...] = acc
        b_ref[...] = acc.astype(BF16)

    blk = pl.BlockSpec((None, tr, cols), lambda k, i, c_ref: (k, i, 0))
    return pl.pallas_call(
        body,
        name=name,
        grid_spec=pltpu.PrefetchScalarGridSpec(
            num_scalar_prefetch=1,
            grid=(s, h // tr),
            in_specs=[pl.BlockSpec((None, None, tr, cols), lambda k, i, c_ref: (k, c_ref[0], i, 0)), blk],
            out_specs=[blk, blk],
        ),
        out_shape=[jax.ShapeDtypeStruct((s, h, cols), F32), jax.ShapeDtypeStruct((s, h, cols), BF16)],
        compiler_params=_cparams(("parallel", "parallel")),
    )(c, grad.reshape(s, 2, h, cols), got)


def _add_chip_partials(mine, got, place, name):
    _, h, cols = mine.shape
    tr = _pick(h, ELEM_ROWS)

    def body(place_ref, m_ref, g_ref, o_ref):
        acc = m_ref[...]
        for j in range(3):
            acc = acc + g_ref[j].astype(F32)
        o_ref[...] = acc

    return pl.pallas_call(
        body,
        name=name,
        grid_spec=pltpu.PrefetchScalarGridSpec(
            num_scalar_prefetch=1,
            grid=(h // tr,),
            in_specs=[
                pl.BlockSpec((None, tr, cols), lambda i, place_ref: (place_ref[0], i, 0)),
                pl.BlockSpec((3, tr, cols), lambda i, place_ref: (0, i, 0)),
            ],
            out_specs=pl.BlockSpec((None, tr, cols), lambda i, place_ref: (place_ref[1], i, 0)),
        ),
        out_shape=jax.ShapeDtypeStruct((2, h, cols), F32),
        compiler_params=_cparams(("parallel",)),
    )(place, mine, got)


def _adamw_math(w, m, v, g):
    nm = ADAM_B1 * m + (1.0 - ADAM_B1) * g
    nv = ADAM_B2 * v + (1.0 - ADAM_B2) * (g * g)
    m_hat = nm * (1.0 / (1.0 - ADAM_B1 ** ADAM_STEP))
    v_hat = nv * (1.0 / (1.0 - ADAM_B2 ** ADAM_STEP))
    return -ADAM_LR * (m_hat / (jnp.sqrt(v_hat) + ADAM_EPS) + ADAM_WD * w), nm, nv


def _adamw_layer(w, m, v, g, layer, prev, name):
    nl, r, cols = w.shape
    tr = _pick(r, ELEM_ROWS)

    def body(w_ref, m_ref, v_ref, g_ref, *rest):
        go_ref, d_ref, nm_ref, nv_ref = rest[-4:]
        gv = g_ref[...]
        d_ref[...], nm_ref[...], nv_ref[...] = _adamw_math(w_ref[...], m_ref[...], v_ref[...], gv)
        go_ref[...] = gv

    lay = pl.BlockSpec((None, tr, cols), lambda i: (layer, i, 0))
    return pl.pallas_call(
        body,
        name=name,
        grid=(r // tr,),
        in_specs=[lay] * 3 + [pl.BlockSpec((tr, cols), lambda i: (i, 0))] + ([ANY] * 4 if prev else []),
        out_specs=[lay] * 4,
        out_shape=[jax.ShapeDtypeStruct((nl, r, cols), F32)] * 4,
        input_output_aliases={4 + k: k for k in range(4)} if prev else {},
        compiler_params=_cparams(("parallel",)),
    )(w, m, v, g, *(prev or ()))


def _adamw(w, m, v, g, name):
    r, cols = w.shape
    tr = _pick(r, ELEM_ROWS)

    def body(w_ref, m_ref, v_ref, g_ref, d_ref, nm_ref, nv_ref):
        d_ref[...], nm_ref[...], nv_ref[...] = _adamw_math(w_ref[...], m_ref[...], v_ref[...], g_ref[...])

    blk = pl.BlockSpec((tr, cols), lambda i: (i, 0))
    return pl.pallas_call(
        body,
        name=name,
        grid=(r // tr,),
        in_specs=[blk] * 4,
        out_specs=[blk] * 3,
        out_shape=[jax.ShapeDtypeStruct((r, cols), F32)] * 3,
        compiler_params=_cparams(("parallel",)),
    )(w, m, v, g)


SMALL_COLS = 384
SMALL_ROWS = 16


def _pad_rows(flat, rows, cols):
    return jnp.pad(flat, (0, rows * cols - flat.shape[0])).reshape(rows, cols)


def kernel(x, hg_norm, hg_w_in, hg_lb_logits, hg_out_norm, hg_w_out, kv_norm, w_kv, attn_norm, attn_w_q, attn_sinks, attn_w_o, ffn_norm, ffn_w_up, ffn_conv_w, ffn_conv_b, ffn_w_down, final_norm, loss_target, m_hg_norm, m_hg_w_in, m_hg_lb_logits, m_hg_out_norm, m_hg_w_out, m_kv_norm, m_w_kv, m_attn_norm, m_attn_w_q, m_attn_sinks, m_attn_w_o, m_ffn_norm, m_ffn_w_up, m_ffn_conv_w, m_ffn_conv_b, m_ffn_w_down, m_final_norm, v_hg_norm, v_hg_w_in, v_hg_lb_logits, v_hg_out_norm, v_hg_w_out, v_kv_norm, v_w_kv, v_attn_norm, v_attn_w_q, v_attn_sinks, v_attn_w_o, v_ffn_norm, v_ffn_w_up, v_ffn_conv_w, v_ffn_conv_b, v_ffn_w_down, v_final_norm):
    wts = dict(hg_norm=hg_norm, hg_w_in=hg_w_in, hg_lb_logits=hg_lb_logits, hg_out_norm=hg_out_norm, hg_w_out=hg_w_out, kv_norm=kv_norm, w_kv=w_kv, attn_norm=attn_norm, attn_w_q=attn_w_q, attn_sinks=attn_sinks, attn_w_o=attn_w_o, ffn_norm=ffn_norm, ffn_w_up=ffn_w_up, ffn_conv_w=ffn_conv_w, ffn_conv_b=ffn_conv_b, ffn_w_down=ffn_w_down, final_norm=final_norm)
    mom1 = dict(hg_norm=m_hg_norm, hg_w_in=m_hg_w_in, hg_lb_logits=m_hg_lb_logits, hg_out_norm=m_hg_out_norm, hg_w_out=m_hg_w_out, kv_norm=m_kv_norm, w_kv=m_w_kv, attn_norm=m_attn_norm, attn_w_q=m_attn_w_q, attn_sinks=m_attn_sinks, attn_w_o=m_attn_w_o, ffn_norm=m_ffn_norm, ffn_w_up=m_ffn_w_up, ffn_conv_w=m_ffn_conv_w, ffn_conv_b=m_ffn_conv_b, ffn_w_down=m_ffn_w_down, final_norm=m_final_norm)
    mom2 = dict(hg_norm=v_hg_norm, hg_w_in=v_hg_w_in, hg_lb_logits=v_hg_lb_logits, hg_out_norm=v_hg_out_norm, hg_w_out=v_hg_w_out, kv_norm=v_kv_norm, w_kv=v_w_kv, attn_norm=v_attn_norm, attn_w_q=v_attn_w_q, attn_sinks=v_attn_sinks, attn_w_o=v_attn_w_o, ffn_norm=v_ffn_norm, ffn_w_up=v_ffn_w_up, ffn_conv_w=v_ffn_conv_w, ffn_conv_b=v_ffn_conv_b, ffn_w_down=v_ffn_w_down, final_norm=v_final_norm)
    names = list(wts)
    chip = 2 * lax.axis_index("x") + lax.axis_index("y")
    core = lax.axis_index("c")
    core_arr = jnp.reshape(core, (1,)).astype(jnp.int32)
    fs = D_FF // N_CHIPS
    ds = D_MODEL // N_CHIPS

    place_arr = jnp.stack([chip, core]).astype(jnp.int32)
    small = jnp.concatenate([hg_norm.reshape(-1), hg_lb_logits.reshape(-1), ffn_conv_w.reshape(-1)])
    n_small = small.shape[0]
    shards = [
        ("small", _pad_rows(small, SMALL_ROWS, SMALL_COLS), F32), ("hg_w_in", hg_w_in[0], BF16),
        ("hg_w_out", hg_w_out[0], BF16), ("ffn_w_up0", ffn_w_up[0], BF16), ("ffn_w_down0", ffn_w_down[0], BF16),
        ("w_kv", w_kv, BF16), ("attn_w_q", attn_w_q[0], BF16), ("attn_w_o", attn_w_o[0], BF16),
        ("ffn_w_up1", ffn_w_up[1], BF16), ("ffn_w_down1", ffn_w_down[1], BF16),
    ]
    n_first = 3
    spans = dict(layer0=(0, 2), layer1=(2, 7))
    placed = [_place_shard(s, place_arr, dt, name=f"place_{nm}") for nm, s, dt in shards[:n_first]]
    first = _start_copies("gather_start_first", placed, 3 * n_first, _gather_copies(0, n_first))
    placed = [_place_shard(s, place_arr, dt, name=f"place_{nm}", deps=(first[3],)) for nm, s, dt in shards[n_first:]]
    rest = _start_copies("gather_start_rest", placed, 3 * len(placed), _gather_half_copies(0, len(placed), True))
    relayed = {}

    def fetch(w, stage, after):
        if stage == "first":
            got = _wait_copies("gather_wait_first", first[2][:2], first[0], first[1], after, _gather_copies(0, 2))
        elif stage == "mixer_out":
            got = _wait_copies("gather_wait_mixer_out", first[2][2:], first[0], first[1], after, _gather_copies(2, 1))
        elif stage.endswith("_relay"):
            lo, hi = spans[stage[:-6]]
            relayed[stage[:-6]] = _relay_copies(
                f"gather_{stage}", rest[2][lo:hi], rest[0], rest[1], after,
                _gather_half_copies(lo, hi - lo, True), 3 * (hi - lo), _gather_half_copies(0, hi - lo, False))
            return w
        else:
            lo, hi = spans[stage]
            send_sems, recv_sems, bufs, _ = relayed[stage]
            got = _wait_copies(f"gather_wait_{stage}", bufs, send_sems, recv_sems, after, _gather_half_copies(0, hi - lo, False))
        w = dict(w)
        if stage == "first":
            g_small = got[0].reshape(N_CHIPS, -1)[:, :n_small]
            conv_w = g_small[:, 3 * ds:].reshape(N_CHIPS, 2, 3, fs).transpose(1, 2, 0, 3).reshape(2, 3, D_FF)
            w.update(
                hg_norm=g_small[:, :ds].reshape(1, D_MODEL),
                hg_lb=g_small[:, ds:3 * ds].reshape(N_CHIPS, 2, ds).transpose(1, 0, 2).reshape(2, D_MODEL),
                ffn_conv_w=[conv_w[0], conv_w[1]], hg_w_in=got[1],
            )
        elif stage == "mixer_out":
            w.update(hg_w_out=got[0].reshape(1, D_MODEL, D_MODEL))
        elif stage == "layer0":
            w.update(ffn_w_up=[got[0], None], ffn_w_down=[got[1].reshape(1, D_FF, D_MODEL), None])
        else:
            w.update(
                w_kv=got[0].reshape(1, D_MODEL, 2 * LANES), attn_w_q=got[1].reshape(1, D_MODEL, D_MODEL),
                attn_w_o=got[2].reshape(1, D_MODEL, D_MODEL), ffn_w_up=[w["ffn_w_up"][0], got[3]],
                ffn_w_down=[w["ffn_w_down"][0], got[4].reshape(1, D_FF, D_MODEL)],
            )
        return w

    whole = dict(
        hg_out_norm=hg_out_norm, kv_norm=kv_norm.reshape(1, D_MODEL), attn_norm=attn_norm, attn_sinks=attn_sinks.reshape(ATT_QH),
        ffn_norm=[ffn_norm[0:1], ffn_norm[1:2]], ffn_conv_b=[ffn_conv_b[0:1], ffn_conv_b[1:2]], final_norm=final_norm.reshape(1, D_MODEL),
    )
    whole = fetch(whole, "first", rest[3])

    red, layer1 = {}, {}

    def by_rows(g, rows):
        return g.reshape(N_CHIPS, rows, g.shape[2])

    def hook(point, dh, grads):
        if point == "ffn1":
            red["ffn1"] = _Reduction("ffn1", [by_rows(grads["ffn_w_down"], fs), grads["ffn_w_up"]], place_arr, core_arr)
            return (red["ffn1"].token,)
        if point == "attn":
            red["ffn1"].to_chips(dh)
            layer1.update(grads)
            return (red["ffn1"].token,)
        if point == "ffn0":
            group = [by_rows(layer1["attn_w_o"], ds), by_rows(layer1["attn_w_q"], ds), by_rows(layer1["w_kv"], ds),
                     by_rows(grads["ffn_w_down"], fs), grads["ffn_w_up"]]
            red["mid"] = _Reduction("mid", group, place_arr, core_arr)
            return (red["mid"].token,)
        if point == "hgrn":
            red["ffn1"].to_core(dh)
            red["mid"].to_chips(dh)
            return (red["ffn1"].token, red["mid"].token)
        red["hg"] = _Reduction("hg", [by_rows(grads["hg_w_out"], ds), grads["hg_w_in"]], place_arr, core_arr)
        return (red["hg"].token,)

    loss, dx, grads = _local_step(x[0], loss_target[0], whole, fetch, hook)

    small_parts = [
        loss.reshape(-1), grads["hg_out_norm"].reshape(-1), grads["attn_sinks"].reshape(-1), grads["kv_norm"].reshape(-1),
        grads["attn_norm"].reshape(-1), grads["ffn_norm"][0].reshape(-1), grads["ffn_norm"][1].reshape(-1),
        grads["ffn_conv_b"][0].reshape(-1), grads["ffn_conv_b"][1].reshape(-1), grads["final_norm"].reshape(-1),
        grads["hg_norm"].reshape(-1), grads["hg_lb"].reshape(-1), grads["ffn_conv_w"][0].reshape(-1), grads["ffn_conv_w"][1].reshape(-1),
    ]
    sizes = [p.shape[0] for p in small_parts]
    flat = jnp.concatenate(small_parts)
    rows = -(-flat.shape[0] // (SUBLANES * LANES)) * SUBLANES
    summed = _allreduce_small(_pad_rows(flat, rows, LANES)).reshape(-1)
    red["hg"].to_chips(summed)
    offs = [0]
    for sz in sizes:
        offs.append(offs[-1] + sz)
    sm = [summed[offs[i]:offs[i + 1]] for i in range(len(sizes))]
    loss_out = sm[0][0]
    conv_w_full = jnp.stack([sm[12].reshape(3, D_FF), sm[13].reshape(3, D_FF)])
    small_grads = dict(
        hg_out_norm=sm[1].reshape(1, HG_DK), attn_sinks=sm[2][:ATT_QH].reshape(1, ATT_QH), kv_norm=sm[3], attn_norm=sm[4].reshape(1, D_MODEL),
        ffn_norm=jnp.stack([sm[5], sm[6]]), ffn_conv_b=jnp.stack([sm[7], sm[8]]), final_norm=sm[9],
        hg_norm=lax.dynamic_slice(sm[10].reshape(1, D_MODEL), (0, chip * ds), (1, ds)),
        hg_lb_logits=lax.dynamic_slice(sm[11].reshape(2, D_MODEL), (0, chip * ds), (2, ds)),
        ffn_conv_w=lax.dynamic_slice(conv_w_full, (0, 0, chip * fs), (2, 3, fs)),
    )

    out_g, out_d, out_m, out_v = {}, {}, {}, {}

    def update(name, g2):
        shape = wts[name].shape
        d2, m2, v2 = _adamw(wts[name].reshape(g2.shape), mom1[name].reshape(g2.shape), mom2[name].reshape(g2.shape), g2, name=f"adamw_{name}")
        out_g[name], out_d[name], out_m[name], out_v[name] = g2.reshape(shape), d2.reshape(shape), m2.reshape(shape), v2.reshape(shape)
        return d2

    def update_layer(name, g2, layer, prev):
        res = _adamw_layer(wts[name], mom1[name], mom2[name], g2, layer, prev, name=f"adamw_{name}{layer}")
        out_g[name], out_d[name], out_m[name], out_v[name] = res
        return res

    g_down1, g_up1 = red["ffn1"].finish(red["hg"].token)
    down1 = update_layer("ffn_w_down", g_down1, 1, None)
    up1 = update_layer("ffn_w_up", g_up1, 1, None)
    red["mid"].to_core(up1[1])
    g_o, g_q, g_kv, g_down0, g_up0 = red["mid"].finish(up1[2])
    update("attn_w_o", g_o)
    update("attn_w_q", g_q)
    update("w_kv", g_kv)
    update_layer("ffn_w_down", g_down0, 0, down1)
    last = update_layer("ffn_w_up", g_up0, 0, up1)
    red["hg"].to_core(last[1])
    g_out, g_in = red["hg"].finish(last[2])
    update("hg_w_out", g_out)
    update("hg_w_in", g_in)

    small_names = [n for n in names if n not in out_g]
    cat = lambda d: jnp.concatenate([d[n].reshape(-1) for n in small_names])
    n_flat = sum(wts[n].size for n in small_names)
    srows = -(-n_flat // (SUBLANES * LANES)) * SUBLANES
    packed = [_pad_rows(cat(d), srows, LANES) for d in (wts, mom1, mom2, small_grads)]
    d_s, m_s, v_s = _adamw(*packed, name="adamw_small")
    off = 0
    for n in small_names:
        sz, shape = wts[n].size, wts[n].shape
        out_g[n] = small_grads[n].reshape(shape)
        out_d[n] = d_s.reshape(-1)[off:off + sz].reshape(shape)
        out_m[n] = m_s.reshape(-1)[off:off + sz].reshape(shape)
        out_v[n] = v_s.reshape(-1)[off:off + sz].reshape(shape)
        off += sz

    grad_x = dx.reshape(x.shape)
    return (loss_out, grad_x, *[out_g[n] for n in names], *[out_d[n] for n in names], *[out_m[n] for n in names], *[out_v[n] for n in names])
```

```python
import functools

import jax
import jax.numpy as jnp
from jax import lax
from jax.experimental import pallas as pl
from jax.experimental.pallas import tpu as pltpu

F32 = jnp.float32
BF16 = jnp.bfloat16
MESH = pl.DeviceIdType.MESH

EPS = 1e-6
D_MODEL = 1024
HG_HEADS = 8
HG_DK = 128
HG_CHUNK = 64
ATT_HD = 64
ATT_QH = 16
ATT_KVH = 2
ATT_GROUP = ATT_QH // ATT_KVH
WINDOW = 128
D_FF = 2816
N_CHIPS = 4
N_DEV = 8
LANES = 128
SUBLANES = 8
VMEM_LIMIT_BYTES = 56 * 1024 * 1024
NEG = -1e30
ALIBI_SLOPES = tuple(2.0 ** (-8.0 * h / ATT_QH) for h in range(1, ATT_QH + 1))

ADAM_LR = 0.001
ADAM_B1 = 0.9
ADAM_B2 = 0.999
ADAM_EPS = 1e-08
ADAM_WD = 0.01
ADAM_STEP = 10


def _cparams(sem=None):
    return pltpu.CompilerParams(dimension_semantics=sem, vmem_limit_bytes=VMEM_LIMIT_BYTES)


def _pick(n, cands):
    for c in cands:
        if n % c == 0:
            return c
    return n


def _sigmoid(x):
    return 0.5 * jnp.tanh(0.5 * x) + 0.5


def _dot(a, b, dims):
    return lax.dot_general(a, b, (dims, ((), ())), preferred_element_type=F32)


NN = ((1,), (0,))
NT = ((1,), (1,))
TN = ((0,), (0,))


MM_ROWS = 1024


def _rms_stats(xv):
    rstd = lax.rsqrt(jnp.mean(xv * xv, axis=-1, keepdims=True) + EPS)
    return xv * rstd, rstd


def _mm_operand(a_ref, gain_ref):
    if gain_ref is None:
        return a_ref[...].astype(BF16)
    return (_rms_stats(a_ref[...])[0] * gain_ref[...]).astype(BF16)


def _mm_nn(a, w, res=None, out_dtype=F32, name="mm_nn", gain=None):
    m, k = a.shape
    s, _, ns = w.shape
    tm = min(m, MM_ROWS)
    tn = _pick(ns, (1024, 1408, 512, 256, 128))
    npb = ns // tn

    def body(a_ref, w_ref, *rest):
        o_ref = rest[-1]
        acc = _dot(_mm_operand(a_ref, rest[0] if gain is not None else None), w_ref[...], NN)
        if res is not None:
            acc = acc + rest[-2][...]
        o_ref[...] = acc.astype(o_ref.dtype)

    in_specs = [
        pl.BlockSpec((tm, k), lambda i, j: (i, 0)),
        pl.BlockSpec((None, k, tn), lambda i, j: (j // npb, 0, j % npb)),
    ]
    args = [a, w]
    if gain is not None:
        in_specs.append(pl.BlockSpec((1, k), lambda i, j: (0, 0)))
        args.append(gain)
    if res is not None:
        in_specs.append(pl.BlockSpec((tm, tn), lambda i, j: (i, j)))
        args.append(res)
    return pl.pallas_call(
        body,
        name=name,
        grid=(m // tm, s * npb),
        in_specs=in_specs,
        out_specs=pl.BlockSpec((tm, tn), lambda i, j: (i, j)),
        out_shape=jax.ShapeDtypeStruct((m, s * ns), out_dtype),
        compiler_params=_cparams(("parallel", "parallel")),
    )(*args)


def _dy_spec(stacked, tm, tn, npb, row, kk):
    if stacked:
        return pl.BlockSpec((None, tm, tn), lambda *g: (kk(g) // npb, row(g), kk(g) % npb))
    return pl.BlockSpec((tm, tn), lambda *g: (row(g), kk(g)))


def _dep_specs(deps):
    return [pl.BlockSpec(d.shape, lambda *g: (0, 0)) for d in deps]


def _mm_nt(dy, w, stacked=False, out_dtype=F32, name="mm_nt", deps=(), norm_of=None):
    s, k, ns = w.shape
    m = dy.shape[1] if stacked else dy.shape[0]
    tm = min(m, MM_ROWS)
    tko = _pick(k, (1024, 1408, 512, 256))
    tn = _pick(ns, (1024, 1408, 512, 256))
    npb = ns // tn
    nk = s * npb
    fused = norm_of is not None
    assert not fused or tko == k

    def body(dy_ref, w_ref, *rest):
        acc_ref = rest[-1]
        i, kk = pl.program_id(0), pl.program_id(2)

        @pl.when(kk == 0)
        def _():
            acc_ref[...] = jnp.zeros_like(acc_ref)

        acc_ref[...] += _dot(dy_ref[...].astype(BF16), w_ref[...], NT)

        if not fused:
            @pl.when(kk == nk - 1)
            def _():
                rest[-2][...] = acc_ref[...].astype(rest[-2].dtype)
            return
        x_ref, g_ref, dres_ref = rest[:3]
        dx_ref, dg_ref = rest[-3], rest[-2]

        @pl.when(jnp.logical_and(i == 0, kk == 0))
        def _():
            dg_ref[...] = jnp.zeros_like(dg_ref)

        @pl.when(kk == nk - 1)
        def _():
            dxn = acc_ref[...]
            xhat, rstd = _rms_stats(x_ref[...])
            gd = dxn * g_ref[...]
            dx_ref[...] = dres_ref[...] + rstd * (gd - xhat * jnp.mean(gd * xhat, axis=-1, keepdims=True))
            dg_ref[...] += jnp.sum(dxn * xhat, axis=0, keepdims=True)

    row = pl.BlockSpec((tm, tko), lambda i, j, kk: (i, j))
    vec = pl.BlockSpec((1, k), lambda i, j, kk: (0, 0))
    return pl.pallas_call(
        body,
        name=name,
        grid=(m // tm, k // tko, nk),
        in_specs=[
            _dy_spec(stacked, tm, tn, npb, lambda g: g[0], lambda g: g[2]),
            pl.BlockSpec((None, tko, tn), lambda i, j, kk: (kk // npb, j, kk % npb)),
        ] + ([row, vec, row] if fused else []) + _dep_specs(deps),
        out_specs=[row, vec] if fused else row,
        out_shape=[jax.ShapeDtypeStruct((m, k), F32), jax.ShapeDtypeStruct((1, k), F32)] if fused else jax.ShapeDtypeStruct((m, k), out_dtype),
        scratch_shapes=[pltpu.VMEM((tm, tko), F32)],
        compiler_params=_cparams(("arbitrary",) * 3 if fused else ("parallel", "parallel", "arbitrary")),
    )(dy, w, *(norm_of or ()), *deps)


def _mm_tn(a, dy, s, ns, stacked=False, name="mm_tn", deps=(), gain=None):
    m, k = a.shape
    tm = min(m, MM_ROWS)
    tk = _pick(k, (1024, 1408, 512, 256))
    tn = _pick(ns, (1024, 1408, 512, 256, 128))
    npb = ns // tn
    nm = m // tm
    assert gain is None or tk == k

    def body(a_ref, dy_ref, *rest):
        o_ref, acc_ref = rest[-2:]
        mm = pl.program_id(2)

        @pl.when(mm == 0)
        def _():
            acc_ref[...] = jnp.zeros_like(acc_ref)

        acc_ref[...] += _dot(_mm_operand(a_ref, rest[0] if gain is not None else None), dy_ref[...].astype(BF16), TN)

        @pl.when(mm == nm - 1)
        def _():
            o_ref[...] = acc_ref[...]

    return pl.pallas_call(
        body,
        name=name,
        grid=(k // tk, s * npb, nm),
        in_specs=[
            pl.BlockSpec((tm, tk), lambda i, j, mm: (mm, i)),
            _dy_spec(stacked, tm, tn, npb, lambda g: g[2], lambda g: g[1]),
        ] + ([pl.BlockSpec((1, k), lambda i, j, mm: (0, 0))] if gain is not None else []) + _dep_specs(deps),
        out_specs=pl.BlockSpec((None, tk, tn), lambda i, j, mm: (j // npb, i, j % npb)),
        out_shape=jax.ShapeDtypeStruct((s, k, ns), F32),
        scratch_shapes=[pltpu.VMEM((tk, tn), F32)],
        compiler_params=_cparams(("parallel", "parallel", "arbitrary")),
    )(a, dy, *(() if gain is None else (gain,)), *deps)


ROW_TILE = 512


def _loss_head(h, g, target):
    t, d = h.shape
    r = min(t, ROW_TILE)

    def body(h_ref, g_ref, t_ref, dh_ref, dg_ref, loss_ref):
        @pl.when(pl.program_id(0) == 0)
        def _():
            dg_ref[...] = jnp.zeros_like(dg_ref)
            loss_ref[...] = jnp.zeros_like(loss_ref)

        xv = h_ref[...]
        rstd = lax.rsqrt(jnp.mean(xv * xv, axis=-1, keepdims=True) + EPS)
        xhat = xv * rstd
        gv = g_ref[...]
        err = xhat * gv - t_ref[...]
        loss_ref[...] += 0.5 * jnp.sum(jnp.mean(err * err, axis=-1, keepdims=True), axis=0, keepdims=True)
        dy = err * (1.0 / d)
        gd = dy * gv
        dh_ref[...] = rstd * (gd - xhat * jnp.mean(gd * xhat, axis=-1, keepdims=True))
        dg_ref[...] += jnp.sum(dy * xhat, axis=0, keepdims=True)

    return pl.pallas_call(
        body,
        name="loss_head",
        grid=(t // r,),
        in_specs=[
            pl.BlockSpec((r, d), lambda i: (i, 0)),
            pl.BlockSpec((1, d), lambda i: (0, 0)),
            pl.BlockSpec((r, d), lambda i: (i, 0)),
        ],
        out_specs=[
            pl.BlockSpec((r, d), lambda i: (i, 0)),
            pl.BlockSpec((1, d), lambda i: (0, 0)),
            pl.BlockSpec((1, LANES), lambda i: (0, 0)),
        ],
        out_shape=[
            jax.ShapeDtypeStruct((t, d), F32),
            jax.ShapeDtypeStruct((1, d), F32),
            jax.ShapeDtypeStruct((1, LANES), F32),
        ],
        compiler_params=_cparams(("arbitrary",)),
    )(h, g, target)


CONV_ROWS = 256
CONV_COLS = 1408


def _conv_taps(x_ext, n):
    tot = x_ext.shape[0]
    g1 = pltpu.roll(x_ext, 1, 0)[tot - n:]
    g2 = pltpu.roll(x_ext, 2, 0)[tot - n:]
    return g2, g1


def _conv_fwd(up, conv_w, conv_b, name="conv_fwd"):
    t = up.shape[0]
    r = min(t, CONV_ROWS)
    tc = CONV_COLS
    ncb = D_FF // tc
    hb = r // SUBLANES

    def body(g_ref, halo_ref, v_ref, w_ref, b_ref, o_ref):
        i = pl.program_id(1)
        g0 = g_ref[...]
        halo = halo_ref[...] * jnp.where(i > 0, 1.0, 0.0)
        g2, g1 = _conv_taps(jnp.concatenate([halo, g0], axis=0), r)
        c = b_ref[...] + w_ref[0:1, :] * g2 + w_ref[1:2, :] * g1 + w_ref[2:3, :] * g0
        o_ref[...] = (c * _sigmoid(c) * v_ref[...]).astype(BF16)

    return pl.pallas_call(
        body,
        name=name,
        grid=(ncb, t // r),
        in_specs=[
            pl.BlockSpec((r, tc), lambda j, i: (i, j)),
            pl.BlockSpec((SUBLANES, tc), lambda j, i: (jnp.maximum(i * hb - 1, 0), j)),
            pl.BlockSpec((r, tc), lambda j, i: (i, ncb + j)),
            pl.BlockSpec((3, tc), lambda j, i: (0, j)),
            pl.BlockSpec((1, tc), lambda j, i: (0, j)),
        ],
        out_specs=pl.BlockSpec((r, tc), lambda j, i: (i, j)),
        out_shape=jax.ShapeDtypeStruct((t, D_FF), BF16),
        compiler_params=_cparams(("parallel", "parallel")),
    )(up, up, up, conv_w, conv_b)


def _conv_bwd(up, conv_w, conv_b, dact, name="conv_bwd"):
    t = up.shape[0]
    r = min(t, CONV_ROWS)
    tc = CONV_COLS
    ncb = D_FF // tc
    hb = r // SUBLANES
    nrt = t // r

    def body(g_ref, halo_ref, v_ref, w_ref, b_ref, da_ref, dup_ref, dw_ref, db_ref, nxt_ref):
        ii = pl.program_id(1)
        i = nrt - 1 - ii

        @pl.when(ii == 0)
        def _():
            nxt_ref[...] = jnp.zeros_like(nxt_ref)
            dw_ref[...] = jnp.zeros_like(dw_ref)
            db_ref[...] = jnp.zeros_like(db_ref)

        g0 = g_ref[...]
        halo = halo_ref[...] * jnp.where(i > 0, 1.0, 0.0)
        g2, g1 = _conv_taps(jnp.concatenate([halo, g0], axis=0), r)
        w0, w1, w2 = w_ref[0:1, :], w_ref[1:2, :], w_ref[2:3, :]
        c = b_ref[...] + w0 * g2 + w1 * g1 + w2 * g0
        sg = _sigmoid(c)
        da = da_ref[...]
        dval = da * (c * sg)
        dc = da * v_ref[...] * (sg * (1.0 + c * (1.0 - sg)))
        db_ref[...] += jnp.sum(dc, axis=0, keepdims=True)
        dw_ref[0:1, :] += jnp.sum(dc * g2, axis=0, keepdims=True)
        dw_ref[1:2, :] += jnp.sum(dc * g1, axis=0, keepdims=True)
        dw_ref[2:3, :] += jnp.sum(dc * g0, axis=0, keepdims=True)
        ext = jnp.concatenate([dc, nxt_ref[...]], axis=0)
        tot = r + SUBLANES
        d1 = pltpu.roll(ext, tot - 1, 0)[:r]
        d2 = pltpu.roll(ext, tot - 2, 0)[:r]
        dgate = w2 * dc + w1 * d1 + w0 * d2
        nxt_ref[...] = dc[:SUBLANES]
        dup_ref[0] = dgate.astype(BF16)
        dup_ref[1] = dval.astype(BF16)

    rev = lambda ii: nrt - 1 - ii
    dup, dw, db = pl.pallas_call(
        body,
        name=name,
        grid=(ncb, nrt),
        in_specs=[
            pl.BlockSpec((r, tc), lambda j, ii: (rev(ii), j)),
            pl.BlockSpec((SUBLANES, tc), lambda j, ii: (jnp.maximum(rev(ii) * hb - 1, 0), j)),
            pl.BlockSpec((r, tc), lambda j, ii: (rev(ii), ncb + j)),
            pl.BlockSpec((3, tc), lambda j, ii: (0, j)),
            pl.BlockSpec((1, tc), lambda j, ii: (0, j)),
            pl.BlockSpec((r, tc), lambda j, ii: (rev(ii), j)),
        ],
        out_specs=[
            pl.BlockSpec((2, None, r, tc), lambda j, ii: (0, j, rev(ii), 0)),
            pl.BlockSpec((3, tc), lambda j, ii: (0, j)),
            pl.BlockSpec((1, tc), lambda j, ii: (0, j)),
        ],
        out_shape=[
            jax.ShapeDtypeStruct((2, ncb, t, tc), BF16),
            jax.ShapeDtypeStruct((3, D_FF), F32),
            jax.ShapeDtypeStruct((1, D_FF), F32),
        ],
        scratch_shapes=[pltpu.VMEM((SUBLANES, tc), F32)],
        compiler_params=_cparams(("parallel", "arbitrary")),
    )(up, up, up, conv_w, conv_b, dact)
    return dup.reshape(2 * ncb, t, tc), dw, db


def _split3(x):
    x1 = x.astype(BF16)
    r1 = x - x1.astype(F32)
    x2 = r1.astype(BF16)
    x3 = (r1 - x2.astype(F32)).astype(BF16)
    return x1, x2, x3


def _tri_dot(tri, x, dims):
    x1, x2, x3 = _split3(x)
    return _dot(tri, x1, dims) + _dot(tri, x2, dims) + _dot(tri, x3, dims)


def _lower_bound(logits_ref):
    return _sigmoid(logits_ref[0:1, :] - logits_ref[1:2, :])


def _hg_gates(qr, fr, lb):
    q = qr * _sigmoid(qr) * (HG_DK ** -0.5)
    sf = _sigmoid(fr)
    fg = lb + (1.0 - lb) * sf
    return q, sf, fg


def _hg_chunk_terms(q, fg, tril_b, low_half):
    g = jnp.log(fg)
    k = 1.0 - fg
    cum = _tri_dot(tril_b, g, NN)
    c_last = jnp.sum(g, axis=0, keepdims=True)
    c_mid = jnp.sum(jnp.where(low_half, g, 0.0), axis=0, keepdims=True)
    e_q = jnp.exp(cum - c_mid)
    e_k = jnp.exp(c_mid - cum)
    e_0 = jnp.exp(cum)
    e_l = jnp.exp(c_last - cum)
    return k, e_q, e_k, e_0, e_l, jnp.exp(c_last)


HG_BLOCK = 256


def _hg_proj_specs(rb, row):
    return [pl.BlockSpec((rb, D_MODEL), functools.partial(lambda i, k: (row(i), k), k=k)) for k in range(4)]


def _hg_consts(c):
    tril = lax.broadcasted_iota(jnp.int32, (c, c), 0) >= lax.broadcasted_iota(jnp.int32, (c, c), 1)
    low_half = lax.broadcasted_iota(jnp.int32, (c, D_MODEL), 0) < c // 2
    return tril, tril.astype(BF16), low_half


def _hgrn_fwd(proj, lb, wn):
    t = proj.shape[0]
    c = HG_CHUNK
    rb = min(t, HG_BLOCK)
    cpb = rb // c

    def body(q_ref, f_ref, i_ref, g_ref, lb_ref, wn_ref, o_ref, y_ref, st_ref, s_scr):
        @pl.when(pl.program_id(0) == 0)
        def _():
            s_scr[...] = jnp.zeros_like(s_scr)

        lb_all = _lower_bound(lb_ref)
        wnv = wn_ref[...]
        tril, tril_b, low_half = _hg_consts(c)

        def chunk(n, carry):
            rows = pl.ds(pl.multiple_of(n * c, c), c)
            q, _, fg = _hg_gates(q_ref[rows, :], f_ref[rows, :], lb_all)
            k, e_q, e_k, e_0, e_l, e_last = _hg_chunk_terms(q, fg, tril_b, low_half)
            qi, ki, q0, kl = (q * e_q).astype(BF16), (k * e_k).astype(BF16), (q * e_0).astype(BF16), (k * e_l).astype(BF16)
            v = i_ref[rows, :].astype(BF16)
            gr = g_ref[rows, :]
            gate = gr * _sigmoid(gr)
            for h in range(HG_HEADS):
                cols = slice(h * HG_DK, (h + 1) * HG_DK)
                st = s_scr[h]
                st_ref[h, n] = st
                a = jnp.where(tril, _dot(qi[:, cols], ki[:, cols], NT), 0.0)
                o = _dot(q0[:, cols], st.astype(BF16), NT) + _dot(a.astype(BF16), v[:, cols], NN)
                s_scr[h] = st * e_last[:, cols] + _dot(v[:, cols], kl[:, cols], TN)
                o_ref[rows, cols] = o
                rstd = lax.rsqrt(jnp.mean(o * o, axis=-1, keepdims=True) + EPS)
                y_ref[rows, cols] = (o * rstd * wnv * gate[:, cols]).astype(BF16)
            return carry

        lax.fori_loop(0, cpb, chunk, 0)

    blk = pl.BlockSpec((rb, D_MODEL), lambda i: (i, 0))
    return pl.pallas_call(
        body,
        name="hgrn_fwd",
        grid=(t // rb,),
        in_specs=_hg_proj_specs(rb, lambda i: i) + [pl.BlockSpec((2, D_MODEL), lambda i: (0, 0)), pl.BlockSpec((1, HG_DK), lambda i: (0, 0))],
        out_specs=[blk, blk, pl.BlockSpec((HG_HEADS, cpb, HG_DK, HG_DK), lambda i: (0, i, 0, 0))],
        out_shape=[
            jax.ShapeDtypeStruct((t, D_MODEL), F32),
            jax.ShapeDtypeStruct((t, D_MODEL), BF16),
            jax.ShapeDtypeStruct((HG_HEADS, t // c, HG_DK, HG_DK), F32),
        ],
        scratch_shapes=[pltpu.VMEM((HG_HEADS, HG_DK, HG_DK), F32)],
        compiler_params=_cparams(("arbitrary",)),
    )(proj, proj, proj, proj, lb, wn)


def _hgrn_bwd(proj, lb, wn, o, states, dy):
    t = proj.shape[0]
    c = HG_CHUNK
    rb = min(t, HG_BLOCK)
    cpb = rb // c
    nb = t // rb

    def body(q_ref, f_ref, i_ref, g_ref, lb_ref, wn_ref, o_ref, st_ref, dy_ref, dp_ref, dl_ref, dwn_ref, ds_scr, dlb_scr):
        step = pl.program_id(0)

        @pl.when(step == 0)
        def _():
            dwn_ref[...] = jnp.zeros_like(dwn_ref)
            ds_scr[...] = jnp.zeros_like(ds_scr)
            dlb_scr[...] = jnp.zeros_like(dlb_scr)

        lb_all = _lower_bound(lb_ref)
        wnv = wn_ref[...]
        tril, tril_b, low_half = _hg_consts(c)

        def chunk(nn, carry):
            n = cpb - 1 - nn
            rows = pl.ds(pl.multiple_of(n * c, c), c)
            qr = q_ref[rows, :]
            gr = g_ref[rows, :]
            q, sf, fg = _hg_gates(qr, f_ref[rows, :], lb_all)
            k, e_q, e_k, e_0, e_l, e_last = _hg_chunk_terms(q, fg, tril_b, low_half)
            qi, qi_lo, _ = _split3(q * e_q)
            ki, ki_lo, _ = _split3(k * e_k)
            q0 = (q * e_0).astype(BF16)
            kl = (k * e_l).astype(BF16)
            v = i_ref[rows, :].astype(BF16)
            sg = _sigmoid(gr)
            silu_g = gr * sg
            dsilu_g = sg * (1.0 + gr * (1.0 - sg))
            dqs, dks, d_lasts = [], [], []
            for h in range(HG_HEADS):
                cols = slice(h * HG_DK, (h + 1) * HG_DK)
                ov = o_ref[rows, cols]
                dyv = dy_ref[rows, cols].astype(F32)
                rstd = lax.rsqrt(jnp.mean(ov * ov, axis=-1, keepdims=True) + EPS)
                ohat = ov * rstd
                dp_ref[3, rows, cols] = (dyv * (ohat * wnv) * dsilu_g[:, cols]).astype(BF16)
                don = dyv * silu_g[:, cols]
                dwn_ref[...] += jnp.sum(don * ohat, axis=0, keepdims=True)
                gd = don * wnv
                do_b = (rstd * (gd - ohat * jnp.mean(gd * ohat, axis=-1, keepdims=True))).astype(BF16)
                st = st_ref[h, n]
                ds = ds_scr[h]
                ds_b = ds.astype(BF16)
                vh, kh = v[:, cols], k[:, cols]
                a_b = jnp.where(tril, _dot(qi[:, cols], ki[:, cols], NT), 0.0).astype(BF16)
                da_b = jnp.where(tril, _dot(do_b, vh, NT), 0.0).astype(BF16)
                dqs.append(_dot(do_b, st.astype(BF16), NN) * e_0[:, cols]
                           + (_dot(da_b, ki[:, cols], NN) + _dot(da_b, ki_lo[:, cols], NN)) * e_q[:, cols])
                dk_state = _dot(vh, ds_b, NN) * e_l[:, cols]
                dks.append((_dot(da_b, qi[:, cols], TN) + _dot(da_b, qi_lo[:, cols], TN)) * e_k[:, cols] + dk_state)
                dp_ref[2, rows, cols] = (_dot(a_b, do_b, TN) + _dot(kl[:, cols], ds_b, NT)).astype(BF16)
                ds_scr[h] = ds * e_last[:, cols] + _dot(do_b, q0[:, cols], TN)
                d_lasts.append(jnp.sum(dk_state * kh, axis=0, keepdims=True) + jnp.sum(ds * st, axis=0, keepdims=True) * e_last[:, cols])
            dq = jnp.concatenate(dqs, axis=1)
            dk = jnp.concatenate(dks, axis=1)
            dlogf = _tri_dot(tril_b, q * dq - k * dk, TN) + jnp.concatenate(d_lasts, axis=1)
            dfg = dlogf / fg - dk
            dlb_scr[...] += jnp.sum(dfg * (1.0 - sf), axis=0, keepdims=True)
            sq = _sigmoid(qr)
            dp_ref[0, rows, :] = (dq * (HG_DK ** -0.5) * (sq * (1.0 + qr * (1.0 - sq)))).astype(BF16)
            dp_ref[1, rows, :] = (dfg * (1.0 - lb_all) * sf * (1.0 - sf)).astype(BF16)
            return carry

        lax.fori_loop(0, cpb, chunk, 0)

        @pl.when(step == nb - 1)
        def _():
            d0 = dlb_scr[...] * lb_all * (1.0 - lb_all)
            dl_ref[0:1, :] = d0
            dl_ref[1:2, :] = -d0

    rev = lambda i: nb - 1 - i
    blk = pl.BlockSpec((rb, D_MODEL), lambda i: (rev(i), 0))
    return pl.pallas_call(
        body,
        name="hgrn_bwd",
        grid=(nb,),
        in_specs=_hg_proj_specs(rb, rev)
        + [pl.BlockSpec((2, D_MODEL), lambda i: (0, 0)), pl.BlockSpec((1, HG_DK), lambda i: (0, 0)), blk,
           pl.BlockSpec((HG_HEADS, cpb, HG_DK, HG_DK), lambda i: (0, rev(i), 0, 0)), blk],
        out_specs=[
            pl.BlockSpec((4, rb, D_MODEL), lambda i: (0, rev(i), 0)),
            pl.BlockSpec((2, D_MODEL), lambda i: (0, 0)),
            pl.BlockSpec((1, HG_DK), lambda i: (0, 0)),
        ],
        out_shape=[
            jax.ShapeDtypeStruct((4, t, D_MODEL), BF16),
            jax.ShapeDtypeStruct((2, D_MODEL), F32),
            jax.ShapeDtypeStruct((1, HG_DK), F32),
        ],
        scratch_shapes=[pltpu.VMEM((HG_HEADS, HG_DK, HG_DK), F32), pltpu.VMEM((1, D_MODEL), F32)],
        compiler_params=_cparams(("arbitrary",)),
    )(proj, proj, proj, proj, lb, wn, o, states, dy)


ATT_STACK = 8


def _att_stack(q_ref, sink_ref, first, lo, bias_p, bias_c, extra_ref=None):
    qs, bps, bcs, sinks, extras = [], [], [], None, []
    rows = lax.broadcasted_iota(jnp.int32, (ATT_STACK * WINDOW, 1), 0)
    for i in range(ATT_STACK):
        hq = first + i
        cols = slice((hq // 2) * LANES, (hq // 2 + 1) * LANES)
        sel = lo if hq % 2 == 0 else jnp.logical_not(lo)
        qp = q_ref[:, cols] * (ATT_HD ** -0.5)
        qs.append(jnp.where(sel, qp, jnp.zeros_like(qp)))
        bps.append(ALIBI_SLOPES[hq] * bias_p)
        bcs.append(ALIBI_SLOPES[hq] * bias_c)
        sinks = sink_ref[hq] if sinks is None else jnp.where(rows < i * WINDOW, sinks, sink_ref[hq])
        if extra_ref is not None:
            ep = extra_ref[:, cols]
            extras.append(jnp.where(sel, ep, jnp.zeros_like(ep)))
    cat = lambda parts: jnp.concatenate(parts, axis=0)
    return cat(qs), cat(bps), cat(bcs), sinks, (cat(extras) if extras else None)


def _att_rows(i):
    return slice(i * WINDOW, (i + 1) * WINDOW)


def _att_bias(n):
    tq = lax.broadcasted_iota(jnp.int32, (WINDOW, WINDOW), 0)
    sk = lax.broadcasted_iota(jnp.int32, (WINDOW, WINDOW), 1)
    valid_c = sk <= tq
    valid_p = (sk - tq) > jnp.where(n > 0, 0, WINDOW)
    dist_c = (tq - sk).astype(F32)
    return jnp.where(valid_p, -dist_c - float(WINDOW), NEG), jnp.where(valid_c, -dist_c, NEG)


def _att_halves(x, lo, kh):
    r = pltpu.roll(x, ATT_HD, 1)
    zero = jnp.zeros_like(x)
    if kh == 0:
        return jnp.where(lo, x, r), jnp.where(lo, x, zero), jnp.where(lo, zero, r)
    return jnp.where(lo, r, x), jnp.where(lo, r, zero), jnp.where(lo, zero, x)


def _att_probs(qm, k2p, k2c, bias_p, bias_c, sink):
    sp = _dot(qm, k2p, NT) + bias_p
    sc = _dot(qm, k2c, NT) + bias_c
    m = jnp.maximum(jnp.maximum(jnp.max(sp, axis=-1, keepdims=True), jnp.max(sc, axis=-1, keepdims=True)), sink)
    ep = jnp.exp(sp - m)
    ec = jnp.exp(sc - m)
    es = jnp.exp(sink - m)
    inv = 1.0 / (jnp.sum(ep, axis=-1, keepdims=True) + jnp.sum(ec, axis=-1, keepdims=True) + es)
    return ep * inv, ec * inv, es * inv


def _attn_fwd(q, kv, sinks):
    t = q.shape[0]
    nb = t // WINDOW

    def body(sink_ref, q_ref, kvp_ref, kvc_ref, o_ref):
        n = pl.program_id(0)
        bias_p, bias_c = _att_bias(n)
        lo = lax.broadcasted_iota(jnp.int32, (WINDOW, LANES), 1) < ATT_HD
        for kh in range(ATT_KVH):
            k2p, _, _ = _att_halves(kvp_ref[:, 0:LANES], lo, kh)
            k2c, _, _ = _att_halves(kvc_ref[:, 0:LANES], lo, kh)
            _, vlo_p, vhi_p = _att_halves(kvp_ref[:, LANES:2 * LANES], lo, kh)
            _, vlo_c, vhi_c = _att_halves(kvc_ref[:, LANES:2 * LANES], lo, kh)
            for first in range(kh * ATT_GROUP, (kh + 1) * ATT_GROUP, ATT_STACK):
                qs, bp, bc, sinks, _ = _att_stack(q_ref, sink_ref, first, lo, bias_p, bias_c)
                pp, pc, _ = _att_probs(qs, k2p, k2c, bp, bc, sinks)
                pp, pc = pp.astype(BF16), pc.astype(BF16)
                for i in range(0, ATT_STACK, 2):
                    even, odd = _att_rows(i), _att_rows(i + 1)
                    out = (_dot(pp[even], vlo_p, NN) + _dot(pc[even], vlo_c, NN)
                           + _dot(pp[odd], vhi_p, NN) + _dot(pc[odd], vhi_c, NN))
                    j = (first + i) // 2
                    o_ref[:, j * LANES:(j + 1) * LANES] = out.astype(BF16)

    return pl.pallas_call(
        body,
        name="attn_fwd",
        grid=(nb,),
        in_specs=[
            pl.BlockSpec(memory_space=pltpu.SMEM),
            pl.BlockSpec((WINDOW, D_MODEL), lambda n: (n, 0)),
            pl.BlockSpec((WINDOW, 2 * LANES), lambda n: (jnp.maximum(n - 1, 0), 0)),
            pl.BlockSpec((WINDOW, 2 * LANES), lambda n: (n, 0)),
        ],
        out_specs=pl.BlockSpec((WINDOW, D_MODEL), lambda n: (n, 0)),
        out_shape=jax.ShapeDtypeStruct((t, D_MODEL), BF16),
        compiler_params=_cparams(("parallel",)),
    )(sinks, q, kv, kv)


def _attn_bwd(q, kv, sinks, dout):
    t = q.shape[0]
    nb = t // WINDOW

    def body(sink_ref, q_ref, kvp_ref, kvc_ref, do_ref, dq_ref, dkv_ref, dsink_ref, carry_ref):
        n = pl.program_id(0)

        @pl.when(n == 0)
        def _():
            carry_ref[...] = jnp.zeros_like(carry_ref)
            dsink_ref[...] = jnp.zeros_like(dsink_ref)

        @pl.when(n == nb)
        def _():
            dkv_ref[...] = carry_ref[...].astype(BF16)

        @pl.when(n < nb)
        def _():
            bias_p, bias_c = _att_bias(n)
            lo = lax.broadcasted_iota(jnp.int32, (WINDOW, LANES), 1) < ATT_HD
            lane1 = lax.broadcasted_iota(jnp.int32, (1, LANES), 1)
            dsink = jnp.zeros((1, LANES), F32)
            halves = []
            for kh in range(ATT_KVH):
                k2p, klo_p, khi_p = _att_halves(kvp_ref[:, 0:LANES], lo, kh)
                k2c, klo_c, khi_c = _att_halves(kvc_ref[:, 0:LANES], lo, kh)
                v2p, _, _ = _att_halves(kvp_ref[:, LANES:2 * LANES], lo, kh)
                v2c, _, _ = _att_halves(kvc_ref[:, LANES:2 * LANES], lo, kh)
                acc = [jnp.zeros((WINDOW, LANES), F32) for _ in range(4)]
                for first in range(kh * ATT_GROUP, (kh + 1) * ATT_GROUP, ATT_STACK):
                    qs, bp, bc, sinks, dos = _att_stack(q_ref, sink_ref, first, lo, bias_p, bias_c, do_ref)
                    pp, pc, ps = _att_probs(qs, k2p, k2c, bp, bc, sinks)
                    dpp = _dot(dos, v2p, NT)
                    dpc = _dot(dos, v2c, NT)
                    delta = jnp.sum(pp * dpp, axis=-1, keepdims=True) + jnp.sum(pc * dpc, axis=-1, keepdims=True)
                    dsp = (pp * (dpp - delta)).astype(BF16)
                    dsc = (pc * (dpc - delta)).astype(BF16)
                    sink_term = ps * delta
                    for i in range(ATT_STACK):
                        dsink = dsink + jnp.where(lane1 == first + i, -jnp.sum(sink_term[_att_rows(i)], axis=0, keepdims=True), 0.0)
                    for i in range(0, ATT_STACK, 2):
                        even, odd = _att_rows(i), _att_rows(i + 1)
                        dq_pair = (_dot(dsp[even], klo_p, NN) + _dot(dsc[even], klo_c, NN)
                                   + _dot(dsp[odd], khi_p, NN) + _dot(dsc[odd], khi_c, NN))
                        j = (first + i) // 2
                        dq_ref[:, j * LANES:(j + 1) * LANES] = (dq_pair * (ATT_HD ** -0.5)).astype(BF16)
                    acc[0] = acc[0] + _dot(dsp, qs, TN)
                    acc[1] = acc[1] + _dot(dsc, qs, TN)
                    acc[2] = acc[2] + _dot(pp.astype(BF16), dos, TN)
                    acc[3] = acc[3] + _dot(pc.astype(BF16), dos, TN)
                halves.append([a + pltpu.roll(a, ATT_HD, 1) for a in acc])
            prev = jnp.concatenate(
                [jnp.where(lo, halves[0][0], halves[1][0]), jnp.where(lo, halves[0][2], halves[1][2])], axis=1)
            cur = jnp.concatenate(
                [jnp.where(lo, halves[0][1], halves[1][1]), jnp.where(lo, halves[0][3], halves[1][3])], axis=1)
            dkv_ref[...] = (carry_ref[...] + prev).astype(BF16)
            carry_ref[...] = cur
            dsink_ref[...] += dsink

    blk = lambda n: jnp.minimum(n, nb - 1)
    return pl.pallas_call(
        body,
        name="attn_bwd",
        grid=(nb + 1,),
        in_specs=[
            pl.BlockSpec(memory_space=pltpu.SMEM),
            pl.BlockSpec((WINDOW, D_MODEL), lambda n: (blk(n), 0)),
            pl.BlockSpec((WINDOW, 2 * LANES), lambda n: (jnp.maximum(blk(n) - 1, 0), 0)),
            pl.BlockSpec((WINDOW, 2 * LANES), lambda n: (blk(n), 0)),
            pl.BlockSpec((WINDOW, D_MODEL), lambda n: (blk(n), 0)),
        ],
        out_specs=[
            pl.BlockSpec((WINDOW, D_MODEL), lambda n: (blk(n), 0)),
            pl.BlockSpec((WINDOW, 2 * LANES), lambda n: (jnp.maximum(n - 1, 0), 0)),
            pl.BlockSpec((1, LANES), lambda n: (0, 0)),
        ],
        out_shape=[
            jax.ShapeDtypeStruct((t, D_MODEL), BF16),
            jax.ShapeDtypeStruct((t, 2 * LANES), BF16),
            jax.ShapeDtypeStruct((1, LANES), F32),
        ],
        scratch_shapes=[pltpu.VMEM((WINDOW, 2 * LANES), F32)],
        compiler_params=_cparams(("arbitrary",)),
    )(sinks, q, kv, kv, dout)


def _ffn_fwd(h, norm_g, w_up, conv_w, conv_b, w_down, tag, after_up=lambda up: None):
    up = _mm_nn(h, w_up, gain=norm_g, name=f"ffn{tag}_up")
    after_up(up)
    act = _conv_fwd(up, conv_w, conv_b, name=f"ffn{tag}_conv")
    h_out = _mm_nn(act, w_down, res=h, name=f"ffn{tag}_down")
    return h_out, (up, act)


def _ffn_bwd(dh, h, norm_g, w_up, conv_w, conv_b, w_down, saved, tag, deps=()):
    up, act = saved
    dw_down = _mm_tn(act, dh, 1, D_MODEL, name=f"ffn{tag}_dwdown", deps=deps)
    dact = _mm_nt(dh, w_down, name=f"ffn{tag}_dact", deps=deps)
    dup, dconv_w, dconv_b = _conv_bwd(up, conv_w, conv_b, dact, name=f"ffn{tag}_dconv")
    dw_up = _mm_tn(h, dup, N_CHIPS, CONV_COLS, stacked=True, gain=norm_g, name=f"ffn{tag}_dwup")
    dh_in, dnorm = _mm_nt(dup, w_up, stacked=True, norm_of=(h, norm_g, dh), name=f"ffn{tag}_dxn")
    return dh_in, dict(ffn_w_down=dw_down, ffn_w_up=dw_up, ffn_conv_w=dconv_w, ffn_conv_b=dconv_b, ffn_norm=dnorm)


def _local_step(x, target, w, fetch=lambda w, stage, after: w, hook=lambda point, dh, grads: ()):
    proj = _mm_nn(x, w["hg_w_in"], gain=w["hg_norm"], name="hg_in")
    o, y, states = _hgrn_fwd(proj, w["hg_lb"], w["hg_out_norm"])
    w = fetch(w, "mixer_out", y)
    fetch(w, "layer0_relay", y)
    h_a = _mm_nn(y, w["hg_w_out"], res=x, name="hg_out")
    w = fetch(w, "layer0", h_a)
    h1, ffn0 = _ffn_fwd(h_a, w["ffn_norm"][0], w["ffn_w_up"][0], w["ffn_conv_w"][0], w["ffn_conv_b"][0], w["ffn_w_down"][0], 0,
                        lambda up: fetch(w, "layer1_relay", up))
    w = fetch(w, "layer1", h1)
    kv = _mm_nn(h1, w["w_kv"], gain=w["kv_norm"], out_dtype=BF16, name="kv_proj")
    qa = _mm_nn(h1, w["attn_w_q"], gain=w["attn_norm"], out_dtype=BF16, name="attn_q")
    ao = _attn_fwd(qa, kv, w["attn_sinks"])
    h_b = _mm_nn(ao, w["attn_w_o"], res=h1, name="attn_o")
    h2, ffn1 = _ffn_fwd(h_b, w["ffn_norm"][1], w["ffn_w_up"][1], w["ffn_conv_w"][1], w["ffn_conv_b"][1], w["ffn_w_down"][1], 1)
    dh2, d_final, loss = _loss_head(h2, w["final_norm"], target)

    dh_b, g1 = _ffn_bwd(dh2, h_b, w["ffn_norm"][1], w["ffn_w_up"][1], w["ffn_conv_w"][1], w["ffn_conv_b"][1], w["ffn_w_down"][1], ffn1, 1)
    deps = hook("ffn1", dh_b, g1)
    dw_o = _mm_tn(ao, dh_b, 1, D_MODEL, name="attn_dwo", deps=deps)
    dao = _mm_nt(dh_b, w["attn_w_o"], out_dtype=BF16, name="attn_dao", deps=deps)
    dqa, dkv, dsinks = _attn_bwd(qa, kv, w["attn_sinks"], dao)
    dw_q = _mm_tn(h1, dqa, 1, D_MODEL, gain=w["attn_norm"], name="attn_dwq")
    dh1, d_attn_norm = _mm_nt(dqa, w["attn_w_q"], norm_of=(h1, w["attn_norm"], dh_b), name="attn_dxa")
    dw_kv = _mm_tn(h1, dkv, 1, 2 * LANES, gain=w["kv_norm"], name="kv_dw")
    dh1, d_kv_norm = _mm_nt(dkv, w["w_kv"], norm_of=(h1, w["kv_norm"], dh1), name="kv_dx")
    deps = hook("attn", dh1, dict(attn_w_o=dw_o, attn_w_q=dw_q, w_kv=dw_kv))
    dh_a, g0 = _ffn_bwd(dh1, h_a, w["ffn_norm"][0], w["ffn_w_up"][0], w["ffn_conv_w"][0], w["ffn_conv_b"][0], w["ffn_w_down"][0], ffn0, 0, deps)
    deps = hook("ffn0", dh_a, g0)
    dw_out = _mm_tn(y, dh_a, 1, D_MODEL, name="hg_dwout", deps=deps)
    dy = _mm_nt(dh_a, w["hg_w_out"], out_dtype=BF16, name="hg_dy", deps=deps)
    dproj, dlb, d_out_norm = _hgrn_bwd(proj, w["hg_lb"], w["hg_out_norm"], o, states, dy)
    deps = hook("hgrn", dproj, None)
    dw_in = _mm_tn(x, dproj, N_CHIPS, D_MODEL, stacked=True, gain=w["hg_norm"], name="hg_dwin", deps=deps)
    deps = hook("hg_w", dproj, dict(hg_w_out=dw_out, hg_w_in=dw_in))
    dx, d_hg_norm = _mm_nt(dproj, w["hg_w_in"], stacked=True, norm_of=(x, w["hg_norm"], dh_a), name="hg_dxn", deps=deps)

    grads = dict(
        hg_norm=d_hg_norm, hg_w_in=dw_in, hg_lb=dlb, hg_out_norm=d_out_norm, hg_w_out=dw_out,
        kv_norm=d_kv_norm, w_kv=dw_kv, attn_norm=d_attn_norm, attn_w_q=dw_q, attn_sinks=dsinks, attn_w_o=dw_o,
        final_norm=d_final,
    )
    for name in g0:
        grads[name] = [g0[name], g1[name]]
    return loss, dx, grads


ANY = pl.BlockSpec(memory_space=pl.ANY)


def _place():
    x, y, c = lax.axis_index("x"), lax.axis_index("y"), lax.axis_index("c")
    chips = [(1 - x, y), (x, 1 - y), (1 - x, 1 - y)]
    return x, y, c, chips


def _rcopy(src, dst, send_sem, recv_sem, to):
    return pltpu.make_async_remote_copy(src_ref=src, dst_ref=dst, send_sem=send_sem, recv_sem=recv_sem, device_id=to, device_id_type=MESH)


HBM = pl.BlockSpec(memory_space=pltpu.HBM)
SEM = pl.BlockSpec(memory_space=pltpu.SEMAPHORE)
EFFECT = pltpu.SideEffectType.DATAFLOW_SIDE_EFFECTING


def _in_hbm(a):
    return pltpu.with_memory_space_constraint(a, pltpu.HBM)


def _place_shard(shard, place, dtype, name, deps=()):
    r, cols = shard.shape
    tr = _pick(r, ELEM_ROWS)

    def body(place_ref, s_ref, *rest):
        o_ref = rest[-1]
        o_ref[...] = s_ref[...].astype(o_ref.dtype)

    return pl.pallas_call(
        body,
        name=name,
        grid_spec=pltpu.PrefetchScalarGridSpec(
            num_scalar_prefetch=1,
            grid=(r // tr,),
            in_specs=[pl.BlockSpec((tr, cols), lambda i, place_ref: (i, 0))] + _dep_specs(deps),
            out_specs=pl.BlockSpec((None, tr, cols), lambda i, place_ref: (place_ref[0], i, 0)),
        ),
        out_shape=jax.ShapeDtypeStruct((N_CHIPS, r, cols), dtype),
        compiler_params=_cparams(("parallel",)),
    )(place, shard, *deps)


def _start_copies(name, bufs, n_sem, copies):
    n = len(bufs)

    def body(*refs):
        for cp in copies(refs[:n], refs[n], refs[n + 1]):
            cp.start()
        refs[-1][...] = jnp.zeros_like(refs[-1])

    outs = pl.pallas_call(
        body,
        name=name,
        in_specs=[HBM] * n,
        out_specs=[SEM, SEM] + [HBM] * n + [pl.BlockSpec(memory_space=pltpu.VMEM)],
        out_shape=[pltpu.SemaphoreType.DMA((n_sem,)), pltpu.SemaphoreType.DMA((n_sem,))] + [pltpu.HBM(b.shape, b.dtype) for b in bufs]
        + [jax.ShapeDtypeStruct((SUBLANES, LANES), F32)],
        input_output_aliases={i: 2 + i for i in range(n)},
        compiler_params=pltpu.CompilerParams(has_side_effects=EFFECT),
    )(*[_in_hbm(b) for b in bufs])
    return outs[0], outs[1], list(outs[2:-1]), outs[-1]


def _wait_copies(name, bufs, send_sems, recv_sems, after, copies):
    n = len(bufs)

    def body(*refs):
        for cp in copies(refs[:n], refs[n], refs[n + 1]):
            cp.wait_send()
            cp.wait_recv()

    return pl.pallas_call(
        body,
        name=name,
        in_specs=[HBM] * n + [SEM, SEM, ANY],
        out_specs=[HBM] * n,
        out_shape=[pltpu.HBM(b.shape, b.dtype) for b in bufs],
        input_output_aliases={i: i for i in range(n)},
        compiler_params=pltpu.CompilerParams(has_side_effects=EFFECT),
    )(*bufs, send_sems, recv_sems, after)


def _relay_copies(name, bufs, send_sems, recv_sems, after, landed, n_sem, onward):
    n = len(bufs)

    def body(*refs):
        for cp in landed(refs[:n], refs[n], refs[n + 1]):
            cp.wait_send()
            cp.wait_recv()
        for cp in onward(refs[:n], refs[n + 3], refs[n + 4]):
            cp.start()
        refs[-1][...] = jnp.zeros_like(refs[-1])

    outs = pl.pallas_call(
        body,
        name=name,
        in_specs=[HBM] * n + [SEM, SEM, ANY],
        out_specs=[SEM, SEM] + [HBM] * n + [pl.BlockSpec(memory_space=pltpu.VMEM)],
        out_shape=[pltpu.SemaphoreType.DMA((n_sem,)), pltpu.SemaphoreType.DMA((n_sem,))] + [pltpu.HBM(b.shape, b.dtype) for b in bufs]
        + [jax.ShapeDtypeStruct((SUBLANES, LANES), F32)],
        input_output_aliases={i: 2 + i for i in range(n)},
        compiler_params=pltpu.CompilerParams(has_side_effects=EFFECT),
    )(*bufs, send_sems, recv_sems, after)
    return outs[0], outs[1], list(outs[2:-1]), outs[-1]


def _gather_half_copies(first, count, over_ici):
    def copies(refs, send_sems, recv_sems):
        x, y, c, chips = _place()
        out = []
        for i in range(count):
            h = refs[i].shape[1] // 2
            mine = pl.ds(c * h, h)
            for j, (px, py) in enumerate(chips):
                k = 3 * (first + i) + j
                slot = 2 * x + y if over_ici else 2 * px + py
                to = (px, py, c) if over_ici else (x, y, 1 - c)
                out.append(_rcopy(refs[i].at[slot, mine], refs[i].at[slot, mine], send_sems.at[k], recv_sems.at[k], to))
        return out

    return copies


def _gather_copies(first, count):
    def copies(refs, send_sems, recv_sems):
        x, y, c, chips = _place()
        me = 2 * x + y
        out = []
        for i in range(count):
            for j, (px, py) in enumerate(chips):
                k = 3 * (first + i) + j
                out.append(_rcopy(refs[i].at[me], refs[i].at[me], send_sems.at[k], recv_sems.at[k], (px, py, c)))
        return out

    return copies


def _swap_copies(n):
    def copies(refs, send_sems, recv_sems):
        x, y, c, _ = _place()
        out = []
        for i in range(n):
            h = refs[i].shape[1] // 2
            out.append(_rcopy(refs[i].at[:, pl.ds((1 - c) * h, h)], refs[n + i], send_sems.at[i], recv_sems.at[i], (x, y, 1 - c)))
        return out

    return copies


def _partial_copies(n):
    def copies(refs, send_sems, recv_sems):
        x, y, c, chips = _place()
        out = []
        for i in range(n):
            for j, (px, py) in enumerate(chips):
                out.append(_rcopy(refs[i].at[2 * px + py], refs[n + i].at[j], send_sems.at[3 * i + j], recv_sems.at[3 * i + j], (px, py, c)))
        return out

    return copies


def _share_copies(n):
    def copies(refs, send_sems, recv_sems):
        x, y, c, _ = _place()
        return [_rcopy(refs[i].at[c], refs[i].at[c], send_sems.at[i], recv_sems.at[i], (x, y, 1 - c)) for i in range(n)]

    return copies


def _allreduce_small(vec):
    rows = vec.shape[0]

    def body(v_ref, o_ref, buf, send_sems, recv_sems):
        x, y, c, _ = _place()
        me = 4 * x + 2 * y + c
        buf[me] = v_ref[...]
        copies = []
        for k in range(1, N_DEV):
            peer = (x ^ (k >> 2), y ^ ((k >> 1) & 1), c ^ (k & 1))
            cp = _rcopy(v_ref, buf.at[me], send_sems.at[k - 1], recv_sems.at[k - 1], peer)
            cp.start()
            copies.append(cp)
        for cp in copies:
            cp.wait()
        acc = buf[0]
        for d in range(1, N_DEV):
            acc = acc + buf[d]
        o_ref[...] = acc

    return pl.pallas_call(
        body,
        name="allreduce_small",
        in_specs=[pl.BlockSpec(memory_space=pltpu.VMEM)],
        out_specs=pl.BlockSpec(memory_space=pltpu.VMEM),
        out_shape=jax.ShapeDtypeStruct(vec.shape, F32),
        scratch_shapes=[pltpu.VMEM((N_DEV, rows, LANES), F32), pltpu.SemaphoreType.DMA((N_DEV - 1,)), pltpu.SemaphoreType.DMA((N_DEV - 1,))],
        compiler_params=pltpu.CompilerParams(vmem_limit_bytes=VMEM_LIMIT_BYTES),
    )(vec)


class _Reduction:
    def __init__(self, tag, grads, place, core):
        self.tag, self.n, self.place, self.core = tag, len(grads), place, core
        lands = [lax.empty((N_CHIPS, g.shape[1] // 2, g.shape[2]), F32) for g in grads]
        self._start("swap", list(grads) + lands, self.n, _swap_copies(self.n))

    def _start(self, stage, bufs, n_sem, copies):
        *self.flight, self.token = _start_copies(f"rs_{stage}_start_{self.tag}", bufs, n_sem, copies)

    def _landed(self, stage, after, copies):
        send_sems, recv_sems, bufs = self.flight
        return _wait_copies(f"rs_{stage}_wait_{self.tag}", bufs, send_sems, recv_sems, after, copies)

    def to_chips(self, after):
        n = self.n
        bufs = self._landed("swap", after, _swap_copies(n))
        sums = [_add_core_halves(g, o, self.core, name=f"rs_add_core_{self.tag}_{i}") for i, (g, o) in enumerate(zip(bufs[:n], bufs[n:]))]
        self.mine = [f for f, _ in sums]
        parts = [b for _, b in sums]
        lands = [lax.empty((3,) + p.shape[1:], BF16) for p in parts]
        self._start("send", parts + lands, 3 * n, _partial_copies(n))

    def to_core(self, after):
        n = self.n
        bufs = self._landed("send", after, _partial_copies(n))
        halves = [_add_chip_partials(f, o, self.place, name=f"rs_add_chip_{self.tag}_{i}") for i, (f, o) in enumerate(zip(self.mine, bufs[n:]))]
        self._start("share", halves, n, _share_copies(n))

    def finish(self, after):
        return [b.reshape((-1,) + b.shape[2:]) for b in self._landed("share", after, _share_copies(self.n))]


ELEM_ROWS = (256, 176, 128, 64, 32, 16, 8)


def _add_core_halves(grad, got, c, name):
    s, r, cols = grad.shape
    h = r // 2
    tr = _pick(h, ELEM_ROWS)

    def body(c_ref, g_ref, o_ref, f_ref, b_ref):
        acc = g_ref[...] + o_ref[...]
        f_ref[...] = acc
        b_ref[...] = acc.astype(BF16)

    blk = pl.BlockSpec((None, tr, cols), lambda k, i, c_ref: (k, i, 0))
    return pl.pallas_call(
        body,
        name=name,
        grid_spec=pltpu.PrefetchScalarGridSpec(
            num_scalar_prefetch=1,
            grid=(s, h // tr),
            in_specs=[pl.BlockSpec((None, None, tr, cols), lambda k, i, c_ref: (k, c_ref[0], i, 0)), blk],
            out_specs=[blk, blk],
        ),
        out_shape=[jax.ShapeDtypeStruct((s, h, cols), F32), jax.ShapeDtypeStruct((s, h, cols), BF16)],
        compiler_params=_cparams(("parallel", "parallel")),
    )(c, grad.reshape(s, 2, h, cols), got)


def _add_chip_partials(mine, got, place, name):
    _, h, cols = mine.shape
    tr = _pick(h, ELEM_ROWS)

    def body(place_ref, m_ref, g_ref, o_ref):
        acc = m_ref[...]
        for j in range(3):
            acc = acc + g_ref[j].astype(F32)
        o_ref[...] = acc

    return pl.pallas_call(
        body,
        name=name,
        grid_spec=pltpu.PrefetchScalarGridSpec(
            num_scalar_prefetch=1,
            grid=(h // tr,),
            in_specs=[
                pl.BlockSpec((None, tr, cols), lambda i, place_ref: (place_ref[0], i, 0)),
                pl.BlockSpec((3, tr, cols), lambda i, place_ref: (0, i, 0)),
            ],
            out_specs=pl.BlockSpec((None, tr, cols), lambda i, place_ref: (place_ref[1], i, 0)),
        ),
        out_shape=jax.ShapeDtypeStruct((2, h, cols), F32),
        compiler_params=_cparams(("parallel",)),
    )(place, mine, got)


def _adamw_math(w, m, v, g):
    nm = ADAM_B1 * m + (1.0 - ADAM_B1) * g
    nv = ADAM_B2 * v + (1.0 - ADAM_B2) * (g * g)
    m_hat = nm * (1.0 / (1.0 - ADAM_B1 ** ADAM_STEP))
    v_hat = nv * (1.0 / (1.0 - ADAM_B2 ** ADAM_STEP))
    return -ADAM_LR * (m_hat / (jnp.sqrt(v_hat) + ADAM_EPS) + ADAM_WD * w), nm, nv


def _adamw_layer(w, m, v, g, layer, prev, name):
    nl, r, cols = w.shape
    tr = _pick(r, ELEM_ROWS)

    def body(w_ref, m_ref, v_ref, g_ref, *rest):
        go_ref, d_ref, nm_ref, nv_ref = rest[-4:]
        gv = g_ref[...]
        d_ref[...], nm_ref[...], nv_ref[...] = _adamw_math(w_ref[...], m_ref[...], v_ref[...], gv)
        go_ref[...] = gv

    lay = pl.BlockSpec((None, tr, cols), lambda i: (layer, i, 0))
    return pl.pallas_call(
        body,
        name=name,
        grid=(r // tr,),
        in_specs=[lay] * 3 + [pl.BlockSpec((tr, cols), lambda i: (i, 0))] + ([ANY] * 4 if prev else []),
        out_specs=[lay] * 4,
        out_shape=[jax.ShapeDtypeStruct((nl, r, cols), F32)] * 4,
        input_output_aliases={4 + k: k for k in range(4)} if prev else {},
        compiler_params=_cparams(("parallel",)),
    )(w, m, v, g, *(prev or ()))


def _adamw(w, m, v, g, name):
    r, cols = w.shape
    tr = _pick(r, ELEM_ROWS)

    def body(w_ref, m_ref, v_ref, g_ref, d_ref, nm_ref, nv_ref):
        d_ref[...], nm_ref[...], nv_ref[...] = _adamw_math(w_ref[...], m_ref[...], v_ref[...], g_ref[...])

    blk = pl.BlockSpec((tr, cols), lambda i: (i, 0))
    return pl.pallas_call(
        body,
        name=name,
        grid=(r // tr,),
        in_specs=[blk] * 4,
        out_specs=[blk] * 3,
        out_shape=[jax.ShapeDtypeStruct((r, cols), F32)] * 3,
        compiler_params=_cparams(("parallel",)),
    )(w, m, v, g)


SMALL_COLS = 384
SMALL_ROWS = 16


def _pad_rows(flat, rows, cols):
    return jnp.pad(flat, (0, rows * cols - flat.shape[0])).reshape(rows, cols)


def kernel(x, hg_norm, hg_w_in, hg_lb_logits, hg_out_norm, hg_w_out, kv_norm, w_kv, attn_norm, attn_w_q, attn_sinks, attn_w_o, ffn_norm, ffn_w_up, ffn_conv_w, ffn_conv_b, ffn_w_down, final_norm, loss_target, m_hg_norm, m_hg_w_in, m_hg_lb_logits, m_hg_out_norm, m_hg_w_out, m_kv_norm, m_w_kv, m_attn_norm, m_attn_w_q, m_attn_sinks, m_attn_w_o, m_ffn_norm, m_ffn_w_up, m_ffn_conv_w, m_ffn_conv_b, m_ffn_w_down, m_final_norm, v_hg_norm, v_hg_w_in, v_hg_lb_logits, v_hg_out_norm, v_hg_w_out, v_kv_norm, v_w_kv, v_attn_norm, v_attn_w_q, v_attn_sinks, v_attn_w_o, v_ffn_norm, v_ffn_w_up, v_ffn_conv_w, v_ffn_conv_b, v_ffn_w_down, v_final_norm):
    wts = dict(hg_norm=hg_norm, hg_w_in=hg_w_in, hg_lb_logits=hg_lb_logits, hg_out_norm=hg_out_norm, hg_w_out=hg_w_out, kv_norm=kv_norm, w_kv=w_kv, attn_norm=attn_norm, attn_w_q=attn_w_q, attn_sinks=attn_sinks, attn_w_o=attn_w_o, ffn_norm=ffn_norm, ffn_w_up=ffn_w_up, ffn_conv_w=ffn_conv_w, ffn_conv_b=ffn_conv_b, ffn_w_down=ffn_w_down, final_norm=final_norm)
    mom1 = dict(hg_norm=m_hg_norm, hg_w_in=m_hg_w_in, hg_lb_logits=m_hg_lb_logits, hg_out_norm=m_hg_out_norm, hg_w_out=m_hg_w_out, kv_norm=m_kv_norm, w_kv=m_w_kv, attn_norm=m_attn_norm, attn_w_q=m_attn_w_q, attn_sinks=m_attn_sinks, attn_w_o=m_attn_w_o, ffn_norm=m_ffn_norm, ffn_w_up=m_ffn_w_up, ffn_conv_w=m_ffn_conv_w, ffn_conv_b=m_ffn_conv_b, ffn_w_down=m_ffn_w_down, final_norm=m_final_norm)
    mom2 = dict(hg_norm=v_hg_norm, hg_w_in=v_hg_w_in, hg_lb_logits=v_hg_lb_logits, hg_out_norm=v_hg_out_norm, hg_w_out=v_hg_w_out, kv_norm=v_kv_norm, w_kv=v_w_kv, attn_norm=v_attn_norm, attn_w_q=v_attn_w_q, attn_sinks=v_attn_sinks, attn_w_o=v_attn_w_o, ffn_norm=v_ffn_norm, ffn_w_up=v_ffn_w_up, ffn_conv_w=v_ffn_conv_w, ffn_conv_b=v_ffn_conv_b, ffn_w_down=v_ffn_w_down, final_norm=v_final_norm)
    names = list(wts)
    chip = 2 * lax.axis_index("x") + lax.axis_index("y")
    core = lax.axis_index("c")
    core_arr = jnp.reshape(core, (1,)).astype(jnp.int32)
    fs = D_FF // N_CHIPS
    ds = D_MODEL // N_CHIPS

    place_arr = jnp.stack([chip, core]).astype(jnp.int32)
    small = jnp.concatenate([hg_norm.reshape(-1), hg_lb_logits.reshape(-1), ffn_conv_w.reshape(-1)])
    n_small = small.shape[0]
    shards = [
        ("small", _pad_rows(small, SMALL_ROWS, SMALL_COLS), F32), ("hg_w_in", hg_w_in[0], BF16),
        ("hg_w_out", hg_w_out[0], BF16), ("ffn_w_up0", ffn_w_up[0], BF16), ("ffn_w_down0", ffn_w_down[0], BF16),
        ("w_kv", w_kv, BF16), ("attn_w_q", attn_w_q[0], BF16), ("attn_w_o", attn_w_o[0], BF16),
        ("ffn_w_up1", ffn_w_up[1], BF16), ("ffn_w_down1", ffn_w_down[1], BF16),
    ]
    n_first = 3
    spans = dict(layer0=(0, 2), layer1=(2, 7))
    placed = [_place_shard(s, place_arr, dt, name=f"place_{nm}") for nm, s, dt in shards[:n_first]]
    first = _start_copies("gather_start_first", placed, 3 * n_first, _gather_copies(0, n_first))
    placed = [_place_shard(s, place_arr, dt, name=f"place_{nm}", deps=(first[3],)) for nm, s, dt in shards[n_first:]]
    rest = _start_copies("gather_start_rest", placed, 3 * len(placed), _gather_half_copies(0, len(placed), True))
    relayed = {}

    def fetch(w, stage, after):
        if stage == "first":
            got = _wait_copies("gather_wait_first", first[2][:2], first[0], first[1], after, _gather_copies(0, 2))
        elif stage == "mixer_out":
            got = _wait_copies("gather_wait_mixer_out", first[2][2:], first[0], first[1], after, _gather_copies(2, 1))
        elif stage.endswith("_relay"):
            lo, hi = spans[stage[:-6]]
            relayed[stage[:-6]] = _relay_copies(
                f"gather_{stage}", rest[2][lo:hi], rest[0], rest[1], after,
                _gather_half_copies(lo, hi - lo, True), 3 * (hi - lo), _gather_half_copies(0, hi - lo, False))
            return w
        else:
            lo, hi = spans[stage]
            send_sems, recv_sems, bufs, _ = relayed[stage]
            got = _wait_copies(f"gather_wait_{stage}", bufs, send_sems, recv_sems, after, _gather_half_copies(0, hi - lo, False))
        w = dict(w)
        if stage == "first":
            g_small = got[0].reshape(N_CHIPS, -1)[:, :n_small]
            conv_w = g_small[:, 3 * ds:].reshape(N_CHIPS, 2, 3, fs).transpose(1, 2, 0, 3).reshape(2, 3, D_FF)
            w.update(
                hg_norm=g_small[:, :ds].reshape(1, D_MODEL),
                hg_lb=g_small[:, ds:3 * ds].reshape(N_CHIPS, 2, ds).transpose(1, 0, 2).reshape(2, D_MODEL),
                ffn_conv_w=[conv_w[0], conv_w[1]], hg_w_in=got[1],
            )
        elif stage == "mixer_out":
            w.update(hg_w_out=got[0].reshape(1, D_MODEL, D_MODEL))
        elif stage == "layer0":
            w.update(ffn_w_up=[got[0], None], ffn_w_down=[got[1].reshape(1, D_FF, D_MODEL), None])
        else:
            w.update(
                w_kv=got[0].reshape(1, D_MODEL, 2 * LANES), attn_w_q=got[1].reshape(1, D_MODEL, D_MODEL),
                attn_w_o=got[2].reshape(1, D_MODEL, D_MODEL), ffn_w_up=[w["ffn_w_up"][0], got[3]],
                ffn_w_down=[w["ffn_w_down"][0], got[4].reshape(1, D_FF, D_MODEL)],
            )
        return w

    whole = dict(
        hg_out_norm=hg_out_norm, kv_norm=kv_norm.reshape(1, D_MODEL), attn_norm=attn_norm, attn_sinks=attn_sinks.reshape(ATT_QH),
        ffn_norm=[ffn_norm[0:1], ffn_norm[1:2]], ffn_conv_b=[ffn_conv_b[0:1], ffn_conv_b[1:2]], final_norm=final_norm.reshape(1, D_MODEL),
    )
    whole = fetch(whole, "first", rest[3])

    red, layer1 = {}, {}

    def by_rows(g, rows):
        return g.reshape(N_CHIPS, rows, g.shape[2])

    def hook(point, dh, grads):
        if point == "ffn1":
            red["ffn1"] = _Reduction("ffn1", [by_rows(grads["ffn_w_down"], fs), grads["ffn_w_up"]], place_arr, core_arr)
            return (red["ffn1"].token,)
        if point == "attn":
            red["ffn1"].to_chips(dh)
            layer1.update(grads)
            return (red["ffn1"].token,)
        if point == "ffn0":
            group = [by_rows(layer1["attn_w_o"], ds), by_rows(layer1["attn_w_q"], ds), by_rows(layer1["w_kv"], ds),
                     by_rows(grads["ffn_w_down"], fs), grads["ffn_w_up"]]
            red["mid"] = _Reduction("mid", group, place_arr, core_arr)
            return (red["mid"].token,)
        if point == "hgrn":
            red["ffn1"].to_core(dh)
            red["mid"].to_chips(dh)
            return (red["ffn1"].token, red["mid"].token)
        red["hg"] = _Reduction("hg", [by_rows(grads["hg_w_out"], ds), grads["hg_w_in"]], place_arr, core_arr)
        return (red["hg"].token,)

    loss, dx, grads = _local_step(x[0], loss_target[0], whole, fetch, hook)

    small_parts = [
        loss.reshape(-1), grads["hg_out_norm"].reshape(-1), grads["attn_sinks"].reshape(-1), grads["kv_norm"].reshape(-1),
        grads["attn_norm"].reshape(-1), grads["ffn_norm"][0].reshape(-1), grads["ffn_norm"][1].reshape(-1),
        grads["ffn_conv_b"][0].reshape(-1), grads["ffn_conv_b"][1].reshape(-1), grads["final_norm"].reshape(-1),
        grads["hg_norm"].reshape(-1), grads["hg_lb"].reshape(-1), grads["ffn_conv_w"][0].reshape(-1), grads["ffn_conv_w"][1].reshape(-1),
    ]
    sizes = [p.shape[0] for p in small_parts]
    flat = jnp.concatenate(small_parts)
    rows = -(-flat.shape[0] // (SUBLANES * LANES)) * SUBLANES
    summed = _allreduce_small(_pad_rows(flat, rows, LANES)).reshape(-1)
    red["hg"].to_chips(summed)
    offs = [0]
    for sz in sizes:
        offs.append(offs[-1] + sz)
    sm = [summed[offs[i]:offs[i + 1]] for i in range(len(sizes))]
    loss_out = sm[0][0]
    conv_w_full = jnp.stack([sm[12].reshape(3, D_FF), sm[13].reshape(3, D_FF)])
    small_grads = dict(
        hg_out_norm=sm[1].reshape(1, HG_DK), attn_sinks=sm[2][:ATT_QH].reshape(1, ATT_QH), kv_norm=sm[3], attn_norm=sm[4].reshape(1, D_MODEL),
        ffn_norm=jnp.stack([sm[5], sm[6]]), ffn_conv_b=jnp.stack([sm[7], sm[8]]), final_norm=sm[9],
        hg_norm=lax.dynamic_slice(sm[10].reshape(1, D_MODEL), (0, chip * ds), (1, ds)),
        hg_lb_logits=lax.dynamic_slice(sm[11].reshape(2, D_MODEL), (0, chip * ds), (2, ds)),
        ffn_conv_w=lax.dynamic_slice(conv_w_full, (0, 0, chip * fs), (2, 3, fs)),
    )

    out_g, out_d, out_m, out_v = {}, {}, {}, {}

    def update(name, g2):
        shape = wts[name].shape
        d2, m2, v2 = _adamw(wts[name].reshape(g2.shape), mom1[name].reshape(g2.shape), mom2[name].reshape(g2.shape), g2, name=f"adamw_{name}")
        out_g[name], out_d[name], out_m[name], out_v[name] = g2.reshape(shape), d2.reshape(shape), m2.reshape(shape), v2.reshape(shape)
        return d2

    def update_layer(name, g2, layer, prev):
        res = _adamw_layer(wts[name], mom1[name], mom2[name], g2, layer, prev, name=f"adamw_{name}{layer}")
        out_g[name], out_d[name], out_m[name], out_v[name] = res
        return res

    g_down1, g_up1 = red["ffn1"].finish(red["hg"].token)
    down1 = update_layer("ffn_w_down", g_down1, 1, None)
    up1 = update_layer("ffn_w_up", g_up1, 1, None)
    red["mid"].to_core(up1[1])
    g_o, g_q, g_kv, g_down0, g_up0 = red["mid"].finish(up1[2])
    update("attn_w_o", g_o)
    update("attn_w_q", g_q)
    update("w_kv", g_kv)
    update_layer("ffn_w_down", g_down0, 0, down1)
    last = update_layer("ffn_w_up", g_up0, 0, up1)
    red["hg"].to_core(last[1])
    g_out, g_in = red["hg"].finish(last[2])
    update("hg_w_out", g_out)
    update("hg_w_in", g_in)

    small_names = [n for n in names if n not in out_g]
    cat = lambda d: jnp.concatenate([d[n].reshape(-1) for n in small_names])
    n_flat = sum(wts[n].size for n in small_names)
    srows = -(-n_flat // (SUBLANES * LANES)) * SUBLANES
    packed = [_pad_rows(cat(d), srows, LANES) for d in (wts, mom1, mom2, small_grads)]
    d_s, m_s, v_s = _adamw(*packed, name="adamw_small")
    off = 0
    for n in small_names:
        sz, shape = wts[n].size, wts[n].shape
        out_g[n] = small_grads[n].reshape(shape)
        out_d[n] = d_s.reshape(-1)[off:off + sz].reshape(shape)
        out_m[n] = m_s.reshape(-1)[off:off + sz].reshape(shape)
        out_v[n] = v_s.reshape(-1)[off:off + sz].reshape(shape)
        off += sz

    grad_x = dx.reshape(x.shape)
    return (loss_out, grad_x, *[out_g[n] for n in names], *[out_d[n] for n in names], *[out_m[n] for n in names], *[out_v[n] for n in names])
```

```python
import functools

import jax
import jax.numpy as jnp
from jax import lax
from jax.experimental import pallas as pl
from jax.experimental.pallas import tpu as pltpu

F32 = jnp.float32
BF16 = jnp.bfloat16
MESH = pl.DeviceIdType.MESH

EPS = 1e-6
D_MODEL = 1024
HG_HEADS = 8
HG_DK = 128
HG_CHUNK = 64
ATT_HD = 64
ATT_QH = 16
ATT_KVH = 2
ATT_GROUP = ATT_QH // ATT_KVH
WINDOW = 128
D_FF = 2816
N_CHIPS = 4
N_DEV = 8
LANES = 128
SUBLANES = 8
VMEM_LIMIT_BYTES = 56 * 1024 * 1024
NEG = -1e30
ALIBI_SLOPES = tuple(2.0 ** (-8.0 * h / ATT_QH) for h in range(1, ATT_QH + 1))

ADAM_LR = 0.001
ADAM_B1 = 0.9
ADAM_B2 = 0.999
ADAM_EPS = 1e-08
ADAM_WD = 0.01
ADAM_STEP = 10


def _cparams(sem=None):
    return pltpu.CompilerParams(dimension_semantics=sem, vmem_limit_bytes=VMEM_LIMIT_BYTES)


def _pick(n, cands):
    for c in cands:
        if n % c == 0:
            return c
    return n


def _sigmoid(x):
    return 0.5 * jnp.tanh(0.5 * x) + 0.5


def _dot(a, b, dims):
    return lax.dot_general(a, b, (dims, ((), ())), preferred_element_type=F32)


NN = ((1,), (0,))
NT = ((1,), (1,))
TN = ((0,), (0,))


MM_ROWS = 1024


def _rms_stats(xv):
    rstd = lax.rsqrt(jnp.mean(xv * xv, axis=-1, keepdims=True) + EPS)
    return xv * rstd, rstd


def _mm_operand(a_ref, gain_ref):
    if gain_ref is None:
        return a_ref[...].astype(BF16)
    return (_rms_stats(a_ref[...])[0] * gain_ref[...]).astype(BF16)


def _mm_nn(a, w, res=None, out_dtype=F32, name="mm_nn", gain=None):
    m, k = a.shape
    s, _, ns = w.shape
    tm = min(m, MM_ROWS)
    tn = _pick(ns, (1024, 1408, 512, 256, 128))
    npb = ns // tn

    def body(a_ref, w_ref, *rest):
        o_ref = rest[-1]
        acc = _dot(_mm_operand(a_ref, rest[0] if gain is not None else None), w_ref[...], NN)
        if res is not None:
            acc = acc + rest[-2][...]
        o_ref[...] = acc.astype(o_ref.dtype)

    in_specs = [
        pl.BlockSpec((tm, k), lambda i, j: (i, 0)),
        pl.BlockSpec((None, k, tn), lambda i, j: (j // npb, 0, j % npb)),
    ]
    args = [a, w]
    if gain is not None:
        in_specs.append(pl.BlockSpec((1, k), lambda i, j: (0, 0)))
        args.append(gain)
    if res is not None:
        in_specs.append(pl.BlockSpec((tm, tn), lambda i, j: (i, j)))
        args.append(res)
    return pl.pallas_call(
        body,
        name=name,
        grid=(m // tm, s * npb),
        in_specs=in_specs,
        out_specs=pl.BlockSpec((tm, tn), lambda i, j: (i, j)),
        out_shape=jax.ShapeDtypeStruct((m, s * ns), out_dtype),
        compiler_params=_cparams(("parallel", "parallel")),
    )(*args)


def _dy_spec(stacked, tm, tn, npb, row, kk):
    if stacked:
        return pl.BlockSpec((None, tm, tn), lambda *g: (kk(g) // npb, row(g), kk(g) % npb))
    return pl.BlockSpec((tm, tn), lambda *g: (row(g), kk(g)))


def _dep_specs(deps):
    return [pl.BlockSpec(d.shape, lambda *g: (0, 0)) for d in deps]


def _mm_nt(dy, w, stacked=False, out_dtype=F32, name="mm_nt", deps=(), norm_of=None):
    s, k, ns = w.shape
    m = dy.shape[1] if stacked else dy.shape[0]
    tm = min(m, MM_ROWS)
    tko = _pick(k, (1024, 1408, 512, 256))
    tn = _pick(ns, (1024, 1408, 512, 256))
    npb = ns // tn
    nk = s * npb
    fused = norm_of is not None
    assert not fused or tko == k

    def body(dy_ref, w_ref, *rest):
        acc_ref = rest[-1]
        i, kk = pl.program_id(0), pl.program_id(2)

        @pl.when(kk == 0)
        def _():
            acc_ref[...] = jnp.zeros_like(acc_ref)

        acc_ref[...] += _dot(dy_ref[...].astype(BF16), w_ref[...], NT)

        if not fused:
            @pl.when(kk == nk - 1)
            def _():
                rest[-2][...] = acc_ref[...].astype(rest[-2].dtype)
            return
        x_ref, g_ref, dres_ref = rest[:3]
        dx_ref, dg_ref = rest[-3], rest[-2]

        @pl.when(jnp.logical_and(i == 0, kk == 0))
        def _():
            dg_ref[...] = jnp.zeros_like(dg_ref)

        @pl.when(kk == nk - 1)
        def _():
            dxn = acc_ref[...]
            xhat, rstd = _rms_stats(x_ref[...])
            gd = dxn * g_ref[...]
            dx_ref[...] = dres_ref[...] + rstd * (gd - xhat * jnp.mean(gd * xhat, axis=-1, keepdims=True))
            dg_ref[...] += jnp.sum(dxn * xhat, axis=0, keepdims=True)

    row = pl.BlockSpec((tm, tko), lambda i, j, kk: (i, j))
    vec = pl.BlockSpec((1, k), lambda i, j, kk: (0, 0))
    return pl.pallas_call(
        body,
        name=name,
        grid=(m // tm, k // tko, nk),
        in_specs=[
            _dy_spec(stacked, tm, tn, npb, lambda g: g[0], lambda g: g[2]),
            pl.BlockSpec((None, tko, tn), lambda i, j, kk: (kk // npb, j, kk % npb)),
        ] + ([row, vec, row] if fused else []) + _dep_specs(deps),
        out_specs=[row, vec] if fused else row,
        out_shape=[jax.ShapeDtypeStruct((m, k), F32), jax.ShapeDtypeStruct((1, k), F32)] if fused else jax.ShapeDtypeStruct((m, k), out_dtype),
        scratch_shapes=[pltpu.VMEM((tm, tko), F32)],
        compiler_params=_cparams(("arbitrary",) * 3 if fused else ("parallel", "parallel", "arbitrary")),
    )(dy, w, *(norm_of or ()), *deps)


def _mm_tn(a, dy, s, ns, stacked=False, name="mm_tn", deps=(), gain=None):
    m, k = a.shape
    tm = min(m, MM_ROWS)
    tk = _pick(k, (1024, 1408, 512, 256))
    tn = _pick(ns, (1024, 1408, 512, 256, 128))
    npb = ns // tn
    nm = m // tm
    assert gain is None or tk == k

    def body(a_ref, dy_ref, *rest):
        o_ref, acc_ref = rest[-2:]
        mm = pl.program_id(2)

        @pl.when(mm == 0)
        def _():
            acc_ref[...] = jnp.zeros_like(acc_ref)

        acc_ref[...] += _dot(_mm_operand(a_ref, rest[0] if gain is not None else None), dy_ref[...].astype(BF16), TN)

        @pl.when(mm == nm - 1)
        def _():
            o_ref[...] = acc_ref[...]

    return pl.pallas_call(
        body,
        name=name,
        grid=(k // tk, s * npb, nm),
        in_specs=[
            pl.BlockSpec((tm, tk), lambda i, j, mm: (mm, i)),
            _dy_spec(stacked, tm, tn, npb, lambda g: g[2], lambda g: g[1]),
        ] + ([pl.BlockSpec((1, k), lambda i, j, mm: (0, 0))] if gain is not None else []) + _dep_specs(deps),
        out_specs=pl.BlockSpec((None, tk, tn), lambda i, j, mm: (j // npb, i, j % npb)),
        out_shape=jax.ShapeDtypeStruct((s, k, ns), F32),
        scratch_shapes=[pltpu.VMEM((tk, tn), F32)],
        compiler_params=_cparams(("parallel", "parallel", "arbitrary")),
    )(a, dy, *(() if gain is None else (gain,)), *deps)


ROW_TILE = 512


def _loss_head(h, g, target):
    t, d = h.shape
    r = min(t, ROW_TILE)

    def body(h_ref, g_ref, t_ref, dh_ref, dg_ref, loss_ref):
        @pl.when(pl.program_id(0) == 0)
        def _():
            dg_ref[...] = jnp.zeros_like(dg_ref)
            loss_ref[...] = jnp.zeros_like(loss_ref)

        xv = h_ref[...]
        rstd = lax.rsqrt(jnp.mean(xv * xv, axis=-1, keepdims=True) + EPS)
        xhat = xv * rstd
        gv = g_ref[...]
        err = xhat * gv - t_ref[...]
        loss_ref[...] += 0.5 * jnp.sum(jnp.mean(err * err, axis=-1, keepdims=True), axis=0, keepdims=True)
        dy = err * (1.0 / d)
        gd = dy * gv
        dh_ref[...] = rstd * (gd - xhat * jnp.mean(gd * xhat, axis=-1, keepdims=True))
        dg_ref[...] += jnp.sum(dy * xhat, axis=0, keepdims=True)

    return pl.pallas_call(
        body,
        name="loss_head",
        grid=(t // r,),
        in_specs=[
            pl.BlockSpec((r, d), lambda i: (i, 0)),
            pl.BlockSpec((1, d), lambda i: (0, 0)),
            pl.BlockSpec((r, d), lambda i: (i, 0)),
        ],
        out_specs=[
            pl.BlockSpec((r, d), lambda i: (i, 0)),
            pl.BlockSpec((1, d), lambda i: (0, 0)),
            pl.BlockSpec((1, LANES), lambda i: (0, 0)),
        ],
        out_shape=[
            jax.ShapeDtypeStruct((t, d), F32),
            jax.ShapeDtypeStruct((1, d), F32),
            jax.ShapeDtypeStruct((1, LANES), F32),
        ],
        compiler_params=_cparams(("arbitrary",)),
    )(h, g, target)


CONV_ROWS = 256
CONV_COLS = 1408


def _conv_taps(x_ext, n):
    tot = x_ext.shape[0]
    g1 = pltpu.roll(x_ext, 1, 0)[tot - n:]
    g2 = pltpu.roll(x_ext, 2, 0)[tot - n:]
    return g2, g1


def _conv_fwd(up, conv_w, conv_b, name="conv_fwd"):
    t = up.shape[0]
    r = min(t, CONV_ROWS)
    tc = CONV_COLS
    ncb = D_FF // tc
    hb = r // SUBLANES

    def body(g_ref, halo_ref, v_ref, w_ref, b_ref, o_ref):
        i = pl.program_id(1)
        g0 = g_ref[...]
        halo = halo_ref[...] * jnp.where(i > 0, 1.0, 0.0)
        g2, g1 = _conv_taps(jnp.concatenate([halo, g0], axis=0), r)
        c = b_ref[...] + w_ref[0:1, :] * g2 + w_ref[1:2, :] * g1 + w_ref[2:3, :] * g0
        o_ref[...] = (c * _sigmoid(c) * v_ref[...]).astype(BF16)

    return pl.pallas_call(
        body,
        name=name,
        grid=(ncb, t // r),
        in_specs=[
            pl.BlockSpec((r, tc), lambda j, i: (i, j)),
            pl.BlockSpec((SUBLANES, tc), lambda j, i: (jnp.maximum(i * hb - 1, 0), j)),
            pl.BlockSpec((r, tc), lambda j, i: (i, ncb + j)),
            pl.BlockSpec((3, tc), lambda j, i: (0, j)),
            pl.BlockSpec((1, tc), lambda j, i: (0, j)),
        ],
        out_specs=pl.BlockSpec((r, tc), lambda j, i: (i, j)),
        out_shape=jax.ShapeDtypeStruct((t, D_FF), BF16),
        compiler_params=_cparams(("parallel", "parallel")),
    )(up, up, up, conv_w, conv_b)


def _conv_bwd(up, conv_w, conv_b, dact, name="conv_bwd"):
    t = up.shape[0]
    r = min(t, CONV_ROWS)
    tc = CONV_COLS
    ncb = D_FF // tc
    hb = r // SUBLANES
    nrt = t // r

    def body(g_ref, halo_ref, v_ref, w_ref, b_ref, da_ref, dup_ref, dw_ref, db_ref, nxt_ref):
        ii = pl.program_id(1)
        i = nrt - 1 - ii

        @pl.when(ii == 0)
        def _():
            nxt_ref[...] = jnp.zeros_like(nxt_ref)
            dw_ref[...] = jnp.zeros_like(dw_ref)
            db_ref[...] = jnp.zeros_like(db_ref)

        g0 = g_ref[...]
        halo = halo_ref[...] * jnp.where(i > 0, 1.0, 0.0)
        g2, g1 = _conv_taps(jnp.concatenate([halo, g0], axis=0), r)
        w0, w1, w2 = w_ref[0:1, :], w_ref[1:2, :], w_ref[2:3, :]
        c = b_ref[...] + w0 * g2 + w1 * g1 + w2 * g0
        sg = _sigmoid(c)
        da = da_ref[...]
        dval = da * (c * sg)
        dc = da * v_ref[...] * (sg * (1.0 + c * (1.0 - sg)))
        db_ref[...] += jnp.sum(dc, axis=0, keepdims=True)
        dw_ref[0:1, :] += jnp.sum(dc * g2, axis=0, keepdims=True)
        dw_ref[1:2, :] += jnp.sum(dc * g1, axis=0, keepdims=True)
        dw_ref[2:3, :] += jnp.sum(dc * g0, axis=0, keepdims=True)
        ext = jnp.concatenate([dc, nxt_ref[...]], axis=0)
        tot = r + SUBLANES
        d1 = pltpu.roll(ext, tot - 1, 0)[:r]
        d2 = pltpu.roll(ext, tot - 2, 0)[:r]
        dgate = w2 * dc + w1 * d1 + w0 * d2
        nxt_ref[...] = dc[:SUBLANES]
        dup_ref[0] = dgate.astype(BF16)
        dup_ref[1] = dval.astype(BF16)

    rev = lambda ii: nrt - 1 - ii
    dup, dw, db = pl.pallas_call(
        body,
        name=name,
        grid=(ncb, nrt),
        in_specs=[
            pl.BlockSpec((r, tc), lambda j, ii: (rev(ii), j)),
            pl.BlockSpec((SUBLANES, tc), lambda j, ii: (jnp.maximum(rev(ii) * hb - 1, 0), j)),
            pl.BlockSpec((r, tc), lambda j, ii: (rev(ii), ncb + j)),
            pl.BlockSpec((3, tc), lambda j, ii: (0, j)),
            pl.BlockSpec((1, tc), lambda j, ii: (0, j)),
            pl.BlockSpec((r, tc), lambda j, ii: (rev(ii), j)),
        ],
        out_specs=[
            pl.BlockSpec((2, None, r, tc), lambda j, ii: (0, j, rev(ii), 0)),
            pl.BlockSpec((3, tc), lambda j, ii: (0, j)),
            pl.BlockSpec((1, tc), lambda j, ii: (0, j)),
        ],
        out_shape=[
            jax.ShapeDtypeStruct((2, ncb, t, tc), BF16),
            jax.ShapeDtypeStruct((3, D_FF), F32),
            jax.ShapeDtypeStruct((1, D_FF), F32),
        ],
        scratch_shapes=[pltpu.VMEM((SUBLANES, tc), F32)],
        compiler_params=_cparams(("parallel", "arbitrary")),
    )(up, up, up, conv_w, conv_b, dact)
    return dup.reshape(2 * ncb, t, tc), dw, db


def _split3(x):
    x1 = x.astype(BF16)
    r1 = x - x1.astype(F32)
    x2 = r1.astype(BF16)
    x3 = (r1 - x2.astype(F32)).astype(BF16)
    return x1, x2, x3


def _tri_dot(tri, x, dims):
    x1, x2, x3 = _split3(x)
    return _dot(tri, x1, dims) + _dot(tri, x2, dims) + _dot(tri, x3, dims)


def _lower_bound(logits_ref):
    return _sigmoid(logits_ref[0:1, :] - logits_ref[1:2, :])


def _hg_gates(qr, fr, lb):
    q = qr * _sigmoid(qr) * (HG_DK ** -0.5)
    sf = _sigmoid(fr)
    fg = lb + (1.0 - lb) * sf
    return q, sf, fg


def _hg_chunk_terms(q, fg, tril_b, low_half):
    g = jnp.log(fg)
    k = 1.0 - fg
    cum = _tri_dot(tril_b, g, NN)
    c_last = jnp.sum(g, axis=0, keepdims=True)
    c_mid = jnp.sum(jnp.where(low_half, g, 0.0), axis=0, keepdims=True)
    e_q = jnp.exp(cum - c_mid)
    e_k = jnp.exp(c_mid - cum)
    e_0 = jnp.exp(cum)
    e_l = jnp.exp(c_last - cum)
    return k, e_q, e_k, e_0, e_l, jnp.exp(c_last)


HG_BLOCK = 256


def _hg_proj_specs(rb, row):
    return [pl.BlockSpec((rb, D_MODEL), functools.partial(lambda i, k: (row(i), k), k=k)) for k in range(4)]


def _hg_consts(c):
    tril = lax.broadcasted_iota(jnp.int32, (c, c), 0) >= lax.broadcasted_iota(jnp.int32, (c, c), 1)
    low_half = lax.broadcasted_iota(jnp.int32, (c, D_MODEL), 0) < c // 2
    return tril, tril.astype(BF16), low_half


def _hgrn_fwd(proj, lb, wn):
    t = proj.shape[0]
    c = HG_CHUNK
    rb = min(t, HG_BLOCK)
    cpb = rb // c

    def body(q_ref, f_ref, i_ref, g_ref, lb_ref, wn_ref, o_ref, y_ref, st_ref, s_scr):
        @pl.when(pl.program_id(0) == 0)
        def _():
            s_scr[...] = jnp.zeros_like(s_scr)

        lb_all = _lower_bound(lb_ref)
        wnv = wn_ref[...]
        tril, tril_b, low_half = _hg_consts(c)

        def chunk(n, carry):
            rows = pl.ds(pl.multiple_of(n * c, c), c)
            q, _, fg = _hg_gates(q_ref[rows, :], f_ref[rows, :], lb_all)
            k, e_q, e_k, e_0, e_l, e_last = _hg_chunk_terms(q, fg, tril_b, low_half)
            qi, ki, q0, kl = (q * e_q).astype(BF16), (k * e_k).astype(BF16), (q * e_0).astype(BF16), (k * e_l).astype(BF16)
            v = i_ref[rows, :].astype(BF16)
            gr = g_ref[rows, :]
            gate = gr * _sigmoid(gr)
            for h in range(HG_HEADS):
                cols = slice(h * HG_DK, (h + 1) * HG_DK)
                st = s_scr[h]
                st_ref[h, n] = st
                a = jnp.where(tril, _dot(qi[:, cols], ki[:, cols], NT), 0.0)
                o = _dot(q0[:, cols], st.astype(BF16), NT) + _dot(a.astype(BF16), v[:, cols], NN)
                s_scr[h] = st * e_last[:, cols] + _dot(v[:, cols], kl[:, cols], TN)
                o_ref[rows, cols] = o
                rstd = lax.rsqrt(jnp.mean(o * o, axis=-1, keepdims=True) + EPS)
                y_ref[rows, cols] = (o * rstd * wnv * gate[:, cols]).astype(BF16)
            return carry

        lax.fori_loop(0, cpb, chunk, 0)

    blk = pl.BlockSpec((rb, D_MODEL), lambda i: (i, 0))
    return pl.pallas_call(
        body,
        name="hgrn_fwd",
        grid=(t // rb,),
        in_specs=_hg_proj_specs(rb, lambda i: i) + [pl.BlockSpec((2, D_MODEL), lambda i: (0, 0)), pl.BlockSpec((1, HG_DK), lambda i: (0, 0))],
        out_specs=[blk, blk, pl.BlockSpec((HG_HEADS, cpb, HG_DK, HG_DK), lambda i: (0, i, 0, 0))],
        out_shape=[
            jax.ShapeDtypeStruct((t, D_MODEL), F32),
            jax.ShapeDtypeStruct((t, D_MODEL), BF16),
            jax.ShapeDtypeStruct((HG_HEADS, t // c, HG_DK, HG_DK), F32),
        ],
        scratch_shapes=[pltpu.VMEM((HG_HEADS, HG_DK, HG_DK), F32)],
        compiler_params=_cparams(("arbitrary",)),
    )(proj, proj, proj, proj, lb, wn)


def _hgrn_bwd(proj, lb, wn, o, states, dy):
    t = proj.shape[0]
    c = HG_CHUNK
    rb = min(t, HG_BLOCK)
    cpb = rb // c
    nb = t // rb

    def body(q_ref, f_ref, i_ref, g_ref, lb_ref, wn_ref, o_ref, st_ref, dy_ref, dp_ref, dl_ref, dwn_ref, ds_scr, dlb_scr):
        step = pl.program_id(0)

        @pl.when(step == 0)
        def _():
            dwn_ref[...] = jnp.zeros_like(dwn_ref)
            ds_scr[...] = jnp.zeros_like(ds_scr)
            dlb_scr[...] = jnp.zeros_like(dlb_scr)

        lb_all = _lower_bound(lb_ref)
        wnv = wn_ref[...]
        tril, tril_b, low_half = _hg_consts(c)

        def chunk(nn, carry):
            n = cpb - 1 - nn
            rows = pl.ds(pl.multiple_of(n * c, c), c)
            qr = q_ref[rows, :]
            gr = g_ref[rows, :]
            q, sf, fg = _hg_gates(qr, f_ref[rows, :], lb_all)
            k, e_q, e_k, e_0, e_l, e_last = _hg_chunk_terms(q, fg, tril_b, low_half)
            qi, qi_lo, _ = _split3(q * e_q)
            ki, ki_lo, _ = _split3(k * e_k)
            q0 = (q * e_0).astype(BF16)
            kl = (k * e_l).astype(BF16)
            v = i_ref[rows, :].astype(BF16)
            sg = _sigmoid(gr)
            silu_g = gr * sg
            dsilu_g = sg * (1.0 + gr * (1.0 - sg))
            dqs, dks, d_lasts = [], [], []
            for h in range(HG_HEADS):
                cols = slice(h * HG_DK, (h + 1) * HG_DK)
                ov = o_ref[rows, cols]
                dyv = dy_ref[rows, cols].astype(F32)
                rstd = lax.rsqrt(jnp.mean(ov * ov, axis=-1, keepdims=True) + EPS)
                ohat = ov * rstd
                dp_ref[3, rows, cols] = (dyv * (ohat * wnv) * dsilu_g[:, cols]).astype(BF16)
                don = dyv * silu_g[:, cols]
                dwn_ref[...] += jnp.sum(don * ohat, axis=0, keepdims=True)
                gd = don * wnv
                do_b = (rstd * (gd - ohat * jnp.mean(gd * ohat, axis=-1, keepdims=True))).astype(BF16)
                st = st_ref[h, n]
                ds = ds_scr[h]
                ds_b = ds.astype(BF16)
                vh, kh = v[:, cols], k[:, cols]
                a_b = jnp.where(tril, _dot(qi[:, cols], ki[:, cols], NT), 0.0).astype(BF16)
                da_b = jnp.where(tril, _dot(do_b, vh, NT), 0.0).astype(BF16)
                dqs.append(_dot(do_b, st.astype(BF16), NN) * e_0[:, cols]
                           + (_dot(da_b, ki[:, cols], NN) + _dot(da_b, ki_lo[:, cols], NN)) * e_q[:, cols])
                dk_state = _dot(vh, ds_b, NN) * e_l[:, cols]
                dks.append((_dot(da_b, qi[:, cols], TN) + _dot(da_b, qi_lo[:, cols], TN)) * e_k[:, cols] + dk_state)
                dp_ref[2, rows, cols] = (_dot(a_b, do_b, TN) + _dot(kl[:, cols], ds_b, NT)).astype(BF16)
                ds_scr[h] = ds * e_last[:, cols] + _dot(do_b, q0[:, cols], TN)
                d_lasts.append(jnp.sum(dk_state * kh, axis=0, keepdims=True) + jnp.sum(ds * st, axis=0, keepdims=True) * e_last[:, cols])
            dq = jnp.concatenate(dqs, axis=1)
            dk = jnp.concatenate(dks, axis=1)
            dlogf = _tri_dot(tril_b, q * dq - k * dk, TN) + jnp.concatenate(d_lasts, axis=1)
            dfg = dlogf / fg - dk
            dlb_scr[...] += jnp.sum(dfg * (1.0 - sf), axis=0, keepdims=True)
            sq = _sigmoid(qr)
            dp_ref[0, rows, :] = (dq * (HG_DK ** -0.5) * (sq * (1.0 + qr * (1.0 - sq)))).astype(BF16)
            dp_ref[1, rows, :] = (dfg * (1.0 - lb_all) * sf * (1.0 - sf)).astype(BF16)
            return carry

        lax.fori_loop(0, cpb, chunk, 0)

        @pl.when(step == nb - 1)
        def _():
            d0 = dlb_scr[...] * lb_all * (1.0 - lb_all)
            dl_ref[0:1, :] = d0
            dl_ref[1:2, :] = -d0

    rev = lambda i: nb - 1 - i
    blk = pl.BlockSpec((rb, D_MODEL), lambda i: (rev(i), 0))
    return pl.pallas_call(
        body,
        name="hgrn_bwd",
        grid=(nb,),
        in_specs=_hg_proj_specs(rb, rev)
        + [pl.BlockSpec((2, D_MODEL), lambda i: (0, 0)), pl.BlockSpec((1, HG_DK), lambda i: (0, 0)), blk,
           pl.BlockSpec((HG_HEADS, cpb, HG_DK, HG_DK), lambda i: (0, rev(i), 0, 0)), blk],
        out_specs=[
            pl.BlockSpec((4, rb, D_MODEL), lambda i: (0, rev(i), 0)),
            pl.BlockSpec((2, D_MODEL), lambda i: (0, 0)),
            pl.BlockSpec((1, HG_DK), lambda i: (0, 0)),
        ],
        out_shape=[
            jax.ShapeDtypeStruct((4, t, D_MODEL), BF16),
            jax.ShapeDtypeStruct((2, D_MODEL), F32),
            jax.ShapeDtypeStruct((1, HG_DK), F32),
        ],
        scratch_shapes=[pltpu.VMEM((HG_HEADS, HG_DK, HG_DK), F32), pltpu.VMEM((1, D_MODEL), F32)],
        compiler_params=_cparams(("arbitrary",)),
    )(proj, proj, proj, proj, lb, wn, o, states, dy)


ATT_STACK = 8


def _att_stack(q_ref, sink_ref, first, lo, bias_p, bias_c, extra_ref=None):
    qs, bps, bcs, sinks, extras = [], [], [], None, []
    rows = lax.broadcasted_iota(jnp.int32, (ATT_STACK * WINDOW, 1), 0)
    for i in range(ATT_STACK):
        hq = first + i
        cols = slice((hq // 2) * LANES, (hq // 2 + 1) * LANES)
        sel = lo if hq % 2 == 0 else jnp.logical_not(lo)
        qp = q_ref[:, cols] * (ATT_HD ** -0.5)
        qs.append(jnp.where(sel, qp, jnp.zeros_like(qp)))
        bps.append(ALIBI_SLOPES[hq] * bias_p)
        bcs.append(ALIBI_SLOPES[hq] * bias_c)
        sinks = sink_ref[hq] if sinks is None else jnp.where(rows < i * WINDOW, sinks, sink_ref[hq])
        if extra_ref is not None:
            ep = extra_ref[:, cols]
            extras.append(jnp.where(sel, ep, jnp.zeros_like(ep)))
    cat = lambda parts: jnp.concatenate(parts, axis=0)
    return cat(qs), cat(bps), cat(bcs), sinks, (cat(extras) if extras else None)


def _att_rows(i):
    return slice(i * WINDOW, (i + 1) * WINDOW)


def _att_bias(n):
    tq = lax.broadcasted_iota(jnp.int32, (WINDOW, WINDOW), 0)
    sk = lax.broadcasted_iota(jnp.int32, (WINDOW, WINDOW), 1)
    valid_c = sk <= tq
    valid_p = (sk - tq) > jnp.where(n > 0, 0, WINDOW)
    dist_c = (tq - sk).astype(F32)
    return jnp.where(valid_p, -dist_c - float(WINDOW), NEG), jnp.where(valid_c, -dist_c, NEG)


def _att_halves(x, lo, kh):
    r = pltpu.roll(x, ATT_HD, 1)
    zero = jnp.zeros_like(x)
    if kh == 0:
        return jnp.where(lo, x, r), jnp.where(lo, x, zero), jnp.where(lo, zero, r)
    return jnp.where(lo, r, x), jnp.where(lo, r, zero), jnp.where(lo, zero, x)


def _att_probs(qm, k2p, k2c, bias_p, bias_c, sink):
    sp = _dot(qm, k2p, NT) + bias_p
    sc = _dot(qm, k2c, NT) + bias_c
    m = jnp.maximum(jnp.maximum(jnp.max(sp, axis=-1, keepdims=True), jnp.max(sc, axis=-1, keepdims=True)), sink)
    ep = jnp.exp(sp - m)
    ec = jnp.exp(sc - m)
    es = jnp.exp(sink - m)
    inv = 1.0 / (jnp.sum(ep, axis=-1, keepdims=True) + jnp.sum(ec, axis=-1, keepdims=True) + es)
    return ep * inv, ec * inv, es * inv


def _attn_fwd(q, kv, sinks):
    t = q.shape[0]
    nb = t // WINDOW

    def body(sink_ref, q_ref, kvp_ref, kvc_ref, o_ref):
        n = pl.program_id(0)
        bias_p, bias_c = _att_bias(n)
        lo = lax.broadcasted_iota(jnp.int32, (WINDOW, LANES), 1) < ATT_HD
        for kh in range(ATT_KVH):
            k2p, _, _ = _att_halves(kvp_ref[:, 0:LANES], lo, kh)
            k2c, _, _ = _att_halves(kvc_ref[:, 0:LANES], lo, kh)
            _, vlo_p, vhi_p = _att_halves(kvp_ref[:, LANES:2 * LANES], lo, kh)
            _, vlo_c, vhi_c = _att_halves(kvc_ref[:, LANES:2 * LANES], lo, kh)
            for first in range(kh * ATT_GROUP, (kh + 1) * ATT_GROUP, ATT_STACK):
                qs, bp, bc, sinks, _ = _att_stack(q_ref, sink_ref, first, lo, bias_p, bias_c)
                pp, pc, _ = _att_probs(qs, k2p, k2c, bp, bc, sinks)
                pp, pc = pp.astype(BF16), pc.astype(BF16)
                for i in range(0, ATT_STACK, 2):
                    even, odd = _att_rows(i), _att_rows(i + 1)
                    out = (_dot(pp[even], vlo_p, NN) + _dot(pc[even], vlo_c, NN)
                           + _dot(pp[odd], vhi_p, NN) + _dot(pc[odd], vhi_c, NN))
                    j = (first + i) // 2
                    o_ref[:, j * LANES:(j + 1) * LANES] = out.astype(BF16)

    return pl.pallas_call(
        body,
        name="attn_fwd",
        grid=(nb,),
        in_specs=[
            pl.BlockSpec(memory_space=pltpu.SMEM),
            pl.BlockSpec((WINDOW, D_MODEL), lambda n: (n, 0)),
            pl.BlockSpec((WINDOW, 2 * LANES), lambda n: (jnp.maximum(n - 1, 0), 0)),
            pl.BlockSpec((WINDOW, 2 * LANES), lambda n: (n, 0)),
        ],
        out_specs=pl.BlockSpec((WINDOW, D_MODEL), lambda n: (n, 0)),
        out_shape=jax.ShapeDtypeStruct((t, D_MODEL), BF16),
        compiler_params=_cparams(("parallel",)),
    )(sinks, q, kv, kv)


def _attn_bwd(q, kv, sinks, dout):
    t = q.shape[0]
    nb = t // WINDOW

    def body(sink_ref, q_ref, kvp_ref, kvc_ref, do_ref, dq_ref, dkv_ref, dsink_ref, carry_ref):
        n = pl.program_id(0)

        @pl.when(n == 0)
        def _():
            carry_ref[...] = jnp.zeros_like(carry_ref)
            dsink_ref[...] = jnp.zeros_like(dsink_ref)

        @pl.when(n == nb)
        def _():
            dkv_ref[...] = carry_ref[...].astype(BF16)

        @pl.when(n < nb)
        def _():
            bias_p, bias_c = _att_bias(n)
            lo = lax.broadcasted_iota(jnp.int32, (WINDOW, LANES), 1) < ATT_HD
            lane1 = lax.broadcasted_iota(jnp.int32, (1, LANES), 1)
            dsink = jnp.zeros((1, LANES), F32)
            halves = []
            for kh in range(ATT_KVH):
                k2p, klo_p, khi_p = _att_halves(kvp_ref[:, 0:LANES], lo, kh)
                k2c, klo_c, khi_c = _att_halves(kvc_ref[:, 0:LANES], lo, kh)
                v2p, _, _ = _att_halves(kvp_ref[:, LANES:2 * LANES], lo, kh)
                v2c, _, _ = _att_halves(kvc_ref[:, LANES:2 * LANES], lo, kh)
                acc = [jnp.zeros((WINDOW, LANES), F32) for _ in range(4)]
                for first in range(kh * ATT_GROUP, (kh + 1) * ATT_GROUP, ATT_STACK):
                    qs, bp, bc, sinks, dos = _att_stack(q_ref, sink_ref, first, lo, bias_p, bias_c, do_ref)
                    pp, pc, ps = _att_probs(qs, k2p, k2c, bp, bc, sinks)
                    dpp = _dot(dos, v2p, NT)
                    dpc = _dot(dos, v2c, NT)
                    delta = jnp.sum(pp * dpp, axis=-1, keepdims=True) + jnp.sum(pc * dpc, axis=-1, keepdims=True)
                    dsp = (pp * (dpp - delta)).astype(BF16)
                    dsc = (pc * (dpc - delta)).astype(BF16)
                    sink_term = ps * delta
                    for i in range(ATT_STACK):
                        dsink = dsink + jnp.where(lane1 == first + i, -jnp.sum(sink_term[_att_rows(i)], axis=0, keepdims=True), 0.0)
                    for i in range(0, ATT_STACK, 2):
                        even, odd = _att_rows(i), _att_rows(i + 1)
                        dq_pair = (_dot(dsp[even], klo_p, NN) + _dot(dsc[even], klo_c, NN)
                                   + _dot(dsp[odd], khi_p, NN) + _dot(dsc[odd], khi_c, NN))
                        j = (first + i) // 2
                        dq_ref[:, j * LANES:(j + 1) * LANES] = (dq_pair * (ATT_HD ** -0.5)).astype(BF16)
                    acc[0] = acc[0] + _dot(dsp, qs, TN)
                    acc[1] = acc[1] + _dot(dsc, qs, TN)
                    acc[2] = acc[2] + _dot(pp.astype(BF16), dos, TN)
                    acc[3] = acc[3] + _dot(pc.astype(BF16), dos, TN)
                halves.append([a + pltpu.roll(a, ATT_HD, 1) for a in acc])
            prev = jnp.concatenate(
                [jnp.where(lo, halves[0][0], halves[1][0]), jnp.where(lo, halves[0][2], halves[1][2])], axis=1)
            cur = jnp.concatenate(
                [jnp.where(lo, halves[0][1], halves[1][1]), jnp.where(lo, halves[0][3], halves[1][3])], axis=1)
            dkv_ref[...] = (carry_ref[...] + prev).astype(BF16)
            carry_ref[...] = cur
            dsink_ref[...] += dsink

    blk = lambda n: jnp.minimum(n, nb - 1)
    return pl.pallas_call(
        body,
        name="attn_bwd",
        grid=(nb + 1,),
        in_specs=[
            pl.BlockSpec(memory_space=pltpu.SMEM),
            pl.BlockSpec((WINDOW, D_MODEL), lambda n: (blk(n), 0)),
            pl.BlockSpec((WINDOW, 2 * LANES), lambda n: (jnp.maximum(blk(n) - 1, 0), 0)),
            pl.BlockSpec((WINDOW, 2 * LANES), lambda n: (blk(n), 0)),
            pl.BlockSpec((WINDOW, D_MODEL), lambda n: (blk(n), 0)),
        ],
        out_specs=[
            pl.BlockSpec((WINDOW, D_MODEL), lambda n: (blk(n), 0)),
            pl.BlockSpec((WINDOW, 2 * LANES), lambda n: (jnp.maximum(n - 1, 0), 0)),
            pl.BlockSpec((1, LANES), lambda n: (0, 0)),
        ],
        out_shape=[
            jax.ShapeDtypeStruct((t, D_MODEL), BF16),
            jax.ShapeDtypeStruct((t, 2 * LANES), BF16),
            jax.ShapeDtypeStruct((1, LANES), F32),
        ],
        scratch_shapes=[pltpu.VMEM((WINDOW, 2 * LANES), F32)],
        compiler_params=_cparams(("arbitrary",)),
    )(sinks, q, kv, kv, dout)


def _ffn_fwd(h, norm_g, w_up, conv_w, conv_b, w_down, tag, after_up=lambda up: None):
    up = _mm_nn(h, w_up, gain=norm_g, name=f"ffn{tag}_up")
    after_up(up)
    act = _conv_fwd(up, conv_w, conv_b, name=f"ffn{tag}_conv")
    h_out = _mm_nn(act, w_down, res=h, name=f"ffn{tag}_down")
    return h_out, (up, act)


def _ffn_bwd(dh, h, norm_g, w_up, conv_w, conv_b, w_down, saved, tag, deps=()):
    up, act = saved
    dw_down = _mm_tn(act, dh, 1, D_MODEL, name=f"ffn{tag}_dwdown", deps=deps)
    dact = _mm_nt(dh, w_down, name=f"ffn{tag}_dact", deps=deps)
    dup, dconv_w, dconv_b = _conv_bwd(up, conv_w, conv_b, dact, name=f"ffn{tag}_dconv")
    dw_up = _mm_tn(h, dup, N_CHIPS, CONV_COLS, stacked=True, gain=norm_g, name=f"ffn{tag}_dwup")
    dh_in, dnorm = _mm_nt(dup, w_up, stacked=True, norm_of=(h, norm_g, dh), name=f"ffn{tag}_dxn")
    return dh_in, dict(ffn_w_down=dw_down, ffn_w_up=dw_up, ffn_conv_w=dconv_w, ffn_conv_b=dconv_b, ffn_norm=dnorm)


def _local_step(x, target, w, fetch=lambda w, stage, after: w, hook=lambda point, dh, grads: ()):
    proj = _mm_nn(x, w["hg_w_in"], gain=w["hg_norm"], name="hg_in")
    o, y, states = _hgrn_fwd(proj, w["hg_lb"], w["hg_out_norm"])
    w = fetch(w, "mixer_out", y)
    fetch(w, "layer0_relay", y)
    h_a = _mm_nn(y, w["hg_w_out"], res=x, name="hg_out")
    w = fetch(w, "layer0", h_a)
    h1, ffn0 = _ffn_fwd(h_a, w["ffn_norm"][0], w["ffn_w_up"][0], w["ffn_conv_w"][0], w["ffn_conv_b"][0], w["ffn_w_down"][0], 0,
                        lambda up: fetch(w, "layer1_relay", up))
    w = fetch(w, "layer1", h1)
    kv = _mm_nn(h1, w["w_kv"], gain=w["kv_norm"], out_dtype=BF16, name="kv_proj")
    qa = _mm_nn(h1, w["attn_w_q"], gain=w["attn_norm"], out_dtype=BF16, name="attn_q")
    ao = _attn_fwd(qa, kv, w["attn_sinks"])
    h_b = _mm_nn(ao, w["attn_w_o"], res=h1, name="attn_o")
    h2, ffn1 = _ffn_fwd(h_b, w["ffn_norm"][1], w["ffn_w_up"][1], w["ffn_conv_w"][1], w["ffn_conv_b"][1], w["ffn_w_down"][1], 1)
    dh2, d_final, loss = _loss_head(h2, w["final_norm"], target)

    dh_b, g1 = _ffn_bwd(dh2, h_b, w["ffn_norm"][1], w["ffn_w_up"][1], w["ffn_conv_w"][1], w["ffn_conv_b"][1], w["ffn_w_down"][1], ffn1, 1)
    deps = hook("ffn1", dh_b, g1)
    dw_o = _mm_tn(ao, dh_b, 1, D_MODEL, name="attn_dwo", deps=deps)
    dao = _mm_nt(dh_b, w["attn_w_o"], out_dtype=BF16, name="attn_dao", deps=deps)
    dqa, dkv, dsinks = _attn_bwd(qa, kv, w["attn_sinks"], dao)
    dw_q = _mm_tn(h1, dqa, 1, D_MODEL, gain=w["attn_norm"], name="attn_dwq")
    dh1, d_attn_norm = _mm_nt(dqa, w["attn_w_q"], norm_of=(h1, w["attn_norm"], dh_b), name="attn_dxa")
    dw_kv = _mm_tn(h1, dkv, 1, 2 * LANES, gain=w["kv_norm"], name="kv_dw")
    dh1, d_kv_norm = _mm_nt(dkv, w["w_kv"], norm_of=(h1, w["kv_norm"], dh1), name="kv_dx")
    deps = hook("attn", dh1, dict(attn_w_o=dw_o, attn_w_q=dw_q, w_kv=dw_kv))
    dh_a, g0 = _ffn_bwd(dh1, h_a, w["ffn_norm"][0], w["ffn_w_up"][0], w["ffn_conv_w"][0], w["ffn_conv_b"][0], w["ffn_w_down"][0], ffn0, 0, deps)
    deps = hook("ffn0", dh_a, g0)
    dw_out = _mm_tn(y, dh_a, 1, D_MODEL, name="hg_dwout", deps=deps)
    dy = _mm_nt(dh_a, w["hg_w_out"], out_dtype=BF16, name="hg_dy", deps=deps)
    dproj, dlb, d_out_norm = _hgrn_bwd(proj, w["hg_lb"], w["hg_out_norm"], o, states, dy)
    deps = hook("hgrn", dproj, None)
    dw_in = _mm_tn(x, dproj, N_CHIPS, D_MODEL, stacked=True, gain=w["hg_norm"], name="hg_dwin", deps=deps)
    deps = hook("hg_w", dproj, dict(hg_w_out=dw_out, hg_w_in=dw_in))
    dx, d_hg_norm = _mm_nt(dproj, w["hg_w_in"], stacked=True, norm_of=(x, w["hg_norm"], dh_a), name="hg_dxn", deps=deps)

    grads = dict(
        hg_norm=d_hg_norm, hg_w_in=dw_in, hg_lb=dlb, hg_out_norm=d_out_norm, hg_w_out=dw_out,
        kv_norm=d_kv_norm, w_kv=dw_kv, attn_norm=d_attn_norm, attn_w_q=dw_q, attn_sinks=dsinks, attn_w_o=dw_o,
        final_norm=d_final,
    )
    for name in g0:
        grads[name] = [g0[name], g1[name]]
    return loss, dx, grads


ANY = pl.BlockSpec(memory_space=pl.ANY)


def _place():
    x, y, c = lax.axis_index("x"), lax.axis_index("y"), lax.axis_index("c")
    chips = [(1 - x, y), (x, 1 - y), (1 - x, 1 - y)]
    return x, y, c, chips


def _rcopy(src, dst, send_sem, recv_sem, to):
    return pltpu.make_async_remote_copy(src_ref=src, dst_ref=dst, send_sem=send_sem, recv_sem=recv_sem, device_id=to, device_id_type=MESH)


HBM = pl.BlockSpec(memory_space=pltpu.HBM)
SEM = pl.BlockSpec(memory_space=pltpu.SEMAPHORE)
EFFECT = pltpu.SideEffectType.DATAFLOW_SIDE_EFFECTING


def _in_hbm(a):
    return pltpu.with_memory_space_constraint(a, pltpu.HBM)


def _place_shard(shard, place, dtype, name, deps=()):
    r, cols = shard.shape
    tr = _pick(r, ELEM_ROWS)

    def body(place_ref, s_ref, *rest):
        o_ref = rest[-1]
        o_ref[...] = s_ref[...].astype(o_ref.dtype)

    return pl.pallas_call(
        body,
        name=name,
        grid_spec=pltpu.PrefetchScalarGridSpec(
            num_scalar_prefetch=1,
            grid=(r // tr,),
            in_specs=[pl.BlockSpec((tr, cols), lambda i, place_ref: (i, 0))] + _dep_specs(deps),
            out_specs=pl.BlockSpec((None, tr, cols), lambda i, place_ref: (place_ref[0], i, 0)),
        ),
        out_shape=jax.ShapeDtypeStruct((N_CHIPS, r, cols), dtype),
        compiler_params=_cparams(("parallel",)),
    )(place, shard, *deps)


def _start_copies(name, bufs, n_sem, copies):
    n = len(bufs)

    def body(*refs):
        for cp in copies(refs[:n], refs[n], refs[n + 1]):
            cp.start()
        refs[-1][...] = jnp.zeros_like(refs[-1])

    outs = pl.pallas_call(
        body,
        name=name,
        in_specs=[HBM] * n,
        out_specs=[SEM, SEM] + [HBM] * n + [pl.BlockSpec(memory_space=pltpu.VMEM)],
        out_shape=[pltpu.SemaphoreType.DMA((n_sem,)), pltpu.SemaphoreType.DMA((n_sem,))] + [pltpu.HBM(b.shape, b.dtype) for b in bufs]
        + [jax.ShapeDtypeStruct((SUBLANES, LANES), F32)],
        input_output_aliases={i: 2 + i for i in range(n)},
        compiler_params=pltpu.CompilerParams(has_side_effects=EFFECT),
    )(*[_in_hbm(b) for b in bufs])
    return outs[0], outs[1], list(outs[2:-1]), outs[-1]


def _wait_copies(name, bufs, send_sems, recv_sems, after, copies):
    n = len(bufs)

    def body(*refs):
        for cp in copies(refs[:n], refs[n], refs[n + 1]):
            cp.wait_send()
            cp.wait_recv()

    return pl.pallas_call(
        body,
        name=name,
        in_specs=[HBM] * n + [SEM, SEM, ANY],
        out_specs=[HBM] * n,
        out_shape=[pltpu.HBM(b.shape, b.dtype) for b in bufs],
        input_output_aliases={i: i for i in range(n)},
        compiler_params=pltpu.CompilerParams(has_side_effects=EFFECT),
    )(*bufs, send_sems, recv_sems, after)


def _relay_copies(name, bufs, send_sems, recv_sems, after, landed, n_sem, onward):
    n = len(bufs)

    def body(*refs):
        for cp in landed(refs[:n], refs[n], refs[n + 1]):
            cp.wait_send()
            cp.wait_recv()
        for cp in onward(refs[:n], refs[n + 3], refs[n + 4]):
            cp.start()
        refs[-1][...] = jnp.zeros_like(refs[-1])

    outs = pl.pallas_call(
        body,
        name=name,
        in_specs=[HBM] * n + [SEM, SEM, ANY],
        out_specs=[SEM, SEM] + [HBM] * n + [pl.BlockSpec(memory_space=pltpu.VMEM)],
        out_shape=[pltpu.SemaphoreType.DMA((n_sem,)), pltpu.SemaphoreType.DMA((n_sem,))] + [pltpu.HBM(b.shape, b.dtype) for b in bufs]
        + [jax.ShapeDtypeStruct((SUBLANES, LANES), F32)],
        input_output_aliases={i: 2 + i for i in range(n)},
        compiler_params=pltpu.CompilerParams(has_side_effects=EFFECT),
    )(*bufs, send_sems, recv_sems, after)
    return outs[0], outs[1], list(outs[2:-1]), outs[-1]


def _gather_half_copies(first, count, over_ici):
    def copies(refs, send_sems, recv_sems):
        x, y, c, chips = _place()
        out = []
        for i in range(count):
            h = refs[i].shape[1] // 2
            mine = pl.ds(c * h, h)
            for j, (px, py) in enumerate(chips):
                k = 3 * (first + i) + j
                slot = 2 * x + y if over_ici else 2 * px + py
                to = (px, py, c) if over_ici else (x, y, 1 - c)
                out.append(_rcopy(refs[i].at[slot, mine], refs[i].at[slot, mine], send_sems.at[k], recv_sems.at[k], to))
        return out

    return copies


def _gather_copies(first, count):
    def copies(refs, send_sems, recv_sems):
        x, y, c, chips = _place()
        me = 2 * x + y
        out = []
        for i in range(count):
            for j, (px, py) in enumerate(chips):
                k = 3 * (first + i) + j
                out.append(_rcopy(refs[i].at[me], refs[i].at[me], send_sems.at[k], recv_sems.at[k], (px, py, c)))
        return out

    return copies


def _swap_copies(n):
    def copies(refs, send_sems, recv_sems):
        x, y, c, _ = _place()
        out = []
        for i in range(n):
            h = refs[i].shape[1] // 2
            out.append(_rcopy(refs[i].at[:, pl.ds((1 - c) * h, h)], refs[n + i], send_sems.at[i], recv_sems.at[i], (x, y, 1 - c)))
        return out

    return copies


def _partial_copies(n):
    def copies(refs, send_sems, recv_sems):
        x, y, c, chips = _place()
        out = []
        for i in range(n):
            for j, (px, py) in enumerate(chips):
                out.append(_rcopy(refs[i].at[2 * px + py], refs[n + i].at[j], send_sems.at[3 * i + j], recv_sems.at[3 * i + j], (px, py, c)))
        return out

    return copies


def _share_copies(n):
    def copies(refs, send_sems, recv_sems):
        x, y, c, _ = _place()
        return [_rcopy(refs[i].at[c], refs[i].at[c], send_sems.at[i], recv_sems.at[i], (x, y, 1 - c)) for i in range(n)]

    return copies


def _allreduce_small(vec):
    rows = vec.shape[0]

    def body(v_ref, o_ref, buf, send_sems, recv_sems):
        x, y, c, _ = _place()
        me = 4 * x + 2 * y + c
        buf[me] = v_ref[...]
        copies = []
        for k in range(1, N_DEV):
            peer = (x ^ (k >> 2), y ^ ((k >> 1) & 1), c ^ (k & 1))
            cp = _rcopy(v_ref, buf.at[me], send_sems.at[k - 1], recv_sems.at[k - 1], peer)
            cp.start()
            copies.append(cp)
        for cp in copies:
            cp.wait()
        acc = buf[0]
        for d in range(1, N_DEV):
            acc = acc + buf[d]
        o_ref[...] = acc

    return pl.pallas_call(
        body,
        name="allreduce_small",
        in_specs=[pl.BlockSpec(memory_space=pltpu.VMEM)],
        out_specs=pl.BlockSpec(memory_space=pltpu.VMEM),
        out_shape=jax.ShapeDtypeStruct(vec.shape, F32),
        scratch_shapes=[pltpu.VMEM((N_DEV, rows, LANES), F32), pltpu.SemaphoreType.DMA((N_DEV - 1,)), pltpu.SemaphoreType.DMA((N_DEV - 1,))],
        compiler_params=pltpu.CompilerParams(vmem_limit_bytes=VMEM_LIMIT_BYTES),
    )(vec)


class _Reduction:
    def __init__(self, tag, grads, place):
        self.tag, self.n, self.place = tag, len(grads), place
        lands = [lax.empty((N_CHIPS, g.shape[1] // 2, g.shape[2]), F32) for g in grads]
        self._start("swap", list(grads) + lands, self.n, _swap_copies(self.n))

    def _start(self, stage, bufs, n_sem, copies):
        *self.flight, self.token = _start_copies(f"rs_{stage}_start_{self.tag}", bufs, n_sem, copies)

    def _landed(self, stage, after, copies):
        send_sems, recv_sems, bufs = self.flight
        return _wait_copies(f"rs_{stage}_wait_{self.tag}", bufs, send_sems, recv_sems, after, copies)

    def to_chips(self, after):
        n = self.n
        bufs = self._landed("swap", after, _swap_copies(n))
        sums = [_add_core_halves(g, o, self.place, name=f"rs_add_core_{self.tag}_{i}") for i, (g, o) in enumerate(zip(bufs[:n], bufs[n:]))]
        self.mine = [f for f, _ in sums]
        parts = [b for _, b in sums]
        lands = [lax.empty((3,) + p.shape[1:], BF16) for p in parts]
        self._start("send", parts + lands, 3 * n, _partial_copies(n))

    def to_core(self, after):
        n = self.n
        bufs = self._landed("send", after, _partial_copies(n))
        halves = [_add_chip_partials(f, o, self.place, name=f"rs_add_chip_{self.tag}_{i}") for i, (f, o) in enumerate(zip(self.mine, bufs[n:]))]
        self._start("share", halves, n, _share_copies(n))

    def finish(self, after):
        return [b.reshape((-1,) + b.shape[2:]) for b in self._landed("share", after, _share_copies(self.n))]


ELEM_ROWS = (256, 176, 128, 64, 32, 16, 8)


def _add_core_halves(grad, got, place, name):
    s, r, cols = grad.shape
    h = r // 2
    tr = _pick(h, ELEM_ROWS)

    def body(place_ref, g_ref, o_ref, f_ref, b_ref):
        acc = g_ref[...] + o_ref[...]
        b_ref[...] = acc.astype(BF16)

        @pl.when(pl.program_id(1) == place_ref[0])
        def _():
            f_ref[...] = acc

    blk = pl.BlockSpec((None, tr, cols), lambda i, k, place_ref: (k, i, 0))
    return pl.pallas_call(
        body,
        name=name,
        grid_spec=pltpu.PrefetchScalarGridSpec(
            num_scalar_prefetch=1,
            grid=(h // tr, s),
            in_specs=[pl.BlockSpec((None, None, tr, cols), lambda i, k, place_ref: (k, place_ref[1], i, 0)), blk],
            out_specs=[pl.BlockSpec((tr, cols), lambda i, k, place_ref: (i, 0)), blk],
        ),
        out_shape=[jax.ShapeDtypeStruct((h, cols), F32), jax.ShapeDtypeStruct((s, h, cols), BF16)],
        compiler_params=_cparams(("parallel", "arbitrary")),
    )(place, grad.reshape(s, 2, h, cols), got)


def _add_chip_partials(mine, got, place, name):
    h, cols = mine.shape
    tr = _pick(h, ELEM_ROWS)

    def body(place_ref, m_ref, g_ref, o_ref):
        acc = m_ref[...]
        for j in range(3):
            acc = acc + g_ref[j].astype(F32)
        o_ref[...] = acc

    return pl.pallas_call(
        body,
        name=name,
        grid_spec=pltpu.PrefetchScalarGridSpec(
            num_scalar_prefetch=1,
            grid=(h // tr,),
            in_specs=[
                pl.BlockSpec((tr, cols), lambda i, place_ref: (i, 0)),
                pl.BlockSpec((3, tr, cols), lambda i, place_ref: (0, i, 0)),
            ],
            out_specs=pl.BlockSpec((None, tr, cols), lambda i, place_ref: (place_ref[1], i, 0)),
        ),
        out_shape=jax.ShapeDtypeStruct((2, h, cols), F32),
        compiler_params=_cparams(("parallel",)),
    )(place, mine, got)


def _adamw_math(w, m, v, g):
    nm = ADAM_B1 * m + (1.0 - ADAM_B1) * g
    nv = ADAM_B2 * v + (1.0 - ADAM_B2) * (g * g)
    m_hat = nm * (1.0 / (1.0 - ADAM_B1 ** ADAM_STEP))
    v_hat = nv * (1.0 / (1.0 - ADAM_B2 ** ADAM_STEP))
    return -ADAM_LR * (m_hat / (jnp.sqrt(v_hat) + ADAM_EPS) + ADAM_WD * w), nm, nv


def _adamw_layer(w, m, v, g, layer, prev, name):
    nl, r, cols = w.shape
    tr = _pick(r, ELEM_ROWS)

    def body(w_ref, m_ref, v_ref, g_ref, *rest):
        go_ref, d_ref, nm_ref, nv_ref = rest[-4:]
        gv = g_ref[...]
        d_ref[...], nm_ref[...], nv_ref[...] = _adamw_math(w_ref[...], m_ref[...], v_ref[...], gv)
        go_ref[...] = gv

    lay = pl.BlockSpec((None, tr, cols), lambda i: (layer, i, 0))
    return pl.pallas_call(
        body,
        name=name,
        grid=(r // tr,),
        in_specs=[lay] * 3 + [pl.BlockSpec((tr, cols), lambda i: (i, 0))] + ([ANY] * 4 if prev else []),
        out_specs=[lay] * 4,
        out_shape=[jax.ShapeDtypeStruct((nl, r, cols), F32)] * 4,
        input_output_aliases={4 + k: k for k in range(4)} if prev else {},
        compiler_params=_cparams(("parallel",)),
    )(w, m, v, g, *(prev or ()))


def _adamw(w, m, v, g, name):
    r, cols = w.shape
    tr = _pick(r, ELEM_ROWS)

    def body(w_ref, m_ref, v_ref, g_ref, d_ref, nm_ref, nv_ref):
        d_ref[...], nm_ref[...], nv_ref[...] = _adamw_math(w_ref[...], m_ref[...], v_ref[...], g_ref[...])

    blk = pl.BlockSpec((tr, cols), lambda i: (i, 0))
    return pl.pallas_call(
        body,
        name=name,
        grid=(r // tr,),
        in_specs=[blk] * 4,
        out_specs=[blk] * 3,
        out_shape=[jax.ShapeDtypeStruct((r, cols), F32)] * 3,
        compiler_params=_cparams(("parallel",)),
    )(w, m, v, g)


SMALL_COLS = 384
SMALL_ROWS = 16


def _pad_rows(flat, rows, cols):
    return jnp.pad(flat, (0, rows * cols - flat.shape[0])).reshape(rows, cols)


def kernel(x, hg_norm, hg_w_in, hg_lb_logits, hg_out_norm, hg_w_out, kv_norm, w_kv, attn_norm, attn_w_q, attn_sinks, attn_w_o, ffn_norm, ffn_w_up, ffn_conv_w, ffn_conv_b, ffn_w_down, final_norm, loss_target, m_hg_norm, m_hg_w_in, m_hg_lb_logits, m_hg_out_norm, m_hg_w_out, m_kv_norm, m_w_kv, m_attn_norm, m_attn_w_q, m_attn_sinks, m_attn_w_o, m_ffn_norm, m_ffn_w_up, m_ffn_conv_w, m_ffn_conv_b, m_ffn_w_down, m_final_norm, v_hg_norm, v_hg_w_in, v_hg_lb_logits, v_hg_out_norm, v_hg_w_out, v_kv_norm, v_w_kv, v_attn_norm, v_attn_w_q, v_attn_sinks, v_attn_w_o, v_ffn_norm, v_ffn_w_up, v_ffn_conv_w, v_ffn_conv_b, v_ffn_w_down, v_final_norm):
    wts = dict(hg_norm=hg_norm, hg_w_in=hg_w_in, hg_lb_logits=hg_lb_logits, hg_out_norm=hg_out_norm, hg_w_out=hg_w_out, kv_norm=kv_norm, w_kv=w_kv, attn_norm=attn_norm, attn_w_q=attn_w_q, attn_sinks=attn_sinks, attn_w_o=attn_w_o, ffn_norm=ffn_norm, ffn_w_up=ffn_w_up, ffn_conv_w=ffn_conv_w, ffn_conv_b=ffn_conv_b, ffn_w_down=ffn_w_down, final_norm=final_norm)
    mom1 = dict(hg_norm=m_hg_norm, hg_w_in=m_hg_w_in, hg_lb_logits=m_hg_lb_logits, hg_out_norm=m_hg_out_norm, hg_w_out=m_hg_w_out, kv_norm=m_kv_norm, w_kv=m_w_kv, attn_norm=m_attn_norm, attn_w_q=m_attn_w_q, attn_sinks=m_attn_sinks, attn_w_o=m_attn_w_o, ffn_norm=m_ffn_norm, ffn_w_up=m_ffn_w_up, ffn_conv_w=m_ffn_conv_w, ffn_conv_b=m_ffn_conv_b, ffn_w_down=m_ffn_w_down, final_norm=m_final_norm)
    mom2 = dict(hg_norm=v_hg_norm, hg_w_in=v_hg_w_in, hg_lb_logits=v_hg_lb_logits, hg_out_norm=v_hg_out_norm, hg_w_out=v_hg_w_out, kv_norm=v_kv_norm, w_kv=v_w_kv, attn_norm=v_attn_norm, attn_w_q=v_attn_w_q, attn_sinks=v_attn_sinks, attn_w_o=v_attn_w_o, ffn_norm=v_ffn_norm, ffn_w_up=v_ffn_w_up, ffn_conv_w=v_ffn_conv_w, ffn_conv_b=v_ffn_conv_b, ffn_w_down=v_ffn_w_down, final_norm=v_final_norm)
    names = list(wts)
    chip = 2 * lax.axis_index("x") + lax.axis_index("y")
    core = lax.axis_index("c")
    fs = D_FF // N_CHIPS
    ds = D_MODEL // N_CHIPS

    place_arr = jnp.stack([chip, core]).astype(jnp.int32)
    small = jnp.concatenate([hg_norm.reshape(-1), hg_lb_logits.reshape(-1), ffn_conv_w.reshape(-1)])
    n_small = small.shape[0]
    shards = [
        ("small", _pad_rows(small, SMALL_ROWS, SMALL_COLS), F32), ("hg_w_in", hg_w_in[0], BF16),
        ("hg_w_out", hg_w_out[0], BF16), ("ffn_w_up0", ffn_w_up[0], BF16), ("ffn_w_down0", ffn_w_down[0], BF16),
        ("w_kv", w_kv, BF16), ("attn_w_q", attn_w_q[0], BF16), ("attn_w_o", attn_w_o[0], BF16),
        ("ffn_w_up1", ffn_w_up[1], BF16), ("ffn_w_down1", ffn_w_down[1], BF16),
    ]
    n_first = 3
    spans = dict(layer0=(0, 2), layer1=(2, 7))

    def first_copies(refs, send_sems, recv_sems):
        return (_gather_copies(0, 1)(refs[:1], send_sems, recv_sems) + _gather_half_copies(1, 1, True)(refs[1:2], send_sems, recv_sems)
                + _gather_copies(2, 1)(refs[2:3], send_sems, recv_sems))

    placed = [_place_shard(s, place_arr, dt, name=f"place_{nm}") for nm, s, dt in shards[:n_first]]
    first = _start_copies("gather_start_first", placed, 3 * n_first, first_copies)
    placed = [_place_shard(s, place_arr, dt, name=f"place_{nm}", deps=(first[3],)) for nm, s, dt in shards[n_first:]]
    rest = _start_copies("gather_start_rest", placed, 3 * len(placed), _gather_half_copies(0, len(placed), True))
    relayed = {}

    def fetch(w, stage, after):
        if stage == "first":
            w_in = _relay_copies("gather_first_relay", first[2][1:2], first[0], first[1], after,
                                 _gather_half_copies(1, 1, True), 3, _gather_half_copies(0, 1, False))
            got = _wait_copies("gather_wait_small", first[2][:1], first[0], first[1], w_in[3], _gather_copies(0, 1))
            got += _wait_copies("gather_wait_first", w_in[2], w_in[0], w_in[1], got[0], _gather_half_copies(0, 1, False))
        elif stage == "mixer_out":
            got = _wait_copies("gather_wait_mixer_out", first[2][2:], first[0], first[1], after, _gather_copies(2, 1))
        elif stage.endswith("_relay"):
            lo, hi = spans[stage[:-6]]
            relayed[stage[:-6]] = _relay_copies(
                f"gather_{stage}", rest[2][lo:hi], rest[0], rest[1], after,
                _gather_half_copies(lo, hi - lo, True), 3 * (hi - lo), _gather_half_copies(0, hi - lo, False))
            return w
        else:
            lo, hi = spans[stage]
            send_sems, recv_sems, bufs, _ = relayed[stage]
            got = _wait_copies(f"gather_wait_{stage}", bufs, send_sems, recv_sems, after, _gather_half_copies(0, hi - lo, False))
        w = dict(w)
        if stage == "first":
            g_small = got[0].reshape(N_CHIPS, -1)[:, :n_small]
            conv_w = g_small[:, 3 * ds:].reshape(N_CHIPS, 2, 3, fs).transpose(1, 2, 0, 3).reshape(2, 3, D_FF)
            w.update(
                hg_norm=g_small[:, :ds].reshape(1, D_MODEL),
                hg_lb=g_small[:, ds:3 * ds].reshape(N_CHIPS, 2, ds).transpose(1, 0, 2).reshape(2, D_MODEL),
                ffn_conv_w=[conv_w[0], conv_w[1]], hg_w_in=got[1],
            )
        elif stage == "mixer_out":
            w.update(hg_w_out=got[0].reshape(1, D_MODEL, D_MODEL))
        elif stage == "layer0":
            w.update(ffn_w_up=[got[0], None], ffn_w_down=[got[1].reshape(1, D_FF, D_MODEL), None])
        else:
            w.update(
                w_kv=got[0].reshape(1, D_MODEL, 2 * LANES), attn_w_q=got[1].reshape(1, D_MODEL, D_MODEL),
                attn_w_o=got[2].reshape(1, D_MODEL, D_MODEL), ffn_w_up=[w["ffn_w_up"][0], got[3]],
                ffn_w_down=[w["ffn_w_down"][0], got[4].reshape(1, D_FF, D_MODEL)],
            )
        return w

    whole = dict(
        hg_out_norm=hg_out_norm, kv_norm=kv_norm.reshape(1, D_MODEL), attn_norm=attn_norm, attn_sinks=attn_sinks.reshape(ATT_QH),
        ffn_norm=[ffn_norm[0:1], ffn_norm[1:2]], ffn_conv_b=[ffn_conv_b[0:1], ffn_conv_b[1:2]], final_norm=final_norm.reshape(1, D_MODEL),
    )
    whole = fetch(whole, "first", rest[3])

    red, layer1 = {}, {}

    def by_rows(g, rows):
        return g.reshape(N_CHIPS, rows, g.shape[2])

    def hook(point, dh, grads):
        if point == "ffn1":
            red["ffn1"] = _Reduction("ffn1", [by_rows(grads["ffn_w_down"], fs), grads["ffn_w_up"]], place_arr)
            return (red["ffn1"].token,)
        if point == "attn":
            red["ffn1"].to_chips(dh)
            layer1.update(grads)
            return (red["ffn1"].token,)
        if point == "ffn0":
            group = [by_rows(layer1["attn_w_o"], ds), by_rows(layer1["attn_w_q"], ds), by_rows(layer1["w_kv"], ds),
                     by_rows(grads["ffn_w_down"], fs), grads["ffn_w_up"]]
            red["mid"] = _Reduction("mid", group, place_arr)
            return (red["mid"].token,)
        if point == "hgrn":
            red["ffn1"].to_core(dh)
            red["mid"].to_chips(dh)
            return (red["ffn1"].token, red["mid"].token)
        red["hg"] = _Reduction("hg", [by_rows(grads["hg_w_out"], ds), grads["hg_w_in"]], place_arr)
        return (red["hg"].token,)

    loss, dx, grads = _local_step(x[0], loss_target[0], whole, fetch, hook)

    small_parts = [
        loss.reshape(-1), grads["hg_out_norm"].reshape(-1), grads["attn_sinks"].reshape(-1), grads["kv_norm"].reshape(-1),
        grads["attn_norm"].reshape(-1), grads["ffn_norm"][0].reshape(-1), grads["ffn_norm"][1].reshape(-1),
        grads["ffn_conv_b"][0].reshape(-1), grads["ffn_conv_b"][1].reshape(-1), grads["final_norm"].reshape(-1),
        grads["hg_norm"].reshape(-1), grads["hg_lb"].reshape(-1), grads["ffn_conv_w"][0].reshape(-1), grads["ffn_conv_w"][1].reshape(-1),
    ]
    sizes = [p.shape[0] for p in small_parts]
    flat = jnp.concatenate(small_parts)
    rows = -(-flat.shape[0] // (SUBLANES * LANES)) * SUBLANES
    summed = _allreduce_small(_pad_rows(flat, rows, LANES)).reshape(-1)
    red["hg"].to_chips(summed)
    offs = [0]
    for sz in sizes:
        offs.append(offs[-1] + sz)
    sm = [summed[offs[i]:offs[i + 1]] for i in range(len(sizes))]
    loss_out = sm[0][0]
    conv_w_full = jnp.stack([sm[12].reshape(3, D_FF), sm[13].reshape(3, D_FF)])
    small_grads = dict(
        hg_out_norm=sm[1].reshape(1, HG_DK), attn_sinks=sm[2][:ATT_QH].reshape(1, ATT_QH), kv_norm=sm[3], attn_norm=sm[4].reshape(1, D_MODEL),
        ffn_norm=jnp.stack([sm[5], sm[6]]), ffn_conv_b=jnp.stack([sm[7], sm[8]]), final_norm=sm[9],
        hg_norm=lax.dynamic_slice(sm[10].reshape(1, D_MODEL), (0, chip * ds), (1, ds)),
        hg_lb_logits=lax.dynamic_slice(sm[11].reshape(2, D_MODEL), (0, chip * ds), (2, ds)),
        ffn_conv_w=lax.dynamic_slice(conv_w_full, (0, 0, chip * fs), (2, 3, fs)),
    )

    out_g, out_d, out_m, out_v = {}, {}, {}, {}

    def update(name, g2):
        shape = wts[name].shape
        d2, m2, v2 = _adamw(wts[name].reshape(g2.shape), mom1[name].reshape(g2.shape), mom2[name].reshape(g2.shape), g2, name=f"adamw_{name}")
        out_g[name], out_d[name], out_m[name], out_v[name] = g2.reshape(shape), d2.reshape(shape), m2.reshape(shape), v2.reshape(shape)
        return d2

    def update_layer(name, g2, layer, prev):
        res = _adamw_layer(wts[name], mom1[name], mom2[name], g2, layer, prev, name=f"adamw_{name}{layer}")
        out_g[name], out_d[name], out_m[name], out_v[name] = res
        return res

    g_down1, g_up1 = red["ffn1"].finish(red["hg"].token)
    down1 = update_layer("ffn_w_down", g_down1, 1, None)
    up1 = update_layer("ffn_w_up", g_up1, 1, None)
    red["mid"].to_core(up1[1])
    g_o, g_q, g_kv, g_down0, g_up0 = red["mid"].finish(up1[2])
    update("attn_w_o", g_o)
    update("attn_w_q", g_q)
    update("w_kv", g_kv)
    update_layer("ffn_w_down", g_down0, 0, down1)
    last = update_layer("ffn_w_up", g_up0, 0, up1)
    red["hg"].to_core(last[1])
    g_out, g_in = red["hg"].finish(last[2])
    update("hg_w_out", g_out)
    update("hg_w_in", g_in)

    small_names = [n for n in names if n not in out_g]
    cat = lambda d: jnp.concatenate([d[n].reshape(-1) for n in small_names])
    n_flat = sum(wts[n].size for n in small_names)
    srows = -(-n_flat // (SUBLANES * LANES)) * SUBLANES
    packed = [_pad_rows(cat(d), srows, LANES) for d in (wts, mom1, mom2, small_grads)]
    d_s, m_s, v_s = _adamw(*packed, name="adamw_small")
    off = 0
    for n in small_names:
        sz, shape = wts[n].size, wts[n].shape
        out_g[n] = small_grads[n].reshape(shape)
        out_d[n] = d_s.reshape(-1)[off:off + sz].reshape(shape)
        out_m[n] = m_s.reshape(-1)[off:off + sz].reshape(shape)
        out_v[n] = v_s.reshape(-1)[off:off + sz].reshape(shape)
        off += sz

    grad_x = dx.reshape(x.shape)
    return (loss_out, grad_x, *[out_g[n] for n in names], *[out_d[n] for n in names], *[out_m[n] for n in names], *[out_v[n] for n in names])
```

```python
import functools

import jax
import jax.numpy as jnp
from jax import lax
from jax.experimental import pallas as pl
from jax.experimental.pallas import tpu as pltpu

F32 = jnp.float32
BF16 = jnp.bfloat16
MESH = pl.DeviceIdType.MESH

EPS = 1e-6
D_MODEL = 1024
HG_HEADS = 8
HG_DK = 128
HG_CHUNK = 64
ATT_HD = 64
ATT_QH = 16
ATT_KVH = 2
ATT_GROUP = ATT_QH // ATT_KVH
WINDOW = 128
D_FF = 2816
N_CHIPS = 4
N_DEV = 8
LANES = 128
SUBLANES = 8
VMEM_LIMIT_BYTES = 56 * 1024 * 1024
NEG = -1e30
ALIBI_SLOPES = tuple(2.0 ** (-8.0 * h / ATT_QH) for h in range(1, ATT_QH + 1))

ADAM_LR = 0.001
ADAM_B1 = 0.9
ADAM_B2 = 0.999
ADAM_EPS = 1e-08
ADAM_WD = 0.01
ADAM_STEP = 10


def _cparams(sem=None):
    return pltpu.CompilerParams(dimension_semantics=sem, vmem_limit_bytes=VMEM_LIMIT_BYTES)


def _pick(n, cands):
    for c in cands:
        if n % c == 0:
            return c
    return n


def _sigmoid(x):
    return 0.5 * jnp.tanh(0.5 * x) + 0.5


def _dot(a, b, dims):
    return lax.dot_general(a, b, (dims, ((), ())), preferred_element_type=F32)


NN = ((1,), (0,))
NT = ((1,), (1,))
TN = ((0,), (0,))


MM_ROWS = 1024


def _rms_stats(xv):
    rstd = lax.rsqrt(jnp.mean(xv * xv, axis=-1, keepdims=True) + EPS)
    return xv * rstd, rstd


def _mm_operand(a_ref, gain_ref):
    if gain_ref is None:
        return a_ref[...].astype(BF16)
    return (_rms_stats(a_ref[...])[0] * gain_ref[...]).astype(BF16)


def _mm_nn(a, w, res=None, out_dtype=F32, name="mm_nn", gain=None):
    m, k = a.shape
    s, _, ns = w.shape
    tm = min(m, MM_ROWS)
    tn = _pick(ns, (1024, 1408, 512, 256, 128))
    npb = ns // tn

    def body(a_ref, w_ref, *rest):
        o_ref = rest[-1]
        acc = _dot(_mm_operand(a_ref, rest[0] if gain is not None else None), w_ref[...], NN)
        if res is not None:
            acc = acc + rest[-2][...]
        o_ref[...] = acc.astype(o_ref.dtype)

    in_specs = [
        pl.BlockSpec((tm, k), lambda i, j: (i, 0)),
        pl.BlockSpec((None, k, tn), lambda i, j: (j // npb, 0, j % npb)),
    ]
    args = [a, w]
    if gain is not None:
        in_specs.append(pl.BlockSpec((1, k), lambda i, j: (0, 0)))
        args.append(gain)
    if res is not None:
        in_specs.append(pl.BlockSpec((tm, tn), lambda i, j: (i, j)))
        args.append(res)
    return pl.pallas_call(
        body,
        name=name,
        grid=(m // tm, s * npb),
        in_specs=in_specs,
        out_specs=pl.BlockSpec((tm, tn), lambda i, j: (i, j)),
        out_shape=jax.ShapeDtypeStruct((m, s * ns), out_dtype),
        compiler_params=_cparams(("parallel", "parallel")),
    )(*args)


def _dy_spec(stacked, tm, tn, npb, row, kk):
    if stacked:
        return pl.BlockSpec((None, tm, tn), lambda *g: (kk(g) // npb, row(g), kk(g) % npb))
    return pl.BlockSpec((tm, tn), lambda *g: (row(g), kk(g)))


def _dep_specs(deps):
    return [pl.BlockSpec(d.shape, lambda *g: (0, 0)) for d in deps]


def _mm_nt(dy, w, stacked=False, out_dtype=F32, name="mm_nt", deps=(), norm_of=None):
    s, k, ns = w.shape
    m = dy.shape[1] if stacked else dy.shape[0]
    tm = min(m, MM_ROWS)
    tko = _pick(k, (1024, 1408, 512, 256))
    tn = _pick(ns, (1024, 1408, 512, 256))
    npb = ns // tn
    nk = s * npb
    fused = norm_of is not None
    assert not fused or tko == k

    def body(dy_ref, w_ref, *rest):
        acc_ref = rest[-1]
        i, kk = pl.program_id(0), pl.program_id(2)

        @pl.when(kk == 0)
        def _():
            acc_ref[...] = jnp.zeros_like(acc_ref)

        acc_ref[...] += _dot(dy_ref[...].astype(BF16), w_ref[...], NT)

        if not fused:
            @pl.when(kk == nk - 1)
            def _():
                rest[-2][...] = acc_ref[...].astype(rest[-2].dtype)
            return
        x_ref, g_ref, dres_ref = rest[:3]
        dx_ref, dg_ref = rest[-3], rest[-2]

        @pl.when(jnp.logical_and(i == 0, kk == 0))
        def _():
            dg_ref[...] = jnp.zeros_like(dg_ref)

        @pl.when(kk == nk - 1)
        def _():
            dxn = acc_ref[...]
            xhat, rstd = _rms_stats(x_ref[...])
            gd = dxn * g_ref[...]
            dx_ref[...] = dres_ref[...] + rstd * (gd - xhat * jnp.mean(gd * xhat, axis=-1, keepdims=True))
            dg_ref[...] += jnp.sum(dxn * xhat, axis=0, keepdims=True)

    row = pl.BlockSpec((tm, tko), lambda i, j, kk: (i, j))
    vec = pl.BlockSpec((1, k), lambda i, j, kk: (0, 0))
    return pl.pallas_call(
        body,
        name=name,
        grid=(m // tm, k // tko, nk),
        in_specs=[
            _dy_spec(stacked, tm, tn, npb, lambda g: g[0], lambda g: g[2]),
            pl.BlockSpec((None, tko, tn), lambda i, j, kk: (kk // npb, j, kk % npb)),
        ] + ([row, vec, row] if fused else []) + _dep_specs(deps),
        out_specs=[row, vec] if fused else row,
        out_shape=[jax.ShapeDtypeStruct((m, k), F32), jax.ShapeDtypeStruct((1, k), F32)] if fused else jax.ShapeDtypeStruct((m, k), out_dtype),
        scratch_shapes=[pltpu.VMEM((tm, tko), F32)],
        compiler_params=_cparams(("arbitrary",) * 3 if fused else ("parallel", "parallel", "arbitrary")),
    )(dy, w, *(norm_of or ()), *deps)


def _mm_tn(a, dy, s, ns, stacked=False, name="mm_tn", deps=(), gain=None):
    m, k = a.shape
    tm = min(m, MM_ROWS)
    tk = _pick(k, (1024, 1408, 512, 256))
    tn = _pick(ns, (1024, 1408, 512, 256, 128))
    npb = ns // tn
    nm = m // tm
    assert gain is None or tk == k

    def body(a_ref, dy_ref, *rest):
        o_ref, acc_ref = rest[-2:]
        mm = pl.program_id(2)

        @pl.when(mm == 0)
        def _():
            acc_ref[...] = jnp.zeros_like(acc_ref)

        acc_ref[...] += _dot(_mm_operand(a_ref, rest[0] if gain is not None else None), dy_ref[...].astype(BF16), TN)

        @pl.when(mm == nm - 1)
        def _():
            o_ref[...] = acc_ref[...]

    return pl.pallas_call(
        body,
        name=name,
        grid=(k // tk, s * npb, nm),
        in_specs=[
            pl.BlockSpec((tm, tk), lambda i, j, mm: (mm, i)),
            _dy_spec(stacked, tm, tn, npb, lambda g: g[2], lambda g: g[1]),
        ] + ([pl.BlockSpec((1, k), lambda i, j, mm: (0, 0))] if gain is not None else []) + _dep_specs(deps),
        out_specs=pl.BlockSpec((None, tk, tn), lambda i, j, mm: (j // npb, i, j % npb)),
        out_shape=jax.ShapeDtypeStruct((s, k, ns), F32),
        scratch_shapes=[pltpu.VMEM((tk, tn), F32)],
        compiler_params=_cparams(("parallel", "parallel", "arbitrary")),
    )(a, dy, *(() if gain is None else (gain,)), *deps)


ROW_TILE = 512


def _loss_head(h, g, target):
    t, d = h.shape
    r = min(t, ROW_TILE)

    def body(h_ref, g_ref, t_ref, dh_ref, dg_ref, loss_ref):
        @pl.when(pl.program_id(0) == 0)
        def _():
            dg_ref[...] = jnp.zeros_like(dg_ref)
            loss_ref[...] = jnp.zeros_like(loss_ref)

        xv = h_ref[...]
        rstd = lax.rsqrt(jnp.mean(xv * xv, axis=-1, keepdims=True) + EPS)
        xhat = xv * rstd
        gv = g_ref[...]
        err = xhat * gv - t_ref[...]
        loss_ref[...] += 0.5 * jnp.sum(jnp.mean(err * err, axis=-1, keepdims=True), axis=0, keepdims=True)
        dy = err * (1.0 / d)
        gd = dy * gv
        dh_ref[...] = rstd * (gd - xhat * jnp.mean(gd * xhat, axis=-1, keepdims=True))
        dg_ref[...] += jnp.sum(dy * xhat, axis=0, keepdims=True)

    return pl.pallas_call(
        body,
        name="loss_head",
        grid=(t // r,),
        in_specs=[
            pl.BlockSpec((r, d), lambda i: (i, 0)),
            pl.BlockSpec((1, d), lambda i: (0, 0)),
            pl.BlockSpec((r, d), lambda i: (i, 0)),
        ],
        out_specs=[
            pl.BlockSpec((r, d), lambda i: (i, 0)),
            pl.BlockSpec((1, d), lambda i: (0, 0)),
            pl.BlockSpec((1, LANES), lambda i: (0, 0)),
        ],
        out_shape=[
            jax.ShapeDtypeStruct((t, d), F32),
            jax.ShapeDtypeStruct((1, d), F32),
            jax.ShapeDtypeStruct((1, LANES), F32),
        ],
        compiler_params=_cparams(("arbitrary",)),
    )(h, g, target)


CONV_ROWS = 256
CONV_COLS = 1408


def _conv_taps(x_ext, n):
    tot = x_ext.shape[0]
    g1 = pltpu.roll(x_ext, 1, 0)[tot - n:]
    g2 = pltpu.roll(x_ext, 2, 0)[tot - n:]
    return g2, g1


def _conv_fwd(up, conv_w, conv_b, name="conv_fwd"):
    t = up.shape[0]
    r = min(t, CONV_ROWS)
    tc = CONV_COLS
    ncb = D_FF // tc
    hb = r // SUBLANES

    def body(g_ref, halo_ref, v_ref, w_ref, b_ref, o_ref):
        i = pl.program_id(1)
        g0 = g_ref[...]
        halo = halo_ref[...] * jnp.where(i > 0, 1.0, 0.0)
        g2, g1 = _conv_taps(jnp.concatenate([halo, g0], axis=0), r)
        c = b_ref[...] + w_ref[0:1, :] * g2 + w_ref[1:2, :] * g1 + w_ref[2:3, :] * g0
        o_ref[...] = (c * _sigmoid(c) * v_ref[...]).astype(BF16)

    return pl.pallas_call(
        body,
        name=name,
        grid=(ncb, t // r),
        in_specs=[
            pl.BlockSpec((r, tc), lambda j, i: (i, j)),
            pl.BlockSpec((SUBLANES, tc), lambda j, i: (jnp.maximum(i * hb - 1, 0), j)),
            pl.BlockSpec((r, tc), lambda j, i: (i, ncb + j)),
            pl.BlockSpec((3, tc), lambda j, i: (0, j)),
            pl.BlockSpec((1, tc), lambda j, i: (0, j)),
        ],
        out_specs=pl.BlockSpec((r, tc), lambda j, i: (i, j)),
        out_shape=jax.ShapeDtypeStruct((t, D_FF), BF16),
        compiler_params=_cparams(("parallel", "parallel")),
    )(up, up, up, conv_w, conv_b)


def _conv_bwd(up, conv_w, conv_b, dact, name="conv_bwd"):
    t = up.shape[0]
    r = min(t, CONV_ROWS)
    tc = CONV_COLS
    ncb = D_FF // tc
    hb = r // SUBLANES
    nrt = t // r

    def body(g_ref, halo_ref, v_ref, w_ref, b_ref, da_ref, dup_ref, dw_ref, db_ref, nxt_ref):
        ii = pl.program_id(1)
        i = nrt - 1 - ii

        @pl.when(ii == 0)
        def _():
            nxt_ref[...] = jnp.zeros_like(nxt_ref)
            dw_ref[...] = jnp.zeros_like(dw_ref)
            db_ref[...] = jnp.zeros_like(db_ref)

        g0 = g_ref[...]
        halo = halo_ref[...] * jnp.where(i > 0, 1.0, 0.0)
        g2, g1 = _conv_taps(jnp.concatenate([halo, g0], axis=0), r)
        w0, w1, w2 = w_ref[0:1, :], w_ref[1:2, :], w_ref[2:3, :]
        c = b_ref[...] + w0 * g2 + w1 * g1 + w2 * g0
        sg = _sigmoid(c)
        da = da_ref[...]
        dval = da * (c * sg)
        dc = da * v_ref[...] * (sg * (1.0 + c * (1.0 - sg)))
        db_ref[...] += jnp.sum(dc, axis=0, keepdims=True)
        dw_ref[0:1, :] += jnp.sum(dc * g2, axis=0, keepdims=True)
        dw_ref[1:2, :] += jnp.sum(dc * g1, axis=0, keepdims=True)
        dw_ref[2:3, :] += jnp.sum(dc * g0, axis=0, keepdims=True)
        ext = jnp.concatenate([dc, nxt_ref[...]], axis=0)
        tot = r + SUBLANES
        d1 = pltpu.roll(ext, tot - 1, 0)[:r]
        d2 = pltpu.roll(ext, tot - 2, 0)[:r]
        dgate = w2 * dc + w1 * d1 + w0 * d2
        nxt_ref[...] = dc[:SUBLANES]
        dup_ref[0] = dgate.astype(BF16)
        dup_ref[1] = dval.astype(BF16)

    rev = lambda ii: nrt - 1 - ii
    dup, dw, db = pl.pallas_call(
        body,
        name=name,
        grid=(ncb, nrt),
        in_specs=[
            pl.BlockSpec((r, tc), lambda j, ii: (rev(ii), j)),
            pl.BlockSpec((SUBLANES, tc), lambda j, ii: (jnp.maximum(rev(ii) * hb - 1, 0), j)),
            pl.BlockSpec((r, tc), lambda j, ii: (rev(ii), ncb + j)),
            pl.BlockSpec((3, tc), lambda j, ii: (0, j)),
            pl.BlockSpec((1, tc), lambda j, ii: (0, j)),
            pl.BlockSpec((r, tc), lambda j, ii: (rev(ii), j)),
        ],
        out_specs=[
            pl.BlockSpec((2, None, r, tc), lambda j, ii: (0, j, rev(ii), 0)),
            pl.BlockSpec((3, tc), lambda j, ii: (0, j)),
            pl.BlockSpec((1, tc), lambda j, ii: (0, j)),
        ],
        out_shape=[
            jax.ShapeDtypeStruct((2, ncb, t, tc), BF16),
            jax.ShapeDtypeStruct((3, D_FF), F32),
            jax.ShapeDtypeStruct((1, D_FF), F32),
        ],
        scratch_shapes=[pltpu.VMEM((SUBLANES, tc), F32)],
        compiler_params=_cparams(("parallel", "arbitrary")),
    )(up, up, up, conv_w, conv_b, dact)
    return dup.reshape(2 * ncb, t, tc), dw, db


def _split3(x):
    x1 = x.astype(BF16)
    r1 = x - x1.astype(F32)
    x2 = r1.astype(BF16)
    x3 = (r1 - x2.astype(F32)).astype(BF16)
    return x1, x2, x3


def _tri_dot(tri, x, dims):
    x1, x2, x3 = _split3(x)
    return _dot(tri, x1, dims) + _dot(tri, x2, dims) + _dot(tri, x3, dims)


def _lower_bound(logits_ref):
    return _sigmoid(logits_ref[0:1, :] - logits_ref[1:2, :])


def _hg_gates(qr, fr, lb):
    q = qr * _sigmoid(qr) * (HG_DK ** -0.5)
    sf = _sigmoid(fr)
    fg = lb + (1.0 - lb) * sf
    return q, sf, fg


def _hg_chunk_terms(q, fg, tril_b, low_half):
    g = jnp.log(fg)
    k = 1.0 - fg
    cum = _tri_dot(tril_b, g, NN)
    c_last = jnp.sum(g, axis=0, keepdims=True)
    c_mid = jnp.sum(jnp.where(low_half, g, 0.0), axis=0, keepdims=True)
    e_q = jnp.exp(cum - c_mid)
    e_k = jnp.exp(c_mid - cum)
    e_0 = jnp.exp(cum)
    e_l = jnp.exp(c_last - cum)
    return k, e_q, e_k, e_0, e_l, jnp.exp(c_last)


HG_BLOCK = 256


def _hg_proj_specs(rb, row):
    return [pl.BlockSpec((rb, D_MODEL), functools.partial(lambda i, k: (row(i), k), k=k)) for k in range(4)]


def _hg_consts(c):
    tril = lax.broadcasted_iota(jnp.int32, (c, c), 0) >= lax.broadcasted_iota(jnp.int32, (c, c), 1)
    low_half = lax.broadcasted_iota(jnp.int32, (c, D_MODEL), 0) < c // 2
    return tril, tril.astype(BF16), low_half


def _hgrn_fwd(proj, lb, wn):
    t = proj.shape[0]
    c = HG_CHUNK
    rb = min(t, HG_BLOCK)
    cpb = rb // c

    def body(q_ref, f_ref, i_ref, g_ref, lb_ref, wn_ref, o_ref, y_ref, st_ref, s_scr):
        @pl.when(pl.program_id(0) == 0)
        def _():
            s_scr[...] = jnp.zeros_like(s_scr)

        lb_all = _lower_bound(lb_ref)
        wnv = wn_ref[...]
        tril, tril_b, low_half = _hg_consts(c)

        def chunk(n, carry):
            rows = pl.ds(pl.multiple_of(n * c, c), c)
            q, _, fg = _hg_gates(q_ref[rows, :], f_ref[rows, :], lb_all)
            k, e_q, e_k, e_0, e_l, e_last = _hg_chunk_terms(q, fg, tril_b, low_half)
            qi, ki, q0, kl = (q * e_q).astype(BF16), (k * e_k).astype(BF16), (q * e_0).astype(BF16), (k * e_l).astype(BF16)
            v = i_ref[rows, :].astype(BF16)
            gr = g_ref[rows, :]
            gate = gr * _sigmoid(gr)
            for h in range(HG_HEADS):
                cols = slice(h * HG_DK, (h + 1) * HG_DK)
                st = s_scr[h]
                st_ref[h, n] = st
                a = jnp.where(tril, _dot(qi[:, cols], ki[:, cols], NT), 0.0)
                o = _dot(q0[:, cols], st.astype(BF16), NT) + _dot(a.astype(BF16), v[:, cols], NN)
                s_scr[h] = st * e_last[:, cols] + _dot(v[:, cols], kl[:, cols], TN)
                o_ref[rows, cols] = o
                rstd = lax.rsqrt(jnp.mean(o * o, axis=-1, keepdims=True) + EPS)
                y_ref[rows, cols] = (o * rstd * wnv * gate[:, cols]).astype(BF16)
            return carry

        lax.fori_loop(0, cpb, chunk, 0)

    blk = pl.BlockSpec((rb, D_MODEL), lambda i: (i, 0))
    return pl.pallas_call(
        body,
        name="hgrn_fwd",
        grid=(t // rb,),
        in_specs=_hg_proj_specs(rb, lambda i: i) + [pl.BlockSpec((2, D_MODEL), lambda i: (0, 0)), pl.BlockSpec((1, HG_DK), lambda i: (0, 0))],
        out_specs=[blk, blk, pl.BlockSpec((HG_HEADS, cpb, HG_DK, HG_DK), lambda i: (0, i, 0, 0))],
        out_shape=[
            jax.ShapeDtypeStruct((t, D_MODEL), F32),
            jax.ShapeDtypeStruct((t, D_MODEL), BF16),
            jax.ShapeDtypeStruct((HG_HEADS, t // c, HG_DK, HG_DK), F32),
        ],
        scratch_shapes=[pltpu.VMEM((HG_HEADS, HG_DK, HG_DK), F32)],
        compiler_params=_cparams(("arbitrary",)),
    )(proj, proj, proj, proj, lb, wn)


def _hgrn_bwd(proj, lb, wn, o, states, dy):
    t = proj.shape[0]
    c = HG_CHUNK
    rb = min(t, HG_BLOCK)
    cpb = rb // c
    nb = t // rb

    def body(q_ref, f_ref, i_ref, g_ref, lb_ref, wn_ref, o_ref, st_ref, dy_ref, dp_ref, dl_ref, dwn_ref, ds_scr, dlb_scr):
        step = pl.program_id(0)

        @pl.when(step == 0)
        def _():
            dwn_ref[...] = jnp.zeros_like(dwn_ref)
            ds_scr[...] = jnp.zeros_like(ds_scr)
            dlb_scr[...] = jnp.zeros_like(dlb_scr)

        lb_all = _lower_bound(lb_ref)
        wnv = wn_ref[...]
        tril, tril_b, low_half = _hg_consts(c)

        def chunk(nn, carry):
            n = cpb - 1 - nn
            rows = pl.ds(pl.multiple_of(n * c, c), c)
            qr = q_ref[rows, :]
            gr = g_ref[rows, :]
            q, sf, fg = _hg_gates(qr, f_ref[rows, :], lb_all)
            k, e_q, e_k, e_0, e_l, e_last = _hg_chunk_terms(q, fg, tril_b, low_half)
            qi, qi_lo, _ = _split3(q * e_q)
            ki, ki_lo, _ = _split3(k * e_k)
            q0 = (q * e_0).astype(BF16)
            kl = (k * e_l).astype(BF16)
            v = i_ref[rows, :].astype(BF16)
            sg = _sigmoid(gr)
            silu_g = gr * sg
            dsilu_g = sg * (1.0 + gr * (1.0 - sg))
            dqs, dks, d_lasts = [], [], []
            for h in range(HG_HEADS):
                cols = slice(h * HG_DK, (h + 1) * HG_DK)
                ov = o_ref[rows, cols]
                dyv = dy_ref[rows, cols].astype(F32)
                rstd = lax.rsqrt(jnp.mean(ov * ov, axis=-1, keepdims=True) + EPS)
                ohat = ov * rstd
                dp_ref[3, rows, cols] = (dyv * (ohat * wnv) * dsilu_g[:, cols]).astype(BF16)
                don = dyv * silu_g[:, cols]
                dwn_ref[...] += jnp.sum(don * ohat, axis=0, keepdims=True)
                gd = don * wnv
                do_b = (rstd * (gd - ohat * jnp.mean(gd * ohat, axis=-1, keepdims=True))).astype(BF16)
                st = st_ref[h, n]
                ds = ds_scr[h]
                ds_b = ds.astype(BF16)
                vh, kh = v[:, cols], k[:, cols]
                a_b = jnp.where(tril, _dot(qi[:, cols], ki[:, cols], NT), 0.0).astype(BF16)
                da_b = jnp.where(tril, _dot(do_b, vh, NT), 0.0).astype(BF16)
                dqs.append(_dot(do_b, st.astype(BF16), NN) * e_0[:, cols]
                           + (_dot(da_b, ki[:, cols], NN) + _dot(da_b, ki_lo[:, cols], NN)) * e_q[:, cols])
                dk_state = _dot(vh, ds_b, NN) * e_l[:, cols]
                dks.append((_dot(da_b, qi[:, cols], TN) + _dot(da_b, qi_lo[:, cols], TN)) * e_k[:, cols] + dk_state)
                dp_ref[2, rows, cols] = (_dot(a_b, do_b, TN) + _dot(kl[:, cols], ds_b, NT)).astype(BF16)
                ds_scr[h] = ds * e_last[:, cols] + _dot(do_b, q0[:, cols], TN)
                d_lasts.append(jnp.sum(dk_state * kh, axis=0, keepdims=True) + jnp.sum(ds * st, axis=0, keepdims=True) * e_last[:, cols])
            dq = jnp.concatenate(dqs, axis=1)
            dk = jnp.concatenate(dks, axis=1)
            dlogf = _tri_dot(tril_b, q * dq - k * dk, TN) + jnp.concatenate(d_lasts, axis=1)
            dfg = dlogf / fg - dk
            dlb_scr[...] += jnp.sum(dfg * (1.0 - sf), axis=0, keepdims=True)
            sq = _sigmoid(qr)
            dp_ref[0, rows, :] = (dq * (HG_DK ** -0.5) * (sq * (1.0 + qr * (1.0 - sq)))).astype(BF16)
            dp_ref[1, rows, :] = (dfg * (1.0 - lb_all) * sf * (1.0 - sf)).astype(BF16)
            return carry

        lax.fori_loop(0, cpb, chunk, 0)

        @pl.when(step == nb - 1)
        def _():
            d0 = dlb_scr[...] * lb_all * (1.0 - lb_all)
            dl_ref[0:1, :] = d0
            dl_ref[1:2, :] = -d0

    rev = lambda i: nb - 1 - i
    blk = pl.BlockSpec((rb, D_MODEL), lambda i: (rev(i), 0))
    return pl.pallas_call(
        body,
        name="hgrn_bwd",
        grid=(nb,),
        in_specs=_hg_proj_specs(rb, rev)
        + [pl.BlockSpec((2, D_MODEL), lambda i: (0, 0)), pl.BlockSpec((1, HG_DK), lambda i: (0, 0)), blk,
           pl.BlockSpec((HG_HEADS, cpb, HG_DK, HG_DK), lambda i: (0, rev(i), 0, 0)), blk],
        out_specs=[
            pl.BlockSpec((4, rb, D_MODEL), lambda i: (0, rev(i), 0)),
            pl.BlockSpec((2, D_MODEL), lambda i: (0, 0)),
            pl.BlockSpec((1, HG_DK), lambda i: (0, 0)),
        ],
        out_shape=[
            jax.ShapeDtypeStruct((4, t, D_MODEL), BF16),
            jax.ShapeDtypeStruct((2, D_MODEL), F32),
            jax.ShapeDtypeStruct((1, HG_DK), F32),
        ],
        scratch_shapes=[pltpu.VMEM((HG_HEADS, HG_DK, HG_DK), F32), pltpu.VMEM((1, D_MODEL), F32)],
        compiler_params=_cparams(("arbitrary",)),
    )(proj, proj, proj, proj, lb, wn, o, states, dy)


ATT_STACK = 8


def _att_stack(q_ref, sink_ref, first, lo, bias_p, bias_c, extra_ref=None):
    qs, bps, bcs, sinks, extras = [], [], [], None, []
    rows = lax.broadcasted_iota(jnp.int32, (ATT_STACK * WINDOW, 1), 0)
    for i in range(ATT_STACK):
        hq = first + i
        cols = slice((hq // 2) * LANES, (hq // 2 + 1) * LANES)
        sel = lo if hq % 2 == 0 else jnp.logical_not(lo)
        qp = q_ref[:, cols] * (ATT_HD ** -0.5)
        qs.append(jnp.where(sel, qp, jnp.zeros_like(qp)))
        bps.append(ALIBI_SLOPES[hq] * bias_p)
        bcs.append(ALIBI_SLOPES[hq] * bias_c)
        sinks = sink_ref[hq] if sinks is None else jnp.where(rows < i * WINDOW, sinks, sink_ref[hq])
        if extra_ref is not None:
            ep = extra_ref[:, cols]
            extras.append(jnp.where(sel, ep, jnp.zeros_like(ep)))
    cat = lambda parts: jnp.concatenate(parts, axis=0)
    return cat(qs), cat(bps), cat(bcs), sinks, (cat(extras) if extras else None)


def _att_rows(i):
    return slice(i * WINDOW, (i + 1) * WINDOW)


def _att_bias(n):
    tq = lax.broadcasted_iota(jnp.int32, (WINDOW, WINDOW), 0)
    sk = lax.broadcasted_iota(jnp.int32, (WINDOW, WINDOW), 1)
    valid_c = sk <= tq
    valid_p = (sk - tq) > jnp.where(n > 0, 0, WINDOW)
    dist_c = (tq - sk).astype(F32)
    return jnp.where(valid_p, -dist_c - float(WINDOW), NEG), jnp.where(valid_c, -dist_c, NEG)


def _att_halves(x, lo, kh):
    r = pltpu.roll(x, ATT_HD, 1)
    zero = jnp.zeros_like(x)
    if kh == 0:
        return jnp.where(lo, x, r), jnp.where(lo, x, zero), jnp.where(lo, zero, r)
    return jnp.where(lo, r, x), jnp.where(lo, r, zero), jnp.where(lo, zero, x)


def _att_probs(qm, k2p, k2c, bias_p, bias_c, sink):
    sp = _dot(qm, k2p, NT) + bias_p
    sc = _dot(qm, k2c, NT) + bias_c
    m = jnp.maximum(jnp.maximum(jnp.max(sp, axis=-1, keepdims=True), jnp.max(sc, axis=-1, keepdims=True)), sink)
    ep = jnp.exp(sp - m)
    ec = jnp.exp(sc - m)
    es = jnp.exp(sink - m)
    inv = 1.0 / (jnp.sum(ep, axis=-1, keepdims=True) + jnp.sum(ec, axis=-1, keepdims=True) + es)
    return ep * inv, ec * inv, es * inv


def _attn_fwd(q, kv, sinks):
    t = q.shape[0]
    nb = t // WINDOW

    def body(sink_ref, q_ref, kvp_ref, kvc_ref, o_ref):
        n = pl.program_id(0)
        bias_p, bias_c = _att_bias(n)
        lo = lax.broadcasted_iota(jnp.int32, (WINDOW, LANES), 1) < ATT_HD
        for kh in range(ATT_KVH):
            k2p, _, _ = _att_halves(kvp_ref[:, 0:LANES], lo, kh)
            k2c, _, _ = _att_halves(kvc_ref[:, 0:LANES], lo, kh)
            _, vlo_p, vhi_p = _att_halves(kvp_ref[:, LANES:2 * LANES], lo, kh)
            _, vlo_c, vhi_c = _att_halves(kvc_ref[:, LANES:2 * LANES], lo, kh)
            for first in range(kh * ATT_GROUP, (kh + 1) * ATT_GROUP, ATT_STACK):
                qs, bp, bc, sinks, _ = _att_stack(q_ref, sink_ref, first, lo, bias_p, bias_c)
                pp, pc, _ = _att_probs(qs, k2p, k2c, bp, bc, sinks)
                pp, pc = pp.astype(BF16), pc.astype(BF16)
                for i in range(0, ATT_STACK, 2):
                    even, odd = _att_rows(i), _att_rows(i + 1)
                    out = (_dot(pp[even], vlo_p, NN) + _dot(pc[even], vlo_c, NN)
                           + _dot(pp[odd], vhi_p, NN) + _dot(pc[odd], vhi_c, NN))
                    j = (first + i) // 2
                    o_ref[:, j * LANES:(j + 1) * LANES] = out.astype(BF16)

    return pl.pallas_call(
        body,
        name="attn_fwd",
        grid=(nb,),
        in_specs=[
            pl.BlockSpec(memory_space=pltpu.SMEM),
            pl.BlockSpec((WINDOW, D_MODEL), lambda n: (n, 0)),
            pl.BlockSpec((WINDOW, 2 * LANES), lambda n: (jnp.maximum(n - 1, 0), 0)),
            pl.BlockSpec((WINDOW, 2 * LANES), lambda n: (n, 0)),
        ],
        out_specs=pl.BlockSpec((WINDOW, D_MODEL), lambda n: (n, 0)),
        out_shape=jax.ShapeDtypeStruct((t, D_MODEL), BF16),
        compiler_params=_cparams(("parallel",)),
    )(sinks, q, kv, kv)


def _attn_bwd(q, kv, sinks, dout):
    t = q.shape[0]
    nb = t // WINDOW

    def body(sink_ref, q_ref, kvp_ref, kvc_ref, do_ref, dq_ref, dkv_ref, dsink_ref, carry_ref):
        n = pl.program_id(0)

        @pl.when(n == 0)
        def _():
            carry_ref[...] = jnp.zeros_like(carry_ref)
            dsink_ref[...] = jnp.zeros_like(dsink_ref)

        @pl.when(n == nb)
        def _():
            dkv_ref[...] = carry_ref[...].astype(BF16)

        @pl.when(n < nb)
        def _():
            bias_p, bias_c = _att_bias(n)
            lo = lax.broadcasted_iota(jnp.int32, (WINDOW, LANES), 1) < ATT_HD
            lane1 = lax.broadcasted_iota(jnp.int32, (1, LANES), 1)
            dsink = jnp.zeros((1, LANES), F32)
            halves = []
            for kh in range(ATT_KVH):
                k2p, klo_p, khi_p = _att_halves(kvp_ref[:, 0:LANES], lo, kh)
                k2c, klo_c, khi_c = _att_halves(kvc_ref[:, 0:LANES], lo, kh)
                v2p, _, _ = _att_halves(kvp_ref[:, LANES:2 * LANES], lo, kh)
                v2c, _, _ = _att_halves(kvc_ref[:, LANES:2 * LANES], lo, kh)
                acc = [jnp.zeros((WINDOW, LANES), F32) for _ in range(4)]
                for first in range(kh * ATT_GROUP, (kh + 1) * ATT_GROUP, ATT_STACK):
                    qs, bp, bc, sinks, dos = _att_stack(q_ref, sink_ref, first, lo, bias_p, bias_c, do_ref)
                    pp, pc, ps = _att_probs(qs, k2p, k2c, bp, bc, sinks)
                    dpp = _dot(dos, v2p, NT)
                    dpc = _dot(dos, v2c, NT)
                    delta = jnp.sum(pp * dpp, axis=-1, keepdims=True) + jnp.sum(pc * dpc, axis=-1, keepdims=True)
                    dsp = (pp * (dpp - delta)).astype(BF16)
                    dsc = (pc * (dpc - delta)).astype(BF16)
                    sink_term = ps * delta
                    for i in range(ATT_STACK):
                        dsink = dsink + jnp.where(lane1 == first + i, -jnp.sum(sink_term[_att_rows(i)], axis=0, keepdims=True), 0.0)
                    for i in range(0, ATT_STACK, 2):
                        even, odd = _att_rows(i), _att_rows(i + 1)
                        dq_pair = (_dot(dsp[even], klo_p, NN) + _dot(dsc[even], klo_c, NN)
                                   + _dot(dsp[odd], khi_p, NN) + _dot(dsc[odd], khi_c, NN))
                        j = (first + i) // 2
                        dq_ref[:, j * LANES:(j + 1) * LANES] = (dq_pair * (ATT_HD ** -0.5)).astype(BF16)
                    acc[0] = acc[0] + _dot(dsp, qs, TN)
                    acc[1] = acc[1] + _dot(dsc, qs, TN)
                    acc[2] = acc[2] + _dot(pp.astype(BF16), dos, TN)
                    acc[3] = acc[3] + _dot(pc.astype(BF16), dos, TN)
                halves.append([a + pltpu.roll(a, ATT_HD, 1) for a in acc])
            prev = jnp.concatenate(
                [jnp.where(lo, halves[0][0], halves[1][0]), jnp.where(lo, halves[0][2], halves[1][2])], axis=1)
            cur = jnp.concatenate(
                [jnp.where(lo, halves[0][1], halves[1][1]), jnp.where(lo, halves[0][3], halves[1][3])], axis=1)
            dkv_ref[...] = (carry_ref[...] + prev).astype(BF16)
            carry_ref[...] = cur
            dsink_ref[...] += dsink

    blk = lambda n: jnp.minimum(n, nb - 1)
    return pl.pallas_call(
        body,
        name="attn_bwd",
        grid=(nb + 1,),
        in_specs=[
            pl.BlockSpec(memory_space=pltpu.SMEM),
            pl.BlockSpec((WINDOW, D_MODEL), lambda n: (blk(n), 0)),
            pl.BlockSpec((WINDOW, 2 * LANES), lambda n: (jnp.maximum(blk(n) - 1, 0), 0)),
            pl.BlockSpec((WINDOW, 2 * LANES), lambda n: (blk(n), 0)),
            pl.BlockSpec((WINDOW, D_MODEL), lambda n: (blk(n), 0)),
        ],
        out_specs=[
            pl.BlockSpec((WINDOW, D_MODEL), lambda n: (blk(n), 0)),
            pl.BlockSpec((WINDOW, 2 * LANES), lambda n: (jnp.maximum(n - 1, 0), 0)),
            pl.BlockSpec((1, LANES), lambda n: (0, 0)),
        ],
        out_shape=[
            jax.ShapeDtypeStruct((t, D_MODEL), BF16),
            jax.ShapeDtypeStruct((t, 2 * LANES), BF16),
            jax.ShapeDtypeStruct((1, LANES), F32),
        ],
        scratch_shapes=[pltpu.VMEM((WINDOW, 2 * LANES), F32)],
        compiler_params=_cparams(("arbitrary",)),
    )(sinks, q, kv, kv, dout)


def _ffn_fwd(h, norm_g, w_up, conv_w, conv_b, w_down, tag, after_up=lambda up: None):
    up = _mm_nn(h, w_up, gain=norm_g, name=f"ffn{tag}_up")
    after_up(up)
    act = _conv_fwd(up, conv_w, conv_b, name=f"ffn{tag}_conv")
    h_out = _mm_nn(act, w_down, res=h, name=f"ffn{tag}_down")
    return h_out, (up, act)


def _ffn_bwd(dh, h, norm_g, w_up, conv_w, conv_b, w_down, saved, tag, deps=()):
    up, act = saved
    dw_down = _mm_tn(act, dh, 1, D_MODEL, name=f"ffn{tag}_dwdown", deps=deps)
    dact = _mm_nt(dh, w_down, name=f"ffn{tag}_dact", deps=deps)
    dup, dconv_w, dconv_b = _conv_bwd(up, conv_w, conv_b, dact, name=f"ffn{tag}_dconv")
    dw_up = _mm_tn(h, dup, N_CHIPS, CONV_COLS, stacked=True, gain=norm_g, name=f"ffn{tag}_dwup")
    dh_in, dnorm = _mm_nt(dup, w_up, stacked=True, norm_of=(h, norm_g, dh), name=f"ffn{tag}_dxn")
    return dh_in, dict(ffn_w_down=dw_down, ffn_w_up=dw_up, ffn_conv_w=dconv_w, ffn_conv_b=dconv_b, ffn_norm=dnorm)


def _local_step(x, target, w, fetch=lambda w, stage, after: w, hook=lambda point, dh, grads: ()):
    proj = _mm_nn(x, w["hg_w_in"], gain=w["hg_norm"], name="hg_in")
    o, y, states = _hgrn_fwd(proj, w["hg_lb"], w["hg_out_norm"])
    w = fetch(w, "mixer_out", y)
    fetch(w, "layer0_relay", y)
    h_a = _mm_nn(y, w["hg_w_out"], res=x, name="hg_out")
    w = fetch(w, "layer0", h_a)
    h1, ffn0 = _ffn_fwd(h_a, w["ffn_norm"][0], w["ffn_w_up"][0], w["ffn_conv_w"][0], w["ffn_conv_b"][0], w["ffn_w_down"][0], 0,
                        lambda up: fetch(w, "layer1_relay", up))
    w = fetch(w, "layer1", h1)
    kv = _mm_nn(h1, w["w_kv"], gain=w["kv_norm"], out_dtype=BF16, name="kv_proj")
    qa = _mm_nn(h1, w["attn_w_q"], gain=w["attn_norm"], out_dtype=BF16, name="attn_q")
    ao = _attn_fwd(qa, kv, w["attn_sinks"])
    h_b = _mm_nn(ao, w["attn_w_o"], res=h1, name="attn_o")
    h2, ffn1 = _ffn_fwd(h_b, w["ffn_norm"][1], w["ffn_w_up"][1], w["ffn_conv_w"][1], w["ffn_conv_b"][1], w["ffn_w_down"][1], 1)
    dh2, d_final, loss = _loss_head(h2, w["final_norm"], target)

    dh_b, g1 = _ffn_bwd(dh2, h_b, w["ffn_norm"][1], w["ffn_w_up"][1], w["ffn_conv_w"][1], w["ffn_conv_b"][1], w["ffn_w_down"][1], ffn1, 1)
    deps = hook("ffn1", dh_b, g1)
    dw_o = _mm_tn(ao, dh_b, 1, D_MODEL, name="attn_dwo", deps=deps)
    dao = _mm_nt(dh_b, w["attn_w_o"], out_dtype=BF16, name="attn_dao", deps=deps)
    dqa, dkv, dsinks = _attn_bwd(qa, kv, w["attn_sinks"], dao)
    dw_q = _mm_tn(h1, dqa, 1, D_MODEL, gain=w["attn_norm"], name="attn_dwq")
    dh1, d_attn_norm = _mm_nt(dqa, w["attn_w_q"], norm_of=(h1, w["attn_norm"], dh_b), name="attn_dxa")
    dw_kv = _mm_tn(h1, dkv, 1, 2 * LANES, gain=w["kv_norm"], name="kv_dw")
    dh1, d_kv_norm = _mm_nt(dkv, w["w_kv"], norm_of=(h1, w["kv_norm"], dh1), name="kv_dx")
    deps = hook("attn", dh1, dict(attn_w_o=dw_o, attn_w_q=dw_q, w_kv=dw_kv))
    dh_a, g0 = _ffn_bwd(dh1, h_a, w["ffn_norm"][0], w["ffn_w_up"][0], w["ffn_conv_w"][0], w["ffn_conv_b"][0], w["ffn_w_down"][0], ffn0, 0, deps)
    deps = hook("ffn0", dh_a, g0)
    dw_out = _mm_tn(y, dh_a, 1, D_MODEL, name="hg_dwout", deps=deps)
    dy = _mm_nt(dh_a, w["hg_w_out"], out_dtype=BF16, name="hg_dy", deps=deps)
    dproj, dlb, d_out_norm = _hgrn_bwd(proj, w["hg_lb"], w["hg_out_norm"], o, states, dy)
    deps = hook("hgrn", dproj, None)
    dw_in = _mm_tn(x, dproj, N_CHIPS, D_MODEL, stacked=True, gain=w["hg_norm"], name="hg_dwin", deps=deps)
    deps = hook("hg_w", dproj, dict(hg_w_out=dw_out, hg_w_in=dw_in))
    dx, d_hg_norm = _mm_nt(dproj, w["hg_w_in"], stacked=True, norm_of=(x, w["hg_norm"], dh_a), name="hg_dxn", deps=deps)

    grads = dict(
        hg_norm=d_hg_norm, hg_w_in=dw_in, hg_lb=dlb, hg_out_norm=d_out_norm, hg_w_out=dw_out,
        kv_norm=d_kv_norm, w_kv=dw_kv, attn_norm=d_attn_norm, attn_w_q=dw_q, attn_sinks=dsinks, attn_w_o=dw_o,
        final_norm=d_final,
    )
    for name in g0:
        grads[name] = [g0[name], g1[name]]
    return loss, dx, grads


ANY = pl.BlockSpec(memory_space=pl.ANY)


def _place():
    x, y, c = lax.axis_index("x"), lax.axis_index("y"), lax.axis_index("c")
    chips = [(1 - x, y), (x, 1 - y), (1 - x, 1 - y)]
    return x, y, c, chips


def _rcopy(src, dst, send_sem, recv_sem, to):
    return pltpu.make_async_remote_copy(src_ref=src, dst_ref=dst, send_sem=send_sem, recv_sem=recv_sem, device_id=to, device_id_type=MESH)


HBM = pl.BlockSpec(memory_space=pltpu.HBM)
SEM = pl.BlockSpec(memory_space=pltpu.SEMAPHORE)
EFFECT = pltpu.SideEffectType.DATAFLOW_SIDE_EFFECTING


def _in_hbm(a):
    return pltpu.with_memory_space_constraint(a, pltpu.HBM)


def _place_shard(shard, place, dtype, name, deps=()):
    r, cols = shard.shape
    tr = _pick(r, ELEM_ROWS)

    def body(place_ref, s_ref, *rest):
        o_ref = rest[-1]
        o_ref[...] = s_ref[...].astype(o_ref.dtype)

    return pl.pallas_call(
        body,
        name=name,
        grid_spec=pltpu.PrefetchScalarGridSpec(
            num_scalar_prefetch=1,
            grid=(r // tr,),
            in_specs=[pl.BlockSpec((tr, cols), lambda i, place_ref: (i, 0))] + _dep_specs(deps),
            out_specs=pl.BlockSpec((None, tr, cols), lambda i, place_ref: (place_ref[0], i, 0)),
        ),
        out_shape=jax.ShapeDtypeStruct((N_CHIPS, r, cols), dtype),
        compiler_params=_cparams(("parallel",)),
    )(place, shard, *deps)


def _start_copies(name, bufs, n_sem, copies):
    n = len(bufs)

    def body(*refs):
        for cp in copies(refs[:n], refs[n], refs[n + 1]):
            cp.start()
        refs[-1][...] = jnp.zeros_like(refs[-1])

    outs = pl.pallas_call(
        body,
        name=name,
        in_specs=[HBM] * n,
        out_specs=[SEM, SEM] + [HBM] * n + [pl.BlockSpec(memory_space=pltpu.VMEM)],
        out_shape=[pltpu.SemaphoreType.DMA((n_sem,)), pltpu.SemaphoreType.DMA((n_sem,))] + [pltpu.HBM(b.shape, b.dtype) for b in bufs]
        + [jax.ShapeDtypeStruct((SUBLANES, LANES), F32)],
        input_output_aliases={i: 2 + i for i in range(n)},
        compiler_params=pltpu.CompilerParams(has_side_effects=EFFECT),
    )(*[_in_hbm(b) for b in bufs])
    return outs[0], outs[1], list(outs[2:-1]), outs[-1]


def _wait_copies(name, bufs, send_sems, recv_sems, after, copies):
    n = len(bufs)

    def body(*refs):
        for cp in copies(refs[:n], refs[n], refs[n + 1]):
            cp.wait_send()
            cp.wait_recv()

    return pl.pallas_call(
        body,
        name=name,
        in_specs=[HBM] * n + [SEM, SEM, ANY],
        out_specs=[HBM] * n,
        out_shape=[pltpu.HBM(b.shape, b.dtype) for b in bufs],
        input_output_aliases={i: i for i in range(n)},
        compiler_params=pltpu.CompilerParams(has_side_effects=EFFECT),
    )(*bufs, send_sems, recv_sems, after)


def _relay_copies(name, bufs, send_sems, recv_sems, after, landed, n_sem, onward):
    n = len(bufs)

    def body(*refs):
        for cp in landed(refs[:n], refs[n], refs[n + 1]):
            cp.wait_send()
            cp.wait_recv()
        for cp in onward(refs[:n], refs[n + 3], refs[n + 4]):
            cp.start()
        refs[-1][...] = jnp.zeros_like(refs[-1])

    outs = pl.pallas_call(
        body,
        name=name,
        in_specs=[HBM] * n + [SEM, SEM, ANY],
        out_specs=[SEM, SEM] + [HBM] * n + [pl.BlockSpec(memory_space=pltpu.VMEM)],
        out_shape=[pltpu.SemaphoreType.DMA((n_sem,)), pltpu.SemaphoreType.DMA((n_sem,))] + [pltpu.HBM(b.shape, b.dtype) for b in bufs]
        + [jax.ShapeDtypeStruct((SUBLANES, LANES), F32)],
        input_output_aliases={i: 2 + i for i in range(n)},
        compiler_params=pltpu.CompilerParams(has_side_effects=EFFECT),
    )(*bufs, send_sems, recv_sems, after)
    return outs[0], outs[1], list(outs[2:-1]), outs[-1]


def _gather_half_copies(first, count, over_ici):
    def copies(refs, send_sems, recv_sems):
        x, y, c, chips = _place()
        out = []
        for i in range(count):
            h = refs[i].shape[1] // 2
            mine = pl.ds(c * h, h)
            for j, (px, py) in enumerate(chips):
                k = 3 * (first + i) + j
                slot = 2 * x + y if over_ici else 2 * px + py
                to = (px, py, c) if over_ici else (x, y, 1 - c)
                out.append(_rcopy(refs[i].at[slot, mine], refs[i].at[slot, mine], send_sems.at[k], recv_sems.at[k], to))
        return out

    return copies


def _gather_copies(first, count):
    def copies(refs, send_sems, recv_sems):
        x, y, c, chips = _place()
        me = 2 * x + y
        out = []
        for i in range(count):
            for j, (px, py) in enumerate(chips):
                k = 3 * (first + i) + j
                out.append(_rcopy(refs[i].at[me], refs[i].at[me], send_sems.at[k], recv_sems.at[k], (px, py, c)))
        return out

    return copies


def _swap_copies(n):
    def copies(refs, send_sems, recv_sems):
        x, y, c, _ = _place()
        out = []
        for i in range(n):
            h = refs[i].shape[1] // 2
            out.append(_rcopy(refs[i].at[:, pl.ds((1 - c) * h, h)], refs[n + i], send_sems.at[i], recv_sems.at[i], (x, y, 1 - c)))
        return out

    return copies


def _partial_copies(n):
    def copies(refs, send_sems, recv_sems):
        x, y, c, chips = _place()
        out = []
        for i in range(n):
            for j, (px, py) in enumerate(chips):
                out.append(_rcopy(refs[i].at[2 * px + py], refs[n + i].at[j], send_sems.at[3 * i + j], recv_sems.at[3 * i + j], (px, py, c)))
        return out

    return copies


def _share_copies(n):
    def copies(refs, send_sems, recv_sems):
        x, y, c, _ = _place()
        return [_rcopy(refs[i].at[c], refs[i].at[c], send_sems.at[i], recv_sems.at[i], (x, y, 1 - c)) for i in range(n)]

    return copies


def _allreduce_small(groups, widths):
    flat = [a for g in groups for a in g]
    n = len(flat)
    rows = -(-sum(a.shape[0] for a in flat) // SUBLANES) * SUBLANES
    cols = max(a.shape[1] for a in flat)
    out_shapes = [(sum(a.shape[0] for a in g), wd or g[0].shape[1]) for g, wd in zip(groups, widths)]

    def body(*refs):
        ins, outs = refs[:n], refs[n:n + len(groups)]
        mine, buf, send_sems, recv_sems = refs[n + len(groups):]
        x, y, c, _ = _place()
        me = 4 * x + 2 * y + c
        mine[...] = jnp.zeros_like(mine)
        r0 = 0
        for a_ref in ins:
            r, w = a_ref.shape
            mine[r0:r0 + r, 0:w] = a_ref[...]
            r0 += r
        buf[me] = mine[...]
        copies = []
        for k in range(1, N_DEV):
            peer = (x ^ (k >> 2), y ^ ((k >> 1) & 1), c ^ (k & 1))
            cp = _rcopy(mine, buf.at[me], send_sems.at[k - 1], recv_sems.at[k - 1], peer)
            cp.start()
            copies.append(cp)
        for cp in copies:
            cp.wait()
        acc = buf[0]
        for d in range(1, N_DEV):
            acc = acc + buf[d]
        mine[...] = acc
        r0 = 0
        for o_ref in outs:
            r, w = o_ref.shape
            o_ref[...] = mine[r0:r0 + r, 0:w]
            r0 += r

    vmem = pl.BlockSpec(memory_space=pltpu.VMEM)
    return pl.pallas_call(
        body,
        name="allreduce_small",
        in_specs=[vmem] * n,
        out_specs=[vmem] * len(groups),
        out_shape=[jax.ShapeDtypeStruct(s, F32) for s in out_shapes],
        scratch_shapes=[pltpu.VMEM((rows, cols), F32), pltpu.VMEM((N_DEV, rows, cols), F32),
                        pltpu.SemaphoreType.DMA((N_DEV - 1,)), pltpu.SemaphoreType.DMA((N_DEV - 1,))],
        compiler_params=pltpu.CompilerParams(vmem_limit_bytes=VMEM_LIMIT_BYTES),
    )(*flat)


def _adamw_small(items):
    n = len(items)

    def body(*refs):
        for i in range(n):
            w_ref, m_ref, v_ref, g_ref = refs[4 * i:4 * i + 4]
            d_ref, nm_ref, nv_ref = refs[4 * n + 3 * i:4 * n + 3 * i + 3]
            d_ref[...], nm_ref[...], nv_ref[...] = _adamw_math(w_ref[...], m_ref[...], v_ref[...], g_ref[...])

    vmem = pl.BlockSpec(memory_space=pltpu.VMEM)
    outs = pl.pallas_call(
        body,
        name="adamw_small",
        in_specs=[vmem] * (4 * n),
        out_specs=[vmem] * (3 * n),
        out_shape=[jax.ShapeDtypeStruct(it[0].shape, F32) for it in items for _ in range(3)],
        compiler_params=pltpu.CompilerParams(vmem_limit_bytes=VMEM_LIMIT_BYTES),
    )(*[a for it in items for a in it])
    return [tuple(outs[3 * i:3 * i + 3]) for i in range(n)]


class _Reduction:
    def __init__(self, tag, grads, place):
        self.tag, self.n, self.place = tag, len(grads), place
        lands = [lax.empty((N_CHIPS, g.shape[1] // 2, g.shape[2]), F32) for g in grads]
        self._start("swap", list(grads) + lands, self.n, _swap_copies(self.n))

    def _start(self, stage, bufs, n_sem, copies):
        *self.flight, self.token = _start_copies(f"rs_{stage}_start_{self.tag}", bufs, n_sem, copies)

    def _landed(self, stage, after, copies):
        send_sems, recv_sems, bufs = self.flight
        return _wait_copies(f"rs_{stage}_wait_{self.tag}", bufs, send_sems, recv_sems, after, copies)

    def to_chips(self, after):
        n = self.n
        bufs = self._landed("swap", after, _swap_copies(n))
        sums = [_add_core_halves(g, o, self.place, name=f"rs_add_core_{self.tag}_{i}") for i, (g, o) in enumerate(zip(bufs[:n], bufs[n:]))]
        self.mine = [f for f, _ in sums]
        parts = [b for _, b in sums]
        lands = [lax.empty((3,) + p.shape[1:], BF16) for p in parts]
        self._start("send", parts + lands, 3 * n, _partial_copies(n))

    def to_core(self, after):
        n = self.n
        bufs = self._landed("send", after, _partial_copies(n))
        halves = [_add_chip_partials(f, o, self.place, name=f"rs_add_chip_{self.tag}_{i}") for i, (f, o) in enumerate(zip(self.mine, bufs[n:]))]
        self._start("share", halves, n, _share_copies(n))

    def finish(self, after):
        return [b.reshape((-1,) + b.shape[2:]) for b in self._landed("share", after, _share_copies(self.n))]


ELEM_ROWS = (256, 176, 128, 64, 32, 16, 8)


def _add_core_halves(grad, got, place, name):
    s, r, cols = grad.shape
    h = r // 2
    tr = _pick(h, ELEM_ROWS)

    def body(place_ref, g_ref, o_ref, f_ref, b_ref):
        acc = g_ref[...] + o_ref[...]
        b_ref[...] = acc.astype(BF16)

        @pl.when(pl.program_id(1) == place_ref[0])
        def _():
            f_ref[...] = acc

    blk = pl.BlockSpec((None, tr, cols), lambda i, k, place_ref: (k, i, 0))
    return pl.pallas_call(
        body,
        name=name,
        grid_spec=pltpu.PrefetchScalarGridSpec(
            num_scalar_prefetch=1,
            grid=(h // tr, s),
            in_specs=[pl.BlockSpec((None, None, tr, cols), lambda i, k, place_ref: (k, place_ref[1], i, 0)), blk],
            out_specs=[pl.BlockSpec((tr, cols), lambda i, k, place_ref: (i, 0)), blk],
        ),
        out_shape=[jax.ShapeDtypeStruct((h, cols), F32), jax.ShapeDtypeStruct((s, h, cols), BF16)],
        compiler_params=_cparams(("parallel", "arbitrary")),
    )(place, grad.reshape(s, 2, h, cols), got)


def _add_chip_partials(mine, got, place, name):
    h, cols = mine.shape
    tr = _pick(h, ELEM_ROWS)

    def body(place_ref, m_ref, g_ref, o_ref):
        acc = m_ref[...]
        for j in range(3):
            acc = acc + g_ref[j].astype(F32)
        o_ref[...] = acc

    return pl.pallas_call(
        body,
        name=name,
        grid_spec=pltpu.PrefetchScalarGridSpec(
            num_scalar_prefetch=1,
            grid=(h // tr,),
            in_specs=[
                pl.BlockSpec((tr, cols), lambda i, place_ref: (i, 0)),
                pl.BlockSpec((3, tr, cols), lambda i, place_ref: (0, i, 0)),
            ],
            out_specs=pl.BlockSpec((None, tr, cols), lambda i, place_ref: (place_ref[1], i, 0)),
        ),
        out_shape=jax.ShapeDtypeStruct((2, h, cols), F32),
        compiler_params=_cparams(("parallel",)),
    )(place, mine, got)


def _adamw_math(w, m, v, g):
    nm = ADAM_B1 * m + (1.0 - ADAM_B1) * g
    nv = ADAM_B2 * v + (1.0 - ADAM_B2) * (g * g)
    m_hat = nm * (1.0 / (1.0 - ADAM_B1 ** ADAM_STEP))
    v_hat = nv * (1.0 / (1.0 - ADAM_B2 ** ADAM_STEP))
    return -ADAM_LR * (m_hat / (jnp.sqrt(v_hat) + ADAM_EPS) + ADAM_WD * w), nm, nv


def _adamw_layer(w, m, v, g, layer, prev, name):
    nl, r, cols = w.shape
    tr = _pick(r, ELEM_ROWS)

    def body(w_ref, m_ref, v_ref, g_ref, *rest):
        go_ref, d_ref, nm_ref, nv_ref = rest[-4:]
        gv = g_ref[...]
        d_ref[...], nm_ref[...], nv_ref[...] = _adamw_math(w_ref[...], m_ref[...], v_ref[...], gv)
        go_ref[...] = gv

    lay = pl.BlockSpec((None, tr, cols), lambda i: (layer, i, 0))
    return pl.pallas_call(
        body,
        name=name,
        grid=(r // tr,),
        in_specs=[lay] * 3 + [pl.BlockSpec((tr, cols), lambda i: (i, 0))] + ([ANY] * 4 if prev else []),
        out_specs=[lay] * 4,
        out_shape=[jax.ShapeDtypeStruct((nl, r, cols), F32)] * 4,
        input_output_aliases={4 + k: k for k in range(4)} if prev else {},
        compiler_params=_cparams(("parallel",)),
    )(w, m, v, g, *(prev or ()))


def _adamw(w, m, v, g, name):
    r, cols = w.shape
    tr = _pick(r, ELEM_ROWS)

    def body(w_ref, m_ref, v_ref, g_ref, d_ref, nm_ref, nv_ref):
        d_ref[...], nm_ref[...], nv_ref[...] = _adamw_math(w_ref[...], m_ref[...], v_ref[...], g_ref[...])

    blk = pl.BlockSpec((tr, cols), lambda i: (i, 0))
    return pl.pallas_call(
        body,
        name=name,
        grid=(r // tr,),
        in_specs=[blk] * 4,
        out_specs=[blk] * 3,
        out_shape=[jax.ShapeDtypeStruct((r, cols), F32)] * 3,
        compiler_params=_cparams(("parallel",)),
    )(w, m, v, g)


SMALL_COLS = 384
SMALL_ROWS = 16


def _pad_rows(flat, rows, cols):
    return jnp.pad(flat, (0, rows * cols - flat.shape[0])).reshape(rows, cols)


def kernel(x, hg_norm, hg_w_in, hg_lb_logits, hg_out_norm, hg_w_out, kv_norm, w_kv, attn_norm, attn_w_q, attn_sinks, attn_w_o, ffn_norm, ffn_w_up, ffn_conv_w, ffn_conv_b, ffn_w_down, final_norm, loss_target, m_hg_norm, m_hg_w_in, m_hg_lb_logits, m_hg_out_norm, m_hg_w_out, m_kv_norm, m_w_kv, m_attn_norm, m_attn_w_q, m_attn_sinks, m_attn_w_o, m_ffn_norm, m_ffn_w_up, m_ffn_conv_w, m_ffn_conv_b, m_ffn_w_down, m_final_norm, v_hg_norm, v_hg_w_in, v_hg_lb_logits, v_hg_out_norm, v_hg_w_out, v_kv_norm, v_w_kv, v_attn_norm, v_attn_w_q, v_attn_sinks, v_attn_w_o, v_ffn_norm, v_ffn_w_up, v_ffn_conv_w, v_ffn_conv_b, v_ffn_w_down, v_final_norm):
    wts = dict(hg_norm=hg_norm, hg_w_in=hg_w_in, hg_lb_logits=hg_lb_logits, hg_out_norm=hg_out_norm, hg_w_out=hg_w_out, kv_norm=kv_norm, w_kv=w_kv, attn_norm=attn_norm, attn_w_q=attn_w_q, attn_sinks=attn_sinks, attn_w_o=attn_w_o, ffn_norm=ffn_norm, ffn_w_up=ffn_w_up, ffn_conv_w=ffn_conv_w, ffn_conv_b=ffn_conv_b, ffn_w_down=ffn_w_down, final_norm=final_norm)
    mom1 = dict(hg_norm=m_hg_norm, hg_w_in=m_hg_w_in, hg_lb_logits=m_hg_lb_logits, hg_out_norm=m_hg_out_norm, hg_w_out=m_hg_w_out, kv_norm=m_kv_norm, w_kv=m_w_kv, attn_norm=m_attn_norm, attn_w_q=m_attn_w_q, attn_sinks=m_attn_sinks, attn_w_o=m_attn_w_o, ffn_norm=m_ffn_norm, ffn_w_up=m_ffn_w_up, ffn_conv_w=m_ffn_conv_w, ffn_conv_b=m_ffn_conv_b, ffn_w_down=m_ffn_w_down, final_norm=m_final_norm)
    mom2 = dict(hg_norm=v_hg_norm, hg_w_in=v_hg_w_in, hg_lb_logits=v_hg_lb_logits, hg_out_norm=v_hg_out_norm, hg_w_out=v_hg_w_out, kv_norm=v_kv_norm, w_kv=v_w_kv, attn_norm=v_attn_norm, attn_w_q=v_attn_w_q, attn_sinks=v_attn_sinks, attn_w_o=v_attn_w_o, ffn_norm=v_ffn_norm, ffn_w_up=v_ffn_w_up, ffn_conv_w=v_ffn_conv_w, ffn_conv_b=v_ffn_conv_b, ffn_w_down=v_ffn_w_down, final_norm=v_final_norm)
    names = list(wts)
    chip = 2 * lax.axis_index("x") + lax.axis_index("y")
    core = lax.axis_index("c")
    fs = D_FF // N_CHIPS
    ds = D_MODEL // N_CHIPS

    place_arr = jnp.stack([chip, core]).astype(jnp.int32)
    small = jnp.concatenate([hg_norm.reshape(-1), hg_lb_logits.reshape(-1), ffn_conv_w.reshape(-1)])
    n_small = small.shape[0]
    shards = [
        ("small", _pad_rows(small, SMALL_ROWS, SMALL_COLS), F32), ("hg_w_in", hg_w_in[0], BF16),
        ("hg_w_out", hg_w_out[0], BF16), ("ffn_w_up0", ffn_w_up[0], BF16), ("ffn_w_down0", ffn_w_down[0], BF16),
        ("w_kv", w_kv, BF16), ("attn_w_q", attn_w_q[0], BF16), ("attn_w_o", attn_w_o[0], BF16),
        ("ffn_w_up1", ffn_w_up[1], BF16), ("ffn_w_down1", ffn_w_down[1], BF16),
    ]
    n_first = 3
    spans = dict(layer0=(0, 2), layer1=(2, 7))

    def first_copies(refs, send_sems, recv_sems):
        return (_gather_copies(0, 1)(refs[:1], send_sems, recv_sems) + _gather_half_copies(1, 1, True)(refs[1:2], send_sems, recv_sems)
                + _gather_copies(2, 1)(refs[2:3], send_sems, recv_sems))

    placed = [_place_shard(s, place_arr, dt, name=f"place_{nm}") for nm, s, dt in shards[:n_first]]
    first = _start_copies("gather_start_first", placed, 3 * n_first, first_copies)
    placed = [_place_shard(s, place_arr, dt, name=f"place_{nm}", deps=(first[3],)) for nm, s, dt in shards[n_first:]]
    rest = _start_copies("gather_start_rest", placed, 3 * len(placed), _gather_half_copies(0, len(placed), True))
    relayed = {}

    def fetch(w, stage, after):
        if stage == "first":
            w_in = _relay_copies("gather_first_relay", first[2][1:2], first[0], first[1], after,
                                 _gather_half_copies(1, 1, True), 3, _gather_half_copies(0, 1, False))
            got = _wait_copies("gather_wait_small", first[2][:1], first[0], first[1], w_in[3], _gather_copies(0, 1))
            got += _wait_copies("gather_wait_first", w_in[2], w_in[0], w_in[1], got[0], _gather_half_copies(0, 1, False))
        elif stage == "mixer_out":
            got = _wait_copies("gather_wait_mixer_out", first[2][2:], first[0], first[1], after, _gather_copies(2, 1))
        elif stage.endswith("_relay"):
            lo, hi = spans[stage[:-6]]
            relayed[stage[:-6]] = _relay_copies(
                f"gather_{stage}", rest[2][lo:hi], rest[0], rest[1], after,
                _gather_half_copies(lo, hi - lo, True), 3 * (hi - lo), _gather_half_copies(0, hi - lo, False))
            return w
        else:
            lo, hi = spans[stage]
            send_sems, recv_sems, bufs, _ = relayed[stage]
            got = _wait_copies(f"gather_wait_{stage}", bufs, send_sems, recv_sems, after, _gather_half_copies(0, hi - lo, False))
        w = dict(w)
        if stage == "first":
            g_small = got[0].reshape(N_CHIPS, -1)[:, :n_small]
            conv_w = g_small[:, 3 * ds:].reshape(N_CHIPS, 2, 3, fs).transpose(1, 2, 0, 3).reshape(2, 3, D_FF)
            w.update(
                hg_norm=g_small[:, :ds].reshape(1, D_MODEL),
                hg_lb=g_small[:, ds:3 * ds].reshape(N_CHIPS, 2, ds).transpose(1, 0, 2).reshape(2, D_MODEL),
                ffn_conv_w=[conv_w[0], conv_w[1]], hg_w_in=got[1],
            )
        elif stage == "mixer_out":
            w.update(hg_w_out=got[0].reshape(1, D_MODEL, D_MODEL))
        elif stage == "layer0":
            w.update(ffn_w_up=[got[0], None], ffn_w_down=[got[1].reshape(1, D_FF, D_MODEL), None])
        else:
            w.update(
                w_kv=got[0].reshape(1, D_MODEL, 2 * LANES), attn_w_q=got[1].reshape(1, D_MODEL, D_MODEL),
                attn_w_o=got[2].reshape(1, D_MODEL, D_MODEL), ffn_w_up=[w["ffn_w_up"][0], got[3]],
                ffn_w_down=[w["ffn_w_down"][0], got[4].reshape(1, D_FF, D_MODEL)],
            )
        return w

    whole = dict(
        hg_out_norm=hg_out_norm, kv_norm=kv_norm.reshape(1, D_MODEL), attn_norm=attn_norm, attn_sinks=attn_sinks.reshape(ATT_QH),
        ffn_norm=[ffn_norm[0:1], ffn_norm[1:2]], ffn_conv_b=[ffn_conv_b[0:1], ffn_conv_b[1:2]], final_norm=final_norm.reshape(1, D_MODEL),
    )
    whole = fetch(whole, "first", rest[3])

    red, layer1 = {}, {}

    def by_rows(g, rows):
        return g.reshape(N_CHIPS, rows, g.shape[2])

    def hook(point, dh, grads):
        if point == "ffn1":
            red["ffn1"] = _Reduction("ffn1", [by_rows(grads["ffn_w_down"], fs), grads["ffn_w_up"]], place_arr)
            return (red["ffn1"].token,)
        if point == "attn":
            red["ffn1"].to_chips(dh)
            layer1.update(grads)
            return (red["ffn1"].token,)
        if point == "ffn0":
            group = [by_rows(layer1["attn_w_o"], ds), by_rows(layer1["attn_w_q"], ds), by_rows(layer1["w_kv"], ds),
                     by_rows(grads["ffn_w_down"], fs), grads["ffn_w_up"]]
            red["mid"] = _Reduction("mid", group, place_arr)
            return (red["mid"].token,)
        if point == "hgrn":
            red["ffn1"].to_core(dh)
            red["mid"].to_chips(dh)
            return (red["ffn1"].token, red["mid"].token)
        red["hg"] = _Reduction("hg", [by_rows(grads["hg_w_out"], ds), grads["hg_w_in"]], place_arr)
        return (red["hg"].token,)

    loss, dx, grads = _local_step(x[0], loss_target[0], whole, fetch, hook)

    small_names = ["hg_out_norm", "attn_sinks", "kv_norm", "attn_norm", "ffn_norm", "ffn_conv_b", "final_norm", "hg_norm", "hg_lb_logits", "ffn_conv_w"]
    groups = [[loss]] + [grads[n] if isinstance(grads[n], list) else [grads[n]] for n in small_names[:-2]] + [[grads["hg_lb"]], grads["ffn_conv_w"]]
    summed = _allreduce_small(groups, [None, None, ATT_QH] + [None] * 8)
    red["hg"].to_chips(summed[1])
    loss_out = summed[0][0, 0]
    small_grads = dict(zip(small_names, summed[1:]))
    small_grads["hg_norm"] = lax.dynamic_slice(small_grads["hg_norm"], (0, chip * ds), (1, ds))
    small_grads["hg_lb_logits"] = lax.dynamic_slice(small_grads["hg_lb_logits"], (0, chip * ds), (2, ds))
    small_grads["ffn_conv_w"] = lax.dynamic_slice(small_grads["ffn_conv_w"], (0, chip * fs), (2 * 3, fs))

    out_g, out_d, out_m, out_v = {}, {}, {}, {}

    def update(name, g2):
        shape = wts[name].shape
        d2, m2, v2 = _adamw(wts[name].reshape(g2.shape), mom1[name].reshape(g2.shape), mom2[name].reshape(g2.shape), g2, name=f"adamw_{name}")
        out_g[name], out_d[name], out_m[name], out_v[name] = g2.reshape(shape), d2.reshape(shape), m2.reshape(shape), v2.reshape(shape)
        return d2

    def update_layer(name, g2, layer, prev):
        res = _adamw_layer(wts[name], mom1[name], mom2[name], g2, layer, prev, name=f"adamw_{name}{layer}")
        out_g[name], out_d[name], out_m[name], out_v[name] = res
        return res

    g_down1, g_up1 = red["ffn1"].finish(red["hg"].token)
    down1 = update_layer("ffn_w_down", g_down1, 1, None)
    up1 = update_layer("ffn_w_up", g_up1, 1, None)
    red["mid"].to_core(up1[1])
    g_o, g_q, g_kv, g_down0, g_up0 = red["mid"].finish(up1[2])
    update("attn_w_o", g_o)
    update("attn_w_q", g_q)
    update("w_kv", g_kv)
    update_layer("ffn_w_down", g_down0, 0, down1)
    last = update_layer("ffn_w_up", g_up0, 0, up1)
    red["hg"].to_core(last[1])
    g_out, g_in = red["hg"].finish(last[2])
    update("hg_w_out", g_out)
    update("hg_w_in", g_in)

    as_2d = lambda a, n: a.reshape(small_grads[n].shape)
    updated = _adamw_small([(as_2d(wts[n], n), as_2d(mom1[n], n), as_2d(mom2[n], n), small_grads[n]) for n in small_names])
    for n, (d2, m2, v2) in zip(small_names, updated):
        shape = wts[n].shape
        out_g[n], out_d[n], out_m[n], out_v[n] = small_grads[n].reshape(shape), d2.reshape(shape), m2.reshape(shape), v2.reshape(shape)

    grad_x = dx.reshape(x.shape)
    return (loss_out, grad_x, *[out_g[n] for n in names], *[out_d[n] for n in names], *[out_m[n] for n in names], *[out_v[n] for n in names])
```

```python
import functools

import jax
import jax.numpy as jnp
from jax import lax
from jax.experimental import pallas as pl
from jax.experimental.pallas import tpu as pltpu

F32 = jnp.float32
BF16 = jnp.bfloat16
MESH = pl.DeviceIdType.MESH

EPS = 1e-6
D_MODEL = 1024
HG_HEADS = 8
HG_DK = 128
HG_CHUNK = 64
ATT_HD = 64
ATT_QH = 16
ATT_KVH = 2
ATT_GROUP = ATT_QH // ATT_KVH
WINDOW = 128
D_FF = 2816
N_CHIPS = 4
N_DEV = 8
LANES = 128
SUBLANES = 8
VMEM_LIMIT_BYTES = 56 * 1024 * 1024
NEG = -1e30
ALIBI_SLOPES = tuple(2.0 ** (-8.0 * h / ATT_QH) for h in range(1, ATT_QH + 1))

ADAM_LR = 0.001
ADAM_B1 = 0.9
ADAM_B2 = 0.999
ADAM_EPS = 1e-08
ADAM_WD = 0.01
ADAM_STEP = 10


def _cparams(sem=None):
    return pltpu.CompilerParams(dimension_semantics=sem, vmem_limit_bytes=VMEM_LIMIT_BYTES)


def _pick(n, cands):
    for c in cands:
        if n % c == 0:
            return c
    return n


def _sigmoid(x):
    return 0.5 * jnp.tanh(0.5 * x) + 0.5


def _dot(a, b, dims):
    return lax.dot_general(a, b, (dims, ((), ())), preferred_element_type=F32)


NN = ((1,), (0,))
NT = ((1,), (1,))
TN = ((0,), (0,))


MM_ROWS = 1024


def _rms_stats(xv):
    rstd = lax.rsqrt(jnp.mean(xv * xv, axis=-1, keepdims=True) + EPS)
    return xv * rstd, rstd


def _mm_operand(a_ref, gain_ref):
    if gain_ref is None:
        return a_ref[...].astype(BF16)
    return (_rms_stats(a_ref[...])[0] * gain_ref[...]).astype(BF16)


def _mm_nn(a, w, res=None, out_dtype=F32, name="mm_nn", gain=None):
    m, k = a.shape
    s, _, ns = w.shape
    tm = min(m, MM_ROWS)
    tn = _pick(ns, (1024, 1408, 512, 256, 128))
    npb = ns // tn

    def body(a_ref, w_ref, *rest):
        o_ref = rest[-1]
        acc = _dot(_mm_operand(a_ref, rest[0] if gain is not None else None), w_ref[...], NN)
        if res is not None:
            acc = acc + rest[-2][...]
        o_ref[...] = acc.astype(o_ref.dtype)

    in_specs = [
        pl.BlockSpec((tm, k), lambda i, j: (i, 0)),
        pl.BlockSpec((None, k, tn), lambda i, j: (j // npb, 0, j % npb)),
    ]
    args = [a, w]
    if gain is not None:
        in_specs.append(pl.BlockSpec((1, k), lambda i, j: (0, 0)))
        args.append(gain)
    if res is not None:
        in_specs.append(pl.BlockSpec((tm, tn), lambda i, j: (i, j)))
        args.append(res)
    return pl.pallas_call(
        body,
        name=name,
        grid=(m // tm, s * npb),
        in_specs=in_specs,
        out_specs=pl.BlockSpec((tm, tn), lambda i, j: (i, j)),
        out_shape=jax.ShapeDtypeStruct((m, s * ns), out_dtype),
        compiler_params=_cparams(("parallel", "parallel")),
    )(*args)


def _dy_spec(stacked, tm, tn, npb, row, kk):
    if stacked:
        return pl.BlockSpec((None, tm, tn), lambda *g: (kk(g) // npb, row(g), kk(g) % npb))
    return pl.BlockSpec((tm, tn), lambda *g: (row(g), kk(g)))


def _dep_specs(deps):
    return [pl.BlockSpec(d.shape, lambda *g: (0, 0)) for d in deps]


def _mm_nt(dy, w, stacked=False, out_dtype=F32, name="mm_nt", deps=(), norm_of=None):
    s, k, ns = w.shape
    m = dy.shape[1] if stacked else dy.shape[0]
    tm = min(m, MM_ROWS)
    tko = _pick(k, (1024, 1408, 512, 256))
    tn = _pick(ns, (1024, 1408, 512, 256))
    npb = ns // tn
    nk = s * npb
    fused = norm_of is not None
    assert not fused or tko == k

    def body(dy_ref, w_ref, *rest):
        acc_ref = rest[-1]
        i, kk = pl.program_id(0), pl.program_id(2)

        @pl.when(kk == 0)
        def _():
            acc_ref[...] = jnp.zeros_like(acc_ref)

        acc_ref[...] += _dot(dy_ref[...].astype(BF16), w_ref[...], NT)

        if not fused:
            @pl.when(kk == nk - 1)
            def _():
                rest[-2][...] = acc_ref[...].astype(rest[-2].dtype)
            return
        x_ref, g_ref, dres_ref = rest[:3]
        dx_ref, dg_ref = rest[-3], rest[-2]

        @pl.when(jnp.logical_and(i == 0, kk == 0))
        def _():
            dg_ref[...] = jnp.zeros_like(dg_ref)

        @pl.when(kk == nk - 1)
        def _():
            dxn = acc_ref[...]
            xhat, rstd = _rms_stats(x_ref[...])
            gd = dxn * g_ref[...]
            dx_ref[...] = dres_ref[...] + rstd * (gd - xhat * jnp.mean(gd * xhat, axis=-1, keepdims=True))
            dg_ref[...] += jnp.sum(dxn * xhat, axis=0, keepdims=True)

    row = pl.BlockSpec((tm, tko), lambda i, j, kk: (i, j))
    vec = pl.BlockSpec((1, k), lambda i, j, kk: (0, 0))
    return pl.pallas_call(
        body,
        name=name,
        grid=(m // tm, k // tko, nk),
        in_specs=[
            _dy_spec(stacked, tm, tn, npb, lambda g: g[0], lambda g: g[2]),
            pl.BlockSpec((None, tko, tn), lambda i, j, kk: (kk // npb, j, kk % npb)),
        ] + ([row, vec, row] if fused else []) + _dep_specs(deps),
        out_specs=[row, vec] if fused else row,
        out_shape=[jax.ShapeDtypeStruct((m, k), F32), jax.ShapeDtypeStruct((1, k), F32)] if fused else jax.ShapeDtypeStruct((m, k), out_dtype),
        scratch_shapes=[pltpu.VMEM((tm, tko), F32)],
        compiler_params=_cparams(("arbitrary",) * 3 if fused else ("parallel", "parallel", "arbitrary")),
    )(dy, w, *(norm_of or ()), *deps)


def _mm_tn(a, dy, s, ns, stacked=False, name="mm_tn", deps=(), gain=None):
    m, k = a.shape
    tm = min(m, MM_ROWS)
    tk = _pick(k, (1024, 1408, 512, 256))
    tn = _pick(ns, (1024, 1408, 512, 256, 128))
    npb = ns // tn
    nm = m // tm
    assert gain is None or tk == k

    def body(a_ref, dy_ref, *rest):
        o_ref, acc_ref = rest[-2:]
        mm = pl.program_id(2)

        @pl.when(mm == 0)
        def _():
            acc_ref[...] = jnp.zeros_like(acc_ref)

        acc_ref[...] += _dot(_mm_operand(a_ref, rest[0] if gain is not None else None), dy_ref[...].astype(BF16), TN)

        @pl.when(mm == nm - 1)
        def _():
            o_ref[...] = acc_ref[...]

    return pl.pallas_call(
        body,
        name=name,
        grid=(k // tk, s * npb, nm),
        in_specs=[
            pl.BlockSpec((tm, tk), lambda i, j, mm: (mm, i)),
            _dy_spec(stacked, tm, tn, npb, lambda g: g[2], lambda g: g[1]),
        ] + ([pl.BlockSpec((1, k), lambda i, j, mm: (0, 0))] if gain is not None else []) + _dep_specs(deps),
        out_specs=pl.BlockSpec((None, tk, tn), lambda i, j, mm: (j // npb, i, j % npb)),
        out_shape=jax.ShapeDtypeStruct((s, k, ns), F32),
        scratch_shapes=[pltpu.VMEM((tk, tn), F32)],
        compiler_params=_cparams(("parallel", "parallel", "arbitrary")),
    )(a, dy, *(() if gain is None else (gain,)), *deps)


ROW_TILE = 512


def _loss_head(h, g, target):
    t, d = h.shape
    r = min(t, ROW_TILE)

    def body(h_ref, g_ref, t_ref, dh_ref, dg_ref, loss_ref):
        @pl.when(pl.program_id(0) == 0)
        def _():
            dg_ref[...] = jnp.zeros_like(dg_ref)
            loss_ref[...] = jnp.zeros_like(loss_ref)

        xv = h_ref[...]
        rstd = lax.rsqrt(jnp.mean(xv * xv, axis=-1, keepdims=True) + EPS)
        xhat = xv * rstd
        gv = g_ref[...]
        err = xhat * gv - t_ref[...]
        loss_ref[...] += 0.5 * jnp.sum(jnp.mean(err * err, axis=-1, keepdims=True), axis=0, keepdims=True)
        dy = err * (1.0 / d)
        gd = dy * gv
        dh_ref[...] = rstd * (gd - xhat * jnp.mean(gd * xhat, axis=-1, keepdims=True))
        dg_ref[...] += jnp.sum(dy * xhat, axis=0, keepdims=True)

    return pl.pallas_call(
        body,
        name="loss_head",
        grid=(t // r,),
        in_specs=[
            pl.BlockSpec((r, d), lambda i: (i, 0)),
            pl.BlockSpec((1, d), lambda i: (0, 0)),
            pl.BlockSpec((r, d), lambda i: (i, 0)),
        ],
        out_specs=[
            pl.BlockSpec((r, d), lambda i: (i, 0)),
            pl.BlockSpec((1, d), lambda i: (0, 0)),
            pl.BlockSpec((1, LANES), lambda i: (0, 0)),
        ],
        out_shape=[
            jax.ShapeDtypeStruct((t, d), F32),
            jax.ShapeDtypeStruct((1, d), F32),
            jax.ShapeDtypeStruct((1, LANES), F32),
        ],
        compiler_params=_cparams(("arbitrary",)),
    )(h, g, target)


CONV_ROWS = 256
CONV_COLS = 1408


def _conv_taps(x_ext, n):
    tot = x_ext.shape[0]
    g1 = pltpu.roll(x_ext, 1, 0)[tot - n:]
    g2 = pltpu.roll(x_ext, 2, 0)[tot - n:]
    return g2, g1


def _conv_fwd(up, conv_w, conv_b, name="conv_fwd"):
    t = up.shape[0]
    r = min(t, CONV_ROWS)
    tc = CONV_COLS
    ncb = D_FF // tc
    hb = r // SUBLANES

    def body(g_ref, halo_ref, v_ref, w_ref, b_ref, o_ref):
        i = pl.program_id(1)
        g0 = g_ref[...]
        halo = halo_ref[...] * jnp.where(i > 0, 1.0, 0.0)
        g2, g1 = _conv_taps(jnp.concatenate([halo, g0], axis=0), r)
        c = b_ref[...] + w_ref[0:1, :] * g2 + w_ref[1:2, :] * g1 + w_ref[2:3, :] * g0
        o_ref[...] = (c * _sigmoid(c) * v_ref[...]).astype(BF16)

    return pl.pallas_call(
        body,
        name=name,
        grid=(ncb, t // r),
        in_specs=[
            pl.BlockSpec((r, tc), lambda j, i: (i, j)),
            pl.BlockSpec((SUBLANES, tc), lambda j, i: (jnp.maximum(i * hb - 1, 0), j)),
            pl.BlockSpec((r, tc), lambda j, i: (i, ncb + j)),
            pl.BlockSpec((3, tc), lambda j, i: (0, j)),
            pl.BlockSpec((1, tc), lambda j, i: (0, j)),
        ],
        out_specs=pl.BlockSpec((r, tc), lambda j, i: (i, j)),
        out_shape=jax.ShapeDtypeStruct((t, D_FF), BF16),
        compiler_params=_cparams(("parallel", "parallel")),
    )(up, up, up, conv_w, conv_b)


def _conv_bwd(up, conv_w, conv_b, dact, name="conv_bwd"):
    t = up.shape[0]
    r = min(t, CONV_ROWS)
    tc = CONV_COLS
    ncb = D_FF // tc
    hb = r // SUBLANES
    nrt = t // r

    def body(g_ref, halo_ref, v_ref, w_ref, b_ref, da_ref, dup_ref, dw_ref, db_ref, nxt_ref):
        ii = pl.program_id(1)
        i = nrt - 1 - ii

        @pl.when(ii == 0)
        def _():
            nxt_ref[...] = jnp.zeros_like(nxt_ref)
            dw_ref[...] = jnp.zeros_like(dw_ref)
            db_ref[...] = jnp.zeros_like(db_ref)

        g0 = g_ref[...]
        halo = halo_ref[...] * jnp.where(i > 0, 1.0, 0.0)
        g2, g1 = _conv_taps(jnp.concatenate([halo, g0], axis=0), r)
        w0, w1, w2 = w_ref[0:1, :], w_ref[1:2, :], w_ref[2:3, :]
        c = b_ref[...] + w0 * g2 + w1 * g1 + w2 * g0
        sg = _sigmoid(c)
        da = da_ref[...]
        dval = da * (c * sg)
        dc = da * v_ref[...] * (sg * (1.0 + c * (1.0 - sg)))
        db_ref[...] += jnp.sum(dc, axis=0, keepdims=True)
        dw_ref[0:1, :] += jnp.sum(dc * g2, axis=0, keepdims=True)
        dw_ref[1:2, :] += jnp.sum(dc * g1, axis=0, keepdims=True)
        dw_ref[2:3, :] += jnp.sum(dc * g0, axis=0, keepdims=True)
        ext = jnp.concatenate([dc, nxt_ref[...]], axis=0)
        tot = r + SUBLANES
        d1 = pltpu.roll(ext, tot - 1, 0)[:r]
        d2 = pltpu.roll(ext, tot - 2, 0)[:r]
        dgate = w2 * dc + w1 * d1 + w0 * d2
        nxt_ref[...] = dc[:SUBLANES]
        dup_ref[0] = dgate.astype(BF16)
        dup_ref[1] = dval.astype(BF16)

    rev = lambda ii: nrt - 1 - ii
    dup, dw, db = pl.pallas_call(
        body,
        name=name,
        grid=(ncb, nrt),
        in_specs=[
            pl.BlockSpec((r, tc), lambda j, ii: (rev(ii), j)),
            pl.BlockSpec((SUBLANES, tc), lambda j, ii: (jnp.maximum(rev(ii) * hb - 1, 0), j)),
            pl.BlockSpec((r, tc), lambda j, ii: (rev(ii), ncb + j)),
            pl.BlockSpec((3, tc), lambda j, ii: (0, j)),
            pl.BlockSpec((1, tc), lambda j, ii: (0, j)),
            pl.BlockSpec((r, tc), lambda j, ii: (rev(ii), j)),
        ],
        out_specs=[
            pl.BlockSpec((2, None, r, tc), lambda j, ii: (0, j, rev(ii), 0)),
            pl.BlockSpec((3, tc), lambda j, ii: (0, j)),
            pl.BlockSpec((1, tc), lambda j, ii: (0, j)),
        ],
        out_shape=[
            jax.ShapeDtypeStruct((2, ncb, t, tc), BF16),
            jax.ShapeDtypeStruct((3, D_FF), F32),
            jax.ShapeDtypeStruct((1, D_FF), F32),
        ],
        scratch_shapes=[pltpu.VMEM((SUBLANES, tc), F32)],
        compiler_params=_cparams(("parallel", "arbitrary")),
    )(up, up, up, conv_w, conv_b, dact)
    return dup.reshape(2 * ncb, t, tc), dw, db


def _split3(x):
    x1 = x.astype(BF16)
    r1 = x - x1.astype(F32)
    x2 = r1.astype(BF16)
    x3 = (r1 - x2.astype(F32)).astype(BF16)
    return x1, x2, x3


def _tri_dot(tri, x, dims):
    x1, x2, x3 = _split3(x)
    return _dot(tri, x1, dims) + _dot(tri, x2, dims) + _dot(tri, x3, dims)


def _lower_bound(logits_ref):
    return _sigmoid(logits_ref[0:1, :] - logits_ref[1:2, :])


def _hg_gates(qr, fr, lb):
    q = qr * _sigmoid(qr) * (HG_DK ** -0.5)
    sf = _sigmoid(fr)
    fg = lb + (1.0 - lb) * sf
    return q, sf, fg


def _hg_chunk_terms(q, fg, tril_b, low_half):
    g = jnp.log(fg)
    k = 1.0 - fg
    cum = _tri_dot(tril_b, g, NN)
    c_last = jnp.sum(g, axis=0, keepdims=True)
    c_mid = jnp.sum(jnp.where(low_half, g, 0.0), axis=0, keepdims=True)
    e_q = jnp.exp(cum - c_mid)
    e_k = jnp.exp(c_mid - cum)
    e_0 = jnp.exp(cum)
    e_l = jnp.exp(c_last - cum)
    return k, e_q, e_k, e_0, e_l, jnp.exp(c_last)


HG_BLOCK = 256


def _hg_proj_specs(rb, row):
    return [pl.BlockSpec((rb, D_MODEL), functools.partial(lambda i, k: (row(i), k), k=k)) for k in range(4)]


def _hg_consts(c):
    tril = lax.broadcasted_iota(jnp.int32, (c, c), 0) >= lax.broadcasted_iota(jnp.int32, (c, c), 1)
    low_half = lax.broadcasted_iota(jnp.int32, (c, D_MODEL), 0) < c // 2
    return tril, tril.astype(BF16), low_half


def _hgrn_fwd(proj, lb, wn):
    t = proj.shape[0]
    c = HG_CHUNK
    rb = min(t, HG_BLOCK)
    cpb = rb // c

    def body(q_ref, f_ref, i_ref, g_ref, lb_ref, wn_ref, o_ref, y_ref, st_ref, s_scr):
        @pl.when(pl.program_id(0) == 0)
        def _():
            s_scr[...] = jnp.zeros_like(s_scr)

        lb_all = _lower_bound(lb_ref)
        wnv = wn_ref[...]
        tril, tril_b, low_half = _hg_consts(c)

        def chunk(n, carry):
            rows = pl.ds(pl.multiple_of(n * c, c), c)
            q, _, fg = _hg_gates(q_ref[rows, :], f_ref[rows, :], lb_all)
            k, e_q, e_k, e_0, e_l, e_last = _hg_chunk_terms(q, fg, tril_b, low_half)
            qi, ki, q0, kl = (q * e_q).astype(BF16), (k * e_k).astype(BF16), (q * e_0).astype(BF16), (k * e_l).astype(BF16)
            v = i_ref[rows, :].astype(BF16)
            gr = g_ref[rows, :]
            gate = gr * _sigmoid(gr)
            for h in range(HG_HEADS):
                cols = slice(h * HG_DK, (h + 1) * HG_DK)
                st = s_scr[h]
                st_ref[h, n] = st
                a = jnp.where(tril, _dot(qi[:, cols], ki[:, cols], NT), 0.0)
                o = _dot(q0[:, cols], st.astype(BF16), NT) + _dot(a.astype(BF16), v[:, cols], NN)
                s_scr[h] = st * e_last[:, cols] + _dot(v[:, cols], kl[:, cols], TN)
                o_ref[rows, cols] = o
                rstd = lax.rsqrt(jnp.mean(o * o, axis=-1, keepdims=True) + EPS)
                y_ref[rows, cols] = (o * rstd * wnv * gate[:, cols]).astype(BF16)
            return carry

        lax.fori_loop(0, cpb, chunk, 0)

    blk = pl.BlockSpec((rb, D_MODEL), lambda i: (i, 0))
    return pl.pallas_call(
        body,
        name="hgrn_fwd",
        grid=(t // rb,),
        in_specs=_hg_proj_specs(rb, lambda i: i) + [pl.BlockSpec((2, D_MODEL), lambda i: (0, 0)), pl.BlockSpec((1, HG_DK), lambda i: (0, 0))],
        out_specs=[blk, blk, pl.BlockSpec((HG_HEADS, cpb, HG_DK, HG_DK), lambda i: (0, i, 0, 0))],
        out_shape=[
            jax.ShapeDtypeStruct((t, D_MODEL), F32),
            jax.ShapeDtypeStruct((t, D_MODEL), BF16),
            jax.ShapeDtypeStruct((HG_HEADS, t // c, HG_DK, HG_DK), F32),
        ],
        scratch_shapes=[pltpu.VMEM((HG_HEADS, HG_DK, HG_DK), F32)],
        compiler_params=_cparams(("arbitrary",)),
    )(proj, proj, proj, proj, lb, wn)


def _hgrn_bwd(proj, lb, wn, o, states, dy):
    t = proj.shape[0]
    c = HG_CHUNK
    rb = min(t, HG_BLOCK)
    cpb = rb // c
    nb = t // rb

    def body(q_ref, f_ref, i_ref, g_ref, lb_ref, wn_ref, o_ref, st_ref, dy_ref, dp_ref, dl_ref, dwn_ref, ds_scr, dlb_scr):
        step = pl.program_id(0)

        @pl.when(step == 0)
        def _():
            dwn_ref[...] = jnp.zeros_like(dwn_ref)
            ds_scr[...] = jnp.zeros_like(ds_scr)
            dlb_scr[...] = jnp.zeros_like(dlb_scr)

        lb_all = _lower_bound(lb_ref)
        wnv = wn_ref[...]
        tril, tril_b, low_half = _hg_consts(c)

        def chunk(nn, carry):
            n = cpb - 1 - nn
            rows = pl.ds(pl.multiple_of(n * c, c), c)
            qr = q_ref[rows, :]
            gr = g_ref[rows, :]
            q, sf, fg = _hg_gates(qr, f_ref[rows, :], lb_all)
            k, e_q, e_k, e_0, e_l, e_last = _hg_chunk_terms(q, fg, tril_b, low_half)
            qi, qi_lo, _ = _split3(q * e_q)
            ki, ki_lo, _ = _split3(k * e_k)
            q0 = (q * e_0).astype(BF16)
            kl = (k * e_l).astype(BF16)
            v = i_ref[rows, :].astype(BF16)
            sg = _sigmoid(gr)
            silu_g = gr * sg
            dsilu_g = sg * (1.0 + gr * (1.0 - sg))
            dqs, dks, d_lasts = [], [], []
            for h in range(HG_HEADS):
                cols = slice(h * HG_DK, (h + 1) * HG_DK)
                ov = o_ref[rows, cols]
                dyv = dy_ref[rows, cols].astype(F32)
                rstd = lax.rsqrt(jnp.mean(ov * ov, axis=-1, keepdims=True) + EPS)
                ohat = ov * rstd
                dp_ref[3, rows, cols] = (dyv * (ohat * wnv) * dsilu_g[:, cols]).astype(BF16)
                don = dyv * silu_g[:, cols]
                dwn_ref[...] += jnp.sum(don * ohat, axis=0, keepdims=True)
                gd = don * wnv
                do_b = (rstd * (gd - ohat * jnp.mean(gd * ohat, axis=-1, keepdims=True))).astype(BF16)
                st = st_ref[h, n]
                ds = ds_scr[h]
                ds_b = ds.astype(BF16)
                vh, kh = v[:, cols], k[:, cols]
                a_b = jnp.where(tril, _dot(qi[:, cols], ki[:, cols], NT), 0.0).astype(BF16)
                da_b = jnp.where(tril, _dot(do_b, vh, NT), 0.0).astype(BF16)
                dqs.append(_dot(do_b, st.astype(BF16), NN) * e_0[:, cols]
                           + (_dot(da_b, ki[:, cols], NN) + _dot(da_b, ki_lo[:, cols], NN)) * e_q[:, cols])
                dk_state = _dot(vh, ds_b, NN) * e_l[:, cols]
                dks.append((_dot(da_b, qi[:, cols], TN) + _dot(da_b, qi_lo[:, cols], TN)) * e_k[:, cols] + dk_state)
                dp_ref[2, rows, cols] = (_dot(a_b, do_b, TN) + _dot(kl[:, cols], ds_b, NT)).astype(BF16)
                ds_scr[h] = ds * e_last[:, cols] + _dot(do_b, q0[:, cols], TN)
                d_lasts.append(jnp.sum(dk_state * kh, axis=0, keepdims=True) + jnp.sum(ds * st, axis=0, keepdims=True) * e_last[:, cols])
            dq = jnp.concatenate(dqs, axis=1)
            dk = jnp.concatenate(dks, axis=1)
            dlogf = _tri_dot(tril_b, q * dq - k * dk, TN) + jnp.concatenate(d_lasts, axis=1)
            dfg = dlogf / fg - dk
            dlb_scr[...] += jnp.sum(dfg * (1.0 - sf), axis=0, keepdims=True)
            sq = _sigmoid(qr)
            dp_ref[0, rows, :] = (dq * (HG_DK ** -0.5) * (sq * (1.0 + qr * (1.0 - sq)))).astype(BF16)
            dp_ref[1, rows, :] = (dfg * (1.0 - lb_all) * sf * (1.0 - sf)).astype(BF16)
            return carry

        lax.fori_loop(0, cpb, chunk, 0)

        @pl.when(step == nb - 1)
        def _():
            d0 = dlb_scr[...] * lb_all * (1.0 - lb_all)
            dl_ref[0:1, :] = d0
            dl_ref[1:2, :] = -d0

    rev = lambda i: nb - 1 - i
    blk = pl.BlockSpec((rb, D_MODEL), lambda i: (rev(i), 0))
    return pl.pallas_call(
        body,
        name="hgrn_bwd",
        grid=(nb,),
        in_specs=_hg_proj_specs(rb, rev)
        + [pl.BlockSpec((2, D_MODEL), lambda i: (0, 0)), pl.BlockSpec((1, HG_DK), lambda i: (0, 0)), blk,
           pl.BlockSpec((HG_HEADS, cpb, HG_DK, HG_DK), lambda i: (0, rev(i), 0, 0)), blk],
        out_specs=[
            pl.BlockSpec((4, rb, D_MODEL), lambda i: (0, rev(i), 0)),
            pl.BlockSpec((2, D_MODEL), lambda i: (0, 0)),
            pl.BlockSpec((1, HG_DK), lambda i: (0, 0)),
        ],
        out_shape=[
            jax.ShapeDtypeStruct((4, t, D_MODEL), BF16),
            jax.ShapeDtypeStruct((2, D_MODEL), F32),
            jax.ShapeDtypeStruct((1, HG_DK), F32),
        ],
        scratch_shapes=[pltpu.VMEM((HG_HEADS, HG_DK, HG_DK), F32), pltpu.VMEM((1, D_MODEL), F32)],
        compiler_params=_cparams(("arbitrary",)),
    )(proj, proj, proj, proj, lb, wn, o, states, dy)


ATT_STACK = 8


def _att_stack(q_ref, sink_ref, first, lo, bias_p, bias_c, extra_ref=None):
    qs, bps, bcs, sinks, extras = [], [], [], None, []
    rows = lax.broadcasted_iota(jnp.int32, (ATT_STACK * WINDOW, 1), 0)
    for i in range(ATT_STACK):
        hq = first + i
        cols = slice((hq // 2) * LANES, (hq // 2 + 1) * LANES)
        sel = lo if hq % 2 == 0 else jnp.logical_not(lo)
        qp = q_ref[:, cols] * (ATT_HD ** -0.5)
        qs.append(jnp.where(sel, qp, jnp.zeros_like(qp)))
        bps.append(ALIBI_SLOPES[hq] * bias_p)
        bcs.append(ALIBI_SLOPES[hq] * bias_c)
        sinks = sink_ref[hq] if sinks is None else jnp.where(rows < i * WINDOW, sinks, sink_ref[hq])
        if extra_ref is not None:
            ep = extra_ref[:, cols]
            extras.append(jnp.where(sel, ep, jnp.zeros_like(ep)))
    cat = lambda parts: jnp.concatenate(parts, axis=0)
    return cat(qs), cat(bps), cat(bcs), sinks, (cat(extras) if extras else None)


def _att_rows(i):
    return slice(i * WINDOW, (i + 1) * WINDOW)


def _att_bias(n):
    tq = lax.broadcasted_iota(jnp.int32, (WINDOW, WINDOW), 0)
    sk = lax.broadcasted_iota(jnp.int32, (WINDOW, WINDOW), 1)
    valid_c = sk <= tq
    valid_p = (sk - tq) > jnp.where(n > 0, 0, WINDOW)
    dist_c = (tq - sk).astype(F32)
    return jnp.where(valid_p, -dist_c - float(WINDOW), NEG), jnp.where(valid_c, -dist_c, NEG)


def _att_halves(x, lo, kh):
    r = pltpu.roll(x, ATT_HD, 1)
    zero = jnp.zeros_like(x)
    if kh == 0:
        return jnp.where(lo, x, r), jnp.where(lo, x, zero), jnp.where(lo, zero, r)
    return jnp.where(lo, r, x), jnp.where(lo, r, zero), jnp.where(lo, zero, x)


def _att_probs(qm, k2p, k2c, bias_p, bias_c, sink):
    sp = _dot(qm, k2p, NT) + bias_p
    sc = _dot(qm, k2c, NT) + bias_c
    m = jnp.maximum(jnp.maximum(jnp.max(sp, axis=-1, keepdims=True), jnp.max(sc, axis=-1, keepdims=True)), sink)
    ep = jnp.exp(sp - m)
    ec = jnp.exp(sc - m)
    es = jnp.exp(sink - m)
    inv = 1.0 / (jnp.sum(ep, axis=-1, keepdims=True) + jnp.sum(ec, axis=-1, keepdims=True) + es)
    return ep * inv, ec * inv, es * inv


def _attn_fwd(q, kv, sinks):
    t = q.shape[0]
    nb = t // WINDOW

    def body(sink_ref, q_ref, kvp_ref, kvc_ref, o_ref):
        n = pl.program_id(0)
        bias_p, bias_c = _att_bias(n)
        lo = lax.broadcasted_iota(jnp.int32, (WINDOW, LANES), 1) < ATT_HD
        for kh in range(ATT_KVH):
            k2p, _, _ = _att_halves(kvp_ref[:, 0:LANES], lo, kh)
            k2c, _, _ = _att_halves(kvc_ref[:, 0:LANES], lo, kh)
            _, vlo_p, vhi_p = _att_halves(kvp_ref[:, LANES:2 * LANES], lo, kh)
            _, vlo_c, vhi_c = _att_halves(kvc_ref[:, LANES:2 * LANES], lo, kh)
            for first in range(kh * ATT_GROUP, (kh + 1) * ATT_GROUP, ATT_STACK):
                qs, bp, bc, sinks, _ = _att_stack(q_ref, sink_ref, first, lo, bias_p, bias_c)
                pp, pc, _ = _att_probs(qs, k2p, k2c, bp, bc, sinks)
                pp, pc = pp.astype(BF16), pc.astype(BF16)
                for i in range(0, ATT_STACK, 2):
                    even, odd = _att_rows(i), _att_rows(i + 1)
                    out = (_dot(pp[even], vlo_p, NN) + _dot(pc[even], vlo_c, NN)
                           + _dot(pp[odd], vhi_p, NN) + _dot(pc[odd], vhi_c, NN))
                    j = (first + i) // 2
                    o_ref[:, j * LANES:(j + 1) * LANES] = out.astype(BF16)

    return pl.pallas_call(
        body,
        name="attn_fwd",
        grid=(nb,),
        in_specs=[
            pl.BlockSpec(memory_space=pltpu.SMEM),
            pl.BlockSpec((WINDOW, D_MODEL), lambda n: (n, 0)),
            pl.BlockSpec((WINDOW, 2 * LANES), lambda n: (jnp.maximum(n - 1, 0), 0)),
            pl.BlockSpec((WINDOW, 2 * LANES), lambda n: (n, 0)),
        ],
        out_specs=pl.BlockSpec((WINDOW, D_MODEL), lambda n: (n, 0)),
        out_shape=jax.ShapeDtypeStruct((t, D_MODEL), BF16),
        compiler_params=_cparams(("parallel",)),
    )(sinks, q, kv, kv)


def _attn_bwd(q, kv, sinks, dout):
    t = q.shape[0]
    nb = t // WINDOW

    def body(sink_ref, q_ref, kvp_ref, kvc_ref, do_ref, dq_ref, dkv_ref, dsink_ref, carry_ref):
        n = pl.program_id(0)

        @pl.when(n == 0)
        def _():
            carry_ref[...] = jnp.zeros_like(carry_ref)
            dsink_ref[...] = jnp.zeros_like(dsink_ref)

        @pl.when(n == nb)
        def _():
            dkv_ref[...] = carry_ref[...].astype(BF16)

        @pl.when(n < nb)
        def _():
            bias_p, bias_c = _att_bias(n)
            lo = lax.broadcasted_iota(jnp.int32, (WINDOW, LANES), 1) < ATT_HD
            lane1 = lax.broadcasted_iota(jnp.int32, (1, LANES), 1)
            dsink = jnp.zeros((1, LANES), F32)
            halves = []
            for kh in range(ATT_KVH):
                k2p, klo_p, khi_p = _att_halves(kvp_ref[:, 0:LANES], lo, kh)
                k2c, klo_c, khi_c = _att_halves(kvc_ref[:, 0:LANES], lo, kh)
                v2p, _, _ = _att_halves(kvp_ref[:, LANES:2 * LANES], lo, kh)
                v2c, _, _ = _att_halves(kvc_ref[:, LANES:2 * LANES], lo, kh)
                acc = [jnp.zeros((WINDOW, LANES), F32) for _ in range(4)]
                for first in range(kh * ATT_GROUP, (kh + 1) * ATT_GROUP, ATT_STACK):
                    qs, bp, bc, sinks, dos = _att_stack(q_ref, sink_ref, first, lo, bias_p, bias_c, do_ref)
                    pp, pc, ps = _att_probs(qs, k2p, k2c, bp, bc, sinks)
                    dpp = _dot(dos, v2p, NT)
                    dpc = _dot(dos, v2c, NT)
                    delta = jnp.sum(pp * dpp, axis=-1, keepdims=True) + jnp.sum(pc * dpc, axis=-1, keepdims=True)
                    dsp = (pp * (dpp - delta)).astype(BF16)
                    dsc = (pc * (dpc - delta)).astype(BF16)
                    sink_term = ps * delta
                    for i in range(ATT_STACK):
                        dsink = dsink + jnp.where(lane1 == first + i, -jnp.sum(sink_term[_att_rows(i)], axis=0, keepdims=True), 0.0)
                    for i in range(0, ATT_STACK, 2):
                        even, odd = _att_rows(i), _att_rows(i + 1)
                        dq_pair = (_dot(dsp[even], klo_p, NN) + _dot(dsc[even], klo_c, NN)
                                   + _dot(dsp[odd], khi_p, NN) + _dot(dsc[odd], khi_c, NN))
                        j = (first + i) // 2
                        dq_ref[:, j * LANES:(j + 1) * LANES] = (dq_pair * (ATT_HD ** -0.5)).astype(BF16)
                    acc[0] = acc[0] + _dot(dsp, qs, TN)
                    acc[1] = acc[1] + _dot(dsc, qs, TN)
                    acc[2] = acc[2] + _dot(pp.astype(BF16), dos, TN)
                    acc[3] = acc[3] + _dot(pc.astype(BF16), dos, TN)
                halves.append([a + pltpu.roll(a, ATT_HD, 1) for a in acc])
            prev = jnp.concatenate(
                [jnp.where(lo, halves[0][0], halves[1][0]), jnp.where(lo, halves[0][2], halves[1][2])], axis=1)
            cur = jnp.concatenate(
                [jnp.where(lo, halves[0][1], halves[1][1]), jnp.where(lo, halves[0][3], halves[1][3])], axis=1)
            dkv_ref[...] = (carry_ref[...] + prev).astype(BF16)
            carry_ref[...] = cur
            dsink_ref[...] += dsink

    blk = lambda n: jnp.minimum(n, nb - 1)
    return pl.pallas_call(
        body,
        name="attn_bwd",
        grid=(nb + 1,),
        in_specs=[
            pl.BlockSpec(memory_space=pltpu.SMEM),
            pl.BlockSpec((WINDOW, D_MODEL), lambda n: (blk(n), 0)),
            pl.BlockSpec((WINDOW, 2 * LANES), lambda n: (jnp.maximum(blk(n) - 1, 0), 0)),
            pl.BlockSpec((WINDOW, 2 * LANES), lambda n: (blk(n), 0)),
            pl.BlockSpec((WINDOW, D_MODEL), lambda n: (blk(n), 0)),
        ],
        out_specs=[
            pl.BlockSpec((WINDOW, D_MODEL), lambda n: (blk(n), 0)),
            pl.BlockSpec((WINDOW, 2 * LANES), lambda n: (jnp.maximum(n - 1, 0), 0)),
            pl.BlockSpec((1, LANES), lambda n: (0, 0)),
        ],
        out_shape=[
            jax.ShapeDtypeStruct((t, D_MODEL), BF16),
            jax.ShapeDtypeStruct((t, 2 * LANES), BF16),
            jax.ShapeDtypeStruct((1, LANES), F32),
        ],
        scratch_shapes=[pltpu.VMEM((WINDOW, 2 * LANES), F32)],
        compiler_params=_cparams(("arbitrary",)),
    )(sinks, q, kv, kv, dout)


def _ffn_fwd(h, norm_g, w_up, conv_w, conv_b, w_down, tag, after_up=lambda up: None):
    up = _mm_nn(h, w_up, gain=norm_g, name=f"ffn{tag}_up")
    after_up(up)
    act = _conv_fwd(up, conv_w, conv_b, name=f"ffn{tag}_conv")
    h_out = _mm_nn(act, w_down, res=h, name=f"ffn{tag}_down")
    return h_out, (up, act)


def _ffn_bwd(dh, h, norm_g, w_up, conv_w, conv_b, w_down, saved, tag, deps=()):
    up, act = saved
    dw_down = _mm_tn(act, dh, 1, D_MODEL, name=f"ffn{tag}_dwdown", deps=deps)
    dact = _mm_nt(dh, w_down, name=f"ffn{tag}_dact", deps=deps)
    dup, dconv_w, dconv_b = _conv_bwd(up, conv_w, conv_b, dact, name=f"ffn{tag}_dconv")
    dw_up = _mm_tn(h, dup, N_CHIPS, CONV_COLS, stacked=True, gain=norm_g, name=f"ffn{tag}_dwup")
    dh_in, dnorm = _mm_nt(dup, w_up, stacked=True, norm_of=(h, norm_g, dh), name=f"ffn{tag}_dxn")
    return dh_in, dict(ffn_w_down=dw_down, ffn_w_up=dw_up, ffn_conv_w=dconv_w, ffn_conv_b=dconv_b, ffn_norm=dnorm)


def _local_step(x, target, w, fetch=lambda w, stage, after: w, hook=lambda point, dh, grads: ()):
    proj = _mm_nn(x, w["hg_w_in"], gain=w["hg_norm"], name="hg_in")
    o, y, states = _hgrn_fwd(proj, w["hg_lb"], w["hg_out_norm"])
    w = fetch(w, "mixer_out", y)
    fetch(w, "layer0_relay", y)
    h_a = _mm_nn(y, w["hg_w_out"], res=x, name="hg_out")
    w = fetch(w, "layer0", h_a)
    h1, ffn0 = _ffn_fwd(h_a, w["ffn_norm"][0], w["ffn_w_up"][0], w["ffn_conv_w"][0], w["ffn_conv_b"][0], w["ffn_w_down"][0], 0,
                        lambda up: fetch(w, "layer1_relay", up))
    w = fetch(w, "layer1", h1)
    kv = _mm_nn(h1, w["w_kv"], gain=w["kv_norm"], out_dtype=BF16, name="kv_proj")
    qa = _mm_nn(h1, w["attn_w_q"], gain=w["attn_norm"], out_dtype=BF16, name="attn_q")
    ao = _attn_fwd(qa, kv, w["attn_sinks"])
    h_b = _mm_nn(ao, w["attn_w_o"], res=h1, name="attn_o")
    h2, ffn1 = _ffn_fwd(h_b, w["ffn_norm"][1], w["ffn_w_up"][1], w["ffn_conv_w"][1], w["ffn_conv_b"][1], w["ffn_w_down"][1], 1)
    dh2, d_final, loss = _loss_head(h2, w["final_norm"], target)

    dh_b, g1 = _ffn_bwd(dh2, h_b, w["ffn_norm"][1], w["ffn_w_up"][1], w["ffn_conv_w"][1], w["ffn_conv_b"][1], w["ffn_w_down"][1], ffn1, 1)
    deps = hook("ffn1", dh_b, g1)
    dw_o = _mm_tn(ao, dh_b, 1, D_MODEL, name="attn_dwo", deps=deps)
    dao = _mm_nt(dh_b, w["attn_w_o"], out_dtype=BF16, name="attn_dao", deps=deps)
    dqa, dkv, dsinks = _attn_bwd(qa, kv, w["attn_sinks"], dao)
    dw_q = _mm_tn(h1, dqa, 1, D_MODEL, gain=w["attn_norm"], name="attn_dwq")
    dh1, d_attn_norm = _mm_nt(dqa, w["attn_w_q"], norm_of=(h1, w["attn_norm"], dh_b), name="attn_dxa")
    dw_kv = _mm_tn(h1, dkv, 1, 2 * LANES, gain=w["kv_norm"], name="kv_dw")
    dh1, d_kv_norm = _mm_nt(dkv, w["w_kv"], norm_of=(h1, w["kv_norm"], dh1), name="kv_dx")
    deps = hook("attn", dh1, dict(attn_w_o=dw_o, attn_w_q=dw_q, w_kv=dw_kv))
    dh_a, g0 = _ffn_bwd(dh1, h_a, w["ffn_norm"][0], w["ffn_w_up"][0], w["ffn_conv_w"][0], w["ffn_conv_b"][0], w["ffn_w_down"][0], ffn0, 0, deps)
    dw_out = _mm_tn(y, dh_a, 1, D_MODEL, name="hg_dwout")
    deps = hook("ffn0", dh_a, dict(g0, hg_w_out=dw_out))
    dy = _mm_nt(dh_a, w["hg_w_out"], out_dtype=BF16, name="hg_dy", deps=deps)
    dproj, dlb, d_out_norm = _hgrn_bwd(proj, w["hg_lb"], w["hg_out_norm"], o, states, dy)
    deps = hook("hgrn", dproj, None)
    dw_in = _mm_tn(x, dproj, N_CHIPS, D_MODEL, stacked=True, gain=w["hg_norm"], name="hg_dwin", deps=deps)
    deps = hook("hg_w", dproj, dict(hg_w_in=dw_in))
    dx, d_hg_norm = _mm_nt(dproj, w["hg_w_in"], stacked=True, norm_of=(x, w["hg_norm"], dh_a), name="hg_dxn", deps=deps)

    grads = dict(
        hg_norm=d_hg_norm, hg_w_in=dw_in, hg_lb=dlb, hg_out_norm=d_out_norm, hg_w_out=dw_out,
        kv_norm=d_kv_norm, w_kv=dw_kv, attn_norm=d_attn_norm, attn_w_q=dw_q, attn_sinks=dsinks, attn_w_o=dw_o,
        final_norm=d_final,
    )
    for name in g0:
        grads[name] = [g0[name], g1[name]]
    return loss, dx, grads


ANY = pl.BlockSpec(memory_space=pl.ANY)


def _place():
    x, y, c = lax.axis_index("x"), lax.axis_index("y"), lax.axis_index("c")
    chips = [(1 - x, y), (x, 1 - y), (1 - x, 1 - y)]
    return x, y, c, chips


def _rcopy(src, dst, send_sem, recv_sem, to):
    return pltpu.make_async_remote_copy(src_ref=src, dst_ref=dst, send_sem=send_sem, recv_sem=recv_sem, device_id=to, device_id_type=MESH)


HBM = pl.BlockSpec(memory_space=pltpu.HBM)
SEM = pl.BlockSpec(memory_space=pltpu.SEMAPHORE)
EFFECT = pltpu.SideEffectType.DATAFLOW_SIDE_EFFECTING


def _in_hbm(a):
    return pltpu.with_memory_space_constraint(a, pltpu.HBM)


def _place_shard(shard, place, dtype, name, deps=(), layer=None):
    r, cols = shard.shape[-2:]
    tr = _pick(r, ELEM_ROWS)
    src = pl.BlockSpec((tr, cols), lambda i, place_ref: (i, 0)) if layer is None else pl.BlockSpec((None, tr, cols), lambda i, place_ref: (layer, i, 0))

    def body(place_ref, s_ref, *rest):
        o_ref = rest[-1]
        o_ref[...] = s_ref[...].astype(o_ref.dtype)

    return pl.pallas_call(
        body,
        name=name,
        grid_spec=pltpu.PrefetchScalarGridSpec(
            num_scalar_prefetch=1,
            grid=(r // tr,),
            in_specs=[src] + _dep_specs(deps),
            out_specs=pl.BlockSpec((None, tr, cols), lambda i, place_ref: (place_ref[0], i, 0)),
        ),
        out_shape=jax.ShapeDtypeStruct((N_CHIPS, r, cols), dtype),
        compiler_params=_cparams(("parallel",)),
    )(place, shard, *deps)


def _start_copies(name, bufs, n_sem, copies):
    n = len(bufs)

    def body(*refs):
        for cp in copies(refs[:n], refs[n], refs[n + 1]):
            cp.start()
        refs[-1][...] = jnp.zeros_like(refs[-1])

    outs = pl.pallas_call(
        body,
        name=name,
        in_specs=[HBM] * n,
        out_specs=[SEM, SEM] + [HBM] * n + [pl.BlockSpec(memory_space=pltpu.VMEM)],
        out_shape=[pltpu.SemaphoreType.DMA((n_sem,)), pltpu.SemaphoreType.DMA((n_sem,))] + [pltpu.HBM(b.shape, b.dtype) for b in bufs]
        + [jax.ShapeDtypeStruct((SUBLANES, LANES), F32)],
        input_output_aliases={i: 2 + i for i in range(n)},
        compiler_params=pltpu.CompilerParams(has_side_effects=EFFECT),
    )(*[_in_hbm(b) for b in bufs])
    return outs[0], outs[1], list(outs[2:-1]), outs[-1]


def _wait_copies(name, bufs, send_sems, recv_sems, after, copies):
    n = len(bufs)

    def body(*refs):
        for cp in copies(refs[:n], refs[n], refs[n + 1]):
            cp.wait_send()
            cp.wait_recv()

    return pl.pallas_call(
        body,
        name=name,
        in_specs=[HBM] * n + [SEM, SEM, ANY],
        out_specs=[HBM] * n,
        out_shape=[pltpu.HBM(b.shape, b.dtype) for b in bufs],
        input_output_aliases={i: i for i in range(n)},
        compiler_params=pltpu.CompilerParams(has_side_effects=EFFECT),
    )(*bufs, send_sems, recv_sems, after)


def _relay_copies(name, bufs, send_sems, recv_sems, after, landed, n_sem, onward):
    n = len(bufs)

    def body(*refs):
        for cp in landed(refs[:n], refs[n], refs[n + 1]):
            cp.wait_send()
            cp.wait_recv()
        for cp in onward(refs[:n], refs[n + 3], refs[n + 4]):
            cp.start()
        refs[-1][...] = jnp.zeros_like(refs[-1])

    outs = pl.pallas_call(
        body,
        name=name,
        in_specs=[HBM] * n + [SEM, SEM, ANY],
        out_specs=[SEM, SEM] + [HBM] * n + [pl.BlockSpec(memory_space=pltpu.VMEM)],
        out_shape=[pltpu.SemaphoreType.DMA((n_sem,)), pltpu.SemaphoreType.DMA((n_sem,))] + [pltpu.HBM(b.shape, b.dtype) for b in bufs]
        + [jax.ShapeDtypeStruct((SUBLANES, LANES), F32)],
        input_output_aliases={i: 2 + i for i in range(n)},
        compiler_params=pltpu.CompilerParams(has_side_effects=EFFECT),
    )(*bufs, send_sems, recv_sems, after)
    return outs[0], outs[1], list(outs[2:-1]), outs[-1]


def _gather_half_copies(first, count, over_ici):
    def copies(refs, send_sems, recv_sems):
        x, y, c, chips = _place()
        out = []
        for i in range(count):
            h = refs[i].shape[1] // 2
            mine = pl.ds(c * h, h)
            for j, (px, py) in enumerate(chips):
                k = 3 * (first + i) + j
                slot = 2 * x + y if over_ici else 2 * px + py
                to = (px, py, c) if over_ici else (x, y, 1 - c)
                out.append(_rcopy(refs[i].at[slot, mine], refs[i].at[slot, mine], send_sems.at[k], recv_sems.at[k], to))
        return out

    return copies


def _gather_copies(first, count):
    def copies(refs, send_sems, recv_sems):
        x, y, c, chips = _place()
        me = 2 * x + y
        out = []
        for i in range(count):
            for j, (px, py) in enumerate(chips):
                k = 3 * (first + i) + j
                out.append(_rcopy(refs[i].at[me], refs[i].at[me], send_sems.at[k], recv_sems.at[k], (px, py, c)))
        return out

    return copies


def _swap_copies(n):
    def copies(refs, send_sems, recv_sems):
        x, y, c, _ = _place()
        out = []
        for i in range(n):
            h = refs[i].shape[1] // 2
            out.append(_rcopy(refs[i].at[:, pl.ds((1 - c) * h, h)], refs[n + i], send_sems.at[i], recv_sems.at[i], (x, y, 1 - c)))
        return out

    return copies


def _partial_copies(n):
    def copies(refs, send_sems, recv_sems):
        x, y, c, chips = _place()
        out = []
        for i in range(n):
            for j, (px, py) in enumerate(chips):
                out.append(_rcopy(refs[i].at[2 * px + py], refs[n + i].at[j], send_sems.at[3 * i + j], recv_sems.at[3 * i + j], (px, py, c)))
        return out

    return copies


def _share_copies(n):
    def copies(refs, send_sems, recv_sems):
        x, y, c, _ = _place()
        return [_rcopy(refs[i].at[c], refs[i].at[c], send_sems.at[i], recv_sems.at[i], (x, y, 1 - c)) for i in range(n)]

    return copies


def _allreduce_small(groups, widths):
    flat = [a for g in groups for a in g]
    n = len(flat)
    rows = -(-sum(a.shape[0] for a in flat) // SUBLANES) * SUBLANES
    cols = max(a.shape[1] for a in flat)
    out_shapes = [(sum(a.shape[0] for a in g), wd or g[0].shape[1]) for g, wd in zip(groups, widths)]

    def body(*refs):
        ins, outs = refs[:n], refs[n:n + len(groups)]
        mine, buf, send_sems, recv_sems = refs[n + len(groups):]
        x, y, c, _ = _place()
        me = 4 * x + 2 * y + c
        mine[...] = jnp.zeros_like(mine)
        r0 = 0
        for a_ref in ins:
            r, w = a_ref.shape
            mine[r0:r0 + r, 0:w] = a_ref[...]
            r0 += r
        buf[me] = mine[...]
        copies = []
        for k in range(1, N_DEV):
            peer = (x ^ (k >> 2), y ^ ((k >> 1) & 1), c ^ (k & 1))
            cp = _rcopy(mine, buf.at[me], send_sems.at[k - 1], recv_sems.at[k - 1], peer)
            cp.start()
            copies.append(cp)
        for cp in copies:
            cp.wait()
        acc = buf[0]
        for d in range(1, N_DEV):
            acc = acc + buf[d]
        mine[...] = acc
        r0 = 0
        for o_ref in outs:
            r, w = o_ref.shape
            o_ref[...] = mine[r0:r0 + r, 0:w]
            r0 += r

    vmem = pl.BlockSpec(memory_space=pltpu.VMEM)
    return pl.pallas_call(
        body,
        name="allreduce_small",
        in_specs=[vmem] * n,
        out_specs=[vmem] * len(groups),
        out_shape=[jax.ShapeDtypeStruct(s, F32) for s in out_shapes],
        scratch_shapes=[pltpu.VMEM((rows, cols), F32), pltpu.VMEM((N_DEV, rows, cols), F32),
                        pltpu.SemaphoreType.DMA((N_DEV - 1,)), pltpu.SemaphoreType.DMA((N_DEV - 1,))],
        compiler_params=pltpu.CompilerParams(vmem_limit_bytes=VMEM_LIMIT_BYTES),
    )(*flat)


def _adamw_small(items):
    n = len(items)

    def body(*refs):
        for i in range(n):
            w_ref, m_ref, v_ref, g_ref = refs[4 * i:4 * i + 4]
            d_ref, nm_ref, nv_ref = refs[4 * n + 3 * i:4 * n + 3 * i + 3]
            d_ref[...], nm_ref[...], nv_ref[...] = _adamw_math(w_ref[...], m_ref[...], v_ref[...], g_ref[...])

    vmem = pl.BlockSpec(memory_space=pltpu.VMEM)
    outs = pl.pallas_call(
        body,
        name="adamw_small",
        in_specs=[vmem] * (4 * n),
        out_specs=[vmem] * (3 * n),
        out_shape=[jax.ShapeDtypeStruct(it[0].shape, F32) for it in items for _ in range(3)],
        compiler_params=pltpu.CompilerParams(vmem_limit_bytes=VMEM_LIMIT_BYTES),
    )(*[a for it in items for a in it])
    return [tuple(outs[3 * i:3 * i + 3]) for i in range(n)]


class _Reduction:
    def __init__(self, tag, grads, place):
        self.tag, self.n, self.place = tag, len(grads), place
        lands = [lax.empty((N_CHIPS, g.shape[1] // 2, g.shape[2]), F32) for g in grads]
        self._start("swap", list(grads) + lands, self.n, _swap_copies(self.n))

    def _start(self, stage, bufs, n_sem, copies):
        *self.flight, self.token = _start_copies(f"rs_{stage}_start_{self.tag}", bufs, n_sem, copies)

    def _landed(self, stage, after, copies):
        send_sems, recv_sems, bufs = self.flight
        return _wait_copies(f"rs_{stage}_wait_{self.tag}", bufs, send_sems, recv_sems, after, copies)

    def to_chips(self, after):
        n = self.n
        bufs = self._landed("swap", after, _swap_copies(n))
        sums = [_add_core_halves(g, o, self.place, name=f"rs_add_core_{self.tag}_{i}") for i, (g, o) in enumerate(zip(bufs[:n], bufs[n:]))]
        self.mine = [f for f, _ in sums]
        parts = [b for _, b in sums]
        lands = [lax.empty((3,) + p.shape[1:], BF16) for p in parts]
        self._start("send", parts + lands, 3 * n, _partial_copies(n))

    def to_core(self, after):
        n = self.n
        bufs = self._landed("send", after, _partial_copies(n))
        halves = [_add_chip_partials(f, o, self.place, name=f"rs_add_chip_{self.tag}_{i}") for i, (f, o) in enumerate(zip(self.mine, bufs[n:]))]
        self._start("share", halves, n, _share_copies(n))

    def finish(self, after):
        return [b.reshape((-1,) + b.shape[2:]) for b in self._landed("share", after, _share_copies(self.n))]


ELEM_ROWS = (256, 176, 128, 64, 32, 16, 8)


def _add_core_halves(grad, got, place, name):
    s, r, cols = grad.shape
    h = r // 2
    tr = _pick(h, ELEM_ROWS)

    def body(place_ref, g_ref, o_ref, f_ref, b_ref):
        acc = g_ref[...] + o_ref[...]
        b_ref[...] = acc.astype(BF16)

        @pl.when(pl.program_id(1) == place_ref[0])
        def _():
            f_ref[...] = acc

    blk = pl.BlockSpec((None, tr, cols), lambda i, k, place_ref: (k, i, 0))
    return pl.pallas_call(
        body,
        name=name,
        grid_spec=pltpu.PrefetchScalarGridSpec(
            num_scalar_prefetch=1,
            grid=(h // tr, s),
            in_specs=[pl.BlockSpec((None, None, tr, cols), lambda i, k, place_ref: (k, place_ref[1], i, 0)), blk],
            out_specs=[pl.BlockSpec((tr, cols), lambda i, k, place_ref: (i, 0)), blk],
        ),
        out_shape=[jax.ShapeDtypeStruct((h, cols), F32), jax.ShapeDtypeStruct((s, h, cols), BF16)],
        compiler_params=_cparams(("parallel", "arbitrary")),
    )(place, grad.reshape(s, 2, h, cols), got)


def _add_chip_partials(mine, got, place, name):
    h, cols = mine.shape
    tr = _pick(h, ELEM_ROWS)

    def body(place_ref, m_ref, g_ref, o_ref):
        acc = m_ref[...]
        for j in range(3):
            acc = acc + g_ref[j].astype(F32)
        o_ref[...] = acc

    return pl.pallas_call(
        body,
        name=name,
        grid_spec=pltpu.PrefetchScalarGridSpec(
            num_scalar_prefetch=1,
            grid=(h // tr,),
            in_specs=[
                pl.BlockSpec((tr, cols), lambda i, place_ref: (i, 0)),
                pl.BlockSpec((3, tr, cols), lambda i, place_ref: (0, i, 0)),
            ],
            out_specs=pl.BlockSpec((None, tr, cols), lambda i, place_ref: (place_ref[1], i, 0)),
        ),
        out_shape=jax.ShapeDtypeStruct((2, h, cols), F32),
        compiler_params=_cparams(("parallel",)),
    )(place, mine, got)


def _adamw_math(w, m, v, g):
    nm = ADAM_B1 * m + (1.0 - ADAM_B1) * g
    nv = ADAM_B2 * v + (1.0 - ADAM_B2) * (g * g)
    m_hat = nm * (1.0 / (1.0 - ADAM_B1 ** ADAM_STEP))
    v_hat = nv * (1.0 / (1.0 - ADAM_B2 ** ADAM_STEP))
    return -ADAM_LR * (m_hat / (jnp.sqrt(v_hat) + ADAM_EPS) + ADAM_WD * w), nm, nv


def _adamw_layer(w, m, v, g, layer, prev, name):
    nl, r, cols = w.shape
    tr = _pick(r, ELEM_ROWS)

    def body(w_ref, m_ref, v_ref, g_ref, *rest):
        go_ref, d_ref, nm_ref, nv_ref = rest[-4:]
        gv = g_ref[...]
        d_ref[...], nm_ref[...], nv_ref[...] = _adamw_math(w_ref[...], m_ref[...], v_ref[...], gv)
        go_ref[...] = gv

    lay = pl.BlockSpec((None, tr, cols), lambda i: (layer, i, 0))
    return pl.pallas_call(
        body,
        name=name,
        grid=(r // tr,),
        in_specs=[lay] * 3 + [pl.BlockSpec((tr, cols), lambda i: (i, 0))] + ([ANY] * 4 if prev else []),
        out_specs=[lay] * 4,
        out_shape=[jax.ShapeDtypeStruct((nl, r, cols), F32)] * 4,
        input_output_aliases={4 + k: k for k in range(4)} if prev else {},
        compiler_params=_cparams(("parallel",)),
    )(w, m, v, g, *(prev or ()))


def _adamw(w, m, v, g, name):
    r, cols = w.shape
    tr = _pick(r, ELEM_ROWS)

    def body(w_ref, m_ref, v_ref, g_ref, d_ref, nm_ref, nv_ref):
        d_ref[...], nm_ref[...], nv_ref[...] = _adamw_math(w_ref[...], m_ref[...], v_ref[...], g_ref[...])

    blk = pl.BlockSpec((tr, cols), lambda i: (i, 0))
    return pl.pallas_call(
        body,
        name=name,
        grid=(r // tr,),
        in_specs=[blk] * 4,
        out_specs=[blk] * 3,
        out_shape=[jax.ShapeDtypeStruct((r, cols), F32)] * 3,
        compiler_params=_cparams(("parallel",)),
    )(w, m, v, g)


SMALL_COLS = 384
SMALL_ROWS = 16


def _pad_rows(flat, rows, cols):
    return jnp.pad(flat, (0, rows * cols - flat.shape[0])).reshape(rows, cols)


def kernel(x, hg_norm, hg_w_in, hg_lb_logits, hg_out_norm, hg_w_out, kv_norm, w_kv, attn_norm, attn_w_q, attn_sinks, attn_w_o, ffn_norm, ffn_w_up, ffn_conv_w, ffn_conv_b, ffn_w_down, final_norm, loss_target, m_hg_norm, m_hg_w_in, m_hg_lb_logits, m_hg_out_norm, m_hg_w_out, m_kv_norm, m_w_kv, m_attn_norm, m_attn_w_q, m_attn_sinks, m_attn_w_o, m_ffn_norm, m_ffn_w_up, m_ffn_conv_w, m_ffn_conv_b, m_ffn_w_down, m_final_norm, v_hg_norm, v_hg_w_in, v_hg_lb_logits, v_hg_out_norm, v_hg_w_out, v_kv_norm, v_w_kv, v_attn_norm, v_attn_w_q, v_attn_sinks, v_attn_w_o, v_ffn_norm, v_ffn_w_up, v_ffn_conv_w, v_ffn_conv_b, v_ffn_w_down, v_final_norm):
    wts = dict(hg_norm=hg_norm, hg_w_in=hg_w_in, hg_lb_logits=hg_lb_logits, hg_out_norm=hg_out_norm, hg_w_out=hg_w_out, kv_norm=kv_norm, w_kv=w_kv, attn_norm=attn_norm, attn_w_q=attn_w_q, attn_sinks=attn_sinks, attn_w_o=attn_w_o, ffn_norm=ffn_norm, ffn_w_up=ffn_w_up, ffn_conv_w=ffn_conv_w, ffn_conv_b=ffn_conv_b, ffn_w_down=ffn_w_down, final_norm=final_norm)
    mom1 = dict(hg_norm=m_hg_norm, hg_w_in=m_hg_w_in, hg_lb_logits=m_hg_lb_logits, hg_out_norm=m_hg_out_norm, hg_w_out=m_hg_w_out, kv_norm=m_kv_norm, w_kv=m_w_kv, attn_norm=m_attn_norm, attn_w_q=m_attn_w_q, attn_sinks=m_attn_sinks, attn_w_o=m_attn_w_o, ffn_norm=m_ffn_norm, ffn_w_up=m_ffn_w_up, ffn_conv_w=m_ffn_conv_w, ffn_conv_b=m_ffn_conv_b, ffn_w_down=m_ffn_w_down, final_norm=m_final_norm)
    mom2 = dict(hg_norm=v_hg_norm, hg_w_in=v_hg_w_in, hg_lb_logits=v_hg_lb_logits, hg_out_norm=v_hg_out_norm, hg_w_out=v_hg_w_out, kv_norm=v_kv_norm, w_kv=v_w_kv, attn_norm=v_attn_norm, attn_w_q=v_attn_w_q, attn_sinks=v_attn_sinks, attn_w_o=v_attn_w_o, ffn_norm=v_ffn_norm, ffn_w_up=v_ffn_w_up, ffn_conv_w=v_ffn_conv_w, ffn_conv_b=v_ffn_conv_b, ffn_w_down=v_ffn_w_down, final_norm=v_final_norm)
    names = list(wts)
    chip = 2 * lax.axis_index("x") + lax.axis_index("y")
    core = lax.axis_index("c")
    fs = D_FF // N_CHIPS
    ds = D_MODEL // N_CHIPS

    place_arr = jnp.stack([chip, core]).astype(jnp.int32)
    small = jnp.concatenate([hg_norm.reshape(-1), hg_lb_logits.reshape(-1), ffn_conv_w.reshape(-1)])
    n_small = small.shape[0]
    shards = [
        ("small", _pad_rows(small, SMALL_ROWS, SMALL_COLS), F32, None), ("hg_w_in", hg_w_in, BF16, 0),
        ("hg_w_out", hg_w_out, BF16, 0), ("ffn_w_up0", ffn_w_up, BF16, 0), ("ffn_w_down0", ffn_w_down, BF16, 0),
        ("w_kv", w_kv, BF16, None), ("attn_w_q", attn_w_q, BF16, 0), ("attn_w_o", attn_w_o, BF16, 0),
        ("ffn_w_up1", ffn_w_up, BF16, 1), ("ffn_w_down1", ffn_w_down, BF16, 1),
    ]
    n_first = 3
    spans = dict(layer0=(0, 2), layer1=(2, 7))

    def first_copies(refs, send_sems, recv_sems):
        return (_gather_copies(0, 1)(refs[:1], send_sems, recv_sems) + _gather_half_copies(1, 1, True)(refs[1:2], send_sems, recv_sems)
                + _gather_copies(2, 1)(refs[2:3], send_sems, recv_sems))

    placed = [_place_shard(s, place_arr, dt, name=f"place_{nm}", layer=ly) for nm, s, dt, ly in shards[:n_first]]
    first = _start_copies("gather_start_first", placed, 3 * n_first, first_copies)
    placed = [_place_shard(s, place_arr, dt, name=f"place_{nm}", deps=(first[3],), layer=ly) for nm, s, dt, ly in shards[n_first:]]
    rest = _start_copies("gather_start_rest", placed, 3 * len(placed), _gather_half_copies(0, len(placed), True))
    relayed = {}

    def fetch(w, stage, after):
        if stage == "first":
            w_in = _relay_copies("gather_first_relay", first[2][1:2], first[0], first[1], after,
                                 _gather_half_copies(1, 1, True), 3, _gather_half_copies(0, 1, False))
            got = _wait_copies("gather_wait_small", first[2][:1], first[0], first[1], w_in[3], _gather_copies(0, 1))
            got += _wait_copies("gather_wait_first", w_in[2], w_in[0], w_in[1], got[0], _gather_half_copies(0, 1, False))
        elif stage == "mixer_out":
            got = _wait_copies("gather_wait_mixer_out", first[2][2:], first[0], first[1], after, _gather_copies(2, 1))
        elif stage.endswith("_relay"):
            lo, hi = spans[stage[:-6]]
            relayed[stage[:-6]] = _relay_copies(
                f"gather_{stage}", rest[2][lo:hi], rest[0], rest[1], after,
                _gather_half_copies(lo, hi - lo, True), 3 * (hi - lo), _gather_half_copies(0, hi - lo, False))
            return w
        else:
            lo, hi = spans[stage]
            send_sems, recv_sems, bufs, _ = relayed[stage]
            got = _wait_copies(f"gather_wait_{stage}", bufs, send_sems, recv_sems, after, _gather_half_copies(0, hi - lo, False))
        w = dict(w)
        if stage == "first":
            g_small = got[0].reshape(N_CHIPS, -1)[:, :n_small]
            conv_w = g_small[:, 3 * ds:].reshape(N_CHIPS, 2, 3, fs).transpose(1, 2, 0, 3).reshape(2, 3, D_FF)
            w.update(
                hg_norm=g_small[:, :ds].reshape(1, D_MODEL),
                hg_lb=g_small[:, ds:3 * ds].reshape(N_CHIPS, 2, ds).transpose(1, 0, 2).reshape(2, D_MODEL),
                ffn_conv_w=[conv_w[0], conv_w[1]], hg_w_in=got[1],
            )
        elif stage == "mixer_out":
            w.update(hg_w_out=got[0].reshape(1, D_MODEL, D_MODEL))
        elif stage == "layer0":
            w.update(ffn_w_up=[got[0], None], ffn_w_down=[got[1].reshape(1, D_FF, D_MODEL), None])
        else:
            w.update(
                w_kv=got[0].reshape(1, D_MODEL, 2 * LANES), attn_w_q=got[1].reshape(1, D_MODEL, D_MODEL),
                attn_w_o=got[2].reshape(1, D_MODEL, D_MODEL), ffn_w_up=[w["ffn_w_up"][0], got[3]],
                ffn_w_down=[w["ffn_w_down"][0], got[4].reshape(1, D_FF, D_MODEL)],
            )
        return w

    whole = dict(
        hg_out_norm=hg_out_norm, kv_norm=kv_norm.reshape(1, D_MODEL), attn_norm=attn_norm, attn_sinks=attn_sinks.reshape(ATT_QH),
        ffn_norm=[ffn_norm[0:1], ffn_norm[1:2]], ffn_conv_b=[ffn_conv_b[0:1], ffn_conv_b[1:2]], final_norm=final_norm.reshape(1, D_MODEL),
    )
    whole = fetch(whole, "first", rest[3])

    red, layer1 = {}, {}

    def by_rows(g, rows):
        return g.reshape(N_CHIPS, rows, g.shape[2])

    def hook(point, dh, grads):
        if point == "ffn1":
            red["ffn1"] = _Reduction("ffn1", [by_rows(grads["ffn_w_down"], fs), grads["ffn_w_up"]], place_arr)
            return (red["ffn1"].token,)
        if point == "attn":
            red["ffn1"].to_chips(dh)
            layer1.update(grads)
            return (red["ffn1"].token,)
        if point == "ffn0":
            group = [by_rows(layer1["attn_w_o"], ds), by_rows(layer1["attn_w_q"], ds), by_rows(layer1["w_kv"], ds),
                     by_rows(grads["ffn_w_down"], fs), grads["ffn_w_up"], by_rows(grads["hg_w_out"], ds)]
            red["mid"] = _Reduction("mid", group, place_arr)
            return (red["mid"].token,)
        if point == "hgrn":
            red["ffn1"].to_core(dh)
            red["mid"].to_chips(dh)
            return (red["ffn1"].token, red["mid"].token)
        red["hg"] = _Reduction("hg", [grads["hg_w_in"]], place_arr)
        return (red["hg"].token,)

    loss, dx, grads = _local_step(x[0], loss_target[0], whole, fetch, hook)

    small_names = ["hg_out_norm", "attn_sinks", "kv_norm", "attn_norm", "ffn_norm", "ffn_conv_b", "final_norm", "hg_norm", "hg_lb_logits", "ffn_conv_w"]
    groups = [[loss]] + [grads[n] if isinstance(grads[n], list) else [grads[n]] for n in small_names[:-2]] + [[grads["hg_lb"]], grads["ffn_conv_w"]]
    summed = _allreduce_small(groups, [None, None, ATT_QH] + [None] * 8)
    red["hg"].to_chips(summed[1])
    loss_out = summed[0][0, 0]
    small_grads = dict(zip(small_names, summed[1:]))
    small_grads["hg_norm"] = lax.dynamic_slice(small_grads["hg_norm"], (0, chip * ds), (1, ds))
    small_grads["hg_lb_logits"] = lax.dynamic_slice(small_grads["hg_lb_logits"], (0, chip * ds), (2, ds))
    small_grads["ffn_conv_w"] = lax.dynamic_slice(small_grads["ffn_conv_w"], (0, chip * fs), (2 * 3, fs))

    out_g, out_d, out_m, out_v = {}, {}, {}, {}

    def update(name, g2):
        shape = wts[name].shape
        d2, m2, v2 = _adamw(wts[name].reshape(g2.shape), mom1[name].reshape(g2.shape), mom2[name].reshape(g2.shape), g2, name=f"adamw_{name}")
        out_g[name], out_d[name], out_m[name], out_v[name] = g2.reshape(shape), d2.reshape(shape), m2.reshape(shape), v2.reshape(shape)
        return d2

    def update_layer(name, g2, layer, prev):
        res = _adamw_layer(wts[name], mom1[name], mom2[name], g2, layer, prev, name=f"adamw_{name}{layer}")
        out_g[name], out_d[name], out_m[name], out_v[name] = res
        return res

    g_down1, g_up1 = red["ffn1"].finish(red["hg"].token)
    down1 = update_layer("ffn_w_down", g_down1, 1, None)
    up1 = update_layer("ffn_w_up", g_up1, 1, None)
    red["mid"].to_core(up1[1])
    g_o, g_q, g_kv, g_down0, g_up0, g_out = red["mid"].finish(up1[2])
    update("attn_w_o", g_o)
    update("attn_w_q", g_q)
    update("w_kv", g_kv)
    update("hg_w_out", g_out)
    update_layer("ffn_w_down", g_down0, 0, down1)
    last = update_layer("ffn_w_up", g_up0, 0, up1)
    red["hg"].to_core(last[1])
    (g_in,) = red["hg"].finish(last[2])
    update("hg_w_in", g_in)

    as_2d = lambda a, n: a.reshape(small_grads[n].shape)
    updated = _adamw_small([(as_2d(wts[n], n), as_2d(mom1[n], n), as_2d(mom2[n], n), small_grads[n]) for n in small_names])
    for n, (d2, m2, v2) in zip(small_names, updated):
        shape = wts[n].shape
        out_g[n], out_d[n], out_m[n], out_v[n] = small_grads[n].reshape(shape), d2.reshape(shape), m2.reshape(shape), v2.reshape(shape)

    grad_x = dx.reshape(x.shape)
    return (loss_out, grad_x, *[out_g[n] for n in names], *[out_d[n] for n in names], *[out_m[n] for n in names], *[out_v[n] for n in names])
```

```python
import functools

import jax
import jax.numpy as jnp
from jax import lax
from jax.experimental import pallas as pl
from jax.experimental.pallas import tpu as pltpu

F32 = jnp.float32
BF16 = jnp.bfloat16
MESH = pl.DeviceIdType.MESH

EPS = 1e-6
D_MODEL = 1024
HG_HEADS = 8
HG_DK = 128
HG_CHUNK = 64
ATT_HD = 64
ATT_QH = 16
ATT_KVH = 2
ATT_GROUP = ATT_QH // ATT_KVH
WINDOW = 128
D_FF = 2816
N_CHIPS = 4
N_DEV = 8
LANES = 128
SUBLANES = 8
VMEM_LIMIT_BYTES = 56 * 1024 * 1024
NEG = -1e30
ALIBI_SLOPES = tuple(2.0 ** (-8.0 * h / ATT_QH) for h in range(1, ATT_QH + 1))

ADAM_LR = 0.001
ADAM_B1 = 0.9
ADAM_B2 = 0.999
ADAM_EPS = 1e-08
ADAM_WD = 0.01
ADAM_STEP = 10


def _cparams(sem=None):
    return pltpu.CompilerParams(dimension_semantics=sem, vmem_limit_bytes=VMEM_LIMIT_BYTES)


def _pick(n, cands):
    for c in cands:
        if n % c == 0:
            return c
    return n


def _sigmoid(x):
    return 0.5 * jnp.tanh(0.5 * x) + 0.5


def _dot(a, b, dims):
    return lax.dot_general(a, b, (dims, ((), ())), preferred_element_type=F32)


NN = ((1,), (0,))
NT = ((1,), (1,))
TN = ((0,), (0,))


MM_ROWS = 1024


def _rms_stats(xv):
    rstd = lax.rsqrt(jnp.mean(xv * xv, axis=-1, keepdims=True) + EPS)
    return xv * rstd, rstd


def _mm_operand(a_ref, gain_ref):
    if gain_ref is None:
        return a_ref[...].astype(BF16)
    return (_rms_stats(a_ref[...])[0] * gain_ref[...]).astype(BF16)


def _mm_nn(a, w, res=None, out_dtype=F32, name="mm_nn", gain=None):
    m, k = a.shape
    s, _, ns = w.shape
    tm = min(m, MM_ROWS)
    tn = _pick(ns, (1024, 1408, 512, 256, 128))
    npb = ns // tn

    def body(a_ref, w_ref, *rest):
        o_ref = rest[-1]
        acc = _dot(_mm_operand(a_ref, rest[0] if gain is not None else None), w_ref[...], NN)
        if res is not None:
            acc = acc + rest[-2][...]
        o_ref[...] = acc.astype(o_ref.dtype)

    in_specs = [
        pl.BlockSpec((tm, k), lambda i, j: (i, 0)),
        pl.BlockSpec((None, k, tn), lambda i, j: (j // npb, 0, j % npb)),
    ]
    args = [a, w]
    if gain is not None:
        in_specs.append(pl.BlockSpec((1, k), lambda i, j: (0, 0)))
        args.append(gain)
    if res is not None:
        in_specs.append(pl.BlockSpec((tm, tn), lambda i, j: (i, j)))
        args.append(res)
    return pl.pallas_call(
        body,
        name=name,
        grid=(m // tm, s * npb),
        in_specs=in_specs,
        out_specs=pl.BlockSpec((tm, tn), lambda i, j: (i, j)),
        out_shape=jax.ShapeDtypeStruct((m, s * ns), out_dtype),
        compiler_params=_cparams(("parallel", "parallel")),
    )(*args)


def _dy_spec(stacked, tm, tn, npb, row, kk):
    if stacked:
        return pl.BlockSpec((None, tm, tn), lambda *g: (kk(g) // npb, row(g), kk(g) % npb))
    return pl.BlockSpec((tm, tn), lambda *g: (row(g), kk(g)))


def _dep_specs(deps):
    return [pl.BlockSpec(d.shape, lambda *g: (0, 0)) for d in deps]


def _mm_nt(dy, w, stacked=False, out_dtype=F32, name="mm_nt", deps=(), norm_of=None):
    s, k, ns = w.shape
    m = dy.shape[1] if stacked else dy.shape[0]
    tm = min(m, MM_ROWS)
    tko = _pick(k, (1024, 1408, 512, 256))
    tn = _pick(ns, (1024, 1408, 512, 256))
    npb = ns // tn
    nk = s * npb
    fused = norm_of is not None
    assert not fused or tko == k

    def body(dy_ref, w_ref, *rest):
        acc_ref = rest[-1]
        i, kk = pl.program_id(0), pl.program_id(2)

        @pl.when(kk == 0)
        def _():
            acc_ref[...] = jnp.zeros_like(acc_ref)

        acc_ref[...] += _dot(dy_ref[...].astype(BF16), w_ref[...], NT)

        if not fused:
            @pl.when(kk == nk - 1)
            def _():
                rest[-2][...] = acc_ref[...].astype(rest[-2].dtype)
            return
        x_ref, g_ref, dres_ref = rest[:3]
        dx_ref, dg_ref = rest[-3], rest[-2]

        @pl.when(jnp.logical_and(i == 0, kk == 0))
        def _():
            dg_ref[...] = jnp.zeros_like(dg_ref)

        @pl.when(kk == nk - 1)
        def _():
            dxn = acc_ref[...]
            xhat, rstd = _rms_stats(x_ref[...])
            gd = dxn * g_ref[...]
            dx_ref[...] = dres_ref[...] + rstd * (gd - xhat * jnp.mean(gd * xhat, axis=-1, keepdims=True))
            dg_ref[...] += jnp.sum(dxn * xhat, axis=0, keepdims=True)

    row = pl.BlockSpec((tm, tko), lambda i, j, kk: (i, j))
    vec = pl.BlockSpec((1, k), lambda i, j, kk: (0, 0))
    return pl.pallas_call(
        body,
        name=name,
        grid=(m // tm, k // tko, nk),
        in_specs=[
            _dy_spec(stacked, tm, tn, npb, lambda g: g[0], lambda g: g[2]),
            pl.BlockSpec((None, tko, tn), lambda i, j, kk: (kk // npb, j, kk % npb)),
        ] + ([row, vec, row] if fused else []) + _dep_specs(deps),
        out_specs=[row, vec] if fused else row,
        out_shape=[jax.ShapeDtypeStruct((m, k), F32), jax.ShapeDtypeStruct((1, k), F32)] if fused else jax.ShapeDtypeStruct((m, k), out_dtype),
        scratch_shapes=[pltpu.VMEM((tm, tko), F32)],
        compiler_params=_cparams(("arbitrary",) * 3 if fused else ("parallel", "parallel", "arbitrary")),
    )(dy, w, *(norm_of or ()), *deps)


def _mm_tn(a, dy, s, ns, stacked=False, name="mm_tn", deps=(), gain=None):
    m, k = a.shape
    tm = min(m, MM_ROWS)
    tk = _pick(k, (1024, 1408, 512, 256))
    tn = _pick(ns, (1024, 1408, 512, 256, 128))
    npb = ns // tn
    nm = m // tm
    assert gain is None or tk == k

    def body(a_ref, dy_ref, *rest):
        o_ref, acc_ref = rest[-2:]
        mm = pl.program_id(2)

        @pl.when(mm == 0)
        def _():
            acc_ref[...] = jnp.zeros_like(acc_ref)

        acc_ref[...] += _dot(_mm_operand(a_ref, rest[0] if gain is not None else None), dy_ref[...].astype(BF16), TN)

        @pl.when(mm == nm - 1)
        def _():
            o_ref[...] = acc_ref[...]

    return pl.pallas_call(
        body,
        name=name,
        grid=(k // tk, s * npb, nm),
        in_specs=[
            pl.BlockSpec((tm, tk), lambda i, j, mm: (mm, i)),
            _dy_spec(stacked, tm, tn, npb, lambda g: g[2], lambda g: g[1]),
        ] + ([pl.BlockSpec((1, k), lambda i, j, mm: (0, 0))] if gain is not None else []) + _dep_specs(deps),
        out_specs=pl.BlockSpec((None, tk, tn), lambda i, j, mm: (j // npb, i, j % npb)),
        out_shape=jax.ShapeDtypeStruct((s, k, ns), F32),
        scratch_shapes=[pltpu.VMEM((tk, tn), F32)],
        compiler_params=_cparams(("parallel", "parallel", "arbitrary")),
    )(a, dy, *(() if gain is None else (gain,)), *deps)


ROW_TILE = 512


def _loss_head(h, g, target):
    t, d = h.shape
    r = min(t, ROW_TILE)

    def body(h_ref, g_ref, t_ref, dh_ref, dg_ref, loss_ref):
        @pl.when(pl.program_id(0) == 0)
        def _():
            dg_ref[...] = jnp.zeros_like(dg_ref)
            loss_ref[...] = jnp.zeros_like(loss_ref)

        xv = h_ref[...]
        rstd = lax.rsqrt(jnp.mean(xv * xv, axis=-1, keepdims=True) + EPS)
        xhat = xv * rstd
        gv = g_ref[...]
        err = xhat * gv - t_ref[...]
        loss_ref[...] += 0.5 * jnp.sum(jnp.mean(err * err, axis=-1, keepdims=True), axis=0, keepdims=True)
        dy = err * (1.0 / d)
        gd = dy * gv
        dh_ref[...] = rstd * (gd - xhat * jnp.mean(gd * xhat, axis=-1, keepdims=True))
        dg_ref[...] += jnp.sum(dy * xhat, axis=0, keepdims=True)

    return pl.pallas_call(
        body,
        name="loss_head",
        grid=(t // r,),
        in_specs=[
            pl.BlockSpec((r, d), lambda i: (i, 0)),
            pl.BlockSpec((1, d), lambda i: (0, 0)),
            pl.BlockSpec((r, d), lambda i: (i, 0)),
        ],
        out_specs=[
            pl.BlockSpec((r, d), lambda i: (i, 0)),
            pl.BlockSpec((1, d), lambda i: (0, 0)),
            pl.BlockSpec((1, LANES), lambda i: (0, 0)),
        ],
        out_shape=[
            jax.ShapeDtypeStruct((t, d), F32),
            jax.ShapeDtypeStruct((1, d), F32),
            jax.ShapeDtypeStruct((1, LANES), F32),
        ],
        compiler_params=_cparams(("arbitrary",)),
    )(h, g, target)


CONV_ROWS = 256
CONV_COLS = 1408


def _conv_taps(x_ext, n):
    tot = x_ext.shape[0]
    g1 = pltpu.roll(x_ext, 1, 0)[tot - n:]
    g2 = pltpu.roll(x_ext, 2, 0)[tot - n:]
    return g2, g1


def _conv_fwd(up, conv_w, conv_b, name="conv_fwd"):
    t = up.shape[0]
    r = min(t, CONV_ROWS)
    tc = CONV_COLS
    ncb = D_FF // tc
    hb = r // SUBLANES

    def body(g_ref, halo_ref, v_ref, w_ref, b_ref, o_ref, c_ref):
        i = pl.program_id(1)
        g0 = g_ref[...]
        halo = halo_ref[...] * jnp.where(i > 0, 1.0, 0.0)
        g2, g1 = _conv_taps(jnp.concatenate([halo, g0], axis=0), r)
        c = b_ref[...] + w_ref[0:1, :] * g2 + w_ref[1:2, :] * g1 + w_ref[2:3, :] * g0
        c_ref[...] = c
        o_ref[...] = (c * _sigmoid(c) * v_ref[...]).astype(BF16)

    blk = pl.BlockSpec((r, tc), lambda j, i: (i, j))
    return pl.pallas_call(
        body,
        name=name,
        grid=(ncb, t // r),
        in_specs=[
            blk,
            pl.BlockSpec((SUBLANES, tc), lambda j, i: (jnp.maximum(i * hb - 1, 0), j)),
            pl.BlockSpec((r, tc), lambda j, i: (i, ncb + j)),
            pl.BlockSpec((3, tc), lambda j, i: (0, j)),
            pl.BlockSpec((1, tc), lambda j, i: (0, j)),
        ],
        out_specs=[blk, blk],
        out_shape=[jax.ShapeDtypeStruct((t, D_FF), BF16), jax.ShapeDtypeStruct((t, D_FF), F32)],
        compiler_params=_cparams(("parallel", "parallel")),
    )(up, up, up, conv_w, conv_b)


def _conv_bwd(up, conv_w, c, dact, name="conv_bwd"):
    t = up.shape[0]
    r = min(t, CONV_ROWS)
    tc = CONV_COLS
    ncb = D_FF // tc
    nrt = t // r

    def body(g_ref, v_ref, w_ref, c_ref, da_ref, dup_ref, dw_ref, db_ref, nxt_ref):
        ii = pl.program_id(1)

        @pl.when(ii == 0)
        def _():
            nxt_ref[...] = jnp.zeros_like(nxt_ref)
            dw_ref[...] = jnp.zeros_like(dw_ref)
            db_ref[...] = jnp.zeros_like(db_ref)

        g0 = g_ref[...]
        w0, w1, w2 = w_ref[0:1, :], w_ref[1:2, :], w_ref[2:3, :]
        c = c_ref[...]
        sg = _sigmoid(c)
        da = da_ref[...]
        dup_ref[1] = (da * (c * sg)).astype(BF16)
        dc = da * v_ref[...] * (sg * (1.0 + c * (1.0 - sg)))
        ext = jnp.concatenate([dc, nxt_ref[...]], axis=0)
        tot = r + SUBLANES
        d1 = pltpu.roll(ext, tot - 1, 0)[:r]
        d2 = pltpu.roll(ext, tot - 2, 0)[:r]
        nxt_ref[...] = dc[:SUBLANES]
        dup_ref[0] = (w2 * dc + w1 * d1 + w0 * d2).astype(BF16)
        db_ref[...] += jnp.sum(dc, axis=0, keepdims=True)
        dw_ref[0:1, :] += jnp.sum(d2 * g0, axis=0, keepdims=True)
        dw_ref[1:2, :] += jnp.sum(d1 * g0, axis=0, keepdims=True)
        dw_ref[2:3, :] += jnp.sum(dc * g0, axis=0, keepdims=True)

    rev = lambda ii: nrt - 1 - ii
    dup, dw, db = pl.pallas_call(
        body,
        name=name,
        grid=(ncb, nrt),
        in_specs=[
            pl.BlockSpec((r, tc), lambda j, ii: (rev(ii), j)),
            pl.BlockSpec((r, tc), lambda j, ii: (rev(ii), ncb + j)),
            pl.BlockSpec((3, tc), lambda j, ii: (0, j)),
            pl.BlockSpec((r, tc), lambda j, ii: (rev(ii), j)),
            pl.BlockSpec((r, tc), lambda j, ii: (rev(ii), j)),
        ],
        out_specs=[
            pl.BlockSpec((2, None, r, tc), lambda j, ii: (0, j, rev(ii), 0)),
            pl.BlockSpec((3, tc), lambda j, ii: (0, j)),
            pl.BlockSpec((1, tc), lambda j, ii: (0, j)),
        ],
        out_shape=[
            jax.ShapeDtypeStruct((2, ncb, t, tc), BF16),
            jax.ShapeDtypeStruct((3, D_FF), F32),
            jax.ShapeDtypeStruct((1, D_FF), F32),
        ],
        scratch_shapes=[pltpu.VMEM((SUBLANES, tc), F32)],
        compiler_params=_cparams(("parallel", "arbitrary")),
    )(up, up, conv_w, c, dact)
    return dup.reshape(2 * ncb, t, tc), dw, db


def _split3(x):
    x1 = x.astype(BF16)
    r1 = x - x1.astype(F32)
    x2 = r1.astype(BF16)
    x3 = (r1 - x2.astype(F32)).astype(BF16)
    return x1, x2, x3


def _tri_dot(tri, x, dims):
    x1, x2, x3 = _split3(x)
    return _dot(tri, x1, dims) + _dot(tri, x2, dims) + _dot(tri, x3, dims)


def _lower_bound(logits_ref):
    return _sigmoid(logits_ref[0:1, :] - logits_ref[1:2, :])


def _hg_gates(qr, fr, lb):
    q = qr * _sigmoid(qr) * (HG_DK ** -0.5)
    sf = _sigmoid(fr)
    fg = lb + (1.0 - lb) * sf
    return q, sf, fg


def _hg_chunk_terms(q, fg, tril_b, low_half):
    g = jnp.log(fg)
    k = 1.0 - fg
    cum = _tri_dot(tril_b, g, NN)
    c_last = jnp.sum(g, axis=0, keepdims=True)
    c_mid = jnp.sum(jnp.where(low_half, g, 0.0), axis=0, keepdims=True)
    e_q = jnp.exp(cum - c_mid)
    e_k = jnp.exp(c_mid - cum)
    e_0 = jnp.exp(cum)
    e_l = jnp.exp(c_last - cum)
    return k, e_q, e_k, e_0, e_l, jnp.exp(c_last)


HG_BLOCK = 256


def _hg_proj_specs(rb, row):
    return [pl.BlockSpec((rb, D_MODEL), functools.partial(lambda i, k: (row(i), k), k=k)) for k in range(4)]


def _hg_consts(c):
    tril = lax.broadcasted_iota(jnp.int32, (c, c), 0) >= lax.broadcasted_iota(jnp.int32, (c, c), 1)
    low_half = lax.broadcasted_iota(jnp.int32, (c, D_MODEL), 0) < c // 2
    return tril, tril.astype(BF16), low_half


def _hgrn_fwd(proj, lb, wn):
    t = proj.shape[0]
    c = HG_CHUNK
    rb = min(t, HG_BLOCK)
    cpb = rb // c

    def body(q_ref, f_ref, i_ref, g_ref, lb_ref, wn_ref, o_ref, y_ref, st_ref, s_scr):
        @pl.when(pl.program_id(0) == 0)
        def _():
            s_scr[...] = jnp.zeros_like(s_scr)

        lb_all = _lower_bound(lb_ref)
        wnv = wn_ref[...]
        tril, tril_b, low_half = _hg_consts(c)

        def chunk(n, carry):
            rows = pl.ds(pl.multiple_of(n * c, c), c)
            q, _, fg = _hg_gates(q_ref[rows, :], f_ref[rows, :], lb_all)
            k, e_q, e_k, e_0, e_l, e_last = _hg_chunk_terms(q, fg, tril_b, low_half)
            qi, ki, q0, kl = (q * e_q).astype(BF16), (k * e_k).astype(BF16), (q * e_0).astype(BF16), (k * e_l).astype(BF16)
            v = i_ref[rows, :].astype(BF16)
            gr = g_ref[rows, :]
            gate = gr * _sigmoid(gr)
            for h in range(HG_HEADS):
                cols = slice(h * HG_DK, (h + 1) * HG_DK)
                st = s_scr[h]
                st_ref[h, n] = st
                a = jnp.where(tril, _dot(qi[:, cols], ki[:, cols], NT), 0.0)
                o = _dot(q0[:, cols], st.astype(BF16), NT) + _dot(a.astype(BF16), v[:, cols], NN)
                s_scr[h] = st * e_last[:, cols] + _dot(v[:, cols], kl[:, cols], TN)
                o_ref[rows, cols] = o
                rstd = lax.rsqrt(jnp.mean(o * o, axis=-1, keepdims=True) + EPS)
                y_ref[rows, cols] = (o * rstd * wnv * gate[:, cols]).astype(BF16)
            return carry

        lax.fori_loop(0, cpb, chunk, 0)

    blk = pl.BlockSpec((rb, D_MODEL), lambda i: (i, 0))
    return pl.pallas_call(
        body,
        name="hgrn_fwd",
        grid=(t // rb,),
        in_specs=_hg_proj_specs(rb, lambda i: i) + [pl.BlockSpec((2, D_MODEL), lambda i: (0, 0)), pl.BlockSpec((1, HG_DK), lambda i: (0, 0))],
        out_specs=[blk, blk, pl.BlockSpec((HG_HEADS, cpb, HG_DK, HG_DK), lambda i: (0, i, 0, 0))],
        out_shape=[
            jax.ShapeDtypeStruct((t, D_MODEL), F32),
            jax.ShapeDtypeStruct((t, D_MODEL), BF16),
            jax.ShapeDtypeStruct((HG_HEADS, t // c, HG_DK, HG_DK), F32),
        ],
        scratch_shapes=[pltpu.VMEM((HG_HEADS, HG_DK, HG_DK), F32)],
        compiler_params=_cparams(("arbitrary",)),
    )(proj, proj, proj, proj, lb, wn)


def _hgrn_bwd(proj, lb, wn, o, states, dy):
    t = proj.shape[0]
    c = HG_CHUNK
    rb = min(t, HG_BLOCK)
    cpb = rb // c
    nb = t // rb

    def body(q_ref, f_ref, i_ref, g_ref, lb_ref, wn_ref, o_ref, st_ref, dy_ref, dp_ref, dl_ref, dwn_ref, ds_scr, dlb_scr):
        step = pl.program_id(0)

        @pl.when(step == 0)
        def _():
            dwn_ref[...] = jnp.zeros_like(dwn_ref)
            ds_scr[...] = jnp.zeros_like(ds_scr)
            dlb_scr[...] = jnp.zeros_like(dlb_scr)

        lb_all = _lower_bound(lb_ref)
        wnv = wn_ref[...]
        tril, tril_b, low_half = _hg_consts(c)

        def chunk(nn, carry):
            n = cpb - 1 - nn
            rows = pl.ds(pl.multiple_of(n * c, c), c)
            qr = q_ref[rows, :]
            gr = g_ref[rows, :]
            q, sf, fg = _hg_gates(qr, f_ref[rows, :], lb_all)
            k, e_q, e_k, e_0, e_l, e_last = _hg_chunk_terms(q, fg, tril_b, low_half)
            qi, qi_lo, _ = _split3(q * e_q)
            ki, ki_lo, _ = _split3(k * e_k)
            q0 = (q * e_0).astype(BF16)
            kl = (k * e_l).astype(BF16)
            v = i_ref[rows, :].astype(BF16)
            sg = _sigmoid(gr)
            silu_g = gr * sg
            dsilu_g = sg * (1.0 + gr * (1.0 - sg))
            dqs, dks, d_lasts = [], [], []
            for h in range(HG_HEADS):
                cols = slice(h * HG_DK, (h + 1) * HG_DK)
                ov = o_ref[rows, cols]
                dyv = dy_ref[rows, cols].astype(F32)
                rstd = lax.rsqrt(jnp.mean(ov * ov, axis=-1, keepdims=True) + EPS)
                ohat = ov * rstd
                dp_ref[3, rows, cols] = (dyv * (ohat * wnv) * dsilu_g[:, cols]).astype(BF16)
                don = dyv * silu_g[:, cols]
                dwn_ref[...] += jnp.sum(don * ohat, axis=0, keepdims=True)
                gd = don * wnv
                do_b = (rstd * (gd - ohat * jnp.mean(gd * ohat, axis=-1, keepdims=True))).astype(BF16)
                st = st_ref[h, n]
                ds = ds_scr[h]
                ds_b = ds.astype(BF16)
                vh, kh = v[:, cols], k[:, cols]
                a_b = jnp.where(tril, _dot(qi[:, cols], ki[:, cols], NT), 0.0).astype(BF16)
                da_b = jnp.where(tril, _dot(do_b, vh, NT), 0.0).astype(BF16)
                dqs.append(_dot(do_b, st.astype(BF16), NN) * e_0[:, cols]
                           + (_dot(da_b, ki[:, cols], NN) + _dot(da_b, ki_lo[:, cols], NN)) * e_q[:, cols])
                dk_state = _dot(vh, ds_b, NN) * e_l[:, cols]
                dks.append((_dot(da_b, qi[:, cols], TN) + _dot(da_b, qi_lo[:, cols], TN)) * e_k[:, cols] + dk_state)
                dp_ref[2, rows, cols] = (_dot(a_b, do_b, TN) + _dot(kl[:, cols], ds_b, NT)).astype(BF16)
                ds_scr[h] = ds * e_last[:, cols] + _dot(do_b, q0[:, cols], TN)
                d_lasts.append(jnp.sum(dk_state * kh, axis=0, keepdims=True) + jnp.sum(ds * st, axis=0, keepdims=True) * e_last[:, cols])
            dq = jnp.concatenate(dqs, axis=1)
            dk = jnp.concatenate(dks, axis=1)
            dlogf = _tri_dot(tril_b, q * dq - k * dk, TN) + jnp.concatenate(d_lasts, axis=1)
            dfg = dlogf / fg - dk
            dlb_scr[...] += jnp.sum(dfg * (1.0 - sf), axis=0, keepdims=True)
            sq = _sigmoid(qr)
            dp_ref[0, rows, :] = (dq * (HG_DK ** -0.5) * (sq * (1.0 + qr * (1.0 - sq)))).astype(BF16)
            dp_ref[1, rows, :] = (dfg * (1.0 - lb_all) * sf * (1.0 - sf)).astype(BF16)
            return carry

        lax.fori_loop(0, cpb, chunk, 0)

        @pl.when(step == nb - 1)
        def _():
            d0 = dlb_scr[...] * lb_all * (1.0 - lb_all)
            dl_ref[0:1, :] = d0
            dl_ref[1:2, :] = -d0

    rev = lambda i: nb - 1 - i
    blk = pl.BlockSpec((rb, D_MODEL), lambda i: (rev(i), 0))
    return pl.pallas_call(
        body,
        name="hgrn_bwd",
        grid=(nb,),
        in_specs=_hg_proj_specs(rb, rev)
        + [pl.BlockSpec((2, D_MODEL), lambda i: (0, 0)), pl.BlockSpec((1, HG_DK), lambda i: (0, 0)), blk,
           pl.BlockSpec((HG_HEADS, cpb, HG_DK, HG_DK), lambda i: (0, rev(i), 0, 0)), blk],
        out_specs=[
            pl.BlockSpec((4, rb, D_MODEL), lambda i: (0, rev(i), 0)),
            pl.BlockSpec((2, D_MODEL), lambda i: (0, 0)),
            pl.BlockSpec((1, HG_DK), lambda i: (0, 0)),
        ],
        out_shape=[
            jax.ShapeDtypeStruct((4, t, D_MODEL), BF16),
            jax.ShapeDtypeStruct((2, D_MODEL), F32),
            jax.ShapeDtypeStruct((1, HG_DK), F32),
        ],
        scratch_shapes=[pltpu.VMEM((HG_HEADS, HG_DK, HG_DK), F32), pltpu.VMEM((1, D_MODEL), F32)],
        compiler_params=_cparams(("arbitrary",)),
    )(proj, proj, proj, proj, lb, wn, o, states, dy)


ATT_STACK = 8


def _att_stack(q_ref, sink_ref, first, lo, bias_p, bias_c, extra_ref=None):
    qs, bps, bcs, sinks, extras = [], [], [], None, []
    rows = lax.broadcasted_iota(jnp.int32, (ATT_STACK * WINDOW, 1), 0)
    for i in range(ATT_STACK):
        hq = first + i
        cols = slice((hq // 2) * LANES, (hq // 2 + 1) * LANES)
        sel = lo if hq % 2 == 0 else jnp.logical_not(lo)
        qp = q_ref[:, cols] * (ATT_HD ** -0.5)
        qs.append(jnp.where(sel, qp, jnp.zeros_like(qp)))
        bps.append(ALIBI_SLOPES[hq] * bias_p)
        bcs.append(ALIBI_SLOPES[hq] * bias_c)
        sinks = sink_ref[hq] if sinks is None else jnp.where(rows < i * WINDOW, sinks, sink_ref[hq])
        if extra_ref is not None:
            ep = extra_ref[:, cols]
            extras.append(jnp.where(sel, ep, jnp.zeros_like(ep)))
    cat = lambda parts: jnp.concatenate(parts, axis=0)
    return cat(qs), cat(bps), cat(bcs), sinks, (cat(extras) if extras else None)


def _att_rows(i):
    return slice(i * WINDOW, (i + 1) * WINDOW)


def _att_bias(n):
    tq = lax.broadcasted_iota(jnp.int32, (WINDOW, WINDOW), 0)
    sk = lax.broadcasted_iota(jnp.int32, (WINDOW, WINDOW), 1)
    valid_c = sk <= tq
    valid_p = (sk - tq) > jnp.where(n > 0, 0, WINDOW)
    dist_c = (tq - sk).astype(F32)
    return jnp.where(valid_p, -dist_c - float(WINDOW), NEG), jnp.where(valid_c, -dist_c, NEG)


def _att_halves(x, lo, kh):
    r = pltpu.roll(x, ATT_HD, 1)
    zero = jnp.zeros_like(x)
    if kh == 0:
        return jnp.where(lo, x, r), jnp.where(lo, x, zero), jnp.where(lo, zero, r)
    return jnp.where(lo, r, x), jnp.where(lo, r, zero), jnp.where(lo, zero, x)


def _att_probs(qm, k2p, k2c, bias_p, bias_c, sink):
    sp = _dot(qm, k2p, NT) + bias_p
    sc = _dot(qm, k2c, NT) + bias_c
    m = jnp.maximum(jnp.maximum(jnp.max(sp, axis=-1, keepdims=True), jnp.max(sc, axis=-1, keepdims=True)), sink)
    ep = jnp.exp(sp - m)
    ec = jnp.exp(sc - m)
    es = jnp.exp(sink - m)
    inv = 1.0 / (jnp.sum(ep, axis=-1, keepdims=True) + jnp.sum(ec, axis=-1, keepdims=True) + es)
    return ep * inv, ec * inv, es * inv


def _attn_fwd(q, kv, sinks):
    t = q.shape[0]
    nb = t // WINDOW

    def body(sink_ref, q_ref, kvp_ref, kvc_ref, o_ref):
        n = pl.program_id(0)
        bias_p, bias_c = _att_bias(n)
        lo = lax.broadcasted_iota(jnp.int32, (WINDOW, LANES), 1) < ATT_HD
        for kh in range(ATT_KVH):
            k2p, _, _ = _att_halves(kvp_ref[:, 0:LANES], lo, kh)
            k2c, _, _ = _att_halves(kvc_ref[:, 0:LANES], lo, kh)
            _, vlo_p, vhi_p = _att_halves(kvp_ref[:, LANES:2 * LANES], lo, kh)
            _, vlo_c, vhi_c = _att_halves(kvc_ref[:, LANES:2 * LANES], lo, kh)
            for first in range(kh * ATT_GROUP, (kh + 1) * ATT_GROUP, ATT_STACK):
                qs, bp, bc, sinks, _ = _att_stack(q_ref, sink_ref, first, lo, bias_p, bias_c)
                pp, pc, _ = _att_probs(qs, k2p, k2c, bp, bc, sinks)
                pp, pc = pp.astype(BF16), pc.astype(BF16)
                for i in range(0, ATT_STACK, 2):
                    even, odd = _att_rows(i), _att_rows(i + 1)
                    out = (_dot(pp[even], vlo_p, NN) + _dot(pc[even], vlo_c, NN)
                           + _dot(pp[odd], vhi_p, NN) + _dot(pc[odd], vhi_c, NN))
                    j = (first + i) // 2
                    o_ref[:, j * LANES:(j + 1) * LANES] = out.astype(BF16)

    return pl.pallas_call(
        body,
        name="attn_fwd",
        grid=(nb,),
        in_specs=[
            pl.BlockSpec(memory_space=pltpu.SMEM),
            pl.BlockSpec((WINDOW, D_MODEL), lambda n: (n, 0)),
            pl.BlockSpec((WINDOW, 2 * LANES), lambda n: (jnp.maximum(n - 1, 0), 0)),
            pl.BlockSpec((WINDOW, 2 * LANES), lambda n: (n, 0)),
        ],
        out_specs=pl.BlockSpec((WINDOW, D_MODEL), lambda n: (n, 0)),
        out_shape=jax.ShapeDtypeStruct((t, D_MODEL), BF16),
        compiler_params=_cparams(("parallel",)),
    )(sinks, q, kv, kv)


def _attn_bwd(q, kv, sinks, dout):
    t = q.shape[0]
    nb = t // WINDOW

    def body(sink_ref, q_ref, kvp_ref, kvc_ref, do_ref, dq_ref, dkv_ref, dsink_ref, carry_ref):
        n = pl.program_id(0)

        @pl.when(n == 0)
        def _():
            carry_ref[...] = jnp.zeros_like(carry_ref)
            dsink_ref[...] = jnp.zeros_like(dsink_ref)

        @pl.when(n == nb)
        def _():
            dkv_ref[...] = carry_ref[...].astype(BF16)

        @pl.when(n < nb)
        def _():
            bias_p, bias_c = _att_bias(n)
            lo = lax.broadcasted_iota(jnp.int32, (WINDOW, LANES), 1) < ATT_HD
            lane1 = lax.broadcasted_iota(jnp.int32, (1, LANES), 1)
            dsink = jnp.zeros((1, LANES), F32)
            halves = []
            for kh in range(ATT_KVH):
                k2p, klo_p, khi_p = _att_halves(kvp_ref[:, 0:LANES], lo, kh)
                k2c, klo_c, khi_c = _att_halves(kvc_ref[:, 0:LANES], lo, kh)
                v2p, _, _ = _att_halves(kvp_ref[:, LANES:2 * LANES], lo, kh)
                v2c, _, _ = _att_halves(kvc_ref[:, LANES:2 * LANES], lo, kh)
                acc = [jnp.zeros((WINDOW, LANES), F32) for _ in range(4)]
                for first in range(kh * ATT_GROUP, (kh + 1) * ATT_GROUP, ATT_STACK):
                    qs, bp, bc, sinks, dos = _att_stack(q_ref, sink_ref, first, lo, bias_p, bias_c, do_ref)
                    pp, pc, ps = _att_probs(qs, k2p, k2c, bp, bc, sinks)
                    dpp = _dot(dos, v2p, NT)
                    dpc = _dot(dos, v2c, NT)
                    delta = jnp.sum(pp * dpp, axis=-1, keepdims=True) + jnp.sum(pc * dpc, axis=-1, keepdims=True)
                    dsp = (pp * (dpp - delta)).astype(BF16)
                    dsc = (pc * (dpc - delta)).astype(BF16)
                    sink_term = ps * delta
                    for i in range(ATT_STACK):
                        dsink = dsink + jnp.where(lane1 == first + i, -jnp.sum(sink_term[_att_rows(i)], axis=0, keepdims=True), 0.0)
                    for i in range(0, ATT_STACK, 2):
                        even, odd = _att_rows(i), _att_rows(i + 1)
                        dq_pair = (_dot(dsp[even], klo_p, NN) + _dot(dsc[even], klo_c, NN)
                                   + _dot(dsp[odd], khi_p, NN) + _dot(dsc[odd], khi_c, NN))
                        j = (first + i) // 2
                        dq_ref[:, j * LANES:(j + 1) * LANES] = (dq_pair * (ATT_HD ** -0.5)).astype(BF16)
                    acc[0] = acc[0] + _dot(dsp, qs, TN)
                    acc[1] = acc[1] + _dot(dsc, qs, TN)
                    acc[2] = acc[2] + _dot(pp.astype(BF16), dos, TN)
                    acc[3] = acc[3] + _dot(pc.astype(BF16), dos, TN)
                halves.append([a + pltpu.roll(a, ATT_HD, 1) for a in acc])
            prev = jnp.concatenate(
                [jnp.where(lo, halves[0][0], halves[1][0]), jnp.where(lo, halves[0][2], halves[1][2])], axis=1)
            cur = jnp.concatenate(
                [jnp.where(lo, halves[0][1], halves[1][1]), jnp.where(lo, halves[0][3], halves[1][3])], axis=1)
            dkv_ref[...] = (carry_ref[...] + prev).astype(BF16)
            carry_ref[...] = cur
            dsink_ref[...] += dsink

    blk = lambda n: jnp.minimum(n, nb - 1)
    return pl.pallas_call(
        body,
        name="attn_bwd",
        grid=(nb + 1,),
        in_specs=[
            pl.BlockSpec(memory_space=pltpu.SMEM),
            pl.BlockSpec((WINDOW, D_MODEL), lambda n: (blk(n), 0)),
            pl.BlockSpec((WINDOW, 2 * LANES), lambda n: (jnp.maximum(blk(n) - 1, 0), 0)),
            pl.BlockSpec((WINDOW, 2 * LANES), lambda n: (blk(n), 0)),
            pl.BlockSpec((WINDOW, D_MODEL), lambda n: (blk(n), 0)),
        ],
        out_specs=[
            pl.BlockSpec((WINDOW, D_MODEL), lambda n: (blk(n), 0)),
            pl.BlockSpec((WINDOW, 2 * LANES), lambda n: (jnp.maximum(n - 1, 0), 0)),
            pl.BlockSpec((1, LANES), lambda n: (0, 0)),
        ],
        out_shape=[
            jax.ShapeDtypeStruct((t, D_MODEL), BF16),
            jax.ShapeDtypeStruct((t, 2 * LANES), BF16),
            jax.ShapeDtypeStruct((1, LANES), F32),
        ],
        scratch_shapes=[pltpu.VMEM((WINDOW, 2 * LANES), F32)],
        compiler_params=_cparams(("arbitrary",)),
    )(sinks, q, kv, kv, dout)


def _ffn_fwd(h, norm_g, w_up, conv_w, conv_b, w_down, tag, after_up=lambda up: None):
    up = _mm_nn(h, w_up, gain=norm_g, name=f"ffn{tag}_up")
    after_up(up)
    act, c = _conv_fwd(up, conv_w, conv_b, name=f"ffn{tag}_conv")
    h_out = _mm_nn(act, w_down, res=h, name=f"ffn{tag}_down")
    return h_out, (up, act, c)


def _ffn_bwd(dh, h, norm_g, w_up, conv_w, conv_b, w_down, saved, tag, deps=()):
    up, act, c = saved
    dw_down = _mm_tn(act, dh, 1, D_MODEL, name=f"ffn{tag}_dwdown", deps=deps)
    dact = _mm_nt(dh, w_down, name=f"ffn{tag}_dact", deps=deps)
    dup, dconv_w, dconv_b = _conv_bwd(up, conv_w, c, dact, name=f"ffn{tag}_dconv")
    dw_up = _mm_tn(h, dup, N_CHIPS, CONV_COLS, stacked=True, gain=norm_g, name=f"ffn{tag}_dwup")
    dh_in, dnorm = _mm_nt(dup, w_up, stacked=True, norm_of=(h, norm_g, dh), name=f"ffn{tag}_dxn")
    return dh_in, dict(ffn_w_down=dw_down, ffn_w_up=dw_up, ffn_conv_w=dconv_w, ffn_conv_b=dconv_b, ffn_norm=dnorm)


def _local_step(x, target, w, fetch=lambda w, stage, after: w, hook=lambda point, dh, grads: ()):
    proj = _mm_nn(x, w["hg_w_in"], gain=w["hg_norm"], name="hg_in")
    o, y, states = _hgrn_fwd(proj, w["hg_lb"], w["hg_out_norm"])
    w = fetch(w, "mixer_out", y)
    fetch(w, "layer0_relay", y)
    h_a = _mm_nn(y, w["hg_w_out"], res=x, name="hg_out")
    w = fetch(w, "layer0", h_a)
    h1, ffn0 = _ffn_fwd(h_a, w["ffn_norm"][0], w["ffn_w_up"][0], w["ffn_conv_w"][0], w["ffn_conv_b"][0], w["ffn_w_down"][0], 0,
                        lambda up: fetch(w, "layer1_relay", up))
    w = fetch(w, "layer1", h1)
    kv = _mm_nn(h1, w["w_kv"], gain=w["kv_norm"], out_dtype=BF16, name="kv_proj")
    qa = _mm_nn(h1, w["attn_w_q"], gain=w["attn_norm"], out_dtype=BF16, name="attn_q")
    ao = _attn_fwd(qa, kv, w["attn_sinks"])
    h_b = _mm_nn(ao, w["attn_w_o"], res=h1, name="attn_o")
    h2, ffn1 = _ffn_fwd(h_b, w["ffn_norm"][1], w["ffn_w_up"][1], w["ffn_conv_w"][1], w["ffn_conv_b"][1], w["ffn_w_down"][1], 1)
    dh2, d_final, loss = _loss_head(h2, w["final_norm"], target)

    dh_b, g1 = _ffn_bwd(dh2, h_b, w["ffn_norm"][1], w["ffn_w_up"][1], w["ffn_conv_w"][1], w["ffn_conv_b"][1], w["ffn_w_down"][1], ffn1, 1)
    deps = hook("ffn1", dh_b, g1)
    dw_o = _mm_tn(ao, dh_b, 1, D_MODEL, name="attn_dwo", deps=deps)
    dao = _mm_nt(dh_b, w["attn_w_o"], out_dtype=BF16, name="attn_dao", deps=deps)
    dqa, dkv, dsinks = _attn_bwd(qa, kv, w["attn_sinks"], dao)
    dw_q = _mm_tn(h1, dqa, 1, D_MODEL, gain=w["attn_norm"], name="attn_dwq")
    dh1, d_attn_norm = _mm_nt(dqa, w["attn_w_q"], norm_of=(h1, w["attn_norm"], dh_b), name="attn_dxa")
    dw_kv = _mm_tn(h1, dkv, 1, 2 * LANES, gain=w["kv_norm"], name="kv_dw")
    dh1, d_kv_norm = _mm_nt(dkv, w["w_kv"], norm_of=(h1, w["kv_norm"], dh1), name="kv_dx")
    deps = hook("attn", dh1, dict(attn_w_o=dw_o, attn_w_q=dw_q, w_kv=dw_kv))
    dh_a, g0 = _ffn_bwd(dh1, h_a, w["ffn_norm"][0], w["ffn_w_up"][0], w["ffn_conv_w"][0], w["ffn_conv_b"][0], w["ffn_w_down"][0], ffn0, 0, deps)
    dw_out = _mm_tn(y, dh_a, 1, D_MODEL, name="hg_dwout")
    deps = hook("ffn0", dh_a, dict(g0, hg_w_out=dw_out))
    dy = _mm_nt(dh_a, w["hg_w_out"], out_dtype=BF16, name="hg_dy", deps=deps)
    dproj, dlb, d_out_norm = _hgrn_bwd(proj, w["hg_lb"], w["hg_out_norm"], o, states, dy)
    deps = hook("hgrn", dproj, None)
    dw_in = _mm_tn(x, dproj, N_CHIPS, D_MODEL, stacked=True, gain=w["hg_norm"], name="hg_dwin", deps=deps)
    deps = hook("hg_w", dproj, dict(hg_w_in=dw_in))
    dx, d_hg_norm = _mm_nt(dproj, w["hg_w_in"], stacked=True, norm_of=(x, w["hg_norm"], dh_a), name="hg_dxn", deps=deps)

    grads = dict(
        hg_norm=d_hg_norm, hg_w_in=dw_in, hg_lb=dlb, hg_out_norm=d_out_norm, hg_w_out=dw_out,
        kv_norm=d_kv_norm, w_kv=dw_kv, attn_norm=d_attn_norm, attn_w_q=dw_q, attn_sinks=dsinks, attn_w_o=dw_o,
        final_norm=d_final,
    )
    for name in g0:
        grads[name] = [g0[name], g1[name]]
    return loss, dx, grads


ANY = pl.BlockSpec(memory_space=pl.ANY)


def _place():
    x, y, c = lax.axis_index("x"), lax.axis_index("y"), lax.axis_index("c")
    chips = [(1 - x, y), (x, 1 - y), (1 - x, 1 - y)]
    return x, y, c, chips


def _rcopy(src, dst, send_sem, recv_sem, to):
    return pltpu.make_async_remote_copy(src_ref=src, dst_ref=dst, send_sem=send_sem, recv_sem=recv_sem, device_id=to, device_id_type=MESH)


HBM = pl.BlockSpec(memory_space=pltpu.HBM)
SEM = pl.BlockSpec(memory_space=pltpu.SEMAPHORE)
EFFECT = pltpu.SideEffectType.DATAFLOW_SIDE_EFFECTING


def _in_hbm(a):
    return pltpu.with_memory_space_constraint(a, pltpu.HBM)


def _place_shard(shard, place, dtype, name, deps=(), layer=None):
    r, cols = shard.shape[-2:]
    tr = _pick(r, ELEM_ROWS)
    src = pl.BlockSpec((tr, cols), lambda i, place_ref: (i, 0)) if layer is None else pl.BlockSpec((None, tr, cols), lambda i, place_ref: (layer, i, 0))

    def body(place_ref, s_ref, *rest):
        o_ref = rest[-1]
        o_ref[...] = s_ref[...].astype(o_ref.dtype)

    return pl.pallas_call(
        body,
        name=name,
        grid_spec=pltpu.PrefetchScalarGridSpec(
            num_scalar_prefetch=1,
            grid=(r // tr,),
            in_specs=[src] + _dep_specs(deps),
            out_specs=pl.BlockSpec((None, tr, cols), lambda i, place_ref: (place_ref[0], i, 0)),
        ),
        out_shape=jax.ShapeDtypeStruct((N_CHIPS, r, cols), dtype),
        compiler_params=_cparams(("parallel",)),
    )(place, shard, *deps)


def _start_copies(name, bufs, n_sem, copies):
    n = len(bufs)

    def body(*refs):
        for cp in copies(refs[:n], refs[n], refs[n + 1]):
            cp.start()
        refs[-1][...] = jnp.zeros_like(refs[-1])

    outs = pl.pallas_call(
        body,
        name=name,
        in_specs=[HBM] * n,
        out_specs=[SEM, SEM] + [HBM] * n + [pl.BlockSpec(memory_space=pltpu.VMEM)],
        out_shape=[pltpu.SemaphoreType.DMA((n_sem,)), pltpu.SemaphoreType.DMA((n_sem,))] + [pltpu.HBM(b.shape, b.dtype) for b in bufs]
        + [jax.ShapeDtypeStruct((SUBLANES, LANES), F32)],
        input_output_aliases={i: 2 + i for i in range(n)},
        compiler_params=pltpu.CompilerParams(has_side_effects=EFFECT),
    )(*[_in_hbm(b) for b in bufs])
    return outs[0], outs[1], list(outs[2:-1]), outs[-1]


def _wait_copies(name, bufs, send_sems, recv_sems, after, copies):
    n = len(bufs)

    def body(*refs):
        for cp in copies(refs[:n], refs[n], refs[n + 1]):
            cp.wait_send()
            cp.wait_recv()

    return pl.pallas_call(
        body,
        name=name,
        in_specs=[HBM] * n + [SEM, SEM, ANY],
        out_specs=[HBM] * n,
        out_shape=[pltpu.HBM(b.shape, b.dtype) for b in bufs],
        input_output_aliases={i: i for i in range(n)},
        compiler_params=pltpu.CompilerParams(has_side_effects=EFFECT),
    )(*bufs, send_sems, recv_sems, after)


def _relay_copies(name, bufs, send_sems, recv_sems, after, landed, n_sem, onward):
    n = len(bufs)

    def body(*refs):
        for cp in landed(refs[:n], refs[n], refs[n + 1]):
            cp.wait_send()
            cp.wait_recv()
        for cp in onward(refs[:n], refs[n + 3], refs[n + 4]):
            cp.start()
        refs[-1][...] = jnp.zeros_like(refs[-1])

    outs = pl.pallas_call(
        body,
        name=name,
        in_specs=[HBM] * n + [SEM, SEM, ANY],
        out_specs=[SEM, SEM] + [HBM] * n + [pl.BlockSpec(memory_space=pltpu.VMEM)],
        out_shape=[pltpu.SemaphoreType.DMA((n_sem,)), pltpu.SemaphoreType.DMA((n_sem,))] + [pltpu.HBM(b.shape, b.dtype) for b in bufs]
        + [jax.ShapeDtypeStruct((SUBLANES, LANES), F32)],
        input_output_aliases={i: 2 + i for i in range(n)},
        compiler_params=pltpu.CompilerParams(has_side_effects=EFFECT),
    )(*bufs, send_sems, recv_sems, after)
    return outs[0], outs[1], list(outs[2:-1]), outs[-1]


def _gather_half_copies(first, count, over_ici):
    def copies(refs, send_sems, recv_sems):
        x, y, c, chips = _place()
        out = []
        for i in range(count):
            h = refs[i].shape[1] // 2
            mine = pl.ds(c * h, h)
            for j, (px, py) in enumerate(chips):
                k = 3 * (first + i) + j
                slot = 2 * x + y if over_ici else 2 * px + py
                to = (px, py, c) if over_ici else (x, y, 1 - c)
                out.append(_rcopy(refs[i].at[slot, mine], refs[i].at[slot, mine], send_sems.at[k], recv_sems.at[k], to))
        return out

    return copies


def _gather_copies(first, count):
    def copies(refs, send_sems, recv_sems):
        x, y, c, chips = _place()
        me = 2 * x + y
        out = []
        for i in range(count):
            for j, (px, py) in enumerate(chips):
                k = 3 * (first + i) + j
                out.append(_rcopy(refs[i].at[me], refs[i].at[me], send_sems.at[k], recv_sems.at[k], (px, py, c)))
        return out

    return copies


def _swap_copies(n):
    def copies(refs, send_sems, recv_sems):
        x, y, c, _ = _place()
        out = []
        for i in range(n):
            h = refs[i].shape[1] // 2
            out.append(_rcopy(refs[i].at[:, pl.ds((1 - c) * h, h)], refs[n + i], send_sems.at[i], recv_sems.at[i], (x, y, 1 - c)))
        return out

    return copies


def _partial_copies(n):
    def copies(refs, send_sems, recv_sems):
        x, y, c, chips = _place()
        out = []
        for i in range(n):
            for j, (px, py) in enumerate(chips):
                out.append(_rcopy(refs[i].at[2 * px + py], refs[n + i].at[j], send_sems.at[3 * i + j], recv_sems.at[3 * i + j], (px, py, c)))
        return out

    return copies


def _share_copies(n):
    def copies(refs, send_sems, recv_sems):
        x, y, c, _ = _place()
        return [_rcopy(refs[i].at[c], refs[i].at[c], send_sems.at[i], recv_sems.at[i], (x, y, 1 - c)) for i in range(n)]

    return copies


def _allreduce_small(groups, widths):
    flat = [a for g in groups for a in g]
    n = len(flat)
    rows = -(-sum(a.shape[0] for a in flat) // SUBLANES) * SUBLANES
    cols = max(a.shape[1] for a in flat)
    out_shapes = [(sum(a.shape[0] for a in g), wd or g[0].shape[1]) for g, wd in zip(groups, widths)]

    def body(*refs):
        ins, outs = refs[:n], refs[n:n + len(groups)]
        mine, buf, send_sems, recv_sems = refs[n + len(groups):]
        x, y, c, _ = _place()
        me = 4 * x + 2 * y + c
        mine[...] = jnp.zeros_like(mine)
        r0 = 0
        for a_ref in ins:
            r, w = a_ref.shape
            mine[r0:r0 + r, 0:w] = a_ref[...]
            r0 += r
        buf[me] = mine[...]
        copies = []
        for k in range(1, N_DEV):
            peer = (x ^ (k >> 2), y ^ ((k >> 1) & 1), c ^ (k & 1))
            cp = _rcopy(mine, buf.at[me], send_sems.at[k - 1], recv_sems.at[k - 1], peer)
            cp.start()
            copies.append(cp)
        for cp in copies:
            cp.wait()
        acc = buf[0]
        for d in range(1, N_DEV):
            acc = acc + buf[d]
        mine[...] = acc
        r0 = 0
        for o_ref in outs:
            r, w = o_ref.shape
            o_ref[...] = mine[r0:r0 + r, 0:w]
            r0 += r

    vmem = pl.BlockSpec(memory_space=pltpu.VMEM)
    return pl.pallas_call(
        body,
        name="allreduce_small",
        in_specs=[vmem] * n,
        out_specs=[vmem] * len(groups),
        out_shape=[jax.ShapeDtypeStruct(s, F32) for s in out_shapes],
        scratch_shapes=[pltpu.VMEM((rows, cols), F32), pltpu.VMEM((N_DEV, rows, cols), F32),
                        pltpu.SemaphoreType.DMA((N_DEV - 1,)), pltpu.SemaphoreType.DMA((N_DEV - 1,))],
        compiler_params=pltpu.CompilerParams(vmem_limit_bytes=VMEM_LIMIT_BYTES),
    )(*flat)


def _adamw_small(items):
    n = len(items)

    def body(*refs):
        for i in range(n):
            w_ref, m_ref, v_ref, g_ref = refs[4 * i:4 * i + 4]
            d_ref, nm_ref, nv_ref = refs[4 * n + 3 * i:4 * n + 3 * i + 3]
            d_ref[...], nm_ref[...], nv_ref[...] = _adamw_math(w_ref[...], m_ref[...], v_ref[...], g_ref[...])

    vmem = pl.BlockSpec(memory_space=pltpu.VMEM)
    outs = pl.pallas_call(
        body,
        name="adamw_small",
        in_specs=[vmem] * (4 * n),
        out_specs=[vmem] * (3 * n),
        out_shape=[jax.ShapeDtypeStruct(it[0].shape, F32) for it in items for _ in range(3)],
        compiler_params=pltpu.CompilerParams(vmem_limit_bytes=VMEM_LIMIT_BYTES),
    )(*[a for it in items for a in it])
    return [tuple(outs[3 * i:3 * i + 3]) for i in range(n)]


class _Reduction:
    def __init__(self, tag, grads, place):
        self.tag, self.n, self.place = tag, len(grads), place
        lands = [lax.empty((N_CHIPS, g.shape[1] // 2, g.shape[2]), F32) for g in grads]
        self._start("swap", list(grads) + lands, self.n, _swap_copies(self.n))

    def _start(self, stage, bufs, n_sem, copies):
        *self.flight, self.token = _start_copies(f"rs_{stage}_start_{self.tag}", bufs, n_sem, copies)

    def _landed(self, stage, after, copies):
        send_sems, recv_sems, bufs = self.flight
        return _wait_copies(f"rs_{stage}_wait_{self.tag}", bufs, send_sems, recv_sems, after, copies)

    def to_chips(self, after):
        n = self.n
        bufs = self._landed("swap", after, _swap_copies(n))
        sums = [_add_core_halves(g, o, self.place, name=f"rs_add_core_{self.tag}_{i}") for i, (g, o) in enumerate(zip(bufs[:n], bufs[n:]))]
        self.mine = [f for f, _ in sums]
        parts = [b for _, b in sums]
        lands = [lax.empty((3,) + p.shape[1:], BF16) for p in parts]
        self._start("send", parts + lands, 3 * n, _partial_copies(n))

    def to_core(self, after):
        n = self.n
        bufs = self._landed("send", after, _partial_copies(n))
        halves = [_add_chip_partials(f, o, self.place, name=f"rs_add_chip_{self.tag}_{i}") for i, (f, o) in enumerate(zip(self.mine, bufs[n:]))]
        self._start("share", halves, n, _share_copies(n))

    def finish(self, after):
        return [b.reshape((-1,) + b.shape[2:]) for b in self._landed("share", after, _share_copies(self.n))]


ELEM_ROWS = (256, 176, 128, 64, 32, 16, 8)


def _add_core_halves(grad, got, place, name):
    s, r, cols = grad.shape
    h = r // 2
    tr = _pick(h, ELEM_ROWS)

    def body(place_ref, g_ref, o_ref, f_ref, b_ref):
        acc = g_ref[...] + o_ref[...]
        b_ref[...] = acc.astype(BF16)

        @pl.when(pl.program_id(1) == place_ref[0])
        def _():
            f_ref[...] = acc

    blk = pl.BlockSpec((None, tr, cols), lambda i, k, place_ref: (k, i, 0))
    return pl.pallas_call(
        body,
        name=name,
        grid_spec=pltpu.PrefetchScalarGridSpec(
            num_scalar_prefetch=1,
            grid=(h // tr, s),
            in_specs=[pl.BlockSpec((None, None, tr, cols), lambda i, k, place_ref: (k, place_ref[1], i, 0)), blk],
            out_specs=[pl.BlockSpec((tr, cols), lambda i, k, place_ref: (i, 0)), blk],
        ),
        out_shape=[jax.ShapeDtypeStruct((h, cols), F32), jax.ShapeDtypeStruct((s, h, cols), BF16)],
        compiler_params=_cparams(("parallel", "arbitrary")),
    )(place, grad.reshape(s, 2, h, cols), got)


def _add_chip_partials(mine, got, place, name):
    h, cols = mine.shape
    tr = _pick(h, ELEM_ROWS)

    def body(place_ref, m_ref, g_ref, o_ref):
        acc = m_ref[...]
        for j in range(3):
            acc = acc + g_ref[j].astype(F32)
        o_ref[...] = acc

    return pl.pallas_call(
        body,
        name=name,
        grid_spec=pltpu.PrefetchScalarGridSpec(
            num_scalar_prefetch=1,
            grid=(h // tr,),
            in_specs=[
                pl.BlockSpec((tr, cols), lambda i, place_ref: (i, 0)),
                pl.BlockSpec((3, tr, cols), lambda i, place_ref: (0, i, 0)),
            ],
            out_specs=pl.BlockSpec((None, tr, cols), lambda i, place_ref: (place_ref[1], i, 0)),
        ),
        out_shape=jax.ShapeDtypeStruct((2, h, cols), F32),
        compiler_params=_cparams(("parallel",)),
    )(place, mine, got)


def _adamw_math(w, m, v, g):
    nm = ADAM_B1 * m + (1.0 - ADAM_B1) * g
    nv = ADAM_B2 * v + (1.0 - ADAM_B2) * (g * g)
    m_hat = nm * (1.0 / (1.0 - ADAM_B1 ** ADAM_STEP))
    v_hat = nv * (1.0 / (1.0 - ADAM_B2 ** ADAM_STEP))
    return -ADAM_LR * (m_hat / (jnp.sqrt(v_hat) + ADAM_EPS) + ADAM_WD * w), nm, nv


def _adamw_layer(w, m, v, g, layer, prev, name):
    nl, r, cols = w.shape
    tr = _pick(r, ELEM_ROWS)

    def body(w_ref, m_ref, v_ref, g_ref, *rest):
        go_ref, d_ref, nm_ref, nv_ref = rest[-4:]
        gv = g_ref[...]
        d_ref[...], nm_ref[...], nv_ref[...] = _adamw_math(w_ref[...], m_ref[...], v_ref[...], gv)
        go_ref[...] = gv

    lay = pl.BlockSpec((None, tr, cols), lambda i: (layer, i, 0))
    return pl.pallas_call(
        body,
        name=name,
        grid=(r // tr,),
        in_specs=[lay] * 3 + [pl.BlockSpec((tr, cols), lambda i: (i, 0))] + ([ANY] * 4 if prev else []),
        out_specs=[lay] * 4,
        out_shape=[jax.ShapeDtypeStruct((nl, r, cols), F32)] * 4,
        input_output_aliases={4 + k: k for k in range(4)} if prev else {},
        compiler_params=_cparams(("parallel",)),
    )(w, m, v, g, *(prev or ()))


def _adamw(w, m, v, g, name):
    r, cols = w.shape
    tr = _pick(r, ELEM_ROWS)

    def body(w_ref, m_ref, v_ref, g_ref, d_ref, nm_ref, nv_ref):
        d_ref[...], nm_ref[...], nv_ref[...] = _adamw_math(w_ref[...], m_ref[...], v_ref[...], g_ref[...])

    blk = pl.BlockSpec((tr, cols), lambda i: (i, 0))
    return pl.pallas_call(
        body,
        name=name,
        grid=(r // tr,),
        in_specs=[blk] * 4,
        out_specs=[blk] * 3,
        out_shape=[jax.ShapeDtypeStruct((r, cols), F32)] * 3,
        compiler_params=_cparams(("parallel",)),
    )(w, m, v, g)


SMALL_COLS = 384
SMALL_ROWS = 16


def _pad_rows(flat, rows, cols):
    return jnp.pad(flat, (0, rows * cols - flat.shape[0])).reshape(rows, cols)


def kernel(x, hg_norm, hg_w_in, hg_lb_logits, hg_out_norm, hg_w_out, kv_norm, w_kv, attn_norm, attn_w_q, attn_sinks, attn_w_o, ffn_norm, ffn_w_up, ffn_conv_w, ffn_conv_b, ffn_w_down, final_norm, loss_target, m_hg_norm, m_hg_w_in, m_hg_lb_logits, m_hg_out_norm, m_hg_w_out, m_kv_norm, m_w_kv, m_attn_norm, m_attn_w_q, m_attn_sinks, m_attn_w_o, m_ffn_norm, m_ffn_w_up, m_ffn_conv_w, m_ffn_conv_b, m_ffn_w_down, m_final_norm, v_hg_norm, v_hg_w_in, v_hg_lb_logits, v_hg_out_norm, v_hg_w_out, v_kv_norm, v_w_kv, v_attn_norm, v_attn_w_q, v_attn_sinks, v_attn_w_o, v_ffn_norm, v_ffn_w_up, v_ffn_conv_w, v_ffn_conv_b, v_ffn_w_down, v_final_norm):
    wts = dict(hg_norm=hg_norm, hg_w_in=hg_w_in, hg_lb_logits=hg_lb_logits, hg_out_norm=hg_out_norm, hg_w_out=hg_w_out, kv_norm=kv_norm, w_kv=w_kv, attn_norm=attn_norm, attn_w_q=attn_w_q, attn_sinks=attn_sinks, attn_w_o=attn_w_o, ffn_norm=ffn_norm, ffn_w_up=ffn_w_up, ffn_conv_w=ffn_conv_w, ffn_conv_b=ffn_conv_b, ffn_w_down=ffn_w_down, final_norm=final_norm)
    mom1 = dict(hg_norm=m_hg_norm, hg_w_in=m_hg_w_in, hg_lb_logits=m_hg_lb_logits, hg_out_norm=m_hg_out_norm, hg_w_out=m_hg_w_out, kv_norm=m_kv_norm, w_kv=m_w_kv, attn_norm=m_attn_norm, attn_w_q=m_attn_w_q, attn_sinks=m_attn_sinks, attn_w_o=m_attn_w_o, ffn_norm=m_ffn_norm, ffn_w_up=m_ffn_w_up, ffn_conv_w=m_ffn_conv_w, ffn_conv_b=m_ffn_conv_b, ffn_w_down=m_ffn_w_down, final_norm=m_final_norm)
    mom2 = dict(hg_norm=v_hg_norm, hg_w_in=v_hg_w_in, hg_lb_logits=v_hg_lb_logits, hg_out_norm=v_hg_out_norm, hg_w_out=v_hg_w_out, kv_norm=v_kv_norm, w_kv=v_w_kv, attn_norm=v_attn_norm, attn_w_q=v_attn_w_q, attn_sinks=v_attn_sinks, attn_w_o=v_attn_w_o, ffn_norm=v_ffn_norm, ffn_w_up=v_ffn_w_up, ffn_conv_w=v_ffn_conv_w, ffn_conv_b=v_ffn_conv_b, ffn_w_down=v_ffn_w_down, final_norm=v_final_norm)
    names = list(wts)
    chip = 2 * lax.axis_index("x") + lax.axis_index("y")
    core = lax.axis_index("c")
    fs = D_FF // N_CHIPS
    ds = D_MODEL // N_CHIPS

    place_arr = jnp.stack([chip, core]).astype(jnp.int32)
    small = jnp.concatenate([hg_norm.reshape(-1), hg_lb_logits.reshape(-1), ffn_conv_w.reshape(-1)])
    n_small = small.shape[0]
    shards = [
        ("small", _pad_rows(small, SMALL_ROWS, SMALL_COLS), F32, None), ("hg_w_in", hg_w_in, BF16, 0),
        ("hg_w_out", hg_w_out, BF16, 0), ("ffn_w_up0", ffn_w_up, BF16, 0), ("ffn_w_down0", ffn_w_down, BF16, 0),
        ("w_kv", w_kv, BF16, None), ("attn_w_q", attn_w_q, BF16, 0), ("attn_w_o", attn_w_o, BF16, 0),
        ("ffn_w_up1", ffn_w_up, BF16, 1), ("ffn_w_down1", ffn_w_down, BF16, 1),
    ]
    n_first = 3
    spans = dict(layer0=(0, 2), layer1=(2, 7))

    def first_copies(refs, send_sems, recv_sems):
        return (_gather_copies(0, 1)(refs[:1], send_sems, recv_sems) + _gather_half_copies(1, 1, True)(refs[1:2], send_sems, recv_sems)
                + _gather_copies(2, 1)(refs[2:3], send_sems, recv_sems))

    placed = [_place_shard(s, place_arr, dt, name=f"place_{nm}", layer=ly) for nm, s, dt, ly in shards[:n_first]]
    first = _start_copies("gather_start_first", placed, 3 * n_first, first_copies)
    placed = [_place_shard(s, place_arr, dt, name=f"place_{nm}", deps=(first[3],), layer=ly) for nm, s, dt, ly in shards[n_first:]]
    rest = _start_copies("gather_start_rest", placed, 3 * len(placed), _gather_half_copies(0, len(placed), True))
    relayed = {}

    def fetch(w, stage, after):
        if stage == "first":
            w_in = _relay_copies("gather_first_relay", first[2][1:2], first[0], first[1], after,
                                 _gather_half_copies(1, 1, True), 3, _gather_half_copies(0, 1, False))
            got = _wait_copies("gather_wait_small", first[2][:1], first[0], first[1], w_in[3], _gather_copies(0, 1))
            got += _wait_copies("gather_wait_first", w_in[2], w_in[0], w_in[1], got[0], _gather_half_copies(0, 1, False))
        elif stage == "mixer_out":
            got = _wait_copies("gather_wait_mixer_out", first[2][2:], first[0], first[1], after, _gather_copies(2, 1))
        elif stage.endswith("_relay"):
            lo, hi = spans[stage[:-6]]
            relayed[stage[:-6]] = _relay_copies(
                f"gather_{stage}", rest[2][lo:hi], rest[0], rest[1], after,
                _gather_half_copies(lo, hi - lo, True), 3 * (hi - lo), _gather_half_copies(0, hi - lo, False))
            return w
        else:
            lo, hi = spans[stage]
            send_sems, recv_sems, bufs, _ = relayed[stage]
            got = _wait_copies(f"gather_wait_{stage}", bufs, send_sems, recv_sems, after, _gather_half_copies(0, hi - lo, False))
        w = dict(w)
        if stage == "first":
            g_small = got[0].reshape(N_CHIPS, -1)[:, :n_small]
            conv_w = g_small[:, 3 * ds:].reshape(N_CHIPS, 2, 3, fs).transpose(1, 2, 0, 3).reshape(2, 3, D_FF)
            w.update(
                hg_norm=g_small[:, :ds].reshape(1, D_MODEL),
                hg_lb=g_small[:, ds:3 * ds].reshape(N_CHIPS, 2, ds).transpose(1, 0, 2).reshape(2, D_MODEL),
                ffn_conv_w=[conv_w[0], conv_w[1]], hg_w_in=got[1],
            )
        elif stage == "mixer_out":
            w.update(hg_w_out=got[0].reshape(1, D_MODEL, D_MODEL))
        elif stage == "layer0":
            w.update(ffn_w_up=[got[0], None], ffn_w_down=[got[1].reshape(1, D_FF, D_MODEL), None])
        else:
            w.update(
                w_kv=got[0].reshape(1, D_MODEL, 2 * LANES), attn_w_q=got[1].reshape(1, D_MODEL, D_MODEL),
                attn_w_o=got[2].reshape(1, D_MODEL, D_MODEL), ffn_w_up=[w["ffn_w_up"][0], got[3]],
                ffn_w_down=[w["ffn_w_down"][0], got[4].reshape(1, D_FF, D_MODEL)],
            )
        return w

    whole = dict(
        hg_out_norm=hg_out_norm, kv_norm=kv_norm.reshape(1, D_MODEL), attn_norm=attn_norm, attn_sinks=attn_sinks.reshape(ATT_QH),
        ffn_norm=[ffn_norm[0:1], ffn_norm[1:2]], ffn_conv_b=[ffn_conv_b[0:1], ffn_conv_b[1:2]], final_norm=final_norm.reshape(1, D_MODEL),
    )
    whole = fetch(whole, "first", rest[3])

    red, layer1 = {}, {}

    def by_rows(g, rows):
        return g.reshape(N_CHIPS, rows, g.shape[2])

    def hook(point, dh, grads):
        if point == "ffn1":
            red["ffn1"] = _Reduction("ffn1", [by_rows(grads["ffn_w_down"], fs), grads["ffn_w_up"]], place_arr)
            return (red["ffn1"].token,)
        if point == "attn":
            red["ffn1"].to_chips(dh)
            layer1.update(grads)
            return (red["ffn1"].token,)
        if point == "ffn0":
            group = [by_rows(layer1["attn_w_o"], ds), by_rows(layer1["attn_w_q"], ds), by_rows(layer1["w_kv"], ds),
                     by_rows(grads["ffn_w_down"], fs), grads["ffn_w_up"], by_rows(grads["hg_w_out"], ds)]
            red["mid"] = _Reduction("mid", group, place_arr)
            return (red["mid"].token,)
        if point == "hgrn":
            red["ffn1"].to_core(dh)
            red["mid"].to_chips(dh)
            return (red["ffn1"].token, red["mid"].token)
        red["hg"] = _Reduction("hg", [grads["hg_w_in"]], place_arr)
        return (red["hg"].token,)

    loss, dx, grads = _local_step(x[0], loss_target[0], whole, fetch, hook)

    small_names = ["hg_out_norm", "attn_sinks", "kv_norm", "attn_norm", "ffn_norm", "ffn_conv_b", "final_norm", "hg_norm", "hg_lb_logits", "ffn_conv_w"]
    groups = [[loss]] + [grads[n] if isinstance(grads[n], list) else [grads[n]] for n in small_names[:-2]] + [[grads["hg_lb"]], grads["ffn_conv_w"]]
    summed = _allreduce_small(groups, [None, None, ATT_QH] + [None] * 8)
    red["hg"].to_chips(summed[1])
    loss_out = summed[0][0, 0]
    small_grads = dict(zip(small_names, summed[1:]))
    small_grads["hg_norm"] = lax.dynamic_slice(small_grads["hg_norm"], (0, chip * ds), (1, ds))
    small_grads["hg_lb_logits"] = lax.dynamic_slice(small_grads["hg_lb_logits"], (0, chip * ds), (2, ds))
    small_grads["ffn_conv_w"] = lax.dynamic_slice(small_grads["ffn_conv_w"], (0, chip * fs), (2 * 3, fs))

    out_g, out_d, out_m, out_v = {}, {}, {}, {}

    def update(name, g2):
        shape = wts[name].shape
        d2, m2, v2 = _adamw(wts[name].reshape(g2.shape), mom1[name].reshape(g2.shape), mom2[name].reshape(g2.shape), g2, name=f"adamw_{name}")
        out_g[name], out_d[name], out_m[name], out_v[name] = g2.reshape(shape), d2.reshape(shape), m2.reshape(shape), v2.reshape(shape)
        return d2

    def update_layer(name, g2, layer, prev):
        res = _adamw_layer(wts[name], mom1[name], mom2[name], g2, layer, prev, name=f"adamw_{name}{layer}")
        out_g[name], out_d[name], out_m[name], out_v[name] = res
        return res

    g_down1, g_up1 = red["ffn1"].finish(red["hg"].token)
    down1 = update_layer("ffn_w_down", g_down1, 1, None)
    up1 = update_layer("ffn_w_up", g_up1, 1, None)
    red["mid"].to_core(up1[1])
    g_o, g_q, g_kv, g_down0, g_up0, g_out = red["mid"].finish(up1[2])
    update("attn_w_o", g_o)
    update("attn_w_q", g_q)
    update("w_kv", g_kv)
    update("hg_w_out", g_out)
    update_layer("ffn_w_down", g_down0, 0, down1)
    last = update_layer("ffn_w_up", g_up0, 0, up1)
    red["hg"].to_core(last[1])
    (g_in,) = red["hg"].finish(last[2])
    update("hg_w_in", g_in)

    as_2d = lambda a, n: a.reshape(small_grads[n].shape)
    updated = _adamw_small([(as_2d(wts[n], n), as_2d(mom1[n], n), as_2d(mom2[n], n), small_grads[n]) for n in small_names])
    for n, (d2, m2, v2) in zip(small_names, updated):
        shape = wts[n].shape
        out_g[n], out_d[n], out_m[n], out_v[n] = small_grads[n].reshape(shape), d2.reshape(shape), m2.reshape(shape), v2.reshape(shape)

    grad_x = dx.reshape(x.shape)
    return (loss_out, grad_x, *[out_g[n] for n in names], *[out_d[n] for n in names], *[out_m[n] for n in names], *[out_v[n] for n in names])
```

```python
import functools

import jax
import jax.numpy as jnp
from jax import lax
from jax.experimental import pallas as pl
from jax.experimental.pallas import tpu as pltpu

F32 = jnp.float32
BF16 = jnp.bfloat16
MESH = pl.DeviceIdType.MESH

EPS = 1e-6
D_MODEL = 1024
HG_HEADS = 8
HG_DK = 128
HG_CHUNK = 64
ATT_HD = 64
ATT_QH = 16
ATT_KVH = 2
ATT_GROUP = ATT_QH // ATT_KVH
WINDOW = 128
D_FF = 2816
N_CHIPS = 4
N_DEV = 8
LANES = 128
SUBLANES = 8
VMEM_LIMIT_BYTES = 56 * 1024 * 1024
NEG = -1e30
ALIBI_SLOPES = tuple(2.0 ** (-8.0 * h / ATT_QH) for h in range(1, ATT_QH + 1))

ADAM_LR = 0.001
ADAM_B1 = 0.9
ADAM_B2 = 0.999
ADAM_EPS = 1e-08
ADAM_WD = 0.01
ADAM_STEP = 10


def _cparams(sem=None):
    return pltpu.CompilerParams(dimension_semantics=sem, vmem_limit_bytes=VMEM_LIMIT_BYTES)


def _pick(n, cands):
    for c in cands:
        if n % c == 0:
            return c
    return n


def _sigmoid(x):
    return 0.5 * jnp.tanh(0.5 * x) + 0.5


def _dot(a, b, dims):
    return lax.dot_general(a, b, (dims, ((), ())), preferred_element_type=F32)


NN = ((1,), (0,))
NT = ((1,), (1,))
TN = ((0,), (0,))


MM_ROWS = 1024


def _rms_stats(xv):
    rstd = lax.rsqrt(jnp.mean(xv * xv, axis=-1, keepdims=True) + EPS)
    return xv * rstd, rstd


def _mm_operand(a_ref, gain_ref):
    if gain_ref is None:
        return a_ref[...].astype(BF16)
    return (_rms_stats(a_ref[...])[0] * gain_ref[...]).astype(BF16)


def _mm_nn(a, w, res=None, out_dtype=F32, name="mm_nn", gain=None):
    m, k = a.shape
    s, _, ns = w.shape
    tm = min(m, MM_ROWS)
    tn = _pick(ns, (1024, 1408, 512, 256, 128))
    npb = ns // tn

    def body(a_ref, w_ref, *rest):
        o_ref = rest[-1]
        acc = _dot(_mm_operand(a_ref, rest[0] if gain is not None else None), w_ref[...], NN)
        if res is not None:
            acc = acc + rest[-2][...]
        o_ref[...] = acc.astype(o_ref.dtype)

    in_specs = [
        pl.BlockSpec((tm, k), lambda i, j: (i, 0)),
        pl.BlockSpec((None, k, tn), lambda i, j: (j // npb, 0, j % npb)),
    ]
    args = [a, w]
    if gain is not None:
        in_specs.append(pl.BlockSpec((1, k), lambda i, j: (0, 0)))
        args.append(gain)
    if res is not None:
        in_specs.append(pl.BlockSpec((tm, tn), lambda i, j: (i, j)))
        args.append(res)
    return pl.pallas_call(
        body,
        name=name,
        grid=(m // tm, s * npb),
        in_specs=in_specs,
        out_specs=pl.BlockSpec((tm, tn), lambda i, j: (i, j)),
        out_shape=jax.ShapeDtypeStruct((m, s * ns), out_dtype),
        compiler_params=_cparams(("parallel", "parallel")),
    )(*args)


def _dy_spec(stacked, tm, tn, npb, row, kk):
    if stacked:
        return pl.BlockSpec((None, tm, tn), lambda *g: (kk(g) // npb, row(g), kk(g) % npb))
    return pl.BlockSpec((tm, tn), lambda *g: (row(g), kk(g)))


def _dep_specs(deps):
    return [pl.BlockSpec(d.shape, lambda *g: (0, 0)) for d in deps]


def _mm_nt(dy, w, stacked=False, out_dtype=F32, name="mm_nt", deps=(), norm_of=None):
    s, k, ns = w.shape
    m = dy.shape[1] if stacked else dy.shape[0]
    tm = min(m, MM_ROWS)
    tko = _pick(k, (1024, 1408, 512, 256))
    tn = _pick(ns, (1024, 1408, 512, 256))
    npb = ns // tn
    nk = s * npb
    fused = norm_of is not None
    assert not fused or tko == k

    def body(dy_ref, w_ref, *rest):
        acc_ref = rest[-1]
        i, kk = pl.program_id(0), pl.program_id(2)

        @pl.when(kk == 0)
        def _():
            acc_ref[...] = jnp.zeros_like(acc_ref)

        acc_ref[...] += _dot(dy_ref[...].astype(BF16), w_ref[...], NT)

        if not fused:
            @pl.when(kk == nk - 1)
            def _():
                rest[-2][...] = acc_ref[...].astype(rest[-2].dtype)
            return
        x_ref, g_ref, dres_ref = rest[:3]
        dx_ref, dg_ref = rest[-3], rest[-2]

        @pl.when(jnp.logical_and(i == 0, kk == 0))
        def _():
            dg_ref[...] = jnp.zeros_like(dg_ref)

        @pl.when(kk == nk - 1)
        def _():
            dxn = acc_ref[...]
            xhat, rstd = _rms_stats(x_ref[...])
            gd = dxn * g_ref[...]
            dx_ref[...] = dres_ref[...] + rstd * (gd - xhat * jnp.mean(gd * xhat, axis=-1, keepdims=True))
            dg_ref[...] += jnp.sum(dxn * xhat, axis=0, keepdims=True)

    row = pl.BlockSpec((tm, tko), lambda i, j, kk: (i, j))
    vec = pl.BlockSpec((1, k), lambda i, j, kk: (0, 0))
    return pl.pallas_call(
        body,
        name=name,
        grid=(m // tm, k // tko, nk),
        in_specs=[
            _dy_spec(stacked, tm, tn, npb, lambda g: g[0], lambda g: g[2]),
            pl.BlockSpec((None, tko, tn), lambda i, j, kk: (kk // npb, j, kk % npb)),
        ] + ([row, vec, row] if fused else []) + _dep_specs(deps),
        out_specs=[row, vec] if fused else row,
        out_shape=[jax.ShapeDtypeStruct((m, k), F32), jax.ShapeDtypeStruct((1, k), F32)] if fused else jax.ShapeDtypeStruct((m, k), out_dtype),
        scratch_shapes=[pltpu.VMEM((tm, tko), F32)],
        compiler_params=_cparams(("arbitrary",) * 3 if fused else ("parallel", "parallel", "arbitrary")),
    )(dy, w, *(norm_of or ()), *deps)


def _mm_tn(a, dy, s, ns, stacked=False, name="mm_tn", deps=(), gain=None):
    m, k = a.shape
    tm = min(m, MM_ROWS)
    tk = _pick(k, (1024, 1408, 512, 256))
    tn = _pick(ns, (1024, 1408, 512, 256, 128))
    npb = ns // tn
    nm = m // tm
    assert gain is None or tk == k

    def body(a_ref, dy_ref, *rest):
        o_ref, acc_ref = rest[-2:]
        mm = pl.program_id(2)

        @pl.when(mm == 0)
        def _():
            acc_ref[...] = jnp.zeros_like(acc_ref)

        acc_ref[...] += _dot(_mm_operand(a_ref, rest[0] if gain is not None else None), dy_ref[...].astype(BF16), TN)

        @pl.when(mm == nm - 1)
        def _():
            o_ref[...] = acc_ref[...]

    return pl.pallas_call(
        body,
        name=name,
        grid=(k // tk, s * npb, nm),
        in_specs=[
            pl.BlockSpec((tm, tk), lambda i, j, mm: (mm, i)),
            _dy_spec(stacked, tm, tn, npb, lambda g: g[2], lambda g: g[1]),
        ] + ([pl.BlockSpec((1, k), lambda i, j, mm: (0, 0))] if gain is not None else []) + _dep_specs(deps),
        out_specs=pl.BlockSpec((None, tk, tn), lambda i, j, mm: (j // npb, i, j % npb)),
        out_shape=jax.ShapeDtypeStruct((s, k, ns), F32),
        scratch_shapes=[pltpu.VMEM((tk, tn), F32)],
        compiler_params=_cparams(("parallel", "parallel", "arbitrary")),
    )(a, dy, *(() if gain is None else (gain,)), *deps)


ROW_TILE = 512


def _loss_head(h, g, target):
    t, d = h.shape
    r = min(t, ROW_TILE)

    def body(h_ref, g_ref, t_ref, dh_ref, dg_ref, loss_ref):
        @pl.when(pl.program_id(0) == 0)
        def _():
            dg_ref[...] = jnp.zeros_like(dg_ref)
            loss_ref[...] = jnp.zeros_like(loss_ref)

        xv = h_ref[...]
        rstd = lax.rsqrt(jnp.mean(xv * xv, axis=-1, keepdims=True) + EPS)
        xhat = xv * rstd
        gv = g_ref[...]
        err = xhat * gv - t_ref[...]
        loss_ref[...] += 0.5 * jnp.sum(jnp.mean(err * err, axis=-1, keepdims=True), axis=0, keepdims=True)
        dy = err * (1.0 / d)
        gd = dy * gv
        dh_ref[...] = rstd * (gd - xhat * jnp.mean(gd * xhat, axis=-1, keepdims=True))
        dg_ref[...] += jnp.sum(dy * xhat, axis=0, keepdims=True)

    return pl.pallas_call(
        body,
        name="loss_head",
        grid=(t // r,),
        in_specs=[
            pl.BlockSpec((r, d), lambda i: (i, 0)),
            pl.BlockSpec((1, d), lambda i: (0, 0)),
            pl.BlockSpec((r, d), lambda i: (i, 0)),
        ],
        out_specs=[
            pl.BlockSpec((r, d), lambda i: (i, 0)),
            pl.BlockSpec((1, d), lambda i: (0, 0)),
            pl.BlockSpec((1, LANES), lambda i: (0, 0)),
        ],
        out_shape=[
            jax.ShapeDtypeStruct((t, d), F32),
            jax.ShapeDtypeStruct((1, d), F32),
            jax.ShapeDtypeStruct((1, LANES), F32),
        ],
        compiler_params=_cparams(("arbitrary",)),
    )(h, g, target)


CONV_ROWS = 256
CONV_COLS = 1408


def _conv_taps(x_ext, n):
    tot = x_ext.shape[0]
    g1 = pltpu.roll(x_ext, 1, 0)[tot - n:]
    g2 = pltpu.roll(x_ext, 2, 0)[tot - n:]
    return g2, g1


def _conv_fwd(up, conv_w, conv_b, name="conv_fwd"):
    t = up.shape[0]
    r = min(t, CONV_ROWS)
    tc = CONV_COLS
    ncb = D_FF // tc
    hb = r // SUBLANES

    def body(g_ref, halo_ref, v_ref, w_ref, b_ref, o_ref, c_ref):
        i = pl.program_id(1)
        g0 = g_ref[...]
        halo = halo_ref[...] * jnp.where(i > 0, 1.0, 0.0)
        g2, g1 = _conv_taps(jnp.concatenate([halo, g0], axis=0), r)
        c = b_ref[...] + w_ref[0:1, :] * g2 + w_ref[1:2, :] * g1 + w_ref[2:3, :] * g0
        c_ref[...] = c
        o_ref[...] = (c * _sigmoid(c) * v_ref[...]).astype(BF16)

    blk = pl.BlockSpec((r, tc), lambda j, i: (i, j))
    return pl.pallas_call(
        body,
        name=name,
        grid=(ncb, t // r),
        in_specs=[
            blk,
            pl.BlockSpec((SUBLANES, tc), lambda j, i: (jnp.maximum(i * hb - 1, 0), j)),
            pl.BlockSpec((r, tc), lambda j, i: (i, ncb + j)),
            pl.BlockSpec((3, tc), lambda j, i: (0, j)),
            pl.BlockSpec((1, tc), lambda j, i: (0, j)),
        ],
        out_specs=[blk, blk],
        out_shape=[jax.ShapeDtypeStruct((t, D_FF), BF16), jax.ShapeDtypeStruct((t, D_FF), F32)],
        compiler_params=_cparams(("parallel", "parallel")),
    )(up, up, up, conv_w, conv_b)


def _conv_bwd(up, conv_w, c, dact, name="conv_bwd"):
    t = up.shape[0]
    r = min(t, CONV_ROWS)
    tc = CONV_COLS
    ncb = D_FF // tc
    nrt = t // r

    def body(g_ref, v_ref, w_ref, c_ref, da_ref, dup_ref, dw_ref, db_ref, nxt_ref):
        ii = pl.program_id(1)

        @pl.when(ii == 0)
        def _():
            nxt_ref[...] = jnp.zeros_like(nxt_ref)
            dw_ref[...] = jnp.zeros_like(dw_ref)
            db_ref[...] = jnp.zeros_like(db_ref)

        g0 = g_ref[...]
        w0, w1, w2 = w_ref[0:1, :], w_ref[1:2, :], w_ref[2:3, :]
        c = c_ref[...]
        sg = _sigmoid(c)
        da = da_ref[...].astype(F32)
        dup_ref[1] = (da * (c * sg)).astype(BF16)
        dc = da * v_ref[...] * (sg * (1.0 + c * (1.0 - sg)))
        ext = jnp.concatenate([dc, nxt_ref[...]], axis=0)
        tot = r + SUBLANES
        d1 = pltpu.roll(ext, tot - 1, 0)[:r]
        d2 = pltpu.roll(ext, tot - 2, 0)[:r]
        nxt_ref[...] = dc[:SUBLANES]
        dup_ref[0] = (w2 * dc + w1 * d1 + w0 * d2).astype(BF16)
        db_ref[...] += jnp.sum(dc, axis=0, keepdims=True)
        dw_ref[0:1, :] += jnp.sum(d2 * g0, axis=0, keepdims=True)
        dw_ref[1:2, :] += jnp.sum(d1 * g0, axis=0, keepdims=True)
        dw_ref[2:3, :] += jnp.sum(dc * g0, axis=0, keepdims=True)

    rev = lambda ii: nrt - 1 - ii
    dup, dw, db = pl.pallas_call(
        body,
        name=name,
        grid=(ncb, nrt),
        in_specs=[
            pl.BlockSpec((r, tc), lambda j, ii: (rev(ii), j)),
            pl.BlockSpec((r, tc), lambda j, ii: (rev(ii), ncb + j)),
            pl.BlockSpec((3, tc), lambda j, ii: (0, j)),
            pl.BlockSpec((r, tc), lambda j, ii: (rev(ii), j)),
            pl.BlockSpec((r, tc), lambda j, ii: (rev(ii), j)),
        ],
        out_specs=[
            pl.BlockSpec((2, None, r, tc), lambda j, ii: (0, j, rev(ii), 0)),
            pl.BlockSpec((3, tc), lambda j, ii: (0, j)),
            pl.BlockSpec((1, tc), lambda j, ii: (0, j)),
        ],
        out_shape=[
            jax.ShapeDtypeStruct((2, ncb, t, tc), BF16),
            jax.ShapeDtypeStruct((3, D_FF), F32),
            jax.ShapeDtypeStruct((1, D_FF), F32),
        ],
        scratch_shapes=[pltpu.VMEM((SUBLANES, tc), F32)],
        compiler_params=_cparams(("parallel", "arbitrary")),
    )(up, up, conv_w, c, dact)
    return dup.reshape(2 * ncb, t, tc), dw, db


def _split3(x):
    x1 = x.astype(BF16)
    r1 = x - x1.astype(F32)
    x2 = r1.astype(BF16)
    x3 = (r1 - x2.astype(F32)).astype(BF16)
    return x1, x2, x3


def _tri_dot(tri, x, dims):
    x1, x2, x3 = _split3(x)
    return _dot(tri, x1, dims) + _dot(tri, x2, dims) + _dot(tri, x3, dims)


def _lower_bound(logits_ref):
    return _sigmoid(logits_ref[0:1, :] - logits_ref[1:2, :])


def _hg_gates(qr, fr, lb):
    q = qr * _sigmoid(qr) * (HG_DK ** -0.5)
    sf = _sigmoid(fr)
    fg = lb + (1.0 - lb) * sf
    return q, sf, fg


def _hg_chunk_terms(q, fg, tril_b, low_half):
    g = jnp.log(fg)
    k = 1.0 - fg
    cum = _tri_dot(tril_b, g, NN)
    c_last = jnp.sum(g, axis=0, keepdims=True)
    c_mid = jnp.sum(jnp.where(low_half, g, 0.0), axis=0, keepdims=True)
    e_q = jnp.exp(cum - c_mid)
    e_k = jnp.exp(c_mid - cum)
    e_0 = jnp.exp(cum)
    e_l = jnp.exp(c_last - cum)
    return k, e_q, e_k, e_0, e_l, jnp.exp(c_last)


HG_BLOCK = 256


def _hg_proj_specs(rb, row):
    return [pl.BlockSpec((rb, D_MODEL), functools.partial(lambda i, k: (row(i), k), k=k)) for k in range(4)]


def _hg_consts(c):
    tril = lax.broadcasted_iota(jnp.int32, (c, c), 0) >= lax.broadcasted_iota(jnp.int32, (c, c), 1)
    low_half = lax.broadcasted_iota(jnp.int32, (c, D_MODEL), 0) < c // 2
    return tril, tril.astype(BF16), low_half


def _hgrn_fwd(proj, lb, wn):
    t = proj.shape[0]
    c = HG_CHUNK
    rb = min(t, HG_BLOCK)
    cpb = rb // c

    def body(q_ref, f_ref, i_ref, g_ref, lb_ref, wn_ref, o_ref, y_ref, st_ref, s_scr):
        @pl.when(pl.program_id(0) == 0)
        def _():
            s_scr[...] = jnp.zeros_like(s_scr)

        lb_all = _lower_bound(lb_ref)
        wnv = wn_ref[...]
        tril, tril_b, low_half = _hg_consts(c)

        def chunk(n, carry):
            rows = pl.ds(pl.multiple_of(n * c, c), c)
            q, _, fg = _hg_gates(q_ref[rows, :], f_ref[rows, :], lb_all)
            k, e_q, e_k, e_0, e_l, e_last = _hg_chunk_terms(q, fg, tril_b, low_half)
            qi, ki, q0, kl = (q * e_q).astype(BF16), (k * e_k).astype(BF16), (q * e_0).astype(BF16), (k * e_l).astype(BF16)
            v = i_ref[rows, :].astype(BF16)
            gr = g_ref[rows, :]
            gate = gr * _sigmoid(gr)
            for h in range(HG_HEADS):
                cols = slice(h * HG_DK, (h + 1) * HG_DK)
                st = s_scr[h]
                st_ref[h, n] = st
                a = jnp.where(tril, _dot(qi[:, cols], ki[:, cols], NT), 0.0)
                o = _dot(q0[:, cols], st.astype(BF16), NT) + _dot(a.astype(BF16), v[:, cols], NN)
                s_scr[h] = st * e_last[:, cols] + _dot(v[:, cols], kl[:, cols], TN)
                o_ref[rows, cols] = o
                rstd = lax.rsqrt(jnp.mean(o * o, axis=-1, keepdims=True) + EPS)
                y_ref[rows, cols] = (o * rstd * wnv * gate[:, cols]).astype(BF16)
            return carry

        lax.fori_loop(0, cpb, chunk, 0)

    blk = pl.BlockSpec((rb, D_MODEL), lambda i: (i, 0))
    return pl.pallas_call(
        body,
        name="hgrn_fwd",
        grid=(t // rb,),
        in_specs=_hg_proj_specs(rb, lambda i: i) + [pl.BlockSpec((2, D_MODEL), lambda i: (0, 0)), pl.BlockSpec((1, HG_DK), lambda i: (0, 0))],
        out_specs=[blk, blk, pl.BlockSpec((HG_HEADS, cpb, HG_DK, HG_DK), lambda i: (0, i, 0, 0))],
        out_shape=[
            jax.ShapeDtypeStruct((t, D_MODEL), F32),
            jax.ShapeDtypeStruct((t, D_MODEL), BF16),
            jax.ShapeDtypeStruct((HG_HEADS, t // c, HG_DK, HG_DK), F32),
        ],
        scratch_shapes=[pltpu.VMEM((HG_HEADS, HG_DK, HG_DK), F32)],
        compiler_params=_cparams(("arbitrary",)),
    )(proj, proj, proj, proj, lb, wn)


def _hgrn_bwd(proj, lb, wn, o, states, dy):
    t = proj.shape[0]
    c = HG_CHUNK
    rb = min(t, HG_BLOCK)
    cpb = rb // c
    nb = t // rb

    def body(q_ref, f_ref, i_ref, g_ref, lb_ref, wn_ref, o_ref, st_ref, dy_ref, dp_ref, dl_ref, dwn_ref, ds_scr, dlb_scr):
        step = pl.program_id(0)

        @pl.when(step == 0)
        def _():
            dwn_ref[...] = jnp.zeros_like(dwn_ref)
            ds_scr[...] = jnp.zeros_like(ds_scr)
            dlb_scr[...] = jnp.zeros_like(dlb_scr)

        lb_all = _lower_bound(lb_ref)
        wnv = wn_ref[...]
        tril, tril_b, low_half = _hg_consts(c)

        def chunk(nn, carry):
            n = cpb - 1 - nn
            rows = pl.ds(pl.multiple_of(n * c, c), c)
            qr = q_ref[rows, :]
            gr = g_ref[rows, :]
            q, sf, fg = _hg_gates(qr, f_ref[rows, :], lb_all)
            k, e_q, e_k, e_0, e_l, e_last = _hg_chunk_terms(q, fg, tril_b, low_half)
            qi, qi_lo, _ = _split3(q * e_q)
            ki, ki_lo, _ = _split3(k * e_k)
            q0 = (q * e_0).astype(BF16)
            kl = (k * e_l).astype(BF16)
            v = i_ref[rows, :].astype(BF16)
            sg = _sigmoid(gr)
            silu_g = gr * sg
            dsilu_g = sg * (1.0 + gr * (1.0 - sg))
            dqs, dks, d_lasts = [], [], []
            for h in range(HG_HEADS):
                cols = slice(h * HG_DK, (h + 1) * HG_DK)
                ov = o_ref[rows, cols]
                dyv = dy_ref[rows, cols].astype(F32)
                rstd = lax.rsqrt(jnp.mean(ov * ov, axis=-1, keepdims=True) + EPS)
                ohat = ov * rstd
                dp_ref[3, rows, cols] = (dyv * (ohat * wnv) * dsilu_g[:, cols]).astype(BF16)
                don = dyv * silu_g[:, cols]
                dwn_ref[...] += jnp.sum(don * ohat, axis=0, keepdims=True)
                gd = don * wnv
                do_b = (rstd * (gd - ohat * jnp.mean(gd * ohat, axis=-1, keepdims=True))).astype(BF16)
                st = st_ref[h, n]
                ds = ds_scr[h]
                ds_b = ds.astype(BF16)
                vh, kh = v[:, cols], k[:, cols]
                a_b = jnp.where(tril, _dot(qi[:, cols], ki[:, cols], NT), 0.0).astype(BF16)
                da_b = jnp.where(tril, _dot(do_b, vh, NT), 0.0).astype(BF16)
                dqs.append(_dot(do_b, st.astype(BF16), NN) * e_0[:, cols]
                           + (_dot(da_b, ki[:, cols], NN) + _dot(da_b, ki_lo[:, cols], NN)) * e_q[:, cols])
                dk_state = _dot(vh, ds_b, NN) * e_l[:, cols]
                dks.append((_dot(da_b, qi[:, cols], TN) + _dot(da_b, qi_lo[:, cols], TN)) * e_k[:, cols] + dk_state)
                dp_ref[2, rows, cols] = (_dot(a_b, do_b, TN) + _dot(kl[:, cols], ds_b, NT)).astype(BF16)
                ds_scr[h] = ds * e_last[:, cols] + _dot(do_b, q0[:, cols], TN)
                d_lasts.append(jnp.sum(dk_state * kh, axis=0, keepdims=True) + jnp.sum(ds * st, axis=0, keepdims=True) * e_last[:, cols])
            dq = jnp.concatenate(dqs, axis=1)
            dk = jnp.concatenate(dks, axis=1)
            dlogf = _tri_dot(tril_b, q * dq - k * dk, TN) + jnp.concatenate(d_lasts, axis=1)
            dfg = dlogf / fg - dk
            dlb_scr[...] += jnp.sum(dfg * (1.0 - sf), axis=0, keepdims=True)
            sq = _sigmoid(qr)
            dp_ref[0, rows, :] = (dq * (HG_DK ** -0.5) * (sq * (1.0 + qr * (1.0 - sq)))).astype(BF16)
            dp_ref[1, rows, :] = (dfg * (1.0 - lb_all) * sf * (1.0 - sf)).astype(BF16)
            return carry

        lax.fori_loop(0, cpb, chunk, 0)

        @pl.when(step == nb - 1)
        def _():
            d0 = dlb_scr[...] * lb_all * (1.0 - lb_all)
            dl_ref[0:1, :] = d0
            dl_ref[1:2, :] = -d0

    rev = lambda i: nb - 1 - i
    blk = pl.BlockSpec((rb, D_MODEL), lambda i: (rev(i), 0))
    return pl.pallas_call(
        body,
        name="hgrn_bwd",
        grid=(nb,),
        in_specs=_hg_proj_specs(rb, rev)
        + [pl.BlockSpec((2, D_MODEL), lambda i: (0, 0)), pl.BlockSpec((1, HG_DK), lambda i: (0, 0)), blk,
           pl.BlockSpec((HG_HEADS, cpb, HG_DK, HG_DK), lambda i: (0, rev(i), 0, 0)), blk],
        out_specs=[
            pl.BlockSpec((4, rb, D_MODEL), lambda i: (0, rev(i), 0)),
            pl.BlockSpec((2, D_MODEL), lambda i: (0, 0)),
            pl.BlockSpec((1, HG_DK), lambda i: (0, 0)),
        ],
        out_shape=[
            jax.ShapeDtypeStruct((4, t, D_MODEL), BF16),
            jax.ShapeDtypeStruct((2, D_MODEL), F32),
            jax.ShapeDtypeStruct((1, HG_DK), F32),
        ],
        scratch_shapes=[pltpu.VMEM((HG_HEADS, HG_DK, HG_DK), F32), pltpu.VMEM((1, D_MODEL), F32)],
        compiler_params=_cparams(("arbitrary",)),
    )(proj, proj, proj, proj, lb, wn, o, states, dy)


ATT_STACK = 8


def _att_stack(q_ref, sink_ref, first, lo, bias_p, bias_c, extra_ref=None):
    qs, bps, bcs, sinks, extras = [], [], [], None, []
    rows = lax.broadcasted_iota(jnp.int32, (ATT_STACK * WINDOW, 1), 0)
    for i in range(ATT_STACK):
        hq = first + i
        cols = slice((hq // 2) * LANES, (hq // 2 + 1) * LANES)
        sel = lo if hq % 2 == 0 else jnp.logical_not(lo)
        qp = q_ref[:, cols] * (ATT_HD ** -0.5)
        qs.append(jnp.where(sel, qp, jnp.zeros_like(qp)))
        bps.append(ALIBI_SLOPES[hq] * bias_p)
        bcs.append(ALIBI_SLOPES[hq] * bias_c)
        sinks = sink_ref[hq] if sinks is None else jnp.where(rows < i * WINDOW, sinks, sink_ref[hq])
        if extra_ref is not None:
            ep = extra_ref[:, cols]
            extras.append(jnp.where(sel, ep, jnp.zeros_like(ep)))
    cat = lambda parts: jnp.concatenate(parts, axis=0)
    return cat(qs), cat(bps), cat(bcs), sinks, (cat(extras) if extras else None)


def _att_rows(i):
    return slice(i * WINDOW, (i + 1) * WINDOW)


def _att_bias(n):
    tq = lax.broadcasted_iota(jnp.int32, (WINDOW, WINDOW), 0)
    sk = lax.broadcasted_iota(jnp.int32, (WINDOW, WINDOW), 1)
    valid_c = sk <= tq
    valid_p = (sk - tq) > jnp.where(n > 0, 0, WINDOW)
    dist_c = (tq - sk).astype(F32)
    return jnp.where(valid_p, -dist_c - float(WINDOW), NEG), jnp.where(valid_c, -dist_c, NEG)


def _att_halves(x, lo, kh):
    r = pltpu.roll(x, ATT_HD, 1)
    zero = jnp.zeros_like(x)
    if kh == 0:
        return jnp.where(lo, x, r), jnp.where(lo, x, zero), jnp.where(lo, zero, r)
    return jnp.where(lo, r, x), jnp.where(lo, r, zero), jnp.where(lo, zero, x)


def _att_probs(qm, k2p, k2c, bias_p, bias_c, sink):
    sp = _dot(qm, k2p, NT) + bias_p
    sc = _dot(qm, k2c, NT) + bias_c
    m = jnp.maximum(jnp.maximum(jnp.max(sp, axis=-1, keepdims=True), jnp.max(sc, axis=-1, keepdims=True)), sink)
    ep = jnp.exp(sp - m)
    ec = jnp.exp(sc - m)
    es = jnp.exp(sink - m)
    inv = 1.0 / (jnp.sum(ep, axis=-1, keepdims=True) + jnp.sum(ec, axis=-1, keepdims=True) + es)
    return ep * inv, ec * inv, es * inv


def _attn_fwd(q, kv, sinks):
    t = q.shape[0]
    nb = t // WINDOW

    def body(sink_ref, q_ref, kvp_ref, kvc_ref, o_ref):
        n = pl.program_id(0)
        bias_p, bias_c = _att_bias(n)
        lo = lax.broadcasted_iota(jnp.int32, (WINDOW, LANES), 1) < ATT_HD
        for kh in range(ATT_KVH):
            k2p, _, _ = _att_halves(kvp_ref[:, 0:LANES], lo, kh)
            k2c, _, _ = _att_halves(kvc_ref[:, 0:LANES], lo, kh)
            _, vlo_p, vhi_p = _att_halves(kvp_ref[:, LANES:2 * LANES], lo, kh)
            _, vlo_c, vhi_c = _att_halves(kvc_ref[:, LANES:2 * LANES], lo, kh)
            for first in range(kh * ATT_GROUP, (kh + 1) * ATT_GROUP, ATT_STACK):
                qs, bp, bc, sinks, _ = _att_stack(q_ref, sink_ref, first, lo, bias_p, bias_c)
                pp, pc, _ = _att_probs(qs, k2p, k2c, bp, bc, sinks)
                pp, pc = pp.astype(BF16), pc.astype(BF16)
                for i in range(0, ATT_STACK, 2):
                    even, odd = _att_rows(i), _att_rows(i + 1)
                    out = (_dot(pp[even], vlo_p, NN) + _dot(pc[even], vlo_c, NN)
                           + _dot(pp[odd], vhi_p, NN) + _dot(pc[odd], vhi_c, NN))
                    j = (first + i) // 2
                    o_ref[:, j * LANES:(j + 1) * LANES] = out.astype(BF16)

    return pl.pallas_call(
        body,
        name="attn_fwd",
        grid=(nb,),
        in_specs=[
            pl.BlockSpec(memory_space=pltpu.SMEM),
            pl.BlockSpec((WINDOW, D_MODEL), lambda n: (n, 0)),
            pl.BlockSpec((WINDOW, 2 * LANES), lambda n: (jnp.maximum(n - 1, 0), 0)),
            pl.BlockSpec((WINDOW, 2 * LANES), lambda n: (n, 0)),
        ],
        out_specs=pl.BlockSpec((WINDOW, D_MODEL), lambda n: (n, 0)),
        out_shape=jax.ShapeDtypeStruct((t, D_MODEL), BF16),
        compiler_params=_cparams(("parallel",)),
    )(sinks, q, kv, kv)


def _attn_bwd(q, kv, sinks, dout):
    t = q.shape[0]
    nb = t // WINDOW

    def body(sink_ref, q_ref, kvp_ref, kvc_ref, do_ref, dq_ref, dkv_ref, dsink_ref, carry_ref):
        n = pl.program_id(0)

        @pl.when(n == 0)
        def _():
            carry_ref[...] = jnp.zeros_like(carry_ref)
            dsink_ref[...] = jnp.zeros_like(dsink_ref)

        @pl.when(n == nb)
        def _():
            dkv_ref[...] = carry_ref[...].astype(BF16)

        @pl.when(n < nb)
        def _():
            bias_p, bias_c = _att_bias(n)
            lo = lax.broadcasted_iota(jnp.int32, (WINDOW, LANES), 1) < ATT_HD
            lane1 = lax.broadcasted_iota(jnp.int32, (1, LANES), 1)
            dsink = jnp.zeros((1, LANES), F32)
            halves = []
            for kh in range(ATT_KVH):
                k2p, klo_p, khi_p = _att_halves(kvp_ref[:, 0:LANES], lo, kh)
                k2c, klo_c, khi_c = _att_halves(kvc_ref[:, 0:LANES], lo, kh)
                v2p, _, _ = _att_halves(kvp_ref[:, LANES:2 * LANES], lo, kh)
                v2c, _, _ = _att_halves(kvc_ref[:, LANES:2 * LANES], lo, kh)
                acc = [jnp.zeros((WINDOW, LANES), F32) for _ in range(4)]
                for first in range(kh * ATT_GROUP, (kh + 1) * ATT_GROUP, ATT_STACK):
                    qs, bp, bc, sinks, dos = _att_stack(q_ref, sink_ref, first, lo, bias_p, bias_c, do_ref)
                    pp, pc, ps = _att_probs(qs, k2p, k2c, bp, bc, sinks)
                    dpp = _dot(dos, v2p, NT)
                    dpc = _dot(dos, v2c, NT)
                    delta = jnp.sum(pp * dpp, axis=-1, keepdims=True) + jnp.sum(pc * dpc, axis=-1, keepdims=True)
                    dsp = (pp * (dpp - delta)).astype(BF16)
                    dsc = (pc * (dpc - delta)).astype(BF16)
                    sink_term = ps * delta
                    for i in range(ATT_STACK):
                        dsink = dsink + jnp.where(lane1 == first + i, -jnp.sum(sink_term[_att_rows(i)], axis=0, keepdims=True), 0.0)
                    for i in range(0, ATT_STACK, 2):
                        even, odd = _att_rows(i), _att_rows(i + 1)
                        dq_pair = (_dot(dsp[even], klo_p, NN) + _dot(dsc[even], klo_c, NN)
                                   + _dot(dsp[odd], khi_p, NN) + _dot(dsc[odd], khi_c, NN))
                        j = (first + i) // 2
                        dq_ref[:, j * LANES:(j + 1) * LANES] = (dq_pair * (ATT_HD ** -0.5)).astype(BF16)
                    acc[0] = acc[0] + _dot(dsp, qs, TN)
                    acc[1] = acc[1] + _dot(dsc, qs, TN)
                    acc[2] = acc[2] + _dot(pp.astype(BF16), dos, TN)
                    acc[3] = acc[3] + _dot(pc.astype(BF16), dos, TN)
                halves.append([a + pltpu.roll(a, ATT_HD, 1) for a in acc])
            prev = jnp.concatenate(
                [jnp.where(lo, halves[0][0], halves[1][0]), jnp.where(lo, halves[0][2], halves[1][2])], axis=1)
            cur = jnp.concatenate(
                [jnp.where(lo, halves[0][1], halves[1][1]), jnp.where(lo, halves[0][3], halves[1][3])], axis=1)
            dkv_ref[...] = (carry_ref[...] + prev).astype(BF16)
            carry_ref[...] = cur
            dsink_ref[...] += dsink

    blk = lambda n: jnp.minimum(n, nb - 1)
    return pl.pallas_call(
        body,
        name="attn_bwd",
        grid=(nb + 1,),
        in_specs=[
            pl.BlockSpec(memory_space=pltpu.SMEM),
            pl.BlockSpec((WINDOW, D_MODEL), lambda n: (blk(n), 0)),
            pl.BlockSpec((WINDOW, 2 * LANES), lambda n: (jnp.maximum(blk(n) - 1, 0), 0)),
            pl.BlockSpec((WINDOW, 2 * LANES), lambda n: (blk(n), 0)),
            pl.BlockSpec((WINDOW, D_MODEL), lambda n: (blk(n), 0)),
        ],
        out_specs=[
            pl.BlockSpec((WINDOW, D_MODEL), lambda n: (blk(n), 0)),
            pl.BlockSpec((WINDOW, 2 * LANES), lambda n: (jnp.maximum(n - 1, 0), 0)),
            pl.BlockSpec((1, LANES), lambda n: (0, 0)),
        ],
        out_shape=[
            jax.ShapeDtypeStruct((t, D_MODEL), BF16),
            jax.ShapeDtypeStruct((t, 2 * LANES), BF16),
            jax.ShapeDtypeStruct((1, LANES), F32),
        ],
        scratch_shapes=[pltpu.VMEM((WINDOW, 2 * LANES), F32)],
        compiler_params=_cparams(("arbitrary",)),
    )(sinks, q, kv, kv, dout)


def _ffn_fwd(h, norm_g, w_up, conv_w, conv_b, w_down, tag, after_up=lambda up: None):
    up = _mm_nn(h, w_up, gain=norm_g, name=f"ffn{tag}_up")
    after_up(up)
    act, c = _conv_fwd(up, conv_w, conv_b, name=f"ffn{tag}_conv")
    h_out = _mm_nn(act, w_down, res=h, name=f"ffn{tag}_down")
    return h_out, (up, act, c)


def _ffn_bwd(dh, h, norm_g, w_up, conv_w, conv_b, w_down, saved, tag, deps=()):
    up, act, c = saved
    dw_down = _mm_tn(act, dh, 1, D_MODEL, name=f"ffn{tag}_dwdown", deps=deps)
    dact = _mm_nt(dh, w_down, out_dtype=BF16, name=f"ffn{tag}_dact", deps=deps)
    dup, dconv_w, dconv_b = _conv_bwd(up, conv_w, c, dact, name=f"ffn{tag}_dconv")
    dw_up = _mm_tn(h, dup, N_CHIPS, CONV_COLS, stacked=True, gain=norm_g, name=f"ffn{tag}_dwup")
    dh_in, dnorm = _mm_nt(dup, w_up, stacked=True, norm_of=(h, norm_g, dh), name=f"ffn{tag}_dxn")
    return dh_in, dict(ffn_w_down=dw_down, ffn_w_up=dw_up, ffn_conv_w=dconv_w, ffn_conv_b=dconv_b, ffn_norm=dnorm)


def _local_step(x, target, w, fetch=lambda w, stage, after: w, hook=lambda point, dh, grads: ()):
    proj = _mm_nn(x, w["hg_w_in"], gain=w["hg_norm"], name="hg_in")
    o, y, states = _hgrn_fwd(proj, w["hg_lb"], w["hg_out_norm"])
    w = fetch(w, "mixer_out", y)
    fetch(w, "layer0_relay", y)
    h_a = _mm_nn(y, w["hg_w_out"], res=x, name="hg_out")
    w = fetch(w, "layer0", h_a)
    h1, ffn0 = _ffn_fwd(h_a, w["ffn_norm"][0], w["ffn_w_up"][0], w["ffn_conv_w"][0], w["ffn_conv_b"][0], w["ffn_w_down"][0], 0,
                        lambda up: fetch(w, "layer1_relay", up))
    w = fetch(w, "layer1", h1)
    kv = _mm_nn(h1, w["w_kv"], gain=w["kv_norm"], out_dtype=BF16, name="kv_proj")
    qa = _mm_nn(h1, w["attn_w_q"], gain=w["attn_norm"], out_dtype=BF16, name="attn_q")
    ao = _attn_fwd(qa, kv, w["attn_sinks"])
    h_b = _mm_nn(ao, w["attn_w_o"], res=h1, name="attn_o")
    h2, ffn1 = _ffn_fwd(h_b, w["ffn_norm"][1], w["ffn_w_up"][1], w["ffn_conv_w"][1], w["ffn_conv_b"][1], w["ffn_w_down"][1], 1)
    dh2, d_final, loss = _loss_head(h2, w["final_norm"], target)

    dh_b, g1 = _ffn_bwd(dh2, h_b, w["ffn_norm"][1], w["ffn_w_up"][1], w["ffn_conv_w"][1], w["ffn_conv_b"][1], w["ffn_w_down"][1], ffn1, 1)
    deps = hook("ffn1", dh_b, g1)
    dw_o = _mm_tn(ao, dh_b, 1, D_MODEL, name="attn_dwo", deps=deps)
    dao = _mm_nt(dh_b, w["attn_w_o"], out_dtype=BF16, name="attn_dao", deps=deps)
    dqa, dkv, dsinks = _attn_bwd(qa, kv, w["attn_sinks"], dao)
    dw_q = _mm_tn(h1, dqa, 1, D_MODEL, gain=w["attn_norm"], name="attn_dwq")
    dh1, d_attn_norm = _mm_nt(dqa, w["attn_w_q"], norm_of=(h1, w["attn_norm"], dh_b), name="attn_dxa")
    dw_kv = _mm_tn(h1, dkv, 1, 2 * LANES, gain=w["kv_norm"], name="kv_dw")
    dh1, d_kv_norm = _mm_nt(dkv, w["w_kv"], norm_of=(h1, w["kv_norm"], dh1), name="kv_dx")
    deps = hook("attn", dh1, dict(attn_w_o=dw_o, attn_w_q=dw_q, w_kv=dw_kv))
    dh_a, g0 = _ffn_bwd(dh1, h_a, w["ffn_norm"][0], w["ffn_w_up"][0], w["ffn_conv_w"][0], w["ffn_conv_b"][0], w["ffn_w_down"][0], ffn0, 0, deps)
    dw_out = _mm_tn(y, dh_a, 1, D_MODEL, name="hg_dwout")
    deps = hook("ffn0", dh_a, dict(g0, hg_w_out=dw_out))
    dy = _mm_nt(dh_a, w["hg_w_out"], out_dtype=BF16, name="hg_dy", deps=deps)
    dproj, dlb, d_out_norm = _hgrn_bwd(proj, w["hg_lb"], w["hg_out_norm"], o, states, dy)
    deps = hook("hgrn", dproj, None)
    dw_in = _mm_tn(x, dproj, N_CHIPS, D_MODEL, stacked=True, gain=w["hg_norm"], name="hg_dwin", deps=deps)
    deps = hook("hg_w", dproj, dict(hg_w_in=dw_in))
    dx, d_hg_norm = _mm_nt(dproj, w["hg_w_in"], stacked=True, norm_of=(x, w["hg_norm"], dh_a), name="hg_dxn", deps=deps)

    grads = dict(
        hg_norm=d_hg_norm, hg_w_in=dw_in, hg_lb=dlb, hg_out_norm=d_out_norm, hg_w_out=dw_out,
        kv_norm=d_kv_norm, w_kv=dw_kv, attn_norm=d_attn_norm, attn_w_q=dw_q, attn_sinks=dsinks, attn_w_o=dw_o,
        final_norm=d_final,
    )
    for name in g0:
        grads[name] = [g0[name], g1[name]]
    return loss, dx, grads


ANY = pl.BlockSpec(memory_space=pl.ANY)


def _place():
    x, y, c = lax.axis_index("x"), lax.axis_index("y"), lax.axis_index("c")
    chips = [(1 - x, y), (x, 1 - y), (1 - x, 1 - y)]
    return x, y, c, chips


def _rcopy(src, dst, send_sem, recv_sem, to):
    return pltpu.make_async_remote_copy(src_ref=src, dst_ref=dst, send_sem=send_sem, recv_sem=recv_sem, device_id=to, device_id_type=MESH)


HBM = pl.BlockSpec(memory_space=pltpu.HBM)
SEM = pl.BlockSpec(memory_space=pltpu.SEMAPHORE)
EFFECT = pltpu.SideEffectType.DATAFLOW_SIDE_EFFECTING


def _in_hbm(a):
    return pltpu.with_memory_space_constraint(a, pltpu.HBM)


def _place_shard(shard, place, dtype, name, deps=(), layer=None):
    r, cols = shard.shape[-2:]
    tr = _pick(r, ELEM_ROWS)
    src = pl.BlockSpec((tr, cols), lambda i, place_ref: (i, 0)) if layer is None else pl.BlockSpec((None, tr, cols), lambda i, place_ref: (layer, i, 0))

    def body(place_ref, s_ref, *rest):
        o_ref = rest[-1]
        o_ref[...] = s_ref[...].astype(o_ref.dtype)

    return pl.pallas_call(
        body,
        name=name,
        grid_spec=pltpu.PrefetchScalarGridSpec(
            num_scalar_prefetch=1,
            grid=(r // tr,),
            in_specs=[src] + _dep_specs(deps),
            out_specs=pl.BlockSpec((None, tr, cols), lambda i, place_ref: (place_ref[0], i, 0)),
        ),
        out_shape=jax.ShapeDtypeStruct((N_CHIPS, r, cols), dtype),
        compiler_params=_cparams(("parallel",)),
    )(place, shard, *deps)


def _start_copies(name, bufs, n_sem, copies):
    n = len(bufs)

    def body(*refs):
        for cp in copies(refs[:n], refs[n], refs[n + 1]):
            cp.start()
        refs[-1][...] = jnp.zeros_like(refs[-1])

    outs = pl.pallas_call(
        body,
        name=name,
        in_specs=[HBM] * n,
        out_specs=[SEM, SEM] + [HBM] * n + [pl.BlockSpec(memory_space=pltpu.VMEM)],
        out_shape=[pltpu.SemaphoreType.DMA((n_sem,)), pltpu.SemaphoreType.DMA((n_sem,))] + [pltpu.HBM(b.shape, b.dtype) for b in bufs]
        + [jax.ShapeDtypeStruct((SUBLANES, LANES), F32)],
        input_output_aliases={i: 2 + i for i in range(n)},
        compiler_params=pltpu.CompilerParams(has_side_effects=EFFECT),
    )(*[_in_hbm(b) for b in bufs])
    return outs[0], outs[1], list(outs[2:-1]), outs[-1]


def _wait_copies(name, bufs, send_sems, recv_sems, after, copies):
    n = len(bufs)

    def body(*refs):
        for cp in copies(refs[:n], refs[n], refs[n + 1]):
            cp.wait_send()
            cp.wait_recv()

    return pl.pallas_call(
        body,
        name=name,
        in_specs=[HBM] * n + [SEM, SEM, ANY],
        out_specs=[HBM] * n,
        out_shape=[pltpu.HBM(b.shape, b.dtype) for b in bufs],
        input_output_aliases={i: i for i in range(n)},
        compiler_params=pltpu.CompilerParams(has_side_effects=EFFECT),
    )(*bufs, send_sems, recv_sems, after)


def _relay_copies(name, bufs, send_sems, recv_sems, after, landed, n_sem, onward):
    n = len(bufs)

    def body(*refs):
        for cp in landed(refs[:n], refs[n], refs[n + 1]):
            cp.wait_send()
            cp.wait_recv()
        for cp in onward(refs[:n], refs[n + 3], refs[n + 4]):
            cp.start()
        refs[-1][...] = jnp.zeros_like(refs[-1])

    outs = pl.pallas_call(
        body,
        name=name,
        in_specs=[HBM] * n + [SEM, SEM, ANY],
        out_specs=[SEM, SEM] + [HBM] * n + [pl.BlockSpec(memory_space=pltpu.VMEM)],
        out_shape=[pltpu.SemaphoreType.DMA((n_sem,)), pltpu.SemaphoreType.DMA((n_sem,))] + [pltpu.HBM(b.shape, b.dtype) for b in bufs]
        + [jax.ShapeDtypeStruct((SUBLANES, LANES), F32)],
        input_output_aliases={i: 2 + i for i in range(n)},
        compiler_params=pltpu.CompilerParams(has_side_effects=EFFECT),
    )(*bufs, send_sems, recv_sems, after)
    return outs[0], outs[1], list(outs[2:-1]), outs[-1]


def _gather_half_copies(first, count, over_ici):
    def copies(refs, send_sems, recv_sems):
        x, y, c, chips = _place()
        out = []
        for i in range(count):
            h = refs[i].shape[1] // 2
            mine = pl.ds(c * h, h)
            for j, (px, py) in enumerate(chips):
                k = 3 * (first + i) + j
                slot = 2 * x + y if over_ici else 2 * px + py
                to = (px, py, c) if over_ici else (x, y, 1 - c)
                out.append(_rcopy(refs[i].at[slot, mine], refs[i].at[slot, mine], send_sems.at[k], recv_sems.at[k], to))
        return out

    return copies


def _gather_copies(first, count):
    def copies(refs, send_sems, recv_sems):
        x, y, c, chips = _place()
        me = 2 * x + y
        out = []
        for i in range(count):
            for j, (px, py) in enumerate(chips):
                k = 3 * (first + i) + j
                out.append(_rcopy(refs[i].at[me], refs[i].at[me], send_sems.at[k], recv_sems.at[k], (px, py, c)))
        return out

    return copies


def _swap_copies(n):
    def copies(refs, send_sems, recv_sems):
        x, y, c, _ = _place()
        out = []
        for i in range(n):
            h = refs[i].shape[1] // 2
            out.append(_rcopy(refs[i].at[:, pl.ds((1 - c) * h, h)], refs[n + i], send_sems.at[i], recv_sems.at[i], (x, y, 1 - c)))
        return out

    return copies


def _partial_copies(n):
    def copies(refs, send_sems, recv_sems):
        x, y, c, chips = _place()
        out = []
        for i in range(n):
            for j, (px, py) in enumerate(chips):
                out.append(_rcopy(refs[i].at[2 * px + py], refs[n + i].at[j], send_sems.at[3 * i + j], recv_sems.at[3 * i + j], (px, py, c)))
        return out

    return copies


def _share_copies(n):
    def copies(refs, send_sems, recv_sems):
        x, y, c, _ = _place()
        return [_rcopy(refs[i].at[c], refs[i].at[c], send_sems.at[i], recv_sems.at[i], (x, y, 1 - c)) for i in range(n)]

    return copies


def _allreduce_small(groups, widths):
    flat = [a for g in groups for a in g]
    n = len(flat)
    rows = -(-sum(a.shape[0] for a in flat) // SUBLANES) * SUBLANES
    cols = max(a.shape[1] for a in flat)
    out_shapes = [(sum(a.shape[0] for a in g), wd or g[0].shape[1]) for g, wd in zip(groups, widths)]

    def body(*refs):
        ins, outs = refs[:n], refs[n:n + len(groups)]
        mine, buf, send_sems, recv_sems = refs[n + len(groups):]
        x, y, c, _ = _place()
        me = 4 * x + 2 * y + c
        mine[...] = jnp.zeros_like(mine)
        r0 = 0
        for a_ref in ins:
            r, w = a_ref.shape
            mine[r0:r0 + r, 0:w] = a_ref[...]
            r0 += r
        buf[me] = mine[...]
        copies = []
        for k in range(1, N_DEV):
            peer = (x ^ (k >> 2), y ^ ((k >> 1) & 1), c ^ (k & 1))
            cp = _rcopy(mine, buf.at[me], send_sems.at[k - 1], recv_sems.at[k - 1], peer)
            cp.start()
            copies.append(cp)
        for cp in copies:
            cp.wait()
        acc = buf[0]
        for d in range(1, N_DEV):
            acc = acc + buf[d]
        mine[...] = acc
        r0 = 0
        for o_ref in outs:
            r, w = o_ref.shape
            o_ref[...] = mine[r0:r0 + r, 0:w]
            r0 += r

    vmem = pl.BlockSpec(memory_space=pltpu.VMEM)
    return pl.pallas_call(
        body,
        name="allreduce_small",
        in_specs=[vmem] * n,
        out_specs=[vmem] * len(groups),
        out_shape=[jax.ShapeDtypeStruct(s, F32) for s in out_shapes],
        scratch_shapes=[pltpu.VMEM((rows, cols), F32), pltpu.VMEM((N_DEV, rows, cols), F32),
                        pltpu.SemaphoreType.DMA((N_DEV - 1,)), pltpu.SemaphoreType.DMA((N_DEV - 1,))],
        compiler_params=pltpu.CompilerParams(vmem_limit_bytes=VMEM_LIMIT_BYTES),
    )(*flat)


def _adamw_small(items):
    n = len(items)

    def body(*refs):
        for i in range(n):
            w_ref, m_ref, v_ref, g_ref = refs[4 * i:4 * i + 4]
            d_ref, nm_ref, nv_ref = refs[4 * n + 3 * i:4 * n + 3 * i + 3]
            d_ref[...], nm_ref[...], nv_ref[...] = _adamw_math(w_ref[...], m_ref[...], v_ref[...], g_ref[...])

    vmem = pl.BlockSpec(memory_space=pltpu.VMEM)
    outs = pl.pallas_call(
        body,
        name="adamw_small",
        in_specs=[vmem] * (4 * n),
        out_specs=[vmem] * (3 * n),
        out_shape=[jax.ShapeDtypeStruct(it[0].shape, F32) for it in items for _ in range(3)],
        compiler_params=pltpu.CompilerParams(vmem_limit_bytes=VMEM_LIMIT_BYTES),
    )(*[a for it in items for a in it])
    return [tuple(outs[3 * i:3 * i + 3]) for i in range(n)]


class _Reduction:
    def __init__(self, tag, grads, place):
        self.tag, self.n, self.place = tag, len(grads), place
        lands = [lax.empty((N_CHIPS, g.shape[1] // 2, g.shape[2]), F32) for g in grads]
        self._start("swap", list(grads) + lands, self.n, _swap_copies(self.n))

    def _start(self, stage, bufs, n_sem, copies):
        *self.flight, self.token = _start_copies(f"rs_{stage}_start_{self.tag}", bufs, n_sem, copies)

    def _landed(self, stage, after, copies):
        send_sems, recv_sems, bufs = self.flight
        return _wait_copies(f"rs_{stage}_wait_{self.tag}", bufs, send_sems, recv_sems, after, copies)

    def to_chips(self, after):
        n = self.n
        bufs = self._landed("swap", after, _swap_copies(n))
        sums = [_add_core_halves(g, o, self.place, name=f"rs_add_core_{self.tag}_{i}") for i, (g, o) in enumerate(zip(bufs[:n], bufs[n:]))]
        self.mine = [f for f, _ in sums]
        parts = [b for _, b in sums]
        lands = [lax.empty((3,) + p.shape[1:], BF16) for p in parts]
        self._start("send", parts + lands, 3 * n, _partial_copies(n))

    def to_core(self, after):
        n = self.n
        bufs = self._landed("send", after, _partial_copies(n))
        halves = [_add_chip_partials(f, o, self.place, name=f"rs_add_chip_{self.tag}_{i}") for i, (f, o) in enumerate(zip(self.mine, bufs[n:]))]
        self._start("share", halves, n, _share_copies(n))

    def finish(self, after):
        return [b.reshape((-1,) + b.shape[2:]) for b in self._landed("share", after, _share_copies(self.n))]


ELEM_ROWS = (256, 176, 128, 64, 32, 16, 8)


def _add_core_halves(grad, got, place, name):
    s, r, cols = grad.shape
    h = r // 2
    tr = _pick(h, ELEM_ROWS)

    def body(place_ref, g_ref, o_ref, f_ref, b_ref):
        acc = g_ref[...] + o_ref[...]
        b_ref[...] = acc.astype(BF16)

        @pl.when(pl.program_id(1) == place_ref[0])
        def _():
            f_ref[...] = acc

    blk = pl.BlockSpec((None, tr, cols), lambda i, k, place_ref: (k, i, 0))
    return pl.pallas_call(
        body,
        name=name,
        grid_spec=pltpu.PrefetchScalarGridSpec(
            num_scalar_prefetch=1,
            grid=(h // tr, s),
            in_specs=[pl.BlockSpec((None, None, tr, cols), lambda i, k, place_ref: (k, place_ref[1], i, 0)), blk],
            out_specs=[pl.BlockSpec((tr, cols), lambda i, k, place_ref: (i, 0)), blk],
        ),
        out_shape=[jax.ShapeDtypeStruct((h, cols), F32), jax.ShapeDtypeStruct((s, h, cols), BF16)],
        compiler_params=_cparams(("parallel", "arbitrary")),
    )(place, grad.reshape(s, 2, h, cols), got)


def _add_chip_partials(mine, got, place, name):
    h, cols = mine.shape
    tr = _pick(h, ELEM_ROWS)

    def body(place_ref, m_ref, g_ref, o_ref):
        acc = m_ref[...]
        for j in range(3):
            acc = acc + g_ref[j].astype(F32)
        o_ref[...] = acc

    return pl.pallas_call(
        body,
        name=name,
        grid_spec=pltpu.PrefetchScalarGridSpec(
            num_scalar_prefetch=1,
            grid=(h // tr,),
            in_specs=[
                pl.BlockSpec((tr, cols), lambda i, place_ref: (i, 0)),
                pl.BlockSpec((3, tr, cols), lambda i, place_ref: (0, i, 0)),
            ],
            out_specs=pl.BlockSpec((None, tr, cols), lambda i, place_ref: (place_ref[1], i, 0)),
        ),
        out_shape=jax.ShapeDtypeStruct((2, h, cols), F32),
        compiler_params=_cparams(("parallel",)),
    )(place, mine, got)


def _adamw_math(w, m, v, g):
    nm = ADAM_B1 * m + (1.0 - ADAM_B1) * g
    nv = ADAM_B2 * v + (1.0 - ADAM_B2) * (g * g)
    m_hat = nm * (1.0 / (1.0 - ADAM_B1 ** ADAM_STEP))
    v_hat = nv * (1.0 / (1.0 - ADAM_B2 ** ADAM_STEP))
    return -ADAM_LR * (m_hat / (jnp.sqrt(v_hat) + ADAM_EPS) + ADAM_WD * w), nm, nv


def _adamw_layer(w, m, v, g, layer, prev, name):
    nl, r, cols = w.shape
    tr = _pick(r, ELEM_ROWS)

    def body(w_ref, m_ref, v_ref, g_ref, *rest):
        go_ref, d_ref, nm_ref, nv_ref = rest[-4:]
        gv = g_ref[...]
        d_ref[...], nm_ref[...], nv_ref[...] = _adamw_math(w_ref[...], m_ref[...], v_ref[...], gv)
        go_ref[...] = gv

    lay = pl.BlockSpec((None, tr, cols), lambda i: (layer, i, 0))
    return pl.pallas_call(
        body,
        name=name,
        grid=(r // tr,),
        in_specs=[lay] * 3 + [pl.BlockSpec((tr, cols), lambda i: (i, 0))] + ([ANY] * 4 if prev else []),
        out_specs=[lay] * 4,
        out_shape=[jax.ShapeDtypeStruct((nl, r, cols), F32)] * 4,
        input_output_aliases={4 + k: k for k in range(4)} if prev else {},
        compiler_params=_cparams(("parallel",)),
    )(w, m, v, g, *(prev or ()))


def _adamw(w, m, v, g, name):
    r, cols = w.shape
    tr = _pick(r, ELEM_ROWS)

    def body(w_ref, m_ref, v_ref, g_ref, d_ref, nm_ref, nv_ref):
        d_ref[...], nm_ref[...], nv_ref[...] = _adamw_math(w_ref[...], m_ref[...], v_ref[...], g_ref[...])

    blk = pl.BlockSpec((tr, cols), lambda i: (i, 0))
    return pl.pallas_call(
        body,
        name=name,
        grid=(r // tr,),
        in_specs=[blk] * 4,
        out_specs=[blk] * 3,
        out_shape=[jax.ShapeDtypeStruct((r, cols), F32)] * 3,
        compiler_params=_cparams(("parallel",)),
    )(w, m, v, g)


SMALL_COLS = 384
SMALL_ROWS = 16


def _pad_rows(flat, rows, cols):
    return jnp.pad(flat, (0, rows * cols - flat.shape[0])).reshape(rows, cols)


def kernel(x, hg_norm, hg_w_in, hg_lb_logits, hg_out_norm, hg_w_out, kv_norm, w_kv, attn_norm, attn_w_q, attn_sinks, attn_w_o, ffn_norm, ffn_w_up, ffn_conv_w, ffn_conv_b, ffn_w_down, final_norm, loss_target, m_hg_norm, m_hg_w_in, m_hg_lb_logits, m_hg_out_norm, m_hg_w_out, m_kv_norm, m_w_kv, m_attn_norm, m_attn_w_q, m_attn_sinks, m_attn_w_o, m_ffn_norm, m_ffn_w_up, m_ffn_conv_w, m_ffn_conv_b, m_ffn_w_down, m_final_norm, v_hg_norm, v_hg_w_in, v_hg_lb_logits, v_hg_out_norm, v_hg_w_out, v_kv_norm, v_w_kv, v_attn_norm, v_attn_w_q, v_attn_sinks, v_attn_w_o, v_ffn_norm, v_ffn_w_up, v_ffn_conv_w, v_ffn_conv_b, v_ffn_w_down, v_final_norm):
    wts = dict(hg_norm=hg_norm, hg_w_in=hg_w_in, hg_lb_logits=hg_lb_logits, hg_out_norm=hg_out_norm, hg_w_out=hg_w_out, kv_norm=kv_norm, w_kv=w_kv, attn_norm=attn_norm, attn_w_q=attn_w_q, attn_sinks=attn_sinks, attn_w_o=attn_w_o, ffn_norm=ffn_norm, ffn_w_up=ffn_w_up, ffn_conv_w=ffn_conv_w, ffn_conv_b=ffn_conv_b, ffn_w_down=ffn_w_down, final_norm=final_norm)
    mom1 = dict(hg_norm=m_hg_norm, hg_w_in=m_hg_w_in, hg_lb_logits=m_hg_lb_logits, hg_out_norm=m_hg_out_norm, hg_w_out=m_hg_w_out, kv_norm=m_kv_norm, w_kv=m_w_kv, attn_norm=m_attn_norm, attn_w_q=m_attn_w_q, attn_sinks=m_attn_sinks, attn_w_o=m_attn_w_o, ffn_norm=m_ffn_norm, ffn_w_up=m_ffn_w_up, ffn_conv_w=m_ffn_conv_w, ffn_conv_b=m_ffn_conv_b, ffn_w_down=m_ffn_w_down, final_norm=m_final_norm)
    mom2 = dict(hg_norm=v_hg_norm, hg_w_in=v_hg_w_in, hg_lb_logits=v_hg_lb_logits, hg_out_norm=v_hg_out_norm, hg_w_out=v_hg_w_out, kv_norm=v_kv_norm, w_kv=v_w_kv, attn_norm=v_attn_norm, attn_w_q=v_attn_w_q, attn_sinks=v_attn_sinks, attn_w_o=v_attn_w_o, ffn_norm=v_ffn_norm, ffn_w_up=v_ffn_w_up, ffn_conv_w=v_ffn_conv_w, ffn_conv_b=v_ffn_conv_b, ffn_w_down=v_ffn_w_down, final_norm=v_final_norm)
    names = list(wts)
    chip = 2 * lax.axis_index("x") + lax.axis_index("y")
    core = lax.axis_index("c")
    fs = D_FF // N_CHIPS
    ds = D_MODEL // N_CHIPS

    place_arr = jnp.stack([chip, core]).astype(jnp.int32)
    small = jnp.concatenate([hg_norm.reshape(-1), hg_lb_logits.reshape(-1), ffn_conv_w.reshape(-1)])
    n_small = small.shape[0]
    shards = [
        ("small", _pad_rows(small, SMALL_ROWS, SMALL_COLS), F32, None), ("hg_w_in", hg_w_in, BF16, 0),
        ("hg_w_out", hg_w_out, BF16, 0), ("ffn_w_up0", ffn_w_up, BF16, 0), ("ffn_w_down0", ffn_w_down, BF16, 0),
        ("w_kv", w_kv, BF16, None), ("attn_w_q", attn_w_q, BF16, 0), ("attn_w_o", attn_w_o, BF16, 0),
        ("ffn_w_up1", ffn_w_up, BF16, 1), ("ffn_w_down1", ffn_w_down, BF16, 1),
    ]
    n_first = 3
    spans = dict(layer0=(0, 2), layer1=(2, 7))

    def first_copies(refs, send_sems, recv_sems):
        return (_gather_copies(0, 1)(refs[:1], send_sems, recv_sems) + _gather_half_copies(1, 1, True)(refs[1:2], send_sems, recv_sems)
                + _gather_copies(2, 1)(refs[2:3], send_sems, recv_sems))

    placed = [_place_shard(s, place_arr, dt, name=f"place_{nm}", layer=ly) for nm, s, dt, ly in shards[:n_first]]
    first = _start_copies("gather_start_first", placed, 3 * n_first, first_copies)
    placed = [_place_shard(s, place_arr, dt, name=f"place_{nm}", deps=(first[3],), layer=ly) for nm, s, dt, ly in shards[n_first:]]
    rest = _start_copies("gather_start_rest", placed, 3 * len(placed), _gather_half_copies(0, len(placed), True))
    relayed = {}

    def fetch(w, stage, after):
        if stage == "first":
            w_in = _relay_copies("gather_first_relay", first[2][1:2], first[0], first[1], after,
                                 _gather_half_copies(1, 1, True), 3, _gather_half_copies(0, 1, False))
            got = _wait_copies("gather_wait_small", first[2][:1], first[0], first[1], w_in[3], _gather_copies(0, 1))
            got += _wait_copies("gather_wait_first", w_in[2], w_in[0], w_in[1], got[0], _gather_half_copies(0, 1, False))
        elif stage == "mixer_out":
            got = _wait_copies("gather_wait_mixer_out", first[2][2:], first[0], first[1], after, _gather_copies(2, 1))
        elif stage.endswith("_relay"):
            lo, hi = spans[stage[:-6]]
            relayed[stage[:-6]] = _relay_copies(
                f"gather_{stage}", rest[2][lo:hi], rest[0], rest[1], after,
                _gather_half_copies(lo, hi - lo, True), 3 * (hi - lo), _gather_half_copies(0, hi - lo, False))
            return w
        else:
            lo, hi = spans[stage]
            send_sems, recv_sems, bufs, _ = relayed[stage]
            got = _wait_copies(f"gather_wait_{stage}", bufs, send_sems, recv_sems, after, _gather_half_copies(0, hi - lo, False))
        w = dict(w)
        if stage == "first":
            g_small = got[0].reshape(N_CHIPS, -1)[:, :n_small]
            conv_w = g_small[:, 3 * ds:].reshape(N_CHIPS, 2, 3, fs).transpose(1, 2, 0, 3).reshape(2, 3, D_FF)
            w.update(
                hg_norm=g_small[:, :ds].reshape(1, D_MODEL),
                hg_lb=g_small[:, ds:3 * ds].reshape(N_CHIPS, 2, ds).transpose(1, 0, 2).reshape(2, D_MODEL),
                ffn_conv_w=[conv_w[0], conv_w[1]], hg_w_in=got[1],
            )
        elif stage == "mixer_out":
            w.update(hg_w_out=got[0].reshape(1, D_MODEL, D_MODEL))
        elif stage == "layer0":
            w.update(ffn_w_up=[got[0], None], ffn_w_down=[got[1].reshape(1, D_FF, D_MODEL), None])
        else:
            w.update(
                w_kv=got[0].reshape(1, D_MODEL, 2 * LANES), attn_w_q=got[1].reshape(1, D_MODEL, D_MODEL),
                attn_w_o=got[2].reshape(1, D_MODEL, D_MODEL), ffn_w_up=[w["ffn_w_up"][0], got[3]],
                ffn_w_down=[w["ffn_w_down"][0], got[4].reshape(1, D_FF, D_MODEL)],
            )
        return w

    whole = dict(
        hg_out_norm=hg_out_norm, kv_norm=kv_norm.reshape(1, D_MODEL), attn_norm=attn_norm, attn_sinks=attn_sinks.reshape(ATT_QH),
        ffn_norm=[ffn_norm[0:1], ffn_norm[1:2]], ffn_conv_b=[ffn_conv_b[0:1], ffn_conv_b[1:2]], final_norm=final_norm.reshape(1, D_MODEL),
    )
    whole = fetch(whole, "first", rest[3])

    red, layer1 = {}, {}

    def by_rows(g, rows):
        return g.reshape(N_CHIPS, rows, g.shape[2])

    def hook(point, dh, grads):
        if point == "ffn1":
            red["ffn1"] = _Reduction("ffn1", [by_rows(grads["ffn_w_down"], fs), grads["ffn_w_up"]], place_arr)
            return (red["ffn1"].token,)
        if point == "attn":
            red["ffn1"].to_chips(dh)
            layer1.update(grads)
            return (red["ffn1"].token,)
        if point == "ffn0":
            group = [by_rows(layer1["attn_w_o"], ds), by_rows(layer1["attn_w_q"], ds), by_rows(layer1["w_kv"], ds),
                     by_rows(grads["ffn_w_down"], fs), grads["ffn_w_up"], by_rows(grads["hg_w_out"], ds)]
            red["mid"] = _Reduction("mid", group, place_arr)
            return (red["mid"].token,)
        if point == "hgrn":
            red["ffn1"].to_core(dh)
            red["mid"].to_chips(dh)
            return (red["ffn1"].token, red["mid"].token)
        red["hg"] = _Reduction("hg", [grads["hg_w_in"]], place_arr)
        return (red["hg"].token,)

    loss, dx, grads = _local_step(x[0], loss_target[0], whole, fetch, hook)

    small_names = ["hg_out_norm", "attn_sinks", "kv_norm", "attn_norm", "ffn_norm", "ffn_conv_b", "final_norm", "hg_norm", "hg_lb_logits", "ffn_conv_w"]
    groups = [[loss]] + [grads[n] if isinstance(grads[n], list) else [grads[n]] for n in small_names[:-2]] + [[grads["hg_lb"]], grads["ffn_conv_w"]]
    summed = _allreduce_small(groups, [None, None, ATT_QH] + [None] * 8)
    red["hg"].to_chips(summed[1])
    loss_out = summed[0][0, 0]
    small_grads = dict(zip(small_names, summed[1:]))
    small_grads["hg_norm"] = lax.dynamic_slice(small_grads["hg_norm"], (0, chip * ds), (1, ds))
    small_grads["hg_lb_logits"] = lax.dynamic_slice(small_grads["hg_lb_logits"], (0, chip * ds), (2, ds))
    small_grads["ffn_conv_w"] = lax.dynamic_slice(small_grads["ffn_conv_w"], (0, chip * fs), (2 * 3, fs))

    out_g, out_d, out_m, out_v = {}, {}, {}, {}

    def update(name, g2):
        shape = wts[name].shape
        d2, m2, v2 = _adamw(wts[name].reshape(g2.shape), mom1[name].reshape(g2.shape), mom2[name].reshape(g2.shape), g2, name=f"adamw_{name}")
        out_g[name], out_d[name], out_m[name], out_v[name] = g2.reshape(shape), d2.reshape(shape), m2.reshape(shape), v2.reshape(shape)
        return d2

    def update_layer(name, g2, layer, prev):
        res = _adamw_layer(wts[name], mom1[name], mom2[name], g2, layer, prev, name=f"adamw_{name}{layer}")
        out_g[name], out_d[name], out_m[name], out_v[name] = res
        return res

    g_down1, g_up1 = red["ffn1"].finish(red["hg"].token)
    down1 = update_layer("ffn_w_down", g_down1, 1, None)
    up1 = update_layer("ffn_w_up", g_up1, 1, None)
    red["mid"].to_core(up1[1])
    g_o, g_q, g_kv, g_down0, g_up0, g_out = red["mid"].finish(up1[2])
    update("attn_w_o", g_o)
    update("attn_w_q", g_q)
    update("w_kv", g_kv)
    update("hg_w_out", g_out)
    update_layer("ffn_w_down", g_down0, 0, down1)
    last = update_layer("ffn_w_up", g_up0, 0, up1)
    red["hg"].to_core(last[1])
    (g_in,) = red["hg"].finish(last[2])
    update("hg_w_in", g_in)

    as_2d = lambda a, n: a.reshape(small_grads[n].shape)
    updated = _adamw_small([(as_2d(wts[n], n), as_2d(mom1[n], n), as_2d(mom2[n], n), small_grads[n]) for n in small_names])
    for n, (d2, m2, v2) in zip(small_names, updated):
        shape = wts[n].shape
        out_g[n], out_d[n], out_m[n], out_v[n] = small_grads[n].reshape(shape), d2.reshape(shape), m2.reshape(shape), v2.reshape(shape)

    grad_x = dx.reshape(x.shape)
    return (loss_out, grad_x, *[out_g[n] for n in names], *[out_d[n] for n in names], *[out_m[n] for n in names], *[out_v[n] for n in names])
```

```python
import functools

import jax
import jax.numpy as jnp
from jax import lax
from jax.experimental import pallas as pl
from jax.experimental.pallas import tpu as pltpu

F32 = jnp.float32
BF16 = jnp.bfloat16
MESH = pl.DeviceIdType.MESH

EPS = 1e-6
D_MODEL = 1024
HG_HEADS = 8
HG_DK = 128
HG_CHUNK = 64
ATT_HD = 64
ATT_QH = 16
ATT_KVH = 2
ATT_GROUP = ATT_QH // ATT_KVH
WINDOW = 128
D_FF = 2816
N_CHIPS = 4
N_DEV = 8
LANES = 128
SUBLANES = 8
VMEM_LIMIT_BYTES = 56 * 1024 * 1024
NEG = -1e30
ALIBI_SLOPES = tuple(2.0 ** (-8.0 * h / ATT_QH) for h in range(1, ATT_QH + 1))

ADAM_LR = 0.001
ADAM_B1 = 0.9
ADAM_B2 = 0.999
ADAM_EPS = 1e-08
ADAM_WD = 0.01
ADAM_STEP = 10


def _cparams(sem=None):
    return pltpu.CompilerParams(dimension_semantics=sem, vmem_limit_bytes=VMEM_LIMIT_BYTES)


def _pick(n, cands):
    for c in cands:
        if n % c == 0:
            return c
    return n


def _sigmoid(x):
    return 0.5 * jnp.tanh(0.5 * x) + 0.5


def _dot(a, b, dims):
    return lax.dot_general(a, b, (dims, ((), ())), preferred_element_type=F32)


NN = ((1,), (0,))
NT = ((1,), (1,))
TN = ((0,), (0,))


MM_ROWS = 1024


def _rms_stats(xv):
    rstd = lax.rsqrt(jnp.mean(xv * xv, axis=-1, keepdims=True) + EPS)
    return xv * rstd, rstd


def _mm_operand(a_ref, gain_ref):
    if gain_ref is None:
        return a_ref[...].astype(BF16)
    return (_rms_stats(a_ref[...])[0] * gain_ref[...]).astype(BF16)


def _mm_nn(a, w, res=None, out_dtype=F32, name="mm_nn", gain=None):
    m, k = a.shape
    s, _, ns = w.shape
    tm = min(m, MM_ROWS)
    tn = _pick(ns, (1024, 1408, 512, 256, 128))
    npb = ns // tn

    def body(a_ref, w_ref, *rest):
        if gain is None:
            o_ref = rest[-1]
            lhs = a_ref[...].astype(BF16)
        else:
            o_ref, xn_ref = rest[-2:]

            @pl.when(pl.program_id(1) == 0)
            def _():
                xn_ref[...] = _mm_operand(a_ref, rest[0])

            lhs = xn_ref[...]
        acc = _dot(lhs, w_ref[...], NN)
        if res is not None:
            acc = acc + rest[1 if gain is not None else 0][...]
        o_ref[...] = acc.astype(o_ref.dtype)

    in_specs = [
        pl.BlockSpec((tm, k), lambda i, j: (i, 0)),
        pl.BlockSpec((None, k, tn), lambda i, j: (j // npb, 0, j % npb)),
    ]
    args = [a, w]
    if gain is not None:
        in_specs.append(pl.BlockSpec((1, k), lambda i, j: (0, 0)))
        args.append(gain)
    if res is not None:
        in_specs.append(pl.BlockSpec((tm, tn), lambda i, j: (i, j)))
        args.append(res)
    return pl.pallas_call(
        body,
        name=name,
        grid=(m // tm, s * npb),
        in_specs=in_specs,
        out_specs=pl.BlockSpec((tm, tn), lambda i, j: (i, j)),
        out_shape=jax.ShapeDtypeStruct((m, s * ns), out_dtype),
        scratch_shapes=[pltpu.VMEM((tm, k), BF16)] if gain is not None else [],
        compiler_params=_cparams(("parallel", "arbitrary" if gain is not None else "parallel")),
    )(*args)


def _dy_spec(stacked, tm, tn, npb, row, kk):
    if stacked:
        return pl.BlockSpec((None, tm, tn), lambda *g: (kk(g) // npb, row(g), kk(g) % npb))
    return pl.BlockSpec((tm, tn), lambda *g: (row(g), kk(g)))


def _dep_specs(deps):
    return [pl.BlockSpec(d.shape, lambda *g: (0, 0)) for d in deps]


def _mm_nt(dy, w, stacked=False, out_dtype=F32, name="mm_nt", deps=(), norm_of=None):
    s, k, ns = w.shape
    m = dy.shape[1] if stacked else dy.shape[0]
    tm = min(m, MM_ROWS)
    tko = _pick(k, (1024, 1408, 512, 256))
    tn = _pick(ns, (1024, 1408, 512, 256))
    npb = ns // tn
    nk = s * npb
    fused = norm_of is not None
    assert not fused or tko == k

    def body(dy_ref, w_ref, *rest):
        acc_ref = rest[-1]
        i, kk = pl.program_id(0), pl.program_id(2)

        @pl.when(kk == 0)
        def _():
            acc_ref[...] = jnp.zeros_like(acc_ref)

        acc_ref[...] += _dot(dy_ref[...].astype(BF16), w_ref[...], NT)

        if not fused:
            @pl.when(kk == nk - 1)
            def _():
                rest[-2][...] = acc_ref[...].astype(rest[-2].dtype)
            return
        x_ref, g_ref, dres_ref = rest[:3]
        dx_ref, dg_ref = rest[-3], rest[-2]

        @pl.when(jnp.logical_and(i == 0, kk == 0))
        def _():
            dg_ref[...] = jnp.zeros_like(dg_ref)

        @pl.when(kk == nk - 1)
        def _():
            dxn = acc_ref[...]
            xhat, rstd = _rms_stats(x_ref[...])
            gd = dxn * g_ref[...]
            dx_ref[...] = dres_ref[...] + rstd * (gd - xhat * jnp.mean(gd * xhat, axis=-1, keepdims=True))
            dg_ref[...] += jnp.sum(dxn * xhat, axis=0, keepdims=True)

    row = pl.BlockSpec((tm, tko), lambda i, j, kk: (i, j))
    vec = pl.BlockSpec((1, k), lambda i, j, kk: (0, 0))
    return pl.pallas_call(
        body,
        name=name,
        grid=(m // tm, k // tko, nk),
        in_specs=[
            _dy_spec(stacked, tm, tn, npb, lambda g: g[0], lambda g: g[2]),
            pl.BlockSpec((None, tko, tn), lambda i, j, kk: (kk // npb, j, kk % npb)),
        ] + ([row, vec, row] if fused else []) + _dep_specs(deps),
        out_specs=[row, vec] if fused else row,
        out_shape=[jax.ShapeDtypeStruct((m, k), F32), jax.ShapeDtypeStruct((1, k), F32)] if fused else jax.ShapeDtypeStruct((m, k), out_dtype),
        scratch_shapes=[pltpu.VMEM((tm, tko), F32)],
        compiler_params=_cparams(("arbitrary",) * 3 if fused else ("parallel", "parallel", "arbitrary")),
    )(dy, w, *(norm_of or ()), *deps)


def _mm_tn(a, dy, s, ns, stacked=False, name="mm_tn", deps=(), gain=None):
    m, k = a.shape
    tm = min(m, MM_ROWS)
    tk = _pick(k, (1024, 1408, 512, 256))
    tn = _pick(ns, (1024, 1408, 512, 256, 128))
    npb = ns // tn
    nm = m // tm
    assert gain is None or tk == k

    def body(a_ref, dy_ref, *rest):
        j, mm = pl.program_id(1), pl.program_id(2)
        if gain is None:
            o_ref, acc_ref = rest[-2:]
            lhs = a_ref[...].astype(BF16)
        else:
            o_ref, acc_ref, xn_ref = rest[-3:]

            @pl.when(j == 0)
            def _():
                xn_ref[mm] = _mm_operand(a_ref, rest[0])

            lhs = xn_ref[mm]

        @pl.when(mm == 0)
        def _():
            acc_ref[...] = jnp.zeros_like(acc_ref)

        acc_ref[...] += _dot(lhs, dy_ref[...].astype(BF16), TN)

        @pl.when(mm == nm - 1)
        def _():
            o_ref[...] = acc_ref[...]

    a_rows = (lambda i, j, mm: (mm, i)) if gain is None else (lambda i, j, mm: (jnp.where(j == 0, mm, 0), i))
    return pl.pallas_call(
        body,
        name=name,
        grid=(k // tk, s * npb, nm),
        in_specs=[
            pl.BlockSpec((tm, tk), a_rows),
            _dy_spec(stacked, tm, tn, npb, lambda g: g[2], lambda g: g[1]),
        ] + ([pl.BlockSpec((1, k), lambda i, j, mm: (0, 0))] if gain is not None else []) + _dep_specs(deps),
        out_specs=pl.BlockSpec((None, tk, tn), lambda i, j, mm: (j // npb, i, j % npb)),
        out_shape=jax.ShapeDtypeStruct((s, k, ns), F32),
        scratch_shapes=[pltpu.VMEM((tk, tn), F32)] + ([pltpu.VMEM((nm, tm, tk), BF16)] if gain is not None else []),
        compiler_params=_cparams(("parallel", "arbitrary", "arbitrary") if gain is not None else ("parallel", "parallel", "arbitrary")),
    )(a, dy, *(() if gain is None else (gain,)), *deps)


ROW_TILE = 512


def _loss_head(h, g, target):
    t, d = h.shape
    r = min(t, ROW_TILE)

    def body(h_ref, g_ref, t_ref, dh_ref, dg_ref, loss_ref):
        @pl.when(pl.program_id(0) == 0)
        def _():
            dg_ref[...] = jnp.zeros_like(dg_ref)
            loss_ref[...] = jnp.zeros_like(loss_ref)

        xv = h_ref[...]
        rstd = lax.rsqrt(jnp.mean(xv * xv, axis=-1, keepdims=True) + EPS)
        xhat = xv * rstd
        gv = g_ref[...]
        err = xhat * gv - t_ref[...]
        loss_ref[...] += 0.5 * jnp.sum(jnp.mean(err * err, axis=-1, keepdims=True), axis=0, keepdims=True)
        dy = err * (1.0 / d)
        gd = dy * gv
        dh_ref[...] = rstd * (gd - xhat * jnp.mean(gd * xhat, axis=-1, keepdims=True))
        dg_ref[...] += jnp.sum(dy * xhat, axis=0, keepdims=True)

    return pl.pallas_call(
        body,
        name="loss_head",
        grid=(t // r,),
        in_specs=[
            pl.BlockSpec((r, d), lambda i: (i, 0)),
            pl.BlockSpec((1, d), lambda i: (0, 0)),
            pl.BlockSpec((r, d), lambda i: (i, 0)),
        ],
        out_specs=[
            pl.BlockSpec((r, d), lambda i: (i, 0)),
            pl.BlockSpec((1, d), lambda i: (0, 0)),
            pl.BlockSpec((1, LANES), lambda i: (0, 0)),
        ],
        out_shape=[
            jax.ShapeDtypeStruct((t, d), F32),
            jax.ShapeDtypeStruct((1, d), F32),
            jax.ShapeDtypeStruct((1, LANES), F32),
        ],
        compiler_params=_cparams(("arbitrary",)),
    )(h, g, target)


CONV_ROWS = 256
CONV_COLS = 1408


def _conv_taps(x_ext, n):
    tot = x_ext.shape[0]
    g1 = pltpu.roll(x_ext, 1, 0)[tot - n:]
    g2 = pltpu.roll(x_ext, 2, 0)[tot - n:]
    return g2, g1


def _conv_fwd(up, conv_w, conv_b, name="conv_fwd"):
    t = up.shape[0]
    r = min(t, CONV_ROWS)
    tc = CONV_COLS
    ncb = D_FF // tc
    hb = r // SUBLANES

    def body(g_ref, halo_ref, v_ref, w_ref, b_ref, o_ref, c_ref):
        i = pl.program_id(1)
        g0 = g_ref[...]
        halo = halo_ref[...] * jnp.where(i > 0, 1.0, 0.0)
        g2, g1 = _conv_taps(jnp.concatenate([halo, g0], axis=0), r)
        c = b_ref[...] + w_ref[0:1, :] * g2 + w_ref[1:2, :] * g1 + w_ref[2:3, :] * g0
        c_ref[...] = c
        o_ref[...] = (c * _sigmoid(c) * v_ref[...]).astype(BF16)

    blk = pl.BlockSpec((r, tc), lambda j, i: (i, j))
    return pl.pallas_call(
        body,
        name=name,
        grid=(ncb, t // r),
        in_specs=[
            blk,
            pl.BlockSpec((SUBLANES, tc), lambda j, i: (jnp.maximum(i * hb - 1, 0), j)),
            pl.BlockSpec((r, tc), lambda j, i: (i, ncb + j)),
            pl.BlockSpec((3, tc), lambda j, i: (0, j)),
            pl.BlockSpec((1, tc), lambda j, i: (0, j)),
        ],
        out_specs=[blk, blk],
        out_shape=[jax.ShapeDtypeStruct((t, D_FF), BF16), jax.ShapeDtypeStruct((t, D_FF), F32)],
        compiler_params=_cparams(("parallel", "parallel")),
    )(up, up, up, conv_w, conv_b)


def _conv_bwd(up, conv_w, c, dact, name="conv_bwd"):
    t = up.shape[0]
    r = min(t, CONV_ROWS)
    tc = CONV_COLS
    ncb = D_FF // tc
    nrt = t // r

    def body(g_ref, v_ref, w_ref, c_ref, da_ref, dup_ref, dw_ref, db_ref, nxt_ref):
        ii = pl.program_id(1)

        @pl.when(ii == 0)
        def _():
            nxt_ref[...] = jnp.zeros_like(nxt_ref)
            dw_ref[...] = jnp.zeros_like(dw_ref)
            db_ref[...] = jnp.zeros_like(db_ref)

        g0 = g_ref[...]
        w0, w1, w2 = w_ref[0:1, :], w_ref[1:2, :], w_ref[2:3, :]
        c = c_ref[...]
        sg = _sigmoid(c)
        da = da_ref[...]
        dup_ref[1] = (da * (c * sg)).astype(BF16)
        dc = da * v_ref[...] * (sg * (1.0 + c * (1.0 - sg)))
        ext = jnp.concatenate([dc, nxt_ref[...]], axis=0)
        tot = r + SUBLANES
        d1 = pltpu.roll(ext, tot - 1, 0)[:r]
        d2 = pltpu.roll(ext, tot - 2, 0)[:r]
        nxt_ref[...] = dc[:SUBLANES]
        dup_ref[0] = (w2 * dc + w1 * d1 + w0 * d2).astype(BF16)
        db_ref[...] += jnp.sum(dc, axis=0, keepdims=True)
        dw_ref[0:1, :] += jnp.sum(d2 * g0, axis=0, keepdims=True)
        dw_ref[1:2, :] += jnp.sum(d1 * g0, axis=0, keepdims=True)
        dw_ref[2:3, :] += jnp.sum(dc * g0, axis=0, keepdims=True)

    rev = lambda ii: nrt - 1 - ii
    dup, dw, db = pl.pallas_call(
        body,
        name=name,
        grid=(ncb, nrt),
        in_specs=[
            pl.BlockSpec((r, tc), lambda j, ii: (rev(ii), j)),
            pl.BlockSpec((r, tc), lambda j, ii: (rev(ii), ncb + j)),
            pl.BlockSpec((3, tc), lambda j, ii: (0, j)),
            pl.BlockSpec((r, tc), lambda j, ii: (rev(ii), j)),
            pl.BlockSpec((r, tc), lambda j, ii: (rev(ii), j)),
        ],
        out_specs=[
            pl.BlockSpec((2, None, r, tc), lambda j, ii: (0, j, rev(ii), 0)),
            pl.BlockSpec((3, tc), lambda j, ii: (0, j)),
            pl.BlockSpec((1, tc), lambda j, ii: (0, j)),
        ],
        out_shape=[
            jax.ShapeDtypeStruct((2, ncb, t, tc), BF16),
            jax.ShapeDtypeStruct((3, D_FF), F32),
            jax.ShapeDtypeStruct((1, D_FF), F32),
        ],
        scratch_shapes=[pltpu.VMEM((SUBLANES, tc), F32)],
        compiler_params=_cparams(("parallel", "arbitrary")),
    )(up, up, conv_w, c, dact)
    return dup.reshape(2 * ncb, t, tc), dw, db


def _split3(x):
    x1 = x.astype(BF16)
    r1 = x - x1.astype(F32)
    x2 = r1.astype(BF16)
    x3 = (r1 - x2.astype(F32)).astype(BF16)
    return x1, x2, x3


def _tri_dot(tri, x, dims):
    x1, x2, x3 = _split3(x)
    return _dot(tri, x1, dims) + _dot(tri, x2, dims) + _dot(tri, x3, dims)


def _lower_bound(logits_ref):
    return _sigmoid(logits_ref[0:1, :] - logits_ref[1:2, :])


def _hg_gates(qr, fr, lb):
    q = qr * _sigmoid(qr) * (HG_DK ** -0.5)
    sf = _sigmoid(fr)
    fg = lb + (1.0 - lb) * sf
    return q, sf, fg


def _hg_chunk_terms(q, fg, tril_b, low_half):
    g = jnp.log(fg)
    k = 1.0 - fg
    cum = _tri_dot(tril_b, g, NN)
    c_last = jnp.sum(g, axis=0, keepdims=True)
    c_mid = jnp.sum(jnp.where(low_half, g, 0.0), axis=0, keepdims=True)
    e_q = jnp.exp(cum - c_mid)
    e_k = jnp.exp(c_mid - cum)
    e_0 = jnp.exp(cum)
    e_l = jnp.exp(c_last - cum)
    return k, e_q, e_k, e_0, e_l, jnp.exp(c_last)


HG_BLOCK = 256


def _hg_proj_specs(rb, row):
    return [pl.BlockSpec((rb, D_MODEL), functools.partial(lambda i, k: (row(i), k), k=k)) for k in range(4)]


def _hg_consts(c):
    tril = lax.broadcasted_iota(jnp.int32, (c, c), 0) >= lax.broadcasted_iota(jnp.int32, (c, c), 1)
    low_half = lax.broadcasted_iota(jnp.int32, (c, D_MODEL), 0) < c // 2
    return tril, tril.astype(BF16), low_half


def _hgrn_fwd(proj, lb, wn):
    t = proj.shape[0]
    c = HG_CHUNK
    rb = min(t, HG_BLOCK)
    cpb = rb // c

    def body(q_ref, f_ref, i_ref, g_ref, lb_ref, wn_ref, o_ref, y_ref, st_ref, s_scr):
        @pl.when(pl.program_id(0) == 0)
        def _():
            s_scr[...] = jnp.zeros_like(s_scr)

        lb_all = _lower_bound(lb_ref)
        wnv = wn_ref[...]
        tril, tril_b, low_half = _hg_consts(c)

        def chunk(n, carry):
            rows = pl.ds(pl.multiple_of(n * c, c), c)
            q, _, fg = _hg_gates(q_ref[rows, :], f_ref[rows, :], lb_all)
            k, e_q, e_k, e_0, e_l, e_last = _hg_chunk_terms(q, fg, tril_b, low_half)
            qi, ki, q0, kl = (q * e_q).astype(BF16), (k * e_k).astype(BF16), (q * e_0).astype(BF16), (k * e_l).astype(BF16)
            v = i_ref[rows, :].astype(BF16)
            gr = g_ref[rows, :]
            gate = gr * _sigmoid(gr)
            for h in range(HG_HEADS):
                cols = slice(h * HG_DK, (h + 1) * HG_DK)
                st = s_scr[h]
                st_ref[h, n] = st
                a = jnp.where(tril, _dot(qi[:, cols], ki[:, cols], NT), 0.0)
                o = _dot(q0[:, cols], st.astype(BF16), NT) + _dot(a.astype(BF16), v[:, cols], NN)
                s_scr[h] = st * e_last[:, cols] + _dot(v[:, cols], kl[:, cols], TN)
                o_ref[rows, cols] = o
                rstd = lax.rsqrt(jnp.mean(o * o, axis=-1, keepdims=True) + EPS)
                y_ref[rows, cols] = (o * rstd * wnv * gate[:, cols]).astype(BF16)
            return carry

        lax.fori_loop(0, cpb, chunk, 0)

    blk = pl.BlockSpec((rb, D_MODEL), lambda i: (i, 0))
    return pl.pallas_call(
        body,
        name="hgrn_fwd",
        grid=(t // rb,),
        in_specs=_hg_proj_specs(rb, lambda i: i) + [pl.BlockSpec((2, D_MODEL), lambda i: (0, 0)), pl.BlockSpec((1, HG_DK), lambda i: (0, 0))],
        out_specs=[blk, blk, pl.BlockSpec((HG_HEADS, cpb, HG_DK, HG_DK), lambda i: (0, i, 0, 0))],
        out_shape=[
            jax.ShapeDtypeStruct((t, D_MODEL), F32),
            jax.ShapeDtypeStruct((t, D_MODEL), BF16),
            jax.ShapeDtypeStruct((HG_HEADS, t // c, HG_DK, HG_DK), F32),
        ],
        scratch_shapes=[pltpu.VMEM((HG_HEADS, HG_DK, HG_DK), F32)],
        compiler_params=_cparams(("arbitrary",)),
    )(proj, proj, proj, proj, lb, wn)


def _hgrn_bwd(proj, lb, wn, o, states, dy):
    t = proj.shape[0]
    c = HG_CHUNK
    rb = min(t, HG_BLOCK)
    cpb = rb // c
    nb = t // rb

    def body(q_ref, f_ref, i_ref, g_ref, lb_ref, wn_ref, o_ref, st_ref, dy_ref, dp_ref, dl_ref, dwn_ref, ds_scr, dlb_scr):
        step = pl.program_id(0)

        @pl.when(step == 0)
        def _():
            dwn_ref[...] = jnp.zeros_like(dwn_ref)
            ds_scr[...] = jnp.zeros_like(ds_scr)
            dlb_scr[...] = jnp.zeros_like(dlb_scr)

        lb_all = _lower_bound(lb_ref)
        wnv = wn_ref[...]
        tril, tril_b, low_half = _hg_consts(c)

        def chunk(nn, carry):
            n = cpb - 1 - nn
            rows = pl.ds(pl.multiple_of(n * c, c), c)
            qr = q_ref[rows, :]
            gr = g_ref[rows, :]
            q, sf, fg = _hg_gates(qr, f_ref[rows, :], lb_all)
            k, e_q, e_k, e_0, e_l, e_last = _hg_chunk_terms(q, fg, tril_b, low_half)
            qi, qi_lo, _ = _split3(q * e_q)
            ki, ki_lo, _ = _split3(k * e_k)
            q0 = (q * e_0).astype(BF16)
            kl = (k * e_l).astype(BF16)
            v = i_ref[rows, :].astype(BF16)
            sg = _sigmoid(gr)
            silu_g = gr * sg
            dsilu_g = sg * (1.0 + gr * (1.0 - sg))
            dqs, dks, d_lasts = [], [], []
            for h in range(HG_HEADS):
                cols = slice(h * HG_DK, (h + 1) * HG_DK)
                ov = o_ref[rows, cols]
                dyv = dy_ref[rows, cols].astype(F32)
                rstd = lax.rsqrt(jnp.mean(ov * ov, axis=-1, keepdims=True) + EPS)
                ohat = ov * rstd
                dp_ref[3, rows, cols] = (dyv * (ohat * wnv) * dsilu_g[:, cols]).astype(BF16)
                don = dyv * silu_g[:, cols]
                dwn_ref[...] += jnp.sum(don * ohat, axis=0, keepdims=True)
                gd = don * wnv
                do_b = (rstd * (gd - ohat * jnp.mean(gd * ohat, axis=-1, keepdims=True))).astype(BF16)
                st = st_ref[h, n]
                ds = ds_scr[h]
                ds_b = ds.astype(BF16)
                vh, kh = v[:, cols], k[:, cols]
                a_b = jnp.where(tril, _dot(qi[:, cols], ki[:, cols], NT), 0.0).astype(BF16)
                da_b = jnp.where(tril, _dot(do_b, vh, NT), 0.0).astype(BF16)
                dqs.append(_dot(do_b, st.astype(BF16), NN) * e_0[:, cols]
                           + (_dot(da_b, ki[:, cols], NN) + _dot(da_b, ki_lo[:, cols], NN)) * e_q[:, cols])
                dk_state = _dot(vh, ds_b, NN) * e_l[:, cols]
                dks.append((_dot(da_b, qi[:, cols], TN) + _dot(da_b, qi_lo[:, cols], TN)) * e_k[:, cols] + dk_state)
                dp_ref[2, rows, cols] = (_dot(a_b, do_b, TN) + _dot(kl[:, cols], ds_b, NT)).astype(BF16)
                ds_scr[h] = ds * e_last[:, cols] + _dot(do_b, q0[:, cols], TN)
                d_lasts.append(jnp.sum(dk_state * kh, axis=0, keepdims=True) + jnp.sum(ds * st, axis=0, keepdims=True) * e_last[:, cols])
            dq = jnp.concatenate(dqs, axis=1)
            dk = jnp.concatenate(dks, axis=1)
            dlogf = _tri_dot(tril_b, q * dq - k * dk, TN) + jnp.concatenate(d_lasts, axis=1)
            dfg = dlogf / fg - dk
            dlb_scr[...] += jnp.sum(dfg * (1.0 - sf), axis=0, keepdims=True)
            sq = _sigmoid(qr)
            dp_ref[0, rows, :] = (dq * (HG_DK ** -0.5) * (sq * (1.0 + qr * (1.0 - sq)))).astype(BF16)
            dp_ref[1, rows, :] = (dfg * (1.0 - lb_all) * sf * (1.0 - sf)).astype(BF16)
            return carry

        lax.fori_loop(0, cpb, chunk, 0)

        @pl.when(step == nb - 1)
        def _():
            d0 = dlb_scr[...] * lb_all * (1.0 - lb_all)
            dl_ref[0:1, :] = d0
            dl_ref[1:2, :] = -d0

    rev = lambda i: nb - 1 - i
    blk = pl.BlockSpec((rb, D_MODEL), lambda i: (rev(i), 0))
    return pl.pallas_call(
        body,
        name="hgrn_bwd",
        grid=(nb,),
        in_specs=_hg_proj_specs(rb, rev)
        + [pl.BlockSpec((2, D_MODEL), lambda i: (0, 0)), pl.BlockSpec((1, HG_DK), lambda i: (0, 0)), blk,
           pl.BlockSpec((HG_HEADS, cpb, HG_DK, HG_DK), lambda i: (0, rev(i), 0, 0)), blk],
        out_specs=[
            pl.BlockSpec((4, rb, D_MODEL), lambda i: (0, rev(i), 0)),
            pl.BlockSpec((2, D_MODEL), lambda i: (0, 0)),
            pl.BlockSpec((1, HG_DK), lambda i: (0, 0)),
        ],
        out_shape=[
            jax.ShapeDtypeStruct((4, t, D_MODEL), BF16),
            jax.ShapeDtypeStruct((2, D_MODEL), F32),
            jax.ShapeDtypeStruct((1, HG_DK), F32),
        ],
        scratch_shapes=[pltpu.VMEM((HG_HEADS, HG_DK, HG_DK), F32), pltpu.VMEM((1, D_MODEL), F32)],
        compiler_params=_cparams(("arbitrary",)),
    )(proj, proj, proj, proj, lb, wn, o, states, dy)


ATT_STACK = 8


def _att_stack(q_ref, sink_ref, first, lo, bias_p, bias_c, extra_ref=None):
    qs, bps, bcs, sinks, extras = [], [], [], None, []
    rows = lax.broadcasted_iota(jnp.int32, (ATT_STACK * WINDOW, 1), 0)
    for i in range(ATT_STACK):
        hq = first + i
        cols = slice((hq // 2) * LANES, (hq // 2 + 1) * LANES)
        sel = lo if hq % 2 == 0 else jnp.logical_not(lo)
        qp = q_ref[:, cols] * (ATT_HD ** -0.5)
        qs.append(jnp.where(sel, qp, jnp.zeros_like(qp)))
        bps.append(ALIBI_SLOPES[hq] * bias_p)
        bcs.append(ALIBI_SLOPES[hq] * bias_c)
        sinks = sink_ref[hq] if sinks is None else jnp.where(rows < i * WINDOW, sinks, sink_ref[hq])
        if extra_ref is not None:
            ep = extra_ref[:, cols]
            extras.append(jnp.where(sel, ep, jnp.zeros_like(ep)))
    cat = lambda parts: jnp.concatenate(parts, axis=0)
    return cat(qs), cat(bps), cat(bcs), sinks, (cat(extras) if extras else None)


def _att_rows(i):
    return slice(i * WINDOW, (i + 1) * WINDOW)


def _att_bias(n):
    tq = lax.broadcasted_iota(jnp.int32, (WINDOW, WINDOW), 0)
    sk = lax.broadcasted_iota(jnp.int32, (WINDOW, WINDOW), 1)
    valid_c = sk <= tq
    valid_p = (sk - tq) > jnp.where(n > 0, 0, WINDOW)
    dist_c = (tq - sk).astype(F32)
    return jnp.where(valid_p, -dist_c - float(WINDOW), NEG), jnp.where(valid_c, -dist_c, NEG)


def _att_halves(x, lo, kh):
    r = pltpu.roll(x, ATT_HD, 1)
    zero = jnp.zeros_like(x)
    if kh == 0:
        return jnp.where(lo, x, r), jnp.where(lo, x, zero), jnp.where(lo, zero, r)
    return jnp.where(lo, r, x), jnp.where(lo, r, zero), jnp.where(lo, zero, x)


def _att_probs(qm, k2p, k2c, bias_p, bias_c, sink):
    sp = _dot(qm, k2p, NT) + bias_p
    sc = _dot(qm, k2c, NT) + bias_c
    m = jnp.maximum(jnp.maximum(jnp.max(sp, axis=-1, keepdims=True), jnp.max(sc, axis=-1, keepdims=True)), sink)
    ep = jnp.exp(sp - m)
    ec = jnp.exp(sc - m)
    es = jnp.exp(sink - m)
    inv = 1.0 / (jnp.sum(ep, axis=-1, keepdims=True) + jnp.sum(ec, axis=-1, keepdims=True) + es)
    return ep * inv, ec * inv, es * inv


def _attn_fwd(q, kv, sinks):
    t = q.shape[0]
    nb = t // WINDOW

    def body(sink_ref, q_ref, kvp_ref, kvc_ref, o_ref):
        n = pl.program_id(0)
        bias_p, bias_c = _att_bias(n)
        lo = lax.broadcasted_iota(jnp.int32, (WINDOW, LANES), 1) < ATT_HD
        for kh in range(ATT_KVH):
            k2p, _, _ = _att_halves(kvp_ref[:, 0:LANES], lo, kh)
            k2c, _, _ = _att_halves(kvc_ref[:, 0:LANES], lo, kh)
            _, vlo_p, vhi_p = _att_halves(kvp_ref[:, LANES:2 * LANES], lo, kh)
            _, vlo_c, vhi_c = _att_halves(kvc_ref[:, LANES:2 * LANES], lo, kh)
            for first in range(kh * ATT_GROUP, (kh + 1) * ATT_GROUP, ATT_STACK):
                qs, bp, bc, sinks, _ = _att_stack(q_ref, sink_ref, first, lo, bias_p, bias_c)
                pp, pc, _ = _att_probs(qs, k2p, k2c, bp, bc, sinks)
                pp, pc = pp.astype(BF16), pc.astype(BF16)
                for i in range(0, ATT_STACK, 2):
                    even, odd = _att_rows(i), _att_rows(i + 1)
                    out = (_dot(pp[even], vlo_p, NN) + _dot(pc[even], vlo_c, NN)
                           + _dot(pp[odd], vhi_p, NN) + _dot(pc[odd], vhi_c, NN))
                    j = (first + i) // 2
                    o_ref[:, j * LANES:(j + 1) * LANES] = out.astype(BF16)

    return pl.pallas_call(
        body,
        name="attn_fwd",
        grid=(nb,),
        in_specs=[
            pl.BlockSpec(memory_space=pltpu.SMEM),
            pl.BlockSpec((WINDOW, D_MODEL), lambda n: (n, 0)),
            pl.BlockSpec((WINDOW, 2 * LANES), lambda n: (jnp.maximum(n - 1, 0), 0)),
            pl.BlockSpec((WINDOW, 2 * LANES), lambda n: (n, 0)),
        ],
        out_specs=pl.BlockSpec((WINDOW, D_MODEL), lambda n: (n, 0)),
        out_shape=jax.ShapeDtypeStruct((t, D_MODEL), BF16),
        compiler_params=_cparams(("parallel",)),
    )(sinks, q, kv, kv)


def _attn_bwd(q, kv, sinks, dout):
    t = q.shape[0]
    nb = t // WINDOW

    def body(sink_ref, q_ref, kvp_ref, kvc_ref, do_ref, dq_ref, dkv_ref, dsink_ref, carry_ref):
        n = pl.program_id(0)

        @pl.when(n == 0)
        def _():
            carry_ref[...] = jnp.zeros_like(carry_ref)
            dsink_ref[...] = jnp.zeros_like(dsink_ref)

        @pl.when(n == nb)
        def _():
            dkv_ref[...] = carry_ref[...].astype(BF16)

        @pl.when(n < nb)
        def _():
            bias_p, bias_c = _att_bias(n)
            lo = lax.broadcasted_iota(jnp.int32, (WINDOW, LANES), 1) < ATT_HD
            lane1 = lax.broadcasted_iota(jnp.int32, (1, LANES), 1)
            dsink = jnp.zeros((1, LANES), F32)
            halves = []
            for kh in range(ATT_KVH):
                k2p, klo_p, khi_p = _att_halves(kvp_ref[:, 0:LANES], lo, kh)
                k2c, klo_c, khi_c = _att_halves(kvc_ref[:, 0:LANES], lo, kh)
                v2p, _, _ = _att_halves(kvp_ref[:, LANES:2 * LANES], lo, kh)
                v2c, _, _ = _att_halves(kvc_ref[:, LANES:2 * LANES], lo, kh)
                acc = [jnp.zeros((WINDOW, LANES), F32) for _ in range(4)]
                for first in range(kh * ATT_GROUP, (kh + 1) * ATT_GROUP, ATT_STACK):
                    qs, bp, bc, sinks, dos = _att_stack(q_ref, sink_ref, first, lo, bias_p, bias_c, do_ref)
                    pp, pc, ps = _att_probs(qs, k2p, k2c, bp, bc, sinks)
                    dpp = _dot(dos, v2p, NT)
                    dpc = _dot(dos, v2c, NT)
                    delta = jnp.sum(pp * dpp, axis=-1, keepdims=True) + jnp.sum(pc * dpc, axis=-1, keepdims=True)
                    dsp = (pp * (dpp - delta)).astype(BF16)
                    dsc = (pc * (dpc - delta)).astype(BF16)
                    sink_term = ps * delta
                    for i in range(ATT_STACK):
                        dsink = dsink + jnp.where(lane1 == first + i, -jnp.sum(sink_term[_att_rows(i)], axis=0, keepdims=True), 0.0)
                    for i in range(0, ATT_STACK, 2):
                        even, odd = _att_rows(i), _att_rows(i + 1)
                        dq_pair = (_dot(dsp[even], klo_p, NN) + _dot(dsc[even], klo_c, NN)
                                   + _dot(dsp[odd], khi_p, NN) + _dot(dsc[odd], khi_c, NN))
                        j = (first + i) // 2
                        dq_ref[:, j * LANES:(j + 1) * LANES] = (dq_pair * (ATT_HD ** -0.5)).astype(BF16)
                    acc[0] = acc[0] + _dot(dsp, qs, TN)
                    acc[1] = acc[1] + _dot(dsc, qs, TN)
                    acc[2] = acc[2] + _dot(pp.astype(BF16), dos, TN)
                    acc[3] = acc[3] + _dot(pc.astype(BF16), dos, TN)
                halves.append([a + pltpu.roll(a, ATT_HD, 1) for a in acc])
            prev = jnp.concatenate(
                [jnp.where(lo, halves[0][0], halves[1][0]), jnp.where(lo, halves[0][2], halves[1][2])], axis=1)
            cur = jnp.concatenate(
                [jnp.where(lo, halves[0][1], halves[1][1]), jnp.where(lo, halves[0][3], halves[1][3])], axis=1)
            dkv_ref[...] = (carry_ref[...] + prev).astype(BF16)
            carry_ref[...] = cur
            dsink_ref[...] += dsink

    blk = lambda n: jnp.minimum(n, nb - 1)
    return pl.pallas_call(
        body,
        name="attn_bwd",
        grid=(nb + 1,),
        in_specs=[
            pl.BlockSpec(memory_space=pltpu.SMEM),
            pl.BlockSpec((WINDOW, D_MODEL), lambda n: (blk(n), 0)),
            pl.BlockSpec((WINDOW, 2 * LANES), lambda n: (jnp.maximum(blk(n) - 1, 0), 0)),
            pl.BlockSpec((WINDOW, 2 * LANES), lambda n: (blk(n), 0)),
            pl.BlockSpec((WINDOW, D_MODEL), lambda n: (blk(n), 0)),
        ],
        out_specs=[
            pl.BlockSpec((WINDOW, D_MODEL), lambda n: (blk(n), 0)),
            pl.BlockSpec((WINDOW, 2 * LANES), lambda n: (jnp.maximum(n - 1, 0), 0)),
            pl.BlockSpec((1, LANES), lambda n: (0, 0)),
        ],
        out_shape=[
            jax.ShapeDtypeStruct((t, D_MODEL), BF16),
            jax.ShapeDtypeStruct((t, 2 * LANES), BF16),
            jax.ShapeDtypeStruct((1, LANES), F32),
        ],
        scratch_shapes=[pltpu.VMEM((WINDOW, 2 * LANES), F32)],
        compiler_params=_cparams(("arbitrary",)),
    )(sinks, q, kv, kv, dout)


def _ffn_fwd(h, norm_g, w_up, conv_w, conv_b, w_down, tag, after_up=lambda up: None):
    up = _mm_nn(h, w_up, gain=norm_g, name=f"ffn{tag}_up")
    after_up(up)
    act, c = _conv_fwd(up, conv_w, conv_b, name=f"ffn{tag}_conv")
    h_out = _mm_nn(act, w_down, res=h, name=f"ffn{tag}_down")
    return h_out, (up, act, c)


def _ffn_bwd(dh, h, norm_g, w_up, conv_w, conv_b, w_down, saved, tag, deps=()):
    up, act, c = saved
    dw_down = _mm_tn(act, dh, 1, D_MODEL, name=f"ffn{tag}_dwdown", deps=deps)
    dact = _mm_nt(dh, w_down, name=f"ffn{tag}_dact", deps=deps)
    dup, dconv_w, dconv_b = _conv_bwd(up, conv_w, c, dact, name=f"ffn{tag}_dconv")
    dw_up = _mm_tn(h, dup, N_CHIPS, CONV_COLS, stacked=True, gain=norm_g, name=f"ffn{tag}_dwup")
    dh_in, dnorm = _mm_nt(dup, w_up, stacked=True, norm_of=(h, norm_g, dh), name=f"ffn{tag}_dxn")
    return dh_in, dict(ffn_w_down=dw_down, ffn_w_up=dw_up, ffn_conv_w=dconv_w, ffn_conv_b=dconv_b, ffn_norm=dnorm)


def _local_step(x, target, w, fetch=lambda w, stage, after: w, hook=lambda point, dh, grads: ()):
    proj = _mm_nn(x, w["hg_w_in"], gain=w["hg_norm"], name="hg_in")
    o, y, states = _hgrn_fwd(proj, w["hg_lb"], w["hg_out_norm"])
    w = fetch(w, "mixer_out", y)
    fetch(w, "layer0_relay", y)
    h_a = _mm_nn(y, w["hg_w_out"], res=x, name="hg_out")
    w = fetch(w, "layer0", h_a)
    h1, ffn0 = _ffn_fwd(h_a, w["ffn_norm"][0], w["ffn_w_up"][0], w["ffn_conv_w"][0], w["ffn_conv_b"][0], w["ffn_w_down"][0], 0,
                        lambda up: fetch(w, "layer1_relay", up))
    w = fetch(w, "layer1", h1)
    kv = _mm_nn(h1, w["w_kv"], gain=w["kv_norm"], out_dtype=BF16, name="kv_proj")
    qa = _mm_nn(h1, w["attn_w_q"], gain=w["attn_norm"], out_dtype=BF16, name="attn_q")
    ao = _attn_fwd(qa, kv, w["attn_sinks"])
    h_b = _mm_nn(ao, w["attn_w_o"], res=h1, name="attn_o")
    h2, ffn1 = _ffn_fwd(h_b, w["ffn_norm"][1], w["ffn_w_up"][1], w["ffn_conv_w"][1], w["ffn_conv_b"][1], w["ffn_w_down"][1], 1)
    dh2, d_final, loss = _loss_head(h2, w["final_norm"], target)

    dh_b, g1 = _ffn_bwd(dh2, h_b, w["ffn_norm"][1], w["ffn_w_up"][1], w["ffn_conv_w"][1], w["ffn_conv_b"][1], w["ffn_w_down"][1], ffn1, 1)
    deps = hook("ffn1", dh_b, g1)
    dw_o = _mm_tn(ao, dh_b, 1, D_MODEL, name="attn_dwo", deps=deps)
    dao = _mm_nt(dh_b, w["attn_w_o"], out_dtype=BF16, name="attn_dao", deps=deps)
    dqa, dkv, dsinks = _attn_bwd(qa, kv, w["attn_sinks"], dao)
    dw_q = _mm_tn(h1, dqa, 1, D_MODEL, gain=w["attn_norm"], name="attn_dwq")
    dh1, d_attn_norm = _mm_nt(dqa, w["attn_w_q"], norm_of=(h1, w["attn_norm"], dh_b), name="attn_dxa")
    dw_kv = _mm_tn(h1, dkv, 1, 2 * LANES, gain=w["kv_norm"], name="kv_dw")
    dh1, d_kv_norm = _mm_nt(dkv, w["w_kv"], norm_of=(h1, w["kv_norm"], dh1), name="kv_dx")
    deps = hook("attn", dh1, dict(attn_w_o=dw_o, attn_w_q=dw_q, w_kv=dw_kv))
    dh_a, g0 = _ffn_bwd(dh1, h_a, w["ffn_norm"][0], w["ffn_w_up"][0], w["ffn_conv_w"][0], w["ffn_conv_b"][0], w["ffn_w_down"][0], ffn0, 0, deps)
    dw_out = _mm_tn(y, dh_a, 1, D_MODEL, name="hg_dwout")
    deps = hook("ffn0", dh_a, dict(g0, hg_w_out=dw_out))
    dy = _mm_nt(dh_a, w["hg_w_out"], out_dtype=BF16, name="hg_dy", deps=deps)
    dproj, dlb, d_out_norm = _hgrn_bwd(proj, w["hg_lb"], w["hg_out_norm"], o, states, dy)
    deps = hook("hgrn", dproj, None)
    dw_in = _mm_tn(x, dproj, N_CHIPS, D_MODEL, stacked=True, gain=w["hg_norm"], name="hg_dwin", deps=deps)
    deps = hook("hg_w", dproj, dict(hg_w_in=dw_in))
    dx, d_hg_norm = _mm_nt(dproj, w["hg_w_in"], stacked=True, norm_of=(x, w["hg_norm"], dh_a), name="hg_dxn", deps=deps)

    grads = dict(
        hg_norm=d_hg_norm, hg_w_in=dw_in, hg_lb=dlb, hg_out_norm=d_out_norm, hg_w_out=dw_out,
        kv_norm=d_kv_norm, w_kv=dw_kv, attn_norm=d_attn_norm, attn_w_q=dw_q, attn_sinks=dsinks, attn_w_o=dw_o,
        final_norm=d_final,
    )
    for name in g0:
        grads[name] = [g0[name], g1[name]]
    return loss, dx, grads


ANY = pl.BlockSpec(memory_space=pl.ANY)


def _place():
    x, y, c = lax.axis_index("x"), lax.axis_index("y"), lax.axis_index("c")
    chips = [(1 - x, y), (x, 1 - y), (1 - x, 1 - y)]
    return x, y, c, chips


def _rcopy(src, dst, send_sem, recv_sem, to):
    return pltpu.make_async_remote_copy(src_ref=src, dst_ref=dst, send_sem=send_sem, recv_sem=recv_sem, device_id=to, device_id_type=MESH)


HBM = pl.BlockSpec(memory_space=pltpu.HBM)
SEM = pl.BlockSpec(memory_space=pltpu.SEMAPHORE)
EFFECT = pltpu.SideEffectType.DATAFLOW_SIDE_EFFECTING


def _in_hbm(a):
    return pltpu.with_memory_space_constraint(a, pltpu.HBM)


def _place_shard(shard, place, dtype, name, deps=(), layer=None):
    r, cols = shard.shape[-2:]
    tr = _pick(r, ELEM_ROWS)
    src = pl.BlockSpec((tr, cols), lambda i, place_ref: (i, 0)) if layer is None else pl.BlockSpec((None, tr, cols), lambda i, place_ref: (layer, i, 0))

    def body(place_ref, s_ref, *rest):
        o_ref = rest[-1]
        o_ref[...] = s_ref[...].astype(o_ref.dtype)

    return pl.pallas_call(
        body,
        name=name,
        grid_spec=pltpu.PrefetchScalarGridSpec(
            num_scalar_prefetch=1,
            grid=(r // tr,),
            in_specs=[src] + _dep_specs(deps),
            out_specs=pl.BlockSpec((None, tr, cols), lambda i, place_ref: (place_ref[0], i, 0)),
        ),
        out_shape=jax.ShapeDtypeStruct((N_CHIPS, r, cols), dtype),
        compiler_params=_cparams(("parallel",)),
    )(place, shard, *deps)


def _start_copies(name, bufs, n_sem, copies):
    n = len(bufs)

    def body(*refs):
        for cp in copies(refs[:n], refs[n], refs[n + 1]):
            cp.start()
        refs[-1][...] = jnp.zeros_like(refs[-1])

    outs = pl.pallas_call(
        body,
        name=name,
        in_specs=[HBM] * n,
        out_specs=[SEM, SEM] + [HBM] * n + [pl.BlockSpec(memory_space=pltpu.VMEM)],
        out_shape=[pltpu.SemaphoreType.DMA((n_sem,)), pltpu.SemaphoreType.DMA((n_sem,))] + [pltpu.HBM(b.shape, b.dtype) for b in bufs]
        + [jax.ShapeDtypeStruct((SUBLANES, LANES), F32)],
        input_output_aliases={i: 2 + i for i in range(n)},
        compiler_params=pltpu.CompilerParams(has_side_effects=EFFECT),
    )(*[_in_hbm(b) for b in bufs])
    return outs[0], outs[1], list(outs[2:-1]), outs[-1]


def _wait_copies(name, bufs, send_sems, recv_sems, after, copies):
    n = len(bufs)

    def body(*refs):
        for cp in copies(refs[:n], refs[n], refs[n + 1]):
            cp.wait_send()
            cp.wait_recv()

    return pl.pallas_call(
        body,
        name=name,
        in_specs=[HBM] * n + [SEM, SEM, ANY],
        out_specs=[HBM] * n,
        out_shape=[pltpu.HBM(b.shape, b.dtype) for b in bufs],
        input_output_aliases={i: i for i in range(n)},
        compiler_params=pltpu.CompilerParams(has_side_effects=EFFECT),
    )(*bufs, send_sems, recv_sems, after)


def _relay_copies(name, bufs, send_sems, recv_sems, after, landed, n_sem, onward):
    n = len(bufs)

    def body(*refs):
        for cp in landed(refs[:n], refs[n], refs[n + 1]):
            cp.wait_send()
            cp.wait_recv()
        for cp in onward(refs[:n], refs[n + 3], refs[n + 4]):
            cp.start()
        refs[-1][...] = jnp.zeros_like(refs[-1])

    outs = pl.pallas_call(
        body,
        name=name,
        in_specs=[HBM] * n + [SEM, SEM, ANY],
        out_specs=[SEM, SEM] + [HBM] * n + [pl.BlockSpec(memory_space=pltpu.VMEM)],
        out_shape=[pltpu.SemaphoreType.DMA((n_sem,)), pltpu.SemaphoreType.DMA((n_sem,))] + [pltpu.HBM(b.shape, b.dtype) for b in bufs]
        + [jax.ShapeDtypeStruct((SUBLANES, LANES), F32)],
        input_output_aliases={i: 2 + i for i in range(n)},
        compiler_params=pltpu.CompilerParams(has_side_effects=EFFECT),
    )(*bufs, send_sems, recv_sems, after)
    return outs[0], outs[1], list(outs[2:-1]), outs[-1]


def _gather_half_copies(first, count, over_ici):
    def copies(refs, send_sems, recv_sems):
        x, y, c, chips = _place()
        out = []
        for i in range(count):
            h = refs[i].shape[1] // 2
            mine = pl.ds(c * h, h)
            for j, (px, py) in enumerate(chips):
                k = 3 * (first + i) + j
                slot = 2 * x + y if over_ici else 2 * px + py
                to = (px, py, c) if over_ici else (x, y, 1 - c)
                out.append(_rcopy(refs[i].at[slot, mine], refs[i].at[slot, mine], send_sems.at[k], recv_sems.at[k], to))
        return out

    return copies


def _gather_copies(first, count):
    def copies(refs, send_sems, recv_sems):
        x, y, c, chips = _place()
        me = 2 * x + y
        out = []
        for i in range(count):
            for j, (px, py) in enumerate(chips):
                k = 3 * (first + i) + j
                out.append(_rcopy(refs[i].at[me], refs[i].at[me], send_sems.at[k], recv_sems.at[k], (px, py, c)))
        return out

    return copies


def _swap_copies(n):
    def copies(refs, send_sems, recv_sems):
        x, y, c, _ = _place()
        out = []
        for i in range(n):
            h = refs[i].shape[1] // 2
            out.append(_rcopy(refs[i].at[:, pl.ds((1 - c) * h, h)], refs[n + i], send_sems.at[i], recv_sems.at[i], (x, y, 1 - c)))
        return out

    return copies


def _partial_copies(n):
    def copies(refs, send_sems, recv_sems):
        x, y, c, chips = _place()
        out = []
        for i in range(n):
            for j, (px, py) in enumerate(chips):
                out.append(_rcopy(refs[i].at[2 * px + py], refs[n + i].at[j], send_sems.at[3 * i + j], recv_sems.at[3 * i + j], (px, py, c)))
        return out

    return copies


def _share_copies(n):
    def copies(refs, send_sems, recv_sems):
        x, y, c, _ = _place()
        return [_rcopy(refs[i].at[c], refs[i].at[c], send_sems.at[i], recv_sems.at[i], (x, y, 1 - c)) for i in range(n)]

    return copies


def _allreduce_small(groups, widths):
    flat = [a for g in groups for a in g]
    n = len(flat)
    rows = -(-sum(a.shape[0] for a in flat) // SUBLANES) * SUBLANES
    cols = max(a.shape[1] for a in flat)
    out_shapes = [(sum(a.shape[0] for a in g), wd or g[0].shape[1]) for g, wd in zip(groups, widths)]

    def body(*refs):
        ins, outs = refs[:n], refs[n:n + len(groups)]
        mine, buf, send_sems, recv_sems = refs[n + len(groups):]
        x, y, c, _ = _place()
        me = 4 * x + 2 * y + c
        mine[...] = jnp.zeros_like(mine)
        r0 = 0
        for a_ref in ins:
            r, w = a_ref.shape
            mine[r0:r0 + r, 0:w] = a_ref[...]
            r0 += r
        buf[me] = mine[...]
        copies = []
        for k in range(1, N_DEV):
            peer = (x ^ (k >> 2), y ^ ((k >> 1) & 1), c ^ (k & 1))
            cp = _rcopy(mine, buf.at[me], send_sems.at[k - 1], recv_sems.at[k - 1], peer)
            cp.start()
            copies.append(cp)
        for cp in copies:
            cp.wait()
        acc = buf[0]
        for d in range(1, N_DEV):
            acc = acc + buf[d]
        mine[...] = acc
        r0 = 0
        for o_ref in outs:
            r, w = o_ref.shape
            o_ref[...] = mine[r0:r0 + r, 0:w]
            r0 += r

    vmem = pl.BlockSpec(memory_space=pltpu.VMEM)
    return pl.pallas_call(
        body,
        name="allreduce_small",
        in_specs=[vmem] * n,
        out_specs=[vmem] * len(groups),
        out_shape=[jax.ShapeDtypeStruct(s, F32) for s in out_shapes],
        scratch_shapes=[pltpu.VMEM((rows, cols), F32), pltpu.VMEM((N_DEV, rows, cols), F32),
                        pltpu.SemaphoreType.DMA((N_DEV - 1,)), pltpu.SemaphoreType.DMA((N_DEV - 1,))],
        compiler_params=pltpu.CompilerParams(vmem_limit_bytes=VMEM_LIMIT_BYTES),
    )(*flat)


def _adamw_small(items):
    n = len(items)

    def body(*refs):
        for i in range(n):
            w_ref, m_ref, v_ref, g_ref = refs[4 * i:4 * i + 4]
            d_ref, nm_ref, nv_ref = refs[4 * n + 3 * i:4 * n + 3 * i + 3]
            d_ref[...], nm_ref[...], nv_ref[...] = _adamw_math(w_ref[...], m_ref[...], v_ref[...], g_ref[...])

    vmem = pl.BlockSpec(memory_space=pltpu.VMEM)
    outs = pl.pallas_call(
        body,
        name="adamw_small",
        in_specs=[vmem] * (4 * n),
        out_specs=[vmem] * (3 * n),
        out_shape=[jax.ShapeDtypeStruct(it[0].shape, F32) for it in items for _ in range(3)],
        compiler_params=pltpu.CompilerParams(vmem_limit_bytes=VMEM_LIMIT_BYTES),
    )(*[a for it in items for a in it])
    return [tuple(outs[3 * i:3 * i + 3]) for i in range(n)]


class _Reduction:
    def __init__(self, tag, grads, place):
        self.tag, self.n, self.place = tag, len(grads), place
        lands = [lax.empty((N_CHIPS, g.shape[1] // 2, g.shape[2]), F32) for g in grads]
        self._start("swap", list(grads) + lands, self.n, _swap_copies(self.n))

    def _start(self, stage, bufs, n_sem, copies):
        *self.flight, self.token = _start_copies(f"rs_{stage}_start_{self.tag}", bufs, n_sem, copies)

    def _landed(self, stage, after, copies):
        send_sems, recv_sems, bufs = self.flight
        return _wait_copies(f"rs_{stage}_wait_{self.tag}", bufs, send_sems, recv_sems, after, copies)

    def to_chips(self, after):
        n = self.n
        bufs = self._landed("swap", after, _swap_copies(n))
        sums = [_add_core_halves(g, o, self.place, name=f"rs_add_core_{self.tag}_{i}") for i, (g, o) in enumerate(zip(bufs[:n], bufs[n:]))]
        self.mine = [f for f, _ in sums]
        parts = [b for _, b in sums]
        lands = [lax.empty((3,) + p.shape[1:], BF16) for p in parts]
        self._start("send", parts + lands, 3 * n, _partial_copies(n))

    def to_core(self, after):
        n = self.n
        bufs = self._landed("send", after, _partial_copies(n))
        halves = [_add_chip_partials(f, o, self.place, name=f"rs_add_chip_{self.tag}_{i}") for i, (f, o) in enumerate(zip(self.mine, bufs[n:]))]
        self._start("share", halves, n, _share_copies(n))

    def finish(self, after):
        return [b.reshape((-1,) + b.shape[2:]) for b in self._landed("share", after, _share_copies(self.n))]


ELEM_ROWS = (256, 176, 128, 64, 32, 16, 8)


def _add_core_halves(grad, got, place, name):
    s, r, cols = grad.shape
    h = r // 2
    tr = _pick(h, ELEM_ROWS)

    def body(place_ref, g_ref, o_ref, f_ref, b_ref):
        acc = g_ref[...] + o_ref[...]
        b_ref[...] = acc.astype(BF16)

        @pl.when(pl.program_id(1) == place_ref[0])
        def _():
            f_ref[...] = acc

    blk = pl.BlockSpec((None, tr, cols), lambda i, k, place_ref: (k, i, 0))
    return pl.pallas_call(
        body,
        name=name,
        grid_spec=pltpu.PrefetchScalarGridSpec(
            num_scalar_prefetch=1,
            grid=(h // tr, s),
            in_specs=[pl.BlockSpec((None, None, tr, cols), lambda i, k, place_ref: (k, place_ref[1], i, 0)), blk],
            out_specs=[pl.BlockSpec((tr, cols), lambda i, k, place_ref: (i, 0)), blk],
        ),
        out_shape=[jax.ShapeDtypeStruct((h, cols), F32), jax.ShapeDtypeStruct((s, h, cols), BF16)],
        compiler_params=_cparams(("parallel", "arbitrary")),
    )(place, grad.reshape(s, 2, h, cols), got)


def _add_chip_partials(mine, got, place, name):
    h, cols = mine.shape
    tr = _pick(h, ELEM_ROWS)

    def body(place_ref, m_ref, g_ref, o_ref):
        acc = m_ref[...]
        for j in range(3):
            acc = acc + g_ref[j].astype(F32)
        o_ref[...] = acc

    return pl.pallas_call(
        body,
        name=name,
        grid_spec=pltpu.PrefetchScalarGridSpec(
            num_scalar_prefetch=1,
            grid=(h // tr,),
            in_specs=[
                pl.BlockSpec((tr, cols), lambda i, place_ref: (i, 0)),
                pl.BlockSpec((3, tr, cols), lambda i, place_ref: (0, i, 0)),
            ],
            out_specs=pl.BlockSpec((None, tr, cols), lambda i, place_ref: (place_ref[1], i, 0)),
        ),
        out_shape=jax.ShapeDtypeStruct((2, h, cols), F32),
        compiler_params=_cparams(("parallel",)),
    )(place, mine, got)


def _adamw_math(w, m, v, g):
    nm = ADAM_B1 * m + (1.0 - ADAM_B1) * g
    nv = ADAM_B2 * v + (1.0 - ADAM_B2) * (g * g)
    m_hat = nm * (1.0 / (1.0 - ADAM_B1 ** ADAM_STEP))
    v_hat = nv * (1.0 / (1.0 - ADAM_B2 ** ADAM_STEP))
    return -ADAM_LR * (m_hat / (jnp.sqrt(v_hat) + ADAM_EPS) + ADAM_WD * w), nm, nv


def _adamw_layer(w, m, v, g, layer, prev, name):
    nl, r, cols = w.shape
    tr = _pick(r, ELEM_ROWS)

    def body(w_ref, m_ref, v_ref, g_ref, *rest):
        go_ref, d_ref, nm_ref, nv_ref = rest[-4:]
        gv = g_ref[...]
        d_ref[...], nm_ref[...], nv_ref[...] = _adamw_math(w_ref[...], m_ref[...], v_ref[...], gv)
        go_ref[...] = gv

    lay = pl.BlockSpec((None, tr, cols), lambda i: (layer, i, 0))
    return pl.pallas_call(
        body,
        name=name,
        grid=(r // tr,),
        in_specs=[lay] * 3 + [pl.BlockSpec((tr, cols), lambda i: (i, 0))] + ([ANY] * 4 if prev else []),
        out_specs=[lay] * 4,
        out_shape=[jax.ShapeDtypeStruct((nl, r, cols), F32)] * 4,
        input_output_aliases={4 + k: k for k in range(4)} if prev else {},
        compiler_params=_cparams(("parallel",)),
    )(w, m, v, g, *(prev or ()))


def _adamw(w, m, v, g, name):
    r, cols = w.shape
    tr = _pick(r, ELEM_ROWS)

    def body(w_ref, m_ref, v_ref, g_ref, d_ref, nm_ref, nv_ref):
        d_ref[...], nm_ref[...], nv_ref[...] = _adamw_math(w_ref[...], m_ref[...], v_ref[...], g_ref[...])

    blk = pl.BlockSpec((tr, cols), lambda i: (i, 0))
    return pl.pallas_call(
        body,
        name=name,
        grid=(r // tr,),
        in_specs=[blk] * 4,
        out_specs=[blk] * 3,
        out_shape=[jax.ShapeDtypeStruct((r, cols), F32)] * 3,
        compiler_params=_cparams(("parallel",)),
    )(w, m, v, g)


SMALL_COLS = 384
SMALL_ROWS = 16


def _pad_rows(flat, rows, cols):
    return jnp.pad(flat, (0, rows * cols - flat.shape[0])).reshape(rows, cols)


def kernel(x, hg_norm, hg_w_in, hg_lb_logits, hg_out_norm, hg_w_out, kv_norm, w_kv, attn_norm, attn_w_q, attn_sinks, attn_w_o, ffn_norm, ffn_w_up, ffn_conv_w, ffn_conv_b, ffn_w_down, final_norm, loss_target, m_hg_norm, m_hg_w_in, m_hg_lb_logits, m_hg_out_norm, m_hg_w_out, m_kv_norm, m_w_kv, m_attn_norm, m_attn_w_q, m_attn_sinks, m_attn_w_o, m_ffn_norm, m_ffn_w_up, m_ffn_conv_w, m_ffn_conv_b, m_ffn_w_down, m_final_norm, v_hg_norm, v_hg_w_in, v_hg_lb_logits, v_hg_out_norm, v_hg_w_out, v_kv_norm, v_w_kv, v_attn_norm, v_attn_w_q, v_attn_sinks, v_attn_w_o, v_ffn_norm, v_ffn_w_up, v_ffn_conv_w, v_ffn_conv_b, v_ffn_w_down, v_final_norm):
    wts = dict(hg_norm=hg_norm, hg_w_in=hg_w_in, hg_lb_logits=hg_lb_logits, hg_out_norm=hg_out_norm, hg_w_out=hg_w_out, kv_norm=kv_norm, w_kv=w_kv, attn_norm=attn_norm, attn_w_q=attn_w_q, attn_sinks=attn_sinks, attn_w_o=attn_w_o, ffn_norm=ffn_norm, ffn_w_up=ffn_w_up, ffn_conv_w=ffn_conv_w, ffn_conv_b=ffn_conv_b, ffn_w_down=ffn_w_down, final_norm=final_norm)
    mom1 = dict(hg_norm=m_hg_norm, hg_w_in=m_hg_w_in, hg_lb_logits=m_hg_lb_logits, hg_out_norm=m_hg_out_norm, hg_w_out=m_hg_w_out, kv_norm=m_kv_norm, w_kv=m_w_kv, attn_norm=m_attn_norm, attn_w_q=m_attn_w_q, attn_sinks=m_attn_sinks, attn_w_o=m_attn_w_o, ffn_norm=m_ffn_norm, ffn_w_up=m_ffn_w_up, ffn_conv_w=m_ffn_conv_w, ffn_conv_b=m_ffn_conv_b, ffn_w_down=m_ffn_w_down, final_norm=m_final_norm)
    mom2 = dict(hg_norm=v_hg_norm, hg_w_in=v_hg_w_in, hg_lb_logits=v_hg_lb_logits, hg_out_norm=v_hg_out_norm, hg_w_out=v_hg_w_out, kv_norm=v_kv_norm, w_kv=v_w_kv, attn_norm=v_attn_norm, attn_w_q=v_attn_w_q, attn_sinks=v_attn_sinks, attn_w_o=v_attn_w_o, ffn_norm=v_ffn_norm, ffn_w_up=v_ffn_w_up, ffn_conv_w=v_ffn_conv_w, ffn_conv_b=v_ffn_conv_b, ffn_w_down=v_ffn_w_down, final_norm=v_final_norm)
    names = list(wts)
    chip = 2 * lax.axis_index("x") + lax.axis_index("y")
    core = lax.axis_index("c")
    fs = D_FF // N_CHIPS
    ds = D_MODEL // N_CHIPS

    place_arr = jnp.stack([chip, core]).astype(jnp.int32)
    small = jnp.concatenate([hg_norm.reshape(-1), hg_lb_logits.reshape(-1), ffn_conv_w.reshape(-1)])
    n_small = small.shape[0]
    shards = [
        ("small", _pad_rows(small, SMALL_ROWS, SMALL_COLS), F32, None), ("hg_w_in", hg_w_in, BF16, 0),
        ("hg_w_out", hg_w_out, BF16, 0), ("ffn_w_up0", ffn_w_up, BF16, 0), ("ffn_w_down0", ffn_w_down, BF16, 0),
        ("w_kv", w_kv, BF16, None), ("attn_w_q", attn_w_q, BF16, 0), ("attn_w_o", attn_w_o, BF16, 0),
        ("ffn_w_up1", ffn_w_up, BF16, 1), ("ffn_w_down1", ffn_w_down, BF16, 1),
    ]
    n_first = 3
    spans = dict(layer0=(0, 2), layer1=(2, 7))

    def first_copies(refs, send_sems, recv_sems):
        return (_gather_copies(0, 1)(refs[:1], send_sems, recv_sems) + _gather_half_copies(1, 1, True)(refs[1:2], send_sems, recv_sems)
                + _gather_copies(2, 1)(refs[2:3], send_sems, recv_sems))

    placed = [_place_shard(s, place_arr, dt, name=f"place_{nm}", layer=ly) for nm, s, dt, ly in shards[:n_first]]
    first = _start_copies("gather_start_first", placed, 3 * n_first, first_copies)
    placed = [_place_shard(s, place_arr, dt, name=f"place_{nm}", deps=(first[3],), layer=ly) for nm, s, dt, ly in shards[n_first:]]
    rest = _start_copies("gather_start_rest", placed, 3 * len(placed), _gather_half_copies(0, len(placed), True))
    relayed = {}

    def fetch(w, stage, after):
        if stage == "first":
            w_in = _relay_copies("gather_first_relay", first[2][1:2], first[0], first[1], after,
                                 _gather_half_copies(1, 1, True), 3, _gather_half_copies(0, 1, False))
            got = _wait_copies("gather_wait_small", first[2][:1], first[0], first[1], w_in[3], _gather_copies(0, 1))
            got += _wait_copies("gather_wait_first", w_in[2], w_in[0], w_in[1], got[0], _gather_half_copies(0, 1, False))
        elif stage == "mixer_out":
            got = _wait_copies("gather_wait_mixer_out", first[2][2:], first[0], first[1], after, _gather_copies(2, 1))
        elif stage.endswith("_relay"):
            lo, hi = spans[stage[:-6]]
            relayed[stage[:-6]] = _relay_copies(
                f"gather_{stage}", rest[2][lo:hi], rest[0], rest[1], after,
                _gather_half_copies(lo, hi - lo, True), 3 * (hi - lo), _gather_half_copies(0, hi - lo, False))
            return w
        else:
            lo, hi = spans[stage]
            send_sems, recv_sems, bufs, _ = relayed[stage]
            got = _wait_copies(f"gather_wait_{stage}", bufs, send_sems, recv_sems, after, _gather_half_copies(0, hi - lo, False))
        w = dict(w)
        if stage == "first":
            g_small = got[0].reshape(N_CHIPS, -1)[:, :n_small]
            conv_w = g_small[:, 3 * ds:].reshape(N_CHIPS, 2, 3, fs).transpose(1, 2, 0, 3).reshape(2, 3, D_FF)
            w.update(
                hg_norm=g_small[:, :ds].reshape(1, D_MODEL),
                hg_lb=g_small[:, ds:3 * ds].reshape(N_CHIPS, 2, ds).transpose(1, 0, 2).reshape(2, D_MODEL),
                ffn_conv_w=[conv_w[0], conv_w[1]], hg_w_in=got[1],
            )
        elif stage == "mixer_out":
            w.update(hg_w_out=got[0].reshape(1, D_MODEL, D_MODEL))
        elif stage == "layer0":
            w.update(ffn_w_up=[got[0], None], ffn_w_down=[got[1].reshape(1, D_FF, D_MODEL), None])
        else:
            w.update(
                w_kv=got[0].reshape(1, D_MODEL, 2 * LANES), attn_w_q=got[1].reshape(1, D_MODEL, D_MODEL),
                attn_w_o=got[2].reshape(1, D_MODEL, D_MODEL), ffn_w_up=[w["ffn_w_up"][0], got[3]],
                ffn_w_down=[w["ffn_w_down"][0], got[4].reshape(1, D_FF, D_MODEL)],
            )
        return w

    whole = dict(
        hg_out_norm=hg_out_norm, kv_norm=kv_norm.reshape(1, D_MODEL), attn_norm=attn_norm, attn_sinks=attn_sinks.reshape(ATT_QH),
        ffn_norm=[ffn_norm[0:1], ffn_norm[1:2]], ffn_conv_b=[ffn_conv_b[0:1], ffn_conv_b[1:2]], final_norm=final_norm.reshape(1, D_MODEL),
    )
    whole = fetch(whole, "first", rest[3])

    red, layer1 = {}, {}

    def by_rows(g, rows):
        return g.reshape(N_CHIPS, rows, g.shape[2])

    def hook(point, dh, grads):
        if point == "ffn1":
            red["ffn1"] = _Reduction("ffn1", [by_rows(grads["ffn_w_down"], fs), grads["ffn_w_up"]], place_arr)
            return (red["ffn1"].token,)
        if point == "attn":
            red["ffn1"].to_chips(dh)
            layer1.update(grads)
            return (red["ffn1"].token,)
        if point == "ffn0":
            group = [by_rows(layer1["attn_w_o"], ds), by_rows(layer1["attn_w_q"], ds), by_rows(layer1["w_kv"], ds),
                     by_rows(grads["ffn_w_down"], fs), grads["ffn_w_up"], by_rows(grads["hg_w_out"], ds)]
            red["mid"] = _Reduction("mid", group, place_arr)
            return (red["mid"].token,)
        if point == "hgrn":
            red["ffn1"].to_core(dh)
            red["mid"].to_chips(dh)
            return (red["ffn1"].token, red["mid"].token)
        red["hg"] = _Reduction("hg", [grads["hg_w_in"]], place_arr)
        return (red["hg"].token,)

    loss, dx, grads = _local_step(x[0], loss_target[0], whole, fetch, hook)

    small_names = ["hg_out_norm", "attn_sinks", "kv_norm", "attn_norm", "ffn_norm", "ffn_conv_b", "final_norm", "hg_norm", "hg_lb_logits", "ffn_conv_w"]
    groups = [[loss]] + [grads[n] if isinstance(grads[n], list) else [grads[n]] for n in small_names[:-2]] + [[grads["hg_lb"]], grads["ffn_conv_w"]]
    summed = _allreduce_small(groups, [None, None, ATT_QH] + [None] * 8)
    red["hg"].to_chips(summed[1])
    loss_out = summed[0][0, 0]
    small_grads = dict(zip(small_names, summed[1:]))
    small_grads["hg_norm"] = lax.dynamic_slice(small_grads["hg_norm"], (0, chip * ds), (1, ds))
    small_grads["hg_lb_logits"] = lax.dynamic_slice(small_grads["hg_lb_logits"], (0, chip * ds), (2, ds))
    small_grads["ffn_conv_w"] = lax.dynamic_slice(small_grads["ffn_conv_w"], (0, chip * fs), (2 * 3, fs))

    out_g, out_d, out_m, out_v = {}, {}, {}, {}

    def update(name, g2):
        shape = wts[name].shape
        d2, m2, v2 = _adamw(wts[name].reshape(g2.shape), mom1[name].reshape(g2.shape), mom2[name].reshape(g2.shape), g2, name=f"adamw_{name}")
        out_g[name], out_d[name], out_m[name], out_v[name] = g2.reshape(shape), d2.reshape(shape), m2.reshape(shape), v2.reshape(shape)
        return d2

    def update_layer(name, g2, layer, prev):
        res = _adamw_layer(wts[name], mom1[name], mom2[name], g2, layer, prev, name=f"adamw_{name}{layer}")
        out_g[name], out_d[name], out_m[name], out_v[name] = res
        return res

    g_down1, g_up1 = red["ffn1"].finish(red["hg"].token)
    down1 = update_layer("ffn_w_down", g_down1, 1, None)
    up1 = update_layer("ffn_w_up", g_up1, 1, None)
    red["mid"].to_core(up1[1])
    g_o, g_q, g_kv, g_down0, g_up0, g_out = red["mid"].finish(up1[2])
    update("attn_w_o", g_o)
    update("attn_w_q", g_q)
    update("w_kv", g_kv)
    update("hg_w_out", g_out)
    update_layer("ffn_w_down", g_down0, 0, down1)
    last = update_layer("ffn_w_up", g_up0, 0, up1)
    red["hg"].to_core(last[1])
    (g_in,) = red["hg"].finish(last[2])
    update("hg_w_in", g_in)

    as_2d = lambda a, n: a.reshape(small_grads[n].shape)
    updated = _adamw_small([(as_2d(wts[n], n), as_2d(mom1[n], n), as_2d(mom2[n], n), small_grads[n]) for n in small_names])
    for n, (d2, m2, v2) in zip(small_names, updated):
        shape = wts[n].shape
        out_g[n], out_d[n], out_m[n], out_v[n] = small_grads[n].reshape(shape), d2.reshape(shape), m2.reshape(shape), v2.reshape(shape)

    grad_x = dx.reshape(x.shape)
    return (loss_out, grad_x, *[out_g[n] for n in names], *[out_d[n] for n in names], *[out_m[n] for n in names], *[out_v[n] for n in names])
```

```python
import functools

import jax
import jax.numpy as jnp
from jax import lax
from jax.experimental import pallas as pl
from jax.experimental.pallas import tpu as pltpu

F32 = jnp.float32
BF16 = jnp.bfloat16
MESH = pl.DeviceIdType.MESH

EPS = 1e-6
D_MODEL = 1024
HG_HEADS = 8
HG_DK = 128
HG_CHUNK = 64
ATT_HD = 64
ATT_QH = 16
ATT_KVH = 2
ATT_GROUP = ATT_QH // ATT_KVH
WINDOW = 128
D_FF = 2816
N_CHIPS = 4
N_DEV = 8
LANES = 128
SUBLANES = 8
VMEM_LIMIT_BYTES = 56 * 1024 * 1024
NEG = -1e30
ALIBI_SLOPES = tuple(2.0 ** (-8.0 * h / ATT_QH) for h in range(1, ATT_QH + 1))

ADAM_LR = 0.001
ADAM_B1 = 0.9
ADAM_B2 = 0.999
ADAM_EPS = 1e-08
ADAM_WD = 0.01
ADAM_STEP = 10


def _cparams(sem=None):
    return pltpu.CompilerParams(dimension_semantics=sem, vmem_limit_bytes=VMEM_LIMIT_BYTES)


def _pick(n, cands):
    for c in cands:
        if n % c == 0:
            return c
    return n


def _sigmoid(x):
    return 0.5 * jnp.tanh(0.5 * x) + 0.5


def _dot(a, b, dims):
    return lax.dot_general(a, b, (dims, ((), ())), preferred_element_type=F32)


NN = ((1,), (0,))
NT = ((1,), (1,))
TN = ((0,), (0,))


MM_ROWS = 1024


def _rms_stats(xv):
    rstd = lax.rsqrt(jnp.mean(xv * xv, axis=-1, keepdims=True) + EPS)
    return xv * rstd, rstd


def _mm_operand(a_ref, gain_ref):
    if gain_ref is None:
        return a_ref[...].astype(BF16)
    return (_rms_stats(a_ref[...])[0] * gain_ref[...]).astype(BF16)


def _mm_nn(a, w, res=None, out_dtype=F32, name="mm_nn", gain=None):
    m, k = a.shape
    s, _, ns = w.shape
    tm = min(m, MM_ROWS)
    tn = _pick(ns, (1024, 1408, 512, 256, 128))
    npb = ns // tn

    def body(a_ref, w_ref, *rest):
        o_ref = rest[-1]
        acc = _dot(_mm_operand(a_ref, rest[0] if gain is not None else None), w_ref[...], NN)
        if res is not None:
            acc = acc + rest[-2][...]
        o_ref[...] = acc.astype(o_ref.dtype)

    in_specs = [
        pl.BlockSpec((tm, k), lambda i, j: (i, 0)),
        pl.BlockSpec((None, k, tn), lambda i, j: (j // npb, 0, j % npb)),
    ]
    args = [a, w]
    if gain is not None:
        in_specs.append(pl.BlockSpec((1, k), lambda i, j: (0, 0)))
        args.append(gain)
    if res is not None:
        in_specs.append(pl.BlockSpec((tm, tn), lambda i, j: (i, j)))
        args.append(res)
    return pl.pallas_call(
        body,
        name=name,
        grid=(m // tm, s * npb),
        in_specs=in_specs,
        out_specs=pl.BlockSpec((tm, tn), lambda i, j: (i, j)),
        out_shape=jax.ShapeDtypeStruct((m, s * ns), out_dtype),
        compiler_params=_cparams(("parallel", "parallel")),
    )(*args)


def _dy_spec(stacked, tm, tn, npb, row, kk):
    if stacked:
        return pl.BlockSpec((None, tm, tn), lambda *g: (kk(g) // npb, row(g), kk(g) % npb))
    return pl.BlockSpec((tm, tn), lambda *g: (row(g), kk(g)))


def _dep_specs(deps):
    return [pl.BlockSpec(d.shape, lambda *g: (0, 0)) for d in deps]


def _mm_nt(dy, w, stacked=False, out_dtype=F32, name="mm_nt", deps=(), norm_of=None):
    s, k, ns = w.shape
    m = dy.shape[1] if stacked else dy.shape[0]
    tm = min(m, MM_ROWS)
    tko = _pick(k, (1024, 1408, 512, 256))
    tn = _pick(ns, (1024, 1408, 512, 256))
    npb = ns // tn
    nk = s * npb
    fused = norm_of is not None
    assert not fused or tko == k

    def body(dy_ref, w_ref, *rest):
        acc_ref = rest[-1]
        i, kk = pl.program_id(0), pl.program_id(2)

        @pl.when(kk == 0)
        def _():
            acc_ref[...] = jnp.zeros_like(acc_ref)

        acc_ref[...] += _dot(dy_ref[...].astype(BF16), w_ref[...], NT)

        if not fused:
            @pl.when(kk == nk - 1)
            def _():
                rest[-2][...] = acc_ref[...].astype(rest[-2].dtype)
            return
        x_ref, g_ref, dres_ref = rest[:3]
        dx_ref, dg_ref = rest[-3], rest[-2]

        @pl.when(jnp.logical_and(i == 0, kk == 0))
        def _():
            dg_ref[...] = jnp.zeros_like(dg_ref)

        @pl.when(kk == nk - 1)
        def _():
            dxn = acc_ref[...]
            xhat, rstd = _rms_stats(x_ref[...])
            gd = dxn * g_ref[...]
            dx_ref[...] = dres_ref[...] + rstd * (gd - xhat * jnp.mean(gd * xhat, axis=-1, keepdims=True))
            dg_ref[...] += jnp.sum(dxn * xhat, axis=0, keepdims=True)

    row = pl.BlockSpec((tm, tko), lambda i, j, kk: (i, j))
    vec = pl.BlockSpec((1, k), lambda i, j, kk: (0, 0))
    return pl.pallas_call(
        body,
        name=name,
        grid=(m // tm, k // tko, nk),
        in_specs=[
            _dy_spec(stacked, tm, tn, npb, lambda g: g[0], lambda g: g[2]),
            pl.BlockSpec((None, tko, tn), lambda i, j, kk: (kk // npb, j, kk % npb)),
        ] + ([row, vec, row] if fused else []) + _dep_specs(deps),
        out_specs=[row, vec] if fused else row,
        out_shape=[jax.ShapeDtypeStruct((m, k), F32), jax.ShapeDtypeStruct((1, k), F32)] if fused else jax.ShapeDtypeStruct((m, k), out_dtype),
        scratch_shapes=[pltpu.VMEM((tm, tko), F32)],
        compiler_params=_cparams(("arbitrary",) * 3 if fused else ("parallel", "parallel", "arbitrary")),
    )(dy, w, *(norm_of or ()), *deps)


def _mm_tn(a, dy, s, ns, stacked=False, name="mm_tn", deps=(), gain=None):
    m, k = a.shape
    tm = min(m, MM_ROWS)
    tk = _pick(k, (1024, 1408, 512, 256))
    tn = _pick(ns, (1024, 1408, 512, 256, 128))
    npb = ns // tn
    nm = m // tm
    assert gain is None or tk == k

    def body(a_ref, dy_ref, *rest):
        j, mm = pl.program_id(1), pl.program_id(2)
        if gain is None:
            o_ref, acc_ref = rest[-2:]
            lhs = a_ref[...].astype(BF16)
        else:
            o_ref, acc_ref, xn_ref = rest[-3:]

            @pl.when(j == 0)
            def _():
                xn_ref[mm] = _mm_operand(a_ref, rest[0])

            lhs = xn_ref[mm]

        @pl.when(mm == 0)
        def _():
            acc_ref[...] = jnp.zeros_like(acc_ref)

        acc_ref[...] += _dot(lhs, dy_ref[...].astype(BF16), TN)

        @pl.when(mm == nm - 1)
        def _():
            o_ref[...] = acc_ref[...]

    a_rows = (lambda i, j, mm: (mm, i)) if gain is None else (lambda i, j, mm: (jnp.where(j == 0, mm, 0), i))
    return pl.pallas_call(
        body,
        name=name,
        grid=(k // tk, s * npb, nm),
        in_specs=[
            pl.BlockSpec((tm, tk), a_rows),
            _dy_spec(stacked, tm, tn, npb, lambda g: g[2], lambda g: g[1]),
        ] + ([pl.BlockSpec((1, k), lambda i, j, mm: (0, 0))] if gain is not None else []) + _dep_specs(deps),
        out_specs=pl.BlockSpec((None, tk, tn), lambda i, j, mm: (j // npb, i, j % npb)),
        out_shape=jax.ShapeDtypeStruct((s, k, ns), F32),
        scratch_shapes=[pltpu.VMEM((tk, tn), F32)] + ([pltpu.VMEM((nm, tm, tk), BF16)] if gain is not None else []),
        compiler_params=_cparams(("parallel", "arbitrary", "arbitrary") if gain is not None else ("parallel", "parallel", "arbitrary")),
    )(a, dy, *(() if gain is None else (gain,)), *deps)


ROW_TILE = 512


def _loss_head(h, g, target):
    t, d = h.shape
    r = min(t, ROW_TILE)

    def body(h_ref, g_ref, t_ref, dh_ref, dg_ref, loss_ref):
        @pl.when(pl.program_id(0) == 0)
        def _():
            dg_ref[...] = jnp.zeros_like(dg_ref)
            loss_ref[...] = jnp.zeros_like(loss_ref)

        xv = h_ref[...]
        rstd = lax.rsqrt(jnp.mean(xv * xv, axis=-1, keepdims=True) + EPS)
        xhat = xv * rstd
        gv = g_ref[...]
        err = xhat * gv - t_ref[...]
        loss_ref[...] += 0.5 * jnp.sum(jnp.mean(err * err, axis=-1, keepdims=True), axis=0, keepdims=True)
        dy = err * (1.0 / d)
        gd = dy * gv
        dh_ref[...] = rstd * (gd - xhat * jnp.mean(gd * xhat, axis=-1, keepdims=True))
        dg_ref[...] += jnp.sum(dy * xhat, axis=0, keepdims=True)

    return pl.pallas_call(
        body,
        name="loss_head",
        grid=(t // r,),
        in_specs=[
            pl.BlockSpec((r, d), lambda i: (i, 0)),
            pl.BlockSpec((1, d), lambda i: (0, 0)),
            pl.BlockSpec((r, d), lambda i: (i, 0)),
        ],
        out_specs=[
            pl.BlockSpec((r, d), lambda i: (i, 0)),
            pl.BlockSpec((1, d), lambda i: (0, 0)),
            pl.BlockSpec((1, LANES), lambda i: (0, 0)),
        ],
        out_shape=[
            jax.ShapeDtypeStruct((t, d), F32),
            jax.ShapeDtypeStruct((1, d), F32),
            jax.ShapeDtypeStruct((1, LANES), F32),
        ],
        compiler_params=_cparams(("arbitrary",)),
    )(h, g, target)


CONV_ROWS = 256
CONV_COLS = 1408


def _conv_taps(x_ext, n):
    tot = x_ext.shape[0]
    g1 = pltpu.roll(x_ext, 1, 0)[tot - n:]
    g2 = pltpu.roll(x_ext, 2, 0)[tot - n:]
    return g2, g1


def _conv_fwd(up, conv_w, conv_b, name="conv_fwd"):
    t = up.shape[0]
    r = min(t, CONV_ROWS)
    tc = CONV_COLS
    ncb = D_FF // tc
    hb = r // SUBLANES

    def body(g_ref, halo_ref, v_ref, w_ref, b_ref, o_ref, c_ref):
        i = pl.program_id(1)
        g0 = g_ref[...]
        halo = halo_ref[...] * jnp.where(i > 0, 1.0, 0.0)
        g2, g1 = _conv_taps(jnp.concatenate([halo, g0], axis=0), r)
        c = b_ref[...] + w_ref[0:1, :] * g2 + w_ref[1:2, :] * g1 + w_ref[2:3, :] * g0
        c_ref[...] = c
        o_ref[...] = (c * _sigmoid(c) * v_ref[...]).astype(BF16)

    blk = pl.BlockSpec((r, tc), lambda j, i: (i, j))
    return pl.pallas_call(
        body,
        name=name,
        grid=(ncb, t // r),
        in_specs=[
            blk,
            pl.BlockSpec((SUBLANES, tc), lambda j, i: (jnp.maximum(i * hb - 1, 0), j)),
            pl.BlockSpec((r, tc), lambda j, i: (i, ncb + j)),
            pl.BlockSpec((3, tc), lambda j, i: (0, j)),
            pl.BlockSpec((1, tc), lambda j, i: (0, j)),
        ],
        out_specs=[blk, blk],
        out_shape=[jax.ShapeDtypeStruct((t, D_FF), BF16), jax.ShapeDtypeStruct((t, D_FF), F32)],
        compiler_params=_cparams(("parallel", "parallel")),
    )(up, up, up, conv_w, conv_b)


def _conv_bwd(up, conv_w, c, dact, name="conv_bwd"):
    t = up.shape[0]
    r = min(t, CONV_ROWS)
    tc = CONV_COLS
    ncb = D_FF // tc
    nrt = t // r

    def body(g_ref, v_ref, w_ref, c_ref, da_ref, dup_ref, dw_ref, db_ref, nxt_ref):
        ii = pl.program_id(1)

        @pl.when(ii == 0)
        def _():
            nxt_ref[...] = jnp.zeros_like(nxt_ref)
            dw_ref[...] = jnp.zeros_like(dw_ref)
            db_ref[...] = jnp.zeros_like(db_ref)

        g0 = g_ref[...]
        w0, w1, w2 = w_ref[0:1, :], w_ref[1:2, :], w_ref[2:3, :]
        c = c_ref[...]
        sg = _sigmoid(c)
        da = da_ref[...]
        dup_ref[1] = (da * (c * sg)).astype(BF16)
        dc = da * v_ref[...] * (sg * (1.0 + c * (1.0 - sg)))
        ext = jnp.concatenate([dc, nxt_ref[...]], axis=0)
        tot = r + SUBLANES
        d1 = pltpu.roll(ext, tot - 1, 0)[:r]
        d2 = pltpu.roll(ext, tot - 2, 0)[:r]
        nxt_ref[...] = dc[:SUBLANES]
        dup_ref[0] = (w2 * dc + w1 * d1 + w0 * d2).astype(BF16)
        db_ref[...] += jnp.sum(dc, axis=0, keepdims=True)
        dw_ref[0:1, :] += jnp.sum(d2 * g0, axis=0, keepdims=True)
        dw_ref[1:2, :] += jnp.sum(d1 * g0, axis=0, keepdims=True)
        dw_ref[2:3, :] += jnp.sum(dc * g0, axis=0, keepdims=True)

    rev = lambda ii: nrt - 1 - ii
    dup, dw, db = pl.pallas_call(
        body,
        name=name,
        grid=(ncb, nrt),
        in_specs=[
            pl.BlockSpec((r, tc), lambda j, ii: (rev(ii), j)),
            pl.BlockSpec((r, tc), lambda j, ii: (rev(ii), ncb + j)),
            pl.BlockSpec((3, tc), lambda j, ii: (0, j)),
            pl.BlockSpec((r, tc), lambda j, ii: (rev(ii), j)),
            pl.BlockSpec((r, tc), lambda j, ii: (rev(ii), j)),
        ],
        out_specs=[
            pl.BlockSpec((2, None, r, tc), lambda j, ii: (0, j, rev(ii), 0)),
            pl.BlockSpec((3, tc), lambda j, ii: (0, j)),
            pl.BlockSpec((1, tc), lambda j, ii: (0, j)),
        ],
        out_shape=[
            jax.ShapeDtypeStruct((2, ncb, t, tc), BF16),
            jax.ShapeDtypeStruct((3, D_FF), F32),
            jax.ShapeDtypeStruct((1, D_FF), F32),
        ],
        scratch_shapes=[pltpu.VMEM((SUBLANES, tc), F32)],
        compiler_params=_cparams(("parallel", "arbitrary")),
    )(up, up, conv_w, c, dact)
    return dup.reshape(2 * ncb, t, tc), dw, db


def _split3(x):
    x1 = x.astype(BF16)
    r1 = x - x1.astype(F32)
    x2 = r1.astype(BF16)
    x3 = (r1 - x2.astype(F32)).astype(BF16)
    return x1, x2, x3


def _tri_dot(tri, x, dims):
    x1, x2, x3 = _split3(x)
    return _dot(tri, x1, dims) + _dot(tri, x2, dims) + _dot(tri, x3, dims)


def _lower_bound(logits_ref):
    return _sigmoid(logits_ref[0:1, :] - logits_ref[1:2, :])


def _hg_gates(qr, fr, lb):
    q = qr * _sigmoid(qr) * (HG_DK ** -0.5)
    sf = _sigmoid(fr)
    fg = lb + (1.0 - lb) * sf
    return q, sf, fg


def _hg_chunk_terms(q, fg, tril_b, low_half):
    g = jnp.log(fg)
    k = 1.0 - fg
    cum = _tri_dot(tril_b, g, NN)
    c_last = jnp.sum(g, axis=0, keepdims=True)
    c_mid = jnp.sum(jnp.where(low_half, g, 0.0), axis=0, keepdims=True)
    e_q = jnp.exp(cum - c_mid)
    e_k = jnp.exp(c_mid - cum)
    e_0 = jnp.exp(cum)
    e_l = jnp.exp(c_last - cum)
    return k, e_q, e_k, e_0, e_l, jnp.exp(c_last)


HG_BLOCK = 256


def _hg_proj_specs(rb, row):
    return [pl.BlockSpec((rb, D_MODEL), functools.partial(lambda i, k: (row(i), k), k=k)) for k in range(4)]


def _hg_consts(c):
    tril = lax.broadcasted_iota(jnp.int32, (c, c), 0) >= lax.broadcasted_iota(jnp.int32, (c, c), 1)
    low_half = lax.broadcasted_iota(jnp.int32, (c, D_MODEL), 0) < c // 2
    return tril, tril.astype(BF16), low_half


def _hgrn_fwd(proj, lb, wn):
    t = proj.shape[0]
    c = HG_CHUNK
    rb = min(t, HG_BLOCK)
    cpb = rb // c

    def body(q_ref, f_ref, i_ref, g_ref, lb_ref, wn_ref, o_ref, y_ref, st_ref, s_scr):
        @pl.when(pl.program_id(0) == 0)
        def _():
            s_scr[...] = jnp.zeros_like(s_scr)

        lb_all = _lower_bound(lb_ref)
        wnv = wn_ref[...]
        tril, tril_b, low_half = _hg_consts(c)

        def chunk(n, carry):
            rows = pl.ds(pl.multiple_of(n * c, c), c)
            q, _, fg = _hg_gates(q_ref[rows, :], f_ref[rows, :], lb_all)
            k, e_q, e_k, e_0, e_l, e_last = _hg_chunk_terms(q, fg, tril_b, low_half)
            qi, ki, q0, kl = (q * e_q).astype(BF16), (k * e_k).astype(BF16), (q * e_0).astype(BF16), (k * e_l).astype(BF16)
            v = i_ref[rows, :].astype(BF16)
            gr = g_ref[rows, :]
            gate = gr * _sigmoid(gr)
            for h in range(HG_HEADS):
                cols = slice(h * HG_DK, (h + 1) * HG_DK)
                st = s_scr[h]
                st_ref[h, n] = st
                a = jnp.where(tril, _dot(qi[:, cols], ki[:, cols], NT), 0.0)
                o = _dot(q0[:, cols], st.astype(BF16), NT) + _dot(a.astype(BF16), v[:, cols], NN)
                s_scr[h] = st * e_last[:, cols] + _dot(v[:, cols], kl[:, cols], TN)
                o_ref[rows, cols] = o
                rstd = lax.rsqrt(jnp.mean(o * o, axis=-1, keepdims=True) + EPS)
                y_ref[rows, cols] = (o * rstd * wnv * gate[:, cols]).astype(BF16)
            return carry

        lax.fori_loop(0, cpb, chunk, 0)

    blk = pl.BlockSpec((rb, D_MODEL), lambda i: (i, 0))
    return pl.pallas_call(
        body,
        name="hgrn_fwd",
        grid=(t // rb,),
        in_specs=_hg_proj_specs(rb, lambda i: i) + [pl.BlockSpec((2, D_MODEL), lambda i: (0, 0)), pl.BlockSpec((1, HG_DK), lambda i: (0, 0))],
        out_specs=[blk, blk, pl.BlockSpec((HG_HEADS, cpb, HG_DK, HG_DK), lambda i: (0, i, 0, 0))],
        out_shape=[
            jax.ShapeDtypeStruct((t, D_MODEL), F32),
            jax.ShapeDtypeStruct((t, D_MODEL), BF16),
            jax.ShapeDtypeStruct((HG_HEADS, t // c, HG_DK, HG_DK), F32),
        ],
        scratch_shapes=[pltpu.VMEM((HG_HEADS, HG_DK, HG_DK), F32)],
        compiler_params=_cparams(("arbitrary",)),
    )(proj, proj, proj, proj, lb, wn)


def _hgrn_bwd(proj, lb, wn, o, states, dy):
    t = proj.shape[0]
    c = HG_CHUNK
    rb = min(t, HG_BLOCK)
    cpb = rb // c
    nb = t // rb

    def body(q_ref, f_ref, i_ref, g_ref, lb_ref, wn_ref, o_ref, st_ref, dy_ref, dp_ref, dl_ref, dwn_ref, ds_scr, dlb_scr):
        step = pl.program_id(0)

        @pl.when(step == 0)
        def _():
            dwn_ref[...] = jnp.zeros_like(dwn_ref)
            ds_scr[...] = jnp.zeros_like(ds_scr)
            dlb_scr[...] = jnp.zeros_like(dlb_scr)

        lb_all = _lower_bound(lb_ref)
        wnv = wn_ref[...]
        tril, tril_b, low_half = _hg_consts(c)

        def chunk(nn, carry):
            n = cpb - 1 - nn
            rows = pl.ds(pl.multiple_of(n * c, c), c)
            qr = q_ref[rows, :]
            gr = g_ref[rows, :]
            q, sf, fg = _hg_gates(qr, f_ref[rows, :], lb_all)
            k, e_q, e_k, e_0, e_l, e_last = _hg_chunk_terms(q, fg, tril_b, low_half)
            qi, qi_lo, _ = _split3(q * e_q)
            ki, ki_lo, _ = _split3(k * e_k)
            q0 = (q * e_0).astype(BF16)
            kl = (k * e_l).astype(BF16)
            v = i_ref[rows, :].astype(BF16)
            sg = _sigmoid(gr)
            silu_g = gr * sg
            dsilu_g = sg * (1.0 + gr * (1.0 - sg))
            dqs, dks, d_lasts = [], [], []
            for h in range(HG_HEADS):
                cols = slice(h * HG_DK, (h + 1) * HG_DK)
                ov = o_ref[rows, cols]
                dyv = dy_ref[rows, cols].astype(F32)
                rstd = lax.rsqrt(jnp.mean(ov * ov, axis=-1, keepdims=True) + EPS)
                ohat = ov * rstd
                dp_ref[3, rows, cols] = (dyv * (ohat * wnv) * dsilu_g[:, cols]).astype(BF16)
                don = dyv * silu_g[:, cols]
                dwn_ref[...] += jnp.sum(don * ohat, axis=0, keepdims=True)
                gd = don * wnv
                do_b = (rstd * (gd - ohat * jnp.mean(gd * ohat, axis=-1, keepdims=True))).astype(BF16)
                st = st_ref[h, n]
                ds = ds_scr[h]
                ds_b = ds.astype(BF16)
                vh, kh = v[:, cols], k[:, cols]
                a_b = jnp.where(tril, _dot(qi[:, cols], ki[:, cols], NT), 0.0).astype(BF16)
                da_b = jnp.where(tril, _dot(do_b, vh, NT), 0.0).astype(BF16)
                dqs.append(_dot(do_b, st.astype(BF16), NN) * e_0[:, cols]
                           + (_dot(da_b, ki[:, cols], NN) + _dot(da_b, ki_lo[:, cols], NN)) * e_q[:, cols])
                dk_state = _dot(vh, ds_b, NN) * e_l[:, cols]
                dks.append((_dot(da_b, qi[:, cols], TN) + _dot(da_b, qi_lo[:, cols], TN)) * e_k[:, cols] + dk_state)
                dp_ref[2, rows, cols] = (_dot(a_b, do_b, TN) + _dot(kl[:, cols], ds_b, NT)).astype(BF16)
                ds_scr[h] = ds * e_last[:, cols] + _dot(do_b, q0[:, cols], TN)
                d_lasts.append(jnp.sum(dk_state * kh, axis=0, keepdims=True) + jnp.sum(ds * st, axis=0, keepdims=True) * e_last[:, cols])
            dq = jnp.concatenate(dqs, axis=1)
            dk = jnp.concatenate(dks, axis=1)
            dlogf = _tri_dot(tril_b, q * dq - k * dk, TN) + jnp.concatenate(d_lasts, axis=1)
            dfg = dlogf / fg - dk
            dlb_scr[...] += jnp.sum(dfg * (1.0 - sf), axis=0, keepdims=True)
            sq = _sigmoid(qr)
            dp_ref[0, rows, :] = (dq * (HG_DK ** -0.5) * (sq * (1.0 + qr * (1.0 - sq)))).astype(BF16)
            dp_ref[1, rows, :] = (dfg * (1.0 - lb_all) * sf * (1.0 - sf)).astype(BF16)
            return carry

        lax.fori_loop(0, cpb, chunk, 0)

        @pl.when(step == nb - 1)
        def _():
            d0 = dlb_scr[...] * lb_all * (1.0 - lb_all)
            dl_ref[0:1, :] = d0
            dl_ref[1:2, :] = -d0

    rev = lambda i: nb - 1 - i
    blk = pl.BlockSpec((rb, D_MODEL), lambda i: (rev(i), 0))
    return pl.pallas_call(
        body,
        name="hgrn_bwd",
        grid=(nb,),
        in_specs=_hg_proj_specs(rb, rev)
        + [pl.BlockSpec((2, D_MODEL), lambda i: (0, 0)), pl.BlockSpec((1, HG_DK), lambda i: (0, 0)), blk,
           pl.BlockSpec((HG_HEADS, cpb, HG_DK, HG_DK), lambda i: (0, rev(i), 0, 0)), blk],
        out_specs=[
            pl.BlockSpec((4, rb, D_MODEL), lambda i: (0, rev(i), 0)),
            pl.BlockSpec((2, D_MODEL), lambda i: (0, 0)),
            pl.BlockSpec((1, HG_DK), lambda i: (0, 0)),
        ],
        out_shape=[
            jax.ShapeDtypeStruct((4, t, D_MODEL), BF16),
            jax.ShapeDtypeStruct((2, D_MODEL), F32),
            jax.ShapeDtypeStruct((1, HG_DK), F32),
        ],
        scratch_shapes=[pltpu.VMEM((HG_HEADS, HG_DK, HG_DK), F32), pltpu.VMEM((1, D_MODEL), F32)],
        compiler_params=_cparams(("arbitrary",)),
    )(proj, proj, proj, proj, lb, wn, o, states, dy)


ATT_STACK = 8


def _att_stack(q_ref, sink_ref, first, lo, bias_p, bias_c, extra_ref=None):
    qs, bps, bcs, sinks, extras = [], [], [], None, []
    rows = lax.broadcasted_iota(jnp.int32, (ATT_STACK * WINDOW, 1), 0)
    for i in range(ATT_STACK):
        hq = first + i
        cols = slice((hq // 2) * LANES, (hq // 2 + 1) * LANES)
        sel = lo if hq % 2 == 0 else jnp.logical_not(lo)
        qp = q_ref[:, cols] * (ATT_HD ** -0.5)
        qs.append(jnp.where(sel, qp, jnp.zeros_like(qp)))
        bps.append(ALIBI_SLOPES[hq] * bias_p)
        bcs.append(ALIBI_SLOPES[hq] * bias_c)
        sinks = sink_ref[hq] if sinks is None else jnp.where(rows < i * WINDOW, sinks, sink_ref[hq])
        if extra_ref is not None:
            ep = extra_ref[:, cols]
            extras.append(jnp.where(sel, ep, jnp.zeros_like(ep)))
    cat = lambda parts: jnp.concatenate(parts, axis=0)
    return cat(qs), cat(bps), cat(bcs), sinks, (cat(extras) if extras else None)


def _att_rows(i):
    return slice(i * WINDOW, (i + 1) * WINDOW)


def _att_bias(n):
    tq = lax.broadcasted_iota(jnp.int32, (WINDOW, WINDOW), 0)
    sk = lax.broadcasted_iota(jnp.int32, (WINDOW, WINDOW), 1)
    valid_c = sk <= tq
    valid_p = (sk - tq) > jnp.where(n > 0, 0, WINDOW)
    dist_c = (tq - sk).astype(F32)
    return jnp.where(valid_p, -dist_c - float(WINDOW), NEG), jnp.where(valid_c, -dist_c, NEG)


def _att_halves(x, lo, kh):
    r = pltpu.roll(x, ATT_HD, 1)
    zero = jnp.zeros_like(x)
    if kh == 0:
        return jnp.where(lo, x, r), jnp.where(lo, x, zero), jnp.where(lo, zero, r)
    return jnp.where(lo, r, x), jnp.where(lo, r, zero), jnp.where(lo, zero, x)


def _att_probs(qm, k2p, k2c, bias_p, bias_c, sink):
    sp = _dot(qm, k2p, NT) + bias_p
    sc = _dot(qm, k2c, NT) + bias_c
    m = jnp.maximum(jnp.maximum(jnp.max(sp, axis=-1, keepdims=True), jnp.max(sc, axis=-1, keepdims=True)), sink)
    ep = jnp.exp(sp - m)
    ec = jnp.exp(sc - m)
    es = jnp.exp(sink - m)
    inv = 1.0 / (jnp.sum(ep, axis=-1, keepdims=True) + jnp.sum(ec, axis=-1, keepdims=True) + es)
    return ep * inv, ec * inv, es * inv


def _attn_fwd(q, kv, sinks):
    t = q.shape[0]
    nb = t // WINDOW

    def body(sink_ref, q_ref, kvp_ref, kvc_ref, o_ref):
        n = pl.program_id(0)
        bias_p, bias_c = _att_bias(n)
        lo = lax.broadcasted_iota(jnp.int32, (WINDOW, LANES), 1) < ATT_HD
        for kh in range(ATT_KVH):
            k2p, _, _ = _att_halves(kvp_ref[:, 0:LANES], lo, kh)
            k2c, _, _ = _att_halves(kvc_ref[:, 0:LANES], lo, kh)
            _, vlo_p, vhi_p = _att_halves(kvp_ref[:, LANES:2 * LANES], lo, kh)
            _, vlo_c, vhi_c = _att_halves(kvc_ref[:, LANES:2 * LANES], lo, kh)
            for first in range(kh * ATT_GROUP, (kh + 1) * ATT_GROUP, ATT_STACK):
                qs, bp, bc, sinks, _ = _att_stack(q_ref, sink_ref, first, lo, bias_p, bias_c)
                pp, pc, _ = _att_probs(qs, k2p, k2c, bp, bc, sinks)
                pp, pc = pp.astype(BF16), pc.astype(BF16)
                for i in range(0, ATT_STACK, 2):
                    even, odd = _att_rows(i), _att_rows(i + 1)
                    out = (_dot(pp[even], vlo_p, NN) + _dot(pc[even], vlo_c, NN)
                           + _dot(pp[odd], vhi_p, NN) + _dot(pc[odd], vhi_c, NN))
                    j = (first + i) // 2
                    o_ref[:, j * LANES:(j + 1) * LANES] = out.astype(BF16)

    return pl.pallas_call(
        body,
        name="attn_fwd",
        grid=(nb,),
        in_specs=[
            pl.BlockSpec(memory_space=pltpu.SMEM),
            pl.BlockSpec((WINDOW, D_MODEL), lambda n: (n, 0)),
            pl.BlockSpec((WINDOW, 2 * LANES), lambda n: (jnp.maximum(n - 1, 0), 0)),
            pl.BlockSpec((WINDOW, 2 * LANES), lambda n: (n, 0)),
        ],
        out_specs=pl.BlockSpec((WINDOW, D_MODEL), lambda n: (n, 0)),
        out_shape=jax.ShapeDtypeStruct((t, D_MODEL), BF16),
        compiler_params=_cparams(("parallel",)),
    )(sinks, q, kv, kv)


def _attn_bwd(q, kv, sinks, dout):
    t = q.shape[0]
    nb = t // WINDOW

    def body(sink_ref, q_ref, kvp_ref, kvc_ref, do_ref, dq_ref, dkv_ref, dsink_ref, carry_ref):
        n = pl.program_id(0)

        @pl.when(n == 0)
        def _():
            carry_ref[...] = jnp.zeros_like(carry_ref)
            dsink_ref[...] = jnp.zeros_like(dsink_ref)

        @pl.when(n == nb)
        def _():
            dkv_ref[...] = carry_ref[...].astype(BF16)

        @pl.when(n < nb)
        def _():
            bias_p, bias_c = _att_bias(n)
            lo = lax.broadcasted_iota(jnp.int32, (WINDOW, LANES), 1) < ATT_HD
            lane1 = lax.broadcasted_iota(jnp.int32, (1, LANES), 1)
            dsink = jnp.zeros((1, LANES), F32)
            halves = []
            for kh in range(ATT_KVH):
                k2p, klo_p, khi_p = _att_halves(kvp_ref[:, 0:LANES], lo, kh)
                k2c, klo_c, khi_c = _att_halves(kvc_ref[:, 0:LANES], lo, kh)
                v2p, _, _ = _att_halves(kvp_ref[:, LANES:2 * LANES], lo, kh)
                v2c, _, _ = _att_halves(kvc_ref[:, LANES:2 * LANES], lo, kh)
                acc = [jnp.zeros((WINDOW, LANES), F32) for _ in range(4)]
                for first in range(kh * ATT_GROUP, (kh + 1) * ATT_GROUP, ATT_STACK):
                    qs, bp, bc, sinks, dos = _att_stack(q_ref, sink_ref, first, lo, bias_p, bias_c, do_ref)
                    pp, pc, ps = _att_probs(qs, k2p, k2c, bp, bc, sinks)
                    dpp = _dot(dos, v2p, NT)
                    dpc = _dot(dos, v2c, NT)
                    delta = jnp.sum(pp * dpp, axis=-1, keepdims=True) + jnp.sum(pc * dpc, axis=-1, keepdims=True)
                    dsp = (pp * (dpp - delta)).astype(BF16)
                    dsc = (pc * (dpc - delta)).astype(BF16)
                    sink_term = ps * delta
                    for i in range(ATT_STACK):
                        dsink = dsink + jnp.where(lane1 == first + i, -jnp.sum(sink_term[_att_rows(i)], axis=0, keepdims=True), 0.0)
                    for i in range(0, ATT_STACK, 2):
                        even, odd = _att_rows(i), _att_rows(i + 1)
                        dq_pair = (_dot(dsp[even], klo_p, NN) + _dot(dsc[even], klo_c, NN)
                                   + _dot(dsp[odd], khi_p, NN) + _dot(dsc[odd], khi_c, NN))
                        j = (first + i) // 2
                        dq_ref[:, j * LANES:(j + 1) * LANES] = (dq_pair * (ATT_HD ** -0.5)).astype(BF16)
                    acc[0] = acc[0] + _dot(dsp, qs, TN)
                    acc[1] = acc[1] + _dot(dsc, qs, TN)
                    acc[2] = acc[2] + _dot(pp.astype(BF16), dos, TN)
                    acc[3] = acc[3] + _dot(pc.astype(BF16), dos, TN)
                halves.append([a + pltpu.roll(a, ATT_HD, 1) for a in acc])
            prev = jnp.concatenate(
                [jnp.where(lo, halves[0][0], halves[1][0]), jnp.where(lo, halves[0][2], halves[1][2])], axis=1)
            cur = jnp.concatenate(
                [jnp.where(lo, halves[0][1], halves[1][1]), jnp.where(lo, halves[0][3], halves[1][3])], axis=1)
            dkv_ref[...] = (carry_ref[...] + prev).astype(BF16)
            carry_ref[...] = cur
            dsink_ref[...] += dsink

    blk = lambda n: jnp.minimum(n, nb - 1)
    return pl.pallas_call(
        body,
        name="attn_bwd",
        grid=(nb + 1,),
        in_specs=[
            pl.BlockSpec(memory_space=pltpu.SMEM),
            pl.BlockSpec((WINDOW, D_MODEL), lambda n: (blk(n), 0)),
            pl.BlockSpec((WINDOW, 2 * LANES), lambda n: (jnp.maximum(blk(n) - 1, 0), 0)),
            pl.BlockSpec((WINDOW, 2 * LANES), lambda n: (blk(n), 0)),
            pl.BlockSpec((WINDOW, D_MODEL), lambda n: (blk(n), 0)),
        ],
        out_specs=[
            pl.BlockSpec((WINDOW, D_MODEL), lambda n: (blk(n), 0)),
            pl.BlockSpec((WINDOW, 2 * LANES), lambda n: (jnp.maximum(n - 1, 0), 0)),
            pl.BlockSpec((1, LANES), lambda n: (0, 0)),
        ],
        out_shape=[
            jax.ShapeDtypeStruct((t, D_MODEL), BF16),
            jax.ShapeDtypeStruct((t, 2 * LANES), BF16),
            jax.ShapeDtypeStruct((1, LANES), F32),
        ],
        scratch_shapes=[pltpu.VMEM((WINDOW, 2 * LANES), F32)],
        compiler_params=_cparams(("arbitrary",)),
    )(sinks, q, kv, kv, dout)


def _ffn_fwd(h, norm_g, w_up, conv_w, conv_b, w_down, tag, after_up=lambda up: None):
    up = _mm_nn(h, w_up, gain=norm_g, name=f"ffn{tag}_up")
    after_up(up)
    act, c = _conv_fwd(up, conv_w, conv_b, name=f"ffn{tag}_conv")
    h_out = _mm_nn(act, w_down, res=h, name=f"ffn{tag}_down")
    return h_out, (up, act, c)


def _ffn_bwd(dh, h, norm_g, w_up, conv_w, conv_b, w_down, saved, tag, deps=()):
    up, act, c = saved
    dw_down = _mm_tn(act, dh, 1, D_MODEL, name=f"ffn{tag}_dwdown", deps=deps)
    dact = _mm_nt(dh, w_down, name=f"ffn{tag}_dact", deps=deps)
    dup, dconv_w, dconv_b = _conv_bwd(up, conv_w, c, dact, name=f"ffn{tag}_dconv")
    dw_up = _mm_tn(h, dup, N_CHIPS, CONV_COLS, stacked=True, gain=norm_g, name=f"ffn{tag}_dwup")
    dh_in, dnorm = _mm_nt(dup, w_up, stacked=True, norm_of=(h, norm_g, dh), name=f"ffn{tag}_dxn")
    return dh_in, dict(ffn_w_down=dw_down, ffn_w_up=dw_up, ffn_conv_w=dconv_w, ffn_conv_b=dconv_b, ffn_norm=dnorm)


def _local_step(x, target, w, fetch=lambda w, stage, after: w, hook=lambda point, dh, grads: ()):
    proj = _mm_nn(x, w["hg_w_in"], gain=w["hg_norm"], name="hg_in")
    o, y, states = _hgrn_fwd(proj, w["hg_lb"], w["hg_out_norm"])
    w = fetch(w, "mixer_out", y)
    fetch(w, "layer0_relay", y)
    h_a = _mm_nn(y, w["hg_w_out"], res=x, name="hg_out")
    w = fetch(w, "layer0", h_a)
    h1, ffn0 = _ffn_fwd(h_a, w["ffn_norm"][0], w["ffn_w_up"][0], w["ffn_conv_w"][0], w["ffn_conv_b"][0], w["ffn_w_down"][0], 0,
                        lambda up: fetch(w, "layer1_relay", up))
    w = fetch(w, "layer1", h1)
    kv = _mm_nn(h1, w["w_kv"], gain=w["kv_norm"], out_dtype=BF16, name="kv_proj")
    qa = _mm_nn(h1, w["attn_w_q"], gain=w["attn_norm"], out_dtype=BF16, name="attn_q")
    ao = _attn_fwd(qa, kv, w["attn_sinks"])
    h_b = _mm_nn(ao, w["attn_w_o"], res=h1, name="attn_o")
    h2, ffn1 = _ffn_fwd(h_b, w["ffn_norm"][1], w["ffn_w_up"][1], w["ffn_conv_w"][1], w["ffn_conv_b"][1], w["ffn_w_down"][1], 1)
    dh2, d_final, loss = _loss_head(h2, w["final_norm"], target)

    dh_b, g1 = _ffn_bwd(dh2, h_b, w["ffn_norm"][1], w["ffn_w_up"][1], w["ffn_conv_w"][1], w["ffn_conv_b"][1], w["ffn_w_down"][1], ffn1, 1)
    deps = hook("ffn1", dh_b, g1)
    dw_o = _mm_tn(ao, dh_b, 1, D_MODEL, name="attn_dwo", deps=deps)
    dao = _mm_nt(dh_b, w["attn_w_o"], out_dtype=BF16, name="attn_dao", deps=deps)
    dqa, dkv, dsinks = _attn_bwd(qa, kv, w["attn_sinks"], dao)
    dw_q = _mm_tn(h1, dqa, 1, D_MODEL, gain=w["attn_norm"], name="attn_dwq")
    dh1, d_attn_norm = _mm_nt(dqa, w["attn_w_q"], norm_of=(h1, w["attn_norm"], dh_b), name="attn_dxa")
    dw_kv = _mm_tn(h1, dkv, 1, 2 * LANES, gain=w["kv_norm"], name="kv_dw")
    dh1, d_kv_norm = _mm_nt(dkv, w["w_kv"], norm_of=(h1, w["kv_norm"], dh1), name="kv_dx")
    deps = hook("attn", dh1, dict(attn_w_o=dw_o, attn_w_q=dw_q, w_kv=dw_kv))
    dh_a, g0 = _ffn_bwd(dh1, h_a, w["ffn_norm"][0], w["ffn_w_up"][0], w["ffn_conv_w"][0], w["ffn_conv_b"][0], w["ffn_w_down"][0], ffn0, 0, deps)
    dw_out = _mm_tn(y, dh_a, 1, D_MODEL, name="hg_dwout")
    deps = hook("ffn0", dh_a, dict(g0, hg_w_out=dw_out))
    dy = _mm_nt(dh_a, w["hg_w_out"], out_dtype=BF16, name="hg_dy", deps=deps)
    dproj, dlb, d_out_norm = _hgrn_bwd(proj, w["hg_lb"], w["hg_out_norm"], o, states, dy)
    deps = hook("hgrn", dproj, None)
    dw_in = _mm_tn(x, dproj, N_CHIPS, D_MODEL, stacked=True, gain=w["hg_norm"], name="hg_dwin", deps=deps)
    deps = hook("hg_w", dproj, dict(hg_w_in=dw_in))
    dx, d_hg_norm = _mm_nt(dproj, w["hg_w_in"], stacked=True, norm_of=(x, w["hg_norm"], dh_a), name="hg_dxn", deps=deps)

    grads = dict(
        hg_norm=d_hg_norm, hg_w_in=dw_in, hg_lb=dlb, hg_out_norm=d_out_norm, hg_w_out=dw_out,
        kv_norm=d_kv_norm, w_kv=dw_kv, attn_norm=d_attn_norm, attn_w_q=dw_q, attn_sinks=dsinks, attn_w_o=dw_o,
        final_norm=d_final,
    )
    for name in g0:
        grads[name] = [g0[name], g1[name]]
    return loss, dx, grads


ANY = pl.BlockSpec(memory_space=pl.ANY)


def _place():
    x, y, c = lax.axis_index("x"), lax.axis_index("y"), lax.axis_index("c")
    chips = [(1 - x, y), (x, 1 - y), (1 - x, 1 - y)]
    return x, y, c, chips


def _rcopy(src, dst, send_sem, recv_sem, to):
    return pltpu.make_async_remote_copy(src_ref=src, dst_ref=dst, send_sem=send_sem, recv_sem=recv_sem, device_id=to, device_id_type=MESH)


HBM = pl.BlockSpec(memory_space=pltpu.HBM)
SEM = pl.BlockSpec(memory_space=pltpu.SEMAPHORE)
EFFECT = pltpu.SideEffectType.DATAFLOW_SIDE_EFFECTING


def _in_hbm(a):
    return pltpu.with_memory_space_constraint(a, pltpu.HBM)


def _place_shard(shard, place, dtype, name, deps=(), layer=None):
    r, cols = shard.shape[-2:]
    tr = _pick(r, ELEM_ROWS)
    src = pl.BlockSpec((tr, cols), lambda i, place_ref: (i, 0)) if layer is None else pl.BlockSpec((None, tr, cols), lambda i, place_ref: (layer, i, 0))

    def body(place_ref, s_ref, *rest):
        o_ref = rest[-1]
        o_ref[...] = s_ref[...].astype(o_ref.dtype)

    return pl.pallas_call(
        body,
        name=name,
        grid_spec=pltpu.PrefetchScalarGridSpec(
            num_scalar_prefetch=1,
            grid=(r // tr,),
            in_specs=[src] + _dep_specs(deps),
            out_specs=pl.BlockSpec((None, tr, cols), lambda i, place_ref: (place_ref[0], i, 0)),
        ),
        out_shape=jax.ShapeDtypeStruct((N_CHIPS, r, cols), dtype),
        compiler_params=_cparams(("parallel",)),
    )(place, shard, *deps)


def _start_copies(name, bufs, n_sem, copies):
    n = len(bufs)

    def body(*refs):
        for cp in copies(refs[:n], refs[n], refs[n + 1]):
            cp.start()
        refs[-1][...] = jnp.zeros_like(refs[-1])

    outs = pl.pallas_call(
        body,
        name=name,
        in_specs=[HBM] * n,
        out_specs=[SEM, SEM] + [HBM] * n + [pl.BlockSpec(memory_space=pltpu.VMEM)],
        out_shape=[pltpu.SemaphoreType.DMA((n_sem,)), pltpu.SemaphoreType.DMA((n_sem,))] + [pltpu.HBM(b.shape, b.dtype) for b in bufs]
        + [jax.ShapeDtypeStruct((SUBLANES, LANES), F32)],
        input_output_aliases={i: 2 + i for i in range(n)},
        compiler_params=pltpu.CompilerParams(has_side_effects=EFFECT),
    )(*[_in_hbm(b) for b in bufs])
    return outs[0], outs[1], list(outs[2:-1]), outs[-1]


def _wait_copies(name, bufs, send_sems, recv_sems, after, copies):
    n = len(bufs)

    def body(*refs):
        for cp in copies(refs[:n], refs[n], refs[n + 1]):
            cp.wait_send()
            cp.wait_recv()

    return pl.pallas_call(
        body,
        name=name,
        in_specs=[HBM] * n + [SEM, SEM, ANY],
        out_specs=[HBM] * n,
        out_shape=[pltpu.HBM(b.shape, b.dtype) for b in bufs],
        input_output_aliases={i: i for i in range(n)},
        compiler_params=pltpu.CompilerParams(has_side_effects=EFFECT),
    )(*bufs, send_sems, recv_sems, after)


def _relay_copies(name, bufs, send_sems, recv_sems, after, landed, n_sem, onward):
    n = len(bufs)

    def body(*refs):
        for cp in landed(refs[:n], refs[n], refs[n + 1]):
            cp.wait_send()
            cp.wait_recv()
        for cp in onward(refs[:n], refs[n + 3], refs[n + 4]):
            cp.start()
        refs[-1][...] = jnp.zeros_like(refs[-1])

    outs = pl.pallas_call(
        body,
        name=name,
        in_specs=[HBM] * n + [SEM, SEM, ANY],
        out_specs=[SEM, SEM] + [HBM] * n + [pl.BlockSpec(memory_space=pltpu.VMEM)],
        out_shape=[pltpu.SemaphoreType.DMA((n_sem,)), pltpu.SemaphoreType.DMA((n_sem,))] + [pltpu.HBM(b.shape, b.dtype) for b in bufs]
        + [jax.ShapeDtypeStruct((SUBLANES, LANES), F32)],
        input_output_aliases={i: 2 + i for i in range(n)},
        compiler_params=pltpu.CompilerParams(has_side_effects=EFFECT),
    )(*bufs, send_sems, recv_sems, after)
    return outs[0], outs[1], list(outs[2:-1]), outs[-1]


def _gather_half_copies(first, count, over_ici):
    def copies(refs, send_sems, recv_sems):
        x, y, c, chips = _place()
        out = []
        for i in range(count):
            h = refs[i].shape[1] // 2
            mine = pl.ds(c * h, h)
            for j, (px, py) in enumerate(chips):
                k = 3 * (first + i) + j
                slot = 2 * x + y if over_ici else 2 * px + py
                to = (px, py, c) if over_ici else (x, y, 1 - c)
                out.append(_rcopy(refs[i].at[slot, mine], refs[i].at[slot, mine], send_sems.at[k], recv_sems.at[k], to))
        return out

    return copies


def _gather_copies(first, count):
    def copies(refs, send_sems, recv_sems):
        x, y, c, chips = _place()
        me = 2 * x + y
        out = []
        for i in range(count):
            for j, (px, py) in enumerate(chips):
                k = 3 * (first + i) + j
                out.append(_rcopy(refs[i].at[me], refs[i].at[me], send_sems.at[k], recv_sems.at[k], (px, py, c)))
        return out

    return copies


def _swap_copies(n):
    def copies(refs, send_sems, recv_sems):
        x, y, c, _ = _place()
        out = []
        for i in range(n):
            h = refs[i].shape[1] // 2
            out.append(_rcopy(refs[i].at[:, pl.ds((1 - c) * h, h)], refs[n + i], send_sems.at[i], recv_sems.at[i], (x, y, 1 - c)))
        return out

    return copies


def _partial_copies(n):
    def copies(refs, send_sems, recv_sems):
        x, y, c, chips = _place()
        out = []
        for i in range(n):
            for j, (px, py) in enumerate(chips):
                out.append(_rcopy(refs[i].at[2 * px + py], refs[n + i].at[j], send_sems.at[3 * i + j], recv_sems.at[3 * i + j], (px, py, c)))
        return out

    return copies


def _share_copies(n):
    def copies(refs, send_sems, recv_sems):
        x, y, c, _ = _place()
        return [_rcopy(refs[i].at[c], refs[i].at[c], send_sems.at[i], recv_sems.at[i], (x, y, 1 - c)) for i in range(n)]

    return copies


def _small_layout(groups):
    flat = [a for g in groups for a in g]
    rows = -(-sum(a.shape[0] for a in flat) // SUBLANES) * SUBLANES
    return flat, rows, max(a.shape[1] for a in flat)


def _pack_small(groups, device):
    flat, rows, cols = _small_layout(groups)

    def body(dev_ref, *refs):
        o_ref = refs[-1]
        o_ref[...] = jnp.zeros_like(o_ref)
        r0 = 0
        for a_ref in refs[:-1]:
            r, w = a_ref.shape
            o_ref[r0:r0 + r, 0:w] = a_ref[...]
            r0 += r

    return pl.pallas_call(
        body,
        name="small_pack",
        grid_spec=pltpu.PrefetchScalarGridSpec(
            num_scalar_prefetch=1,
            grid=(1,),
            in_specs=[pl.BlockSpec(a.shape, lambda i, dev_ref: (0, 0)) for a in flat],
            out_specs=pl.BlockSpec((None, rows, cols), lambda i, dev_ref: (dev_ref[0], 0, 0)),
        ),
        out_shape=jax.ShapeDtypeStruct((N_DEV, rows, cols), F32),
        compiler_params=_cparams(("arbitrary",)),
    )(device, *flat)


def _small_copies(refs, send_sems, recv_sems):
    x, y, c, _ = _place()
    me = 4 * x + 2 * y + c
    out = []
    for k in range(1, N_DEV):
        peer = (x ^ (k >> 2), y ^ ((k >> 1) & 1), c ^ (k & 1))
        out.append(_rcopy(refs[0].at[me], refs[0].at[me], send_sems.at[k - 1], recv_sems.at[k - 1], peer))
    return out


def _sum_small(slots, groups, widths):
    out_shapes = [(sum(a.shape[0] for a in g), wd or g[0].shape[1]) for g, wd in zip(groups, widths)]

    def body(s_ref, *refs):
        outs, acc_ref = refs[:-1], refs[-1]
        acc = s_ref[0]
        for d in range(1, N_DEV):
            acc = acc + s_ref[d]
        acc_ref[...] = acc
        r0 = 0
        for o_ref in outs:
            r, w = o_ref.shape
            o_ref[...] = acc_ref[r0:r0 + r, 0:w]
            r0 += r

    vmem = pl.BlockSpec(memory_space=pltpu.VMEM)
    return pl.pallas_call(
        body,
        name="small_sum",
        in_specs=[vmem],
        out_specs=[vmem] * len(groups),
        out_shape=[jax.ShapeDtypeStruct(s, F32) for s in out_shapes],
        scratch_shapes=[pltpu.VMEM(slots.shape[1:], F32)],
        compiler_params=pltpu.CompilerParams(vmem_limit_bytes=VMEM_LIMIT_BYTES),
    )(slots)


def _adamw_small(items):
    n = len(items)

    def body(*refs):
        for i in range(n):
            w_ref, m_ref, v_ref, g_ref = refs[4 * i:4 * i + 4]
            d_ref, nm_ref, nv_ref = refs[4 * n + 3 * i:4 * n + 3 * i + 3]
            d_ref[...], nm_ref[...], nv_ref[...] = _adamw_math(w_ref[...], m_ref[...], v_ref[...], g_ref[...])

    vmem = pl.BlockSpec(memory_space=pltpu.VMEM)
    outs = pl.pallas_call(
        body,
        name="adamw_small",
        in_specs=[vmem] * (4 * n),
        out_specs=[vmem] * (3 * n),
        out_shape=[jax.ShapeDtypeStruct(it[0].shape, F32) for it in items for _ in range(3)],
        compiler_params=pltpu.CompilerParams(vmem_limit_bytes=VMEM_LIMIT_BYTES),
    )(*[a for it in items for a in it])
    return [tuple(outs[3 * i:3 * i + 3]) for i in range(n)]


class _Reduction:
    def __init__(self, tag, grads, place):
        self.tag, self.n, self.place = tag, len(grads), place
        lands = [lax.empty((N_CHIPS, g.shape[1] // 2, g.shape[2]), F32) for g in grads]
        self._start("swap", list(grads) + lands, self.n, _swap_copies(self.n))

    def _start(self, stage, bufs, n_sem, copies):
        *self.flight, self.token = _start_copies(f"rs_{stage}_start_{self.tag}", bufs, n_sem, copies)

    def _landed(self, stage, after, copies):
        send_sems, recv_sems, bufs = self.flight
        return _wait_copies(f"rs_{stage}_wait_{self.tag}", bufs, send_sems, recv_sems, after, copies)

    def to_chips(self, after):
        n = self.n
        bufs = self._landed("swap", after, _swap_copies(n))
        sums = [_add_core_halves(g, o, self.place, name=f"rs_add_core_{self.tag}_{i}") for i, (g, o) in enumerate(zip(bufs[:n], bufs[n:]))]
        self.mine = [f for f, _ in sums]
        parts = [b for _, b in sums]
        lands = [lax.empty((3,) + p.shape[1:], BF16) for p in parts]
        self._start("send", parts + lands, 3 * n, _partial_copies(n))

    def to_core(self, after):
        n = self.n
        bufs = self._landed("send", after, _partial_copies(n))
        halves = [_add_chip_partials(f, o, self.place, name=f"rs_add_chip_{self.tag}_{i}") for i, (f, o) in enumerate(zip(self.mine, bufs[n:]))]
        self._start("share", halves, n, _share_copies(n))

    def finish(self, after):
        return [b.reshape((-1,) + b.shape[2:]) for b in self._landed("share", after, _share_copies(self.n))]


ELEM_ROWS = (256, 176, 128, 64, 32, 16, 8)


def _add_core_halves(grad, got, place, name):
    s, r, cols = grad.shape
    h = r // 2
    tr = _pick(h, ELEM_ROWS)

    def body(place_ref, g_ref, o_ref, f_ref, b_ref):
        acc = g_ref[...] + o_ref[...]
        b_ref[...] = acc.astype(BF16)

        @pl.when(pl.program_id(1) == place_ref[0])
        def _():
            f_ref[...] = acc

    blk = pl.BlockSpec((None, tr, cols), lambda i, k, place_ref: (k, i, 0))
    return pl.pallas_call(
        body,
        name=name,
        grid_spec=pltpu.PrefetchScalarGridSpec(
            num_scalar_prefetch=1,
            grid=(h // tr, s),
            in_specs=[pl.BlockSpec((None, None, tr, cols), lambda i, k, place_ref: (k, place_ref[1], i, 0)), blk],
            out_specs=[pl.BlockSpec((tr, cols), lambda i, k, place_ref: (i, 0)), blk],
        ),
        out_shape=[jax.ShapeDtypeStruct((h, cols), F32), jax.ShapeDtypeStruct((s, h, cols), BF16)],
        compiler_params=_cparams(("parallel", "arbitrary")),
    )(place, grad.reshape(s, 2, h, cols), got)


def _add_chip_partials(mine, got, place, name):
    h, cols = mine.shape
    tr = _pick(h, ELEM_ROWS)

    def body(place_ref, m_ref, g_ref, o_ref):
        acc = m_ref[...]
        for j in range(3):
            acc = acc + g_ref[j].astype(F32)
        o_ref[...] = acc

    return pl.pallas_call(
        body,
        name=name,
        grid_spec=pltpu.PrefetchScalarGridSpec(
            num_scalar_prefetch=1,
            grid=(h // tr,),
            in_specs=[
                pl.BlockSpec((tr, cols), lambda i, place_ref: (i, 0)),
                pl.BlockSpec((3, tr, cols), lambda i, place_ref: (0, i, 0)),
            ],
            out_specs=pl.BlockSpec((None, tr, cols), lambda i, place_ref: (place_ref[1], i, 0)),
        ),
        out_shape=jax.ShapeDtypeStruct((2, h, cols), F32),
        compiler_params=_cparams(("parallel",)),
    )(place, mine, got)


def _adamw_math(w, m, v, g):
    nm = ADAM_B1 * m + (1.0 - ADAM_B1) * g
    nv = ADAM_B2 * v + (1.0 - ADAM_B2) * (g * g)
    m_hat = nm * (1.0 / (1.0 - ADAM_B1 ** ADAM_STEP))
    v_hat = nv * (1.0 / (1.0 - ADAM_B2 ** ADAM_STEP))
    return -ADAM_LR * (m_hat / (jnp.sqrt(v_hat) + ADAM_EPS) + ADAM_WD * w), nm, nv


def _adamw_layer(w, m, v, g, layer, prev, name):
    nl, r, cols = w.shape
    tr = _pick(r, ELEM_ROWS)

    def body(w_ref, m_ref, v_ref, g_ref, *rest):
        go_ref, d_ref, nm_ref, nv_ref = rest[-4:]
        gv = g_ref[...]
        d_ref[...], nm_ref[...], nv_ref[...] = _adamw_math(w_ref[...], m_ref[...], v_ref[...], gv)
        go_ref[...] = gv

    lay = pl.BlockSpec((None, tr, cols), lambda i: (layer, i, 0))
    return pl.pallas_call(
        body,
        name=name,
        grid=(r // tr,),
        in_specs=[lay] * 3 + [pl.BlockSpec((tr, cols), lambda i: (i, 0))] + ([ANY] * 4 if prev else []),
        out_specs=[lay] * 4,
        out_shape=[jax.ShapeDtypeStruct((nl, r, cols), F32)] * 4,
        input_output_aliases={4 + k: k for k in range(4)} if prev else {},
        compiler_params=_cparams(("parallel",)),
    )(w, m, v, g, *(prev or ()))


def _adamw(w, m, v, g, name):
    r, cols = w.shape
    tr = _pick(r, ELEM_ROWS)

    def body(w_ref, m_ref, v_ref, g_ref, d_ref, nm_ref, nv_ref):
        d_ref[...], nm_ref[...], nv_ref[...] = _adamw_math(w_ref[...], m_ref[...], v_ref[...], g_ref[...])

    blk = pl.BlockSpec((tr, cols), lambda i: (i, 0))
    return pl.pallas_call(
        body,
        name=name,
        grid=(r // tr,),
        in_specs=[blk] * 4,
        out_specs=[blk] * 3,
        out_shape=[jax.ShapeDtypeStruct((r, cols), F32)] * 3,
        compiler_params=_cparams(("parallel",)),
    )(w, m, v, g)


SMALL_COLS = 384
SMALL_ROWS = 16


def _pad_rows(flat, rows, cols):
    return jnp.pad(flat, (0, rows * cols - flat.shape[0])).reshape(rows, cols)


def kernel(x, hg_norm, hg_w_in, hg_lb_logits, hg_out_norm, hg_w_out, kv_norm, w_kv, attn_norm, attn_w_q, attn_sinks, attn_w_o, ffn_norm, ffn_w_up, ffn_conv_w, ffn_conv_b, ffn_w_down, final_norm, loss_target, m_hg_norm, m_hg_w_in, m_hg_lb_logits, m_hg_out_norm, m_hg_w_out, m_kv_norm, m_w_kv, m_attn_norm, m_attn_w_q, m_attn_sinks, m_attn_w_o, m_ffn_norm, m_ffn_w_up, m_ffn_conv_w, m_ffn_conv_b, m_ffn_w_down, m_final_norm, v_hg_norm, v_hg_w_in, v_hg_lb_logits, v_hg_out_norm, v_hg_w_out, v_kv_norm, v_w_kv, v_attn_norm, v_attn_w_q, v_attn_sinks, v_attn_w_o, v_ffn_norm, v_ffn_w_up, v_ffn_conv_w, v_ffn_conv_b, v_ffn_w_down, v_final_norm):
    wts = dict(hg_norm=hg_norm, hg_w_in=hg_w_in, hg_lb_logits=hg_lb_logits, hg_out_norm=hg_out_norm, hg_w_out=hg_w_out, kv_norm=kv_norm, w_kv=w_kv, attn_norm=attn_norm, attn_w_q=attn_w_q, attn_sinks=attn_sinks, attn_w_o=attn_w_o, ffn_norm=ffn_norm, ffn_w_up=ffn_w_up, ffn_conv_w=ffn_conv_w, ffn_conv_b=ffn_conv_b, ffn_w_down=ffn_w_down, final_norm=final_norm)
    mom1 = dict(hg_norm=m_hg_norm, hg_w_in=m_hg_w_in, hg_lb_logits=m_hg_lb_logits, hg_out_norm=m_hg_out_norm, hg_w_out=m_hg_w_out, kv_norm=m_kv_norm, w_kv=m_w_kv, attn_norm=m_attn_norm, attn_w_q=m_attn_w_q, attn_sinks=m_attn_sinks, attn_w_o=m_attn_w_o, ffn_norm=m_ffn_norm, ffn_w_up=m_ffn_w_up, ffn_conv_w=m_ffn_conv_w, ffn_conv_b=m_ffn_conv_b, ffn_w_down=m_ffn_w_down, final_norm=m_final_norm)
    mom2 = dict(hg_norm=v_hg_norm, hg_w_in=v_hg_w_in, hg_lb_logits=v_hg_lb_logits, hg_out_norm=v_hg_out_norm, hg_w_out=v_hg_w_out, kv_norm=v_kv_norm, w_kv=v_w_kv, attn_norm=v_attn_norm, attn_w_q=v_attn_w_q, attn_sinks=v_attn_sinks, attn_w_o=v_attn_w_o, ffn_norm=v_ffn_norm, ffn_w_up=v_ffn_w_up, ffn_conv_w=v_ffn_conv_w, ffn_conv_b=v_ffn_conv_b, ffn_w_down=v_ffn_w_down, final_norm=v_final_norm)
    names = list(wts)
    chip = 2 * lax.axis_index("x") + lax.axis_index("y")
    core = lax.axis_index("c")
    fs = D_FF // N_CHIPS
    ds = D_MODEL // N_CHIPS

    place_arr = jnp.stack([chip, core]).astype(jnp.int32)
    small = jnp.concatenate([hg_norm.reshape(-1), hg_lb_logits.reshape(-1), ffn_conv_w.reshape(-1)])
    n_small = small.shape[0]
    shards = [
        ("small", _pad_rows(small, SMALL_ROWS, SMALL_COLS), F32, None), ("hg_w_in", hg_w_in, BF16, 0),
        ("hg_w_out", hg_w_out, BF16, 0), ("ffn_w_up0", ffn_w_up, BF16, 0), ("ffn_w_down0", ffn_w_down, BF16, 0),
        ("w_kv", w_kv, BF16, None), ("attn_w_q", attn_w_q, BF16, 0), ("attn_w_o", attn_w_o, BF16, 0),
        ("ffn_w_up1", ffn_w_up, BF16, 1), ("ffn_w_down1", ffn_w_down, BF16, 1),
    ]
    n_first = 3
    spans = dict(layer0=(0, 2), layer1=(2, 7))

    def first_copies(refs, send_sems, recv_sems):
        return (_gather_copies(0, 1)(refs[:1], send_sems, recv_sems) + _gather_half_copies(1, 1, True)(refs[1:2], send_sems, recv_sems)
                + _gather_copies(2, 1)(refs[2:3], send_sems, recv_sems))

    placed = [_place_shard(s, place_arr, dt, name=f"place_{nm}", layer=ly) for nm, s, dt, ly in shards[:n_first]]
    first = _start_copies("gather_start_first", placed, 3 * n_first, first_copies)
    placed = [_place_shard(s, place_arr, dt, name=f"place_{nm}", deps=(first[3],), layer=ly) for nm, s, dt, ly in shards[n_first:]]
    rest = _start_copies("gather_start_rest", placed, 3 * len(placed), _gather_half_copies(0, len(placed), True))
    relayed = {}

    def fetch(w, stage, after):
        if stage == "first":
            w_in = _relay_copies("gather_first_relay", first[2][1:2], first[0], first[1], after,
                                 _gather_half_copies(1, 1, True), 3, _gather_half_copies(0, 1, False))
            got = _wait_copies("gather_wait_small", first[2][:1], first[0], first[1], w_in[3], _gather_copies(0, 1))
            got += _wait_copies("gather_wait_first", w_in[2], w_in[0], w_in[1], got[0], _gather_half_copies(0, 1, False))
        elif stage == "mixer_out":
            got = _wait_copies("gather_wait_mixer_out", first[2][2:], first[0], first[1], after, _gather_copies(2, 1))
        elif stage.endswith("_relay"):
            lo, hi = spans[stage[:-6]]
            relayed[stage[:-6]] = _relay_copies(
                f"gather_{stage}", rest[2][lo:hi], rest[0], rest[1], after,
                _gather_half_copies(lo, hi - lo, True), 3 * (hi - lo), _gather_half_copies(0, hi - lo, False))
            return w
        else:
            lo, hi = spans[stage]
            send_sems, recv_sems, bufs, _ = relayed[stage]
            got = _wait_copies(f"gather_wait_{stage}", bufs, send_sems, recv_sems, after, _gather_half_copies(0, hi - lo, False))
        w = dict(w)
        if stage == "first":
            g_small = got[0].reshape(N_CHIPS, -1)[:, :n_small]
            conv_w = g_small[:, 3 * ds:].reshape(N_CHIPS, 2, 3, fs).transpose(1, 2, 0, 3).reshape(2, 3, D_FF)
            w.update(
                hg_norm=g_small[:, :ds].reshape(1, D_MODEL),
                hg_lb=g_small[:, ds:3 * ds].reshape(N_CHIPS, 2, ds).transpose(1, 0, 2).reshape(2, D_MODEL),
                ffn_conv_w=[conv_w[0], conv_w[1]], hg_w_in=got[1],
            )
        elif stage == "mixer_out":
            w.update(hg_w_out=got[0].reshape(1, D_MODEL, D_MODEL))
        elif stage == "layer0":
            w.update(ffn_w_up=[got[0], None], ffn_w_down=[got[1].reshape(1, D_FF, D_MODEL), None])
        else:
            w.update(
                w_kv=got[0].reshape(1, D_MODEL, 2 * LANES), attn_w_q=got[1].reshape(1, D_MODEL, D_MODEL),
                attn_w_o=got[2].reshape(1, D_MODEL, D_MODEL), ffn_w_up=[w["ffn_w_up"][0], got[3]],
                ffn_w_down=[w["ffn_w_down"][0], got[4].reshape(1, D_FF, D_MODEL)],
            )
        return w

    whole = dict(
        hg_out_norm=hg_out_norm, kv_norm=kv_norm.reshape(1, D_MODEL), attn_norm=attn_norm, attn_sinks=attn_sinks.reshape(ATT_QH),
        ffn_norm=[ffn_norm[0:1], ffn_norm[1:2]], ffn_conv_b=[ffn_conv_b[0:1], ffn_conv_b[1:2]], final_norm=final_norm.reshape(1, D_MODEL),
    )
    whole = fetch(whole, "first", rest[3])

    red, layer1 = {}, {}

    def by_rows(g, rows):
        return g.reshape(N_CHIPS, rows, g.shape[2])

    def hook(point, dh, grads):
        if point == "ffn1":
            red["ffn1"] = _Reduction("ffn1", [by_rows(grads["ffn_w_down"], fs), grads["ffn_w_up"]], place_arr)
            return (red["ffn1"].token,)
        if point == "attn":
            red["ffn1"].to_chips(dh)
            layer1.update(grads)
            return (red["ffn1"].token,)
        if point == "ffn0":
            group = [by_rows(layer1["attn_w_o"], ds), by_rows(layer1["attn_w_q"], ds), by_rows(layer1["w_kv"], ds),
                     by_rows(grads["ffn_w_down"], fs), grads["ffn_w_up"], by_rows(grads["hg_w_out"], ds)]
            red["mid"] = _Reduction("mid", group, place_arr)
            return (red["mid"].token,)
        if point == "hgrn":
            red["ffn1"].to_core(dh)
            red["mid"].to_chips(dh)
            return (red["ffn1"].token, red["mid"].token)
        red["hg"] = _Reduction("hg", [grads["hg_w_in"]], place_arr)
        return (red["hg"].token,)

    loss, dx, grads = _local_step(x[0], loss_target[0], whole, fetch, hook)

    small_names = ["hg_out_norm", "attn_sinks", "kv_norm", "attn_norm", "ffn_norm", "ffn_conv_b", "final_norm", "hg_norm", "hg_lb_logits", "ffn_conv_w"]
    groups = [[loss]] + [grads[n] if isinstance(grads[n], list) else [grads[n]] for n in small_names[:-2]] + [[grads["hg_lb"]], grads["ffn_conv_w"]]
    widths = [None, None, ATT_QH] + [None] * 8
    packed = _pack_small(groups, jnp.reshape(2 * chip + core, (1,)).astype(jnp.int32))
    small_flight = _start_copies("small_start", [packed], N_DEV - 1, _small_copies)
    red["hg"].to_chips(small_flight[3])

    out_g, out_d, out_m, out_v = {}, {}, {}, {}

    def update(name, g2):
        shape = wts[name].shape
        d2, m2, v2 = _adamw(wts[name].reshape(g2.shape), mom1[name].reshape(g2.shape), mom2[name].reshape(g2.shape), g2, name=f"adamw_{name}")
        out_g[name], out_d[name], out_m[name], out_v[name] = g2.reshape(shape), d2.reshape(shape), m2.reshape(shape), v2.reshape(shape)
        return d2

    def update_layer(name, g2, layer, prev):
        res = _adamw_layer(wts[name], mom1[name], mom2[name], g2, layer, prev, name=f"adamw_{name}{layer}")
        out_g[name], out_d[name], out_m[name], out_v[name] = res
        return res

    g_down1, g_up1 = red["ffn1"].finish(red["hg"].token)
    down1 = update_layer("ffn_w_down", g_down1, 1, None)
    up1 = update_layer("ffn_w_up", g_up1, 1, None)
    summed = _sum_small(_wait_copies("small_wait", small_flight[2], small_flight[0], small_flight[1], up1[3], _small_copies)[0], groups, widths)
    loss_out = summed[0][0, 0]
    small_grads = dict(zip(small_names, summed[1:]))
    small_grads["hg_norm"] = lax.dynamic_slice(small_grads["hg_norm"], (0, chip * ds), (1, ds))
    small_grads["hg_lb_logits"] = lax.dynamic_slice(small_grads["hg_lb_logits"], (0, chip * ds), (2, ds))
    small_grads["ffn_conv_w"] = lax.dynamic_slice(small_grads["ffn_conv_w"], (0, chip * fs), (2 * 3, fs))
    red["mid"].to_core(up1[1])
    g_o, g_q, g_kv, g_down0, g_up0, g_out = red["mid"].finish(up1[2])
    update("attn_w_o", g_o)
    update("attn_w_q", g_q)
    update("w_kv", g_kv)
    update("hg_w_out", g_out)
    update_layer("ffn_w_down", g_down0, 0, down1)
    last = update_layer("ffn_w_up", g_up0, 0, up1)
    red["hg"].to_core(last[1])
    (g_in,) = red["hg"].finish(last[2])
    update("hg_w_in", g_in)

    as_2d = lambda a, n: a.reshape(small_grads[n].shape)
    updated = _adamw_small([(as_2d(wts[n], n), as_2d(mom1[n], n), as_2d(mom2[n], n), small_grads[n]) for n in small_names])
    for n, (d2, m2, v2) in zip(small_names, updated):
        shape = wts[n].shape
        out_g[n], out_d[n], out_m[n], out_v[n] = small_grads[n].reshape(shape), d2.reshape(shape), m2.reshape(shape), v2.reshape(shape)

    grad_x = dx.reshape(x.shape)
    return (loss_out, grad_x, *[out_g[n] for n in names], *[out_d[n] for n in names], *[out_m[n] for n in names], *[out_v[n] for n in names])
```

```python
import functools

import jax
import jax.numpy as jnp
from jax import lax
from jax.experimental import pallas as pl
from jax.experimental.pallas import tpu as pltpu

F32 = jnp.float32
BF16 = jnp.bfloat16
MESH = pl.DeviceIdType.MESH

EPS = 1e-6
D_MODEL = 1024
HG_HEADS = 8
HG_DK = 128
HG_CHUNK = 64
ATT_HD = 64
ATT_QH = 16
ATT_KVH = 2
ATT_GROUP = ATT_QH // ATT_KVH
WINDOW = 128
D_FF = 2816
N_CHIPS = 4
N_DEV = 8
LANES = 128
SUBLANES = 8
VMEM_LIMIT_BYTES = 56 * 1024 * 1024
NEG = -1e30
ALIBI_SLOPES = tuple(2.0 ** (-8.0 * h / ATT_QH) for h in range(1, ATT_QH + 1))

ADAM_LR = 0.001
ADAM_B1 = 0.9
ADAM_B2 = 0.999
ADAM_EPS = 1e-08
ADAM_WD = 0.01
ADAM_STEP = 10


def _cparams(sem=None):
    return pltpu.CompilerParams(dimension_semantics=sem, vmem_limit_bytes=VMEM_LIMIT_BYTES)


def _pick(n, cands):
    for c in cands:
        if n % c == 0:
            return c
    return n


def _sigmoid(x):
    return 0.5 * jnp.tanh(0.5 * x) + 0.5


def _dot(a, b, dims):
    return lax.dot_general(a, b, (dims, ((), ())), preferred_element_type=F32)


NN = ((1,), (0,))
NT = ((1,), (1,))
TN = ((0,), (0,))


MM_ROWS = 1024


def _rms_stats(xv):
    rstd = lax.rsqrt(jnp.mean(xv * xv, axis=-1, keepdims=True) + EPS)
    return xv * rstd, rstd


def _mm_operand(a_ref, gain_ref):
    if gain_ref is None:
        return a_ref[...].astype(BF16)
    return (_rms_stats(a_ref[...])[0] * gain_ref[...]).astype(BF16)


def _mm_nn(a, w, res=None, out_dtype=F32, name="mm_nn", gain=None):
    m, k = a.shape
    s, _, ns = w.shape
    tm = min(m, MM_ROWS)
    tn = _pick(ns, (1024, 1408, 512, 256, 128))
    npb = ns // tn

    def body(a_ref, w_ref, *rest):
        o_ref = rest[-1]
        acc = _dot(_mm_operand(a_ref, rest[0] if gain is not None else None), w_ref[...], NN)
        if res is not None:
            acc = acc + rest[-2][...]
        o_ref[...] = acc.astype(o_ref.dtype)

    in_specs = [
        pl.BlockSpec((tm, k), lambda i, j: (i, 0)),
        pl.BlockSpec((None, k, tn), lambda i, j: (j // npb, 0, j % npb)),
    ]
    args = [a, w]
    if gain is not None:
        in_specs.append(pl.BlockSpec((1, k), lambda i, j: (0, 0)))
        args.append(gain)
    if res is not None:
        in_specs.append(pl.BlockSpec((tm, tn), lambda i, j: (i, j)))
        args.append(res)
    return pl.pallas_call(
        body,
        name=name,
        grid=(m // tm, s * npb),
        in_specs=in_specs,
        out_specs=pl.BlockSpec((tm, tn), lambda i, j: (i, j)),
        out_shape=jax.ShapeDtypeStruct((m, s * ns), out_dtype),
        compiler_params=_cparams(("parallel", "parallel")),
    )(*args)


def _dy_spec(stacked, tm, tn, npb, row, kk):
    if stacked:
        return pl.BlockSpec((None, tm, tn), lambda *g: (kk(g) // npb, row(g), kk(g) % npb))
    return pl.BlockSpec((tm, tn), lambda *g: (row(g), kk(g)))


def _dep_specs(deps):
    return [pl.BlockSpec(d.shape, lambda *g: (0, 0)) for d in deps]


def _mm_nt(dy, w, stacked=False, out_dtype=F32, name="mm_nt", deps=(), norm_of=None):
    s, k, ns = w.shape
    m = dy.shape[1] if stacked else dy.shape[0]
    tm = min(m, MM_ROWS)
    tko = _pick(k, (1024, 1408, 512, 256))
    tn = _pick(ns, (1024, 1408, 512, 256))
    npb = ns // tn
    nk = s * npb
    fused = norm_of is not None
    assert not fused or tko == k

    def body(dy_ref, w_ref, *rest):
        acc_ref = rest[-1]
        i, kk = pl.program_id(0), pl.program_id(2)

        @pl.when(kk == 0)
        def _():
            acc_ref[...] = jnp.zeros_like(acc_ref)

        acc_ref[...] += _dot(dy_ref[...].astype(BF16), w_ref[...], NT)

        if not fused:
            @pl.when(kk == nk - 1)
            def _():
                rest[-2][...] = acc_ref[...].astype(rest[-2].dtype)
            return
        x_ref, g_ref, dres_ref = rest[:3]
        dx_ref, dg_ref = rest[-3], rest[-2]

        @pl.when(jnp.logical_and(i == 0, kk == 0))
        def _():
            dg_ref[...] = jnp.zeros_like(dg_ref)

        @pl.when(kk == nk - 1)
        def _():
            dxn = acc_ref[...]
            xhat, rstd = _rms_stats(x_ref[...])
            gd = dxn * g_ref[...]
            dx_ref[...] = dres_ref[...] + rstd * (gd - xhat * jnp.mean(gd * xhat, axis=-1, keepdims=True))
            dg_ref[...] += jnp.sum(dxn * xhat, axis=0, keepdims=True)

    row = pl.BlockSpec((tm, tko), lambda i, j, kk: (i, j))
    vec = pl.BlockSpec((1, k), lambda i, j, kk: (0, 0))
    return pl.pallas_call(
        body,
        name=name,
        grid=(m // tm, k // tko, nk),
        in_specs=[
            _dy_spec(stacked, tm, tn, npb, lambda g: g[0], lambda g: g[2]),
            pl.BlockSpec((None, tko, tn), lambda i, j, kk: (kk // npb, j, kk % npb)),
        ] + ([row, vec, row] if fused else []) + _dep_specs(deps),
        out_specs=[row, vec] if fused else row,
        out_shape=[jax.ShapeDtypeStruct((m, k), F32), jax.ShapeDtypeStruct((1, k), F32)] if fused else jax.ShapeDtypeStruct((m, k), out_dtype),
        scratch_shapes=[pltpu.VMEM((tm, tko), F32)],
        compiler_params=_cparams(("arbitrary",) * 3 if fused else ("parallel", "parallel", "arbitrary")),
    )(dy, w, *(norm_of or ()), *deps)


def _mm_tn(a, dy, s, ns, stacked=False, name="mm_tn", deps=(), gain=None):
    m, k = a.shape
    tm = min(m, MM_ROWS)
    tk = _pick(k, (1024, 1408, 512, 256))
    tn = _pick(ns, (1024, 1408, 512, 256, 128))
    npb = ns // tn
    nm = m // tm
    assert gain is None or tk == k

    def body(a_ref, dy_ref, *rest):
        j, mm = pl.program_id(1), pl.program_id(2)
        if gain is None:
            o_ref, acc_ref = rest[-2:]
            lhs = a_ref[...].astype(BF16)
        else:
            o_ref, acc_ref, xn_ref = rest[-3:]

            @pl.when(j == 0)
            def _():
                xn_ref[mm] = _mm_operand(a_ref, rest[0])

            lhs = xn_ref[mm]

        @pl.when(mm == 0)
        def _():
            acc_ref[...] = jnp.zeros_like(acc_ref)

        acc_ref[...] += _dot(lhs, dy_ref[...].astype(BF16), TN)

        @pl.when(mm == nm - 1)
        def _():
            o_ref[...] = acc_ref[...]

    a_rows = (lambda i, j, mm: (mm, i)) if gain is None else (lambda i, j, mm: (jnp.where(j == 0, mm, 0), i))
    return pl.pallas_call(
        body,
        name=name,
        grid=(k // tk, s * npb, nm),
        in_specs=[
            pl.BlockSpec((tm, tk), a_rows),
            _dy_spec(stacked, tm, tn, npb, lambda g: g[2], lambda g: g[1]),
        ] + ([pl.BlockSpec((1, k), lambda i, j, mm: (0, 0))] if gain is not None else []) + _dep_specs(deps),
        out_specs=pl.BlockSpec((None, tk, tn), lambda i, j, mm: (j // npb, i, j % npb)),
        out_shape=jax.ShapeDtypeStruct((s, k, ns), F32),
        scratch_shapes=[pltpu.VMEM((tk, tn), F32)] + ([pltpu.VMEM((nm, tm, tk), BF16)] if gain is not None else []),
        compiler_params=_cparams(("parallel", "arbitrary", "arbitrary") if gain is not None else ("parallel", "parallel", "arbitrary")),
    )(a, dy, *(() if gain is None else (gain,)), *deps)


ROW_TILE = 512


def _loss_head(h, g, target):
    t, d = h.shape
    r = min(t, ROW_TILE)

    def body(h_ref, g_ref, t_ref, dh_ref, dg_ref, loss_ref):
        @pl.when(pl.program_id(0) == 0)
        def _():
            dg_ref[...] = jnp.zeros_like(dg_ref)
            loss_ref[...] = jnp.zeros_like(loss_ref)

        xv = h_ref[...]
        rstd = lax.rsqrt(jnp.mean(xv * xv, axis=-1, keepdims=True) + EPS)
        xhat = xv * rstd
        gv = g_ref[...]
        err = xhat * gv - t_ref[...]
        loss_ref[...] += 0.5 * jnp.sum(jnp.mean(err * err, axis=-1, keepdims=True), axis=0, keepdims=True)
        dy = err * (1.0 / d)
        gd = dy * gv
        dh_ref[...] = rstd * (gd - xhat * jnp.mean(gd * xhat, axis=-1, keepdims=True))
        dg_ref[...] += jnp.sum(dy * xhat, axis=0, keepdims=True)

    return pl.pallas_call(
        body,
        name="loss_head",
        grid=(t // r,),
        in_specs=[
            pl.BlockSpec((r, d), lambda i: (i, 0)),
            pl.BlockSpec((1, d), lambda i: (0, 0)),
            pl.BlockSpec((r, d), lambda i: (i, 0)),
        ],
        out_specs=[
            pl.BlockSpec((r, d), lambda i: (i, 0)),
            pl.BlockSpec((1, d), lambda i: (0, 0)),
            pl.BlockSpec((1, LANES), lambda i: (0, 0)),
        ],
        out_shape=[
            jax.ShapeDtypeStruct((t, d), F32),
            jax.ShapeDtypeStruct((1, d), F32),
            jax.ShapeDtypeStruct((1, LANES), F32),
        ],
        compiler_params=_cparams(("arbitrary",)),
    )(h, g, target)


CONV_ROWS = 256
CONV_COLS = 1408


def _conv_taps(x_ext, n):
    tot = x_ext.shape[0]
    g1 = pltpu.roll(x_ext, 1, 0)[tot - n:]
    g2 = pltpu.roll(x_ext, 2, 0)[tot - n:]
    return g2, g1


def _conv_fwd(up, conv_w, conv_b, name="conv_fwd"):
    t = up.shape[0]
    r = min(t, CONV_ROWS)
    tc = CONV_COLS
    ncb = D_FF // tc
    hb = r // SUBLANES

    def body(g_ref, halo_ref, v_ref, w_ref, b_ref, o_ref, c_ref):
        i = pl.program_id(1)
        g0 = g_ref[...]
        halo = halo_ref[...] * jnp.where(i > 0, 1.0, 0.0)
        g2, g1 = _conv_taps(jnp.concatenate([halo, g0], axis=0), r)
        c = b_ref[...] + w_ref[0:1, :] * g2 + w_ref[1:2, :] * g1 + w_ref[2:3, :] * g0
        c_ref[...] = c
        o_ref[...] = (c * _sigmoid(c) * v_ref[...]).astype(BF16)

    blk = pl.BlockSpec((r, tc), lambda j, i: (i, j))
    return pl.pallas_call(
        body,
        name=name,
        grid=(ncb, t // r),
        in_specs=[
            blk,
            pl.BlockSpec((SUBLANES, tc), lambda j, i: (jnp.maximum(i * hb - 1, 0), j)),
            pl.BlockSpec((r, tc), lambda j, i: (i, ncb + j)),
            pl.BlockSpec((3, tc), lambda j, i: (0, j)),
            pl.BlockSpec((1, tc), lambda j, i: (0, j)),
        ],
        out_specs=[blk, blk],
        out_shape=[jax.ShapeDtypeStruct((t, D_FF), BF16), jax.ShapeDtypeStruct((t, D_FF), F32)],
        compiler_params=_cparams(("parallel", "parallel")),
    )(up, up, up, conv_w, conv_b)


def _conv_bwd(up, conv_w, c, dact, name="conv_bwd"):
    t = up.shape[0]
    r = min(t, CONV_ROWS)
    tc = CONV_COLS
    ncb = D_FF // tc
    nrt = t // r

    def body(g_ref, v_ref, w_ref, c_ref, da_ref, dup_ref, dw_ref, db_ref, nxt_ref):
        ii = pl.program_id(1)

        @pl.when(ii == 0)
        def _():
            nxt_ref[...] = jnp.zeros_like(nxt_ref)
            dw_ref[...] = jnp.zeros_like(dw_ref)
            db_ref[...] = jnp.zeros_like(db_ref)

        g0 = g_ref[...]
        w0, w1, w2 = w_ref[0:1, :], w_ref[1:2, :], w_ref[2:3, :]
        c = c_ref[...]
        sg = _sigmoid(c)
        da = da_ref[...]
        dup_ref[1] = (da * (c * sg)).astype(BF16)
        dc = da * v_ref[...] * (sg * (1.0 + c * (1.0 - sg)))
        ext = jnp.concatenate([dc, nxt_ref[...]], axis=0)
        tot = r + SUBLANES
        d1 = pltpu.roll(ext, tot - 1, 0)[:r]
        d2 = pltpu.roll(ext, tot - 2, 0)[:r]
        nxt_ref[...] = dc[:SUBLANES]
        dup_ref[0] = (w2 * dc + w1 * d1 + w0 * d2).astype(BF16)
        db_ref[...] += jnp.sum(dc, axis=0, keepdims=True)
        dw_ref[0:1, :] += jnp.sum(d2 * g0, axis=0, keepdims=True)
        dw_ref[1:2, :] += jnp.sum(d1 * g0, axis=0, keepdims=True)
        dw_ref[2:3, :] += jnp.sum(dc * g0, axis=0, keepdims=True)

    rev = lambda ii: nrt - 1 - ii
    dup, dw, db = pl.pallas_call(
        body,
        name=name,
        grid=(ncb, nrt),
        in_specs=[
            pl.BlockSpec((r, tc), lambda j, ii: (rev(ii), j)),
            pl.BlockSpec((r, tc), lambda j, ii: (rev(ii), ncb + j)),
            pl.BlockSpec((3, tc), lambda j, ii: (0, j)),
            pl.BlockSpec((r, tc), lambda j, ii: (rev(ii), j)),
            pl.BlockSpec((r, tc), lambda j, ii: (rev(ii), j)),
        ],
        out_specs=[
            pl.BlockSpec((2, None, r, tc), lambda j, ii: (0, j, rev(ii), 0)),
            pl.BlockSpec((3, tc), lambda j, ii: (0, j)),
            pl.BlockSpec((1, tc), lambda j, ii: (0, j)),
        ],
        out_shape=[
            jax.ShapeDtypeStruct((2, ncb, t, tc), BF16),
            jax.ShapeDtypeStruct((3, D_FF), F32),
            jax.ShapeDtypeStruct((1, D_FF), F32),
        ],
        scratch_shapes=[pltpu.VMEM((SUBLANES, tc), F32)],
        compiler_params=_cparams(("parallel", "arbitrary")),
    )(up, up, conv_w, c, dact)
    return dup.reshape(2 * ncb, t, tc), dw, db


def _split3(x):
    x1 = x.astype(BF16)
    r1 = x - x1.astype(F32)
    x2 = r1.astype(BF16)
    x3 = (r1 - x2.astype(F32)).astype(BF16)
    return x1, x2, x3


def _tri_dot(tri, x, dims):
    x1, x2, x3 = _split3(x)
    return _dot(tri, x1, dims) + _dot(tri, x2, dims) + _dot(tri, x3, dims)


def _lower_bound(logits_ref):
    return _sigmoid(logits_ref[0:1, :] - logits_ref[1:2, :])


def _hg_gates(qr, fr, lb):
    q = qr * _sigmoid(qr) * (HG_DK ** -0.5)
    sf = _sigmoid(fr)
    fg = lb + (1.0 - lb) * sf
    return q, sf, fg


def _hg_chunk_terms(q, fg, tril_b, low_half):
    g = jnp.log(fg)
    k = 1.0 - fg
    cum = _tri_dot(tril_b, g, NN)
    c_last = jnp.sum(g, axis=0, keepdims=True)
    c_mid = jnp.sum(jnp.where(low_half, g, 0.0), axis=0, keepdims=True)
    e_q = jnp.exp(cum - c_mid)
    e_k = jnp.exp(c_mid - cum)
    e_0 = jnp.exp(cum)
    e_l = jnp.exp(c_last - cum)
    return k, e_q, e_k, e_0, e_l, jnp.exp(c_last)


HG_BLOCK = 512


def _hg_proj_specs(rb, row):
    return [pl.BlockSpec((rb, D_MODEL), functools.partial(lambda i, k: (row(i), k), k=k)) for k in range(4)]


def _hg_consts(c):
    tril = lax.broadcasted_iota(jnp.int32, (c, c), 0) >= lax.broadcasted_iota(jnp.int32, (c, c), 1)
    low_half = lax.broadcasted_iota(jnp.int32, (c, D_MODEL), 0) < c // 2
    return tril, tril.astype(BF16), low_half


def _hgrn_fwd(proj, lb, wn):
    t = proj.shape[0]
    c = HG_CHUNK
    rb = min(t, HG_BLOCK)
    cpb = rb // c

    def body(q_ref, f_ref, i_ref, g_ref, lb_ref, wn_ref, o_ref, y_ref, st_ref, s_scr):
        @pl.when(pl.program_id(0) == 0)
        def _():
            s_scr[...] = jnp.zeros_like(s_scr)

        lb_all = _lower_bound(lb_ref)
        wnv = wn_ref[...]
        tril, tril_b, low_half = _hg_consts(c)

        def chunk(n, carry):
            rows = pl.ds(pl.multiple_of(n * c, c), c)
            q, _, fg = _hg_gates(q_ref[rows, :], f_ref[rows, :], lb_all)
            k, e_q, e_k, e_0, e_l, e_last = _hg_chunk_terms(q, fg, tril_b, low_half)
            qi, ki, q0, kl = (q * e_q).astype(BF16), (k * e_k).astype(BF16), (q * e_0).astype(BF16), (k * e_l).astype(BF16)
            v = i_ref[rows, :].astype(BF16)
            gr = g_ref[rows, :]
            gate = gr * _sigmoid(gr)
            for h in range(HG_HEADS):
                cols = slice(h * HG_DK, (h + 1) * HG_DK)
                st = s_scr[h]
                st_ref[h, n] = st
                a = jnp.where(tril, _dot(qi[:, cols], ki[:, cols], NT), 0.0)
                o = _dot(q0[:, cols], st.astype(BF16), NT) + _dot(a.astype(BF16), v[:, cols], NN)
                s_scr[h] = st * e_last[:, cols] + _dot(v[:, cols], kl[:, cols], TN)
                o_ref[rows, cols] = o
                rstd = lax.rsqrt(jnp.mean(o * o, axis=-1, keepdims=True) + EPS)
                y_ref[rows, cols] = (o * rstd * wnv * gate[:, cols]).astype(BF16)
            return carry

        lax.fori_loop(0, cpb, chunk, 0)

    blk = pl.BlockSpec((rb, D_MODEL), lambda i: (i, 0))
    return pl.pallas_call(
        body,
        name="hgrn_fwd",
        grid=(t // rb,),
        in_specs=_hg_proj_specs(rb, lambda i: i) + [pl.BlockSpec((2, D_MODEL), lambda i: (0, 0)), pl.BlockSpec((1, HG_DK), lambda i: (0, 0))],
        out_specs=[blk, blk, pl.BlockSpec((HG_HEADS, cpb, HG_DK, HG_DK), lambda i: (0, i, 0, 0))],
        out_shape=[
            jax.ShapeDtypeStruct((t, D_MODEL), F32),
            jax.ShapeDtypeStruct((t, D_MODEL), BF16),
            jax.ShapeDtypeStruct((HG_HEADS, t // c, HG_DK, HG_DK), F32),
        ],
        scratch_shapes=[pltpu.VMEM((HG_HEADS, HG_DK, HG_DK), F32)],
        compiler_params=_cparams(("arbitrary",)),
    )(proj, proj, proj, proj, lb, wn)


def _hgrn_bwd(proj, lb, wn, o, states, dy):
    t = proj.shape[0]
    c = HG_CHUNK
    rb = min(t, HG_BLOCK)
    cpb = rb // c
    nb = t // rb

    def body(q_ref, f_ref, i_ref, g_ref, lb_ref, wn_ref, o_ref, st_ref, dy_ref, dp_ref, dl_ref, dwn_ref, ds_scr, dlb_scr):
        step = pl.program_id(0)

        @pl.when(step == 0)
        def _():
            dwn_ref[...] = jnp.zeros_like(dwn_ref)
            ds_scr[...] = jnp.zeros_like(ds_scr)
            dlb_scr[...] = jnp.zeros_like(dlb_scr)

        lb_all = _lower_bound(lb_ref)
        wnv = wn_ref[...]
        tril, tril_b, low_half = _hg_consts(c)

        def chunk(nn, carry):
            n = cpb - 1 - nn
            rows = pl.ds(pl.multiple_of(n * c, c), c)
            qr = q_ref[rows, :]
            gr = g_ref[rows, :]
            q, sf, fg = _hg_gates(qr, f_ref[rows, :], lb_all)
            k, e_q, e_k, e_0, e_l, e_last = _hg_chunk_terms(q, fg, tril_b, low_half)
            qi, qi_lo, _ = _split3(q * e_q)
            ki, ki_lo, _ = _split3(k * e_k)
            q0 = (q * e_0).astype(BF16)
            kl = (k * e_l).astype(BF16)
            v = i_ref[rows, :].astype(BF16)
            sg = _sigmoid(gr)
            silu_g = gr * sg
            dsilu_g = sg * (1.0 + gr * (1.0 - sg))
            dqs, dks, d_lasts = [], [], []
            for h in range(HG_HEADS):
                cols = slice(h * HG_DK, (h + 1) * HG_DK)
                ov = o_ref[rows, cols]
                dyv = dy_ref[rows, cols].astype(F32)
                rstd = lax.rsqrt(jnp.mean(ov * ov, axis=-1, keepdims=True) + EPS)
                ohat = ov * rstd
                dp_ref[3, rows, cols] = (dyv * (ohat * wnv) * dsilu_g[:, cols]).astype(BF16)
                don = dyv * silu_g[:, cols]
                dwn_ref[...] += jnp.sum(don * ohat, axis=0, keepdims=True)
                gd = don * wnv
                do_b = (rstd * (gd - ohat * jnp.mean(gd * ohat, axis=-1, keepdims=True))).astype(BF16)
                st = st_ref[h, n]
                ds = ds_scr[h]
                ds_b = ds.astype(BF16)
                vh, kh = v[:, cols], k[:, cols]
                a_b = jnp.where(tril, _dot(qi[:, cols], ki[:, cols], NT), 0.0).astype(BF16)
                da_b = jnp.where(tril, _dot(do_b, vh, NT), 0.0).astype(BF16)
                dqs.append(_dot(do_b, st.astype(BF16), NN) * e_0[:, cols]
                           + (_dot(da_b, ki[:, cols], NN) + _dot(da_b, ki_lo[:, cols], NN)) * e_q[:, cols])
                dk_state = _dot(vh, ds_b, NN) * e_l[:, cols]
                dks.append((_dot(da_b, qi[:, cols], TN) + _dot(da_b, qi_lo[:, cols], TN)) * e_k[:, cols] + dk_state)
                dp_ref[2, rows, cols] = (_dot(a_b, do_b, TN) + _dot(kl[:, cols], ds_b, NT)).astype(BF16)
                ds_scr[h] = ds * e_last[:, cols] + _dot(do_b, q0[:, cols], TN)
                d_lasts.append(jnp.sum(dk_state * kh, axis=0, keepdims=True) + jnp.sum(ds * st, axis=0, keepdims=True) * e_last[:, cols])
            dq = jnp.concatenate(dqs, axis=1)
            dk = jnp.concatenate(dks, axis=1)
            dlogf = _tri_dot(tril_b, q * dq - k * dk, TN) + jnp.concatenate(d_lasts, axis=1)
            dfg = dlogf / fg - dk
            dlb_scr[...] += jnp.sum(dfg * (1.0 - sf), axis=0, keepdims=True)
            sq = _sigmoid(qr)
            dp_ref[0, rows, :] = (dq * (HG_DK ** -0.5) * (sq * (1.0 + qr * (1.0 - sq)))).astype(BF16)
            dp_ref[1, rows, :] = (dfg * (1.0 - lb_all) * sf * (1.0 - sf)).astype(BF16)
            return carry

        lax.fori_loop(0, cpb, chunk, 0)

        @pl.when(step == nb - 1)
        def _():
            d0 = dlb_scr[...] * lb_all * (1.0 - lb_all)
            dl_ref[0:1, :] = d0
            dl_ref[1:2, :] = -d0

    rev = lambda i: nb - 1 - i
    blk = pl.BlockSpec((rb, D_MODEL), lambda i: (rev(i), 0))
    return pl.pallas_call(
        body,
        name="hgrn_bwd",
        grid=(nb,),
        in_specs=_hg_proj_specs(rb, rev)
        + [pl.BlockSpec((2, D_MODEL), lambda i: (0, 0)), pl.BlockSpec((1, HG_DK), lambda i: (0, 0)), blk,
           pl.BlockSpec((HG_HEADS, cpb, HG_DK, HG_DK), lambda i: (0, rev(i), 0, 0)), blk],
        out_specs=[
            pl.BlockSpec((4, rb, D_MODEL), lambda i: (0, rev(i), 0)),
            pl.BlockSpec((2, D_MODEL), lambda i: (0, 0)),
            pl.BlockSpec((1, HG_DK), lambda i: (0, 0)),
        ],
        out_shape=[
            jax.ShapeDtypeStruct((4, t, D_MODEL), BF16),
            jax.ShapeDtypeStruct((2, D_MODEL), F32),
            jax.ShapeDtypeStruct((1, HG_DK), F32),
        ],
        scratch_shapes=[pltpu.VMEM((HG_HEADS, HG_DK, HG_DK), F32), pltpu.VMEM((1, D_MODEL), F32)],
        compiler_params=_cparams(("arbitrary",)),
    )(proj, proj, proj, proj, lb, wn, o, states, dy)


ATT_STACK = 8


def _att_stack(q_ref, sink_ref, first, lo, bias_p, bias_c, extra_ref=None):
    qs, bps, bcs, sinks, extras = [], [], [], None, []
    rows = lax.broadcasted_iota(jnp.int32, (ATT_STACK * WINDOW, 1), 0)
    for i in range(ATT_STACK):
        hq = first + i
        cols = slice((hq // 2) * LANES, (hq // 2 + 1) * LANES)
        sel = lo if hq % 2 == 0 else jnp.logical_not(lo)
        qp = q_ref[:, cols] * (ATT_HD ** -0.5)
        qs.append(jnp.where(sel, qp, jnp.zeros_like(qp)))
        bps.append(ALIBI_SLOPES[hq] * bias_p)
        bcs.append(ALIBI_SLOPES[hq] * bias_c)
        sinks = sink_ref[hq] if sinks is None else jnp.where(rows < i * WINDOW, sinks, sink_ref[hq])
        if extra_ref is not None:
            ep = extra_ref[:, cols]
            extras.append(jnp.where(sel, ep, jnp.zeros_like(ep)))
    cat = lambda parts: jnp.concatenate(parts, axis=0)
    return cat(qs), cat(bps), cat(bcs), sinks, (cat(extras) if extras else None)


def _att_rows(i):
    return slice(i * WINDOW, (i + 1) * WINDOW)


def _att_bias(n):
    tq = lax.broadcasted_iota(jnp.int32, (WINDOW, WINDOW), 0)
    sk = lax.broadcasted_iota(jnp.int32, (WINDOW, WINDOW), 1)
    valid_c = sk <= tq
    valid_p = (sk - tq) > jnp.where(n > 0, 0, WINDOW)
    dist_c = (tq - sk).astype(F32)
    return jnp.where(valid_p, -dist_c - float(WINDOW), NEG), jnp.where(valid_c, -dist_c, NEG)


def _att_halves(x, lo, kh):
    r = pltpu.roll(x, ATT_HD, 1)
    zero = jnp.zeros_like(x)
    if kh == 0:
        return jnp.where(lo, x, r), jnp.where(lo, x, zero), jnp.where(lo, zero, r)
    return jnp.where(lo, r, x), jnp.where(lo, r, zero), jnp.where(lo, zero, x)


def _att_probs(qm, k2p, k2c, bias_p, bias_c, sink):
    sp = _dot(qm, k2p, NT) + bias_p
    sc = _dot(qm, k2c, NT) + bias_c
    m = jnp.maximum(jnp.maximum(jnp.max(sp, axis=-1, keepdims=True), jnp.max(sc, axis=-1, keepdims=True)), sink)
    ep = jnp.exp(sp - m)
    ec = jnp.exp(sc - m)
    es = jnp.exp(sink - m)
    inv = 1.0 / (jnp.sum(ep, axis=-1, keepdims=True) + jnp.sum(ec, axis=-1, keepdims=True) + es)
    return ep * inv, ec * inv, es * inv


def _attn_fwd(q, kv, sinks):
    t = q.shape[0]
    nb = t // WINDOW

    def body(sink_ref, q_ref, kvp_ref, kvc_ref, o_ref):
        n = pl.program_id(0)
        bias_p, bias_c = _att_bias(n)
        lo = lax.broadcasted_iota(jnp.int32, (WINDOW, LANES), 1) < ATT_HD
        for kh in range(ATT_KVH):
            k2p, _, _ = _att_halves(kvp_ref[:, 0:LANES], lo, kh)
            k2c, _, _ = _att_halves(kvc_ref[:, 0:LANES], lo, kh)
            _, vlo_p, vhi_p = _att_halves(kvp_ref[:, LANES:2 * LANES], lo, kh)
            _, vlo_c, vhi_c = _att_halves(kvc_ref[:, LANES:2 * LANES], lo, kh)
            for first in range(kh * ATT_GROUP, (kh + 1) * ATT_GROUP, ATT_STACK):
                qs, bp, bc, sinks, _ = _att_stack(q_ref, sink_ref, first, lo, bias_p, bias_c)
                pp, pc, _ = _att_probs(qs, k2p, k2c, bp, bc, sinks)
                pp, pc = pp.astype(BF16), pc.astype(BF16)
                for i in range(0, ATT_STACK, 2):
                    even, odd = _att_rows(i), _att_rows(i + 1)
                    out = (_dot(pp[even], vlo_p, NN) + _dot(pc[even], vlo_c, NN)
                           + _dot(pp[odd], vhi_p, NN) + _dot(pc[odd], vhi_c, NN))
                    j = (first + i) // 2
                    o_ref[:, j * LANES:(j + 1) * LANES] = out.astype(BF16)

    return pl.pallas_call(
        body,
        name="attn_fwd",
        grid=(nb,),
        in_specs=[
            pl.BlockSpec(memory_space=pltpu.SMEM),
            pl.BlockSpec((WINDOW, D_MODEL), lambda n: (n, 0)),
            pl.BlockSpec((WINDOW, 2 * LANES), lambda n: (jnp.maximum(n - 1, 0), 0)),
            pl.BlockSpec((WINDOW, 2 * LANES), lambda n: (n, 0)),
        ],
        out_specs=pl.BlockSpec((WINDOW, D_MODEL), lambda n: (n, 0)),
        out_shape=jax.ShapeDtypeStruct((t, D_MODEL), BF16),
        compiler_params=_cparams(("parallel",)),
    )(sinks, q, kv, kv)


def _attn_bwd(q, kv, sinks, dout):
    t = q.shape[0]
    nb = t // WINDOW

    def body(sink_ref, q_ref, kvp_ref, kvc_ref, do_ref, dq_ref, dkv_ref, dsink_ref, carry_ref):
        n = pl.program_id(0)

        @pl.when(n == 0)
        def _():
            carry_ref[...] = jnp.zeros_like(carry_ref)
            dsink_ref[...] = jnp.zeros_like(dsink_ref)

        @pl.when(n == nb)
        def _():
            dkv_ref[...] = carry_ref[...].astype(BF16)

        @pl.when(n < nb)
        def _():
            bias_p, bias_c = _att_bias(n)
            lo = lax.broadcasted_iota(jnp.int32, (WINDOW, LANES), 1) < ATT_HD
            lane1 = lax.broadcasted_iota(jnp.int32, (1, LANES), 1)
            dsink = jnp.zeros((1, LANES), F32)
            halves = []
            for kh in range(ATT_KVH):
                k2p, klo_p, khi_p = _att_halves(kvp_ref[:, 0:LANES], lo, kh)
                k2c, klo_c, khi_c = _att_halves(kvc_ref[:, 0:LANES], lo, kh)
                v2p, _, _ = _att_halves(kvp_ref[:, LANES:2 * LANES], lo, kh)
                v2c, _, _ = _att_halves(kvc_ref[:, LANES:2 * LANES], lo, kh)
                acc = [jnp.zeros((WINDOW, LANES), F32) for _ in range(4)]
                for first in range(kh * ATT_GROUP, (kh + 1) * ATT_GROUP, ATT_STACK):
                    qs, bp, bc, sinks, dos = _att_stack(q_ref, sink_ref, first, lo, bias_p, bias_c, do_ref)
                    pp, pc, ps = _att_probs(qs, k2p, k2c, bp, bc, sinks)
                    dpp = _dot(dos, v2p, NT)
                    dpc = _dot(dos, v2c, NT)
                    delta = jnp.sum(pp * dpp, axis=-1, keepdims=True) + jnp.sum(pc * dpc, axis=-1, keepdims=True)
                    dsp = (pp * (dpp - delta)).astype(BF16)
                    dsc = (pc * (dpc - delta)).astype(BF16)
                    sink_term = ps * delta
                    for i in range(ATT_STACK):
                        dsink = dsink + jnp.where(lane1 == first + i, -jnp.sum(sink_term[_att_rows(i)], axis=0, keepdims=True), 0.0)
                    for i in range(0, ATT_STACK, 2):
                        even, odd = _att_rows(i), _att_rows(i + 1)
                        dq_pair = (_dot(dsp[even], klo_p, NN) + _dot(dsc[even], klo_c, NN)
                                   + _dot(dsp[odd], khi_p, NN) + _dot(dsc[odd], khi_c, NN))
                        j = (first + i) // 2
                        dq_ref[:, j * LANES:(j + 1) * LANES] = (dq_pair * (ATT_HD ** -0.5)).astype(BF16)
                    acc[0] = acc[0] + _dot(dsp, qs, TN)
                    acc[1] = acc[1] + _dot(dsc, qs, TN)
                    acc[2] = acc[2] + _dot(pp.astype(BF16), dos, TN)
                    acc[3] = acc[3] + _dot(pc.astype(BF16), dos, TN)
                halves.append([a + pltpu.roll(a, ATT_HD, 1) for a in acc])
            prev = jnp.concatenate(
                [jnp.where(lo, halves[0][0], halves[1][0]), jnp.where(lo, halves[0][2], halves[1][2])], axis=1)
            cur = jnp.concatenate(
                [jnp.where(lo, halves[0][1], halves[1][1]), jnp.where(lo, halves[0][3], halves[1][3])], axis=1)
            dkv_ref[...] = (carry_ref[...] + prev).astype(BF16)
            carry_ref[...] = cur
            dsink_ref[...] += dsink

    blk = lambda n: jnp.minimum(n, nb - 1)
    return pl.pallas_call(
        body,
        name="attn_bwd",
        grid=(nb + 1,),
        in_specs=[
            pl.BlockSpec(memory_space=pltpu.SMEM),
            pl.BlockSpec((WINDOW, D_MODEL), lambda n: (blk(n), 0)),
            pl.BlockSpec((WINDOW, 2 * LANES), lambda n: (jnp.maximum(blk(n) - 1, 0), 0)),
            pl.BlockSpec((WINDOW, 2 * LANES), lambda n: (blk(n), 0)),
            pl.BlockSpec((WINDOW, D_MODEL), lambda n: (blk(n), 0)),
        ],
        out_specs=[
            pl.BlockSpec((WINDOW, D_MODEL), lambda n: (blk(n), 0)),
            pl.BlockSpec((WINDOW, 2 * LANES), lambda n: (jnp.maximum(n - 1, 0), 0)),
            pl.BlockSpec((1, LANES), lambda n: (0, 0)),
        ],
        out_shape=[
            jax.ShapeDtypeStruct((t, D_MODEL), BF16),
            jax.ShapeDtypeStruct((t, 2 * LANES), BF16),
            jax.ShapeDtypeStruct((1, LANES), F32),
        ],
        scratch_shapes=[pltpu.VMEM((WINDOW, 2 * LANES), F32)],
        compiler_params=_cparams(("arbitrary",)),
    )(sinks, q, kv, kv, dout)


def _ffn_fwd(h, norm_g, w_up, conv_w, conv_b, w_down, tag, after_up=lambda up: None):
    up = _mm_nn(h, w_up, gain=norm_g, name=f"ffn{tag}_up")
    after_up(up)
    act, c = _conv_fwd(up, conv_w, conv_b, name=f"ffn{tag}_conv")
    h_out = _mm_nn(act, w_down, res=h, name=f"ffn{tag}_down")
    return h_out, (up, act, c)


def _ffn_bwd(dh, h, norm_g, w_up, conv_w, conv_b, w_down, saved, tag, deps=()):
    up, act, c = saved
    dw_down = _mm_tn(act, dh, 1, D_MODEL, name=f"ffn{tag}_dwdown", deps=deps)
    dact = _mm_nt(dh, w_down, name=f"ffn{tag}_dact", deps=deps)
    dup, dconv_w, dconv_b = _conv_bwd(up, conv_w, c, dact, name=f"ffn{tag}_dconv")
    dw_up = _mm_tn(h, dup, N_CHIPS, CONV_COLS, stacked=True, gain=norm_g, name=f"ffn{tag}_dwup")
    dh_in, dnorm = _mm_nt(dup, w_up, stacked=True, norm_of=(h, norm_g, dh), name=f"ffn{tag}_dxn")
    return dh_in, dict(ffn_w_down=dw_down, ffn_w_up=dw_up, ffn_conv_w=dconv_w, ffn_conv_b=dconv_b, ffn_norm=dnorm)


def _local_step(x, target, w, fetch=lambda w, stage, after: w, hook=lambda point, dh, grads: ()):
    proj = _mm_nn(x, w["hg_w_in"], gain=w["hg_norm"], name="hg_in")
    o, y, states = _hgrn_fwd(proj, w["hg_lb"], w["hg_out_norm"])
    w = fetch(w, "mixer_out", y)
    fetch(w, "layer0_relay", y)
    h_a = _mm_nn(y, w["hg_w_out"], res=x, name="hg_out")
    w = fetch(w, "layer0", h_a)
    h1, ffn0 = _ffn_fwd(h_a, w["ffn_norm"][0], w["ffn_w_up"][0], w["ffn_conv_w"][0], w["ffn_conv_b"][0], w["ffn_w_down"][0], 0,
                        lambda up: fetch(w, "layer1_relay", up))
    w = fetch(w, "layer1", h1)
    kv = _mm_nn(h1, w["w_kv"], gain=w["kv_norm"], out_dtype=BF16, name="kv_proj")
    qa = _mm_nn(h1, w["attn_w_q"], gain=w["attn_norm"], out_dtype=BF16, name="attn_q")
    ao = _attn_fwd(qa, kv, w["attn_sinks"])
    h_b = _mm_nn(ao, w["attn_w_o"], res=h1, name="attn_o")
    h2, ffn1 = _ffn_fwd(h_b, w["ffn_norm"][1], w["ffn_w_up"][1], w["ffn_conv_w"][1], w["ffn_conv_b"][1], w["ffn_w_down"][1], 1)
    dh2, d_final, loss = _loss_head(h2, w["final_norm"], target)

    dh_b, g1 = _ffn_bwd(dh2, h_b, w["ffn_norm"][1], w["ffn_w_up"][1], w["ffn_conv_w"][1], w["ffn_conv_b"][1], w["ffn_w_down"][1], ffn1, 1)
    deps = hook("ffn1", dh_b, g1)
    dw_o = _mm_tn(ao, dh_b, 1, D_MODEL, name="attn_dwo", deps=deps)
    dao = _mm_nt(dh_b, w["attn_w_o"], out_dtype=BF16, name="attn_dao", deps=deps)
    dqa, dkv, dsinks = _attn_bwd(qa, kv, w["attn_sinks"], dao)
    dw_q = _mm_tn(h1, dqa, 1, D_MODEL, gain=w["attn_norm"], name="attn_dwq")
    dh1, d_attn_norm = _mm_nt(dqa, w["attn_w_q"], norm_of=(h1, w["attn_norm"], dh_b), name="attn_dxa")
    dw_kv = _mm_tn(h1, dkv, 1, 2 * LANES, gain=w["kv_norm"], name="kv_dw")
    dh1, d_kv_norm = _mm_nt(dkv, w["w_kv"], norm_of=(h1, w["kv_norm"], dh1), name="kv_dx")
    deps = hook("attn", dh1, dict(attn_w_o=dw_o, attn_w_q=dw_q, w_kv=dw_kv))
    dh_a, g0 = _ffn_bwd(dh1, h_a, w["ffn_norm"][0], w["ffn_w_up"][0], w["ffn_conv_w"][0], w["ffn_conv_b"][0], w["ffn_w_down"][0], ffn0, 0, deps)
    dw_out = _mm_tn(y, dh_a, 1, D_MODEL, name="hg_dwout")
    deps = hook("ffn0", dh_a, dict(g0, hg_w_out=dw_out))
    dy = _mm_nt(dh_a, w["hg_w_out"], out_dtype=BF16, name="hg_dy", deps=deps)
    dproj, dlb, d_out_norm = _hgrn_bwd(proj, w["hg_lb"], w["hg_out_norm"], o, states, dy)
    deps = hook("hgrn", dproj, None)
    dw_in = _mm_tn(x, dproj, N_CHIPS, D_MODEL, stacked=True, gain=w["hg_norm"], name="hg_dwin", deps=deps)
    deps = hook("hg_w", dproj, dict(hg_w_in=dw_in))
    dx, d_hg_norm = _mm_nt(dproj, w["hg_w_in"], stacked=True, norm_of=(x, w["hg_norm"], dh_a), name="hg_dxn", deps=deps)

    grads = dict(
        hg_norm=d_hg_norm, hg_w_in=dw_in, hg_lb=dlb, hg_out_norm=d_out_norm, hg_w_out=dw_out,
        kv_norm=d_kv_norm, w_kv=dw_kv, attn_norm=d_attn_norm, attn_w_q=dw_q, attn_sinks=dsinks, attn_w_o=dw_o,
        final_norm=d_final,
    )
    for name in g0:
        grads[name] = [g0[name], g1[name]]
    return loss, dx, grads


ANY = pl.BlockSpec(memory_space=pl.ANY)


def _place():
    x, y, c = lax.axis_index("x"), lax.axis_index("y"), lax.axis_index("c")
    chips = [(1 - x, y), (x, 1 - y), (1 - x, 1 - y)]
    return x, y, c, chips


def _rcopy(src, dst, send_sem, recv_sem, to):
    return pltpu.make_async_remote_copy(src_ref=src, dst_ref=dst, send_sem=send_sem, recv_sem=recv_sem, device_id=to, device_id_type=MESH)


HBM = pl.BlockSpec(memory_space=pltpu.HBM)
SEM = pl.BlockSpec(memory_space=pltpu.SEMAPHORE)
EFFECT = pltpu.SideEffectType.DATAFLOW_SIDE_EFFECTING


def _in_hbm(a):
    return pltpu.with_memory_space_constraint(a, pltpu.HBM)


def _place_shard(shard, place, dtype, name, deps=(), layer=None):
    r, cols = shard.shape[-2:]
    tr = _pick(r, ELEM_ROWS)
    src = pl.BlockSpec((tr, cols), lambda i, place_ref: (i, 0)) if layer is None else pl.BlockSpec((None, tr, cols), lambda i, place_ref: (layer, i, 0))

    def body(place_ref, s_ref, *rest):
        o_ref = rest[-1]
        o_ref[...] = s_ref[...].astype(o_ref.dtype)

    return pl.pallas_call(
        body,
        name=name,
        grid_spec=pltpu.PrefetchScalarGridSpec(
            num_scalar_prefetch=1,
            grid=(r // tr,),
            in_specs=[src] + _dep_specs(deps),
            out_specs=pl.BlockSpec((None, tr, cols), lambda i, place_ref: (place_ref[0], i, 0)),
        ),
        out_shape=jax.ShapeDtypeStruct((N_CHIPS, r, cols), dtype),
        compiler_params=_cparams(("parallel",)),
    )(place, shard, *deps)


def _start_copies(name, bufs, n_sem, copies):
    n = len(bufs)

    def body(*refs):
        for cp in copies(refs[:n], refs[n], refs[n + 1]):
            cp.start()
        refs[-1][...] = jnp.zeros_like(refs[-1])

    outs = pl.pallas_call(
        body,
        name=name,
        in_specs=[HBM] * n,
        out_specs=[SEM, SEM] + [HBM] * n + [pl.BlockSpec(memory_space=pltpu.VMEM)],
        out_shape=[pltpu.SemaphoreType.DMA((n_sem,)), pltpu.SemaphoreType.DMA((n_sem,))] + [pltpu.HBM(b.shape, b.dtype) for b in bufs]
        + [jax.ShapeDtypeStruct((SUBLANES, LANES), F32)],
        input_output_aliases={i: 2 + i for i in range(n)},
        compiler_params=pltpu.CompilerParams(has_side_effects=EFFECT),
    )(*[_in_hbm(b) for b in bufs])
    return outs[0], outs[1], list(outs[2:-1]), outs[-1]


def _wait_copies(name, bufs, send_sems, recv_sems, after, copies):
    n = len(bufs)

    def body(*refs):
        for cp in copies(refs[:n], refs[n], refs[n + 1]):
            cp.wait_send()
            cp.wait_recv()

    return pl.pallas_call(
        body,
        name=name,
        in_specs=[HBM] * n + [SEM, SEM, ANY],
        out_specs=[HBM] * n,
        out_shape=[pltpu.HBM(b.shape, b.dtype) for b in bufs],
        input_output_aliases={i: i for i in range(n)},
        compiler_params=pltpu.CompilerParams(has_side_effects=EFFECT),
    )(*bufs, send_sems, recv_sems, after)


def _relay_copies(name, bufs, send_sems, recv_sems, after, landed, n_sem, onward):
    n = len(bufs)

    def body(*refs):
        for cp in landed(refs[:n], refs[n], refs[n + 1]):
            cp.wait_send()
            cp.wait_recv()
        for cp in onward(refs[:n], refs[n + 3], refs[n + 4]):
            cp.start()
        refs[-1][...] = jnp.zeros_like(refs[-1])

    outs = pl.pallas_call(
        body,
        name=name,
        in_specs=[HBM] * n + [SEM, SEM, ANY],
        out_specs=[SEM, SEM] + [HBM] * n + [pl.BlockSpec(memory_space=pltpu.VMEM)],
        out_shape=[pltpu.SemaphoreType.DMA((n_sem,)), pltpu.SemaphoreType.DMA((n_sem,))] + [pltpu.HBM(b.shape, b.dtype) for b in bufs]
        + [jax.ShapeDtypeStruct((SUBLANES, LANES), F32)],
        input_output_aliases={i: 2 + i for i in range(n)},
        compiler_params=pltpu.CompilerParams(has_side_effects=EFFECT),
    )(*bufs, send_sems, recv_sems, after)
    return outs[0], outs[1], list(outs[2:-1]), outs[-1]


def _gather_half_copies(first, count, over_ici):
    def copies(refs, send_sems, recv_sems):
        x, y, c, chips = _place()
        out = []
        for i in range(count):
            h = refs[i].shape[1] // 2
            mine = pl.ds(c * h, h)
            for j, (px, py) in enumerate(chips):
                k = 3 * (first + i) + j
                slot = 2 * x + y if over_ici else 2 * px + py
                to = (px, py, c) if over_ici else (x, y, 1 - c)
                out.append(_rcopy(refs[i].at[slot, mine], refs[i].at[slot, mine], send_sems.at[k], recv_sems.at[k], to))
        return out

    return copies


def _gather_copies(first, count):
    def copies(refs, send_sems, recv_sems):
        x, y, c, chips = _place()
        me = 2 * x + y
        out = []
        for i in range(count):
            for j, (px, py) in enumerate(chips):
                k = 3 * (first + i) + j
                out.append(_rcopy(refs[i].at[me], refs[i].at[me], send_sems.at[k], recv_sems.at[k], (px, py, c)))
        return out

    return copies


def _swap_copies(n):
    def copies(refs, send_sems, recv_sems):
        x, y, c, _ = _place()
        out = []
        for i in range(n):
            h = refs[i].shape[1] // 2
            out.append(_rcopy(refs[i].at[:, pl.ds((1 - c) * h, h)], refs[n + i], send_sems.at[i], recv_sems.at[i], (x, y, 1 - c)))
        return out

    return copies


def _partial_copies(n):
    def copies(refs, send_sems, recv_sems):
        x, y, c, chips = _place()
        out = []
        for i in range(n):
            for j, (px, py) in enumerate(chips):
                out.append(_rcopy(refs[i].at[2 * px + py], refs[n + i].at[j], send_sems.at[3 * i + j], recv_sems.at[3 * i + j], (px, py, c)))
        return out

    return copies


def _share_copies(n):
    def copies(refs, send_sems, recv_sems):
        x, y, c, _ = _place()
        return [_rcopy(refs[i].at[c], refs[i].at[c], send_sems.at[i], recv_sems.at[i], (x, y, 1 - c)) for i in range(n)]

    return copies


def _small_layout(groups):
    flat = [a for g in groups for a in g]
    rows = -(-sum(a.shape[0] for a in flat) // SUBLANES) * SUBLANES
    return flat, rows, max(a.shape[1] for a in flat)


def _pack_small(groups, device):
    flat, rows, cols = _small_layout(groups)

    def body(dev_ref, *refs):
        o_ref = refs[-1]
        o_ref[...] = jnp.zeros_like(o_ref)
        r0 = 0
        for a_ref in refs[:-1]:
            r, w = a_ref.shape
            o_ref[r0:r0 + r, 0:w] = a_ref[...]
            r0 += r

    return pl.pallas_call(
        body,
        name="small_pack",
        grid_spec=pltpu.PrefetchScalarGridSpec(
            num_scalar_prefetch=1,
            grid=(1,),
            in_specs=[pl.BlockSpec(a.shape, lambda i, dev_ref: (0, 0)) for a in flat],
            out_specs=pl.BlockSpec((None, rows, cols), lambda i, dev_ref: (dev_ref[0], 0, 0)),
        ),
        out_shape=jax.ShapeDtypeStruct((N_DEV, rows, cols), F32),
        compiler_params=_cparams(("arbitrary",)),
    )(device, *flat)


def _small_copies(refs, send_sems, recv_sems):
    x, y, c, _ = _place()
    me = 4 * x + 2 * y + c
    out = []
    for k in range(1, N_DEV):
        peer = (x ^ (k >> 2), y ^ ((k >> 1) & 1), c ^ (k & 1))
        out.append(_rcopy(refs[0].at[me], refs[0].at[me], send_sems.at[k - 1], recv_sems.at[k - 1], peer))
    return out


def _sum_small(slots, groups, widths):
    out_shapes = [(sum(a.shape[0] for a in g), wd or g[0].shape[1]) for g, wd in zip(groups, widths)]

    def body(s_ref, *refs):
        outs, acc_ref = refs[:-1], refs[-1]
        acc = s_ref[0]
        for d in range(1, N_DEV):
            acc = acc + s_ref[d]
        acc_ref[...] = acc
        r0 = 0
        for o_ref in outs:
            r, w = o_ref.shape
            o_ref[...] = acc_ref[r0:r0 + r, 0:w]
            r0 += r

    vmem = pl.BlockSpec(memory_space=pltpu.VMEM)
    return pl.pallas_call(
        body,
        name="small_sum",
        in_specs=[vmem],
        out_specs=[vmem] * len(groups),
        out_shape=[jax.ShapeDtypeStruct(s, F32) for s in out_shapes],
        scratch_shapes=[pltpu.VMEM(slots.shape[1:], F32)],
        compiler_params=pltpu.CompilerParams(vmem_limit_bytes=VMEM_LIMIT_BYTES),
    )(slots)


def _adamw_small(items):
    n = len(items)

    def body(*refs):
        for i in range(n):
            w_ref, m_ref, v_ref, g_ref = refs[4 * i:4 * i + 4]
            d_ref, nm_ref, nv_ref = refs[4 * n + 3 * i:4 * n + 3 * i + 3]
            d_ref[...], nm_ref[...], nv_ref[...] = _adamw_math(w_ref[...], m_ref[...], v_ref[...], g_ref[...])

    vmem = pl.BlockSpec(memory_space=pltpu.VMEM)
    outs = pl.pallas_call(
        body,
        name="adamw_small",
        in_specs=[vmem] * (4 * n),
        out_specs=[vmem] * (3 * n),
        out_shape=[jax.ShapeDtypeStruct(it[0].shape, F32) for it in items for _ in range(3)],
        compiler_params=pltpu.CompilerParams(vmem_limit_bytes=VMEM_LIMIT_BYTES),
    )(*[a for it in items for a in it])
    return [tuple(outs[3 * i:3 * i + 3]) for i in range(n)]


class _Reduction:
    def __init__(self, tag, grads, place):
        self.tag, self.n, self.place = tag, len(grads), place
        lands = [lax.empty((N_CHIPS, g.shape[1] // 2, g.shape[2]), F32) for g in grads]
        self._start("swap", list(grads) + lands, self.n, _swap_copies(self.n))

    def _start(self, stage, bufs, n_sem, copies):
        *self.flight, self.token = _start_copies(f"rs_{stage}_start_{self.tag}", bufs, n_sem, copies)

    def _landed(self, stage, after, copies):
        send_sems, recv_sems, bufs = self.flight
        return _wait_copies(f"rs_{stage}_wait_{self.tag}", bufs, send_sems, recv_sems, after, copies)

    def to_chips(self, after):
        n = self.n
        bufs = self._landed("swap", after, _swap_copies(n))
        sums = [_add_core_halves(g, o, self.place, name=f"rs_add_core_{self.tag}_{i}") for i, (g, o) in enumerate(zip(bufs[:n], bufs[n:]))]
        self.mine = [f for f, _ in sums]
        parts = [b for _, b in sums]
        lands = [lax.empty((3,) + p.shape[1:], BF16) for p in parts]
        self._start("send", parts + lands, 3 * n, _partial_copies(n))

    def to_core(self, after):
        n = self.n
        bufs = self._landed("send", after, _partial_copies(n))
        halves = [_add_chip_partials(f, o, self.place, name=f"rs_add_chip_{self.tag}_{i}") for i, (f, o) in enumerate(zip(self.mine, bufs[n:]))]
        self._start("share", halves, n, _share_copies(n))

    def finish(self, after):
        return [b.reshape((-1,) + b.shape[2:]) for b in self._landed("share", after, _share_copies(self.n))]


ELEM_ROWS = (256, 176, 128, 64, 32, 16, 8)


def _add_core_halves(grad, got, place, name):
    s, r, cols = grad.shape
    h = r // 2
    tr = _pick(h, ELEM_ROWS)

    def body(place_ref, g_ref, o_ref, f_ref, b_ref):
        acc = g_ref[...] + o_ref[...]
        b_ref[...] = acc.astype(BF16)

        @pl.when(pl.program_id(1) == place_ref[0])
        def _():
            f_ref[...] = acc

    blk = pl.BlockSpec((None, tr, cols), lambda i, k, place_ref: (k, i, 0))
    return pl.pallas_call(
        body,
        name=name,
        grid_spec=pltpu.PrefetchScalarGridSpec(
            num_scalar_prefetch=1,
            grid=(h // tr, s),
            in_specs=[pl.BlockSpec((None, None, tr, cols), lambda i, k, place_ref: (k, place_ref[1], i, 0)), blk],
            out_specs=[pl.BlockSpec((tr, cols), lambda i, k, place_ref: (i, 0)), blk],
        ),
        out_shape=[jax.ShapeDtypeStruct((h, cols), F32), jax.ShapeDtypeStruct((s, h, cols), BF16)],
        compiler_params=_cparams(("parallel", "arbitrary")),
    )(place, grad.reshape(s, 2, h, cols), got)


def _add_chip_partials(mine, got, place, name):
    h, cols = mine.shape
    tr = _pick(h, ELEM_ROWS)

    def body(place_ref, m_ref, g_ref, o_ref):
        acc = m_ref[...]
        for j in range(3):
            acc = acc + g_ref[j].astype(F32)
        o_ref[...] = acc

    return pl.pallas_call(
        body,
        name=name,
        grid_spec=pltpu.PrefetchScalarGridSpec(
            num_scalar_prefetch=1,
            grid=(h // tr,),
            in_specs=[
                pl.BlockSpec((tr, cols), lambda i, place_ref: (i, 0)),
                pl.BlockSpec((3, tr, cols), lambda i, place_ref: (0, i, 0)),
            ],
            out_specs=pl.BlockSpec((None, tr, cols), lambda i, place_ref: (place_ref[1], i, 0)),
        ),
        out_shape=jax.ShapeDtypeStruct((2, h, cols), F32),
        compiler_params=_cparams(("parallel",)),
    )(place, mine, got)


def _adamw_math(w, m, v, g):
    nm = ADAM_B1 * m + (1.0 - ADAM_B1) * g
    nv = ADAM_B2 * v + (1.0 - ADAM_B2) * (g * g)
    m_hat = nm * (1.0 / (1.0 - ADAM_B1 ** ADAM_STEP))
    v_hat = nv * (1.0 / (1.0 - ADAM_B2 ** ADAM_STEP))
    return -ADAM_LR * (m_hat / (jnp.sqrt(v_hat) + ADAM_EPS) + ADAM_WD * w), nm, nv


def _adamw_layer(w, m, v, g, layer, prev, name):
    nl, r, cols = w.shape
    tr = _pick(r, ELEM_ROWS)

    def body(w_ref, m_ref, v_ref, g_ref, *rest):
        go_ref, d_ref, nm_ref, nv_ref = rest[-4:]
        gv = g_ref[...]
        d_ref[...], nm_ref[...], nv_ref[...] = _adamw_math(w_ref[...], m_ref[...], v_ref[...], gv)
        go_ref[...] = gv

    lay = pl.BlockSpec((None, tr, cols), lambda i: (layer, i, 0))
    return pl.pallas_call(
        body,
        name=name,
        grid=(r // tr,),
        in_specs=[lay] * 3 + [pl.BlockSpec((tr, cols), lambda i: (i, 0))] + ([ANY] * 4 if prev else []),
        out_specs=[lay] * 4,
        out_shape=[jax.ShapeDtypeStruct((nl, r, cols), F32)] * 4,
        input_output_aliases={4 + k: k for k in range(4)} if prev else {},
        compiler_params=_cparams(("parallel",)),
    )(w, m, v, g, *(prev or ()))


def _adamw(w, m, v, g, name):
    r, cols = w.shape
    tr = _pick(r, ELEM_ROWS)

    def body(w_ref, m_ref, v_ref, g_ref, d_ref, nm_ref, nv_ref):
        d_ref[...], nm_ref[...], nv_ref[...] = _adamw_math(w_ref[...], m_ref[...], v_ref[...], g_ref[...])

    blk = pl.BlockSpec((tr, cols), lambda i: (i, 0))
    return pl.pallas_call(
        body,
        name=name,
        grid=(r // tr,),
        in_specs=[blk] * 4,
        out_specs=[blk] * 3,
        out_shape=[jax.ShapeDtypeStruct((r, cols), F32)] * 3,
        compiler_params=_cparams(("parallel",)),
    )(w, m, v, g)


SMALL_COLS = 384
SMALL_ROWS = 16


def _pad_rows(flat, rows, cols):
    return jnp.pad(flat, (0, rows * cols - flat.shape[0])).reshape(rows, cols)


def kernel(x, hg_norm, hg_w_in, hg_lb_logits, hg_out_norm, hg_w_out, kv_norm, w_kv, attn_norm, attn_w_q, attn_sinks, attn_w_o, ffn_norm, ffn_w_up, ffn_conv_w, ffn_conv_b, ffn_w_down, final_norm, loss_target, m_hg_norm, m_hg_w_in, m_hg_lb_logits, m_hg_out_norm, m_hg_w_out, m_kv_norm, m_w_kv, m_attn_norm, m_attn_w_q, m_attn_sinks, m_attn_w_o, m_ffn_norm, m_ffn_w_up, m_ffn_conv_w, m_ffn_conv_b, m_ffn_w_down, m_final_norm, v_hg_norm, v_hg_w_in, v_hg_lb_logits, v_hg_out_norm, v_hg_w_out, v_kv_norm, v_w_kv, v_attn_norm, v_attn_w_q, v_attn_sinks, v_attn_w_o, v_ffn_norm, v_ffn_w_up, v_ffn_conv_w, v_ffn_conv_b, v_ffn_w_down, v_final_norm):
    wts = dict(hg_norm=hg_norm, hg_w_in=hg_w_in, hg_lb_logits=hg_lb_logits, hg_out_norm=hg_out_norm, hg_w_out=hg_w_out, kv_norm=kv_norm, w_kv=w_kv, attn_norm=attn_norm, attn_w_q=attn_w_q, attn_sinks=attn_sinks, attn_w_o=attn_w_o, ffn_norm=ffn_norm, ffn_w_up=ffn_w_up, ffn_conv_w=ffn_conv_w, ffn_conv_b=ffn_conv_b, ffn_w_down=ffn_w_down, final_norm=final_norm)
    mom1 = dict(hg_norm=m_hg_norm, hg_w_in=m_hg_w_in, hg_lb_logits=m_hg_lb_logits, hg_out_norm=m_hg_out_norm, hg_w_out=m_hg_w_out, kv_norm=m_kv_norm, w_kv=m_w_kv, attn_norm=m_attn_norm, attn_w_q=m_attn_w_q, attn_sinks=m_attn_sinks, attn_w_o=m_attn_w_o, ffn_norm=m_ffn_norm, ffn_w_up=m_ffn_w_up, ffn_conv_w=m_ffn_conv_w, ffn_conv_b=m_ffn_conv_b, ffn_w_down=m_ffn_w_down, final_norm=m_final_norm)
    mom2 = dict(hg_norm=v_hg_norm, hg_w_in=v_hg_w_in, hg_lb_logits=v_hg_lb_logits, hg_out_norm=v_hg_out_norm, hg_w_out=v_hg_w_out, kv_norm=v_kv_norm, w_kv=v_w_kv, attn_norm=v_attn_norm, attn_w_q=v_attn_w_q, attn_sinks=v_attn_sinks, attn_w_o=v_attn_w_o, ffn_norm=v_ffn_norm, ffn_w_up=v_ffn_w_up, ffn_conv_w=v_ffn_conv_w, ffn_conv_b=v_ffn_conv_b, ffn_w_down=v_ffn_w_down, final_norm=v_final_norm)
    names = list(wts)
    chip = 2 * lax.axis_index("x") + lax.axis_index("y")
    core = lax.axis_index("c")
    fs = D_FF // N_CHIPS
    ds = D_MODEL // N_CHIPS

    place_arr = jnp.stack([chip, core]).astype(jnp.int32)
    small = jnp.concatenate([hg_norm.reshape(-1), hg_lb_logits.reshape(-1), ffn_conv_w.reshape(-1)])
    n_small = small.shape[0]
    shards = [
        ("small", _pad_rows(small, SMALL_ROWS, SMALL_COLS), F32, None), ("hg_w_in", hg_w_in, BF16, 0),
        ("hg_w_out", hg_w_out, BF16, 0), ("ffn_w_up0", ffn_w_up, BF16, 0), ("ffn_w_down0", ffn_w_down, BF16, 0),
        ("w_kv", w_kv, BF16, None), ("attn_w_q", attn_w_q, BF16, 0), ("attn_w_o", attn_w_o, BF16, 0),
        ("ffn_w_up1", ffn_w_up, BF16, 1), ("ffn_w_down1", ffn_w_down, BF16, 1),
    ]
    n_first = 3
    spans = dict(layer0=(0, 2), layer1=(2, 7))

    def first_copies(refs, send_sems, recv_sems):
        return (_gather_copies(0, 1)(refs[:1], send_sems, recv_sems) + _gather_half_copies(1, 1, True)(refs[1:2], send_sems, recv_sems)
                + _gather_copies(2, 1)(refs[2:3], send_sems, recv_sems))

    placed = [_place_shard(s, place_arr, dt, name=f"place_{nm}", layer=ly) for nm, s, dt, ly in shards[:n_first]]
    first = _start_copies("gather_start_first", placed, 3 * n_first, first_copies)
    placed = [_place_shard(s, place_arr, dt, name=f"place_{nm}", deps=(first[3],), layer=ly) for nm, s, dt, ly in shards[n_first:]]
    rest = _start_copies("gather_start_rest", placed, 3 * len(placed), _gather_half_copies(0, len(placed), True))
    relayed = {}

    def fetch(w, stage, after):
        if stage == "first":
            w_in = _relay_copies("gather_first_relay", first[2][1:2], first[0], first[1], after,
                                 _gather_half_copies(1, 1, True), 3, _gather_half_copies(0, 1, False))
            got = _wait_copies("gather_wait_small", first[2][:1], first[0], first[1], w_in[3], _gather_copies(0, 1))
            got += _wait_copies("gather_wait_first", w_in[2], w_in[0], w_in[1], got[0], _gather_half_copies(0, 1, False))
        elif stage == "mixer_out":
            got = _wait_copies("gather_wait_mixer_out", first[2][2:], first[0], first[1], after, _gather_copies(2, 1))
        elif stage.endswith("_relay"):
            lo, hi = spans[stage[:-6]]
            relayed[stage[:-6]] = _relay_copies(
                f"gather_{stage}", rest[2][lo:hi], rest[0], rest[1], after,
                _gather_half_copies(lo, hi - lo, True), 3 * (hi - lo), _gather_half_copies(0, hi - lo, False))
            return w
        else:
            lo, hi = spans[stage]
            send_sems, recv_sems, bufs, _ = relayed[stage]
            got = _wait_copies(f"gather_wait_{stage}", bufs, send_sems, recv_sems, after, _gather_half_copies(0, hi - lo, False))
        w = dict(w)
        if stage == "first":
            g_small = got[0].reshape(N_CHIPS, -1)[:, :n_small]
            conv_w = g_small[:, 3 * ds:].reshape(N_CHIPS, 2, 3, fs).transpose(1, 2, 0, 3).reshape(2, 3, D_FF)
            w.update(
                hg_norm=g_small[:, :ds].reshape(1, D_MODEL),
                hg_lb=g_small[:, ds:3 * ds].reshape(N_CHIPS, 2, ds).transpose(1, 0, 2).reshape(2, D_MODEL),
                ffn_conv_w=[conv_w[0], conv_w[1]], hg_w_in=got[1],
            )
        elif stage == "mixer_out":
            w.update(hg_w_out=got[0].reshape(1, D_MODEL, D_MODEL))
        elif stage == "layer0":
            w.update(ffn_w_up=[got[0], None], ffn_w_down=[got[1].reshape(1, D_FF, D_MODEL), None])
        else:
            w.update(
                w_kv=got[0].reshape(1, D_MODEL, 2 * LANES), attn_w_q=got[1].reshape(1, D_MODEL, D_MODEL),
                attn_w_o=got[2].reshape(1, D_MODEL, D_MODEL), ffn_w_up=[w["ffn_w_up"][0], got[3]],
                ffn_w_down=[w["ffn_w_down"][0], got[4].reshape(1, D_FF, D_MODEL)],
            )
        return w

    whole = dict(
        hg_out_norm=hg_out_norm, kv_norm=kv_norm.reshape(1, D_MODEL), attn_norm=attn_norm, attn_sinks=attn_sinks.reshape(ATT_QH),
        ffn_norm=[ffn_norm[0:1], ffn_norm[1:2]], ffn_conv_b=[ffn_conv_b[0:1], ffn_conv_b[1:2]], final_norm=final_norm.reshape(1, D_MODEL),
    )
    whole = fetch(whole, "first", rest[3])

    red, layer1 = {}, {}

    def by_rows(g, rows):
        return g.reshape(N_CHIPS, rows, g.shape[2])

    def hook(point, dh, grads):
        if point == "ffn1":
            red["ffn1"] = _Reduction("ffn1", [by_rows(grads["ffn_w_down"], fs), grads["ffn_w_up"]], place_arr)
            return (red["ffn1"].token,)
        if point == "attn":
            red["ffn1"].to_chips(dh)
            layer1.update(grads)
            return (red["ffn1"].token,)
        if point == "ffn0":
            group = [by_rows(layer1["attn_w_o"], ds), by_rows(layer1["attn_w_q"], ds), by_rows(layer1["w_kv"], ds),
                     by_rows(grads["ffn_w_down"], fs), grads["ffn_w_up"], by_rows(grads["hg_w_out"], ds)]
            red["mid"] = _Reduction("mid", group, place_arr)
            return (red["mid"].token,)
        if point == "hgrn":
            red["ffn1"].to_core(dh)
            red["mid"].to_chips(dh)
            return (red["ffn1"].token, red["mid"].token)
        red["hg"] = _Reduction("hg", [grads["hg_w_in"]], place_arr)
        return (red["hg"].token,)

    loss, dx, grads = _local_step(x[0], loss_target[0], whole, fetch, hook)

    small_names = ["hg_out_norm", "attn_sinks", "kv_norm", "attn_norm", "ffn_norm", "ffn_conv_b", "final_norm", "hg_norm", "hg_lb_logits", "ffn_conv_w"]
    groups = [[loss]] + [grads[n] if isinstance(grads[n], list) else [grads[n]] for n in small_names[:-2]] + [[grads["hg_lb"]], grads["ffn_conv_w"]]
    widths = [None, None, ATT_QH] + [None] * 8
    packed = _pack_small(groups, jnp.reshape(2 * chip + core, (1,)).astype(jnp.int32))
    small_flight = _start_copies("small_start", [packed], N_DEV - 1, _small_copies)
    red["hg"].to_chips(small_flight[3])

    out_g, out_d, out_m, out_v = {}, {}, {}, {}

    def update(name, g2):
        shape = wts[name].shape
        d2, m2, v2 = _adamw(wts[name].reshape(g2.shape), mom1[name].reshape(g2.shape), mom2[name].reshape(g2.shape), g2, name=f"adamw_{name}")
        out_g[name], out_d[name], out_m[name], out_v[name] = g2.reshape(shape), d2.reshape(shape), m2.reshape(shape), v2.reshape(shape)
        return d2

    def update_layer(name, g2, layer, prev):
        res = _adamw_layer(wts[name], mom1[name], mom2[name], g2, layer, prev, name=f"adamw_{name}{layer}")
        out_g[name], out_d[name], out_m[name], out_v[name] = res
        return res

    g_down1, g_up1 = red["ffn1"].finish(red["hg"].token)
    up1 = update_layer("ffn_w_up", g_up1, 1, None)
    summed = _sum_small(_wait_copies("small_wait", small_flight[2], small_flight[0], small_flight[1], up1[3], _small_copies)[0], groups, widths)
    loss_out = summed[0][0, 0]
    small_grads = dict(zip(small_names, summed[1:]))
    small_grads["hg_norm"] = lax.dynamic_slice(small_grads["hg_norm"], (0, chip * ds), (1, ds))
    small_grads["hg_lb_logits"] = lax.dynamic_slice(small_grads["hg_lb_logits"], (0, chip * ds), (2, ds))
    small_grads["ffn_conv_w"] = lax.dynamic_slice(small_grads["ffn_conv_w"], (0, chip * fs), (2 * 3, fs))
    red["mid"].to_core(up1[1])
    down1 = update_layer("ffn_w_down", g_down1, 1, None)
    g_o, g_q, g_kv, g_down0, g_up0, g_out = red["mid"].finish(down1[1])
    update("attn_w_o", g_o)
    update("attn_w_q", g_q)
    update("w_kv", g_kv)
    update("hg_w_out", g_out)
    update_layer("ffn_w_down", g_down0, 0, down1)
    last = update_layer("ffn_w_up", g_up0, 0, up1)
    red["hg"].to_core(last[1])
    (g_in,) = red["hg"].finish(last[2])
    update("hg_w_in", g_in)

    as_2d = lambda a, n: a.reshape(small_grads[n].shape)
    updated = _adamw_small([(as_2d(wts[n], n), as_2d(mom1[n], n), as_2d(mom2[n], n), small_grads[n]) for n in small_names])
    for n, (d2, m2, v2) in zip(small_names, updated):
        shape = wts[n].shape
        out_g[n], out_d[n], out_m[n], out_v[n] = small_grads[n].reshape(shape), d2.reshape(shape), m2.reshape(shape), v2.reshape(shape)

    grad_x = dx.reshape(x.shape)
    return (loss_out, grad_x, *[out_g[n] for n in names], *[out_d[n] for n in names], *[out_m[n] for n in names], *[out_v[n] for n in names])
```

```python
import functools

import jax
import jax.numpy as jnp
from jax import lax
from jax.experimental import pallas as pl
from jax.experimental.pallas import tpu as pltpu

F32 = jnp.float32
BF16 = jnp.bfloat16
MESH = pl.DeviceIdType.MESH

EPS = 1e-6
D_MODEL = 1024
HG_HEADS = 8
HG_DK = 128
HG_CHUNK = 64
ATT_HD = 64
ATT_QH = 16
ATT_KVH = 2
ATT_GROUP = ATT_QH // ATT_KVH
WINDOW = 128
D_FF = 2816
N_CHIPS = 4
N_DEV = 8
LANES = 128
SUBLANES = 8
VMEM_LIMIT_BYTES = 56 * 1024 * 1024
NEG = -1e30
ALIBI_SLOPES = tuple(2.0 ** (-8.0 * h / ATT_QH) for h in range(1, ATT_QH + 1))

ADAM_LR = 0.001
ADAM_B1 = 0.9
ADAM_B2 = 0.999
ADAM_EPS = 1e-08
ADAM_WD = 0.01
ADAM_STEP = 10


def _cparams(sem=None):
    return pltpu.CompilerParams(dimension_semantics=sem, vmem_limit_bytes=VMEM_LIMIT_BYTES)


def _pick(n, cands):
    for c in cands:
        if n % c == 0:
            return c
    return n


def _sigmoid(x):
    return 0.5 * jnp.tanh(0.5 * x) + 0.5


def _dot(a, b, dims):
    return lax.dot_general(a, b, (dims, ((), ())), preferred_element_type=F32)


NN = ((1,), (0,))
NT = ((1,), (1,))
TN = ((0,), (0,))


MM_ROWS = 1024


def _rms_stats(xv):
    rstd = lax.rsqrt(jnp.mean(xv * xv, axis=-1, keepdims=True) + EPS)
    return xv * rstd, rstd


def _mm_operand(a_ref, gain_ref):
    if gain_ref is None:
        return a_ref[...].astype(BF16)
    return (_rms_stats(a_ref[...])[0] * gain_ref[...]).astype(BF16)


def _mm_nn(a, w, res=None, out_dtype=F32, name="mm_nn", gain=None):
    m, k = a.shape
    s, _, ns = w.shape
    tm = min(m, MM_ROWS)
    tn = _pick(ns, (1024, 1408, 512, 256, 128))
    npb = ns // tn

    def body(a_ref, w_ref, *rest):
        o_ref = rest[-1]
        acc = _dot(_mm_operand(a_ref, rest[0] if gain is not None else None), w_ref[...], NN)
        if res is not None:
            acc = acc + rest[-2][...]
        o_ref[...] = acc.astype(o_ref.dtype)

    in_specs = [
        pl.BlockSpec((tm, k), lambda i, j: (i, 0)),
        pl.BlockSpec((None, k, tn), lambda i, j: (j // npb, 0, j % npb)),
    ]
    args = [a, w]
    if gain is not None:
        in_specs.append(pl.BlockSpec((1, k), lambda i, j: (0, 0)))
        args.append(gain)
    if res is not None:
        in_specs.append(pl.BlockSpec((tm, tn), lambda i, j: (i, j)))
        args.append(res)
    return pl.pallas_call(
        body,
        name=name,
        grid=(m // tm, s * npb),
        in_specs=in_specs,
        out_specs=pl.BlockSpec((tm, tn), lambda i, j: (i, j)),
        out_shape=jax.ShapeDtypeStruct((m, s * ns), out_dtype),
        compiler_params=_cparams(("parallel", "parallel")),
    )(*args)


def _dy_spec(stacked, tm, tn, npb, row, kk):
    if stacked:
        return pl.BlockSpec((None, tm, tn), lambda *g: (kk(g) // npb, row(g), kk(g) % npb))
    return pl.BlockSpec((tm, tn), lambda *g: (row(g), kk(g)))


def _dep_specs(deps):
    return [pl.BlockSpec(d.shape, lambda *g: (0, 0)) for d in deps]


def _mm_nt(dy, w, stacked=False, out_dtype=F32, name="mm_nt", deps=(), norm_of=None, also=None):
    s, k, ns = w.shape
    m = dy.shape[1] if stacked else dy.shape[0]
    tm = min(m, MM_ROWS)
    tko = _pick(k, (1024, 1408, 512, 256))
    tn = _pick(ns, (1024, 1408, 512, 256))
    npb = ns // tn
    nk = s * npb
    fused = norm_of is not None
    assert not fused or tko == k
    assert also is None or fused

    def body(dy_ref, w_ref, *rest):
        acc_ref = rest[-1]
        i, kk = pl.program_id(0), pl.program_id(2)

        @pl.when(kk == 0)
        def _():
            acc_ref[...] = jnp.zeros_like(acc_ref)

        acc_ref[...] += _dot(dy_ref[...].astype(BF16), w_ref[...], NT)

        if not fused:
            @pl.when(kk == nk - 1)
            def _():
                rest[-2][...] = acc_ref[...].astype(rest[-2].dtype)
            return
        x_ref, g_ref, dres_ref = rest[:3]
        n_out = 3 if also is not None else 2
        dx_ref, dg_refs = rest[-1 - n_out], rest[-n_out:-1]

        @pl.when(jnp.logical_and(i == 0, kk == 0))
        def _():
            for dg_ref in dg_refs:
                dg_ref[...] = jnp.zeros_like(dg_ref)

        @pl.when(kk == nk - 1)
        def _():
            dxn = acc_ref[...]
            xhat, rstd = _rms_stats(x_ref[...])
            gd = dxn * g_ref[...]
            dg_refs[0][...] += jnp.sum(dxn * xhat, axis=0, keepdims=True)
            if also is not None:
                dy2_ref, w2_ref, g2_ref = rest[3:6]
                dxn2 = _dot(dy2_ref[...].astype(BF16), w2_ref[...], NT)
                gd = gd + dxn2 * g2_ref[...]
                dg_refs[1][...] += jnp.sum(dxn2 * xhat, axis=0, keepdims=True)
            dx_ref[...] = dres_ref[...] + rstd * (gd - xhat * jnp.mean(gd * xhat, axis=-1, keepdims=True))

    row = pl.BlockSpec((tm, tko), lambda i, j, kk: (i, j))
    vec = pl.BlockSpec((1, k), lambda i, j, kk: (0, 0))
    extra_in, extra_args = [], ()
    if fused:
        extra_in, extra_args = [row, vec, row], tuple(norm_of)
    if also is not None:
        n2 = also[0].shape[1]
        extra_in += [pl.BlockSpec((tm, n2), lambda i, j, kk: (i, 0)), pl.BlockSpec((None, k, n2), lambda i, j, kk: (0, 0, 0)), vec]
        extra_args += tuple(also)
    f32 = lambda shape: jax.ShapeDtypeStruct(shape, F32)
    return pl.pallas_call(
        body,
        name=name,
        grid=(m // tm, k // tko, nk),
        in_specs=[
            _dy_spec(stacked, tm, tn, npb, lambda g: g[0], lambda g: g[2]),
            pl.BlockSpec((None, tko, tn), lambda i, j, kk: (kk // npb, j, kk % npb)),
        ] + extra_in + _dep_specs(deps),
        out_specs=([row, vec] + ([vec] if also is not None else [])) if fused else row,
        out_shape=([f32((m, k)), f32((1, k))] + ([f32((1, k))] if also is not None else [])) if fused else jax.ShapeDtypeStruct((m, k), out_dtype),
        scratch_shapes=[pltpu.VMEM((tm, tko), F32)],
        compiler_params=_cparams(("arbitrary",) * 3 if fused else ("parallel", "parallel", "arbitrary")),
    )(dy, w, *extra_args, *deps)


def _mm_tn(a, dy, s, ns, stacked=False, name="mm_tn", deps=(), gain=None):
    m, k = a.shape
    tm = min(m, MM_ROWS)
    tk = _pick(k, (1024, 1408, 512, 256))
    tn = _pick(ns, (1024, 1408, 512, 256, 128))
    npb = ns // tn
    nm = m // tm
    assert gain is None or tk == k

    def body(a_ref, dy_ref, *rest):
        j, mm = pl.program_id(1), pl.program_id(2)
        if gain is None:
            o_ref, acc_ref = rest[-2:]
            lhs = a_ref[...].astype(BF16)
        else:
            o_ref, acc_ref, xn_ref = rest[-3:]

            @pl.when(j == 0)
            def _():
                xn_ref[mm] = _mm_operand(a_ref, rest[0])

            lhs = xn_ref[mm]

        @pl.when(mm == 0)
        def _():
            acc_ref[...] = jnp.zeros_like(acc_ref)

        acc_ref[...] += _dot(lhs, dy_ref[...].astype(BF16), TN)

        @pl.when(mm == nm - 1)
        def _():
            o_ref[...] = acc_ref[...]

    a_rows = (lambda i, j, mm: (mm, i)) if gain is None else (lambda i, j, mm: (jnp.where(j == 0, mm, 0), i))
    return pl.pallas_call(
        body,
        name=name,
        grid=(k // tk, s * npb, nm),
        in_specs=[
            pl.BlockSpec((tm, tk), a_rows),
            _dy_spec(stacked, tm, tn, npb, lambda g: g[2], lambda g: g[1]),
        ] + ([pl.BlockSpec((1, k), lambda i, j, mm: (0, 0))] if gain is not None else []) + _dep_specs(deps),
        out_specs=pl.BlockSpec((None, tk, tn), lambda i, j, mm: (j // npb, i, j % npb)),
        out_shape=jax.ShapeDtypeStruct((s, k, ns), F32),
        scratch_shapes=[pltpu.VMEM((tk, tn), F32)] + ([pltpu.VMEM((nm, tm, tk), BF16)] if gain is not None else []),
        compiler_params=_cparams(("parallel", "arbitrary", "arbitrary") if gain is not None else ("parallel", "parallel", "arbitrary")),
    )(a, dy, *(() if gain is None else (gain,)), *deps)


ROW_TILE = 512


def _loss_head(h, g, target):
    t, d = h.shape
    r = min(t, ROW_TILE)

    def body(h_ref, g_ref, t_ref, dh_ref, dg_ref, loss_ref):
        @pl.when(pl.program_id(0) == 0)
        def _():
            dg_ref[...] = jnp.zeros_like(dg_ref)
            loss_ref[...] = jnp.zeros_like(loss_ref)

        xv = h_ref[...]
        rstd = lax.rsqrt(jnp.mean(xv * xv, axis=-1, keepdims=True) + EPS)
        xhat = xv * rstd
        gv = g_ref[...]
        err = xhat * gv - t_ref[...]
        loss_ref[...] += 0.5 * jnp.sum(jnp.mean(err * err, axis=-1, keepdims=True), axis=0, keepdims=True)
        dy = err * (1.0 / d)
        gd = dy * gv
        dh_ref[...] = rstd * (gd - xhat * jnp.mean(gd * xhat, axis=-1, keepdims=True))
        dg_ref[...] += jnp.sum(dy * xhat, axis=0, keepdims=True)

    return pl.pallas_call(
        body,
        name="loss_head",
        grid=(t // r,),
        in_specs=[
            pl.BlockSpec((r, d), lambda i: (i, 0)),
            pl.BlockSpec((1, d), lambda i: (0, 0)),
            pl.BlockSpec((r, d), lambda i: (i, 0)),
        ],
        out_specs=[
            pl.BlockSpec((r, d), lambda i: (i, 0)),
            pl.BlockSpec((1, d), lambda i: (0, 0)),
            pl.BlockSpec((1, LANES), lambda i: (0, 0)),
        ],
        out_shape=[
            jax.ShapeDtypeStruct((t, d), F32),
            jax.ShapeDtypeStruct((1, d), F32),
            jax.ShapeDtypeStruct((1, LANES), F32),
        ],
        compiler_params=_cparams(("arbitrary",)),
    )(h, g, target)


CONV_ROWS = 256
CONV_COLS = 1408


def _conv_taps(x_ext, n):
    tot = x_ext.shape[0]
    g1 = pltpu.roll(x_ext, 1, 0)[tot - n:]
    g2 = pltpu.roll(x_ext, 2, 0)[tot - n:]
    return g2, g1


def _conv_fwd(up, conv_w, conv_b, name="conv_fwd"):
    t = up.shape[0]
    r = min(t, CONV_ROWS)
    tc = CONV_COLS
    ncb = D_FF // tc
    hb = r // SUBLANES

    def body(g_ref, halo_ref, v_ref, w_ref, b_ref, o_ref, c_ref):
        i = pl.program_id(1)
        g0 = g_ref[...]
        halo = halo_ref[...] * jnp.where(i > 0, 1.0, 0.0)
        g2, g1 = _conv_taps(jnp.concatenate([halo, g0], axis=0), r)
        c = b_ref[...] + w_ref[0:1, :] * g2 + w_ref[1:2, :] * g1 + w_ref[2:3, :] * g0
        c_ref[...] = c
        o_ref[...] = (c * _sigmoid(c) * v_ref[...]).astype(BF16)

    blk = pl.BlockSpec((r, tc), lambda j, i: (i, j))
    return pl.pallas_call(
        body,
        name=name,
        grid=(ncb, t // r),
        in_specs=[
            blk,
            pl.BlockSpec((SUBLANES, tc), lambda j, i: (jnp.maximum(i * hb - 1, 0), j)),
            pl.BlockSpec((r, tc), lambda j, i: (i, ncb + j)),
            pl.BlockSpec((3, tc), lambda j, i: (0, j)),
            pl.BlockSpec((1, tc), lambda j, i: (0, j)),
        ],
        out_specs=[blk, blk],
        out_shape=[jax.ShapeDtypeStruct((t, D_FF), BF16), jax.ShapeDtypeStruct((t, D_FF), F32)],
        compiler_params=_cparams(("parallel", "parallel")),
    )(up, up, up, conv_w, conv_b)


def _conv_bwd(up, conv_w, c, dact, name="conv_bwd"):
    t = up.shape[0]
    r = min(t, CONV_ROWS)
    tc = CONV_COLS
    ncb = D_FF // tc
    nrt = t // r

    def body(g_ref, v_ref, w_ref, c_ref, da_ref, dup_ref, dw_ref, db_ref, nxt_ref):
        ii = pl.program_id(1)

        @pl.when(ii == 0)
        def _():
            nxt_ref[...] = jnp.zeros_like(nxt_ref)
            dw_ref[...] = jnp.zeros_like(dw_ref)
            db_ref[...] = jnp.zeros_like(db_ref)

        g0 = g_ref[...]
        w0, w1, w2 = w_ref[0:1, :], w_ref[1:2, :], w_ref[2:3, :]
        c = c_ref[...]
        sg = _sigmoid(c)
        da = da_ref[...]
        dup_ref[1] = (da * (c * sg)).astype(BF16)
        dc = da * v_ref[...] * (sg * (1.0 + c * (1.0 - sg)))
        ext = jnp.concatenate([dc, nxt_ref[...]], axis=0)
        tot = r + SUBLANES
        d1 = pltpu.roll(ext, tot - 1, 0)[:r]
        d2 = pltpu.roll(ext, tot - 2, 0)[:r]
        nxt_ref[...] = dc[:SUBLANES]
        dup_ref[0] = (w2 * dc + w1 * d1 + w0 * d2).astype(BF16)
        db_ref[...] += jnp.sum(dc, axis=0, keepdims=True)
        dw_ref[0:1, :] += jnp.sum(d2 * g0, axis=0, keepdims=True)
        dw_ref[1:2, :] += jnp.sum(d1 * g0, axis=0, keepdims=True)
        dw_ref[2:3, :] += jnp.sum(dc * g0, axis=0, keepdims=True)

    rev = lambda ii: nrt - 1 - ii
    dup, dw, db = pl.pallas_call(
        body,
        name=name,
        grid=(ncb, nrt),
        in_specs=[
            pl.BlockSpec((r, tc), lambda j, ii: (rev(ii), j)),
            pl.BlockSpec((r, tc), lambda j, ii: (rev(ii), ncb + j)),
            pl.BlockSpec((3, tc), lambda j, ii: (0, j)),
            pl.BlockSpec((r, tc), lambda j, ii: (rev(ii), j)),
            pl.BlockSpec((r, tc), lambda j, ii: (rev(ii), j)),
        ],
        out_specs=[
            pl.BlockSpec((2, None, r, tc), lambda j, ii: (0, j, rev(ii), 0)),
            pl.BlockSpec((3, tc), lambda j, ii: (0, j)),
            pl.BlockSpec((1, tc), lambda j, ii: (0, j)),
        ],
        out_shape=[
            jax.ShapeDtypeStruct((2, ncb, t, tc), BF16),
            jax.ShapeDtypeStruct((3, D_FF), F32),
            jax.ShapeDtypeStruct((1, D_FF), F32),
        ],
        scratch_shapes=[pltpu.VMEM((SUBLANES, tc), F32)],
        compiler_params=_cparams(("parallel", "arbitrary")),
    )(up, up, conv_w, c, dact)
    return dup.reshape(2 * ncb, t, tc), dw, db


def _split3(x):
    x1 = x.astype(BF16)
    r1 = x - x1.astype(F32)
    x2 = r1.astype(BF16)
    x3 = (r1 - x2.astype(F32)).astype(BF16)
    return x1, x2, x3


def _tri_dot(tri, x, dims):
    x1, x2, x3 = _split3(x)
    return _dot(tri, x1, dims) + _dot(tri, x2, dims) + _dot(tri, x3, dims)


def _lower_bound(logits_ref):
    return _sigmoid(logits_ref[0:1, :] - logits_ref[1:2, :])


def _hg_gates(qr, fr, lb):
    q = qr * _sigmoid(qr) * (HG_DK ** -0.5)
    sf = _sigmoid(fr)
    fg = lb + (1.0 - lb) * sf
    return q, sf, fg


def _hg_chunk_terms(q, fg, tril_b, low_half):
    g = jnp.log(fg)
    k = 1.0 - fg
    cum = _tri_dot(tril_b, g, NN)
    c_last = jnp.sum(g, axis=0, keepdims=True)
    c_mid = jnp.sum(jnp.where(low_half, g, 0.0), axis=0, keepdims=True)
    e_q = jnp.exp(cum - c_mid)
    e_k = jnp.exp(c_mid - cum)
    e_0 = jnp.exp(cum)
    e_l = jnp.exp(c_last - cum)
    return k, e_q, e_k, e_0, e_l, jnp.exp(c_last)


HG_BLOCK = 256


def _hg_proj_specs(rb, row):
    return [pl.BlockSpec((rb, D_MODEL), functools.partial(lambda i, k: (row(i), k), k=k)) for k in range(4)]


def _hg_consts(c):
    tril = lax.broadcasted_iota(jnp.int32, (c, c), 0) >= lax.broadcasted_iota(jnp.int32, (c, c), 1)
    low_half = lax.broadcasted_iota(jnp.int32, (c, D_MODEL), 0) < c // 2
    return tril, tril.astype(BF16), low_half


def _hgrn_fwd(proj, lb, wn):
    t = proj.shape[0]
    c = HG_CHUNK
    rb = min(t, HG_BLOCK)
    cpb = rb // c

    def body(q_ref, f_ref, i_ref, g_ref, lb_ref, wn_ref, o_ref, y_ref, st_ref, s_scr):
        @pl.when(pl.program_id(0) == 0)
        def _():
            s_scr[...] = jnp.zeros_like(s_scr)

        lb_all = _lower_bound(lb_ref)
        wnv = wn_ref[...]
        tril, tril_b, low_half = _hg_consts(c)

        def chunk(n, carry):
            rows = pl.ds(pl.multiple_of(n * c, c), c)
            q, _, fg = _hg_gates(q_ref[rows, :], f_ref[rows, :], lb_all)
            k, e_q, e_k, e_0, e_l, e_last = _hg_chunk_terms(q, fg, tril_b, low_half)
            qi, ki, q0, kl = (q * e_q).astype(BF16), (k * e_k).astype(BF16), (q * e_0).astype(BF16), (k * e_l).astype(BF16)
            v = i_ref[rows, :].astype(BF16)
            gr = g_ref[rows, :]
            gate = gr * _sigmoid(gr)
            for h in range(HG_HEADS):
                cols = slice(h * HG_DK, (h + 1) * HG_DK)
                st = s_scr[h]
                st_ref[h, n] = st
                a = jnp.where(tril, _dot(qi[:, cols], ki[:, cols], NT), 0.0)
                o = _dot(q0[:, cols], st.astype(BF16), NT) + _dot(a.astype(BF16), v[:, cols], NN)
                s_scr[h] = st * e_last[:, cols] + _dot(v[:, cols], kl[:, cols], TN)
                o_ref[rows, cols] = o
                rstd = lax.rsqrt(jnp.mean(o * o, axis=-1, keepdims=True) + EPS)
                y_ref[rows, cols] = (o * rstd * wnv * gate[:, cols]).astype(BF16)
            return carry

        lax.fori_loop(0, cpb, chunk, 0)

    blk = pl.BlockSpec((rb, D_MODEL), lambda i: (i, 0))
    return pl.pallas_call(
        body,
        name="hgrn_fwd",
        grid=(t // rb,),
        in_specs=_hg_proj_specs(rb, lambda i: i) + [pl.BlockSpec((2, D_MODEL), lambda i: (0, 0)), pl.BlockSpec((1, HG_DK), lambda i: (0, 0))],
        out_specs=[blk, blk, pl.BlockSpec((HG_HEADS, cpb, HG_DK, HG_DK), lambda i: (0, i, 0, 0))],
        out_shape=[
            jax.ShapeDtypeStruct((t, D_MODEL), F32),
            jax.ShapeDtypeStruct((t, D_MODEL), BF16),
            jax.ShapeDtypeStruct((HG_HEADS, t // c, HG_DK, HG_DK), F32),
        ],
        scratch_shapes=[pltpu.VMEM((HG_HEADS, HG_DK, HG_DK), F32)],
        compiler_params=_cparams(("arbitrary",)),
    )(proj, proj, proj, proj, lb, wn)


def _hgrn_bwd(proj, lb, wn, o, states, dy):
    t = proj.shape[0]
    c = HG_CHUNK
    rb = min(t, HG_BLOCK)
    cpb = rb // c
    nb = t // rb

    def body(q_ref, f_ref, i_ref, g_ref, lb_ref, wn_ref, o_ref, st_ref, dy_ref, dp_ref, dl_ref, dwn_ref, ds_scr, dlb_scr):
        step = pl.program_id(0)

        @pl.when(step == 0)
        def _():
            dwn_ref[...] = jnp.zeros_like(dwn_ref)
            ds_scr[...] = jnp.zeros_like(ds_scr)
            dlb_scr[...] = jnp.zeros_like(dlb_scr)

        lb_all = _lower_bound(lb_ref)
        wnv = wn_ref[...]
        tril, tril_b, low_half = _hg_consts(c)

        def chunk(nn, carry):
            n = cpb - 1 - nn
            rows = pl.ds(pl.multiple_of(n * c, c), c)
            qr = q_ref[rows, :]
            gr = g_ref[rows, :]
            q, sf, fg = _hg_gates(qr, f_ref[rows, :], lb_all)
            k, e_q, e_k, e_0, e_l, e_last = _hg_chunk_terms(q, fg, tril_b, low_half)
            qi, qi_lo, _ = _split3(q * e_q)
            ki, ki_lo, _ = _split3(k * e_k)
            q0 = (q * e_0).astype(BF16)
            kl = (k * e_l).astype(BF16)
            v = i_ref[rows, :].astype(BF16)
            sg = _sigmoid(gr)
            silu_g = gr * sg
            dsilu_g = sg * (1.0 + gr * (1.0 - sg))
            dqs, dks, d_lasts = [], [], []
            for h in range(HG_HEADS):
                cols = slice(h * HG_DK, (h + 1) * HG_DK)
                ov = o_ref[rows, cols]
                dyv = dy_ref[rows, cols].astype(F32)
                rstd = lax.rsqrt(jnp.mean(ov * ov, axis=-1, keepdims=True) + EPS)
                ohat = ov * rstd
                dp_ref[3, rows, cols] = (dyv * (ohat * wnv) * dsilu_g[:, cols]).astype(BF16)
                don = dyv * silu_g[:, cols]
                dwn_ref[...] += jnp.sum(don * ohat, axis=0, keepdims=True)
                gd = don * wnv
                do_b = (rstd * (gd - ohat * jnp.mean(gd * ohat, axis=-1, keepdims=True))).astype(BF16)
                st = st_ref[h, n]
                ds = ds_scr[h]
                ds_b = ds.astype(BF16)
                vh, kh = v[:, cols], k[:, cols]
                a_b = jnp.where(tril, _dot(qi[:, cols], ki[:, cols], NT), 0.0).astype(BF16)
                da_b = jnp.where(tril, _dot(do_b, vh, NT), 0.0).astype(BF16)
                dqs.append(_dot(do_b, st.astype(BF16), NN) * e_0[:, cols]
                           + (_dot(da_b, ki[:, cols], NN) + _dot(da_b, ki_lo[:, cols], NN)) * e_q[:, cols])
                dk_state = _dot(vh, ds_b, NN) * e_l[:, cols]
                dks.append((_dot(da_b, qi[:, cols], TN) + _dot(da_b, qi_lo[:, cols], TN)) * e_k[:, cols] + dk_state)
                dp_ref[2, rows, cols] = (_dot(a_b, do_b, TN) + _dot(kl[:, cols], ds_b, NT)).astype(BF16)
                ds_scr[h] = ds * e_last[:, cols] + _dot(do_b, q0[:, cols], TN)
                d_lasts.append(jnp.sum(dk_state * kh, axis=0, keepdims=True) + jnp.sum(ds * st, axis=0, keepdims=True) * e_last[:, cols])
            dq = jnp.concatenate(dqs, axis=1)
            dk = jnp.concatenate(dks, axis=1)
            dlogf = _tri_dot(tril_b, q * dq - k * dk, TN) + jnp.concatenate(d_lasts, axis=1)
            dfg = dlogf / fg - dk
            dlb_scr[...] += jnp.sum(dfg * (1.0 - sf), axis=0, keepdims=True)
            sq = _sigmoid(qr)
            dp_ref[0, rows, :] = (dq * (HG_DK ** -0.5) * (sq * (1.0 + qr * (1.0 - sq)))).astype(BF16)
            dp_ref[1, rows, :] = (dfg * (1.0 - lb_all) * sf * (1.0 - sf)).astype(BF16)
            return carry

        lax.fori_loop(0, cpb, chunk, 0)

        @pl.when(step == nb - 1)
        def _():
            d0 = dlb_scr[...] * lb_all * (1.0 - lb_all)
            dl_ref[0:1, :] = d0
            dl_ref[1:2, :] = -d0

    rev = lambda i: nb - 1 - i
    blk = pl.BlockSpec((rb, D_MODEL), lambda i: (rev(i), 0))
    return pl.pallas_call(
        body,
        name="hgrn_bwd",
        grid=(nb,),
        in_specs=_hg_proj_specs(rb, rev)
        + [pl.BlockSpec((2, D_MODEL), lambda i: (0, 0)), pl.BlockSpec((1, HG_DK), lambda i: (0, 0)), blk,
           pl.BlockSpec((HG_HEADS, cpb, HG_DK, HG_DK), lambda i: (0, rev(i), 0, 0)), blk],
        out_specs=[
            pl.BlockSpec((4, rb, D_MODEL), lambda i: (0, rev(i), 0)),
            pl.BlockSpec((2, D_MODEL), lambda i: (0, 0)),
            pl.BlockSpec((1, HG_DK), lambda i: (0, 0)),
        ],
        out_shape=[
            jax.ShapeDtypeStruct((4, t, D_MODEL), BF16),
            jax.ShapeDtypeStruct((2, D_MODEL), F32),
            jax.ShapeDtypeStruct((1, HG_DK), F32),
        ],
        scratch_shapes=[pltpu.VMEM((HG_HEADS, HG_DK, HG_DK), F32), pltpu.VMEM((1, D_MODEL), F32)],
        compiler_params=_cparams(("arbitrary",)),
    )(proj, proj, proj, proj, lb, wn, o, states, dy)


ATT_STACK = 8


def _att_stack(q_ref, sink_ref, first, lo, bias_p, bias_c, extra_ref=None):
    qs, bps, bcs, sinks, extras = [], [], [], None, []
    rows = lax.broadcasted_iota(jnp.int32, (ATT_STACK * WINDOW, 1), 0)
    for i in range(ATT_STACK):
        hq = first + i
        cols = slice((hq // 2) * LANES, (hq // 2 + 1) * LANES)
        sel = lo if hq % 2 == 0 else jnp.logical_not(lo)
        qp = q_ref[:, cols] * (ATT_HD ** -0.5)
        qs.append(jnp.where(sel, qp, jnp.zeros_like(qp)))
        bps.append(ALIBI_SLOPES[hq] * bias_p)
        bcs.append(ALIBI_SLOPES[hq] * bias_c)
        sinks = sink_ref[hq] if sinks is None else jnp.where(rows < i * WINDOW, sinks, sink_ref[hq])
        if extra_ref is not None:
            ep = extra_ref[:, cols]
            extras.append(jnp.where(sel, ep, jnp.zeros_like(ep)))
    cat = lambda parts: jnp.concatenate(parts, axis=0)
    return cat(qs), cat(bps), cat(bcs), sinks, (cat(extras) if extras else None)


def _att_rows(i):
    return slice(i * WINDOW, (i + 1) * WINDOW)


def _att_bias(n):
    tq = lax.broadcasted_iota(jnp.int32, (WINDOW, WINDOW), 0)
    sk = lax.broadcasted_iota(jnp.int32, (WINDOW, WINDOW), 1)
    valid_c = sk <= tq
    valid_p = (sk - tq) > jnp.where(n > 0, 0, WINDOW)
    dist_c = (tq - sk).astype(F32)
    return jnp.where(valid_p, -dist_c - float(WINDOW), NEG), jnp.where(valid_c, -dist_c, NEG)


def _att_halves(x, lo, kh):
    r = pltpu.roll(x, ATT_HD, 1)
    zero = jnp.zeros_like(x)
    if kh == 0:
        return jnp.where(lo, x, r), jnp.where(lo, x, zero), jnp.where(lo, zero, r)
    return jnp.where(lo, r, x), jnp.where(lo, r, zero), jnp.where(lo, zero, x)


def _att_probs(qm, k2p, k2c, bias_p, bias_c, sink):
    sp = _dot(qm, k2p, NT) + bias_p
    sc = _dot(qm, k2c, NT) + bias_c
    m = jnp.maximum(jnp.maximum(jnp.max(sp, axis=-1, keepdims=True), jnp.max(sc, axis=-1, keepdims=True)), sink)
    ep = jnp.exp(sp - m)
    ec = jnp.exp(sc - m)
    es = jnp.exp(sink - m)
    inv = 1.0 / (jnp.sum(ep, axis=-1, keepdims=True) + jnp.sum(ec, axis=-1, keepdims=True) + es)
    return ep * inv, ec * inv, es * inv


def _attn_fwd(q, kv, sinks):
    t = q.shape[0]
    nb = t // WINDOW

    def body(sink_ref, q_ref, kvp_ref, kvc_ref, o_ref):
        n = pl.program_id(0)
        bias_p, bias_c = _att_bias(n)
        lo = lax.broadcasted_iota(jnp.int32, (WINDOW, LANES), 1) < ATT_HD
        for kh in range(ATT_KVH):
            k2p, _, _ = _att_halves(kvp_ref[:, 0:LANES], lo, kh)
            k2c, _, _ = _att_halves(kvc_ref[:, 0:LANES], lo, kh)
            _, vlo_p, vhi_p = _att_halves(kvp_ref[:, LANES:2 * LANES], lo, kh)
            _, vlo_c, vhi_c = _att_halves(kvc_ref[:, LANES:2 * LANES], lo, kh)
            for first in range(kh * ATT_GROUP, (kh + 1) * ATT_GROUP, ATT_STACK):
                qs, bp, bc, sinks, _ = _att_stack(q_ref, sink_ref, first, lo, bias_p, bias_c)
                pp, pc, _ = _att_probs(qs, k2p, k2c, bp, bc, sinks)
                pp, pc = pp.astype(BF16), pc.astype(BF16)
                for i in range(0, ATT_STACK, 2):
                    even, odd = _att_rows(i), _att_rows(i + 1)
                    out = (_dot(pp[even], vlo_p, NN) + _dot(pc[even], vlo_c, NN)
                           + _dot(pp[odd], vhi_p, NN) + _dot(pc[odd], vhi_c, NN))
                    j = (first + i) // 2
                    o_ref[:, j * LANES:(j + 1) * LANES] = out.astype(BF16)

    return pl.pallas_call(
        body,
        name="attn_fwd",
        grid=(nb,),
        in_specs=[
            pl.BlockSpec(memory_space=pltpu.SMEM),
            pl.BlockSpec((WINDOW, D_MODEL), lambda n: (n, 0)),
            pl.BlockSpec((WINDOW, 2 * LANES), lambda n: (jnp.maximum(n - 1, 0), 0)),
            pl.BlockSpec((WINDOW, 2 * LANES), lambda n: (n, 0)),
        ],
        out_specs=pl.BlockSpec((WINDOW, D_MODEL), lambda n: (n, 0)),
        out_shape=jax.ShapeDtypeStruct((t, D_MODEL), BF16),
        compiler_params=_cparams(("parallel",)),
    )(sinks, q, kv, kv)


def _attn_bwd(q, kv, sinks, dout):
    t = q.shape[0]
    nb = t // WINDOW

    def body(sink_ref, q_ref, kvp_ref, kvc_ref, do_ref, dq_ref, dkv_ref, dsink_ref, carry_ref):
        n = pl.program_id(0)

        @pl.when(n == 0)
        def _():
            carry_ref[...] = jnp.zeros_like(carry_ref)
            dsink_ref[...] = jnp.zeros_like(dsink_ref)

        @pl.when(n == nb)
        def _():
            dkv_ref[...] = carry_ref[...].astype(BF16)

        @pl.when(n < nb)
        def _():
            bias_p, bias_c = _att_bias(n)
            lo = lax.broadcasted_iota(jnp.int32, (WINDOW, LANES), 1) < ATT_HD
            lane1 = lax.broadcasted_iota(jnp.int32, (1, LANES), 1)
            dsink = jnp.zeros((1, LANES), F32)
            halves = []
            for kh in range(ATT_KVH):
                k2p, klo_p, khi_p = _att_halves(kvp_ref[:, 0:LANES], lo, kh)
                k2c, klo_c, khi_c = _att_halves(kvc_ref[:, 0:LANES], lo, kh)
                v2p, _, _ = _att_halves(kvp_ref[:, LANES:2 * LANES], lo, kh)
                v2c, _, _ = _att_halves(kvc_ref[:, LANES:2 * LANES], lo, kh)
                acc = [jnp.zeros((WINDOW, LANES), F32) for _ in range(4)]
                for first in range(kh * ATT_GROUP, (kh + 1) * ATT_GROUP, ATT_STACK):
                    qs, bp, bc, sinks, dos = _att_stack(q_ref, sink_ref, first, lo, bias_p, bias_c, do_ref)
                    pp, pc, ps = _att_probs(qs, k2p, k2c, bp, bc, sinks)
                    dpp = _dot(dos, v2p, NT)
                    dpc = _dot(dos, v2c, NT)
                    delta = jnp.sum(pp * dpp, axis=-1, keepdims=True) + jnp.sum(pc * dpc, axis=-1, keepdims=True)
                    dsp = (pp * (dpp - delta)).astype(BF16)
                    dsc = (pc * (dpc - delta)).astype(BF16)
                    sink_term = ps * delta
                    for i in range(ATT_STACK):
                        dsink = dsink + jnp.where(lane1 == first + i, -jnp.sum(sink_term[_att_rows(i)], axis=0, keepdims=True), 0.0)
                    for i in range(0, ATT_STACK, 2):
                        even, odd = _att_rows(i), _att_rows(i + 1)
                        dq_pair = (_dot(dsp[even], klo_p, NN) + _dot(dsc[even], klo_c, NN)
                                   + _dot(dsp[odd], khi_p, NN) + _dot(dsc[odd], khi_c, NN))
                        j = (first + i) // 2
                        dq_ref[:, j * LANES:(j + 1) * LANES] = (dq_pair * (ATT_HD ** -0.5)).astype(BF16)
                    acc[0] = acc[0] + _dot(dsp, qs, TN)
                    acc[1] = acc[1] + _dot(dsc, qs, TN)
                    acc[2] = acc[2] + _dot(pp.astype(BF16), dos, TN)
                    acc[3] = acc[3] + _dot(pc.astype(BF16), dos, TN)
                halves.append([a + pltpu.roll(a, ATT_HD, 1) for a in acc])
            prev = jnp.concatenate(
                [jnp.where(lo, halves[0][0], halves[1][0]), jnp.where(lo, halves[0][2], halves[1][2])], axis=1)
            cur = jnp.concatenate(
                [jnp.where(lo, halves[0][1], halves[1][1]), jnp.where(lo, halves[0][3], halves[1][3])], axis=1)
            dkv_ref[...] = (carry_ref[...] + prev).astype(BF16)
            carry_ref[...] = cur
            dsink_ref[...] += dsink

    blk = lambda n: jnp.minimum(n, nb - 1)
    return pl.pallas_call(
        body,
        name="attn_bwd",
        grid=(nb + 1,),
        in_specs=[
            pl.BlockSpec(memory_space=pltpu.SMEM),
            pl.BlockSpec((WINDOW, D_MODEL), lambda n: (blk(n), 0)),
            pl.BlockSpec((WINDOW, 2 * LANES), lambda n: (jnp.maximum(blk(n) - 1, 0), 0)),
            pl.BlockSpec((WINDOW, 2 * LANES), lambda n: (blk(n), 0)),
            pl.BlockSpec((WINDOW, D_MODEL), lambda n: (blk(n), 0)),
        ],
        out_specs=[
            pl.BlockSpec((WINDOW, D_MODEL), lambda n: (blk(n), 0)),
            pl.BlockSpec((WINDOW, 2 * LANES), lambda n: (jnp.maximum(n - 1, 0), 0)),
            pl.BlockSpec((1, LANES), lambda n: (0, 0)),
        ],
        out_shape=[
            jax.ShapeDtypeStruct((t, D_MODEL), BF16),
            jax.ShapeDtypeStruct((t, 2 * LANES), BF16),
            jax.ShapeDtypeStruct((1, LANES), F32),
        ],
        scratch_shapes=[pltpu.VMEM((WINDOW, 2 * LANES), F32)],
        compiler_params=_cparams(("arbitrary",)),
    )(sinks, q, kv, kv, dout)


def _ffn_fwd(h, norm_g, w_up, conv_w, conv_b, w_down, tag, after_up=lambda up: None):
    up = _mm_nn(h, w_up, gain=norm_g, name=f"ffn{tag}_up")
    after_up(up)
    act, c = _conv_fwd(up, conv_w, conv_b, name=f"ffn{tag}_conv")
    h_out = _mm_nn(act, w_down, res=h, name=f"ffn{tag}_down")
    return h_out, (up, act, c)


def _ffn_bwd(dh, h, norm_g, w_up, conv_w, conv_b, w_down, saved, tag, deps=()):
    up, act, c = saved
    dw_down = _mm_tn(act, dh, 1, D_MODEL, name=f"ffn{tag}_dwdown", deps=deps)
    dact = _mm_nt(dh, w_down, name=f"ffn{tag}_dact", deps=deps)
    dup, dconv_w, dconv_b = _conv_bwd(up, conv_w, c, dact, name=f"ffn{tag}_dconv")
    dw_up = _mm_tn(h, dup, N_CHIPS, CONV_COLS, stacked=True, gain=norm_g, name=f"ffn{tag}_dwup")
    dh_in, dnorm = _mm_nt(dup, w_up, stacked=True, norm_of=(h, norm_g, dh), name=f"ffn{tag}_dxn")
    return dh_in, dict(ffn_w_down=dw_down, ffn_w_up=dw_up, ffn_conv_w=dconv_w, ffn_conv_b=dconv_b, ffn_norm=dnorm)


def _local_step(x, target, w, fetch=lambda w, stage, after: w, hook=lambda point, dh, grads: ()):
    proj = _mm_nn(x, w["hg_w_in"], gain=w["hg_norm"], name="hg_in")
    o, y, states = _hgrn_fwd(proj, w["hg_lb"], w["hg_out_norm"])
    w = fetch(w, "mixer_out", y)
    fetch(w, "layer0_relay", y)
    h_a = _mm_nn(y, w["hg_w_out"], res=x, name="hg_out")
    w = fetch(w, "layer0", h_a)
    h1, ffn0 = _ffn_fwd(h_a, w["ffn_norm"][0], w["ffn_w_up"][0], w["ffn_conv_w"][0], w["ffn_conv_b"][0], w["ffn_w_down"][0], 0,
                        lambda up: fetch(w, "layer1_relay", up))
    w = fetch(w, "layer1", h1)
    kv = _mm_nn(h1, w["w_kv"], gain=w["kv_norm"], out_dtype=BF16, name="kv_proj")
    qa = _mm_nn(h1, w["attn_w_q"], gain=w["attn_norm"], out_dtype=BF16, name="attn_q")
    ao = _attn_fwd(qa, kv, w["attn_sinks"])
    h_b = _mm_nn(ao, w["attn_w_o"], res=h1, name="attn_o")
    h2, ffn1 = _ffn_fwd(h_b, w["ffn_norm"][1], w["ffn_w_up"][1], w["ffn_conv_w"][1], w["ffn_conv_b"][1], w["ffn_w_down"][1], 1)
    dh2, d_final, loss = _loss_head(h2, w["final_norm"], target)

    dh_b, g1 = _ffn_bwd(dh2, h_b, w["ffn_norm"][1], w["ffn_w_up"][1], w["ffn_conv_w"][1], w["ffn_conv_b"][1], w["ffn_w_down"][1], ffn1, 1)
    deps = hook("ffn1", dh_b, g1)
    dw_o = _mm_tn(ao, dh_b, 1, D_MODEL, name="attn_dwo", deps=deps)
    dao = _mm_nt(dh_b, w["attn_w_o"], out_dtype=BF16, name="attn_dao", deps=deps)
    dqa, dkv, dsinks = _attn_bwd(qa, kv, w["attn_sinks"], dao)
    dw_q = _mm_tn(h1, dqa, 1, D_MODEL, gain=w["attn_norm"], name="attn_dwq")
    dw_kv = _mm_tn(h1, dkv, 1, 2 * LANES, gain=w["kv_norm"], name="kv_dw")
    dh1, d_attn_norm, d_kv_norm = _mm_nt(dqa, w["attn_w_q"], norm_of=(h1, w["attn_norm"], dh_b),
                                        also=(dkv, w["w_kv"], w["kv_norm"]), name="attn_dxa")
    deps = hook("attn", dh1, dict(attn_w_o=dw_o, attn_w_q=dw_q, w_kv=dw_kv))
    dh_a, g0 = _ffn_bwd(dh1, h_a, w["ffn_norm"][0], w["ffn_w_up"][0], w["ffn_conv_w"][0], w["ffn_conv_b"][0], w["ffn_w_down"][0], ffn0, 0, deps)
    dw_out = _mm_tn(y, dh_a, 1, D_MODEL, name="hg_dwout")
    deps = hook("ffn0", dh_a, dict(g0, hg_w_out=dw_out))
    dy = _mm_nt(dh_a, w["hg_w_out"], out_dtype=BF16, name="hg_dy", deps=deps)
    dproj, dlb, d_out_norm = _hgrn_bwd(proj, w["hg_lb"], w["hg_out_norm"], o, states, dy)
    deps = hook("hgrn", dproj, None)
    dw_in = _mm_tn(x, dproj, N_CHIPS, D_MODEL, stacked=True, gain=w["hg_norm"], name="hg_dwin", deps=deps)
    deps = hook("hg_w", dproj, dict(hg_w_in=dw_in))
    dx, d_hg_norm = _mm_nt(dproj, w["hg_w_in"], stacked=True, norm_of=(x, w["hg_norm"], dh_a), name="hg_dxn", deps=deps)

    grads = dict(
        hg_norm=d_hg_norm, hg_w_in=dw_in, hg_lb=dlb, hg_out_norm=d_out_norm, hg_w_out=dw_out,
        kv_norm=d_kv_norm, w_kv=dw_kv, attn_norm=d_attn_norm, attn_w_q=dw_q, attn_sinks=dsinks, attn_w_o=dw_o,
        final_norm=d_final,
    )
    for name in g0:
        grads[name] = [g0[name], g1[name]]
    return loss, dx, grads


ANY = pl.BlockSpec(memory_space=pl.ANY)


def _place():
    x, y, c = lax.axis_index("x"), lax.axis_index("y"), lax.axis_index("c")
    chips = [(1 - x, y), (x, 1 - y), (1 - x, 1 - y)]
    return x, y, c, chips


def _rcopy(src, dst, send_sem, recv_sem, to):
    return pltpu.make_async_remote_copy(src_ref=src, dst_ref=dst, send_sem=send_sem, recv_sem=recv_sem, device_id=to, device_id_type=MESH)


HBM = pl.BlockSpec(memory_space=pltpu.HBM)
SEM = pl.BlockSpec(memory_space=pltpu.SEMAPHORE)
EFFECT = pltpu.SideEffectType.DATAFLOW_SIDE_EFFECTING


def _in_hbm(a):
    return pltpu.with_memory_space_constraint(a, pltpu.HBM)


def _place_shard(shard, place, dtype, name, deps=(), layer=None):
    r, cols = shard.shape[-2:]
    tr = _pick(r, ELEM_ROWS)
    src = pl.BlockSpec((tr, cols), lambda i, place_ref: (i, 0)) if layer is None else pl.BlockSpec((None, tr, cols), lambda i, place_ref: (layer, i, 0))

    def body(place_ref, s_ref, *rest):
        o_ref = rest[-1]
        o_ref[...] = s_ref[...].astype(o_ref.dtype)

    return pl.pallas_call(
        body,
        name=name,
        grid_spec=pltpu.PrefetchScalarGridSpec(
            num_scalar_prefetch=1,
            grid=(r // tr,),
            in_specs=[src] + _dep_specs(deps),
            out_specs=pl.BlockSpec((None, tr, cols), lambda i, place_ref: (place_ref[0], i, 0)),
        ),
        out_shape=jax.ShapeDtypeStruct((N_CHIPS, r, cols), dtype),
        compiler_params=_cparams(("parallel",)),
    )(place, shard, *deps)


def _start_copies(name, bufs, n_sem, copies):
    n = len(bufs)

    def body(*refs):
        for cp in copies(refs[:n], refs[n], refs[n + 1]):
            cp.start()
        refs[-1][...] = jnp.zeros_like(refs[-1])

    outs = pl.pallas_call(
        body,
        name=name,
        in_specs=[HBM] * n,
        out_specs=[SEM, SEM] + [HBM] * n + [pl.BlockSpec(memory_space=pltpu.VMEM)],
        out_shape=[pltpu.SemaphoreType.DMA((n_sem,)), pltpu.SemaphoreType.DMA((n_sem,))] + [pltpu.HBM(b.shape, b.dtype) for b in bufs]
        + [jax.ShapeDtypeStruct((SUBLANES, LANES), F32)],
        input_output_aliases={i: 2 + i for i in range(n)},
        compiler_params=pltpu.CompilerParams(has_side_effects=EFFECT),
    )(*[_in_hbm(b) for b in bufs])
    return outs[0], outs[1], list(outs[2:-1]), outs[-1]


def _wait_copies(name, bufs, send_sems, recv_sems, after, copies):
    n = len(bufs)

    def body(*refs):
        for cp in copies(refs[:n], refs[n], refs[n + 1]):
            cp.wait_send()
            cp.wait_recv()

    return pl.pallas_call(
        body,
        name=name,
        in_specs=[HBM] * n + [SEM, SEM, ANY],
        out_specs=[HBM] * n,
        out_shape=[pltpu.HBM(b.shape, b.dtype) for b in bufs],
        input_output_aliases={i: i for i in range(n)},
        compiler_params=pltpu.CompilerParams(has_side_effects=EFFECT),
    )(*bufs, send_sems, recv_sems, after)


def _relay_copies(name, bufs, send_sems, recv_sems, after, landed, n_sem, onward):
    n = len(bufs)

    def body(*refs):
        for cp in landed(refs[:n], refs[n], refs[n + 1]):
            cp.wait_send()
            cp.wait_recv()
        for cp in onward(refs[:n], refs[n + 3], refs[n + 4]):
            cp.start()
        refs[-1][...] = jnp.zeros_like(refs[-1])

    outs = pl.pallas_call(
        body,
        name=name,
        in_specs=[HBM] * n + [SEM, SEM, ANY],
        out_specs=[SEM, SEM] + [HBM] * n + [pl.BlockSpec(memory_space=pltpu.VMEM)],
        out_shape=[pltpu.SemaphoreType.DMA((n_sem,)), pltpu.SemaphoreType.DMA((n_sem,))] + [pltpu.HBM(b.shape, b.dtype) for b in bufs]
        + [jax.ShapeDtypeStruct((SUBLANES, LANES), F32)],
        input_output_aliases={i: 2 + i for i in range(n)},
        compiler_params=pltpu.CompilerParams(has_side_effects=EFFECT),
    )(*bufs, send_sems, recv_sems, after)
    return outs[0], outs[1], list(outs[2:-1]), outs[-1]


def _gather_half_copies(first, count, over_ici):
    def copies(refs, send_sems, recv_sems):
        x, y, c, chips = _place()
        out = []
        for i in range(count):
            h = refs[i].shape[1] // 2
            mine = pl.ds(c * h, h)
            for j, (px, py) in enumerate(chips):
                k = 3 * (first + i) + j
                slot = 2 * x + y if over_ici else 2 * px + py
                to = (px, py, c) if over_ici else (x, y, 1 - c)
                out.append(_rcopy(refs[i].at[slot, mine], refs[i].at[slot, mine], send_sems.at[k], recv_sems.at[k], to))
        return out

    return copies


def _gather_copies(first, count):
    def copies(refs, send_sems, recv_sems):
        x, y, c, chips = _place()
        me = 2 * x + y
        out = []
        for i in range(count):
            for j, (px, py) in enumerate(chips):
                k = 3 * (first + i) + j
                out.append(_rcopy(refs[i].at[me], refs[i].at[me], send_sems.at[k], recv_sems.at[k], (px, py, c)))
        return out

    return copies


def _swap_copies(n):
    def copies(refs, send_sems, recv_sems):
        x, y, c, _ = _place()
        out = []
        for i in range(n):
            h = refs[i].shape[1] // 2
            out.append(_rcopy(refs[i].at[:, pl.ds((1 - c) * h, h)], refs[n + i], send_sems.at[i], recv_sems.at[i], (x, y, 1 - c)))
        return out

    return copies


def _partial_copies(n):
    def copies(refs, send_sems, recv_sems):
        x, y, c, chips = _place()
        out = []
        for i in range(n):
            for j, (px, py) in enumerate(chips):
                out.append(_rcopy(refs[i].at[2 * px + py], refs[n + i].at[j], send_sems.at[3 * i + j], recv_sems.at[3 * i + j], (px, py, c)))
        return out

    return copies


def _share_copies(n):
    def copies(refs, send_sems, recv_sems):
        x, y, c, _ = _place()
        return [_rcopy(refs[i].at[c], refs[i].at[c], send_sems.at[i], recv_sems.at[i], (x, y, 1 - c)) for i in range(n)]

    return copies


def _small_layout(groups):
    flat = [a for g in groups for a in g]
    rows = -(-sum(a.shape[0] for a in flat) // SUBLANES) * SUBLANES
    return flat, rows, max(a.shape[1] for a in flat)


def _pack_small(groups, device):
    flat, rows, cols = _small_layout(groups)

    def body(dev_ref, *refs):
        o_ref = refs[-1]
        o_ref[...] = jnp.zeros_like(o_ref)
        r0 = 0
        for a_ref in refs[:-1]:
            r, w = a_ref.shape
            o_ref[r0:r0 + r, 0:w] = a_ref[...]
            r0 += r

    return pl.pallas_call(
        body,
        name="small_pack",
        grid_spec=pltpu.PrefetchScalarGridSpec(
            num_scalar_prefetch=1,
            grid=(1,),
            in_specs=[pl.BlockSpec(a.shape, lambda i, dev_ref: (0, 0)) for a in flat],
            out_specs=pl.BlockSpec((None, rows, cols), lambda i, dev_ref: (dev_ref[0], 0, 0)),
        ),
        out_shape=jax.ShapeDtypeStruct((N_DEV, rows, cols), F32),
        compiler_params=_cparams(("arbitrary",)),
    )(device, *flat)


def _small_copies(refs, send_sems, recv_sems):
    x, y, c, _ = _place()
    me = 4 * x + 2 * y + c
    out = []
    for k in range(1, N_DEV):
        peer = (x ^ (k >> 2), y ^ ((k >> 1) & 1), c ^ (k & 1))
        out.append(_rcopy(refs[0].at[me], refs[0].at[me], send_sems.at[k - 1], recv_sems.at[k - 1], peer))
    return out


def _sum_small(slots, groups, widths):
    out_shapes = [(sum(a.shape[0] for a in g), wd or g[0].shape[1]) for g, wd in zip(groups, widths)]

    def body(s_ref, *refs):
        outs, acc_ref = refs[:-1], refs[-1]
        acc = s_ref[0]
        for d in range(1, N_DEV):
            acc = acc + s_ref[d]
        acc_ref[...] = acc
        r0 = 0
        for o_ref in outs:
            r, w = o_ref.shape
            o_ref[...] = acc_ref[r0:r0 + r, 0:w]
            r0 += r

    vmem = pl.BlockSpec(memory_space=pltpu.VMEM)
    return pl.pallas_call(
        body,
        name="small_sum",
        in_specs=[vmem],
        out_specs=[vmem] * len(groups),
        out_shape=[jax.ShapeDtypeStruct(s, F32) for s in out_shapes],
        scratch_shapes=[pltpu.VMEM(slots.shape[1:], F32)],
        compiler_params=pltpu.CompilerParams(vmem_limit_bytes=VMEM_LIMIT_BYTES),
    )(slots)


def _adamw_small(items):
    n = len(items)

    def body(*refs):
        for i in range(n):
            w_ref, m_ref, v_ref, g_ref = refs[4 * i:4 * i + 4]
            d_ref, nm_ref, nv_ref = refs[4 * n + 3 * i:4 * n + 3 * i + 3]
            d_ref[...], nm_ref[...], nv_ref[...] = _adamw_math(w_ref[...], m_ref[...], v_ref[...], g_ref[...])

    vmem = pl.BlockSpec(memory_space=pltpu.VMEM)
    outs = pl.pallas_call(
        body,
        name="adamw_small",
        in_specs=[vmem] * (4 * n),
        out_specs=[vmem] * (3 * n),
        out_shape=[jax.ShapeDtypeStruct(it[0].shape, F32) for it in items for _ in range(3)],
        compiler_params=pltpu.CompilerParams(vmem_limit_bytes=VMEM_LIMIT_BYTES),
    )(*[a for it in items for a in it])
    return [tuple(outs[3 * i:3 * i + 3]) for i in range(n)]


class _Reduction:
    def __init__(self, tag, grads, place):
        self.tag, self.n, self.place = tag, len(grads), place
        lands = [lax.empty((N_CHIPS, g.shape[1] // 2, g.shape[2]), F32) for g in grads]
        self._start("swap", list(grads) + lands, self.n, _swap_copies(self.n))

    def _start(self, stage, bufs, n_sem, copies):
        *self.flight, self.token = _start_copies(f"rs_{stage}_start_{self.tag}", bufs, n_sem, copies)

    def _landed(self, stage, after, copies):
        send_sems, recv_sems, bufs = self.flight
        return _wait_copies(f"rs_{stage}_wait_{self.tag}", bufs, send_sems, recv_sems, after, copies)

    def to_chips(self, after):
        n = self.n
        bufs = self._landed("swap", after, _swap_copies(n))
        sums = [_add_core_halves(g, o, self.place, name=f"rs_add_core_{self.tag}_{i}") for i, (g, o) in enumerate(zip(bufs[:n], bufs[n:]))]
        self.mine = [f for f, _ in sums]
        parts = [b for _, b in sums]
        lands = [lax.empty((3,) + p.shape[1:], BF16) for p in parts]
        self._start("send", parts + lands, 3 * n, _partial_copies(n))

    def to_core(self, after):
        n = self.n
        bufs = self._landed("send", after, _partial_copies(n))
        halves = [_add_chip_partials(f, o, self.place, name=f"rs_add_chip_{self.tag}_{i}") for i, (f, o) in enumerate(zip(self.mine, bufs[n:]))]
        self._start("share", halves, n, _share_copies(n))

    def finish(self, after):
        return [b.reshape((-1,) + b.shape[2:]) for b in self._landed("share", after, _share_copies(self.n))]


ELEM_ROWS = (256, 176, 128, 64, 32, 16, 8)


def _add_core_halves(grad, got, place, name):
    s, r, cols = grad.shape
    h = r // 2
    tr = _pick(h, ELEM_ROWS)

    def body(place_ref, g_ref, o_ref, f_ref, b_ref):
        acc = g_ref[...] + o_ref[...]
        b_ref[...] = acc.astype(BF16)

        @pl.when(pl.program_id(1) == place_ref[0])
        def _():
            f_ref[...] = acc

    blk = pl.BlockSpec((None, tr, cols), lambda i, k, place_ref: (k, i, 0))
    return pl.pallas_call(
        body,
        name=name,
        grid_spec=pltpu.PrefetchScalarGridSpec(
            num_scalar_prefetch=1,
            grid=(h // tr, s),
            in_specs=[pl.BlockSpec((None, None, tr, cols), lambda i, k, place_ref: (k, place_ref[1], i, 0)), blk],
            out_specs=[pl.BlockSpec((tr, cols), lambda i, k, place_ref: (i, 0)), blk],
        ),
        out_shape=[jax.ShapeDtypeStruct((h, cols), F32), jax.ShapeDtypeStruct((s, h, cols), BF16)],
        compiler_params=_cparams(("parallel", "arbitrary")),
    )(place, grad.reshape(s, 2, h, cols), got)


def _add_chip_partials(mine, got, place, name):
    h, cols = mine.shape
    tr = _pick(h, ELEM_ROWS)

    def body(place_ref, m_ref, g_ref, o_ref):
        acc = m_ref[...]
        for j in range(3):
            acc = acc + g_ref[j].astype(F32)
        o_ref[...] = acc

    return pl.pallas_call(
        body,
        name=name,
        grid_spec=pltpu.PrefetchScalarGridSpec(
            num_scalar_prefetch=1,
            grid=(h // tr,),
            in_specs=[
                pl.BlockSpec((tr, cols), lambda i, place_ref: (i, 0)),
                pl.BlockSpec((3, tr, cols), lambda i, place_ref: (0, i, 0)),
            ],
            out_specs=pl.BlockSpec((None, tr, cols), lambda i, place_ref: (place_ref[1], i, 0)),
        ),
        out_shape=jax.ShapeDtypeStruct((2, h, cols), F32),
        compiler_params=_cparams(("parallel",)),
    )(place, mine, got)


def _adamw_math(w, m, v, g):
    nm = ADAM_B1 * m + (1.0 - ADAM_B1) * g
    nv = ADAM_B2 * v + (1.0 - ADAM_B2) * (g * g)
    m_hat = nm * (1.0 / (1.0 - ADAM_B1 ** ADAM_STEP))
    v_hat = nv * (1.0 / (1.0 - ADAM_B2 ** ADAM_STEP))
    return -ADAM_LR * (m_hat / (jnp.sqrt(v_hat) + ADAM_EPS) + ADAM_WD * w), nm, nv


def _adamw_layer(w, m, v, g, layer, prev, name):
    nl, r, cols = w.shape
    tr = _pick(r, ELEM_ROWS)

    def body(w_ref, m_ref, v_ref, g_ref, *rest):
        go_ref, d_ref, nm_ref, nv_ref = rest[-4:]
        gv = g_ref[...]
        d_ref[...], nm_ref[...], nv_ref[...] = _adamw_math(w_ref[...], m_ref[...], v_ref[...], gv)
        go_ref[...] = gv

    lay = pl.BlockSpec((None, tr, cols), lambda i: (layer, i, 0))
    return pl.pallas_call(
        body,
        name=name,
        grid=(r // tr,),
        in_specs=[lay] * 3 + [pl.BlockSpec((tr, cols), lambda i: (i, 0))] + ([ANY] * 4 if prev else []),
        out_specs=[lay] * 4,
        out_shape=[jax.ShapeDtypeStruct((nl, r, cols), F32)] * 4,
        input_output_aliases={4 + k: k for k in range(4)} if prev else {},
        compiler_params=_cparams(("parallel",)),
    )(w, m, v, g, *(prev or ()))


def _adamw(w, m, v, g, name):
    r, cols = w.shape
    tr = _pick(r, ELEM_ROWS)

    def body(w_ref, m_ref, v_ref, g_ref, d_ref, nm_ref, nv_ref):
        d_ref[...], nm_ref[...], nv_ref[...] = _adamw_math(w_ref[...], m_ref[...], v_ref[...], g_ref[...])

    blk = pl.BlockSpec((tr, cols), lambda i: (i, 0))
    return pl.pallas_call(
        body,
        name=name,
        grid=(r // tr,),
        in_specs=[blk] * 4,
        out_specs=[blk] * 3,
        out_shape=[jax.ShapeDtypeStruct((r, cols), F32)] * 3,
        compiler_params=_cparams(("parallel",)),
    )(w, m, v, g)


SMALL_COLS = 384
SMALL_ROWS = 16


def _pad_rows(flat, rows, cols):
    return jnp.pad(flat, (0, rows * cols - flat.shape[0])).reshape(rows, cols)


def kernel(x, hg_norm, hg_w_in, hg_lb_logits, hg_out_norm, hg_w_out, kv_norm, w_kv, attn_norm, attn_w_q, attn_sinks, attn_w_o, ffn_norm, ffn_w_up, ffn_conv_w, ffn_conv_b, ffn_w_down, final_norm, loss_target, m_hg_norm, m_hg_w_in, m_hg_lb_logits, m_hg_out_norm, m_hg_w_out, m_kv_norm, m_w_kv, m_attn_norm, m_attn_w_q, m_attn_sinks, m_attn_w_o, m_ffn_norm, m_ffn_w_up, m_ffn_conv_w, m_ffn_conv_b, m_ffn_w_down, m_final_norm, v_hg_norm, v_hg_w_in, v_hg_lb_logits, v_hg_out_norm, v_hg_w_out, v_kv_norm, v_w_kv, v_attn_norm, v_attn_w_q, v_attn_sinks, v_attn_w_o, v_ffn_norm, v_ffn_w_up, v_ffn_conv_w, v_ffn_conv_b, v_ffn_w_down, v_final_norm):
    wts = dict(hg_norm=hg_norm, hg_w_in=hg_w_in, hg_lb_logits=hg_lb_logits, hg_out_norm=hg_out_norm, hg_w_out=hg_w_out, kv_norm=kv_norm, w_kv=w_kv, attn_norm=attn_norm, attn_w_q=attn_w_q, attn_sinks=attn_sinks, attn_w_o=attn_w_o, ffn_norm=ffn_norm, ffn_w_up=ffn_w_up, ffn_conv_w=ffn_conv_w, ffn_conv_b=ffn_conv_b, ffn_w_down=ffn_w_down, final_norm=final_norm)
    mom1 = dict(hg_norm=m_hg_norm, hg_w_in=m_hg_w_in, hg_lb_logits=m_hg_lb_logits, hg_out_norm=m_hg_out_norm, hg_w_out=m_hg_w_out, kv_norm=m_kv_norm, w_kv=m_w_kv, attn_norm=m_attn_norm, attn_w_q=m_attn_w_q, attn_sinks=m_attn_sinks, attn_w_o=m_attn_w_o, ffn_norm=m_ffn_norm, ffn_w_up=m_ffn_w_up, ffn_conv_w=m_ffn_conv_w, ffn_conv_b=m_ffn_conv_b, ffn_w_down=m_ffn_w_down, final_norm=m_final_norm)
    mom2 = dict(hg_norm=v_hg_norm, hg_w_in=v_hg_w_in, hg_lb_logits=v_hg_lb_logits, hg_out_norm=v_hg_out_norm, hg_w_out=v_hg_w_out, kv_norm=v_kv_norm, w_kv=v_w_kv, attn_norm=v_attn_norm, attn_w_q=v_attn_w_q, attn_sinks=v_attn_sinks, attn_w_o=v_attn_w_o, ffn_norm=v_ffn_norm, ffn_w_up=v_ffn_w_up, ffn_conv_w=v_ffn_conv_w, ffn_conv_b=v_ffn_conv_b, ffn_w_down=v_ffn_w_down, final_norm=v_final_norm)
    names = list(wts)
    chip = 2 * lax.axis_index("x") + lax.axis_index("y")
    core = lax.axis_index("c")
    fs = D_FF // N_CHIPS
    ds = D_MODEL // N_CHIPS

    place_arr = jnp.stack([chip, core]).astype(jnp.int32)
    small = jnp.concatenate([hg_norm.reshape(-1), hg_lb_logits.reshape(-1), ffn_conv_w.reshape(-1)])
    n_small = small.shape[0]
    shards = [
        ("small", _pad_rows(small, SMALL_ROWS, SMALL_COLS), F32, None), ("hg_w_in", hg_w_in, BF16, 0),
        ("hg_w_out", hg_w_out, BF16, 0), ("ffn_w_up0", ffn_w_up, BF16, 0), ("ffn_w_down0", ffn_w_down, BF16, 0),
        ("w_kv", w_kv, BF16, None), ("attn_w_q", attn_w_q, BF16, 0), ("attn_w_o", attn_w_o, BF16, 0),
        ("ffn_w_up1", ffn_w_up, BF16, 1), ("ffn_w_down1", ffn_w_down, BF16, 1),
    ]
    n_first = 3
    spans = dict(layer0=(0, 2), layer1=(2, 7))

    def first_copies(refs, send_sems, recv_sems):
        return (_gather_copies(0, 1)(refs[:1], send_sems, recv_sems) + _gather_half_copies(1, 1, True)(refs[1:2], send_sems, recv_sems)
                + _gather_copies(2, 1)(refs[2:3], send_sems, recv_sems))

    placed = [_place_shard(s, place_arr, dt, name=f"place_{nm}", layer=ly) for nm, s, dt, ly in shards[:n_first]]
    first = _start_copies("gather_start_first", placed, 3 * n_first, first_copies)
    placed = [_place_shard(s, place_arr, dt, name=f"place_{nm}", deps=(first[3],), layer=ly) for nm, s, dt, ly in shards[n_first:]]
    rest = _start_copies("gather_start_rest", placed, 3 * len(placed), _gather_half_copies(0, len(placed), True))
    relayed = {}

    def fetch(w, stage, after):
        if stage == "first":
            w_in = _relay_copies("gather_first_relay", first[2][1:2], first[0], first[1], after,
                                 _gather_half_copies(1, 1, True), 3, _gather_half_copies(0, 1, False))
            got = _wait_copies("gather_wait_small", first[2][:1], first[0], first[1], w_in[3], _gather_copies(0, 1))
            got += _wait_copies("gather_wait_first", w_in[2], w_in[0], w_in[1], got[0], _gather_half_copies(0, 1, False))
        elif stage == "mixer_out":
            got = _wait_copies("gather_wait_mixer_out", first[2][2:], first[0], first[1], after, _gather_copies(2, 1))
        elif stage.endswith("_relay"):
            lo, hi = spans[stage[:-6]]
            relayed[stage[:-6]] = _relay_copies(
                f"gather_{stage}", rest[2][lo:hi], rest[0], rest[1], after,
                _gather_half_copies(lo, hi - lo, True), 3 * (hi - lo), _gather_half_copies(0, hi - lo, False))
            return w
        else:
            lo, hi = spans[stage]
            send_sems, recv_sems, bufs, _ = relayed[stage]
            got = _wait_copies(f"gather_wait_{stage}", bufs, send_sems, recv_sems, after, _gather_half_copies(0, hi - lo, False))
        w = dict(w)
        if stage == "first":
            g_small = got[0].reshape(N_CHIPS, -1)[:, :n_small]
            conv_w = g_small[:, 3 * ds:].reshape(N_CHIPS, 2, 3, fs).transpose(1, 2, 0, 3).reshape(2, 3, D_FF)
            w.update(
                hg_norm=g_small[:, :ds].reshape(1, D_MODEL),
                hg_lb=g_small[:, ds:3 * ds].reshape(N_CHIPS, 2, ds).transpose(1, 0, 2).reshape(2, D_MODEL),
                ffn_conv_w=[conv_w[0], conv_w[1]], hg_w_in=got[1],
            )
        elif stage == "mixer_out":
            w.update(hg_w_out=got[0].reshape(1, D_MODEL, D_MODEL))
        elif stage == "layer0":
            w.update(ffn_w_up=[got[0], None], ffn_w_down=[got[1].reshape(1, D_FF, D_MODEL), None])
        else:
            w.update(
                w_kv=got[0].reshape(1, D_MODEL, 2 * LANES), attn_w_q=got[1].reshape(1, D_MODEL, D_MODEL),
                attn_w_o=got[2].reshape(1, D_MODEL, D_MODEL), ffn_w_up=[w["ffn_w_up"][0], got[3]],
                ffn_w_down=[w["ffn_w_down"][0], got[4].reshape(1, D_FF, D_MODEL)],
            )
        return w

    whole = dict(
        hg_out_norm=hg_out_norm, kv_norm=kv_norm.reshape(1, D_MODEL), attn_norm=attn_norm, attn_sinks=attn_sinks.reshape(ATT_QH),
        ffn_norm=[ffn_norm[0:1], ffn_norm[1:2]], ffn_conv_b=[ffn_conv_b[0:1], ffn_conv_b[1:2]], final_norm=final_norm.reshape(1, D_MODEL),
    )
    whole = fetch(whole, "first", rest[3])

    red, layer1 = {}, {}

    def by_rows(g, rows):
        return g.reshape(N_CHIPS, rows, g.shape[2])

    def hook(point, dh, grads):
        if point == "ffn1":
            red["ffn1"] = _Reduction("ffn1", [by_rows(grads["ffn_w_down"], fs), grads["ffn_w_up"]], place_arr)
            return (red["ffn1"].token,)
        if point == "attn":
            red["ffn1"].to_chips(dh)
            layer1.update(grads)
            return (red["ffn1"].token,)
        if point == "ffn0":
            group = [by_rows(layer1["attn_w_o"], ds), by_rows(layer1["attn_w_q"], ds), by_rows(layer1["w_kv"], ds),
                     by_rows(grads["ffn_w_down"], fs), grads["ffn_w_up"], by_rows(grads["hg_w_out"], ds)]
            red["mid"] = _Reduction("mid", group, place_arr)
            return (red["mid"].token,)
        if point == "hgrn":
            red["ffn1"].to_core(dh)
            red["mid"].to_chips(dh)
            return (red["ffn1"].token, red["mid"].token)
        red["hg"] = _Reduction("hg", [grads["hg_w_in"]], place_arr)
        return (red["hg"].token,)

    loss, dx, grads = _local_step(x[0], loss_target[0], whole, fetch, hook)

    small_names = ["hg_out_norm", "attn_sinks", "kv_norm", "attn_norm", "ffn_norm", "ffn_conv_b", "final_norm", "hg_norm", "hg_lb_logits", "ffn_conv_w"]
    groups = [[loss]] + [grads[n] if isinstance(grads[n], list) else [grads[n]] for n in small_names[:-2]] + [[grads["hg_lb"]], grads["ffn_conv_w"]]
    widths = [None, None, ATT_QH] + [None] * 8
    packed = _pack_small(groups, jnp.reshape(2 * chip + core, (1,)).astype(jnp.int32))
    small_flight = _start_copies("small_start", [packed], N_DEV - 1, _small_copies)
    red["hg"].to_chips(small_flight[3])

    out_g, out_d, out_m, out_v = {}, {}, {}, {}

    def update(name, g2):
        shape = wts[name].shape
        d2, m2, v2 = _adamw(wts[name].reshape(g2.shape), mom1[name].reshape(g2.shape), mom2[name].reshape(g2.shape), g2, name=f"adamw_{name}")
        out_g[name], out_d[name], out_m[name], out_v[name] = g2.reshape(shape), d2.reshape(shape), m2.reshape(shape), v2.reshape(shape)
        return d2

    def update_layer(name, g2, layer, prev):
        res = _adamw_layer(wts[name], mom1[name], mom2[name], g2, layer, prev, name=f"adamw_{name}{layer}")
        out_g[name], out_d[name], out_m[name], out_v[name] = res
        return res

    g_down1, g_up1 = red["ffn1"].finish(red["hg"].token)
    up1 = update_layer("ffn_w_up", g_up1, 1, None)
    summed = _sum_small(_wait_copies("small_wait", small_flight[2], small_flight[0], small_flight[1], up1[3], _small_copies)[0], groups, widths)
    loss_out = summed[0][0, 0]
    small_grads = dict(zip(small_names, summed[1:]))
    small_grads["hg_norm"] = lax.dynamic_slice(small_grads["hg_norm"], (0, chip * ds), (1, ds))
    small_grads["hg_lb_logits"] = lax.dynamic_slice(small_grads["hg_lb_logits"], (0, chip * ds), (2, ds))
    small_grads["ffn_conv_w"] = lax.dynamic_slice(small_grads["ffn_conv_w"], (0, chip * fs), (2 * 3, fs))
    red["mid"].to_core(up1[1])
    down1 = update_layer("ffn_w_down", g_down1, 1, None)
    g_o, g_q, g_kv, g_down0, g_up0, g_out = red["mid"].finish(down1[1])
    update("attn_w_o", g_o)
    update("attn_w_q", g_q)
    update("w_kv", g_kv)
    update("hg_w_out", g_out)
    update_layer("ffn_w_down", g_down0, 0, down1)
    last = update_layer("ffn_w_up", g_up0, 0, up1)
    red["hg"].to_core(last[1])
    (g_in,) = red["hg"].finish(last[2])
    update("hg_w_in", g_in)

    as_2d = lambda a, n: a.reshape(small_grads[n].shape)
    updated = _adamw_small([(as_2d(wts[n], n), as_2d(mom1[n], n), as_2d(mom2[n], n), small_grads[n]) for n in small_names])
    for n, (d2, m2, v2) in zip(small_names, updated):
        shape = wts[n].shape
        out_g[n], out_d[n], out_m[n], out_v[n] = small_grads[n].reshape(shape), d2.reshape(shape), m2.reshape(shape), v2.reshape(shape)

    grad_x = dx.reshape(x.shape)
    return (loss_out, grad_x, *[out_g[n] for n in names], *[out_d[n] for n in names], *[out_m[n] for n in names], *[out_v[n] for n in names])
```

```python
import functools

import jax
import jax.numpy as jnp
from jax import lax
from jax.experimental import pallas as pl
from jax.experimental.pallas import tpu as pltpu

F32 = jnp.float32
BF16 = jnp.bfloat16
MESH = pl.DeviceIdType.MESH

EPS = 1e-6
D_MODEL = 1024
HG_HEADS = 8
HG_DK = 128
HG_CHUNK = 64
ATT_HD = 64
ATT_QH = 16
ATT_KVH = 2
ATT_GROUP = ATT_QH // ATT_KVH
WINDOW = 128
D_FF = 2816
N_CHIPS = 4
N_DEV = 8
LANES = 128
SUBLANES = 8
VMEM_LIMIT_BYTES = 56 * 1024 * 1024
NEG = -1e30
ALIBI_SLOPES = tuple(2.0 ** (-8.0 * h / ATT_QH) for h in range(1, ATT_QH + 1))

ADAM_LR = 0.001
ADAM_B1 = 0.9
ADAM_B2 = 0.999
ADAM_EPS = 1e-08
ADAM_WD = 0.01
ADAM_STEP = 10


def _cparams(sem=None):
    return pltpu.CompilerParams(dimension_semantics=sem, vmem_limit_bytes=VMEM_LIMIT_BYTES)


def _pick(n, cands):
    for c in cands:
        if n % c == 0:
            return c
    return n


def _sigmoid(x):
    return 0.5 * jnp.tanh(0.5 * x) + 0.5


def _dot(a, b, dims):
    return lax.dot_general(a, b, (dims, ((), ())), preferred_element_type=F32)


NN = ((1,), (0,))
NT = ((1,), (1,))
TN = ((0,), (0,))


MM_ROWS = 1024


def _rms_stats(xv):
    rstd = lax.rsqrt(jnp.mean(xv * xv, axis=-1, keepdims=True) + EPS)
    return xv * rstd, rstd


def _mm_operand(a_ref, gain_ref):
    if gain_ref is None:
        return a_ref[...].astype(BF16)
    return (_rms_stats(a_ref[...])[0] * gain_ref[...]).astype(BF16)


def _mm_nn(a, w, res=None, out_dtype=F32, name="mm_nn", gain=None):
    m, k = a.shape
    s, _, ns = w.shape
    tm = min(m, MM_ROWS)
    tn = _pick(ns, (1024, 1408, 512, 256, 128))
    npb = ns // tn

    def body(a_ref, w_ref, *rest):
        o_ref = rest[-1]
        acc = _dot(_mm_operand(a_ref, rest[0] if gain is not None else None), w_ref[...], NN)
        if res is not None:
            acc = acc + rest[-2][...]
        o_ref[...] = acc.astype(o_ref.dtype)

    in_specs = [
        pl.BlockSpec((tm, k), lambda i, j: (i, 0)),
        pl.BlockSpec((None, k, tn), lambda i, j: (j // npb, 0, j % npb)),
    ]
    args = [a, w]
    if gain is not None:
        in_specs.append(pl.BlockSpec((1, k), lambda i, j: (0, 0)))
        args.append(gain)
    if res is not None:
        in_specs.append(pl.BlockSpec((tm, tn), lambda i, j: (i, j)))
        args.append(res)
    return pl.pallas_call(
        body,
        name=name,
        grid=(m // tm, s * npb),
        in_specs=in_specs,
        out_specs=pl.BlockSpec((tm, tn), lambda i, j: (i, j)),
        out_shape=jax.ShapeDtypeStruct((m, s * ns), out_dtype),
        compiler_params=_cparams(("parallel", "parallel")),
    )(*args)


def _dy_spec(stacked, tm, tn, npb, row, kk):
    if stacked:
        return pl.BlockSpec((None, tm, tn), lambda *g: (kk(g) // npb, row(g), kk(g) % npb))
    return pl.BlockSpec((tm, tn), lambda *g: (row(g), kk(g)))


def _dep_specs(deps):
    return [pl.BlockSpec(d.shape, lambda *g: (0, 0)) for d in deps]


def _mm_nt(dy, w, stacked=False, out_dtype=F32, name="mm_nt", deps=(), norm_of=None, also=None):
    s, k, ns = w.shape
    m = dy.shape[1] if stacked else dy.shape[0]
    tm = min(m, MM_ROWS)
    tko = _pick(k, (1024, 1408, 512, 256))
    tn = _pick(ns, (1024, 1408, 512, 256))
    npb = ns // tn
    nk = s * npb
    fused = norm_of is not None
    assert not fused or tko == k
    assert also is None or fused

    def body(dy_ref, w_ref, *rest):
        acc_ref = rest[-1]
        i, kk = pl.program_id(0), pl.program_id(2)

        @pl.when(kk == 0)
        def _():
            acc_ref[...] = jnp.zeros_like(acc_ref)

        acc_ref[...] += _dot(dy_ref[...].astype(BF16), w_ref[...], NT)

        if not fused:
            @pl.when(kk == nk - 1)
            def _():
                rest[-2][...] = acc_ref[...].astype(rest[-2].dtype)
            return
        x_ref, g_ref, dres_ref = rest[:3]
        n_out = 3 if also is not None else 2
        dx_ref, dg_refs = rest[-1 - n_out], rest[-n_out:-1]

        @pl.when(jnp.logical_and(i == 0, kk == 0))
        def _():
            for dg_ref in dg_refs:
                dg_ref[...] = jnp.zeros_like(dg_ref)

        @pl.when(kk == nk - 1)
        def _():
            dxn = acc_ref[...]
            xhat, rstd = _rms_stats(x_ref[...])
            gd = dxn * g_ref[...]
            dg_refs[0][...] += jnp.sum(dxn * xhat, axis=0, keepdims=True)
            if also is not None:
                dy2_ref, w2_ref, g2_ref = rest[3:6]
                dxn2 = _dot(dy2_ref[...].astype(BF16), w2_ref[...], NT)
                gd = gd + dxn2 * g2_ref[...]
                dg_refs[1][...] += jnp.sum(dxn2 * xhat, axis=0, keepdims=True)
            dx_ref[...] = dres_ref[...] + rstd * (gd - xhat * jnp.mean(gd * xhat, axis=-1, keepdims=True))

    row = pl.BlockSpec((tm, tko), lambda i, j, kk: (i, j))
    vec = pl.BlockSpec((1, k), lambda i, j, kk: (0, 0))
    extra_in, extra_args = [], ()
    if fused:
        extra_in, extra_args = [row, vec, row], tuple(norm_of)
    if also is not None:
        n2 = also[0].shape[1]
        extra_in += [pl.BlockSpec((tm, n2), lambda i, j, kk: (i, 0)), pl.BlockSpec((None, k, n2), lambda i, j, kk: (0, 0, 0)), vec]
        extra_args += tuple(also)
    f32 = lambda shape: jax.ShapeDtypeStruct(shape, F32)
    return pl.pallas_call(
        body,
        name=name,
        grid=(m // tm, k // tko, nk),
        in_specs=[
            _dy_spec(stacked, tm, tn, npb, lambda g: g[0], lambda g: g[2]),
            pl.BlockSpec((None, tko, tn), lambda i, j, kk: (kk // npb, j, kk % npb)),
        ] + extra_in + _dep_specs(deps),
        out_specs=([row, vec] + ([vec] if also is not None else [])) if fused else row,
        out_shape=([f32((m, k)), f32((1, k))] + ([f32((1, k))] if also is not None else [])) if fused else jax.ShapeDtypeStruct((m, k), out_dtype),
        scratch_shapes=[pltpu.VMEM((tm, tko), F32)],
        compiler_params=_cparams(("arbitrary",) * 3 if fused else ("parallel", "parallel", "arbitrary")),
    )(dy, w, *extra_args, *deps)


def _mm_tn(a, dy, s, ns, stacked=False, name="mm_tn", deps=(), gain=None):
    m, k = a.shape
    tm = min(m, MM_ROWS)
    tk = _pick(k, (1024, 1408, 512, 256))
    tn = _pick(ns, (1024, 1408, 512, 256, 128))
    npb = ns // tn
    nm = m // tm
    assert gain is None or tk == k

    def body(a_ref, dy_ref, *rest):
        j, mm = pl.program_id(1), pl.program_id(2)
        if gain is None:
            o_ref, acc_ref = rest[-2:]
            lhs = a_ref[...].astype(BF16)
        else:
            o_ref, acc_ref, xn_ref = rest[-3:]

            @pl.when(j == 0)
            def _():
                xn_ref[mm] = _mm_operand(a_ref, rest[0])

            lhs = xn_ref[mm]

        @pl.when(mm == 0)
        def _():
            acc_ref[...] = jnp.zeros_like(acc_ref)

        acc_ref[...] += _dot(lhs, dy_ref[...].astype(BF16), TN)

        @pl.when(mm == nm - 1)
        def _():
            o_ref[...] = acc_ref[...]

    a_rows = (lambda i, j, mm: (mm, i)) if gain is None else (lambda i, j, mm: (jnp.where(j == 0, mm, 0), i))
    return pl.pallas_call(
        body,
        name=name,
        grid=(k // tk, s * npb, nm),
        in_specs=[
            pl.BlockSpec((tm, tk), a_rows),
            _dy_spec(stacked, tm, tn, npb, lambda g: g[2], lambda g: g[1]),
        ] + ([pl.BlockSpec((1, k), lambda i, j, mm: (0, 0))] if gain is not None else []) + _dep_specs(deps),
        out_specs=pl.BlockSpec((None, tk, tn), lambda i, j, mm: (j // npb, i, j % npb)),
        out_shape=jax.ShapeDtypeStruct((s, k, ns), F32),
        scratch_shapes=[pltpu.VMEM((tk, tn), F32)] + ([pltpu.VMEM((nm, tm, tk), BF16)] if gain is not None else []),
        compiler_params=_cparams(("parallel", "arbitrary", "arbitrary") if gain is not None else ("parallel", "parallel", "arbitrary")),
    )(a, dy, *(() if gain is None else (gain,)), *deps)


ROW_TILE = 512


def _loss_head(h, g, target):
    t, d = h.shape
    r = min(t, ROW_TILE)

    def body(h_ref, g_ref, t_ref, dh_ref, dg_ref, loss_ref):
        @pl.when(pl.program_id(0) == 0)
        def _():
            dg_ref[...] = jnp.zeros_like(dg_ref)
            loss_ref[...] = jnp.zeros_like(loss_ref)

        xv = h_ref[...]
        rstd = lax.rsqrt(jnp.mean(xv * xv, axis=-1, keepdims=True) + EPS)
        xhat = xv * rstd
        gv = g_ref[...]
        err = xhat * gv - t_ref[...]
        loss_ref[...] += 0.5 * jnp.sum(jnp.mean(err * err, axis=-1, keepdims=True), axis=0, keepdims=True)
        dy = err * (1.0 / d)
        gd = dy * gv
        dh_ref[...] = rstd * (gd - xhat * jnp.mean(gd * xhat, axis=-1, keepdims=True))
        dg_ref[...] += jnp.sum(dy * xhat, axis=0, keepdims=True)

    return pl.pallas_call(
        body,
        name="loss_head",
        grid=(t // r,),
        in_specs=[
            pl.BlockSpec((r, d), lambda i: (i, 0)),
            pl.BlockSpec((1, d), lambda i: (0, 0)),
            pl.BlockSpec((r, d), lambda i: (i, 0)),
        ],
        out_specs=[
            pl.BlockSpec((r, d), lambda i: (i, 0)),
            pl.BlockSpec((1, d), lambda i: (0, 0)),
            pl.BlockSpec((1, LANES), lambda i: (0, 0)),
        ],
        out_shape=[
            jax.ShapeDtypeStruct((t, d), F32),
            jax.ShapeDtypeStruct((1, d), F32),
            jax.ShapeDtypeStruct((1, LANES), F32),
        ],
        compiler_params=_cparams(("arbitrary",)),
    )(h, g, target)


CONV_ROWS = 256
CONV_COLS = 1408


def _conv_taps(x_ext, n):
    tot = x_ext.shape[0]
    g1 = pltpu.roll(x_ext, 1, 0)[tot - n:]
    g2 = pltpu.roll(x_ext, 2, 0)[tot - n:]
    return g2, g1


def _conv_fwd(up, conv_w, conv_b, name="conv_fwd"):
    t = up.shape[0]
    r = min(t, CONV_ROWS)
    tc = CONV_COLS
    ncb = D_FF // tc
    hb = r // SUBLANES

    def body(g_ref, halo_ref, v_ref, w_ref, b_ref, o_ref, c_ref):
        i = pl.program_id(1)
        g0 = g_ref[...]
        halo = halo_ref[...] * jnp.where(i > 0, 1.0, 0.0)
        g2, g1 = _conv_taps(jnp.concatenate([halo, g0], axis=0), r)
        c = b_ref[...] + w_ref[0:1, :] * g2 + w_ref[1:2, :] * g1 + w_ref[2:3, :] * g0
        c_ref[...] = c.astype(BF16)
        o_ref[...] = (c * _sigmoid(c) * v_ref[...]).astype(BF16)

    blk = pl.BlockSpec((r, tc), lambda j, i: (i, j))
    return pl.pallas_call(
        body,
        name=name,
        grid=(ncb, t // r),
        in_specs=[
            blk,
            pl.BlockSpec((SUBLANES, tc), lambda j, i: (jnp.maximum(i * hb - 1, 0), j)),
            pl.BlockSpec((r, tc), lambda j, i: (i, ncb + j)),
            pl.BlockSpec((3, tc), lambda j, i: (0, j)),
            pl.BlockSpec((1, tc), lambda j, i: (0, j)),
        ],
        out_specs=[blk, blk],
        out_shape=[jax.ShapeDtypeStruct((t, D_FF), BF16), jax.ShapeDtypeStruct((t, D_FF), BF16)],
        compiler_params=_cparams(("parallel", "parallel")),
    )(up, up, up, conv_w, conv_b)


def _conv_bwd(up, conv_w, c, dact, name="conv_bwd"):
    t = up.shape[0]
    r = min(t, CONV_ROWS)
    tc = CONV_COLS
    ncb = D_FF // tc
    nrt = t // r

    def body(g_ref, v_ref, w_ref, c_ref, da_ref, dup_ref, dw_ref, db_ref, nxt_ref):
        ii = pl.program_id(1)

        @pl.when(ii == 0)
        def _():
            nxt_ref[...] = jnp.zeros_like(nxt_ref)
            dw_ref[...] = jnp.zeros_like(dw_ref)
            db_ref[...] = jnp.zeros_like(db_ref)

        g0 = g_ref[...]
        w0, w1, w2 = w_ref[0:1, :], w_ref[1:2, :], w_ref[2:3, :]
        c = c_ref[...].astype(F32)
        sg = _sigmoid(c)
        da = da_ref[...]
        dup_ref[1] = (da * (c * sg)).astype(BF16)
        dc = da * v_ref[...] * (sg * (1.0 + c * (1.0 - sg)))
        ext = jnp.concatenate([dc, nxt_ref[...]], axis=0)
        tot = r + SUBLANES
        d1 = pltpu.roll(ext, tot - 1, 0)[:r]
        d2 = pltpu.roll(ext, tot - 2, 0)[:r]
        nxt_ref[...] = dc[:SUBLANES]
        dup_ref[0] = (w2 * dc + w1 * d1 + w0 * d2).astype(BF16)
        db_ref[...] += jnp.sum(dc, axis=0, keepdims=True)
        dw_ref[0:1, :] += jnp.sum(d2 * g0, axis=0, keepdims=True)
        dw_ref[1:2, :] += jnp.sum(d1 * g0, axis=0, keepdims=True)
        dw_ref[2:3, :] += jnp.sum(dc * g0, axis=0, keepdims=True)

    rev = lambda ii: nrt - 1 - ii
    dup, dw, db = pl.pallas_call(
        body,
        name=name,
        grid=(ncb, nrt),
        in_specs=[
            pl.BlockSpec((r, tc), lambda j, ii: (rev(ii), j)),
            pl.BlockSpec((r, tc), lambda j, ii: (rev(ii), ncb + j)),
            pl.BlockSpec((3, tc), lambda j, ii: (0, j)),
            pl.BlockSpec((r, tc), lambda j, ii: (rev(ii), j)),
            pl.BlockSpec((r, tc), lambda j, ii: (rev(ii), j)),
        ],
        out_specs=[
            pl.BlockSpec((2, None, r, tc), lambda j, ii: (0, j, rev(ii), 0)),
            pl.BlockSpec((3, tc), lambda j, ii: (0, j)),
            pl.BlockSpec((1, tc), lambda j, ii: (0, j)),
        ],
        out_shape=[
            jax.ShapeDtypeStruct((2, ncb, t, tc), BF16),
            jax.ShapeDtypeStruct((3, D_FF), F32),
            jax.ShapeDtypeStruct((1, D_FF), F32),
        ],
        scratch_shapes=[pltpu.VMEM((SUBLANES, tc), F32)],
        compiler_params=_cparams(("parallel", "arbitrary")),
    )(up, up, conv_w, c, dact)
    return dup.reshape(2 * ncb, t, tc), dw, db


def _split3(x):
    x1 = x.astype(BF16)
    r1 = x - x1.astype(F32)
    x2 = r1.astype(BF16)
    x3 = (r1 - x2.astype(F32)).astype(BF16)
    return x1, x2, x3


def _tri_dot(tri, x, dims):
    x1, x2, x3 = _split3(x)
    return _dot(tri, x1, dims) + _dot(tri, x2, dims) + _dot(tri, x3, dims)


def _lower_bound(logits_ref):
    return _sigmoid(logits_ref[0:1, :] - logits_ref[1:2, :])


def _hg_gates(qr, fr, lb):
    q = qr * _sigmoid(qr) * (HG_DK ** -0.5)
    sf = _sigmoid(fr)
    fg = lb + (1.0 - lb) * sf
    return q, sf, fg


def _hg_chunk_terms(q, fg, tril_b, low_half):
    g = jnp.log(fg)
    k = 1.0 - fg
    cum = _tri_dot(tril_b, g, NN)
    c_last = jnp.sum(g, axis=0, keepdims=True)
    c_mid = jnp.sum(jnp.where(low_half, g, 0.0), axis=0, keepdims=True)
    e_q = jnp.exp(cum - c_mid)
    e_k = jnp.exp(c_mid - cum)
    e_0 = jnp.exp(cum)
    e_l = jnp.exp(c_last - cum)
    return k, e_q, e_k, e_0, e_l, jnp.exp(c_last)


HG_BLOCK = 256


def _hg_proj_specs(rb, row):
    return [pl.BlockSpec((rb, D_MODEL), functools.partial(lambda i, k: (row(i), k), k=k)) for k in range(4)]


def _hg_consts(c):
    tril = lax.broadcasted_iota(jnp.int32, (c, c), 0) >= lax.broadcasted_iota(jnp.int32, (c, c), 1)
    low_half = lax.broadcasted_iota(jnp.int32, (c, D_MODEL), 0) < c // 2
    return tril, tril.astype(BF16), low_half


def _hgrn_fwd(proj, lb, wn):
    t = proj.shape[0]
    c = HG_CHUNK
    rb = min(t, HG_BLOCK)
    cpb = rb // c

    def body(q_ref, f_ref, i_ref, g_ref, lb_ref, wn_ref, o_ref, y_ref, st_ref, s_scr):
        @pl.when(pl.program_id(0) == 0)
        def _():
            s_scr[...] = jnp.zeros_like(s_scr)

        lb_all = _lower_bound(lb_ref)
        wnv = wn_ref[...]
        tril, tril_b, low_half = _hg_consts(c)

        def chunk(n, carry):
            rows = pl.ds(pl.multiple_of(n * c, c), c)
            q, _, fg = _hg_gates(q_ref[rows, :], f_ref[rows, :], lb_all)
            k, e_q, e_k, e_0, e_l, e_last = _hg_chunk_terms(q, fg, tril_b, low_half)
            qi, ki, q0, kl = (q * e_q).astype(BF16), (k * e_k).astype(BF16), (q * e_0).astype(BF16), (k * e_l).astype(BF16)
            v = i_ref[rows, :].astype(BF16)
            gr = g_ref[rows, :]
            gate = gr * _sigmoid(gr)
            for h in range(HG_HEADS):
                cols = slice(h * HG_DK, (h + 1) * HG_DK)
                st = s_scr[h]
                st_ref[h, n] = st
                a = jnp.where(tril, _dot(qi[:, cols], ki[:, cols], NT), 0.0)
                o = _dot(q0[:, cols], st.astype(BF16), NT) + _dot(a.astype(BF16), v[:, cols], NN)
                s_scr[h] = st * e_last[:, cols] + _dot(v[:, cols], kl[:, cols], TN)
                o_ref[rows, cols] = o
                rstd = lax.rsqrt(jnp.mean(o * o, axis=-1, keepdims=True) + EPS)
                y_ref[rows, cols] = (o * rstd * wnv * gate[:, cols]).astype(BF16)
            return carry

        lax.fori_loop(0, cpb, chunk, 0)

    blk = pl.BlockSpec((rb, D_MODEL), lambda i: (i, 0))
    return pl.pallas_call(
        body,
        name="hgrn_fwd",
        grid=(t // rb,),
        in_specs=_hg_proj_specs(rb, lambda i: i) + [pl.BlockSpec((2, D_MODEL), lambda i: (0, 0)), pl.BlockSpec((1, HG_DK), lambda i: (0, 0))],
        out_specs=[blk, blk, pl.BlockSpec((HG_HEADS, cpb, HG_DK, HG_DK), lambda i: (0, i, 0, 0))],
        out_shape=[
            jax.ShapeDtypeStruct((t, D_MODEL), F32),
            jax.ShapeDtypeStruct((t, D_MODEL), BF16),
            jax.ShapeDtypeStruct((HG_HEADS, t // c, HG_DK, HG_DK), F32),
        ],
        scratch_shapes=[pltpu.VMEM((HG_HEADS, HG_DK, HG_DK), F32)],
        compiler_params=_cparams(("arbitrary",)),
    )(proj, proj, proj, proj, lb, wn)


def _hgrn_bwd(proj, lb, wn, o, states, dy):
    t = proj.shape[0]
    c = HG_CHUNK
    rb = min(t, HG_BLOCK)
    cpb = rb // c
    nb = t // rb

    def body(q_ref, f_ref, i_ref, g_ref, lb_ref, wn_ref, o_ref, st_ref, dy_ref, dp_ref, dl_ref, dwn_ref, ds_scr, dlb_scr):
        step = pl.program_id(0)

        @pl.when(step == 0)
        def _():
            dwn_ref[...] = jnp.zeros_like(dwn_ref)
            ds_scr[...] = jnp.zeros_like(ds_scr)
            dlb_scr[...] = jnp.zeros_like(dlb_scr)

        lb_all = _lower_bound(lb_ref)
        wnv = wn_ref[...]
        tril, tril_b, low_half = _hg_consts(c)

        def chunk(nn, carry):
            n = cpb - 1 - nn
            rows = pl.ds(pl.multiple_of(n * c, c), c)
            qr = q_ref[rows, :]
            gr = g_ref[rows, :]
            q, sf, fg = _hg_gates(qr, f_ref[rows, :], lb_all)
            k, e_q, e_k, e_0, e_l, e_last = _hg_chunk_terms(q, fg, tril_b, low_half)
            qi, qi_lo, _ = _split3(q * e_q)
            ki, ki_lo, _ = _split3(k * e_k)
            q0 = (q * e_0).astype(BF16)
            kl = (k * e_l).astype(BF16)
            v = i_ref[rows, :].astype(BF16)
            sg = _sigmoid(gr)
            silu_g = gr * sg
            dsilu_g = sg * (1.0 + gr * (1.0 - sg))
            dqs, dks, d_lasts = [], [], []
            for h in range(HG_HEADS):
                cols = slice(h * HG_DK, (h + 1) * HG_DK)
                ov = o_ref[rows, cols]
                dyv = dy_ref[rows, cols].astype(F32)
                rstd = lax.rsqrt(jnp.mean(ov * ov, axis=-1, keepdims=True) + EPS)
                ohat = ov * rstd
                dp_ref[3, rows, cols] = (dyv * (ohat * wnv) * dsilu_g[:, cols]).astype(BF16)
                don = dyv * silu_g[:, cols]
                dwn_ref[...] += jnp.sum(don * ohat, axis=0, keepdims=True)
                gd = don * wnv
                do_b = (rstd * (gd - ohat * jnp.mean(gd * ohat, axis=-1, keepdims=True))).astype(BF16)
                st = st_ref[h, n]
                ds = ds_scr[h]
                ds_b = ds.astype(BF16)
                vh, kh = v[:, cols], k[:, cols]
                a_b = jnp.where(tril, _dot(qi[:, cols], ki[:, cols], NT), 0.0).astype(BF16)
                da_b = jnp.where(tril, _dot(do_b, vh, NT), 0.0).astype(BF16)
                dqs.append(_dot(do_b, st.astype(BF16), NN) * e_0[:, cols]
                           + (_dot(da_b, ki[:, cols], NN) + _dot(da_b, ki_lo[:, cols], NN)) * e_q[:, cols])
                dk_state = _dot(vh, ds_b, NN) * e_l[:, cols]
                dks.append((_dot(da_b, qi[:, cols], TN) + _dot(da_b, qi_lo[:, cols], TN)) * e_k[:, cols] + dk_state)
                dp_ref[2, rows, cols] = (_dot(a_b, do_b, TN) + _dot(kl[:, cols], ds_b, NT)).astype(BF16)
                ds_scr[h] = ds * e_last[:, cols] + _dot(do_b, q0[:, cols], TN)
                d_lasts.append(jnp.sum(dk_state * kh, axis=0, keepdims=True) + jnp.sum(ds * st, axis=0, keepdims=True) * e_last[:, cols])
            dq = jnp.concatenate(dqs, axis=1)
            dk = jnp.concatenate(dks, axis=1)
            dlogf = _tri_dot(tril_b, q * dq - k * dk, TN) + jnp.concatenate(d_lasts, axis=1)
            dfg = dlogf / fg - dk
            dlb_scr[...] += jnp.sum(dfg * (1.0 - sf), axis=0, keepdims=True)
            sq = _sigmoid(qr)
            dp_ref[0, rows, :] = (dq * (HG_DK ** -0.5) * (sq * (1.0 + qr * (1.0 - sq)))).astype(BF16)
            dp_ref[1, rows, :] = (dfg * (1.0 - lb_all) * sf * (1.0 - sf)).astype(BF16)
            return carry

        lax.fori_loop(0, cpb, chunk, 0)

        @pl.when(step == nb - 1)
        def _():
            d0 = dlb_scr[...] * lb_all * (1.0 - lb_all)
            dl_ref[0:1, :] = d0
            dl_ref[1:2, :] = -d0

    rev = lambda i: nb - 1 - i
    blk = pl.BlockSpec((rb, D_MODEL), lambda i: (rev(i), 0))
    return pl.pallas_call(
        body,
        name="hgrn_bwd",
        grid=(nb,),
        in_specs=_hg_proj_specs(rb, rev)
        + [pl.BlockSpec((2, D_MODEL), lambda i: (0, 0)), pl.BlockSpec((1, HG_DK), lambda i: (0, 0)), blk,
           pl.BlockSpec((HG_HEADS, cpb, HG_DK, HG_DK), lambda i: (0, rev(i), 0, 0)), blk],
        out_specs=[
            pl.BlockSpec((4, rb, D_MODEL), lambda i: (0, rev(i), 0)),
            pl.BlockSpec((2, D_MODEL), lambda i: (0, 0)),
            pl.BlockSpec((1, HG_DK), lambda i: (0, 0)),
        ],
        out_shape=[
            jax.ShapeDtypeStruct((4, t, D_MODEL), BF16),
            jax.ShapeDtypeStruct((2, D_MODEL), F32),
            jax.ShapeDtypeStruct((1, HG_DK), F32),
        ],
        scratch_shapes=[pltpu.VMEM((HG_HEADS, HG_DK, HG_DK), F32), pltpu.VMEM((1, D_MODEL), F32)],
        compiler_params=_cparams(("arbitrary",)),
    )(proj, proj, proj, proj, lb, wn, o, states, dy)


ATT_STACK = 8


def _att_stack(q_ref, sink_ref, first, lo, bias_p, bias_c, extra_ref=None):
    qs, bps, bcs, sinks, extras = [], [], [], None, []
    rows = lax.broadcasted_iota(jnp.int32, (ATT_STACK * WINDOW, 1), 0)
    for i in range(ATT_STACK):
        hq = first + i
        cols = slice((hq // 2) * LANES, (hq // 2 + 1) * LANES)
        sel = lo if hq % 2 == 0 else jnp.logical_not(lo)
        qp = q_ref[:, cols] * (ATT_HD ** -0.5)
        qs.append(jnp.where(sel, qp, jnp.zeros_like(qp)))
        bps.append(ALIBI_SLOPES[hq] * bias_p)
        bcs.append(ALIBI_SLOPES[hq] * bias_c)
        sinks = sink_ref[hq] if sinks is None else jnp.where(rows < i * WINDOW, sinks, sink_ref[hq])
        if extra_ref is not None:
            ep = extra_ref[:, cols]
            extras.append(jnp.where(sel, ep, jnp.zeros_like(ep)))
    cat = lambda parts: jnp.concatenate(parts, axis=0)
    return cat(qs), cat(bps), cat(bcs), sinks, (cat(extras) if extras else None)


def _att_rows(i):
    return slice(i * WINDOW, (i + 1) * WINDOW)


def _att_bias(n):
    tq = lax.broadcasted_iota(jnp.int32, (WINDOW, WINDOW), 0)
    sk = lax.broadcasted_iota(jnp.int32, (WINDOW, WINDOW), 1)
    valid_c = sk <= tq
    valid_p = (sk - tq) > jnp.where(n > 0, 0, WINDOW)
    dist_c = (tq - sk).astype(F32)
    return jnp.where(valid_p, -dist_c - float(WINDOW), NEG), jnp.where(valid_c, -dist_c, NEG)


def _att_halves(x, lo, kh):
    r = pltpu.roll(x, ATT_HD, 1)
    zero = jnp.zeros_like(x)
    if kh == 0:
        return jnp.where(lo, x, r), jnp.where(lo, x, zero), jnp.where(lo, zero, r)
    return jnp.where(lo, r, x), jnp.where(lo, r, zero), jnp.where(lo, zero, x)


def _att_probs(qm, k2p, k2c, bias_p, bias_c, sink):
    sp = _dot(qm, k2p, NT) + bias_p
    sc = _dot(qm, k2c, NT) + bias_c
    m = jnp.maximum(jnp.maximum(jnp.max(sp, axis=-1, keepdims=True), jnp.max(sc, axis=-1, keepdims=True)), sink)
    ep = jnp.exp(sp - m)
    ec = jnp.exp(sc - m)
    es = jnp.exp(sink - m)
    inv = 1.0 / (jnp.sum(ep, axis=-1, keepdims=True) + jnp.sum(ec, axis=-1, keepdims=True) + es)
    return ep * inv, ec * inv, es * inv


def _attn_fwd(q, kv, sinks):
    t = q.shape[0]
    nb = t // WINDOW

    def body(sink_ref, q_ref, kvp_ref, kvc_ref, o_ref):
        n = pl.program_id(0)
        bias_p, bias_c = _att_bias(n)
        lo = lax.broadcasted_iota(jnp.int32, (WINDOW, LANES), 1) < ATT_HD
        for kh in range(ATT_KVH):
            k2p, _, _ = _att_halves(kvp_ref[:, 0:LANES], lo, kh)
            k2c, _, _ = _att_halves(kvc_ref[:, 0:LANES], lo, kh)
            _, vlo_p, vhi_p = _att_halves(kvp_ref[:, LANES:2 * LANES], lo, kh)
            _, vlo_c, vhi_c = _att_halves(kvc_ref[:, LANES:2 * LANES], lo, kh)
            for first in range(kh * ATT_GROUP, (kh + 1) * ATT_GROUP, ATT_STACK):
                qs, bp, bc, sinks, _ = _att_stack(q_ref, sink_ref, first, lo, bias_p, bias_c)
                pp, pc, _ = _att_probs(qs, k2p, k2c, bp, bc, sinks)
                pp, pc = pp.astype(BF16), pc.astype(BF16)
                for i in range(0, ATT_STACK, 2):
                    even, odd = _att_rows(i), _att_rows(i + 1)
                    out = (_dot(pp[even], vlo_p, NN) + _dot(pc[even], vlo_c, NN)
                           + _dot(pp[odd], vhi_p, NN) + _dot(pc[odd], vhi_c, NN))
                    j = (first + i) // 2
                    o_ref[:, j * LANES:(j + 1) * LANES] = out.astype(BF16)

    return pl.pallas_call(
        body,
        name="attn_fwd",
        grid=(nb,),
        in_specs=[
            pl.BlockSpec(memory_space=pltpu.SMEM),
            pl.BlockSpec((WINDOW, D_MODEL), lambda n: (n, 0)),
            pl.BlockSpec((WINDOW, 2 * LANES), lambda n: (jnp.maximum(n - 1, 0), 0)),
            pl.BlockSpec((WINDOW, 2 * LANES), lambda n: (n, 0)),
        ],
        out_specs=pl.BlockSpec((WINDOW, D_MODEL), lambda n: (n, 0)),
        out_shape=jax.ShapeDtypeStruct((t, D_MODEL), BF16),
        compiler_params=_cparams(("parallel",)),
    )(sinks, q, kv, kv)


def _attn_bwd(q, kv, sinks, dout):
    t = q.shape[0]
    nb = t // WINDOW

    def body(sink_ref, q_ref, kvp_ref, kvc_ref, do_ref, dq_ref, dkv_ref, dsink_ref, carry_ref):
        n = pl.program_id(0)

        @pl.when(n == 0)
        def _():
            carry_ref[...] = jnp.zeros_like(carry_ref)
            dsink_ref[...] = jnp.zeros_like(dsink_ref)

        @pl.when(n == nb)
        def _():
            dkv_ref[...] = carry_ref[...].astype(BF16)

        @pl.when(n < nb)
        def _():
            bias_p, bias_c = _att_bias(n)
            lo = lax.broadcasted_iota(jnp.int32, (WINDOW, LANES), 1) < ATT_HD
            lane1 = lax.broadcasted_iota(jnp.int32, (1, LANES), 1)
            dsink = jnp.zeros((1, LANES), F32)
            halves = []
            for kh in range(ATT_KVH):
                k2p, klo_p, khi_p = _att_halves(kvp_ref[:, 0:LANES], lo, kh)
                k2c, klo_c, khi_c = _att_halves(kvc_ref[:, 0:LANES], lo, kh)
                v2p, _, _ = _att_halves(kvp_ref[:, LANES:2 * LANES], lo, kh)
                v2c, _, _ = _att_halves(kvc_ref[:, LANES:2 * LANES], lo, kh)
                acc = [jnp.zeros((WINDOW, LANES), F32) for _ in range(4)]
                for first in range(kh * ATT_GROUP, (kh + 1) * ATT_GROUP, ATT_STACK):
                    qs, bp, bc, sinks, dos = _att_stack(q_ref, sink_ref, first, lo, bias_p, bias_c, do_ref)
                    pp, pc, ps = _att_probs(qs, k2p, k2c, bp, bc, sinks)
                    dpp = _dot(dos, v2p, NT)
                    dpc = _dot(dos, v2c, NT)
                    delta = jnp.sum(pp * dpp, axis=-1, keepdims=True) + jnp.sum(pc * dpc, axis=-1, keepdims=True)
                    dsp = (pp * (dpp - delta)).astype(BF16)
                    dsc = (pc * (dpc - delta)).astype(BF16)
                    sink_term = ps * delta
                    for i in range(ATT_STACK):
                        dsink = dsink + jnp.where(lane1 == first + i, -jnp.sum(sink_term[_att_rows(i)], axis=0, keepdims=True), 0.0)
                    for i in range(0, ATT_STACK, 2):
                        even, odd = _att_rows(i), _att_rows(i + 1)
                        dq_pair = (_dot(dsp[even], klo_p, NN) + _dot(dsc[even], klo_c, NN)
                                   + _dot(dsp[odd], khi_p, NN) + _dot(dsc[odd], khi_c, NN))
                        j = (first + i) // 2
                        dq_ref[:, j * LANES:(j + 1) * LANES] = (dq_pair * (ATT_HD ** -0.5)).astype(BF16)
                    acc[0] = acc[0] + _dot(dsp, qs, TN)
                    acc[1] = acc[1] + _dot(dsc, qs, TN)
                    acc[2] = acc[2] + _dot(pp.astype(BF16), dos, TN)
                    acc[3] = acc[3] + _dot(pc.astype(BF16), dos, TN)
                halves.append([a + pltpu.roll(a, ATT_HD, 1) for a in acc])
            prev = jnp.concatenate(
                [jnp.where(lo, halves[0][0], halves[1][0]), jnp.where(lo, halves[0][2], halves[1][2])], axis=1)
            cur = jnp.concatenate(
                [jnp.where(lo, halves[0][1], halves[1][1]), jnp.where(lo, halves[0][3], halves[1][3])], axis=1)
            dkv_ref[...] = (carry_ref[...] + prev).astype(BF16)
            carry_ref[...] = cur
            dsink_ref[...] += dsink

    blk = lambda n: jnp.minimum(n, nb - 1)
    return pl.pallas_call(
        body,
        name="attn_bwd",
        grid=(nb + 1,),
        in_specs=[
            pl.BlockSpec(memory_space=pltpu.SMEM),
            pl.BlockSpec((WINDOW, D_MODEL), lambda n: (blk(n), 0)),
            pl.BlockSpec((WINDOW, 2 * LANES), lambda n: (jnp.maximum(blk(n) - 1, 0), 0)),
            pl.BlockSpec((WINDOW, 2 * LANES), lambda n: (blk(n), 0)),
            pl.BlockSpec((WINDOW, D_MODEL), lambda n: (blk(n), 0)),
        ],
        out_specs=[
            pl.BlockSpec((WINDOW, D_MODEL), lambda n: (blk(n), 0)),
            pl.BlockSpec((WINDOW, 2 * LANES), lambda n: (jnp.maximum(n - 1, 0), 0)),
            pl.BlockSpec((1, LANES), lambda n: (0, 0)),
        ],
        out_shape=[
            jax.ShapeDtypeStruct((t, D_MODEL), BF16),
            jax.ShapeDtypeStruct((t, 2 * LANES), BF16),
            jax.ShapeDtypeStruct((1, LANES), F32),
        ],
        scratch_shapes=[pltpu.VMEM((WINDOW, 2 * LANES), F32)],
        compiler_params=_cparams(("arbitrary",)),
    )(sinks, q, kv, kv, dout)


def _ffn_fwd(h, norm_g, w_up, conv_w, conv_b, w_down, tag, after_up=lambda up: None):
    up = _mm_nn(h, w_up, gain=norm_g, name=f"ffn{tag}_up")
    after_up(up)
    act, c = _conv_fwd(up, conv_w, conv_b, name=f"ffn{tag}_conv")
    h_out = _mm_nn(act, w_down, res=h, name=f"ffn{tag}_down")
    return h_out, (up, act, c)


def _ffn_bwd(dh, h, norm_g, w_up, conv_w, conv_b, w_down, saved, tag, deps=()):
    up, act, c = saved
    dw_down = _mm_tn(act, dh, 1, D_MODEL, name=f"ffn{tag}_dwdown", deps=deps)
    dact = _mm_nt(dh, w_down, name=f"ffn{tag}_dact", deps=deps)
    dup, dconv_w, dconv_b = _conv_bwd(up, conv_w, c, dact, name=f"ffn{tag}_dconv")
    dw_up = _mm_tn(h, dup, N_CHIPS, CONV_COLS, stacked=True, gain=norm_g, name=f"ffn{tag}_dwup")
    dh_in, dnorm = _mm_nt(dup, w_up, stacked=True, norm_of=(h, norm_g, dh), name=f"ffn{tag}_dxn")
    return dh_in, dict(ffn_w_down=dw_down, ffn_w_up=dw_up, ffn_conv_w=dconv_w, ffn_conv_b=dconv_b, ffn_norm=dnorm)


def _local_step(x, target, w, fetch=lambda w, stage, after: w, hook=lambda point, dh, grads: ()):
    proj = _mm_nn(x, w["hg_w_in"], gain=w["hg_norm"], name="hg_in")
    o, y, states = _hgrn_fwd(proj, w["hg_lb"], w["hg_out_norm"])
    w = fetch(w, "mixer_out", y)
    fetch(w, "layer0_relay", y)
    h_a = _mm_nn(y, w["hg_w_out"], res=x, name="hg_out")
    w = fetch(w, "layer0", h_a)
    h1, ffn0 = _ffn_fwd(h_a, w["ffn_norm"][0], w["ffn_w_up"][0], w["ffn_conv_w"][0], w["ffn_conv_b"][0], w["ffn_w_down"][0], 0,
                        lambda up: fetch(w, "layer1_relay", up))
    w = fetch(w, "layer1", h1)
    kv = _mm_nn(h1, w["w_kv"], gain=w["kv_norm"], out_dtype=BF16, name="kv_proj")
    qa = _mm_nn(h1, w["attn_w_q"], gain=w["attn_norm"], out_dtype=BF16, name="attn_q")
    ao = _attn_fwd(qa, kv, w["attn_sinks"])
    h_b = _mm_nn(ao, w["attn_w_o"], res=h1, name="attn_o")
    h2, ffn1 = _ffn_fwd(h_b, w["ffn_norm"][1], w["ffn_w_up"][1], w["ffn_conv_w"][1], w["ffn_conv_b"][1], w["ffn_w_down"][1], 1)
    dh2, d_final, loss = _loss_head(h2, w["final_norm"], target)

    dh_b, g1 = _ffn_bwd(dh2, h_b, w["ffn_norm"][1], w["ffn_w_up"][1], w["ffn_conv_w"][1], w["ffn_conv_b"][1], w["ffn_w_down"][1], ffn1, 1)
    deps = hook("ffn1", dh_b, g1)
    dw_o = _mm_tn(ao, dh_b, 1, D_MODEL, name="attn_dwo", deps=deps)
    dao = _mm_nt(dh_b, w["attn_w_o"], out_dtype=BF16, name="attn_dao", deps=deps)
    dqa, dkv, dsinks = _attn_bwd(qa, kv, w["attn_sinks"], dao)
    dw_q = _mm_tn(h1, dqa, 1, D_MODEL, gain=w["attn_norm"], name="attn_dwq")
    dw_kv = _mm_tn(h1, dkv, 1, 2 * LANES, gain=w["kv_norm"], name="kv_dw")
    dh1, d_attn_norm, d_kv_norm = _mm_nt(dqa, w["attn_w_q"], norm_of=(h1, w["attn_norm"], dh_b),
                                        also=(dkv, w["w_kv"], w["kv_norm"]), name="attn_dxa")
    deps = hook("attn", dh1, dict(attn_w_o=dw_o, attn_w_q=dw_q, w_kv=dw_kv))
    dh_a, g0 = _ffn_bwd(dh1, h_a, w["ffn_norm"][0], w["ffn_w_up"][0], w["ffn_conv_w"][0], w["ffn_conv_b"][0], w["ffn_w_down"][0], ffn0, 0, deps)
    dw_out = _mm_tn(y, dh_a, 1, D_MODEL, name="hg_dwout")
    deps = hook("ffn0", dh_a, dict(g0, hg_w_out=dw_out))
    dy = _mm_nt(dh_a, w["hg_w_out"], out_dtype=BF16, name="hg_dy", deps=deps)
    dproj, dlb, d_out_norm = _hgrn_bwd(proj, w["hg_lb"], w["hg_out_norm"], o, states, dy)
    deps = hook("hgrn", dproj, None)
    dw_in = _mm_tn(x, dproj, N_CHIPS, D_MODEL, stacked=True, gain=w["hg_norm"], name="hg_dwin", deps=deps)
    deps = hook("hg_w", dproj, dict(hg_w_in=dw_in))
    dx, d_hg_norm = _mm_nt(dproj, w["hg_w_in"], stacked=True, norm_of=(x, w["hg_norm"], dh_a), name="hg_dxn", deps=deps)

    grads = dict(
        hg_norm=d_hg_norm, hg_w_in=dw_in, hg_lb=dlb, hg_out_norm=d_out_norm, hg_w_out=dw_out,
        kv_norm=d_kv_norm, w_kv=dw_kv, attn_norm=d_attn_norm, attn_w_q=dw_q, attn_sinks=dsinks, attn_w_o=dw_o,
        final_norm=d_final,
    )
    for name in g0:
        grads[name] = [g0[name], g1[name]]
    return loss, dx, grads


ANY = pl.BlockSpec(memory_space=pl.ANY)


def _place():
    x, y, c = lax.axis_index("x"), lax.axis_index("y"), lax.axis_index("c")
    chips = [(1 - x, y), (x, 1 - y), (1 - x, 1 - y)]
    return x, y, c, chips


def _rcopy(src, dst, send_sem, recv_sem, to):
    return pltpu.make_async_remote_copy(src_ref=src, dst_ref=dst, send_sem=send_sem, recv_sem=recv_sem, device_id=to, device_id_type=MESH)


HBM = pl.BlockSpec(memory_space=pltpu.HBM)
SEM = pl.BlockSpec(memory_space=pltpu.SEMAPHORE)
EFFECT = pltpu.SideEffectType.DATAFLOW_SIDE_EFFECTING


def _in_hbm(a):
    return pltpu.with_memory_space_constraint(a, pltpu.HBM)


def _place_shard(shard, place, dtype, name, deps=(), layer=None):
    r, cols = shard.shape[-2:]
    tr = _pick(r, ELEM_ROWS)
    src = pl.BlockSpec((tr, cols), lambda i, place_ref: (i, 0)) if layer is None else pl.BlockSpec((None, tr, cols), lambda i, place_ref: (layer, i, 0))

    def body(place_ref, s_ref, *rest):
        o_ref = rest[-1]
        o_ref[...] = s_ref[...].astype(o_ref.dtype)

    return pl.pallas_call(
        body,
        name=name,
        grid_spec=pltpu.PrefetchScalarGridSpec(
            num_scalar_prefetch=1,
            grid=(r // tr,),
            in_specs=[src] + _dep_specs(deps),
            out_specs=pl.BlockSpec((None, tr, cols), lambda i, place_ref: (place_ref[0], i, 0)),
        ),
        out_shape=jax.ShapeDtypeStruct((N_CHIPS, r, cols), dtype),
        compiler_params=_cparams(("parallel",)),
    )(place, shard, *deps)


def _start_copies(name, bufs, n_sem, copies):
    n = len(bufs)

    def body(*refs):
        for cp in copies(refs[:n], refs[n], refs[n + 1]):
            cp.start()
        refs[-1][...] = jnp.zeros_like(refs[-1])

    outs = pl.pallas_call(
        body,
        name=name,
        in_specs=[HBM] * n,
        out_specs=[SEM, SEM] + [HBM] * n + [pl.BlockSpec(memory_space=pltpu.VMEM)],
        out_shape=[pltpu.SemaphoreType.DMA((n_sem,)), pltpu.SemaphoreType.DMA((n_sem,))] + [pltpu.HBM(b.shape, b.dtype) for b in bufs]
        + [jax.ShapeDtypeStruct((SUBLANES, LANES), F32)],
        input_output_aliases={i: 2 + i for i in range(n)},
        compiler_params=pltpu.CompilerParams(has_side_effects=EFFECT),
    )(*[_in_hbm(b) for b in bufs])
    return outs[0], outs[1], list(outs[2:-1]), outs[-1]


def _wait_copies(name, bufs, send_sems, recv_sems, after, copies):
    n = len(bufs)

    def body(*refs):
        for cp in copies(refs[:n], refs[n], refs[n + 1]):
            cp.wait_send()
            cp.wait_recv()

    return pl.pallas_call(
        body,
        name=name,
        in_specs=[HBM] * n + [SEM, SEM, ANY],
        out_specs=[HBM] * n,
        out_shape=[pltpu.HBM(b.shape, b.dtype) for b in bufs],
        input_output_aliases={i: i for i in range(n)},
        compiler_params=pltpu.CompilerParams(has_side_effects=EFFECT),
    )(*bufs, send_sems, recv_sems, after)


def _relay_copies(name, bufs, send_sems, recv_sems, after, landed, n_sem, onward):
    n = len(bufs)

    def body(*refs):
        for cp in landed(refs[:n], refs[n], refs[n + 1]):
            cp.wait_send()
            cp.wait_recv()
        for cp in onward(refs[:n], refs[n + 3], refs[n + 4]):
            cp.start()
        refs[-1][...] = jnp.zeros_like(refs[-1])

    outs = pl.pallas_call(
        body,
        name=name,
        in_specs=[HBM] * n + [SEM, SEM, ANY],
        out_specs=[SEM, SEM] + [HBM] * n + [pl.BlockSpec(memory_space=pltpu.VMEM)],
        out_shape=[pltpu.SemaphoreType.DMA((n_sem,)), pltpu.SemaphoreType.DMA((n_sem,))] + [pltpu.HBM(b.shape, b.dtype) for b in bufs]
        + [jax.ShapeDtypeStruct((SUBLANES, LANES), F32)],
        input_output_aliases={i: 2 + i for i in range(n)},
        compiler_params=pltpu.CompilerParams(has_side_effects=EFFECT),
    )(*bufs, send_sems, recv_sems, after)
    return outs[0], outs[1], list(outs[2:-1]), outs[-1]


def _gather_half_copies(first, count, over_ici):
    def copies(refs, send_sems, recv_sems):
        x, y, c, chips = _place()
        out = []
        for i in range(count):
            h = refs[i].shape[1] // 2
            mine = pl.ds(c * h, h)
            for j, (px, py) in enumerate(chips):
                k = 3 * (first + i) + j
                slot = 2 * x + y if over_ici else 2 * px + py
                to = (px, py, c) if over_ici else (x, y, 1 - c)
                out.append(_rcopy(refs[i].at[slot, mine], refs[i].at[slot, mine], send_sems.at[k], recv_sems.at[k], to))
        return out

    return copies


def _gather_copies(first, count):
    def copies(refs, send_sems, recv_sems):
        x, y, c, chips = _place()
        me = 2 * x + y
        out = []
        for i in range(count):
            for j, (px, py) in enumerate(chips):
                k = 3 * (first + i) + j
                out.append(_rcopy(refs[i].at[me], refs[i].at[me], send_sems.at[k], recv_sems.at[k], (px, py, c)))
        return out

    return copies


def _swap_copies(n):
    def copies(refs, send_sems, recv_sems):
        x, y, c, _ = _place()
        out = []
        for i in range(n):
            h = refs[i].shape[1] // 2
            out.append(_rcopy(refs[i].at[:, pl.ds((1 - c) * h, h)], refs[n + i], send_sems.at[i], recv_sems.at[i], (x, y, 1 - c)))
        return out

    return copies


def _partial_copies(n):
    def copies(refs, send_sems, recv_sems):
        x, y, c, chips = _place()
        out = []
        for i in range(n):
            for j, (px, py) in enumerate(chips):
                out.append(_rcopy(refs[i].at[2 * px + py], refs[n + i].at[j], send_sems.at[3 * i + j], recv_sems.at[3 * i + j], (px, py, c)))
        return out

    return copies


def _share_copies(n):
    def copies(refs, send_sems, recv_sems):
        x, y, c, _ = _place()
        return [_rcopy(refs[i].at[c], refs[i].at[c], send_sems.at[i], recv_sems.at[i], (x, y, 1 - c)) for i in range(n)]

    return copies


def _small_layout(groups):
    flat = [a for g in groups for a in g]
    rows = -(-sum(a.shape[0] for a in flat) // SUBLANES) * SUBLANES
    return flat, rows, max(a.shape[1] for a in flat)


def _pack_small(groups, device):
    flat, rows, cols = _small_layout(groups)

    def body(dev_ref, *refs):
        o_ref = refs[-1]
        o_ref[...] = jnp.zeros_like(o_ref)
        r0 = 0
        for a_ref in refs[:-1]:
            r, w = a_ref.shape
            o_ref[r0:r0 + r, 0:w] = a_ref[...]
            r0 += r

    return pl.pallas_call(
        body,
        name="small_pack",
        grid_spec=pltpu.PrefetchScalarGridSpec(
            num_scalar_prefetch=1,
            grid=(1,),
            in_specs=[pl.BlockSpec(a.shape, lambda i, dev_ref: (0, 0)) for a in flat],
            out_specs=pl.BlockSpec((None, rows, cols), lambda i, dev_ref: (dev_ref[0], 0, 0)),
        ),
        out_shape=jax.ShapeDtypeStruct((N_DEV, rows, cols), F32),
        compiler_params=_cparams(("arbitrary",)),
    )(device, *flat)


def _small_copies(refs, send_sems, recv_sems):
    x, y, c, _ = _place()
    me = 4 * x + 2 * y + c
    out = []
    for k in range(1, N_DEV):
        peer = (x ^ (k >> 2), y ^ ((k >> 1) & 1), c ^ (k & 1))
        out.append(_rcopy(refs[0].at[me], refs[0].at[me], send_sems.at[k - 1], recv_sems.at[k - 1], peer))
    return out


def _sum_small(slots, groups, widths):
    out_shapes = [(sum(a.shape[0] for a in g), wd or g[0].shape[1]) for g, wd in zip(groups, widths)]

    def body(s_ref, *refs):
        outs, acc_ref = refs[:-1], refs[-1]
        acc = s_ref[0]
        for d in range(1, N_DEV):
            acc = acc + s_ref[d]
        acc_ref[...] = acc
        r0 = 0
        for o_ref in outs:
            r, w = o_ref.shape
            o_ref[...] = acc_ref[r0:r0 + r, 0:w]
            r0 += r

    vmem = pl.BlockSpec(memory_space=pltpu.VMEM)
    return pl.pallas_call(
        body,
        name="small_sum",
        in_specs=[vmem],
        out_specs=[vmem] * len(groups),
        out_shape=[jax.ShapeDtypeStruct(s, F32) for s in out_shapes],
        scratch_shapes=[pltpu.VMEM(slots.shape[1:], F32)],
        compiler_params=pltpu.CompilerParams(vmem_limit_bytes=VMEM_LIMIT_BYTES),
    )(slots)


def _adamw_small(items):
    n = len(items)

    def body(*refs):
        for i in range(n):
            w_ref, m_ref, v_ref, g_ref = refs[4 * i:4 * i + 4]
            d_ref, nm_ref, nv_ref = refs[4 * n + 3 * i:4 * n + 3 * i + 3]
            d_ref[...], nm_ref[...], nv_ref[...] = _adamw_math(w_ref[...], m_ref[...], v_ref[...], g_ref[...])

    vmem = pl.BlockSpec(memory_space=pltpu.VMEM)
    outs = pl.pallas_call(
        body,
        name="adamw_small",
        in_specs=[vmem] * (4 * n),
        out_specs=[vmem] * (3 * n),
        out_shape=[jax.ShapeDtypeStruct(it[0].shape, F32) for it in items for _ in range(3)],
        compiler_params=pltpu.CompilerParams(vmem_limit_bytes=VMEM_LIMIT_BYTES),
    )(*[a for it in items for a in it])
    return [tuple(outs[3 * i:3 * i + 3]) for i in range(n)]


class _Reduction:
    def __init__(self, tag, grads, place):
        self.tag, self.n, self.place = tag, len(grads), place
        lands = [lax.empty((N_CHIPS, g.shape[1] // 2, g.shape[2]), F32) for g in grads]
        self._start("swap", list(grads) + lands, self.n, _swap_copies(self.n))

    def _start(self, stage, bufs, n_sem, copies):
        *self.flight, self.token = _start_copies(f"rs_{stage}_start_{self.tag}", bufs, n_sem, copies)

    def _landed(self, stage, after, copies):
        send_sems, recv_sems, bufs = self.flight
        return _wait_copies(f"rs_{stage}_wait_{self.tag}", bufs, send_sems, recv_sems, after, copies)

    def to_chips(self, after):
        n = self.n
        bufs = self._landed("swap", after, _swap_copies(n))
        sums = [_add_core_halves(g, o, self.place, name=f"rs_add_core_{self.tag}_{i}") for i, (g, o) in enumerate(zip(bufs[:n], bufs[n:]))]
        self.mine = [f for f, _ in sums]
        parts = [b for _, b in sums]
        lands = [lax.empty((3,) + p.shape[1:], BF16) for p in parts]
        self._start("send", parts + lands, 3 * n, _partial_copies(n))

    def to_core(self, after):
        n = self.n
        bufs = self._landed("send", after, _partial_copies(n))
        halves = [_add_chip_partials(f, o, self.place, name=f"rs_add_chip_{self.tag}_{i}") for i, (f, o) in enumerate(zip(self.mine, bufs[n:]))]
        self._start("share", halves, n, _share_copies(n))

    def finish(self, after):
        return [b.reshape((-1,) + b.shape[2:]) for b in self._landed("share", after, _share_copies(self.n))]


ELEM_ROWS = (256, 176, 128, 64, 32, 16, 8)


def _add_core_halves(grad, got, place, name):
    s, r, cols = grad.shape
    h = r // 2
    tr = _pick(h, ELEM_ROWS)

    def body(place_ref, g_ref, o_ref, f_ref, b_ref):
        acc = g_ref[...] + o_ref[...]
        b_ref[...] = acc.astype(BF16)

        @pl.when(pl.program_id(1) == place_ref[0])
        def _():
            f_ref[...] = acc

    blk = pl.BlockSpec((None, tr, cols), lambda i, k, place_ref: (k, i, 0))
    return pl.pallas_call(
        body,
        name=name,
        grid_spec=pltpu.PrefetchScalarGridSpec(
            num_scalar_prefetch=1,
            grid=(h // tr, s),
            in_specs=[pl.BlockSpec((None, None, tr, cols), lambda i, k, place_ref: (k, place_ref[1], i, 0)), blk],
            out_specs=[pl.BlockSpec((tr, cols), lambda i, k, place_ref: (i, 0)), blk],
        ),
        out_shape=[jax.ShapeDtypeStruct((h, cols), F32), jax.ShapeDtypeStruct((s, h, cols), BF16)],
        compiler_params=_cparams(("parallel", "arbitrary")),
    )(place, grad.reshape(s, 2, h, cols), got)


def _add_chip_partials(mine, got, place, name):
    h, cols = mine.shape
    tr = _pick(h, ELEM_ROWS)

    def body(place_ref, m_ref, g_ref, o_ref):
        acc = m_ref[...]
        for j in range(3):
            acc = acc + g_ref[j].astype(F32)
        o_ref[...] = acc

    return pl.pallas_call(
        body,
        name=name,
        grid_spec=pltpu.PrefetchScalarGridSpec(
            num_scalar_prefetch=1,
            grid=(h // tr,),
            in_specs=[
                pl.BlockSpec((tr, cols), lambda i, place_ref: (i, 0)),
                pl.BlockSpec((3, tr, cols), lambda i, place_ref: (0, i, 0)),
            ],
            out_specs=pl.BlockSpec((None, tr, cols), lambda i, place_ref: (place_ref[1], i, 0)),
        ),
        out_shape=jax.ShapeDtypeStruct((2, h, cols), F32),
        compiler_params=_cparams(("parallel",)),
    )(place, mine, got)


def _adamw_math(w, m, v, g):
    nm = ADAM_B1 * m + (1.0 - ADAM_B1) * g
    nv = ADAM_B2 * v + (1.0 - ADAM_B2) * (g * g)
    m_hat = nm * (1.0 / (1.0 - ADAM_B1 ** ADAM_STEP))
    v_hat = nv * (1.0 / (1.0 - ADAM_B2 ** ADAM_STEP))
    return -ADAM_LR * (m_hat / (jnp.sqrt(v_hat) + ADAM_EPS) + ADAM_WD * w), nm, nv


def _adamw_layer(w, m, v, g, layer, prev, name):
    nl, r, cols = w.shape
    tr = _pick(r, ELEM_ROWS)

    def body(w_ref, m_ref, v_ref, g_ref, *rest):
        go_ref, d_ref, nm_ref, nv_ref = rest[-4:]
        gv = g_ref[...]
        d_ref[...], nm_ref[...], nv_ref[...] = _adamw_math(w_ref[...], m_ref[...], v_ref[...], gv)
        go_ref[...] = gv

    lay = pl.BlockSpec((None, tr, cols), lambda i: (layer, i, 0))
    return pl.pallas_call(
        body,
        name=name,
        grid=(r // tr,),
        in_specs=[lay] * 3 + [pl.BlockSpec((tr, cols), lambda i: (i, 0))] + ([ANY] * 4 if prev else []),
        out_specs=[lay] * 4,
        out_shape=[jax.ShapeDtypeStruct((nl, r, cols), F32)] * 4,
        input_output_aliases={4 + k: k for k in range(4)} if prev else {},
        compiler_params=_cparams(("parallel",)),
    )(w, m, v, g, *(prev or ()))


def _adamw(w, m, v, g, name):
    r, cols = w.shape
    tr = _pick(r, ELEM_ROWS)

    def body(w_ref, m_ref, v_ref, g_ref, d_ref, nm_ref, nv_ref):
        d_ref[...], nm_ref[...], nv_ref[...] = _adamw_math(w_ref[...], m_ref[...], v_ref[...], g_ref[...])

    blk = pl.BlockSpec((tr, cols), lambda i: (i, 0))
    return pl.pallas_call(
        body,
        name=name,
        grid=(r // tr,),
        in_specs=[blk] * 4,
        out_specs=[blk] * 3,
        out_shape=[jax.ShapeDtypeStruct((r, cols), F32)] * 3,
        compiler_params=_cparams(("parallel",)),
    )(w, m, v, g)


SMALL_COLS = 384
SMALL_ROWS = 16


def _pad_rows(flat, rows, cols):
    return jnp.pad(flat, (0, rows * cols - flat.shape[0])).reshape(rows, cols)


def kernel(x, hg_norm, hg_w_in, hg_lb_logits, hg_out_norm, hg_w_out, kv_norm, w_kv, attn_norm, attn_w_q, attn_sinks, attn_w_o, ffn_norm, ffn_w_up, ffn_conv_w, ffn_conv_b, ffn_w_down, final_norm, loss_target, m_hg_norm, m_hg_w_in, m_hg_lb_logits, m_hg_out_norm, m_hg_w_out, m_kv_norm, m_w_kv, m_attn_norm, m_attn_w_q, m_attn_sinks, m_attn_w_o, m_ffn_norm, m_ffn_w_up, m_ffn_conv_w, m_ffn_conv_b, m_ffn_w_down, m_final_norm, v_hg_norm, v_hg_w_in, v_hg_lb_logits, v_hg_out_norm, v_hg_w_out, v_kv_norm, v_w_kv, v_attn_norm, v_attn_w_q, v_attn_sinks, v_attn_w_o, v_ffn_norm, v_ffn_w_up, v_ffn_conv_w, v_ffn_conv_b, v_ffn_w_down, v_final_norm):
    wts = dict(hg_norm=hg_norm, hg_w_in=hg_w_in, hg_lb_logits=hg_lb_logits, hg_out_norm=hg_out_norm, hg_w_out=hg_w_out, kv_norm=kv_norm, w_kv=w_kv, attn_norm=attn_norm, attn_w_q=attn_w_q, attn_sinks=attn_sinks, attn_w_o=attn_w_o, ffn_norm=ffn_norm, ffn_w_up=ffn_w_up, ffn_conv_w=ffn_conv_w, ffn_conv_b=ffn_conv_b, ffn_w_down=ffn_w_down, final_norm=final_norm)
    mom1 = dict(hg_norm=m_hg_norm, hg_w_in=m_hg_w_in, hg_lb_logits=m_hg_lb_logits, hg_out_norm=m_hg_out_norm, hg_w_out=m_hg_w_out, kv_norm=m_kv_norm, w_kv=m_w_kv, attn_norm=m_attn_norm, attn_w_q=m_attn_w_q, attn_sinks=m_attn_sinks, attn_w_o=m_attn_w_o, ffn_norm=m_ffn_norm, ffn_w_up=m_ffn_w_up, ffn_conv_w=m_ffn_conv_w, ffn_conv_b=m_ffn_conv_b, ffn_w_down=m_ffn_w_down, final_norm=m_final_norm)
    mom2 = dict(hg_norm=v_hg_norm, hg_w_in=v_hg_w_in, hg_lb_logits=v_hg_lb_logits, hg_out_norm=v_hg_out_norm, hg_w_out=v_hg_w_out, kv_norm=v_kv_norm, w_kv=v_w_kv, attn_norm=v_attn_norm, attn_w_q=v_attn_w_q, attn_sinks=v_attn_sinks, attn_w_o=v_attn_w_o, ffn_norm=v_ffn_norm, ffn_w_up=v_ffn_w_up, ffn_conv_w=v_ffn_conv_w, ffn_conv_b=v_ffn_conv_b, ffn_w_down=v_ffn_w_down, final_norm=v_final_norm)
    names = list(wts)
    chip = 2 * lax.axis_index("x") + lax.axis_index("y")
    core = lax.axis_index("c")
    fs = D_FF // N_CHIPS
    ds = D_MODEL // N_CHIPS

    place_arr = jnp.stack([chip, core]).astype(jnp.int32)
    small = jnp.concatenate([hg_norm.reshape(-1), hg_lb_logits.reshape(-1), ffn_conv_w.reshape(-1)])
    n_small = small.shape[0]
    shards = [
        ("small", _pad_rows(small, SMALL_ROWS, SMALL_COLS), F32, None), ("hg_w_in", hg_w_in, BF16, 0),
        ("hg_w_out", hg_w_out, BF16, 0), ("ffn_w_up0", ffn_w_up, BF16, 0), ("ffn_w_down0", ffn_w_down, BF16, 0),
        ("w_kv", w_kv, BF16, None), ("attn_w_q", attn_w_q, BF16, 0), ("attn_w_o", attn_w_o, BF16, 0),
        ("ffn_w_up1", ffn_w_up, BF16, 1), ("ffn_w_down1", ffn_w_down, BF16, 1),
    ]
    n_first = 3
    spans = dict(layer0=(0, 2), layer1=(2, 7))

    def first_copies(refs, send_sems, recv_sems):
        return (_gather_copies(0, 1)(refs[:1], send_sems, recv_sems) + _gather_half_copies(1, 1, True)(refs[1:2], send_sems, recv_sems)
                + _gather_copies(2, 1)(refs[2:3], send_sems, recv_sems))

    placed = [_place_shard(s, place_arr, dt, name=f"place_{nm}", layer=ly) for nm, s, dt, ly in shards[:n_first]]
    first = _start_copies("gather_start_first", placed, 3 * n_first, first_copies)
    placed = [_place_shard(s, place_arr, dt, name=f"place_{nm}", deps=(first[3],), layer=ly) for nm, s, dt, ly in shards[n_first:]]
    rest = _start_copies("gather_start_rest", placed, 3 * len(placed), _gather_half_copies(0, len(placed), True))
    relayed = {}

    def fetch(w, stage, after):
        if stage == "first":
            w_in = _relay_copies("gather_first_relay", first[2][1:2], first[0], first[1], after,
                                 _gather_half_copies(1, 1, True), 3, _gather_half_copies(0, 1, False))
            got = _wait_copies("gather_wait_small", first[2][:1], first[0], first[1], w_in[3], _gather_copies(0, 1))
            got += _wait_copies("gather_wait_first", w_in[2], w_in[0], w_in[1], got[0], _gather_half_copies(0, 1, False))
        elif stage == "mixer_out":
            got = _wait_copies("gather_wait_mixer_out", first[2][2:], first[0], first[1], after, _gather_copies(2, 1))
        elif stage.endswith("_relay"):
            lo, hi = spans[stage[:-6]]
            relayed[stage[:-6]] = _relay_copies(
                f"gather_{stage}", rest[2][lo:hi], rest[0], rest[1], after,
                _gather_half_copies(lo, hi - lo, True), 3 * (hi - lo), _gather_half_copies(0, hi - lo, False))
            return w
        else:
            lo, hi = spans[stage]
            send_sems, recv_sems, bufs, _ = relayed[stage]
            got = _wait_copies(f"gather_wait_{stage}", bufs, send_sems, recv_sems, after, _gather_half_copies(0, hi - lo, False))
        w = dict(w)
        if stage == "first":
            g_small = got[0].reshape(N_CHIPS, -1)[:, :n_small]
            conv_w = g_small[:, 3 * ds:].reshape(N_CHIPS, 2, 3, fs).transpose(1, 2, 0, 3).reshape(2, 3, D_FF)
            w.update(
                hg_norm=g_small[:, :ds].reshape(1, D_MODEL),
                hg_lb=g_small[:, ds:3 * ds].reshape(N_CHIPS, 2, ds).transpose(1, 0, 2).reshape(2, D_MODEL),
                ffn_conv_w=[conv_w[0], conv_w[1]], hg_w_in=got[1],
            )
        elif stage == "mixer_out":
            w.update(hg_w_out=got[0].reshape(1, D_MODEL, D_MODEL))
        elif stage == "layer0":
            w.update(ffn_w_up=[got[0], None], ffn_w_down=[got[1].reshape(1, D_FF, D_MODEL), None])
        else:
            w.update(
                w_kv=got[0].reshape(1, D_MODEL, 2 * LANES), attn_w_q=got[1].reshape(1, D_MODEL, D_MODEL),
                attn_w_o=got[2].reshape(1, D_MODEL, D_MODEL), ffn_w_up=[w["ffn_w_up"][0], got[3]],
                ffn_w_down=[w["ffn_w_down"][0], got[4].reshape(1, D_FF, D_MODEL)],
            )
        return w

    whole = dict(
        hg_out_norm=hg_out_norm, kv_norm=kv_norm.reshape(1, D_MODEL), attn_norm=attn_norm, attn_sinks=attn_sinks.reshape(ATT_QH),
        ffn_norm=[ffn_norm[0:1], ffn_norm[1:2]], ffn_conv_b=[ffn_conv_b[0:1], ffn_conv_b[1:2]], final_norm=final_norm.reshape(1, D_MODEL),
    )
    whole = fetch(whole, "first", rest[3])

    red, layer1 = {}, {}

    def by_rows(g, rows):
        return g.reshape(N_CHIPS, rows, g.shape[2])

    def hook(point, dh, grads):
        if point == "ffn1":
            red["ffn1"] = _Reduction("ffn1", [by_rows(grads["ffn_w_down"], fs), grads["ffn_w_up"]], place_arr)
            return (red["ffn1"].token,)
        if point == "attn":
            red["ffn1"].to_chips(dh)
            layer1.update(grads)
            return (red["ffn1"].token,)
        if point == "ffn0":
            group = [by_rows(layer1["attn_w_o"], ds), by_rows(layer1["attn_w_q"], ds), by_rows(layer1["w_kv"], ds),
                     by_rows(grads["ffn_w_down"], fs), grads["ffn_w_up"], by_rows(grads["hg_w_out"], ds)]
            red["mid"] = _Reduction("mid", group, place_arr)
            return (red["mid"].token,)
        if point == "hgrn":
            red["ffn1"].to_core(dh)
            red["mid"].to_chips(dh)
            return (red["ffn1"].token, red["mid"].token)
        red["hg"] = _Reduction("hg", [grads["hg_w_in"]], place_arr)
        return (red["hg"].token,)

    loss, dx, grads = _local_step(x[0], loss_target[0], whole, fetch, hook)

    small_names = ["hg_out_norm", "attn_sinks", "kv_norm", "attn_norm", "ffn_norm", "ffn_conv_b", "final_norm", "hg_norm", "hg_lb_logits", "ffn_conv_w"]
    groups = [[loss]] + [grads[n] if isinstance(grads[n], list) else [grads[n]] for n in small_names[:-2]] + [[grads["hg_lb"]], grads["ffn_conv_w"]]
    widths = [None, None, ATT_QH] + [None] * 8
    packed = _pack_small(groups, jnp.reshape(2 * chip + core, (1,)).astype(jnp.int32))
    small_flight = _start_copies("small_start", [packed], N_DEV - 1, _small_copies)
    red["hg"].to_chips(small_flight[3])

    out_g, out_d, out_m, out_v = {}, {}, {}, {}

    def update(name, g2):
        shape = wts[name].shape
        d2, m2, v2 = _adamw(wts[name].reshape(g2.shape), mom1[name].reshape(g2.shape), mom2[name].reshape(g2.shape), g2, name=f"adamw_{name}")
        out_g[name], out_d[name], out_m[name], out_v[name] = g2.reshape(shape), d2.reshape(shape), m2.reshape(shape), v2.reshape(shape)
        return d2

    def update_layer(name, g2, layer, prev):
        res = _adamw_layer(wts[name], mom1[name], mom2[name], g2, layer, prev, name=f"adamw_{name}{layer}")
        out_g[name], out_d[name], out_m[name], out_v[name] = res
        return res

    g_down1, g_up1 = red["ffn1"].finish(red["hg"].token)
    up1 = update_layer("ffn_w_up", g_up1, 1, None)
    summed = _sum_small(_wait_copies("small_wait", small_flight[2], small_flight[0], small_flight[1], up1[3], _small_copies)[0], groups, widths)
    loss_out = summed[0][0, 0]
    small_grads = dict(zip(small_names, summed[1:]))
    small_grads["hg_norm"] = lax.dynamic_slice(small_grads["hg_norm"], (0, chip * ds), (1, ds))
    small_grads["hg_lb_logits"] = lax.dynamic_slice(small_grads["hg_lb_logits"], (0, chip * ds), (2, ds))
    small_grads["ffn_conv_w"] = lax.dynamic_slice(small_grads["ffn_conv_w"], (0, chip * fs), (2 * 3, fs))
    red["mid"].to_core(up1[1])
    down1 = update_layer("ffn_w_down", g_down1, 1, None)
    g_o, g_q, g_kv, g_down0, g_up0, g_out = red["mid"].finish(down1[1])
    update("attn_w_o", g_o)
    update("attn_w_q", g_q)
    update("w_kv", g_kv)
    update("hg_w_out", g_out)
    update_layer("ffn_w_down", g_down0, 0, down1)
    last = update_layer("ffn_w_up", g_up0, 0, up1)
    red["hg"].to_core(last[1])
    (g_in,) = red["hg"].finish(last[2])
    update("hg_w_in", g_in)

    as_2d = lambda a, n: a.reshape(small_grads[n].shape)
    updated = _adamw_small([(as_2d(wts[n], n), as_2d(mom1[n], n), as_2d(mom2[n], n), small_grads[n]) for n in small_names])
    for n, (d2, m2, v2) in zip(small_names, updated):
        shape = wts[n].shape
        out_g[n], out_d[n], out_m[n], out_v[n] = small_grads[n].reshape(shape), d2.reshape(shape), m2.reshape(shape), v2.reshape(shape)

    grad_x = dx.reshape(x.shape)
    return (loss_out, grad_x, *[out_g[n] for n in names], *[out_d[n] for n in names], *[out_m[n] for n in names], *[out_v[n] for n in names])
```

```python
import functools

import jax
import jax.numpy as jnp
from jax import lax
from jax.experimental import pallas as pl
from jax.experimental.pallas import tpu as pltpu

F32 = jnp.float32
BF16 = jnp.bfloat16
MESH = pl.DeviceIdType.MESH

EPS = 1e-6
D_MODEL = 1024
HG_HEADS = 8
HG_DK = 128
HG_CHUNK = 64
ATT_HD = 64
ATT_QH = 16
ATT_KVH = 2
ATT_GROUP = ATT_QH // ATT_KVH
WINDOW = 128
D_FF = 2816
N_CHIPS = 4
N_DEV = 8
LANES = 128
SUBLANES = 8
VMEM_LIMIT_BYTES = 56 * 1024 * 1024
NEG = -1e30
ALIBI_SLOPES = tuple(2.0 ** (-8.0 * h / ATT_QH) for h in range(1, ATT_QH + 1))

ADAM_LR = 0.001
ADAM_B1 = 0.9
ADAM_B2 = 0.999
ADAM_EPS = 1e-08
ADAM_WD = 0.01
ADAM_STEP = 10


def _cparams(sem=None):
    return pltpu.CompilerParams(dimension_semantics=sem, vmem_limit_bytes=VMEM_LIMIT_BYTES)


def _pick(n, cands):
    for c in cands:
        if n % c == 0:
            return c
    return n


def _sigmoid(x):
    return 0.5 * jnp.tanh(0.5 * x) + 0.5


def _dot(a, b, dims):
    return lax.dot_general(a, b, (dims, ((), ())), preferred_element_type=F32)


NN = ((1,), (0,))
NT = ((1,), (1,))
TN = ((0,), (0,))


MM_ROWS = 1024


def _rms_stats(xv):
    rstd = lax.rsqrt(jnp.mean(xv * xv, axis=-1, keepdims=True) + EPS)
    return xv * rstd, rstd


def _mm_operand(a_ref, gain_ref):
    if gain_ref is None:
        return a_ref[...].astype(BF16)
    return (_rms_stats(a_ref[...])[0] * gain_ref[...]).astype(BF16)


def _mm_nn(a, w, res=None, out_dtype=F32, name="mm_nn", gain=None):
    m, k = a.shape
    s, _, ns = w.shape
    tm = min(m, MM_ROWS)
    tn = _pick(ns, (1024, 1408, 512, 256, 128))
    npb = ns // tn

    def body(a_ref, w_ref, *rest):
        o_ref = rest[-1]
        acc = _dot(_mm_operand(a_ref, rest[0] if gain is not None else None), w_ref[...], NN)
        if res is not None:
            acc = acc + rest[-2][...]
        o_ref[...] = acc.astype(o_ref.dtype)

    in_specs = [
        pl.BlockSpec((tm, k), lambda i, j: (i, 0)),
        pl.BlockSpec((None, k, tn), lambda i, j: (j // npb, 0, j % npb)),
    ]
    args = [a, w]
    if gain is not None:
        in_specs.append(pl.BlockSpec((1, k), lambda i, j: (0, 0)))
        args.append(gain)
    if res is not None:
        in_specs.append(pl.BlockSpec((tm, tn), lambda i, j: (i, j)))
        args.append(res)
    return pl.pallas_call(
        body,
        name=name,
        grid=(m // tm, s * npb),
        in_specs=in_specs,
        out_specs=pl.BlockSpec((tm, tn), lambda i, j: (i, j)),
        out_shape=jax.ShapeDtypeStruct((m, s * ns), out_dtype),
        compiler_params=_cparams(("parallel", "parallel")),
    )(*args)


def _dy_spec(stacked, tm, tn, npb, row, kk):
    if stacked:
        return pl.BlockSpec((None, tm, tn), lambda *g: (kk(g) // npb, row(g), kk(g) % npb))
    return pl.BlockSpec((tm, tn), lambda *g: (row(g), kk(g)))


def _dep_specs(deps):
    return [pl.BlockSpec(d.shape, lambda *g: (0, 0)) for d in deps]


def _mm_nt(dy, w, stacked=False, out_dtype=F32, name="mm_nt", deps=(), norm_of=None, also=None):
    s, k, ns = w.shape
    m = dy.shape[1] if stacked else dy.shape[0]
    tm = min(m, MM_ROWS)
    tko = _pick(k, (1024, 1408, 512, 256))
    tn = _pick(ns, (1024, 1408, 512, 256))
    npb = ns // tn
    nk = s * npb
    fused = norm_of is not None
    assert not fused or tko == k
    assert also is None or fused

    def body(dy_ref, w_ref, *rest):
        acc_ref = rest[-1]
        i, kk = pl.program_id(0), pl.program_id(2)

        @pl.when(kk == 0)
        def _():
            acc_ref[...] = jnp.zeros_like(acc_ref)

        acc_ref[...] += _dot(dy_ref[...].astype(BF16), w_ref[...], NT)

        if not fused:
            @pl.when(kk == nk - 1)
            def _():
                rest[-2][...] = acc_ref[...].astype(rest[-2].dtype)
            return
        x_ref, g_ref, dres_ref = rest[:3]
        n_out = 3 if also is not None else 2
        dx_ref, dg_refs = rest[-1 - n_out], rest[-n_out:-1]

        @pl.when(jnp.logical_and(i == 0, kk == 0))
        def _():
            for dg_ref in dg_refs:
                dg_ref[...] = jnp.zeros_like(dg_ref)

        @pl.when(kk == nk - 1)
        def _():
            dxn = acc_ref[...]
            xhat, rstd = _rms_stats(x_ref[...])
            gd = dxn * g_ref[...]
            dg_refs[0][...] += jnp.sum(dxn * xhat, axis=0, keepdims=True)
            if also is not None:
                dy2_ref, w2_ref, g2_ref = rest[3:6]
                dxn2 = _dot(dy2_ref[...].astype(BF16), w2_ref[...], NT)
                gd = gd + dxn2 * g2_ref[...]
                dg_refs[1][...] += jnp.sum(dxn2 * xhat, axis=0, keepdims=True)
            dx_ref[...] = dres_ref[...] + rstd * (gd - xhat * jnp.mean(gd * xhat, axis=-1, keepdims=True))

    row = pl.BlockSpec((tm, tko), lambda i, j, kk: (i, j))
    vec = pl.BlockSpec((1, k), lambda i, j, kk: (0, 0))
    extra_in, extra_args = [], ()
    if fused:
        extra_in, extra_args = [row, vec, row], tuple(norm_of)
    if also is not None:
        n2 = also[0].shape[1]
        extra_in += [pl.BlockSpec((tm, n2), lambda i, j, kk: (i, 0)), pl.BlockSpec((None, k, n2), lambda i, j, kk: (0, 0, 0)), vec]
        extra_args += tuple(also)
    f32 = lambda shape: jax.ShapeDtypeStruct(shape, F32)
    return pl.pallas_call(
        body,
        name=name,
        grid=(m // tm, k // tko, nk),
        in_specs=[
            _dy_spec(stacked, tm, tn, npb, lambda g: g[0], lambda g: g[2]),
            pl.BlockSpec((None, tko, tn), lambda i, j, kk: (kk // npb, j, kk % npb)),
        ] + extra_in + _dep_specs(deps),
        out_specs=([row, vec] + ([vec] if also is not None else [])) if fused else row,
        out_shape=([f32((m, k)), f32((1, k))] + ([f32((1, k))] if also is not None else [])) if fused else jax.ShapeDtypeStruct((m, k), out_dtype),
        scratch_shapes=[pltpu.VMEM((tm, tko), F32)],
        compiler_params=_cparams(("arbitrary",) * 3 if fused else ("parallel", "parallel", "arbitrary")),
    )(dy, w, *extra_args, *deps)


def _mm_tn(a, dy, s, ns, stacked=False, name="mm_tn", deps=(), gain=None):
    m, k = a.shape
    tm = min(m, MM_ROWS)
    tk = _pick(k, (1024, 1408, 512, 256))
    tn = _pick(ns, (1024, 1408, 512, 256, 128))
    npb = ns // tn
    nm = m // tm
    assert gain is None or tk == k

    def body(a_ref, dy_ref, *rest):
        j, mm = pl.program_id(1), pl.program_id(2)
        if gain is None:
            o_ref, acc_ref = rest[-2:]
            lhs = a_ref[...].astype(BF16)
        else:
            o_ref, acc_ref, xn_ref = rest[-3:]

            @pl.when(j == 0)
            def _():
                xn_ref[mm] = _mm_operand(a_ref, rest[0])

            lhs = xn_ref[mm]

        @pl.when(mm == 0)
        def _():
            acc_ref[...] = jnp.zeros_like(acc_ref)

        acc_ref[...] += _dot(lhs, dy_ref[...].astype(BF16), TN)

        @pl.when(mm == nm - 1)
        def _():
            o_ref[...] = acc_ref[...]

    a_rows = (lambda i, j, mm: (mm, i)) if gain is None else (lambda i, j, mm: (jnp.where(j == 0, mm, 0), i))
    return pl.pallas_call(
        body,
        name=name,
        grid=(k // tk, s * npb, nm),
        in_specs=[
            pl.BlockSpec((tm, tk), a_rows),
            _dy_spec(stacked, tm, tn, npb, lambda g: g[2], lambda g: g[1]),
        ] + ([pl.BlockSpec((1, k), lambda i, j, mm: (0, 0))] if gain is not None else []) + _dep_specs(deps),
        out_specs=pl.BlockSpec((None, tk, tn), lambda i, j, mm: (j // npb, i, j % npb)),
        out_shape=jax.ShapeDtypeStruct((s, k, ns), F32),
        scratch_shapes=[pltpu.VMEM((tk, tn), F32)] + ([pltpu.VMEM((nm, tm, tk), BF16)] if gain is not None else []),
        compiler_params=_cparams(("parallel", "arbitrary", "arbitrary") if gain is not None else ("parallel", "parallel", "arbitrary")),
    )(a, dy, *(() if gain is None else (gain,)), *deps)


ROW_TILE = 512


def _loss_head(h, g, target):
    t, d = h.shape
    r = min(t, ROW_TILE)

    def body(h_ref, g_ref, t_ref, dh_ref, dg_ref, loss_ref):
        @pl.when(pl.program_id(0) == 0)
        def _():
            dg_ref[...] = jnp.zeros_like(dg_ref)
            loss_ref[...] = jnp.zeros_like(loss_ref)

        xv = h_ref[...]
        rstd = lax.rsqrt(jnp.mean(xv * xv, axis=-1, keepdims=True) + EPS)
        xhat = xv * rstd
        gv = g_ref[...]
        err = xhat * gv - t_ref[...]
        loss_ref[...] += 0.5 * jnp.sum(jnp.mean(err * err, axis=-1, keepdims=True), axis=0, keepdims=True)
        dy = err * (1.0 / d)
        gd = dy * gv
        dh_ref[...] = rstd * (gd - xhat * jnp.mean(gd * xhat, axis=-1, keepdims=True))
        dg_ref[...] += jnp.sum(dy * xhat, axis=0, keepdims=True)

    return pl.pallas_call(
        body,
        name="loss_head",
        grid=(t // r,),
        in_specs=[
            pl.BlockSpec((r, d), lambda i: (i, 0)),
            pl.BlockSpec((1, d), lambda i: (0, 0)),
            pl.BlockSpec((r, d), lambda i: (i, 0)),
        ],
        out_specs=[
            pl.BlockSpec((r, d), lambda i: (i, 0)),
            pl.BlockSpec((1, d), lambda i: (0, 0)),
            pl.BlockSpec((1, LANES), lambda i: (0, 0)),
        ],
        out_shape=[
            jax.ShapeDtypeStruct((t, d), F32),
            jax.ShapeDtypeStruct((1, d), F32),
            jax.ShapeDtypeStruct((1, LANES), F32),
        ],
        compiler_params=_cparams(("arbitrary",)),
    )(h, g, target)


CONV_ROWS = 256
CONV_COLS = 1408


def _conv_taps(x_ext, n):
    tot = x_ext.shape[0]
    g1 = pltpu.roll(x_ext, 1, 0)[tot - n:]
    g2 = pltpu.roll(x_ext, 2, 0)[tot - n:]
    return g2, g1


def _conv_fwd(up, conv_w, conv_b, name="conv_fwd"):
    t = up.shape[0]
    r = min(t, CONV_ROWS)
    tc = CONV_COLS
    ncb = D_FF // tc
    hb = r // SUBLANES

    def body(g_ref, halo_ref, v_ref, w_ref, b_ref, o_ref, c_ref):
        i = pl.program_id(1)
        g0 = g_ref[...]
        halo = halo_ref[...] * jnp.where(i > 0, 1.0, 0.0)
        g2, g1 = _conv_taps(jnp.concatenate([halo, g0], axis=0), r)
        c = b_ref[...] + w_ref[0:1, :] * g2 + w_ref[1:2, :] * g1 + w_ref[2:3, :] * g0
        c_ref[...] = c.astype(BF16)
        o_ref[...] = (c * _sigmoid(c) * v_ref[...]).astype(BF16)

    blk = pl.BlockSpec((r, tc), lambda j, i: (i, j))
    return pl.pallas_call(
        body,
        name=name,
        grid=(ncb, t // r),
        in_specs=[
            blk,
            pl.BlockSpec((SUBLANES, tc), lambda j, i: (jnp.maximum(i * hb - 1, 0), j)),
            pl.BlockSpec((r, tc), lambda j, i: (i, ncb + j)),
            pl.BlockSpec((3, tc), lambda j, i: (0, j)),
            pl.BlockSpec((1, tc), lambda j, i: (0, j)),
        ],
        out_specs=[blk, blk],
        out_shape=[jax.ShapeDtypeStruct((t, D_FF), BF16), jax.ShapeDtypeStruct((t, D_FF), BF16)],
        compiler_params=_cparams(("parallel", "parallel")),
    )(up, up, up, conv_w, conv_b)


def _conv_bwd(up, conv_w, c, dact, name="conv_bwd"):
    t = up.shape[0]
    r = min(t, CONV_ROWS)
    tc = CONV_COLS
    ncb = D_FF // tc
    nrt = t // r

    def body(g_ref, v_ref, w_ref, c_ref, da_ref, dup_ref, dw_ref, db_ref, nxt_ref):
        ii = pl.program_id(1)

        @pl.when(ii == 0)
        def _():
            nxt_ref[...] = jnp.zeros_like(nxt_ref)
            dw_ref[...] = jnp.zeros_like(dw_ref)
            db_ref[...] = jnp.zeros_like(db_ref)

        g0 = g_ref[...]
        w0, w1, w2 = w_ref[0:1, :], w_ref[1:2, :], w_ref[2:3, :]
        c = c_ref[...].astype(F32)
        sg = _sigmoid(c)
        da = da_ref[...]
        dup_ref[1] = (da * (c * sg)).astype(BF16)
        dc = da * v_ref[...] * (sg * (1.0 + c * (1.0 - sg)))
        ext = jnp.concatenate([dc, nxt_ref[...]], axis=0)
        tot = r + SUBLANES
        d1 = pltpu.roll(ext, tot - 1, 0)[:r]
        d2 = pltpu.roll(ext, tot - 2, 0)[:r]
        nxt_ref[...] = dc[:SUBLANES]
        dup_ref[0] = (w2 * dc + w1 * d1 + w0 * d2).astype(BF16)
        db_ref[...] += jnp.sum(dc, axis=0, keepdims=True)
        dw_ref[0:1, :] += jnp.sum(d2 * g0, axis=0, keepdims=True)
        dw_ref[1:2, :] += jnp.sum(d1 * g0, axis=0, keepdims=True)
        dw_ref[2:3, :] += jnp.sum(dc * g0, axis=0, keepdims=True)

    rev = lambda ii: nrt - 1 - ii
    dup, dw, db = pl.pallas_call(
        body,
        name=name,
        grid=(ncb, nrt),
        in_specs=[
            pl.BlockSpec((r, tc), lambda j, ii: (rev(ii), j)),
            pl.BlockSpec((r, tc), lambda j, ii: (rev(ii), ncb + j)),
            pl.BlockSpec((3, tc), lambda j, ii: (0, j)),
            pl.BlockSpec((r, tc), lambda j, ii: (rev(ii), j)),
            pl.BlockSpec((r, tc), lambda j, ii: (rev(ii), j)),
        ],
        out_specs=[
            pl.BlockSpec((2, None, r, tc), lambda j, ii: (0, j, rev(ii), 0)),
            pl.BlockSpec((3, tc), lambda j, ii: (0, j)),
            pl.BlockSpec((1, tc), lambda j, ii: (0, j)),
        ],
        out_shape=[
            jax.ShapeDtypeStruct((2, ncb, t, tc), BF16),
            jax.ShapeDtypeStruct((3, D_FF), F32),
            jax.ShapeDtypeStruct((1, D_FF), F32),
        ],
        scratch_shapes=[pltpu.VMEM((SUBLANES, tc), F32)],
        compiler_params=_cparams(("parallel", "arbitrary")),
    )(up, up, conv_w, c, dact)
    return dup.reshape(2 * ncb, t, tc), dw, db


def _split3(x):
    x1 = x.astype(BF16)
    r1 = x - x1.astype(F32)
    x2 = r1.astype(BF16)
    x3 = (r1 - x2.astype(F32)).astype(BF16)
    return x1, x2, x3


def _tri_dot(tri, x, dims):
    x1, x2, x3 = _split3(x)
    return _dot(tri, x1, dims) + _dot(tri, x2, dims) + _dot(tri, x3, dims)


def _lower_bound(logits_ref):
    return _sigmoid(logits_ref[0:1, :] - logits_ref[1:2, :])


def _hg_gates(qr, fr, lb):
    q = qr * _sigmoid(qr) * (HG_DK ** -0.5)
    sf = _sigmoid(fr)
    fg = lb + (1.0 - lb) * sf
    return q, sf, fg


def _hg_chunk_terms(q, fg, tril_b, low_half):
    g = jnp.log(fg)
    k = 1.0 - fg
    cum = _tri_dot(tril_b, g, NN)
    c_last = jnp.sum(g, axis=0, keepdims=True)
    c_mid = jnp.sum(jnp.where(low_half, g, 0.0), axis=0, keepdims=True)
    e_q = jnp.exp(cum - c_mid)
    e_k = jnp.exp(c_mid - cum)
    e_0 = jnp.exp(cum)
    e_l = jnp.exp(c_last - cum)
    return k, e_q, e_k, e_0, e_l, jnp.exp(c_last)


HG_BLOCK = 256


def _hg_proj_specs(rb, row):
    return [pl.BlockSpec((rb, D_MODEL), functools.partial(lambda i, k: (row(i), k), k=k)) for k in range(4)]


def _hg_consts(c):
    tril = lax.broadcasted_iota(jnp.int32, (c, c), 0) >= lax.broadcasted_iota(jnp.int32, (c, c), 1)
    low_half = lax.broadcasted_iota(jnp.int32, (c, D_MODEL), 0) < c // 2
    return tril, tril.astype(BF16), low_half


def _hgrn_fwd(proj, lb, wn):
    t = proj.shape[0]
    c = HG_CHUNK
    rb = min(t, HG_BLOCK)
    cpb = rb // c

    def body(q_ref, f_ref, i_ref, g_ref, lb_ref, wn_ref, o_ref, y_ref, st_ref, s_scr):
        @pl.when(pl.program_id(0) == 0)
        def _():
            s_scr[...] = jnp.zeros_like(s_scr)

        lb_all = _lower_bound(lb_ref)
        wnv = wn_ref[...]
        tril, tril_b, low_half = _hg_consts(c)

        def chunk(n, carry):
            rows = pl.ds(pl.multiple_of(n * c, c), c)
            q, _, fg = _hg_gates(q_ref[rows, :], f_ref[rows, :], lb_all)
            k, e_q, e_k, e_0, e_l, e_last = _hg_chunk_terms(q, fg, tril_b, low_half)
            qi, ki, q0, kl = (q * e_q).astype(BF16), (k * e_k).astype(BF16), (q * e_0).astype(BF16), (k * e_l).astype(BF16)
            v = i_ref[rows, :].astype(BF16)
            gr = g_ref[rows, :]
            gate = gr * _sigmoid(gr)
            for h in range(HG_HEADS):
                cols = slice(h * HG_DK, (h + 1) * HG_DK)
                st = s_scr[h]
                st_ref[h, n] = st
                a = jnp.where(tril, _dot(qi[:, cols], ki[:, cols], NT), 0.0)
                o = _dot(q0[:, cols], st.astype(BF16), NT) + _dot(a.astype(BF16), v[:, cols], NN)
                s_scr[h] = st * e_last[:, cols] + _dot(v[:, cols], kl[:, cols], TN)
                o_ref[rows, cols] = o
                rstd = lax.rsqrt(jnp.mean(o * o, axis=-1, keepdims=True) + EPS)
                y_ref[rows, cols] = (o * rstd * wnv * gate[:, cols]).astype(BF16)
            return carry

        lax.fori_loop(0, cpb, chunk, 0, unroll=2)

    blk = pl.BlockSpec((rb, D_MODEL), lambda i: (i, 0))
    return pl.pallas_call(
        body,
        name="hgrn_fwd",
        grid=(t // rb,),
        in_specs=_hg_proj_specs(rb, lambda i: i) + [pl.BlockSpec((2, D_MODEL), lambda i: (0, 0)), pl.BlockSpec((1, HG_DK), lambda i: (0, 0))],
        out_specs=[blk, blk, pl.BlockSpec((HG_HEADS, cpb, HG_DK, HG_DK), lambda i: (0, i, 0, 0))],
        out_shape=[
            jax.ShapeDtypeStruct((t, D_MODEL), F32),
            jax.ShapeDtypeStruct((t, D_MODEL), BF16),
            jax.ShapeDtypeStruct((HG_HEADS, t // c, HG_DK, HG_DK), F32),
        ],
        scratch_shapes=[pltpu.VMEM((HG_HEADS, HG_DK, HG_DK), F32)],
        compiler_params=_cparams(("arbitrary",)),
    )(proj, proj, proj, proj, lb, wn)


def _hgrn_bwd(proj, lb, wn, o, states, dy):
    t = proj.shape[0]
    c = HG_CHUNK
    rb = min(t, HG_BLOCK)
    cpb = rb // c
    nb = t // rb

    def body(q_ref, f_ref, i_ref, g_ref, lb_ref, wn_ref, o_ref, st_ref, dy_ref, dp_ref, dl_ref, dwn_ref, ds_scr, dlb_scr):
        step = pl.program_id(0)

        @pl.when(step == 0)
        def _():
            dwn_ref[...] = jnp.zeros_like(dwn_ref)
            ds_scr[...] = jnp.zeros_like(ds_scr)
            dlb_scr[...] = jnp.zeros_like(dlb_scr)

        lb_all = _lower_bound(lb_ref)
        wnv = wn_ref[...]
        tril, tril_b, low_half = _hg_consts(c)

        def chunk(nn, carry):
            n = cpb - 1 - nn
            rows = pl.ds(pl.multiple_of(n * c, c), c)
            qr = q_ref[rows, :]
            gr = g_ref[rows, :]
            q, sf, fg = _hg_gates(qr, f_ref[rows, :], lb_all)
            k, e_q, e_k, e_0, e_l, e_last = _hg_chunk_terms(q, fg, tril_b, low_half)
            qi, qi_lo, _ = _split3(q * e_q)
            ki, ki_lo, _ = _split3(k * e_k)
            q0 = (q * e_0).astype(BF16)
            kl = (k * e_l).astype(BF16)
            v = i_ref[rows, :].astype(BF16)
            sg = _sigmoid(gr)
            silu_g = gr * sg
            dsilu_g = sg * (1.0 + gr * (1.0 - sg))
            dqs, dks, d_lasts = [], [], []
            for h in range(HG_HEADS):
                cols = slice(h * HG_DK, (h + 1) * HG_DK)
                ov = o_ref[rows, cols]
                dyv = dy_ref[rows, cols].astype(F32)
                rstd = lax.rsqrt(jnp.mean(ov * ov, axis=-1, keepdims=True) + EPS)
                ohat = ov * rstd
                dp_ref[3, rows, cols] = (dyv * (ohat * wnv) * dsilu_g[:, cols]).astype(BF16)
                don = dyv * silu_g[:, cols]
                dwn_ref[...] += jnp.sum(don * ohat, axis=0, keepdims=True)
                gd = don * wnv
                do_b = (rstd * (gd - ohat * jnp.mean(gd * ohat, axis=-1, keepdims=True))).astype(BF16)
                st = st_ref[h, n]
                ds = ds_scr[h]
                ds_b = ds.astype(BF16)
                vh, kh = v[:, cols], k[:, cols]
                a_b = jnp.where(tril, _dot(qi[:, cols], ki[:, cols], NT), 0.0).astype(BF16)
                da_b = jnp.where(tril, _dot(do_b, vh, NT), 0.0).astype(BF16)
                dqs.append(_dot(do_b, st.astype(BF16), NN) * e_0[:, cols]
                           + (_dot(da_b, ki[:, cols], NN) + _dot(da_b, ki_lo[:, cols], NN)) * e_q[:, cols])
                dk_state = _dot(vh, ds_b, NN) * e_l[:, cols]
                dks.append((_dot(da_b, qi[:, cols], TN) + _dot(da_b, qi_lo[:, cols], TN)) * e_k[:, cols] + dk_state)
                dp_ref[2, rows, cols] = (_dot(a_b, do_b, TN) + _dot(kl[:, cols], ds_b, NT)).astype(BF16)
                ds_scr[h] = ds * e_last[:, cols] + _dot(do_b, q0[:, cols], TN)
                d_lasts.append(jnp.sum(dk_state * kh, axis=0, keepdims=True) + jnp.sum(ds * st, axis=0, keepdims=True) * e_last[:, cols])
            dq = jnp.concatenate(dqs, axis=1)
            dk = jnp.concatenate(dks, axis=1)
            dlogf = _tri_dot(tril_b, q * dq - k * dk, TN) + jnp.concatenate(d_lasts, axis=1)
            dfg = dlogf / fg - dk
            dlb_scr[...] += jnp.sum(dfg * (1.0 - sf), axis=0, keepdims=True)
            sq = _sigmoid(qr)
            dp_ref[0, rows, :] = (dq * (HG_DK ** -0.5) * (sq * (1.0 + qr * (1.0 - sq)))).astype(BF16)
            dp_ref[1, rows, :] = (dfg * (1.0 - lb_all) * sf * (1.0 - sf)).astype(BF16)
            return carry

        lax.fori_loop(0, cpb, chunk, 0, unroll=2)

        @pl.when(step == nb - 1)
        def _():
            d0 = dlb_scr[...] * lb_all * (1.0 - lb_all)
            dl_ref[0:1, :] = d0
            dl_ref[1:2, :] = -d0

    rev = lambda i: nb - 1 - i
    blk = pl.BlockSpec((rb, D_MODEL), lambda i: (rev(i), 0))
    return pl.pallas_call(
        body,
        name="hgrn_bwd",
        grid=(nb,),
        in_specs=_hg_proj_specs(rb, rev)
        + [pl.BlockSpec((2, D_MODEL), lambda i: (0, 0)), pl.BlockSpec((1, HG_DK), lambda i: (0, 0)), blk,
           pl.BlockSpec((HG_HEADS, cpb, HG_DK, HG_DK), lambda i: (0, rev(i), 0, 0)), blk],
        out_specs=[
            pl.BlockSpec((4, rb, D_MODEL), lambda i: (0, rev(i), 0)),
            pl.BlockSpec((2, D_MODEL), lambda i: (0, 0)),
            pl.BlockSpec((1, HG_DK), lambda i: (0, 0)),
        ],
        out_shape=[
            jax.ShapeDtypeStruct((4, t, D_MODEL), BF16),
            jax.ShapeDtypeStruct((2, D_MODEL), F32),
            jax.ShapeDtypeStruct((1, HG_DK), F32),
        ],
        scratch_shapes=[pltpu.VMEM((HG_HEADS, HG_DK, HG_DK), F32), pltpu.VMEM((1, D_MODEL), F32)],
        compiler_params=_cparams(("arbitrary",)),
    )(proj, proj, proj, proj, lb, wn, o, states, dy)


ATT_STACK = 8


def _att_stack(q_ref, sink_ref, first, lo, bias_p, bias_c, extra_ref=None):
    qs, bps, bcs, sinks, extras = [], [], [], None, []
    rows = lax.broadcasted_iota(jnp.int32, (ATT_STACK * WINDOW, 1), 0)
    for i in range(ATT_STACK):
        hq = first + i
        cols = slice((hq // 2) * LANES, (hq // 2 + 1) * LANES)
        sel = lo if hq % 2 == 0 else jnp.logical_not(lo)
        qp = q_ref[:, cols] * (ATT_HD ** -0.5)
        qs.append(jnp.where(sel, qp, jnp.zeros_like(qp)))
        bps.append(ALIBI_SLOPES[hq] * bias_p)
        bcs.append(ALIBI_SLOPES[hq] * bias_c)
        sinks = sink_ref[hq] if sinks is None else jnp.where(rows < i * WINDOW, sinks, sink_ref[hq])
        if extra_ref is not None:
            ep = extra_ref[:, cols]
            extras.append(jnp.where(sel, ep, jnp.zeros_like(ep)))
    cat = lambda parts: jnp.concatenate(parts, axis=0)
    return cat(qs), cat(bps), cat(bcs), sinks, (cat(extras) if extras else None)


def _att_rows(i):
    return slice(i * WINDOW, (i + 1) * WINDOW)


def _att_bias(n):
    tq = lax.broadcasted_iota(jnp.int32, (WINDOW, WINDOW), 0)
    sk = lax.broadcasted_iota(jnp.int32, (WINDOW, WINDOW), 1)
    valid_c = sk <= tq
    valid_p = (sk - tq) > jnp.where(n > 0, 0, WINDOW)
    dist_c = (tq - sk).astype(F32)
    return jnp.where(valid_p, -dist_c - float(WINDOW), NEG), jnp.where(valid_c, -dist_c, NEG)


def _att_halves(x, lo, kh):
    r = pltpu.roll(x, ATT_HD, 1)
    zero = jnp.zeros_like(x)
    if kh == 0:
        return jnp.where(lo, x, r), jnp.where(lo, x, zero), jnp.where(lo, zero, r)
    return jnp.where(lo, r, x), jnp.where(lo, r, zero), jnp.where(lo, zero, x)


def _att_probs(qm, k2p, k2c, bias_p, bias_c, sink):
    sp = _dot(qm, k2p, NT) + bias_p
    sc = _dot(qm, k2c, NT) + bias_c
    m = jnp.maximum(jnp.maximum(jnp.max(sp, axis=-1, keepdims=True), jnp.max(sc, axis=-1, keepdims=True)), sink)
    ep = jnp.exp(sp - m)
    ec = jnp.exp(sc - m)
    es = jnp.exp(sink - m)
    inv = 1.0 / (jnp.sum(ep, axis=-1, keepdims=True) + jnp.sum(ec, axis=-1, keepdims=True) + es)
    return ep * inv, ec * inv, es * inv


def _attn_fwd(q, kv, sinks):
    t = q.shape[0]
    nb = t // WINDOW

    def body(sink_ref, q_ref, kvp_ref, kvc_ref, o_ref):
        n = pl.program_id(0)
        bias_p, bias_c = _att_bias(n)
        lo = lax.broadcasted_iota(jnp.int32, (WINDOW, LANES), 1) < ATT_HD
        for kh in range(ATT_KVH):
            k2p, _, _ = _att_halves(kvp_ref[:, 0:LANES], lo, kh)
            k2c, _, _ = _att_halves(kvc_ref[:, 0:LANES], lo, kh)
            _, vlo_p, vhi_p = _att_halves(kvp_ref[:, LANES:2 * LANES], lo, kh)
            _, vlo_c, vhi_c = _att_halves(kvc_ref[:, LANES:2 * LANES], lo, kh)
            for first in range(kh * ATT_GROUP, (kh + 1) * ATT_GROUP, ATT_STACK):
                qs, bp, bc, sinks, _ = _att_stack(q_ref, sink_ref, first, lo, bias_p, bias_c)
                pp, pc, _ = _att_probs(qs, k2p, k2c, bp, bc, sinks)
                pp, pc = pp.astype(BF16), pc.astype(BF16)
                for i in range(0, ATT_STACK, 2):
                    even, odd = _att_rows(i), _att_rows(i + 1)
                    out = (_dot(pp[even], vlo_p, NN) + _dot(pc[even], vlo_c, NN)
                           + _dot(pp[odd], vhi_p, NN) + _dot(pc[odd], vhi_c, NN))
                    j = (first + i) // 2
                    o_ref[:, j * LANES:(j + 1) * LANES] = out.astype(BF16)

    return pl.pallas_call(
        body,
        name="attn_fwd",
        grid=(nb,),
        in_specs=[
            pl.BlockSpec(memory_space=pltpu.SMEM),
            pl.BlockSpec((WINDOW, D_MODEL), lambda n: (n, 0)),
            pl.BlockSpec((WINDOW, 2 * LANES), lambda n: (jnp.maximum(n - 1, 0), 0)),
            pl.BlockSpec((WINDOW, 2 * LANES), lambda n: (n, 0)),
        ],
        out_specs=pl.BlockSpec((WINDOW, D_MODEL), lambda n: (n, 0)),
        out_shape=jax.ShapeDtypeStruct((t, D_MODEL), BF16),
        compiler_params=_cparams(("parallel",)),
    )(sinks, q, kv, kv)


def _attn_bwd(q, kv, sinks, dout):
    t = q.shape[0]
    nb = t // WINDOW

    def body(sink_ref, q_ref, kvp_ref, kvc_ref, do_ref, dq_ref, dkv_ref, dsink_ref, carry_ref):
        n = pl.program_id(0)

        @pl.when(n == 0)
        def _():
            carry_ref[...] = jnp.zeros_like(carry_ref)
            dsink_ref[...] = jnp.zeros_like(dsink_ref)

        @pl.when(n == nb)
        def _():
            dkv_ref[...] = carry_ref[...].astype(BF16)

        @pl.when(n < nb)
        def _():
            bias_p, bias_c = _att_bias(n)
            lo = lax.broadcasted_iota(jnp.int32, (WINDOW, LANES), 1) < ATT_HD
            lane1 = lax.broadcasted_iota(jnp.int32, (1, LANES), 1)
            dsink = jnp.zeros((1, LANES), F32)
            halves = []
            for kh in range(ATT_KVH):
                k2p, klo_p, khi_p = _att_halves(kvp_ref[:, 0:LANES], lo, kh)
                k2c, klo_c, khi_c = _att_halves(kvc_ref[:, 0:LANES], lo, kh)
                v2p, _, _ = _att_halves(kvp_ref[:, LANES:2 * LANES], lo, kh)
                v2c, _, _ = _att_halves(kvc_ref[:, LANES:2 * LANES], lo, kh)
                acc = [jnp.zeros((WINDOW, LANES), F32) for _ in range(4)]
                for first in range(kh * ATT_GROUP, (kh + 1) * ATT_GROUP, ATT_STACK):
                    qs, bp, bc, sinks, dos = _att_stack(q_ref, sink_ref, first, lo, bias_p, bias_c, do_ref)
                    pp, pc, ps = _att_probs(qs, k2p, k2c, bp, bc, sinks)
                    dpp = _dot(dos, v2p, NT)
                    dpc = _dot(dos, v2c, NT)
                    delta = jnp.sum(pp * dpp, axis=-1, keepdims=True) + jnp.sum(pc * dpc, axis=-1, keepdims=True)
                    dsp = (pp * (dpp - delta)).astype(BF16)
                    dsc = (pc * (dpc - delta)).astype(BF16)
                    sink_term = ps * delta
                    for i in range(ATT_STACK):
                        dsink = dsink + jnp.where(lane1 == first + i, -jnp.sum(sink_term[_att_rows(i)], axis=0, keepdims=True), 0.0)
                    for i in range(0, ATT_STACK, 2):
                        even, odd = _att_rows(i), _att_rows(i + 1)
                        dq_pair = (_dot(dsp[even], klo_p, NN) + _dot(dsc[even], klo_c, NN)
                                   + _dot(dsp[odd], khi_p, NN) + _dot(dsc[odd], khi_c, NN))
                        j = (first + i) // 2
                        dq_ref[:, j * LANES:(j + 1) * LANES] = (dq_pair * (ATT_HD ** -0.5)).astype(BF16)
                    acc[0] = acc[0] + _dot(dsp, qs, TN)
                    acc[1] = acc[1] + _dot(dsc, qs, TN)
                    acc[2] = acc[2] + _dot(pp.astype(BF16), dos, TN)
                    acc[3] = acc[3] + _dot(pc.astype(BF16), dos, TN)
                halves.append([a + pltpu.roll(a, ATT_HD, 1) for a in acc])
            prev = jnp.concatenate(
                [jnp.where(lo, halves[0][0], halves[1][0]), jnp.where(lo, halves[0][2], halves[1][2])], axis=1)
            cur = jnp.concatenate(
                [jnp.where(lo, halves[0][1], halves[1][1]), jnp.where(lo, halves[0][3], halves[1][3])], axis=1)
            dkv_ref[...] = (carry_ref[...] + prev).astype(BF16)
            carry_ref[...] = cur
            dsink_ref[...] += dsink

    blk = lambda n: jnp.minimum(n, nb - 1)
    return pl.pallas_call(
        body,
        name="attn_bwd",
        grid=(nb + 1,),
        in_specs=[
            pl.BlockSpec(memory_space=pltpu.SMEM),
            pl.BlockSpec((WINDOW, D_MODEL), lambda n: (blk(n), 0)),
            pl.BlockSpec((WINDOW, 2 * LANES), lambda n: (jnp.maximum(blk(n) - 1, 0), 0)),
            pl.BlockSpec((WINDOW, 2 * LANES), lambda n: (blk(n), 0)),
            pl.BlockSpec((WINDOW, D_MODEL), lambda n: (blk(n), 0)),
        ],
        out_specs=[
            pl.BlockSpec((WINDOW, D_MODEL), lambda n: (blk(n), 0)),
            pl.BlockSpec((WINDOW, 2 * LANES), lambda n: (jnp.maximum(n - 1, 0), 0)),
            pl.BlockSpec((1, LANES), lambda n: (0, 0)),
        ],
        out_shape=[
            jax.ShapeDtypeStruct((t, D_MODEL), BF16),
            jax.ShapeDtypeStruct((t, 2 * LANES), BF16),
            jax.ShapeDtypeStruct((1, LANES), F32),
        ],
        scratch_shapes=[pltpu.VMEM((WINDOW, 2 * LANES), F32)],
        compiler_params=_cparams(("arbitrary",)),
    )(sinks, q, kv, kv, dout)


def _ffn_fwd(h, norm_g, w_up, conv_w, conv_b, w_down, tag, after_up=lambda up: None):
    up = _mm_nn(h, w_up, gain=norm_g, name=f"ffn{tag}_up")
    after_up(up)
    act, c = _conv_fwd(up, conv_w, conv_b, name=f"ffn{tag}_conv")
    h_out = _mm_nn(act, w_down, res=h, name=f"ffn{tag}_down")
    return h_out, (up, act, c)


def _ffn_bwd(dh, h, norm_g, w_up, conv_w, conv_b, w_down, saved, tag, deps=()):
    up, act, c = saved
    dw_down = _mm_tn(act, dh, 1, D_MODEL, name=f"ffn{tag}_dwdown", deps=deps)
    dact = _mm_nt(dh, w_down, name=f"ffn{tag}_dact", deps=deps)
    dup, dconv_w, dconv_b = _conv_bwd(up, conv_w, c, dact, name=f"ffn{tag}_dconv")
    dw_up = _mm_tn(h, dup, N_CHIPS, CONV_COLS, stacked=True, gain=norm_g, name=f"ffn{tag}_dwup")
    dh_in, dnorm = _mm_nt(dup, w_up, stacked=True, norm_of=(h, norm_g, dh), name=f"ffn{tag}_dxn")
    return dh_in, dict(ffn_w_down=dw_down, ffn_w_up=dw_up, ffn_conv_w=dconv_w, ffn_conv_b=dconv_b, ffn_norm=dnorm)


def _local_step(x, target, w, fetch=lambda w, stage, after: w, hook=lambda point, dh, grads: ()):
    proj = _mm_nn(x, w["hg_w_in"], gain=w["hg_norm"], name="hg_in")
    o, y, states = _hgrn_fwd(proj, w["hg_lb"], w["hg_out_norm"])
    w = fetch(w, "mixer_out", y)
    fetch(w, "layer0_relay", y)
    h_a = _mm_nn(y, w["hg_w_out"], res=x, name="hg_out")
    w = fetch(w, "layer0", h_a)
    h1, ffn0 = _ffn_fwd(h_a, w["ffn_norm"][0], w["ffn_w_up"][0], w["ffn_conv_w"][0], w["ffn_conv_b"][0], w["ffn_w_down"][0], 0,
                        lambda up: fetch(w, "layer1_relay", up))
    w = fetch(w, "layer1", h1)
    kv = _mm_nn(h1, w["w_kv"], gain=w["kv_norm"], out_dtype=BF16, name="kv_proj")
    qa = _mm_nn(h1, w["attn_w_q"], gain=w["attn_norm"], out_dtype=BF16, name="attn_q")
    ao = _attn_fwd(qa, kv, w["attn_sinks"])
    h_b = _mm_nn(ao, w["attn_w_o"], res=h1, name="attn_o")
    h2, ffn1 = _ffn_fwd(h_b, w["ffn_norm"][1], w["ffn_w_up"][1], w["ffn_conv_w"][1], w["ffn_conv_b"][1], w["ffn_w_down"][1], 1)
    dh2, d_final, loss = _loss_head(h2, w["final_norm"], target)

    dh_b, g1 = _ffn_bwd(dh2, h_b, w["ffn_norm"][1], w["ffn_w_up"][1], w["ffn_conv_w"][1], w["ffn_conv_b"][1], w["ffn_w_down"][1], ffn1, 1)
    deps = hook("ffn1", dh_b, g1)
    dw_o = _mm_tn(ao, dh_b, 1, D_MODEL, name="attn_dwo", deps=deps)
    dao = _mm_nt(dh_b, w["attn_w_o"], out_dtype=BF16, name="attn_dao", deps=deps)
    dqa, dkv, dsinks = _attn_bwd(qa, kv, w["attn_sinks"], dao)
    dw_q = _mm_tn(h1, dqa, 1, D_MODEL, gain=w["attn_norm"], name="attn_dwq")
    dw_kv = _mm_tn(h1, dkv, 1, 2 * LANES, gain=w["kv_norm"], name="kv_dw")
    dh1, d_attn_norm, d_kv_norm = _mm_nt(dqa, w["attn_w_q"], norm_of=(h1, w["attn_norm"], dh_b),
                                        also=(dkv, w["w_kv"], w["kv_norm"]), name="attn_dxa")
    deps = hook("attn", dh1, dict(attn_w_o=dw_o, attn_w_q=dw_q, w_kv=dw_kv))
    dh_a, g0 = _ffn_bwd(dh1, h_a, w["ffn_norm"][0], w["ffn_w_up"][0], w["ffn_conv_w"][0], w["ffn_conv_b"][0], w["ffn_w_down"][0], ffn0, 0, deps)
    dw_out = _mm_tn(y, dh_a, 1, D_MODEL, name="hg_dwout")
    deps = hook("ffn0", dh_a, dict(g0, hg_w_out=dw_out))
    dy = _mm_nt(dh_a, w["hg_w_out"], out_dtype=BF16, name="hg_dy", deps=deps)
    dproj, dlb, d_out_norm = _hgrn_bwd(proj, w["hg_lb"], w["hg_out_norm"], o, states, dy)
    deps = hook("hgrn", dproj, None)
    dw_in = _mm_tn(x, dproj, N_CHIPS, D_MODEL, stacked=True, gain=w["hg_norm"], name="hg_dwin", deps=deps)
    deps = hook("hg_w", dproj, dict(hg_w_in=dw_in))
    dx, d_hg_norm = _mm_nt(dproj, w["hg_w_in"], stacked=True, norm_of=(x, w["hg_norm"], dh_a), name="hg_dxn", deps=deps)

    grads = dict(
        hg_norm=d_hg_norm, hg_w_in=dw_in, hg_lb=dlb, hg_out_norm=d_out_norm, hg_w_out=dw_out,
        kv_norm=d_kv_norm, w_kv=dw_kv, attn_norm=d_attn_norm, attn_w_q=dw_q, attn_sinks=dsinks, attn_w_o=dw_o,
        final_norm=d_final,
    )
    for name in g0:
        grads[name] = [g0[name], g1[name]]
    return loss, dx, grads


ANY = pl.BlockSpec(memory_space=pl.ANY)


def _place():
    x, y, c = lax.axis_index("x"), lax.axis_index("y"), lax.axis_index("c")
    chips = [(1 - x, y), (x, 1 - y), (1 - x, 1 - y)]
    return x, y, c, chips


def _rcopy(src, dst, send_sem, recv_sem, to):
    return pltpu.make_async_remote_copy(src_ref=src, dst_ref=dst, send_sem=send_sem, recv_sem=recv_sem, device_id=to, device_id_type=MESH)


HBM = pl.BlockSpec(memory_space=pltpu.HBM)
SEM = pl.BlockSpec(memory_space=pltpu.SEMAPHORE)
EFFECT = pltpu.SideEffectType.DATAFLOW_SIDE_EFFECTING


def _in_hbm(a):
    return pltpu.with_memory_space_constraint(a, pltpu.HBM)


def _place_shard(shard, place, dtype, name, deps=(), layer=None):
    r, cols = shard.shape[-2:]
    tr = _pick(r, ELEM_ROWS)
    src = pl.BlockSpec((tr, cols), lambda i, place_ref: (i, 0)) if layer is None else pl.BlockSpec((None, tr, cols), lambda i, place_ref: (layer, i, 0))

    def body(place_ref, s_ref, *rest):
        o_ref = rest[-1]
        o_ref[...] = s_ref[...].astype(o_ref.dtype)

    return pl.pallas_call(
        body,
        name=name,
        grid_spec=pltpu.PrefetchScalarGridSpec(
            num_scalar_prefetch=1,
            grid=(r // tr,),
            in_specs=[src] + _dep_specs(deps),
            out_specs=pl.BlockSpec((None, tr, cols), lambda i, place_ref: (place_ref[0], i, 0)),
        ),
        out_shape=jax.ShapeDtypeStruct((N_CHIPS, r, cols), dtype),
        compiler_params=_cparams(("parallel",)),
    )(place, shard, *deps)


def _start_copies(name, bufs, n_sem, copies):
    n = len(bufs)

    def body(*refs):
        for cp in copies(refs[:n], refs[n], refs[n + 1]):
            cp.start()
        refs[-1][...] = jnp.zeros_like(refs[-1])

    outs = pl.pallas_call(
        body,
        name=name,
        in_specs=[HBM] * n,
        out_specs=[SEM, SEM] + [HBM] * n + [pl.BlockSpec(memory_space=pltpu.VMEM)],
        out_shape=[pltpu.SemaphoreType.DMA((n_sem,)), pltpu.SemaphoreType.DMA((n_sem,))] + [pltpu.HBM(b.shape, b.dtype) for b in bufs]
        + [jax.ShapeDtypeStruct((SUBLANES, LANES), F32)],
        input_output_aliases={i: 2 + i for i in range(n)},
        compiler_params=pltpu.CompilerParams(has_side_effects=EFFECT),
    )(*[_in_hbm(b) for b in bufs])
    return outs[0], outs[1], list(outs[2:-1]), outs[-1]


def _wait_copies(name, bufs, send_sems, recv_sems, after, copies):
    n = len(bufs)

    def body(*refs):
        for cp in copies(refs[:n], refs[n], refs[n + 1]):
            cp.wait_send()
            cp.wait_recv()

    return pl.pallas_call(
        body,
        name=name,
        in_specs=[HBM] * n + [SEM, SEM, ANY],
        out_specs=[HBM] * n,
        out_shape=[pltpu.HBM(b.shape, b.dtype) for b in bufs],
        input_output_aliases={i: i for i in range(n)},
        compiler_params=pltpu.CompilerParams(has_side_effects=EFFECT),
    )(*bufs, send_sems, recv_sems, after)


def _relay_copies(name, bufs, send_sems, recv_sems, after, landed, n_sem, onward):
    n = len(bufs)

    def body(*refs):
        for cp in landed(refs[:n], refs[n], refs[n + 1]):
            cp.wait_send()
            cp.wait_recv()
        for cp in onward(refs[:n], refs[n + 3], refs[n + 4]):
            cp.start()
        refs[-1][...] = jnp.zeros_like(refs[-1])

    outs = pl.pallas_call(
        body,
        name=name,
        in_specs=[HBM] * n + [SEM, SEM, ANY],
        out_specs=[SEM, SEM] + [HBM] * n + [pl.BlockSpec(memory_space=pltpu.VMEM)],
        out_shape=[pltpu.SemaphoreType.DMA((n_sem,)), pltpu.SemaphoreType.DMA((n_sem,))] + [pltpu.HBM(b.shape, b.dtype) for b in bufs]
        + [jax.ShapeDtypeStruct((SUBLANES, LANES), F32)],
        input_output_aliases={i: 2 + i for i in range(n)},
        compiler_params=pltpu.CompilerParams(has_side_effects=EFFECT),
    )(*bufs, send_sems, recv_sems, after)
    return outs[0], outs[1], list(outs[2:-1]), outs[-1]


def _gather_half_copies(first, count, over_ici):
    def copies(refs, send_sems, recv_sems):
        x, y, c, chips = _place()
        out = []
        for i in range(count):
            h = refs[i].shape[1] // 2
            mine = pl.ds(c * h, h)
            for j, (px, py) in enumerate(chips):
                k = 3 * (first + i) + j
                slot = 2 * x + y if over_ici else 2 * px + py
                to = (px, py, c) if over_ici else (x, y, 1 - c)
                out.append(_rcopy(refs[i].at[slot, mine], refs[i].at[slot, mine], send_sems.at[k], recv_sems.at[k], to))
        return out

    return copies


def _gather_copies(first, count):
    def copies(refs, send_sems, recv_sems):
        x, y, c, chips = _place()
        me = 2 * x + y
        out = []
        for i in range(count):
            for j, (px, py) in enumerate(chips):
                k = 3 * (first + i) + j
                out.append(_rcopy(refs[i].at[me], refs[i].at[me], send_sems.at[k], recv_sems.at[k], (px, py, c)))
        return out

    return copies


def _swap_copies(n):
    def copies(refs, send_sems, recv_sems):
        x, y, c, _ = _place()
        out = []
        for i in range(n):
            h = refs[i].shape[1] // 2
            out.append(_rcopy(refs[i].at[:, pl.ds((1 - c) * h, h)], refs[n + i], send_sems.at[i], recv_sems.at[i], (x, y, 1 - c)))
        return out

    return copies


def _partial_copies(n):
    def copies(refs, send_sems, recv_sems):
        x, y, c, chips = _place()
        out = []
        for i in range(n):
            for j, (px, py) in enumerate(chips):
                out.append(_rcopy(refs[i].at[2 * px + py], refs[n + i].at[j], send_sems.at[3 * i + j], recv_sems.at[3 * i + j], (px, py, c)))
        return out

    return copies


def _share_copies(n):
    def copies(refs, send_sems, recv_sems):
        x, y, c, _ = _place()
        return [_rcopy(refs[i].at[c], refs[i].at[c], send_sems.at[i], recv_sems.at[i], (x, y, 1 - c)) for i in range(n)]

    return copies


def _small_layout(groups):
    flat = [a for g in groups for a in g]
    rows = -(-sum(a.shape[0] for a in flat) // SUBLANES) * SUBLANES
    return flat, rows, max(a.shape[1] for a in flat)


def _pack_small(groups, device):
    flat, rows, cols = _small_layout(groups)

    def body(dev_ref, *refs):
        o_ref = refs[-1]
        o_ref[...] = jnp.zeros_like(o_ref)
        r0 = 0
        for a_ref in refs[:-1]:
            r, w = a_ref.shape
            o_ref[r0:r0 + r, 0:w] = a_ref[...]
            r0 += r

    return pl.pallas_call(
        body,
        name="small_pack",
        grid_spec=pltpu.PrefetchScalarGridSpec(
            num_scalar_prefetch=1,
            grid=(1,),
            in_specs=[pl.BlockSpec(a.shape, lambda i, dev_ref: (0, 0)) for a in flat],
            out_specs=pl.BlockSpec((None, rows, cols), lambda i, dev_ref: (dev_ref[0], 0, 0)),
        ),
        out_shape=jax.ShapeDtypeStruct((N_DEV, rows, cols), F32),
        compiler_params=_cparams(("arbitrary",)),
    )(device, *flat)


def _small_copies(refs, send_sems, recv_sems):
    x, y, c, _ = _place()
    me = 4 * x + 2 * y + c
    out = []
    for k in range(1, N_DEV):
        peer = (x ^ (k >> 2), y ^ ((k >> 1) & 1), c ^ (k & 1))
        out.append(_rcopy(refs[0].at[me], refs[0].at[me], send_sems.at[k - 1], recv_sems.at[k - 1], peer))
    return out


def _sum_small(slots, groups, widths):
    out_shapes = [(sum(a.shape[0] for a in g), wd or g[0].shape[1]) for g, wd in zip(groups, widths)]

    def body(s_ref, *refs):
        outs, acc_ref = refs[:-1], refs[-1]
        acc = s_ref[0]
        for d in range(1, N_DEV):
            acc = acc + s_ref[d]
        acc_ref[...] = acc
        r0 = 0
        for o_ref in outs:
            r, w = o_ref.shape
            o_ref[...] = acc_ref[r0:r0 + r, 0:w]
            r0 += r

    vmem = pl.BlockSpec(memory_space=pltpu.VMEM)
    return pl.pallas_call(
        body,
        name="small_sum",
        in_specs=[vmem],
        out_specs=[vmem] * len(groups),
        out_shape=[jax.ShapeDtypeStruct(s, F32) for s in out_shapes],
        scratch_shapes=[pltpu.VMEM(slots.shape[1:], F32)],
        compiler_params=pltpu.CompilerParams(vmem_limit_bytes=VMEM_LIMIT_BYTES),
    )(slots)


def _adamw_small(items):
    n = len(items)

    def body(*refs):
        for i in range(n):
            w_ref, m_ref, v_ref, g_ref = refs[4 * i:4 * i + 4]
            d_ref, nm_ref, nv_ref = refs[4 * n + 3 * i:4 * n + 3 * i + 3]
            d_ref[...], nm_ref[...], nv_ref[...] = _adamw_math(w_ref[...], m_ref[...], v_ref[...], g_ref[...])

    vmem = pl.BlockSpec(memory_space=pltpu.VMEM)
    outs = pl.pallas_call(
        body,
        name="adamw_small",
        in_specs=[vmem] * (4 * n),
        out_specs=[vmem] * (3 * n),
        out_shape=[jax.ShapeDtypeStruct(it[0].shape, F32) for it in items for _ in range(3)],
        compiler_params=pltpu.CompilerParams(vmem_limit_bytes=VMEM_LIMIT_BYTES),
    )(*[a for it in items for a in it])
    return [tuple(outs[3 * i:3 * i + 3]) for i in range(n)]


class _Reduction:
    def __init__(self, tag, grads, place):
        self.tag, self.n, self.place = tag, len(grads), place
        lands = [lax.empty((N_CHIPS, g.shape[1] // 2, g.shape[2]), F32) for g in grads]
        self._start("swap", list(grads) + lands, self.n, _swap_copies(self.n))

    def _start(self, stage, bufs, n_sem, copies):
        *self.flight, self.token = _start_copies(f"rs_{stage}_start_{self.tag}", bufs, n_sem, copies)

    def _landed(self, stage, after, copies):
        send_sems, recv_sems, bufs = self.flight
        return _wait_copies(f"rs_{stage}_wait_{self.tag}", bufs, send_sems, recv_sems, after, copies)

    def to_chips(self, after):
        n = self.n
        bufs = self._landed("swap", after, _swap_copies(n))
        sums = [_add_core_halves(g, o, self.place, name=f"rs_add_core_{self.tag}_{i}") for i, (g, o) in enumerate(zip(bufs[:n], bufs[n:]))]
        self.mine = [f for f, _ in sums]
        parts = [b for _, b in sums]
        lands = [lax.empty((3,) + p.shape[1:], BF16) for p in parts]
        self._start("send", parts + lands, 3 * n, _partial_copies(n))

    def to_core(self, after):
        n = self.n
        bufs = self._landed("send", after, _partial_copies(n))
        halves = [_add_chip_partials(f, o, self.place, name=f"rs_add_chip_{self.tag}_{i}") for i, (f, o) in enumerate(zip(self.mine, bufs[n:]))]
        self._start("share", halves, n, _share_copies(n))

    def finish(self, after):
        return [b.reshape((-1,) + b.shape[2:]) for b in self._landed("share", after, _share_copies(self.n))]


ELEM_ROWS = (256, 176, 128, 64, 32, 16, 8)


def _add_core_halves(grad, got, place, name):
    s, r, cols = grad.shape
    h = r // 2
    tr = _pick(h, ELEM_ROWS)

    def body(place_ref, g_ref, o_ref, f_ref, b_ref):
        acc = g_ref[...] + o_ref[...]
        b_ref[...] = acc.astype(BF16)

        @pl.when(pl.program_id(1) == place_ref[0])
        def _():
            f_ref[...] = acc

    blk = pl.BlockSpec((None, tr, cols), lambda i, k, place_ref: (k, i, 0))
    return pl.pallas_call(
        body,
        name=name,
        grid_spec=pltpu.PrefetchScalarGridSpec(
            num_scalar_prefetch=1,
            grid=(h // tr, s),
            in_specs=[pl.BlockSpec((None, None, tr, cols), lambda i, k, place_ref: (k, place_ref[1], i, 0)), blk],
            out_specs=[pl.BlockSpec((tr, cols), lambda i, k, place_ref: (i, 0)), blk],
        ),
        out_shape=[jax.ShapeDtypeStruct((h, cols), F32), jax.ShapeDtypeStruct((s, h, cols), BF16)],
        compiler_params=_cparams(("parallel", "arbitrary")),
    )(place, grad.reshape(s, 2, h, cols), got)


def _add_chip_partials(mine, got, place, name):
    h, cols = mine.shape
    tr = _pick(h, ELEM_ROWS)

    def body(place_ref, m_ref, g_ref, o_ref):
        acc = m_ref[...]
        for j in range(3):
            acc = acc + g_ref[j].astype(F32)
        o_ref[...] = acc

    return pl.pallas_call(
        body,
        name=name,
        grid_spec=pltpu.PrefetchScalarGridSpec(
            num_scalar_prefetch=1,
            grid=(h // tr,),
            in_specs=[
                pl.BlockSpec((tr, cols), lambda i, place_ref: (i, 0)),
                pl.BlockSpec((3, tr, cols), lambda i, place_ref: (0, i, 0)),
            ],
            out_specs=pl.BlockSpec((None, tr, cols), lambda i, place_ref: (place_ref[1], i, 0)),
        ),
        out_shape=jax.ShapeDtypeStruct((2, h, cols), F32),
        compiler_params=_cparams(("parallel",)),
    )(place, mine, got)


def _adamw_math(w, m, v, g):
    nm = ADAM_B1 * m + (1.0 - ADAM_B1) * g
    nv = ADAM_B2 * v + (1.0 - ADAM_B2) * (g * g)
    m_hat = nm * (1.0 / (1.0 - ADAM_B1 ** ADAM_STEP))
    v_hat = nv * (1.0 / (1.0 - ADAM_B2 ** ADAM_STEP))
    return -ADAM_LR * (m_hat / (jnp.sqrt(v_hat) + ADAM_EPS) + ADAM_WD * w), nm, nv


def _adamw_layer(w, m, v, g, layer, prev, name):
    nl, r, cols = w.shape
    tr = _pick(r, ELEM_ROWS)

    def body(w_ref, m_ref, v_ref, g_ref, *rest):
        go_ref, d_ref, nm_ref, nv_ref = rest[-4:]
        gv = g_ref[...]
        d_ref[...], nm_ref[...], nv_ref[...] = _adamw_math(w_ref[...], m_ref[...], v_ref[...], gv)
        go_ref[...] = gv

    lay = pl.BlockSpec((None, tr, cols), lambda i: (layer, i, 0))
    return pl.pallas_call(
        body,
        name=name,
        grid=(r // tr,),
        in_specs=[lay] * 3 + [pl.BlockSpec((tr, cols), lambda i: (i, 0))] + ([ANY] * 4 if prev else []),
        out_specs=[lay] * 4,
        out_shape=[jax.ShapeDtypeStruct((nl, r, cols), F32)] * 4,
        input_output_aliases={4 + k: k for k in range(4)} if prev else {},
        compiler_params=_cparams(("parallel",)),
    )(w, m, v, g, *(prev or ()))


def _adamw(w, m, v, g, name):
    r, cols = w.shape
    tr = _pick(r, ELEM_ROWS)

    def body(w_ref, m_ref, v_ref, g_ref, d_ref, nm_ref, nv_ref):
        d_ref[...], nm_ref[...], nv_ref[...] = _adamw_math(w_ref[...], m_ref[...], v_ref[...], g_ref[...])

    blk = pl.BlockSpec((tr, cols), lambda i: (i, 0))
    return pl.pallas_call(
        body,
        name=name,
        grid=(r // tr,),
        in_specs=[blk] * 4,
        out_specs=[blk] * 3,
        out_shape=[jax.ShapeDtypeStruct((r, cols), F32)] * 3,
        compiler_params=_cparams(("parallel",)),
    )(w, m, v, g)


SMALL_COLS = 384
SMALL_ROWS = 16


def _pad_rows(flat, rows, cols):
    return jnp.pad(flat, (0, rows * cols - flat.shape[0])).reshape(rows, cols)


def kernel(x, hg_norm, hg_w_in, hg_lb_logits, hg_out_norm, hg_w_out, kv_norm, w_kv, attn_norm, attn_w_q, attn_sinks, attn_w_o, ffn_norm, ffn_w_up, ffn_conv_w, ffn_conv_b, ffn_w_down, final_norm, loss_target, m_hg_norm, m_hg_w_in, m_hg_lb_logits, m_hg_out_norm, m_hg_w_out, m_kv_norm, m_w_kv, m_attn_norm, m_attn_w_q, m_attn_sinks, m_attn_w_o, m_ffn_norm, m_ffn_w_up, m_ffn_conv_w, m_ffn_conv_b, m_ffn_w_down, m_final_norm, v_hg_norm, v_hg_w_in, v_hg_lb_logits, v_hg_out_norm, v_hg_w_out, v_kv_norm, v_w_kv, v_attn_norm, v_attn_w_q, v_attn_sinks, v_attn_w_o, v_ffn_norm, v_ffn_w_up, v_ffn_conv_w, v_ffn_conv_b, v_ffn_w_down, v_final_norm):
    wts = dict(hg_norm=hg_norm, hg_w_in=hg_w_in, hg_lb_logits=hg_lb_logits, hg_out_norm=hg_out_norm, hg_w_out=hg_w_out, kv_norm=kv_norm, w_kv=w_kv, attn_norm=attn_norm, attn_w_q=attn_w_q, attn_sinks=attn_sinks, attn_w_o=attn_w_o, ffn_norm=ffn_norm, ffn_w_up=ffn_w_up, ffn_conv_w=ffn_conv_w, ffn_conv_b=ffn_conv_b, ffn_w_down=ffn_w_down, final_norm=final_norm)
    mom1 = dict(hg_norm=m_hg_norm, hg_w_in=m_hg_w_in, hg_lb_logits=m_hg_lb_logits, hg_out_norm=m_hg_out_norm, hg_w_out=m_hg_w_out, kv_norm=m_kv_norm, w_kv=m_w_kv, attn_norm=m_attn_norm, attn_w_q=m_attn_w_q, attn_sinks=m_attn_sinks, attn_w_o=m_attn_w_o, ffn_norm=m_ffn_norm, ffn_w_up=m_ffn_w_up, ffn_conv_w=m_ffn_conv_w, ffn_conv_b=m_ffn_conv_b, ffn_w_down=m_ffn_w_down, final_norm=m_final_norm)
    mom2 = dict(hg_norm=v_hg_norm, hg_w_in=v_hg_w_in, hg_lb_logits=v_hg_lb_logits, hg_out_norm=v_hg_out_norm, hg_w_out=v_hg_w_out, kv_norm=v_kv_norm, w_kv=v_w_kv, attn_norm=v_attn_norm, attn_w_q=v_attn_w_q, attn_sinks=v_attn_sinks, attn_w_o=v_attn_w_o, ffn_norm=v_ffn_norm, ffn_w_up=v_ffn_w_up, ffn_conv_w=v_ffn_conv_w, ffn_conv_b=v_ffn_conv_b, ffn_w_down=v_ffn_w_down, final_norm=v_final_norm)
    names = list(wts)
    chip = 2 * lax.axis_index("x") + lax.axis_index("y")
    core = lax.axis_index("c")
    fs = D_FF // N_CHIPS
    ds = D_MODEL // N_CHIPS

    place_arr = jnp.stack([chip, core]).astype(jnp.int32)
    small = jnp.concatenate([hg_norm.reshape(-1), hg_lb_logits.reshape(-1), ffn_conv_w.reshape(-1)])
    n_small = small.shape[0]
    shards = [
        ("small", _pad_rows(small, SMALL_ROWS, SMALL_COLS), F32, None), ("hg_w_in", hg_w_in, BF16, 0),
        ("hg_w_out", hg_w_out, BF16, 0), ("ffn_w_up0", ffn_w_up, BF16, 0), ("ffn_w_down0", ffn_w_down, BF16, 0),
        ("w_kv", w_kv, BF16, None), ("attn_w_q", attn_w_q, BF16, 0), ("attn_w_o", attn_w_o, BF16, 0),
        ("ffn_w_up1", ffn_w_up, BF16, 1), ("ffn_w_down1", ffn_w_down, BF16, 1),
    ]
    n_first = 3
    spans = dict(layer0=(0, 2), layer1=(2, 7))

    def first_copies(refs, send_sems, recv_sems):
        return (_gather_copies(0, 1)(refs[:1], send_sems, recv_sems) + _gather_half_copies(1, 1, True)(refs[1:2], send_sems, recv_sems)
                + _gather_copies(2, 1)(refs[2:3], send_sems, recv_sems))

    placed = [_place_shard(s, place_arr, dt, name=f"place_{nm}", layer=ly) for nm, s, dt, ly in shards[:n_first]]
    first = _start_copies("gather_start_first", placed, 3 * n_first, first_copies)
    placed = [_place_shard(s, place_arr, dt, name=f"place_{nm}", deps=(first[3],), layer=ly) for nm, s, dt, ly in shards[n_first:]]
    rest = _start_copies("gather_start_rest", placed, 3 * len(placed), _gather_half_copies(0, len(placed), True))
    relayed = {}

    def fetch(w, stage, after):
        if stage == "first":
            w_in = _relay_copies("gather_first_relay", first[2][1:2], first[0], first[1], after,
                                 _gather_half_copies(1, 1, True), 3, _gather_half_copies(0, 1, False))
            got = _wait_copies("gather_wait_small", first[2][:1], first[0], first[1], w_in[3], _gather_copies(0, 1))
            got += _wait_copies("gather_wait_first", w_in[2], w_in[0], w_in[1], got[0], _gather_half_copies(0, 1, False))
        elif stage == "mixer_out":
            got = _wait_copies("gather_wait_mixer_out", first[2][2:], first[0], first[1], after, _gather_copies(2, 1))
        elif stage.endswith("_relay"):
            lo, hi = spans[stage[:-6]]
            relayed[stage[:-6]] = _relay_copies(
                f"gather_{stage}", rest[2][lo:hi], rest[0], rest[1], after,
                _gather_half_copies(lo, hi - lo, True), 3 * (hi - lo), _gather_half_copies(0, hi - lo, False))
            return w
        else:
            lo, hi = spans[stage]
            send_sems, recv_sems, bufs, _ = relayed[stage]
            got = _wait_copies(f"gather_wait_{stage}", bufs, send_sems, recv_sems, after, _gather_half_copies(0, hi - lo, False))
        w = dict(w)
        if stage == "first":
            g_small = got[0].reshape(N_CHIPS, -1)[:, :n_small]
            conv_w = g_small[:, 3 * ds:].reshape(N_CHIPS, 2, 3, fs).transpose(1, 2, 0, 3).reshape(2, 3, D_FF)
            w.update(
                hg_norm=g_small[:, :ds].reshape(1, D_MODEL),
                hg_lb=g_small[:, ds:3 * ds].reshape(N_CHIPS, 2, ds).transpose(1, 0, 2).reshape(2, D_MODEL),
                ffn_conv_w=[conv_w[0], conv_w[1]], hg_w_in=got[1],
            )
        elif stage == "mixer_out":
            w.update(hg_w_out=got[0].reshape(1, D_MODEL, D_MODEL))
        elif stage == "layer0":
            w.update(ffn_w_up=[got[0], None], ffn_w_down=[got[1].reshape(1, D_FF, D_MODEL), None])
        else:
            w.update(
                w_kv=got[0].reshape(1, D_MODEL, 2 * LANES), attn_w_q=got[1].reshape(1, D_MODEL, D_MODEL),
                attn_w_o=got[2].reshape(1, D_MODEL, D_MODEL), ffn_w_up=[w["ffn_w_up"][0], got[3]],
                ffn_w_down=[w["ffn_w_down"][0], got[4].reshape(1, D_FF, D_MODEL)],
            )
        return w

    whole = dict(
        hg_out_norm=hg_out_norm, kv_norm=kv_norm.reshape(1, D_MODEL), attn_norm=attn_norm, attn_sinks=attn_sinks.reshape(ATT_QH),
        ffn_norm=[ffn_norm[0:1], ffn_norm[1:2]], ffn_conv_b=[ffn_conv_b[0:1], ffn_conv_b[1:2]], final_norm=final_norm.reshape(1, D_MODEL),
    )
    whole = fetch(whole, "first", rest[3])

    red, layer1 = {}, {}

    def by_rows(g, rows):
        return g.reshape(N_CHIPS, rows, g.shape[2])

    def hook(point, dh, grads):
        if point == "ffn1":
            red["ffn1"] = _Reduction("ffn1", [by_rows(grads["ffn_w_down"], fs), grads["ffn_w_up"]], place_arr)
            return (red["ffn1"].token,)
        if point == "attn":
            red["ffn1"].to_chips(dh)
            layer1.update(grads)
            return (red["ffn1"].token,)
        if point == "ffn0":
            group = [by_rows(layer1["attn_w_o"], ds), by_rows(layer1["attn_w_q"], ds), by_rows(layer1["w_kv"], ds),
                     by_rows(grads["ffn_w_down"], fs), grads["ffn_w_up"], by_rows(grads["hg_w_out"], ds)]
            red["mid"] = _Reduction("mid", group, place_arr)
            return (red["mid"].token,)
        if point == "hgrn":
            red["ffn1"].to_core(dh)
            red["mid"].to_chips(dh)
            return (red["ffn1"].token, red["mid"].token)
        red["hg"] = _Reduction("hg", [grads["hg_w_in"]], place_arr)
        return (red["hg"].token,)

    loss, dx, grads = _local_step(x[0], loss_target[0], whole, fetch, hook)

    small_names = ["hg_out_norm", "attn_sinks", "kv_norm", "attn_norm", "ffn_norm", "ffn_conv_b", "final_norm", "hg_norm", "hg_lb_logits", "ffn_conv_w"]
    groups = [[loss]] + [grads[n] if isinstance(grads[n], list) else [grads[n]] for n in small_names[:-2]] + [[grads["hg_lb"]], grads["ffn_conv_w"]]
    widths = [None, None, ATT_QH] + [None] * 8
    packed = _pack_small(groups, jnp.reshape(2 * chip + core, (1,)).astype(jnp.int32))
    small_flight = _start_copies("small_start", [packed], N_DEV - 1, _small_copies)
    red["hg"].to_chips(small_flight[3])

    out_g, out_d, out_m, out_v = {}, {}, {}, {}

    def update(name, g2):
        shape = wts[name].shape
        d2, m2, v2 = _adamw(wts[name].reshape(g2.shape), mom1[name].reshape(g2.shape), mom2[name].reshape(g2.shape), g2, name=f"adamw_{name}")
        out_g[name], out_d[name], out_m[name], out_v[name] = g2.reshape(shape), d2.reshape(shape), m2.reshape(shape), v2.reshape(shape)
        return d2

    def update_layer(name, g2, layer, prev):
        res = _adamw_layer(wts[name], mom1[name], mom2[name], g2, layer, prev, name=f"adamw_{name}{layer}")
        out_g[name], out_d[name], out_m[name], out_v[name] = res
        return res

    g_down1, g_up1 = red["ffn1"].finish(red["hg"].token)
    up1 = update_layer("ffn_w_up", g_up1, 1, None)
    summed = _sum_small(_wait_copies("small_wait", small_flight[2], small_flight[0], small_flight[1], up1[3], _small_copies)[0], groups, widths)
    loss_out = summed[0][0, 0]
    small_grads = dict(zip(small_names, summed[1:]))
    small_grads["hg_norm"] = lax.dynamic_slice(small_grads["hg_norm"], (0, chip * ds), (1, ds))
    small_grads["hg_lb_logits"] = lax.dynamic_slice(small_grads["hg_lb_logits"], (0, chip * ds), (2, ds))
    small_grads["ffn_conv_w"] = lax.dynamic_slice(small_grads["ffn_conv_w"], (0, chip * fs), (2 * 3, fs))
    red["mid"].to_core(up1[1])
    down1 = update_layer("ffn_w_down", g_down1, 1, None)
    g_o, g_q, g_kv, g_down0, g_up0, g_out = red["mid"].finish(down1[1])
    update("attn_w_o", g_o)
    update("attn_w_q", g_q)
    update("w_kv", g_kv)
    update("hg_w_out", g_out)
    update_layer("ffn_w_down", g_down0, 0, down1)
    last = update_layer("ffn_w_up", g_up0, 0, up1)
    red["hg"].to_core(last[1])
    (g_in,) = red["hg"].finish(last[2])
    update("hg_w_in", g_in)

    as_2d = lambda a, n: a.reshape(small_grads[n].shape)
    updated = _adamw_small([(as_2d(wts[n], n), as_2d(mom1[n], n), as_2d(mom2[n], n), small_grads[n]) for n in small_names])
    for n, (d2, m2, v2) in zip(small_names, updated):
        shape = wts[n].shape
        out_g[n], out_d[n], out_m[n], out_v[n] = small_grads[n].reshape(shape), d2.reshape(shape), m2.reshape(shape), v2.reshape(shape)

    grad_x = dx.reshape(x.shape)
    return (loss_out, grad_x, *[out_g[n] for n in names], *[out_d[n] for n in names], *[out_m[n] for n in names], *[out_v[n] for n in names])
```

```python
import functools

import jax
import jax.numpy as jnp
from jax import lax
from jax.experimental import pallas as pl
from jax.experimental.pallas import tpu as pltpu

F32 = jnp.float32
BF16 = jnp.bfloat16
MESH = pl.DeviceIdType.MESH

EPS = 1e-6
D_MODEL = 1024
HG_HEADS = 8
HG_DK = 128
HG_CHUNK = 64
ATT_HD = 64
ATT_QH = 16
ATT_KVH = 2
ATT_GROUP = ATT_QH // ATT_KVH
WINDOW = 128
D_FF = 2816
N_CHIPS = 4
N_DEV = 8
LANES = 128
SUBLANES = 8
VMEM_LIMIT_BYTES = 56 * 1024 * 1024
NEG = -1e30
ALIBI_SLOPES = tuple(2.0 ** (-8.0 * h / ATT_QH) for h in range(1, ATT_QH + 1))

ADAM_LR = 0.001
ADAM_B1 = 0.9
ADAM_B2 = 0.999
ADAM_EPS = 1e-08
ADAM_WD = 0.01
ADAM_STEP = 10


def _cparams(sem=None):
    return pltpu.CompilerParams(dimension_semantics=sem, vmem_limit_bytes=VMEM_LIMIT_BYTES)


def _pick(n, cands):
    for c in cands:
        if n % c == 0:
            return c
    return n


def _sigmoid(x):
    return 0.5 * jnp.tanh(0.5 * x) + 0.5


def _dot(a, b, dims):
    return lax.dot_general(a, b, (dims, ((), ())), preferred_element_type=F32)


NN = ((1,), (0,))
NT = ((1,), (1,))
TN = ((0,), (0,))


MM_ROWS = 1024


def _rms_stats(xv):
    rstd = lax.rsqrt(jnp.mean(xv * xv, axis=-1, keepdims=True) + EPS)
    return xv * rstd, rstd


def _mm_operand(a_ref, gain_ref):
    if gain_ref is None:
        return a_ref[...].astype(BF16)
    return (_rms_stats(a_ref[...])[0] * gain_ref[...]).astype(BF16)


def _mm_nn(a, w, res=None, out_dtype=F32, name="mm_nn", gain=None):
    m, k = a.shape
    s, _, ns = w.shape
    tm = min(m, MM_ROWS)
    tn = _pick(ns, (1024, 1408, 512, 256, 128))
    npb = ns // tn

    def body(a_ref, w_ref, *rest):
        o_ref = rest[-1]
        acc = _dot(_mm_operand(a_ref, rest[0] if gain is not None else None), w_ref[...], NN)
        if res is not None:
            acc = acc + rest[-2][...]
        o_ref[...] = acc.astype(o_ref.dtype)

    in_specs = [
        pl.BlockSpec((tm, k), lambda i, j: (i, 0)),
        pl.BlockSpec((None, k, tn), lambda i, j: (j // npb, 0, j % npb)),
    ]
    args = [a, w]
    if gain is not None:
        in_specs.append(pl.BlockSpec((1, k), lambda i, j: (0, 0)))
        args.append(gain)
    if res is not None:
        in_specs.append(pl.BlockSpec((tm, tn), lambda i, j: (i, j)))
        args.append(res)
    return pl.pallas_call(
        body,
        name=name,
        grid=(m // tm, s * npb),
        in_specs=in_specs,
        out_specs=pl.BlockSpec((tm, tn), lambda i, j: (i, j)),
        out_shape=jax.ShapeDtypeStruct((m, s * ns), out_dtype),
        compiler_params=_cparams(("parallel", "parallel")),
    )(*args)


def _dy_spec(stacked, tm, tn, npb, row, kk):
    if stacked:
        return pl.BlockSpec((None, tm, tn), lambda *g: (kk(g) // npb, row(g), kk(g) % npb))
    return pl.BlockSpec((tm, tn), lambda *g: (row(g), kk(g)))


def _dep_specs(deps):
    return [pl.BlockSpec(d.shape, lambda *g: (0, 0)) for d in deps]


def _mm_nt(dy, w, stacked=False, out_dtype=F32, name="mm_nt", deps=(), norm_of=None, also=None):
    s, k, ns = w.shape
    m = dy.shape[1] if stacked else dy.shape[0]
    tm = min(m, MM_ROWS)
    tko = _pick(k, (1024, 1408, 512, 256))
    tn = _pick(ns, (1024, 1408, 512, 256))
    npb = ns // tn
    nk = s * npb
    fused = norm_of is not None
    assert not fused or tko == k
    assert also is None or fused

    def body(dy_ref, w_ref, *rest):
        acc_ref = rest[-1]
        i, kk = pl.program_id(0), pl.program_id(2)

        @pl.when(kk == 0)
        def _():
            acc_ref[...] = jnp.zeros_like(acc_ref)

        acc_ref[...] += _dot(dy_ref[...].astype(BF16), w_ref[...], NT)

        if not fused:
            @pl.when(kk == nk - 1)
            def _():
                rest[-2][...] = acc_ref[...].astype(rest[-2].dtype)
            return
        x_ref, g_ref, dres_ref = rest[:3]
        n_out = 3 if also is not None else 2
        dx_ref, dg_refs = rest[-1 - n_out], rest[-n_out:-1]

        @pl.when(jnp.logical_and(i == 0, kk == 0))
        def _():
            for dg_ref in dg_refs:
                dg_ref[...] = jnp.zeros_like(dg_ref)

        @pl.when(kk == nk - 1)
        def _():
            dxn = acc_ref[...]
            xhat, rstd = _rms_stats(x_ref[...])
            gd = dxn * g_ref[...]
            dg_refs[0][...] += jnp.sum(dxn * xhat, axis=0, keepdims=True)
            if also is not None:
                dy2_ref, w2_ref, g2_ref = rest[3:6]
                dxn2 = _dot(dy2_ref[...].astype(BF16), w2_ref[...], NT)
                gd = gd + dxn2 * g2_ref[...]
                dg_refs[1][...] += jnp.sum(dxn2 * xhat, axis=0, keepdims=True)
            dx_ref[...] = dres_ref[...] + rstd * (gd - xhat * jnp.mean(gd * xhat, axis=-1, keepdims=True))

    row = pl.BlockSpec((tm, tko), lambda i, j, kk: (i, j))
    vec = pl.BlockSpec((1, k), lambda i, j, kk: (0, 0))
    extra_in, extra_args = [], ()
    if fused:
        extra_in, extra_args = [row, vec, row], tuple(norm_of)
    if also is not None:
        n2 = also[0].shape[1]
        extra_in += [pl.BlockSpec((tm, n2), lambda i, j, kk: (i, 0)), pl.BlockSpec((None, k, n2), lambda i, j, kk: (0, 0, 0)), vec]
        extra_args += tuple(also)
    f32 = lambda shape: jax.ShapeDtypeStruct(shape, F32)
    return pl.pallas_call(
        body,
        name=name,
        grid=(m // tm, k // tko, nk),
        in_specs=[
            _dy_spec(stacked, tm, tn, npb, lambda g: g[0], lambda g: g[2]),
            pl.BlockSpec((None, tko, tn), lambda i, j, kk: (kk // npb, j, kk % npb)),
        ] + extra_in + _dep_specs(deps),
        out_specs=([row, vec] + ([vec] if also is not None else [])) if fused else row,
        out_shape=([f32((m, k)), f32((1, k))] + ([f32((1, k))] if also is not None else [])) if fused else jax.ShapeDtypeStruct((m, k), out_dtype),
        scratch_shapes=[pltpu.VMEM((tm, tko), F32)],
        compiler_params=_cparams(("arbitrary",) * 3 if fused else ("parallel", "parallel", "arbitrary")),
    )(dy, w, *extra_args, *deps)


def _mm_tn(a, dy, s, ns, stacked=False, name="mm_tn", deps=(), gain=None):
    m, k = a.shape
    tm = min(m, MM_ROWS)
    tk = _pick(k, (1024, 1408, 512, 256))
    tn = _pick(ns, (1024, 1408, 512, 256, 128))
    npb = ns // tn
    nm = m // tm
    assert gain is None or tk == k

    def body(a_ref, dy_ref, *rest):
        j, mm = pl.program_id(1), pl.program_id(2)
        if gain is None:
            o_ref, acc_ref = rest[-2:]
            lhs = a_ref[...].astype(BF16)
        else:
            o_ref, acc_ref, xn_ref = rest[-3:]

            @pl.when(j == 0)
            def _():
                xn_ref[mm] = _mm_operand(a_ref, rest[0])

            lhs = xn_ref[mm]

        @pl.when(mm == 0)
        def _():
            acc_ref[...] = jnp.zeros_like(acc_ref)

        acc_ref[...] += _dot(lhs, dy_ref[...].astype(BF16), TN)

        @pl.when(mm == nm - 1)
        def _():
            o_ref[...] = acc_ref[...]

    a_rows = (lambda i, j, mm: (mm, i)) if gain is None else (lambda i, j, mm: (jnp.where(j == 0, mm, 0), i))
    return pl.pallas_call(
        body,
        name=name,
        grid=(k // tk, s * npb, nm),
        in_specs=[
            pl.BlockSpec((tm, tk), a_rows),
            _dy_spec(stacked, tm, tn, npb, lambda g: g[2], lambda g: g[1]),
        ] + ([pl.BlockSpec((1, k), lambda i, j, mm: (0, 0))] if gain is not None else []) + _dep_specs(deps),
        out_specs=pl.BlockSpec((None, tk, tn), lambda i, j, mm: (j // npb, i, j % npb)),
        out_shape=jax.ShapeDtypeStruct((s, k, ns), F32),
        scratch_shapes=[pltpu.VMEM((tk, tn), F32)] + ([pltpu.VMEM((nm, tm, tk), BF16)] if gain is not None else []),
        compiler_params=_cparams(("parallel", "arbitrary", "arbitrary") if gain is not None else ("parallel", "parallel", "arbitrary")),
    )(a, dy, *(() if gain is None else (gain,)), *deps)


ROW_TILE = 512


def _loss_head(h, g, target):
    t, d = h.shape
    r = min(t, ROW_TILE)

    def body(h_ref, g_ref, t_ref, dh_ref, dg_ref, loss_ref):
        @pl.when(pl.program_id(0) == 0)
        def _():
            dg_ref[...] = jnp.zeros_like(dg_ref)
            loss_ref[...] = jnp.zeros_like(loss_ref)

        xv = h_ref[...]
        rstd = lax.rsqrt(jnp.mean(xv * xv, axis=-1, keepdims=True) + EPS)
        xhat = xv * rstd
        gv = g_ref[...]
        err = xhat * gv - t_ref[...]
        loss_ref[...] += 0.5 * jnp.sum(jnp.mean(err * err, axis=-1, keepdims=True), axis=0, keepdims=True)
        dy = err * (1.0 / d)
        gd = dy * gv
        dh_ref[...] = rstd * (gd - xhat * jnp.mean(gd * xhat, axis=-1, keepdims=True))
        dg_ref[...] += jnp.sum(dy * xhat, axis=0, keepdims=True)

    return pl.pallas_call(
        body,
        name="loss_head",
        grid=(t // r,),
        in_specs=[
            pl.BlockSpec((r, d), lambda i: (i, 0)),
            pl.BlockSpec((1, d), lambda i: (0, 0)),
            pl.BlockSpec((r, d), lambda i: (i, 0)),
        ],
        out_specs=[
            pl.BlockSpec((r, d), lambda i: (i, 0)),
            pl.BlockSpec((1, d), lambda i: (0, 0)),
            pl.BlockSpec((1, LANES), lambda i: (0, 0)),
        ],
        out_shape=[
            jax.ShapeDtypeStruct((t, d), F32),
            jax.ShapeDtypeStruct((1, d), F32),
            jax.ShapeDtypeStruct((1, LANES), F32),
        ],
        compiler_params=_cparams(("arbitrary",)),
    )(h, g, target)


CONV_ROWS = 512
CONV_COLS = 1408


def _conv_taps(x_ext, n):
    tot = x_ext.shape[0]
    g1 = pltpu.roll(x_ext, 1, 0)[tot - n:]
    g2 = pltpu.roll(x_ext, 2, 0)[tot - n:]
    return g2, g1


def _conv_fwd(up, conv_w, conv_b, name="conv_fwd"):
    t = up.shape[0]
    r = min(t, CONV_ROWS)
    tc = CONV_COLS
    ncb = D_FF // tc
    hb = r // SUBLANES

    def body(g_ref, halo_ref, v_ref, w_ref, b_ref, o_ref, c_ref):
        i = pl.program_id(1)
        g0 = g_ref[...]
        halo = halo_ref[...] * jnp.where(i > 0, 1.0, 0.0)
        g2, g1 = _conv_taps(jnp.concatenate([halo, g0], axis=0), r)
        c = b_ref[...] + w_ref[0:1, :] * g2 + w_ref[1:2, :] * g1 + w_ref[2:3, :] * g0
        c_ref[...] = c.astype(BF16)
        o_ref[...] = (c * _sigmoid(c) * v_ref[...]).astype(BF16)

    blk = pl.BlockSpec((r, tc), lambda j, i: (i, j))
    return pl.pallas_call(
        body,
        name=name,
        grid=(ncb, t // r),
        in_specs=[
            blk,
            pl.BlockSpec((SUBLANES, tc), lambda j, i: (jnp.maximum(i * hb - 1, 0), j)),
            pl.BlockSpec((r, tc), lambda j, i: (i, ncb + j)),
            pl.BlockSpec((3, tc), lambda j, i: (0, j)),
            pl.BlockSpec((1, tc), lambda j, i: (0, j)),
        ],
        out_specs=[blk, blk],
        out_shape=[jax.ShapeDtypeStruct((t, D_FF), BF16), jax.ShapeDtypeStruct((t, D_FF), BF16)],
        compiler_params=_cparams(("parallel", "parallel")),
    )(up, up, up, conv_w, conv_b)


def _conv_bwd(up, conv_w, c, dact, name="conv_bwd"):
    t = up.shape[0]
    r = min(t, CONV_ROWS)
    tc = CONV_COLS
    ncb = D_FF // tc
    nrt = t // r

    def body(g_ref, v_ref, w_ref, c_ref, da_ref, dup_ref, dw_ref, db_ref, nxt_ref):
        ii = pl.program_id(1)

        @pl.when(ii == 0)
        def _():
            nxt_ref[...] = jnp.zeros_like(nxt_ref)
            dw_ref[...] = jnp.zeros_like(dw_ref)
            db_ref[...] = jnp.zeros_like(db_ref)

        g0 = g_ref[...]
        w0, w1, w2 = w_ref[0:1, :], w_ref[1:2, :], w_ref[2:3, :]
        c = c_ref[...].astype(F32)
        sg = _sigmoid(c)
        da = da_ref[...]
        dup_ref[1] = (da * (c * sg)).astype(BF16)
        dc = da * v_ref[...] * (sg * (1.0 + c * (1.0 - sg)))
        ext = jnp.concatenate([dc, nxt_ref[...]], axis=0)
        tot = r + SUBLANES
        d1 = pltpu.roll(ext, tot - 1, 0)[:r]
        d2 = pltpu.roll(ext, tot - 2, 0)[:r]
        nxt_ref[...] = dc[:SUBLANES]
        dup_ref[0] = (w2 * dc + w1 * d1 + w0 * d2).astype(BF16)
        db_ref[...] += jnp.sum(dc, axis=0, keepdims=True)
        dw_ref[0:1, :] += jnp.sum(d2 * g0, axis=0, keepdims=True)
        dw_ref[1:2, :] += jnp.sum(d1 * g0, axis=0, keepdims=True)
        dw_ref[2:3, :] += jnp.sum(dc * g0, axis=0, keepdims=True)

    rev = lambda ii: nrt - 1 - ii
    dup, dw, db = pl.pallas_call(
        body,
        name=name,
        grid=(ncb, nrt),
        in_specs=[
            pl.BlockSpec((r, tc), lambda j, ii: (rev(ii), j)),
            pl.BlockSpec((r, tc), lambda j, ii: (rev(ii), ncb + j)),
            pl.BlockSpec((3, tc), lambda j, ii: (0, j)),
            pl.BlockSpec((r, tc), lambda j, ii: (rev(ii), j)),
            pl.BlockSpec((r, tc), lambda j, ii: (rev(ii), j)),
        ],
        out_specs=[
            pl.BlockSpec((2, None, r, tc), lambda j, ii: (0, j, rev(ii), 0)),
            pl.BlockSpec((3, tc), lambda j, ii: (0, j)),
            pl.BlockSpec((1, tc), lambda j, ii: (0, j)),
        ],
        out_shape=[
            jax.ShapeDtypeStruct((2, ncb, t, tc), BF16),
            jax.ShapeDtypeStruct((3, D_FF), F32),
            jax.ShapeDtypeStruct((1, D_FF), F32),
        ],
        scratch_shapes=[pltpu.VMEM((SUBLANES, tc), F32)],
        compiler_params=_cparams(("parallel", "arbitrary")),
    )(up, up, conv_w, c, dact)
    return dup.reshape(2 * ncb, t, tc), dw, db


def _split3(x):
    x1 = x.astype(BF16)
    r1 = x - x1.astype(F32)
    x2 = r1.astype(BF16)
    x3 = (r1 - x2.astype(F32)).astype(BF16)
    return x1, x2, x3


def _tri_dot(tri, x, dims):
    x1, x2, x3 = _split3(x)
    return _dot(tri, x1, dims) + _dot(tri, x2, dims) + _dot(tri, x3, dims)


def _lower_bound(logits_ref):
    return _sigmoid(logits_ref[0:1, :] - logits_ref[1:2, :])


def _hg_gates(qr, fr, lb):
    q = qr * _sigmoid(qr) * (HG_DK ** -0.5)
    sf = _sigmoid(fr)
    fg = lb + (1.0 - lb) * sf
    return q, sf, fg


def _hg_chunk_terms(q, fg, tril_b, low_half):
    g = jnp.log(fg)
    k = 1.0 - fg
    cum = _tri_dot(tril_b, g, NN)
    c_last = jnp.sum(g, axis=0, keepdims=True)
    c_mid = jnp.sum(jnp.where(low_half, g, 0.0), axis=0, keepdims=True)
    e_q = jnp.exp(cum - c_mid)
    e_k = jnp.exp(c_mid - cum)
    e_0 = jnp.exp(cum)
    e_l = jnp.exp(c_last - cum)
    return k, e_q, e_k, e_0, e_l, jnp.exp(c_last)


HG_BLOCK = 256


def _hg_proj_specs(rb, row):
    return [pl.BlockSpec((rb, D_MODEL), functools.partial(lambda i, k: (row(i), k), k=k)) for k in range(4)]


def _hg_consts(c):
    tril = lax.broadcasted_iota(jnp.int32, (c, c), 0) >= lax.broadcasted_iota(jnp.int32, (c, c), 1)
    low_half = lax.broadcasted_iota(jnp.int32, (c, D_MODEL), 0) < c // 2
    return tril, tril.astype(BF16), low_half


def _hgrn_fwd(proj, lb, wn):
    t = proj.shape[0]
    c = HG_CHUNK
    rb = min(t, HG_BLOCK)
    cpb = rb // c

    def body(q_ref, f_ref, i_ref, g_ref, lb_ref, wn_ref, o_ref, y_ref, st_ref, s_scr):
        @pl.when(pl.program_id(0) == 0)
        def _():
            s_scr[...] = jnp.zeros_like(s_scr)

        lb_all = _lower_bound(lb_ref)
        wnv = wn_ref[...]
        tril, tril_b, low_half = _hg_consts(c)

        def chunk(n, carry):
            rows = pl.ds(pl.multiple_of(n * c, c), c)
            q, _, fg = _hg_gates(q_ref[rows, :], f_ref[rows, :], lb_all)
            k, e_q, e_k, e_0, e_l, e_last = _hg_chunk_terms(q, fg, tril_b, low_half)
            qi, ki, q0, kl = (q * e_q).astype(BF16), (k * e_k).astype(BF16), (q * e_0).astype(BF16), (k * e_l).astype(BF16)
            v = i_ref[rows, :].astype(BF16)
            gr = g_ref[rows, :]
            gate = gr * _sigmoid(gr)
            for h in range(HG_HEADS):
                cols = slice(h * HG_DK, (h + 1) * HG_DK)
                st = s_scr[h]
                st_ref[h, n] = st
                a = jnp.where(tril, _dot(qi[:, cols], ki[:, cols], NT), 0.0)
                o = _dot(q0[:, cols], st.astype(BF16), NT) + _dot(a.astype(BF16), v[:, cols], NN)
                s_scr[h] = st * e_last[:, cols] + _dot(v[:, cols], kl[:, cols], TN)
                o_ref[rows, cols] = o
                rstd = lax.rsqrt(jnp.mean(o * o, axis=-1, keepdims=True) + EPS)
                y_ref[rows, cols] = (o * rstd * wnv * gate[:, cols]).astype(BF16)
            return carry

        lax.fori_loop(0, cpb, chunk, 0, unroll=2)

    blk = pl.BlockSpec((rb, D_MODEL), lambda i: (i, 0))
    return pl.pallas_call(
        body,
        name="hgrn_fwd",
        grid=(t // rb,),
        in_specs=_hg_proj_specs(rb, lambda i: i) + [pl.BlockSpec((2, D_MODEL), lambda i: (0, 0)), pl.BlockSpec((1, HG_DK), lambda i: (0, 0))],
        out_specs=[blk, blk, pl.BlockSpec((HG_HEADS, cpb, HG_DK, HG_DK), lambda i: (0, i, 0, 0))],
        out_shape=[
            jax.ShapeDtypeStruct((t, D_MODEL), F32),
            jax.ShapeDtypeStruct((t, D_MODEL), BF16),
            jax.ShapeDtypeStruct((HG_HEADS, t // c, HG_DK, HG_DK), F32),
        ],
        scratch_shapes=[pltpu.VMEM((HG_HEADS, HG_DK, HG_DK), F32)],
        compiler_params=_cparams(("arbitrary",)),
    )(proj, proj, proj, proj, lb, wn)


def _hgrn_bwd(proj, lb, wn, o, states, dy):
    t = proj.shape[0]
    c = HG_CHUNK
    rb = min(t, HG_BLOCK)
    cpb = rb // c
    nb = t // rb

    def body(q_ref, f_ref, i_ref, g_ref, lb_ref, wn_ref, o_ref, st_ref, dy_ref, dp_ref, dl_ref, dwn_ref, ds_scr, dlb_scr):
        step = pl.program_id(0)

        @pl.when(step == 0)
        def _():
            dwn_ref[...] = jnp.zeros_like(dwn_ref)
            ds_scr[...] = jnp.zeros_like(ds_scr)
            dlb_scr[...] = jnp.zeros_like(dlb_scr)

        lb_all = _lower_bound(lb_ref)
        wnv = wn_ref[...]
        tril, tril_b, low_half = _hg_consts(c)

        def chunk(nn, carry):
            n = cpb - 1 - nn
            rows = pl.ds(pl.multiple_of(n * c, c), c)
            qr = q_ref[rows, :]
            gr = g_ref[rows, :]
            q, sf, fg = _hg_gates(qr, f_ref[rows, :], lb_all)
            k, e_q, e_k, e_0, e_l, e_last = _hg_chunk_terms(q, fg, tril_b, low_half)
            qi, qi_lo, _ = _split3(q * e_q)
            ki, ki_lo, _ = _split3(k * e_k)
            q0 = (q * e_0).astype(BF16)
            kl = (k * e_l).astype(BF16)
            v = i_ref[rows, :].astype(BF16)
            sg = _sigmoid(gr)
            silu_g = gr * sg
            dsilu_g = sg * (1.0 + gr * (1.0 - sg))
            dqs, dks, d_lasts = [], [], []
            for h in range(HG_HEADS):
                cols = slice(h * HG_DK, (h + 1) * HG_DK)
                ov = o_ref[rows, cols]
                dyv = dy_ref[rows, cols].astype(F32)
                rstd = lax.rsqrt(jnp.mean(ov * ov, axis=-1, keepdims=True) + EPS)
                ohat = ov * rstd
                dp_ref[3, rows, cols] = (dyv * (ohat * wnv) * dsilu_g[:, cols]).astype(BF16)
                don = dyv * silu_g[:, cols]
                dwn_ref[...] += jnp.sum(don * ohat, axis=0, keepdims=True)
                gd = don * wnv
                do_b = (rstd * (gd - ohat * jnp.mean(gd * ohat, axis=-1, keepdims=True))).astype(BF16)
                st = st_ref[h, n]
                ds = ds_scr[h]
                ds_b = ds.astype(BF16)
                vh, kh = v[:, cols], k[:, cols]
                a_b = jnp.where(tril, _dot(qi[:, cols], ki[:, cols], NT), 0.0).astype(BF16)
                da_b = jnp.where(tril, _dot(do_b, vh, NT), 0.0).astype(BF16)
                dqs.append(_dot(do_b, st.astype(BF16), NN) * e_0[:, cols]
                           + (_dot(da_b, ki[:, cols], NN) + _dot(da_b, ki_lo[:, cols], NN)) * e_q[:, cols])
                dk_state = _dot(vh, ds_b, NN) * e_l[:, cols]
                dks.append((_dot(da_b, qi[:, cols], TN) + _dot(da_b, qi_lo[:, cols], TN)) * e_k[:, cols] + dk_state)
                dp_ref[2, rows, cols] = (_dot(a_b, do_b, TN) + _dot(kl[:, cols], ds_b, NT)).astype(BF16)
                ds_scr[h] = ds * e_last[:, cols] + _dot(do_b, q0[:, cols], TN)
                d_lasts.append(jnp.sum(dk_state * kh, axis=0, keepdims=True) + jnp.sum(ds * st, axis=0, keepdims=True) * e_last[:, cols])
            dq = jnp.concatenate(dqs, axis=1)
            dk = jnp.concatenate(dks, axis=1)
            dlogf = _tri_dot(tril_b, q * dq - k * dk, TN) + jnp.concatenate(d_lasts, axis=1)
            dfg = dlogf / fg - dk
            dlb_scr[...] += jnp.sum(dfg * (1.0 - sf), axis=0, keepdims=True)
            sq = _sigmoid(qr)
            dp_ref[0, rows, :] = (dq * (HG_DK ** -0.5) * (sq * (1.0 + qr * (1.0 - sq)))).astype(BF16)
            dp_ref[1, rows, :] = (dfg * (1.0 - lb_all) * sf * (1.0 - sf)).astype(BF16)
            return carry

        lax.fori_loop(0, cpb, chunk, 0, unroll=2)

        @pl.when(step == nb - 1)
        def _():
            d0 = dlb_scr[...] * lb_all * (1.0 - lb_all)
            dl_ref[0:1, :] = d0
            dl_ref[1:2, :] = -d0

    rev = lambda i: nb - 1 - i
    blk = pl.BlockSpec((rb, D_MODEL), lambda i: (rev(i), 0))
    return pl.pallas_call(
        body,
        name="hgrn_bwd",
        grid=(nb,),
        in_specs=_hg_proj_specs(rb, rev)
        + [pl.BlockSpec((2, D_MODEL), lambda i: (0, 0)), pl.BlockSpec((1, HG_DK), lambda i: (0, 0)), blk,
           pl.BlockSpec((HG_HEADS, cpb, HG_DK, HG_DK), lambda i: (0, rev(i), 0, 0)), blk],
        out_specs=[
            pl.BlockSpec((4, rb, D_MODEL), lambda i: (0, rev(i), 0)),
            pl.BlockSpec((2, D_MODEL), lambda i: (0, 0)),
            pl.BlockSpec((1, HG_DK), lambda i: (0, 0)),
        ],
        out_shape=[
            jax.ShapeDtypeStruct((4, t, D_MODEL), BF16),
            jax.ShapeDtypeStruct((2, D_MODEL), F32),
            jax.ShapeDtypeStruct((1, HG_DK), F32),
        ],
        scratch_shapes=[pltpu.VMEM((HG_HEADS, HG_DK, HG_DK), F32), pltpu.VMEM((1, D_MODEL), F32)],
        compiler_params=_cparams(("arbitrary",)),
    )(proj, proj, proj, proj, lb, wn, o, states, dy)


ATT_STACK = 8


def _att_stack(q_ref, sink_ref, first, lo, bias_p, bias_c, extra_ref=None):
    qs, bps, bcs, sinks, extras = [], [], [], None, []
    rows = lax.broadcasted_iota(jnp.int32, (ATT_STACK * WINDOW, 1), 0)
    for i in range(ATT_STACK):
        hq = first + i
        cols = slice((hq // 2) * LANES, (hq // 2 + 1) * LANES)
        sel = lo if hq % 2 == 0 else jnp.logical_not(lo)
        qp = q_ref[:, cols] * (ATT_HD ** -0.5)
        qs.append(jnp.where(sel, qp, jnp.zeros_like(qp)))
        bps.append(ALIBI_SLOPES[hq] * bias_p)
        bcs.append(ALIBI_SLOPES[hq] * bias_c)
        sinks = sink_ref[hq] if sinks is None else jnp.where(rows < i * WINDOW, sinks, sink_ref[hq])
        if extra_ref is not None:
            ep = extra_ref[:, cols]
            extras.append(jnp.where(sel, ep, jnp.zeros_like(ep)))
    cat = lambda parts: jnp.concatenate(parts, axis=0)
    return cat(qs), cat(bps), cat(bcs), sinks, (cat(extras) if extras else None)


def _att_rows(i):
    return slice(i * WINDOW, (i + 1) * WINDOW)


def _att_bias(n):
    tq = lax.broadcasted_iota(jnp.int32, (WINDOW, WINDOW), 0)
    sk = lax.broadcasted_iota(jnp.int32, (WINDOW, WINDOW), 1)
    valid_c = sk <= tq
    valid_p = (sk - tq) > jnp.where(n > 0, 0, WINDOW)
    dist_c = (tq - sk).astype(F32)
    return jnp.where(valid_p, -dist_c - float(WINDOW), NEG), jnp.where(valid_c, -dist_c, NEG)


def _att_halves(x, lo, kh):
    r = pltpu.roll(x, ATT_HD, 1)
    zero = jnp.zeros_like(x)
    if kh == 0:
        return jnp.where(lo, x, r), jnp.where(lo, x, zero), jnp.where(lo, zero, r)
    return jnp.where(lo, r, x), jnp.where(lo, r, zero), jnp.where(lo, zero, x)


def _att_probs(qm, k2p, k2c, bias_p, bias_c, sink):
    sp = _dot(qm, k2p, NT) + bias_p
    sc = _dot(qm, k2c, NT) + bias_c
    m = jnp.maximum(jnp.maximum(jnp.max(sp, axis=-1, keepdims=True), jnp.max(sc, axis=-1, keepdims=True)), sink)
    ep = jnp.exp(sp - m)
    ec = jnp.exp(sc - m)
    es = jnp.exp(sink - m)
    inv = 1.0 / (jnp.sum(ep, axis=-1, keepdims=True) + jnp.sum(ec, axis=-1, keepdims=True) + es)
    return ep * inv, ec * inv, es * inv


def _attn_fwd(q, kv, sinks):
    t = q.shape[0]
    nb = t // WINDOW

    def body(sink_ref, q_ref, kvp_ref, kvc_ref, o_ref):
        n = pl.program_id(0)
        bias_p, bias_c = _att_bias(n)
        lo = lax.broadcasted_iota(jnp.int32, (WINDOW, LANES), 1) < ATT_HD
        for kh in range(ATT_KVH):
            k2p, _, _ = _att_halves(kvp_ref[:, 0:LANES], lo, kh)
            k2c, _, _ = _att_halves(kvc_ref[:, 0:LANES], lo, kh)
            _, vlo_p, vhi_p = _att_halves(kvp_ref[:, LANES:2 * LANES], lo, kh)
            _, vlo_c, vhi_c = _att_halves(kvc_ref[:, LANES:2 * LANES], lo, kh)
            for first in range(kh * ATT_GROUP, (kh + 1) * ATT_GROUP, ATT_STACK):
                qs, bp, bc, sinks, _ = _att_stack(q_ref, sink_ref, first, lo, bias_p, bias_c)
                pp, pc, _ = _att_probs(qs, k2p, k2c, bp, bc, sinks)
                pp, pc = pp.astype(BF16), pc.astype(BF16)
                for i in range(0, ATT_STACK, 2):
                    even, odd = _att_rows(i), _att_rows(i + 1)
                    out = (_dot(pp[even], vlo_p, NN) + _dot(pc[even], vlo_c, NN)
                           + _dot(pp[odd], vhi_p, NN) + _dot(pc[odd], vhi_c, NN))
                    j = (first + i) // 2
                    o_ref[:, j * LANES:(j + 1) * LANES] = out.astype(BF16)

    return pl.pallas_call(
        body,
        name="attn_fwd",
        grid=(nb,),
        in_specs=[
            pl.BlockSpec(memory_space=pltpu.SMEM),
            pl.BlockSpec((WINDOW, D_MODEL), lambda n: (n, 0)),
            pl.BlockSpec((WINDOW, 2 * LANES), lambda n: (jnp.maximum(n - 1, 0), 0)),
            pl.BlockSpec((WINDOW, 2 * LANES), lambda n: (n, 0)),
        ],
        out_specs=pl.BlockSpec((WINDOW, D_MODEL), lambda n: (n, 0)),
        out_shape=jax.ShapeDtypeStruct((t, D_MODEL), BF16),
        compiler_params=_cparams(("parallel",)),
    )(sinks, q, kv, kv)


def _attn_bwd(q, kv, sinks, dout):
    t = q.shape[0]
    nb = t // WINDOW

    def body(sink_ref, q_ref, kvp_ref, kvc_ref, do_ref, dq_ref, dkv_ref, dsink_ref, carry_ref):
        n = pl.program_id(0)

        @pl.when(n == 0)
        def _():
            carry_ref[...] = jnp.zeros_like(carry_ref)
            dsink_ref[...] = jnp.zeros_like(dsink_ref)

        @pl.when(n == nb)
        def _():
            dkv_ref[...] = carry_ref[...].astype(BF16)

        @pl.when(n < nb)
        def _():
            bias_p, bias_c = _att_bias(n)
            lo = lax.broadcasted_iota(jnp.int32, (WINDOW, LANES), 1) < ATT_HD
            lane1 = lax.broadcasted_iota(jnp.int32, (1, LANES), 1)
            dsink = jnp.zeros((1, LANES), F32)
            halves = []
            for kh in range(ATT_KVH):
                k2p, klo_p, khi_p = _att_halves(kvp_ref[:, 0:LANES], lo, kh)
                k2c, klo_c, khi_c = _att_halves(kvc_ref[:, 0:LANES], lo, kh)
                v2p, _, _ = _att_halves(kvp_ref[:, LANES:2 * LANES], lo, kh)
                v2c, _, _ = _att_halves(kvc_ref[:, LANES:2 * LANES], lo, kh)
                acc = [jnp.zeros((WINDOW, LANES), F32) for _ in range(4)]
                for first in range(kh * ATT_GROUP, (kh + 1) * ATT_GROUP, ATT_STACK):
                    qs, bp, bc, sinks, dos = _att_stack(q_ref, sink_ref, first, lo, bias_p, bias_c, do_ref)
                    pp, pc, ps = _att_probs(qs, k2p, k2c, bp, bc, sinks)
                    dpp = _dot(dos, v2p, NT)
                    dpc = _dot(dos, v2c, NT)
                    delta = jnp.sum(pp * dpp, axis=-1, keepdims=True) + jnp.sum(pc * dpc, axis=-1, keepdims=True)
                    dsp = (pp * (dpp - delta)).astype(BF16)
                    dsc = (pc * (dpc - delta)).astype(BF16)
                    sink_term = ps * delta
                    for i in range(ATT_STACK):
                        dsink = dsink + jnp.where(lane1 == first + i, -jnp.sum(sink_term[_att_rows(i)], axis=0, keepdims=True), 0.0)
                    for i in range(0, ATT_STACK, 2):
                        even, odd = _att_rows(i), _att_rows(i + 1)
                        dq_pair = (_dot(dsp[even], klo_p, NN) + _dot(dsc[even], klo_c, NN)
                                   + _dot(dsp[odd], khi_p, NN) + _dot(dsc[odd], khi_c, NN))
                        j = (first + i) // 2
                        dq_ref[:, j * LANES:(j + 1) * LANES] = (dq_pair * (ATT_HD ** -0.5)).astype(BF16)
                    acc[0] = acc[0] + _dot(dsp, qs, TN)
                    acc[1] = acc[1] + _dot(dsc, qs, TN)
                    acc[2] = acc[2] + _dot(pp.astype(BF16), dos, TN)
                    acc[3] = acc[3] + _dot(pc.astype(BF16), dos, TN)
                halves.append([a + pltpu.roll(a, ATT_HD, 1) for a in acc])
            prev = jnp.concatenate(
                [jnp.where(lo, halves[0][0], halves[1][0]), jnp.where(lo, halves[0][2], halves[1][2])], axis=1)
            cur = jnp.concatenate(
                [jnp.where(lo, halves[0][1], halves[1][1]), jnp.where(lo, halves[0][3], halves[1][3])], axis=1)
            dkv_ref[...] = (carry_ref[...] + prev).astype(BF16)
            carry_ref[...] = cur
            dsink_ref[...] += dsink

    blk = lambda n: jnp.minimum(n, nb - 1)
    return pl.pallas_call(
        body,
        name="attn_bwd",
        grid=(nb + 1,),
        in_specs=[
            pl.BlockSpec(memory_space=pltpu.SMEM),
            pl.BlockSpec((WINDOW, D_MODEL), lambda n: (blk(n), 0)),
            pl.BlockSpec((WINDOW, 2 * LANES), lambda n: (jnp.maximum(blk(n) - 1, 0), 0)),
            pl.BlockSpec((WINDOW, 2 * LANES), lambda n: (blk(n), 0)),
            pl.BlockSpec((WINDOW, D_MODEL), lambda n: (blk(n), 0)),
        ],
        out_specs=[
            pl.BlockSpec((WINDOW, D_MODEL), lambda n: (blk(n), 0)),
            pl.BlockSpec((WINDOW, 2 * LANES), lambda n: (jnp.maximum(n - 1, 0), 0)),
            pl.BlockSpec((1, LANES), lambda n: (0, 0)),
        ],
        out_shape=[
            jax.ShapeDtypeStruct((t, D_MODEL), BF16),
            jax.ShapeDtypeStruct((t, 2 * LANES), BF16),
            jax.ShapeDtypeStruct((1, LANES), F32),
        ],
        scratch_shapes=[pltpu.VMEM((WINDOW, 2 * LANES), F32)],
        compiler_params=_cparams(("arbitrary",)),
    )(sinks, q, kv, kv, dout)


def _ffn_fwd(h, norm_g, w_up, conv_w, conv_b, w_down, tag, after_up=lambda up: None):
    up = _mm_nn(h, w_up, gain=norm_g, name=f"ffn{tag}_up")
    after_up(up)
    act, c = _conv_fwd(up, conv_w, conv_b, name=f"ffn{tag}_conv")
    h_out = _mm_nn(act, w_down, res=h, name=f"ffn{tag}_down")
    return h_out, (up, act, c)


def _ffn_bwd(dh, h, norm_g, w_up, conv_w, conv_b, w_down, saved, tag, deps=()):
    up, act, c = saved
    dw_down = _mm_tn(act, dh, 1, D_MODEL, name=f"ffn{tag}_dwdown", deps=deps)
    dact = _mm_nt(dh, w_down, name=f"ffn{tag}_dact", deps=deps)
    dup, dconv_w, dconv_b = _conv_bwd(up, conv_w, c, dact, name=f"ffn{tag}_dconv")
    dw_up = _mm_tn(h, dup, N_CHIPS, CONV_COLS, stacked=True, gain=norm_g, name=f"ffn{tag}_dwup")
    dh_in, dnorm = _mm_nt(dup, w_up, stacked=True, norm_of=(h, norm_g, dh), name=f"ffn{tag}_dxn")
    return dh_in, dict(ffn_w_down=dw_down, ffn_w_up=dw_up, ffn_conv_w=dconv_w, ffn_conv_b=dconv_b, ffn_norm=dnorm)


def _local_step(x, target, w, fetch=lambda w, stage, after: w, hook=lambda point, dh, grads: ()):
    proj = _mm_nn(x, w["hg_w_in"], gain=w["hg_norm"], name="hg_in")
    o, y, states = _hgrn_fwd(proj, w["hg_lb"], w["hg_out_norm"])
    w = fetch(w, "mixer_out", y)
    fetch(w, "layer0_relay", y)
    h_a = _mm_nn(y, w["hg_w_out"], res=x, name="hg_out")
    w = fetch(w, "layer0", h_a)
    h1, ffn0 = _ffn_fwd(h_a, w["ffn_norm"][0], w["ffn_w_up"][0], w["ffn_conv_w"][0], w["ffn_conv_b"][0], w["ffn_w_down"][0], 0,
                        lambda up: fetch(w, "layer1_relay", up))
    w = fetch(w, "layer1", h1)
    kv = _mm_nn(h1, w["w_kv"], gain=w["kv_norm"], out_dtype=BF16, name="kv_proj")
    qa = _mm_nn(h1, w["attn_w_q"], gain=w["attn_norm"], out_dtype=BF16, name="attn_q")
    ao = _attn_fwd(qa, kv, w["attn_sinks"])
    h_b = _mm_nn(ao, w["attn_w_o"], res=h1, name="attn_o")
    h2, ffn1 = _ffn_fwd(h_b, w["ffn_norm"][1], w["ffn_w_up"][1], w["ffn_conv_w"][1], w["ffn_conv_b"][1], w["ffn_w_down"][1], 1)
    dh2, d_final, loss = _loss_head(h2, w["final_norm"], target)

    dh_b, g1 = _ffn_bwd(dh2, h_b, w["ffn_norm"][1], w["ffn_w_up"][1], w["ffn_conv_w"][1], w["ffn_conv_b"][1], w["ffn_w_down"][1], ffn1, 1)
    deps = hook("ffn1", dh_b, g1)
    dw_o = _mm_tn(ao, dh_b, 1, D_MODEL, name="attn_dwo", deps=deps)
    dao = _mm_nt(dh_b, w["attn_w_o"], out_dtype=BF16, name="attn_dao", deps=deps)
    dqa, dkv, dsinks = _attn_bwd(qa, kv, w["attn_sinks"], dao)
    dw_q = _mm_tn(h1, dqa, 1, D_MODEL, gain=w["attn_norm"], name="attn_dwq")
    dw_kv = _mm_tn(h1, dkv, 1, 2 * LANES, gain=w["kv_norm"], name="kv_dw")
    dh1, d_attn_norm, d_kv_norm = _mm_nt(dqa, w["attn_w_q"], norm_of=(h1, w["attn_norm"], dh_b),
                                        also=(dkv, w["w_kv"], w["kv_norm"]), name="attn_dxa")
    deps = hook("attn", dh1, dict(attn_w_o=dw_o, attn_w_q=dw_q, w_kv=dw_kv))
    dh_a, g0 = _ffn_bwd(dh1, h_a, w["ffn_norm"][0], w["ffn_w_up"][0], w["ffn_conv_w"][0], w["ffn_conv_b"][0], w["ffn_w_down"][0], ffn0, 0, deps)
    dw_out = _mm_tn(y, dh_a, 1, D_MODEL, name="hg_dwout")
    deps = hook("ffn0", dh_a, dict(g0, hg_w_out=dw_out))
    dy = _mm_nt(dh_a, w["hg_w_out"], out_dtype=BF16, name="hg_dy", deps=deps)
    dproj, dlb, d_out_norm = _hgrn_bwd(proj, w["hg_lb"], w["hg_out_norm"], o, states, dy)
    deps = hook("hgrn", dproj, None)
    dw_in = _mm_tn(x, dproj, N_CHIPS, D_MODEL, stacked=True, gain=w["hg_norm"], name="hg_dwin", deps=deps)
    deps = hook("hg_w", dproj, dict(hg_w_in=dw_in))
    dx, d_hg_norm = _mm_nt(dproj, w["hg_w_in"], stacked=True, norm_of=(x, w["hg_norm"], dh_a), name="hg_dxn", deps=deps)

    grads = dict(
        hg_norm=d_hg_norm, hg_w_in=dw_in, hg_lb=dlb, hg_out_norm=d_out_norm, hg_w_out=dw_out,
        kv_norm=d_kv_norm, w_kv=dw_kv, attn_norm=d_attn_norm, attn_w_q=dw_q, attn_sinks=dsinks, attn_w_o=dw_o,
        final_norm=d_final,
    )
    for name in g0:
        grads[name] = [g0[name], g1[name]]
    return loss, dx, grads


ANY = pl.BlockSpec(memory_space=pl.ANY)


def _place():
    x, y, c = lax.axis_index("x"), lax.axis_index("y"), lax.axis_index("c")
    chips = [(1 - x, y), (x, 1 - y), (1 - x, 1 - y)]
    return x, y, c, chips


def _rcopy(src, dst, send_sem, recv_sem, to):
    return pltpu.make_async_remote_copy(src_ref=src, dst_ref=dst, send_sem=send_sem, recv_sem=recv_sem, device_id=to, device_id_type=MESH)


HBM = pl.BlockSpec(memory_space=pltpu.HBM)
SEM = pl.BlockSpec(memory_space=pltpu.SEMAPHORE)
EFFECT = pltpu.SideEffectType.DATAFLOW_SIDE_EFFECTING


def _in_hbm(a):
    return pltpu.with_memory_space_constraint(a, pltpu.HBM)


def _place_shard(shard, place, dtype, name, deps=(), layer=None):
    r, cols = shard.shape[-2:]
    tr = _pick(r, ELEM_ROWS)
    src = pl.BlockSpec((tr, cols), lambda i, place_ref: (i, 0)) if layer is None else pl.BlockSpec((None, tr, cols), lambda i, place_ref: (layer, i, 0))

    def body(place_ref, s_ref, *rest):
        o_ref = rest[-1]
        o_ref[...] = s_ref[...].astype(o_ref.dtype)

    return pl.pallas_call(
        body,
        name=name,
        grid_spec=pltpu.PrefetchScalarGridSpec(
            num_scalar_prefetch=1,
            grid=(r // tr,),
            in_specs=[src] + _dep_specs(deps),
            out_specs=pl.BlockSpec((None, tr, cols), lambda i, place_ref: (place_ref[0], i, 0)),
        ),
        out_shape=jax.ShapeDtypeStruct((N_CHIPS, r, cols), dtype),
        compiler_params=_cparams(("parallel",)),
    )(place, shard, *deps)


def _start_copies(name, bufs, n_sem, copies):
    n = len(bufs)

    def body(*refs):
        for cp in copies(refs[:n], refs[n], refs[n + 1]):
            cp.start()
        refs[-1][...] = jnp.zeros_like(refs[-1])

    outs = pl.pallas_call(
        body,
        name=name,
        in_specs=[HBM] * n,
        out_specs=[SEM, SEM] + [HBM] * n + [pl.BlockSpec(memory_space=pltpu.VMEM)],
        out_shape=[pltpu.SemaphoreType.DMA((n_sem,)), pltpu.SemaphoreType.DMA((n_sem,))] + [pltpu.HBM(b.shape, b.dtype) for b in bufs]
        + [jax.ShapeDtypeStruct((SUBLANES, LANES), F32)],
        input_output_aliases={i: 2 + i for i in range(n)},
        compiler_params=pltpu.CompilerParams(has_side_effects=EFFECT),
    )(*[_in_hbm(b) for b in bufs])
    return outs[0], outs[1], list(outs[2:-1]), outs[-1]


def _wait_copies(name, bufs, send_sems, recv_sems, after, copies):
    n = len(bufs)

    def body(*refs):
        for cp in copies(refs[:n], refs[n], refs[n + 1]):
            cp.wait_send()
            cp.wait_recv()

    return pl.pallas_call(
        body,
        name=name,
        in_specs=[HBM] * n + [SEM, SEM, ANY],
        out_specs=[HBM] * n,
        out_shape=[pltpu.HBM(b.shape, b.dtype) for b in bufs],
        input_output_aliases={i: i for i in range(n)},
        compiler_params=pltpu.CompilerParams(has_side_effects=EFFECT),
    )(*bufs, send_sems, recv_sems, after)


def _relay_copies(name, bufs, send_sems, recv_sems, after, landed, n_sem, onward):
    n = len(bufs)

    def body(*refs):
        for cp in landed(refs[:n], refs[n], refs[n + 1]):
            cp.wait_send()
            cp.wait_recv()
        for cp in onward(refs[:n], refs[n + 3], refs[n + 4]):
            cp.start()
        refs[-1][...] = jnp.zeros_like(refs[-1])

    outs = pl.pallas_call(
        body,
        name=name,
        in_specs=[HBM] * n + [SEM, SEM, ANY],
        out_specs=[SEM, SEM] + [HBM] * n + [pl.BlockSpec(memory_space=pltpu.VMEM)],
        out_shape=[pltpu.SemaphoreType.DMA((n_sem,)), pltpu.SemaphoreType.DMA((n_sem,))] + [pltpu.HBM(b.shape, b.dtype) for b in bufs]
        + [jax.ShapeDtypeStruct((SUBLANES, LANES), F32)],
        input_output_aliases={i: 2 + i for i in range(n)},
        compiler_params=pltpu.CompilerParams(has_side_effects=EFFECT),
    )(*bufs, send_sems, recv_sems, after)
    return outs[0], outs[1], list(outs[2:-1]), outs[-1]


def _gather_half_copies(first, count, over_ici):
    def copies(refs, send_sems, recv_sems):
        x, y, c, chips = _place()
        out = []
        for i in range(count):
            h = refs[i].shape[1] // 2
            mine = pl.ds(c * h, h)
            for j, (px, py) in enumerate(chips):
                k = 3 * (first + i) + j
                slot = 2 * x + y if over_ici else 2 * px + py
                to = (px, py, c) if over_ici else (x, y, 1 - c)
                out.append(_rcopy(refs[i].at[slot, mine], refs[i].at[slot, mine], send_sems.at[k], recv_sems.at[k], to))
        return out

    return copies


def _gather_copies(first, count):
    def copies(refs, send_sems, recv_sems):
        x, y, c, chips = _place()
        me = 2 * x + y
        out = []
        for i in range(count):
            for j, (px, py) in enumerate(chips):
                k = 3 * (first + i) + j
                out.append(_rcopy(refs[i].at[me], refs[i].at[me], send_sems.at[k], recv_sems.at[k], (px, py, c)))
        return out

    return copies


def _swap_copies(n):
    def copies(refs, send_sems, recv_sems):
        x, y, c, _ = _place()
        out = []
        for i in range(n):
            h = refs[i].shape[1] // 2
            out.append(_rcopy(refs[i].at[:, pl.ds((1 - c) * h, h)], refs[n + i], send_sems.at[i], recv_sems.at[i], (x, y, 1 - c)))
        return out

    return copies


def _partial_copies(n):
    def copies(refs, send_sems, recv_sems):
        x, y, c, chips = _place()
        out = []
        for i in range(n):
            for j, (px, py) in enumerate(chips):
                out.append(_rcopy(refs[i].at[2 * px + py], refs[n + i].at[j], send_sems.at[3 * i + j], recv_sems.at[3 * i + j], (px, py, c)))
        return out

    return copies


def _share_copies(n):
    def copies(refs, send_sems, recv_sems):
        x, y, c, _ = _place()
        return [_rcopy(refs[i].at[c], refs[i].at[c], send_sems.at[i], recv_sems.at[i], (x, y, 1 - c)) for i in range(n)]

    return copies


def _small_layout(groups):
    flat = [a for g in groups for a in g]
    rows = -(-sum(a.shape[0] for a in flat) // SUBLANES) * SUBLANES
    return flat, rows, max(a.shape[1] for a in flat)


def _pack_small(groups, device):
    flat, rows, cols = _small_layout(groups)

    def body(dev_ref, *refs):
        o_ref = refs[-1]
        o_ref[...] = jnp.zeros_like(o_ref)
        r0 = 0
        for a_ref in refs[:-1]:
            r, w = a_ref.shape
            o_ref[r0:r0 + r, 0:w] = a_ref[...]
            r0 += r

    return pl.pallas_call(
        body,
        name="small_pack",
        grid_spec=pltpu.PrefetchScalarGridSpec(
            num_scalar_prefetch=1,
            grid=(1,),
            in_specs=[pl.BlockSpec(a.shape, lambda i, dev_ref: (0, 0)) for a in flat],
            out_specs=pl.BlockSpec((None, rows, cols), lambda i, dev_ref: (dev_ref[0], 0, 0)),
        ),
        out_shape=jax.ShapeDtypeStruct((N_DEV, rows, cols), F32),
        compiler_params=_cparams(("arbitrary",)),
    )(device, *flat)


def _small_copies(refs, send_sems, recv_sems):
    x, y, c, _ = _place()
    me = 4 * x + 2 * y + c
    out = []
    for k in range(1, N_DEV):
        peer = (x ^ (k >> 2), y ^ ((k >> 1) & 1), c ^ (k & 1))
        out.append(_rcopy(refs[0].at[me], refs[0].at[me], send_sems.at[k - 1], recv_sems.at[k - 1], peer))
    return out


def _sum_small(slots, groups, widths):
    out_shapes = [(sum(a.shape[0] for a in g), wd or g[0].shape[1]) for g, wd in zip(groups, widths)]

    def body(s_ref, *refs):
        outs, acc_ref = refs[:-1], refs[-1]
        acc = s_ref[0]
        for d in range(1, N_DEV):
            acc = acc + s_ref[d]
        acc_ref[...] = acc
        r0 = 0
        for o_ref in outs:
            r, w = o_ref.shape
            o_ref[...] = acc_ref[r0:r0 + r, 0:w]
            r0 += r

    vmem = pl.BlockSpec(memory_space=pltpu.VMEM)
    return pl.pallas_call(
        body,
        name="small_sum",
        in_specs=[vmem],
        out_specs=[vmem] * len(groups),
        out_shape=[jax.ShapeDtypeStruct(s, F32) for s in out_shapes],
        scratch_shapes=[pltpu.VMEM(slots.shape[1:], F32)],
        compiler_params=pltpu.CompilerParams(vmem_limit_bytes=VMEM_LIMIT_BYTES),
    )(slots)


def _adamw_small(items):
    n = len(items)

    def body(*refs):
        for i in range(n):
            w_ref, m_ref, v_ref, g_ref = refs[4 * i:4 * i + 4]
            d_ref, nm_ref, nv_ref = refs[4 * n + 3 * i:4 * n + 3 * i + 3]
            d_ref[...], nm_ref[...], nv_ref[...] = _adamw_math(w_ref[...], m_ref[...], v_ref[...], g_ref[...])

    vmem = pl.BlockSpec(memory_space=pltpu.VMEM)
    outs = pl.pallas_call(
        body,
        name="adamw_small",
        in_specs=[vmem] * (4 * n),
        out_specs=[vmem] * (3 * n),
        out_shape=[jax.ShapeDtypeStruct(it[0].shape, F32) for it in items for _ in range(3)],
        compiler_params=pltpu.CompilerParams(vmem_limit_bytes=VMEM_LIMIT_BYTES),
    )(*[a for it in items for a in it])
    return [tuple(outs[3 * i:3 * i + 3]) for i in range(n)]


class _Reduction:
    def __init__(self, tag, grads, place):
        self.tag, self.n, self.place = tag, len(grads), place
        lands = [lax.empty((N_CHIPS, g.shape[1] // 2, g.shape[2]), F32) for g in grads]
        self._start("swap", list(grads) + lands, self.n, _swap_copies(self.n))

    def _start(self, stage, bufs, n_sem, copies):
        *self.flight, self.token = _start_copies(f"rs_{stage}_start_{self.tag}", bufs, n_sem, copies)

    def _landed(self, stage, after, copies):
        send_sems, recv_sems, bufs = self.flight
        return _wait_copies(f"rs_{stage}_wait_{self.tag}", bufs, send_sems, recv_sems, after, copies)

    def to_chips(self, after):
        n = self.n
        bufs = self._landed("swap", after, _swap_copies(n))
        sums = [_add_core_halves(g, o, self.place, name=f"rs_add_core_{self.tag}_{i}") for i, (g, o) in enumerate(zip(bufs[:n], bufs[n:]))]
        self.mine = [f for f, _ in sums]
        parts = [b for _, b in sums]
        lands = [lax.empty((3,) + p.shape[1:], BF16) for p in parts]
        self._start("send", parts + lands, 3 * n, _partial_copies(n))

    def to_core(self, after):
        n = self.n
        bufs = self._landed("send", after, _partial_copies(n))
        halves = [_add_chip_partials(f, o, self.place, name=f"rs_add_chip_{self.tag}_{i}") for i, (f, o) in enumerate(zip(self.mine, bufs[n:]))]
        self._start("share", halves, n, _share_copies(n))

    def finish(self, after):
        return [b.reshape((-1,) + b.shape[2:]) for b in self._landed("share", after, _share_copies(self.n))]


ELEM_ROWS = (256, 176, 128, 64, 32, 16, 8)


def _add_core_halves(grad, got, place, name):
    s, r, cols = grad.shape
    h = r // 2
    tr = _pick(h, ELEM_ROWS)

    def body(place_ref, g_ref, o_ref, f_ref, b_ref):
        acc = g_ref[...] + o_ref[...]
        b_ref[...] = acc.astype(BF16)

        @pl.when(pl.program_id(1) == place_ref[0])
        def _():
            f_ref[...] = acc

    blk = pl.BlockSpec((None, tr, cols), lambda i, k, place_ref: (k, i, 0))
    return pl.pallas_call(
        body,
        name=name,
        grid_spec=pltpu.PrefetchScalarGridSpec(
            num_scalar_prefetch=1,
            grid=(h // tr, s),
            in_specs=[pl.BlockSpec((None, None, tr, cols), lambda i, k, place_ref: (k, place_ref[1], i, 0)), blk],
            out_specs=[pl.BlockSpec((tr, cols), lambda i, k, place_ref: (i, 0)), blk],
        ),
        out_shape=[jax.ShapeDtypeStruct((h, cols), F32), jax.ShapeDtypeStruct((s, h, cols), BF16)],
        compiler_params=_cparams(("parallel", "arbitrary")),
    )(place, grad.reshape(s, 2, h, cols), got)


def _add_chip_partials(mine, got, place, name):
    h, cols = mine.shape
    tr = _pick(h, ELEM_ROWS)

    def body(place_ref, m_ref, g_ref, o_ref):
        acc = m_ref[...]
        for j in range(3):
            acc = acc + g_ref[j].astype(F32)
        o_ref[...] = acc

    return pl.pallas_call(
        body,
        name=name,
        grid_spec=pltpu.PrefetchScalarGridSpec(
            num_scalar_prefetch=1,
            grid=(h // tr,),
            in_specs=[
                pl.BlockSpec((tr, cols), lambda i, place_ref: (i, 0)),
                pl.BlockSpec((3, tr, cols), lambda i, place_ref: (0, i, 0)),
            ],
            out_specs=pl.BlockSpec((None, tr, cols), lambda i, place_ref: (place_ref[1], i, 0)),
        ),
        out_shape=jax.ShapeDtypeStruct((2, h, cols), F32),
        compiler_params=_cparams(("parallel",)),
    )(place, mine, got)


def _adamw_math(w, m, v, g):
    nm = ADAM_B1 * m + (1.0 - ADAM_B1) * g
    nv = ADAM_B2 * v + (1.0 - ADAM_B2) * (g * g)
    m_hat = nm * (1.0 / (1.0 - ADAM_B1 ** ADAM_STEP))
    v_hat = nv * (1.0 / (1.0 - ADAM_B2 ** ADAM_STEP))
    return -ADAM_LR * (m_hat / (jnp.sqrt(v_hat) + ADAM_EPS) + ADAM_WD * w), nm, nv


def _adamw_layer(w, m, v, g, layer, prev, name):
    nl, r, cols = w.shape
    tr = _pick(r, ELEM_ROWS)

    def body(w_ref, m_ref, v_ref, g_ref, *rest):
        go_ref, d_ref, nm_ref, nv_ref = rest[-4:]
        gv = g_ref[...]
        d_ref[...], nm_ref[...], nv_ref[...] = _adamw_math(w_ref[...], m_ref[...], v_ref[...], gv)
        go_ref[...] = gv

    lay = pl.BlockSpec((None, tr, cols), lambda i: (layer, i, 0))
    return pl.pallas_call(
        body,
        name=name,
        grid=(r // tr,),
        in_specs=[lay] * 3 + [pl.BlockSpec((tr, cols), lambda i: (i, 0))] + ([ANY] * 4 if prev else []),
        out_specs=[lay] * 4,
        out_shape=[jax.ShapeDtypeStruct((nl, r, cols), F32)] * 4,
        input_output_aliases={4 + k: k for k in range(4)} if prev else {},
        compiler_params=_cparams(("parallel",)),
    )(w, m, v, g, *(prev or ()))


def _adamw(w, m, v, g, name):
    r, cols = w.shape
    tr = _pick(r, ELEM_ROWS)

    def body(w_ref, m_ref, v_ref, g_ref, d_ref, nm_ref, nv_ref):
        d_ref[...], nm_ref[...], nv_ref[...] = _adamw_math(w_ref[...], m_ref[...], v_ref[...], g_ref[...])

    blk = pl.BlockSpec((tr, cols), lambda i: (i, 0))
    return pl.pallas_call(
        body,
        name=name,
        grid=(r // tr,),
        in_specs=[blk] * 4,
        out_specs=[blk] * 3,
        out_shape=[jax.ShapeDtypeStruct((r, cols), F32)] * 3,
        compiler_params=_cparams(("parallel",)),
    )(w, m, v, g)


SMALL_COLS = 384
SMALL_ROWS = 16


def _pad_rows(flat, rows, cols):
    return jnp.pad(flat, (0, rows * cols - flat.shape[0])).reshape(rows, cols)


def kernel(x, hg_norm, hg_w_in, hg_lb_logits, hg_out_norm, hg_w_out, kv_norm, w_kv, attn_norm, attn_w_q, attn_sinks, attn_w_o, ffn_norm, ffn_w_up, ffn_conv_w, ffn_conv_b, ffn_w_down, final_norm, loss_target, m_hg_norm, m_hg_w_in, m_hg_lb_logits, m_hg_out_norm, m_hg_w_out, m_kv_norm, m_w_kv, m_attn_norm, m_attn_w_q, m_attn_sinks, m_attn_w_o, m_ffn_norm, m_ffn_w_up, m_ffn_conv_w, m_ffn_conv_b, m_ffn_w_down, m_final_norm, v_hg_norm, v_hg_w_in, v_hg_lb_logits, v_hg_out_norm, v_hg_w_out, v_kv_norm, v_w_kv, v_attn_norm, v_attn_w_q, v_attn_sinks, v_attn_w_o, v_ffn_norm, v_ffn_w_up, v_ffn_conv_w, v_ffn_conv_b, v_ffn_w_down, v_final_norm):
    wts = dict(hg_norm=hg_norm, hg_w_in=hg_w_in, hg_lb_logits=hg_lb_logits, hg_out_norm=hg_out_norm, hg_w_out=hg_w_out, kv_norm=kv_norm, w_kv=w_kv, attn_norm=attn_norm, attn_w_q=attn_w_q, attn_sinks=attn_sinks, attn_w_o=attn_w_o, ffn_norm=ffn_norm, ffn_w_up=ffn_w_up, ffn_conv_w=ffn_conv_w, ffn_conv_b=ffn_conv_b, ffn_w_down=ffn_w_down, final_norm=final_norm)
    mom1 = dict(hg_norm=m_hg_norm, hg_w_in=m_hg_w_in, hg_lb_logits=m_hg_lb_logits, hg_out_norm=m_hg_out_norm, hg_w_out=m_hg_w_out, kv_norm=m_kv_norm, w_kv=m_w_kv, attn_norm=m_attn_norm, attn_w_q=m_attn_w_q, attn_sinks=m_attn_sinks, attn_w_o=m_attn_w_o, ffn_norm=m_ffn_norm, ffn_w_up=m_ffn_w_up, ffn_conv_w=m_ffn_conv_w, ffn_conv_b=m_ffn_conv_b, ffn_w_down=m_ffn_w_down, final_norm=m_final_norm)
    mom2 = dict(hg_norm=v_hg_norm, hg_w_in=v_hg_w_in, hg_lb_logits=v_hg_lb_logits, hg_out_norm=v_hg_out_norm, hg_w_out=v_hg_w_out, kv_norm=v_kv_norm, w_kv=v_w_kv, attn_norm=v_attn_norm, attn_w_q=v_attn_w_q, attn_sinks=v_attn_sinks, attn_w_o=v_attn_w_o, ffn_norm=v_ffn_norm, ffn_w_up=v_ffn_w_up, ffn_conv_w=v_ffn_conv_w, ffn_conv_b=v_ffn_conv_b, ffn_w_down=v_ffn_w_down, final_norm=v_final_norm)
    names = list(wts)
    chip = 2 * lax.axis_index("x") + lax.axis_index("y")
    core = lax.axis_index("c")
    fs = D_FF // N_CHIPS
    ds = D_MODEL // N_CHIPS

    place_arr = jnp.stack([chip, core]).astype(jnp.int32)
    small = jnp.concatenate([hg_norm.reshape(-1), hg_lb_logits.reshape(-1), ffn_conv_w.reshape(-1)])
    n_small = small.shape[0]
    shards = [
        ("small", _pad_rows(small, SMALL_ROWS, SMALL_COLS), F32, None), ("hg_w_in", hg_w_in, BF16, 0),
        ("hg_w_out", hg_w_out, BF16, 0), ("ffn_w_up0", ffn_w_up, BF16, 0), ("ffn_w_down0", ffn_w_down, BF16, 0),
        ("w_kv", w_kv, BF16, None), ("attn_w_q", attn_w_q, BF16, 0), ("attn_w_o", attn_w_o, BF16, 0),
        ("ffn_w_up1", ffn_w_up, BF16, 1), ("ffn_w_down1", ffn_w_down, BF16, 1),
    ]
    n_first = 3
    spans = dict(layer0=(0, 2), layer1=(2, 7))

    def first_copies(refs, send_sems, recv_sems):
        return (_gather_copies(0, 1)(refs[:1], send_sems, recv_sems) + _gather_half_copies(1, 1, True)(refs[1:2], send_sems, recv_sems)
                + _gather_copies(2, 1)(refs[2:3], send_sems, recv_sems))

    placed = [_place_shard(s, place_arr, dt, name=f"place_{nm}", layer=ly) for nm, s, dt, ly in shards[:n_first]]
    first = _start_copies("gather_start_first", placed, 3 * n_first, first_copies)
    placed = [_place_shard(s, place_arr, dt, name=f"place_{nm}", deps=(first[3],), layer=ly) for nm, s, dt, ly in shards[n_first:]]
    rest = _start_copies("gather_start_rest", placed, 3 * len(placed), _gather_half_copies(0, len(placed), True))
    relayed = {}

    def fetch(w, stage, after):
        if stage == "first":
            w_in = _relay_copies("gather_first_relay", first[2][1:2], first[0], first[1], after,
                                 _gather_half_copies(1, 1, True), 3, _gather_half_copies(0, 1, False))
            got = _wait_copies("gather_wait_small", first[2][:1], first[0], first[1], w_in[3], _gather_copies(0, 1))
            got += _wait_copies("gather_wait_first", w_in[2], w_in[0], w_in[1], got[0], _gather_half_copies(0, 1, False))
        elif stage == "mixer_out":
            got = _wait_copies("gather_wait_mixer_out", first[2][2:], first[0], first[1], after, _gather_copies(2, 1))
        elif stage.endswith("_relay"):
            lo, hi = spans[stage[:-6]]
            relayed[stage[:-6]] = _relay_copies(
                f"gather_{stage}", rest[2][lo:hi], rest[0], rest[1], after,
                _gather_half_copies(lo, hi - lo, True), 3 * (hi - lo), _gather_half_copies(0, hi - lo, False))
            return w
        else:
            lo, hi = spans[stage]
            send_sems, recv_sems, bufs, _ = relayed[stage]
            got = _wait_copies(f"gather_wait_{stage}", bufs, send_sems, recv_sems, after, _gather_half_copies(0, hi - lo, False))
        w = dict(w)
        if stage == "first":
            g_small = got[0].reshape(N_CHIPS, -1)[:, :n_small]
            conv_w = g_small[:, 3 * ds:].reshape(N_CHIPS, 2, 3, fs).transpose(1, 2, 0, 3).reshape(2, 3, D_FF)
            w.update(
                hg_norm=g_small[:, :ds].reshape(1, D_MODEL),
                hg_lb=g_small[:, ds:3 * ds].reshape(N_CHIPS, 2, ds).transpose(1, 0, 2).reshape(2, D_MODEL),
                ffn_conv_w=[conv_w[0], conv_w[1]], hg_w_in=got[1],
            )
        elif stage == "mixer_out":
            w.update(hg_w_out=got[0].reshape(1, D_MODEL, D_MODEL))
        elif stage == "layer0":
            w.update(ffn_w_up=[got[0], None], ffn_w_down=[got[1].reshape(1, D_FF, D_MODEL), None])
        else:
            w.update(
                w_kv=got[0].reshape(1, D_MODEL, 2 * LANES), attn_w_q=got[1].reshape(1, D_MODEL, D_MODEL),
                attn_w_o=got[2].reshape(1, D_MODEL, D_MODEL), ffn_w_up=[w["ffn_w_up"][0], got[3]],
                ffn_w_down=[w["ffn_w_down"][0], got[4].reshape(1, D_FF, D_MODEL)],
            )
        return w

    whole = dict(
        hg_out_norm=hg_out_norm, kv_norm=kv_norm.reshape(1, D_MODEL), attn_norm=attn_norm, attn_sinks=attn_sinks.reshape(ATT_QH),
        ffn_norm=[ffn_norm[0:1], ffn_norm[1:2]], ffn_conv_b=[ffn_conv_b[0:1], ffn_conv_b[1:2]], final_norm=final_norm.reshape(1, D_MODEL),
    )
    whole = fetch(whole, "first", rest[3])

    red, layer1 = {}, {}

    def by_rows(g, rows):
        return g.reshape(N_CHIPS, rows, g.shape[2])

    def hook(point, dh, grads):
        if point == "ffn1":
            red["ffn1"] = _Reduction("ffn1", [by_rows(grads["ffn_w_down"], fs), grads["ffn_w_up"]], place_arr)
            return (red["ffn1"].token,)
        if point == "attn":
            red["ffn1"].to_chips(dh)
            layer1.update(grads)
            return (red["ffn1"].token,)
        if point == "ffn0":
            group = [by_rows(layer1["attn_w_o"], ds), by_rows(layer1["attn_w_q"], ds), by_rows(layer1["w_kv"], ds),
                     by_rows(grads["ffn_w_down"], fs), grads["ffn_w_up"], by_rows(grads["hg_w_out"], ds)]
            red["mid"] = _Reduction("mid", group, place_arr)
            return (red["mid"].token,)
        if point == "hgrn":
            red["ffn1"].to_core(dh)
            red["mid"].to_chips(dh)
            return (red["ffn1"].token, red["mid"].token)
        red["hg"] = _Reduction("hg", [grads["hg_w_in"]], place_arr)
        return (red["hg"].token,)

    loss, dx, grads = _local_step(x[0], loss_target[0], whole, fetch, hook)

    small_names = ["hg_out_norm", "attn_sinks", "kv_norm", "attn_norm", "ffn_norm", "ffn_conv_b", "final_norm", "hg_norm", "hg_lb_logits", "ffn_conv_w"]
    groups = [[loss]] + [grads[n] if isinstance(grads[n], list) else [grads[n]] for n in small_names[:-2]] + [[grads["hg_lb"]], grads["ffn_conv_w"]]
    widths = [None, None, ATT_QH] + [None] * 8
    packed = _pack_small(groups, jnp.reshape(2 * chip + core, (1,)).astype(jnp.int32))
    small_flight = _start_copies("small_start", [packed], N_DEV - 1, _small_copies)
    red["hg"].to_chips(small_flight[3])

    out_g, out_d, out_m, out_v = {}, {}, {}, {}

    def update(name, g2):
        shape = wts[name].shape
        d2, m2, v2 = _adamw(wts[name].reshape(g2.shape), mom1[name].reshape(g2.shape), mom2[name].reshape(g2.shape), g2, name=f"adamw_{name}")
        out_g[name], out_d[name], out_m[name], out_v[name] = g2.reshape(shape), d2.reshape(shape), m2.reshape(shape), v2.reshape(shape)
        return d2

    def update_layer(name, g2, layer, prev):
        res = _adamw_layer(wts[name], mom1[name], mom2[name], g2, layer, prev, name=f"adamw_{name}{layer}")
        out_g[name], out_d[name], out_m[name], out_v[name] = res
        return res

    g_down1, g_up1 = red["ffn1"].finish(red["hg"].token)
    up1 = update_layer("ffn_w_up", g_up1, 1, None)
    summed = _sum_small(_wait_copies("small_wait", small_flight[2], small_flight[0], small_flight[1], up1[3], _small_copies)[0], groups, widths)
    loss_out = summed[0][0, 0]
    small_grads = dict(zip(small_names, summed[1:]))
    small_grads["hg_norm"] = lax.dynamic_slice(small_grads["hg_norm"], (0, chip * ds), (1, ds))
    small_grads["hg_lb_logits"] = lax.dynamic_slice(small_grads["hg_lb_logits"], (0, chip * ds), (2, ds))
    small_grads["ffn_conv_w"] = lax.dynamic_slice(small_grads["ffn_conv_w"], (0, chip * fs), (2 * 3, fs))
    red["mid"].to_core(up1[1])
    down1 = update_layer("ffn_w_down", g_down1, 1, None)
    g_o, g_q, g_kv, g_down0, g_up0, g_out = red["mid"].finish(down1[1])
    update("attn_w_o", g_o)
    update("attn_w_q", g_q)
    update("w_kv", g_kv)
    update("hg_w_out", g_out)
    update_layer("ffn_w_down", g_down0, 0, down1)
    last = update_layer("ffn_w_up", g_up0, 0, up1)
    red["hg"].to_core(last[1])
    (g_in,) = red["hg"].finish(last[2])
    update("hg_w_in", g_in)

    as_2d = lambda a, n: a.reshape(small_grads[n].shape)
    updated = _adamw_small([(as_2d(wts[n], n), as_2d(mom1[n], n), as_2d(mom2[n], n), small_grads[n]) for n in small_names])
    for n, (d2, m2, v2) in zip(small_names, updated):
        shape = wts[n].shape
        out_g[n], out_d[n], out_m[n], out_v[n] = small_grads[n].reshape(shape), d2.reshape(shape), m2.reshape(shape), v2.reshape(shape)

    grad_x = dx.reshape(x.shape)
    return (loss_out, grad_x, *[out_g[n] for n in names], *[out_d[n] for n in names], *[out_m[n] for n in names], *[out_v[n] for n in names])
```

```python
import functools

import jax
import jax.numpy as jnp
from jax import lax
from jax.experimental import pallas as pl
from jax.experimental.pallas import tpu as pltpu

F32 = jnp.float32
BF16 = jnp.bfloat16
MESH = pl.DeviceIdType.MESH

EPS = 1e-6
D_MODEL = 1024
HG_HEADS = 8
HG_DK = 128
HG_CHUNK = 64
ATT_HD = 64
ATT_QH = 16
ATT_KVH = 2
ATT_GROUP = ATT_QH // ATT_KVH
WINDOW = 128
D_FF = 2816
N_CHIPS = 4
N_DEV = 8
LANES = 128
SUBLANES = 8
VMEM_LIMIT_BYTES = 56 * 1024 * 1024
NEG = -1e30
ALIBI_SLOPES = tuple(2.0 ** (-8.0 * h / ATT_QH) for h in range(1, ATT_QH + 1))

ADAM_LR = 0.001
ADAM_B1 = 0.9
ADAM_B2 = 0.999
ADAM_EPS = 1e-08
ADAM_WD = 0.01
ADAM_STEP = 10


def _cparams(sem=None):
    return pltpu.CompilerParams(dimension_semantics=sem, vmem_limit_bytes=VMEM_LIMIT_BYTES)


def _pick(n, cands):
    for c in cands:
        if n % c == 0:
            return c
    return n


def _sigmoid(x):
    return 0.5 * jnp.tanh(0.5 * x) + 0.5


def _dot(a, b, dims):
    return lax.dot_general(a, b, (dims, ((), ())), preferred_element_type=F32)


NN = ((1,), (0,))
NT = ((1,), (1,))
TN = ((0,), (0,))


MM_ROWS = 1024


def _rms_stats(xv):
    rstd = lax.rsqrt(jnp.mean(xv * xv, axis=-1, keepdims=True) + EPS)
    return xv * rstd, rstd


def _mm_operand(a_ref, gain_ref):
    if gain_ref is None:
        return a_ref[...].astype(BF16)
    return (_rms_stats(a_ref[...])[0] * gain_ref[...]).astype(BF16)


def _mm_nn(a, w, res=None, out_dtype=F32, name="mm_nn", gain=None):
    m, k = a.shape
    s, _, ns = w.shape
    tm = min(m, MM_ROWS)
    tn = _pick(ns, (1024, 1408, 512, 256, 128))
    npb = ns // tn

    def body(a_ref, w_ref, *rest):
        o_ref = rest[-1]
        acc = _dot(_mm_operand(a_ref, rest[0] if gain is not None else None), w_ref[...], NN)
        if res is not None:
            acc = acc + rest[-2][...]
        o_ref[...] = acc.astype(o_ref.dtype)

    in_specs = [
        pl.BlockSpec((tm, k), lambda i, j: (i, 0)),
        pl.BlockSpec((None, k, tn), lambda i, j: (j // npb, 0, j % npb)),
    ]
    args = [a, w]
    if gain is not None:
        in_specs.append(pl.BlockSpec((1, k), lambda i, j: (0, 0)))
        args.append(gain)
    if res is not None:
        in_specs.append(pl.BlockSpec((tm, tn), lambda i, j: (i, j)))
        args.append(res)
    return pl.pallas_call(
        body,
        name=name,
        grid=(m // tm, s * npb),
        in_specs=in_specs,
        out_specs=pl.BlockSpec((tm, tn), lambda i, j: (i, j)),
        out_shape=jax.ShapeDtypeStruct((m, s * ns), out_dtype),
        compiler_params=_cparams(("parallel", "parallel")),
    )(*args)


def _dy_spec(stacked, tm, tn, npb, row, kk):
    if stacked:
        return pl.BlockSpec((None, tm, tn), lambda *g: (kk(g) // npb, row(g), kk(g) % npb))
    return pl.BlockSpec((tm, tn), lambda *g: (row(g), kk(g)))


def _dep_specs(deps):
    return [pl.BlockSpec(d.shape, lambda *g: (0, 0)) for d in deps]


def _mm_nt(dy, w, stacked=False, out_dtype=F32, name="mm_nt", deps=(), norm_of=None, also=None):
    s, k, ns = w.shape
    m = dy.shape[1] if stacked else dy.shape[0]
    tm = min(m, MM_ROWS)
    tko = _pick(k, (1024, 1408, 512, 256))
    tn = _pick(ns, (1024, 1408, 512, 256))
    npb = ns // tn
    nk = s * npb
    fused = norm_of is not None
    assert not fused or tko == k
    assert also is None or fused

    def body(dy_ref, w_ref, *rest):
        acc_ref = rest[-1]
        i, kk = pl.program_id(0), pl.program_id(2)

        @pl.when(kk == 0)
        def _():
            acc_ref[...] = jnp.zeros_like(acc_ref)

        acc_ref[...] += _dot(dy_ref[...].astype(BF16), w_ref[...], NT)

        if not fused:
            @pl.when(kk == nk - 1)
            def _():
                rest[-2][...] = acc_ref[...].astype(rest[-2].dtype)
            return
        x_ref, g_ref, dres_ref = rest[:3]
        n_out = 3 if also is not None else 2
        dx_ref, dg_refs = rest[-1 - n_out], rest[-n_out:-1]

        @pl.when(jnp.logical_and(i == 0, kk == 0))
        def _():
            for dg_ref in dg_refs:
                dg_ref[...] = jnp.zeros_like(dg_ref)

        @pl.when(kk == nk - 1)
        def _():
            dxn = acc_ref[...]
            xhat, rstd = _rms_stats(x_ref[...])
            gd = dxn * g_ref[...]
            dg_refs[0][...] += jnp.sum(dxn * xhat, axis=0, keepdims=True)
            if also is not None:
                dy2_ref, w2_ref, g2_ref = rest[3:6]
                dxn2 = _dot(dy2_ref[...].astype(BF16), w2_ref[...], NT)
                gd = gd + dxn2 * g2_ref[...]
                dg_refs[1][...] += jnp.sum(dxn2 * xhat, axis=0, keepdims=True)
            dx_ref[...] = dres_ref[...] + rstd * (gd - xhat * jnp.mean(gd * xhat, axis=-1, keepdims=True))

    row = pl.BlockSpec((tm, tko), lambda i, j, kk: (i, j))
    vec = pl.BlockSpec((1, k), lambda i, j, kk: (0, 0))
    extra_in, extra_args = [], ()
    if fused:
        extra_in, extra_args = [row, vec, row], tuple(norm_of)
    if also is not None:
        n2 = also[0].shape[1]
        extra_in += [pl.BlockSpec((tm, n2), lambda i, j, kk: (i, 0)), pl.BlockSpec((None, k, n2), lambda i, j, kk: (0, 0, 0)), vec]
        extra_args += tuple(also)
    f32 = lambda shape: jax.ShapeDtypeStruct(shape, F32)
    return pl.pallas_call(
        body,
        name=name,
        grid=(m // tm, k // tko, nk),
        in_specs=[
            _dy_spec(stacked, tm, tn, npb, lambda g: g[0], lambda g: g[2]),
            pl.BlockSpec((None, tko, tn), lambda i, j, kk: (kk // npb, j, kk % npb)),
        ] + extra_in + _dep_specs(deps),
        out_specs=([row, vec] + ([vec] if also is not None else [])) if fused else row,
        out_shape=([f32((m, k)), f32((1, k))] + ([f32((1, k))] if also is not None else [])) if fused else jax.ShapeDtypeStruct((m, k), out_dtype),
        scratch_shapes=[pltpu.VMEM((tm, tko), F32)],
        compiler_params=_cparams(("arbitrary",) * 3 if fused else ("parallel", "parallel", "arbitrary")),
    )(dy, w, *extra_args, *deps)


def _mm_tn(a, dy, s, ns, stacked=False, name="mm_tn", deps=(), gain=None):
    m, k = a.shape
    tm = min(m, MM_ROWS)
    tk = _pick(k, (1024, 1408, 512, 256))
    tn = _pick(ns, (1024, 1408, 512, 256, 128))
    npb = ns // tn
    nm = m // tm
    assert gain is None or tk == k

    def body(a_ref, dy_ref, *rest):
        j, mm = pl.program_id(1), pl.program_id(2)
        if gain is None:
            o_ref, acc_ref = rest[-2:]
            lhs = a_ref[...].astype(BF16)
        else:
            o_ref, acc_ref, xn_ref = rest[-3:]

            @pl.when(j == 0)
            def _():
                xn_ref[mm] = _mm_operand(a_ref, rest[0])

            lhs = xn_ref[mm]

        @pl.when(mm == 0)
        def _():
            acc_ref[...] = jnp.zeros_like(acc_ref)

        acc_ref[...] += _dot(lhs, dy_ref[...].astype(BF16), TN)

        @pl.when(mm == nm - 1)
        def _():
            o_ref[...] = acc_ref[...]

    a_rows = (lambda i, j, mm: (mm, i)) if gain is None else (lambda i, j, mm: (jnp.where(j == 0, mm, 0), i))
    return pl.pallas_call(
        body,
        name=name,
        grid=(k // tk, s * npb, nm),
        in_specs=[
            pl.BlockSpec((tm, tk), a_rows),
            _dy_spec(stacked, tm, tn, npb, lambda g: g[2], lambda g: g[1]),
        ] + ([pl.BlockSpec((1, k), lambda i, j, mm: (0, 0))] if gain is not None else []) + _dep_specs(deps),
        out_specs=pl.BlockSpec((None, tk, tn), lambda i, j, mm: (j // npb, i, j % npb)),
        out_shape=jax.ShapeDtypeStruct((s, k, ns), F32),
        scratch_shapes=[pltpu.VMEM((tk, tn), F32)] + ([pltpu.VMEM((nm, tm, tk), BF16)] if gain is not None else []),
        compiler_params=_cparams(("parallel", "arbitrary", "arbitrary") if gain is not None else ("parallel", "parallel", "arbitrary")),
    )(a, dy, *(() if gain is None else (gain,)), *deps)


ROW_TILE = 512


def _loss_head(h, g, target):
    t, d = h.shape
    r = min(t, ROW_TILE)

    def body(h_ref, g_ref, t_ref, dh_ref, dg_ref, loss_ref):
        @pl.when(pl.program_id(0) == 0)
        def _():
            dg_ref[...] = jnp.zeros_like(dg_ref)
            loss_ref[...] = jnp.zeros_like(loss_ref)

        xv = h_ref[...]
        rstd = lax.rsqrt(jnp.mean(xv * xv, axis=-1, keepdims=True) + EPS)
        xhat = xv * rstd
        gv = g_ref[...]
        err = xhat * gv - t_ref[...]
        loss_ref[...] += 0.5 * jnp.sum(jnp.mean(err * err, axis=-1, keepdims=True), axis=0, keepdims=True)
        dy = err * (1.0 / d)
        gd = dy * gv
        dh_ref[...] = rstd * (gd - xhat * jnp.mean(gd * xhat, axis=-1, keepdims=True))
        dg_ref[...] += jnp.sum(dy * xhat, axis=0, keepdims=True)

    return pl.pallas_call(
        body,
        name="loss_head",
        grid=(t // r,),
        in_specs=[
            pl.BlockSpec((r, d), lambda i: (i, 0)),
            pl.BlockSpec((1, d), lambda i: (0, 0)),
            pl.BlockSpec((r, d), lambda i: (i, 0)),
        ],
        out_specs=[
            pl.BlockSpec((r, d), lambda i: (i, 0)),
            pl.BlockSpec((1, d), lambda i: (0, 0)),
            pl.BlockSpec((1, LANES), lambda i: (0, 0)),
        ],
        out_shape=[
            jax.ShapeDtypeStruct((t, d), F32),
            jax.ShapeDtypeStruct((1, d), F32),
            jax.ShapeDtypeStruct((1, LANES), F32),
        ],
        compiler_params=_cparams(("arbitrary",)),
    )(h, g, target)


CONV_ROWS = 512
CONV_COLS = 1408


def _conv_taps(x_ext, n):
    tot = x_ext.shape[0]
    g1 = pltpu.roll(x_ext, 1, 0)[tot - n:]
    g2 = pltpu.roll(x_ext, 2, 0)[tot - n:]
    return g2, g1


def _conv_fwd(up, conv_w, conv_b, name="conv_fwd"):
    t = up.shape[0]
    r = min(t, CONV_ROWS)
    tc = CONV_COLS
    ncb = D_FF // tc
    hb = r // SUBLANES

    def body(g_ref, halo_ref, v_ref, w_ref, b_ref, o_ref, c_ref):
        i = pl.program_id(1)
        g0 = g_ref[...]
        halo = halo_ref[...] * jnp.where(i > 0, 1.0, 0.0)
        g2, g1 = _conv_taps(jnp.concatenate([halo, g0], axis=0), r)
        c = b_ref[...] + w_ref[0:1, :] * g2 + w_ref[1:2, :] * g1 + w_ref[2:3, :] * g0
        c_ref[...] = c.astype(BF16)
        o_ref[...] = (c * _sigmoid(c) * v_ref[...]).astype(BF16)

    blk = pl.BlockSpec((r, tc), lambda j, i: (i, j))
    return pl.pallas_call(
        body,
        name=name,
        grid=(ncb, t // r),
        in_specs=[
            blk,
            pl.BlockSpec((SUBLANES, tc), lambda j, i: (jnp.maximum(i * hb - 1, 0), j)),
            pl.BlockSpec((r, tc), lambda j, i: (i, ncb + j)),
            pl.BlockSpec((3, tc), lambda j, i: (0, j)),
            pl.BlockSpec((1, tc), lambda j, i: (0, j)),
        ],
        out_specs=[blk, blk],
        out_shape=[jax.ShapeDtypeStruct((t, D_FF), BF16), jax.ShapeDtypeStruct((t, D_FF), BF16)],
        compiler_params=_cparams(("parallel", "parallel")),
    )(up, up, up, conv_w, conv_b)


def _conv_bwd(up, conv_w, c, dact, name="conv_bwd"):
    t = up.shape[0]
    r = min(t, CONV_ROWS)
    tc = CONV_COLS
    ncb = D_FF // tc
    nrt = t // r

    def body(g_ref, v_ref, w_ref, c_ref, da_ref, dup_ref, dw_ref, db_ref, nxt_ref):
        ii = pl.program_id(1)

        @pl.when(ii == 0)
        def _():
            nxt_ref[...] = jnp.zeros_like(nxt_ref)
            dw_ref[...] = jnp.zeros_like(dw_ref)
            db_ref[...] = jnp.zeros_like(db_ref)

        g0 = g_ref[...]
        w0, w1, w2 = w_ref[0:1, :], w_ref[1:2, :], w_ref[2:3, :]
        c = c_ref[...].astype(F32)
        sg = _sigmoid(c)
        da = da_ref[...]
        dup_ref[1] = (da * (c * sg)).astype(BF16)
        dc = da * v_ref[...] * (sg * (1.0 + c * (1.0 - sg)))
        ext = jnp.concatenate([dc, nxt_ref[...]], axis=0)
        tot = r + SUBLANES
        d1 = pltpu.roll(ext, tot - 1, 0)[:r]
        d2 = pltpu.roll(ext, tot - 2, 0)[:r]
        nxt_ref[...] = dc[:SUBLANES]
        dup_ref[0] = (w2 * dc + w1 * d1 + w0 * d2).astype(BF16)
        db_ref[...] += jnp.sum(dc, axis=0, keepdims=True)
        dw_ref[0:1, :] += jnp.sum(d2 * g0, axis=0, keepdims=True)
        dw_ref[1:2, :] += jnp.sum(d1 * g0, axis=0, keepdims=True)
        dw_ref[2:3, :] += jnp.sum(dc * g0, axis=0, keepdims=True)

    rev = lambda ii: nrt - 1 - ii
    dup, dw, db = pl.pallas_call(
        body,
        name=name,
        grid=(ncb, nrt),
        in_specs=[
            pl.BlockSpec((r, tc), lambda j, ii: (rev(ii), j)),
            pl.BlockSpec((r, tc), lambda j, ii: (rev(ii), ncb + j)),
            pl.BlockSpec((3, tc), lambda j, ii: (0, j)),
            pl.BlockSpec((r, tc), lambda j, ii: (rev(ii), j)),
            pl.BlockSpec((r, tc), lambda j, ii: (rev(ii), j)),
        ],
        out_specs=[
            pl.BlockSpec((2, None, r, tc), lambda j, ii: (0, j, rev(ii), 0)),
            pl.BlockSpec((3, tc), lambda j, ii: (0, j)),
            pl.BlockSpec((1, tc), lambda j, ii: (0, j)),
        ],
        out_shape=[
            jax.ShapeDtypeStruct((2, ncb, t, tc), BF16),
            jax.ShapeDtypeStruct((3, D_FF), F32),
            jax.ShapeDtypeStruct((1, D_FF), F32),
        ],
        scratch_shapes=[pltpu.VMEM((SUBLANES, tc), F32)],
        compiler_params=_cparams(("parallel", "arbitrary")),
    )(up, up, conv_w, c, dact)
    return dup.reshape(2 * ncb, t, tc), dw, db


def _split3(x):
    x1 = x.astype(BF16)
    r1 = x - x1.astype(F32)
    x2 = r1.astype(BF16)
    x3 = (r1 - x2.astype(F32)).astype(BF16)
    return x1, x2, x3


def _tri_dot(tri, x, dims):
    x1, x2, x3 = _split3(x)
    return _dot(tri, x1, dims) + _dot(tri, x2, dims) + _dot(tri, x3, dims)


def _lower_bound(logits_ref):
    return _sigmoid(logits_ref[0:1, :] - logits_ref[1:2, :])


def _hg_gates(qr, fr, lb):
    q = qr * _sigmoid(qr) * (HG_DK ** -0.5)
    sf = _sigmoid(fr)
    fg = lb + (1.0 - lb) * sf
    return q, sf, fg


def _hg_chunk_terms(q, fg, tril_b, low_half):
    g = jnp.log(fg)
    k = 1.0 - fg
    cum = _tri_dot(tril_b, g, NN)
    c_last = jnp.sum(g, axis=0, keepdims=True)
    c_mid = jnp.sum(jnp.where(low_half, g, 0.0), axis=0, keepdims=True)
    e_q = jnp.exp(cum - c_mid)
    e_k = jnp.exp(c_mid - cum)
    e_0 = jnp.exp(cum)
    e_l = jnp.exp(c_last - cum)
    return k, e_q, e_k, e_0, e_l, jnp.exp(c_last)


HG_BLOCK = 256


def _hg_proj_specs(rb, row):
    return [pl.BlockSpec((rb, D_MODEL), functools.partial(lambda i, k: (row(i), k), k=k)) for k in range(4)]


def _hg_consts(c):
    tril = lax.broadcasted_iota(jnp.int32, (c, c), 0) >= lax.broadcasted_iota(jnp.int32, (c, c), 1)
    low_half = lax.broadcasted_iota(jnp.int32, (c, D_MODEL), 0) < c // 2
    return tril, tril.astype(BF16), low_half


def _hgrn_fwd(proj, lb, wn):
    t = proj.shape[0]
    c = HG_CHUNK
    rb = min(t, HG_BLOCK)
    cpb = rb // c

    def body(q_ref, f_ref, i_ref, g_ref, lb_ref, wn_ref, o_ref, y_ref, st_ref, s_scr):
        @pl.when(pl.program_id(0) == 0)
        def _():
            s_scr[...] = jnp.zeros_like(s_scr)

        lb_all = _lower_bound(lb_ref)
        wnv = wn_ref[...]
        tril, tril_b, low_half = _hg_consts(c)

        def chunk(n, carry):
            rows = pl.ds(pl.multiple_of(n * c, c), c)
            q, _, fg = _hg_gates(q_ref[rows, :], f_ref[rows, :], lb_all)
            k, e_q, e_k, e_0, e_l, e_last = _hg_chunk_terms(q, fg, tril_b, low_half)
            qi, ki, q0, kl = (q * e_q).astype(BF16), (k * e_k).astype(BF16), (q * e_0).astype(BF16), (k * e_l).astype(BF16)
            v = i_ref[rows, :].astype(BF16)
            gr = g_ref[rows, :]
            gate = gr * _sigmoid(gr)
            for h in range(HG_HEADS):
                cols = slice(h * HG_DK, (h + 1) * HG_DK)
                st = s_scr[h]
                st_ref[h, n] = st
                a = jnp.where(tril, _dot(qi[:, cols], ki[:, cols], NT), 0.0)
                o = _dot(q0[:, cols], st.astype(BF16), NT) + _dot(a.astype(BF16), v[:, cols], NN)
                s_scr[h] = st * e_last[:, cols] + _dot(v[:, cols], kl[:, cols], TN)
                o_ref[rows, cols] = o
                rstd = lax.rsqrt(jnp.mean(o * o, axis=-1, keepdims=True) + EPS)
                y_ref[rows, cols] = (o * rstd * wnv * gate[:, cols]).astype(BF16)
            return carry

        lax.fori_loop(0, cpb, chunk, 0, unroll=2)

    blk = pl.BlockSpec((rb, D_MODEL), lambda i: (i, 0))
    return pl.pallas_call(
        body,
        name="hgrn_fwd",
        grid=(t // rb,),
        in_specs=_hg_proj_specs(rb, lambda i: i) + [pl.BlockSpec((2, D_MODEL), lambda i: (0, 0)), pl.BlockSpec((1, HG_DK), lambda i: (0, 0))],
        out_specs=[blk, blk, pl.BlockSpec((HG_HEADS, cpb, HG_DK, HG_DK), lambda i: (0, i, 0, 0))],
        out_shape=[
            jax.ShapeDtypeStruct((t, D_MODEL), F32),
            jax.ShapeDtypeStruct((t, D_MODEL), BF16),
            jax.ShapeDtypeStruct((HG_HEADS, t // c, HG_DK, HG_DK), F32),
        ],
        scratch_shapes=[pltpu.VMEM((HG_HEADS, HG_DK, HG_DK), F32)],
        compiler_params=_cparams(("arbitrary",)),
    )(proj, proj, proj, proj, lb, wn)


def _hgrn_bwd(proj, lb, wn, o, states, dy):
    t = proj.shape[0]
    c = HG_CHUNK
    rb = min(t, HG_BLOCK)
    cpb = rb // c
    nb = t // rb

    def body(q_ref, f_ref, i_ref, g_ref, lb_ref, wn_ref, o_ref, st_ref, dy_ref, dp_ref, dl_ref, dwn_ref, ds_scr, dlb_scr):
        step = pl.program_id(0)

        @pl.when(step == 0)
        def _():
            dwn_ref[...] = jnp.zeros_like(dwn_ref)
            ds_scr[...] = jnp.zeros_like(ds_scr)
            dlb_scr[...] = jnp.zeros_like(dlb_scr)

        lb_all = _lower_bound(lb_ref)
        wnv = wn_ref[...]
        tril, tril_b, low_half = _hg_consts(c)

        def chunk(nn, carry):
            n = cpb - 1 - nn
            rows = pl.ds(pl.multiple_of(n * c, c), c)
            qr = q_ref[rows, :]
            gr = g_ref[rows, :]
            q, sf, fg = _hg_gates(qr, f_ref[rows, :], lb_all)
            k, e_q, e_k, e_0, e_l, e_last = _hg_chunk_terms(q, fg, tril_b, low_half)
            qi, qi_lo, _ = _split3(q * e_q)
            ki, ki_lo, _ = _split3(k * e_k)
            q0 = (q * e_0).astype(BF16)
            kl = (k * e_l).astype(BF16)
            v = i_ref[rows, :].astype(BF16)
            sg = _sigmoid(gr)
            silu_g = gr * sg
            dsilu_g = sg * (1.0 + gr * (1.0 - sg))
            dqs, dks, d_lasts = [], [], []
            for h in range(HG_HEADS):
                cols = slice(h * HG_DK, (h + 1) * HG_DK)
                ov = o_ref[rows, cols]
                dyv = dy_ref[rows, cols].astype(F32)
                rstd = lax.rsqrt(jnp.mean(ov * ov, axis=-1, keepdims=True) + EPS)
                ohat = ov * rstd
                dp_ref[3, rows, cols] = (dyv * (ohat * wnv) * dsilu_g[:, cols]).astype(BF16)
                don = dyv * silu_g[:, cols]
                dwn_ref[...] += jnp.sum(don * ohat, axis=0, keepdims=True)
                gd = don * wnv
                do_b = (rstd * (gd - ohat * jnp.mean(gd * ohat, axis=-1, keepdims=True))).astype(BF16)
                st = st_ref[h, n]
                ds = ds_scr[h]
                ds_b = ds.astype(BF16)
                vh, kh = v[:, cols], k[:, cols]
                a_b = jnp.where(tril, _dot(qi[:, cols], ki[:, cols], NT), 0.0).astype(BF16)
                da_b = jnp.where(tril, _dot(do_b, vh, NT), 0.0).astype(BF16)
                dqs.append(_dot(do_b, st.astype(BF16), NN) * e_0[:, cols]
                           + (_dot(da_b, ki[:, cols], NN) + _dot(da_b, ki_lo[:, cols], NN)) * e_q[:, cols])
                dk_state = _dot(vh, ds_b, NN) * e_l[:, cols]
                dks.append((_dot(da_b, qi[:, cols], TN) + _dot(da_b, qi_lo[:, cols], TN)) * e_k[:, cols] + dk_state)
                dp_ref[2, rows, cols] = (_dot(a_b, do_b, TN) + _dot(kl[:, cols], ds_b, NT)).astype(BF16)
                ds_scr[h] = ds * e_last[:, cols] + _dot(do_b, q0[:, cols], TN)
                d_lasts.append(jnp.sum(dk_state * kh, axis=0, keepdims=True) + jnp.sum(ds * st, axis=0, keepdims=True) * e_last[:, cols])
            dq = jnp.concatenate(dqs, axis=1)
            dk = jnp.concatenate(dks, axis=1)
            dlogf = _tri_dot(tril_b, q * dq - k * dk, TN) + jnp.concatenate(d_lasts, axis=1)
            dfg = dlogf / fg - dk
            dlb_scr[...] += jnp.sum(dfg * (1.0 - sf), axis=0, keepdims=True)
            sq = _sigmoid(qr)
            dp_ref[0, rows, :] = (dq * (HG_DK ** -0.5) * (sq * (1.0 + qr * (1.0 - sq)))).astype(BF16)
            dp_ref[1, rows, :] = (dfg * (1.0 - lb_all) * sf * (1.0 - sf)).astype(BF16)
            return carry

        lax.fori_loop(0, cpb, chunk, 0, unroll=2)

        @pl.when(step == nb - 1)
        def _():
            d0 = dlb_scr[...] * lb_all * (1.0 - lb_all)
            dl_ref[0:1, :] = d0
            dl_ref[1:2, :] = -d0

    rev = lambda i: nb - 1 - i
    blk = pl.BlockSpec((rb, D_MODEL), lambda i: (rev(i), 0))
    return pl.pallas_call(
        body,
        name="hgrn_bwd",
        grid=(nb,),
        in_specs=_hg_proj_specs(rb, rev)
        + [pl.BlockSpec((2, D_MODEL), lambda i: (0, 0)), pl.BlockSpec((1, HG_DK), lambda i: (0, 0)), blk,
           pl.BlockSpec((HG_HEADS, cpb, HG_DK, HG_DK), lambda i: (0, rev(i), 0, 0)), blk],
        out_specs=[
            pl.BlockSpec((4, rb, D_MODEL), lambda i: (0, rev(i), 0)),
            pl.BlockSpec((2, D_MODEL), lambda i: (0, 0)),
            pl.BlockSpec((1, HG_DK), lambda i: (0, 0)),
        ],
        out_shape=[
            jax.ShapeDtypeStruct((4, t, D_MODEL), BF16),
            jax.ShapeDtypeStruct((2, D_MODEL), F32),
            jax.ShapeDtypeStruct((1, HG_DK), F32),
        ],
        scratch_shapes=[pltpu.VMEM((HG_HEADS, HG_DK, HG_DK), F32), pltpu.VMEM((1, D_MODEL), F32)],
        compiler_params=_cparams(("arbitrary",)),
    )(proj, proj, proj, proj, lb, wn, o, states, dy)


ATT_STACK = 8


def _att_stack(q_ref, sink_ref, first, lo, bias_p, bias_c, extra_ref=None):
    qs, bps, bcs, sinks, extras = [], [], [], None, []
    rows = lax.broadcasted_iota(jnp.int32, (ATT_STACK * WINDOW, 1), 0)
    for i in range(ATT_STACK):
        hq = first + i
        cols = slice((hq // 2) * LANES, (hq // 2 + 1) * LANES)
        sel = lo if hq % 2 == 0 else jnp.logical_not(lo)
        qp = q_ref[:, cols] * (ATT_HD ** -0.5)
        qs.append(jnp.where(sel, qp, jnp.zeros_like(qp)))
        bps.append(ALIBI_SLOPES[hq] * bias_p)
        bcs.append(ALIBI_SLOPES[hq] * bias_c)
        sinks = sink_ref[hq] if sinks is None else jnp.where(rows < i * WINDOW, sinks, sink_ref[hq])
        if extra_ref is not None:
            ep = extra_ref[:, cols]
            extras.append(jnp.where(sel, ep, jnp.zeros_like(ep)))
    cat = lambda parts: jnp.concatenate(parts, axis=0)
    return cat(qs), cat(bps), cat(bcs), sinks, (cat(extras) if extras else None)


def _att_rows(i):
    return slice(i * WINDOW, (i + 1) * WINDOW)


def _att_bias(n):
    tq = lax.broadcasted_iota(jnp.int32, (WINDOW, WINDOW), 0)
    sk = lax.broadcasted_iota(jnp.int32, (WINDOW, WINDOW), 1)
    valid_c = sk <= tq
    valid_p = (sk - tq) > jnp.where(n > 0, 0, WINDOW)
    dist_c = (tq - sk).astype(F32)
    return jnp.where(valid_p, -dist_c - float(WINDOW), NEG), jnp.where(valid_c, -dist_c, NEG)


def _att_halves(x, lo, kh):
    r = pltpu.roll(x, ATT_HD, 1)
    zero = jnp.zeros_like(x)
    if kh == 0:
        return jnp.where(lo, x, r), jnp.where(lo, x, zero), jnp.where(lo, zero, r)
    return jnp.where(lo, r, x), jnp.where(lo, r, zero), jnp.where(lo, zero, x)


def _att_probs(qm, k2p, k2c, bias_p, bias_c, sink):
    sp = _dot(qm, k2p, NT) + bias_p
    sc = _dot(qm, k2c, NT) + bias_c
    m = jnp.maximum(jnp.maximum(jnp.max(sp, axis=-1, keepdims=True), jnp.max(sc, axis=-1, keepdims=True)), sink)
    ep = jnp.exp(sp - m)
    ec = jnp.exp(sc - m)
    es = jnp.exp(sink - m)
    inv = 1.0 / (jnp.sum(ep, axis=-1, keepdims=True) + jnp.sum(ec, axis=-1, keepdims=True) + es)
    return ep * inv, ec * inv, es * inv


def _attn_fwd(q, kv, sinks):
    t = q.shape[0]
    nb = t // WINDOW

    def body(sink_ref, q_ref, kvp_ref, kvc_ref, o_ref):
        n = pl.program_id(0)
        bias_p, bias_c = _att_bias(n)
        lo = lax.broadcasted_iota(jnp.int32, (WINDOW, LANES), 1) < ATT_HD
        for kh in range(ATT_KVH):
            k2p, _, _ = _att_halves(kvp_ref[:, 0:LANES], lo, kh)
            k2c, _, _ = _att_halves(kvc_ref[:, 0:LANES], lo, kh)
            _, vlo_p, vhi_p = _att_halves(kvp_ref[:, LANES:2 * LANES], lo, kh)
            _, vlo_c, vhi_c = _att_halves(kvc_ref[:, LANES:2 * LANES], lo, kh)
            for first in range(kh * ATT_GROUP, (kh + 1) * ATT_GROUP, ATT_STACK):
                qs, bp, bc, sinks, _ = _att_stack(q_ref, sink_ref, first, lo, bias_p, bias_c)
                pp, pc, _ = _att_probs(qs, k2p, k2c, bp, bc, sinks)
                pp, pc = pp.astype(BF16), pc.astype(BF16)
                for i in range(0, ATT_STACK, 2):
                    even, odd = _att_rows(i), _att_rows(i + 1)
                    out = (_dot(pp[even], vlo_p, NN) + _dot(pc[even], vlo_c, NN)
                           + _dot(pp[odd], vhi_p, NN) + _dot(pc[odd], vhi_c, NN))
                    j = (first + i) // 2
                    o_ref[:, j * LANES:(j + 1) * LANES] = out.astype(BF16)

    return pl.pallas_call(
        body,
        name="attn_fwd",
        grid=(nb,),
        in_specs=[
            pl.BlockSpec(memory_space=pltpu.SMEM),
            pl.BlockSpec((WINDOW, D_MODEL), lambda n: (n, 0)),
            pl.BlockSpec((WINDOW, 2 * LANES), lambda n: (jnp.maximum(n - 1, 0), 0)),
            pl.BlockSpec((WINDOW, 2 * LANES), lambda n: (n, 0)),
        ],
        out_specs=pl.BlockSpec((WINDOW, D_MODEL), lambda n: (n, 0)),
        out_shape=jax.ShapeDtypeStruct((t, D_MODEL), BF16),
        compiler_params=_cparams(("parallel",)),
    )(sinks, q, kv, kv)


def _attn_bwd(q, kv, sinks, dout):
    t = q.shape[0]
    nb = t // WINDOW

    def body(sink_ref, q_ref, kvp_ref, kvc_ref, do_ref, dq_ref, dkv_ref, dsink_ref, carry_ref):
        n = pl.program_id(0)

        @pl.when(n == 0)
        def _():
            carry_ref[...] = jnp.zeros_like(carry_ref)
            dsink_ref[...] = jnp.zeros_like(dsink_ref)

        @pl.when(n == nb)
        def _():
            dkv_ref[...] = carry_ref[...].astype(BF16)

        @pl.when(n < nb)
        def _():
            bias_p, bias_c = _att_bias(n)
            lo = lax.broadcasted_iota(jnp.int32, (WINDOW, LANES), 1) < ATT_HD
            lane1 = lax.broadcasted_iota(jnp.int32, (1, LANES), 1)
            dsink = jnp.zeros((1, LANES), F32)
            halves = []
            for kh in range(ATT_KVH):
                k2p, klo_p, khi_p = _att_halves(kvp_ref[:, 0:LANES], lo, kh)
                k2c, klo_c, khi_c = _att_halves(kvc_ref[:, 0:LANES], lo, kh)
                v2p, _, _ = _att_halves(kvp_ref[:, LANES:2 * LANES], lo, kh)
                v2c, _, _ = _att_halves(kvc_ref[:, LANES:2 * LANES], lo, kh)
                acc = [jnp.zeros((WINDOW, LANES), F32) for _ in range(4)]
                for first in range(kh * ATT_GROUP, (kh + 1) * ATT_GROUP, ATT_STACK):
                    qs, bp, bc, sinks, dos = _att_stack(q_ref, sink_ref, first, lo, bias_p, bias_c, do_ref)
                    pp, pc, ps = _att_probs(qs, k2p, k2c, bp, bc, sinks)
                    dpp = _dot(dos, v2p, NT)
                    dpc = _dot(dos, v2c, NT)
                    delta = jnp.sum(pp * dpp, axis=-1, keepdims=True) + jnp.sum(pc * dpc, axis=-1, keepdims=True)
                    dsp = (pp * (dpp - delta)).astype(BF16)
                    dsc = (pc * (dpc - delta)).astype(BF16)
                    sink_term = ps * delta
                    for i in range(ATT_STACK):
                        dsink = dsink + jnp.where(lane1 == first + i, -jnp.sum(sink_term[_att_rows(i)], axis=0, keepdims=True), 0.0)
                    for i in range(0, ATT_STACK, 2):
                        even, odd = _att_rows(i), _att_rows(i + 1)
                        dq_pair = (_dot(dsp[even], klo_p, NN) + _dot(dsc[even], klo_c, NN)
                                   + _dot(dsp[odd], khi_p, NN) + _dot(dsc[odd], khi_c, NN))
                        j = (first + i) // 2
                        dq_ref[:, j * LANES:(j + 1) * LANES] = (dq_pair * (ATT_HD ** -0.5)).astype(BF16)
                    acc[0] = acc[0] + _dot(dsp, qs, TN)
                    acc[1] = acc[1] + _dot(dsc, qs, TN)
                    acc[2] = acc[2] + _dot(pp.astype(BF16), dos, TN)
                    acc[3] = acc[3] + _dot(pc.astype(BF16), dos, TN)
                halves.append([a + pltpu.roll(a, ATT_HD, 1) for a in acc])
            prev = jnp.concatenate(
                [jnp.where(lo, halves[0][0], halves[1][0]), jnp.where(lo, halves[0][2], halves[1][2])], axis=1)
            cur = jnp.concatenate(
                [jnp.where(lo, halves[0][1], halves[1][1]), jnp.where(lo, halves[0][3], halves[1][3])], axis=1)
            dkv_ref[...] = (carry_ref[...] + prev).astype(BF16)
            carry_ref[...] = cur
            dsink_ref[...] += dsink

    blk = lambda n: jnp.minimum(n, nb - 1)
    return pl.pallas_call(
        body,
        name="attn_bwd",
        grid=(nb + 1,),
        in_specs=[
            pl.BlockSpec(memory_space=pltpu.SMEM),
            pl.BlockSpec((WINDOW, D_MODEL), lambda n: (blk(n), 0)),
            pl.BlockSpec((WINDOW, 2 * LANES), lambda n: (jnp.maximum(blk(n) - 1, 0), 0)),
            pl.BlockSpec((WINDOW, 2 * LANES), lambda n: (blk(n), 0)),
            pl.BlockSpec((WINDOW, D_MODEL), lambda n: (blk(n), 0)),
        ],
        out_specs=[
            pl.BlockSpec((WINDOW, D_MODEL), lambda n: (blk(n), 0)),
            pl.BlockSpec((WINDOW, 2 * LANES), lambda n: (jnp.maximum(n - 1, 0), 0)),
            pl.BlockSpec((1, LANES), lambda n: (0, 0)),
        ],
        out_shape=[
            jax.ShapeDtypeStruct((t, D_MODEL), BF16),
            jax.ShapeDtypeStruct((t, 2 * LANES), BF16),
            jax.ShapeDtypeStruct((1, LANES), F32),
        ],
        scratch_shapes=[pltpu.VMEM((WINDOW, 2 * LANES), F32)],
        compiler_params=_cparams(("arbitrary",)),
    )(sinks, q, kv, kv, dout)


def _ffn_fwd(h, norm_g, w_up, conv_w, conv_b, w_down, tag, after_up=lambda up: None):
    up = _mm_nn(h, w_up, gain=norm_g, name=f"ffn{tag}_up")
    after_up(up)
    act, c = _conv_fwd(up, conv_w, conv_b, name=f"ffn{tag}_conv")
    h_out = _mm_nn(act, w_down, res=h, name=f"ffn{tag}_down")
    return h_out, (up, act, c)


def _ffn_bwd(dh, h, norm_g, w_up, conv_w, conv_b, w_down, saved, tag, deps=()):
    up, act, c = saved
    dw_down = _mm_tn(act, dh, 1, D_MODEL, name=f"ffn{tag}_dwdown", deps=deps)
    dact = _mm_nt(dh, w_down, name=f"ffn{tag}_dact", deps=deps)
    dup, dconv_w, dconv_b = _conv_bwd(up, conv_w, c, dact, name=f"ffn{tag}_dconv")
    dw_up = _mm_tn(h, dup, N_CHIPS, CONV_COLS, stacked=True, gain=norm_g, name=f"ffn{tag}_dwup")
    dh_in, dnorm = _mm_nt(dup, w_up, stacked=True, norm_of=(h, norm_g, dh), name=f"ffn{tag}_dxn")
    return dh_in, dict(ffn_w_down=dw_down, ffn_w_up=dw_up, ffn_conv_w=dconv_w, ffn_conv_b=dconv_b, ffn_norm=dnorm)


def _local_step(x, target, w, fetch=lambda w, stage, after: w, hook=lambda point, dh, grads: ()):
    proj = _mm_nn(x, w["hg_w_in"], gain=w["hg_norm"], name="hg_in")
    o, y, states = _hgrn_fwd(proj, w["hg_lb"], w["hg_out_norm"])
    w = fetch(w, "mixer_out", y)
    fetch(w, "layer0_relay", y)
    h_a = _mm_nn(y, w["hg_w_out"], res=x, name="hg_out")
    w = fetch(w, "layer0", h_a)
    h1, ffn0 = _ffn_fwd(h_a, w["ffn_norm"][0], w["ffn_w_up"][0], w["ffn_conv_w"][0], w["ffn_conv_b"][0], w["ffn_w_down"][0], 0,
                        lambda up: fetch(w, "layer1_relay", up))
    w = fetch(w, "layer1", h1)
    kv = _mm_nn(h1, w["w_kv"], gain=w["kv_norm"], out_dtype=BF16, name="kv_proj")
    qa = _mm_nn(h1, w["attn_w_q"], gain=w["attn_norm"], out_dtype=BF16, name="attn_q")
    ao = _attn_fwd(qa, kv, w["attn_sinks"])
    h_b = _mm_nn(ao, w["attn_w_o"], res=h1, name="attn_o")
    h2, ffn1 = _ffn_fwd(h_b, w["ffn_norm"][1], w["ffn_w_up"][1], w["ffn_conv_w"][1], w["ffn_conv_b"][1], w["ffn_w_down"][1], 1)
    dh2, d_final, loss = _loss_head(h2, w["final_norm"], target)

    dh_b, g1 = _ffn_bwd(dh2, h_b, w["ffn_norm"][1], w["ffn_w_up"][1], w["ffn_conv_w"][1], w["ffn_conv_b"][1], w["ffn_w_down"][1], ffn1, 1)
    deps = hook("ffn1", dh_b, g1)
    dw_o = _mm_tn(ao, dh_b, 1, D_MODEL, name="attn_dwo", deps=deps)
    dao = _mm_nt(dh_b, w["attn_w_o"], out_dtype=BF16, name="attn_dao", deps=deps)
    dqa, dkv, dsinks = _attn_bwd(qa, kv, w["attn_sinks"], dao)
    dw_q = _mm_tn(h1, dqa, 1, D_MODEL, gain=w["attn_norm"], name="attn_dwq")
    dw_kv = _mm_tn(h1, dkv, 1, 2 * LANES, gain=w["kv_norm"], name="kv_dw")
    dh1, d_attn_norm, d_kv_norm = _mm_nt(dqa, w["attn_w_q"], norm_of=(h1, w["attn_norm"], dh_b),
                                        also=(dkv, w["w_kv"], w["kv_norm"]), name="attn_dxa")
    deps = hook("attn", dh1, dict(attn_w_o=dw_o, attn_w_q=dw_q, w_kv=dw_kv))
    dh_a, g0 = _ffn_bwd(dh1, h_a, w["ffn_norm"][0], w["ffn_w_up"][0], w["ffn_conv_w"][0], w["ffn_conv_b"][0], w["ffn_w_down"][0], ffn0, 0, deps)
    dw_out = _mm_tn(y, dh_a, 1, D_MODEL, name="hg_dwout")
    deps = hook("ffn0", dh_a, dict(g0, hg_w_out=dw_out))
    dy = _mm_nt(dh_a, w["hg_w_out"], out_dtype=BF16, name="hg_dy", deps=deps)
    dproj, dlb, d_out_norm = _hgrn_bwd(proj, w["hg_lb"], w["hg_out_norm"], o, states, dy)
    deps = hook("hgrn", dproj, None)
    dw_in = _mm_tn(x, dproj, N_CHIPS, D_MODEL, stacked=True, gain=w["hg_norm"], name="hg_dwin", deps=deps)
    deps = hook("hg_w", dproj, dict(hg_w_in=dw_in))
    dx, d_hg_norm = _mm_nt(dproj, w["hg_w_in"], stacked=True, norm_of=(x, w["hg_norm"], dh_a), name="hg_dxn", deps=deps)

    grads = dict(
        hg_norm=d_hg_norm, hg_w_in=dw_in, hg_lb=dlb, hg_out_norm=d_out_norm, hg_w_out=dw_out,
        kv_norm=d_kv_norm, w_kv=dw_kv, attn_norm=d_attn_norm, attn_w_q=dw_q, attn_sinks=dsinks, attn_w_o=dw_o,
        final_norm=d_final,
    )
    for name in g0:
        grads[name] = [g0[name], g1[name]]
    return loss, dx, grads


ANY = pl.BlockSpec(memory_space=pl.ANY)


def _place():
    x, y, c = lax.axis_index("x"), lax.axis_index("y"), lax.axis_index("c")
    chips = [(1 - x, y), (x, 1 - y), (1 - x, 1 - y)]
    return x, y, c, chips


def _rcopy(src, dst, send_sem, recv_sem, to):
    return pltpu.make_async_remote_copy(src_ref=src, dst_ref=dst, send_sem=send_sem, recv_sem=recv_sem, device_id=to, device_id_type=MESH)


HBM = pl.BlockSpec(memory_space=pltpu.HBM)
SEM = pl.BlockSpec(memory_space=pltpu.SEMAPHORE)
EFFECT = pltpu.SideEffectType.DATAFLOW_SIDE_EFFECTING


def _in_hbm(a):
    return pltpu.with_memory_space_constraint(a, pltpu.HBM)


def _place_shard(shard, place, dtype, name, deps=(), layer=None):
    r, cols = shard.shape[-2:]
    tr = _pick(r, ELEM_ROWS)
    src = pl.BlockSpec((tr, cols), lambda i, place_ref: (i, 0)) if layer is None else pl.BlockSpec((None, tr, cols), lambda i, place_ref: (layer, i, 0))

    def body(place_ref, s_ref, *rest):
        o_ref = rest[-1]
        o_ref[...] = s_ref[...].astype(o_ref.dtype)

    return pl.pallas_call(
        body,
        name=name,
        grid_spec=pltpu.PrefetchScalarGridSpec(
            num_scalar_prefetch=1,
            grid=(r // tr,),
            in_specs=[src] + _dep_specs(deps),
            out_specs=pl.BlockSpec((None, tr, cols), lambda i, place_ref: (place_ref[0], i, 0)),
        ),
        out_shape=jax.ShapeDtypeStruct((N_CHIPS, r, cols), dtype),
        compiler_params=_cparams(("parallel",)),
    )(place, shard, *deps)


def _start_copies(name, bufs, n_sem, copies):
    n = len(bufs)

    def body(*refs):
        for cp in copies(refs[:n], refs[n], refs[n + 1]):
            cp.start()
        refs[-1][...] = jnp.zeros_like(refs[-1])

    outs = pl.pallas_call(
        body,
        name=name,
        in_specs=[HBM] * n,
        out_specs=[SEM, SEM] + [HBM] * n + [pl.BlockSpec(memory_space=pltpu.VMEM)],
        out_shape=[pltpu.SemaphoreType.DMA((n_sem,)), pltpu.SemaphoreType.DMA((n_sem,))] + [pltpu.HBM(b.shape, b.dtype) for b in bufs]
        + [jax.ShapeDtypeStruct((SUBLANES, LANES), F32)],
        input_output_aliases={i: 2 + i for i in range(n)},
        compiler_params=pltpu.CompilerParams(has_side_effects=EFFECT),
    )(*[_in_hbm(b) for b in bufs])
    return outs[0], outs[1], list(outs[2:-1]), outs[-1]


def _wait_copies(name, bufs, send_sems, recv_sems, after, copies):
    n = len(bufs)

    def body(*refs):
        for cp in copies(refs[:n], refs[n], refs[n + 1]):
            cp.wait_send()
            cp.wait_recv()

    return pl.pallas_call(
        body,
        name=name,
        in_specs=[HBM] * n + [SEM, SEM, ANY],
        out_specs=[HBM] * n,
        out_shape=[pltpu.HBM(b.shape, b.dtype) for b in bufs],
        input_output_aliases={i: i for i in range(n)},
        compiler_params=pltpu.CompilerParams(has_side_effects=EFFECT),
    )(*bufs, send_sems, recv_sems, after)


def _relay_copies(name, bufs, send_sems, recv_sems, after, landed, n_sem, onward):
    n = len(bufs)

    def body(*refs):
        for cp in landed(refs[:n], refs[n], refs[n + 1]):
            cp.wait_send()
            cp.wait_recv()
        for cp in onward(refs[:n], refs[n + 3], refs[n + 4]):
            cp.start()
        refs[-1][...] = jnp.zeros_like(refs[-1])

    outs = pl.pallas_call(
        body,
        name=name,
        in_specs=[HBM] * n + [SEM, SEM, ANY],
        out_specs=[SEM, SEM] + [HBM] * n + [pl.BlockSpec(memory_space=pltpu.VMEM)],
        out_shape=[pltpu.SemaphoreType.DMA((n_sem,)), pltpu.SemaphoreType.DMA((n_sem,))] + [pltpu.HBM(b.shape, b.dtype) for b in bufs]
        + [jax.ShapeDtypeStruct((SUBLANES, LANES), F32)],
        input_output_aliases={i: 2 + i for i in range(n)},
        compiler_params=pltpu.CompilerParams(has_side_effects=EFFECT),
    )(*bufs, send_sems, recv_sems, after)
    return outs[0], outs[1], list(outs[2:-1]), outs[-1]


def _gather_half_copies(first, count, over_ici):
    def copies(refs, send_sems, recv_sems):
        x, y, c, chips = _place()
        out = []
        for i in range(count):
            h = refs[i].shape[1] // 2
            mine = pl.ds(c * h, h)
            for j, (px, py) in enumerate(chips):
                k = 3 * (first + i) + j
                slot = 2 * x + y if over_ici else 2 * px + py
                to = (px, py, c) if over_ici else (x, y, 1 - c)
                out.append(_rcopy(refs[i].at[slot, mine], refs[i].at[slot, mine], send_sems.at[k], recv_sems.at[k], to))
        return out

    return copies


def _gather_copies(first, count):
    def copies(refs, send_sems, recv_sems):
        x, y, c, chips = _place()
        me = 2 * x + y
        out = []
        for i in range(count):
            for j, (px, py) in enumerate(chips):
                k = 3 * (first + i) + j
                out.append(_rcopy(refs[i].at[me], refs[i].at[me], send_sems.at[k], recv_sems.at[k], (px, py, c)))
        return out

    return copies


def _swap_copies(n):
    def copies(refs, send_sems, recv_sems):
        x, y, c, _ = _place()
        out = []
        for i in range(n):
            h = refs[i].shape[1] // 2
            out.append(_rcopy(refs[i].at[:, pl.ds((1 - c) * h, h)], refs[n + i], send_sems.at[i], recv_sems.at[i], (x, y, 1 - c)))
        return out

    return copies


def _partial_copies(n):
    def copies(refs, send_sems, recv_sems):
        x, y, c, chips = _place()
        out = []
        for i in range(n):
            for j, (px, py) in enumerate(chips):
                out.append(_rcopy(refs[i].at[2 * px + py], refs[n + i].at[j], send_sems.at[3 * i + j], recv_sems.at[3 * i + j], (px, py, c)))
        return out

    return copies


def _share_copies(n):
    def copies(refs, send_sems, recv_sems):
        x, y, c, _ = _place()
        return [_rcopy(refs[i].at[c], refs[i].at[c], send_sems.at[i], recv_sems.at[i], (x, y, 1 - c)) for i in range(n)]

    return copies


def _small_layout(groups):
    flat = [a for g in groups for a in g]
    rows = -(-sum(a.shape[0] for a in flat) // SUBLANES) * SUBLANES
    return flat, rows, max(a.shape[1] for a in flat)


def _pack_small(groups, device):
    flat, rows, cols = _small_layout(groups)

    def body(dev_ref, *refs):
        o_ref = refs[-1]
        o_ref[...] = jnp.zeros_like(o_ref)
        r0 = 0
        for a_ref in refs[:-1]:
            r, w = a_ref.shape
            o_ref[r0:r0 + r, 0:w] = a_ref[...]
            r0 += r

    return pl.pallas_call(
        body,
        name="small_pack",
        grid_spec=pltpu.PrefetchScalarGridSpec(
            num_scalar_prefetch=1,
            grid=(1,),
            in_specs=[pl.BlockSpec(a.shape, lambda i, dev_ref: (0, 0)) for a in flat],
            out_specs=pl.BlockSpec((None, rows, cols), lambda i, dev_ref: (dev_ref[0], 0, 0)),
        ),
        out_shape=jax.ShapeDtypeStruct((N_DEV, rows, cols), F32),
        compiler_params=_cparams(("arbitrary",)),
    )(device, *flat)


def _small_copies(refs, send_sems, recv_sems):
    x, y, c, _ = _place()
    me = 4 * x + 2 * y + c
    out = []
    for k in range(1, N_DEV):
        peer = (x ^ (k >> 2), y ^ ((k >> 1) & 1), c ^ (k & 1))
        out.append(_rcopy(refs[0].at[me], refs[0].at[me], send_sems.at[k - 1], recv_sems.at[k - 1], peer))
    return out


def _sum_small(slots, groups, widths):
    out_shapes = [(sum(a.shape[0] for a in g), wd or g[0].shape[1]) for g, wd in zip(groups, widths)]

    def body(s_ref, *refs):
        outs, acc_ref = refs[:-1], refs[-1]
        acc = s_ref[0]
        for d in range(1, N_DEV):
            acc = acc + s_ref[d]
        acc_ref[...] = acc
        r0 = 0
        for o_ref in outs:
            r, w = o_ref.shape
            o_ref[...] = acc_ref[r0:r0 + r, 0:w]
            r0 += r

    vmem = pl.BlockSpec(memory_space=pltpu.VMEM)
    return pl.pallas_call(
        body,
        name="small_sum",
        in_specs=[vmem],
        out_specs=[vmem] * len(groups),
        out_shape=[jax.ShapeDtypeStruct(s, F32) for s in out_shapes],
        scratch_shapes=[pltpu.VMEM(slots.shape[1:], F32)],
        compiler_params=pltpu.CompilerParams(vmem_limit_bytes=VMEM_LIMIT_BYTES),
    )(slots)


def _adamw_small(items):
    n = len(items)

    def body(*refs):
        for i in range(n):
            w_ref, m_ref, v_ref, g_ref = refs[4 * i:4 * i + 4]
            d_ref, nm_ref, nv_ref = refs[4 * n + 3 * i:4 * n + 3 * i + 3]
            d_ref[...], nm_ref[...], nv_ref[...] = _adamw_math(w_ref[...], m_ref[...], v_ref[...], g_ref[...])

    vmem = pl.BlockSpec(memory_space=pltpu.VMEM)
    outs = pl.pallas_call(
        body,
        name="adamw_small",
        in_specs=[vmem] * (4 * n),
        out_specs=[vmem] * (3 * n),
        out_shape=[jax.ShapeDtypeStruct(it[0].shape, F32) for it in items for _ in range(3)],
        compiler_params=pltpu.CompilerParams(vmem_limit_bytes=VMEM_LIMIT_BYTES),
    )(*[a for it in items for a in it])
    return [tuple(outs[3 * i:3 * i + 3]) for i in range(n)]


class _Reduction:
    def __init__(self, tag, grads, place):
        self.tag, self.n, self.place = tag, len(grads), place
        lands = [lax.empty((N_CHIPS, g.shape[1] // 2, g.shape[2]), F32) for g in grads]
        self._start("swap", list(grads) + lands, self.n, _swap_copies(self.n))

    def _start(self, stage, bufs, n_sem, copies):
        *self.flight, self.token = _start_copies(f"rs_{stage}_start_{self.tag}", bufs, n_sem, copies)

    def _landed(self, stage, after, copies):
        send_sems, recv_sems, bufs = self.flight
        return _wait_copies(f"rs_{stage}_wait_{self.tag}", bufs, send_sems, recv_sems, after, copies)

    def to_chips(self, after):
        n = self.n
        bufs = self._landed("swap", after, _swap_copies(n))
        sums = [_add_core_halves(g, o, self.place, name=f"rs_add_core_{self.tag}_{i}") for i, (g, o) in enumerate(zip(bufs[:n], bufs[n:]))]
        self.mine = [f for f, _ in sums]
        parts = [b for _, b in sums]
        lands = [lax.empty((3,) + p.shape[1:], BF16) for p in parts]
        self._start("send", parts + lands, 3 * n, _partial_copies(n))

    def to_core(self, after):
        n = self.n
        bufs = self._landed("send", after, _partial_copies(n))
        halves = [_add_chip_partials(f, o, self.place, name=f"rs_add_chip_{self.tag}_{i}") for i, (f, o) in enumerate(zip(self.mine, bufs[n:]))]
        self._start("share", halves, n, _share_copies(n))

    def finish(self, after):
        return [b.reshape((-1,) + b.shape[2:]) for b in self._landed("share", after, _share_copies(self.n))]


ELEM_ROWS = (512, 352, 256, 176, 128, 64, 32, 16, 8)


def _add_core_halves(grad, got, place, name):
    s, r, cols = grad.shape
    h = r // 2
    tr = _pick(h, ELEM_ROWS)

    def body(place_ref, g_ref, o_ref, f_ref, b_ref):
        acc = g_ref[...] + o_ref[...]
        b_ref[...] = acc.astype(BF16)

        @pl.when(pl.program_id(1) == place_ref[0])
        def _():
            f_ref[...] = acc

    blk = pl.BlockSpec((None, tr, cols), lambda i, k, place_ref: (k, i, 0))
    return pl.pallas_call(
        body,
        name=name,
        grid_spec=pltpu.PrefetchScalarGridSpec(
            num_scalar_prefetch=1,
            grid=(h // tr, s),
            in_specs=[pl.BlockSpec((None, None, tr, cols), lambda i, k, place_ref: (k, place_ref[1], i, 0)), blk],
            out_specs=[pl.BlockSpec((tr, cols), lambda i, k, place_ref: (i, 0)), blk],
        ),
        out_shape=[jax.ShapeDtypeStruct((h, cols), F32), jax.ShapeDtypeStruct((s, h, cols), BF16)],
        compiler_params=_cparams(("parallel", "arbitrary")),
    )(place, grad.reshape(s, 2, h, cols), got)


def _add_chip_partials(mine, got, place, name):
    h, cols = mine.shape
    tr = _pick(h, ELEM_ROWS)

    def body(place_ref, m_ref, g_ref, o_ref):
        acc = m_ref[...]
        for j in range(3):
            acc = acc + g_ref[j].astype(F32)
        o_ref[...] = acc

    return pl.pallas_call(
        body,
        name=name,
        grid_spec=pltpu.PrefetchScalarGridSpec(
            num_scalar_prefetch=1,
            grid=(h // tr,),
            in_specs=[
                pl.BlockSpec((tr, cols), lambda i, place_ref: (i, 0)),
                pl.BlockSpec((3, tr, cols), lambda i, place_ref: (0, i, 0)),
            ],
            out_specs=pl.BlockSpec((None, tr, cols), lambda i, place_ref: (place_ref[1], i, 0)),
        ),
        out_shape=jax.ShapeDtypeStruct((2, h, cols), F32),
        compiler_params=_cparams(("parallel",)),
    )(place, mine, got)


def _adamw_math(w, m, v, g):
    nm = ADAM_B1 * m + (1.0 - ADAM_B1) * g
    nv = ADAM_B2 * v + (1.0 - ADAM_B2) * (g * g)
    m_hat = nm * (1.0 / (1.0 - ADAM_B1 ** ADAM_STEP))
    v_hat = nv * (1.0 / (1.0 - ADAM_B2 ** ADAM_STEP))
    return -ADAM_LR * (m_hat / (jnp.sqrt(v_hat) + ADAM_EPS) + ADAM_WD * w), nm, nv


def _adamw_layer(w, m, v, g, layer, prev, name):
    nl, r, cols = w.shape
    tr = _pick(r, ELEM_ROWS)

    def body(w_ref, m_ref, v_ref, g_ref, *rest):
        go_ref, d_ref, nm_ref, nv_ref = rest[-4:]
        gv = g_ref[...]
        d_ref[...], nm_ref[...], nv_ref[...] = _adamw_math(w_ref[...], m_ref[...], v_ref[...], gv)
        go_ref[...] = gv

    lay = pl.BlockSpec((None, tr, cols), lambda i: (layer, i, 0))
    return pl.pallas_call(
        body,
        name=name,
        grid=(r // tr,),
        in_specs=[lay] * 3 + [pl.BlockSpec((tr, cols), lambda i: (i, 0))] + ([ANY] * 4 if prev else []),
        out_specs=[lay] * 4,
        out_shape=[jax.ShapeDtypeStruct((nl, r, cols), F32)] * 4,
        input_output_aliases={4 + k: k for k in range(4)} if prev else {},
        compiler_params=_cparams(("parallel",)),
    )(w, m, v, g, *(prev or ()))


def _adamw(w, m, v, g, name):
    r, cols = w.shape
    tr = _pick(r, ELEM_ROWS)

    def body(w_ref, m_ref, v_ref, g_ref, d_ref, nm_ref, nv_ref):
        d_ref[...], nm_ref[...], nv_ref[...] = _adamw_math(w_ref[...], m_ref[...], v_ref[...], g_ref[...])

    blk = pl.BlockSpec((tr, cols), lambda i: (i, 0))
    return pl.pallas_call(
        body,
        name=name,
        grid=(r // tr,),
        in_specs=[blk] * 4,
        out_specs=[blk] * 3,
        out_shape=[jax.ShapeDtypeStruct((r, cols), F32)] * 3,
        compiler_params=_cparams(("parallel",)),
    )(w, m, v, g)


SMALL_COLS = 384
SMALL_ROWS = 16


def _pad_rows(flat, rows, cols):
    return jnp.pad(flat, (0, rows * cols - flat.shape[0])).reshape(rows, cols)


def kernel(x, hg_norm, hg_w_in, hg_lb_logits, hg_out_norm, hg_w_out, kv_norm, w_kv, attn_norm, attn_w_q, attn_sinks, attn_w_o, ffn_norm, ffn_w_up, ffn_conv_w, ffn_conv_b, ffn_w_down, final_norm, loss_target, m_hg_norm, m_hg_w_in, m_hg_lb_logits, m_hg_out_norm, m_hg_w_out, m_kv_norm, m_w_kv, m_attn_norm, m_attn_w_q, m_attn_sinks, m_attn_w_o, m_ffn_norm, m_ffn_w_up, m_ffn_conv_w, m_ffn_conv_b, m_ffn_w_down, m_final_norm, v_hg_norm, v_hg_w_in, v_hg_lb_logits, v_hg_out_norm, v_hg_w_out, v_kv_norm, v_w_kv, v_attn_norm, v_attn_w_q, v_attn_sinks, v_attn_w_o, v_ffn_norm, v_ffn_w_up, v_ffn_conv_w, v_ffn_conv_b, v_ffn_w_down, v_final_norm):
    wts = dict(hg_norm=hg_norm, hg_w_in=hg_w_in, hg_lb_logits=hg_lb_logits, hg_out_norm=hg_out_norm, hg_w_out=hg_w_out, kv_norm=kv_norm, w_kv=w_kv, attn_norm=attn_norm, attn_w_q=attn_w_q, attn_sinks=attn_sinks, attn_w_o=attn_w_o, ffn_norm=ffn_norm, ffn_w_up=ffn_w_up, ffn_conv_w=ffn_conv_w, ffn_conv_b=ffn_conv_b, ffn_w_down=ffn_w_down, final_norm=final_norm)
    mom1 = dict(hg_norm=m_hg_norm, hg_w_in=m_hg_w_in, hg_lb_logits=m_hg_lb_logits, hg_out_norm=m_hg_out_norm, hg_w_out=m_hg_w_out, kv_norm=m_kv_norm, w_kv=m_w_kv, attn_norm=m_attn_norm, attn_w_q=m_attn_w_q, attn_sinks=m_attn_sinks, attn_w_o=m_attn_w_o, ffn_norm=m_ffn_norm, ffn_w_up=m_ffn_w_up, ffn_conv_w=m_ffn_conv_w, ffn_conv_b=m_ffn_conv_b, ffn_w_down=m_ffn_w_down, final_norm=m_final_norm)
    mom2 = dict(hg_norm=v_hg_norm, hg_w_in=v_hg_w_in, hg_lb_logits=v_hg_lb_logits, hg_out_norm=v_hg_out_norm, hg_w_out=v_hg_w_out, kv_norm=v_kv_norm, w_kv=v_w_kv, attn_norm=v_attn_norm, attn_w_q=v_attn_w_q, attn_sinks=v_attn_sinks, attn_w_o=v_attn_w_o, ffn_norm=v_ffn_norm, ffn_w_up=v_ffn_w_up, ffn_conv_w=v_ffn_conv_w, ffn_conv_b=v_ffn_conv_b, ffn_w_down=v_ffn_w_down, final_norm=v_final_norm)
    names = list(wts)
    chip = 2 * lax.axis_index("x") + lax.axis_index("y")
    core = lax.axis_index("c")
    fs = D_FF // N_CHIPS
    ds = D_MODEL // N_CHIPS

    place_arr = jnp.stack([chip, core]).astype(jnp.int32)
    small = jnp.concatenate([hg_norm.reshape(-1), hg_lb_logits.reshape(-1), ffn_conv_w.reshape(-1)])
    n_small = small.shape[0]
    shards = [
        ("small", _pad_rows(small, SMALL_ROWS, SMALL_COLS), F32, None), ("hg_w_in", hg_w_in, BF16, 0),
        ("hg_w_out", hg_w_out, BF16, 0), ("ffn_w_up0", ffn_w_up, BF16, 0), ("ffn_w_down0", ffn_w_down, BF16, 0),
        ("w_kv", w_kv, BF16, None), ("attn_w_q", attn_w_q, BF16, 0), ("attn_w_o", attn_w_o, BF16, 0),
        ("ffn_w_up1", ffn_w_up, BF16, 1), ("ffn_w_down1", ffn_w_down, BF16, 1),
    ]
    n_first = 3
    spans = dict(layer0=(0, 2), layer1=(2, 7))

    def first_copies(refs, send_sems, recv_sems):
        return (_gather_copies(0, 1)(refs[:1], send_sems, recv_sems) + _gather_half_copies(1, 1, True)(refs[1:2], send_sems, recv_sems)
                + _gather_copies(2, 1)(refs[2:3], send_sems, recv_sems))

    placed = [_place_shard(s, place_arr, dt, name=f"place_{nm}", layer=ly) for nm, s, dt, ly in shards[:n_first]]
    first = _start_copies("gather_start_first", placed, 3 * n_first, first_copies)
    placed = [_place_shard(s, place_arr, dt, name=f"place_{nm}", deps=(first[3],), layer=ly) for nm, s, dt, ly in shards[n_first:]]
    rest = _start_copies("gather_start_rest", placed, 3 * len(placed), _gather_half_copies(0, len(placed), True))
    relayed = {}

    def fetch(w, stage, after):
        if stage == "first":
            w_in = _relay_copies("gather_first_relay", first[2][1:2], first[0], first[1], after,
                                 _gather_half_copies(1, 1, True), 3, _gather_half_copies(0, 1, False))
            got = _wait_copies("gather_wait_small", first[2][:1], first[0], first[1], w_in[3], _gather_copies(0, 1))
            got += _wait_copies("gather_wait_first", w_in[2], w_in[0], w_in[1], got[0], _gather_half_copies(0, 1, False))
        elif stage == "mixer_out":
            got = _wait_copies("gather_wait_mixer_out", first[2][2:], first[0], first[1], after, _gather_copies(2, 1))
        elif stage.endswith("_relay"):
            lo, hi = spans[stage[:-6]]
            relayed[stage[:-6]] = _relay_copies(
                f"gather_{stage}", rest[2][lo:hi], rest[0], rest[1], after,
                _gather_half_copies(lo, hi - lo, True), 3 * (hi - lo), _gather_half_copies(0, hi - lo, False))
            return w
        else:
            lo, hi = spans[stage]
            send_sems, recv_sems, bufs, _ = relayed[stage]
            got = _wait_copies(f"gather_wait_{stage}", bufs, send_sems, recv_sems, after, _gather_half_copies(0, hi - lo, False))
        w = dict(w)
        if stage == "first":
            g_small = got[0].reshape(N_CHIPS, -1)[:, :n_small]
            conv_w = g_small[:, 3 * ds:].reshape(N_CHIPS, 2, 3, fs).transpose(1, 2, 0, 3).reshape(2, 3, D_FF)
            w.update(
                hg_norm=g_small[:, :ds].reshape(1, D_MODEL),
                hg_lb=g_small[:, ds:3 * ds].reshape(N_CHIPS, 2, ds).transpose(1, 0, 2).reshape(2, D_MODEL),
                ffn_conv_w=[conv_w[0], conv_w[1]], hg_w_in=got[1],
            )
        elif stage == "mixer_out":
            w.update(hg_w_out=got[0].reshape(1, D_MODEL, D_MODEL))
        elif stage == "layer0":
            w.update(ffn_w_up=[got[0], None], ffn_w_down=[got[1].reshape(1, D_FF, D_MODEL), None])
        else:
            w.update(
                w_kv=got[0].reshape(1, D_MODEL, 2 * LANES), attn_w_q=got[1].reshape(1, D_MODEL, D_MODEL),
                attn_w_o=got[2].reshape(1, D_MODEL, D_MODEL), ffn_w_up=[w["ffn_w_up"][0], got[3]],
                ffn_w_down=[w["ffn_w_down"][0], got[4].reshape(1, D_FF, D_MODEL)],
            )
        return w

    whole = dict(
        hg_out_norm=hg_out_norm, kv_norm=kv_norm.reshape(1, D_MODEL), attn_norm=attn_norm, attn_sinks=attn_sinks.reshape(ATT_QH),
        ffn_norm=[ffn_norm[0:1], ffn_norm[1:2]], ffn_conv_b=[ffn_conv_b[0:1], ffn_conv_b[1:2]], final_norm=final_norm.reshape(1, D_MODEL),
    )
    whole = fetch(whole, "first", rest[3])

    red, layer1 = {}, {}

    def by_rows(g, rows):
        return g.reshape(N_CHIPS, rows, g.shape[2])

    def hook(point, dh, grads):
        if point == "ffn1":
            red["ffn1"] = _Reduction("ffn1", [by_rows(grads["ffn_w_down"], fs), grads["ffn_w_up"]], place_arr)
            return (red["ffn1"].token,)
        if point == "attn":
            red["ffn1"].to_chips(dh)
            layer1.update(grads)
            return (red["ffn1"].token,)
        if point == "ffn0":
            group = [by_rows(layer1["attn_w_o"], ds), by_rows(layer1["attn_w_q"], ds), by_rows(layer1["w_kv"], ds),
                     by_rows(grads["ffn_w_down"], fs), grads["ffn_w_up"], by_rows(grads["hg_w_out"], ds)]
            red["mid"] = _Reduction("mid", group, place_arr)
            return (red["mid"].token,)
        if point == "hgrn":
            red["ffn1"].to_core(dh)
            red["mid"].to_chips(dh)
            return (red["ffn1"].token, red["mid"].token)
        red["hg"] = _Reduction("hg", [grads["hg_w_in"]], place_arr)
        return (red["hg"].token,)

    loss, dx, grads = _local_step(x[0], loss_target[0], whole, fetch, hook)

    small_names = ["hg_out_norm", "attn_sinks", "kv_norm", "attn_norm", "ffn_norm", "ffn_conv_b", "final_norm", "hg_norm", "hg_lb_logits", "ffn_conv_w"]
    groups = [[loss]] + [grads[n] if isinstance(grads[n], list) else [grads[n]] for n in small_names[:-2]] + [[grads["hg_lb"]], grads["ffn_conv_w"]]
    widths = [None, None, ATT_QH] + [None] * 8
    packed = _pack_small(groups, jnp.reshape(2 * chip + core, (1,)).astype(jnp.int32))
    small_flight = _start_copies("small_start", [packed], N_DEV - 1, _small_copies)
    red["hg"].to_chips(small_flight[3])

    out_g, out_d, out_m, out_v = {}, {}, {}, {}

    def update(name, g2):
        shape = wts[name].shape
        d2, m2, v2 = _adamw(wts[name].reshape(g2.shape), mom1[name].reshape(g2.shape), mom2[name].reshape(g2.shape), g2, name=f"adamw_{name}")
        out_g[name], out_d[name], out_m[name], out_v[name] = g2.reshape(shape), d2.reshape(shape), m2.reshape(shape), v2.reshape(shape)
        return d2

    def update_layer(name, g2, layer, prev):
        res = _adamw_layer(wts[name], mom1[name], mom2[name], g2, layer, prev, name=f"adamw_{name}{layer}")
        out_g[name], out_d[name], out_m[name], out_v[name] = res
        return res

    g_down1, g_up1 = red["ffn1"].finish(red["hg"].token)
    up1 = update_layer("ffn_w_up", g_up1, 1, None)
    summed = _sum_small(_wait_copies("small_wait", small_flight[2], small_flight[0], small_flight[1], up1[3], _small_copies)[0], groups, widths)
    loss_out = summed[0][0, 0]
    small_grads = dict(zip(small_names, summed[1:]))
    small_grads["hg_norm"] = lax.dynamic_slice(small_grads["hg_norm"], (0, chip * ds), (1, ds))
    small_grads["hg_lb_logits"] = lax.dynamic_slice(small_grads["hg_lb_logits"], (0, chip * ds), (2, ds))
    small_grads["ffn_conv_w"] = lax.dynamic_slice(small_grads["ffn_conv_w"], (0, chip * fs), (2 * 3, fs))
    red["mid"].to_core(up1[1])
    down1 = update_layer("ffn_w_down", g_down1, 1, None)
    g_o, g_q, g_kv, g_down0, g_up0, g_out = red["mid"].finish(down1[1])
    update("attn_w_o", g_o)
    update("attn_w_q", g_q)
    update("w_kv", g_kv)
    update("hg_w_out", g_out)
    update_layer("ffn_w_down", g_down0, 0, down1)
    last = update_layer("ffn_w_up", g_up0, 0, up1)
    red["hg"].to_core(last[1])
    (g_in,) = red["hg"].finish(last[2])
    update("hg_w_in", g_in)

    as_2d = lambda a, n: a.reshape(small_grads[n].shape)
    updated = _adamw_small([(as_2d(wts[n], n), as_2d(mom1[n], n), as_2d(mom2[n], n), small_grads[n]) for n in small_names])
    for n, (d2, m2, v2) in zip(small_names, updated):
        shape = wts[n].shape
        out_g[n], out_d[n], out_m[n], out_v[n] = small_grads[n].reshape(shape), d2.reshape(shape), m2.reshape(shape), v2.reshape(shape)

    grad_x = dx.reshape(x.shape)
    return (loss_out, grad_x, *[out_g[n] for n in names], *[out_d[n] for n in names], *[out_m[n] for n in names], *[out_v[n] for n in names])
```

```python
import functools

import jax
import jax.numpy as jnp
from jax import lax
from jax.experimental import pallas as pl
from jax.experimental.pallas import tpu as pltpu

F32 = jnp.float32
BF16 = jnp.bfloat16
MESH = pl.DeviceIdType.MESH

EPS = 1e-6
D_MODEL = 1024
HG_HEADS = 8
HG_DK = 128
HG_CHUNK = 64
ATT_HD = 64
ATT_QH = 16
ATT_KVH = 2
ATT_GROUP = ATT_QH // ATT_KVH
WINDOW = 128
D_FF = 2816
N_CHIPS = 4
N_DEV = 8
LANES = 128
SUBLANES = 8
VMEM_LIMIT_BYTES = 56 * 1024 * 1024
NEG = -1e30
ALIBI_SLOPES = tuple(2.0 ** (-8.0 * h / ATT_QH) for h in range(1, ATT_QH + 1))

ADAM_LR = 0.001
ADAM_B1 = 0.9
ADAM_B2 = 0.999
ADAM_EPS = 1e-08
ADAM_WD = 0.01
ADAM_STEP = 10


def _cparams(sem=None):
    return pltpu.CompilerParams(dimension_semantics=sem, vmem_limit_bytes=VMEM_LIMIT_BYTES)


def _pick(n, cands):
    for c in cands:
        if n % c == 0:
            return c
    return n


def _sigmoid(x):
    return 0.5 * jnp.tanh(0.5 * x) + 0.5


def _dot(a, b, dims):
    return lax.dot_general(a, b, (dims, ((), ())), preferred_element_type=F32)


NN = ((1,), (0,))
NT = ((1,), (1,))
TN = ((0,), (0,))


MM_ROWS = 1024


def _rms_stats(xv):
    rstd = lax.rsqrt(jnp.mean(xv * xv, axis=-1, keepdims=True) + EPS)
    return xv * rstd, rstd


def _mm_operand(a_ref, gain_ref):
    if gain_ref is None:
        return a_ref[...].astype(BF16)
    return (_rms_stats(a_ref[...])[0] * gain_ref[...]).astype(BF16)


def _mm_nn(a, w, res=None, out_dtype=F32, name="mm_nn", gain=None):
    m, k = a.shape
    s, _, ns = w.shape
    tm = min(m, MM_ROWS)
    tn = _pick(ns, (1024, 1408, 512, 256, 128))
    npb = ns // tn

    def body(a_ref, w_ref, *rest):
        o_ref = rest[-1]
        acc = _dot(_mm_operand(a_ref, rest[0] if gain is not None else None), w_ref[...], NN)
        if res is not None:
            acc = acc + rest[-2][...]
        o_ref[...] = acc.astype(o_ref.dtype)

    in_specs = [
        pl.BlockSpec((tm, k), lambda i, j: (i, 0)),
        pl.BlockSpec((None, k, tn), lambda i, j: (j // npb, 0, j % npb)),
    ]
    args = [a, w]
    if gain is not None:
        in_specs.append(pl.BlockSpec((1, k), lambda i, j: (0, 0)))
        args.append(gain)
    if res is not None:
        in_specs.append(pl.BlockSpec((tm, tn), lambda i, j: (i, j)))
        args.append(res)
    return pl.pallas_call(
        body,
        name=name,
        grid=(m // tm, s * npb),
        in_specs=in_specs,
        out_specs=pl.BlockSpec((tm, tn), lambda i, j: (i, j)),
        out_shape=jax.ShapeDtypeStruct((m, s * ns), out_dtype),
        compiler_params=_cparams(("parallel", "parallel")),
    )(*args)


def _dy_spec(stacked, tm, tn, npb, row, kk):
    if stacked:
        return pl.BlockSpec((None, tm, tn), lambda *g: (kk(g) // npb, row(g), kk(g) % npb))
    return pl.BlockSpec((tm, tn), lambda *g: (row(g), kk(g)))


def _dep_specs(deps):
    return [pl.BlockSpec(d.shape, lambda *g: (0, 0)) for d in deps]


def _mm_nt(dy, w, stacked=False, out_dtype=F32, name="mm_nt", deps=(), norm_of=None, also=None):
    s, k, ns = w.shape
    m = dy.shape[1] if stacked else dy.shape[0]
    tm = min(m, MM_ROWS)
    tko = _pick(k, (1024, 1408, 512, 256))
    tn = _pick(ns, (1024, 1408, 512, 256))
    npb = ns // tn
    nk = s * npb
    fused = norm_of is not None
    assert not fused or tko == k
    assert also is None or fused

    def body(dy_ref, w_ref, *rest):
        acc_ref = rest[-1]
        i, kk = pl.program_id(0), pl.program_id(2)

        @pl.when(kk == 0)
        def _():
            acc_ref[...] = jnp.zeros_like(acc_ref)

        acc_ref[...] += _dot(dy_ref[...].astype(BF16), w_ref[...], NT)

        if not fused:
            @pl.when(kk == nk - 1)
            def _():
                rest[-2][...] = acc_ref[...].astype(rest[-2].dtype)
            return
        x_ref, g_ref, dres_ref = rest[:3]
        n_out = 3 if also is not None else 2
        dx_ref, dg_refs = rest[-1 - n_out], rest[-n_out:-1]

        @pl.when(jnp.logical_and(i == 0, kk == 0))
        def _():
            for dg_ref in dg_refs:
                dg_ref[...] = jnp.zeros_like(dg_ref)

        @pl.when(kk == nk - 1)
        def _():
            dxn = acc_ref[...]
            xhat, rstd = _rms_stats(x_ref[...])
            gd = dxn * g_ref[...]
            dg_refs[0][...] += jnp.sum(dxn * xhat, axis=0, keepdims=True)
            if also is not None:
                dy2_ref, w2_ref, g2_ref = rest[3:6]
                dxn2 = _dot(dy2_ref[...].astype(BF16), w2_ref[...], NT)
                gd = gd + dxn2 * g2_ref[...]
                dg_refs[1][...] += jnp.sum(dxn2 * xhat, axis=0, keepdims=True)
            dx_ref[...] = dres_ref[...] + rstd * (gd - xhat * jnp.mean(gd * xhat, axis=-1, keepdims=True))

    row = pl.BlockSpec((tm, tko), lambda i, j, kk: (i, j))
    vec = pl.BlockSpec((1, k), lambda i, j, kk: (0, 0))
    extra_in, extra_args = [], ()
    if fused:
        extra_in, extra_args = [row, vec, row], tuple(norm_of)
    if also is not None:
        n2 = also[0].shape[1]
        extra_in += [pl.BlockSpec((tm, n2), lambda i, j, kk: (i, 0)), pl.BlockSpec((None, k, n2), lambda i, j, kk: (0, 0, 0)), vec]
        extra_args += tuple(also)
    f32 = lambda shape: jax.ShapeDtypeStruct(shape, F32)
    return pl.pallas_call(
        body,
        name=name,
        grid=(m // tm, k // tko, nk),
        in_specs=[
            _dy_spec(stacked, tm, tn, npb, lambda g: g[0], lambda g: g[2]),
            pl.BlockSpec((None, tko, tn), lambda i, j, kk: (kk // npb, j, kk % npb)),
        ] + extra_in + _dep_specs(deps),
        out_specs=([row, vec] + ([vec] if also is not None else [])) if fused else row,
        out_shape=([f32((m, k)), f32((1, k))] + ([f32((1, k))] if also is not None else [])) if fused else jax.ShapeDtypeStruct((m, k), out_dtype),
        scratch_shapes=[pltpu.VMEM((tm, tko), F32)],
        compiler_params=_cparams(("arbitrary",) * 3 if fused else ("parallel", "parallel", "arbitrary")),
    )(dy, w, *extra_args, *deps)


def _mm_tn(a, dy, s, ns, stacked=False, name="mm_tn", deps=(), gain=None):
    m, k = a.shape
    tm = min(m, MM_ROWS)
    tk = _pick(k, (1024, 1408, 512, 256))
    tn = _pick(ns, (1024, 1408, 512, 256, 128))
    npb = ns // tn
    nm = m // tm
    assert gain is None or tk == k

    def body(a_ref, dy_ref, *rest):
        j, mm = pl.program_id(1), pl.program_id(2)
        if gain is None:
            o_ref, acc_ref = rest[-2:]
            lhs = a_ref[...].astype(BF16)
        else:
            o_ref, acc_ref, xn_ref = rest[-3:]

            @pl.when(j == 0)
            def _():
                xn_ref[mm] = _mm_operand(a_ref, rest[0])

            lhs = xn_ref[mm]

        @pl.when(mm == 0)
        def _():
            acc_ref[...] = jnp.zeros_like(acc_ref)

        acc_ref[...] += _dot(lhs, dy_ref[...].astype(BF16), TN)

        @pl.when(mm == nm - 1)
        def _():
            o_ref[...] = acc_ref[...]

    a_rows = (lambda i, j, mm: (mm, i)) if gain is None else (lambda i, j, mm: (jnp.where(j == 0, mm, 0), i))
    return pl.pallas_call(
        body,
        name=name,
        grid=(k // tk, s * npb, nm),
        in_specs=[
            pl.BlockSpec((tm, tk), a_rows),
            _dy_spec(stacked, tm, tn, npb, lambda g: g[2], lambda g: g[1]),
        ] + ([pl.BlockSpec((1, k), lambda i, j, mm: (0, 0))] if gain is not None else []) + _dep_specs(deps),
        out_specs=pl.BlockSpec((None, tk, tn), lambda i, j, mm: (j // npb, i, j % npb)),
        out_shape=jax.ShapeDtypeStruct((s, k, ns), F32),
        scratch_shapes=[pltpu.VMEM((tk, tn), F32)] + ([pltpu.VMEM((nm, tm, tk), BF16)] if gain is not None else []),
        compiler_params=_cparams(("parallel", "arbitrary", "arbitrary") if gain is not None else ("parallel", "parallel", "arbitrary")),
    )(a, dy, *(() if gain is None else (gain,)), *deps)


ROW_TILE = 512


def _loss_head(h, g, target):
    t, d = h.shape
    r = min(t, ROW_TILE)

    def body(h_ref, g_ref, t_ref, dh_ref, dg_ref, loss_ref):
        @pl.when(pl.program_id(0) == 0)
        def _():
            dg_ref[...] = jnp.zeros_like(dg_ref)
            loss_ref[...] = jnp.zeros_like(loss_ref)

        xv = h_ref[...]
        rstd = lax.rsqrt(jnp.mean(xv * xv, axis=-1, keepdims=True) + EPS)
        xhat = xv * rstd
        gv = g_ref[...]
        err = xhat * gv - t_ref[...]
        loss_ref[...] += 0.5 * jnp.sum(jnp.mean(err * err, axis=-1, keepdims=True), axis=0, keepdims=True)
        dy = err * (1.0 / d)
        gd = dy * gv
        dh_ref[...] = rstd * (gd - xhat * jnp.mean(gd * xhat, axis=-1, keepdims=True))
        dg_ref[...] += jnp.sum(dy * xhat, axis=0, keepdims=True)

    return pl.pallas_call(
        body,
        name="loss_head",
        grid=(t // r,),
        in_specs=[
            pl.BlockSpec((r, d), lambda i: (i, 0)),
            pl.BlockSpec((1, d), lambda i: (0, 0)),
            pl.BlockSpec((r, d), lambda i: (i, 0)),
        ],
        out_specs=[
            pl.BlockSpec((r, d), lambda i: (i, 0)),
            pl.BlockSpec((1, d), lambda i: (0, 0)),
            pl.BlockSpec((1, LANES), lambda i: (0, 0)),
        ],
        out_shape=[
            jax.ShapeDtypeStruct((t, d), F32),
            jax.ShapeDtypeStruct((1, d), F32),
            jax.ShapeDtypeStruct((1, LANES), F32),
        ],
        compiler_params=_cparams(("arbitrary",)),
    )(h, g, target)


CONV_ROWS = 512
CONV_COLS = 1408
BF16_ROWS = 16


def _conv_taps(x_ext, n):
    tot = x_ext.shape[0]
    g1 = pltpu.roll(x_ext, 1, 0)[tot - n:]
    g2 = pltpu.roll(x_ext, 2, 0)[tot - n:]
    return g2, g1


def _conv_fwd(up, conv_w, conv_b, name="conv_fwd"):
    t = up.shape[0]
    r = min(t, CONV_ROWS)
    tc = CONV_COLS
    ncb = D_FF // tc
    hb = r // BF16_ROWS

    def body(g_ref, halo_ref, v_ref, w_ref, b_ref, o_ref, c_ref):
        i = pl.program_id(1)
        g0 = g_ref[...].astype(F32)
        halo = halo_ref[...].astype(F32)[BF16_ROWS - SUBLANES:] * jnp.where(i > 0, 1.0, 0.0)
        g2, g1 = _conv_taps(jnp.concatenate([halo, g0], axis=0), r)
        c = b_ref[...] + w_ref[0:1, :] * g2 + w_ref[1:2, :] * g1 + w_ref[2:3, :] * g0
        c_ref[...] = c.astype(BF16)
        o_ref[...] = (c * _sigmoid(c) * v_ref[...].astype(F32)).astype(BF16)

    blk = pl.BlockSpec((r, tc), lambda j, i: (i, j))
    return pl.pallas_call(
        body,
        name=name,
        grid=(ncb, t // r),
        in_specs=[
            blk,
            pl.BlockSpec((BF16_ROWS, tc), lambda j, i: (jnp.maximum(i * hb - 1, 0), j)),
            pl.BlockSpec((r, tc), lambda j, i: (i, ncb + j)),
            pl.BlockSpec((3, tc), lambda j, i: (0, j)),
            pl.BlockSpec((1, tc), lambda j, i: (0, j)),
        ],
        out_specs=[blk, blk],
        out_shape=[jax.ShapeDtypeStruct((t, D_FF), BF16), jax.ShapeDtypeStruct((t, D_FF), BF16)],
        compiler_params=_cparams(("parallel", "parallel")),
    )(up, up, up, conv_w, conv_b)


def _conv_bwd(up, conv_w, c, dact, name="conv_bwd"):
    t = up.shape[0]
    r = min(t, CONV_ROWS)
    tc = CONV_COLS
    ncb = D_FF // tc
    nrt = t // r

    def body(g_ref, v_ref, w_ref, c_ref, da_ref, dup_ref, dw_ref, db_ref, nxt_ref):
        ii = pl.program_id(1)

        @pl.when(ii == 0)
        def _():
            nxt_ref[...] = jnp.zeros_like(nxt_ref)
            dw_ref[...] = jnp.zeros_like(dw_ref)
            db_ref[...] = jnp.zeros_like(db_ref)

        g0 = g_ref[...].astype(F32)
        w0, w1, w2 = w_ref[0:1, :], w_ref[1:2, :], w_ref[2:3, :]
        c = c_ref[...].astype(F32)
        sg = _sigmoid(c)
        da = da_ref[...]
        dup_ref[1] = (da * (c * sg)).astype(BF16)
        dc = da * v_ref[...].astype(F32) * (sg * (1.0 + c * (1.0 - sg)))
        ext = jnp.concatenate([dc, nxt_ref[...]], axis=0)
        tot = r + SUBLANES
        d1 = pltpu.roll(ext, tot - 1, 0)[:r]
        d2 = pltpu.roll(ext, tot - 2, 0)[:r]
        nxt_ref[...] = dc[:SUBLANES]
        dup_ref[0] = (w2 * dc + w1 * d1 + w0 * d2).astype(BF16)
        db_ref[...] += jnp.sum(dc, axis=0, keepdims=True)
        dw_ref[0:1, :] += jnp.sum(d2 * g0, axis=0, keepdims=True)
        dw_ref[1:2, :] += jnp.sum(d1 * g0, axis=0, keepdims=True)
        dw_ref[2:3, :] += jnp.sum(dc * g0, axis=0, keepdims=True)

    rev = lambda ii: nrt - 1 - ii
    dup, dw, db = pl.pallas_call(
        body,
        name=name,
        grid=(ncb, nrt),
        in_specs=[
            pl.BlockSpec((r, tc), lambda j, ii: (rev(ii), j)),
            pl.BlockSpec((r, tc), lambda j, ii: (rev(ii), ncb + j)),
            pl.BlockSpec((3, tc), lambda j, ii: (0, j)),
            pl.BlockSpec((r, tc), lambda j, ii: (rev(ii), j)),
            pl.BlockSpec((r, tc), lambda j, ii: (rev(ii), j)),
        ],
        out_specs=[
            pl.BlockSpec((2, None, r, tc), lambda j, ii: (0, j, rev(ii), 0)),
            pl.BlockSpec((3, tc), lambda j, ii: (0, j)),
            pl.BlockSpec((1, tc), lambda j, ii: (0, j)),
        ],
        out_shape=[
            jax.ShapeDtypeStruct((2, ncb, t, tc), BF16),
            jax.ShapeDtypeStruct((3, D_FF), F32),
            jax.ShapeDtypeStruct((1, D_FF), F32),
        ],
        scratch_shapes=[pltpu.VMEM((SUBLANES, tc), F32)],
        compiler_params=_cparams(("parallel", "arbitrary")),
    )(up, up, conv_w, c, dact)
    return dup.reshape(2 * ncb, t, tc), dw, db


def _split3(x):
    x1 = x.astype(BF16)
    r1 = x - x1.astype(F32)
    x2 = r1.astype(BF16)
    x3 = (r1 - x2.astype(F32)).astype(BF16)
    return x1, x2, x3


def _tri_dot(tri, x, dims):
    x1, x2, x3 = _split3(x)
    return _dot(tri, x1, dims) + _dot(tri, x2, dims) + _dot(tri, x3, dims)


def _lower_bound(logits_ref):
    return _sigmoid(logits_ref[0:1, :] - logits_ref[1:2, :])


def _hg_gates(qr, fr, lb):
    q = qr * _sigmoid(qr) * (HG_DK ** -0.5)
    sf = _sigmoid(fr)
    fg = lb + (1.0 - lb) * sf
    return q, sf, fg


def _hg_chunk_terms(q, fg, tril_b, low_half):
    g = jnp.log(fg)
    k = 1.0 - fg
    cum = _tri_dot(tril_b, g, NN)
    c_last = jnp.sum(g, axis=0, keepdims=True)
    c_mid = jnp.sum(jnp.where(low_half, g, 0.0), axis=0, keepdims=True)
    e_q = jnp.exp(cum - c_mid)
    e_k = jnp.exp(c_mid - cum)
    e_0 = jnp.exp(cum)
    e_l = jnp.exp(c_last - cum)
    return k, e_q, e_k, e_0, e_l, jnp.exp(c_last)


HG_BLOCK = 256


def _hg_proj_specs(rb, row):
    return [pl.BlockSpec((rb, D_MODEL), functools.partial(lambda i, k: (row(i), k), k=k)) for k in range(4)]


def _hg_consts(c):
    tril = lax.broadcasted_iota(jnp.int32, (c, c), 0) >= lax.broadcasted_iota(jnp.int32, (c, c), 1)
    low_half = lax.broadcasted_iota(jnp.int32, (c, D_MODEL), 0) < c // 2
    return tril, tril.astype(BF16), low_half


def _hgrn_fwd(proj, lb, wn):
    t = proj.shape[0]
    c = HG_CHUNK
    rb = min(t, HG_BLOCK)
    cpb = rb // c

    def body(q_ref, f_ref, i_ref, g_ref, lb_ref, wn_ref, o_ref, y_ref, st_ref, s_scr):
        @pl.when(pl.program_id(0) == 0)
        def _():
            s_scr[...] = jnp.zeros_like(s_scr)

        lb_all = _lower_bound(lb_ref)
        wnv = wn_ref[...]
        tril, tril_b, low_half = _hg_consts(c)

        def chunk(n, carry):
            rows = pl.ds(pl.multiple_of(n * c, c), c)
            q, _, fg = _hg_gates(q_ref[rows, :], f_ref[rows, :], lb_all)
            k, e_q, e_k, e_0, e_l, e_last = _hg_chunk_terms(q, fg, tril_b, low_half)
            qi, ki, q0, kl = (q * e_q).astype(BF16), (k * e_k).astype(BF16), (q * e_0).astype(BF16), (k * e_l).astype(BF16)
            v = i_ref[rows, :].astype(BF16)
            gr = g_ref[rows, :]
            gate = gr * _sigmoid(gr)
            for h in range(HG_HEADS):
                cols = slice(h * HG_DK, (h + 1) * HG_DK)
                st = s_scr[h]
                st_ref[h, n] = st
                a = jnp.where(tril, _dot(qi[:, cols], ki[:, cols], NT), 0.0)
                o = _dot(q0[:, cols], st.astype(BF16), NT) + _dot(a.astype(BF16), v[:, cols], NN)
                s_scr[h] = st * e_last[:, cols] + _dot(v[:, cols], kl[:, cols], TN)
                o_ref[rows, cols] = o
                rstd = lax.rsqrt(jnp.mean(o * o, axis=-1, keepdims=True) + EPS)
                y_ref[rows, cols] = (o * rstd * wnv * gate[:, cols]).astype(BF16)
            return carry

        lax.fori_loop(0, cpb, chunk, 0, unroll=2)

    blk = pl.BlockSpec((rb, D_MODEL), lambda i: (i, 0))
    return pl.pallas_call(
        body,
        name="hgrn_fwd",
        grid=(t // rb,),
        in_specs=_hg_proj_specs(rb, lambda i: i) + [pl.BlockSpec((2, D_MODEL), lambda i: (0, 0)), pl.BlockSpec((1, HG_DK), lambda i: (0, 0))],
        out_specs=[blk, blk, pl.BlockSpec((HG_HEADS, cpb, HG_DK, HG_DK), lambda i: (0, i, 0, 0))],
        out_shape=[
            jax.ShapeDtypeStruct((t, D_MODEL), F32),
            jax.ShapeDtypeStruct((t, D_MODEL), BF16),
            jax.ShapeDtypeStruct((HG_HEADS, t // c, HG_DK, HG_DK), F32),
        ],
        scratch_shapes=[pltpu.VMEM((HG_HEADS, HG_DK, HG_DK), F32)],
        compiler_params=_cparams(("arbitrary",)),
    )(proj, proj, proj, proj, lb, wn)


def _hgrn_bwd(proj, lb, wn, o, states, dy):
    t = proj.shape[0]
    c = HG_CHUNK
    rb = min(t, HG_BLOCK)
    cpb = rb // c
    nb = t // rb

    def body(q_ref, f_ref, i_ref, g_ref, lb_ref, wn_ref, o_ref, st_ref, dy_ref, dp_ref, dl_ref, dwn_ref, ds_scr, dlb_scr):
        step = pl.program_id(0)

        @pl.when(step == 0)
        def _():
            dwn_ref[...] = jnp.zeros_like(dwn_ref)
            ds_scr[...] = jnp.zeros_like(ds_scr)
            dlb_scr[...] = jnp.zeros_like(dlb_scr)

        lb_all = _lower_bound(lb_ref)
        wnv = wn_ref[...]
        tril, tril_b, low_half = _hg_consts(c)

        def chunk(nn, carry):
            n = cpb - 1 - nn
            rows = pl.ds(pl.multiple_of(n * c, c), c)
            qr = q_ref[rows, :]
            gr = g_ref[rows, :]
            q, sf, fg = _hg_gates(qr, f_ref[rows, :], lb_all)
            k, e_q, e_k, e_0, e_l, e_last = _hg_chunk_terms(q, fg, tril_b, low_half)
            qi, qi_lo, _ = _split3(q * e_q)
            ki, ki_lo, _ = _split3(k * e_k)
            q0 = (q * e_0).astype(BF16)
            kl = (k * e_l).astype(BF16)
            v = i_ref[rows, :].astype(BF16)
            sg = _sigmoid(gr)
            silu_g = gr * sg
            dsilu_g = sg * (1.0 + gr * (1.0 - sg))
            dqs, dks, d_lasts = [], [], []
            for h in range(HG_HEADS):
                cols = slice(h * HG_DK, (h + 1) * HG_DK)
                ov = o_ref[rows, cols]
                dyv = dy_ref[rows, cols].astype(F32)
                rstd = lax.rsqrt(jnp.mean(ov * ov, axis=-1, keepdims=True) + EPS)
                ohat = ov * rstd
                dp_ref[3, rows, cols] = (dyv * (ohat * wnv) * dsilu_g[:, cols]).astype(BF16)
                don = dyv * silu_g[:, cols]
                dwn_ref[...] += jnp.sum(don * ohat, axis=0, keepdims=True)
                gd = don * wnv
                do_b = (rstd * (gd - ohat * jnp.mean(gd * ohat, axis=-1, keepdims=True))).astype(BF16)
                st = st_ref[h, n]
                ds = ds_scr[h]
                ds_b = ds.astype(BF16)
                vh, kh = v[:, cols], k[:, cols]
                a_b = jnp.where(tril, _dot(qi[:, cols], ki[:, cols], NT), 0.0).astype(BF16)
                da_b = jnp.where(tril, _dot(do_b, vh, NT), 0.0).astype(BF16)
                dqs.append(_dot(do_b, st.astype(BF16), NN) * e_0[:, cols]
                           + (_dot(da_b, ki[:, cols], NN) + _dot(da_b, ki_lo[:, cols], NN)) * e_q[:, cols])
                dk_state = _dot(vh, ds_b, NN) * e_l[:, cols]
                dks.append((_dot(da_b, qi[:, cols], TN) + _dot(da_b, qi_lo[:, cols], TN)) * e_k[:, cols] + dk_state)
                dp_ref[2, rows, cols] = (_dot(a_b, do_b, TN) + _dot(kl[:, cols], ds_b, NT)).astype(BF16)
                ds_scr[h] = ds * e_last[:, cols] + _dot(do_b, q0[:, cols], TN)
                d_lasts.append(jnp.sum(dk_state * kh, axis=0, keepdims=True) + jnp.sum(ds * st, axis=0, keepdims=True) * e_last[:, cols])
            dq = jnp.concatenate(dqs, axis=1)
            dk = jnp.concatenate(dks, axis=1)
            dlogf = _tri_dot(tril_b, q * dq - k * dk, TN) + jnp.concatenate(d_lasts, axis=1)
            dfg = dlogf / fg - dk
            dlb_scr[...] += jnp.sum(dfg * (1.0 - sf), axis=0, keepdims=True)
            sq = _sigmoid(qr)
            dp_ref[0, rows, :] = (dq * (HG_DK ** -0.5) * (sq * (1.0 + qr * (1.0 - sq)))).astype(BF16)
            dp_ref[1, rows, :] = (dfg * (1.0 - lb_all) * sf * (1.0 - sf)).astype(BF16)
            return carry

        lax.fori_loop(0, cpb, chunk, 0, unroll=2)

        @pl.when(step == nb - 1)
        def _():
            d0 = dlb_scr[...] * lb_all * (1.0 - lb_all)
            dl_ref[0:1, :] = d0
            dl_ref[1:2, :] = -d0

    rev = lambda i: nb - 1 - i
    blk = pl.BlockSpec((rb, D_MODEL), lambda i: (rev(i), 0))
    return pl.pallas_call(
        body,
        name="hgrn_bwd",
        grid=(nb,),
        in_specs=_hg_proj_specs(rb, rev)
        + [pl.BlockSpec((2, D_MODEL), lambda i: (0, 0)), pl.BlockSpec((1, HG_DK), lambda i: (0, 0)), blk,
           pl.BlockSpec((HG_HEADS, cpb, HG_DK, HG_DK), lambda i: (0, rev(i), 0, 0)), blk],
        out_specs=[
            pl.BlockSpec((4, rb, D_MODEL), lambda i: (0, rev(i), 0)),
            pl.BlockSpec((2, D_MODEL), lambda i: (0, 0)),
            pl.BlockSpec((1, HG_DK), lambda i: (0, 0)),
        ],
        out_shape=[
            jax.ShapeDtypeStruct((4, t, D_MODEL), BF16),
            jax.ShapeDtypeStruct((2, D_MODEL), F32),
            jax.ShapeDtypeStruct((1, HG_DK), F32),
        ],
        scratch_shapes=[pltpu.VMEM((HG_HEADS, HG_DK, HG_DK), F32), pltpu.VMEM((1, D_MODEL), F32)],
        compiler_params=_cparams(("arbitrary",)),
    )(proj, proj, proj, proj, lb, wn, o, states, dy)


ATT_STACK = 8


def _att_stack(q_ref, sink_ref, first, lo, bias_p, bias_c, extra_ref=None):
    qs, bps, bcs, sinks, extras = [], [], [], None, []
    rows = lax.broadcasted_iota(jnp.int32, (ATT_STACK * WINDOW, 1), 0)
    for i in range(ATT_STACK):
        hq = first + i
        cols = slice((hq // 2) * LANES, (hq // 2 + 1) * LANES)
        sel = lo if hq % 2 == 0 else jnp.logical_not(lo)
        qp = q_ref[:, cols] * (ATT_HD ** -0.5)
        qs.append(jnp.where(sel, qp, jnp.zeros_like(qp)))
        bps.append(ALIBI_SLOPES[hq] * bias_p)
        bcs.append(ALIBI_SLOPES[hq] * bias_c)
        sinks = sink_ref[hq] if sinks is None else jnp.where(rows < i * WINDOW, sinks, sink_ref[hq])
        if extra_ref is not None:
            ep = extra_ref[:, cols]
            extras.append(jnp.where(sel, ep, jnp.zeros_like(ep)))
    cat = lambda parts: jnp.concatenate(parts, axis=0)
    return cat(qs), cat(bps), cat(bcs), sinks, (cat(extras) if extras else None)


def _att_rows(i):
    return slice(i * WINDOW, (i + 1) * WINDOW)


def _att_bias(n):
    tq = lax.broadcasted_iota(jnp.int32, (WINDOW, WINDOW), 0)
    sk = lax.broadcasted_iota(jnp.int32, (WINDOW, WINDOW), 1)
    valid_c = sk <= tq
    valid_p = (sk - tq) > jnp.where(n > 0, 0, WINDOW)
    dist_c = (tq - sk).astype(F32)
    return jnp.where(valid_p, -dist_c - float(WINDOW), NEG), jnp.where(valid_c, -dist_c, NEG)


def _att_halves(x, lo, kh):
    r = pltpu.roll(x, ATT_HD, 1)
    zero = jnp.zeros_like(x)
    if kh == 0:
        return jnp.where(lo, x, r), jnp.where(lo, x, zero), jnp.where(lo, zero, r)
    return jnp.where(lo, r, x), jnp.where(lo, r, zero), jnp.where(lo, zero, x)


def _att_probs(qm, k2p, k2c, bias_p, bias_c, sink):
    sp = _dot(qm, k2p, NT) + bias_p
    sc = _dot(qm, k2c, NT) + bias_c
    m = jnp.maximum(jnp.maximum(jnp.max(sp, axis=-1, keepdims=True), jnp.max(sc, axis=-1, keepdims=True)), sink)
    ep = jnp.exp(sp - m)
    ec = jnp.exp(sc - m)
    es = jnp.exp(sink - m)
    inv = 1.0 / (jnp.sum(ep, axis=-1, keepdims=True) + jnp.sum(ec, axis=-1, keepdims=True) + es)
    return ep * inv, ec * inv, es * inv


def _attn_fwd(q, kv, sinks):
    t = q.shape[0]
    nb = t // WINDOW

    def body(sink_ref, q_ref, kvp_ref, kvc_ref, o_ref):
        n = pl.program_id(0)
        bias_p, bias_c = _att_bias(n)
        lo = lax.broadcasted_iota(jnp.int32, (WINDOW, LANES), 1) < ATT_HD
        for kh in range(ATT_KVH):
            k2p, _, _ = _att_halves(kvp_ref[:, 0:LANES], lo, kh)
            k2c, _, _ = _att_halves(kvc_ref[:, 0:LANES], lo, kh)
            _, vlo_p, vhi_p = _att_halves(kvp_ref[:, LANES:2 * LANES], lo, kh)
            _, vlo_c, vhi_c = _att_halves(kvc_ref[:, LANES:2 * LANES], lo, kh)
            for first in range(kh * ATT_GROUP, (kh + 1) * ATT_GROUP, ATT_STACK):
                qs, bp, bc, sinks, _ = _att_stack(q_ref, sink_ref, first, lo, bias_p, bias_c)
                pp, pc, _ = _att_probs(qs, k2p, k2c, bp, bc, sinks)
                pp, pc = pp.astype(BF16), pc.astype(BF16)
                for i in range(0, ATT_STACK, 2):
                    even, odd = _att_rows(i), _att_rows(i + 1)
                    out = (_dot(pp[even], vlo_p, NN) + _dot(pc[even], vlo_c, NN)
                           + _dot(pp[odd], vhi_p, NN) + _dot(pc[odd], vhi_c, NN))
                    j = (first + i) // 2
                    o_ref[:, j * LANES:(j + 1) * LANES] = out.astype(BF16)

    return pl.pallas_call(
        body,
        name="attn_fwd",
        grid=(nb,),
        in_specs=[
            pl.BlockSpec(memory_space=pltpu.SMEM),
            pl.BlockSpec((WINDOW, D_MODEL), lambda n: (n, 0)),
            pl.BlockSpec((WINDOW, 2 * LANES), lambda n: (jnp.maximum(n - 1, 0), 0)),
            pl.BlockSpec((WINDOW, 2 * LANES), lambda n: (n, 0)),
        ],
        out_specs=pl.BlockSpec((WINDOW, D_MODEL), lambda n: (n, 0)),
        out_shape=jax.ShapeDtypeStruct((t, D_MODEL), BF16),
        compiler_params=_cparams(("parallel",)),
    )(sinks, q, kv, kv)


def _attn_bwd(q, kv, sinks, dout):
    t = q.shape[0]
    nb = t // WINDOW

    def body(sink_ref, q_ref, kvp_ref, kvc_ref, do_ref, dq_ref, dkv_ref, dsink_ref, carry_ref):
        n = pl.program_id(0)

        @pl.when(n == 0)
        def _():
            carry_ref[...] = jnp.zeros_like(carry_ref)
            dsink_ref[...] = jnp.zeros_like(dsink_ref)

        @pl.when(n == nb)
        def _():
            dkv_ref[...] = carry_ref[...].astype(BF16)

        @pl.when(n < nb)
        def _():
            bias_p, bias_c = _att_bias(n)
            lo = lax.broadcasted_iota(jnp.int32, (WINDOW, LANES), 1) < ATT_HD
            lane1 = lax.broadcasted_iota(jnp.int32, (1, LANES), 1)
            dsink = jnp.zeros((1, LANES), F32)
            halves = []
            for kh in range(ATT_KVH):
                k2p, klo_p, khi_p = _att_halves(kvp_ref[:, 0:LANES], lo, kh)
                k2c, klo_c, khi_c = _att_halves(kvc_ref[:, 0:LANES], lo, kh)
                v2p, _, _ = _att_halves(kvp_ref[:, LANES:2 * LANES], lo, kh)
                v2c, _, _ = _att_halves(kvc_ref[:, LANES:2 * LANES], lo, kh)
                acc = [jnp.zeros((WINDOW, LANES), F32) for _ in range(4)]
                for first in range(kh * ATT_GROUP, (kh + 1) * ATT_GROUP, ATT_STACK):
                    qs, bp, bc, sinks, dos = _att_stack(q_ref, sink_ref, first, lo, bias_p, bias_c, do_ref)
                    pp, pc, ps = _att_probs(qs, k2p, k2c, bp, bc, sinks)
                    dpp = _dot(dos, v2p, NT)
                    dpc = _dot(dos, v2c, NT)
                    delta = jnp.sum(pp * dpp, axis=-1, keepdims=True) + jnp.sum(pc * dpc, axis=-1, keepdims=True)
                    dsp = (pp * (dpp - delta)).astype(BF16)
                    dsc = (pc * (dpc - delta)).astype(BF16)
                    sink_term = ps * delta
                    for i in range(ATT_STACK):
                        dsink = dsink + jnp.where(lane1 == first + i, -jnp.sum(sink_term[_att_rows(i)], axis=0, keepdims=True), 0.0)
                    for i in range(0, ATT_STACK, 2):
                        even, odd = _att_rows(i), _att_rows(i + 1)
                        dq_pair = (_dot(dsp[even], klo_p, NN) + _dot(dsc[even], klo_c, NN)
                                   + _dot(dsp[odd], khi_p, NN) + _dot(dsc[odd], khi_c, NN))
                        j = (first + i) // 2
                        dq_ref[:, j * LANES:(j + 1) * LANES] = (dq_pair * (ATT_HD ** -0.5)).astype(BF16)
                    acc[0] = acc[0] + _dot(dsp, qs, TN)
                    acc[1] = acc[1] + _dot(dsc, qs, TN)
                    acc[2] = acc[2] + _dot(pp.astype(BF16), dos, TN)
                    acc[3] = acc[3] + _dot(pc.astype(BF16), dos, TN)
                halves.append([a + pltpu.roll(a, ATT_HD, 1) for a in acc])
            prev = jnp.concatenate(
                [jnp.where(lo, halves[0][0], halves[1][0]), jnp.where(lo, halves[0][2], halves[1][2])], axis=1)
            cur = jnp.concatenate(
                [jnp.where(lo, halves[0][1], halves[1][1]), jnp.where(lo, halves[0][3], halves[1][3])], axis=1)
            dkv_ref[...] = (carry_ref[...] + prev).astype(BF16)
            carry_ref[...] = cur
            dsink_ref[...] += dsink

    blk = lambda n: jnp.minimum(n, nb - 1)
    return pl.pallas_call(
        body,
        name="attn_bwd",
        grid=(nb + 1,),
        in_specs=[
            pl.BlockSpec(memory_space=pltpu.SMEM),
            pl.BlockSpec((WINDOW, D_MODEL), lambda n: (blk(n), 0)),
            pl.BlockSpec((WINDOW, 2 * LANES), lambda n: (jnp.maximum(blk(n) - 1, 0), 0)),
            pl.BlockSpec((WINDOW, 2 * LANES), lambda n: (blk(n), 0)),
            pl.BlockSpec((WINDOW, D_MODEL), lambda n: (blk(n), 0)),
        ],
        out_specs=[
            pl.BlockSpec((WINDOW, D_MODEL), lambda n: (blk(n), 0)),
            pl.BlockSpec((WINDOW, 2 * LANES), lambda n: (jnp.maximum(n - 1, 0), 0)),
            pl.BlockSpec((1, LANES), lambda n: (0, 0)),
        ],
        out_shape=[
            jax.ShapeDtypeStruct((t, D_MODEL), BF16),
            jax.ShapeDtypeStruct((t, 2 * LANES), BF16),
            jax.ShapeDtypeStruct((1, LANES), F32),
        ],
        scratch_shapes=[pltpu.VMEM((WINDOW, 2 * LANES), F32)],
        compiler_params=_cparams(("arbitrary",)),
    )(sinks, q, kv, kv, dout)


def _ffn_fwd(h, norm_g, w_up, conv_w, conv_b, w_down, tag, after_up=lambda up: None):
    up = _mm_nn(h, w_up, gain=norm_g, out_dtype=BF16, name=f"ffn{tag}_up")
    after_up(up)
    act, c = _conv_fwd(up, conv_w, conv_b, name=f"ffn{tag}_conv")
    h_out = _mm_nn(act, w_down, res=h, name=f"ffn{tag}_down")
    return h_out, (up, act, c)


def _ffn_bwd(dh, h, norm_g, w_up, conv_w, conv_b, w_down, saved, tag, deps=()):
    up, act, c = saved
    dw_down = _mm_tn(act, dh, 1, D_MODEL, name=f"ffn{tag}_dwdown", deps=deps)
    dact = _mm_nt(dh, w_down, name=f"ffn{tag}_dact", deps=deps)
    dup, dconv_w, dconv_b = _conv_bwd(up, conv_w, c, dact, name=f"ffn{tag}_dconv")
    dw_up = _mm_tn(h, dup, N_CHIPS, CONV_COLS, stacked=True, gain=norm_g, name=f"ffn{tag}_dwup")
    dh_in, dnorm = _mm_nt(dup, w_up, stacked=True, norm_of=(h, norm_g, dh), name=f"ffn{tag}_dxn")
    return dh_in, dict(ffn_w_down=dw_down, ffn_w_up=dw_up, ffn_conv_w=dconv_w, ffn_conv_b=dconv_b, ffn_norm=dnorm)


def _local_step(x, target, w, fetch=lambda w, stage, after: w, hook=lambda point, dh, grads: ()):
    proj = _mm_nn(x, w["hg_w_in"], gain=w["hg_norm"], name="hg_in")
    o, y, states = _hgrn_fwd(proj, w["hg_lb"], w["hg_out_norm"])
    w = fetch(w, "mixer_out", y)
    fetch(w, "layer0_relay", y)
    h_a = _mm_nn(y, w["hg_w_out"], res=x, name="hg_out")
    w = fetch(w, "layer0", h_a)
    h1, ffn0 = _ffn_fwd(h_a, w["ffn_norm"][0], w["ffn_w_up"][0], w["ffn_conv_w"][0], w["ffn_conv_b"][0], w["ffn_w_down"][0], 0,
                        lambda up: fetch(w, "layer1_relay", up))
    w = fetch(w, "layer1", h1)
    kv = _mm_nn(h1, w["w_kv"], gain=w["kv_norm"], out_dtype=BF16, name="kv_proj")
    qa = _mm_nn(h1, w["attn_w_q"], gain=w["attn_norm"], out_dtype=BF16, name="attn_q")
    ao = _attn_fwd(qa, kv, w["attn_sinks"])
    h_b = _mm_nn(ao, w["attn_w_o"], res=h1, name="attn_o")
    h2, ffn1 = _ffn_fwd(h_b, w["ffn_norm"][1], w["ffn_w_up"][1], w["ffn_conv_w"][1], w["ffn_conv_b"][1], w["ffn_w_down"][1], 1)
    dh2, d_final, loss = _loss_head(h2, w["final_norm"], target)

    dh_b, g1 = _ffn_bwd(dh2, h_b, w["ffn_norm"][1], w["ffn_w_up"][1], w["ffn_conv_w"][1], w["ffn_conv_b"][1], w["ffn_w_down"][1], ffn1, 1)
    deps = hook("ffn1", dh_b, g1)
    dw_o = _mm_tn(ao, dh_b, 1, D_MODEL, name="attn_dwo", deps=deps)
    dao = _mm_nt(dh_b, w["attn_w_o"], out_dtype=BF16, name="attn_dao", deps=deps)
    dqa, dkv, dsinks = _attn_bwd(qa, kv, w["attn_sinks"], dao)
    dw_q = _mm_tn(h1, dqa, 1, D_MODEL, gain=w["attn_norm"], name="attn_dwq")
    dw_kv = _mm_tn(h1, dkv, 1, 2 * LANES, gain=w["kv_norm"], name="kv_dw")
    dh1, d_attn_norm, d_kv_norm = _mm_nt(dqa, w["attn_w_q"], norm_of=(h1, w["attn_norm"], dh_b),
                                        also=(dkv, w["w_kv"], w["kv_norm"]), name="attn_dxa")
    deps = hook("attn", dh1, dict(attn_w_o=dw_o, attn_w_q=dw_q, w_kv=dw_kv))
    dh_a, g0 = _ffn_bwd(dh1, h_a, w["ffn_norm"][0], w["ffn_w_up"][0], w["ffn_conv_w"][0], w["ffn_conv_b"][0], w["ffn_w_down"][0], ffn0, 0, deps)
    dw_out = _mm_tn(y, dh_a, 1, D_MODEL, name="hg_dwout")
    deps = hook("ffn0", dh_a, dict(g0, hg_w_out=dw_out))
    dy = _mm_nt(dh_a, w["hg_w_out"], out_dtype=BF16, name="hg_dy", deps=deps)
    dproj, dlb, d_out_norm = _hgrn_bwd(proj, w["hg_lb"], w["hg_out_norm"], o, states, dy)
    deps = hook("hgrn", dproj, None)
    dw_in = _mm_tn(x, dproj, N_CHIPS, D_MODEL, stacked=True, gain=w["hg_norm"], name="hg_dwin", deps=deps)
    deps = hook("hg_w", dproj, dict(hg_w_in=dw_in))
    dx, d_hg_norm = _mm_nt(dproj, w["hg_w_in"], stacked=True, norm_of=(x, w["hg_norm"], dh_a), name="hg_dxn", deps=deps)

    grads = dict(
        hg_norm=d_hg_norm, hg_w_in=dw_in, hg_lb=dlb, hg_out_norm=d_out_norm, hg_w_out=dw_out,
        kv_norm=d_kv_norm, w_kv=dw_kv, attn_norm=d_attn_norm, attn_w_q=dw_q, attn_sinks=dsinks, attn_w_o=dw_o,
        final_norm=d_final,
    )
    for name in g0:
        grads[name] = [g0[name], g1[name]]
    return loss, dx, grads


ANY = pl.BlockSpec(memory_space=pl.ANY)


def _place():
    x, y, c = lax.axis_index("x"), lax.axis_index("y"), lax.axis_index("c")
    chips = [(1 - x, y), (x, 1 - y), (1 - x, 1 - y)]
    return x, y, c, chips


def _rcopy(src, dst, send_sem, recv_sem, to):
    return pltpu.make_async_remote_copy(src_ref=src, dst_ref=dst, send_sem=send_sem, recv_sem=recv_sem, device_id=to, device_id_type=MESH)


HBM = pl.BlockSpec(memory_space=pltpu.HBM)
SEM = pl.BlockSpec(memory_space=pltpu.SEMAPHORE)
EFFECT = pltpu.SideEffectType.DATAFLOW_SIDE_EFFECTING


def _in_hbm(a):
    return pltpu.with_memory_space_constraint(a, pltpu.HBM)


def _place_shard(shard, place, dtype, name, deps=(), layer=None):
    r, cols = shard.shape[-2:]
    tr = _pick(r, ELEM_ROWS)
    src = pl.BlockSpec((tr, cols), lambda i, place_ref: (i, 0)) if layer is None else pl.BlockSpec((None, tr, cols), lambda i, place_ref: (layer, i, 0))

    def body(place_ref, s_ref, *rest):
        o_ref = rest[-1]
        o_ref[...] = s_ref[...].astype(o_ref.dtype)

    return pl.pallas_call(
        body,
        name=name,
        grid_spec=pltpu.PrefetchScalarGridSpec(
            num_scalar_prefetch=1,
            grid=(r // tr,),
            in_specs=[src] + _dep_specs(deps),
            out_specs=pl.BlockSpec((None, tr, cols), lambda i, place_ref: (place_ref[0], i, 0)),
        ),
        out_shape=jax.ShapeDtypeStruct((N_CHIPS, r, cols), dtype),
        compiler_params=_cparams(("parallel",)),
    )(place, shard, *deps)


def _start_copies(name, bufs, n_sem, copies):
    n = len(bufs)

    def body(*refs):
        for cp in copies(refs[:n], refs[n], refs[n + 1]):
            cp.start()
        refs[-1][...] = jnp.zeros_like(refs[-1])

    outs = pl.pallas_call(
        body,
        name=name,
        in_specs=[HBM] * n,
        out_specs=[SEM, SEM] + [HBM] * n + [pl.BlockSpec(memory_space=pltpu.VMEM)],
        out_shape=[pltpu.SemaphoreType.DMA((n_sem,)), pltpu.SemaphoreType.DMA((n_sem,))] + [pltpu.HBM(b.shape, b.dtype) for b in bufs]
        + [jax.ShapeDtypeStruct((SUBLANES, LANES), F32)],
        input_output_aliases={i: 2 + i for i in range(n)},
        compiler_params=pltpu.CompilerParams(has_side_effects=EFFECT),
    )(*[_in_hbm(b) for b in bufs])
    return outs[0], outs[1], list(outs[2:-1]), outs[-1]


def _wait_copies(name, bufs, send_sems, recv_sems, after, copies):
    n = len(bufs)

    def body(*refs):
        for cp in copies(refs[:n], refs[n], refs[n + 1]):
            cp.wait_send()
            cp.wait_recv()

    return pl.pallas_call(
        body,
        name=name,
        in_specs=[HBM] * n + [SEM, SEM, ANY],
        out_specs=[HBM] * n,
        out_shape=[pltpu.HBM(b.shape, b.dtype) for b in bufs],
        input_output_aliases={i: i for i in range(n)},
        compiler_params=pltpu.CompilerParams(has_side_effects=EFFECT),
    )(*bufs, send_sems, recv_sems, after)


def _relay_copies(name, bufs, send_sems, recv_sems, after, landed, n_sem, onward):
    n = len(bufs)

    def body(*refs):
        for cp in landed(refs[:n], refs[n], refs[n + 1]):
            cp.wait_send()
            cp.wait_recv()
        for cp in onward(refs[:n], refs[n + 3], refs[n + 4]):
            cp.start()
        refs[-1][...] = jnp.zeros_like(refs[-1])

    outs = pl.pallas_call(
        body,
        name=name,
        in_specs=[HBM] * n + [SEM, SEM, ANY],
        out_specs=[SEM, SEM] + [HBM] * n + [pl.BlockSpec(memory_space=pltpu.VMEM)],
        out_shape=[pltpu.SemaphoreType.DMA((n_sem,)), pltpu.SemaphoreType.DMA((n_sem,))] + [pltpu.HBM(b.shape, b.dtype) for b in bufs]
        + [jax.ShapeDtypeStruct((SUBLANES, LANES), F32)],
        input_output_aliases={i: 2 + i for i in range(n)},
        compiler_params=pltpu.CompilerParams(has_side_effects=EFFECT),
    )(*bufs, send_sems, recv_sems, after)
    return outs[0], outs[1], list(outs[2:-1]), outs[-1]


def _gather_half_copies(first, count, over_ici):
    def copies(refs, send_sems, recv_sems):
        x, y, c, chips = _place()
        out = []
        for i in range(count):
            h = refs[i].shape[1] // 2
            mine = pl.ds(c * h, h)
            for j, (px, py) in enumerate(chips):
                k = 3 * (first + i) + j
                slot = 2 * x + y if over_ici else 2 * px + py
                to = (px, py, c) if over_ici else (x, y, 1 - c)
                out.append(_rcopy(refs[i].at[slot, mine], refs[i].at[slot, mine], send_sems.at[k], recv_sems.at[k], to))
        return out

    return copies


def _gather_copies(first, count):
    def copies(refs, send_sems, recv_sems):
        x, y, c, chips = _place()
        me = 2 * x + y
        out = []
        for i in range(count):
            for j, (px, py) in enumerate(chips):
                k = 3 * (first + i) + j
                out.append(_rcopy(refs[i].at[me], refs[i].at[me], send_sems.at[k], recv_sems.at[k], (px, py, c)))
        return out

    return copies


def _swap_copies(n):
    def copies(refs, send_sems, recv_sems):
        x, y, c, _ = _place()
        out = []
        for i in range(n):
            h = refs[i].shape[1] // 2
            out.append(_rcopy(refs[i].at[:, pl.ds((1 - c) * h, h)], refs[n + i], send_sems.at[i], recv_sems.at[i], (x, y, 1 - c)))
        return out

    return copies


def _partial_copies(n):
    def copies(refs, send_sems, recv_sems):
        x, y, c, chips = _place()
        out = []
        for i in range(n):
            for j, (px, py) in enumerate(chips):
                out.append(_rcopy(refs[i].at[2 * px + py], refs[n + i].at[j], send_sems.at[3 * i + j], recv_sems.at[3 * i + j], (px, py, c)))
        return out

    return copies


def _share_copies(n):
    def copies(refs, send_sems, recv_sems):
        x, y, c, _ = _place()
        return [_rcopy(refs[i].at[c], refs[i].at[c], send_sems.at[i], recv_sems.at[i], (x, y, 1 - c)) for i in range(n)]

    return copies


def _small_layout(groups):
    flat = [a for g in groups for a in g]
    rows = -(-sum(a.shape[0] for a in flat) // SUBLANES) * SUBLANES
    return flat, rows, max(a.shape[1] for a in flat)


def _pack_small(groups, device):
    flat, rows, cols = _small_layout(groups)

    def body(dev_ref, *refs):
        o_ref = refs[-1]
        o_ref[...] = jnp.zeros_like(o_ref)
        r0 = 0
        for a_ref in refs[:-1]:
            r, w = a_ref.shape
            o_ref[r0:r0 + r, 0:w] = a_ref[...]
            r0 += r

    return pl.pallas_call(
        body,
        name="small_pack",
        grid_spec=pltpu.PrefetchScalarGridSpec(
            num_scalar_prefetch=1,
            grid=(1,),
            in_specs=[pl.BlockSpec(a.shape, lambda i, dev_ref: (0, 0)) for a in flat],
            out_specs=pl.BlockSpec((None, rows, cols), lambda i, dev_ref: (dev_ref[0], 0, 0)),
        ),
        out_shape=jax.ShapeDtypeStruct((N_DEV, rows, cols), F32),
        compiler_params=_cparams(("arbitrary",)),
    )(device, *flat)


def _small_copies(refs, send_sems, recv_sems):
    x, y, c, _ = _place()
    me = 4 * x + 2 * y + c
    out = []
    for k in range(1, N_DEV):
        peer = (x ^ (k >> 2), y ^ ((k >> 1) & 1), c ^ (k & 1))
        out.append(_rcopy(refs[0].at[me], refs[0].at[me], send_sems.at[k - 1], recv_sems.at[k - 1], peer))
    return out


def _sum_small(slots, groups, widths):
    out_shapes = [(sum(a.shape[0] for a in g), wd or g[0].shape[1]) for g, wd in zip(groups, widths)]

    def body(s_ref, *refs):
        outs, acc_ref = refs[:-1], refs[-1]
        acc = s_ref[0]
        for d in range(1, N_DEV):
            acc = acc + s_ref[d]
        acc_ref[...] = acc
        r0 = 0
        for o_ref in outs:
            r, w = o_ref.shape
            o_ref[...] = acc_ref[r0:r0 + r, 0:w]
            r0 += r

    vmem = pl.BlockSpec(memory_space=pltpu.VMEM)
    return pl.pallas_call(
        body,
        name="small_sum",
        in_specs=[vmem],
        out_specs=[vmem] * len(groups),
        out_shape=[jax.ShapeDtypeStruct(s, F32) for s in out_shapes],
        scratch_shapes=[pltpu.VMEM(slots.shape[1:], F32)],
        compiler_params=pltpu.CompilerParams(vmem_limit_bytes=VMEM_LIMIT_BYTES),
    )(slots)


def _adamw_small(items):
    n = len(items)

    def body(*refs):
        for i in range(n):
            w_ref, m_ref, v_ref, g_ref = refs[4 * i:4 * i + 4]
            d_ref, nm_ref, nv_ref = refs[4 * n + 3 * i:4 * n + 3 * i + 3]
            d_ref[...], nm_ref[...], nv_ref[...] = _adamw_math(w_ref[...], m_ref[...], v_ref[...], g_ref[...])

    vmem = pl.BlockSpec(memory_space=pltpu.VMEM)
    outs = pl.pallas_call(
        body,
        name="adamw_small",
        in_specs=[vmem] * (4 * n),
        out_specs=[vmem] * (3 * n),
        out_shape=[jax.ShapeDtypeStruct(it[0].shape, F32) for it in items for _ in range(3)],
        compiler_params=pltpu.CompilerParams(vmem_limit_bytes=VMEM_LIMIT_BYTES),
    )(*[a for it in items for a in it])
    return [tuple(outs[3 * i:3 * i + 3]) for i in range(n)]


class _Reduction:
    def __init__(self, tag, grads, place):
        self.tag, self.n, self.place = tag, len(grads), place
        lands = [lax.empty((N_CHIPS, g.shape[1] // 2, g.shape[2]), F32) for g in grads]
        self._start("swap", list(grads) + lands, self.n, _swap_copies(self.n))

    def _start(self, stage, bufs, n_sem, copies):
        *self.flight, self.token = _start_copies(f"rs_{stage}_start_{self.tag}", bufs, n_sem, copies)

    def _landed(self, stage, after, copies):
        send_sems, recv_sems, bufs = self.flight
        return _wait_copies(f"rs_{stage}_wait_{self.tag}", bufs, send_sems, recv_sems, after, copies)

    def to_chips(self, after):
        n = self.n
        bufs = self._landed("swap", after, _swap_copies(n))
        sums = [_add_core_halves(g, o, self.place, name=f"rs_add_core_{self.tag}_{i}") for i, (g, o) in enumerate(zip(bufs[:n], bufs[n:]))]
        self.mine = [f for f, _ in sums]
        parts = [b for _, b in sums]
        lands = [lax.empty((3,) + p.shape[1:], BF16) for p in parts]
        self._start("send", parts + lands, 3 * n, _partial_copies(n))

    def to_core(self, after):
        n = self.n
        bufs = self._landed("send", after, _partial_copies(n))
        halves = [_add_chip_partials(f, o, self.place, name=f"rs_add_chip_{self.tag}_{i}") for i, (f, o) in enumerate(zip(self.mine, bufs[n:]))]
        self._start("share", halves, n, _share_copies(n))

    def finish(self, after):
        return [b.reshape((-1,) + b.shape[2:]) for b in self._landed("share", after, _share_copies(self.n))]


ELEM_ROWS = (512, 352, 256, 176, 128, 64, 32, 16, 8)


def _add_core_halves(grad, got, place, name):
    s, r, cols = grad.shape
    h = r // 2
    tr = _pick(h, ELEM_ROWS)

    def body(place_ref, g_ref, o_ref, f_ref, b_ref):
        acc = g_ref[...] + o_ref[...]
        b_ref[...] = acc.astype(BF16)

        @pl.when(pl.program_id(1) == place_ref[0])
        def _():
            f_ref[...] = acc

    blk = pl.BlockSpec((None, tr, cols), lambda i, k, place_ref: (k, i, 0))
    return pl.pallas_call(
        body,
        name=name,
        grid_spec=pltpu.PrefetchScalarGridSpec(
            num_scalar_prefetch=1,
            grid=(h // tr, s),
            in_specs=[pl.BlockSpec((None, None, tr, cols), lambda i, k, place_ref: (k, place_ref[1], i, 0)), blk],
            out_specs=[pl.BlockSpec((tr, cols), lambda i, k, place_ref: (i, 0)), blk],
        ),
        out_shape=[jax.ShapeDtypeStruct((h, cols), F32), jax.ShapeDtypeStruct((s, h, cols), BF16)],
        compiler_params=_cparams(("parallel", "arbitrary")),
    )(place, grad.reshape(s, 2, h, cols), got)


def _add_chip_partials(mine, got, place, name):
    h, cols = mine.shape
    tr = _pick(h, ELEM_ROWS)

    def body(place_ref, m_ref, g_ref, o_ref):
        acc = m_ref[...]
        for j in range(3):
            acc = acc + g_ref[j].astype(F32)
        o_ref[...] = acc

    return pl.pallas_call(
        body,
        name=name,
        grid_spec=pltpu.PrefetchScalarGridSpec(
            num_scalar_prefetch=1,
            grid=(h // tr,),
            in_specs=[
                pl.BlockSpec((tr, cols), lambda i, place_ref: (i, 0)),
                pl.BlockSpec((3, tr, cols), lambda i, place_ref: (0, i, 0)),
            ],
            out_specs=pl.BlockSpec((None, tr, cols), lambda i, place_ref: (place_ref[1], i, 0)),
        ),
        out_shape=jax.ShapeDtypeStruct((2, h, cols), F32),
        compiler_params=_cparams(("parallel",)),
    )(place, mine, got)


def _adamw_math(w, m, v, g):
    nm = ADAM_B1 * m + (1.0 - ADAM_B1) * g
    nv = ADAM_B2 * v + (1.0 - ADAM_B2) * (g * g)
    m_hat = nm * (1.0 / (1.0 - ADAM_B1 ** ADAM_STEP))
    v_hat = nv * (1.0 / (1.0 - ADAM_B2 ** ADAM_STEP))
    return -ADAM_LR * (m_hat / (jnp.sqrt(v_hat) + ADAM_EPS) + ADAM_WD * w), nm, nv


def _adamw_layer(w, m, v, g, layer, prev, name):
    nl, r, cols = w.shape
    tr = _pick(r, ELEM_ROWS)

    def body(w_ref, m_ref, v_ref, g_ref, *rest):
        go_ref, d_ref, nm_ref, nv_ref = rest[-4:]
        gv = g_ref[...]
        d_ref[...], nm_ref[...], nv_ref[...] = _adamw_math(w_ref[...], m_ref[...], v_ref[...], gv)
        go_ref[...] = gv

    lay = pl.BlockSpec((None, tr, cols), lambda i: (layer, i, 0))
    return pl.pallas_call(
        body,
        name=name,
        grid=(r // tr,),
        in_specs=[lay] * 3 + [pl.BlockSpec((tr, cols), lambda i: (i, 0))] + ([ANY] * 4 if prev else []),
        out_specs=[lay] * 4,
        out_shape=[jax.ShapeDtypeStruct((nl, r, cols), F32)] * 4,
        input_output_aliases={4 + k: k for k in range(4)} if prev else {},
        compiler_params=_cparams(("parallel",)),
    )(w, m, v, g, *(prev or ()))


def _adamw(w, m, v, g, name):
    r, cols = w.shape
    tr = _pick(r, ELEM_ROWS)

    def body(w_ref, m_ref, v_ref, g_ref, d_ref, nm_ref, nv_ref):
        d_ref[...], nm_ref[...], nv_ref[...] = _adamw_math(w_ref[...], m_ref[...], v_ref[...], g_ref[...])

    blk = pl.BlockSpec((tr, cols), lambda i: (i, 0))
    return pl.pallas_call(
        body,
        name=name,
        grid=(r // tr,),
        in_specs=[blk] * 4,
        out_specs=[blk] * 3,
        out_shape=[jax.ShapeDtypeStruct((r, cols), F32)] * 3,
        compiler_params=_cparams(("parallel",)),
    )(w, m, v, g)


SMALL_COLS = 384
SMALL_ROWS = 16


def _pad_rows(flat, rows, cols):
    return jnp.pad(flat, (0, rows * cols - flat.shape[0])).reshape(rows, cols)


def kernel(x, hg_norm, hg_w_in, hg_lb_logits, hg_out_norm, hg_w_out, kv_norm, w_kv, attn_norm, attn_w_q, attn_sinks, attn_w_o, ffn_norm, ffn_w_up, ffn_conv_w, ffn_conv_b, ffn_w_down, final_norm, loss_target, m_hg_norm, m_hg_w_in, m_hg_lb_logits, m_hg_out_norm, m_hg_w_out, m_kv_norm, m_w_kv, m_attn_norm, m_attn_w_q, m_attn_sinks, m_attn_w_o, m_ffn_norm, m_ffn_w_up, m_ffn_conv_w, m_ffn_conv_b, m_ffn_w_down, m_final_norm, v_hg_norm, v_hg_w_in, v_hg_lb_logits, v_hg_out_norm, v_hg_w_out, v_kv_norm, v_w_kv, v_attn_norm, v_attn_w_q, v_attn_sinks, v_attn_w_o, v_ffn_norm, v_ffn_w_up, v_ffn_conv_w, v_ffn_conv_b, v_ffn_w_down, v_final_norm):
    wts = dict(hg_norm=hg_norm, hg_w_in=hg_w_in, hg_lb_logits=hg_lb_logits, hg_out_norm=hg_out_norm, hg_w_out=hg_w_out, kv_norm=kv_norm, w_kv=w_kv, attn_norm=attn_norm, attn_w_q=attn_w_q, attn_sinks=attn_sinks, attn_w_o=attn_w_o, ffn_norm=ffn_norm, ffn_w_up=ffn_w_up, ffn_conv_w=ffn_conv_w, ffn_conv_b=ffn_conv_b, ffn_w_down=ffn_w_down, final_norm=final_norm)
    mom1 = dict(hg_norm=m_hg_norm, hg_w_in=m_hg_w_in, hg_lb_logits=m_hg_lb_logits, hg_out_norm=m_hg_out_norm, hg_w_out=m_hg_w_out, kv_norm=m_kv_norm, w_kv=m_w_kv, attn_norm=m_attn_norm, attn_w_q=m_attn_w_q, attn_sinks=m_attn_sinks, attn_w_o=m_attn_w_o, ffn_norm=m_ffn_norm, ffn_w_up=m_ffn_w_up, ffn_conv_w=m_ffn_conv_w, ffn_conv_b=m_ffn_conv_b, ffn_w_down=m_ffn_w_down, final_norm=m_final_norm)
    mom2 = dict(hg_norm=v_hg_norm, hg_w_in=v_hg_w_in, hg_lb_logits=v_hg_lb_logits, hg_out_norm=v_hg_out_norm, hg_w_out=v_hg_w_out, kv_norm=v_kv_norm, w_kv=v_w_kv, attn_norm=v_attn_norm, attn_w_q=v_attn_w_q, attn_sinks=v_attn_sinks, attn_w_o=v_attn_w_o, ffn_norm=v_ffn_norm, ffn_w_up=v_ffn_w_up, ffn_conv_w=v_ffn_conv_w, ffn_conv_b=v_ffn_conv_b, ffn_w_down=v_ffn_w_down, final_norm=v_final_norm)
    names = list(wts)
    chip = 2 * lax.axis_index("x") + lax.axis_index("y")
    core = lax.axis_index("c")
    fs = D_FF // N_CHIPS
    ds = D_MODEL // N_CHIPS

    place_arr = jnp.stack([chip, core]).astype(jnp.int32)
    small = jnp.concatenate([hg_norm.reshape(-1), hg_lb_logits.reshape(-1), ffn_conv_w.reshape(-1)])
    n_small = small.shape[0]
    shards = [
        ("small", _pad_rows(small, SMALL_ROWS, SMALL_COLS), F32, None), ("hg_w_in", hg_w_in, BF16, 0),
        ("hg_w_out", hg_w_out, BF16, 0), ("ffn_w_up0", ffn_w_up, BF16, 0), ("ffn_w_down0", ffn_w_down, BF16, 0),
        ("w_kv", w_kv, BF16, None), ("attn_w_q", attn_w_q, BF16, 0), ("attn_w_o", attn_w_o, BF16, 0),
        ("ffn_w_up1", ffn_w_up, BF16, 1), ("ffn_w_down1", ffn_w_down, BF16, 1),
    ]
    n_first = 3
    spans = dict(layer0=(0, 2), layer1=(2, 7))

    def first_copies(refs, send_sems, recv_sems):
        return (_gather_copies(0, 1)(refs[:1], send_sems, recv_sems) + _gather_half_copies(1, 1, True)(refs[1:2], send_sems, recv_sems)
                + _gather_copies(2, 1)(refs[2:3], send_sems, recv_sems))

    placed = [_place_shard(s, place_arr, dt, name=f"place_{nm}", layer=ly) for nm, s, dt, ly in shards[:n_first]]
    first = _start_copies("gather_start_first", placed, 3 * n_first, first_copies)
    placed = [_place_shard(s, place_arr, dt, name=f"place_{nm}", deps=(first[3],), layer=ly) for nm, s, dt, ly in shards[n_first:]]
    rest = _start_copies("gather_start_rest", placed, 3 * len(placed), _gather_half_copies(0, len(placed), True))
    relayed = {}

    def fetch(w, stage, after):
        if stage == "first":
            w_in = _relay_copies("gather_first_relay", first[2][1:2], first[0], first[1], after,
                                 _gather_half_copies(1, 1, True), 3, _gather_half_copies(0, 1, False))
            got = _wait_copies("gather_wait_small", first[2][:1], first[0], first[1], w_in[3], _gather_copies(0, 1))
            got += _wait_copies("gather_wait_first", w_in[2], w_in[0], w_in[1], got[0], _gather_half_copies(0, 1, False))
        elif stage == "mixer_out":
            got = _wait_copies("gather_wait_mixer_out", first[2][2:], first[0], first[1], after, _gather_copies(2, 1))
        elif stage.endswith("_relay"):
            lo, hi = spans[stage[:-6]]
            relayed[stage[:-6]] = _relay_copies(
                f"gather_{stage}", rest[2][lo:hi], rest[0], rest[1], after,
                _gather_half_copies(lo, hi - lo, True), 3 * (hi - lo), _gather_half_copies(0, hi - lo, False))
            return w
        else:
            lo, hi = spans[stage]
            send_sems, recv_sems, bufs, _ = relayed[stage]
            got = _wait_copies(f"gather_wait_{stage}", bufs, send_sems, recv_sems, after, _gather_half_copies(0, hi - lo, False))
        w = dict(w)
        if stage == "first":
            g_small = got[0].reshape(N_CHIPS, -1)[:, :n_small]
            conv_w = g_small[:, 3 * ds:].reshape(N_CHIPS, 2, 3, fs).transpose(1, 2, 0, 3).reshape(2, 3, D_FF)
            w.update(
                hg_norm=g_small[:, :ds].reshape(1, D_MODEL),
                hg_lb=g_small[:, ds:3 * ds].reshape(N_CHIPS, 2, ds).transpose(1, 0, 2).reshape(2, D_MODEL),
                ffn_conv_w=[conv_w[0], conv_w[1]], hg_w_in=got[1],
            )
        elif stage == "mixer_out":
            w.update(hg_w_out=got[0].reshape(1, D_MODEL, D_MODEL))
        elif stage == "layer0":
            w.update(ffn_w_up=[got[0], None], ffn_w_down=[got[1].reshape(1, D_FF, D_MODEL), None])
        else:
            w.update(
                w_kv=got[0].reshape(1, D_MODEL, 2 * LANES), attn_w_q=got[1].reshape(1, D_MODEL, D_MODEL),
                attn_w_o=got[2].reshape(1, D_MODEL, D_MODEL), ffn_w_up=[w["ffn_w_up"][0], got[3]],
                ffn_w_down=[w["ffn_w_down"][0], got[4].reshape(1, D_FF, D_MODEL)],
            )
        return w

    whole = dict(
        hg_out_norm=hg_out_norm, kv_norm=kv_norm.reshape(1, D_MODEL), attn_norm=attn_norm, attn_sinks=attn_sinks.reshape(ATT_QH),
        ffn_norm=[ffn_norm[0:1], ffn_norm[1:2]], ffn_conv_b=[ffn_conv_b[0:1], ffn_conv_b[1:2]], final_norm=final_norm.reshape(1, D_MODEL),
    )
    whole = fetch(whole, "first", rest[3])

    red, layer1 = {}, {}

    def by_rows(g, rows):
        return g.reshape(N_CHIPS, rows, g.shape[2])

    def hook(point, dh, grads):
        if point == "ffn1":
            red["ffn1"] = _Reduction("ffn1", [by_rows(grads["ffn_w_down"], fs), grads["ffn_w_up"]], place_arr)
            return (red["ffn1"].token,)
        if point == "attn":
            red["ffn1"].to_chips(dh)
            layer1.update(grads)
            return (red["ffn1"].token,)
        if point == "ffn0":
            group = [by_rows(layer1["attn_w_o"], ds), by_rows(layer1["attn_w_q"], ds), by_rows(layer1["w_kv"], ds),
                     by_rows(grads["ffn_w_down"], fs), grads["ffn_w_up"], by_rows(grads["hg_w_out"], ds)]
            red["mid"] = _Reduction("mid", group, place_arr)
            return (red["mid"].token,)
        if point == "hgrn":
            red["ffn1"].to_core(dh)
            red["mid"].to_chips(dh)
            return (red["ffn1"].token, red["mid"].token)
        red["hg"] = _Reduction("hg", [grads["hg_w_in"]], place_arr)
        return (red["hg"].token,)

    loss, dx, grads = _local_step(x[0], loss_target[0], whole, fetch, hook)

    small_names = ["hg_out_norm", "attn_sinks", "kv_norm", "attn_norm", "ffn_norm", "ffn_conv_b", "final_norm", "hg_norm", "hg_lb_logits", "ffn_conv_w"]
    groups = [[loss]] + [grads[n] if isinstance(grads[n], list) else [grads[n]] for n in small_names[:-2]] + [[grads["hg_lb"]], grads["ffn_conv_w"]]
    widths = [None, None, ATT_QH] + [None] * 8
    packed = _pack_small(groups, jnp.reshape(2 * chip + core, (1,)).astype(jnp.int32))
    small_flight = _start_copies("small_start", [packed], N_DEV - 1, _small_copies)
    red["hg"].to_chips(small_flight[3])

    out_g, out_d, out_m, out_v = {}, {}, {}, {}

    def update(name, g2):
        shape = wts[name].shape
        d2, m2, v2 = _adamw(wts[name].reshape(g2.shape), mom1[name].reshape(g2.shape), mom2[name].reshape(g2.shape), g2, name=f"adamw_{name}")
        out_g[name], out_d[name], out_m[name], out_v[name] = g2.reshape(shape), d2.reshape(shape), m2.reshape(shape), v2.reshape(shape)
        return d2

    def update_layer(name, g2, layer, prev):
        res = _adamw_layer(wts[name], mom1[name], mom2[name], g2, layer, prev, name=f"adamw_{name}{layer}")
        out_g[name], out_d[name], out_m[name], out_v[name] = res
        return res

    g_down1, g_up1 = red["ffn1"].finish(red["hg"].token)
    up1 = update_layer("ffn_w_up", g_up1, 1, None)
    summed = _sum_small(_wait_copies("small_wait", small_flight[2], small_flight[0], small_flight[1], up1[3], _small_copies)[0], groups, widths)
    loss_out = summed[0][0, 0]
    small_grads = dict(zip(small_names, summed[1:]))
    small_grads["hg_norm"] = lax.dynamic_slice(small_grads["hg_norm"], (0, chip * ds), (1, ds))
    small_grads["hg_lb_logits"] = lax.dynamic_slice(small_grads["hg_lb_logits"], (0, chip * ds), (2, ds))
    small_grads["ffn_conv_w"] = lax.dynamic_slice(small_grads["ffn_conv_w"], (0, chip * fs), (2 * 3, fs))
    red["mid"].to_core(up1[1])
    down1 = update_layer("ffn_w_down", g_down1, 1, None)
    g_o, g_q, g_kv, g_down0, g_up0, g_out = red["mid"].finish(down1[1])
    update("attn_w_o", g_o)
    update("attn_w_q", g_q)
    update("w_kv", g_kv)
    update("hg_w_out", g_out)
    update_layer("ffn_w_down", g_down0, 0, down1)
    last = update_layer("ffn_w_up", g_up0, 0, up1)
    red["hg"].to_core(last[1])
    (g_in,) = red["hg"].finish(last[2])
    update("hg_w_in", g_in)

    as_2d = lambda a, n: a.reshape(small_grads[n].shape)
    updated = _adamw_small([(as_2d(wts[n], n), as_2d(mom1[n], n), as_2d(mom2[n], n), small_grads[n]) for n in small_names])
    for n, (d2, m2, v2) in zip(small_names, updated):
        shape = wts[n].shape
        out_g[n], out_d[n], out_m[n], out_v[n] = small_grads[n].reshape(shape), d2.reshape(shape), m2.reshape(shape), v2.reshape(shape)

    grad_x = dx.reshape(x.shape)
    return (loss_out, grad_x, *[out_g[n] for n in names], *[out_d[n] for n in names], *[out_m[n] for n in names], *[out_v[n] for n in names])
```

```python
import functools

import jax
import jax.numpy as jnp
from jax import lax
from jax.experimental import pallas as pl
from jax.experimental.pallas import tpu as pltpu

F32 = jnp.float32
BF16 = jnp.bfloat16
MESH = pl.DeviceIdType.MESH

EPS = 1e-6
D_MODEL = 1024
HG_HEADS = 8
HG_DK = 128
HG_CHUNK = 64
ATT_HD = 64
ATT_QH = 16
ATT_KVH = 2
ATT_GROUP = ATT_QH // ATT_KVH
WINDOW = 128
D_FF = 2816
N_CHIPS = 4
N_DEV = 8
LANES = 128
SUBLANES = 8
VMEM_LIMIT_BYTES = 56 * 1024 * 1024
NEG = -1e30
ALIBI_SLOPES = tuple(2.0 ** (-8.0 * h / ATT_QH) for h in range(1, ATT_QH + 1))

ADAM_LR = 0.001
ADAM_B1 = 0.9
ADAM_B2 = 0.999
ADAM_EPS = 1e-08
ADAM_WD = 0.01
ADAM_STEP = 10


def _cparams(sem=None):
    return pltpu.CompilerParams(dimension_semantics=sem, vmem_limit_bytes=VMEM_LIMIT_BYTES)


def _pick(n, cands):
    for c in cands:
        if n % c == 0:
            return c
    return n


def _sigmoid(x):
    return 0.5 * jnp.tanh(0.5 * x) + 0.5


def _dot(a, b, dims):
    return lax.dot_general(a, b, (dims, ((), ())), preferred_element_type=F32)


NN = ((1,), (0,))
NT = ((1,), (1,))
TN = ((0,), (0,))


MM_ROWS = 1024


def _rms_stats(xv):
    rstd = lax.rsqrt(jnp.mean(xv * xv, axis=-1, keepdims=True) + EPS)
    return xv * rstd, rstd


def _mm_operand(a_ref, gain_ref):
    if gain_ref is None:
        return a_ref[...].astype(BF16)
    return (_rms_stats(a_ref[...])[0] * gain_ref[...]).astype(BF16)


def _mm_nn(a, w, res=None, out_dtype=F32, name="mm_nn", gain=None):
    m, k = a.shape
    s, _, ns = w.shape
    tm = min(m, MM_ROWS)
    tn = _pick(ns, (1024, 1408, 512, 256, 128))
    npb = ns // tn

    def body(a_ref, w_ref, *rest):
        o_ref = rest[-1]
        acc = _dot(_mm_operand(a_ref, rest[0] if gain is not None else None), w_ref[...], NN)
        if res is not None:
            acc = acc + rest[-2][...]
        o_ref[...] = acc.astype(o_ref.dtype)

    in_specs = [
        pl.BlockSpec((tm, k), lambda i, j: (i, 0)),
        pl.BlockSpec((None, k, tn), lambda i, j: (j // npb, 0, j % npb)),
    ]
    args = [a, w]
    if gain is not None:
        in_specs.append(pl.BlockSpec((1, k), lambda i, j: (0, 0)))
        args.append(gain)
    if res is not None:
        in_specs.append(pl.BlockSpec((tm, tn), lambda i, j: (i, j)))
        args.append(res)
    return pl.pallas_call(
        body,
        name=name,
        grid=(m // tm, s * npb),
        in_specs=in_specs,
        out_specs=pl.BlockSpec((tm, tn), lambda i, j: (i, j)),
        out_shape=jax.ShapeDtypeStruct((m, s * ns), out_dtype),
        compiler_params=_cparams(("parallel", "parallel")),
    )(*args)


def _dy_spec(stacked, tm, tn, npb, row, kk):
    if stacked:
        return pl.BlockSpec((None, tm, tn), lambda *g: (kk(g) // npb, row(g), kk(g) % npb))
    return pl.BlockSpec((tm, tn), lambda *g: (row(g), kk(g)))


def _dep_specs(deps):
    return [pl.BlockSpec(d.shape, lambda *g: (0, 0)) for d in deps]


def _mm_nt(dy, w, stacked=False, out_dtype=F32, name="mm_nt", deps=(), norm_of=None, also=None):
    s, k, ns = w.shape
    m = dy.shape[1] if stacked else dy.shape[0]
    tm = min(m, MM_ROWS)
    tko = _pick(k, (1024, 1408, 512, 256))
    tn = _pick(ns, (1024, 1408, 512, 256))
    npb = ns // tn
    nk = s * npb
    fused = norm_of is not None
    assert not fused or tko == k
    assert also is None or fused

    def body(dy_ref, w_ref, *rest):
        acc_ref = rest[-1]
        i, kk = pl.program_id(0), pl.program_id(2)

        @pl.when(kk == 0)
        def _():
            acc_ref[...] = jnp.zeros_like(acc_ref)

        acc_ref[...] += _dot(dy_ref[...].astype(BF16), w_ref[...], NT)

        if not fused:
            @pl.when(kk == nk - 1)
            def _():
                rest[-2][...] = acc_ref[...].astype(rest[-2].dtype)
            return
        x_ref, g_ref, dres_ref = rest[:3]
        n_out = 3 if also is not None else 2
        dx_ref, dg_refs = rest[-1 - n_out], rest[-n_out:-1]

        @pl.when(jnp.logical_and(i == 0, kk == 0))
        def _():
            for dg_ref in dg_refs:
                dg_ref[...] = jnp.zeros_like(dg_ref)

        @pl.when(kk == nk - 1)
        def _():
            dxn = acc_ref[...]
            xhat, rstd = _rms_stats(x_ref[...])
            gd = dxn * g_ref[...]
            dg_refs[0][...] += jnp.sum(dxn * xhat, axis=0, keepdims=True)
            if also is not None:
                dy2_ref, w2_ref, g2_ref = rest[3:6]
                dxn2 = _dot(dy2_ref[...].astype(BF16), w2_ref[...], NT)
                gd = gd + dxn2 * g2_ref[...]
                dg_refs[1][...] += jnp.sum(dxn2 * xhat, axis=0, keepdims=True)
            dx_ref[...] = dres_ref[...] + rstd * (gd - xhat * jnp.mean(gd * xhat, axis=-1, keepdims=True))

    row = pl.BlockSpec((tm, tko), lambda i, j, kk: (i, j))
    vec = pl.BlockSpec((1, k), lambda i, j, kk: (0, 0))
    extra_in, extra_args = [], ()
    if fused:
        extra_in, extra_args = [row, vec, row], tuple(norm_of)
    if also is not None:
        n2 = also[0].shape[1]
        extra_in += [pl.BlockSpec((tm, n2), lambda i, j, kk: (i, 0)), pl.BlockSpec((None, k, n2), lambda i, j, kk: (0, 0, 0)), vec]
        extra_args += tuple(also)
    f32 = lambda shape: jax.ShapeDtypeStruct(shape, F32)
    return pl.pallas_call(
        body,
        name=name,
        grid=(m // tm, k // tko, nk),
        in_specs=[
            _dy_spec(stacked, tm, tn, npb, lambda g: g[0], lambda g: g[2]),
            pl.BlockSpec((None, tko, tn), lambda i, j, kk: (kk // npb, j, kk % npb)),
        ] + extra_in + _dep_specs(deps),
        out_specs=([row, vec] + ([vec] if also is not None else [])) if fused else row,
        out_shape=([f32((m, k)), f32((1, k))] + ([f32((1, k))] if also is not None else [])) if fused else jax.ShapeDtypeStruct((m, k), out_dtype),
        scratch_shapes=[pltpu.VMEM((tm, tko), F32)],
        compiler_params=_cparams(("arbitrary",) * 3 if fused else ("parallel", "parallel", "arbitrary")),
    )(dy, w, *extra_args, *deps)


def _mm_tn(a, dy, s, ns, stacked=False, name="mm_tn", deps=(), gain=None):
    m, k = a.shape
    tm = min(m, MM_ROWS)
    tk = _pick(k, (1024, 1408, 512, 256))
    tn = _pick(ns, (1024, 1408, 512, 256, 128))
    npb = ns // tn
    nm = m // tm
    assert gain is None or tk == k

    def body(a_ref, dy_ref, *rest):
        j, mm = pl.program_id(1), pl.program_id(2)
        if gain is None:
            o_ref, acc_ref = rest[-2:]
            lhs = a_ref[...].astype(BF16)
        else:
            o_ref, acc_ref, xn_ref = rest[-3:]

            @pl.when(j == 0)
            def _():
                xn_ref[mm] = _mm_operand(a_ref, rest[0])

            lhs = xn_ref[mm]

        @pl.when(mm == 0)
        def _():
            acc_ref[...] = jnp.zeros_like(acc_ref)

        acc_ref[...] += _dot(lhs, dy_ref[...].astype(BF16), TN)

        @pl.when(mm == nm - 1)
        def _():
            o_ref[...] = acc_ref[...]

    a_rows = (lambda i, j, mm: (mm, i)) if gain is None else (lambda i, j, mm: (jnp.where(j == 0, mm, 0), i))
    return pl.pallas_call(
        body,
        name=name,
        grid=(k // tk, s * npb, nm),
        in_specs=[
            pl.BlockSpec((tm, tk), a_rows),
            _dy_spec(stacked, tm, tn, npb, lambda g: g[2], lambda g: g[1]),
        ] + ([pl.BlockSpec((1, k), lambda i, j, mm: (0, 0))] if gain is not None else []) + _dep_specs(deps),
        out_specs=pl.BlockSpec((None, tk, tn), lambda i, j, mm: (j // npb, i, j % npb)),
        out_shape=jax.ShapeDtypeStruct((s, k, ns), F32),
        scratch_shapes=[pltpu.VMEM((tk, tn), F32)] + ([pltpu.VMEM((nm, tm, tk), BF16)] if gain is not None else []),
        compiler_params=_cparams(("parallel", "arbitrary", "arbitrary") if gain is not None else ("parallel", "parallel", "arbitrary")),
    )(a, dy, *(() if gain is None else (gain,)), *deps)


ROW_TILE = 512


def _loss_head(h, g, target):
    t, d = h.shape
    r = min(t, ROW_TILE)

    def body(h_ref, g_ref, t_ref, dh_ref, dg_ref, loss_ref):
        @pl.when(pl.program_id(0) == 0)
        def _():
            dg_ref[...] = jnp.zeros_like(dg_ref)
            loss_ref[...] = jnp.zeros_like(loss_ref)

        xv = h_ref[...]
        rstd = lax.rsqrt(jnp.mean(xv * xv, axis=-1, keepdims=True) + EPS)
        xhat = xv * rstd
        gv = g_ref[...]
        err = xhat * gv - t_ref[...]
        loss_ref[...] += 0.5 * jnp.sum(jnp.mean(err * err, axis=-1, keepdims=True), axis=0, keepdims=True)
        dy = err * (1.0 / d)
        gd = dy * gv
        dh_ref[...] = rstd * (gd - xhat * jnp.mean(gd * xhat, axis=-1, keepdims=True))
        dg_ref[...] += jnp.sum(dy * xhat, axis=0, keepdims=True)

    return pl.pallas_call(
        body,
        name="loss_head",
        grid=(t // r,),
        in_specs=[
            pl.BlockSpec((r, d), lambda i: (i, 0)),
            pl.BlockSpec((1, d), lambda i: (0, 0)),
            pl.BlockSpec((r, d), lambda i: (i, 0)),
        ],
        out_specs=[
            pl.BlockSpec((r, d), lambda i: (i, 0)),
            pl.BlockSpec((1, d), lambda i: (0, 0)),
            pl.BlockSpec((1, LANES), lambda i: (0, 0)),
        ],
        out_shape=[
            jax.ShapeDtypeStruct((t, d), F32),
            jax.ShapeDtypeStruct((1, d), F32),
            jax.ShapeDtypeStruct((1, LANES), F32),
        ],
        compiler_params=_cparams(("arbitrary",)),
    )(h, g, target)


CONV_ROWS = 512
CONV_COLS = 1408
BF16_ROWS = 16


def _conv_taps(x_ext, n):
    tot = x_ext.shape[0]
    g1 = pltpu.roll(x_ext, 1, 0)[tot - n:]
    g2 = pltpu.roll(x_ext, 2, 0)[tot - n:]
    return g2, g1


def _conv_fwd(up, conv_w, conv_b, name="conv_fwd"):
    t = up.shape[0]
    r = min(t, CONV_ROWS)
    tc = CONV_COLS
    ncb = D_FF // tc
    hb = r // BF16_ROWS

    def body(g_ref, halo_ref, v_ref, w_ref, b_ref, o_ref, c_ref):
        i = pl.program_id(1)
        g0 = g_ref[...].astype(F32)
        halo = halo_ref[...].astype(F32)[BF16_ROWS - SUBLANES:] * jnp.where(i > 0, 1.0, 0.0)
        g2, g1 = _conv_taps(jnp.concatenate([halo, g0], axis=0), r)
        c = b_ref[...] + w_ref[0:1, :] * g2 + w_ref[1:2, :] * g1 + w_ref[2:3, :] * g0
        c_ref[...] = c.astype(BF16)
        o_ref[...] = (c * _sigmoid(c) * v_ref[...].astype(F32)).astype(BF16)

    blk = pl.BlockSpec((r, tc), lambda j, i: (i, j))
    return pl.pallas_call(
        body,
        name=name,
        grid=(ncb, t // r),
        in_specs=[
            blk,
            pl.BlockSpec((BF16_ROWS, tc), lambda j, i: (jnp.maximum(i * hb - 1, 0), j)),
            pl.BlockSpec((r, tc), lambda j, i: (i, ncb + j)),
            pl.BlockSpec((3, tc), lambda j, i: (0, j)),
            pl.BlockSpec((1, tc), lambda j, i: (0, j)),
        ],
        out_specs=[blk, blk],
        out_shape=[jax.ShapeDtypeStruct((t, D_FF), BF16), jax.ShapeDtypeStruct((t, D_FF), BF16)],
        compiler_params=_cparams(("parallel", "parallel")),
    )(up, up, up, conv_w, conv_b)


def _conv_bwd(up, conv_w, c, dact, name="conv_bwd"):
    t = up.shape[0]
    r = min(t, CONV_ROWS)
    tc = CONV_COLS
    ncb = D_FF // tc
    nrt = t // r

    def body(g_ref, v_ref, w_ref, c_ref, da_ref, dup_ref, dw_ref, db_ref, nxt_ref):
        ii = pl.program_id(1)

        @pl.when(ii == 0)
        def _():
            nxt_ref[...] = jnp.zeros_like(nxt_ref)
            dw_ref[...] = jnp.zeros_like(dw_ref)
            db_ref[...] = jnp.zeros_like(db_ref)

        g0 = g_ref[...].astype(F32)
        w0, w1, w2 = w_ref[0:1, :], w_ref[1:2, :], w_ref[2:3, :]
        c = c_ref[...].astype(F32)
        sg = _sigmoid(c)
        da = da_ref[...].astype(F32)
        dup_ref[1] = (da * (c * sg)).astype(BF16)
        dc = da * v_ref[...].astype(F32) * (sg * (1.0 + c * (1.0 - sg)))
        ext = jnp.concatenate([dc, nxt_ref[...]], axis=0)
        tot = r + SUBLANES
        d1 = pltpu.roll(ext, tot - 1, 0)[:r]
        d2 = pltpu.roll(ext, tot - 2, 0)[:r]
        nxt_ref[...] = dc[:SUBLANES]
        dup_ref[0] = (w2 * dc + w1 * d1 + w0 * d2).astype(BF16)
        db_ref[...] += jnp.sum(dc, axis=0, keepdims=True)
        dw_ref[0:1, :] += jnp.sum(d2 * g0, axis=0, keepdims=True)
        dw_ref[1:2, :] += jnp.sum(d1 * g0, axis=0, keepdims=True)
        dw_ref[2:3, :] += jnp.sum(dc * g0, axis=0, keepdims=True)

    rev = lambda ii: nrt - 1 - ii
    dup, dw, db = pl.pallas_call(
        body,
        name=name,
        grid=(ncb, nrt),
        in_specs=[
            pl.BlockSpec((r, tc), lambda j, ii: (rev(ii), j)),
            pl.BlockSpec((r, tc), lambda j, ii: (rev(ii), ncb + j)),
            pl.BlockSpec((3, tc), lambda j, ii: (0, j)),
            pl.BlockSpec((r, tc), lambda j, ii: (rev(ii), j)),
            pl.BlockSpec((r, tc), lambda j, ii: (rev(ii), j)),
        ],
        out_specs=[
            pl.BlockSpec((2, None, r, tc), lambda j, ii: (0, j, rev(ii), 0)),
            pl.BlockSpec((3, tc), lambda j, ii: (0, j)),
            pl.BlockSpec((1, tc), lambda j, ii: (0, j)),
        ],
        out_shape=[
            jax.ShapeDtypeStruct((2, ncb, t, tc), BF16),
            jax.ShapeDtypeStruct((3, D_FF), F32),
            jax.ShapeDtypeStruct((1, D_FF), F32),
        ],
        scratch_shapes=[pltpu.VMEM((SUBLANES, tc), F32)],
        compiler_params=_cparams(("parallel", "arbitrary")),
    )(up, up, conv_w, c, dact)
    return dup.reshape(2 * ncb, t, tc), dw, db


def _split3(x):
    x1 = x.astype(BF16)
    r1 = x - x1.astype(F32)
    x2 = r1.astype(BF16)
    x3 = (r1 - x2.astype(F32)).astype(BF16)
    return x1, x2, x3


def _tri_dot(tri, x, dims):
    x1, x2, x3 = _split3(x)
    return _dot(tri, x1, dims) + _dot(tri, x2, dims) + _dot(tri, x3, dims)


def _lower_bound(logits_ref):
    return _sigmoid(logits_ref[0:1, :] - logits_ref[1:2, :])


def _hg_gates(qr, fr, lb):
    q = qr * _sigmoid(qr) * (HG_DK ** -0.5)
    sf = _sigmoid(fr)
    fg = lb + (1.0 - lb) * sf
    return q, sf, fg


def _hg_chunk_terms(q, fg, tril_b, low_half):
    g = jnp.log(fg)
    k = 1.0 - fg
    cum = _tri_dot(tril_b, g, NN)
    c_last = jnp.sum(g, axis=0, keepdims=True)
    c_mid = jnp.sum(jnp.where(low_half, g, 0.0), axis=0, keepdims=True)
    e_q = jnp.exp(cum - c_mid)
    e_k = jnp.exp(c_mid - cum)
    e_0 = jnp.exp(cum)
    e_l = jnp.exp(c_last - cum)
    return k, e_q, e_k, e_0, e_l, jnp.exp(c_last)


HG_BLOCK = 256


def _hg_proj_specs(rb, row):
    return [pl.BlockSpec((rb, D_MODEL), functools.partial(lambda i, k: (row(i), k), k=k)) for k in range(4)]


def _hg_consts(c):
    tril = lax.broadcasted_iota(jnp.int32, (c, c), 0) >= lax.broadcasted_iota(jnp.int32, (c, c), 1)
    low_half = lax.broadcasted_iota(jnp.int32, (c, D_MODEL), 0) < c // 2
    return tril, tril.astype(BF16), low_half


def _hgrn_fwd(proj, lb, wn):
    t = proj.shape[0]
    c = HG_CHUNK
    rb = min(t, HG_BLOCK)
    cpb = rb // c

    def body(q_ref, f_ref, i_ref, g_ref, lb_ref, wn_ref, o_ref, y_ref, st_ref, s_scr):
        @pl.when(pl.program_id(0) == 0)
        def _():
            s_scr[...] = jnp.zeros_like(s_scr)

        lb_all = _lower_bound(lb_ref)
        wnv = wn_ref[...]
        tril, tril_b, low_half = _hg_consts(c)

        def chunk(n, carry):
            rows = pl.ds(pl.multiple_of(n * c, c), c)
            q, _, fg = _hg_gates(q_ref[rows, :], f_ref[rows, :], lb_all)
            k, e_q, e_k, e_0, e_l, e_last = _hg_chunk_terms(q, fg, tril_b, low_half)
            qi, ki, q0, kl = (q * e_q).astype(BF16), (k * e_k).astype(BF16), (q * e_0).astype(BF16), (k * e_l).astype(BF16)
            v = i_ref[rows, :].astype(BF16)
            gr = g_ref[rows, :]
            gate = gr * _sigmoid(gr)
            for h in range(HG_HEADS):
                cols = slice(h * HG_DK, (h + 1) * HG_DK)
                st = s_scr[h]
                st_ref[h, n] = st
                a = jnp.where(tril, _dot(qi[:, cols], ki[:, cols], NT), 0.0)
                o = _dot(q0[:, cols], st.astype(BF16), NT) + _dot(a.astype(BF16), v[:, cols], NN)
                s_scr[h] = st * e_last[:, cols] + _dot(v[:, cols], kl[:, cols], TN)
                o_ref[rows, cols] = o
                rstd = lax.rsqrt(jnp.mean(o * o, axis=-1, keepdims=True) + EPS)
                y_ref[rows, cols] = (o * rstd * wnv * gate[:, cols]).astype(BF16)
            return carry

        lax.fori_loop(0, cpb, chunk, 0, unroll=2)

    blk = pl.BlockSpec((rb, D_MODEL), lambda i: (i, 0))
    return pl.pallas_call(
        body,
        name="hgrn_fwd",
        grid=(t // rb,),
        in_specs=_hg_proj_specs(rb, lambda i: i) + [pl.BlockSpec((2, D_MODEL), lambda i: (0, 0)), pl.BlockSpec((1, HG_DK), lambda i: (0, 0))],
        out_specs=[blk, blk, pl.BlockSpec((HG_HEADS, cpb, HG_DK, HG_DK), lambda i: (0, i, 0, 0))],
        out_shape=[
            jax.ShapeDtypeStruct((t, D_MODEL), F32),
            jax.ShapeDtypeStruct((t, D_MODEL), BF16),
            jax.ShapeDtypeStruct((HG_HEADS, t // c, HG_DK, HG_DK), F32),
        ],
        scratch_shapes=[pltpu.VMEM((HG_HEADS, HG_DK, HG_DK), F32)],
        compiler_params=_cparams(("arbitrary",)),
    )(proj, proj, proj, proj, lb, wn)


def _hgrn_bwd(proj, lb, wn, o, states, dy):
    t = proj.shape[0]
    c = HG_CHUNK
    rb = min(t, HG_BLOCK)
    cpb = rb // c
    nb = t // rb

    def body(q_ref, f_ref, i_ref, g_ref, lb_ref, wn_ref, o_ref, st_ref, dy_ref, dp_ref, dl_ref, dwn_ref, ds_scr, dlb_scr):
        step = pl.program_id(0)

        @pl.when(step == 0)
        def _():
            dwn_ref[...] = jnp.zeros_like(dwn_ref)
            ds_scr[...] = jnp.zeros_like(ds_scr)
            dlb_scr[...] = jnp.zeros_like(dlb_scr)

        lb_all = _lower_bound(lb_ref)
        wnv = wn_ref[...]
        tril, tril_b, low_half = _hg_consts(c)

        def chunk(nn, carry):
            n = cpb - 1 - nn
            rows = pl.ds(pl.multiple_of(n * c, c), c)
            qr = q_ref[rows, :]
            gr = g_ref[rows, :]
            q, sf, fg = _hg_gates(qr, f_ref[rows, :], lb_all)
            k, e_q, e_k, e_0, e_l, e_last = _hg_chunk_terms(q, fg, tril_b, low_half)
            qi, qi_lo, _ = _split3(q * e_q)
            ki, ki_lo, _ = _split3(k * e_k)
            q0 = (q * e_0).astype(BF16)
            kl = (k * e_l).astype(BF16)
            v = i_ref[rows, :].astype(BF16)
            sg = _sigmoid(gr)
            silu_g = gr * sg
            dsilu_g = sg * (1.0 + gr * (1.0 - sg))
            dqs, dks, d_lasts = [], [], []
            for h in range(HG_HEADS):
                cols = slice(h * HG_DK, (h + 1) * HG_DK)
                ov = o_ref[rows, cols]
                dyv = dy_ref[rows, cols].astype(F32)
                rstd = lax.rsqrt(jnp.mean(ov * ov, axis=-1, keepdims=True) + EPS)
                ohat = ov * rstd
                dp_ref[3, rows, cols] = (dyv * (ohat * wnv) * dsilu_g[:, cols]).astype(BF16)
                don = dyv * silu_g[:, cols]
                dwn_ref[...] += jnp.sum(don * ohat, axis=0, keepdims=True)
                gd = don * wnv
                do_b = (rstd * (gd - ohat * jnp.mean(gd * ohat, axis=-1, keepdims=True))).astype(BF16)
                st = st_ref[h, n]
                ds = ds_scr[h]
                ds_b = ds.astype(BF16)
                vh, kh = v[:, cols], k[:, cols]
                a_b = jnp.where(tril, _dot(qi[:, cols], ki[:, cols], NT), 0.0).astype(BF16)
                da_b = jnp.where(tril, _dot(do_b, vh, NT), 0.0).astype(BF16)
                dqs.append(_dot(do_b, st.astype(BF16), NN) * e_0[:, cols]
                           + (_dot(da_b, ki[:, cols], NN) + _dot(da_b, ki_lo[:, cols], NN)) * e_q[:, cols])
                dk_state = _dot(vh, ds_b, NN) * e_l[:, cols]
                dks.append((_dot(da_b, qi[:, cols], TN) + _dot(da_b, qi_lo[:, cols], TN)) * e_k[:, cols] + dk_state)
                dp_ref[2, rows, cols] = (_dot(a_b, do_b, TN) + _dot(kl[:, cols], ds_b, NT)).astype(BF16)
                ds_scr[h] = ds * e_last[:, cols] + _dot(do_b, q0[:, cols], TN)
                d_lasts.append(jnp.sum(dk_state * kh, axis=0, keepdims=True) + jnp.sum(ds * st, axis=0, keepdims=True) * e_last[:, cols])
            dq = jnp.concatenate(dqs, axis=1)
            dk = jnp.concatenate(dks, axis=1)
            dlogf = _tri_dot(tril_b, q * dq - k * dk, TN) + jnp.concatenate(d_lasts, axis=1)
            dfg = dlogf / fg - dk
            dlb_scr[...] += jnp.sum(dfg * (1.0 - sf), axis=0, keepdims=True)
            sq = _sigmoid(qr)
            dp_ref[0, rows, :] = (dq * (HG_DK ** -0.5) * (sq * (1.0 + qr * (1.0 - sq)))).astype(BF16)
            dp_ref[1, rows, :] = (dfg * (1.0 - lb_all) * sf * (1.0 - sf)).astype(BF16)
            return carry

        lax.fori_loop(0, cpb, chunk, 0, unroll=2)

        @pl.when(step == nb - 1)
        def _():
            d0 = dlb_scr[...] * lb_all * (1.0 - lb_all)
            dl_ref[0:1, :] = d0
            dl_ref[1:2, :] = -d0

    rev = lambda i: nb - 1 - i
    blk = pl.BlockSpec((rb, D_MODEL), lambda i: (rev(i), 0))
    return pl.pallas_call(
        body,
        name="hgrn_bwd",
        grid=(nb,),
        in_specs=_hg_proj_specs(rb, rev)
        + [pl.BlockSpec((2, D_MODEL), lambda i: (0, 0)), pl.BlockSpec((1, HG_DK), lambda i: (0, 0)), blk,
           pl.BlockSpec((HG_HEADS, cpb, HG_DK, HG_DK), lambda i: (0, rev(i), 0, 0)), blk],
        out_specs=[
            pl.BlockSpec((4, rb, D_MODEL), lambda i: (0, rev(i), 0)),
            pl.BlockSpec((2, D_MODEL), lambda i: (0, 0)),
            pl.BlockSpec((1, HG_DK), lambda i: (0, 0)),
        ],
        out_shape=[
            jax.ShapeDtypeStruct((4, t, D_MODEL), BF16),
            jax.ShapeDtypeStruct((2, D_MODEL), F32),
            jax.ShapeDtypeStruct((1, HG_DK), F32),
        ],
        scratch_shapes=[pltpu.VMEM((HG_HEADS, HG_DK, HG_DK), F32), pltpu.VMEM((1, D_MODEL), F32)],
        compiler_params=_cparams(("arbitrary",)),
    )(proj, proj, proj, proj, lb, wn, o, states, dy)


ATT_STACK = 8


def _att_stack(q_ref, sink_ref, first, lo, bias_p, bias_c, extra_ref=None):
    qs, bps, bcs, sinks, extras = [], [], [], None, []
    rows = lax.broadcasted_iota(jnp.int32, (ATT_STACK * WINDOW, 1), 0)
    for i in range(ATT_STACK):
        hq = first + i
        cols = slice((hq // 2) * LANES, (hq // 2 + 1) * LANES)
        sel = lo if hq % 2 == 0 else jnp.logical_not(lo)
        qp = q_ref[:, cols] * (ATT_HD ** -0.5)
        qs.append(jnp.where(sel, qp, jnp.zeros_like(qp)))
        bps.append(ALIBI_SLOPES[hq] * bias_p)
        bcs.append(ALIBI_SLOPES[hq] * bias_c)
        sinks = sink_ref[hq] if sinks is None else jnp.where(rows < i * WINDOW, sinks, sink_ref[hq])
        if extra_ref is not None:
            ep = extra_ref[:, cols]
            extras.append(jnp.where(sel, ep, jnp.zeros_like(ep)))
    cat = lambda parts: jnp.concatenate(parts, axis=0)
    return cat(qs), cat(bps), cat(bcs), sinks, (cat(extras) if extras else None)


def _att_rows(i):
    return slice(i * WINDOW, (i + 1) * WINDOW)


def _att_bias(n):
    tq = lax.broadcasted_iota(jnp.int32, (WINDOW, WINDOW), 0)
    sk = lax.broadcasted_iota(jnp.int32, (WINDOW, WINDOW), 1)
    valid_c = sk <= tq
    valid_p = (sk - tq) > jnp.where(n > 0, 0, WINDOW)
    dist_c = (tq - sk).astype(F32)
    return jnp.where(valid_p, -dist_c - float(WINDOW), NEG), jnp.where(valid_c, -dist_c, NEG)


def _att_halves(x, lo, kh):
    r = pltpu.roll(x, ATT_HD, 1)
    zero = jnp.zeros_like(x)
    if kh == 0:
        return jnp.where(lo, x, r), jnp.where(lo, x, zero), jnp.where(lo, zero, r)
    return jnp.where(lo, r, x), jnp.where(lo, r, zero), jnp.where(lo, zero, x)


def _att_probs(qm, k2p, k2c, bias_p, bias_c, sink):
    sp = _dot(qm, k2p, NT) + bias_p
    sc = _dot(qm, k2c, NT) + bias_c
    m = jnp.maximum(jnp.maximum(jnp.max(sp, axis=-1, keepdims=True), jnp.max(sc, axis=-1, keepdims=True)), sink)
    ep = jnp.exp(sp - m)
    ec = jnp.exp(sc - m)
    es = jnp.exp(sink - m)
    inv = 1.0 / (jnp.sum(ep, axis=-1, keepdims=True) + jnp.sum(ec, axis=-1, keepdims=True) + es)
    return ep * inv, ec * inv, es * inv


def _attn_fwd(q, kv, sinks):
    t = q.shape[0]
    nb = t // WINDOW

    def body(sink_ref, q_ref, kvp_ref, kvc_ref, o_ref):
        n = pl.program_id(0)
        bias_p, bias_c = _att_bias(n)
        lo = lax.broadcasted_iota(jnp.int32, (WINDOW, LANES), 1) < ATT_HD
        for kh in range(ATT_KVH):
            k2p, _, _ = _att_halves(kvp_ref[:, 0:LANES], lo, kh)
            k2c, _, _ = _att_halves(kvc_ref[:, 0:LANES], lo, kh)
            _, vlo_p, vhi_p = _att_halves(kvp_ref[:, LANES:2 * LANES], lo, kh)
            _, vlo_c, vhi_c = _att_halves(kvc_ref[:, LANES:2 * LANES], lo, kh)
            for first in range(kh * ATT_GROUP, (kh + 1) * ATT_GROUP, ATT_STACK):
                qs, bp, bc, sinks, _ = _att_stack(q_ref, sink_ref, first, lo, bias_p, bias_c)
                pp, pc, _ = _att_probs(qs, k2p, k2c, bp, bc, sinks)
                pp, pc = pp.astype(BF16), pc.astype(BF16)
                for i in range(0, ATT_STACK, 2):
                    even, odd = _att_rows(i), _att_rows(i + 1)
                    out = (_dot(pp[even], vlo_p, NN) + _dot(pc[even], vlo_c, NN)
                           + _dot(pp[odd], vhi_p, NN) + _dot(pc[odd], vhi_c, NN))
                    j = (first + i) // 2
                    o_ref[:, j * LANES:(j + 1) * LANES] = out.astype(BF16)

    return pl.pallas_call(
        body,
        name="attn_fwd",
        grid=(nb,),
        in_specs=[
            pl.BlockSpec(memory_space=pltpu.SMEM),
            pl.BlockSpec((WINDOW, D_MODEL), lambda n: (n, 0)),
            pl.BlockSpec((WINDOW, 2 * LANES), lambda n: (jnp.maximum(n - 1, 0), 0)),
            pl.BlockSpec((WINDOW, 2 * LANES), lambda n: (n, 0)),
        ],
        out_specs=pl.BlockSpec((WINDOW, D_MODEL), lambda n: (n, 0)),
        out_shape=jax.ShapeDtypeStruct((t, D_MODEL), BF16),
        compiler_params=_cparams(("parallel",)),
    )(sinks, q, kv, kv)


def _attn_bwd(q, kv, sinks, dout):
    t = q.shape[0]
    nb = t // WINDOW

    def body(sink_ref, q_ref, kvp_ref, kvc_ref, do_ref, dq_ref, dkv_ref, dsink_ref, carry_ref):
        n = pl.program_id(0)

        @pl.when(n == 0)
        def _():
            carry_ref[...] = jnp.zeros_like(carry_ref)
            dsink_ref[...] = jnp.zeros_like(dsink_ref)

        @pl.when(n == nb)
        def _():
            dkv_ref[...] = carry_ref[...].astype(BF16)

        @pl.when(n < nb)
        def _():
            bias_p, bias_c = _att_bias(n)
            lo = lax.broadcasted_iota(jnp.int32, (WINDOW, LANES), 1) < ATT_HD
            lane1 = lax.broadcasted_iota(jnp.int32, (1, LANES), 1)
            dsink = jnp.zeros((1, LANES), F32)
            halves = []
            for kh in range(ATT_KVH):
                k2p, klo_p, khi_p = _att_halves(kvp_ref[:, 0:LANES], lo, kh)
                k2c, klo_c, khi_c = _att_halves(kvc_ref[:, 0:LANES], lo, kh)
                v2p, _, _ = _att_halves(kvp_ref[:, LANES:2 * LANES], lo, kh)
                v2c, _, _ = _att_halves(kvc_ref[:, LANES:2 * LANES], lo, kh)
                acc = [jnp.zeros((WINDOW, LANES), F32) for _ in range(4)]
                for first in range(kh * ATT_GROUP, (kh + 1) * ATT_GROUP, ATT_STACK):
                    qs, bp, bc, sinks, dos = _att_stack(q_ref, sink_ref, first, lo, bias_p, bias_c, do_ref)
                    pp, pc, ps = _att_probs(qs, k2p, k2c, bp, bc, sinks)
                    dpp = _dot(dos, v2p, NT)
                    dpc = _dot(dos, v2c, NT)
                    delta = jnp.sum(pp * dpp, axis=-1, keepdims=True) + jnp.sum(pc * dpc, axis=-1, keepdims=True)
                    dsp = (pp * (dpp - delta)).astype(BF16)
                    dsc = (pc * (dpc - delta)).astype(BF16)
                    sink_term = ps * delta
                    for i in range(ATT_STACK):
                        dsink = dsink + jnp.where(lane1 == first + i, -jnp.sum(sink_term[_att_rows(i)], axis=0, keepdims=True), 0.0)
                    for i in range(0, ATT_STACK, 2):
                        even, odd = _att_rows(i), _att_rows(i + 1)
                        dq_pair = (_dot(dsp[even], klo_p, NN) + _dot(dsc[even], klo_c, NN)
                                   + _dot(dsp[odd], khi_p, NN) + _dot(dsc[odd], khi_c, NN))
                        j = (first + i) // 2
                        dq_ref[:, j * LANES:(j + 1) * LANES] = (dq_pair * (ATT_HD ** -0.5)).astype(BF16)
                    acc[0] = acc[0] + _dot(dsp, qs, TN)
                    acc[1] = acc[1] + _dot(dsc, qs, TN)
                    acc[2] = acc[2] + _dot(pp.astype(BF16), dos, TN)
                    acc[3] = acc[3] + _dot(pc.astype(BF16), dos, TN)
                halves.append([a + pltpu.roll(a, ATT_HD, 1) for a in acc])
            prev = jnp.concatenate(
                [jnp.where(lo, halves[0][0], halves[1][0]), jnp.where(lo, halves[0][2], halves[1][2])], axis=1)
            cur = jnp.concatenate(
                [jnp.where(lo, halves[0][1], halves[1][1]), jnp.where(lo, halves[0][3], halves[1][3])], axis=1)
            dkv_ref[...] = (carry_ref[...] + prev).astype(BF16)
            carry_ref[...] = cur
            dsink_ref[...] += dsink

    blk = lambda n: jnp.minimum(n, nb - 1)
    return pl.pallas_call(
        body,
        name="attn_bwd",
        grid=(nb + 1,),
        in_specs=[
            pl.BlockSpec(memory_space=pltpu.SMEM),
            pl.BlockSpec((WINDOW, D_MODEL), lambda n: (blk(n), 0)),
            pl.BlockSpec((WINDOW, 2 * LANES), lambda n: (jnp.maximum(blk(n) - 1, 0), 0)),
            pl.BlockSpec((WINDOW, 2 * LANES), lambda n: (blk(n), 0)),
            pl.BlockSpec((WINDOW, D_MODEL), lambda n: (blk(n), 0)),
        ],
        out_specs=[
            pl.BlockSpec((WINDOW, D_MODEL), lambda n: (blk(n), 0)),
            pl.BlockSpec((WINDOW, 2 * LANES), lambda n: (jnp.maximum(n - 1, 0), 0)),
            pl.BlockSpec((1, LANES), lambda n: (0, 0)),
        ],
        out_shape=[
            jax.ShapeDtypeStruct((t, D_MODEL), BF16),
            jax.ShapeDtypeStruct((t, 2 * LANES), BF16),
            jax.ShapeDtypeStruct((1, LANES), F32),
        ],
        scratch_shapes=[pltpu.VMEM((WINDOW, 2 * LANES), F32)],
        compiler_params=_cparams(("arbitrary",)),
    )(sinks, q, kv, kv, dout)


def _ffn_fwd(h, norm_g, w_up, conv_w, conv_b, w_down, tag, after_up=lambda up: None):
    up = _mm_nn(h, w_up, gain=norm_g, out_dtype=BF16, name=f"ffn{tag}_up")
    after_up(up)
    act, c = _conv_fwd(up, conv_w, conv_b, name=f"ffn{tag}_conv")
    h_out = _mm_nn(act, w_down, res=h, name=f"ffn{tag}_down")
    return h_out, (up, act, c)


def _ffn_bwd(dh, h, norm_g, w_up, conv_w, conv_b, w_down, saved, tag, deps=()):
    up, act, c = saved
    dw_down = _mm_tn(act, dh, 1, D_MODEL, name=f"ffn{tag}_dwdown", deps=deps)
    dact = _mm_nt(dh, w_down, out_dtype=BF16, name=f"ffn{tag}_dact", deps=deps)
    dup, dconv_w, dconv_b = _conv_bwd(up, conv_w, c, dact, name=f"ffn{tag}_dconv")
    dw_up = _mm_tn(h, dup, N_CHIPS, CONV_COLS, stacked=True, gain=norm_g, name=f"ffn{tag}_dwup")
    dh_in, dnorm = _mm_nt(dup, w_up, stacked=True, norm_of=(h, norm_g, dh), name=f"ffn{tag}_dxn")
    return dh_in, dict(ffn_w_down=dw_down, ffn_w_up=dw_up, ffn_conv_w=dconv_w, ffn_conv_b=dconv_b, ffn_norm=dnorm)


def _local_step(x, target, w, fetch=lambda w, stage, after: w, hook=lambda point, dh, grads: ()):
    proj = _mm_nn(x, w["hg_w_in"], gain=w["hg_norm"], name="hg_in")
    o, y, states = _hgrn_fwd(proj, w["hg_lb"], w["hg_out_norm"])
    w = fetch(w, "mixer_out", y)
    fetch(w, "layer0_relay", y)
    h_a = _mm_nn(y, w["hg_w_out"], res=x, name="hg_out")
    w = fetch(w, "layer0", h_a)
    h1, ffn0 = _ffn_fwd(h_a, w["ffn_norm"][0], w["ffn_w_up"][0], w["ffn_conv_w"][0], w["ffn_conv_b"][0], w["ffn_w_down"][0], 0,
                        lambda up: fetch(w, "layer1_relay", up))
    w = fetch(w, "layer1", h1)
    kv = _mm_nn(h1, w["w_kv"], gain=w["kv_norm"], out_dtype=BF16, name="kv_proj")
    qa = _mm_nn(h1, w["attn_w_q"], gain=w["attn_norm"], out_dtype=BF16, name="attn_q")
    ao = _attn_fwd(qa, kv, w["attn_sinks"])
    h_b = _mm_nn(ao, w["attn_w_o"], res=h1, name="attn_o")
    h2, ffn1 = _ffn_fwd(h_b, w["ffn_norm"][1], w["ffn_w_up"][1], w["ffn_conv_w"][1], w["ffn_conv_b"][1], w["ffn_w_down"][1], 1)
    dh2, d_final, loss = _loss_head(h2, w["final_norm"], target)

    dh_b, g1 = _ffn_bwd(dh2, h_b, w["ffn_norm"][1], w["ffn_w_up"][1], w["ffn_conv_w"][1], w["ffn_conv_b"][1], w["ffn_w_down"][1], ffn1, 1)
    deps = hook("ffn1", dh_b, g1)
    dw_o = _mm_tn(ao, dh_b, 1, D_MODEL, name="attn_dwo", deps=deps)
    dao = _mm_nt(dh_b, w["attn_w_o"], out_dtype=BF16, name="attn_dao", deps=deps)
    dqa, dkv, dsinks = _attn_bwd(qa, kv, w["attn_sinks"], dao)
    dw_q = _mm_tn(h1, dqa, 1, D_MODEL, gain=w["attn_norm"], name="attn_dwq")
    dw_kv = _mm_tn(h1, dkv, 1, 2 * LANES, gain=w["kv_norm"], name="kv_dw")
    dh1, d_attn_norm, d_kv_norm = _mm_nt(dqa, w["attn_w_q"], norm_of=(h1, w["attn_norm"], dh_b),
                                        also=(dkv, w["w_kv"], w["kv_norm"]), name="attn_dxa")
    deps = hook("attn", dh1, dict(attn_w_o=dw_o, attn_w_q=dw_q, w_kv=dw_kv))
    dh_a, g0 = _ffn_bwd(dh1, h_a, w["ffn_norm"][0], w["ffn_w_up"][0], w["ffn_conv_w"][0], w["ffn_conv_b"][0], w["ffn_w_down"][0], ffn0, 0, deps)
    dw_out = _mm_tn(y, dh_a, 1, D_MODEL, name="hg_dwout")
    deps = hook("ffn0", dh_a, dict(g0, hg_w_out=dw_out))
    dy = _mm_nt(dh_a, w["hg_w_out"], out_dtype=BF16, name="hg_dy", deps=deps)
    dproj, dlb, d_out_norm = _hgrn_bwd(proj, w["hg_lb"], w["hg_out_norm"], o, states, dy)
    deps = hook("hgrn", dproj, None)
    dw_in = _mm_tn(x, dproj, N_CHIPS, D_MODEL, stacked=True, gain=w["hg_norm"], name="hg_dwin", deps=deps)
    deps = hook("hg_w", dproj, dict(hg_w_in=dw_in))
    dx, d_hg_norm = _mm_nt(dproj, w["hg_w_in"], stacked=True, norm_of=(x, w["hg_norm"], dh_a), name="hg_dxn", deps=deps)

    grads = dict(
        hg_norm=d_hg_norm, hg_w_in=dw_in, hg_lb=dlb, hg_out_norm=d_out_norm, hg_w_out=dw_out,
        kv_norm=d_kv_norm, w_kv=dw_kv, attn_norm=d_attn_norm, attn_w_q=dw_q, attn_sinks=dsinks, attn_w_o=dw_o,
        final_norm=d_final,
    )
    for name in g0:
        grads[name] = [g0[name], g1[name]]
    return loss, dx, grads


ANY = pl.BlockSpec(memory_space=pl.ANY)


def _place():
    x, y, c = lax.axis_index("x"), lax.axis_index("y"), lax.axis_index("c")
    chips = [(1 - x, y), (x, 1 - y), (1 - x, 1 - y)]
    return x, y, c, chips


def _rcopy(src, dst, send_sem, recv_sem, to):
    return pltpu.make_async_remote_copy(src_ref=src, dst_ref=dst, send_sem=send_sem, recv_sem=recv_sem, device_id=to, device_id_type=MESH)


HBM = pl.BlockSpec(memory_space=pltpu.HBM)
SEM = pl.BlockSpec(memory_space=pltpu.SEMAPHORE)
EFFECT = pltpu.SideEffectType.DATAFLOW_SIDE_EFFECTING


def _in_hbm(a):
    return pltpu.with_memory_space_constraint(a, pltpu.HBM)


def _place_shard(shard, place, dtype, name, deps=(), layer=None):
    r, cols = shard.shape[-2:]
    tr = _pick(r, ELEM_ROWS)
    src = pl.BlockSpec((tr, cols), lambda i, place_ref: (i, 0)) if layer is None else pl.BlockSpec((None, tr, cols), lambda i, place_ref: (layer, i, 0))

    def body(place_ref, s_ref, *rest):
        o_ref = rest[-1]
        o_ref[...] = s_ref[...].astype(o_ref.dtype)

    return pl.pallas_call(
        body,
        name=name,
        grid_spec=pltpu.PrefetchScalarGridSpec(
            num_scalar_prefetch=1,
            grid=(r // tr,),
            in_specs=[src] + _dep_specs(deps),
            out_specs=pl.BlockSpec((None, tr, cols), lambda i, place_ref: (place_ref[0], i, 0)),
        ),
        out_shape=jax.ShapeDtypeStruct((N_CHIPS, r, cols), dtype),
        compiler_params=_cparams(("parallel",)),
    )(place, shard, *deps)


def _start_copies(name, bufs, n_sem, copies):
    n = len(bufs)

    def body(*refs):
        for cp in copies(refs[:n], refs[n], refs[n + 1]):
            cp.start()
        refs[-1][...] = jnp.zeros_like(refs[-1])

    outs = pl.pallas_call(
        body,
        name=name,
        in_specs=[HBM] * n,
        out_specs=[SEM, SEM] + [HBM] * n + [pl.BlockSpec(memory_space=pltpu.VMEM)],
        out_shape=[pltpu.SemaphoreType.DMA((n_sem,)), pltpu.SemaphoreType.DMA((n_sem,))] + [pltpu.HBM(b.shape, b.dtype) for b in bufs]
        + [jax.ShapeDtypeStruct((SUBLANES, LANES), F32)],
        input_output_aliases={i: 2 + i for i in range(n)},
        compiler_params=pltpu.CompilerParams(has_side_effects=EFFECT),
    )(*[_in_hbm(b) for b in bufs])
    return outs[0], outs[1], list(outs[2:-1]), outs[-1]


def _wait_copies(name, bufs, send_sems, recv_sems, after, copies):
    n = len(bufs)

    def body(*refs):
        for cp in copies(refs[:n], refs[n], refs[n + 1]):
            cp.wait_send()
            cp.wait_recv()

    return pl.pallas_call(
        body,
        name=name,
        in_specs=[HBM] * n + [SEM, SEM, ANY],
        out_specs=[HBM] * n,
        out_shape=[pltpu.HBM(b.shape, b.dtype) for b in bufs],
        input_output_aliases={i: i for i in range(n)},
        compiler_params=pltpu.CompilerParams(has_side_effects=EFFECT),
    )(*bufs, send_sems, recv_sems, after)


def _relay_copies(name, bufs, send_sems, recv_sems, after, landed, n_sem, onward):
    n = len(bufs)

    def body(*refs):
        for cp in landed(refs[:n], refs[n], refs[n + 1]):
            cp.wait_send()
            cp.wait_recv()
        for cp in onward(refs[:n], refs[n + 3], refs[n + 4]):
            cp.start()
        refs[-1][...] = jnp.zeros_like(refs[-1])

    outs = pl.pallas_call(
        body,
        name=name,
        in_specs=[HBM] * n + [SEM, SEM, ANY],
        out_specs=[SEM, SEM] + [HBM] * n + [pl.BlockSpec(memory_space=pltpu.VMEM)],
        out_shape=[pltpu.SemaphoreType.DMA((n_sem,)), pltpu.SemaphoreType.DMA((n_sem,))] + [pltpu.HBM(b.shape, b.dtype) for b in bufs]
        + [jax.ShapeDtypeStruct((SUBLANES, LANES), F32)],
        input_output_aliases={i: 2 + i for i in range(n)},
        compiler_params=pltpu.CompilerParams(has_side_effects=EFFECT),
    )(*bufs, send_sems, recv_sems, after)
    return outs[0], outs[1], list(outs[2:-1]), outs[-1]


def _gather_half_copies(first, count, over_ici):
    def copies(refs, send_sems, recv_sems):
        x, y, c, chips = _place()
        out = []
        for i in range(count):
            h = refs[i].shape[1] // 2
            mine = pl.ds(c * h, h)
            for j, (px, py) in enumerate(chips):
                k = 3 * (first + i) + j
                slot = 2 * x + y if over_ici else 2 * px + py
                to = (px, py, c) if over_ici else (x, y, 1 - c)
                out.append(_rcopy(refs[i].at[slot, mine], refs[i].at[slot, mine], send_sems.at[k], recv_sems.at[k], to))
        return out

    return copies


def _gather_copies(first, count):
    def copies(refs, send_sems, recv_sems):
        x, y, c, chips = _place()
        me = 2 * x + y
        out = []
        for i in range(count):
            for j, (px, py) in enumerate(chips):
                k = 3 * (first + i) + j
                out.append(_rcopy(refs[i].at[me], refs[i].at[me], send_sems.at[k], recv_sems.at[k], (px, py, c)))
        return out

    return copies


def _swap_copies(n):
    def copies(refs, send_sems, recv_sems):
        x, y, c, _ = _place()
        out = []
        for i in range(n):
            h = refs[i].shape[1] // 2
            out.append(_rcopy(refs[i].at[:, pl.ds((1 - c) * h, h)], refs[n + i], send_sems.at[i], recv_sems.at[i], (x, y, 1 - c)))
        return out

    return copies


def _partial_copies(n):
    def copies(refs, send_sems, recv_sems):
        x, y, c, chips = _place()
        out = []
        for i in range(n):
            for j, (px, py) in enumerate(chips):
                out.append(_rcopy(refs[i].at[2 * px + py], refs[n + i].at[j], send_sems.at[3 * i + j], recv_sems.at[3 * i + j], (px, py, c)))
        return out

    return copies


def _share_copies(n):
    def copies(refs, send_sems, recv_sems):
        x, y, c, _ = _place()
        return [_rcopy(refs[i].at[c], refs[i].at[c], send_sems.at[i], recv_sems.at[i], (x, y, 1 - c)) for i in range(n)]

    return copies


def _small_layout(groups):
    flat = [a for g in groups for a in g]
    rows = -(-sum(a.shape[0] for a in flat) // SUBLANES) * SUBLANES
    return flat, rows, max(a.shape[1] for a in flat)


def _pack_small(groups, device):
    flat, rows, cols = _small_layout(groups)

    def body(dev_ref, *refs):
        o_ref = refs[-1]
        o_ref[...] = jnp.zeros_like(o_ref)
        r0 = 0
        for a_ref in refs[:-1]:
            r, w = a_ref.shape
            o_ref[r0:r0 + r, 0:w] = a_ref[...]
            r0 += r

    return pl.pallas_call(
        body,
        name="small_pack",
        grid_spec=pltpu.PrefetchScalarGridSpec(
            num_scalar_prefetch=1,
            grid=(1,),
            in_specs=[pl.BlockSpec(a.shape, lambda i, dev_ref: (0, 0)) for a in flat],
            out_specs=pl.BlockSpec((None, rows, cols), lambda i, dev_ref: (dev_ref[0], 0, 0)),
        ),
        out_shape=jax.ShapeDtypeStruct((N_DEV, rows, cols), F32),
        compiler_params=_cparams(("arbitrary",)),
    )(device, *flat)


def _small_copies(refs, send_sems, recv_sems):
    x, y, c, _ = _place()
    me = 4 * x + 2 * y + c
    out = []
    for k in range(1, N_DEV):
        peer = (x ^ (k >> 2), y ^ ((k >> 1) & 1), c ^ (k & 1))
        out.append(_rcopy(refs[0].at[me], refs[0].at[me], send_sems.at[k - 1], recv_sems.at[k - 1], peer))
    return out


def _sum_small(slots, groups, widths):
    out_shapes = [(sum(a.shape[0] for a in g), wd or g[0].shape[1]) for g, wd in zip(groups, widths)]

    def body(s_ref, *refs):
        outs, acc_ref = refs[:-1], refs[-1]
        acc = s_ref[0]
        for d in range(1, N_DEV):
            acc = acc + s_ref[d]
        acc_ref[...] = acc
        r0 = 0
        for o_ref in outs:
            r, w = o_ref.shape
            o_ref[...] = acc_ref[r0:r0 + r, 0:w]
            r0 += r

    vmem = pl.BlockSpec(memory_space=pltpu.VMEM)
    return pl.pallas_call(
        body,
        name="small_sum",
        in_specs=[vmem],
        out_specs=[vmem] * len(groups),
        out_shape=[jax.ShapeDtypeStruct(s, F32) for s in out_shapes],
        scratch_shapes=[pltpu.VMEM(slots.shape[1:], F32)],
        compiler_params=pltpu.CompilerParams(vmem_limit_bytes=VMEM_LIMIT_BYTES),
    )(slots)


def _adamw_small(items):
    n = len(items)

    def body(*refs):
        for i in range(n):
            w_ref, m_ref, v_ref, g_ref = refs[4 * i:4 * i + 4]
            d_ref, nm_ref, nv_ref = refs[4 * n + 3 * i:4 * n + 3 * i + 3]
            d_ref[...], nm_ref[...], nv_ref[...] = _adamw_math(w_ref[...], m_ref[...], v_ref[...], g_ref[...])

    vmem = pl.BlockSpec(memory_space=pltpu.VMEM)
    outs = pl.pallas_call(
        body,
        name="adamw_small",
        in_specs=[vmem] * (4 * n),
        out_specs=[vmem] * (3 * n),
        out_shape=[jax.ShapeDtypeStruct(it[0].shape, F32) for it in items for _ in range(3)],
        compiler_params=pltpu.CompilerParams(vmem_limit_bytes=VMEM_LIMIT_BYTES),
    )(*[a for it in items for a in it])
    return [tuple(outs[3 * i:3 * i + 3]) for i in range(n)]


class _Reduction:
    def __init__(self, tag, grads, place):
        self.tag, self.n, self.place = tag, len(grads), place
        lands = [lax.empty((N_CHIPS, g.shape[1] // 2, g.shape[2]), F32) for g in grads]
        self._start("swap", list(grads) + lands, self.n, _swap_copies(self.n))

    def _start(self, stage, bufs, n_sem, copies):
        *self.flight, self.token = _start_copies(f"rs_{stage}_start_{self.tag}", bufs, n_sem, copies)

    def _landed(self, stage, after, copies):
        send_sems, recv_sems, bufs = self.flight
        return _wait_copies(f"rs_{stage}_wait_{self.tag}", bufs, send_sems, recv_sems, after, copies)

    def to_chips(self, after):
        n = self.n
        bufs = self._landed("swap", after, _swap_copies(n))
        sums = [_add_core_halves(g, o, self.place, name=f"rs_add_core_{self.tag}_{i}") for i, (g, o) in enumerate(zip(bufs[:n], bufs[n:]))]
        self.mine = [f for f, _ in sums]
        parts = [b for _, b in sums]
        lands = [lax.empty((3,) + p.shape[1:], BF16) for p in parts]
        self._start("send", parts + lands, 3 * n, _partial_copies(n))

    def to_core(self, after):
        n = self.n
        bufs = self._landed("send", after, _partial_copies(n))
        halves = [_add_chip_partials(f, o, self.place, name=f"rs_add_chip_{self.tag}_{i}") for i, (f, o) in enumerate(zip(self.mine, bufs[n:]))]
        self._start("share", halves, n, _share_copies(n))

    def finish(self, after):
        return [b.reshape((-1,) + b.shape[2:]) for b in self._landed("share", after, _share_copies(self.n))]


ELEM_ROWS = (512, 352, 256, 176, 128, 64, 32, 16, 8)


def _add_core_halves(grad, got, place, name):
    s, r, cols = grad.shape
    h = r // 2
    tr = _pick(h, ELEM_ROWS)

    def body(place_ref, g_ref, o_ref, f_ref, b_ref):
        acc = g_ref[...] + o_ref[...]
        b_ref[...] = acc.astype(BF16)

        @pl.when(pl.program_id(1) == place_ref[0])
        def _():
            f_ref[...] = acc

    blk = pl.BlockSpec((None, tr, cols), lambda i, k, place_ref: (k, i, 0))
    return pl.pallas_call(
        body,
        name=name,
        grid_spec=pltpu.PrefetchScalarGridSpec(
            num_scalar_prefetch=1,
            grid=(h // tr, s),
            in_specs=[pl.BlockSpec((None, None, tr, cols), lambda i, k, place_ref: (k, place_ref[1], i, 0)), blk],
            out_specs=[pl.BlockSpec((tr, cols), lambda i, k, place_ref: (i, 0)), blk],
        ),
        out_shape=[jax.ShapeDtypeStruct((h, cols), F32), jax.ShapeDtypeStruct((s, h, cols), BF16)],
        compiler_params=_cparams(("parallel", "arbitrary")),
    )(place, grad.reshape(s, 2, h, cols), got)


def _add_chip_partials(mine, got, place, name):
    h, cols = mine.shape
    tr = _pick(h, ELEM_ROWS)

    def body(place_ref, m_ref, g_ref, o_ref):
        acc = m_ref[...]
        for j in range(3):
            acc = acc + g_ref[j].astype(F32)
        o_ref[...] = acc

    return pl.pallas_call(
        body,
        name=name,
        grid_spec=pltpu.PrefetchScalarGridSpec(
            num_scalar_prefetch=1,
            grid=(h // tr,),
            in_specs=[
                pl.BlockSpec((tr, cols), lambda i, place_ref: (i, 0)),
                pl.BlockSpec((3, tr, cols), lambda i, place_ref: (0, i, 0)),
            ],
            out_specs=pl.BlockSpec((None, tr, cols), lambda i, place_ref: (place_ref[1], i, 0)),
        ),
        out_shape=jax.ShapeDtypeStruct((2, h, cols), F32),
        compiler_params=_cparams(("parallel",)),
    )(place, mine, got)


def _adamw_math(w, m, v, g):
    nm = ADAM_B1 * m + (1.0 - ADAM_B1) * g
    nv = ADAM_B2 * v + (1.0 - ADAM_B2) * (g * g)
    m_hat = nm * (1.0 / (1.0 - ADAM_B1 ** ADAM_STEP))
    v_hat = nv * (1.0 / (1.0 - ADAM_B2 ** ADAM_STEP))
    return -ADAM_LR * (m_hat / (jnp.sqrt(v_hat) + ADAM_EPS) + ADAM_WD * w), nm, nv


def _adamw_layer(w, m, v, g, layer, prev, name):
    nl, r, cols = w.shape
    tr = _pick(r, ELEM_ROWS)

    def body(w_ref, m_ref, v_ref, g_ref, *rest):
        go_ref, d_ref, nm_ref, nv_ref = rest[-4:]
        gv = g_ref[...]
        d_ref[...], nm_ref[...], nv_ref[...] = _adamw_math(w_ref[...], m_ref[...], v_ref[...], gv)
        go_ref[...] = gv

    lay = pl.BlockSpec((None, tr, cols), lambda i: (layer, i, 0))
    return pl.pallas_call(
        body,
        name=name,
        grid=(r // tr,),
        in_specs=[lay] * 3 + [pl.BlockSpec((tr, cols), lambda i: (i, 0))] + ([ANY] * 4 if prev else []),
        out_specs=[lay] * 4,
        out_shape=[jax.ShapeDtypeStruct((nl, r, cols), F32)] * 4,
        input_output_aliases={4 + k: k for k in range(4)} if prev else {},
        compiler_params=_cparams(("parallel",)),
    )(w, m, v, g, *(prev or ()))


def _adamw(w, m, v, g, name):
    r, cols = w.shape
    tr = _pick(r, ELEM_ROWS)

    def body(w_ref, m_ref, v_ref, g_ref, d_ref, nm_ref, nv_ref):
        d_ref[...], nm_ref[...], nv_ref[...] = _adamw_math(w_ref[...], m_ref[...], v_ref[...], g_ref[...])

    blk = pl.BlockSpec((tr, cols), lambda i: (i, 0))
    return pl.pallas_call(
        body,
        name=name,
        grid=(r // tr,),
        in_specs=[blk] * 4,
        out_specs=[blk] * 3,
        out_shape=[jax.ShapeDtypeStruct((r, cols), F32)] * 3,
        compiler_params=_cparams(("parallel",)),
    )(w, m, v, g)


SMALL_COLS = 384
SMALL_ROWS = 16


def _pad_rows(flat, rows, cols):
    return jnp.pad(flat, (0, rows * cols - flat.shape[0])).reshape(rows, cols)


def kernel(x, hg_norm, hg_w_in, hg_lb_logits, hg_out_norm, hg_w_out, kv_norm, w_kv, attn_norm, attn_w_q, attn_sinks, attn_w_o, ffn_norm, ffn_w_up, ffn_conv_w, ffn_conv_b, ffn_w_down, final_norm, loss_target, m_hg_norm, m_hg_w_in, m_hg_lb_logits, m_hg_out_norm, m_hg_w_out, m_kv_norm, m_w_kv, m_attn_norm, m_attn_w_q, m_attn_sinks, m_attn_w_o, m_ffn_norm, m_ffn_w_up, m_ffn_conv_w, m_ffn_conv_b, m_ffn_w_down, m_final_norm, v_hg_norm, v_hg_w_in, v_hg_lb_logits, v_hg_out_norm, v_hg_w_out, v_kv_norm, v_w_kv, v_attn_norm, v_attn_w_q, v_attn_sinks, v_attn_w_o, v_ffn_norm, v_ffn_w_up, v_ffn_conv_w, v_ffn_conv_b, v_ffn_w_down, v_final_norm):
    wts = dict(hg_norm=hg_norm, hg_w_in=hg_w_in, hg_lb_logits=hg_lb_logits, hg_out_norm=hg_out_norm, hg_w_out=hg_w_out, kv_norm=kv_norm, w_kv=w_kv, attn_norm=attn_norm, attn_w_q=attn_w_q, attn_sinks=attn_sinks, attn_w_o=attn_w_o, ffn_norm=ffn_norm, ffn_w_up=ffn_w_up, ffn_conv_w=ffn_conv_w, ffn_conv_b=ffn_conv_b, ffn_w_down=ffn_w_down, final_norm=final_norm)
    mom1 = dict(hg_norm=m_hg_norm, hg_w_in=m_hg_w_in, hg_lb_logits=m_hg_lb_logits, hg_out_norm=m_hg_out_norm, hg_w_out=m_hg_w_out, kv_norm=m_kv_norm, w_kv=m_w_kv, attn_norm=m_attn_norm, attn_w_q=m_attn_w_q, attn_sinks=m_attn_sinks, attn_w_o=m_attn_w_o, ffn_norm=m_ffn_norm, ffn_w_up=m_ffn_w_up, ffn_conv_w=m_ffn_conv_w, ffn_conv_b=m_ffn_conv_b, ffn_w_down=m_ffn_w_down, final_norm=m_final_norm)
    mom2 = dict(hg_norm=v_hg_norm, hg_w_in=v_hg_w_in, hg_lb_logits=v_hg_lb_logits, hg_out_norm=v_hg_out_norm, hg_w_out=v_hg_w_out, kv_norm=v_kv_norm, w_kv=v_w_kv, attn_norm=v_attn_norm, attn_w_q=v_attn_w_q, attn_sinks=v_attn_sinks, attn_w_o=v_attn_w_o, ffn_norm=v_ffn_norm, ffn_w_up=v_ffn_w_up, ffn_conv_w=v_ffn_conv_w, ffn_conv_b=v_ffn_conv_b, ffn_w_down=v_ffn_w_down, final_norm=v_final_norm)
    names = list(wts)
    chip = 2 * lax.axis_index("x") + lax.axis_index("y")
    core = lax.axis_index("c")
    fs = D_FF // N_CHIPS
    ds = D_MODEL // N_CHIPS

    place_arr = jnp.stack([chip, core]).astype(jnp.int32)
    small = jnp.concatenate([hg_norm.reshape(-1), hg_lb_logits.reshape(-1), ffn_conv_w.reshape(-1)])
    n_small = small.shape[0]
    shards = [
        ("small", _pad_rows(small, SMALL_ROWS, SMALL_COLS), F32, None), ("hg_w_in", hg_w_in, BF16, 0),
        ("hg_w_out", hg_w_out, BF16, 0), ("ffn_w_up0", ffn_w_up, BF16, 0), ("ffn_w_down0", ffn_w_down, BF16, 0),
        ("w_kv", w_kv, BF16, None), ("attn_w_q", attn_w_q, BF16, 0), ("attn_w_o", attn_w_o, BF16, 0),
        ("ffn_w_up1", ffn_w_up, BF16, 1), ("ffn_w_down1", ffn_w_down, BF16, 1),
    ]
    n_first = 3
    spans = dict(layer0=(0, 2), layer1=(2, 7))

    def first_copies(refs, send_sems, recv_sems):
        return (_gather_copies(0, 1)(refs[:1], send_sems, recv_sems) + _gather_half_copies(1, 1, True)(refs[1:2], send_sems, recv_sems)
                + _gather_copies(2, 1)(refs[2:3], send_sems, recv_sems))

    placed = [_place_shard(s, place_arr, dt, name=f"place_{nm}", layer=ly) for nm, s, dt, ly in shards[:n_first]]
    first = _start_copies("gather_start_first", placed, 3 * n_first, first_copies)
    placed = [_place_shard(s, place_arr, dt, name=f"place_{nm}", deps=(first[3],), layer=ly) for nm, s, dt, ly in shards[n_first:]]
    rest = _start_copies("gather_start_rest", placed, 3 * len(placed), _gather_half_copies(0, len(placed), True))
    relayed = {}

    def fetch(w, stage, after):
        if stage == "first":
            w_in = _relay_copies("gather_first_relay", first[2][1:2], first[0], first[1], after,
                                 _gather_half_copies(1, 1, True), 3, _gather_half_copies(0, 1, False))
            got = _wait_copies("gather_wait_small", first[2][:1], first[0], first[1], w_in[3], _gather_copies(0, 1))
            got += _wait_copies("gather_wait_first", w_in[2], w_in[0], w_in[1], got[0], _gather_half_copies(0, 1, False))
        elif stage == "mixer_out":
            got = _wait_copies("gather_wait_mixer_out", first[2][2:], first[0], first[1], after, _gather_copies(2, 1))
        elif stage.endswith("_relay"):
            lo, hi = spans[stage[:-6]]
            relayed[stage[:-6]] = _relay_copies(
                f"gather_{stage}", rest[2][lo:hi], rest[0], rest[1], after,
                _gather_half_copies(lo, hi - lo, True), 3 * (hi - lo), _gather_half_copies(0, hi - lo, False))
            return w
        else:
            lo, hi = spans[stage]
            send_sems, recv_sems, bufs, _ = relayed[stage]
            got = _wait_copies(f"gather_wait_{stage}", bufs, send_sems, recv_sems, after, _gather_half_copies(0, hi - lo, False))
        w = dict(w)
        if stage == "first":
            g_small = got[0].reshape(N_CHIPS, -1)[:, :n_small]
            conv_w = g_small[:, 3 * ds:].reshape(N_CHIPS, 2, 3, fs).transpose(1, 2, 0, 3).reshape(2, 3, D_FF)
            w.update(
                hg_norm=g_small[:, :ds].reshape(1, D_MODEL),
                hg_lb=g_small[:, ds:3 * ds].reshape(N_CHIPS, 2, ds).transpose(1, 0, 2).reshape(2, D_MODEL),
                ffn_conv_w=[conv_w[0], conv_w[1]], hg_w_in=got[1],
            )
        elif stage == "mixer_out":
            w.update(hg_w_out=got[0].reshape(1, D_MODEL, D_MODEL))
        elif stage == "layer0":
            w.update(ffn_w_up=[got[0], None], ffn_w_down=[got[1].reshape(1, D_FF, D_MODEL), None])
        else:
            w.update(
                w_kv=got[0].reshape(1, D_MODEL, 2 * LANES), attn_w_q=got[1].reshape(1, D_MODEL, D_MODEL),
                attn_w_o=got[2].reshape(1, D_MODEL, D_MODEL), ffn_w_up=[w["ffn_w_up"][0], got[3]],
                ffn_w_down=[w["ffn_w_down"][0], got[4].reshape(1, D_FF, D_MODEL)],
            )
        return w

    whole = dict(
        hg_out_norm=hg_out_norm, kv_norm=kv_norm.reshape(1, D_MODEL), attn_norm=attn_norm, attn_sinks=attn_sinks.reshape(ATT_QH),
        ffn_norm=[ffn_norm[0:1], ffn_norm[1:2]], ffn_conv_b=[ffn_conv_b[0:1], ffn_conv_b[1:2]], final_norm=final_norm.reshape(1, D_MODEL),
    )
    whole = fetch(whole, "first", rest[3])

    red, layer1 = {}, {}

    def by_rows(g, rows):
        return g.reshape(N_CHIPS, rows, g.shape[2])

    def hook(point, dh, grads):
        if point == "ffn1":
            red["ffn1"] = _Reduction("ffn1", [by_rows(grads["ffn_w_down"], fs), grads["ffn_w_up"]], place_arr)
            return (red["ffn1"].token,)
        if point == "attn":
            red["ffn1"].to_chips(dh)
            layer1.update(grads)
            return (red["ffn1"].token,)
        if point == "ffn0":
            group = [by_rows(layer1["attn_w_o"], ds), by_rows(layer1["attn_w_q"], ds), by_rows(layer1["w_kv"], ds),
                     by_rows(grads["ffn_w_down"], fs), grads["ffn_w_up"], by_rows(grads["hg_w_out"], ds)]
            red["mid"] = _Reduction("mid", group, place_arr)
            return (red["mid"].token,)
        if point == "hgrn":
            red["ffn1"].to_core(dh)
            red["mid"].to_chips(dh)
            return (red["ffn1"].token, red["mid"].token)
        red["hg"] = _Reduction("hg", [grads["hg_w_in"]], place_arr)
        return (red["hg"].token,)

    loss, dx, grads = _local_step(x[0], loss_target[0], whole, fetch, hook)

    small_names = ["hg_out_norm", "attn_sinks", "kv_norm", "attn_norm", "ffn_norm", "ffn_conv_b", "final_norm", "hg_norm", "hg_lb_logits", "ffn_conv_w"]
    groups = [[loss]] + [grads[n] if isinstance(grads[n], list) else [grads[n]] for n in small_names[:-2]] + [[grads["hg_lb"]], grads["ffn_conv_w"]]
    widths = [None, None, ATT_QH] + [None] * 8
    packed = _pack_small(groups, jnp.reshape(2 * chip + core, (1,)).astype(jnp.int32))
    small_flight = _start_copies("small_start", [packed], N_DEV - 1, _small_copies)
    red["hg"].to_chips(small_flight[3])

    out_g, out_d, out_m, out_v = {}, {}, {}, {}

    def update(name, g2):
        shape = wts[name].shape
        d2, m2, v2 = _adamw(wts[name].reshape(g2.shape), mom1[name].reshape(g2.shape), mom2[name].reshape(g2.shape), g2, name=f"adamw_{name}")
        out_g[name], out_d[name], out_m[name], out_v[name] = g2.reshape(shape), d2.reshape(shape), m2.reshape(shape), v2.reshape(shape)
        return d2

    def update_layer(name, g2, layer, prev):
        res = _adamw_layer(wts[name], mom1[name], mom2[name], g2, layer, prev, name=f"adamw_{name}{layer}")
        out_g[name], out_d[name], out_m[name], out_v[name] = res
        return res

    g_down1, g_up1 = red["ffn1"].finish(red["hg"].token)
    up1 = update_layer("ffn_w_up", g_up1, 1, None)
    summed = _sum_small(_wait_copies("small_wait", small_flight[2], small_flight[0], small_flight[1], up1[3], _small_copies)[0], groups, widths)
    loss_out = summed[0][0, 0]
    small_grads = dict(zip(small_names, summed[1:]))
    small_grads["hg_norm"] = lax.dynamic_slice(small_grads["hg_norm"], (0, chip * ds), (1, ds))
    small_grads["hg_lb_logits"] = lax.dynamic_slice(small_grads["hg_lb_logits"], (0, chip * ds), (2, ds))
    small_grads["ffn_conv_w"] = lax.dynamic_slice(small_grads["ffn_conv_w"], (0, chip * fs), (2 * 3, fs))
    red["mid"].to_core(up1[1])
    down1 = update_layer("ffn_w_down", g_down1, 1, None)
    g_o, g_q, g_kv, g_down0, g_up0, g_out = red["mid"].finish(down1[1])
    update("attn_w_o", g_o)
    update("attn_w_q", g_q)
    update("w_kv", g_kv)
    update("hg_w_out", g_out)
    update_layer("ffn_w_down", g_down0, 0, down1)
    last = update_layer("ffn_w_up", g_up0, 0, up1)
    red["hg"].to_core(last[1])
    (g_in,) = red["hg"].finish(last[2])
    update("hg_w_in", g_in)

    as_2d = lambda a, n: a.reshape(small_grads[n].shape)
    updated = _adamw_small([(as_2d(wts[n], n), as_2d(mom1[n], n), as_2d(mom2[n], n), small_grads[n]) for n in small_names])
    for n, (d2, m2, v2) in zip(small_names, updated):
        shape = wts[n].shape
        out_g[n], out_d[n], out_m[n], out_v[n] = small_grads[n].reshape(shape), d2.reshape(shape), m2.reshape(shape), v2.reshape(shape)

    grad_x = dx.reshape(x.shape)
    return (loss_out, grad_x, *[out_g[n] for n in names], *[out_d[n] for n in names], *[out_m[n] for n in names], *[out_v[n] for n in names])
```
